```python
import jax, jax.numpy as jnp
from jax import lax
import numpy as np

D_MODEL = 1024
BATCH = 16
SEQ = 2048
DEPTH = 1

D_MIX = D_MODEL
D_CONV = D_MIX // 2
CONV_HEADS = 8
CONV_WIDTH = 31
D_POOL = D_MIX - D_CONV
POOL_WINDOWS = (2, 4, 8, 16)
POOL_GROUPS = len(POOL_WINDOWS)
POOL_GROUP_DIM = D_POOL // POOL_GROUPS
D_IN = 2 * D_CONV + D_POOL
N_MEM = 256
XATTN_HEADS = 4
XATTN_HEAD_DIM = D_MODEL // XATTN_HEADS
D_FF = 2816
FFN_CONV_WIDTH = 3
EPS = 1e-6

kernel_name = "hybrid_conformer_pool_xattn_convffn"


def rmsnorm(x, g):
    xf = x.astype(jnp.float32)
    y = xf * lax.rsqrt(jnp.mean(xf * xf, axis=-1, keepdims=True) + EPS)
    return (y * g.astype(jnp.float32)).astype(x.dtype)


def layernorm(x, g, b):
    xf = x.astype(jnp.float32)
    mu = jnp.mean(xf, axis=-1, keepdims=True)
    var = jnp.mean(jnp.square(xf - mu), axis=-1, keepdims=True)
    y = (xf - mu) * lax.rsqrt(var + EPS)
    return (y * g.astype(jnp.float32) + b.astype(jnp.float32)).astype(x.dtype)


def causal_depthwise_conv(x, w, b):
    k, c = w.shape
    y = lax.conv_general_dilated(
        x, w[:, None, :].astype(x.dtype), window_strides=(1,),
        padding=[(k - 1, 0)], dimension_numbers=("NWC", "WIO", "NWC"),
        feature_group_count=c)
    return y + b.astype(x.dtype)


def conformer_conv_mixer(u, dw_w, dw_b, ln_g, ln_b):
    val, gate = jnp.split(u, 2, axis=-1)
    h = val * jax.nn.sigmoid(gate)
    h = causal_depthwise_conv(h, dw_w, dw_b)
    h = layernorm(h, ln_g, ln_b)
    return jax.nn.silu(h)


def causal_window_mean_minus_self(v, w):
    s = v.shape[1]
    vf = v.astype(jnp.float32)
    c = jnp.cumsum(vf, axis=1)
    c_shift = jnp.pad(c, ((0, 0), (w, 0), (0, 0)))[:, :s]
    count = jnp.minimum(jnp.arange(1, s + 1, dtype=jnp.float32), float(w))
    mean = (c - c_shift) / count[None, :, None]
    return (mean - vf).astype(v.dtype)


def pooling_mixer(u, pool_w, pool_scale):
    groups = jnp.split(u, POOL_GROUPS, axis=-1)
    pooled = jnp.stack([causal_window_mean_minus_self(gv, w)
                        for gv, w in zip(groups, POOL_WINDOWS)], axis=2)
    mixed = jnp.einsum("bsgc,gcd->bsgd", pooled, pool_w.astype(u.dtype))
    b, s = u.shape[:2]
    return mixed.reshape(b, s, D_POOL) * pool_scale.astype(u.dtype)


def memory_cross_attention(h, mem_n, w_q, w_kv, w_o):
    b, s, _ = h.shape
    q = (h @ w_q).reshape(b, s, XATTN_HEADS, XATTN_HEAD_DIM)
    k, v = jnp.split(mem_n @ w_kv, 2, axis=-1)
    k = k.reshape(b, N_MEM, XATTN_HEADS, XATTN_HEAD_DIM)
    v = v.reshape(b, N_MEM, XATTN_HEADS, XATTN_HEAD_DIM)
    scores = jnp.einsum("bqhd,bkhd->bhqk", q.astype(jnp.float32), k.astype(jnp.float32))
    probs = jax.nn.softmax(scores * (XATTN_HEAD_DIM ** -0.5), axis=-1).astype(h.dtype)
    o = jnp.einsum("bhqk,bkhd->bqhd", probs, v).reshape(b, s, D_MODEL)
    return o @ w_o


def conv_ffn(h, w_up, dw_w, dw_b, w_down):
    u = causal_depthwise_conv(h @ w_up, dw_w, dw_b)
    gate, val = jnp.split(u, 2, axis=-1)
    return (jax.nn.silu(gate) * val) @ w_down


def _fwd_setup_inputs(seed: int = 0) -> dict:
    key = jax.random.key(seed)
    ks = jax.random.split(key, 24)
    f32 = jnp.float32

    def nrm(k, shape, scale):
        return jax.random.normal(k, shape, f32) * scale

    def gain(k, shape):
        return 1.0 + 0.05 * jax.random.normal(k, shape, f32)

    L = DEPTH
    return {
        "x": jax.random.normal(ks[0], (BATCH, SEQ, D_MODEL), f32),
        "mem": jax.random.normal(ks[1], (BATCH, N_MEM, D_MODEL), f32),
        "norm_mix_g": gain(ks[2], (L, D_MODEL)),
        "w_in": nrm(ks[3], (L, D_MODEL, D_IN), D_MODEL ** -0.5),
        "conv_dw_w": nrm(ks[4], (L, CONV_WIDTH, D_CONV), CONV_WIDTH ** -0.5),
        "conv_dw_b": nrm(ks[5], (L, D_CONV), 0.02),
        "conv_ln_g": gain(ks[6], (L, D_CONV)),
        "conv_ln_b": nrm(ks[7], (L, D_CONV), 0.02),
        "pool_w": nrm(ks[8], (L, POOL_GROUPS, POOL_GROUP_DIM, POOL_GROUP_DIM), POOL_GROUP_DIM ** -0.5),
        "pool_scale": gain(ks[9], (L, D_POOL)),
        "w_out": nrm(ks[10], (L, D_MIX, D_MODEL), D_MIX ** -0.5),
        "norm_xattn_g": gain(ks[11], (L, D_MODEL)),
        "norm_mem_g": gain(ks[12], (L, D_MODEL)),
        "w_q": nrm(ks[13], (L, D_MODEL, D_MODEL), D_MODEL ** -0.5),
        "w_kv": nrm(ks[14], (L, D_MODEL, 2 * D_MODEL), D_MODEL ** -0.5),
        "w_o": nrm(ks[15], (L, D_MODEL, D_MODEL), D_MODEL ** -0.5),
        "norm_ffn_g": gain(ks[16], (L, D_MODEL)),
        "w_up": nrm(ks[17], (L, D_MODEL, 2 * D_FF), D_MODEL ** -0.5),
        "ffn_dw_w": nrm(ks[18], (L, FFN_CONV_WIDTH, 2 * D_FF), FFN_CONV_WIDTH ** -0.5),
        "ffn_dw_b": nrm(ks[19], (L, 2 * D_FF), 0.02),
        "w_down": nrm(ks[20], (L, D_FF, D_MODEL), D_FF ** -0.5),
        "norm_final_g": gain(ks[21], (D_MODEL,)),
    }


def _fwd_reference(x, mem, norm_mix_g, w_in, conv_dw_w, conv_dw_b, conv_ln_g, conv_ln_b,
              pool_w, pool_scale, w_out, norm_xattn_g, norm_mem_g, w_q, w_kv, w_o,
              norm_ffn_g, w_up, ffn_dw_w, ffn_dw_b, w_down, norm_final_g):
    for l in range(DEPTH):
        h = rmsnorm(x, norm_mix_g[l])
        u = h @ w_in[l]
        u_conv = u[..., :2 * D_CONV]
        u_pool = u[..., 2 * D_CONV:]
        y_conv = conformer_conv_mixer(u_conv, conv_dw_w[l], conv_dw_b[l],
                                      conv_ln_g[l], conv_ln_b[l])
        y_pool = pooling_mixer(u_pool, pool_w[l], pool_scale[l])
        y = jnp.concatenate([y_conv, y_pool], axis=-1)
        x = x + y @ w_out[l]
        h = rmsnorm(x, norm_xattn_g[l])
        mem_n = rmsnorm(mem, norm_mem_g[l])
        x = x + memory_cross_attention(h, mem_n, w_q[l], w_kv[l], w_o[l])
        h = rmsnorm(x, norm_ffn_g[l])
        x = x + conv_ffn(h, w_up[l], ffn_dw_w[l], ffn_dw_b[l], w_down[l])
    return rmsnorm(x, norm_final_g)


import jax as _jax
import jax.numpy as _jnp

TWIN_FORMAT = 'train_step'
FWD_PARAMS = ['x', 'mem', 'norm_mix_g', 'w_in', 'conv_dw_w', 'conv_dw_b', 'conv_ln_g', 'conv_ln_b', 'pool_w', 'pool_scale', 'w_out', 'norm_xattn_g', 'norm_mem_g', 'w_q', 'w_kv', 'w_o', 'norm_ffn_g', 'w_up', 'ffn_dw_w', 'ffn_dw_b', 'w_down', 'norm_final_g']
TWIN_WEIGHTS = ['norm_mix_g', 'w_in', 'conv_dw_w', 'conv_dw_b', 'conv_ln_g', 'conv_ln_b', 'pool_w', 'pool_scale', 'w_out', 'norm_xattn_g', 'norm_mem_g', 'w_q', 'w_kv', 'w_o', 'norm_ffn_g', 'w_up', 'ffn_dw_w', 'ffn_dw_b', 'w_down', 'norm_final_g']
TWIN_DIFF_INPUT = 'x'
TWIN_INPUTS = ['x', 'mem', 'norm_mix_g', 'w_in', 'conv_dw_w', 'conv_dw_b', 'conv_ln_g', 'conv_ln_b', 'pool_w', 'pool_scale', 'w_out', 'norm_xattn_g', 'norm_mem_g', 'w_q', 'w_kv', 'w_o', 'norm_ffn_g', 'w_up', 'ffn_dw_w', 'ffn_dw_b', 'w_down', 'norm_final_g', 'loss_target', 'm_norm_mix_g', 'm_w_in', 'm_conv_dw_w', 'm_conv_dw_b', 'm_conv_ln_g', 'm_conv_ln_b', 'm_pool_w', 'm_pool_scale', 'm_w_out', 'm_norm_xattn_g', 'm_norm_mem_g', 'm_w_q', 'm_w_kv', 'm_w_o', 'm_norm_ffn_g', 'm_w_up', 'm_ffn_dw_w', 'm_ffn_dw_b', 'm_w_down', 'm_norm_final_g', 'v_norm_mix_g', 'v_w_in', 'v_conv_dw_w', 'v_conv_dw_b', 'v_conv_ln_g', 'v_conv_ln_b', 'v_pool_w', 'v_pool_scale', 'v_w_out', 'v_norm_xattn_g', 'v_norm_mem_g', 'v_w_q', 'v_w_kv', 'v_w_o', 'v_norm_ffn_g', 'v_w_up', 'v_ffn_dw_w', 'v_ffn_dw_b', 'v_w_down', 'v_norm_final_g']
TWIN_OUTPUTS = ['loss', 'grad_x', 'grad_norm_mix_g', 'grad_w_in', 'grad_conv_dw_w', 'grad_conv_dw_b', 'grad_conv_ln_g', 'grad_conv_ln_b', 'grad_pool_w', 'grad_pool_scale', 'grad_w_out', 'grad_norm_xattn_g', 'grad_norm_mem_g', 'grad_w_q', 'grad_w_kv', 'grad_w_o', 'grad_norm_ffn_g', 'grad_w_up', 'grad_ffn_dw_w', 'grad_ffn_dw_b', 'grad_w_down', 'grad_norm_final_g', 'delta_norm_mix_g', 'delta_w_in', 'delta_conv_dw_w', 'delta_conv_dw_b', 'delta_conv_ln_g', 'delta_conv_ln_b', 'delta_pool_w', 'delta_pool_scale', 'delta_w_out', 'delta_norm_xattn_g', 'delta_norm_mem_g', 'delta_w_q', 'delta_w_kv', 'delta_w_o', 'delta_norm_ffn_g', 'delta_w_up', 'delta_ffn_dw_w', 'delta_ffn_dw_b', 'delta_w_down', 'delta_norm_final_g', 'new_m_norm_mix_g', 'new_m_w_in', 'new_m_conv_dw_w', 'new_m_conv_dw_b', 'new_m_conv_ln_g', 'new_m_conv_ln_b', 'new_m_pool_w', 'new_m_pool_scale', 'new_m_w_out', 'new_m_norm_xattn_g', 'new_m_norm_mem_g', 'new_m_w_q', 'new_m_w_kv', 'new_m_w_o', 'new_m_norm_ffn_g', 'new_m_w_up', 'new_m_ffn_dw_w', 'new_m_ffn_dw_b', 'new_m_w_down', 'new_m_norm_final_g', 'new_v_norm_mix_g', 'new_v_w_in', 'new_v_conv_dw_w', 'new_v_conv_dw_b', 'new_v_conv_ln_g', 'new_v_conv_ln_b', 'new_v_pool_w', 'new_v_pool_scale', 'new_v_w_out', 'new_v_norm_xattn_g', 'new_v_norm_mem_g', 'new_v_w_q', 'new_v_w_kv', 'new_v_w_o', 'new_v_norm_ffn_g', 'new_v_w_up', 'new_v_ffn_dw_w', 'new_v_ffn_dw_b', 'new_v_w_down', 'new_v_norm_final_g']
TWIN_LEAF_KINDS = {'loss': 'loss', 'grad_x': 'grad_x', 'grad_norm_mix_g': 'grad_w', 'grad_w_in': 'grad_w', 'grad_conv_dw_w': 'grad_w', 'grad_conv_dw_b': 'grad_w', 'grad_conv_ln_g': 'grad_w', 'grad_conv_ln_b': 'grad_w', 'grad_pool_w': 'grad_w', 'grad_pool_scale': 'grad_w', 'grad_w_out': 'grad_w', 'grad_norm_xattn_g': 'grad_w', 'grad_norm_mem_g': 'grad_w', 'grad_w_q': 'grad_w', 'grad_w_kv': 'grad_w', 'grad_w_o': 'grad_w', 'grad_norm_ffn_g': 'grad_w', 'grad_w_up': 'grad_w', 'grad_ffn_dw_w': 'grad_w', 'grad_ffn_dw_b': 'grad_w', 'grad_w_down': 'grad_w', 'grad_norm_final_g': 'grad_w', 'delta_norm_mix_g': 'delta_w', 'delta_w_in': 'delta_w', 'delta_conv_dw_w': 'delta_w', 'delta_conv_dw_b': 'delta_w', 'delta_conv_ln_g': 'delta_w', 'delta_conv_ln_b': 'delta_w', 'delta_pool_w': 'delta_w', 'delta_pool_scale': 'delta_w', 'delta_w_out': 'delta_w', 'delta_norm_xattn_g': 'delta_w', 'delta_norm_mem_g': 'delta_w', 'delta_w_q': 'delta_w', 'delta_w_kv': 'delta_w', 'delta_w_o': 'delta_w', 'delta_norm_ffn_g': 'delta_w', 'delta_w_up': 'delta_w', 'delta_ffn_dw_w': 'delta_w', 'delta_ffn_dw_b': 'delta_w', 'delta_w_down': 'delta_w', 'delta_norm_final_g': 'delta_w', 'new_m_norm_mix_g': 'new_m', 'new_m_w_in': 'new_m', 'new_m_conv_dw_w': 'new_m', 'new_m_conv_dw_b': 'new_m', 'new_m_conv_ln_g': 'new_m', 'new_m_conv_ln_b': 'new_m', 'new_m_pool_w': 'new_m', 'new_m_pool_scale': 'new_m', 'new_m_w_out': 'new_m', 'new_m_norm_xattn_g': 'new_m', 'new_m_norm_mem_g': 'new_m', 'new_m_w_q': 'new_m', 'new_m_w_kv': 'new_m', 'new_m_w_o': 'new_m', 'new_m_norm_ffn_g': 'new_m', 'new_m_w_up': 'new_m', 'new_m_ffn_dw_w': 'new_m', 'new_m_ffn_dw_b': 'new_m', 'new_m_w_down': 'new_m', 'new_m_norm_final_g': 'new_m', 'new_v_norm_mix_g': 'new_v', 'new_v_w_in': 'new_v', 'new_v_conv_dw_w': 'new_v', 'new_v_conv_dw_b': 'new_v', 'new_v_conv_ln_g': 'new_v', 'new_v_conv_ln_b': 'new_v', 'new_v_pool_w': 'new_v', 'new_v_pool_scale': 'new_v', 'new_v_w_out': 'new_v', 'new_v_norm_xattn_g': 'new_v', 'new_v_norm_mem_g': 'new_v', 'new_v_w_q': 'new_v', 'new_v_w_kv': 'new_v', 'new_v_w_o': 'new_v', 'new_v_norm_ffn_g': 'new_v', 'new_v_w_up': 'new_v', 'new_v_ffn_dw_w': 'new_v', 'new_v_ffn_dw_b': 'new_v', 'new_v_w_down': 'new_v', 'new_v_norm_final_g': 'new_v'}


def _forward(args):
    return _fwd_reference(*[args[k] for k in FWD_PARAMS])


def _output_shape():
    out = _jax.eval_shape(lambda: _forward(_fwd_setup_inputs(0)))
    return out.shape, out.dtype

N_MICROBATCH = 1
ADAM_LR = 0.001
ADAM_B1 = 0.9
ADAM_B2 = 0.999
ADAM_EPS = 1e-08
ADAM_WD = 0.01
ADAM_STEP = 10
PER_EXAMPLE_BATCH_AXIS = {'x': 0, 'mem': 0, 'loss_target': 0}
SHARED_INPUTS = []
_WEIGHT_DTYPES = {'norm_mix_g': _jnp.float32, 'w_in': _jnp.float32, 'conv_dw_w': _jnp.float32, 'conv_dw_b': _jnp.float32, 'conv_ln_g': _jnp.float32, 'conv_ln_b': _jnp.float32, 'pool_w': _jnp.float32, 'pool_scale': _jnp.float32, 'w_out': _jnp.float32, 'norm_xattn_g': _jnp.float32, 'norm_mem_g': _jnp.float32, 'w_q': _jnp.float32, 'w_kv': _jnp.float32, 'w_o': _jnp.float32, 'norm_ffn_g': _jnp.float32, 'w_up': _jnp.float32, 'ffn_dw_w': _jnp.float32, 'ffn_dw_b': _jnp.float32, 'w_down': _jnp.float32, 'norm_final_g': _jnp.float32}
MOMENT_SCALE = {'norm_mix_g': 1.241121e-01, 'w_in': 1.009859e-01, 'conv_dw_w': 9.682314e-02, 'conv_dw_b': 2.153178e-01, 'conv_ln_g': 1.194347e-01, 'conv_ln_b': 1.240928e-01, 'pool_w': 1.411597e-01, 'pool_scale': 1.371563e-01, 'w_out': 1.218131e-01, 'norm_xattn_g': 1.690777e-02, 'norm_mem_g': 2.456126e-02, 'w_q': 1.673266e-02, 'w_kv': 1.715872e-02, 'w_o': 1.722088e-02, 'norm_ffn_g': 1.104724e-01, 'w_up': 4.750284e-02, 'ffn_dw_w': 4.784878e-02, 'ffn_dw_b': 4.821892e-02, 'w_down': 7.840662e-02, 'norm_final_g': 3.207737e+01}


def _to_microbatches(a, axis):
    t = _jnp.moveaxis(a, axis, 0)
    t = t.reshape((N_MICROBATCH, t.shape[0] // N_MICROBATCH) + t.shape[1:])
    return _jnp.moveaxis(t, 1, axis + 1)


def setup_inputs(seed: int = 0) -> dict:
    inp = _fwd_setup_inputs(seed)
    key = _jax.random.fold_in(_jax.random.key(seed), 7919)
    shape, _ = _output_shape()
    out = dict(inp)
    out["loss_target"] = _jax.random.normal(_jax.random.fold_in(key, 0), shape, _jnp.float32)
    for i, name in enumerate(TWIN_WEIGHTS):
        w = inp[name].astype(_jnp.float32)
        if MOMENT_SCALE is None:
            s = _jnp.sqrt(_jnp.mean(_jnp.square(w)) + 1e-30)
        else:
            s = MOMENT_SCALE[name]
        km, kv = _jax.random.split(_jax.random.fold_in(key, i + 1))
        out[name] = w
        out["m_" + name] = s * _jax.random.normal(km, w.shape, _jnp.float32)
        out["v_" + name] = (s * s) * _jax.random.uniform(kv, w.shape, _jnp.float32, 0.5, 1.5)
    if N_MICROBATCH > 1:
        for name, axis in PER_EXAMPLE_BATCH_AXIS.items():
            out[name] = _to_microbatches(out[name], axis)
    return {'x': out['x'], 'mem': out['mem'], 'norm_mix_g': out['norm_mix_g'], 'w_in': out['w_in'], 'conv_dw_w': out['conv_dw_w'], 'conv_dw_b': out['conv_dw_b'], 'conv_ln_g': out['conv_ln_g'], 'conv_ln_b': out['conv_ln_b'], 'pool_w': out['pool_w'], 'pool_scale': out['pool_scale'], 'w_out': out['w_out'], 'norm_xattn_g': out['norm_xattn_g'], 'norm_mem_g': out['norm_mem_g'], 'w_q': out['w_q'], 'w_kv': out['w_kv'], 'w_o': out['w_o'], 'norm_ffn_g': out['norm_ffn_g'], 'w_up': out['w_up'], 'ffn_dw_w': out['ffn_dw_w'], 'ffn_dw_b': out['ffn_dw_b'], 'w_down': out['w_down'], 'norm_final_g': out['norm_final_g'], 'loss_target': out['loss_target'], 'm_norm_mix_g': out['m_norm_mix_g'], 'm_w_in': out['m_w_in'], 'm_conv_dw_w': out['m_conv_dw_w'], 'm_conv_dw_b': out['m_conv_dw_b'], 'm_conv_ln_g': out['m_conv_ln_g'], 'm_conv_ln_b': out['m_conv_ln_b'], 'm_pool_w': out['m_pool_w'], 'm_pool_scale': out['m_pool_scale'], 'm_w_out': out['m_w_out'], 'm_norm_xattn_g': out['m_norm_xattn_g'], 'm_norm_mem_g': out['m_norm_mem_g'], 'm_w_q': out['m_w_q'], 'm_w_kv': out['m_w_kv'], 'm_w_o': out['m_w_o'], 'm_norm_ffn_g': out['m_norm_ffn_g'], 'm_w_up': out['m_w_up'], 'm_ffn_dw_w': out['m_ffn_dw_w'], 'm_ffn_dw_b': out['m_ffn_dw_b'], 'm_w_down': out['m_w_down'], 'm_norm_final_g': out['m_norm_final_g'], 'v_norm_mix_g': out['v_norm_mix_g'], 'v_w_in': out['v_w_in'], 'v_conv_dw_w': out['v_conv_dw_w'], 'v_conv_dw_b': out['v_conv_dw_b'], 'v_conv_ln_g': out['v_conv_ln_g'], 'v_conv_ln_b': out['v_conv_ln_b'], 'v_pool_w': out['v_pool_w'], 'v_pool_scale': out['v_pool_scale'], 'v_w_out': out['v_w_out'], 'v_norm_xattn_g': out['v_norm_xattn_g'], 'v_norm_mem_g': out['v_norm_mem_g'], 'v_w_q': out['v_w_q'], 'v_w_kv': out['v_w_kv'], 'v_w_o': out['v_w_o'], 'v_norm_ffn_g': out['v_norm_ffn_g'], 'v_w_up': out['v_w_up'], 'v_ffn_dw_w': out['v_ffn_dw_w'], 'v_ffn_dw_b': out['v_ffn_dw_b'], 'v_w_down': out['v_w_down'], 'v_norm_final_g': out['v_norm_final_g']}


def _loss(weights, diff, rest, loss_target):
    with _jax.named_scope("forward"):
        args = {**rest, TWIN_DIFF_INPUT: diff, **{k: w.astype(_WEIGHT_DTYPES[k]) for k, w in weights.items()}}
        y = _forward(args)
    with _jax.named_scope("loss_head"):
        err = _jnp.square(y.astype(_jnp.float32) - loss_target)
        return 0.5 * _jnp.sum(_jnp.mean(err, axis=-1)) if err.ndim else 0.5 * err


def _adamw(w, g, m, v):
    m = ADAM_B1 * m + (1.0 - ADAM_B1) * g
    v = ADAM_B2 * v + (1.0 - ADAM_B2) * _jnp.square(g)
    m_hat = m / (1.0 - ADAM_B1 ** ADAM_STEP)
    v_hat = v / (1.0 - ADAM_B2 ** ADAM_STEP)
    delta = -ADAM_LR * (m_hat / (_jnp.sqrt(v_hat) + ADAM_EPS) + ADAM_WD * w)
    return delta, m, v


def reference(x, mem, norm_mix_g, w_in, conv_dw_w, conv_dw_b, conv_ln_g, conv_ln_b, pool_w, pool_scale, w_out, norm_xattn_g, norm_mem_g, w_q, w_kv, w_o, norm_ffn_g, w_up, ffn_dw_w, ffn_dw_b, w_down, norm_final_g, loss_target, m_norm_mix_g, m_w_in, m_conv_dw_w, m_conv_dw_b, m_conv_ln_g, m_conv_ln_b, m_pool_w, m_pool_scale, m_w_out, m_norm_xattn_g, m_norm_mem_g, m_w_q, m_w_kv, m_w_o, m_norm_ffn_g, m_w_up, m_ffn_dw_w, m_ffn_dw_b, m_w_down, m_norm_final_g, v_norm_mix_g, v_w_in, v_conv_dw_w, v_conv_dw_b, v_conv_ln_g, v_conv_ln_b, v_pool_w, v_pool_scale, v_w_out, v_norm_xattn_g, v_norm_mem_g, v_w_q, v_w_kv, v_w_o, v_norm_ffn_g, v_w_up, v_ffn_dw_w, v_ffn_dw_b, v_w_down, v_norm_final_g):
    given = dict(x=x, mem=mem, norm_mix_g=norm_mix_g, w_in=w_in, conv_dw_w=conv_dw_w, conv_dw_b=conv_dw_b, conv_ln_g=conv_ln_g, conv_ln_b=conv_ln_b, pool_w=pool_w, pool_scale=pool_scale, w_out=w_out, norm_xattn_g=norm_xattn_g, norm_mem_g=norm_mem_g, w_q=w_q, w_kv=w_kv, w_o=w_o, norm_ffn_g=norm_ffn_g, w_up=w_up, ffn_dw_w=ffn_dw_w, ffn_dw_b=ffn_dw_b, w_down=w_down, norm_final_g=norm_final_g, loss_target=loss_target, m_norm_mix_g=m_norm_mix_g, m_w_in=m_w_in, m_conv_dw_w=m_conv_dw_w, m_conv_dw_b=m_conv_dw_b, m_conv_ln_g=m_conv_ln_g, m_conv_ln_b=m_conv_ln_b, m_pool_w=m_pool_w, m_pool_scale=m_pool_scale, m_w_out=m_w_out, m_norm_xattn_g=m_norm_xattn_g, m_norm_mem_g=m_norm_mem_g, m_w_q=m_w_q, m_w_kv=m_w_kv, m_w_o=m_w_o, m_norm_ffn_g=m_norm_ffn_g, m_w_up=m_w_up, m_ffn_dw_w=m_ffn_dw_w, m_ffn_dw_b=m_ffn_dw_b, m_w_down=m_w_down, m_norm_final_g=m_norm_final_g, v_norm_mix_g=v_norm_mix_g, v_w_in=v_w_in, v_conv_dw_w=v_conv_dw_w, v_conv_dw_b=v_conv_dw_b, v_conv_ln_g=v_conv_ln_g, v_conv_ln_b=v_conv_ln_b, v_pool_w=v_pool_w, v_pool_scale=v_pool_scale, v_w_out=v_w_out, v_norm_xattn_g=v_norm_xattn_g, v_norm_mem_g=v_norm_mem_g, v_w_q=v_w_q, v_w_kv=v_w_kv, v_w_o=v_w_o, v_norm_ffn_g=v_norm_ffn_g, v_w_up=v_w_up, v_ffn_dw_w=v_ffn_dw_w, v_ffn_dw_b=v_ffn_dw_b, v_w_down=v_w_down, v_norm_final_g=v_norm_final_g)
    weights = {n: given[n] for n in TWIN_WEIGHTS}
    shared = {n: given[n] for n in SHARED_INPUTS}
    per_example = {n: given[n] for n in ['x', 'mem']}
    grad_fn = _jax.value_and_grad(_loss, argnums=(0, 1))

    def one_microbatch(ex, loss_target):
        ex = dict(ex)
        diff = ex.pop(TWIN_DIFF_INPUT)
        return grad_fn(weights, diff, {**shared, **ex}, loss_target)

    if N_MICROBATCH == 1:
        loss, (grad_w, grad_x) = one_microbatch(per_example, given["loss_target"])
    else:
        def body(carry, xs):
            loss_sum, grad_sum = carry
            l_k, (gw_k, gx_k) = one_microbatch(xs[0], xs[1])
            with _jax.named_scope("update"):
                return (loss_sum + l_k, _jax.tree.map(_jnp.add, grad_sum, gw_k)), gx_k

        init = (_jnp.zeros((), _jnp.float32), _jax.tree.map(_jnp.zeros_like, weights))
        (loss, grad_w), grad_x = _jax.lax.scan(body, init, (per_example, given["loss_target"]))
    with _jax.named_scope("update"):
        delta_w, new_m, new_v = {}, {}, {}
        for n in TWIN_WEIGHTS:
            delta_w[n], new_m[n], new_v[n] = _adamw(weights[n], grad_w[n], given["m_" + n], given["v_" + n])
    return (loss, grad_x, *[grad_w[n] for n in TWIN_WEIGHTS], *[delta_w[n] for n in TWIN_WEIGHTS],
            *[new_m[n] for n in TWIN_WEIGHTS], *[new_v[n] for n in TWIN_WEIGHTS])
```

```python
import functools

import jax
import jax.numpy as jnp
from jax import lax
from jax.experimental import pallas as pl
from jax.experimental.pallas import tpu as pltpu

F32 = jnp.float32
BF16 = jnp.bfloat16
MESH = pl.DeviceIdType.MESH

N_DEV = 8
D_MODEL = 1024
D_CONV = 512
D_POOL = 512
CONV_WIDTH = 31
POOL_WINDOWS = (2, 4, 8, 16)
POOL_GROUP_DIM = 128
D_IN = 1536
N_MEM = 256
HEADS = 4
HEAD_DIM = 256
D_FF = 2816
FFN_CONV_WIDTH = 3
EPS = 1e-6
ADAM_LR = 0.001
ADAM_B1 = 0.9
ADAM_B2 = 0.999
ADAM_EPS = 1e-08
ADAM_WD = 0.01
ADAM_STEP = 10

VMEM_LIMIT_V7X = 56 * 1024 * 1024
CONV_HALO = 32
POOL_HALO = 16
FFN_HALO = 8
FFN_CHUNK = 1408

W_ROWS = (("w_in", 192), ("w_out", 128), ("w_q", 128), ("w_kv", 256), ("w_o", 128), ("w_up", 704), ("w_down", 352))
W_OFF = {}
_o = 0
for _n, _r in W_ROWS:
    W_OFF[_n] = (_o, _r)
    _o += _r
PACK_ROWS = _o


def _dot(a, b):
    return jnp.dot(a, b, preferred_element_type=F32)


def _dot_nt(a, b):
    return lax.dot_general(a, b, (((1,), (1,)), ((), ())), preferred_element_type=F32)


def _dot_tn(a, b):
    return lax.dot_general(a, b, (((0,), (0,)), ((), ())), preferred_element_type=F32)


def _sigmoid(v):
    return 1.0 / (1.0 + jnp.exp(-v))


def _rms_fwd(v):
    r = lax.rsqrt(jnp.mean(v * v, axis=-1, keepdims=True) + EPS)
    return v * r, r


def _rms_bwd(dh, vh, r, g):
    gd = dh * g
    return r * (gd - vh * jnp.mean(gd * vh, axis=-1, keepdims=True))


def _colsum(v):
    return jnp.sum(v, axis=0, keepdims=True)


def _full(shape):
    return pl.BlockSpec(shape, lambda *_: (0,) * len(shape))


def _params(sem=("arbitrary",), vmem=VMEM_LIMIT_V7X):
    return pltpu.CompilerParams(dimension_semantics=sem, vmem_limit_bytes=vmem)


def _load_weight(g_hbm, name, dst, sem):
    off, rows = W_OFF[name]
    return [pltpu.make_async_copy(g_hbm.at[d, pl.ds(off, rows), :], dst.at[pl.ds(d * rows, rows), :], sem)
            for d in range(N_DEV)]


def _position():
    x, y, c = lax.axis_index("x"), lax.axis_index("y"), lax.axis_index("c")
    chips = [(1 - x, y), (x, 1 - y), (1 - x, 1 - y)]
    return x, y, c, chips


def _dev(px, py, pc):
    return 4 * px + 2 * py + pc


def _all_gather(arrs, name):
    n = len(arrs)

    def body(*refs):
        ins, outs = refs[:n], refs[n:2 * n]
        send_sems, recv_sems, local_sems = refs[2 * n:]
        x, y, c, chips = _position()
        me, sibling = (x, y, c), (x, y, 1 - c)

        def copy(a, k, block, to, src=None):
            rows = outs[a].at[_dev(*block)]
            return pltpu.make_async_remote_copy(
                src_ref=rows if src is None else src, dst_ref=rows,
                send_sem=send_sems.at[a, k], recv_sem=recv_sems.at[a, k], device_id=to, device_id_type=MESH)

        started = []
        for a in range(n):
            mine = pltpu.make_async_copy(ins[a], outs[a].at[_dev(*me)], local_sems.at[a])
            mine.start()
            started.append(mine)
        sends = []
        for a in range(n):
            first = [copy(a, 0, me, sibling, src=ins[a])]
            first += [copy(a, 1 + j, me, (*chip, c), src=ins[a]) for j, chip in enumerate(chips)]
            for cp in first:
                cp.start()
            sends += first
        for j, chip in enumerate(chips):
            for a in range(n):
                copy(a, 1 + j, (*chip, c), me).wait_recv()
                passed = copy(a, 4 + j, (*chip, c), sibling)
                passed.start()
                sends.append(passed)
        for a in range(n):
            copy(a, 0, sibling, me).wait_recv()
            for j, chip in enumerate(chips):
                copy(a, 4 + j, (*chip, 1 - c), me).wait_recv()
        for cp in sends:
            cp.wait_send()
        for mine in started:
            mine.wait()

    any_spec = pl.BlockSpec(memory_space=pl.ANY)
    return pl.pallas_call(
        body, name=name,
        out_shape=[jax.ShapeDtypeStruct((N_DEV,) + a.shape, a.dtype) for a in arrs],
        in_specs=[any_spec] * n, out_specs=[any_spec] * n,
        scratch_shapes=[pltpu.SemaphoreType.DMA((n, 7)), pltpu.SemaphoreType.DMA((n, 7)), pltpu.SemaphoreType.DMA((n,))],
    )(*arrs)


def _exchange_sibling(g):
    _, rows, cols = g.shape

    def body(g_ref, out_ref, send_sems, recv_sems):
        x, y, c, chips = _position()
        sibling = (x, y, 1 - c)
        copies = []
        for j, chip in enumerate([(x, y)] + chips):
            cp = pltpu.make_async_remote_copy(
                src_ref=g_ref.at[_dev(*chip, 1 - c)], dst_ref=out_ref.at[j],
                send_sem=send_sems.at[j], recv_sem=recv_sems.at[j], device_id=sibling, device_id_type=MESH)
            cp.start()
            copies.append(cp)
        for cp in copies:
            cp.wait_recv()
        for cp in copies:
            cp.wait_send()

    any_spec = pl.BlockSpec(memory_space=pl.ANY)
    return pl.pallas_call(
        body, name="rs_sibling_exchange",
        out_shape=jax.ShapeDtypeStruct((4, rows, cols), g.dtype),
        in_specs=[any_spec], out_specs=any_spec,
        scratch_shapes=[pltpu.SemaphoreType.DMA((4,)), pltpu.SemaphoreType.DMA((4,))],
    )(g)


def _exchange_chips(s):
    _, rows, cols = s.shape

    def body(s_ref, out_ref, send_sems, recv_sems):
        x, y, c, chips = _position()
        copies = []
        for j, chip in enumerate(chips):
            cp = pltpu.make_async_remote_copy(
                src_ref=s_ref.at[j], dst_ref=out_ref.at[j],
                send_sem=send_sems.at[j], recv_sem=recv_sems.at[j], device_id=(*chip, c), device_id_type=MESH)
            cp.start()
            copies.append(cp)
        for cp in copies:
            cp.wait_recv()
        for cp in copies:
            cp.wait_send()

    any_spec = pl.BlockSpec(memory_space=pl.ANY)
    return pl.pallas_call(
        body, name="rs_chip_exchange",
        out_shape=jax.ShapeDtypeStruct((3, rows, cols), s.dtype),
        in_specs=[any_spec], out_specs=any_spec,
        scratch_shapes=[pltpu.SemaphoreType.DMA((3,)), pltpu.SemaphoreType.DMA((3,))],
    )(s)


def _owner_table():
    x, y, c = lax.axis_index("x"), lax.axis_index("y"), lax.axis_index("c")
    chips = [(x, y), (1 - x, y), (x, 1 - y), (1 - x, 1 - y)]
    return jnp.stack([_dev(px, py, c) for px, py in chips]).astype(jnp.int32)


def _chip_partial_sums(table, g, from_sibling, tile):
    _, rows, cols = g.shape

    def body(tab_ref, g_ref, l_ref, out_ref):
        del tab_ref
        out_ref[...] = (g_ref[...].astype(F32) + l_ref[...].astype(F32)).astype(out_ref.dtype)

    grid_spec = pltpu.PrefetchScalarGridSpec(
        num_scalar_prefetch=1, grid=(3, rows // tile),
        in_specs=[pl.BlockSpec((None, tile, cols), lambda j, t, tab: (tab[j + 1], t, 0)),
                  pl.BlockSpec((None, tile, cols), lambda j, t, tab: (j + 1, t, 0))],
        out_specs=pl.BlockSpec((None, tile, cols), lambda j, t, tab: (j, t, 0)))
    return pl.pallas_call(
        body, name="rs_chip_partial_sums", grid_spec=grid_spec,
        out_shape=jax.ShapeDtypeStruct((3, rows, cols), BF16),
        compiler_params=_params(("arbitrary", "arbitrary")),
    )(table, g, from_sibling)


def _final_grad_sums(table, g, from_sibling, from_chips, tile):
    _, rows, cols = g.shape

    def body(tab_ref, g_ref, l_ref, c_ref, out_ref):
        del tab_ref
        acc = g_ref[...].astype(F32) + l_ref[...].astype(F32)
        for j in range(3):
            acc = acc + c_ref[j].astype(F32)
        out_ref[...] = acc

    grid_spec = pltpu.PrefetchScalarGridSpec(
        num_scalar_prefetch=1, grid=(rows // tile,),
        in_specs=[pl.BlockSpec((None, tile, cols), lambda t, tab: (tab[0], t, 0)),
                  pl.BlockSpec((None, tile, cols), lambda t, tab: (0, t, 0)),
                  pl.BlockSpec((3, tile, cols), lambda t, tab: (0, t, 0))],
        out_specs=pl.BlockSpec((tile, cols), lambda t, tab: (t, 0)))
    return pl.pallas_call(
        body, name="rs_final_sums", grid_spec=grid_spec,
        out_shape=jax.ShapeDtypeStruct((rows, cols), F32),
        compiler_params=_params(("arbitrary",)),
    )(table, g, from_sibling, from_chips)


def _sum_blocks(g8):
    _, rows, cols = g8.shape

    def body(g_ref, out_ref):
        acc = g_ref[0]
        for d in range(1, N_DEV):
            acc = acc + g_ref[d]
        out_ref[...] = acc

    return pl.pallas_call(
        body, name="small_grad_sum", grid=(1,),
        in_specs=[_full((N_DEV, rows, cols))], out_specs=_full((rows, cols)),
        out_shape=jax.ShapeDtypeStruct((rows, cols), F32),
        compiler_params=_params(("arbitrary",)),
    )(g8)


def _fwd_mix(x2d, gw, g_mix, conv_w, conv_b, ln_g, ln_b, pool_w, pool_scale, seq, tm):
    tokens = x2d.shape[0]
    n_tiles = tokens // tm
    tps = seq // tm

    def body(x_ref, gmix_ref, gw_hbm, cw_ref, cb_ref, lng_ref, lnb_ref, pw_ref, ps_ref,
             x1_ref, u_ref, c_ref, pooled_ref, ymix_ref, h1_ref,
             win_v, wout_v, hc_carry, up_carry, sem):
        i = pl.program_id(0)

        @pl.when(i == 0)
        def _():
            copies = _load_weight(gw_hbm, "w_in", win_v, sem) + _load_weight(gw_hbm, "w_out", wout_v, sem)
            for cp in copies:
                cp.start()
            for cp in copies:
                cp.wait()

        @pl.when(i % tps == 0)
        def _():
            hc_carry[...] = jnp.zeros_like(hc_carry)
            up_carry[...] = jnp.zeros_like(up_carry)

        x = x_ref[...]
        xh, _ = _rms_fwd(x)
        h1 = (xh * gmix_ref[...]).astype(BF16)
        h1_ref[...] = h1
        u = _dot_nt(h1, win_v[...])
        u_ref[...] = u
        val, gate, up = u[:, :D_CONV], u[:, D_CONV:2 * D_CONV], u[:, 2 * D_CONV:]

        hc = val * _sigmoid(gate)
        ext = jnp.concatenate([hc_carry[...], hc], axis=0)
        hc_carry[...] = hc[tm - CONV_HALO:, :]
        conv = jnp.broadcast_to(cb_ref[...], (tm, D_CONV))
        for k in range(CONV_WIDTH):
            shift = CONV_WIDTH - 1 - k
            tap = ext if shift == 0 else pltpu.roll(ext, shift, 0)
            conv = conv + cw_ref[k:k + 1, :] * tap[CONV_HALO:, :]
        c_ref[...] = conv
        mu = jnp.mean(conv, axis=-1, keepdims=True)
        cen = conv - mu
        ln = cen * lax.rsqrt(jnp.mean(cen * cen, axis=-1, keepdims=True) + EPS) * lng_ref[...] + lnb_ref[...]
        y_conv = ln * _sigmoid(ln)

        extp = jnp.concatenate([up_carry[...], up], axis=0)
        up_carry[...] = up[tm - POOL_HALO:, :]
        pos = lax.broadcasted_iota(jnp.int32, (tm, 1), 0) + (i % tps) * tm
        run = extp
        mixed = []
        for g, w in enumerate(POOL_WINDOWS):
            lo = g * POOL_GROUP_DIM
            run = run[:, POOL_GROUP_DIM if g else 0:]
            run = run + pltpu.roll(run, w // 2, 0)
            cnt = jnp.minimum(pos + 1, w).astype(F32)
            pooled = run[POOL_HALO:, :POOL_GROUP_DIM] / cnt - up[:, lo:lo + POOL_GROUP_DIM]
            pooled = pooled.astype(BF16)
            pooled_ref[:, lo:lo + POOL_GROUP_DIM] = pooled
            mixed.append(_dot(pooled, pw_ref[g].astype(BF16)))
        y_pool = jnp.concatenate(mixed, axis=-1) * ps_ref[...]

        ymix = jnp.concatenate([y_conv, y_pool], axis=-1).astype(BF16)
        ymix_ref[...] = ymix
        x1_ref[...] = x + _dot(ymix, wout_v[...])

    row = lambda w: pl.BlockSpec((tm, w), lambda i: (i, 0))
    return pl.pallas_call(
        body, name="fwd_mix", grid=(n_tiles,),
        in_specs=[row(D_MODEL), _full((1, D_MODEL)), pl.BlockSpec(memory_space=pl.ANY),
                  _full((CONV_WIDTH, D_CONV)), _full((1, D_CONV)), _full((1, D_CONV)), _full((1, D_CONV)),
                  _full((4, POOL_GROUP_DIM, POOL_GROUP_DIM)), _full((1, D_POOL))],
        out_specs=[row(D_MODEL), row(D_IN), row(D_CONV), row(D_POOL), row(D_MODEL), row(D_MODEL)],
        out_shape=[jax.ShapeDtypeStruct((tokens, D_MODEL), F32), jax.ShapeDtypeStruct((tokens, D_IN), F32),
                   jax.ShapeDtypeStruct((tokens, D_CONV), F32), jax.ShapeDtypeStruct((tokens, D_POOL), BF16),
                   jax.ShapeDtypeStruct((tokens, D_MODEL), BF16), jax.ShapeDtypeStruct((tokens, D_MODEL), BF16)],
        scratch_shapes=[pltpu.VMEM((D_IN, D_MODEL), BF16), pltpu.VMEM((D_MODEL, D_MODEL), BF16),
                        pltpu.VMEM((CONV_HALO, D_CONV), F32), pltpu.VMEM((POOL_HALO, D_POOL), F32),
                        pltpu.SemaphoreType.DMA],
        compiler_params=_params(),
    )(x2d, g_mix, gw, conv_w, conv_b, ln_g, ln_b, pool_w, pool_scale)


def _fwd_kv(mem2d, gw, g_mem):
    rows = mem2d.shape[0]
    n_b = rows // N_MEM

    def body(mem_ref, g_ref, gw_hbm, mn_ref, kv_ref, wkv_v, sem):
        @pl.when(pl.program_id(0) == 0)
        def _():
            copies = _load_weight(gw_hbm, "w_kv", wkv_v, sem)
            for cp in copies:
                cp.start()
            for cp in copies:
                cp.wait()

        mh, _ = _rms_fwd(mem_ref[...])
        mn = (mh * g_ref[...]).astype(BF16)
        mn_ref[...] = mn
        kv_ref[...] = _dot_nt(mn, wkv_v[...]).astype(BF16)

    return pl.pallas_call(
        body, name="fwd_kv", grid=(n_b,),
        in_specs=[pl.BlockSpec((N_MEM, D_MODEL), lambda b: (b, 0)), _full((1, D_MODEL)), pl.BlockSpec(memory_space=pl.ANY)],
        out_specs=[pl.BlockSpec((N_MEM, D_MODEL), lambda b: (b, 0)), pl.BlockSpec((N_MEM, 2 * D_MODEL), lambda b: (b, 0))],
        out_shape=[jax.ShapeDtypeStruct((rows, D_MODEL), BF16), jax.ShapeDtypeStruct((rows, 2 * D_MODEL), BF16)],
        scratch_shapes=[pltpu.VMEM((2 * D_MODEL, D_MODEL), BF16), pltpu.SemaphoreType.DMA],
        compiler_params=_params(),
    )(mem2d, g_mem, gw)


def _softmax_rows(s):
    e = jnp.exp(s - jnp.max(s, axis=-1, keepdims=True))
    return e / jnp.sum(e, axis=-1, keepdims=True)


def _fwd_attn(x1, kv, gw, g_x, seq, tm):
    tokens = x1.shape[0]
    n_tiles = tokens // tm
    tps = seq // tm

    def body(x1_ref, kv_ref, g_ref, gw_hbm, x2_ref, h2_ref, q_ref, o_ref, wq_v, wo_v, sem):
        @pl.when(pl.program_id(0) == 0)
        def _():
            copies = _load_weight(gw_hbm, "w_q", wq_v, sem) + _load_weight(gw_hbm, "w_o", wo_v, sem)
            for cp in copies:
                cp.start()
            for cp in copies:
                cp.wait()

        x1v = x1_ref[...]
        xh, _ = _rms_fwd(x1v)
        h2 = (xh * g_ref[...]).astype(BF16)
        h2_ref[...] = h2
        q = (_dot(h2, wq_v[...]) * (HEAD_DIM ** -0.5)).astype(BF16)
        q_ref[...] = q
        outs = []
        for h in range(HEADS):
            lo = h * HEAD_DIM
            p = _softmax_rows(_dot_nt(q[:, lo:lo + HEAD_DIM], kv_ref[:, lo:lo + HEAD_DIM]))
            outs.append(_dot(p.astype(BF16), kv_ref[:, D_MODEL + lo:D_MODEL + lo + HEAD_DIM]))
        o = jnp.concatenate(outs, axis=-1).astype(BF16)
        o_ref[...] = o
        x2_ref[...] = x1v + _dot(o, wo_v[...])

    row = lambda w: pl.BlockSpec((tm, w), lambda i: (i, 0))
    return pl.pallas_call(
        body, name="fwd_attn", grid=(n_tiles,),
        in_specs=[row(D_MODEL), pl.BlockSpec((N_MEM, 2 * D_MODEL), lambda i: (i // tps, 0)), _full((1, D_MODEL)),
                  pl.BlockSpec(memory_space=pl.ANY)],
        out_specs=[row(D_MODEL)] * 4,
        out_shape=[jax.ShapeDtypeStruct((tokens, D_MODEL), F32)] + [jax.ShapeDtypeStruct((tokens, D_MODEL), BF16)] * 3,
        scratch_shapes=[pltpu.VMEM((D_MODEL, D_MODEL), BF16), pltpu.VMEM((D_MODEL, D_MODEL), BF16), pltpu.SemaphoreType.DMA],
        compiler_params=_params(),
    )(x1, kv, g_x, gw)


def _ffn_conv(uu, halo, w_ref, b_ref, cols):
    ext = jnp.concatenate([halo, uu], axis=0)
    p1 = pltpu.roll(ext, 1, 0)[FFN_HALO:, :]
    p2 = pltpu.roll(ext, 2, 0)[FFN_HALO:, :]
    return b_ref[:, cols] + w_ref[2:3, cols] * uu + w_ref[1:2, cols] * p1 + w_ref[0:1, cols] * p2


def _fwd_ffn(x2, target, gw, g_ffn, ffn_w, ffn_b, g_final, seq, tm):
    tokens = x2.shape[0]
    n_tiles = tokens // tm
    tps = seq // tm
    n_chunks = D_FF // FFN_CHUNK

    def body(x2_ref, tgt_ref, gffn_ref, gw_hbm, fw_ref, fb_ref, gfin_ref,
             uu_ref, a_ref, h3_ref, dx3_ref, dx3b_ref, loss_ref, dgfin_ref,
             wup_v, wdown_v, carry, sem):
        i = pl.program_id(0)

        @pl.when(i == 0)
        def _():
            copies = _load_weight(gw_hbm, "w_up", wup_v, sem) + _load_weight(gw_hbm, "w_down", wdown_v, sem)
            for cp in copies:
                cp.start()
            for cp in copies:
                cp.wait()
            loss_ref[...] = jnp.zeros_like(loss_ref)
            dgfin_ref[...] = jnp.zeros_like(dgfin_ref)

        @pl.when(i % tps == 0)
        def _():
            carry[...] = jnp.zeros_like(carry)

        x2v = x2_ref[...]
        xh, _ = _rms_fwd(x2v)
        h3 = (xh * gffn_ref[...]).astype(BF16)
        h3_ref[...] = h3
        acc = jnp.zeros((tm, D_MODEL), F32)
        for jc in range(n_chunks):
            halves = []
            for half in range(2):
                cols = pl.ds(half * D_FF + jc * FFN_CHUNK, FFN_CHUNK)
                uu = _dot_nt(h3, wup_v[cols, :])
                uu_ref[:, cols] = uu
                halves.append(_ffn_conv(uu, carry[:, cols], fw_ref, fb_ref, cols))
                carry[:, cols] = uu[tm - FFN_HALO:, :]
            gate, val = halves
            a = (gate * _sigmoid(gate) * val).astype(BF16)
            a_ref[:, pl.ds(jc * FFN_CHUNK, FFN_CHUNK)] = a
            acc = acc + _dot(a, wdown_v[pl.ds(jc * FFN_CHUNK, FFN_CHUNK), :])
        x3 = x2v + acc

        xh3, r3 = _rms_fwd(x3)
        gfin = gfin_ref[...]
        err = xh3 * gfin - tgt_ref[...]
        loss_ref[...] += jnp.full(loss_ref.shape, jnp.sum(err * err) * (0.5 / D_MODEL), F32)
        dy = err * (1.0 / D_MODEL)
        dgfin_ref[...] += _colsum(dy * xh3)
        dx3 = _rms_bwd(dy, xh3, r3, gfin)
        dx3_ref[...] = dx3
        dx3b_ref[...] = dx3.astype(BF16)

    row = lambda w: pl.BlockSpec((tm, w), lambda i: (i, 0))
    return pl.pallas_call(
        body, name="fwd_ffn", grid=(n_tiles,),
        in_specs=[row(D_MODEL), row(D_MODEL), _full((1, D_MODEL)), pl.BlockSpec(memory_space=pl.ANY),
                  _full((FFN_CONV_WIDTH, 2 * D_FF)), _full((1, 2 * D_FF)), _full((1, D_MODEL))],
        out_specs=[row(2 * D_FF), row(D_FF), row(D_MODEL), row(D_MODEL), row(D_MODEL), _full((8, 128)), _full((1, D_MODEL))],
        out_shape=[jax.ShapeDtypeStruct((tokens, 2 * D_FF), F32), jax.ShapeDtypeStruct((tokens, D_FF), BF16),
                   jax.ShapeDtypeStruct((tokens, D_MODEL), BF16), jax.ShapeDtypeStruct((tokens, D_MODEL), F32),
                   jax.ShapeDtypeStruct((tokens, D_MODEL), BF16),
                   jax.ShapeDtypeStruct((8, 128), F32), jax.ShapeDtypeStruct((1, D_MODEL), F32)],
        scratch_shapes=[pltpu.VMEM((2 * D_FF, D_MODEL), BF16), pltpu.VMEM((D_FF, D_MODEL), BF16),
                        pltpu.VMEM((FFN_HALO, 2 * D_FF), F32), pltpu.SemaphoreType.DMA],
        compiler_params=_params(),
    )(x2, target, g_ffn, gw, ffn_w, ffn_b, g_final)


def _bwd_ffn(dx3, x2, uu_all, gw, g_ffn, ffn_w, ffn_b, seq, tm):
    tokens = x2.shape[0]
    n_tiles = tokens // tm
    tps = seq // tm
    n_chunks = D_FF // FFN_CHUNK
    per8 = tm // FFN_HALO

    def body(dx3_ref, x2_ref, uu_ref, prev_ref, gffn_ref, gw_hbm, fw_ref, fb_ref,
             dx2_ref, dx2b_ref, duu_ref, dfb_ref, dfw_ref, dg_ref,
             wup_v, wdown_v, carry, sem):
        i = pl.program_id(0)
        t = n_tiles - 1 - i

        @pl.when(i == 0)
        def _():
            copies = _load_weight(gw_hbm, "w_up", wup_v, sem) + _load_weight(gw_hbm, "w_down", wdown_v, sem)
            for cp in copies:
                cp.start()
            for cp in copies:
                cp.wait()
            dfb_ref[...] = jnp.zeros_like(dfb_ref)
            dfw_ref[...] = jnp.zeros_like(dfw_ref)
            dg_ref[...] = jnp.zeros_like(dg_ref)

        @pl.when(t % tps == tps - 1)
        def _():
            carry[...] = jnp.zeros_like(carry)

        starts_sequence = (t % tps == 0)
        dx3v = dx3_ref[...]
        dx3b = dx3v.astype(BF16)
        dh3 = jnp.zeros((tm, D_MODEL), F32)
        for jc in range(n_chunks):
            da = _dot_nt(dx3b, wdown_v[pl.ds(jc * FFN_CHUNK, FFN_CHUNK), :])
            uus, ccs, colss = [], [], []
            for half in range(2):
                cols = pl.ds(half * D_FF + jc * FFN_CHUNK, FFN_CHUNK)
                uu = uu_ref[:, cols]
                halo = jnp.where(starts_sequence, 0.0, prev_ref[:, cols])
                uus.append(uu)
                colss.append(cols)
                ccs.append(_ffn_conv(uu, halo, fw_ref, fb_ref, cols))
            gate, val = ccs
            sg = _sigmoid(gate)
            dgate = da * val * (sg * (1.0 + gate * (1.0 - sg)))
            dval = da * (gate * sg)
            for dcc, uu, cols in zip((dgate, dval), uus, colss):
                dfb_ref[:, cols] += _colsum(dcc)
                ext = jnp.concatenate([dcc, carry[:, cols]], axis=0)
                carry[:, cols] = dcc[:FFN_HALO, :]
                n1 = pltpu.roll(ext, tm + FFN_HALO - 1, 0)[:tm, :]
                n2 = pltpu.roll(ext, tm + FFN_HALO - 2, 0)[:tm, :]
                duu = fw_ref[2:3, cols] * dcc + fw_ref[1:2, cols] * n1 + fw_ref[0:1, cols] * n2
                dfw_ref[2:3, cols] += _colsum(uu * dcc)
                dfw_ref[1:2, cols] += _colsum(uu * n1)
                dfw_ref[0:1, cols] += _colsum(uu * n2)
                duub = duu.astype(BF16)
                duu_ref[:, cols] = duub
                dh3 = dh3 + _dot(duub, wup_v[cols, :])
        xh, r = _rms_fwd(x2_ref[...])
        dg_ref[...] += _colsum(dh3 * xh)
        dx2 = dx3v + _rms_bwd(dh3, xh, r, gffn_ref[...])
        dx2_ref[...] = dx2
        dx2b_ref[...] = dx2.astype(BF16)

    rev = lambda w: pl.BlockSpec((tm, w), lambda i: (n_tiles - 1 - i, 0))
    prev = pl.BlockSpec((FFN_HALO, 2 * D_FF), lambda i: (jnp.maximum((n_tiles - 1 - i) * per8 - 1, 0), 0))
    return pl.pallas_call(
        body, name="bwd_ffn", grid=(n_tiles,),
        in_specs=[rev(D_MODEL), rev(D_MODEL), rev(2 * D_FF), prev, _full((1, D_MODEL)), pl.BlockSpec(memory_space=pl.ANY),
                  _full((FFN_CONV_WIDTH, 2 * D_FF)), _full((1, 2 * D_FF))],
        out_specs=[rev(D_MODEL), rev(D_MODEL), rev(2 * D_FF), _full((1, 2 * D_FF)), _full((FFN_CONV_WIDTH, 2 * D_FF)),
                   _full((1, D_MODEL))],
        out_shape=[jax.ShapeDtypeStruct((tokens, D_MODEL), F32), jax.ShapeDtypeStruct((tokens, D_MODEL), BF16),
                   jax.ShapeDtypeStruct((tokens, 2 * D_FF), BF16),
                   jax.ShapeDtypeStruct((1, 2 * D_FF), F32), jax.ShapeDtypeStruct((FFN_CONV_WIDTH, 2 * D_FF), F32),
                   jax.ShapeDtypeStruct((1, D_MODEL), F32)],
        scratch_shapes=[pltpu.VMEM((2 * D_FF, D_MODEL), BF16), pltpu.VMEM((D_FF, D_MODEL), BF16),
                        pltpu.VMEM((FFN_HALO, 2 * D_FF), F32), pltpu.SemaphoreType.DMA],
        compiler_params=_params(),
    )(dx3, x2, uu_all, uu_all, g_ffn, gw, ffn_w, ffn_b)


def _bwd_attn(dx2, x1, q, kv, gw, g_x, seq, tm):
    tokens = x1.shape[0]
    n_tiles = tokens // tm
    tps = seq // tm
    n_b = tokens // seq

    def body(dx2_ref, x1_ref, q_ref, kv_ref, g_ref, gw_hbm, dx1_ref, dx1b_ref, dq_ref, dkv_ref, dg_ref,
             wq_v, wo_v, sem):
        i = pl.program_id(0)

        @pl.when(i == 0)
        def _():
            copies = _load_weight(gw_hbm, "w_q", wq_v, sem) + _load_weight(gw_hbm, "w_o", wo_v, sem)
            for cp in copies:
                cp.start()
            for cp in copies:
                cp.wait()
            dg_ref[...] = jnp.zeros_like(dg_ref)

        @pl.when(i % tps == 0)
        def _():
            dkv_ref[...] = jnp.zeros_like(dkv_ref)

        dx2v = dx2_ref[...]
        do = _dot_nt(dx2v.astype(BF16), wo_v[...]).astype(BF16)
        q = q_ref[...]
        dqs = []
        for h in range(HEADS):
            lo = h * HEAD_DIM
            kcols, vcols = pl.ds(lo, HEAD_DIM), pl.ds(D_MODEL + lo, HEAD_DIM)
            qh, doh = q[:, lo:lo + HEAD_DIM], do[:, lo:lo + HEAD_DIM]
            p = _softmax_rows(_dot_nt(qh, kv_ref[:, kcols]))
            dp = _dot_nt(doh, kv_ref[:, vcols])
            dkv_ref[:, vcols] += _dot_tn(p.astype(BF16), doh)
            ds = (p * (dp - jnp.sum(dp * p, axis=-1, keepdims=True))).astype(BF16)
            dqs.append(_dot(ds, kv_ref[:, kcols]) * (HEAD_DIM ** -0.5))
            dkv_ref[:, kcols] += _dot_tn(ds, qh)
        dq = jnp.concatenate(dqs, axis=-1).astype(BF16)
        dq_ref[...] = dq
        dh2 = _dot_nt(dq, wq_v[...])
        xh, r = _rms_fwd(x1_ref[...])
        dg_ref[...] += _colsum(dh2 * xh)
        dx1 = dx2v + _rms_bwd(dh2, xh, r, g_ref[...])
        dx1_ref[...] = dx1
        dx1b_ref[...] = dx1.astype(BF16)

    row = lambda w: pl.BlockSpec((tm, w), lambda i: (i, 0))
    per_b = pl.BlockSpec((N_MEM, 2 * D_MODEL), lambda i: (i // tps, 0))
    return pl.pallas_call(
        body, name="bwd_attn", grid=(n_tiles,),
        in_specs=[row(D_MODEL), row(D_MODEL), row(D_MODEL), per_b, _full((1, D_MODEL)), pl.BlockSpec(memory_space=pl.ANY)],
        out_specs=[row(D_MODEL), row(D_MODEL), row(D_MODEL), per_b, _full((1, D_MODEL))],
        out_shape=[jax.ShapeDtypeStruct((tokens, D_MODEL), F32), jax.ShapeDtypeStruct((tokens, D_MODEL), BF16),
                   jax.ShapeDtypeStruct((tokens, D_MODEL), BF16),
                   jax.ShapeDtypeStruct((n_b * N_MEM, 2 * D_MODEL), F32), jax.ShapeDtypeStruct((1, D_MODEL), F32)],
        scratch_shapes=[pltpu.VMEM((D_MODEL, D_MODEL), BF16), pltpu.VMEM((D_MODEL, D_MODEL), BF16), pltpu.SemaphoreType.DMA],
        compiler_params=_params(),
    )(dx2, x1, q, kv, g_x, gw)


def _bwd_kv(dkv, mem2d, gw, g_mem):
    rows = mem2d.shape[0]
    n_b = rows // N_MEM

    def body(dkv_ref, mem_ref, gw_hbm, dkvb_ref, dg_ref, wkv_v, sem):
        @pl.when(pl.program_id(0) == 0)
        def _():
            copies = _load_weight(gw_hbm, "w_kv", wkv_v, sem)
            for cp in copies:
                cp.start()
            for cp in copies:
                cp.wait()
            dg_ref[...] = jnp.zeros_like(dg_ref)

        dkvb = dkv_ref[...].astype(BF16)
        dkvb_ref[...] = dkvb
        dmn = _dot(dkvb, wkv_v[...])
        mh, _ = _rms_fwd(mem_ref[...])
        dg_ref[...] += _colsum(dmn * mh)

    del g_mem
    return pl.pallas_call(
        body, name="bwd_kv", grid=(n_b,),
        in_specs=[pl.BlockSpec((N_MEM, 2 * D_MODEL), lambda b: (b, 0)), pl.BlockSpec((N_MEM, D_MODEL), lambda b: (b, 0)),
                  pl.BlockSpec(memory_space=pl.ANY)],
        out_specs=[pl.BlockSpec((N_MEM, 2 * D_MODEL), lambda b: (b, 0)), _full((1, D_MODEL))],
        out_shape=[jax.ShapeDtypeStruct((rows, 2 * D_MODEL), BF16), jax.ShapeDtypeStruct((1, D_MODEL), F32)],
        scratch_shapes=[pltpu.VMEM((2 * D_MODEL, D_MODEL), BF16), pltpu.SemaphoreType.DMA],
        compiler_params=_params(),
    )(dkv, mem2d, gw)


def _bwd_mix(dx1, x2d, u_all, c_all, pooled_all, gw, g_mix, conv_w, ln_g, ln_b, pool_w, pool_scale, seq, tm):
    tokens = x2d.shape[0]
    n_tiles = tokens // tm
    tps = seq // tm

    def body(dx1_ref, x_ref, u_ref, c_ref, pooled_ref, gmix_ref, gw_hbm, cw_ref, lng_ref, lnb_ref, pw_ref, ps_ref,
             dx_ref, du_ref, dgmix_ref, dcw_ref, dcb_ref, dlng_ref, dlnb_ref, dpw_ref, dps_ref,
             win_v, wout_v, dc_carry, e_carry, sem):
        i = pl.program_id(0)
        t = n_tiles - 1 - i

        @pl.when(i == 0)
        def _():
            copies = _load_weight(gw_hbm, "w_in", win_v, sem) + _load_weight(gw_hbm, "w_out", wout_v, sem)
            for cp in copies:
                cp.start()
            for cp in copies:
                cp.wait()
            for ref in (dgmix_ref, dcw_ref, dcb_ref, dlng_ref, dlnb_ref, dpw_ref, dps_ref):
                ref[...] = jnp.zeros_like(ref)

        @pl.when(t % tps == tps - 1)
        def _():
            dc_carry[...] = jnp.zeros_like(dc_carry)
            e_carry[...] = jnp.zeros_like(e_carry)

        dx1v = dx1_ref[...]
        dymix = _dot_nt(dx1v.astype(BF16), wout_v[...])
        dyc, dyp = dymix[:, :D_CONV], dymix[:, D_CONV:]
        u = u_ref[...]
        val, gate = u[:, :D_CONV], u[:, D_CONV:2 * D_CONV]

        conv = c_ref[...]
        mu = jnp.mean(conv, axis=-1, keepdims=True)
        cen = conv - mu
        rs = lax.rsqrt(jnp.mean(cen * cen, axis=-1, keepdims=True) + EPS)
        chat = cen * rs
        ln = chat * lng_ref[...] + lnb_ref[...]
        sl = _sigmoid(ln)
        dln = dyc * (sl * (1.0 + ln * (1.0 - sl)))
        dlng_ref[...] += _colsum(dln * chat)
        dlnb_ref[...] += _colsum(dln)
        dchat = dln * lng_ref[...]
        dc = rs * (dchat - jnp.mean(dchat, axis=-1, keepdims=True)
                   - chat * jnp.mean(dchat * chat, axis=-1, keepdims=True))
        dcb_ref[...] += _colsum(dc)
        sg = _sigmoid(gate)
        hc = val * sg
        ext = jnp.concatenate([dc, dc_carry[...]], axis=0)
        dc_carry[...] = dc[:CONV_HALO, :]
        dhc = jnp.zeros((tm, D_CONV), F32)
        for k in range(CONV_WIDTH):
            ahead = CONV_WIDTH - 1 - k
            tap = (ext if ahead == 0 else pltpu.roll(ext, tm + CONV_HALO - ahead, 0))[:tm, :]
            dhc = dhc + cw_ref[k:k + 1, :] * tap
            dcw_ref[k:k + 1, :] += _colsum(hc * tap)
        du_ref[:, :D_CONV] = (dhc * sg).astype(BF16)
        du_ref[:, D_CONV:2 * D_CONV] = (dhc * val * (sg * (1.0 - sg))).astype(BF16)

        pos = lax.broadcasted_iota(jnp.int32, (tm, 1), 0) + (t % tps) * tm
        es, dpooled = [], []
        for g, w in enumerate(POOL_WINDOWS):
            cols = pl.ds(g * POOL_GROUP_DIM, POOL_GROUP_DIM)
            lo = g * POOL_GROUP_DIM
            pooled = pooled_ref[:, cols]
            pw = pw_ref[g].astype(BF16)
            dyg = dyp[:, lo:lo + POOL_GROUP_DIM]
            dps_ref[:, cols] += _colsum(dyg * _dot(pooled, pw))
            dmixed = (dyg * ps_ref[:, cols]).astype(BF16)
            dpw_ref[g] += _dot_tn(pooled, dmixed)
            dpo = _dot_nt(dmixed, pw)
            dpooled.append(dpo)
            es.append(dpo / jnp.minimum(pos + 1, w).astype(F32))
        e = jnp.concatenate(es, axis=-1)
        run = jnp.concatenate([e, e_carry[...]], axis=0)
        e_carry[...] = e[:POOL_HALO, :]
        rows = tm + POOL_HALO
        for g, w in enumerate(POOL_WINDOWS):
            lo = g * POOL_GROUP_DIM
            run = run[:, POOL_GROUP_DIM if g else 0:]
            run = run + pltpu.roll(run, rows - w // 2, 0)
            du_ref[:, 2 * D_CONV + lo:2 * D_CONV + lo + POOL_GROUP_DIM] = (
                run[:tm, :POOL_GROUP_DIM] - dpooled[g]).astype(BF16)

        dh1 = _dot(du_ref[...], win_v[...])
        xh, r = _rms_fwd(x_ref[...])
        dgmix_ref[...] += _colsum(dh1 * xh)
        dx_ref[...] = dx1v + _rms_bwd(dh1, xh, r, gmix_ref[...])

    rev = lambda w: pl.BlockSpec((tm, w), lambda i: (n_tiles - 1 - i, 0))
    return pl.pallas_call(
        body, name="bwd_mix", grid=(n_tiles,),
        in_specs=[rev(D_MODEL), rev(D_MODEL), rev(D_IN), rev(D_CONV), rev(D_POOL), _full((1, D_MODEL)),
                  pl.BlockSpec(memory_space=pl.ANY), _full((CONV_WIDTH, D_CONV)), _full((1, D_CONV)), _full((1, D_CONV)),
                  _full((4, POOL_GROUP_DIM, POOL_GROUP_DIM)), _full((1, D_POOL))],
        out_specs=[rev(D_MODEL), rev(D_IN), _full((1, D_MODEL)), _full((CONV_WIDTH, D_CONV)), _full((1, D_CONV)),
                   _full((1, D_CONV)), _full((1, D_CONV)), _full((4, POOL_GROUP_DIM, POOL_GROUP_DIM)), _full((1, D_POOL))],
        out_shape=[jax.ShapeDtypeStruct((tokens, D_MODEL), F32), jax.ShapeDtypeStruct((tokens, D_IN), BF16),
                   jax.ShapeDtypeStruct((1, D_MODEL), F32), jax.ShapeDtypeStruct((CONV_WIDTH, D_CONV), F32),
                   jax.ShapeDtypeStruct((1, D_CONV), F32), jax.ShapeDtypeStruct((1, D_CONV), F32),
                   jax.ShapeDtypeStruct((1, D_CONV), F32),
                   jax.ShapeDtypeStruct((4, POOL_GROUP_DIM, POOL_GROUP_DIM), F32), jax.ShapeDtypeStruct((1, D_POOL), F32)],
        scratch_shapes=[pltpu.VMEM((D_IN, D_MODEL), BF16), pltpu.VMEM((D_MODEL, D_MODEL), BF16),
                        pltpu.VMEM((CONV_HALO, D_CONV), F32), pltpu.VMEM((POOL_HALO, D_POOL), F32),
                        pltpu.SemaphoreType.DMA],
        compiler_params=_params(),
    )(dx1, x2d, u_all, c_all, pooled_all, g_mix, gw, conv_w, ln_g, ln_b, pool_w, pool_scale)


def _wgrad(a, b, name, tm=256):
    tokens, m = a.shape
    n = b.shape[1]

    def body(a_ref, b_ref, out_ref):
        out_ref[...] = _dot_tn(a_ref[...], b_ref[...]).astype(out_ref.dtype)

    return pl.pallas_call(
        body, name=name, grid=(m // tm,),
        in_specs=[pl.BlockSpec((tokens, tm), lambda i: (0, i)), _full((tokens, n))],
        out_specs=pl.BlockSpec((tm, n), lambda i: (i, 0)),
        out_shape=jax.ShapeDtypeStruct((m, n), BF16),
        compiler_params=_params(),
    )(a, b)


def _adamw(w, g, m, v, name):
    rows, cols = w.shape
    tile = rows
    for cand in (512, 256, 128, 64, 32, 16, 8):
        if rows % cand == 0:
            tile = cand
            break

    def body(w_ref, g_ref, m_ref, v_ref, d_ref, nm_ref, nv_ref):
        gv = g_ref[...]
        nm = ADAM_B1 * m_ref[...] + (1.0 - ADAM_B1) * gv
        nv = ADAM_B2 * v_ref[...] + (1.0 - ADAM_B2) * (gv * gv)
        m_hat = nm / (1.0 - ADAM_B1 ** ADAM_STEP)
        v_hat = nv / (1.0 - ADAM_B2 ** ADAM_STEP)
        d_ref[...] = -ADAM_LR * (m_hat / (jnp.sqrt(v_hat) + ADAM_EPS) + ADAM_WD * w_ref[...])
        nm_ref[...] = nm
        nv_ref[...] = nv

    spec = pl.BlockSpec((tile, cols), lambda i: (i, 0))
    return pl.pallas_call(
        body, name=name, grid=(rows // tile,),
        in_specs=[spec] * 4, out_specs=[spec] * 3,
        out_shape=[jax.ShapeDtypeStruct((rows, cols), F32)] * 3,
        compiler_params=_params(("arbitrary",)),
    )(w, g, m, v)


SMALL = (("norm_mix_g", (1, 1024)), ("conv_dw_b", (1, 512)), ("conv_ln_g", (1, 512)), ("conv_ln_b", (1, 512)),
         ("pool_w", (1, 4, 128, 128)), ("pool_scale", (1, 512)), ("norm_xattn_g", (1, 1024)), ("norm_mem_g", (1, 1024)),
         ("norm_ffn_g", (1, 1024)), ("ffn_dw_b", (1, 5632)), ("norm_final_g", (1024,)))
LANES = 128


def _pack_rows(arrs):
    flat = jnp.concatenate([a.reshape(-1) for a in arrs])
    pad = (-flat.shape[0]) % (8 * LANES)
    return jnp.pad(flat, (0, pad)).reshape(-1, LANES)


def kernel(x, mem, norm_mix_g, w_in, conv_dw_w, conv_dw_b, conv_ln_g, conv_ln_b, pool_w, pool_scale, w_out, norm_xattn_g, norm_mem_g, w_q, w_kv, w_o, norm_ffn_g, w_up, ffn_dw_w, ffn_dw_b, w_down, norm_final_g, loss_target, m_norm_mix_g, m_w_in, m_conv_dw_w, m_conv_dw_b, m_conv_ln_g, m_conv_ln_b, m_pool_w, m_pool_scale, m_w_out, m_norm_xattn_g, m_norm_mem_g, m_w_q, m_w_kv, m_w_o, m_norm_ffn_g, m_w_up, m_ffn_dw_w, m_ffn_dw_b, m_w_down, m_norm_final_g, v_norm_mix_g, v_w_in, v_conv_dw_w, v_conv_dw_b, v_conv_ln_g, v_conv_ln_b, v_pool_w, v_pool_scale, v_w_out, v_norm_xattn_g, v_norm_mem_g, v_w_q, v_w_kv, v_w_o, v_norm_ffn_g, v_w_up, v_ffn_dw_w, v_ffn_dw_b, v_w_down, v_norm_final_g):
    weights = dict(norm_mix_g=norm_mix_g, w_in=w_in, conv_dw_w=conv_dw_w, conv_dw_b=conv_dw_b, conv_ln_g=conv_ln_g,
                   conv_ln_b=conv_ln_b, pool_w=pool_w, pool_scale=pool_scale, w_out=w_out, norm_xattn_g=norm_xattn_g,
                   norm_mem_g=norm_mem_g, w_q=w_q, w_kv=w_kv, w_o=w_o, norm_ffn_g=norm_ffn_g, w_up=w_up,
                   ffn_dw_w=ffn_dw_w, ffn_dw_b=ffn_dw_b, w_down=w_down, norm_final_g=norm_final_g)
    moments_m = dict(norm_mix_g=m_norm_mix_g, w_in=m_w_in, conv_dw_w=m_conv_dw_w, conv_dw_b=m_conv_dw_b,
                     conv_ln_g=m_conv_ln_g, conv_ln_b=m_conv_ln_b, pool_w=m_pool_w, pool_scale=m_pool_scale,
                     w_out=m_w_out, norm_xattn_g=m_norm_xattn_g, norm_mem_g=m_norm_mem_g, w_q=m_w_q, w_kv=m_w_kv,
                     w_o=m_w_o, norm_ffn_g=m_norm_ffn_g, w_up=m_w_up, ffn_dw_w=m_ffn_dw_w, ffn_dw_b=m_ffn_dw_b,
                     w_down=m_w_down, norm_final_g=m_norm_final_g)
    moments_v = dict(norm_mix_g=v_norm_mix_g, w_in=v_w_in, conv_dw_w=v_conv_dw_w, conv_dw_b=v_conv_dw_b,
                     conv_ln_g=v_conv_ln_g, conv_ln_b=v_conv_ln_b, pool_w=v_pool_w, pool_scale=v_pool_scale,
                     w_out=v_w_out, norm_xattn_g=v_norm_xattn_g, norm_mem_g=v_norm_mem_g, w_q=v_w_q, w_kv=v_w_kv,
                     w_o=v_w_o, norm_ffn_g=v_norm_ffn_g, w_up=v_w_up, ffn_dw_w=v_ffn_dw_w, ffn_dw_b=v_ffn_dw_b,
                     w_down=v_w_down, norm_final_g=v_norm_final_g)
    order = list(weights)
    transposed = ("w_in", "w_kv", "w_up")

    n_b, seq, _ = x.shape
    tokens = n_b * seq
    tm_mix = min(512, seq // 2)
    tm_ffn = min(256, seq // 2)
    dev = 4 * lax.axis_index("x") + 2 * lax.axis_index("y") + lax.axis_index("c")

    shards = [weights[n][0].T if n in transposed else weights[n][0] for n, _ in W_ROWS]
    packed = jnp.concatenate(shards, axis=0).astype(BF16)
    small_sharded = _pack_rows([conv_dw_w[0], ffn_dw_w[0]])
    gw, gsmall = _all_gather([packed, small_sharded], "weights_all_gather")
    gflat = gsmall.reshape(N_DEV, -1)
    n_cw = CONV_WIDTH * (D_CONV // N_DEV)
    n_fw = FFN_CONV_WIDTH * (2 * D_FF // N_DEV)
    conv_w = gflat[:, :n_cw].reshape(N_DEV, CONV_WIDTH, D_CONV // N_DEV).transpose(1, 0, 2).reshape(CONV_WIDTH, D_CONV)
    ffn_w = gflat[:, n_cw:n_cw + n_fw].reshape(N_DEV, FFN_CONV_WIDTH, 2 * D_FF // N_DEV).transpose(1, 0, 2).reshape(
        FFN_CONV_WIDTH, 2 * D_FF)

    x2d = x.reshape(tokens, D_MODEL)
    mem2d = mem.reshape(n_b * N_MEM, D_MODEL)
    tgt2d = loss_target.reshape(tokens, D_MODEL)
    g_final = norm_final_g.reshape(1, D_MODEL)

    x1, u_all, c_all, pooled_all, ymix, h1 = _fwd_mix(
        x2d, gw, norm_mix_g, conv_w, conv_dw_b, conv_ln_g, conv_ln_b, pool_w[0], pool_scale, seq, tm_mix)
    mem_n, kv = _fwd_kv(mem2d, gw, norm_mem_g)
    x2, h2, q, o = _fwd_attn(x1, kv, gw, norm_xattn_g, seq, tm_mix)
    uu_all, a_all, h3, dx3, dx3b, loss_part, dg_final = _fwd_ffn(
        x2, tgt2d, gw, norm_ffn_g, ffn_w, ffn_dw_b, g_final, seq, tm_ffn)
    loss = lax.psum(loss_part[0, 0], ("x", "y", "c"))

    dx2, dx2b, duu, d_ffn_b, d_ffn_w, dg_ffn = _bwd_ffn(dx3, x2, uu_all, gw, norm_ffn_g, ffn_w, ffn_dw_b, seq, tm_ffn)
    dx1, dx1b, dq, dkv, dg_x = _bwd_attn(dx2, x1, q, kv, gw, norm_xattn_g, seq, tm_mix)
    dkv_b, dg_mem = _bwd_kv(dkv, mem2d, gw, norm_mem_g)
    dx, du, dg_mix, d_conv_w, d_conv_b, d_ln_g, d_ln_b, d_pool_w, d_pool_scale = _bwd_mix(
        dx1, x2d, u_all, c_all, pooled_all, gw, norm_mix_g, conv_w, conv_ln_g, conv_ln_b, pool_w[0], pool_scale,
        seq, tm_mix)
    grad_x = dx.reshape(x.shape)

    big = dict(
        w_in=_wgrad(du, h1, "wgrad_w_in"), w_out=_wgrad(ymix, dx1b, "wgrad_w_out"), w_q=_wgrad(h2, dq, "wgrad_w_q"),
        w_kv=_wgrad(dkv_b, mem_n, "wgrad_w_kv"), w_o=_wgrad(o, dx2b, "wgrad_w_o"), w_up=_wgrad(duu, h3, "wgrad_w_up"),
        w_down=_wgrad(a_all, dx3b, "wgrad_w_down"))

    g_packed = jnp.concatenate([big[n].reshape(N_DEV, r, D_MODEL) for n, r in W_ROWS], axis=1)
    table = _owner_table()
    from_sibling = _exchange_sibling(g_packed)
    chip_sums = _chip_partial_sums(table, g_packed, from_sibling, 944)
    from_chips = _exchange_chips(chip_sums)
    g_mine = _final_grad_sums(table, g_packed, from_sibling, from_chips, 944)

    small_grads = dict(norm_mix_g=dg_mix, conv_dw_b=d_conv_b, conv_ln_g=d_ln_g, conv_ln_b=d_ln_b, pool_w=d_pool_w,
                       pool_scale=d_pool_scale, norm_xattn_g=dg_x, norm_mem_g=dg_mem, norm_ffn_g=dg_ffn,
                       ffn_dw_b=d_ffn_b, norm_final_g=dg_final)
    small_list = [small_grads[n] for n, _ in SMALL] + [d_conv_w, d_ffn_w]
    (small_all,) = _all_gather([_pack_rows(small_list)], "small_grads_all_gather")
    small_sum = _sum_blocks(small_all).reshape(-1)

    grads = {}
    pos = 0
    for n, shape in SMALL:
        size = 1
        for s in shape:
            size *= s
        grads[n] = small_sum[pos:pos + size].reshape(shape)
        pos += size
    full_conv_w = small_sum[pos:pos + CONV_WIDTH * D_CONV].reshape(CONV_WIDTH, D_CONV)
    pos += CONV_WIDTH * D_CONV
    full_ffn_w = small_sum[pos:pos + FFN_CONV_WIDTH * 2 * D_FF].reshape(FFN_CONV_WIDTH, 2 * D_FF)
    grads["conv_dw_w"] = lax.dynamic_slice_in_dim(full_conv_w, dev * (D_CONV // N_DEV), D_CONV // N_DEV, axis=1)[None]
    grads["ffn_dw_w"] = lax.dynamic_slice_in_dim(full_ffn_w, dev * (2 * D_FF // N_DEV), 2 * D_FF // N_DEV, axis=1)[None]
    for n, (off, r) in W_OFF.items():
        blk = g_mine[off:off + r]
        grads[n] = (blk.T if n in transposed else blk)[None]

    delta, new_m, new_v = {}, {}, {}
    for n, _ in W_ROWS:
        shape = weights[n].shape
        as2d = lambda t: t.reshape(shape[1], shape[2])
        d, nm, nv = _adamw(as2d(weights[n]), as2d(grads[n]), as2d(moments_m[n]), as2d(moments_v[n]), "adamw_" + n)
        delta[n], new_m[n], new_v[n] = d.reshape(shape), nm.reshape(shape), nv.reshape(shape)
    small_names = [n for n in order if n not in W_OFF]
    packs = [_pack_rows([t[n] for n in small_names]) for t in (weights, grads, moments_m, moments_v)]
    outs = _adamw(*packs, "adamw_small")
    for res, out in zip((delta, new_m, new_v), outs):
        flat = out.reshape(-1)
        pos = 0
        for n in small_names:
            size = weights[n].size
            res[n] = flat[pos:pos + size].reshape(weights[n].shape)
            pos += size

    return (loss, grad_x, *[grads[n] for n in order], *[delta[n] for n in order],
            *[new_m[n] for n in order], *[new_v[n] for n in order])
```

```python
import functools

import jax
import jax.numpy as jnp
from jax import lax
from jax.experimental import pallas as pl
from jax.experimental.pallas import tpu as pltpu

F32 = jnp.float32
BF16 = jnp.bfloat16
MESH = pl.DeviceIdType.MESH

N_DEV = 8
D_MODEL = 1024
D_CONV = 512
D_POOL = 512
CONV_WIDTH = 31
POOL_WINDOWS = (2, 4, 8, 16)
POOL_GROUP_DIM = 128
D_IN = 1536
N_MEM = 256
HEADS = 4
HEAD_DIM = 256
D_FF = 2816
FFN_CONV_WIDTH = 3
EPS = 1e-6
ADAM_LR = 0.001
ADAM_B1 = 0.9
ADAM_B2 = 0.999
ADAM_EPS = 1e-08
ADAM_WD = 0.01
ADAM_STEP = 10

VMEM_LIMIT_V7X = 56 * 1024 * 1024
CONV_HALO = 32
POOL_HALO = 16
FFN_HALO = 8
FFN_CHUNK = 1408

W_ROWS = (("w_in", 192), ("w_out", 128), ("w_q", 128), ("w_kv", 256), ("w_o", 128), ("w_up", 704), ("w_down", 352))
W_OFF = {}
_o = 0
for _n, _r in W_ROWS:
    W_OFF[_n] = (_o, _r)
    _o += _r
PACK_ROWS = _o


def _dot(a, b):
    return jnp.dot(a, b, preferred_element_type=F32)


def _dot_nt(a, b):
    return lax.dot_general(a, b, (((1,), (1,)), ((), ())), preferred_element_type=F32)


def _dot_tn(a, b):
    return lax.dot_general(a, b, (((0,), (0,)), ((), ())), preferred_element_type=F32)


def _sigmoid(v):
    return 1.0 / (1.0 + jnp.exp(-v))


def _rms_fwd(v):
    r = lax.rsqrt(jnp.mean(v * v, axis=-1, keepdims=True) + EPS)
    return v * r, r


def _rms_bwd(dh, vh, r, g):
    gd = dh * g
    return r * (gd - vh * jnp.mean(gd * vh, axis=-1, keepdims=True))


def _colsum(v):
    return jnp.sum(v, axis=0, keepdims=True)


def _full(shape):
    return pl.BlockSpec(shape, lambda *_: (0,) * len(shape))


def _params(sem=("arbitrary",), vmem=VMEM_LIMIT_V7X):
    return pltpu.CompilerParams(dimension_semantics=sem, vmem_limit_bytes=vmem)


def _load_weight(g_hbm, name, dst, sem):
    off, rows = W_OFF[name]
    return [pltpu.make_async_copy(g_hbm.at[d, pl.ds(off, rows), :], dst.at[pl.ds(d * rows, rows), :], sem)
            for d in range(N_DEV)]


def _position():
    x, y, c = lax.axis_index("x"), lax.axis_index("y"), lax.axis_index("c")
    chips = [(1 - x, y), (x, 1 - y), (1 - x, 1 - y)]
    return x, y, c, chips


def _dev(px, py, pc):
    return 4 * px + 2 * py + pc


def _all_gather(arrs, name):
    n = len(arrs)

    def body(*refs):
        ins, outs = refs[:n], refs[n:2 * n]
        send_sems, recv_sems, local_sems = refs[2 * n:]
        x, y, c, chips = _position()
        me, sibling = (x, y, c), (x, y, 1 - c)

        def copy(a, k, block, to, src=None):
            rows = outs[a].at[_dev(*block)]
            return pltpu.make_async_remote_copy(
                src_ref=rows if src is None else src, dst_ref=rows,
                send_sem=send_sems.at[a, k], recv_sem=recv_sems.at[a, k], device_id=to, device_id_type=MESH)

        started = []
        for a in range(n):
            mine = pltpu.make_async_copy(ins[a], outs[a].at[_dev(*me)], local_sems.at[a])
            mine.start()
            started.append(mine)
        sends = []
        for a in range(n):
            first = [copy(a, 0, me, sibling, src=ins[a])]
            first += [copy(a, 1 + j, me, (*chip, c), src=ins[a]) for j, chip in enumerate(chips)]
            for cp in first:
                cp.start()
            sends += first
        for j, chip in enumerate(chips):
            for a in range(n):
                copy(a, 1 + j, (*chip, c), me).wait_recv()
                passed = copy(a, 4 + j, (*chip, c), sibling)
                passed.start()
                sends.append(passed)
        for a in range(n):
            copy(a, 0, sibling, me).wait_recv()
            for j, chip in enumerate(chips):
                copy(a, 4 + j, (*chip, 1 - c), me).wait_recv()
        for cp in sends:
            cp.wait_send()
        for mine in started:
            mine.wait()

    any_spec = pl.BlockSpec(memory_space=pl.ANY)
    return pl.pallas_call(
        body, name=name,
        out_shape=[jax.ShapeDtypeStruct((N_DEV,) + a.shape, a.dtype) for a in arrs],
        in_specs=[any_spec] * n, out_specs=[any_spec] * n,
        scratch_shapes=[pltpu.SemaphoreType.DMA((n, 7)), pltpu.SemaphoreType.DMA((n, 7)), pltpu.SemaphoreType.DMA((n,))],
    )(*arrs)


def _exchange_sibling(g):
    _, rows, cols = g.shape

    def body(g_ref, out_ref, send_sems, recv_sems):
        x, y, c, chips = _position()
        sibling = (x, y, 1 - c)
        copies = []
        for j, chip in enumerate([(x, y)] + chips):
            cp = pltpu.make_async_remote_copy(
                src_ref=g_ref.at[_dev(*chip, 1 - c)], dst_ref=out_ref.at[j],
                send_sem=send_sems.at[j], recv_sem=recv_sems.at[j], device_id=sibling, device_id_type=MESH)
            cp.start()
            copies.append(cp)
        for cp in copies:
            cp.wait_recv()
        for cp in copies:
            cp.wait_send()

    any_spec = pl.BlockSpec(memory_space=pl.ANY)
    return pl.pallas_call(
        body, name="rs_sibling_exchange",
        out_shape=jax.ShapeDtypeStruct((4, rows, cols), g.dtype),
        in_specs=[any_spec], out_specs=any_spec,
        scratch_shapes=[pltpu.SemaphoreType.DMA((4,)), pltpu.SemaphoreType.DMA((4,))],
    )(g)


def _exchange_chips(s):
    _, rows, cols = s.shape

    def body(s_ref, out_ref, send_sems, recv_sems):
        x, y, c, chips = _position()
        copies = []
        for j, chip in enumerate(chips):
            cp = pltpu.make_async_remote_copy(
                src_ref=s_ref.at[j], dst_ref=out_ref.at[j],
                send_sem=send_sems.at[j], recv_sem=recv_sems.at[j], device_id=(*chip, c), device_id_type=MESH)
            cp.start()
            copies.append(cp)
        for cp in copies:
            cp.wait_recv()
        for cp in copies:
            cp.wait_send()

    any_spec = pl.BlockSpec(memory_space=pl.ANY)
    return pl.pallas_call(
        body, name="rs_chip_exchange",
        out_shape=jax.ShapeDtypeStruct((3, rows, cols), s.dtype),
        in_specs=[any_spec], out_specs=any_spec,
        scratch_shapes=[pltpu.SemaphoreType.DMA((3,)), pltpu.SemaphoreType.DMA((3,))],
    )(s)


def _owner_table():
    x, y, c = lax.axis_index("x"), lax.axis_index("y"), lax.axis_index("c")
    chips = [(x, y), (1 - x, y), (x, 1 - y), (1 - x, 1 - y)]
    return jnp.stack([_dev(px, py, c) for px, py in chips]).astype(jnp.int32)


def _chip_partial_sums(table, g, from_sibling, tile):
    _, rows, cols = g.shape

    def body(tab_ref, g_ref, l_ref, out_ref):
        del tab_ref
        out_ref[...] = (g_ref[...].astype(F32) + l_ref[...].astype(F32)).astype(out_ref.dtype)

    grid_spec = pltpu.PrefetchScalarGridSpec(
        num_scalar_prefetch=1, grid=(3, rows // tile),
        in_specs=[pl.BlockSpec((None, tile, cols), lambda j, t, tab: (tab[j + 1], t, 0)),
                  pl.BlockSpec((None, tile, cols), lambda j, t, tab: (j + 1, t, 0))],
        out_specs=pl.BlockSpec((None, tile, cols), lambda j, t, tab: (j, t, 0)))
    return pl.pallas_call(
        body, name="rs_chip_partial_sums", grid_spec=grid_spec,
        out_shape=jax.ShapeDtypeStruct((3, rows, cols), BF16),
        compiler_params=_params(("arbitrary", "arbitrary")),
    )(table, g, from_sibling)


def _final_grad_sums(table, g, from_sibling, from_chips, tile):
    _, rows, cols = g.shape

    def body(tab_ref, g_ref, l_ref, c_ref, out_ref):
        del tab_ref
        acc = g_ref[...].astype(F32) + l_ref[...].astype(F32)
        for j in range(3):
            acc = acc + c_ref[j].astype(F32)
        out_ref[...] = acc

    grid_spec = pltpu.PrefetchScalarGridSpec(
        num_scalar_prefetch=1, grid=(rows // tile,),
        in_specs=[pl.BlockSpec((None, tile, cols), lambda t, tab: (tab[0], t, 0)),
                  pl.BlockSpec((None, tile, cols), lambda t, tab: (0, t, 0)),
                  pl.BlockSpec((3, tile, cols), lambda t, tab: (0, t, 0))],
        out_specs=pl.BlockSpec((tile, cols), lambda t, tab: (t, 0)))
    return pl.pallas_call(
        body, name="rs_final_sums", grid_spec=grid_spec,
        out_shape=jax.ShapeDtypeStruct((rows, cols), F32),
        compiler_params=_params(("arbitrary",)),
    )(table, g, from_sibling, from_chips)


def _sum_blocks(g8):
    _, rows, cols = g8.shape

    def body(g_ref, out_ref):
        acc = g_ref[0]
        for d in range(1, N_DEV):
            acc = acc + g_ref[d]
        out_ref[...] = acc

    return pl.pallas_call(
        body, name="small_grad_sum", grid=(1,),
        in_specs=[_full((N_DEV, rows, cols))], out_specs=_full((rows, cols)),
        out_shape=jax.ShapeDtypeStruct((rows, cols), F32),
        compiler_params=_params(("arbitrary",)),
    )(g8)


def _fwd_mix(x2d, gw, g_mix, conv_w, conv_b, ln_g, ln_b, pool_w, pool_scale, seq, tm):
    tokens = x2d.shape[0]
    n_tiles = tokens // tm
    tps = seq // tm

    def body(x_ref, gmix_ref, gw_hbm, cw_ref, cb_ref, lng_ref, lnb_ref, pw_ref, ps_ref,
             x1_ref, u_ref, c_ref, pooled_ref, ymix_ref, h1_ref,
             win_v, wout_v, hc_carry, up_carry, sem):
        i = pl.program_id(0)

        @pl.when(i == 0)
        def _():
            copies = _load_weight(gw_hbm, "w_in", win_v, sem) + _load_weight(gw_hbm, "w_out", wout_v, sem)
            for cp in copies:
                cp.start()
            for cp in copies:
                cp.wait()

        @pl.when(i % tps == 0)
        def _():
            hc_carry[...] = jnp.zeros_like(hc_carry)
            up_carry[...] = jnp.zeros_like(up_carry)

        x = x_ref[...]
        xh, _ = _rms_fwd(x)
        h1 = (xh * gmix_ref[...]).astype(BF16)
        h1_ref[...] = h1
        u = _dot_nt(h1, win_v[...])
        u_ref[...] = u
        val, gate, up = u[:, :D_CONV], u[:, D_CONV:2 * D_CONV], u[:, 2 * D_CONV:]

        hc = val * _sigmoid(gate)
        ext = jnp.concatenate([hc_carry[...], hc], axis=0)
        hc_carry[...] = hc[tm - CONV_HALO:, :]
        conv = jnp.broadcast_to(cb_ref[...], (tm, D_CONV))
        for k in range(CONV_WIDTH):
            shift = CONV_WIDTH - 1 - k
            tap = ext if shift == 0 else pltpu.roll(ext, shift, 0)
            conv = conv + cw_ref[k:k + 1, :] * tap[CONV_HALO:, :]
        c_ref[...] = conv
        mu = jnp.mean(conv, axis=-1, keepdims=True)
        cen = conv - mu
        ln = cen * lax.rsqrt(jnp.mean(cen * cen, axis=-1, keepdims=True) + EPS) * lng_ref[...] + lnb_ref[...]
        y_conv = ln * _sigmoid(ln)

        extp = jnp.concatenate([up_carry[...], up], axis=0)
        up_carry[...] = up[tm - POOL_HALO:, :]
        pos = lax.broadcasted_iota(jnp.int32, (tm, 1), 0) + (i % tps) * tm
        run = extp
        mixed = []
        for g, w in enumerate(POOL_WINDOWS):
            lo = g * POOL_GROUP_DIM
            run = run[:, POOL_GROUP_DIM if g else 0:]
            run = run + pltpu.roll(run, w // 2, 0)
            cnt = jnp.minimum(pos + 1, w).astype(F32)
            pooled = run[POOL_HALO:, :POOL_GROUP_DIM] / cnt - up[:, lo:lo + POOL_GROUP_DIM]
            pooled = pooled.astype(BF16)
            pooled_ref[:, lo:lo + POOL_GROUP_DIM] = pooled
            mixed.append(_dot(pooled, pw_ref[g].astype(BF16)))
        y_pool = jnp.concatenate(mixed, axis=-1) * ps_ref[...]

        ymix = jnp.concatenate([y_conv, y_pool], axis=-1).astype(BF16)
        ymix_ref[...] = ymix
        x1_ref[...] = x + _dot(ymix, wout_v[...])

    row = lambda w: pl.BlockSpec((tm, w), lambda i: (i, 0))
    return pl.pallas_call(
        body, name="fwd_mix", grid=(n_tiles,),
        in_specs=[row(D_MODEL), _full((1, D_MODEL)), pl.BlockSpec(memory_space=pl.ANY),
                  _full((CONV_WIDTH, D_CONV)), _full((1, D_CONV)), _full((1, D_CONV)), _full((1, D_CONV)),
                  _full((4, POOL_GROUP_DIM, POOL_GROUP_DIM)), _full((1, D_POOL))],
        out_specs=[row(D_MODEL), row(D_IN), row(D_CONV), row(D_POOL), row(D_MODEL), row(D_MODEL)],
        out_shape=[jax.ShapeDtypeStruct((tokens, D_MODEL), F32), jax.ShapeDtypeStruct((tokens, D_IN), F32),
                   jax.ShapeDtypeStruct((tokens, D_CONV), F32), jax.ShapeDtypeStruct((tokens, D_POOL), BF16),
                   jax.ShapeDtypeStruct((tokens, D_MODEL), BF16), jax.ShapeDtypeStruct((tokens, D_MODEL), BF16)],
        scratch_shapes=[pltpu.VMEM((D_IN, D_MODEL), BF16), pltpu.VMEM((D_MODEL, D_MODEL), BF16),
                        pltpu.VMEM((CONV_HALO, D_CONV), F32), pltpu.VMEM((POOL_HALO, D_POOL), F32),
                        pltpu.SemaphoreType.DMA],
        compiler_params=_params(),
    )(x2d, g_mix, gw, conv_w, conv_b, ln_g, ln_b, pool_w, pool_scale)


def _fwd_kv(mem2d, gw, g_mem):
    rows = mem2d.shape[0]
    n_b = rows // N_MEM

    def body(mem_ref, g_ref, gw_hbm, mn_ref, kv_ref, wkv_v, sem):
        @pl.when(pl.program_id(0) == 0)
        def _():
            copies = _load_weight(gw_hbm, "w_kv", wkv_v, sem)
            for cp in copies:
                cp.start()
            for cp in copies:
                cp.wait()

        mh, _ = _rms_fwd(mem_ref[...])
        mn = (mh * g_ref[...]).astype(BF16)
        mn_ref[...] = mn
        kv_ref[...] = _dot_nt(mn, wkv_v[...]).astype(BF16)

    return pl.pallas_call(
        body, name="fwd_kv", grid=(n_b,),
        in_specs=[pl.BlockSpec((N_MEM, D_MODEL), lambda b: (b, 0)), _full((1, D_MODEL)), pl.BlockSpec(memory_space=pl.ANY)],
        out_specs=[pl.BlockSpec((N_MEM, D_MODEL), lambda b: (b, 0)), pl.BlockSpec((N_MEM, 2 * D_MODEL), lambda b: (b, 0))],
        out_shape=[jax.ShapeDtypeStruct((rows, D_MODEL), BF16), jax.ShapeDtypeStruct((rows, 2 * D_MODEL), BF16)],
        scratch_shapes=[pltpu.VMEM((2 * D_MODEL, D_MODEL), BF16), pltpu.SemaphoreType.DMA],
        compiler_params=_params(),
    )(mem2d, g_mem, gw)


def _softmax_rows(s):
    e = jnp.exp(s - jnp.max(s, axis=-1, keepdims=True))
    return e / jnp.sum(e, axis=-1, keepdims=True)


def _fwd_attn(x1, kv, gw, g_x, seq, tm):
    tokens = x1.shape[0]
    n_tiles = tokens // tm
    tps = seq // tm

    def body(x1_ref, kv_ref, g_ref, gw_hbm, x2_ref, h2_ref, q_ref, o_ref, wq_v, wo_v, sem):
        @pl.when(pl.program_id(0) == 0)
        def _():
            copies = _load_weight(gw_hbm, "w_q", wq_v, sem) + _load_weight(gw_hbm, "w_o", wo_v, sem)
            for cp in copies:
                cp.start()
            for cp in copies:
                cp.wait()

        x1v = x1_ref[...]
        xh, _ = _rms_fwd(x1v)
        h2 = (xh * g_ref[...]).astype(BF16)
        h2_ref[...] = h2
        q = (_dot(h2, wq_v[...]) * (HEAD_DIM ** -0.5)).astype(BF16)
        q_ref[...] = q
        outs = []
        for h in range(HEADS):
            lo = h * HEAD_DIM
            p = _softmax_rows(_dot_nt(q[:, lo:lo + HEAD_DIM], kv_ref[:, lo:lo + HEAD_DIM]))
            outs.append(_dot(p.astype(BF16), kv_ref[:, D_MODEL + lo:D_MODEL + lo + HEAD_DIM]))
        o = jnp.concatenate(outs, axis=-1).astype(BF16)
        o_ref[...] = o
        x2_ref[...] = x1v + _dot(o, wo_v[...])

    row = lambda w: pl.BlockSpec((tm, w), lambda i: (i, 0))
    return pl.pallas_call(
        body, name="fwd_attn", grid=(n_tiles,),
        in_specs=[row(D_MODEL), pl.BlockSpec((N_MEM, 2 * D_MODEL), lambda i: (i // tps, 0)), _full((1, D_MODEL)),
                  pl.BlockSpec(memory_space=pl.ANY)],
        out_specs=[row(D_MODEL)] * 4,
        out_shape=[jax.ShapeDtypeStruct((tokens, D_MODEL), F32)] + [jax.ShapeDtypeStruct((tokens, D_MODEL), BF16)] * 3,
        scratch_shapes=[pltpu.VMEM((D_MODEL, D_MODEL), BF16), pltpu.VMEM((D_MODEL, D_MODEL), BF16), pltpu.SemaphoreType.DMA],
        compiler_params=_params(),
    )(x1, kv, g_x, gw)


def _ffn_conv(uu, halo, w_ref, b_ref, cols):
    ext = jnp.concatenate([halo, uu], axis=0)
    p1 = pltpu.roll(ext, 1, 0)[FFN_HALO:, :]
    p2 = pltpu.roll(ext, 2, 0)[FFN_HALO:, :]
    return b_ref[:, cols] + w_ref[2:3, cols] * uu + w_ref[1:2, cols] * p1 + w_ref[0:1, cols] * p2


def _fwd_ffn(x2, target, gw, g_ffn, ffn_w, ffn_b, g_final, seq, tm):
    tokens = x2.shape[0]
    n_tiles = tokens // tm
    tps = seq // tm
    n_chunks = D_FF // FFN_CHUNK

    def body(x2_ref, tgt_ref, gffn_ref, gw_hbm, fw_ref, fb_ref, gfin_ref,
             uu_ref, a_ref, h3_ref, dx3_ref, dx3b_ref, loss_ref, dgfin_ref,
             wup_v, wdown_v, carry, sem):
        i = pl.program_id(0)

        @pl.when(i == 0)
        def _():
            copies = _load_weight(gw_hbm, "w_up", wup_v, sem) + _load_weight(gw_hbm, "w_down", wdown_v, sem)
            for cp in copies:
                cp.start()
            for cp in copies:
                cp.wait()
            loss_ref[...] = jnp.zeros_like(loss_ref)
            dgfin_ref[...] = jnp.zeros_like(dgfin_ref)

        @pl.when(i % tps == 0)
        def _():
            carry[...] = jnp.zeros_like(carry)

        x2v = x2_ref[...]
        xh, _ = _rms_fwd(x2v)
        h3 = (xh * gffn_ref[...]).astype(BF16)
        h3_ref[...] = h3
        acc = jnp.zeros((tm, D_MODEL), F32)
        for jc in range(n_chunks):
            halves = []
            for half in range(2):
                cols = pl.ds(half * D_FF + jc * FFN_CHUNK, FFN_CHUNK)
                uu = _dot_nt(h3, wup_v[cols, :])
                uu_ref[:, cols] = uu
                halves.append(_ffn_conv(uu, carry[:, cols], fw_ref, fb_ref, cols))
                carry[:, cols] = uu[tm - FFN_HALO:, :]
            gate, val = halves
            a = (gate * _sigmoid(gate) * val).astype(BF16)
            a_ref[:, pl.ds(jc * FFN_CHUNK, FFN_CHUNK)] = a
            acc = acc + _dot(a, wdown_v[pl.ds(jc * FFN_CHUNK, FFN_CHUNK), :])
        x3 = x2v + acc

        xh3, r3 = _rms_fwd(x3)
        gfin = gfin_ref[...]
        err = xh3 * gfin - tgt_ref[...]
        loss_ref[...] += jnp.full(loss_ref.shape, jnp.sum(err * err) * (0.5 / D_MODEL), F32)
        dy = err * (1.0 / D_MODEL)
        dgfin_ref[...] += _colsum(dy * xh3)
        dx3 = _rms_bwd(dy, xh3, r3, gfin)
        dx3_ref[...] = dx3
        dx3b_ref[...] = dx3.astype(BF16)

    row = lambda w: pl.BlockSpec((tm, w), lambda i: (i, 0))
    return pl.pallas_call(
        body, name="fwd_ffn", grid=(n_tiles,),
        in_specs=[row(D_MODEL), row(D_MODEL), _full((1, D_MODEL)), pl.BlockSpec(memory_space=pl.ANY),
                  _full((FFN_CONV_WIDTH, 2 * D_FF)), _full((1, 2 * D_FF)), _full((1, D_MODEL))],
        out_specs=[row(2 * D_FF), row(D_FF), row(D_MODEL), row(D_MODEL), row(D_MODEL), _full((8, 128)), _full((1, D_MODEL))],
        out_shape=[jax.ShapeDtypeStruct((tokens, 2 * D_FF), F32), jax.ShapeDtypeStruct((tokens, D_FF), BF16),
                   jax.ShapeDtypeStruct((tokens, D_MODEL), BF16), jax.ShapeDtypeStruct((tokens, D_MODEL), F32),
                   jax.ShapeDtypeStruct((tokens, D_MODEL), BF16),
                   jax.ShapeDtypeStruct((8, 128), F32), jax.ShapeDtypeStruct((1, D_MODEL), F32)],
        scratch_shapes=[pltpu.VMEM((2 * D_FF, D_MODEL), BF16), pltpu.VMEM((D_FF, D_MODEL), BF16),
                        pltpu.VMEM((FFN_HALO, 2 * D_FF), F32), pltpu.SemaphoreType.DMA],
        compiler_params=_params(),
    )(x2, target, g_ffn, gw, ffn_w, ffn_b, g_final)


def _bwd_ffn(dx3, x2, uu_all, gw, g_ffn, ffn_w, ffn_b, seq, tm):
    tokens = x2.shape[0]
    n_tiles = tokens // tm
    tps = seq // tm
    n_chunks = D_FF // FFN_CHUNK
    per8 = tm // FFN_HALO

    def body(dx3_ref, x2_ref, uu_ref, prev_ref, gffn_ref, gw_hbm, fw_ref, fb_ref,
             dx2_ref, dx2b_ref, duu_ref, dfb_ref, dfw_ref, dg_ref,
             wup_v, wdown_v, carry, sem):
        i = pl.program_id(0)
        t = n_tiles - 1 - i

        @pl.when(i == 0)
        def _():
            copies = _load_weight(gw_hbm, "w_up", wup_v, sem) + _load_weight(gw_hbm, "w_down", wdown_v, sem)
            for cp in copies:
                cp.start()
            for cp in copies:
                cp.wait()
            dfb_ref[...] = jnp.zeros_like(dfb_ref)
            dfw_ref[...] = jnp.zeros_like(dfw_ref)
            dg_ref[...] = jnp.zeros_like(dg_ref)

        @pl.when(t % tps == tps - 1)
        def _():
            carry[...] = jnp.zeros_like(carry)

        starts_sequence = (t % tps == 0)
        dx3v = dx3_ref[...]
        dx3b = dx3v.astype(BF16)
        dh3 = jnp.zeros((tm, D_MODEL), F32)
        for jc in range(n_chunks):
            da = _dot_nt(dx3b, wdown_v[pl.ds(jc * FFN_CHUNK, FFN_CHUNK), :])
            uus, ccs, colss = [], [], []
            for half in range(2):
                cols = pl.ds(half * D_FF + jc * FFN_CHUNK, FFN_CHUNK)
                uu = uu_ref[:, cols]
                halo = jnp.where(starts_sequence, 0.0, prev_ref[:, cols])
                uus.append(uu)
                colss.append(cols)
                ccs.append(_ffn_conv(uu, halo, fw_ref, fb_ref, cols))
            gate, val = ccs
            sg = _sigmoid(gate)
            dgate = da * val * (sg * (1.0 + gate * (1.0 - sg)))
            dval = da * (gate * sg)
            for dcc, uu, cols in zip((dgate, dval), uus, colss):
                dfb_ref[:, cols] += _colsum(dcc)
                ext = jnp.concatenate([dcc, carry[:, cols]], axis=0)
                carry[:, cols] = dcc[:FFN_HALO, :]
                n1 = pltpu.roll(ext, tm + FFN_HALO - 1, 0)[:tm, :]
                n2 = pltpu.roll(ext, tm + FFN_HALO - 2, 0)[:tm, :]
                duu = fw_ref[2:3, cols] * dcc + fw_ref[1:2, cols] * n1 + fw_ref[0:1, cols] * n2
                dfw_ref[2:3, cols] += _colsum(uu * dcc)
                dfw_ref[1:2, cols] += _colsum(uu * n1)
                dfw_ref[0:1, cols] += _colsum(uu * n2)
                duub = duu.astype(BF16)
                duu_ref[:, cols] = duub
                dh3 = dh3 + _dot(duub, wup_v[cols, :])
        xh, r = _rms_fwd(x2_ref[...])
        dg_ref[...] += _colsum(dh3 * xh)
        dx2 = dx3v + _rms_bwd(dh3, xh, r, gffn_ref[...])
        dx2_ref[...] = dx2
        dx2b_ref[...] = dx2.astype(BF16)

    rev = lambda w: pl.BlockSpec((tm, w), lambda i: (n_tiles - 1 - i, 0))
    prev = pl.BlockSpec((FFN_HALO, 2 * D_FF), lambda i: (jnp.maximum((n_tiles - 1 - i) * per8 - 1, 0), 0))
    return pl.pallas_call(
        body, name="bwd_ffn", grid=(n_tiles,),
        in_specs=[rev(D_MODEL), rev(D_MODEL), rev(2 * D_FF), prev, _full((1, D_MODEL)), pl.BlockSpec(memory_space=pl.ANY),
                  _full((FFN_CONV_WIDTH, 2 * D_FF)), _full((1, 2 * D_FF))],
        out_specs=[rev(D_MODEL), rev(D_MODEL), rev(2 * D_FF), _full((1, 2 * D_FF)), _full((FFN_CONV_WIDTH, 2 * D_FF)),
                   _full((1, D_MODEL))],
        out_shape=[jax.ShapeDtypeStruct((tokens, D_MODEL), F32), jax.ShapeDtypeStruct((tokens, D_MODEL), BF16),
                   jax.ShapeDtypeStruct((tokens, 2 * D_FF), BF16),
                   jax.ShapeDtypeStruct((1, 2 * D_FF), F32), jax.ShapeDtypeStruct((FFN_CONV_WIDTH, 2 * D_FF), F32),
                   jax.ShapeDtypeStruct((1, D_MODEL), F32)],
        scratch_shapes=[pltpu.VMEM((2 * D_FF, D_MODEL), BF16), pltpu.VMEM((D_FF, D_MODEL), BF16),
                        pltpu.VMEM((FFN_HALO, 2 * D_FF), F32), pltpu.SemaphoreType.DMA],
        compiler_params=_params(),
    )(dx3, x2, uu_all, uu_all, g_ffn, gw, ffn_w, ffn_b)


def _bwd_attn(dx2, x1, q, kv, gw, g_x, seq, tm):
    tokens = x1.shape[0]
    n_tiles = tokens // tm
    tps = seq // tm
    n_b = tokens // seq

    def body(dx2_ref, x1_ref, q_ref, kv_ref, g_ref, gw_hbm, dx1_ref, dx1b_ref, dq_ref, dkv_ref, dg_ref,
             wq_v, wo_v, sem):
        i = pl.program_id(0)

        @pl.when(i == 0)
        def _():
            copies = _load_weight(gw_hbm, "w_q", wq_v, sem) + _load_weight(gw_hbm, "w_o", wo_v, sem)
            for cp in copies:
                cp.start()
            for cp in copies:
                cp.wait()
            dg_ref[...] = jnp.zeros_like(dg_ref)

        @pl.when(i % tps == 0)
        def _():
            dkv_ref[...] = jnp.zeros_like(dkv_ref)

        dx2v = dx2_ref[...]
        do = _dot_nt(dx2v.astype(BF16), wo_v[...]).astype(BF16)
        q = q_ref[...]
        dqs = []
        for h in range(HEADS):
            lo = h * HEAD_DIM
            kcols, vcols = pl.ds(lo, HEAD_DIM), pl.ds(D_MODEL + lo, HEAD_DIM)
            qh, doh = q[:, lo:lo + HEAD_DIM], do[:, lo:lo + HEAD_DIM]
            p = _softmax_rows(_dot_nt(qh, kv_ref[:, kcols]))
            dp = _dot_nt(doh, kv_ref[:, vcols])
            dkv_ref[:, vcols] += _dot_tn(p.astype(BF16), doh)
            ds = (p * (dp - jnp.sum(dp * p, axis=-1, keepdims=True))).astype(BF16)
            dqs.append(_dot(ds, kv_ref[:, kcols]) * (HEAD_DIM ** -0.5))
            dkv_ref[:, kcols] += _dot_tn(ds, qh)
        dq = jnp.concatenate(dqs, axis=-1).astype(BF16)
        dq_ref[...] = dq
        dh2 = _dot_nt(dq, wq_v[...])
        xh, r = _rms_fwd(x1_ref[...])
        dg_ref[...] += _colsum(dh2 * xh)
        dx1 = dx2v + _rms_bwd(dh2, xh, r, g_ref[...])
        dx1_ref[...] = dx1
        dx1b_ref[...] = dx1.astype(BF16)

    row = lambda w: pl.BlockSpec((tm, w), lambda i: (i, 0))
    per_b = pl.BlockSpec((N_MEM, 2 * D_MODEL), lambda i: (i // tps, 0))
    return pl.pallas_call(
        body, name="bwd_attn", grid=(n_tiles,),
        in_specs=[row(D_MODEL), row(D_MODEL), row(D_MODEL), per_b, _full((1, D_MODEL)), pl.BlockSpec(memory_space=pl.ANY)],
        out_specs=[row(D_MODEL), row(D_MODEL), row(D_MODEL), per_b, _full((1, D_MODEL))],
        out_shape=[jax.ShapeDtypeStruct((tokens, D_MODEL), F32), jax.ShapeDtypeStruct((tokens, D_MODEL), BF16),
                   jax.ShapeDtypeStruct((tokens, D_MODEL), BF16),
                   jax.ShapeDtypeStruct((n_b * N_MEM, 2 * D_MODEL), F32), jax.ShapeDtypeStruct((1, D_MODEL), F32)],
        scratch_shapes=[pltpu.VMEM((D_MODEL, D_MODEL), BF16), pltpu.VMEM((D_MODEL, D_MODEL), BF16), pltpu.SemaphoreType.DMA],
        compiler_params=_params(),
    )(dx2, x1, q, kv, g_x, gw)


def _bwd_kv(dkv, mem2d, gw, g_mem):
    rows = mem2d.shape[0]
    n_b = rows // N_MEM

    def body(dkv_ref, mem_ref, gw_hbm, dkvb_ref, dg_ref, wkv_v, sem):
        @pl.when(pl.program_id(0) == 0)
        def _():
            copies = _load_weight(gw_hbm, "w_kv", wkv_v, sem)
            for cp in copies:
                cp.start()
            for cp in copies:
                cp.wait()
            dg_ref[...] = jnp.zeros_like(dg_ref)

        dkvb = dkv_ref[...].astype(BF16)
        dkvb_ref[...] = dkvb
        dmn = _dot(dkvb, wkv_v[...])
        mh, _ = _rms_fwd(mem_ref[...])
        dg_ref[...] += _colsum(dmn * mh)

    del g_mem
    return pl.pallas_call(
        body, name="bwd_kv", grid=(n_b,),
        in_specs=[pl.BlockSpec((N_MEM, 2 * D_MODEL), lambda b: (b, 0)), pl.BlockSpec((N_MEM, D_MODEL), lambda b: (b, 0)),
                  pl.BlockSpec(memory_space=pl.ANY)],
        out_specs=[pl.BlockSpec((N_MEM, 2 * D_MODEL), lambda b: (b, 0)), _full((1, D_MODEL))],
        out_shape=[jax.ShapeDtypeStruct((rows, 2 * D_MODEL), BF16), jax.ShapeDtypeStruct((1, D_MODEL), F32)],
        scratch_shapes=[pltpu.VMEM((2 * D_MODEL, D_MODEL), BF16), pltpu.SemaphoreType.DMA],
        compiler_params=_params(),
    )(dkv, mem2d, gw)


def _bwd_mix(dx1, x2d, u_all, c_all, pooled_all, gw, g_mix, conv_w, ln_g, ln_b, pool_w, pool_scale, seq, tm):
    tokens = x2d.shape[0]
    n_tiles = tokens // tm
    tps = seq // tm

    def body(dx1_ref, x_ref, u_ref, c_ref, pooled_ref, gmix_ref, gw_hbm, cw_ref, lng_ref, lnb_ref, pw_ref, ps_ref,
             dx_ref, du_ref, dgmix_ref, dcw_ref, dcb_ref, dlng_ref, dlnb_ref, dpw_ref, dps_ref,
             win_v, wout_v, dc_carry, e_carry, sem):
        i = pl.program_id(0)
        t = n_tiles - 1 - i

        @pl.when(i == 0)
        def _():
            copies = _load_weight(gw_hbm, "w_in", win_v, sem) + _load_weight(gw_hbm, "w_out", wout_v, sem)
            for cp in copies:
                cp.start()
            for cp in copies:
                cp.wait()
            for ref in (dgmix_ref, dcw_ref, dcb_ref, dlng_ref, dlnb_ref, dpw_ref, dps_ref):
                ref[...] = jnp.zeros_like(ref)

        @pl.when(t % tps == tps - 1)
        def _():
            dc_carry[...] = jnp.zeros_like(dc_carry)
            e_carry[...] = jnp.zeros_like(e_carry)

        dx1v = dx1_ref[...]
        dymix = _dot_nt(dx1v.astype(BF16), wout_v[...])
        dyc, dyp = dymix[:, :D_CONV], dymix[:, D_CONV:]
        u = u_ref[...]
        val, gate = u[:, :D_CONV], u[:, D_CONV:2 * D_CONV]

        conv = c_ref[...]
        mu = jnp.mean(conv, axis=-1, keepdims=True)
        cen = conv - mu
        rs = lax.rsqrt(jnp.mean(cen * cen, axis=-1, keepdims=True) + EPS)
        chat = cen * rs
        ln = chat * lng_ref[...] + lnb_ref[...]
        sl = _sigmoid(ln)
        dln = dyc * (sl * (1.0 + ln * (1.0 - sl)))
        dlng_ref[...] += _colsum(dln * chat)
        dlnb_ref[...] += _colsum(dln)
        dchat = dln * lng_ref[...]
        dc = rs * (dchat - jnp.mean(dchat, axis=-1, keepdims=True)
                   - chat * jnp.mean(dchat * chat, axis=-1, keepdims=True))
        dcb_ref[...] += _colsum(dc)
        sg = _sigmoid(gate)
        hc = val * sg
        ext = jnp.concatenate([dc, dc_carry[...]], axis=0)
        dc_carry[...] = dc[:CONV_HALO, :]
        dhc = jnp.zeros((tm, D_CONV), F32)
        for k in range(CONV_WIDTH):
            ahead = CONV_WIDTH - 1 - k
            tap = (ext if ahead == 0 else pltpu.roll(ext, tm + CONV_HALO - ahead, 0))[:tm, :]
            dhc = dhc + cw_ref[k:k + 1, :] * tap
            dcw_ref[k:k + 1, :] += _colsum(hc * tap)
        du_ref[:, :D_CONV] = (dhc * sg).astype(BF16)
        du_ref[:, D_CONV:2 * D_CONV] = (dhc * val * (sg * (1.0 - sg))).astype(BF16)

        pos = lax.broadcasted_iota(jnp.int32, (tm, 1), 0) + (t % tps) * tm
        es, dpooled = [], []
        for g, w in enumerate(POOL_WINDOWS):
            cols = pl.ds(g * POOL_GROUP_DIM, POOL_GROUP_DIM)
            lo = g * POOL_GROUP_DIM
            pooled = pooled_ref[:, cols]
            pw = pw_ref[g].astype(BF16)
            dyg = dyp[:, lo:lo + POOL_GROUP_DIM]
            dps_ref[:, cols] += _colsum(dyg * _dot(pooled, pw))
            dmixed = (dyg * ps_ref[:, cols]).astype(BF16)
            dpw_ref[g] += _dot_tn(pooled, dmixed)
            dpo = _dot_nt(dmixed, pw)
            dpooled.append(dpo)
            es.append(dpo / jnp.minimum(pos + 1, w).astype(F32))
        e = jnp.concatenate(es, axis=-1)
        run = jnp.concatenate([e, e_carry[...]], axis=0)
        e_carry[...] = e[:POOL_HALO, :]
        rows = tm + POOL_HALO
        for g, w in enumerate(POOL_WINDOWS):
            lo = g * POOL_GROUP_DIM
            run = run[:, POOL_GROUP_DIM if g else 0:]
            run = run + pltpu.roll(run, rows - w // 2, 0)
            du_ref[:, 2 * D_CONV + lo:2 * D_CONV + lo + POOL_GROUP_DIM] = (
                run[:tm, :POOL_GROUP_DIM] - dpooled[g]).astype(BF16)

        dh1 = _dot(du_ref[...], win_v[...])
        xh, r = _rms_fwd(x_ref[...])
        dgmix_ref[...] += _colsum(dh1 * xh)
        dx_ref[...] = dx1v + _rms_bwd(dh1, xh, r, gmix_ref[...])

    rev = lambda w: pl.BlockSpec((tm, w), lambda i: (n_tiles - 1 - i, 0))
    return pl.pallas_call(
        body, name="bwd_mix", grid=(n_tiles,),
        in_specs=[rev(D_MODEL), rev(D_MODEL), rev(D_IN), rev(D_CONV), rev(D_POOL), _full((1, D_MODEL)),
                  pl.BlockSpec(memory_space=pl.ANY), _full((CONV_WIDTH, D_CONV)), _full((1, D_CONV)), _full((1, D_CONV)),
                  _full((4, POOL_GROUP_DIM, POOL_GROUP_DIM)), _full((1, D_POOL))],
        out_specs=[rev(D_MODEL), rev(D_IN), _full((1, D_MODEL)), _full((CONV_WIDTH, D_CONV)), _full((1, D_CONV)),
                   _full((1, D_CONV)), _full((1, D_CONV)), _full((4, POOL_GROUP_DIM, POOL_GROUP_DIM)), _full((1, D_POOL))],
        out_shape=[jax.ShapeDtypeStruct((tokens, D_MODEL), F32), jax.ShapeDtypeStruct((tokens, D_IN), BF16),
                   jax.ShapeDtypeStruct((1, D_MODEL), F32), jax.ShapeDtypeStruct((CONV_WIDTH, D_CONV), F32),
                   jax.ShapeDtypeStruct((1, D_CONV), F32), jax.ShapeDtypeStruct((1, D_CONV), F32),
                   jax.ShapeDtypeStruct((1, D_CONV), F32),
                   jax.ShapeDtypeStruct((4, POOL_GROUP_DIM, POOL_GROUP_DIM), F32), jax.ShapeDtypeStruct((1, D_POOL), F32)],
        scratch_shapes=[pltpu.VMEM((D_IN, D_MODEL), BF16), pltpu.VMEM((D_MODEL, D_MODEL), BF16),
                        pltpu.VMEM((CONV_HALO, D_CONV), F32), pltpu.VMEM((POOL_HALO, D_POOL), F32),
                        pltpu.SemaphoreType.DMA],
        compiler_params=_params(),
    )(dx1, x2d, u_all, c_all, pooled_all, g_mix, gw, conv_w, ln_g, ln_b, pool_w, pool_scale)


def _wgrad(a, b, name, tm=256):
    tokens, m = a.shape
    n = b.shape[1]

    def body(a_ref, b_ref, out_ref):
        out_ref[...] = _dot_tn(a_ref[...], b_ref[...]).astype(out_ref.dtype)

    return pl.pallas_call(
        body, name=name, grid=(m // tm,),
        in_specs=[pl.BlockSpec((tokens, tm), lambda i: (0, i)), _full((tokens, n))],
        out_specs=pl.BlockSpec((tm, n), lambda i: (i, 0)),
        out_shape=jax.ShapeDtypeStruct((m, n), BF16),
        compiler_params=_params(),
    )(a, b)


def _adamw_update(w, g, m, v):
    nm = ADAM_B1 * m + (1.0 - ADAM_B1) * g
    nv = ADAM_B2 * v + (1.0 - ADAM_B2) * (g * g)
    m_hat = nm / (1.0 - ADAM_B1 ** ADAM_STEP)
    v_hat = nv / (1.0 - ADAM_B2 ** ADAM_STEP)
    return -ADAM_LR * (m_hat / (jnp.sqrt(v_hat) + ADAM_EPS) + ADAM_WD * w), nm, nv


def _adamw_small(ws, gs, ms, vs):
    n = len(ws)

    def body(*refs):
        ins, outs = refs[:4 * n], refs[4 * n:]
        for k in range(n):
            d, nm, nv = _adamw_update(*[ins[j * n + k][...] for j in range(4)])
            outs[k][...] = d
            outs[n + k][...] = nm
            outs[2 * n + k][...] = nv

    vmem = pl.BlockSpec(memory_space=pltpu.VMEM)
    outs = pl.pallas_call(
        body, name="adamw_small",
        in_specs=[vmem] * (4 * n), out_specs=[vmem] * (3 * n),
        out_shape=[jax.ShapeDtypeStruct(w.shape, F32) for w in ws] * 3,
    )(*ws, *gs, *ms, *vs)
    return outs[:n], outs[n:2 * n], outs[2 * n:]


def _adamw(w, g, m, v, name):
    rows, cols = w.shape
    tile = rows
    for cand in (512, 256, 128, 64, 32, 16, 8):
        if rows % cand == 0:
            tile = cand
            break

    def body(w_ref, g_ref, m_ref, v_ref, d_ref, nm_ref, nv_ref):
        d_ref[...], nm_ref[...], nv_ref[...] = _adamw_update(w_ref[...], g_ref[...], m_ref[...], v_ref[...])

    spec = pl.BlockSpec((tile, cols), lambda i: (i, 0))
    return pl.pallas_call(
        body, name=name, grid=(rows // tile,),
        in_specs=[spec] * 4, out_specs=[spec] * 3,
        out_shape=[jax.ShapeDtypeStruct((rows, cols), F32)] * 3,
        compiler_params=_params(("arbitrary",)),
    )(w, g, m, v)


SMALL = (("norm_mix_g", (1, 1024)), ("conv_dw_b", (1, 512)), ("conv_ln_g", (1, 512)), ("conv_ln_b", (1, 512)),
         ("pool_w", (1, 4, 128, 128)), ("pool_scale", (1, 512)), ("norm_xattn_g", (1, 1024)), ("norm_mem_g", (1, 1024)),
         ("norm_ffn_g", (1, 1024)), ("ffn_dw_b", (1, 5632)), ("norm_final_g", (1024,)))
LANES = 128


def _pack_rows(arrs):
    flat = jnp.concatenate([a.reshape(-1) for a in arrs])
    pad = (-flat.shape[0]) % (8 * LANES)
    return jnp.pad(flat, (0, pad)).reshape(-1, LANES)


def kernel(x, mem, norm_mix_g, w_in, conv_dw_w, conv_dw_b, conv_ln_g, conv_ln_b, pool_w, pool_scale, w_out, norm_xattn_g, norm_mem_g, w_q, w_kv, w_o, norm_ffn_g, w_up, ffn_dw_w, ffn_dw_b, w_down, norm_final_g, loss_target, m_norm_mix_g, m_w_in, m_conv_dw_w, m_conv_dw_b, m_conv_ln_g, m_conv_ln_b, m_pool_w, m_pool_scale, m_w_out, m_norm_xattn_g, m_norm_mem_g, m_w_q, m_w_kv, m_w_o, m_norm_ffn_g, m_w_up, m_ffn_dw_w, m_ffn_dw_b, m_w_down, m_norm_final_g, v_norm_mix_g, v_w_in, v_conv_dw_w, v_conv_dw_b, v_conv_ln_g, v_conv_ln_b, v_pool_w, v_pool_scale, v_w_out, v_norm_xattn_g, v_norm_mem_g, v_w_q, v_w_kv, v_w_o, v_norm_ffn_g, v_w_up, v_ffn_dw_w, v_ffn_dw_b, v_w_down, v_norm_final_g):
    weights = dict(norm_mix_g=norm_mix_g, w_in=w_in, conv_dw_w=conv_dw_w, conv_dw_b=conv_dw_b, conv_ln_g=conv_ln_g,
                   conv_ln_b=conv_ln_b, pool_w=pool_w, pool_scale=pool_scale, w_out=w_out, norm_xattn_g=norm_xattn_g,
                   norm_mem_g=norm_mem_g, w_q=w_q, w_kv=w_kv, w_o=w_o, norm_ffn_g=norm_ffn_g, w_up=w_up,
                   ffn_dw_w=ffn_dw_w, ffn_dw_b=ffn_dw_b, w_down=w_down, norm_final_g=norm_final_g)
    moments_m = dict(norm_mix_g=m_norm_mix_g, w_in=m_w_in, conv_dw_w=m_conv_dw_w, conv_dw_b=m_conv_dw_b,
                     conv_ln_g=m_conv_ln_g, conv_ln_b=m_conv_ln_b, pool_w=m_pool_w, pool_scale=m_pool_scale,
                     w_out=m_w_out, norm_xattn_g=m_norm_xattn_g, norm_mem_g=m_norm_mem_g, w_q=m_w_q, w_kv=m_w_kv,
                     w_o=m_w_o, norm_ffn_g=m_norm_ffn_g, w_up=m_w_up, ffn_dw_w=m_ffn_dw_w, ffn_dw_b=m_ffn_dw_b,
                     w_down=m_w_down, norm_final_g=m_norm_final_g)
    moments_v = dict(norm_mix_g=v_norm_mix_g, w_in=v_w_in, conv_dw_w=v_conv_dw_w, conv_dw_b=v_conv_dw_b,
                     conv_ln_g=v_conv_ln_g, conv_ln_b=v_conv_ln_b, pool_w=v_pool_w, pool_scale=v_pool_scale,
                     w_out=v_w_out, norm_xattn_g=v_norm_xattn_g, norm_mem_g=v_norm_mem_g, w_q=v_w_q, w_kv=v_w_kv,
                     w_o=v_w_o, norm_ffn_g=v_norm_ffn_g, w_up=v_w_up, ffn_dw_w=v_ffn_dw_w, ffn_dw_b=v_ffn_dw_b,
                     w_down=v_w_down, norm_final_g=v_norm_final_g)
    order = list(weights)
    transposed = ("w_in", "w_kv", "w_up")

    n_b, seq, _ = x.shape
    tokens = n_b * seq
    tm_mix = min(512, seq // 2)
    tm_ffn = min(256, seq // 2)
    dev = 4 * lax.axis_index("x") + 2 * lax.axis_index("y") + lax.axis_index("c")

    shards = [weights[n][0].T if n in transposed else weights[n][0] for n, _ in W_ROWS]
    packed = jnp.concatenate(shards, axis=0).astype(BF16)
    small_sharded = _pack_rows([conv_dw_w[0], ffn_dw_w[0]])
    gw, gsmall = _all_gather([packed, small_sharded], "weights_all_gather")
    gflat = gsmall.reshape(N_DEV, -1)
    n_cw = CONV_WIDTH * (D_CONV // N_DEV)
    n_fw = FFN_CONV_WIDTH * (2 * D_FF // N_DEV)
    conv_w = gflat[:, :n_cw].reshape(N_DEV, CONV_WIDTH, D_CONV // N_DEV).transpose(1, 0, 2).reshape(CONV_WIDTH, D_CONV)
    ffn_w = gflat[:, n_cw:n_cw + n_fw].reshape(N_DEV, FFN_CONV_WIDTH, 2 * D_FF // N_DEV).transpose(1, 0, 2).reshape(
        FFN_CONV_WIDTH, 2 * D_FF)

    x2d = x.reshape(tokens, D_MODEL)
    mem2d = mem.reshape(n_b * N_MEM, D_MODEL)
    tgt2d = loss_target.reshape(tokens, D_MODEL)
    g_final = norm_final_g.reshape(1, D_MODEL)

    x1, u_all, c_all, pooled_all, ymix, h1 = _fwd_mix(
        x2d, gw, norm_mix_g, conv_w, conv_dw_b, conv_ln_g, conv_ln_b, pool_w[0], pool_scale, seq, tm_mix)
    mem_n, kv = _fwd_kv(mem2d, gw, norm_mem_g)
    x2, h2, q, o = _fwd_attn(x1, kv, gw, norm_xattn_g, seq, tm_mix)
    uu_all, a_all, h3, dx3, dx3b, loss_part, dg_final = _fwd_ffn(
        x2, tgt2d, gw, norm_ffn_g, ffn_w, ffn_dw_b, g_final, seq, tm_ffn)

    dx2, dx2b, duu, d_ffn_b, d_ffn_w, dg_ffn = _bwd_ffn(dx3, x2, uu_all, gw, norm_ffn_g, ffn_w, ffn_dw_b, seq, tm_ffn)
    dx1, dx1b, dq, dkv, dg_x = _bwd_attn(dx2, x1, q, kv, gw, norm_xattn_g, seq, tm_mix)
    dkv_b, dg_mem = _bwd_kv(dkv, mem2d, gw, norm_mem_g)
    dx, du, dg_mix, d_conv_w, d_conv_b, d_ln_g, d_ln_b, d_pool_w, d_pool_scale = _bwd_mix(
        dx1, x2d, u_all, c_all, pooled_all, gw, norm_mix_g, conv_w, conv_ln_g, conv_ln_b, pool_w[0], pool_scale,
        seq, tm_mix)
    grad_x = dx.reshape(x.shape)

    big = dict(
        w_in=_wgrad(du, h1, "wgrad_w_in"), w_out=_wgrad(ymix, dx1b, "wgrad_w_out"), w_q=_wgrad(h2, dq, "wgrad_w_q"),
        w_kv=_wgrad(dkv_b, mem_n, "wgrad_w_kv"), w_o=_wgrad(o, dx2b, "wgrad_w_o"), w_up=_wgrad(duu, h3, "wgrad_w_up"),
        w_down=_wgrad(a_all, dx3b, "wgrad_w_down"))

    g_packed = jnp.concatenate([big[n].reshape(N_DEV, r, D_MODEL) for n, r in W_ROWS], axis=1)
    table = _owner_table()
    from_sibling = _exchange_sibling(g_packed)
    chip_sums = _chip_partial_sums(table, g_packed, from_sibling, 944)
    from_chips = _exchange_chips(chip_sums)
    g_mine = _final_grad_sums(table, g_packed, from_sibling, from_chips, 944)

    small_grads = dict(norm_mix_g=dg_mix, conv_dw_b=d_conv_b, conv_ln_g=d_ln_g, conv_ln_b=d_ln_b, pool_w=d_pool_w,
                       pool_scale=d_pool_scale, norm_xattn_g=dg_x, norm_mem_g=dg_mem, norm_ffn_g=dg_ffn,
                       ffn_dw_b=d_ffn_b, norm_final_g=dg_final)
    small_list = [small_grads[n] for n, _ in SMALL] + [d_conv_w, d_ffn_w, loss_part[:1]]
    (small_all,) = _all_gather([_pack_rows(small_list)], "small_grads_all_gather")
    small_sum = _sum_blocks(small_all).reshape(-1)

    grads = {}
    pos = 0
    for n, shape in SMALL:
        size = 1
        for s in shape:
            size *= s
        grads[n] = small_sum[pos:pos + size].reshape(shape)
        pos += size
    full_conv_w = small_sum[pos:pos + CONV_WIDTH * D_CONV].reshape(CONV_WIDTH, D_CONV)
    pos += CONV_WIDTH * D_CONV
    full_ffn_w = small_sum[pos:pos + FFN_CONV_WIDTH * 2 * D_FF].reshape(FFN_CONV_WIDTH, 2 * D_FF)
    loss = small_sum[pos + FFN_CONV_WIDTH * 2 * D_FF]
    grads["conv_dw_w"] = lax.dynamic_slice_in_dim(full_conv_w, dev * (D_CONV // N_DEV), D_CONV // N_DEV, axis=1)[None]
    grads["ffn_dw_w"] = lax.dynamic_slice_in_dim(full_ffn_w, dev * (2 * D_FF // N_DEV), 2 * D_FF // N_DEV, axis=1)[None]
    for n, (off, r) in W_OFF.items():
        blk = g_mine[off:off + r]
        grads[n] = (blk.T if n in transposed else blk)[None]

    delta, new_m, new_v = {}, {}, {}
    for n, _ in W_ROWS:
        shape = weights[n].shape
        as2d = lambda t: t.reshape(shape[1], shape[2])
        d, nm, nv = _adamw(as2d(weights[n]), as2d(grads[n]), as2d(moments_m[n]), as2d(moments_v[n]), "adamw_" + n)
        delta[n], new_m[n], new_v[n] = d.reshape(shape), nm.reshape(shape), nv.reshape(shape)
    small_names = [n for n in order if n not in W_OFF]
    two_d = lambda t: t.reshape(1, -1) if t.ndim == 1 else t
    outs = _adamw_small(*[[two_d(t[n]) for n in small_names] for t in (weights, grads, moments_m, moments_v)])
    for res, out in zip((delta, new_m, new_v), outs):
        for n, o in zip(small_names, out):
            res[n] = o.reshape(weights[n].shape)

    return (loss, grad_x, *[grads[n] for n in order], *[delta[n] for n in order],
            *[new_m[n] for n in order], *[new_v[n] for n in order])
```

```python
import functools

import jax
import jax.numpy as jnp
from jax import lax
from jax.experimental import pallas as pl
from jax.experimental.pallas import tpu as pltpu

F32 = jnp.float32
BF16 = jnp.bfloat16
MESH = pl.DeviceIdType.MESH

N_DEV = 8
D_MODEL = 1024
D_CONV = 512
D_POOL = 512
CONV_WIDTH = 31
POOL_WINDOWS = (2, 4, 8, 16)
POOL_GROUP_DIM = 128
D_IN = 1536
N_MEM = 256
HEADS = 4
HEAD_DIM = 256
D_FF = 2816
FFN_CONV_WIDTH = 3
EPS = 1e-6
ADAM_LR = 0.001
ADAM_B1 = 0.9
ADAM_B2 = 0.999
ADAM_EPS = 1e-08
ADAM_WD = 0.01
ADAM_STEP = 10

VMEM_LIMIT_V7X = 56 * 1024 * 1024
CONV_HALO = 32
POOL_HALO = 16
FFN_HALO = 8
FFN_CHUNK = 1408

W_ROWS = (("w_in", 192), ("w_out", 128), ("w_q", 128), ("w_kv", 256), ("w_o", 128), ("w_up", 704), ("w_down", 352))
W_OFF = {}
_o = 0
for _n, _r in W_ROWS:
    W_OFF[_n] = (_o, _r)
    _o += _r
PACK_ROWS = _o
RS_GROUPS = {"a": ("w_up", "w_down"), "b": ("w_q", "w_kv", "w_o"), "c": ("w_in", "w_out")}
RS_TILE = {"a": 528, "b": 256, "c": 160}


def _dot(a, b):
    return jnp.dot(a, b, preferred_element_type=F32)


def _dot_nt(a, b):
    return lax.dot_general(a, b, (((1,), (1,)), ((), ())), preferred_element_type=F32)


def _dot_tn(a, b):
    return lax.dot_general(a, b, (((0,), (0,)), ((), ())), preferred_element_type=F32)


def _sigmoid(v):
    return 1.0 / (1.0 + jnp.exp(-v))


def _rms_fwd(v):
    r = lax.rsqrt(jnp.mean(v * v, axis=-1, keepdims=True) + EPS)
    return v * r, r


def _rms_bwd(dh, vh, r, g):
    gd = dh * g
    return r * (gd - vh * jnp.mean(gd * vh, axis=-1, keepdims=True))


def _colsum(v):
    return jnp.sum(v, axis=0, keepdims=True)


def _full(shape):
    return pl.BlockSpec(shape, lambda *_: (0,) * len(shape))


def _params(sem=("arbitrary",), vmem=VMEM_LIMIT_V7X):
    return pltpu.CompilerParams(dimension_semantics=sem, vmem_limit_bytes=vmem)


def _load_weight(g_hbm, name, dst, sem):
    off, rows = W_OFF[name]
    return [pltpu.make_async_copy(g_hbm.at[d, pl.ds(off, rows), :], dst.at[pl.ds(d * rows, rows), :], sem)
            for d in range(N_DEV)]


def _position():
    x, y, c = lax.axis_index("x"), lax.axis_index("y"), lax.axis_index("c")
    chips = [(1 - x, y), (x, 1 - y), (1 - x, 1 - y)]
    return x, y, c, chips


def _dev(px, py, pc):
    return 4 * px + 2 * py + pc


def _all_gather(arrs, name):
    n = len(arrs)

    def body(*refs):
        ins, outs = refs[:n], refs[n:2 * n]
        send_sems, recv_sems, local_sems = refs[2 * n:]
        x, y, c, chips = _position()
        me, sibling = (x, y, c), (x, y, 1 - c)

        def copy(a, k, block, to, src=None):
            rows = outs[a].at[_dev(*block)]
            return pltpu.make_async_remote_copy(
                src_ref=rows if src is None else src, dst_ref=rows,
                send_sem=send_sems.at[a, k], recv_sem=recv_sems.at[a, k], device_id=to, device_id_type=MESH)

        started = []
        for a in range(n):
            mine = pltpu.make_async_copy(ins[a], outs[a].at[_dev(*me)], local_sems.at[a])
            mine.start()
            started.append(mine)
        sends = []
        for a in range(n):
            first = [copy(a, 0, me, sibling, src=ins[a])]
            first += [copy(a, 1 + j, me, (*chip, c), src=ins[a]) for j, chip in enumerate(chips)]
            for cp in first:
                cp.start()
            sends += first
        for j, chip in enumerate(chips):
            for a in range(n):
                copy(a, 1 + j, (*chip, c), me).wait_recv()
                passed = copy(a, 4 + j, (*chip, c), sibling)
                passed.start()
                sends.append(passed)
        for a in range(n):
            copy(a, 0, sibling, me).wait_recv()
            for j, chip in enumerate(chips):
                copy(a, 4 + j, (*chip, 1 - c), me).wait_recv()
        for cp in sends:
            cp.wait_send()
        for mine in started:
            mine.wait()

    any_spec = pl.BlockSpec(memory_space=pl.ANY)
    return pl.pallas_call(
        body, name=name,
        out_shape=[jax.ShapeDtypeStruct((N_DEV,) + a.shape, a.dtype) for a in arrs],
        in_specs=[any_spec] * n, out_specs=[any_spec] * n,
        scratch_shapes=[pltpu.SemaphoreType.DMA((n, 7)), pltpu.SemaphoreType.DMA((n, 7)), pltpu.SemaphoreType.DMA((n,))],
    )(*arrs)


def _exchange_sibling(parts, name):
    n = len(parts)
    rows = [p.shape[1] for p in parts]
    cols = parts[0].shape[2]
    offs = [sum(rows[:k]) for k in range(n)]

    def body(*refs):
        ins = refs[:n]
        own_ref, landed_ref, send_sems, recv_sems, local_sems = refs[n:]
        x, y, c, chips = _position()
        sibling = (x, y, 1 - c)
        copies, locals_ = [], []
        for k in range(n):
            for j, chip in enumerate([(x, y)] + chips):
                dst = pl.ds(offs[k], rows[k])
                cp = pltpu.make_async_remote_copy(
                    src_ref=ins[k].at[_dev(*chip, 1 - c)], dst_ref=landed_ref.at[j, dst, :],
                    send_sem=send_sems.at[k, j], recv_sem=recv_sems.at[k, j], device_id=sibling, device_id_type=MESH)
                cp.start()
                copies.append(cp)
                lc = pltpu.make_async_copy(ins[k].at[_dev(*chip, c)], own_ref.at[j, dst, :], local_sems.at[k, j])
                lc.start()
                locals_.append(lc)
        for cp in copies:
            cp.wait_recv()
        for cp in copies:
            cp.wait_send()
        for lc in locals_:
            lc.wait()

    any_spec = pl.BlockSpec(memory_space=pl.ANY)
    return pl.pallas_call(
        body, name=name,
        out_shape=[jax.ShapeDtypeStruct((4, sum(rows), cols), parts[0].dtype)] * 2,
        in_specs=[any_spec] * n, out_specs=[any_spec] * 2,
        scratch_shapes=[pltpu.SemaphoreType.DMA((n, 4)), pltpu.SemaphoreType.DMA((n, 4)), pltpu.SemaphoreType.DMA((n, 4))],
    )(*parts)


def _chip_exchange_start(s, name):
    def body(s_ref, land_ref, send_sems, recv_sems, s_thru, land_thru, token):
        del s_thru, land_thru
        x, y, c, chips = _position()
        for j, chip in enumerate(chips):
            pltpu.make_async_remote_copy(
                src_ref=s_ref.at[j], dst_ref=land_ref.at[j], send_sem=send_sems.at[j], recv_sem=recv_sems.at[j],
                device_id=(*chip, c), device_id_type=MESH).start()
        token[...] = jnp.zeros_like(token)

    hbm = pl.BlockSpec(memory_space=pltpu.HBM)
    sem = pl.BlockSpec(memory_space=pltpu.SEMAPHORE)
    return pl.pallas_call(
        body, name=name,
        out_shape=(pltpu.SemaphoreType.DMA((3,)), pltpu.SemaphoreType.DMA((3,)), pltpu.HBM(s.shape, s.dtype),
                   pltpu.HBM(s.shape, s.dtype), jax.ShapeDtypeStruct((8, 128), F32)),
        in_specs=(hbm, hbm), out_specs=(sem, sem, hbm, hbm, pl.BlockSpec(memory_space=pltpu.VMEM)),
        input_output_aliases={0: 2, 1: 3},
        compiler_params=pltpu.CompilerParams(has_side_effects=pltpu.SideEffectType.DATAFLOW_SIDE_EFFECTING),
    )(pltpu.with_memory_space_constraint(s, pltpu.HBM),
      pltpu.with_memory_space_constraint(lax.empty(s.shape, s.dtype), pltpu.HBM))


def _chip_exchange_wait(send_sems, recv_sems, s_thru, land_thru, after, name):
    def body(s_ref, land_ref, send_sems, recv_sems, after_ref, s_dead, got_ref):
        del after_ref, s_dead, got_ref
        x, y, c, chips = _position()
        for j, chip in enumerate(chips):
            cp = pltpu.make_async_remote_copy(
                src_ref=s_ref.at[j], dst_ref=land_ref.at[j], send_sem=send_sems.at[j], recv_sem=recv_sems.at[j],
                device_id=(*chip, c), device_id_type=MESH)
            cp.wait_send()
            cp.wait_recv()

    hbm = pl.BlockSpec(memory_space=pltpu.HBM)
    sem = pl.BlockSpec(memory_space=pltpu.SEMAPHORE)
    return pl.pallas_call(
        body, name=name,
        out_shape=(pltpu.HBM(s_thru.shape, s_thru.dtype), pltpu.HBM(s_thru.shape, s_thru.dtype)),
        in_specs=(hbm, hbm, sem, sem, pl.BlockSpec(memory_space=pl.ANY)), out_specs=(hbm, hbm),
        input_output_aliases={0: 0, 1: 1},
        compiler_params=pltpu.CompilerParams(has_side_effects=pltpu.SideEffectType.DATAFLOW_SIDE_EFFECTING),
    )(s_thru, land_thru, send_sems, recv_sems, after)[1]


def _chip_partial_sums(own, landed, tile, name):
    _, rows, cols = own.shape

    def body(g_ref, l_ref, out_ref):
        out_ref[...] = (g_ref[...].astype(F32) + l_ref[...].astype(F32)).astype(out_ref.dtype)

    spec = pl.BlockSpec((None, tile, cols), lambda j, t: (j + 1, t, 0))
    return pl.pallas_call(
        body, name=name, grid=(3, rows // tile), in_specs=[spec, spec],
        out_specs=pl.BlockSpec((None, tile, cols), lambda j, t: (j, t, 0)),
        out_shape=jax.ShapeDtypeStruct((3, rows, cols), BF16),
        compiler_params=_params(("arbitrary", "arbitrary")),
    )(own, landed)


def _final_grad_sums(own, landed, from_chips, tile, name):
    _, rows, cols = own.shape

    def body(g_ref, l_ref, c_ref, out_ref):
        acc = g_ref[...].astype(F32) + l_ref[...].astype(F32)
        for j in range(3):
            acc = acc + c_ref[j].astype(F32)
        out_ref[...] = acc

    spec = pl.BlockSpec((None, tile, cols), lambda t: (0, t, 0))
    return pl.pallas_call(
        body, name=name, grid=(rows // tile,),
        in_specs=[spec, spec, pl.BlockSpec((3, tile, cols), lambda t: (0, t, 0))],
        out_specs=pl.BlockSpec((tile, cols), lambda t: (t, 0)),
        out_shape=jax.ShapeDtypeStruct((rows, cols), F32),
        compiler_params=_params(("arbitrary",)),
    )(own, landed, from_chips)


def _sum_blocks(g8):
    _, rows, cols = g8.shape

    def body(g_ref, out_ref):
        acc = g_ref[0]
        for d in range(1, N_DEV):
            acc = acc + g_ref[d]
        out_ref[...] = acc

    return pl.pallas_call(
        body, name="small_grad_sum", grid=(1,),
        in_specs=[_full((N_DEV, rows, cols))], out_specs=_full((rows, cols)),
        out_shape=jax.ShapeDtypeStruct((rows, cols), F32),
        compiler_params=_params(("arbitrary",)),
    )(g8)


def _fwd_mix(x2d, gw, g_mix, conv_w, conv_b, ln_g, ln_b, pool_w, pool_scale, seq, tm):
    tokens = x2d.shape[0]
    n_tiles = tokens // tm
    tps = seq // tm

    def body(x_ref, gmix_ref, gw_hbm, cw_ref, cb_ref, lng_ref, lnb_ref, pw_ref, ps_ref,
             x1_ref, u_ref, c_ref, pooled_ref, ymix_ref, h1_ref,
             win_v, wout_v, hc_carry, up_carry, sem):
        i = pl.program_id(0)

        @pl.when(i == 0)
        def _():
            copies = _load_weight(gw_hbm, "w_in", win_v, sem) + _load_weight(gw_hbm, "w_out", wout_v, sem)
            for cp in copies:
                cp.start()
            for cp in copies:
                cp.wait()

        @pl.when(i % tps == 0)
        def _():
            hc_carry[...] = jnp.zeros_like(hc_carry)
            up_carry[...] = jnp.zeros_like(up_carry)

        x = x_ref[...]
        xh, _ = _rms_fwd(x)
        h1 = (xh * gmix_ref[...]).astype(BF16)
        h1_ref[...] = h1
        u = _dot_nt(h1, win_v[...])
        u_ref[...] = u
        val, gate, up = u[:, :D_CONV], u[:, D_CONV:2 * D_CONV], u[:, 2 * D_CONV:]

        hc = val * _sigmoid(gate)
        ext = jnp.concatenate([hc_carry[...], hc], axis=0)
        hc_carry[...] = hc[tm - CONV_HALO:, :]
        conv = jnp.broadcast_to(cb_ref[...], (tm, D_CONV))
        for k in range(CONV_WIDTH):
            shift = CONV_WIDTH - 1 - k
            tap = ext if shift == 0 else pltpu.roll(ext, shift, 0)
            conv = conv + cw_ref[k:k + 1, :] * tap[CONV_HALO:, :]
        c_ref[...] = conv
        mu = jnp.mean(conv, axis=-1, keepdims=True)
        cen = conv - mu
        ln = cen * lax.rsqrt(jnp.mean(cen * cen, axis=-1, keepdims=True) + EPS) * lng_ref[...] + lnb_ref[...]
        y_conv = ln * _sigmoid(ln)

        extp = jnp.concatenate([up_carry[...], up], axis=0)
        up_carry[...] = up[tm - POOL_HALO:, :]
        pos = lax.broadcasted_iota(jnp.int32, (tm, 1), 0) + (i % tps) * tm
        run = extp
        mixed = []
        for g, w in enumerate(POOL_WINDOWS):
            lo = g * POOL_GROUP_DIM
            run = run[:, POOL_GROUP_DIM if g else 0:]
            run = run + pltpu.roll(run, w // 2, 0)
            cnt = jnp.minimum(pos + 1, w).astype(F32)
            pooled = run[POOL_HALO:, :POOL_GROUP_DIM] / cnt - up[:, lo:lo + POOL_GROUP_DIM]
            pooled = pooled.astype(BF16)
            pooled_ref[:, lo:lo + POOL_GROUP_DIM] = pooled
            mixed.append(_dot(pooled, pw_ref[g].astype(BF16)))
        y_pool = jnp.concatenate(mixed, axis=-1) * ps_ref[...]

        ymix = jnp.concatenate([y_conv, y_pool], axis=-1).astype(BF16)
        ymix_ref[...] = ymix
        x1_ref[...] = x + _dot(ymix, wout_v[...])

    row = lambda w: pl.BlockSpec((tm, w), lambda i: (i, 0))
    return pl.pallas_call(
        body, name="fwd_mix", grid=(n_tiles,),
        in_specs=[row(D_MODEL), _full((1, D_MODEL)), pl.BlockSpec(memory_space=pl.ANY),
                  _full((CONV_WIDTH, D_CONV)), _full((1, D_CONV)), _full((1, D_CONV)), _full((1, D_CONV)),
                  _full((4, POOL_GROUP_DIM, POOL_GROUP_DIM)), _full((1, D_POOL))],
        out_specs=[row(D_MODEL), row(D_IN), row(D_CONV), row(D_POOL), row(D_MODEL), row(D_MODEL)],
        out_shape=[jax.ShapeDtypeStruct((tokens, D_MODEL), F32), jax.ShapeDtypeStruct((tokens, D_IN), F32),
                   jax.ShapeDtypeStruct((tokens, D_CONV), F32), jax.ShapeDtypeStruct((tokens, D_POOL), BF16),
                   jax.ShapeDtypeStruct((tokens, D_MODEL), BF16), jax.ShapeDtypeStruct((tokens, D_MODEL), BF16)],
        scratch_shapes=[pltpu.VMEM((D_IN, D_MODEL), BF16), pltpu.VMEM((D_MODEL, D_MODEL), BF16),
                        pltpu.VMEM((CONV_HALO, D_CONV), F32), pltpu.VMEM((POOL_HALO, D_POOL), F32),
                        pltpu.SemaphoreType.DMA],
        compiler_params=_params(),
    )(x2d, g_mix, gw, conv_w, conv_b, ln_g, ln_b, pool_w, pool_scale)


def _fwd_kv(mem2d, gw, g_mem):
    rows = mem2d.shape[0]
    n_b = rows // N_MEM

    def body(mem_ref, g_ref, gw_hbm, mn_ref, kv_ref, wkv_v, sem):
        @pl.when(pl.program_id(0) == 0)
        def _():
            copies = _load_weight(gw_hbm, "w_kv", wkv_v, sem)
            for cp in copies:
                cp.start()
            for cp in copies:
                cp.wait()

        mh, _ = _rms_fwd(mem_ref[...])
        mn = (mh * g_ref[...]).astype(BF16)
        mn_ref[...] = mn
        kv_ref[...] = _dot_nt(mn, wkv_v[...]).astype(BF16)

    return pl.pallas_call(
        body, name="fwd_kv", grid=(n_b,),
        in_specs=[pl.BlockSpec((N_MEM, D_MODEL), lambda b: (b, 0)), _full((1, D_MODEL)), pl.BlockSpec(memory_space=pl.ANY)],
        out_specs=[pl.BlockSpec((N_MEM, D_MODEL), lambda b: (b, 0)), pl.BlockSpec((N_MEM, 2 * D_MODEL), lambda b: (b, 0))],
        out_shape=[jax.ShapeDtypeStruct((rows, D_MODEL), BF16), jax.ShapeDtypeStruct((rows, 2 * D_MODEL), BF16)],
        scratch_shapes=[pltpu.VMEM((2 * D_MODEL, D_MODEL), BF16), pltpu.SemaphoreType.DMA],
        compiler_params=_params(),
    )(mem2d, g_mem, gw)


def _softmax_rows(s):
    e = jnp.exp(s - jnp.max(s, axis=-1, keepdims=True))
    return e / jnp.sum(e, axis=-1, keepdims=True)


def _fwd_attn(x1, kv, gw, g_x, seq, tm):
    tokens = x1.shape[0]
    n_tiles = tokens // tm
    tps = seq // tm

    def body(x1_ref, kv_ref, g_ref, gw_hbm, x2_ref, h2_ref, q_ref, o_ref, wq_v, wo_v, sem):
        @pl.when(pl.program_id(0) == 0)
        def _():
            copies = _load_weight(gw_hbm, "w_q", wq_v, sem) + _load_weight(gw_hbm, "w_o", wo_v, sem)
            for cp in copies:
                cp.start()
            for cp in copies:
                cp.wait()

        x1v = x1_ref[...]
        xh, _ = _rms_fwd(x1v)
        h2 = (xh * g_ref[...]).astype(BF16)
        h2_ref[...] = h2
        q = (_dot(h2, wq_v[...]) * (HEAD_DIM ** -0.5)).astype(BF16)
        q_ref[...] = q
        outs = []
        for h in range(HEADS):
            lo = h * HEAD_DIM
            p = _softmax_rows(_dot_nt(q[:, lo:lo + HEAD_DIM], kv_ref[:, lo:lo + HEAD_DIM]))
            outs.append(_dot(p.astype(BF16), kv_ref[:, D_MODEL + lo:D_MODEL + lo + HEAD_DIM]))
        o = jnp.concatenate(outs, axis=-1).astype(BF16)
        o_ref[...] = o
        x2_ref[...] = x1v + _dot(o, wo_v[...])

    row = lambda w: pl.BlockSpec((tm, w), lambda i: (i, 0))
    return pl.pallas_call(
        body, name="fwd_attn", grid=(n_tiles,),
        in_specs=[row(D_MODEL), pl.BlockSpec((N_MEM, 2 * D_MODEL), lambda i: (i // tps, 0)), _full((1, D_MODEL)),
                  pl.BlockSpec(memory_space=pl.ANY)],
        out_specs=[row(D_MODEL)] * 4,
        out_shape=[jax.ShapeDtypeStruct((tokens, D_MODEL), F32)] + [jax.ShapeDtypeStruct((tokens, D_MODEL), BF16)] * 3,
        scratch_shapes=[pltpu.VMEM((D_MODEL, D_MODEL), BF16), pltpu.VMEM((D_MODEL, D_MODEL), BF16), pltpu.SemaphoreType.DMA],
        compiler_params=_params(),
    )(x1, kv, g_x, gw)


def _ffn_conv(uu, halo, w_ref, b_ref, cols):
    ext = jnp.concatenate([halo, uu], axis=0)
    p1 = pltpu.roll(ext, 1, 0)[FFN_HALO:, :]
    p2 = pltpu.roll(ext, 2, 0)[FFN_HALO:, :]
    return b_ref[:, cols] + w_ref[2:3, cols] * uu + w_ref[1:2, cols] * p1 + w_ref[0:1, cols] * p2


def _fwd_ffn(x2, target, gw, g_ffn, ffn_w, ffn_b, g_final, seq, tm):
    tokens = x2.shape[0]
    n_tiles = tokens // tm
    tps = seq // tm
    n_chunks = D_FF // FFN_CHUNK

    def body(x2_ref, tgt_ref, gffn_ref, gw_hbm, fw_ref, fb_ref, gfin_ref,
             uu_ref, a_ref, h3_ref, dx3_ref, dx3b_ref, loss_ref, dgfin_ref,
             wup_v, wdown_v, carry, sem):
        i = pl.program_id(0)

        @pl.when(i == 0)
        def _():
            copies = _load_weight(gw_hbm, "w_up", wup_v, sem) + _load_weight(gw_hbm, "w_down", wdown_v, sem)
            for cp in copies:
                cp.start()
            for cp in copies:
                cp.wait()
            loss_ref[...] = jnp.zeros_like(loss_ref)
            dgfin_ref[...] = jnp.zeros_like(dgfin_ref)

        @pl.when(i % tps == 0)
        def _():
            carry[...] = jnp.zeros_like(carry)

        x2v = x2_ref[...]
        xh, _ = _rms_fwd(x2v)
        h3 = (xh * gffn_ref[...]).astype(BF16)
        h3_ref[...] = h3
        acc = jnp.zeros((tm, D_MODEL), F32)
        for jc in range(n_chunks):
            halves = []
            for half in range(2):
                cols = pl.ds(half * D_FF + jc * FFN_CHUNK, FFN_CHUNK)
                uu = _dot_nt(h3, wup_v[cols, :])
                uu_ref[:, cols] = uu
                halves.append(_ffn_conv(uu, carry[:, cols], fw_ref, fb_ref, cols))
                carry[:, cols] = uu[tm - FFN_HALO:, :]
            gate, val = halves
            a = (gate * _sigmoid(gate) * val).astype(BF16)
            a_ref[:, pl.ds(jc * FFN_CHUNK, FFN_CHUNK)] = a
            acc = acc + _dot(a, wdown_v[pl.ds(jc * FFN_CHUNK, FFN_CHUNK), :])
        x3 = x2v + acc

        xh3, r3 = _rms_fwd(x3)
        gfin = gfin_ref[...]
        err = xh3 * gfin - tgt_ref[...]
        loss_ref[...] += jnp.full(loss_ref.shape, jnp.sum(err * err) * (0.5 / D_MODEL), F32)
        dy = err * (1.0 / D_MODEL)
        dgfin_ref[...] += _colsum(dy * xh3)
        dx3 = _rms_bwd(dy, xh3, r3, gfin)
        dx3_ref[...] = dx3
        dx3b_ref[...] = dx3.astype(BF16)

    row = lambda w: pl.BlockSpec((tm, w), lambda i: (i, 0))
    return pl.pallas_call(
        body, name="fwd_ffn", grid=(n_tiles,),
        in_specs=[row(D_MODEL), row(D_MODEL), _full((1, D_MODEL)), pl.BlockSpec(memory_space=pl.ANY),
                  _full((FFN_CONV_WIDTH, 2 * D_FF)), _full((1, 2 * D_FF)), _full((1, D_MODEL))],
        out_specs=[row(2 * D_FF), row(D_FF), row(D_MODEL), row(D_MODEL), row(D_MODEL), _full((8, 128)), _full((1, D_MODEL))],
        out_shape=[jax.ShapeDtypeStruct((tokens, 2 * D_FF), F32), jax.ShapeDtypeStruct((tokens, D_FF), BF16),
                   jax.ShapeDtypeStruct((tokens, D_MODEL), BF16), jax.ShapeDtypeStruct((tokens, D_MODEL), F32),
                   jax.ShapeDtypeStruct((tokens, D_MODEL), BF16),
                   jax.ShapeDtypeStruct((8, 128), F32), jax.ShapeDtypeStruct((1, D_MODEL), F32)],
        scratch_shapes=[pltpu.VMEM((2 * D_FF, D_MODEL), BF16), pltpu.VMEM((D_FF, D_MODEL), BF16),
                        pltpu.VMEM((FFN_HALO, 2 * D_FF), F32), pltpu.SemaphoreType.DMA],
        compiler_params=_params(),
    )(x2, target, g_ffn, gw, ffn_w, ffn_b, g_final)


def _bwd_ffn(dx3, x2, uu_all, gw, g_ffn, ffn_w, ffn_b, seq, tm):
    tokens = x2.shape[0]
    n_tiles = tokens // tm
    tps = seq // tm
    n_chunks = D_FF // FFN_CHUNK
    per8 = tm // FFN_HALO

    def body(dx3_ref, x2_ref, uu_ref, prev_ref, gffn_ref, gw_hbm, fw_ref, fb_ref,
             dx2_ref, dx2b_ref, duu_ref, dfb_ref, dfw_ref, dg_ref,
             wup_v, wdown_v, carry, sem):
        i = pl.program_id(0)
        t = n_tiles - 1 - i

        @pl.when(i == 0)
        def _():
            copies = _load_weight(gw_hbm, "w_up", wup_v, sem) + _load_weight(gw_hbm, "w_down", wdown_v, sem)
            for cp in copies:
                cp.start()
            for cp in copies:
                cp.wait()
            dfb_ref[...] = jnp.zeros_like(dfb_ref)
            dfw_ref[...] = jnp.zeros_like(dfw_ref)
            dg_ref[...] = jnp.zeros_like(dg_ref)

        @pl.when(t % tps == tps - 1)
        def _():
            carry[...] = jnp.zeros_like(carry)

        starts_sequence = (t % tps == 0)
        dx3v = dx3_ref[...]
        dx3b = dx3v.astype(BF16)
        dh3 = jnp.zeros((tm, D_MODEL), F32)
        for jc in range(n_chunks):
            da = _dot_nt(dx3b, wdown_v[pl.ds(jc * FFN_CHUNK, FFN_CHUNK), :])
            uus, ccs, colss = [], [], []
            for half in range(2):
                cols = pl.ds(half * D_FF + jc * FFN_CHUNK, FFN_CHUNK)
                uu = uu_ref[:, cols]
                halo = jnp.where(starts_sequence, 0.0, prev_ref[:, cols])
                uus.append(uu)
                colss.append(cols)
                ccs.append(_ffn_conv(uu, halo, fw_ref, fb_ref, cols))
            gate, val = ccs
            sg = _sigmoid(gate)
            dgate = da * val * (sg * (1.0 + gate * (1.0 - sg)))
            dval = da * (gate * sg)
            for dcc, uu, cols in zip((dgate, dval), uus, colss):
                dfb_ref[:, cols] += _colsum(dcc)
                ext = jnp.concatenate([dcc, carry[:, cols]], axis=0)
                carry[:, cols] = dcc[:FFN_HALO, :]
                n1 = pltpu.roll(ext, tm + FFN_HALO - 1, 0)[:tm, :]
                n2 = pltpu.roll(ext, tm + FFN_HALO - 2, 0)[:tm, :]
                duu = fw_ref[2:3, cols] * dcc + fw_ref[1:2, cols] * n1 + fw_ref[0:1, cols] * n2
                dfw_ref[2:3, cols] += _colsum(uu * dcc)
                dfw_ref[1:2, cols] += _colsum(uu * n1)
                dfw_ref[0:1, cols] += _colsum(uu * n2)
                duub = duu.astype(BF16)
                duu_ref[:, cols] = duub
                dh3 = dh3 + _dot(duub, wup_v[cols, :])
        xh, r = _rms_fwd(x2_ref[...])
        dg_ref[...] += _colsum(dh3 * xh)
        dx2 = dx3v + _rms_bwd(dh3, xh, r, gffn_ref[...])
        dx2_ref[...] = dx2
        dx2b_ref[...] = dx2.astype(BF16)

    rev = lambda w: pl.BlockSpec((tm, w), lambda i: (n_tiles - 1 - i, 0))
    prev = pl.BlockSpec((FFN_HALO, 2 * D_FF), lambda i: (jnp.maximum((n_tiles - 1 - i) * per8 - 1, 0), 0))
    return pl.pallas_call(
        body, name="bwd_ffn", grid=(n_tiles,),
        in_specs=[rev(D_MODEL), rev(D_MODEL), rev(2 * D_FF), prev, _full((1, D_MODEL)), pl.BlockSpec(memory_space=pl.ANY),
                  _full((FFN_CONV_WIDTH, 2 * D_FF)), _full((1, 2 * D_FF))],
        out_specs=[rev(D_MODEL), rev(D_MODEL), rev(2 * D_FF), _full((1, 2 * D_FF)), _full((FFN_CONV_WIDTH, 2 * D_FF)),
                   _full((1, D_MODEL))],
        out_shape=[jax.ShapeDtypeStruct((tokens, D_MODEL), F32), jax.ShapeDtypeStruct((tokens, D_MODEL), BF16),
                   jax.ShapeDtypeStruct((tokens, 2 * D_FF), BF16),
                   jax.ShapeDtypeStruct((1, 2 * D_FF), F32), jax.ShapeDtypeStruct((FFN_CONV_WIDTH, 2 * D_FF), F32),
                   jax.ShapeDtypeStruct((1, D_MODEL), F32)],
        scratch_shapes=[pltpu.VMEM((2 * D_FF, D_MODEL), BF16), pltpu.VMEM((D_FF, D_MODEL), BF16),
                        pltpu.VMEM((FFN_HALO, 2 * D_FF), F32), pltpu.SemaphoreType.DMA],
        compiler_params=_params(),
    )(dx3, x2, uu_all, uu_all, g_ffn, gw, ffn_w, ffn_b)


def _bwd_attn(dx2, x1, q, kv, gw, g_x, after, seq, tm):
    tokens = x1.shape[0]
    n_tiles = tokens // tm
    tps = seq // tm
    n_b = tokens // seq

    def body(dx2_ref, x1_ref, q_ref, kv_ref, g_ref, gw_hbm, after_ref, dx1_ref, dx1b_ref, dq_ref, dkv_ref, dg_ref,
             wq_v, wo_v, sem):
        del after_ref
        i = pl.program_id(0)

        @pl.when(i == 0)
        def _():
            copies = _load_weight(gw_hbm, "w_q", wq_v, sem) + _load_weight(gw_hbm, "w_o", wo_v, sem)
            for cp in copies:
                cp.start()
            for cp in copies:
                cp.wait()
            dg_ref[...] = jnp.zeros_like(dg_ref)

        @pl.when(i % tps == 0)
        def _():
            dkv_ref[...] = jnp.zeros_like(dkv_ref)

        dx2v = dx2_ref[...]
        do = _dot_nt(dx2v.astype(BF16), wo_v[...]).astype(BF16)
        q = q_ref[...]
        dqs = []
        for h in range(HEADS):
            lo = h * HEAD_DIM
            kcols, vcols = pl.ds(lo, HEAD_DIM), pl.ds(D_MODEL + lo, HEAD_DIM)
            qh, doh = q[:, lo:lo + HEAD_DIM], do[:, lo:lo + HEAD_DIM]
            p = _softmax_rows(_dot_nt(qh, kv_ref[:, kcols]))
            dp = _dot_nt(doh, kv_ref[:, vcols])
            dkv_ref[:, vcols] += _dot_tn(p.astype(BF16), doh)
            ds = (p * (dp - jnp.sum(dp * p, axis=-1, keepdims=True))).astype(BF16)
            dqs.append(_dot(ds, kv_ref[:, kcols]) * (HEAD_DIM ** -0.5))
            dkv_ref[:, kcols] += _dot_tn(ds, qh)
        dq = jnp.concatenate(dqs, axis=-1).astype(BF16)
        dq_ref[...] = dq
        dh2 = _dot_nt(dq, wq_v[...])
        xh, r = _rms_fwd(x1_ref[...])
        dg_ref[...] += _colsum(dh2 * xh)
        dx1 = dx2v + _rms_bwd(dh2, xh, r, g_ref[...])
        dx1_ref[...] = dx1
        dx1b_ref[...] = dx1.astype(BF16)

    row = lambda w: pl.BlockSpec((tm, w), lambda i: (i, 0))
    per_b = pl.BlockSpec((N_MEM, 2 * D_MODEL), lambda i: (i // tps, 0))
    return pl.pallas_call(
        body, name="bwd_attn", grid=(n_tiles,),
        in_specs=[row(D_MODEL), row(D_MODEL), row(D_MODEL), per_b, _full((1, D_MODEL)), pl.BlockSpec(memory_space=pl.ANY),
                  _full(after.shape)],
        out_specs=[row(D_MODEL), row(D_MODEL), row(D_MODEL), per_b, _full((1, D_MODEL))],
        out_shape=[jax.ShapeDtypeStruct((tokens, D_MODEL), F32), jax.ShapeDtypeStruct((tokens, D_MODEL), BF16),
                   jax.ShapeDtypeStruct((tokens, D_MODEL), BF16),
                   jax.ShapeDtypeStruct((n_b * N_MEM, 2 * D_MODEL), F32), jax.ShapeDtypeStruct((1, D_MODEL), F32)],
        scratch_shapes=[pltpu.VMEM((D_MODEL, D_MODEL), BF16), pltpu.VMEM((D_MODEL, D_MODEL), BF16), pltpu.SemaphoreType.DMA],
        compiler_params=_params(),
    )(dx2, x1, q, kv, g_x, gw, after)


def _bwd_kv(dkv, mem2d, gw, g_mem):
    rows = mem2d.shape[0]
    n_b = rows // N_MEM

    def body(dkv_ref, mem_ref, gw_hbm, dkvb_ref, dg_ref, wkv_v, sem):
        @pl.when(pl.program_id(0) == 0)
        def _():
            copies = _load_weight(gw_hbm, "w_kv", wkv_v, sem)
            for cp in copies:
                cp.start()
            for cp in copies:
                cp.wait()
            dg_ref[...] = jnp.zeros_like(dg_ref)

        dkvb = dkv_ref[...].astype(BF16)
        dkvb_ref[...] = dkvb
        dmn = _dot(dkvb, wkv_v[...])
        mh, _ = _rms_fwd(mem_ref[...])
        dg_ref[...] += _colsum(dmn * mh)

    del g_mem
    return pl.pallas_call(
        body, name="bwd_kv", grid=(n_b,),
        in_specs=[pl.BlockSpec((N_MEM, 2 * D_MODEL), lambda b: (b, 0)), pl.BlockSpec((N_MEM, D_MODEL), lambda b: (b, 0)),
                  pl.BlockSpec(memory_space=pl.ANY)],
        out_specs=[pl.BlockSpec((N_MEM, 2 * D_MODEL), lambda b: (b, 0)), _full((1, D_MODEL))],
        out_shape=[jax.ShapeDtypeStruct((rows, 2 * D_MODEL), BF16), jax.ShapeDtypeStruct((1, D_MODEL), F32)],
        scratch_shapes=[pltpu.VMEM((2 * D_MODEL, D_MODEL), BF16), pltpu.SemaphoreType.DMA],
        compiler_params=_params(),
    )(dkv, mem2d, gw)


def _bwd_mix(dx1, x2d, u_all, c_all, pooled_all, gw, g_mix, conv_w, ln_g, ln_b, pool_w, pool_scale, after, seq, tm):
    tokens = x2d.shape[0]
    n_tiles = tokens // tm
    tps = seq // tm

    def body(dx1_ref, x_ref, u_ref, c_ref, pooled_ref, gmix_ref, gw_hbm, cw_ref, lng_ref, lnb_ref, pw_ref, ps_ref,
             after_ref, dx_ref, du_ref, dgmix_ref, dcw_ref, dcb_ref, dlng_ref, dlnb_ref, dpw_ref, dps_ref,
             win_v, wout_v, dc_carry, e_carry, sem):
        del after_ref
        i = pl.program_id(0)
        t = n_tiles - 1 - i

        @pl.when(i == 0)
        def _():
            copies = _load_weight(gw_hbm, "w_in", win_v, sem) + _load_weight(gw_hbm, "w_out", wout_v, sem)
            for cp in copies:
                cp.start()
            for cp in copies:
                cp.wait()
            for ref in (dgmix_ref, dcw_ref, dcb_ref, dlng_ref, dlnb_ref, dpw_ref, dps_ref):
                ref[...] = jnp.zeros_like(ref)

        @pl.when(t % tps == tps - 1)
        def _():
            dc_carry[...] = jnp.zeros_like(dc_carry)
            e_carry[...] = jnp.zeros_like(e_carry)

        dx1v = dx1_ref[...]
        dymix = _dot_nt(dx1v.astype(BF16), wout_v[...])
        dyc, dyp = dymix[:, :D_CONV], dymix[:, D_CONV:]
        u = u_ref[...]
        val, gate = u[:, :D_CONV], u[:, D_CONV:2 * D_CONV]

        conv = c_ref[...]
        mu = jnp.mean(conv, axis=-1, keepdims=True)
        cen = conv - mu
        rs = lax.rsqrt(jnp.mean(cen * cen, axis=-1, keepdims=True) + EPS)
        chat = cen * rs
        ln = chat * lng_ref[...] + lnb_ref[...]
        sl = _sigmoid(ln)
        dln = dyc * (sl * (1.0 + ln * (1.0 - sl)))
        dlng_ref[...] += _colsum(dln * chat)
        dlnb_ref[...] += _colsum(dln)
        dchat = dln * lng_ref[...]
        dc = rs * (dchat - jnp.mean(dchat, axis=-1, keepdims=True)
                   - chat * jnp.mean(dchat * chat, axis=-1, keepdims=True))
        dcb_ref[...] += _colsum(dc)
        sg = _sigmoid(gate)
        hc = val * sg
        ext = jnp.concatenate([dc, dc_carry[...]], axis=0)
        dc_carry[...] = dc[:CONV_HALO, :]
        dhc = jnp.zeros((tm, D_CONV), F32)
        for k in range(CONV_WIDTH):
            ahead = CONV_WIDTH - 1 - k
            tap = (ext if ahead == 0 else pltpu.roll(ext, tm + CONV_HALO - ahead, 0))[:tm, :]
            dhc = dhc + cw_ref[k:k + 1, :] * tap
            dcw_ref[k:k + 1, :] += _colsum(hc * tap)
        du_ref[:, :D_CONV] = (dhc * sg).astype(BF16)
        du_ref[:, D_CONV:2 * D_CONV] = (dhc * val * (sg * (1.0 - sg))).astype(BF16)

        pos = lax.broadcasted_iota(jnp.int32, (tm, 1), 0) + (t % tps) * tm
        es, dpooled = [], []
        for g, w in enumerate(POOL_WINDOWS):
            cols = pl.ds(g * POOL_GROUP_DIM, POOL_GROUP_DIM)
            lo = g * POOL_GROUP_DIM
            pooled = pooled_ref[:, cols]
            pw = pw_ref[g].astype(BF16)
            dyg = dyp[:, lo:lo + POOL_GROUP_DIM]
            dps_ref[:, cols] += _colsum(dyg * _dot(pooled, pw))
            dmixed = (dyg * ps_ref[:, cols]).astype(BF16)
            dpw_ref[g] += _dot_tn(pooled, dmixed)
            dpo = _dot_nt(dmixed, pw)
            dpooled.append(dpo)
            es.append(dpo / jnp.minimum(pos + 1, w).astype(F32))
        e = jnp.concatenate(es, axis=-1)
        run = jnp.concatenate([e, e_carry[...]], axis=0)
        e_carry[...] = e[:POOL_HALO, :]
        rows = tm + POOL_HALO
        for g, w in enumerate(POOL_WINDOWS):
            lo = g * POOL_GROUP_DIM
            run = run[:, POOL_GROUP_DIM if g else 0:]
            run = run + pltpu.roll(run, rows - w // 2, 0)
            du_ref[:, 2 * D_CONV + lo:2 * D_CONV + lo + POOL_GROUP_DIM] = (
                run[:tm, :POOL_GROUP_DIM] - dpooled[g]).astype(BF16)

        dh1 = _dot(du_ref[...], win_v[...])
        xh, r = _rms_fwd(x_ref[...])
        dgmix_ref[...] += _colsum(dh1 * xh)
        dx_ref[...] = dx1v + _rms_bwd(dh1, xh, r, gmix_ref[...])

    rev = lambda w: pl.BlockSpec((tm, w), lambda i: (n_tiles - 1 - i, 0))
    return pl.pallas_call(
        body, name="bwd_mix", grid=(n_tiles,),
        in_specs=[rev(D_MODEL), rev(D_MODEL), rev(D_IN), rev(D_CONV), rev(D_POOL), _full((1, D_MODEL)),
                  pl.BlockSpec(memory_space=pl.ANY), _full((CONV_WIDTH, D_CONV)), _full((1, D_CONV)), _full((1, D_CONV)),
                  _full((4, POOL_GROUP_DIM, POOL_GROUP_DIM)), _full((1, D_POOL)), _full(after.shape)],
        out_specs=[rev(D_MODEL), rev(D_IN), _full((1, D_MODEL)), _full((CONV_WIDTH, D_CONV)), _full((1, D_CONV)),
                   _full((1, D_CONV)), _full((1, D_CONV)), _full((4, POOL_GROUP_DIM, POOL_GROUP_DIM)), _full((1, D_POOL))],
        out_shape=[jax.ShapeDtypeStruct((tokens, D_MODEL), F32), jax.ShapeDtypeStruct((tokens, D_IN), BF16),
                   jax.ShapeDtypeStruct((1, D_MODEL), F32), jax.ShapeDtypeStruct((CONV_WIDTH, D_CONV), F32),
                   jax.ShapeDtypeStruct((1, D_CONV), F32), jax.ShapeDtypeStruct((1, D_CONV), F32),
                   jax.ShapeDtypeStruct((1, D_CONV), F32),
                   jax.ShapeDtypeStruct((4, POOL_GROUP_DIM, POOL_GROUP_DIM), F32), jax.ShapeDtypeStruct((1, D_POOL), F32)],
        scratch_shapes=[pltpu.VMEM((D_IN, D_MODEL), BF16), pltpu.VMEM((D_MODEL, D_MODEL), BF16),
                        pltpu.VMEM((CONV_HALO, D_CONV), F32), pltpu.VMEM((POOL_HALO, D_POOL), F32),
                        pltpu.SemaphoreType.DMA],
        compiler_params=_params(),
    )(dx1, x2d, u_all, c_all, pooled_all, g_mix, gw, conv_w, ln_g, ln_b, pool_w, pool_scale, after)


def _wgrad(a, b, name, tm=256):
    tokens, m = a.shape
    n = b.shape[1]

    def body(a_ref, b_ref, out_ref):
        out_ref[...] = _dot_tn(a_ref[...], b_ref[...]).astype(out_ref.dtype)

    return pl.pallas_call(
        body, name=name, grid=(m // tm,),
        in_specs=[pl.BlockSpec((tokens, tm), lambda i: (0, i)), _full((tokens, n))],
        out_specs=pl.BlockSpec((tm, n), lambda i: (i, 0)),
        out_shape=jax.ShapeDtypeStruct((m, n), BF16),
        compiler_params=_params(),
    )(a, b)


def _adamw_update(w, g, m, v):
    nm = ADAM_B1 * m + (1.0 - ADAM_B1) * g
    nv = ADAM_B2 * v + (1.0 - ADAM_B2) * (g * g)
    m_hat = nm / (1.0 - ADAM_B1 ** ADAM_STEP)
    v_hat = nv / (1.0 - ADAM_B2 ** ADAM_STEP)
    return -ADAM_LR * (m_hat / (jnp.sqrt(v_hat) + ADAM_EPS) + ADAM_WD * w), nm, nv


def _adamw_small(ws, gs, ms, vs):
    n = len(ws)

    def body(*refs):
        ins, outs = refs[:4 * n], refs[4 * n:]
        for k in range(n):
            d, nm, nv = _adamw_update(*[ins[j * n + k][...] for j in range(4)])
            outs[k][...] = d
            outs[n + k][...] = nm
            outs[2 * n + k][...] = nv

    vmem = pl.BlockSpec(memory_space=pltpu.VMEM)
    outs = pl.pallas_call(
        body, name="adamw_small",
        in_specs=[vmem] * (4 * n), out_specs=[vmem] * (3 * n),
        out_shape=[jax.ShapeDtypeStruct(w.shape, F32) for w in ws] * 3,
    )(*ws, *gs, *ms, *vs)
    return outs[:n], outs[n:2 * n], outs[2 * n:]


def _adamw(w, g, m, v, name):
    rows, cols = w.shape
    tile = rows
    for cand in (512, 256, 128, 64, 32, 16, 8):
        if rows % cand == 0:
            tile = cand
            break

    def body(w_ref, g_ref, m_ref, v_ref, d_ref, nm_ref, nv_ref):
        d_ref[...], nm_ref[...], nv_ref[...] = _adamw_update(w_ref[...], g_ref[...], m_ref[...], v_ref[...])

    spec = pl.BlockSpec((tile, cols), lambda i: (i, 0))
    return pl.pallas_call(
        body, name=name, grid=(rows // tile,),
        in_specs=[spec] * 4, out_specs=[spec] * 3,
        out_shape=[jax.ShapeDtypeStruct((rows, cols), F32)] * 3,
        compiler_params=_params(("arbitrary",)),
    )(w, g, m, v)


SMALL = (("norm_mix_g", (1, 1024)), ("conv_dw_b", (1, 512)), ("conv_ln_g", (1, 512)), ("conv_ln_b", (1, 512)),
         ("pool_w", (1, 4, 128, 128)), ("pool_scale", (1, 512)), ("norm_xattn_g", (1, 1024)), ("norm_mem_g", (1, 1024)),
         ("norm_ffn_g", (1, 1024)), ("ffn_dw_b", (1, 5632)), ("norm_final_g", (1024,)))
LANES = 128


def _pack_rows(arrs):
    flat = jnp.concatenate([a.reshape(-1) for a in arrs])
    pad = (-flat.shape[0]) % (8 * LANES)
    return jnp.pad(flat, (0, pad)).reshape(-1, LANES)


def kernel(x, mem, norm_mix_g, w_in, conv_dw_w, conv_dw_b, conv_ln_g, conv_ln_b, pool_w, pool_scale, w_out, norm_xattn_g, norm_mem_g, w_q, w_kv, w_o, norm_ffn_g, w_up, ffn_dw_w, ffn_dw_b, w_down, norm_final_g, loss_target, m_norm_mix_g, m_w_in, m_conv_dw_w, m_conv_dw_b, m_conv_ln_g, m_conv_ln_b, m_pool_w, m_pool_scale, m_w_out, m_norm_xattn_g, m_norm_mem_g, m_w_q, m_w_kv, m_w_o, m_norm_ffn_g, m_w_up, m_ffn_dw_w, m_ffn_dw_b, m_w_down, m_norm_final_g, v_norm_mix_g, v_w_in, v_conv_dw_w, v_conv_dw_b, v_conv_ln_g, v_conv_ln_b, v_pool_w, v_pool_scale, v_w_out, v_norm_xattn_g, v_norm_mem_g, v_w_q, v_w_kv, v_w_o, v_norm_ffn_g, v_w_up, v_ffn_dw_w, v_ffn_dw_b, v_w_down, v_norm_final_g):
    weights = dict(norm_mix_g=norm_mix_g, w_in=w_in, conv_dw_w=conv_dw_w, conv_dw_b=conv_dw_b, conv_ln_g=conv_ln_g,
                   conv_ln_b=conv_ln_b, pool_w=pool_w, pool_scale=pool_scale, w_out=w_out, norm_xattn_g=norm_xattn_g,
                   norm_mem_g=norm_mem_g, w_q=w_q, w_kv=w_kv, w_o=w_o, norm_ffn_g=norm_ffn_g, w_up=w_up,
                   ffn_dw_w=ffn_dw_w, ffn_dw_b=ffn_dw_b, w_down=w_down, norm_final_g=norm_final_g)
    moments_m = dict(norm_mix_g=m_norm_mix_g, w_in=m_w_in, conv_dw_w=m_conv_dw_w, conv_dw_b=m_conv_dw_b,
                     conv_ln_g=m_conv_ln_g, conv_ln_b=m_conv_ln_b, pool_w=m_pool_w, pool_scale=m_pool_scale,
                     w_out=m_w_out, norm_xattn_g=m_norm_xattn_g, norm_mem_g=m_norm_mem_g, w_q=m_w_q, w_kv=m_w_kv,
                     w_o=m_w_o, norm_ffn_g=m_norm_ffn_g, w_up=m_w_up, ffn_dw_w=m_ffn_dw_w, ffn_dw_b=m_ffn_dw_b,
                     w_down=m_w_down, norm_final_g=m_norm_final_g)
    moments_v = dict(norm_mix_g=v_norm_mix_g, w_in=v_w_in, conv_dw_w=v_conv_dw_w, conv_dw_b=v_conv_dw_b,
                     conv_ln_g=v_conv_ln_g, conv_ln_b=v_conv_ln_b, pool_w=v_pool_w, pool_scale=v_pool_scale,
                     w_out=v_w_out, norm_xattn_g=v_norm_xattn_g, norm_mem_g=v_norm_mem_g, w_q=v_w_q, w_kv=v_w_kv,
                     w_o=v_w_o, norm_ffn_g=v_norm_ffn_g, w_up=v_w_up, ffn_dw_w=v_ffn_dw_w, ffn_dw_b=v_ffn_dw_b,
                     w_down=v_w_down, norm_final_g=v_norm_final_g)
    order = list(weights)
    transposed = ("w_in", "w_kv", "w_up")

    n_b, seq, _ = x.shape
    tokens = n_b * seq
    tm_mix = min(512, seq // 2)
    tm_ffn = min(256, seq // 2)
    dev = 4 * lax.axis_index("x") + 2 * lax.axis_index("y") + lax.axis_index("c")

    shards = [weights[n][0].T if n in transposed else weights[n][0] for n, _ in W_ROWS]
    packed = jnp.concatenate(shards, axis=0).astype(BF16)
    small_sharded = _pack_rows([conv_dw_w[0], ffn_dw_w[0]])
    gw, gsmall = _all_gather([packed, small_sharded], "weights_all_gather")
    gflat = gsmall.reshape(N_DEV, -1)
    n_cw = CONV_WIDTH * (D_CONV // N_DEV)
    n_fw = FFN_CONV_WIDTH * (2 * D_FF // N_DEV)
    conv_w = gflat[:, :n_cw].reshape(N_DEV, CONV_WIDTH, D_CONV // N_DEV).transpose(1, 0, 2).reshape(CONV_WIDTH, D_CONV)
    ffn_w = gflat[:, n_cw:n_cw + n_fw].reshape(N_DEV, FFN_CONV_WIDTH, 2 * D_FF // N_DEV).transpose(1, 0, 2).reshape(
        FFN_CONV_WIDTH, 2 * D_FF)

    x2d = x.reshape(tokens, D_MODEL)
    mem2d = mem.reshape(n_b * N_MEM, D_MODEL)
    tgt2d = loss_target.reshape(tokens, D_MODEL)
    g_final = norm_final_g.reshape(1, D_MODEL)

    x1, u_all, c_all, pooled_all, ymix, h1 = _fwd_mix(
        x2d, gw, norm_mix_g, conv_w, conv_dw_b, conv_ln_g, conv_ln_b, pool_w[0], pool_scale, seq, tm_mix)
    mem_n, kv = _fwd_kv(mem2d, gw, norm_mem_g)
    x2, h2, q, o = _fwd_attn(x1, kv, gw, norm_xattn_g, seq, tm_mix)
    uu_all, a_all, h3, dx3, dx3b, loss_part, dg_final = _fwd_ffn(
        x2, tgt2d, gw, norm_ffn_g, ffn_w, ffn_dw_b, g_final, seq, tm_ffn)

    def reduce_start(names, tag):
        parts = [part[n].reshape(N_DEV, W_OFF[n][1], D_MODEL) for n in names]
        own, landed = _exchange_sibling(parts, "rs_sibling_exchange_" + tag)
        sums = _chip_partial_sums(own, landed, RS_TILE[tag], "rs_chip_partial_sums_" + tag)
        return own, landed, _chip_exchange_start(sums, "rs_chip_exchange_start_" + tag)

    def reduce_finish(own, landed, flight, after, tag):
        from_chips = _chip_exchange_wait(*flight[:4], after, "rs_chip_exchange_wait_" + tag)
        return _final_grad_sums(own, landed, from_chips, RS_TILE[tag], "rs_final_sums_" + tag)

    part = {}
    dx2, dx2b, duu, d_ffn_b, d_ffn_w, dg_ffn = _bwd_ffn(dx3, x2, uu_all, gw, norm_ffn_g, ffn_w, ffn_dw_b, seq, tm_ffn)
    part["w_up"] = _wgrad(duu, h3, "wgrad_w_up")
    part["w_down"] = _wgrad(a_all, dx3b, "wgrad_w_down")
    own_a, landed_a, flight_a = reduce_start(RS_GROUPS["a"], "a")
    dx1, dx1b, dq, dkv, dg_x = _bwd_attn(dx2, x1, q, kv, gw, norm_xattn_g, flight_a[4], seq, tm_mix)
    dkv_b, dg_mem = _bwd_kv(dkv, mem2d, gw, norm_mem_g)
    part["w_q"] = _wgrad(h2, dq, "wgrad_w_q")
    part["w_kv"] = _wgrad(dkv_b, mem_n, "wgrad_w_kv")
    part["w_o"] = _wgrad(o, dx2b, "wgrad_w_o")
    own_b, landed_b, flight_b = reduce_start(RS_GROUPS["b"], "b")
    dx, du, dg_mix, d_conv_w, d_conv_b, d_ln_g, d_ln_b, d_pool_w, d_pool_scale = _bwd_mix(
        dx1, x2d, u_all, c_all, pooled_all, gw, norm_mix_g, conv_w, conv_ln_g, conv_ln_b, pool_w[0], pool_scale,
        flight_b[4], seq, tm_mix)
    grad_x = dx.reshape(x.shape)
    part["w_in"] = _wgrad(du, h1, "wgrad_w_in")
    part["w_out"] = _wgrad(ymix, dx1b, "wgrad_w_out")
    own_c, landed_c, flight_c = reduce_start(RS_GROUPS["c"], "c")
    g_group = {}
    g_group["a"] = reduce_finish(own_a, landed_a, flight_a, flight_c[4], "a")
    g_group["b"] = reduce_finish(own_b, landed_b, flight_b, g_group["a"], "b")
    g_group["c"] = reduce_finish(own_c, landed_c, flight_c, g_group["b"], "c")

    small_grads = dict(norm_mix_g=dg_mix, conv_dw_b=d_conv_b, conv_ln_g=d_ln_g, conv_ln_b=d_ln_b, pool_w=d_pool_w,
                       pool_scale=d_pool_scale, norm_xattn_g=dg_x, norm_mem_g=dg_mem, norm_ffn_g=dg_ffn,
                       ffn_dw_b=d_ffn_b, norm_final_g=dg_final)
    small_list = [small_grads[n] for n, _ in SMALL] + [d_conv_w, d_ffn_w, loss_part[:1]]
    (small_all,) = _all_gather([_pack_rows(small_list)], "small_grads_all_gather")
    small_sum = _sum_blocks(small_all).reshape(-1)

    grads = {}
    pos = 0
    for n, shape in SMALL:
        size = 1
        for s in shape:
            size *= s
        grads[n] = small_sum[pos:pos + size].reshape(shape)
        pos += size
    full_conv_w = small_sum[pos:pos + CONV_WIDTH * D_CONV].reshape(CONV_WIDTH, D_CONV)
    pos += CONV_WIDTH * D_CONV
    full_ffn_w = small_sum[pos:pos + FFN_CONV_WIDTH * 2 * D_FF].reshape(FFN_CONV_WIDTH, 2 * D_FF)
    loss = small_sum[pos + FFN_CONV_WIDTH * 2 * D_FF]
    grads["conv_dw_w"] = lax.dynamic_slice_in_dim(full_conv_w, dev * (D_CONV // N_DEV), D_CONV // N_DEV, axis=1)[None]
    grads["ffn_dw_w"] = lax.dynamic_slice_in_dim(full_ffn_w, dev * (2 * D_FF // N_DEV), 2 * D_FF // N_DEV, axis=1)[None]
    for tag, names in RS_GROUPS.items():
        off = 0
        for n in names:
            r = W_OFF[n][1]
            blk = g_group[tag][off:off + r]
            grads[n] = (blk.T if n in transposed else blk)[None]
            off += r

    delta, new_m, new_v = {}, {}, {}
    for n, _ in W_ROWS:
        shape = weights[n].shape
        as2d = lambda t: t.reshape(shape[1], shape[2])
        d, nm, nv = _adamw(as2d(weights[n]), as2d(grads[n]), as2d(moments_m[n]), as2d(moments_v[n]), "adamw_" + n)
        delta[n], new_m[n], new_v[n] = d.reshape(shape), nm.reshape(shape), nv.reshape(shape)
    small_names = [n for n in order if n not in W_OFF]
    two_d = lambda t: t.reshape(1, -1) if t.ndim == 1 else t
    outs = _adamw_small(*[[two_d(t[n]) for n in small_names] for t in (weights, grads, moments_m, moments_v)])
    for res, out in zip((delta, new_m, new_v), outs):
        for n, o in zip(small_names, out):
            res[n] = o.reshape(weights[n].shape)

    return (loss, grad_x, *[grads[n] for n in order], *[delta[n] for n in order],
            *[new_m[n] for n in order], *[new_v[n] for n in order])
```

```python
import functools

import jax
import jax.numpy as jnp
from jax import lax
from jax.experimental import pallas as pl
from jax.experimental.pallas import tpu as pltpu

F32 = jnp.float32
BF16 = jnp.bfloat16
MESH = pl.DeviceIdType.MESH

N_DEV = 8
D_MODEL = 1024
D_CONV = 512
D_POOL = 512
CONV_WIDTH = 31
POOL_WINDOWS = (2, 4, 8, 16)
POOL_GROUP_DIM = 128
D_IN = 1536
N_MEM = 256
HEADS = 4
HEAD_DIM = 256
D_FF = 2816
FFN_CONV_WIDTH = 3
EPS = 1e-6
ADAM_LR = 0.001
ADAM_B1 = 0.9
ADAM_B2 = 0.999
ADAM_EPS = 1e-08
ADAM_WD = 0.01
ADAM_STEP = 10

VMEM_LIMIT_V7X = 56 * 1024 * 1024
CONV_HALO = 32
POOL_HALO = 16
FFN_HALO = 8
FFN_CHUNK = 1408

W_ROWS = (("w_in", 192), ("w_out", 128), ("w_q", 128), ("w_kv", 256), ("w_o", 128), ("w_up", 704), ("w_down", 352))
W_OFF = {}
_o = 0
for _n, _r in W_ROWS:
    W_OFF[_n] = (_o, _r)
    _o += _r
PACK_ROWS = _o
RS_GROUPS = {"a": ("w_up", "w_down"), "b": ("w_q", "w_kv", "w_o"), "c": ("w_in", "w_out")}


def _dot(a, b):
    return jnp.dot(a, b, preferred_element_type=F32)


def _dot_nt(a, b):
    return lax.dot_general(a, b, (((1,), (1,)), ((), ())), preferred_element_type=F32)


def _dot_tn(a, b):
    return lax.dot_general(a, b, (((0,), (0,)), ((), ())), preferred_element_type=F32)


def _sigmoid(v):
    return 1.0 / (1.0 + jnp.exp(-v))


def _rms_fwd(v):
    r = lax.rsqrt(jnp.mean(v * v, axis=-1, keepdims=True) + EPS)
    return v * r, r


def _rms_bwd(dh, vh, r, g):
    gd = dh * g
    return r * (gd - vh * jnp.mean(gd * vh, axis=-1, keepdims=True))


def _colsum(v):
    return jnp.sum(v, axis=0, keepdims=True)


def _full(shape):
    return pl.BlockSpec(shape, lambda *_: (0,) * len(shape))


def _params(sem=("arbitrary",), vmem=VMEM_LIMIT_V7X):
    return pltpu.CompilerParams(dimension_semantics=sem, vmem_limit_bytes=vmem)


def _load_weight(g_hbm, name, dst, sem):
    off, rows = W_OFF[name]
    return [pltpu.make_async_copy(g_hbm.at[d, pl.ds(off, rows), :], dst.at[pl.ds(d * rows, rows), :], sem)
            for d in range(N_DEV)]


def _position():
    x, y, c = lax.axis_index("x"), lax.axis_index("y"), lax.axis_index("c")
    chips = [(1 - x, y), (x, 1 - y), (1 - x, 1 - y)]
    return x, y, c, chips


def _dev(px, py, pc):
    return 4 * px + 2 * py + pc


def _all_gather(arrs, name):
    n = len(arrs)

    def body(*refs):
        ins, outs = refs[:n], refs[n:2 * n]
        send_sems, recv_sems, local_sems = refs[2 * n:2 * n + 3]
        bounce = refs[2 * n + 3:]
        x, y, c, chips = _position()
        me, sibling = (x, y, c), (x, y, 1 - c)

        def copy(a, k, block, to, src=None):
            rows = outs[a].at[_dev(*block)]
            return pltpu.make_async_remote_copy(
                src_ref=rows if src is None else src, dst_ref=rows,
                send_sem=send_sems.at[a, k], recv_sem=recv_sems.at[a, k], device_id=to, device_id_type=MESH)

        sends = []
        for a in range(n):
            first = [copy(a, 0, me, sibling, src=ins[a])]
            first += [copy(a, 1 + j, me, (*chip, c), src=ins[a]) for j, chip in enumerate(chips)]
            for cp in first:
                cp.start()
            sends += first
        started = []
        for a in range(n):
            load = pltpu.make_async_copy(ins[a], bounce[a], local_sems.at[a, 0])
            load.start()
            load.wait()
            mine = pltpu.make_async_copy(bounce[a], outs[a].at[_dev(*me)], local_sems.at[a, 1])
            mine.start()
            started.append(mine)
        for j, chip in enumerate(chips):
            for a in range(n):
                copy(a, 1 + j, (*chip, c), me).wait_recv()
                passed = copy(a, 4 + j, (*chip, c), sibling)
                passed.start()
                sends.append(passed)
        for a in range(n):
            copy(a, 0, sibling, me).wait_recv()
            for j, chip in enumerate(chips):
                copy(a, 4 + j, (*chip, 1 - c), me).wait_recv()
        for cp in sends:
            cp.wait_send()
        for mine in started:
            mine.wait()

    any_spec = pl.BlockSpec(memory_space=pl.ANY)
    return pl.pallas_call(
        body, name=name,
        out_shape=[jax.ShapeDtypeStruct((N_DEV,) + a.shape, a.dtype) for a in arrs],
        in_specs=[any_spec] * n, out_specs=[any_spec] * n,
        scratch_shapes=[pltpu.SemaphoreType.DMA((n, 7)), pltpu.SemaphoreType.DMA((n, 7)), pltpu.SemaphoreType.DMA((n, 2))]
        + [pltpu.VMEM(a.shape, a.dtype) for a in arrs],
    )(*arrs)


def _exchange_sibling(parts, name):
    n = len(parts)

    def body(*refs):
        ins, outs = refs[:n], refs[n:2 * n]
        send_sems, recv_sems = refs[2 * n:]
        x, y, c, chips = _position()
        sibling = (x, y, 1 - c)
        copies = []
        for k in range(n):
            for j, chip in enumerate([(x, y)] + chips):
                cp = pltpu.make_async_remote_copy(
                    src_ref=ins[k].at[_dev(*chip, 1 - c)], dst_ref=outs[k].at[j],
                    send_sem=send_sems.at[k, j], recv_sem=recv_sems.at[k, j], device_id=sibling, device_id_type=MESH)
                cp.start()
                copies.append(cp)
        for cp in copies:
            cp.wait_recv()
        for cp in copies:
            cp.wait_send()

    any_spec = pl.BlockSpec(memory_space=pl.ANY)
    return pl.pallas_call(
        body, name=name,
        out_shape=[jax.ShapeDtypeStruct((4,) + p.shape[1:], p.dtype) for p in parts],
        in_specs=[any_spec] * n, out_specs=[any_spec] * n,
        scratch_shapes=[pltpu.SemaphoreType.DMA((n, 4)), pltpu.SemaphoreType.DMA((n, 4))],
    )(*parts)


def _chip_exchange_start(sums, name):
    n = len(sums)

    def body(*refs):
        s_refs, land_refs = refs[:n], refs[n:2 * n]
        send_sems, recv_sems = refs[2 * n:2 * n + 2]
        token = refs[-1]
        x, y, c, chips = _position()
        for k in range(n):
            for j, chip in enumerate(chips):
                pltpu.make_async_remote_copy(
                    src_ref=s_refs[k].at[j], dst_ref=land_refs[k].at[j], send_sem=send_sems.at[3 * k + j],
                    recv_sem=recv_sems.at[3 * k + j], device_id=(*chip, c), device_id_type=MESH).start()
        token[...] = jnp.zeros_like(token)

    hbm = pl.BlockSpec(memory_space=pltpu.HBM)
    sem = pl.BlockSpec(memory_space=pltpu.SEMAPHORE)
    thru = [pltpu.HBM(s.shape, s.dtype) for s in sums]
    outs = pl.pallas_call(
        body, name=name,
        out_shape=(pltpu.SemaphoreType.DMA((3 * n,)), pltpu.SemaphoreType.DMA((3 * n,)), *thru, *thru,
                   jax.ShapeDtypeStruct((8, 128), F32)),
        in_specs=[hbm] * (2 * n), out_specs=(sem, sem, *[hbm] * (2 * n), pl.BlockSpec(memory_space=pltpu.VMEM)),
        input_output_aliases={k: 2 + k for k in range(2 * n)},
        compiler_params=pltpu.CompilerParams(has_side_effects=pltpu.SideEffectType.DATAFLOW_SIDE_EFFECTING),
    )(*[pltpu.with_memory_space_constraint(s, pltpu.HBM) for s in sums],
      *[pltpu.with_memory_space_constraint(lax.empty(s.shape, s.dtype), pltpu.HBM) for s in sums])
    return outs[0], outs[1], outs[2:2 + n], outs[2 + n:2 + 2 * n], outs[-1]


def _chip_exchange_wait(send_sems, recv_sems, s_thru, land_thru, after, name):
    n = len(s_thru)

    def body(*refs):
        s_refs, land_refs = refs[:n], refs[n:2 * n]
        send_sems, recv_sems = refs[2 * n:2 * n + 2]
        x, y, c, chips = _position()
        for k in range(n):
            for j, chip in enumerate(chips):
                cp = pltpu.make_async_remote_copy(
                    src_ref=s_refs[k].at[j], dst_ref=land_refs[k].at[j], send_sem=send_sems.at[3 * k + j],
                    recv_sem=recv_sems.at[3 * k + j], device_id=(*chip, c), device_id_type=MESH)
                cp.wait_send()
                cp.wait_recv()

    hbm = pl.BlockSpec(memory_space=pltpu.HBM)
    sem = pl.BlockSpec(memory_space=pltpu.SEMAPHORE)
    thru = [pltpu.HBM(s.shape, s.dtype) for s in s_thru]
    outs = pl.pallas_call(
        body, name=name,
        out_shape=(*thru, *thru),
        in_specs=[hbm] * (2 * n) + [sem, sem, pl.BlockSpec(memory_space=pl.ANY)], out_specs=[hbm] * (2 * n),
        input_output_aliases={k: k for k in range(2 * n)},
        compiler_params=pltpu.CompilerParams(has_side_effects=pltpu.SideEffectType.DATAFLOW_SIDE_EFFECTING),
    )(*s_thru, *land_thru, send_sems, recv_sems, after)
    return outs[n:]


def _owner_table():
    x, y, c = lax.axis_index("x"), lax.axis_index("y"), lax.axis_index("c")
    chips = [(x, y), (1 - x, y), (x, 1 - y), (1 - x, 1 - y)]
    return jnp.stack([_dev(px, py, c) for px, py in chips]).astype(jnp.int32)


def _chip_partial_sums(table, g, from_sibling, name):
    _, rows, cols = g.shape

    def body(tab_ref, g_ref, l_ref, out_ref):
        del tab_ref
        out_ref[...] = (g_ref[...].astype(F32) + l_ref[...].astype(F32)).astype(out_ref.dtype)

    grid_spec = pltpu.PrefetchScalarGridSpec(
        num_scalar_prefetch=1, grid=(3,),
        in_specs=[pl.BlockSpec((None, rows, cols), lambda j, tab: (tab[j + 1], 0, 0)),
                  pl.BlockSpec((None, rows, cols), lambda j, tab: (j + 1, 0, 0))],
        out_specs=pl.BlockSpec((None, rows, cols), lambda j, tab: (j, 0, 0)))
    return pl.pallas_call(
        body, name=name, grid_spec=grid_spec,
        out_shape=jax.ShapeDtypeStruct((3, rows, cols), BF16),
        compiler_params=_params(("arbitrary",)),
    )(table, g, from_sibling)


def _final_grad_sums(table, g, from_sibling, from_chips, name):
    _, rows, cols = g.shape
    tile = rows // 2

    def body(tab_ref, g_ref, l_ref, c_ref, out_ref):
        del tab_ref
        acc = g_ref[...].astype(F32) + l_ref[...].astype(F32)
        for j in range(3):
            acc = acc + c_ref[j].astype(F32)
        out_ref[...] = acc

    grid_spec = pltpu.PrefetchScalarGridSpec(
        num_scalar_prefetch=1, grid=(2,),
        in_specs=[pl.BlockSpec((None, tile, cols), lambda t, tab: (tab[0], t, 0)),
                  pl.BlockSpec((None, tile, cols), lambda t, tab: (0, t, 0)),
                  pl.BlockSpec((3, tile, cols), lambda t, tab: (0, t, 0))],
        out_specs=pl.BlockSpec((tile, cols), lambda t, tab: (t, 0)))
    return pl.pallas_call(
        body, name=name, grid_spec=grid_spec,
        out_shape=jax.ShapeDtypeStruct((rows, cols), F32),
        compiler_params=_params(("arbitrary",)),
    )(table, g, from_sibling, from_chips)


def _sum_blocks(g8):
    _, rows, cols = g8.shape

    def body(g_ref, out_ref):
        acc = g_ref[0]
        for d in range(1, N_DEV):
            acc = acc + g_ref[d]
        out_ref[...] = acc

    return pl.pallas_call(
        body, name="small_grad_sum", grid=(1,),
        in_specs=[_full((N_DEV, rows, cols))], out_specs=_full((rows, cols)),
        out_shape=jax.ShapeDtypeStruct((rows, cols), F32),
        compiler_params=_params(("arbitrary",)),
    )(g8)


def _fwd_mix(x2d, gw, g_mix, conv_w, conv_b, ln_g, ln_b, pool_w, pool_scale, seq, tm):
    tokens = x2d.shape[0]
    n_tiles = tokens // tm
    tps = seq // tm

    def body(x_ref, gmix_ref, gw_hbm, cw_ref, cb_ref, lng_ref, lnb_ref, pw_ref, ps_ref,
             x1_ref, u_ref, c_ref, pooled_ref, ymix_ref, h1_ref,
             win_v, wout_v, hc_carry, up_carry, sem):
        i = pl.program_id(0)

        @pl.when(i == 0)
        def _():
            copies = _load_weight(gw_hbm, "w_in", win_v, sem) + _load_weight(gw_hbm, "w_out", wout_v, sem)
            for cp in copies:
                cp.start()
            for cp in copies:
                cp.wait()

        @pl.when(i % tps == 0)
        def _():
            hc_carry[...] = jnp.zeros_like(hc_carry)
            up_carry[...] = jnp.zeros_like(up_carry)

        x = x_ref[...]
        xh, _ = _rms_fwd(x)
        h1 = (xh * gmix_ref[...]).astype(BF16)
        h1_ref[...] = h1
        u = _dot_nt(h1, win_v[...])
        u_ref[...] = u
        val, gate, up = u[:, :D_CONV], u[:, D_CONV:2 * D_CONV], u[:, 2 * D_CONV:]

        hc = val * _sigmoid(gate)
        ext = jnp.concatenate([hc_carry[...], hc], axis=0)
        hc_carry[...] = hc[tm - CONV_HALO:, :]
        conv = jnp.broadcast_to(cb_ref[...], (tm, D_CONV))
        for k in range(CONV_WIDTH):
            shift = CONV_WIDTH - 1 - k
            tap = ext if shift == 0 else pltpu.roll(ext, shift, 0)
            conv = conv + cw_ref[k:k + 1, :] * tap[CONV_HALO:, :]
        c_ref[...] = conv
        mu = jnp.mean(conv, axis=-1, keepdims=True)
        cen = conv - mu
        ln = cen * lax.rsqrt(jnp.mean(cen * cen, axis=-1, keepdims=True) + EPS) * lng_ref[...] + lnb_ref[...]
        y_conv = ln * _sigmoid(ln)

        extp = jnp.concatenate([up_carry[...], up], axis=0)
        up_carry[...] = up[tm - POOL_HALO:, :]
        pos = lax.broadcasted_iota(jnp.int32, (tm, 1), 0) + (i % tps) * tm
        run = extp
        mixed = []
        for g, w in enumerate(POOL_WINDOWS):
            lo = g * POOL_GROUP_DIM
            run = run[:, POOL_GROUP_DIM if g else 0:]
            run = run + pltpu.roll(run, w // 2, 0)
            cnt = jnp.minimum(pos + 1, w).astype(F32)
            pooled = run[POOL_HALO:, :POOL_GROUP_DIM] / cnt - up[:, lo:lo + POOL_GROUP_DIM]
            pooled = pooled.astype(BF16)
            pooled_ref[:, lo:lo + POOL_GROUP_DIM] = pooled
            mixed.append(_dot(pooled, pw_ref[g].astype(BF16)))
        y_pool = jnp.concatenate(mixed, axis=-1) * ps_ref[...]

        ymix = jnp.concatenate([y_conv, y_pool], axis=-1).astype(BF16)
        ymix_ref[...] = ymix
        x1_ref[...] = x + _dot(ymix, wout_v[...])

    row = lambda w: pl.BlockSpec((tm, w), lambda i: (i, 0))
    return pl.pallas_call(
        body, name="fwd_mix", grid=(n_tiles,),
        in_specs=[row(D_MODEL), _full((1, D_MODEL)), pl.BlockSpec(memory_space=pl.ANY),
                  _full((CONV_WIDTH, D_CONV)), _full((1, D_CONV)), _full((1, D_CONV)), _full((1, D_CONV)),
                  _full((4, POOL_GROUP_DIM, POOL_GROUP_DIM)), _full((1, D_POOL))],
        out_specs=[row(D_MODEL), row(D_IN), row(D_CONV), row(D_POOL), row(D_MODEL), row(D_MODEL)],
        out_shape=[jax.ShapeDtypeStruct((tokens, D_MODEL), F32), jax.ShapeDtypeStruct((tokens, D_IN), F32),
                   jax.ShapeDtypeStruct((tokens, D_CONV), F32), jax.ShapeDtypeStruct((tokens, D_POOL), BF16),
                   jax.ShapeDtypeStruct((tokens, D_MODEL), BF16), jax.ShapeDtypeStruct((tokens, D_MODEL), BF16)],
        scratch_shapes=[pltpu.VMEM((D_IN, D_MODEL), BF16), pltpu.VMEM((D_MODEL, D_MODEL), BF16),
                        pltpu.VMEM((CONV_HALO, D_CONV), F32), pltpu.VMEM((POOL_HALO, D_POOL), F32),
                        pltpu.SemaphoreType.DMA],
        compiler_params=_params(),
    )(x2d, g_mix, gw, conv_w, conv_b, ln_g, ln_b, pool_w, pool_scale)


def _fwd_kv(mem2d, gw, g_mem):
    rows = mem2d.shape[0]
    n_b = rows // N_MEM

    def body(mem_ref, g_ref, gw_hbm, mn_ref, kv_ref, wkv_v, sem):
        @pl.when(pl.program_id(0) == 0)
        def _():
            copies = _load_weight(gw_hbm, "w_kv", wkv_v, sem)
            for cp in copies:
                cp.start()
            for cp in copies:
                cp.wait()

        mh, _ = _rms_fwd(mem_ref[...])
        mn = (mh * g_ref[...]).astype(BF16)
        mn_ref[...] = mn
        kv_ref[...] = _dot_nt(mn, wkv_v[...]).astype(BF16)

    return pl.pallas_call(
        body, name="fwd_kv", grid=(n_b,),
        in_specs=[pl.BlockSpec((N_MEM, D_MODEL), lambda b: (b, 0)), _full((1, D_MODEL)), pl.BlockSpec(memory_space=pl.ANY)],
        out_specs=[pl.BlockSpec((N_MEM, D_MODEL), lambda b: (b, 0)), pl.BlockSpec((N_MEM, 2 * D_MODEL), lambda b: (b, 0))],
        out_shape=[jax.ShapeDtypeStruct((rows, D_MODEL), BF16), jax.ShapeDtypeStruct((rows, 2 * D_MODEL), BF16)],
        scratch_shapes=[pltpu.VMEM((2 * D_MODEL, D_MODEL), BF16), pltpu.SemaphoreType.DMA],
        compiler_params=_params(),
    )(mem2d, g_mem, gw)


def _softmax_rows(s):
    e = jnp.exp(s - jnp.max(s, axis=-1, keepdims=True))
    return e / jnp.sum(e, axis=-1, keepdims=True)


def _fwd_attn(x1, kv, gw, g_x, seq, tm):
    tokens = x1.shape[0]
    n_tiles = tokens // tm
    tps = seq // tm

    def body(x1_ref, kv_ref, g_ref, gw_hbm, x2_ref, h2_ref, q_ref, o_ref, wq_v, wo_v, sem):
        @pl.when(pl.program_id(0) == 0)
        def _():
            copies = _load_weight(gw_hbm, "w_q", wq_v, sem) + _load_weight(gw_hbm, "w_o", wo_v, sem)
            for cp in copies:
                cp.start()
            for cp in copies:
                cp.wait()

        x1v = x1_ref[...]
        xh, _ = _rms_fwd(x1v)
        h2 = (xh * g_ref[...]).astype(BF16)
        h2_ref[...] = h2
        q = (_dot(h2, wq_v[...]) * (HEAD_DIM ** -0.5)).astype(BF16)
        q_ref[...] = q
        outs = []
        for h in range(HEADS):
            lo = h * HEAD_DIM
            p = _softmax_rows(_dot_nt(q[:, lo:lo + HEAD_DIM], kv_ref[:, lo:lo + HEAD_DIM]))
            outs.append(_dot(p.astype(BF16), kv_ref[:, D_MODEL + lo:D_MODEL + lo + HEAD_DIM]))
        o = jnp.concatenate(outs, axis=-1).astype(BF16)
        o_ref[...] = o
        x2_ref[...] = x1v + _dot(o, wo_v[...])

    row = lambda w: pl.BlockSpec((tm, w), lambda i: (i, 0))
    return pl.pallas_call(
        body, name="fwd_attn", grid=(n_tiles,),
        in_specs=[row(D_MODEL), pl.BlockSpec((N_MEM, 2 * D_MODEL), lambda i: (i // tps, 0)), _full((1, D_MODEL)),
                  pl.BlockSpec(memory_space=pl.ANY)],
        out_specs=[row(D_MODEL)] * 4,
        out_shape=[jax.ShapeDtypeStruct((tokens, D_MODEL), F32)] + [jax.ShapeDtypeStruct((tokens, D_MODEL), BF16)] * 3,
        scratch_shapes=[pltpu.VMEM((D_MODEL, D_MODEL), BF16), pltpu.VMEM((D_MODEL, D_MODEL), BF16), pltpu.SemaphoreType.DMA],
        compiler_params=_params(),
    )(x1, kv, g_x, gw)


def _ffn_conv(uu, halo, w_ref, b_ref, cols):
    ext = jnp.concatenate([halo, uu], axis=0)
    p1 = pltpu.roll(ext, 1, 0)[FFN_HALO:, :]
    p2 = pltpu.roll(ext, 2, 0)[FFN_HALO:, :]
    return b_ref[:, cols] + w_ref[2:3, cols] * uu + w_ref[1:2, cols] * p1 + w_ref[0:1, cols] * p2


def _fwd_ffn(x2, target, gw, g_ffn, ffn_w, ffn_b, g_final, seq, tm):
    tokens = x2.shape[0]
    n_tiles = tokens // tm
    tps = seq // tm
    n_chunks = D_FF // FFN_CHUNK

    def body(x2_ref, tgt_ref, gffn_ref, gw_hbm, fw_ref, fb_ref, gfin_ref,
             uu_ref, a_ref, h3_ref, dx3_ref, dx3b_ref, loss_ref, dgfin_ref,
             wup_v, wdown_v, carry, sem):
        i = pl.program_id(0)

        @pl.when(i == 0)
        def _():
            copies = _load_weight(gw_hbm, "w_up", wup_v, sem) + _load_weight(gw_hbm, "w_down", wdown_v, sem)
            for cp in copies:
                cp.start()
            for cp in copies:
                cp.wait()
            loss_ref[...] = jnp.zeros_like(loss_ref)
            dgfin_ref[...] = jnp.zeros_like(dgfin_ref)

        @pl.when(i % tps == 0)
        def _():
            carry[...] = jnp.zeros_like(carry)

        x2v = x2_ref[...]
        xh, _ = _rms_fwd(x2v)
        h3 = (xh * gffn_ref[...]).astype(BF16)
        h3_ref[...] = h3
        acc = jnp.zeros((tm, D_MODEL), F32)
        for jc in range(n_chunks):
            halves = []
            for half in range(2):
                cols = pl.ds(half * D_FF + jc * FFN_CHUNK, FFN_CHUNK)
                uu = _dot_nt(h3, wup_v[cols, :])
                uu_ref[:, cols] = uu
                halves.append(_ffn_conv(uu, carry[:, cols], fw_ref, fb_ref, cols))
                carry[:, cols] = uu[tm - FFN_HALO:, :]
            gate, val = halves
            a = (gate * _sigmoid(gate) * val).astype(BF16)
            a_ref[:, pl.ds(jc * FFN_CHUNK, FFN_CHUNK)] = a
            acc = acc + _dot(a, wdown_v[pl.ds(jc * FFN_CHUNK, FFN_CHUNK), :])
        x3 = x2v + acc

        xh3, r3 = _rms_fwd(x3)
        gfin = gfin_ref[...]
        err = xh3 * gfin - tgt_ref[...]
        loss_ref[...] += jnp.full(loss_ref.shape, jnp.sum(err * err) * (0.5 / D_MODEL), F32)
        dy = err * (1.0 / D_MODEL)
        dgfin_ref[...] += _colsum(dy * xh3)
        dx3 = _rms_bwd(dy, xh3, r3, gfin)
        dx3_ref[...] = dx3
        dx3b_ref[...] = dx3.astype(BF16)

    row = lambda w: pl.BlockSpec((tm, w), lambda i: (i, 0))
    return pl.pallas_call(
        body, name="fwd_ffn", grid=(n_tiles,),
        in_specs=[row(D_MODEL), row(D_MODEL), _full((1, D_MODEL)), pl.BlockSpec(memory_space=pl.ANY),
                  _full((FFN_CONV_WIDTH, 2 * D_FF)), _full((1, 2 * D_FF)), _full((1, D_MODEL))],
        out_specs=[row(2 * D_FF), row(D_FF), row(D_MODEL), row(D_MODEL), row(D_MODEL), _full((8, 128)), _full((1, D_MODEL))],
        out_shape=[jax.ShapeDtypeStruct((tokens, 2 * D_FF), F32), jax.ShapeDtypeStruct((tokens, D_FF), BF16),
                   jax.ShapeDtypeStruct((tokens, D_MODEL), BF16), jax.ShapeDtypeStruct((tokens, D_MODEL), F32),
                   jax.ShapeDtypeStruct((tokens, D_MODEL), BF16),
                   jax.ShapeDtypeStruct((8, 128), F32), jax.ShapeDtypeStruct((1, D_MODEL), F32)],
        scratch_shapes=[pltpu.VMEM((2 * D_FF, D_MODEL), BF16), pltpu.VMEM((D_FF, D_MODEL), BF16),
                        pltpu.VMEM((FFN_HALO, 2 * D_FF), F32), pltpu.SemaphoreType.DMA],
        compiler_params=_params(),
    )(x2, target, g_ffn, gw, ffn_w, ffn_b, g_final)


def _bwd_ffn(dx3, x2, uu_all, gw, g_ffn, ffn_w, ffn_b, seq, tm):
    tokens = x2.shape[0]
    n_tiles = tokens // tm
    tps = seq // tm
    n_chunks = D_FF // FFN_CHUNK
    per8 = tm // FFN_HALO

    def body(dx3_ref, x2_ref, uu_ref, prev_ref, gffn_ref, gw_hbm, fw_ref, fb_ref,
             dx2_ref, dx2b_ref, duu_ref, dfb_ref, dfw_ref, dg_ref,
             wup_v, wdown_v, carry, sem):
        i = pl.program_id(0)
        t = n_tiles - 1 - i

        @pl.when(i == 0)
        def _():
            copies = _load_weight(gw_hbm, "w_up", wup_v, sem) + _load_weight(gw_hbm, "w_down", wdown_v, sem)
            for cp in copies:
                cp.start()
            for cp in copies:
                cp.wait()
            dfb_ref[...] = jnp.zeros_like(dfb_ref)
            dfw_ref[...] = jnp.zeros_like(dfw_ref)
            dg_ref[...] = jnp.zeros_like(dg_ref)

        @pl.when(t % tps == tps - 1)
        def _():
            carry[...] = jnp.zeros_like(carry)

        starts_sequence = (t % tps == 0)
        dx3v = dx3_ref[...]
        dx3b = dx3v.astype(BF16)
        dh3 = jnp.zeros((tm, D_MODEL), F32)
        for jc in range(n_chunks):
            da = _dot_nt(dx3b, wdown_v[pl.ds(jc * FFN_CHUNK, FFN_CHUNK), :])
            uus, ccs, colss = [], [], []
            for half in range(2):
                cols = pl.ds(half * D_FF + jc * FFN_CHUNK, FFN_CHUNK)
                uu = uu_ref[:, cols]
                halo = jnp.where(starts_sequence, 0.0, prev_ref[:, cols])
                uus.append(uu)
                colss.append(cols)
                ccs.append(_ffn_conv(uu, halo, fw_ref, fb_ref, cols))
            gate, val = ccs
            sg = _sigmoid(gate)
            dgate = da * val * (sg * (1.0 + gate * (1.0 - sg)))
            dval = da * (gate * sg)
            for dcc, uu, cols in zip((dgate, dval), uus, colss):
                dfb_ref[:, cols] += _colsum(dcc)
                ext = jnp.concatenate([dcc, carry[:, cols]], axis=0)
                carry[:, cols] = dcc[:FFN_HALO, :]
                n1 = pltpu.roll(ext, tm + FFN_HALO - 1, 0)[:tm, :]
                n2 = pltpu.roll(ext, tm + FFN_HALO - 2, 0)[:tm, :]
                duu = fw_ref[2:3, cols] * dcc + fw_ref[1:2, cols] * n1 + fw_ref[0:1, cols] * n2
                dfw_ref[2:3, cols] += _colsum(uu * dcc)
                dfw_ref[1:2, cols] += _colsum(uu * n1)
                dfw_ref[0:1, cols] += _colsum(uu * n2)
                duub = duu.astype(BF16)
                duu_ref[:, cols] = duub
                dh3 = dh3 + _dot(duub, wup_v[cols, :])
        xh, r = _rms_fwd(x2_ref[...])
        dg_ref[...] += _colsum(dh3 * xh)
        dx2 = dx3v + _rms_bwd(dh3, xh, r, gffn_ref[...])
        dx2_ref[...] = dx2
        dx2b_ref[...] = dx2.astype(BF16)

    rev = lambda w: pl.BlockSpec((tm, w), lambda i: (n_tiles - 1 - i, 0))
    prev = pl.BlockSpec((FFN_HALO, 2 * D_FF), lambda i: (jnp.maximum((n_tiles - 1 - i) * per8 - 1, 0), 0))
    return pl.pallas_call(
        body, name="bwd_ffn", grid=(n_tiles,),
        in_specs=[rev(D_MODEL), rev(D_MODEL), rev(2 * D_FF), prev, _full((1, D_MODEL)), pl.BlockSpec(memory_space=pl.ANY),
                  _full((FFN_CONV_WIDTH, 2 * D_FF)), _full((1, 2 * D_FF))],
        out_specs=[rev(D_MODEL), rev(D_MODEL), rev(2 * D_FF), _full((1, 2 * D_FF)), _full((FFN_CONV_WIDTH, 2 * D_FF)),
                   _full((1, D_MODEL))],
        out_shape=[jax.ShapeDtypeStruct((tokens, D_MODEL), F32), jax.ShapeDtypeStruct((tokens, D_MODEL), BF16),
                   jax.ShapeDtypeStruct((tokens, 2 * D_FF), BF16),
                   jax.ShapeDtypeStruct((1, 2 * D_FF), F32), jax.ShapeDtypeStruct((FFN_CONV_WIDTH, 2 * D_FF), F32),
                   jax.ShapeDtypeStruct((1, D_MODEL), F32)],
        scratch_shapes=[pltpu.VMEM((2 * D_FF, D_MODEL), BF16), pltpu.VMEM((D_FF, D_MODEL), BF16),
                        pltpu.VMEM((FFN_HALO, 2 * D_FF), F32), pltpu.SemaphoreType.DMA],
        compiler_params=_params(),
    )(dx3, x2, uu_all, uu_all, g_ffn, gw, ffn_w, ffn_b)


def _bwd_attn(dx2, x1, q, kv, gw, g_x, after, seq, tm):
    tokens = x1.shape[0]
    n_tiles = tokens // tm
    tps = seq // tm
    n_b = tokens // seq

    def body(dx2_ref, x1_ref, q_ref, kv_ref, g_ref, gw_hbm, after_ref, dx1_ref, dx1b_ref, dq_ref, dkv_ref, dg_ref,
             wq_v, wo_v, sem):
        del after_ref
        i = pl.program_id(0)

        @pl.when(i == 0)
        def _():
            copies = _load_weight(gw_hbm, "w_q", wq_v, sem) + _load_weight(gw_hbm, "w_o", wo_v, sem)
            for cp in copies:
                cp.start()
            for cp in copies:
                cp.wait()
            dg_ref[...] = jnp.zeros_like(dg_ref)

        @pl.when(i % tps == 0)
        def _():
            dkv_ref[...] = jnp.zeros_like(dkv_ref)

        dx2v = dx2_ref[...]
        do = _dot_nt(dx2v.astype(BF16), wo_v[...]).astype(BF16)
        q = q_ref[...]
        dqs = []
        for h in range(HEADS):
            lo = h * HEAD_DIM
            kcols, vcols = pl.ds(lo, HEAD_DIM), pl.ds(D_MODEL + lo, HEAD_DIM)
            qh, doh = q[:, lo:lo + HEAD_DIM], do[:, lo:lo + HEAD_DIM]
            p = _softmax_rows(_dot_nt(qh, kv_ref[:, kcols]))
            dp = _dot_nt(doh, kv_ref[:, vcols])
            dkv_ref[:, vcols] += _dot_tn(p.astype(BF16), doh)
            ds = (p * (dp - jnp.sum(dp * p, axis=-1, keepdims=True))).astype(BF16)
            dqs.append(_dot(ds, kv_ref[:, kcols]) * (HEAD_DIM ** -0.5))
            dkv_ref[:, kcols] += _dot_tn(ds, qh)
        dq = jnp.concatenate(dqs, axis=-1).astype(BF16)
        dq_ref[...] = dq
        dh2 = _dot_nt(dq, wq_v[...])
        xh, r = _rms_fwd(x1_ref[...])
        dg_ref[...] += _colsum(dh2 * xh)
        dx1 = dx2v + _rms_bwd(dh2, xh, r, g_ref[...])
        dx1_ref[...] = dx1
        dx1b_ref[...] = dx1.astype(BF16)

    row = lambda w: pl.BlockSpec((tm, w), lambda i: (i, 0))
    per_b = pl.BlockSpec((N_MEM, 2 * D_MODEL), lambda i: (i // tps, 0))
    return pl.pallas_call(
        body, name="bwd_attn", grid=(n_tiles,),
        in_specs=[row(D_MODEL), row(D_MODEL), row(D_MODEL), per_b, _full((1, D_MODEL)), pl.BlockSpec(memory_space=pl.ANY),
                  _full(after.shape)],
        out_specs=[row(D_MODEL), row(D_MODEL), row(D_MODEL), per_b, _full((1, D_MODEL))],
        out_shape=[jax.ShapeDtypeStruct((tokens, D_MODEL), F32), jax.ShapeDtypeStruct((tokens, D_MODEL), BF16),
                   jax.ShapeDtypeStruct((tokens, D_MODEL), BF16),
                   jax.ShapeDtypeStruct((n_b * N_MEM, 2 * D_MODEL), F32), jax.ShapeDtypeStruct((1, D_MODEL), F32)],
        scratch_shapes=[pltpu.VMEM((D_MODEL, D_MODEL), BF16), pltpu.VMEM((D_MODEL, D_MODEL), BF16), pltpu.SemaphoreType.DMA],
        compiler_params=_params(),
    )(dx2, x1, q, kv, g_x, gw, after)


def _bwd_kv(dkv, mem2d, gw, g_mem):
    rows = mem2d.shape[0]
    n_b = rows // N_MEM

    def body(dkv_ref, mem_ref, gw_hbm, dkvb_ref, dg_ref, wkv_v, sem):
        @pl.when(pl.program_id(0) == 0)
        def _():
            copies = _load_weight(gw_hbm, "w_kv", wkv_v, sem)
            for cp in copies:
                cp.start()
            for cp in copies:
                cp.wait()
            dg_ref[...] = jnp.zeros_like(dg_ref)

        dkvb = dkv_ref[...].astype(BF16)
        dkvb_ref[...] = dkvb
        dmn = _dot(dkvb, wkv_v[...])
        mh, _ = _rms_fwd(mem_ref[...])
        dg_ref[...] += _colsum(dmn * mh)

    del g_mem
    return pl.pallas_call(
        body, name="bwd_kv", grid=(n_b,),
        in_specs=[pl.BlockSpec((N_MEM, 2 * D_MODEL), lambda b: (b, 0)), pl.BlockSpec((N_MEM, D_MODEL), lambda b: (b, 0)),
                  pl.BlockSpec(memory_space=pl.ANY)],
        out_specs=[pl.BlockSpec((N_MEM, 2 * D_MODEL), lambda b: (b, 0)), _full((1, D_MODEL))],
        out_shape=[jax.ShapeDtypeStruct((rows, 2 * D_MODEL), BF16), jax.ShapeDtypeStruct((1, D_MODEL), F32)],
        scratch_shapes=[pltpu.VMEM((2 * D_MODEL, D_MODEL), BF16), pltpu.SemaphoreType.DMA],
        compiler_params=_params(),
    )(dkv, mem2d, gw)


def _bwd_mix(dx1, x2d, u_all, c_all, pooled_all, gw, g_mix, conv_w, ln_g, ln_b, pool_w, pool_scale, after, seq, tm):
    tokens = x2d.shape[0]
    n_tiles = tokens // tm
    tps = seq // tm

    def body(dx1_ref, x_ref, u_ref, c_ref, pooled_ref, gmix_ref, gw_hbm, cw_ref, lng_ref, lnb_ref, pw_ref, ps_ref,
             after_ref, dx_ref, du_ref, dgmix_ref, dcw_ref, dcb_ref, dlng_ref, dlnb_ref, dpw_ref, dps_ref,
             win_v, wout_v, dc_carry, e_carry, sem):
        del after_ref
        i = pl.program_id(0)
        t = n_tiles - 1 - i

        @pl.when(i == 0)
        def _():
            copies = _load_weight(gw_hbm, "w_in", win_v, sem) + _load_weight(gw_hbm, "w_out", wout_v, sem)
            for cp in copies:
                cp.start()
            for cp in copies:
                cp.wait()
            for ref in (dgmix_ref, dcw_ref, dcb_ref, dlng_ref, dlnb_ref, dpw_ref, dps_ref):
                ref[...] = jnp.zeros_like(ref)

        @pl.when(t % tps == tps - 1)
        def _():
            dc_carry[...] = jnp.zeros_like(dc_carry)
            e_carry[...] = jnp.zeros_like(e_carry)

        dx1v = dx1_ref[...]
        dymix = _dot_nt(dx1v.astype(BF16), wout_v[...])
        dyc, dyp = dymix[:, :D_CONV], dymix[:, D_CONV:]
        u = u_ref[...]
        val, gate = u[:, :D_CONV], u[:, D_CONV:2 * D_CONV]

        conv = c_ref[...]
        mu = jnp.mean(conv, axis=-1, keepdims=True)
        cen = conv - mu
        rs = lax.rsqrt(jnp.mean(cen * cen, axis=-1, keepdims=True) + EPS)
        chat = cen * rs
        ln = chat * lng_ref[...] + lnb_ref[...]
        sl = _sigmoid(ln)
        dln = dyc * (sl * (1.0 + ln * (1.0 - sl)))
        dlng_ref[...] += _colsum(dln * chat)
        dlnb_ref[...] += _colsum(dln)
        dchat = dln * lng_ref[...]
        dc = rs * (dchat - jnp.mean(dchat, axis=-1, keepdims=True)
                   - chat * jnp.mean(dchat * chat, axis=-1, keepdims=True))
        dcb_ref[...] += _colsum(dc)
        sg = _sigmoid(gate)
        hc = val * sg
        ext = jnp.concatenate([dc, dc_carry[...]], axis=0)
        dc_carry[...] = dc[:CONV_HALO, :]
        dhc = jnp.zeros((tm, D_CONV), F32)
        for k in range(CONV_WIDTH):
            ahead = CONV_WIDTH - 1 - k
            tap = (ext if ahead == 0 else pltpu.roll(ext, tm + CONV_HALO - ahead, 0))[:tm, :]
            dhc = dhc + cw_ref[k:k + 1, :] * tap
            dcw_ref[k:k + 1, :] += _colsum(hc * tap)
        du_ref[:, :D_CONV] = (dhc * sg).astype(BF16)
        du_ref[:, D_CONV:2 * D_CONV] = (dhc * val * (sg * (1.0 - sg))).astype(BF16)

        pos = lax.broadcasted_iota(jnp.int32, (tm, 1), 0) + (t % tps) * tm
        es, dpooled = [], []
        for g, w in enumerate(POOL_WINDOWS):
            cols = pl.ds(g * POOL_GROUP_DIM, POOL_GROUP_DIM)
            lo = g * POOL_GROUP_DIM
            pooled = pooled_ref[:, cols]
            pw = pw_ref[g].astype(BF16)
            dyg = dyp[:, lo:lo + POOL_GROUP_DIM]
            dps_ref[:, cols] += _colsum(dyg * _dot(pooled, pw))
            dmixed = (dyg * ps_ref[:, cols]).astype(BF16)
            dpw_ref[g] += _dot_tn(pooled, dmixed)
            dpo = _dot_nt(dmixed, pw)
            dpooled.append(dpo)
            es.append(dpo / jnp.minimum(pos + 1, w).astype(F32))
        e = jnp.concatenate(es, axis=-1)
        run = jnp.concatenate([e, e_carry[...]], axis=0)
        e_carry[...] = e[:POOL_HALO, :]
        rows = tm + POOL_HALO
        for g, w in enumerate(POOL_WINDOWS):
            lo = g * POOL_GROUP_DIM
            run = run[:, POOL_GROUP_DIM if g else 0:]
            run = run + pltpu.roll(run, rows - w // 2, 0)
            du_ref[:, 2 * D_CONV + lo:2 * D_CONV + lo + POOL_GROUP_DIM] = (
                run[:tm, :POOL_GROUP_DIM] - dpooled[g]).astype(BF16)

        dh1 = _dot(du_ref[...], win_v[...])
        xh, r = _rms_fwd(x_ref[...])
        dgmix_ref[...] += _colsum(dh1 * xh)
        dx_ref[...] = dx1v + _rms_bwd(dh1, xh, r, gmix_ref[...])

    rev = lambda w: pl.BlockSpec((tm, w), lambda i: (n_tiles - 1 - i, 0))
    return pl.pallas_call(
        body, name="bwd_mix", grid=(n_tiles,),
        in_specs=[rev(D_MODEL), rev(D_MODEL), rev(D_IN), rev(D_CONV), rev(D_POOL), _full((1, D_MODEL)),
                  pl.BlockSpec(memory_space=pl.ANY), _full((CONV_WIDTH, D_CONV)), _full((1, D_CONV)), _full((1, D_CONV)),
                  _full((4, POOL_GROUP_DIM, POOL_GROUP_DIM)), _full((1, D_POOL)), _full(after.shape)],
        out_specs=[rev(D_MODEL), rev(D_IN), _full((1, D_MODEL)), _full((CONV_WIDTH, D_CONV)), _full((1, D_CONV)),
                   _full((1, D_CONV)), _full((1, D_CONV)), _full((4, POOL_GROUP_DIM, POOL_GROUP_DIM)), _full((1, D_POOL))],
        out_shape=[jax.ShapeDtypeStruct((tokens, D_MODEL), F32), jax.ShapeDtypeStruct((tokens, D_IN), BF16),
                   jax.ShapeDtypeStruct((1, D_MODEL), F32), jax.ShapeDtypeStruct((CONV_WIDTH, D_CONV), F32),
                   jax.ShapeDtypeStruct((1, D_CONV), F32), jax.ShapeDtypeStruct((1, D_CONV), F32),
                   jax.ShapeDtypeStruct((1, D_CONV), F32),
                   jax.ShapeDtypeStruct((4, POOL_GROUP_DIM, POOL_GROUP_DIM), F32), jax.ShapeDtypeStruct((1, D_POOL), F32)],
        scratch_shapes=[pltpu.VMEM((D_IN, D_MODEL), BF16), pltpu.VMEM((D_MODEL, D_MODEL), BF16),
                        pltpu.VMEM((CONV_HALO, D_CONV), F32), pltpu.VMEM((POOL_HALO, D_POOL), F32),
                        pltpu.SemaphoreType.DMA],
        compiler_params=_params(),
    )(dx1, x2d, u_all, c_all, pooled_all, g_mix, gw, conv_w, ln_g, ln_b, pool_w, pool_scale, after)


def _wgrad(a, b, name, tm=256):
    tokens, m = a.shape
    n = b.shape[1]

    def body(a_ref, b_ref, out_ref):
        out_ref[...] = _dot_tn(a_ref[...], b_ref[...]).astype(out_ref.dtype)

    return pl.pallas_call(
        body, name=name, grid=(m // tm,),
        in_specs=[pl.BlockSpec((tokens, tm), lambda i: (0, i)), _full((tokens, n))],
        out_specs=pl.BlockSpec((tm, n), lambda i: (i, 0)),
        out_shape=jax.ShapeDtypeStruct((m, n), BF16),
        compiler_params=_params(),
    )(a, b)


def _adamw_update(w, g, m, v):
    nm = ADAM_B1 * m + (1.0 - ADAM_B1) * g
    nv = ADAM_B2 * v + (1.0 - ADAM_B2) * (g * g)
    m_hat = nm / (1.0 - ADAM_B1 ** ADAM_STEP)
    v_hat = nv / (1.0 - ADAM_B2 ** ADAM_STEP)
    return -ADAM_LR * (m_hat / (jnp.sqrt(v_hat) + ADAM_EPS) + ADAM_WD * w), nm, nv


def _adamw_small(ws, gs, ms, vs):
    n = len(ws)

    def body(*refs):
        ins, outs = refs[:4 * n], refs[4 * n:]
        for k in range(n):
            d, nm, nv = _adamw_update(*[ins[j * n + k][...] for j in range(4)])
            outs[k][...] = d
            outs[n + k][...] = nm
            outs[2 * n + k][...] = nv

    vmem = pl.BlockSpec(memory_space=pltpu.VMEM)
    outs = pl.pallas_call(
        body, name="adamw_small",
        in_specs=[vmem] * (4 * n), out_specs=[vmem] * (3 * n),
        out_shape=[jax.ShapeDtypeStruct(w.shape, F32) for w in ws] * 3,
    )(*ws, *gs, *ms, *vs)
    return outs[:n], outs[n:2 * n], outs[2 * n:]


def _adamw(w, g, m, v, name):
    rows, cols = w.shape
    tile = rows
    for cand in (512, 256, 128, 64, 32, 16, 8):
        if rows % cand == 0:
            tile = cand
            break

    def body(w_ref, g_ref, m_ref, v_ref, d_ref, nm_ref, nv_ref):
        d_ref[...], nm_ref[...], nv_ref[...] = _adamw_update(w_ref[...], g_ref[...], m_ref[...], v_ref[...])

    spec = pl.BlockSpec((tile, cols), lambda i: (i, 0))
    return pl.pallas_call(
        body, name=name, grid=(rows // tile,),
        in_specs=[spec] * 4, out_specs=[spec] * 3,
        out_shape=[jax.ShapeDtypeStruct((rows, cols), F32)] * 3,
        compiler_params=_params(("arbitrary",)),
    )(w, g, m, v)


SMALL = (("norm_mix_g", (1, 1024)), ("conv_dw_b", (1, 512)), ("conv_ln_g", (1, 512)), ("conv_ln_b", (1, 512)),
         ("pool_w", (1, 4, 128, 128)), ("pool_scale", (1, 512)), ("norm_xattn_g", (1, 1024)), ("norm_mem_g", (1, 1024)),
         ("norm_ffn_g", (1, 1024)), ("ffn_dw_b", (1, 5632)), ("norm_final_g", (1024,)))
LANES = 128


def _pack_rows(arrs):
    flat = jnp.concatenate([a.reshape(-1) for a in arrs])
    pad = (-flat.shape[0]) % (8 * LANES)
    return jnp.pad(flat, (0, pad)).reshape(-1, LANES)


def kernel(x, mem, norm_mix_g, w_in, conv_dw_w, conv_dw_b, conv_ln_g, conv_ln_b, pool_w, pool_scale, w_out, norm_xattn_g, norm_mem_g, w_q, w_kv, w_o, norm_ffn_g, w_up, ffn_dw_w, ffn_dw_b, w_down, norm_final_g, loss_target, m_norm_mix_g, m_w_in, m_conv_dw_w, m_conv_dw_b, m_conv_ln_g, m_conv_ln_b, m_pool_w, m_pool_scale, m_w_out, m_norm_xattn_g, m_norm_mem_g, m_w_q, m_w_kv, m_w_o, m_norm_ffn_g, m_w_up, m_ffn_dw_w, m_ffn_dw_b, m_w_down, m_norm_final_g, v_norm_mix_g, v_w_in, v_conv_dw_w, v_conv_dw_b, v_conv_ln_g, v_conv_ln_b, v_pool_w, v_pool_scale, v_w_out, v_norm_xattn_g, v_norm_mem_g, v_w_q, v_w_kv, v_w_o, v_norm_ffn_g, v_w_up, v_ffn_dw_w, v_ffn_dw_b, v_w_down, v_norm_final_g):
    weights = dict(norm_mix_g=norm_mix_g, w_in=w_in, conv_dw_w=conv_dw_w, conv_dw_b=conv_dw_b, conv_ln_g=conv_ln_g,
                   conv_ln_b=conv_ln_b, pool_w=pool_w, pool_scale=pool_scale, w_out=w_out, norm_xattn_g=norm_xattn_g,
                   norm_mem_g=norm_mem_g, w_q=w_q, w_kv=w_kv, w_o=w_o, norm_ffn_g=norm_ffn_g, w_up=w_up,
                   ffn_dw_w=ffn_dw_w, ffn_dw_b=ffn_dw_b, w_down=w_down, norm_final_g=norm_final_g)
    moments_m = dict(norm_mix_g=m_norm_mix_g, w_in=m_w_in, conv_dw_w=m_conv_dw_w, conv_dw_b=m_conv_dw_b,
                     conv_ln_g=m_conv_ln_g, conv_ln_b=m_conv_ln_b, pool_w=m_pool_w, pool_scale=m_pool_scale,
                     w_out=m_w_out, norm_xattn_g=m_norm_xattn_g, norm_mem_g=m_norm_mem_g, w_q=m_w_q, w_kv=m_w_kv,
                     w_o=m_w_o, norm_ffn_g=m_norm_ffn_g, w_up=m_w_up, ffn_dw_w=m_ffn_dw_w, ffn_dw_b=m_ffn_dw_b,
                     w_down=m_w_down, norm_final_g=m_norm_final_g)
    moments_v = dict(norm_mix_g=v_norm_mix_g, w_in=v_w_in, conv_dw_w=v_conv_dw_w, conv_dw_b=v_conv_dw_b,
                     conv_ln_g=v_conv_ln_g, conv_ln_b=v_conv_ln_b, pool_w=v_pool_w, pool_scale=v_pool_scale,
                     w_out=v_w_out, norm_xattn_g=v_norm_xattn_g, norm_mem_g=v_norm_mem_g, w_q=v_w_q, w_kv=v_w_kv,
                     w_o=v_w_o, norm_ffn_g=v_norm_ffn_g, w_up=v_w_up, ffn_dw_w=v_ffn_dw_w, ffn_dw_b=v_ffn_dw_b,
                     w_down=v_w_down, norm_final_g=v_norm_final_g)
    order = list(weights)
    transposed = ("w_in", "w_kv", "w_up")

    n_b, seq, _ = x.shape
    tokens = n_b * seq
    tm_mix = min(512, seq // 2)
    tm_ffn = min(256, seq // 2)
    dev = 4 * lax.axis_index("x") + 2 * lax.axis_index("y") + lax.axis_index("c")

    shards = [weights[n][0].T if n in transposed else weights[n][0] for n, _ in W_ROWS]
    packed = jnp.concatenate(shards, axis=0).astype(BF16)
    small_sharded = _pack_rows([conv_dw_w[0], ffn_dw_w[0]])
    gw, gsmall = _all_gather([packed, small_sharded], "weights_all_gather")
    gflat = gsmall.reshape(N_DEV, -1)
    n_cw = CONV_WIDTH * (D_CONV // N_DEV)
    n_fw = FFN_CONV_WIDTH * (2 * D_FF // N_DEV)
    conv_w = gflat[:, :n_cw].reshape(N_DEV, CONV_WIDTH, D_CONV // N_DEV).transpose(1, 0, 2).reshape(CONV_WIDTH, D_CONV)
    ffn_w = gflat[:, n_cw:n_cw + n_fw].reshape(N_DEV, FFN_CONV_WIDTH, 2 * D_FF // N_DEV).transpose(1, 0, 2).reshape(
        FFN_CONV_WIDTH, 2 * D_FF)

    x2d = x.reshape(tokens, D_MODEL)
    mem2d = mem.reshape(n_b * N_MEM, D_MODEL)
    tgt2d = loss_target.reshape(tokens, D_MODEL)
    g_final = norm_final_g.reshape(1, D_MODEL)

    x1, u_all, c_all, pooled_all, ymix, h1 = _fwd_mix(
        x2d, gw, norm_mix_g, conv_w, conv_dw_b, conv_ln_g, conv_ln_b, pool_w[0], pool_scale, seq, tm_mix)
    mem_n, kv = _fwd_kv(mem2d, gw, norm_mem_g)
    x2, h2, q, o = _fwd_attn(x1, kv, gw, norm_xattn_g, seq, tm_mix)
    uu_all, a_all, h3, dx3, dx3b, loss_part, dg_final = _fwd_ffn(
        x2, tgt2d, gw, norm_ffn_g, ffn_w, ffn_dw_b, g_final, seq, tm_ffn)

    table = _owner_table()

    def reduce_start(names, tag):
        parts = [part[n].reshape(N_DEV, W_OFF[n][1], D_MODEL) for n in names]
        landed = _exchange_sibling(parts, "rs_sibling_exchange_" + tag)
        sums = [_chip_partial_sums(table, p, l, "rs_chip_partial_sums_" + n) for n, p, l in zip(names, parts, landed)]
        return parts, landed, _chip_exchange_start(sums, "rs_chip_exchange_start_" + tag)

    def reduce_finish(names, parts, landed, flight, after, tag):
        from_chips = _chip_exchange_wait(*flight[:4], after, "rs_chip_exchange_wait_" + tag)
        for n, p, l, f in zip(names, parts, landed, from_chips):
            g_mine[n] = _final_grad_sums(table, p, l, f, "rs_final_sums_" + n)
        return g_mine[names[-1]]

    part, g_mine = {}, {}
    dx2, dx2b, duu, d_ffn_b, d_ffn_w, dg_ffn = _bwd_ffn(dx3, x2, uu_all, gw, norm_ffn_g, ffn_w, ffn_dw_b, seq, tm_ffn)
    part["w_up"] = _wgrad(duu, h3, "wgrad_w_up")
    part["w_down"] = _wgrad(a_all, dx3b, "wgrad_w_down")
    parts_a, landed_a, flight_a = reduce_start(RS_GROUPS["a"], "a")
    dx1, dx1b, dq, dkv, dg_x = _bwd_attn(dx2, x1, q, kv, gw, norm_xattn_g, flight_a[4], seq, tm_mix)
    dkv_b, dg_mem = _bwd_kv(dkv, mem2d, gw, norm_mem_g)
    part["w_q"] = _wgrad(h2, dq, "wgrad_w_q")
    part["w_kv"] = _wgrad(dkv_b, mem_n, "wgrad_w_kv")
    part["w_o"] = _wgrad(o, dx2b, "wgrad_w_o")
    parts_b, landed_b, flight_b = reduce_start(RS_GROUPS["b"], "b")
    dx, du, dg_mix, d_conv_w, d_conv_b, d_ln_g, d_ln_b, d_pool_w, d_pool_scale = _bwd_mix(
        dx1, x2d, u_all, c_all, pooled_all, gw, norm_mix_g, conv_w, conv_ln_g, conv_ln_b, pool_w[0], pool_scale,
        flight_b[4], seq, tm_mix)
    grad_x = dx.reshape(x.shape)
    part["w_in"] = _wgrad(du, h1, "wgrad_w_in")
    part["w_out"] = _wgrad(ymix, dx1b, "wgrad_w_out")
    parts_c, landed_c, flight_c = reduce_start(RS_GROUPS["c"], "c")
    done_a = reduce_finish(RS_GROUPS["a"], parts_a, landed_a, flight_a, flight_c[4], "a")
    done_b = reduce_finish(RS_GROUPS["b"], parts_b, landed_b, flight_b, done_a, "b")
    reduce_finish(RS_GROUPS["c"], parts_c, landed_c, flight_c, done_b, "c")

    small_grads = dict(norm_mix_g=dg_mix, conv_dw_b=d_conv_b, conv_ln_g=d_ln_g, conv_ln_b=d_ln_b, pool_w=d_pool_w,
                       pool_scale=d_pool_scale, norm_xattn_g=dg_x, norm_mem_g=dg_mem, norm_ffn_g=dg_ffn,
                       ffn_dw_b=d_ffn_b, norm_final_g=dg_final)
    small_list = [small_grads[n] for n, _ in SMALL] + [d_conv_w, d_ffn_w, loss_part[:1]]
    (small_all,) = _all_gather([_pack_rows(small_list)], "small_grads_all_gather")
    small_sum = _sum_blocks(small_all).reshape(-1)

    grads = {}
    pos = 0
    for n, shape in SMALL:
        size = 1
        for s in shape:
            size *= s
        grads[n] = small_sum[pos:pos + size].reshape(shape)
        pos += size
    full_conv_w = small_sum[pos:pos + CONV_WIDTH * D_CONV].reshape(CONV_WIDTH, D_CONV)
    pos += CONV_WIDTH * D_CONV
    full_ffn_w = small_sum[pos:pos + FFN_CONV_WIDTH * 2 * D_FF].reshape(FFN_CONV_WIDTH, 2 * D_FF)
    loss = small_sum[pos + FFN_CONV_WIDTH * 2 * D_FF]
    grads["conv_dw_w"] = lax.dynamic_slice_in_dim(full_conv_w, dev * (D_CONV // N_DEV), D_CONV // N_DEV, axis=1)[None]
    grads["ffn_dw_w"] = lax.dynamic_slice_in_dim(full_ffn_w, dev * (2 * D_FF // N_DEV), 2 * D_FF // N_DEV, axis=1)[None]
    for n in W_OFF:
        grads[n] = (g_mine[n].T if n in transposed else g_mine[n])[None]

    delta, new_m, new_v = {}, {}, {}
    for n, _ in W_ROWS:
        shape = weights[n].shape
        as2d = lambda t: t.reshape(shape[1], shape[2])
        d, nm, nv = _adamw(as2d(weights[n]), as2d(grads[n]), as2d(moments_m[n]), as2d(moments_v[n]), "adamw_" + n)
        delta[n], new_m[n], new_v[n] = d.reshape(shape), nm.reshape(shape), nv.reshape(shape)
    small_names = [n for n in order if n not in W_OFF]
    two_d = lambda t: t.reshape(1, -1) if t.ndim == 1 else t
    outs = _adamw_small(*[[two_d(t[n]) for n in small_names] for t in (weights, grads, moments_m, moments_v)])
    for res, out in zip((delta, new_m, new_v), outs):
        for n, o in zip(small_names, out):
            res[n] = o.reshape(weights[n].shape)

    return (loss, grad_x, *[grads[n] for n in order], *[delta[n] for n in order],
            *[new_m[n] for n in order], *[new_v[n] for n in order])
```

```python
import functools

import jax
import jax.numpy as jnp
from jax import lax
from jax.experimental import pallas as pl
from jax.experimental.pallas import tpu as pltpu

F32 = jnp.float32
BF16 = jnp.bfloat16
MESH = pl.DeviceIdType.MESH

N_DEV = 8
D_MODEL = 1024
D_CONV = 512
D_POOL = 512
CONV_WIDTH = 31
POOL_WINDOWS = (2, 4, 8, 16)
POOL_GROUP_DIM = 128
D_IN = 1536
N_MEM = 256
HEADS = 4
HEAD_DIM = 256
D_FF = 2816
FFN_CONV_WIDTH = 3
EPS = 1e-6
ADAM_LR = 0.001
ADAM_B1 = 0.9
ADAM_B2 = 0.999
ADAM_EPS = 1e-08
ADAM_WD = 0.01
ADAM_STEP = 10

VMEM_LIMIT_V7X = 56 * 1024 * 1024
CONV_HALO = 32
POOL_HALO = 16
FFN_HALO = 8
FFN_CHUNK = 1408

W_ROWS = (("w_in", 192), ("w_out", 128), ("w_q", 128), ("w_kv", 256), ("w_o", 128), ("w_up", 704), ("w_down", 352))
AG_GROUPS = (("w_in", "w_out"), ("w_q", "w_kv", "w_o"), ("w_up", "w_down"))
W_OFF = {}
for _names in AG_GROUPS:
    _o = 0
    for _n in _names:
        W_OFF[_n] = (_o, dict(W_ROWS)[_n])
        _o += dict(W_ROWS)[_n]
RS_GROUPS = {"a": ("w_up", "w_down"), "b": ("w_q", "w_kv", "w_o"), "c": ("w_in", "w_out")}


def _dot(a, b):
    return jnp.dot(a, b, preferred_element_type=F32)


def _dot_nt(a, b):
    return lax.dot_general(a, b, (((1,), (1,)), ((), ())), preferred_element_type=F32)


def _dot_tn(a, b):
    return lax.dot_general(a, b, (((0,), (0,)), ((), ())), preferred_element_type=F32)


def _sigmoid(v):
    return 1.0 / (1.0 + jnp.exp(-v))


def _rms_fwd(v):
    r = lax.rsqrt(jnp.mean(v * v, axis=-1, keepdims=True) + EPS)
    return v * r, r


def _rms_bwd(dh, vh, r, g):
    gd = dh * g
    return r * (gd - vh * jnp.mean(gd * vh, axis=-1, keepdims=True))


def _colsum(v):
    return jnp.sum(v, axis=0, keepdims=True)


def _full(shape):
    return pl.BlockSpec(shape, lambda *_: (0,) * len(shape))


def _params(sem=("arbitrary",), vmem=VMEM_LIMIT_V7X):
    return pltpu.CompilerParams(dimension_semantics=sem, vmem_limit_bytes=vmem)


def _load_weight(g_hbm, name, dst, sem):
    off, rows = W_OFF[name]
    return [pltpu.make_async_copy(g_hbm.at[d, pl.ds(off, rows), :], dst.at[pl.ds(d * rows, rows), :], sem)
            for d in range(N_DEV)]


def _position():
    x, y, c = lax.axis_index("x"), lax.axis_index("y"), lax.axis_index("c")
    chips = [(1 - x, y), (x, 1 - y), (1 - x, 1 - y)]
    return x, y, c, chips


def _dev(px, py, pc):
    return 4 * px + 2 * py + pc


def _all_gather(arrs, name):
    n = len(arrs)

    def body(*refs):
        ins, outs = refs[:n], refs[n:2 * n]
        send_sems, recv_sems, local_sems = refs[2 * n:2 * n + 3]
        bounce = refs[2 * n + 3:]
        x, y, c, chips = _position()
        me, sibling = (x, y, c), (x, y, 1 - c)

        def copy(a, k, block, to, src=None):
            rows = outs[a].at[_dev(*block)]
            return pltpu.make_async_remote_copy(
                src_ref=rows if src is None else src, dst_ref=rows,
                send_sem=send_sems.at[a, k], recv_sem=recv_sems.at[a, k], device_id=to, device_id_type=MESH)

        sends = []
        for a in range(n):
            first = [copy(a, 0, me, sibling, src=ins[a])]
            first += [copy(a, 1 + j, me, (*chip, c), src=ins[a]) for j, chip in enumerate(chips)]
            for cp in first:
                cp.start()
            sends += first
        started = []
        for a in range(n):
            load = pltpu.make_async_copy(ins[a], bounce[a], local_sems.at[a, 0])
            load.start()
            load.wait()
            mine = pltpu.make_async_copy(bounce[a], outs[a].at[_dev(*me)], local_sems.at[a, 1])
            mine.start()
            started.append(mine)
        for j, chip in enumerate(chips):
            for a in range(n):
                copy(a, 1 + j, (*chip, c), me).wait_recv()
                passed = copy(a, 4 + j, (*chip, c), sibling)
                passed.start()
                sends.append(passed)
        for a in range(n):
            copy(a, 0, sibling, me).wait_recv()
            for j, chip in enumerate(chips):
                copy(a, 4 + j, (*chip, 1 - c), me).wait_recv()
        for cp in sends:
            cp.wait_send()
        for mine in started:
            mine.wait()

    any_spec = pl.BlockSpec(memory_space=pl.ANY)
    return pl.pallas_call(
        body, name=name,
        out_shape=[jax.ShapeDtypeStruct((N_DEV,) + a.shape, a.dtype) for a in arrs],
        in_specs=[any_spec] * n, out_specs=[any_spec] * n,
        scratch_shapes=[pltpu.SemaphoreType.DMA((n, 7)), pltpu.SemaphoreType.DMA((n, 7)), pltpu.SemaphoreType.DMA((n, 2))]
        + [pltpu.VMEM(a.shape, a.dtype) for a in arrs],
    )(*arrs)


_HBM = pl.BlockSpec(memory_space=pltpu.HBM)
_SEM = pl.BlockSpec(memory_space=pltpu.SEMAPHORE)
_SIDE_EFFECT = pltpu.SideEffectType.DATAFLOW_SIDE_EFFECTING


def _gather_start(buf, after, name):
    def body(buf_ref, after_ref, send_sems, recv_sems, buf_thru, token):
        del after_ref, buf_thru
        x, y, c, chips = _position()
        rows = buf_ref.at[_dev(x, y, c)]
        for k, to in enumerate([(x, y, 1 - c)] + [(*chip, c) for chip in chips]):
            pltpu.make_async_remote_copy(src_ref=rows, dst_ref=rows, send_sem=send_sems.at[k], recv_sem=recv_sems.at[k],
                                         device_id=to, device_id_type=MESH).start()
        token[...] = jnp.zeros_like(token)

    return pl.pallas_call(
        body, name=name,
        out_shape=(pltpu.SemaphoreType.DMA((4,)), pltpu.SemaphoreType.DMA((4,)), pltpu.HBM(buf.shape, buf.dtype),
                   jax.ShapeDtypeStruct((8, 128), F32)),
        in_specs=(_HBM, pl.BlockSpec(memory_space=pl.ANY)),
        out_specs=(_SEM, _SEM, _HBM, pl.BlockSpec(memory_space=pltpu.VMEM)),
        input_output_aliases={0: 2},
        compiler_params=pltpu.CompilerParams(has_side_effects=_SIDE_EFFECT),
    )(pltpu.with_memory_space_constraint(buf, pltpu.HBM), after)


def _gather_forward(send_sems, recv_sems, buf, after, name):
    def body(buf_ref, send_sems, recv_sems, after_ref, fwd_send, fwd_recv, buf_thru):
        del after_ref, buf_thru
        x, y, c, chips = _position()
        sibling = (x, y, 1 - c)

        def copy(block, k, sends, recvs):
            rows = buf_ref.at[_dev(*block)]
            return pltpu.make_async_remote_copy(src_ref=rows, dst_ref=rows, send_sem=sends.at[k], recv_sem=recvs.at[k],
                                                device_id=sibling, device_id_type=MESH)

        for k in range(4):
            copy((x, y, c), k, send_sems, recv_sems).wait_send()
        copy(sibling, 0, send_sems, recv_sems).wait_recv()
        for j, chip in enumerate(chips):
            copy((*chip, c), 1 + j, send_sems, recv_sems).wait_recv()
            copy((*chip, c), j, fwd_send, fwd_recv).start()

    return pl.pallas_call(
        body, name=name,
        out_shape=(pltpu.SemaphoreType.DMA((3,)), pltpu.SemaphoreType.DMA((3,)), pltpu.HBM(buf.shape, buf.dtype)),
        in_specs=(_HBM, _SEM, _SEM, pl.BlockSpec(memory_space=pl.ANY)), out_specs=(_SEM, _SEM, _HBM),
        input_output_aliases={0: 2},
        compiler_params=pltpu.CompilerParams(has_side_effects=_SIDE_EFFECT),
    )(buf, send_sems, recv_sems, after)


def _gather_finish(fwd_send, fwd_recv, buf, name):
    def body(buf_ref, fwd_send, fwd_recv, buf_thru):
        del buf_thru
        x, y, c, chips = _position()
        for j, chip in enumerate(chips):
            cp = pltpu.make_async_remote_copy(
                src_ref=buf_ref.at[_dev(*chip, c)], dst_ref=buf_ref.at[_dev(*chip, 1 - c)], send_sem=fwd_send.at[j],
                recv_sem=fwd_recv.at[j], device_id=(x, y, 1 - c), device_id_type=MESH)
            cp.wait_send()
            cp.wait_recv()

    return pl.pallas_call(
        body, name=name,
        out_shape=pltpu.HBM(buf.shape, buf.dtype),
        in_specs=(_HBM, _SEM, _SEM), out_specs=_HBM,
        input_output_aliases={0: 0},
        compiler_params=pltpu.CompilerParams(has_side_effects=_SIDE_EFFECT),
    )(buf, fwd_send, fwd_recv)


def _exchange_sibling(parts, name):
    n = len(parts)

    def body(*refs):
        ins, outs = refs[:n], refs[n:2 * n]
        send_sems, recv_sems = refs[2 * n:]
        x, y, c, chips = _position()
        sibling = (x, y, 1 - c)
        copies = []
        for k in range(n):
            for j, chip in enumerate([(x, y)] + chips):
                cp = pltpu.make_async_remote_copy(
                    src_ref=ins[k].at[_dev(*chip, 1 - c)], dst_ref=outs[k].at[j],
                    send_sem=send_sems.at[k, j], recv_sem=recv_sems.at[k, j], device_id=sibling, device_id_type=MESH)
                cp.start()
                copies.append(cp)
        for cp in copies:
            cp.wait_recv()
        for cp in copies:
            cp.wait_send()

    any_spec = pl.BlockSpec(memory_space=pl.ANY)
    return pl.pallas_call(
        body, name=name,
        out_shape=[jax.ShapeDtypeStruct((4,) + p.shape[1:], p.dtype) for p in parts],
        in_specs=[any_spec] * n, out_specs=[any_spec] * n,
        scratch_shapes=[pltpu.SemaphoreType.DMA((n, 4)), pltpu.SemaphoreType.DMA((n, 4))],
    )(*parts)


def _chip_exchange_start(sums, name):
    n = len(sums)

    def body(*refs):
        s_refs, land_refs = refs[:n], refs[n:2 * n]
        send_sems, recv_sems = refs[2 * n:2 * n + 2]
        token = refs[-1]
        x, y, c, chips = _position()
        for k in range(n):
            for j, chip in enumerate(chips):
                pltpu.make_async_remote_copy(
                    src_ref=s_refs[k].at[j], dst_ref=land_refs[k].at[j], send_sem=send_sems.at[3 * k + j],
                    recv_sem=recv_sems.at[3 * k + j], device_id=(*chip, c), device_id_type=MESH).start()
        token[...] = jnp.zeros_like(token)

    hbm = pl.BlockSpec(memory_space=pltpu.HBM)
    sem = pl.BlockSpec(memory_space=pltpu.SEMAPHORE)
    thru = [pltpu.HBM(s.shape, s.dtype) for s in sums]
    outs = pl.pallas_call(
        body, name=name,
        out_shape=(pltpu.SemaphoreType.DMA((3 * n,)), pltpu.SemaphoreType.DMA((3 * n,)), *thru, *thru,
                   jax.ShapeDtypeStruct((8, 128), F32)),
        in_specs=[hbm] * (2 * n), out_specs=(sem, sem, *[hbm] * (2 * n), pl.BlockSpec(memory_space=pltpu.VMEM)),
        input_output_aliases={k: 2 + k for k in range(2 * n)},
        compiler_params=pltpu.CompilerParams(has_side_effects=pltpu.SideEffectType.DATAFLOW_SIDE_EFFECTING),
    )(*[pltpu.with_memory_space_constraint(s, pltpu.HBM) for s in sums],
      *[pltpu.with_memory_space_constraint(lax.empty(s.shape, s.dtype), pltpu.HBM) for s in sums])
    return outs[0], outs[1], outs[2:2 + n], outs[2 + n:2 + 2 * n], outs[-1]


def _chip_exchange_wait(send_sems, recv_sems, s_thru, land_thru, after, name):
    n = len(s_thru)

    def body(*refs):
        s_refs, land_refs = refs[:n], refs[n:2 * n]
        send_sems, recv_sems = refs[2 * n:2 * n + 2]
        x, y, c, chips = _position()
        for k in range(n):
            for j, chip in enumerate(chips):
                cp = pltpu.make_async_remote_copy(
                    src_ref=s_refs[k].at[j], dst_ref=land_refs[k].at[j], send_sem=send_sems.at[3 * k + j],
                    recv_sem=recv_sems.at[3 * k + j], device_id=(*chip, c), device_id_type=MESH)
                cp.wait_send()
                cp.wait_recv()

    hbm = pl.BlockSpec(memory_space=pltpu.HBM)
    sem = pl.BlockSpec(memory_space=pltpu.SEMAPHORE)
    thru = [pltpu.HBM(s.shape, s.dtype) for s in s_thru]
    outs = pl.pallas_call(
        body, name=name,
        out_shape=(*thru, *thru),
        in_specs=[hbm] * (2 * n) + [sem, sem, pl.BlockSpec(memory_space=pl.ANY)], out_specs=[hbm] * (2 * n),
        input_output_aliases={k: k for k in range(2 * n)},
        compiler_params=pltpu.CompilerParams(has_side_effects=pltpu.SideEffectType.DATAFLOW_SIDE_EFFECTING),
    )(*s_thru, *land_thru, send_sems, recv_sems, after)
    return outs[n:]


def _owner_table():
    x, y, c = lax.axis_index("x"), lax.axis_index("y"), lax.axis_index("c")
    chips = [(x, y), (1 - x, y), (x, 1 - y), (1 - x, 1 - y)]
    return jnp.stack([_dev(px, py, c) for px, py in chips]).astype(jnp.int32)


def _chip_partial_sums(table, g, from_sibling, name):
    _, rows, cols = g.shape

    def body(tab_ref, g_ref, l_ref, out_ref):
        del tab_ref
        out_ref[...] = (g_ref[...].astype(F32) + l_ref[...].astype(F32)).astype(out_ref.dtype)

    grid_spec = pltpu.PrefetchScalarGridSpec(
        num_scalar_prefetch=1, grid=(3,),
        in_specs=[pl.BlockSpec((None, rows, cols), lambda j, tab: (tab[j + 1], 0, 0)),
                  pl.BlockSpec((None, rows, cols), lambda j, tab: (j + 1, 0, 0))],
        out_specs=pl.BlockSpec((None, rows, cols), lambda j, tab: (j, 0, 0)))
    return pl.pallas_call(
        body, name=name, grid_spec=grid_spec,
        out_shape=jax.ShapeDtypeStruct((3, rows, cols), BF16),
        compiler_params=_params(("arbitrary",)),
    )(table, g, from_sibling)


def _final_grad_sums(table, g, from_sibling, from_chips, name):
    _, rows, cols = g.shape
    tile = rows // 2

    def body(tab_ref, g_ref, l_ref, c_ref, out_ref):
        del tab_ref
        acc = g_ref[...].astype(F32) + l_ref[...].astype(F32)
        for j in range(3):
            acc = acc + c_ref[j].astype(F32)
        out_ref[...] = acc

    grid_spec = pltpu.PrefetchScalarGridSpec(
        num_scalar_prefetch=1, grid=(2,),
        in_specs=[pl.BlockSpec((None, tile, cols), lambda t, tab: (tab[0], t, 0)),
                  pl.BlockSpec((None, tile, cols), lambda t, tab: (0, t, 0)),
                  pl.BlockSpec((3, tile, cols), lambda t, tab: (0, t, 0))],
        out_specs=pl.BlockSpec((tile, cols), lambda t, tab: (t, 0)))
    return pl.pallas_call(
        body, name=name, grid_spec=grid_spec,
        out_shape=jax.ShapeDtypeStruct((rows, cols), F32),
        compiler_params=_params(("arbitrary",)),
    )(table, g, from_sibling, from_chips)


def _sum_blocks(g8):
    _, rows, cols = g8.shape

    def body(g_ref, out_ref):
        acc = g_ref[0]
        for d in range(1, N_DEV):
            acc = acc + g_ref[d]
        out_ref[...] = acc

    return pl.pallas_call(
        body, name="small_grad_sum", grid=(1,),
        in_specs=[_full((N_DEV, rows, cols))], out_specs=_full((rows, cols)),
        out_shape=jax.ShapeDtypeStruct((rows, cols), F32),
        compiler_params=_params(("arbitrary",)),
    )(g8)


def _fwd_mix(x2d, gw, g_mix, conv_w, conv_b, ln_g, ln_b, pool_w, pool_scale, after, seq, tm):
    tokens = x2d.shape[0]
    n_tiles = tokens // tm
    tps = seq // tm

    def body(x_ref, gmix_ref, gw_hbm, cw_ref, cb_ref, lng_ref, lnb_ref, pw_ref, ps_ref, after_ref,
             x1_ref, u_ref, c_ref, pooled_ref, ymix_ref, h1_ref,
             win_v, wout_v, hc_carry, up_carry, sem):
        del after_ref
        i = pl.program_id(0)

        @pl.when(i == 0)
        def _():
            copies = _load_weight(gw_hbm, "w_in", win_v, sem) + _load_weight(gw_hbm, "w_out", wout_v, sem)
            for cp in copies:
                cp.start()
            for cp in copies:
                cp.wait()

        @pl.when(i % tps == 0)
        def _():
            hc_carry[...] = jnp.zeros_like(hc_carry)
            up_carry[...] = jnp.zeros_like(up_carry)

        x = x_ref[...]
        xh, _ = _rms_fwd(x)
        h1 = (xh * gmix_ref[...]).astype(BF16)
        h1_ref[...] = h1
        u = _dot_nt(h1, win_v[...])
        u_ref[...] = u
        val, gate, up = u[:, :D_CONV], u[:, D_CONV:2 * D_CONV], u[:, 2 * D_CONV:]

        hc = val * _sigmoid(gate)
        ext = jnp.concatenate([hc_carry[...], hc], axis=0)
        hc_carry[...] = hc[tm - CONV_HALO:, :]
        conv = jnp.broadcast_to(cb_ref[...], (tm, D_CONV))
        for k in range(CONV_WIDTH):
            shift = CONV_WIDTH - 1 - k
            tap = ext if shift == 0 else pltpu.roll(ext, shift, 0)
            conv = conv + cw_ref[k:k + 1, :] * tap[CONV_HALO:, :]
        c_ref[...] = conv
        mu = jnp.mean(conv, axis=-1, keepdims=True)
        cen = conv - mu
        ln = cen * lax.rsqrt(jnp.mean(cen * cen, axis=-1, keepdims=True) + EPS) * lng_ref[...] + lnb_ref[...]
        y_conv = ln * _sigmoid(ln)

        extp = jnp.concatenate([up_carry[...], up], axis=0)
        up_carry[...] = up[tm - POOL_HALO:, :]
        pos = lax.broadcasted_iota(jnp.int32, (tm, 1), 0) + (i % tps) * tm
        run = extp
        mixed = []
        for g, w in enumerate(POOL_WINDOWS):
            lo = g * POOL_GROUP_DIM
            run = run[:, POOL_GROUP_DIM if g else 0:]
            run = run + pltpu.roll(run, w // 2, 0)
            cnt = jnp.minimum(pos + 1, w).astype(F32)
            pooled = run[POOL_HALO:, :POOL_GROUP_DIM] / cnt - up[:, lo:lo + POOL_GROUP_DIM]
            pooled = pooled.astype(BF16)
            pooled_ref[:, lo:lo + POOL_GROUP_DIM] = pooled
            mixed.append(_dot(pooled, pw_ref[g].astype(BF16)))
        y_pool = jnp.concatenate(mixed, axis=-1) * ps_ref[...]

        ymix = jnp.concatenate([y_conv, y_pool], axis=-1).astype(BF16)
        ymix_ref[...] = ymix
        x1_ref[...] = x + _dot(ymix, wout_v[...])

    row = lambda w: pl.BlockSpec((tm, w), lambda i: (i, 0))
    return pl.pallas_call(
        body, name="fwd_mix", grid=(n_tiles,),
        in_specs=[row(D_MODEL), _full((1, D_MODEL)), pl.BlockSpec(memory_space=pl.ANY),
                  _full((CONV_WIDTH, D_CONV)), _full((1, D_CONV)), _full((1, D_CONV)), _full((1, D_CONV)),
                  _full((4, POOL_GROUP_DIM, POOL_GROUP_DIM)), _full((1, D_POOL)), _full(after.shape)],
        out_specs=[row(D_MODEL), row(D_IN), row(D_CONV), row(D_POOL), row(D_MODEL), row(D_MODEL)],
        out_shape=[jax.ShapeDtypeStruct((tokens, D_MODEL), F32), jax.ShapeDtypeStruct((tokens, D_IN), F32),
                   jax.ShapeDtypeStruct((tokens, D_CONV), F32), jax.ShapeDtypeStruct((tokens, D_POOL), BF16),
                   jax.ShapeDtypeStruct((tokens, D_MODEL), BF16), jax.ShapeDtypeStruct((tokens, D_MODEL), BF16)],
        scratch_shapes=[pltpu.VMEM((D_IN, D_MODEL), BF16), pltpu.VMEM((D_MODEL, D_MODEL), BF16),
                        pltpu.VMEM((CONV_HALO, D_CONV), F32), pltpu.VMEM((POOL_HALO, D_POOL), F32),
                        pltpu.SemaphoreType.DMA],
        compiler_params=_params(),
    )(x2d, g_mix, gw, conv_w, conv_b, ln_g, ln_b, pool_w, pool_scale, after)


def _fwd_kv(mem2d, gw, g_mem):
    rows = mem2d.shape[0]
    n_b = rows // N_MEM

    def body(mem_ref, g_ref, gw_hbm, mn_ref, kv_ref, wkv_v, sem):
        @pl.when(pl.program_id(0) == 0)
        def _():
            copies = _load_weight(gw_hbm, "w_kv", wkv_v, sem)
            for cp in copies:
                cp.start()
            for cp in copies:
                cp.wait()

        mh, _ = _rms_fwd(mem_ref[...])
        mn = (mh * g_ref[...]).astype(BF16)
        mn_ref[...] = mn
        kv_ref[...] = _dot_nt(mn, wkv_v[...]).astype(BF16)

    return pl.pallas_call(
        body, name="fwd_kv", grid=(n_b,),
        in_specs=[pl.BlockSpec((N_MEM, D_MODEL), lambda b: (b, 0)), _full((1, D_MODEL)), pl.BlockSpec(memory_space=pl.ANY)],
        out_specs=[pl.BlockSpec((N_MEM, D_MODEL), lambda b: (b, 0)), pl.BlockSpec((N_MEM, 2 * D_MODEL), lambda b: (b, 0))],
        out_shape=[jax.ShapeDtypeStruct((rows, D_MODEL), BF16), jax.ShapeDtypeStruct((rows, 2 * D_MODEL), BF16)],
        scratch_shapes=[pltpu.VMEM((2 * D_MODEL, D_MODEL), BF16), pltpu.SemaphoreType.DMA],
        compiler_params=_params(),
    )(mem2d, g_mem, gw)


def _softmax_rows(s):
    e = jnp.exp(s - jnp.max(s, axis=-1, keepdims=True))
    return e / jnp.sum(e, axis=-1, keepdims=True)


def _fwd_attn(x1, kv, gw, g_x, seq, tm):
    tokens = x1.shape[0]
    n_tiles = tokens // tm
    tps = seq // tm

    def body(x1_ref, kv_ref, g_ref, gw_hbm, x2_ref, h2_ref, q_ref, o_ref, wq_v, wo_v, sem):
        @pl.when(pl.program_id(0) == 0)
        def _():
            copies = _load_weight(gw_hbm, "w_q", wq_v, sem) + _load_weight(gw_hbm, "w_o", wo_v, sem)
            for cp in copies:
                cp.start()
            for cp in copies:
                cp.wait()

        x1v = x1_ref[...]
        xh, _ = _rms_fwd(x1v)
        h2 = (xh * g_ref[...]).astype(BF16)
        h2_ref[...] = h2
        q = (_dot(h2, wq_v[...]) * (HEAD_DIM ** -0.5)).astype(BF16)
        q_ref[...] = q
        outs = []
        for h in range(HEADS):
            lo = h * HEAD_DIM
            p = _softmax_rows(_dot_nt(q[:, lo:lo + HEAD_DIM], kv_ref[:, lo:lo + HEAD_DIM]))
            outs.append(_dot(p.astype(BF16), kv_ref[:, D_MODEL + lo:D_MODEL + lo + HEAD_DIM]))
        o = jnp.concatenate(outs, axis=-1).astype(BF16)
        o_ref[...] = o
        x2_ref[...] = x1v + _dot(o, wo_v[...])

    row = lambda w: pl.BlockSpec((tm, w), lambda i: (i, 0))
    return pl.pallas_call(
        body, name="fwd_attn", grid=(n_tiles,),
        in_specs=[row(D_MODEL), pl.BlockSpec((N_MEM, 2 * D_MODEL), lambda i: (i // tps, 0)), _full((1, D_MODEL)),
                  pl.BlockSpec(memory_space=pl.ANY)],
        out_specs=[row(D_MODEL)] * 4,
        out_shape=[jax.ShapeDtypeStruct((tokens, D_MODEL), F32)] + [jax.ShapeDtypeStruct((tokens, D_MODEL), BF16)] * 3,
        scratch_shapes=[pltpu.VMEM((D_MODEL, D_MODEL), BF16), pltpu.VMEM((D_MODEL, D_MODEL), BF16), pltpu.SemaphoreType.DMA],
        compiler_params=_params(),
    )(x1, kv, g_x, gw)


def _ffn_conv(uu, halo, w_ref, b_ref, cols):
    ext = jnp.concatenate([halo, uu], axis=0)
    p1 = pltpu.roll(ext, 1, 0)[FFN_HALO:, :]
    p2 = pltpu.roll(ext, 2, 0)[FFN_HALO:, :]
    return b_ref[:, cols] + w_ref[2:3, cols] * uu + w_ref[1:2, cols] * p1 + w_ref[0:1, cols] * p2


def _fwd_ffn(x2, target, gw, g_ffn, ffn_w, ffn_b, g_final, seq, tm):
    tokens = x2.shape[0]
    n_tiles = tokens // tm
    tps = seq // tm
    n_chunks = D_FF // FFN_CHUNK

    def body(x2_ref, tgt_ref, gffn_ref, gw_hbm, fw_ref, fb_ref, gfin_ref,
             uu_ref, a_ref, h3_ref, dx3_ref, dx3b_ref, loss_ref, dgfin_ref,
             wup_v, wdown_v, carry, sem):
        i = pl.program_id(0)

        @pl.when(i == 0)
        def _():
            copies = _load_weight(gw_hbm, "w_up", wup_v, sem) + _load_weight(gw_hbm, "w_down", wdown_v, sem)
            for cp in copies:
                cp.start()
            for cp in copies:
                cp.wait()
            loss_ref[...] = jnp.zeros_like(loss_ref)
            dgfin_ref[...] = jnp.zeros_like(dgfin_ref)

        @pl.when(i % tps == 0)
        def _():
            carry[...] = jnp.zeros_like(carry)

        x2v = x2_ref[...]
        xh, _ = _rms_fwd(x2v)
        h3 = (xh * gffn_ref[...]).astype(BF16)
        h3_ref[...] = h3
        acc = jnp.zeros((tm, D_MODEL), F32)
        for jc in range(n_chunks):
            halves = []
            for half in range(2):
                cols = pl.ds(half * D_FF + jc * FFN_CHUNK, FFN_CHUNK)
                uu = _dot_nt(h3, wup_v[cols, :])
                uu_ref[:, cols] = uu
                halves.append(_ffn_conv(uu, carry[:, cols], fw_ref, fb_ref, cols))
                carry[:, cols] = uu[tm - FFN_HALO:, :]
            gate, val = halves
            a = (gate * _sigmoid(gate) * val).astype(BF16)
            a_ref[:, pl.ds(jc * FFN_CHUNK, FFN_CHUNK)] = a
            acc = acc + _dot(a, wdown_v[pl.ds(jc * FFN_CHUNK, FFN_CHUNK), :])
        x3 = x2v + acc

        xh3, r3 = _rms_fwd(x3)
        gfin = gfin_ref[...]
        err = xh3 * gfin - tgt_ref[...]
        loss_ref[...] += jnp.full(loss_ref.shape, jnp.sum(err * err) * (0.5 / D_MODEL), F32)
        dy = err * (1.0 / D_MODEL)
        dgfin_ref[...] += _colsum(dy * xh3)
        dx3 = _rms_bwd(dy, xh3, r3, gfin)
        dx3_ref[...] = dx3
        dx3b_ref[...] = dx3.astype(BF16)

    row = lambda w: pl.BlockSpec((tm, w), lambda i: (i, 0))
    return pl.pallas_call(
        body, name="fwd_ffn", grid=(n_tiles,),
        in_specs=[row(D_MODEL), row(D_MODEL), _full((1, D_MODEL)), pl.BlockSpec(memory_space=pl.ANY),
                  _full((FFN_CONV_WIDTH, 2 * D_FF)), _full((1, 2 * D_FF)), _full((1, D_MODEL))],
        out_specs=[row(2 * D_FF), row(D_FF), row(D_MODEL), row(D_MODEL), row(D_MODEL), _full((8, 128)), _full((1, D_MODEL))],
        out_shape=[jax.ShapeDtypeStruct((tokens, 2 * D_FF), F32), jax.ShapeDtypeStruct((tokens, D_FF), BF16),
                   jax.ShapeDtypeStruct((tokens, D_MODEL), BF16), jax.ShapeDtypeStruct((tokens, D_MODEL), F32),
                   jax.ShapeDtypeStruct((tokens, D_MODEL), BF16),
                   jax.ShapeDtypeStruct((8, 128), F32), jax.ShapeDtypeStruct((1, D_MODEL), F32)],
        scratch_shapes=[pltpu.VMEM((2 * D_FF, D_MODEL), BF16), pltpu.VMEM((D_FF, D_MODEL), BF16),
                        pltpu.VMEM((FFN_HALO, 2 * D_FF), F32), pltpu.SemaphoreType.DMA],
        compiler_params=_params(),
    )(x2, target, g_ffn, gw, ffn_w, ffn_b, g_final)


def _bwd_ffn(dx3, x2, uu_all, gw, g_ffn, ffn_w, ffn_b, seq, tm):
    tokens = x2.shape[0]
    n_tiles = tokens // tm
    tps = seq // tm
    n_chunks = D_FF // FFN_CHUNK
    per8 = tm // FFN_HALO

    def body(dx3_ref, x2_ref, uu_ref, prev_ref, gffn_ref, gw_hbm, fw_ref, fb_ref,
             dx2_ref, dx2b_ref, duu_ref, dfb_ref, dfw_ref, dg_ref,
             wup_v, wdown_v, carry, sem):
        i = pl.program_id(0)
        t = n_tiles - 1 - i

        @pl.when(i == 0)
        def _():
            copies = _load_weight(gw_hbm, "w_up", wup_v, sem) + _load_weight(gw_hbm, "w_down", wdown_v, sem)
            for cp in copies:
                cp.start()
            for cp in copies:
                cp.wait()
            dfb_ref[...] = jnp.zeros_like(dfb_ref)
            dfw_ref[...] = jnp.zeros_like(dfw_ref)
            dg_ref[...] = jnp.zeros_like(dg_ref)

        @pl.when(t % tps == tps - 1)
        def _():
            carry[...] = jnp.zeros_like(carry)

        starts_sequence = (t % tps == 0)
        dx3v = dx3_ref[...]
        dx3b = dx3v.astype(BF16)
        dh3 = jnp.zeros((tm, D_MODEL), F32)
        for jc in range(n_chunks):
            da = _dot_nt(dx3b, wdown_v[pl.ds(jc * FFN_CHUNK, FFN_CHUNK), :])
            uus, ccs, colss = [], [], []
            for half in range(2):
                cols = pl.ds(half * D_FF + jc * FFN_CHUNK, FFN_CHUNK)
                uu = uu_ref[:, cols]
                halo = jnp.where(starts_sequence, 0.0, prev_ref[:, cols])
                uus.append(uu)
                colss.append(cols)
                ccs.append(_ffn_conv(uu, halo, fw_ref, fb_ref, cols))
            gate, val = ccs
            sg = _sigmoid(gate)
            dgate = da * val * (sg * (1.0 + gate * (1.0 - sg)))
            dval = da * (gate * sg)
            for dcc, uu, cols in zip((dgate, dval), uus, colss):
                dfb_ref[:, cols] += _colsum(dcc)
                ext = jnp.concatenate([dcc, carry[:, cols]], axis=0)
                carry[:, cols] = dcc[:FFN_HALO, :]
                n1 = pltpu.roll(ext, tm + FFN_HALO - 1, 0)[:tm, :]
                n2 = pltpu.roll(ext, tm + FFN_HALO - 2, 0)[:tm, :]
                duu = fw_ref[2:3, cols] * dcc + fw_ref[1:2, cols] * n1 + fw_ref[0:1, cols] * n2
                dfw_ref[2:3, cols] += _colsum(uu * dcc)
                dfw_ref[1:2, cols] += _colsum(uu * n1)
                dfw_ref[0:1, cols] += _colsum(uu * n2)
                duub = duu.astype(BF16)
                duu_ref[:, cols] = duub
                dh3 = dh3 + _dot(duub, wup_v[cols, :])
        xh, r = _rms_fwd(x2_ref[...])
        dg_ref[...] += _colsum(dh3 * xh)
        dx2 = dx3v + _rms_bwd(dh3, xh, r, gffn_ref[...])
        dx2_ref[...] = dx2
        dx2b_ref[...] = dx2.astype(BF16)

    rev = lambda w: pl.BlockSpec((tm, w), lambda i: (n_tiles - 1 - i, 0))
    prev = pl.BlockSpec((FFN_HALO, 2 * D_FF), lambda i: (jnp.maximum((n_tiles - 1 - i) * per8 - 1, 0), 0))
    return pl.pallas_call(
        body, name="bwd_ffn", grid=(n_tiles,),
        in_specs=[rev(D_MODEL), rev(D_MODEL), rev(2 * D_FF), prev, _full((1, D_MODEL)), pl.BlockSpec(memory_space=pl.ANY),
                  _full((FFN_CONV_WIDTH, 2 * D_FF)), _full((1, 2 * D_FF))],
        out_specs=[rev(D_MODEL), rev(D_MODEL), rev(2 * D_FF), _full((1, 2 * D_FF)), _full((FFN_CONV_WIDTH, 2 * D_FF)),
                   _full((1, D_MODEL))],
        out_shape=[jax.ShapeDtypeStruct((tokens, D_MODEL), F32), jax.ShapeDtypeStruct((tokens, D_MODEL), BF16),
                   jax.ShapeDtypeStruct((tokens, 2 * D_FF), BF16),
                   jax.ShapeDtypeStruct((1, 2 * D_FF), F32), jax.ShapeDtypeStruct((FFN_CONV_WIDTH, 2 * D_FF), F32),
                   jax.ShapeDtypeStruct((1, D_MODEL), F32)],
        scratch_shapes=[pltpu.VMEM((2 * D_FF, D_MODEL), BF16), pltpu.VMEM((D_FF, D_MODEL), BF16),
                        pltpu.VMEM((FFN_HALO, 2 * D_FF), F32), pltpu.SemaphoreType.DMA],
        compiler_params=_params(),
    )(dx3, x2, uu_all, uu_all, g_ffn, gw, ffn_w, ffn_b)


def _bwd_attn(dx2, x1, q, kv, gw, g_x, after, seq, tm):
    tokens = x1.shape[0]
    n_tiles = tokens // tm
    tps = seq // tm
    n_b = tokens // seq

    def body(dx2_ref, x1_ref, q_ref, kv_ref, g_ref, gw_hbm, after_ref, dx1_ref, dx1b_ref, dq_ref, dkv_ref, dg_ref,
             wq_v, wo_v, sem):
        del after_ref
        i = pl.program_id(0)

        @pl.when(i == 0)
        def _():
            copies = _load_weight(gw_hbm, "w_q", wq_v, sem) + _load_weight(gw_hbm, "w_o", wo_v, sem)
            for cp in copies:
                cp.start()
            for cp in copies:
                cp.wait()
            dg_ref[...] = jnp.zeros_like(dg_ref)

        @pl.when(i % tps == 0)
        def _():
            dkv_ref[...] = jnp.zeros_like(dkv_ref)

        dx2v = dx2_ref[...]
        do = _dot_nt(dx2v.astype(BF16), wo_v[...]).astype(BF16)
        q = q_ref[...]
        dqs = []
        for h in range(HEADS):
            lo = h * HEAD_DIM
            kcols, vcols = pl.ds(lo, HEAD_DIM), pl.ds(D_MODEL + lo, HEAD_DIM)
            qh, doh = q[:, lo:lo + HEAD_DIM], do[:, lo:lo + HEAD_DIM]
            p = _softmax_rows(_dot_nt(qh, kv_ref[:, kcols]))
            dp = _dot_nt(doh, kv_ref[:, vcols])
            dkv_ref[:, vcols] += _dot_tn(p.astype(BF16), doh)
            ds = (p * (dp - jnp.sum(dp * p, axis=-1, keepdims=True))).astype(BF16)
            dqs.append(_dot(ds, kv_ref[:, kcols]) * (HEAD_DIM ** -0.5))
            dkv_ref[:, kcols] += _dot_tn(ds, qh)
        dq = jnp.concatenate(dqs, axis=-1).astype(BF16)
        dq_ref[...] = dq
        dh2 = _dot_nt(dq, wq_v[...])
        xh, r = _rms_fwd(x1_ref[...])
        dg_ref[...] += _colsum(dh2 * xh)
        dx1 = dx2v + _rms_bwd(dh2, xh, r, g_ref[...])
        dx1_ref[...] = dx1
        dx1b_ref[...] = dx1.astype(BF16)

    row = lambda w: pl.BlockSpec((tm, w), lambda i: (i, 0))
    per_b = pl.BlockSpec((N_MEM, 2 * D_MODEL), lambda i: (i // tps, 0))
    return pl.pallas_call(
        body, name="bwd_attn", grid=(n_tiles,),
        in_specs=[row(D_MODEL), row(D_MODEL), row(D_MODEL), per_b, _full((1, D_MODEL)), pl.BlockSpec(memory_space=pl.ANY),
                  _full(after.shape)],
        out_specs=[row(D_MODEL), row(D_MODEL), row(D_MODEL), per_b, _full((1, D_MODEL))],
        out_shape=[jax.ShapeDtypeStruct((tokens, D_MODEL), F32), jax.ShapeDtypeStruct((tokens, D_MODEL), BF16),
                   jax.ShapeDtypeStruct((tokens, D_MODEL), BF16),
                   jax.ShapeDtypeStruct((n_b * N_MEM, 2 * D_MODEL), F32), jax.ShapeDtypeStruct((1, D_MODEL), F32)],
        scratch_shapes=[pltpu.VMEM((D_MODEL, D_MODEL), BF16), pltpu.VMEM((D_MODEL, D_MODEL), BF16), pltpu.SemaphoreType.DMA],
        compiler_params=_params(),
    )(dx2, x1, q, kv, g_x, gw, after)


def _bwd_kv(dkv, mem2d, gw, g_mem):
    rows = mem2d.shape[0]
    n_b = rows // N_MEM

    def body(dkv_ref, mem_ref, gw_hbm, dkvb_ref, dg_ref, wkv_v, sem):
        @pl.when(pl.program_id(0) == 0)
        def _():
            copies = _load_weight(gw_hbm, "w_kv", wkv_v, sem)
            for cp in copies:
                cp.start()
            for cp in copies:
                cp.wait()
            dg_ref[...] = jnp.zeros_like(dg_ref)

        dkvb = dkv_ref[...].astype(BF16)
        dkvb_ref[...] = dkvb
        dmn = _dot(dkvb, wkv_v[...])
        mh, _ = _rms_fwd(mem_ref[...])
        dg_ref[...] += _colsum(dmn * mh)

    del g_mem
    return pl.pallas_call(
        body, name="bwd_kv", grid=(n_b,),
        in_specs=[pl.BlockSpec((N_MEM, 2 * D_MODEL), lambda b: (b, 0)), pl.BlockSpec((N_MEM, D_MODEL), lambda b: (b, 0)),
                  pl.BlockSpec(memory_space=pl.ANY)],
        out_specs=[pl.BlockSpec((N_MEM, 2 * D_MODEL), lambda b: (b, 0)), _full((1, D_MODEL))],
        out_shape=[jax.ShapeDtypeStruct((rows, 2 * D_MODEL), BF16), jax.ShapeDtypeStruct((1, D_MODEL), F32)],
        scratch_shapes=[pltpu.VMEM((2 * D_MODEL, D_MODEL), BF16), pltpu.SemaphoreType.DMA],
        compiler_params=_params(),
    )(dkv, mem2d, gw)


def _bwd_mix(dx1, x2d, u_all, c_all, pooled_all, gw, g_mix, conv_w, ln_g, ln_b, pool_w, pool_scale, after, seq, tm):
    tokens = x2d.shape[0]
    n_tiles = tokens // tm
    tps = seq // tm

    def body(dx1_ref, x_ref, u_ref, c_ref, pooled_ref, gmix_ref, gw_hbm, cw_ref, lng_ref, lnb_ref, pw_ref, ps_ref,
             after_ref, dx_ref, du_ref, dgmix_ref, dcw_ref, dcb_ref, dlng_ref, dlnb_ref, dpw_ref, dps_ref,
             win_v, wout_v, dc_carry, e_carry, sem):
        del after_ref
        i = pl.program_id(0)
        t = n_tiles - 1 - i

        @pl.when(i == 0)
        def _():
            copies = _load_weight(gw_hbm, "w_in", win_v, sem) + _load_weight(gw_hbm, "w_out", wout_v, sem)
            for cp in copies:
                cp.start()
            for cp in copies:
                cp.wait()
            for ref in (dgmix_ref, dcw_ref, dcb_ref, dlng_ref, dlnb_ref, dpw_ref, dps_ref):
                ref[...] = jnp.zeros_like(ref)

        @pl.when(t % tps == tps - 1)
        def _():
            dc_carry[...] = jnp.zeros_like(dc_carry)
            e_carry[...] = jnp.zeros_like(e_carry)

        dx1v = dx1_ref[...]
        dymix = _dot_nt(dx1v.astype(BF16), wout_v[...])
        dyc, dyp = dymix[:, :D_CONV], dymix[:, D_CONV:]
        u = u_ref[...]
        val, gate = u[:, :D_CONV], u[:, D_CONV:2 * D_CONV]

        conv = c_ref[...]
        mu = jnp.mean(conv, axis=-1, keepdims=True)
        cen = conv - mu
        rs = lax.rsqrt(jnp.mean(cen * cen, axis=-1, keepdims=True) + EPS)
        chat = cen * rs
        ln = chat * lng_ref[...] + lnb_ref[...]
        sl = _sigmoid(ln)
        dln = dyc * (sl * (1.0 + ln * (1.0 - sl)))
        dlng_ref[...] += _colsum(dln * chat)
        dlnb_ref[...] += _colsum(dln)
        dchat = dln * lng_ref[...]
        dc = rs * (dchat - jnp.mean(dchat, axis=-1, keepdims=True)
                   - chat * jnp.mean(dchat * chat, axis=-1, keepdims=True))
        dcb_ref[...] += _colsum(dc)
        sg = _sigmoid(gate)
        hc = val * sg
        ext = jnp.concatenate([dc, dc_carry[...]], axis=0)
        dc_carry[...] = dc[:CONV_HALO, :]
        dhc = jnp.zeros((tm, D_CONV), F32)
        for k in range(CONV_WIDTH):
            ahead = CONV_WIDTH - 1 - k
            tap = (ext if ahead == 0 else pltpu.roll(ext, tm + CONV_HALO - ahead, 0))[:tm, :]
            dhc = dhc + cw_ref[k:k + 1, :] * tap
            dcw_ref[k:k + 1, :] += _colsum(hc * tap)
        du_ref[:, :D_CONV] = (dhc * sg).astype(BF16)
        du_ref[:, D_CONV:2 * D_CONV] = (dhc * val * (sg * (1.0 - sg))).astype(BF16)

        pos = lax.broadcasted_iota(jnp.int32, (tm, 1), 0) + (t % tps) * tm
        es, dpooled = [], []
        for g, w in enumerate(POOL_WINDOWS):
            cols = pl.ds(g * POOL_GROUP_DIM, POOL_GROUP_DIM)
            lo = g * POOL_GROUP_DIM
            pooled = pooled_ref[:, cols]
            pw = pw_ref[g].astype(BF16)
            dyg = dyp[:, lo:lo + POOL_GROUP_DIM]
            dps_ref[:, cols] += _colsum(dyg * _dot(pooled, pw))
            dmixed = (dyg * ps_ref[:, cols]).astype(BF16)
            dpw_ref[g] += _dot_tn(pooled, dmixed)
            dpo = _dot_nt(dmixed, pw)
            dpooled.append(dpo)
            es.append(dpo / jnp.minimum(pos + 1, w).astype(F32))
        e = jnp.concatenate(es, axis=-1)
        run = jnp.concatenate([e, e_carry[...]], axis=0)
        e_carry[...] = e[:POOL_HALO, :]
        rows = tm + POOL_HALO
        for g, w in enumerate(POOL_WINDOWS):
            lo = g * POOL_GROUP_DIM
            run = run[:, POOL_GROUP_DIM if g else 0:]
            run = run + pltpu.roll(run, rows - w // 2, 0)
            du_ref[:, 2 * D_CONV + lo:2 * D_CONV + lo + POOL_GROUP_DIM] = (
                run[:tm, :POOL_GROUP_DIM] - dpooled[g]).astype(BF16)

        dh1 = _dot(du_ref[...], win_v[...])
        xh, r = _rms_fwd(x_ref[...])
        dgmix_ref[...] += _colsum(dh1 * xh)
        dx_ref[...] = dx1v + _rms_bwd(dh1, xh, r, gmix_ref[...])

    rev = lambda w: pl.BlockSpec((tm, w), lambda i: (n_tiles - 1 - i, 0))
    return pl.pallas_call(
        body, name="bwd_mix", grid=(n_tiles,),
        in_specs=[rev(D_MODEL), rev(D_MODEL), rev(D_IN), rev(D_CONV), rev(D_POOL), _full((1, D_MODEL)),
                  pl.BlockSpec(memory_space=pl.ANY), _full((CONV_WIDTH, D_CONV)), _full((1, D_CONV)), _full((1, D_CONV)),
                  _full((4, POOL_GROUP_DIM, POOL_GROUP_DIM)), _full((1, D_POOL)), _full(after.shape)],
        out_specs=[rev(D_MODEL), rev(D_IN), _full((1, D_MODEL)), _full((CONV_WIDTH, D_CONV)), _full((1, D_CONV)),
                   _full((1, D_CONV)), _full((1, D_CONV)), _full((4, POOL_GROUP_DIM, POOL_GROUP_DIM)), _full((1, D_POOL))],
        out_shape=[jax.ShapeDtypeStruct((tokens, D_MODEL), F32), jax.ShapeDtypeStruct((tokens, D_IN), BF16),
                   jax.ShapeDtypeStruct((1, D_MODEL), F32), jax.ShapeDtypeStruct((CONV_WIDTH, D_CONV), F32),
                   jax.ShapeDtypeStruct((1, D_CONV), F32), jax.ShapeDtypeStruct((1, D_CONV), F32),
                   jax.ShapeDtypeStruct((1, D_CONV), F32),
                   jax.ShapeDtypeStruct((4, POOL_GROUP_DIM, POOL_GROUP_DIM), F32), jax.ShapeDtypeStruct((1, D_POOL), F32)],
        scratch_shapes=[pltpu.VMEM((D_IN, D_MODEL), BF16), pltpu.VMEM((D_MODEL, D_MODEL), BF16),
                        pltpu.VMEM((CONV_HALO, D_CONV), F32), pltpu.VMEM((POOL_HALO, D_POOL), F32),
                        pltpu.SemaphoreType.DMA],
        compiler_params=_params(),
    )(dx1, x2d, u_all, c_all, pooled_all, g_mix, gw, conv_w, ln_g, ln_b, pool_w, pool_scale, after)


def _wgrad(a, b, name, tm=256):
    tokens, m = a.shape
    n = b.shape[1]

    def body(a_ref, b_ref, out_ref):
        out_ref[...] = _dot_tn(a_ref[...], b_ref[...]).astype(out_ref.dtype)

    return pl.pallas_call(
        body, name=name, grid=(m // tm,),
        in_specs=[pl.BlockSpec((tokens, tm), lambda i: (0, i)), _full((tokens, n))],
        out_specs=pl.BlockSpec((tm, n), lambda i: (i, 0)),
        out_shape=jax.ShapeDtypeStruct((m, n), BF16),
        compiler_params=_params(),
    )(a, b)


def _adamw_update(w, g, m, v):
    nm = ADAM_B1 * m + (1.0 - ADAM_B1) * g
    nv = ADAM_B2 * v + (1.0 - ADAM_B2) * (g * g)
    m_hat = nm / (1.0 - ADAM_B1 ** ADAM_STEP)
    v_hat = nv / (1.0 - ADAM_B2 ** ADAM_STEP)
    return -ADAM_LR * (m_hat / (jnp.sqrt(v_hat) + ADAM_EPS) + ADAM_WD * w), nm, nv


def _adamw_small(ws, gs, ms, vs):
    n = len(ws)

    def body(*refs):
        ins, outs = refs[:4 * n], refs[4 * n:]
        for k in range(n):
            d, nm, nv = _adamw_update(*[ins[j * n + k][...] for j in range(4)])
            outs[k][...] = d
            outs[n + k][...] = nm
            outs[2 * n + k][...] = nv

    vmem = pl.BlockSpec(memory_space=pltpu.VMEM)
    outs = pl.pallas_call(
        body, name="adamw_small",
        in_specs=[vmem] * (4 * n), out_specs=[vmem] * (3 * n),
        out_shape=[jax.ShapeDtypeStruct(w.shape, F32) for w in ws] * 3,
    )(*ws, *gs, *ms, *vs)
    return outs[:n], outs[n:2 * n], outs[2 * n:]


def _adamw(w, g, m, v, name):
    rows, cols = w.shape
    tile = rows
    for cand in (512, 256, 128, 64, 32, 16, 8):
        if rows % cand == 0:
            tile = cand
            break

    def body(w_ref, g_ref, m_ref, v_ref, d_ref, nm_ref, nv_ref):
        d_ref[...], nm_ref[...], nv_ref[...] = _adamw_update(w_ref[...], g_ref[...], m_ref[...], v_ref[...])

    spec = pl.BlockSpec((tile, cols), lambda i: (i, 0))
    return pl.pallas_call(
        body, name=name, grid=(rows // tile,),
        in_specs=[spec] * 4, out_specs=[spec] * 3,
        out_shape=[jax.ShapeDtypeStruct((rows, cols), F32)] * 3,
        compiler_params=_params(("arbitrary",)),
    )(w, g, m, v)


SMALL = (("norm_mix_g", (1, 1024)), ("conv_dw_b", (1, 512)), ("conv_ln_g", (1, 512)), ("conv_ln_b", (1, 512)),
         ("pool_w", (1, 4, 128, 128)), ("pool_scale", (1, 512)), ("norm_xattn_g", (1, 1024)), ("norm_mem_g", (1, 1024)),
         ("norm_ffn_g", (1, 1024)), ("ffn_dw_b", (1, 5632)), ("norm_final_g", (1024,)))
LANES = 128


def _pack_rows(arrs):
    flat = jnp.concatenate([a.reshape(-1) for a in arrs])
    pad = (-flat.shape[0]) % (8 * LANES)
    return jnp.pad(flat, (0, pad)).reshape(-1, LANES)


def kernel(x, mem, norm_mix_g, w_in, conv_dw_w, conv_dw_b, conv_ln_g, conv_ln_b, pool_w, pool_scale, w_out, norm_xattn_g, norm_mem_g, w_q, w_kv, w_o, norm_ffn_g, w_up, ffn_dw_w, ffn_dw_b, w_down, norm_final_g, loss_target, m_norm_mix_g, m_w_in, m_conv_dw_w, m_conv_dw_b, m_conv_ln_g, m_conv_ln_b, m_pool_w, m_pool_scale, m_w_out, m_norm_xattn_g, m_norm_mem_g, m_w_q, m_w_kv, m_w_o, m_norm_ffn_g, m_w_up, m_ffn_dw_w, m_ffn_dw_b, m_w_down, m_norm_final_g, v_norm_mix_g, v_w_in, v_conv_dw_w, v_conv_dw_b, v_conv_ln_g, v_conv_ln_b, v_pool_w, v_pool_scale, v_w_out, v_norm_xattn_g, v_norm_mem_g, v_w_q, v_w_kv, v_w_o, v_norm_ffn_g, v_w_up, v_ffn_dw_w, v_ffn_dw_b, v_w_down, v_norm_final_g):
    weights = dict(norm_mix_g=norm_mix_g, w_in=w_in, conv_dw_w=conv_dw_w, conv_dw_b=conv_dw_b, conv_ln_g=conv_ln_g,
                   conv_ln_b=conv_ln_b, pool_w=pool_w, pool_scale=pool_scale, w_out=w_out, norm_xattn_g=norm_xattn_g,
                   norm_mem_g=norm_mem_g, w_q=w_q, w_kv=w_kv, w_o=w_o, norm_ffn_g=norm_ffn_g, w_up=w_up,
                   ffn_dw_w=ffn_dw_w, ffn_dw_b=ffn_dw_b, w_down=w_down, norm_final_g=norm_final_g)
    moments_m = dict(norm_mix_g=m_norm_mix_g, w_in=m_w_in, conv_dw_w=m_conv_dw_w, conv_dw_b=m_conv_dw_b,
                     conv_ln_g=m_conv_ln_g, conv_ln_b=m_conv_ln_b, pool_w=m_pool_w, pool_scale=m_pool_scale,
                     w_out=m_w_out, norm_xattn_g=m_norm_xattn_g, norm_mem_g=m_norm_mem_g, w_q=m_w_q, w_kv=m_w_kv,
                     w_o=m_w_o, norm_ffn_g=m_norm_ffn_g, w_up=m_w_up, ffn_dw_w=m_ffn_dw_w, ffn_dw_b=m_ffn_dw_b,
                     w_down=m_w_down, norm_final_g=m_norm_final_g)
    moments_v = dict(norm_mix_g=v_norm_mix_g, w_in=v_w_in, conv_dw_w=v_conv_dw_w, conv_dw_b=v_conv_dw_b,
                     conv_ln_g=v_conv_ln_g, conv_ln_b=v_conv_ln_b, pool_w=v_pool_w, pool_scale=v_pool_scale,
                     w_out=v_w_out, norm_xattn_g=v_norm_xattn_g, norm_mem_g=v_norm_mem_g, w_q=v_w_q, w_kv=v_w_kv,
                     w_o=v_w_o, norm_ffn_g=v_norm_ffn_g, w_up=v_w_up, ffn_dw_w=v_ffn_dw_w, ffn_dw_b=v_ffn_dw_b,
                     w_down=v_w_down, norm_final_g=v_norm_final_g)
    order = list(weights)
    transposed = ("w_in", "w_kv", "w_up")

    n_b, seq, _ = x.shape
    tokens = n_b * seq
    tm_mix = min(512, seq // 2)
    tm_ffn = min(256, seq // 2)
    dev = 4 * lax.axis_index("x") + 2 * lax.axis_index("y") + lax.axis_index("c")

    packs = [jnp.concatenate([weights[n][0].T if n in transposed else weights[n][0] for n in names], axis=0).astype(BF16)
             for names in AG_GROUPS]
    small_sharded = _pack_rows([conv_dw_w[0], ffn_dw_w[0]])
    gw_mix, gsmall = _all_gather([packs[0], small_sharded], "weights_all_gather")
    flights = []
    after = gw_mix
    for k in (1, 2):
        own_in_place = lax.dynamic_update_slice(lax.empty((N_DEV,) + packs[k].shape, BF16), packs[k][None], (dev, 0, 0))
        flights.append(_gather_start(own_in_place, after, "weights_gather_start_%d" % k))
        after = flights[-1][3]
    gflat = gsmall.reshape(N_DEV, -1)
    n_cw = CONV_WIDTH * (D_CONV // N_DEV)
    n_fw = FFN_CONV_WIDTH * (2 * D_FF // N_DEV)
    conv_w = gflat[:, :n_cw].reshape(N_DEV, CONV_WIDTH, D_CONV // N_DEV).transpose(1, 0, 2).reshape(CONV_WIDTH, D_CONV)
    ffn_w = gflat[:, n_cw:n_cw + n_fw].reshape(N_DEV, FFN_CONV_WIDTH, 2 * D_FF // N_DEV).transpose(1, 0, 2).reshape(
        FFN_CONV_WIDTH, 2 * D_FF)

    x2d = x.reshape(tokens, D_MODEL)
    mem2d = mem.reshape(n_b * N_MEM, D_MODEL)
    tgt2d = loss_target.reshape(tokens, D_MODEL)
    g_final = norm_final_g.reshape(1, D_MODEL)

    def gather_finish(flight, after, tag):
        fwd_send, fwd_recv, buf = _gather_forward(*flight[:3], after, "weights_gather_forward_" + tag)
        return _gather_finish(fwd_send, fwd_recv, buf, "weights_gather_finish_" + tag)

    x1, u_all, c_all, pooled_all, ymix, h1 = _fwd_mix(
        x2d, gw_mix, norm_mix_g, conv_w, conv_dw_b, conv_ln_g, conv_ln_b, pool_w[0], pool_scale, flights[1][3],
        seq, tm_mix)
    gw_attn = gather_finish(flights[0], x1, "1")
    mem_n, kv = _fwd_kv(mem2d, gw_attn, norm_mem_g)
    x2, h2, q, o = _fwd_attn(x1, kv, gw_attn, norm_xattn_g, seq, tm_mix)
    gw_ffn = gather_finish(flights[1], x2, "2")
    uu_all, a_all, h3, dx3, dx3b, loss_part, dg_final = _fwd_ffn(
        x2, tgt2d, gw_ffn, norm_ffn_g, ffn_w, ffn_dw_b, g_final, seq, tm_ffn)

    table = _owner_table()

    def reduce_start(names, tag):
        parts = [part[n].reshape(N_DEV, W_OFF[n][1], D_MODEL) for n in names]
        landed = _exchange_sibling(parts, "rs_sibling_exchange_" + tag)
        sums = [_chip_partial_sums(table, p, l, "rs_chip_partial_sums_" + n) for n, p, l in zip(names, parts, landed)]
        return parts, landed, _chip_exchange_start(sums, "rs_chip_exchange_start_" + tag)

    def reduce_finish(names, parts, landed, flight, after, tag):
        from_chips = _chip_exchange_wait(*flight[:4], after, "rs_chip_exchange_wait_" + tag)
        for n, p, l, f in zip(names, parts, landed, from_chips):
            g_mine[n] = _final_grad_sums(table, p, l, f, "rs_final_sums_" + n)
        return g_mine[names[-1]]

    part, g_mine = {}, {}
    dx2, dx2b, duu, d_ffn_b, d_ffn_w, dg_ffn = _bwd_ffn(dx3, x2, uu_all, gw_ffn, norm_ffn_g, ffn_w, ffn_dw_b, seq, tm_ffn)
    part["w_up"] = _wgrad(duu, h3, "wgrad_w_up")
    part["w_down"] = _wgrad(a_all, dx3b, "wgrad_w_down")
    parts_a, landed_a, flight_a = reduce_start(RS_GROUPS["a"], "a")
    dx1, dx1b, dq, dkv, dg_x = _bwd_attn(dx2, x1, q, kv, gw_attn, norm_xattn_g, flight_a[4], seq, tm_mix)
    dkv_b, dg_mem = _bwd_kv(dkv, mem2d, gw_attn, norm_mem_g)
    part["w_q"] = _wgrad(h2, dq, "wgrad_w_q")
    part["w_kv"] = _wgrad(dkv_b, mem_n, "wgrad_w_kv")
    part["w_o"] = _wgrad(o, dx2b, "wgrad_w_o")
    parts_b, landed_b, flight_b = reduce_start(RS_GROUPS["b"], "b")
    dx, du, dg_mix, d_conv_w, d_conv_b, d_ln_g, d_ln_b, d_pool_w, d_pool_scale = _bwd_mix(
        dx1, x2d, u_all, c_all, pooled_all, gw_mix, norm_mix_g, conv_w, conv_ln_g, conv_ln_b, pool_w[0], pool_scale,
        flight_b[4], seq, tm_mix)
    grad_x = dx.reshape(x.shape)
    part["w_in"] = _wgrad(du, h1, "wgrad_w_in")
    part["w_out"] = _wgrad(ymix, dx1b, "wgrad_w_out")
    parts_c, landed_c, flight_c = reduce_start(RS_GROUPS["c"], "c")
    done_a = reduce_finish(RS_GROUPS["a"], parts_a, landed_a, flight_a, flight_c[4], "a")
    done_b = reduce_finish(RS_GROUPS["b"], parts_b, landed_b, flight_b, done_a, "b")
    reduce_finish(RS_GROUPS["c"], parts_c, landed_c, flight_c, done_b, "c")

    small_grads = dict(norm_mix_g=dg_mix, conv_dw_b=d_conv_b, conv_ln_g=d_ln_g, conv_ln_b=d_ln_b, pool_w=d_pool_w,
                       pool_scale=d_pool_scale, norm_xattn_g=dg_x, norm_mem_g=dg_mem, norm_ffn_g=dg_ffn,
                       ffn_dw_b=d_ffn_b, norm_final_g=dg_final)
    small_list = [small_grads[n] for n, _ in SMALL] + [d_conv_w, d_ffn_w, loss_part[:1]]
    (small_all,) = _all_gather([_pack_rows(small_list)], "small_grads_all_gather")
    small_sum = _sum_blocks(small_all).reshape(-1)

    grads = {}
    pos = 0
    for n, shape in SMALL:
        size = 1
        for s in shape:
            size *= s
        grads[n] = small_sum[pos:pos + size].reshape(shape)
        pos += size
    full_conv_w = small_sum[pos:pos + CONV_WIDTH * D_CONV].reshape(CONV_WIDTH, D_CONV)
    pos += CONV_WIDTH * D_CONV
    full_ffn_w = small_sum[pos:pos + FFN_CONV_WIDTH * 2 * D_FF].reshape(FFN_CONV_WIDTH, 2 * D_FF)
    loss = small_sum[pos + FFN_CONV_WIDTH * 2 * D_FF]
    grads["conv_dw_w"] = lax.dynamic_slice_in_dim(full_conv_w, dev * (D_CONV // N_DEV), D_CONV // N_DEV, axis=1)[None]
    grads["ffn_dw_w"] = lax.dynamic_slice_in_dim(full_ffn_w, dev * (2 * D_FF // N_DEV), 2 * D_FF // N_DEV, axis=1)[None]
    for n in W_OFF:
        grads[n] = (g_mine[n].T if n in transposed else g_mine[n])[None]

    delta, new_m, new_v = {}, {}, {}
    for n, _ in W_ROWS:
        shape = weights[n].shape
        as2d = lambda t: t.reshape(shape[1], shape[2])
        d, nm, nv = _adamw(as2d(weights[n]), as2d(grads[n]), as2d(moments_m[n]), as2d(moments_v[n]), "adamw_" + n)
        delta[n], new_m[n], new_v[n] = d.reshape(shape), nm.reshape(shape), nv.reshape(shape)
    small_names = [n for n in order if n not in W_OFF]
    two_d = lambda t: t.reshape(1, -1) if t.ndim == 1 else t
    outs = _adamw_small(*[[two_d(t[n]) for n in small_names] for t in (weights, grads, moments_m, moments_v)])
    for res, out in zip((delta, new_m, new_v), outs):
        for n, o in zip(small_names, out):
            res[n] = o.reshape(weights[n].shape)

    return (loss, grad_x, *[grads[n] for n in order], *[delta[n] for n in order],
            *[new_m[n] for n in order], *[new_v[n] for n in order])
```

```python
import functools

import jax
import jax.numpy as jnp
from jax import lax
from jax.experimental import pallas as pl
from jax.experimental.pallas import tpu as pltpu

F32 = jnp.float32
BF16 = jnp.bfloat16
MESH = pl.DeviceIdType.MESH

N_DEV = 8
D_MODEL = 1024
D_CONV = 512
D_POOL = 512
CONV_WIDTH = 31
POOL_WINDOWS = (2, 4, 8, 16)
POOL_GROUP_DIM = 128
D_IN = 1536
N_MEM = 256
HEADS = 4
HEAD_DIM = 256
D_FF = 2816
FFN_CONV_WIDTH = 3
EPS = 1e-6
ADAM_LR = 0.001
ADAM_B1 = 0.9
ADAM_B2 = 0.999
ADAM_EPS = 1e-08
ADAM_WD = 0.01
ADAM_STEP = 10

VMEM_LIMIT_V7X = 56 * 1024 * 1024
CONV_HALO = 32
POOL_HALO = 16
FFN_HALO = 8
FFN_CHUNK = 1408

W_ROWS = (("w_in", 192), ("w_out", 128), ("w_q", 128), ("w_kv", 256), ("w_o", 128), ("w_up", 704), ("w_down", 352))
AG_GROUPS = (("w_in", "w_out"), ("w_q", "w_kv", "w_o"), ("w_up", "w_down"))
W_OFF = {}
for _names in AG_GROUPS:
    _o = 0
    for _n in _names:
        W_OFF[_n] = (_o, dict(W_ROWS)[_n])
        _o += dict(W_ROWS)[_n]
RS_GROUPS = {"a": ("w_up", "w_down"), "b": ("w_q", "w_kv", "w_o"), "c": ("w_in", "w_out")}


def _dot(a, b):
    return jnp.dot(a, b, preferred_element_type=F32)


def _dot_nt(a, b):
    return lax.dot_general(a, b, (((1,), (1,)), ((), ())), preferred_element_type=F32)


def _dot_tn(a, b):
    return lax.dot_general(a, b, (((0,), (0,)), ((), ())), preferred_element_type=F32)


def _sigmoid(v):
    return 1.0 / (1.0 + jnp.exp(-v))


def _rms_fwd(v):
    r = lax.rsqrt(jnp.mean(v * v, axis=-1, keepdims=True) + EPS)
    return v * r, r


def _rms_bwd(dh, vh, r, g):
    gd = dh * g
    return r * (gd - vh * jnp.mean(gd * vh, axis=-1, keepdims=True))


def _sublane_shifts(v):
    rows = v.shape[0]
    return [v] + [pltpu.roll(v, rows - b, 0) for b in range(1, 8)]


def _colsum(v):
    return jnp.sum(v, axis=0, keepdims=True)


def _full(shape):
    return pl.BlockSpec(shape, lambda *_: (0,) * len(shape))


def _params(sem=("arbitrary",), vmem=VMEM_LIMIT_V7X):
    return pltpu.CompilerParams(dimension_semantics=sem, vmem_limit_bytes=vmem)


def _load_weight(g_hbm, name, dst, sem):
    off, rows = W_OFF[name]
    return [pltpu.make_async_copy(g_hbm.at[d, pl.ds(off, rows), :], dst.at[pl.ds(d * rows, rows), :], sem)
            for d in range(N_DEV)]


def _position():
    x, y, c = lax.axis_index("x"), lax.axis_index("y"), lax.axis_index("c")
    chips = [(1 - x, y), (x, 1 - y), (1 - x, 1 - y)]
    return x, y, c, chips


def _dev(px, py, pc):
    return 4 * px + 2 * py + pc


def _all_gather(arrs, name):
    n = len(arrs)

    def body(*refs):
        ins, outs = refs[:n], refs[n:2 * n]
        send_sems, recv_sems, local_sems = refs[2 * n:2 * n + 3]
        bounce = refs[2 * n + 3:]
        x, y, c, chips = _position()
        me, sibling = (x, y, c), (x, y, 1 - c)

        def copy(a, k, block, to, src=None):
            rows = outs[a].at[_dev(*block)]
            return pltpu.make_async_remote_copy(
                src_ref=rows if src is None else src, dst_ref=rows,
                send_sem=send_sems.at[a, k], recv_sem=recv_sems.at[a, k], device_id=to, device_id_type=MESH)

        sends = []
        for a in range(n):
            first = [copy(a, 0, me, sibling, src=ins[a])]
            first += [copy(a, 1 + j, me, (*chip, c), src=ins[a]) for j, chip in enumerate(chips)]
            for cp in first:
                cp.start()
            sends += first
        started = []
        for a in range(n):
            load = pltpu.make_async_copy(ins[a], bounce[a], local_sems.at[a, 0])
            load.start()
            load.wait()
            mine = pltpu.make_async_copy(bounce[a], outs[a].at[_dev(*me)], local_sems.at[a, 1])
            mine.start()
            started.append(mine)
        for j, chip in enumerate(chips):
            for a in range(n):
                copy(a, 1 + j, (*chip, c), me).wait_recv()
                passed = copy(a, 4 + j, (*chip, c), sibling)
                passed.start()
                sends.append(passed)
        for a in range(n):
            copy(a, 0, sibling, me).wait_recv()
            for j, chip in enumerate(chips):
                copy(a, 4 + j, (*chip, 1 - c), me).wait_recv()
        for cp in sends:
            cp.wait_send()
        for mine in started:
            mine.wait()

    any_spec = pl.BlockSpec(memory_space=pl.ANY)
    return pl.pallas_call(
        body, name=name,
        out_shape=[jax.ShapeDtypeStruct((N_DEV,) + a.shape, a.dtype) for a in arrs],
        in_specs=[any_spec] * n, out_specs=[any_spec] * n,
        scratch_shapes=[pltpu.SemaphoreType.DMA((n, 7)), pltpu.SemaphoreType.DMA((n, 7)), pltpu.SemaphoreType.DMA((n, 2))]
        + [pltpu.VMEM(a.shape, a.dtype) for a in arrs],
    )(*arrs)


_HBM = pl.BlockSpec(memory_space=pltpu.HBM)
_SEM = pl.BlockSpec(memory_space=pltpu.SEMAPHORE)
_SIDE_EFFECT = pltpu.SideEffectType.DATAFLOW_SIDE_EFFECTING


def _gather_start(buf, after, name):
    def body(buf_ref, after_ref, send_sems, recv_sems, buf_thru, token):
        del after_ref, buf_thru
        x, y, c, chips = _position()
        rows = buf_ref.at[_dev(x, y, c)]
        for k, to in enumerate([(x, y, 1 - c)] + [(*chip, c) for chip in chips]):
            pltpu.make_async_remote_copy(src_ref=rows, dst_ref=rows, send_sem=send_sems.at[k], recv_sem=recv_sems.at[k],
                                         device_id=to, device_id_type=MESH).start()
        token[...] = jnp.zeros_like(token)

    return pl.pallas_call(
        body, name=name,
        out_shape=(pltpu.SemaphoreType.DMA((4,)), pltpu.SemaphoreType.DMA((4,)), pltpu.HBM(buf.shape, buf.dtype),
                   jax.ShapeDtypeStruct((8, 128), F32)),
        in_specs=(_HBM, pl.BlockSpec(memory_space=pl.ANY)),
        out_specs=(_SEM, _SEM, _HBM, pl.BlockSpec(memory_space=pltpu.VMEM)),
        input_output_aliases={0: 2},
        compiler_params=pltpu.CompilerParams(has_side_effects=_SIDE_EFFECT),
    )(pltpu.with_memory_space_constraint(buf, pltpu.HBM), after)


def _gather_forward(send_sems, recv_sems, buf, after, name):
    def body(buf_ref, send_sems, recv_sems, after_ref, fwd_send, fwd_recv, buf_thru):
        del after_ref, buf_thru
        x, y, c, chips = _position()
        sibling = (x, y, 1 - c)

        def copy(block, k, sends, recvs):
            rows = buf_ref.at[_dev(*block)]
            return pltpu.make_async_remote_copy(src_ref=rows, dst_ref=rows, send_sem=sends.at[k], recv_sem=recvs.at[k],
                                                device_id=sibling, device_id_type=MESH)

        for k in range(4):
            copy((x, y, c), k, send_sems, recv_sems).wait_send()
        copy(sibling, 0, send_sems, recv_sems).wait_recv()
        for j, chip in enumerate(chips):
            copy((*chip, c), 1 + j, send_sems, recv_sems).wait_recv()
            copy((*chip, c), j, fwd_send, fwd_recv).start()

    return pl.pallas_call(
        body, name=name,
        out_shape=(pltpu.SemaphoreType.DMA((3,)), pltpu.SemaphoreType.DMA((3,)), pltpu.HBM(buf.shape, buf.dtype)),
        in_specs=(_HBM, _SEM, _SEM, pl.BlockSpec(memory_space=pl.ANY)), out_specs=(_SEM, _SEM, _HBM),
        input_output_aliases={0: 2},
        compiler_params=pltpu.CompilerParams(has_side_effects=_SIDE_EFFECT),
    )(buf, send_sems, recv_sems, after)


def _gather_finish(fwd_send, fwd_recv, buf, name):
    def body(buf_ref, fwd_send, fwd_recv, buf_thru):
        del buf_thru
        x, y, c, chips = _position()
        for j, chip in enumerate(chips):
            cp = pltpu.make_async_remote_copy(
                src_ref=buf_ref.at[_dev(*chip, c)], dst_ref=buf_ref.at[_dev(*chip, 1 - c)], send_sem=fwd_send.at[j],
                recv_sem=fwd_recv.at[j], device_id=(x, y, 1 - c), device_id_type=MESH)
            cp.wait_send()
            cp.wait_recv()

    return pl.pallas_call(
        body, name=name,
        out_shape=pltpu.HBM(buf.shape, buf.dtype),
        in_specs=(_HBM, _SEM, _SEM), out_specs=_HBM,
        input_output_aliases={0: 0},
        compiler_params=pltpu.CompilerParams(has_side_effects=_SIDE_EFFECT),
    )(buf, fwd_send, fwd_recv)


def _everyone_else(x, y, c, chips):
    return [(x, y, 1 - c)] + [(*chip, core) for chip in chips for core in (c, 1 - c)]


def _broadcast_start(buf, name):
    def body(buf_ref, send_sems, recv_sems, buf_thru, token):
        del buf_thru
        x, y, c, chips = _position()
        rows = buf_ref.at[_dev(x, y, c)]
        for k, to in enumerate(_everyone_else(x, y, c, chips)):
            pltpu.make_async_remote_copy(src_ref=rows, dst_ref=rows, send_sem=send_sems.at[k], recv_sem=recv_sems.at[k],
                                         device_id=to, device_id_type=MESH).start()
        token[...] = jnp.zeros_like(token)

    return pl.pallas_call(
        body, name=name,
        out_shape=(pltpu.SemaphoreType.DMA((7,)), pltpu.SemaphoreType.DMA((7,)), pltpu.HBM(buf.shape, buf.dtype),
                   jax.ShapeDtypeStruct((8, 128), F32)),
        in_specs=(_HBM,), out_specs=(_SEM, _SEM, _HBM, pl.BlockSpec(memory_space=pltpu.VMEM)),
        input_output_aliases={0: 2},
        compiler_params=pltpu.CompilerParams(has_side_effects=_SIDE_EFFECT),
    )(pltpu.with_memory_space_constraint(buf, pltpu.HBM))


def _broadcast_wait(send_sems, recv_sems, buf, after, name):
    def body(buf_ref, send_sems, recv_sems, after_ref, buf_thru):
        del after_ref, buf_thru
        x, y, c, chips = _position()
        for k, peer in enumerate(_everyone_else(x, y, c, chips)):
            cp = pltpu.make_async_remote_copy(
                src_ref=buf_ref.at[_dev(x, y, c)], dst_ref=buf_ref.at[_dev(*peer)], send_sem=send_sems.at[k],
                recv_sem=recv_sems.at[k], device_id=peer, device_id_type=MESH)
            cp.wait_send()
            cp.wait_recv()

    return pl.pallas_call(
        body, name=name,
        out_shape=pltpu.HBM(buf.shape, buf.dtype),
        in_specs=(_HBM, _SEM, _SEM, pl.BlockSpec(memory_space=pl.ANY)), out_specs=_HBM,
        input_output_aliases={0: 0},
        compiler_params=pltpu.CompilerParams(has_side_effects=_SIDE_EFFECT),
    )(buf, send_sems, recv_sems, after)


def _exchange_sibling(parts, name):
    n = len(parts)

    def body(*refs):
        ins, outs = refs[:n], refs[n:2 * n]
        send_sems, recv_sems = refs[2 * n:]
        x, y, c, chips = _position()
        sibling = (x, y, 1 - c)
        copies = []
        for k in range(n):
            for j, chip in enumerate([(x, y)] + chips):
                cp = pltpu.make_async_remote_copy(
                    src_ref=ins[k].at[_dev(*chip, 1 - c)], dst_ref=outs[k].at[j],
                    send_sem=send_sems.at[k, j], recv_sem=recv_sems.at[k, j], device_id=sibling, device_id_type=MESH)
                cp.start()
                copies.append(cp)
        for cp in copies:
            cp.wait_recv()
        for cp in copies:
            cp.wait_send()

    any_spec = pl.BlockSpec(memory_space=pl.ANY)
    return pl.pallas_call(
        body, name=name,
        out_shape=[jax.ShapeDtypeStruct((4,) + p.shape[1:], p.dtype) for p in parts],
        in_specs=[any_spec] * n, out_specs=[any_spec] * n,
        scratch_shapes=[pltpu.SemaphoreType.DMA((n, 4)), pltpu.SemaphoreType.DMA((n, 4))],
    )(*parts)


def _chip_exchange_start(sums, name):
    n = len(sums)

    def body(*refs):
        s_refs, land_refs = refs[:n], refs[n:2 * n]
        send_sems, recv_sems = refs[2 * n:2 * n + 2]
        token = refs[-1]
        x, y, c, chips = _position()
        for k in range(n):
            for j, chip in enumerate(chips):
                pltpu.make_async_remote_copy(
                    src_ref=s_refs[k].at[j], dst_ref=land_refs[k].at[j], send_sem=send_sems.at[3 * k + j],
                    recv_sem=recv_sems.at[3 * k + j], device_id=(*chip, c), device_id_type=MESH).start()
        token[...] = jnp.zeros_like(token)

    hbm = pl.BlockSpec(memory_space=pltpu.HBM)
    sem = pl.BlockSpec(memory_space=pltpu.SEMAPHORE)
    thru = [pltpu.HBM(s.shape, s.dtype) for s in sums]
    outs = pl.pallas_call(
        body, name=name,
        out_shape=(pltpu.SemaphoreType.DMA((3 * n,)), pltpu.SemaphoreType.DMA((3 * n,)), *thru, *thru,
                   jax.ShapeDtypeStruct((8, 128), F32)),
        in_specs=[hbm] * (2 * n), out_specs=(sem, sem, *[hbm] * (2 * n), pl.BlockSpec(memory_space=pltpu.VMEM)),
        input_output_aliases={k: 2 + k for k in range(2 * n)},
        compiler_params=pltpu.CompilerParams(has_side_effects=pltpu.SideEffectType.DATAFLOW_SIDE_EFFECTING),
    )(*[pltpu.with_memory_space_constraint(s, pltpu.HBM) for s in sums],
      *[pltpu.with_memory_space_constraint(lax.empty(s.shape, s.dtype), pltpu.HBM) for s in sums])
    return outs[0], outs[1], outs[2:2 + n], outs[2 + n:2 + 2 * n], outs[-1]


def _chip_exchange_wait(send_sems, recv_sems, s_thru, land_thru, after, name):
    n = len(s_thru)

    def body(*refs):
        s_refs, land_refs = refs[:n], refs[n:2 * n]
        send_sems, recv_sems = refs[2 * n:2 * n + 2]
        x, y, c, chips = _position()
        for k in range(n):
            for j, chip in enumerate(chips):
                cp = pltpu.make_async_remote_copy(
                    src_ref=s_refs[k].at[j], dst_ref=land_refs[k].at[j], send_sem=send_sems.at[3 * k + j],
                    recv_sem=recv_sems.at[3 * k + j], device_id=(*chip, c), device_id_type=MESH)
                cp.wait_send()
                cp.wait_recv()

    hbm = pl.BlockSpec(memory_space=pltpu.HBM)
    sem = pl.BlockSpec(memory_space=pltpu.SEMAPHORE)
    thru = [pltpu.HBM(s.shape, s.dtype) for s in s_thru]
    outs = pl.pallas_call(
        body, name=name,
        out_shape=(*thru, *thru),
        in_specs=[hbm] * (2 * n) + [sem, sem, pl.BlockSpec(memory_space=pl.ANY)], out_specs=[hbm] * (2 * n),
        input_output_aliases={k: k for k in range(2 * n)},
        compiler_params=pltpu.CompilerParams(has_side_effects=pltpu.SideEffectType.DATAFLOW_SIDE_EFFECTING),
    )(*s_thru, *land_thru, send_sems, recv_sems, after)
    return outs[n:]


def _owner_table():
    x, y, c = lax.axis_index("x"), lax.axis_index("y"), lax.axis_index("c")
    chips = [(x, y), (1 - x, y), (x, 1 - y), (1 - x, 1 - y)]
    return jnp.stack([_dev(px, py, c) for px, py in chips]).astype(jnp.int32)


def _chip_partial_sums(table, parts, from_sibling, name):
    n = len(parts)

    def body(tab_ref, *refs):
        del tab_ref
        for g_ref, l_ref, out_ref in zip(refs[:n], refs[n:2 * n], refs[2 * n:]):
            out_ref[...] = (g_ref[...].astype(F32) + l_ref[...].astype(F32)).astype(out_ref.dtype)

    block = lambda p: (None,) + p.shape[1:]
    grid_spec = pltpu.PrefetchScalarGridSpec(
        num_scalar_prefetch=1, grid=(3,),
        in_specs=[pl.BlockSpec(block(p), lambda j, tab: (tab[j + 1], 0, 0)) for p in parts]
        + [pl.BlockSpec(block(p), lambda j, tab: (j + 1, 0, 0)) for p in parts],
        out_specs=[pl.BlockSpec(block(p), lambda j, tab: (j, 0, 0)) for p in parts])
    return pl.pallas_call(
        body, name=name, grid_spec=grid_spec,
        out_shape=[jax.ShapeDtypeStruct((3,) + p.shape[1:], BF16) for p in parts],
        compiler_params=_params(("arbitrary",)),
    )(table, *parts, *from_sibling)


def _final_grad_sums(table, parts, from_sibling, from_chips, name):
    n = len(parts)

    def body(tab_ref, *refs):
        del tab_ref
        for g_ref, l_ref, c_ref, out_ref in zip(refs[:n], refs[n:2 * n], refs[2 * n:3 * n], refs[3 * n:]):
            acc = g_ref[...].astype(F32) + l_ref[...].astype(F32)
            for j in range(3):
                acc = acc + c_ref[j].astype(F32)
            out_ref[...] = acc

    half = lambda p: (p.shape[1] // 2, p.shape[2])
    grid_spec = pltpu.PrefetchScalarGridSpec(
        num_scalar_prefetch=1, grid=(2,),
        in_specs=[pl.BlockSpec((None,) + half(p), lambda t, tab: (tab[0], t, 0)) for p in parts]
        + [pl.BlockSpec((None,) + half(p), lambda t, tab: (0, t, 0)) for p in parts]
        + [pl.BlockSpec((3,) + half(p), lambda t, tab: (0, t, 0)) for p in parts],
        out_specs=[pl.BlockSpec(half(p), lambda t, tab: (t, 0)) for p in parts])
    return pl.pallas_call(
        body, name=name, grid_spec=grid_spec,
        out_shape=[jax.ShapeDtypeStruct(p.shape[1:], F32) for p in parts],
        compiler_params=_params(("arbitrary",)),
    )(table, *parts, *from_sibling, *from_chips)


def _sum_blocks(g8):
    _, rows, cols = g8.shape

    def body(g_ref, out_ref):
        acc = g_ref[0]
        for d in range(1, N_DEV):
            acc = acc + g_ref[d]
        out_ref[...] = acc

    return pl.pallas_call(
        body, name="small_grad_sum", grid=(1,),
        in_specs=[_full((N_DEV, rows, cols))], out_specs=_full((rows, cols)),
        out_shape=jax.ShapeDtypeStruct((rows, cols), F32),
        compiler_params=_params(("arbitrary",)),
    )(g8)


def _fwd_mix(x2d, gw, g_mix, conv_w, conv_b, ln_g, ln_b, pool_w, pool_scale, after, seq, tm):
    tokens = x2d.shape[0]
    n_tiles = tokens // tm
    tps = seq // tm

    def body(x_ref, gmix_ref, gw_hbm, cw_ref, cb_ref, lng_ref, lnb_ref, pw_ref, ps_ref, after_ref,
             x1_ref, u_ref, c_ref, pooled_ref, ymix_ref, h1_ref,
             win_v, wout_v, hc_carry, up_carry, sem):
        del after_ref
        i = pl.program_id(0)

        @pl.when(i == 0)
        def _():
            copies = _load_weight(gw_hbm, "w_in", win_v, sem) + _load_weight(gw_hbm, "w_out", wout_v, sem)
            for cp in copies:
                cp.start()
            for cp in copies:
                cp.wait()

        @pl.when(i % tps == 0)
        def _():
            hc_carry[...] = jnp.zeros_like(hc_carry)
            up_carry[...] = jnp.zeros_like(up_carry)

        x = x_ref[...]
        xh, _ = _rms_fwd(x)
        h1 = (xh * gmix_ref[...]).astype(BF16)
        h1_ref[...] = h1
        u = _dot_nt(h1, win_v[...])
        u_ref[...] = u
        val, gate, up = u[:, :D_CONV], u[:, D_CONV:2 * D_CONV], u[:, 2 * D_CONV:]

        hc = val * _sigmoid(gate)
        ext = jnp.concatenate([hc_carry[...], hc], axis=0)
        hc_carry[...] = hc[tm - CONV_HALO:, :]
        conv = jnp.broadcast_to(cb_ref[...], (tm, D_CONV))
        ahead_by = _sublane_shifts(ext)
        for k in range(CONV_WIDTH):
            whole, part = divmod(CONV_HALO - (CONV_WIDTH - 1) + k, 8)
            conv = conv + cw_ref[k:k + 1, :] * ahead_by[part][8 * whole:8 * whole + tm, :]
        c_ref[...] = conv
        mu = jnp.mean(conv, axis=-1, keepdims=True)
        cen = conv - mu
        ln = cen * lax.rsqrt(jnp.mean(cen * cen, axis=-1, keepdims=True) + EPS) * lng_ref[...] + lnb_ref[...]
        y_conv = ln * _sigmoid(ln)

        extp = jnp.concatenate([up_carry[...], up], axis=0)
        up_carry[...] = up[tm - POOL_HALO:, :]
        pos = lax.broadcasted_iota(jnp.int32, (tm, 1), 0) + (i % tps) * tm
        run = extp
        mixed = []
        for g, w in enumerate(POOL_WINDOWS):
            lo = g * POOL_GROUP_DIM
            run = run[:, POOL_GROUP_DIM if g else 0:]
            run = run + pltpu.roll(run, w // 2, 0)
            cnt = jnp.minimum(pos + 1, w).astype(F32)
            pooled = run[POOL_HALO:, :POOL_GROUP_DIM] / cnt - up[:, lo:lo + POOL_GROUP_DIM]
            pooled = pooled.astype(BF16)
            pooled_ref[:, lo:lo + POOL_GROUP_DIM] = pooled
            mixed.append(_dot(pooled, pw_ref[g].astype(BF16)))
        y_pool = jnp.concatenate(mixed, axis=-1) * ps_ref[...]

        ymix = jnp.concatenate([y_conv, y_pool], axis=-1).astype(BF16)
        ymix_ref[...] = ymix
        x1_ref[...] = x + _dot(ymix, wout_v[...])

    row = lambda w: pl.BlockSpec((tm, w), lambda i: (i, 0))
    return pl.pallas_call(
        body, name="fwd_mix", grid=(n_tiles,),
        in_specs=[row(D_MODEL), _full((1, D_MODEL)), pl.BlockSpec(memory_space=pl.ANY),
                  _full((CONV_WIDTH, D_CONV)), _full((1, D_CONV)), _full((1, D_CONV)), _full((1, D_CONV)),
                  _full((4, POOL_GROUP_DIM, POOL_GROUP_DIM)), _full((1, D_POOL)), _full(after.shape)],
        out_specs=[row(D_MODEL), row(D_IN), row(D_CONV), row(D_POOL), row(D_MODEL), row(D_MODEL)],
        out_shape=[jax.ShapeDtypeStruct((tokens, D_MODEL), F32), jax.ShapeDtypeStruct((tokens, D_IN), F32),
                   jax.ShapeDtypeStruct((tokens, D_CONV), F32), jax.ShapeDtypeStruct((tokens, D_POOL), BF16),
                   jax.ShapeDtypeStruct((tokens, D_MODEL), BF16), jax.ShapeDtypeStruct((tokens, D_MODEL), BF16)],
        scratch_shapes=[pltpu.VMEM((D_IN, D_MODEL), BF16), pltpu.VMEM((D_MODEL, D_MODEL), BF16),
                        pltpu.VMEM((CONV_HALO, D_CONV), F32), pltpu.VMEM((POOL_HALO, D_POOL), F32),
                        pltpu.SemaphoreType.DMA],
        compiler_params=_params(),
    )(x2d, g_mix, gw, conv_w, conv_b, ln_g, ln_b, pool_w, pool_scale, after)


def _fwd_kv(mem2d, gw, g_mem):
    rows = mem2d.shape[0]
    n_b = rows // N_MEM

    def body(mem_ref, g_ref, gw_hbm, mn_ref, kv_ref, wkv_v, sem):
        @pl.when(pl.program_id(0) == 0)
        def _():
            copies = _load_weight(gw_hbm, "w_kv", wkv_v, sem)
            for cp in copies:
                cp.start()
            for cp in copies:
                cp.wait()

        mh, _ = _rms_fwd(mem_ref[...])
        mn = (mh * g_ref[...]).astype(BF16)
        mn_ref[...] = mn
        kv_ref[...] = _dot_nt(mn, wkv_v[...]).astype(BF16)

    return pl.pallas_call(
        body, name="fwd_kv", grid=(n_b,),
        in_specs=[pl.BlockSpec((N_MEM, D_MODEL), lambda b: (b, 0)), _full((1, D_MODEL)), pl.BlockSpec(memory_space=pl.ANY)],
        out_specs=[pl.BlockSpec((N_MEM, D_MODEL), lambda b: (b, 0)), pl.BlockSpec((N_MEM, 2 * D_MODEL), lambda b: (b, 0))],
        out_shape=[jax.ShapeDtypeStruct((rows, D_MODEL), BF16), jax.ShapeDtypeStruct((rows, 2 * D_MODEL), BF16)],
        scratch_shapes=[pltpu.VMEM((2 * D_MODEL, D_MODEL), BF16), pltpu.SemaphoreType.DMA],
        compiler_params=_params(),
    )(mem2d, g_mem, gw)


def _softmax_rows(s):
    e = jnp.exp(s - jnp.max(s, axis=-1, keepdims=True))
    return e / jnp.sum(e, axis=-1, keepdims=True)


def _fwd_attn(x1, kv, gw, g_x, seq, tm):
    tokens = x1.shape[0]
    n_tiles = tokens // tm
    tps = seq // tm

    def body(x1_ref, kv_ref, g_ref, gw_hbm, x2_ref, h2_ref, q_ref, o_ref, wq_v, wo_v, sem):
        @pl.when(pl.program_id(0) == 0)
        def _():
            copies = _load_weight(gw_hbm, "w_q", wq_v, sem) + _load_weight(gw_hbm, "w_o", wo_v, sem)
            for cp in copies:
                cp.start()
            for cp in copies:
                cp.wait()

        x1v = x1_ref[...]
        xh, _ = _rms_fwd(x1v)
        h2 = (xh * g_ref[...]).astype(BF16)
        h2_ref[...] = h2
        q = (_dot(h2, wq_v[...]) * (HEAD_DIM ** -0.5)).astype(BF16)
        q_ref[...] = q
        outs = []
        for h in range(HEADS):
            lo = h * HEAD_DIM
            p = _softmax_rows(_dot_nt(q[:, lo:lo + HEAD_DIM], kv_ref[:, lo:lo + HEAD_DIM]))
            outs.append(_dot(p.astype(BF16), kv_ref[:, D_MODEL + lo:D_MODEL + lo + HEAD_DIM]))
        o = jnp.concatenate(outs, axis=-1).astype(BF16)
        o_ref[...] = o
        x2_ref[...] = x1v + _dot(o, wo_v[...])

    row = lambda w: pl.BlockSpec((tm, w), lambda i: (i, 0))
    return pl.pallas_call(
        body, name="fwd_attn", grid=(n_tiles,),
        in_specs=[row(D_MODEL), pl.BlockSpec((N_MEM, 2 * D_MODEL), lambda i: (i // tps, 0)), _full((1, D_MODEL)),
                  pl.BlockSpec(memory_space=pl.ANY)],
        out_specs=[row(D_MODEL)] * 4,
        out_shape=[jax.ShapeDtypeStruct((tokens, D_MODEL), F32)] + [jax.ShapeDtypeStruct((tokens, D_MODEL), BF16)] * 3,
        scratch_shapes=[pltpu.VMEM((D_MODEL, D_MODEL), BF16), pltpu.VMEM((D_MODEL, D_MODEL), BF16), pltpu.SemaphoreType.DMA],
        compiler_params=_params(),
    )(x1, kv, g_x, gw)


def _ffn_conv(uu, halo, w_ref, b_ref, cols):
    ext = jnp.concatenate([halo, uu], axis=0)
    p1 = pltpu.roll(ext, 1, 0)[FFN_HALO:, :]
    p2 = pltpu.roll(ext, 2, 0)[FFN_HALO:, :]
    return b_ref[:, cols] + w_ref[2:3, cols] * uu + w_ref[1:2, cols] * p1 + w_ref[0:1, cols] * p2


def _fwd_ffn(x2, target, gw, g_ffn, ffn_w, ffn_b, g_final, seq, tm):
    tokens = x2.shape[0]
    n_tiles = tokens // tm
    tps = seq // tm
    n_chunks = D_FF // FFN_CHUNK

    def body(x2_ref, tgt_ref, gffn_ref, gw_hbm, fw_ref, fb_ref, gfin_ref,
             uu_ref, a_ref, h3_ref, dx3_ref, dx3b_ref, loss_ref, dgfin_ref,
             wup_v, wdown_v, carry, sem):
        i = pl.program_id(0)

        @pl.when(i == 0)
        def _():
            copies = _load_weight(gw_hbm, "w_up", wup_v, sem) + _load_weight(gw_hbm, "w_down", wdown_v, sem)
            for cp in copies:
                cp.start()
            for cp in copies:
                cp.wait()
            loss_ref[...] = jnp.zeros_like(loss_ref)
            dgfin_ref[...] = jnp.zeros_like(dgfin_ref)

        @pl.when(i % tps == 0)
        def _():
            carry[...] = jnp.zeros_like(carry)

        x2v = x2_ref[...]
        xh, _ = _rms_fwd(x2v)
        h3 = (xh * gffn_ref[...]).astype(BF16)
        h3_ref[...] = h3
        acc = jnp.zeros((tm, D_MODEL), F32)
        for jc in range(n_chunks):
            halves = []
            for half in range(2):
                cols = pl.ds(half * D_FF + jc * FFN_CHUNK, FFN_CHUNK)
                uu = _dot_nt(h3, wup_v[cols, :])
                uu_ref[:, cols] = uu
                halves.append(_ffn_conv(uu, carry[:, cols], fw_ref, fb_ref, cols))
                carry[:, cols] = uu[tm - FFN_HALO:, :]
            gate, val = halves
            a = (gate * _sigmoid(gate) * val).astype(BF16)
            a_ref[:, pl.ds(jc * FFN_CHUNK, FFN_CHUNK)] = a
            acc = acc + _dot(a, wdown_v[pl.ds(jc * FFN_CHUNK, FFN_CHUNK), :])
        x3 = x2v + acc

        xh3, r3 = _rms_fwd(x3)
        gfin = gfin_ref[...]
        err = xh3 * gfin - tgt_ref[...]
        loss_ref[...] += jnp.full(loss_ref.shape, jnp.sum(err * err) * (0.5 / D_MODEL), F32)
        dy = err * (1.0 / D_MODEL)
        dgfin_ref[...] += _colsum(dy * xh3)
        dx3 = _rms_bwd(dy, xh3, r3, gfin)
        dx3_ref[...] = dx3
        dx3b_ref[...] = dx3.astype(BF16)

    row = lambda w: pl.BlockSpec((tm, w), lambda i: (i, 0))
    return pl.pallas_call(
        body, name="fwd_ffn", grid=(n_tiles,),
        in_specs=[row(D_MODEL), row(D_MODEL), _full((1, D_MODEL)), pl.BlockSpec(memory_space=pl.ANY),
                  _full((FFN_CONV_WIDTH, 2 * D_FF)), _full((1, 2 * D_FF)), _full((1, D_MODEL))],
        out_specs=[row(2 * D_FF), row(D_FF), row(D_MODEL), row(D_MODEL), row(D_MODEL), _full((8, 128)), _full((1, D_MODEL))],
        out_shape=[jax.ShapeDtypeStruct((tokens, 2 * D_FF), F32), jax.ShapeDtypeStruct((tokens, D_FF), BF16),
                   jax.ShapeDtypeStruct((tokens, D_MODEL), BF16), jax.ShapeDtypeStruct((tokens, D_MODEL), F32),
                   jax.ShapeDtypeStruct((tokens, D_MODEL), BF16),
                   jax.ShapeDtypeStruct((8, 128), F32), jax.ShapeDtypeStruct((1, D_MODEL), F32)],
        scratch_shapes=[pltpu.VMEM((2 * D_FF, D_MODEL), BF16), pltpu.VMEM((D_FF, D_MODEL), BF16),
                        pltpu.VMEM((FFN_HALO, 2 * D_FF), F32), pltpu.SemaphoreType.DMA],
        compiler_params=_params(),
    )(x2, target, g_ffn, gw, ffn_w, ffn_b, g_final)


def _bwd_ffn(dx3, x2, uu_all, gw, g_ffn, ffn_w, ffn_b, seq, tm):
    tokens = x2.shape[0]
    n_tiles = tokens // tm
    tps = seq // tm
    n_chunks = D_FF // FFN_CHUNK
    per8 = tm // FFN_HALO

    def body(dx3_ref, x2_ref, uu_ref, prev_ref, gffn_ref, gw_hbm, fw_ref, fb_ref,
             dx2_ref, dx2b_ref, duu_ref, dfb_ref, dfw_ref, dg_ref,
             wup_v, wdown_v, carry, sem):
        i = pl.program_id(0)
        t = n_tiles - 1 - i

        @pl.when(i == 0)
        def _():
            copies = _load_weight(gw_hbm, "w_up", wup_v, sem) + _load_weight(gw_hbm, "w_down", wdown_v, sem)
            for cp in copies:
                cp.start()
            for cp in copies:
                cp.wait()
            dfb_ref[...] = jnp.zeros_like(dfb_ref)
            dfw_ref[...] = jnp.zeros_like(dfw_ref)
            dg_ref[...] = jnp.zeros_like(dg_ref)

        @pl.when(t % tps == tps - 1)
        def _():
            carry[...] = jnp.zeros_like(carry)

        starts_sequence = (t % tps == 0)
        dx3v = dx3_ref[...]
        dx3b = dx3v.astype(BF16)
        dh3 = jnp.zeros((tm, D_MODEL), F32)
        for jc in range(n_chunks):
            da = _dot_nt(dx3b, wdown_v[pl.ds(jc * FFN_CHUNK, FFN_CHUNK), :])
            uus, ccs, colss = [], [], []
            for half in range(2):
                cols = pl.ds(half * D_FF + jc * FFN_CHUNK, FFN_CHUNK)
                uu = uu_ref[:, cols]
                halo = jnp.where(starts_sequence, 0.0, prev_ref[:, cols])
                uus.append(uu)
                colss.append(cols)
                ccs.append(_ffn_conv(uu, halo, fw_ref, fb_ref, cols))
            gate, val = ccs
            sg = _sigmoid(gate)
            dgate = da * val * (sg * (1.0 + gate * (1.0 - sg)))
            dval = da * (gate * sg)
            for dcc, uu, cols in zip((dgate, dval), uus, colss):
                dfb_ref[:, cols] += _colsum(dcc)
                ext = jnp.concatenate([dcc, carry[:, cols]], axis=0)
                carry[:, cols] = dcc[:FFN_HALO, :]
                n1 = pltpu.roll(ext, tm + FFN_HALO - 1, 0)[:tm, :]
                n2 = pltpu.roll(ext, tm + FFN_HALO - 2, 0)[:tm, :]
                duu = fw_ref[2:3, cols] * dcc + fw_ref[1:2, cols] * n1 + fw_ref[0:1, cols] * n2
                dfw_ref[2:3, cols] += _colsum(uu * dcc)
                dfw_ref[1:2, cols] += _colsum(uu * n1)
                dfw_ref[0:1, cols] += _colsum(uu * n2)
                duub = duu.astype(BF16)
                duu_ref[:, cols] = duub
                dh3 = dh3 + _dot(duub, wup_v[cols, :])
        xh, r = _rms_fwd(x2_ref[...])
        dg_ref[...] += _colsum(dh3 * xh)
        dx2 = dx3v + _rms_bwd(dh3, xh, r, gffn_ref[...])
        dx2_ref[...] = dx2
        dx2b_ref[...] = dx2.astype(BF16)

    rev = lambda w: pl.BlockSpec((tm, w), lambda i: (n_tiles - 1 - i, 0))
    prev = pl.BlockSpec((FFN_HALO, 2 * D_FF), lambda i: (jnp.maximum((n_tiles - 1 - i) * per8 - 1, 0), 0))
    return pl.pallas_call(
        body, name="bwd_ffn", grid=(n_tiles,),
        in_specs=[rev(D_MODEL), rev(D_MODEL), rev(2 * D_FF), prev, _full((1, D_MODEL)), pl.BlockSpec(memory_space=pl.ANY),
                  _full((FFN_CONV_WIDTH, 2 * D_FF)), _full((1, 2 * D_FF))],
        out_specs=[rev(D_MODEL), rev(D_MODEL), rev(2 * D_FF), _full((1, 2 * D_FF)), _full((FFN_CONV_WIDTH, 2 * D_FF)),
                   _full((1, D_MODEL))],
        out_shape=[jax.ShapeDtypeStruct((tokens, D_MODEL), F32), jax.ShapeDtypeStruct((tokens, D_MODEL), BF16),
                   jax.ShapeDtypeStruct((tokens, 2 * D_FF), BF16),
                   jax.ShapeDtypeStruct((1, 2 * D_FF), F32), jax.ShapeDtypeStruct((FFN_CONV_WIDTH, 2 * D_FF), F32),
                   jax.ShapeDtypeStruct((1, D_MODEL), F32)],
        scratch_shapes=[pltpu.VMEM((2 * D_FF, D_MODEL), BF16), pltpu.VMEM((D_FF, D_MODEL), BF16),
                        pltpu.VMEM((FFN_HALO, 2 * D_FF), F32), pltpu.SemaphoreType.DMA],
        compiler_params=_params(),
    )(dx3, x2, uu_all, uu_all, g_ffn, gw, ffn_w, ffn_b)


def _bwd_attn(dx2, x1, q, kv, gw, g_x, after, seq, tm):
    tokens = x1.shape[0]
    n_tiles = tokens // tm
    tps = seq // tm
    n_b = tokens // seq

    def body(dx2_ref, x1_ref, q_ref, kv_ref, g_ref, gw_hbm, after_ref, dx1_ref, dx1b_ref, dq_ref, dkv_ref, dg_ref,
             wq_v, wo_v, sem):
        del after_ref
        i = pl.program_id(0)

        @pl.when(i == 0)
        def _():
            copies = _load_weight(gw_hbm, "w_q", wq_v, sem) + _load_weight(gw_hbm, "w_o", wo_v, sem)
            for cp in copies:
                cp.start()
            for cp in copies:
                cp.wait()
            dg_ref[...] = jnp.zeros_like(dg_ref)

        @pl.when(i % tps == 0)
        def _():
            dkv_ref[...] = jnp.zeros_like(dkv_ref)

        dx2v = dx2_ref[...]
        do = _dot_nt(dx2v.astype(BF16), wo_v[...]).astype(BF16)
        q = q_ref[...]
        dqs = []
        for h in range(HEADS):
            lo = h * HEAD_DIM
            kcols, vcols = pl.ds(lo, HEAD_DIM), pl.ds(D_MODEL + lo, HEAD_DIM)
            qh, doh = q[:, lo:lo + HEAD_DIM], do[:, lo:lo + HEAD_DIM]
            p = _softmax_rows(_dot_nt(qh, kv_ref[:, kcols]))
            dp = _dot_nt(doh, kv_ref[:, vcols])
            dkv_ref[:, vcols] += _dot_tn(p.astype(BF16), doh)
            ds = (p * (dp - jnp.sum(dp * p, axis=-1, keepdims=True))).astype(BF16)
            dqs.append(_dot(ds, kv_ref[:, kcols]) * (HEAD_DIM ** -0.5))
            dkv_ref[:, kcols] += _dot_tn(ds, qh)
        dq = jnp.concatenate(dqs, axis=-1).astype(BF16)
        dq_ref[...] = dq
        dh2 = _dot_nt(dq, wq_v[...])
        xh, r = _rms_fwd(x1_ref[...])
        dg_ref[...] += _colsum(dh2 * xh)
        dx1 = dx2v + _rms_bwd(dh2, xh, r, g_ref[...])
        dx1_ref[...] = dx1
        dx1b_ref[...] = dx1.astype(BF16)

    row = lambda w: pl.BlockSpec((tm, w), lambda i: (i, 0))
    per_b = pl.BlockSpec((N_MEM, 2 * D_MODEL), lambda i: (i // tps, 0))
    return pl.pallas_call(
        body, name="bwd_attn", grid=(n_tiles,),
        in_specs=[row(D_MODEL), row(D_MODEL), row(D_MODEL), per_b, _full((1, D_MODEL)), pl.BlockSpec(memory_space=pl.ANY),
                  _full(after.shape)],
        out_specs=[row(D_MODEL), row(D_MODEL), row(D_MODEL), per_b, _full((1, D_MODEL))],
        out_shape=[jax.ShapeDtypeStruct((tokens, D_MODEL), F32), jax.ShapeDtypeStruct((tokens, D_MODEL), BF16),
                   jax.ShapeDtypeStruct((tokens, D_MODEL), BF16),
                   jax.ShapeDtypeStruct((n_b * N_MEM, 2 * D_MODEL), F32), jax.ShapeDtypeStruct((1, D_MODEL), F32)],
        scratch_shapes=[pltpu.VMEM((D_MODEL, D_MODEL), BF16), pltpu.VMEM((D_MODEL, D_MODEL), BF16), pltpu.SemaphoreType.DMA],
        compiler_params=_params(),
    )(dx2, x1, q, kv, g_x, gw, after)


def _bwd_kv(dkv, mem2d, gw, g_mem):
    rows = mem2d.shape[0]
    n_b = rows // N_MEM

    def body(dkv_ref, mem_ref, gw_hbm, dkvb_ref, dg_ref, wkv_v, sem):
        @pl.when(pl.program_id(0) == 0)
        def _():
            copies = _load_weight(gw_hbm, "w_kv", wkv_v, sem)
            for cp in copies:
                cp.start()
            for cp in copies:
                cp.wait()
            dg_ref[...] = jnp.zeros_like(dg_ref)

        dkvb = dkv_ref[...].astype(BF16)
        dkvb_ref[...] = dkvb
        dmn = _dot(dkvb, wkv_v[...])
        mh, _ = _rms_fwd(mem_ref[...])
        dg_ref[...] += _colsum(dmn * mh)

    del g_mem
    return pl.pallas_call(
        body, name="bwd_kv", grid=(n_b,),
        in_specs=[pl.BlockSpec((N_MEM, 2 * D_MODEL), lambda b: (b, 0)), pl.BlockSpec((N_MEM, D_MODEL), lambda b: (b, 0)),
                  pl.BlockSpec(memory_space=pl.ANY)],
        out_specs=[pl.BlockSpec((N_MEM, 2 * D_MODEL), lambda b: (b, 0)), _full((1, D_MODEL))],
        out_shape=[jax.ShapeDtypeStruct((rows, 2 * D_MODEL), BF16), jax.ShapeDtypeStruct((1, D_MODEL), F32)],
        scratch_shapes=[pltpu.VMEM((2 * D_MODEL, D_MODEL), BF16), pltpu.SemaphoreType.DMA],
        compiler_params=_params(),
    )(dkv, mem2d, gw)


def _bwd_mix(dx1, x2d, u_all, c_all, pooled_all, gw, g_mix, conv_w, ln_g, ln_b, pool_w, pool_scale, after, seq, tm):
    tokens = x2d.shape[0]
    n_tiles = tokens // tm
    tps = seq // tm

    def body(dx1_ref, x_ref, u_ref, c_ref, pooled_ref, gmix_ref, gw_hbm, cw_ref, lng_ref, lnb_ref, pw_ref, ps_ref,
             after_ref, dx_ref, du_ref, dgmix_ref, dcw_ref, dcb_ref, dlng_ref, dlnb_ref, dpw_ref, dps_ref,
             win_v, wout_v, dc_carry, e_carry, sem):
        del after_ref
        i = pl.program_id(0)
        t = n_tiles - 1 - i

        @pl.when(i == 0)
        def _():
            copies = _load_weight(gw_hbm, "w_in", win_v, sem) + _load_weight(gw_hbm, "w_out", wout_v, sem)
            for cp in copies:
                cp.start()
            for cp in copies:
                cp.wait()
            for ref in (dgmix_ref, dcw_ref, dcb_ref, dlng_ref, dlnb_ref, dpw_ref, dps_ref):
                ref[...] = jnp.zeros_like(ref)

        @pl.when(t % tps == tps - 1)
        def _():
            dc_carry[...] = jnp.zeros_like(dc_carry)
            e_carry[...] = jnp.zeros_like(e_carry)

        dx1v = dx1_ref[...]
        dymix = _dot_nt(dx1v.astype(BF16), wout_v[...])
        dyc, dyp = dymix[:, :D_CONV], dymix[:, D_CONV:]
        u = u_ref[...]
        val, gate = u[:, :D_CONV], u[:, D_CONV:2 * D_CONV]

        conv = c_ref[...]
        mu = jnp.mean(conv, axis=-1, keepdims=True)
        cen = conv - mu
        rs = lax.rsqrt(jnp.mean(cen * cen, axis=-1, keepdims=True) + EPS)
        chat = cen * rs
        ln = chat * lng_ref[...] + lnb_ref[...]
        sl = _sigmoid(ln)
        dln = dyc * (sl * (1.0 + ln * (1.0 - sl)))
        dlng_ref[...] += _colsum(dln * chat)
        dlnb_ref[...] += _colsum(dln)
        dchat = dln * lng_ref[...]
        dc = rs * (dchat - jnp.mean(dchat, axis=-1, keepdims=True)
                   - chat * jnp.mean(dchat * chat, axis=-1, keepdims=True))
        dcb_ref[...] += _colsum(dc)
        sg = _sigmoid(gate)
        hc = val * sg
        ext = jnp.concatenate([dc, dc_carry[...]], axis=0)
        dc_carry[...] = dc[:CONV_HALO, :]
        dhc = jnp.zeros((tm, D_CONV), F32)
        ahead_by = _sublane_shifts(ext)
        for k in range(CONV_WIDTH):
            whole, part = divmod(CONV_WIDTH - 1 - k, 8)
            tap = ahead_by[part][8 * whole:8 * whole + tm, :]
            dhc = dhc + cw_ref[k:k + 1, :] * tap
            dcw_ref[k:k + 1, :] += _colsum(hc * tap)
        du_ref[:, :D_CONV] = (dhc * sg).astype(BF16)
        du_ref[:, D_CONV:2 * D_CONV] = (dhc * val * (sg * (1.0 - sg))).astype(BF16)

        pos = lax.broadcasted_iota(jnp.int32, (tm, 1), 0) + (t % tps) * tm
        es, dpooled = [], []
        for g, w in enumerate(POOL_WINDOWS):
            cols = pl.ds(g * POOL_GROUP_DIM, POOL_GROUP_DIM)
            lo = g * POOL_GROUP_DIM
            pooled = pooled_ref[:, cols]
            pw = pw_ref[g].astype(BF16)
            dyg = dyp[:, lo:lo + POOL_GROUP_DIM]
            dps_ref[:, cols] += _colsum(dyg * _dot(pooled, pw))
            dmixed = (dyg * ps_ref[:, cols]).astype(BF16)
            dpw_ref[g] += _dot_tn(pooled, dmixed)
            dpo = _dot_nt(dmixed, pw)
            dpooled.append(dpo)
            es.append(dpo / jnp.minimum(pos + 1, w).astype(F32))
        e = jnp.concatenate(es, axis=-1)
        run = jnp.concatenate([e, e_carry[...]], axis=0)
        e_carry[...] = e[:POOL_HALO, :]
        rows = tm + POOL_HALO
        for g, w in enumerate(POOL_WINDOWS):
            lo = g * POOL_GROUP_DIM
            run = run[:, POOL_GROUP_DIM if g else 0:]
            run = run + pltpu.roll(run, rows - w // 2, 0)
            du_ref[:, 2 * D_CONV + lo:2 * D_CONV + lo + POOL_GROUP_DIM] = (
                run[:tm, :POOL_GROUP_DIM] - dpooled[g]).astype(BF16)

        dh1 = _dot(du_ref[...], win_v[...])
        xh, r = _rms_fwd(x_ref[...])
        dgmix_ref[...] += _colsum(dh1 * xh)
        dx_ref[...] = dx1v + _rms_bwd(dh1, xh, r, gmix_ref[...])

    rev = lambda w: pl.BlockSpec((tm, w), lambda i: (n_tiles - 1 - i, 0))
    return pl.pallas_call(
        body, name="bwd_mix", grid=(n_tiles,),
        in_specs=[rev(D_MODEL), rev(D_MODEL), rev(D_IN), rev(D_CONV), rev(D_POOL), _full((1, D_MODEL)),
                  pl.BlockSpec(memory_space=pl.ANY), _full((CONV_WIDTH, D_CONV)), _full((1, D_CONV)), _full((1, D_CONV)),
                  _full((4, POOL_GROUP_DIM, POOL_GROUP_DIM)), _full((1, D_POOL)), _full(after.shape)],
        out_specs=[rev(D_MODEL), rev(D_IN), _full((1, D_MODEL)), _full((CONV_WIDTH, D_CONV)), _full((1, D_CONV)),
                   _full((1, D_CONV)), _full((1, D_CONV)), _full((4, POOL_GROUP_DIM, POOL_GROUP_DIM)), _full((1, D_POOL))],
        out_shape=[jax.ShapeDtypeStruct((tokens, D_MODEL), F32), jax.ShapeDtypeStruct((tokens, D_IN), BF16),
                   jax.ShapeDtypeStruct((1, D_MODEL), F32), jax.ShapeDtypeStruct((CONV_WIDTH, D_CONV), F32),
                   jax.ShapeDtypeStruct((1, D_CONV), F32), jax.ShapeDtypeStruct((1, D_CONV), F32),
                   jax.ShapeDtypeStruct((1, D_CONV), F32),
                   jax.ShapeDtypeStruct((4, POOL_GROUP_DIM, POOL_GROUP_DIM), F32), jax.ShapeDtypeStruct((1, D_POOL), F32)],
        scratch_shapes=[pltpu.VMEM((D_IN, D_MODEL), BF16), pltpu.VMEM((D_MODEL, D_MODEL), BF16),
                        pltpu.VMEM((CONV_HALO, D_CONV), F32), pltpu.VMEM((POOL_HALO, D_POOL), F32),
                        pltpu.SemaphoreType.DMA],
        compiler_params=_params(),
    )(dx1, x2d, u_all, c_all, pooled_all, g_mix, gw, conv_w, ln_g, ln_b, pool_w, pool_scale, after)


def _wgrad(a, b, name, after=None, tm=256):
    tokens, m = a.shape
    n = b.shape[1]
    extra = [] if after is None else [after]

    def body(a_ref, b_ref, *rest):
        rest[-1][...] = _dot_tn(a_ref[...], b_ref[...]).astype(rest[-1].dtype)

    return pl.pallas_call(
        body, name=name, grid=(m // tm,),
        in_specs=[pl.BlockSpec((tokens, tm), lambda i: (0, i)), _full((tokens, n))] + [_full(t.shape) for t in extra],
        out_specs=pl.BlockSpec((tm, n), lambda i: (i, 0)),
        out_shape=jax.ShapeDtypeStruct((m, n), BF16),
        compiler_params=_params(),
    )(a, b, *extra)


def _adamw_update(w, g, m, v):
    nm = ADAM_B1 * m + (1.0 - ADAM_B1) * g
    nv = ADAM_B2 * v + (1.0 - ADAM_B2) * (g * g)
    m_hat = nm / (1.0 - ADAM_B1 ** ADAM_STEP)
    v_hat = nv / (1.0 - ADAM_B2 ** ADAM_STEP)
    return -ADAM_LR * (m_hat / (jnp.sqrt(v_hat) + ADAM_EPS) + ADAM_WD * w), nm, nv


def _adamw_small(ws, gs, ms, vs):
    n = len(ws)

    def body(*refs):
        ins, outs = refs[:4 * n], refs[4 * n:]
        for k in range(n):
            d, nm, nv = _adamw_update(*[ins[j * n + k][...] for j in range(4)])
            outs[k][...] = d
            outs[n + k][...] = nm
            outs[2 * n + k][...] = nv

    vmem = pl.BlockSpec(memory_space=pltpu.VMEM)
    outs = pl.pallas_call(
        body, name="adamw_small",
        in_specs=[vmem] * (4 * n), out_specs=[vmem] * (3 * n),
        out_shape=[jax.ShapeDtypeStruct(w.shape, F32) for w in ws] * 3,
    )(*ws, *gs, *ms, *vs)
    return outs[:n], outs[n:2 * n], outs[2 * n:]


def _adamw(w, g, m, v, name):
    rows, cols = w.shape
    tile = rows
    for cand in (512, 256, 128, 64, 32, 16, 8):
        if rows % cand == 0:
            tile = cand
            break

    def body(w_ref, g_ref, m_ref, v_ref, d_ref, nm_ref, nv_ref):
        d_ref[...], nm_ref[...], nv_ref[...] = _adamw_update(w_ref[...], g_ref[...], m_ref[...], v_ref[...])

    spec = pl.BlockSpec((tile, cols), lambda i: (i, 0))
    return pl.pallas_call(
        body, name=name, grid=(rows // tile,),
        in_specs=[spec] * 4, out_specs=[spec] * 3,
        out_shape=[jax.ShapeDtypeStruct((rows, cols), F32)] * 3,
        compiler_params=_params(("arbitrary",)),
    )(w, g, m, v)


SMALL = (("norm_mix_g", (1, 1024)), ("conv_dw_b", (1, 512)), ("conv_ln_g", (1, 512)), ("conv_ln_b", (1, 512)),
         ("pool_w", (1, 4, 128, 128)), ("pool_scale", (1, 512)), ("norm_xattn_g", (1, 1024)), ("norm_mem_g", (1, 1024)),
         ("norm_ffn_g", (1, 1024)), ("ffn_dw_b", (1, 5632)), ("norm_final_g", (1024,)))
LANES = 128


def _pack_rows(arrs):
    flat = jnp.concatenate([a.reshape(-1) for a in arrs])
    pad = (-flat.shape[0]) % (8 * LANES)
    return jnp.pad(flat, (0, pad)).reshape(-1, LANES)


def kernel(x, mem, norm_mix_g, w_in, conv_dw_w, conv_dw_b, conv_ln_g, conv_ln_b, pool_w, pool_scale, w_out, norm_xattn_g, norm_mem_g, w_q, w_kv, w_o, norm_ffn_g, w_up, ffn_dw_w, ffn_dw_b, w_down, norm_final_g, loss_target, m_norm_mix_g, m_w_in, m_conv_dw_w, m_conv_dw_b, m_conv_ln_g, m_conv_ln_b, m_pool_w, m_pool_scale, m_w_out, m_norm_xattn_g, m_norm_mem_g, m_w_q, m_w_kv, m_w_o, m_norm_ffn_g, m_w_up, m_ffn_dw_w, m_ffn_dw_b, m_w_down, m_norm_final_g, v_norm_mix_g, v_w_in, v_conv_dw_w, v_conv_dw_b, v_conv_ln_g, v_conv_ln_b, v_pool_w, v_pool_scale, v_w_out, v_norm_xattn_g, v_norm_mem_g, v_w_q, v_w_kv, v_w_o, v_norm_ffn_g, v_w_up, v_ffn_dw_w, v_ffn_dw_b, v_w_down, v_norm_final_g):
    weights = dict(norm_mix_g=norm_mix_g, w_in=w_in, conv_dw_w=conv_dw_w, conv_dw_b=conv_dw_b, conv_ln_g=conv_ln_g,
                   conv_ln_b=conv_ln_b, pool_w=pool_w, pool_scale=pool_scale, w_out=w_out, norm_xattn_g=norm_xattn_g,
                   norm_mem_g=norm_mem_g, w_q=w_q, w_kv=w_kv, w_o=w_o, norm_ffn_g=norm_ffn_g, w_up=w_up,
                   ffn_dw_w=ffn_dw_w, ffn_dw_b=ffn_dw_b, w_down=w_down, norm_final_g=norm_final_g)
    moments_m = dict(norm_mix_g=m_norm_mix_g, w_in=m_w_in, conv_dw_w=m_conv_dw_w, conv_dw_b=m_conv_dw_b,
                     conv_ln_g=m_conv_ln_g, conv_ln_b=m_conv_ln_b, pool_w=m_pool_w, pool_scale=m_pool_scale,
                     w_out=m_w_out, norm_xattn_g=m_norm_xattn_g, norm_mem_g=m_norm_mem_g, w_q=m_w_q, w_kv=m_w_kv,
                     w_o=m_w_o, norm_ffn_g=m_norm_ffn_g, w_up=m_w_up, ffn_dw_w=m_ffn_dw_w, ffn_dw_b=m_ffn_dw_b,
                     w_down=m_w_down, norm_final_g=m_norm_final_g)
    moments_v = dict(norm_mix_g=v_norm_mix_g, w_in=v_w_in, conv_dw_w=v_conv_dw_w, conv_dw_b=v_conv_dw_b,
                     conv_ln_g=v_conv_ln_g, conv_ln_b=v_conv_ln_b, pool_w=v_pool_w, pool_scale=v_pool_scale,
                     w_out=v_w_out, norm_xattn_g=v_norm_xattn_g, norm_mem_g=v_norm_mem_g, w_q=v_w_q, w_kv=v_w_kv,
                     w_o=v_w_o, norm_ffn_g=v_norm_ffn_g, w_up=v_w_up, ffn_dw_w=v_ffn_dw_w, ffn_dw_b=v_ffn_dw_b,
                     w_down=v_w_down, norm_final_g=v_norm_final_g)
    order = list(weights)
    transposed = ("w_in", "w_kv", "w_up")

    n_b, seq, _ = x.shape
    tokens = n_b * seq
    tm_mix = min(512, seq // 2)
    tm_ffn = min(256, seq // 2)
    dev = 4 * lax.axis_index("x") + 2 * lax.axis_index("y") + lax.axis_index("c")

    packs = [jnp.concatenate([weights[n][0].T if n in transposed else weights[n][0] for n in names], axis=0).astype(BF16)
             for names in AG_GROUPS]
    small_sharded = _pack_rows([conv_dw_w[0], ffn_dw_w[0]])
    gw_mix, gsmall = _all_gather([packs[0], small_sharded], "weights_all_gather")
    flights = []
    after = gw_mix
    for k in (1, 2):
        own_in_place = lax.dynamic_update_slice(lax.empty((N_DEV,) + packs[k].shape, BF16), packs[k][None], (dev, 0, 0))
        flights.append(_gather_start(own_in_place, after, "weights_gather_start_%d" % k))
        after = flights[-1][3]
    gflat = gsmall.reshape(N_DEV, -1)
    n_cw = CONV_WIDTH * (D_CONV // N_DEV)
    n_fw = FFN_CONV_WIDTH * (2 * D_FF // N_DEV)
    conv_w = gflat[:, :n_cw].reshape(N_DEV, CONV_WIDTH, D_CONV // N_DEV).transpose(1, 0, 2).reshape(CONV_WIDTH, D_CONV)
    ffn_w = gflat[:, n_cw:n_cw + n_fw].reshape(N_DEV, FFN_CONV_WIDTH, 2 * D_FF // N_DEV).transpose(1, 0, 2).reshape(
        FFN_CONV_WIDTH, 2 * D_FF)

    x2d = x.reshape(tokens, D_MODEL)
    mem2d = mem.reshape(n_b * N_MEM, D_MODEL)
    tgt2d = loss_target.reshape(tokens, D_MODEL)
    g_final = norm_final_g.reshape(1, D_MODEL)

    def gather_finish(flight, after, tag):
        fwd_send, fwd_recv, buf = _gather_forward(*flight[:3], after, "weights_gather_forward_" + tag)
        return _gather_finish(fwd_send, fwd_recv, buf, "weights_gather_finish_" + tag)

    x1, u_all, c_all, pooled_all, ymix, h1 = _fwd_mix(
        x2d, gw_mix, norm_mix_g, conv_w, conv_dw_b, conv_ln_g, conv_ln_b, pool_w[0], pool_scale, flights[1][3],
        seq, tm_mix)
    gw_attn = gather_finish(flights[0], x1, "1")
    mem_n, kv = _fwd_kv(mem2d, gw_attn, norm_mem_g)
    x2, h2, q, o = _fwd_attn(x1, kv, gw_attn, norm_xattn_g, seq, tm_mix)
    gw_ffn = gather_finish(flights[1], x2, "2")
    uu_all, a_all, h3, dx3, dx3b, loss_part, dg_final = _fwd_ffn(
        x2, tgt2d, gw_ffn, norm_ffn_g, ffn_w, ffn_dw_b, g_final, seq, tm_ffn)

    table = _owner_table()

    def reduce_start(names, tag):
        parts = [part[n].reshape(N_DEV, W_OFF[n][1], D_MODEL) for n in names]
        landed = _exchange_sibling(parts, "rs_sibling_exchange_" + tag)
        sums = _chip_partial_sums(table, parts, landed, "rs_chip_partial_sums_" + tag)
        return parts, landed, _chip_exchange_start(sums, "rs_chip_exchange_start_" + tag)

    def reduce_finish(names, parts, landed, flight, after, tag):
        from_chips = _chip_exchange_wait(*flight[:4], after, "rs_chip_exchange_wait_" + tag)
        sums = _final_grad_sums(table, parts, landed, from_chips, "rs_final_sums_" + tag)
        g_mine.update(zip(names, sums))
        return sums[-1]

    part, g_mine = {}, {}
    dx2, dx2b, duu, d_ffn_b, d_ffn_w, dg_ffn = _bwd_ffn(dx3, x2, uu_all, gw_ffn, norm_ffn_g, ffn_w, ffn_dw_b, seq, tm_ffn)
    part["w_up"] = _wgrad(duu, h3, "wgrad_w_up")
    part["w_down"] = _wgrad(a_all, dx3b, "wgrad_w_down")
    parts_a, landed_a, flight_a = reduce_start(RS_GROUPS["a"], "a")
    dx1, dx1b, dq, dkv, dg_x = _bwd_attn(dx2, x1, q, kv, gw_attn, norm_xattn_g, flight_a[4], seq, tm_mix)
    dkv_b, dg_mem = _bwd_kv(dkv, mem2d, gw_attn, norm_mem_g)
    part["w_q"] = _wgrad(h2, dq, "wgrad_w_q")
    part["w_kv"] = _wgrad(dkv_b, mem_n, "wgrad_w_kv")
    part["w_o"] = _wgrad(o, dx2b, "wgrad_w_o")
    parts_b, landed_b, flight_b = reduce_start(RS_GROUPS["b"], "b")
    dx, du, dg_mix, d_conv_w, d_conv_b, d_ln_g, d_ln_b, d_pool_w, d_pool_scale = _bwd_mix(
        dx1, x2d, u_all, c_all, pooled_all, gw_mix, norm_mix_g, conv_w, conv_ln_g, conv_ln_b, pool_w[0], pool_scale,
        flight_b[4], seq, tm_mix)
    grad_x = dx.reshape(x.shape)

    small_grads = dict(norm_mix_g=dg_mix, conv_dw_b=d_conv_b, conv_ln_g=d_ln_g, conv_ln_b=d_ln_b, pool_w=d_pool_w,
                       pool_scale=d_pool_scale, norm_xattn_g=dg_x, norm_mem_g=dg_mem, norm_ffn_g=dg_ffn,
                       ffn_dw_b=d_ffn_b, norm_final_g=dg_final)
    small_list = [small_grads[n] for n, _ in SMALL] + [d_conv_w, d_ffn_w, loss_part[:1]]
    small_mine = _pack_rows(small_list)
    small_flight = _broadcast_start(
        lax.dynamic_update_slice(lax.empty((N_DEV,) + small_mine.shape, F32), small_mine[None], (dev, 0, 0)),
        "small_grads_broadcast_start")

    part["w_in"] = _wgrad(du, h1, "wgrad_w_in", after=small_flight[3])
    part["w_out"] = _wgrad(ymix, dx1b, "wgrad_w_out")
    parts_c, landed_c, flight_c = reduce_start(RS_GROUPS["c"], "c")
    done_a = reduce_finish(RS_GROUPS["a"], parts_a, landed_a, flight_a, flight_c[4], "a")
    done_b = reduce_finish(RS_GROUPS["b"], parts_b, landed_b, flight_b, done_a, "b")
    done_c = reduce_finish(RS_GROUPS["c"], parts_c, landed_c, flight_c, done_b, "c")

    small_all = _broadcast_wait(*small_flight[:3], done_c, "small_grads_broadcast_wait")
    small_sum = _sum_blocks(small_all).reshape(-1)

    grads = {}
    pos = 0
    for n, shape in SMALL:
        size = 1
        for s in shape:
            size *= s
        grads[n] = small_sum[pos:pos + size].reshape(shape)
        pos += size
    full_conv_w = small_sum[pos:pos + CONV_WIDTH * D_CONV].reshape(CONV_WIDTH, D_CONV)
    pos += CONV_WIDTH * D_CONV
    full_ffn_w = small_sum[pos:pos + FFN_CONV_WIDTH * 2 * D_FF].reshape(FFN_CONV_WIDTH, 2 * D_FF)
    loss = small_sum[pos + FFN_CONV_WIDTH * 2 * D_FF]
    grads["conv_dw_w"] = lax.dynamic_slice_in_dim(full_conv_w, dev * (D_CONV // N_DEV), D_CONV // N_DEV, axis=1)[None]
    grads["ffn_dw_w"] = lax.dynamic_slice_in_dim(full_ffn_w, dev * (2 * D_FF // N_DEV), 2 * D_FF // N_DEV, axis=1)[None]
    for n in W_OFF:
        grads[n] = (g_mine[n].T if n in transposed else g_mine[n])[None]

    delta, new_m, new_v = {}, {}, {}
    for n, _ in W_ROWS:
        shape = weights[n].shape
        as2d = lambda t: t.reshape(shape[1], shape[2])
        d, nm, nv = _adamw(as2d(weights[n]), as2d(grads[n]), as2d(moments_m[n]), as2d(moments_v[n]), "adamw_" + n)
        delta[n], new_m[n], new_v[n] = d.reshape(shape), nm.reshape(shape), nv.reshape(shape)
    small_names = [n for n in order if n not in W_OFF]
    two_d = lambda t: t.reshape(1, -1) if t.ndim == 1 else t
    outs = _adamw_small(*[[two_d(t[n]) for n in small_names] for t in (weights, grads, moments_m, moments_v)])
    for res, out in zip((delta, new_m, new_v), outs):
        for n, o in zip(small_names, out):
            res[n] = o.reshape(weights[n].shape)

    return (loss, grad_x, *[grads[n] for n in order], *[delta[n] for n in order],
            *[new_m[n] for n in order], *[new_v[n] for n in order])
```

```python
import functools

import jax
import jax.numpy as jnp
from jax import lax
from jax.experimental import pallas as pl
from jax.experimental.pallas import tpu as pltpu

F32 = jnp.float32
BF16 = jnp.bfloat16
MESH = pl.DeviceIdType.MESH

N_DEV = 8
D_MODEL = 1024
D_CONV = 512
D_POOL = 512
CONV_WIDTH = 31
POOL_WINDOWS = (2, 4, 8, 16)
POOL_GROUP_DIM = 128
D_IN = 1536
N_MEM = 256
HEADS = 4
HEAD_DIM = 256
D_FF = 2816
FFN_CONV_WIDTH = 3
EPS = 1e-6
ADAM_LR = 0.001
ADAM_B1 = 0.9
ADAM_B2 = 0.999
ADAM_EPS = 1e-08
ADAM_WD = 0.01
ADAM_STEP = 10

VMEM_LIMIT_V7X = 56 * 1024 * 1024
CONV_HALO = 32
POOL_HALO = 16
FFN_HALO = 8
FFN_CHUNK = 2816

W_ROWS = (("w_in", 192), ("w_out", 128), ("w_q", 128), ("w_kv", 256), ("w_o", 128), ("w_up", 704), ("w_down", 352))
AG_GROUPS = (("w_in", "w_out"), ("w_q", "w_kv", "w_o"), ("w_up", "w_down"))
W_OFF = {}
for _names in AG_GROUPS:
    _o = 0
    for _n in _names:
        W_OFF[_n] = (_o, dict(W_ROWS)[_n])
        _o += dict(W_ROWS)[_n]
RS_GROUPS = {"a": ("w_up", "w_down"), "b": ("w_q", "w_kv", "w_o"), "c": ("w_in", "w_out")}


def _dot(a, b):
    return jnp.dot(a, b, preferred_element_type=F32)


def _dot_nt(a, b):
    return lax.dot_general(a, b, (((1,), (1,)), ((), ())), preferred_element_type=F32)


def _dot_tn(a, b):
    return lax.dot_general(a, b, (((0,), (0,)), ((), ())), preferred_element_type=F32)


def _sigmoid(v):
    return 1.0 / (1.0 + jnp.exp(-v))


def _rms_fwd(v):
    r = lax.rsqrt(jnp.mean(v * v, axis=-1, keepdims=True) + EPS)
    return v * r, r


def _rms_bwd(dh, vh, r, g):
    gd = dh * g
    return r * (gd - vh * jnp.mean(gd * vh, axis=-1, keepdims=True))


def _sublane_shifts(v):
    rows = v.shape[0]
    return [v] + [pltpu.roll(v, rows - b, 0) for b in range(1, 8)]


def _colsum(v):
    return jnp.sum(v, axis=0, keepdims=True)


def _full(shape):
    return pl.BlockSpec(shape, lambda *_: (0,) * len(shape))


def _params(sem=("arbitrary",), vmem=VMEM_LIMIT_V7X):
    return pltpu.CompilerParams(dimension_semantics=sem, vmem_limit_bytes=vmem)


def _load_weight(g_hbm, name, dst, sem):
    off, rows = W_OFF[name]
    return [pltpu.make_async_copy(g_hbm.at[d, pl.ds(off, rows), :], dst.at[pl.ds(d * rows, rows), :], sem)
            for d in range(N_DEV)]


def _position():
    x, y, c = lax.axis_index("x"), lax.axis_index("y"), lax.axis_index("c")
    chips = [(1 - x, y), (x, 1 - y), (1 - x, 1 - y)]
    return x, y, c, chips


def _dev(px, py, pc):
    return 4 * px + 2 * py + pc


def _all_gather(arrs, name):
    n = len(arrs)

    def body(*refs):
        ins, outs = refs[:n], refs[n:2 * n]
        send_sems, recv_sems, local_sems = refs[2 * n:2 * n + 3]
        bounce = refs[2 * n + 3:]
        x, y, c, chips = _position()
        me, sibling = (x, y, c), (x, y, 1 - c)

        def copy(a, k, block, to, src=None):
            rows = outs[a].at[_dev(*block)]
            return pltpu.make_async_remote_copy(
                src_ref=rows if src is None else src, dst_ref=rows,
                send_sem=send_sems.at[a, k], recv_sem=recv_sems.at[a, k], device_id=to, device_id_type=MESH)

        sends = []
        for a in range(n):
            first = [copy(a, 0, me, sibling, src=ins[a])]
            first += [copy(a, 1 + j, me, (*chip, c), src=ins[a]) for j, chip in enumerate(chips)]
            for cp in first:
                cp.start()
            sends += first
        started = []
        for a in range(n):
            load = pltpu.make_async_copy(ins[a], bounce[a], local_sems.at[a, 0])
            load.start()
            load.wait()
            mine = pltpu.make_async_copy(bounce[a], outs[a].at[_dev(*me)], local_sems.at[a, 1])
            mine.start()
            started.append(mine)
        for j, chip in enumerate(chips):
            for a in range(n):
                copy(a, 1 + j, (*chip, c), me).wait_recv()
                passed = copy(a, 4 + j, (*chip, c), sibling)
                passed.start()
                sends.append(passed)
        for a in range(n):
            copy(a, 0, sibling, me).wait_recv()
            for j, chip in enumerate(chips):
                copy(a, 4 + j, (*chip, 1 - c), me).wait_recv()
        for cp in sends:
            cp.wait_send()
        for mine in started:
            mine.wait()

    any_spec = pl.BlockSpec(memory_space=pl.ANY)
    return pl.pallas_call(
        body, name=name,
        out_shape=[jax.ShapeDtypeStruct((N_DEV,) + a.shape, a.dtype) for a in arrs],
        in_specs=[any_spec] * n, out_specs=[any_spec] * n,
        scratch_shapes=[pltpu.SemaphoreType.DMA((n, 7)), pltpu.SemaphoreType.DMA((n, 7)), pltpu.SemaphoreType.DMA((n, 2))]
        + [pltpu.VMEM(a.shape, a.dtype) for a in arrs],
    )(*arrs)


_HBM = pl.BlockSpec(memory_space=pltpu.HBM)
_SEM = pl.BlockSpec(memory_space=pltpu.SEMAPHORE)
_SIDE_EFFECT = pltpu.SideEffectType.DATAFLOW_SIDE_EFFECTING


def _gather_start(buf, after, name):
    def body(buf_ref, after_ref, send_sems, recv_sems, buf_thru, token):
        del after_ref, buf_thru
        x, y, c, chips = _position()
        rows = buf_ref.at[_dev(x, y, c)]
        for k, to in enumerate([(x, y, 1 - c)] + [(*chip, c) for chip in chips]):
            pltpu.make_async_remote_copy(src_ref=rows, dst_ref=rows, send_sem=send_sems.at[k], recv_sem=recv_sems.at[k],
                                         device_id=to, device_id_type=MESH).start()
        token[...] = jnp.zeros_like(token)

    return pl.pallas_call(
        body, name=name,
        out_shape=(pltpu.SemaphoreType.DMA((4,)), pltpu.SemaphoreType.DMA((4,)), pltpu.HBM(buf.shape, buf.dtype),
                   jax.ShapeDtypeStruct((8, 128), F32)),
        in_specs=(_HBM, pl.BlockSpec(memory_space=pl.ANY)),
        out_specs=(_SEM, _SEM, _HBM, pl.BlockSpec(memory_space=pltpu.VMEM)),
        input_output_aliases={0: 2},
        compiler_params=pltpu.CompilerParams(has_side_effects=_SIDE_EFFECT),
    )(pltpu.with_memory_space_constraint(buf, pltpu.HBM), after)


def _gather_forward(send_sems, recv_sems, buf, after, name):
    def body(buf_ref, send_sems, recv_sems, after_ref, fwd_send, fwd_recv, buf_thru):
        del after_ref, buf_thru
        x, y, c, chips = _position()
        sibling = (x, y, 1 - c)

        def copy(block, k, sends, recvs):
            rows = buf_ref.at[_dev(*block)]
            return pltpu.make_async_remote_copy(src_ref=rows, dst_ref=rows, send_sem=sends.at[k], recv_sem=recvs.at[k],
                                                device_id=sibling, device_id_type=MESH)

        for k in range(4):
            copy((x, y, c), k, send_sems, recv_sems).wait_send()
        copy(sibling, 0, send_sems, recv_sems).wait_recv()
        for j, chip in enumerate(chips):
            copy((*chip, c), 1 + j, send_sems, recv_sems).wait_recv()
            copy((*chip, c), j, fwd_send, fwd_recv).start()

    return pl.pallas_call(
        body, name=name,
        out_shape=(pltpu.SemaphoreType.DMA((3,)), pltpu.SemaphoreType.DMA((3,)), pltpu.HBM(buf.shape, buf.dtype)),
        in_specs=(_HBM, _SEM, _SEM, pl.BlockSpec(memory_space=pl.ANY)), out_specs=(_SEM, _SEM, _HBM),
        input_output_aliases={0: 2},
        compiler_params=pltpu.CompilerParams(has_side_effects=_SIDE_EFFECT),
    )(buf, send_sems, recv_sems, after)


def _gather_finish(fwd_send, fwd_recv, buf, name):
    def body(buf_ref, fwd_send, fwd_recv, buf_thru):
        del buf_thru
        x, y, c, chips = _position()
        for j, chip in enumerate(chips):
            cp = pltpu.make_async_remote_copy(
                src_ref=buf_ref.at[_dev(*chip, c)], dst_ref=buf_ref.at[_dev(*chip, 1 - c)], send_sem=fwd_send.at[j],
                recv_sem=fwd_recv.at[j], device_id=(x, y, 1 - c), device_id_type=MESH)
            cp.wait_send()
            cp.wait_recv()

    return pl.pallas_call(
        body, name=name,
        out_shape=pltpu.HBM(buf.shape, buf.dtype),
        in_specs=(_HBM, _SEM, _SEM), out_specs=_HBM,
        input_output_aliases={0: 0},
        compiler_params=pltpu.CompilerParams(has_side_effects=_SIDE_EFFECT),
    )(buf, fwd_send, fwd_recv)


def _everyone_else(x, y, c, chips):
    return [(x, y, 1 - c)] + [(*chip, core) for chip in chips for core in (c, 1 - c)]


def _broadcast_start(buf, name):
    def body(buf_ref, send_sems, recv_sems, buf_thru, token):
        del buf_thru
        x, y, c, chips = _position()
        rows = buf_ref.at[_dev(x, y, c)]
        for k, to in enumerate(_everyone_else(x, y, c, chips)):
            pltpu.make_async_remote_copy(src_ref=rows, dst_ref=rows, send_sem=send_sems.at[k], recv_sem=recv_sems.at[k],
                                         device_id=to, device_id_type=MESH).start()
        token[...] = jnp.zeros_like(token)

    return pl.pallas_call(
        body, name=name,
        out_shape=(pltpu.SemaphoreType.DMA((7,)), pltpu.SemaphoreType.DMA((7,)), pltpu.HBM(buf.shape, buf.dtype),
                   jax.ShapeDtypeStruct((8, 128), F32)),
        in_specs=(_HBM,), out_specs=(_SEM, _SEM, _HBM, pl.BlockSpec(memory_space=pltpu.VMEM)),
        input_output_aliases={0: 2},
        compiler_params=pltpu.CompilerParams(has_side_effects=_SIDE_EFFECT),
    )(pltpu.with_memory_space_constraint(buf, pltpu.HBM))


def _broadcast_wait(send_sems, recv_sems, buf, after, name):
    def body(buf_ref, send_sems, recv_sems, after_ref, buf_thru):
        del after_ref, buf_thru
        x, y, c, chips = _position()
        for k, peer in enumerate(_everyone_else(x, y, c, chips)):
            cp = pltpu.make_async_remote_copy(
                src_ref=buf_ref.at[_dev(x, y, c)], dst_ref=buf_ref.at[_dev(*peer)], send_sem=send_sems.at[k],
                recv_sem=recv_sems.at[k], device_id=peer, device_id_type=MESH)
            cp.wait_send()
            cp.wait_recv()

    return pl.pallas_call(
        body, name=name,
        out_shape=pltpu.HBM(buf.shape, buf.dtype),
        in_specs=(_HBM, _SEM, _SEM, pl.BlockSpec(memory_space=pl.ANY)), out_specs=_HBM,
        input_output_aliases={0: 0},
        compiler_params=pltpu.CompilerParams(has_side_effects=_SIDE_EFFECT),
    )(buf, send_sems, recv_sems, after)


def _exchange_sibling(parts, name):
    n = len(parts)

    def body(*refs):
        ins, outs = refs[:n], refs[n:2 * n]
        send_sems, recv_sems = refs[2 * n:]
        x, y, c, chips = _position()
        sibling = (x, y, 1 - c)
        copies = []
        for k in range(n):
            for j, chip in enumerate([(x, y)] + chips):
                cp = pltpu.make_async_remote_copy(
                    src_ref=ins[k].at[_dev(*chip, 1 - c)], dst_ref=outs[k].at[j],
                    send_sem=send_sems.at[k, j], recv_sem=recv_sems.at[k, j], device_id=sibling, device_id_type=MESH)
                cp.start()
                copies.append(cp)
        for cp in copies:
            cp.wait_recv()
        for cp in copies:
            cp.wait_send()

    any_spec = pl.BlockSpec(memory_space=pl.ANY)
    return pl.pallas_call(
        body, name=name,
        out_shape=[jax.ShapeDtypeStruct((4,) + p.shape[1:], p.dtype) for p in parts],
        in_specs=[any_spec] * n, out_specs=[any_spec] * n,
        scratch_shapes=[pltpu.SemaphoreType.DMA((n, 4)), pltpu.SemaphoreType.DMA((n, 4))],
    )(*parts)


def _chip_exchange_start(sums, name):
    n = len(sums)

    def body(*refs):
        s_refs, land_refs = refs[:n], refs[n:2 * n]
        send_sems, recv_sems = refs[2 * n:2 * n + 2]
        token = refs[-1]
        x, y, c, chips = _position()
        for k in range(n):
            for j, chip in enumerate(chips):
                pltpu.make_async_remote_copy(
                    src_ref=s_refs[k].at[j], dst_ref=land_refs[k].at[j], send_sem=send_sems.at[3 * k + j],
                    recv_sem=recv_sems.at[3 * k + j], device_id=(*chip, c), device_id_type=MESH).start()
        token[...] = jnp.zeros_like(token)

    hbm = pl.BlockSpec(memory_space=pltpu.HBM)
    sem = pl.BlockSpec(memory_space=pltpu.SEMAPHORE)
    thru = [pltpu.HBM(s.shape, s.dtype) for s in sums]
    outs = pl.pallas_call(
        body, name=name,
        out_shape=(pltpu.SemaphoreType.DMA((3 * n,)), pltpu.SemaphoreType.DMA((3 * n,)), *thru, *thru,
                   jax.ShapeDtypeStruct((8, 128), F32)),
        in_specs=[hbm] * (2 * n), out_specs=(sem, sem, *[hbm] * (2 * n), pl.BlockSpec(memory_space=pltpu.VMEM)),
        input_output_aliases={k: 2 + k for k in range(2 * n)},
        compiler_params=pltpu.CompilerParams(has_side_effects=pltpu.SideEffectType.DATAFLOW_SIDE_EFFECTING),
    )(*[pltpu.with_memory_space_constraint(s, pltpu.HBM) for s in sums],
      *[pltpu.with_memory_space_constraint(lax.empty(s.shape, s.dtype), pltpu.HBM) for s in sums])
    return outs[0], outs[1], outs[2:2 + n], outs[2 + n:2 + 2 * n], outs[-1]


def _chip_exchange_wait(send_sems, recv_sems, s_thru, land_thru, after, name):
    n = len(s_thru)

    def body(*refs):
        s_refs, land_refs = refs[:n], refs[n:2 * n]
        send_sems, recv_sems = refs[2 * n:2 * n + 2]
        x, y, c, chips = _position()
        for k in range(n):
            for j, chip in enumerate(chips):
                cp = pltpu.make_async_remote_copy(
                    src_ref=s_refs[k].at[j], dst_ref=land_refs[k].at[j], send_sem=send_sems.at[3 * k + j],
                    recv_sem=recv_sems.at[3 * k + j], device_id=(*chip, c), device_id_type=MESH)
                cp.wait_send()
                cp.wait_recv()

    hbm = pl.BlockSpec(memory_space=pltpu.HBM)
    sem = pl.BlockSpec(memory_space=pltpu.SEMAPHORE)
    thru = [pltpu.HBM(s.shape, s.dtype) for s in s_thru]
    outs = pl.pallas_call(
        body, name=name,
        out_shape=(*thru, *thru),
        in_specs=[hbm] * (2 * n) + [sem, sem, pl.BlockSpec(memory_space=pl.ANY)], out_specs=[hbm] * (2 * n),
        input_output_aliases={k: k for k in range(2 * n)},
        compiler_params=pltpu.CompilerParams(has_side_effects=pltpu.SideEffectType.DATAFLOW_SIDE_EFFECTING),
    )(*s_thru, *land_thru, send_sems, recv_sems, after)
    return outs[n:]


def _owner_table():
    x, y, c = lax.axis_index("x"), lax.axis_index("y"), lax.axis_index("c")
    chips = [(x, y), (1 - x, y), (x, 1 - y), (1 - x, 1 - y)]
    return jnp.stack([_dev(px, py, c) for px, py in chips]).astype(jnp.int32)


def _chip_partial_sums(table, parts, from_sibling, name):
    n = len(parts)

    def body(tab_ref, *refs):
        del tab_ref
        for g_ref, l_ref, out_ref in zip(refs[:n], refs[n:2 * n], refs[2 * n:]):
            out_ref[...] = (g_ref[...].astype(F32) + l_ref[...].astype(F32)).astype(out_ref.dtype)

    block = lambda p: (None,) + p.shape[1:]
    grid_spec = pltpu.PrefetchScalarGridSpec(
        num_scalar_prefetch=1, grid=(3,),
        in_specs=[pl.BlockSpec(block(p), lambda j, tab: (tab[j + 1], 0, 0)) for p in parts]
        + [pl.BlockSpec(block(p), lambda j, tab: (j + 1, 0, 0)) for p in parts],
        out_specs=[pl.BlockSpec(block(p), lambda j, tab: (j, 0, 0)) for p in parts])
    return pl.pallas_call(
        body, name=name, grid_spec=grid_spec,
        out_shape=[jax.ShapeDtypeStruct((3,) + p.shape[1:], BF16) for p in parts],
        compiler_params=_params(("arbitrary",)),
    )(table, *parts, *from_sibling)


def _final_grad_sums(table, parts, from_sibling, from_chips, name):
    n = len(parts)

    def body(tab_ref, *refs):
        del tab_ref
        for g_ref, l_ref, c_ref, out_ref in zip(refs[:n], refs[n:2 * n], refs[2 * n:3 * n], refs[3 * n:]):
            acc = g_ref[...].astype(F32) + l_ref[...].astype(F32)
            for j in range(3):
                acc = acc + c_ref[j].astype(F32)
            out_ref[...] = acc

    half = lambda p: (p.shape[1] // 2, p.shape[2])
    grid_spec = pltpu.PrefetchScalarGridSpec(
        num_scalar_prefetch=1, grid=(2,),
        in_specs=[pl.BlockSpec((None,) + half(p), lambda t, tab: (tab[0], t, 0)) for p in parts]
        + [pl.BlockSpec((None,) + half(p), lambda t, tab: (0, t, 0)) for p in parts]
        + [pl.BlockSpec((3,) + half(p), lambda t, tab: (0, t, 0)) for p in parts],
        out_specs=[pl.BlockSpec(half(p), lambda t, tab: (t, 0)) for p in parts])
    return pl.pallas_call(
        body, name=name, grid_spec=grid_spec,
        out_shape=[jax.ShapeDtypeStruct(p.shape[1:], F32) for p in parts],
        compiler_params=_params(("arbitrary",)),
    )(table, *parts, *from_sibling, *from_chips)


def _sum_blocks(g8):
    _, rows, cols = g8.shape

    def body(g_ref, out_ref):
        acc = g_ref[0]
        for d in range(1, N_DEV):
            acc = acc + g_ref[d]
        out_ref[...] = acc

    return pl.pallas_call(
        body, name="small_grad_sum", grid=(1,),
        in_specs=[_full((N_DEV, rows, cols))], out_specs=_full((rows, cols)),
        out_shape=jax.ShapeDtypeStruct((rows, cols), F32),
        compiler_params=_params(("arbitrary",)),
    )(g8)


def _fwd_mix(x2d, gw, g_mix, conv_w, conv_b, ln_g, ln_b, pool_w, pool_scale, after, seq, tm):
    tokens = x2d.shape[0]
    n_tiles = tokens // tm
    tps = seq // tm

    def body(x_ref, gmix_ref, gw_hbm, cw_ref, cb_ref, lng_ref, lnb_ref, pw_ref, ps_ref, after_ref,
             x1_ref, u_ref, c_ref, pooled_ref, ymix_ref, h1_ref,
             win_v, wout_v, hc_carry, up_carry, sem):
        del after_ref
        i = pl.program_id(0)

        @pl.when(i == 0)
        def _():
            copies = _load_weight(gw_hbm, "w_in", win_v, sem) + _load_weight(gw_hbm, "w_out", wout_v, sem)
            for cp in copies:
                cp.start()
            for cp in copies:
                cp.wait()

        @pl.when(i % tps == 0)
        def _():
            hc_carry[...] = jnp.zeros_like(hc_carry)
            up_carry[...] = jnp.zeros_like(up_carry)

        x = x_ref[...]
        xh, _ = _rms_fwd(x)
        h1 = (xh * gmix_ref[...]).astype(BF16)
        h1_ref[...] = h1
        u = _dot_nt(h1, win_v[...])
        u_ref[...] = u
        val, gate, up = u[:, :D_CONV], u[:, D_CONV:2 * D_CONV], u[:, 2 * D_CONV:]

        hc = val * _sigmoid(gate)
        ext = jnp.concatenate([hc_carry[...], hc], axis=0)
        hc_carry[...] = hc[tm - CONV_HALO:, :]
        conv = jnp.broadcast_to(cb_ref[...], (tm, D_CONV))
        ahead_by = _sublane_shifts(ext)
        for k in range(CONV_WIDTH):
            whole, part = divmod(CONV_HALO - (CONV_WIDTH - 1) + k, 8)
            conv = conv + cw_ref[k:k + 1, :] * ahead_by[part][8 * whole:8 * whole + tm, :]
        c_ref[...] = conv
        mu = jnp.mean(conv, axis=-1, keepdims=True)
        cen = conv - mu
        ln = cen * lax.rsqrt(jnp.mean(cen * cen, axis=-1, keepdims=True) + EPS) * lng_ref[...] + lnb_ref[...]
        y_conv = ln * _sigmoid(ln)

        extp = jnp.concatenate([up_carry[...], up], axis=0)
        up_carry[...] = up[tm - POOL_HALO:, :]
        pos = lax.broadcasted_iota(jnp.int32, (tm, 1), 0) + (i % tps) * tm
        run = extp
        mixed = []
        for g, w in enumerate(POOL_WINDOWS):
            lo = g * POOL_GROUP_DIM
            run = run[:, POOL_GROUP_DIM if g else 0:]
            run = run + pltpu.roll(run, w // 2, 0)
            cnt = jnp.minimum(pos + 1, w).astype(F32)
            pooled = run[POOL_HALO:, :POOL_GROUP_DIM] / cnt - up[:, lo:lo + POOL_GROUP_DIM]
            pooled = pooled.astype(BF16)
            pooled_ref[:, lo:lo + POOL_GROUP_DIM] = pooled
            mixed.append(_dot(pooled, pw_ref[g].astype(BF16)))
        y_pool = jnp.concatenate(mixed, axis=-1) * ps_ref[...]

        ymix = jnp.concatenate([y_conv, y_pool], axis=-1).astype(BF16)
        ymix_ref[...] = ymix
        x1_ref[...] = x + _dot(ymix, wout_v[...])

    row = lambda w: pl.BlockSpec((tm, w), lambda i: (i, 0))
    return pl.pallas_call(
        body, name="fwd_mix", grid=(n_tiles,),
        in_specs=[row(D_MODEL), _full((1, D_MODEL)), pl.BlockSpec(memory_space=pl.ANY),
                  _full((CONV_WIDTH, D_CONV)), _full((1, D_CONV)), _full((1, D_CONV)), _full((1, D_CONV)),
                  _full((4, POOL_GROUP_DIM, POOL_GROUP_DIM)), _full((1, D_POOL)), _full(after.shape)],
        out_specs=[row(D_MODEL), row(D_IN), row(D_CONV), row(D_POOL), row(D_MODEL), row(D_MODEL)],
        out_shape=[jax.ShapeDtypeStruct((tokens, D_MODEL), F32), jax.ShapeDtypeStruct((tokens, D_IN), F32),
                   jax.ShapeDtypeStruct((tokens, D_CONV), F32), jax.ShapeDtypeStruct((tokens, D_POOL), BF16),
                   jax.ShapeDtypeStruct((tokens, D_MODEL), BF16), jax.ShapeDtypeStruct((tokens, D_MODEL), BF16)],
        scratch_shapes=[pltpu.VMEM((D_IN, D_MODEL), BF16), pltpu.VMEM((D_MODEL, D_MODEL), BF16),
                        pltpu.VMEM((CONV_HALO, D_CONV), F32), pltpu.VMEM((POOL_HALO, D_POOL), F32),
                        pltpu.SemaphoreType.DMA],
        compiler_params=_params(),
    )(x2d, g_mix, gw, conv_w, conv_b, ln_g, ln_b, pool_w, pool_scale, after)


def _fwd_kv(mem2d, gw, g_mem):
    rows = mem2d.shape[0]
    n_b = rows // N_MEM

    def body(mem_ref, g_ref, gw_hbm, mn_ref, kv_ref, wkv_v, sem):
        @pl.when(pl.program_id(0) == 0)
        def _():
            copies = _load_weight(gw_hbm, "w_kv", wkv_v, sem)
            for cp in copies:
                cp.start()
            for cp in copies:
                cp.wait()

        mh, _ = _rms_fwd(mem_ref[...])
        mn = (mh * g_ref[...]).astype(BF16)
        mn_ref[...] = mn
        kv_ref[...] = _dot_nt(mn, wkv_v[...]).astype(BF16)

    return pl.pallas_call(
        body, name="fwd_kv", grid=(n_b,),
        in_specs=[pl.BlockSpec((N_MEM, D_MODEL), lambda b: (b, 0)), _full((1, D_MODEL)), pl.BlockSpec(memory_space=pl.ANY)],
        out_specs=[pl.BlockSpec((N_MEM, D_MODEL), lambda b: (b, 0)), pl.BlockSpec((N_MEM, 2 * D_MODEL), lambda b: (b, 0))],
        out_shape=[jax.ShapeDtypeStruct((rows, D_MODEL), BF16), jax.ShapeDtypeStruct((rows, 2 * D_MODEL), BF16)],
        scratch_shapes=[pltpu.VMEM((2 * D_MODEL, D_MODEL), BF16), pltpu.SemaphoreType.DMA],
        compiler_params=_params(),
    )(mem2d, g_mem, gw)


def _softmax_rows(s):
    e = jnp.exp(s - jnp.max(s, axis=-1, keepdims=True))
    return e / jnp.sum(e, axis=-1, keepdims=True)


def _fwd_attn(x1, kv, gw, g_x, seq, tm):
    tokens = x1.shape[0]
    n_tiles = tokens // tm
    tps = seq // tm

    def body(x1_ref, kv_ref, g_ref, gw_hbm, x2_ref, h2_ref, q_ref, o_ref, wq_v, wo_v, sem):
        @pl.when(pl.program_id(0) == 0)
        def _():
            copies = _load_weight(gw_hbm, "w_q", wq_v, sem) + _load_weight(gw_hbm, "w_o", wo_v, sem)
            for cp in copies:
                cp.start()
            for cp in copies:
                cp.wait()

        x1v = x1_ref[...]
        xh, _ = _rms_fwd(x1v)
        h2 = (xh * g_ref[...]).astype(BF16)
        h2_ref[...] = h2
        q = (_dot(h2, wq_v[...]) * (HEAD_DIM ** -0.5)).astype(BF16)
        q_ref[...] = q
        outs = []
        for h in range(HEADS):
            lo = h * HEAD_DIM
            p = _softmax_rows(_dot_nt(q[:, lo:lo + HEAD_DIM], kv_ref[:, lo:lo + HEAD_DIM]))
            outs.append(_dot(p.astype(BF16), kv_ref[:, D_MODEL + lo:D_MODEL + lo + HEAD_DIM]))
        o = jnp.concatenate(outs, axis=-1).astype(BF16)
        o_ref[...] = o
        x2_ref[...] = x1v + _dot(o, wo_v[...])

    row = lambda w: pl.BlockSpec((tm, w), lambda i: (i, 0))
    return pl.pallas_call(
        body, name="fwd_attn", grid=(n_tiles,),
        in_specs=[row(D_MODEL), pl.BlockSpec((N_MEM, 2 * D_MODEL), lambda i: (i // tps, 0)), _full((1, D_MODEL)),
                  pl.BlockSpec(memory_space=pl.ANY)],
        out_specs=[row(D_MODEL)] * 4,
        out_shape=[jax.ShapeDtypeStruct((tokens, D_MODEL), F32)] + [jax.ShapeDtypeStruct((tokens, D_MODEL), BF16)] * 3,
        scratch_shapes=[pltpu.VMEM((D_MODEL, D_MODEL), BF16), pltpu.VMEM((D_MODEL, D_MODEL), BF16), pltpu.SemaphoreType.DMA],
        compiler_params=_params(),
    )(x1, kv, g_x, gw)


def _ffn_conv(uu, halo, w_ref, b_ref, cols):
    ext = jnp.concatenate([halo, uu], axis=0)
    p1 = pltpu.roll(ext, 1, 0)[FFN_HALO:, :]
    p2 = pltpu.roll(ext, 2, 0)[FFN_HALO:, :]
    return b_ref[:, cols] + w_ref[2:3, cols] * uu + w_ref[1:2, cols] * p1 + w_ref[0:1, cols] * p2


def _fwd_ffn(x2, target, gw, g_ffn, ffn_w, ffn_b, g_final, seq, tm):
    tokens = x2.shape[0]
    n_tiles = tokens // tm
    tps = seq // tm
    n_chunks = D_FF // FFN_CHUNK

    def body(x2_ref, tgt_ref, gffn_ref, gw_hbm, fw_ref, fb_ref, gfin_ref,
             uu_ref, cc_ref, a_ref, h3_ref, dx3_ref, dx3b_ref, loss_ref, dgfin_ref,
             wup_v, wdown_v, carry, sem):
        i = pl.program_id(0)

        @pl.when(i == 0)
        def _():
            copies = _load_weight(gw_hbm, "w_up", wup_v, sem) + _load_weight(gw_hbm, "w_down", wdown_v, sem)
            for cp in copies:
                cp.start()
            for cp in copies:
                cp.wait()
            loss_ref[...] = jnp.zeros_like(loss_ref)
            dgfin_ref[...] = jnp.zeros_like(dgfin_ref)

        @pl.when(i % tps == 0)
        def _():
            carry[...] = jnp.zeros_like(carry)

        x2v = x2_ref[...]
        xh, _ = _rms_fwd(x2v)
        h3 = (xh * gffn_ref[...]).astype(BF16)
        h3_ref[...] = h3
        acc = jnp.zeros((tm, D_MODEL), F32)
        for jc in range(n_chunks):
            halves = []
            for half in range(2):
                cols = pl.ds(half * D_FF + jc * FFN_CHUNK, FFN_CHUNK)
                uu = _dot_nt(h3, wup_v[cols, :])
                uu_ref[:, cols] = uu.astype(BF16)
                cc = _ffn_conv(uu, carry[:, cols], fw_ref, fb_ref, cols)
                cc_ref[:, cols] = cc.astype(BF16)
                halves.append(cc)
                carry[:, cols] = uu[tm - FFN_HALO:, :]
            gate, val = halves
            a = (gate * _sigmoid(gate) * val).astype(BF16)
            a_ref[:, pl.ds(jc * FFN_CHUNK, FFN_CHUNK)] = a
            acc = acc + _dot(a, wdown_v[pl.ds(jc * FFN_CHUNK, FFN_CHUNK), :])
        x3 = x2v + acc

        xh3, r3 = _rms_fwd(x3)
        gfin = gfin_ref[...]
        err = xh3 * gfin - tgt_ref[...]
        loss_ref[...] += jnp.full(loss_ref.shape, jnp.sum(err * err) * (0.5 / D_MODEL), F32)
        dy = err * (1.0 / D_MODEL)
        dgfin_ref[...] += _colsum(dy * xh3)
        dx3 = _rms_bwd(dy, xh3, r3, gfin)
        dx3_ref[...] = dx3
        dx3b_ref[...] = dx3.astype(BF16)

    row = lambda w: pl.BlockSpec((tm, w), lambda i: (i, 0))
    return pl.pallas_call(
        body, name="fwd_ffn", grid=(n_tiles,),
        in_specs=[row(D_MODEL), row(D_MODEL), _full((1, D_MODEL)), pl.BlockSpec(memory_space=pl.ANY),
                  _full((FFN_CONV_WIDTH, 2 * D_FF)), _full((1, 2 * D_FF)), _full((1, D_MODEL))],
        out_specs=[row(2 * D_FF), row(2 * D_FF), row(D_FF), row(D_MODEL), row(D_MODEL), row(D_MODEL), _full((8, 128)),
                   _full((1, D_MODEL))],
        out_shape=[jax.ShapeDtypeStruct((tokens, 2 * D_FF), BF16), jax.ShapeDtypeStruct((tokens, 2 * D_FF), BF16),
                   jax.ShapeDtypeStruct((tokens, D_FF), BF16),
                   jax.ShapeDtypeStruct((tokens, D_MODEL), BF16), jax.ShapeDtypeStruct((tokens, D_MODEL), F32),
                   jax.ShapeDtypeStruct((tokens, D_MODEL), BF16),
                   jax.ShapeDtypeStruct((8, 128), F32), jax.ShapeDtypeStruct((1, D_MODEL), F32)],
        scratch_shapes=[pltpu.VMEM((2 * D_FF, D_MODEL), BF16), pltpu.VMEM((D_FF, D_MODEL), BF16),
                        pltpu.VMEM((FFN_HALO, 2 * D_FF), F32), pltpu.SemaphoreType.DMA],
        compiler_params=_params(),
    )(x2, target, g_ffn, gw, ffn_w, ffn_b, g_final)


def _bwd_ffn(dx3, x2, uu_all, cc_all, gw, g_ffn, ffn_w, seq, tm):
    tokens = x2.shape[0]
    n_tiles = tokens // tm
    tps = seq // tm
    n_chunks = D_FF // FFN_CHUNK

    def body(dx3_ref, x2_ref, uu_ref, cc_ref, gffn_ref, gw_hbm, fw_ref,
             dx2_ref, dx2b_ref, duu_ref, dfb_ref, dfw_ref, dg_ref,
             wup_v, wdown_v, carry, sem):
        i = pl.program_id(0)
        t = n_tiles - 1 - i

        @pl.when(i == 0)
        def _():
            copies = _load_weight(gw_hbm, "w_up", wup_v, sem) + _load_weight(gw_hbm, "w_down", wdown_v, sem)
            for cp in copies:
                cp.start()
            for cp in copies:
                cp.wait()
            dfb_ref[...] = jnp.zeros_like(dfb_ref)
            dfw_ref[...] = jnp.zeros_like(dfw_ref)
            dg_ref[...] = jnp.zeros_like(dg_ref)

        @pl.when(t % tps == tps - 1)
        def _():
            carry[...] = jnp.zeros_like(carry)

        dx3v = dx3_ref[...]
        dx3b = dx3v.astype(BF16)
        dh3 = jnp.zeros((tm, D_MODEL), F32)
        for jc in range(n_chunks):
            da = _dot_nt(dx3b, wdown_v[pl.ds(jc * FFN_CHUNK, FFN_CHUNK), :])
            colss = [pl.ds(half * D_FF + jc * FFN_CHUNK, FFN_CHUNK) for half in range(2)]
            gate, val = [cc_ref[:, cols].astype(F32) for cols in colss]
            sg = _sigmoid(gate)
            dgate = da * val * (sg * (1.0 + gate * (1.0 - sg)))
            dval = da * (gate * sg)
            for dcc, cols in zip((dgate, dval), colss):
                uu = uu_ref[:, cols].astype(F32)
                dfb_ref[:, cols] += _colsum(dcc)
                ext = jnp.concatenate([dcc, carry[:, cols]], axis=0)
                carry[:, cols] = dcc[:FFN_HALO, :]
                n1 = pltpu.roll(ext, tm + FFN_HALO - 1, 0)[:tm, :]
                n2 = pltpu.roll(ext, tm + FFN_HALO - 2, 0)[:tm, :]
                duu = fw_ref[2:3, cols] * dcc + fw_ref[1:2, cols] * n1 + fw_ref[0:1, cols] * n2
                dfw_ref[2:3, cols] += _colsum(uu * dcc)
                dfw_ref[1:2, cols] += _colsum(uu * n1)
                dfw_ref[0:1, cols] += _colsum(uu * n2)
                duub = duu.astype(BF16)
                duu_ref[:, cols] = duub
                dh3 = dh3 + _dot(duub, wup_v[cols, :])
        xh, r = _rms_fwd(x2_ref[...])
        dg_ref[...] += _colsum(dh3 * xh)
        dx2 = dx3v + _rms_bwd(dh3, xh, r, gffn_ref[...])
        dx2_ref[...] = dx2
        dx2b_ref[...] = dx2.astype(BF16)

    rev = lambda w: pl.BlockSpec((tm, w), lambda i: (n_tiles - 1 - i, 0))
    return pl.pallas_call(
        body, name="bwd_ffn", grid=(n_tiles,),
        in_specs=[rev(D_MODEL), rev(D_MODEL), rev(2 * D_FF), rev(2 * D_FF), _full((1, D_MODEL)),
                  pl.BlockSpec(memory_space=pl.ANY), _full((FFN_CONV_WIDTH, 2 * D_FF))],
        out_specs=[rev(D_MODEL), rev(D_MODEL), rev(2 * D_FF), _full((1, 2 * D_FF)), _full((FFN_CONV_WIDTH, 2 * D_FF)),
                   _full((1, D_MODEL))],
        out_shape=[jax.ShapeDtypeStruct((tokens, D_MODEL), F32), jax.ShapeDtypeStruct((tokens, D_MODEL), BF16),
                   jax.ShapeDtypeStruct((tokens, 2 * D_FF), BF16),
                   jax.ShapeDtypeStruct((1, 2 * D_FF), F32), jax.ShapeDtypeStruct((FFN_CONV_WIDTH, 2 * D_FF), F32),
                   jax.ShapeDtypeStruct((1, D_MODEL), F32)],
        scratch_shapes=[pltpu.VMEM((2 * D_FF, D_MODEL), BF16), pltpu.VMEM((D_FF, D_MODEL), BF16),
                        pltpu.VMEM((FFN_HALO, 2 * D_FF), F32), pltpu.SemaphoreType.DMA],
        compiler_params=_params(),
    )(dx3, x2, uu_all, cc_all, g_ffn, gw, ffn_w)


def _bwd_attn(dx2, x1, q, kv, gw, g_x, after, seq, tm):
    tokens = x1.shape[0]
    n_tiles = tokens // tm
    tps = seq // tm
    n_b = tokens // seq

    def body(dx2_ref, x1_ref, q_ref, kv_ref, g_ref, gw_hbm, after_ref, dx1_ref, dx1b_ref, dq_ref, dkv_ref, dg_ref,
             wq_v, wo_v, sem):
        del after_ref
        i = pl.program_id(0)

        @pl.when(i == 0)
        def _():
            copies = _load_weight(gw_hbm, "w_q", wq_v, sem) + _load_weight(gw_hbm, "w_o", wo_v, sem)
            for cp in copies:
                cp.start()
            for cp in copies:
                cp.wait()
            dg_ref[...] = jnp.zeros_like(dg_ref)

        @pl.when(i % tps == 0)
        def _():
            dkv_ref[...] = jnp.zeros_like(dkv_ref)

        dx2v = dx2_ref[...]
        do = _dot_nt(dx2v.astype(BF16), wo_v[...]).astype(BF16)
        q = q_ref[...]
        dqs = []
        for h in range(HEADS):
            lo = h * HEAD_DIM
            kcols, vcols = pl.ds(lo, HEAD_DIM), pl.ds(D_MODEL + lo, HEAD_DIM)
            qh, doh = q[:, lo:lo + HEAD_DIM], do[:, lo:lo + HEAD_DIM]
            p = _softmax_rows(_dot_nt(qh, kv_ref[:, kcols]))
            dp = _dot_nt(doh, kv_ref[:, vcols])
            dkv_ref[:, vcols] += _dot_tn(p.astype(BF16), doh)
            ds = (p * (dp - jnp.sum(dp * p, axis=-1, keepdims=True))).astype(BF16)
            dqs.append(_dot(ds, kv_ref[:, kcols]) * (HEAD_DIM ** -0.5))
            dkv_ref[:, kcols] += _dot_tn(ds, qh)
        dq = jnp.concatenate(dqs, axis=-1).astype(BF16)
        dq_ref[...] = dq
        dh2 = _dot_nt(dq, wq_v[...])
        xh, r = _rms_fwd(x1_ref[...])
        dg_ref[...] += _colsum(dh2 * xh)
        dx1 = dx2v + _rms_bwd(dh2, xh, r, g_ref[...])
        dx1_ref[...] = dx1
        dx1b_ref[...] = dx1.astype(BF16)

    row = lambda w: pl.BlockSpec((tm, w), lambda i: (i, 0))
    per_b = pl.BlockSpec((N_MEM, 2 * D_MODEL), lambda i: (i // tps, 0))
    return pl.pallas_call(
        body, name="bwd_attn", grid=(n_tiles,),
        in_specs=[row(D_MODEL), row(D_MODEL), row(D_MODEL), per_b, _full((1, D_MODEL)), pl.BlockSpec(memory_space=pl.ANY),
                  _full(after.shape)],
        out_specs=[row(D_MODEL), row(D_MODEL), row(D_MODEL), per_b, _full((1, D_MODEL))],
        out_shape=[jax.ShapeDtypeStruct((tokens, D_MODEL), F32), jax.ShapeDtypeStruct((tokens, D_MODEL), BF16),
                   jax.ShapeDtypeStruct((tokens, D_MODEL), BF16),
                   jax.ShapeDtypeStruct((n_b * N_MEM, 2 * D_MODEL), F32), jax.ShapeDtypeStruct((1, D_MODEL), F32)],
        scratch_shapes=[pltpu.VMEM((D_MODEL, D_MODEL), BF16), pltpu.VMEM((D_MODEL, D_MODEL), BF16), pltpu.SemaphoreType.DMA],
        compiler_params=_params(),
    )(dx2, x1, q, kv, g_x, gw, after)


def _bwd_kv(dkv, mem2d, gw, g_mem):
    rows = mem2d.shape[0]
    n_b = rows // N_MEM

    def body(dkv_ref, mem_ref, gw_hbm, dkvb_ref, dg_ref, wkv_v, sem):
        @pl.when(pl.program_id(0) == 0)
        def _():
            copies = _load_weight(gw_hbm, "w_kv", wkv_v, sem)
            for cp in copies:
                cp.start()
            for cp in copies:
                cp.wait()
            dg_ref[...] = jnp.zeros_like(dg_ref)

        dkvb = dkv_ref[...].astype(BF16)
        dkvb_ref[...] = dkvb
        dmn = _dot(dkvb, wkv_v[...])
        mh, _ = _rms_fwd(mem_ref[...])
        dg_ref[...] += _colsum(dmn * mh)

    del g_mem
    return pl.pallas_call(
        body, name="bwd_kv", grid=(n_b,),
        in_specs=[pl.BlockSpec((N_MEM, 2 * D_MODEL), lambda b: (b, 0)), pl.BlockSpec((N_MEM, D_MODEL), lambda b: (b, 0)),
                  pl.BlockSpec(memory_space=pl.ANY)],
        out_specs=[pl.BlockSpec((N_MEM, 2 * D_MODEL), lambda b: (b, 0)), _full((1, D_MODEL))],
        out_shape=[jax.ShapeDtypeStruct((rows, 2 * D_MODEL), BF16), jax.ShapeDtypeStruct((1, D_MODEL), F32)],
        scratch_shapes=[pltpu.VMEM((2 * D_MODEL, D_MODEL), BF16), pltpu.SemaphoreType.DMA],
        compiler_params=_params(),
    )(dkv, mem2d, gw)


def _bwd_mix(dx1, x2d, u_all, c_all, pooled_all, gw, g_mix, conv_w, ln_g, ln_b, pool_w, pool_scale, after, seq, tm):
    tokens = x2d.shape[0]
    n_tiles = tokens // tm
    tps = seq // tm

    def body(dx1_ref, x_ref, u_ref, c_ref, pooled_ref, gmix_ref, gw_hbm, cw_ref, lng_ref, lnb_ref, pw_ref, ps_ref,
             after_ref, dx_ref, du_ref, dgmix_ref, dcw_ref, dcb_ref, dlng_ref, dlnb_ref, dpw_ref, dps_ref,
             win_v, wout_v, dc_carry, e_carry, sem):
        del after_ref
        i = pl.program_id(0)
        t = n_tiles - 1 - i

        @pl.when(i == 0)
        def _():
            copies = _load_weight(gw_hbm, "w_in", win_v, sem) + _load_weight(gw_hbm, "w_out", wout_v, sem)
            for cp in copies:
                cp.start()
            for cp in copies:
                cp.wait()
            for ref in (dgmix_ref, dcw_ref, dcb_ref, dlng_ref, dlnb_ref, dpw_ref, dps_ref):
                ref[...] = jnp.zeros_like(ref)

        @pl.when(t % tps == tps - 1)
        def _():
            dc_carry[...] = jnp.zeros_like(dc_carry)
            e_carry[...] = jnp.zeros_like(e_carry)

        dx1v = dx1_ref[...]
        dymix = _dot_nt(dx1v.astype(BF16), wout_v[...])
        dyc, dyp = dymix[:, :D_CONV], dymix[:, D_CONV:]
        u = u_ref[...]
        val, gate = u[:, :D_CONV], u[:, D_CONV:2 * D_CONV]

        conv = c_ref[...]
        mu = jnp.mean(conv, axis=-1, keepdims=True)
        cen = conv - mu
        rs = lax.rsqrt(jnp.mean(cen * cen, axis=-1, keepdims=True) + EPS)
        chat = cen * rs
        ln = chat * lng_ref[...] + lnb_ref[...]
        sl = _sigmoid(ln)
        dln = dyc * (sl * (1.0 + ln * (1.0 - sl)))
        dlng_ref[...] += _colsum(dln * chat)
        dlnb_ref[...] += _colsum(dln)
        dchat = dln * lng_ref[...]
        dc = rs * (dchat - jnp.mean(dchat, axis=-1, keepdims=True)
                   - chat * jnp.mean(dchat * chat, axis=-1, keepdims=True))
        dcb_ref[...] += _colsum(dc)
        sg = _sigmoid(gate)
        hc = val * sg
        ext = jnp.concatenate([dc, dc_carry[...]], axis=0)
        dc_carry[...] = dc[:CONV_HALO, :]
        dhc = jnp.zeros((tm, D_CONV), F32)
        ahead_by = _sublane_shifts(ext)
        for k in range(CONV_WIDTH):
            whole, part = divmod(CONV_WIDTH - 1 - k, 8)
            tap = ahead_by[part][8 * whole:8 * whole + tm, :]
            dhc = dhc + cw_ref[k:k + 1, :] * tap
            dcw_ref[k:k + 1, :] += _colsum(hc * tap)
        du_ref[:, :D_CONV] = (dhc * sg).astype(BF16)
        du_ref[:, D_CONV:2 * D_CONV] = (dhc * val * (sg * (1.0 - sg))).astype(BF16)

        pos = lax.broadcasted_iota(jnp.int32, (tm, 1), 0) + (t % tps) * tm
        es, dpooled = [], []
        for g, w in enumerate(POOL_WINDOWS):
            cols = pl.ds(g * POOL_GROUP_DIM, POOL_GROUP_DIM)
            lo = g * POOL_GROUP_DIM
            pooled = pooled_ref[:, cols]
            pw = pw_ref[g].astype(BF16)
            dyg = dyp[:, lo:lo + POOL_GROUP_DIM]
            dps_ref[:, cols] += _colsum(dyg * _dot(pooled, pw))
            dmixed = (dyg * ps_ref[:, cols]).astype(BF16)
            dpw_ref[g] += _dot_tn(pooled, dmixed)
            dpo = _dot_nt(dmixed, pw)
            dpooled.append(dpo)
            es.append(dpo / jnp.minimum(pos + 1, w).astype(F32))
        e = jnp.concatenate(es, axis=-1)
        run = jnp.concatenate([e, e_carry[...]], axis=0)
        e_carry[...] = e[:POOL_HALO, :]
        rows = tm + POOL_HALO
        for g, w in enumerate(POOL_WINDOWS):
            lo = g * POOL_GROUP_DIM
            run = run[:, POOL_GROUP_DIM if g else 0:]
            run = run + pltpu.roll(run, rows - w // 2, 0)
            du_ref[:, 2 * D_CONV + lo:2 * D_CONV + lo + POOL_GROUP_DIM] = (
                run[:tm, :POOL_GROUP_DIM] - dpooled[g]).astype(BF16)

        dh1 = _dot(du_ref[...], win_v[...])
        xh, r = _rms_fwd(x_ref[...])
        dgmix_ref[...] += _colsum(dh1 * xh)
        dx_ref[...] = dx1v + _rms_bwd(dh1, xh, r, gmix_ref[...])

    rev = lambda w: pl.BlockSpec((tm, w), lambda i: (n_tiles - 1 - i, 0))
    return pl.pallas_call(
        body, name="bwd_mix", grid=(n_tiles,),
        in_specs=[rev(D_MODEL), rev(D_MODEL), rev(D_IN), rev(D_CONV), rev(D_POOL), _full((1, D_MODEL)),
                  pl.BlockSpec(memory_space=pl.ANY), _full((CONV_WIDTH, D_CONV)), _full((1, D_CONV)), _full((1, D_CONV)),
                  _full((4, POOL_GROUP_DIM, POOL_GROUP_DIM)), _full((1, D_POOL)), _full(after.shape)],
        out_specs=[rev(D_MODEL), rev(D_IN), _full((1, D_MODEL)), _full((CONV_WIDTH, D_CONV)), _full((1, D_CONV)),
                   _full((1, D_CONV)), _full((1, D_CONV)), _full((4, POOL_GROUP_DIM, POOL_GROUP_DIM)), _full((1, D_POOL))],
        out_shape=[jax.ShapeDtypeStruct((tokens, D_MODEL), F32), jax.ShapeDtypeStruct((tokens, D_IN), BF16),
                   jax.ShapeDtypeStruct((1, D_MODEL), F32), jax.ShapeDtypeStruct((CONV_WIDTH, D_CONV), F32),
                   jax.ShapeDtypeStruct((1, D_CONV), F32), jax.ShapeDtypeStruct((1, D_CONV), F32),
                   jax.ShapeDtypeStruct((1, D_CONV), F32),
                   jax.ShapeDtypeStruct((4, POOL_GROUP_DIM, POOL_GROUP_DIM), F32), jax.ShapeDtypeStruct((1, D_POOL), F32)],
        scratch_shapes=[pltpu.VMEM((D_IN, D_MODEL), BF16), pltpu.VMEM((D_MODEL, D_MODEL), BF16),
                        pltpu.VMEM((CONV_HALO, D_CONV), F32), pltpu.VMEM((POOL_HALO, D_POOL), F32),
                        pltpu.SemaphoreType.DMA],
        compiler_params=_params(),
    )(dx1, x2d, u_all, c_all, pooled_all, g_mix, gw, conv_w, ln_g, ln_b, pool_w, pool_scale, after)


def _wgrad(a, b, name, after=None, tm=256):
    tokens, m = a.shape
    n = b.shape[1]
    extra = [] if after is None else [after]

    def body(a_ref, b_ref, *rest):
        rest[-1][...] = _dot_tn(a_ref[...], b_ref[...]).astype(rest[-1].dtype)

    return pl.pallas_call(
        body, name=name, grid=(m // tm,),
        in_specs=[pl.BlockSpec((tokens, tm), lambda i: (0, i)), _full((tokens, n))] + [_full(t.shape) for t in extra],
        out_specs=pl.BlockSpec((tm, n), lambda i: (i, 0)),
        out_shape=jax.ShapeDtypeStruct((m, n), BF16),
        compiler_params=_params(),
    )(a, b, *extra)


def _adamw_update(w, g, m, v):
    nm = ADAM_B1 * m + (1.0 - ADAM_B1) * g
    nv = ADAM_B2 * v + (1.0 - ADAM_B2) * (g * g)
    m_hat = nm / (1.0 - ADAM_B1 ** ADAM_STEP)
    v_hat = nv / (1.0 - ADAM_B2 ** ADAM_STEP)
    return -ADAM_LR * (m_hat / (jnp.sqrt(v_hat) + ADAM_EPS) + ADAM_WD * w), nm, nv


def _adamw_small(ws, gs, ms, vs):
    n = len(ws)

    def body(*refs):
        ins, outs = refs[:4 * n], refs[4 * n:]
        for k in range(n):
            d, nm, nv = _adamw_update(*[ins[j * n + k][...] for j in range(4)])
            outs[k][...] = d
            outs[n + k][...] = nm
            outs[2 * n + k][...] = nv

    vmem = pl.BlockSpec(memory_space=pltpu.VMEM)
    outs = pl.pallas_call(
        body, name="adamw_small",
        in_specs=[vmem] * (4 * n), out_specs=[vmem] * (3 * n),
        out_shape=[jax.ShapeDtypeStruct(w.shape, F32) for w in ws] * 3,
    )(*ws, *gs, *ms, *vs)
    return outs[:n], outs[n:2 * n], outs[2 * n:]


def _adamw(w, g, m, v, name):
    rows, cols = w.shape
    tile = rows
    for cand in (512, 256, 128, 64, 32, 16, 8):
        if rows % cand == 0:
            tile = cand
            break

    def body(w_ref, g_ref, m_ref, v_ref, d_ref, nm_ref, nv_ref):
        d_ref[...], nm_ref[...], nv_ref[...] = _adamw_update(w_ref[...], g_ref[...], m_ref[...], v_ref[...])

    spec = pl.BlockSpec((tile, cols), lambda i: (i, 0))
    return pl.pallas_call(
        body, name=name, grid=(rows // tile,),
        in_specs=[spec] * 4, out_specs=[spec] * 3,
        out_shape=[jax.ShapeDtypeStruct((rows, cols), F32)] * 3,
        compiler_params=_params(("arbitrary",)),
    )(w, g, m, v)


SMALL = (("norm_mix_g", (1, 1024)), ("conv_dw_b", (1, 512)), ("conv_ln_g", (1, 512)), ("conv_ln_b", (1, 512)),
         ("pool_w", (1, 4, 128, 128)), ("pool_scale", (1, 512)), ("norm_xattn_g", (1, 1024)), ("norm_mem_g", (1, 1024)),
         ("norm_ffn_g", (1, 1024)), ("ffn_dw_b", (1, 5632)), ("norm_final_g", (1024,)))
LANES = 128


def _pack_rows(arrs):
    flat = jnp.concatenate([a.reshape(-1) for a in arrs])
    pad = (-flat.shape[0]) % (8 * LANES)
    return jnp.pad(flat, (0, pad)).reshape(-1, LANES)


def kernel(x, mem, norm_mix_g, w_in, conv_dw_w, conv_dw_b, conv_ln_g, conv_ln_b, pool_w, pool_scale, w_out, norm_xattn_g, norm_mem_g, w_q, w_kv, w_o, norm_ffn_g, w_up, ffn_dw_w, ffn_dw_b, w_down, norm_final_g, loss_target, m_norm_mix_g, m_w_in, m_conv_dw_w, m_conv_dw_b, m_conv_ln_g, m_conv_ln_b, m_pool_w, m_pool_scale, m_w_out, m_norm_xattn_g, m_norm_mem_g, m_w_q, m_w_kv, m_w_o, m_norm_ffn_g, m_w_up, m_ffn_dw_w, m_ffn_dw_b, m_w_down, m_norm_final_g, v_norm_mix_g, v_w_in, v_conv_dw_w, v_conv_dw_b, v_conv_ln_g, v_conv_ln_b, v_pool_w, v_pool_scale, v_w_out, v_norm_xattn_g, v_norm_mem_g, v_w_q, v_w_kv, v_w_o, v_norm_ffn_g, v_w_up, v_ffn_dw_w, v_ffn_dw_b, v_w_down, v_norm_final_g):
    weights = dict(norm_mix_g=norm_mix_g, w_in=w_in, conv_dw_w=conv_dw_w, conv_dw_b=conv_dw_b, conv_ln_g=conv_ln_g,
                   conv_ln_b=conv_ln_b, pool_w=pool_w, pool_scale=pool_scale, w_out=w_out, norm_xattn_g=norm_xattn_g,
                   norm_mem_g=norm_mem_g, w_q=w_q, w_kv=w_kv, w_o=w_o, norm_ffn_g=norm_ffn_g, w_up=w_up,
                   ffn_dw_w=ffn_dw_w, ffn_dw_b=ffn_dw_b, w_down=w_down, norm_final_g=norm_final_g)
    moments_m = dict(norm_mix_g=m_norm_mix_g, w_in=m_w_in, conv_dw_w=m_conv_dw_w, conv_dw_b=m_conv_dw_b,
                     conv_ln_g=m_conv_ln_g, conv_ln_b=m_conv_ln_b, pool_w=m_pool_w, pool_scale=m_pool_scale,
                     w_out=m_w_out, norm_xattn_g=m_norm_xattn_g, norm_mem_g=m_norm_mem_g, w_q=m_w_q, w_kv=m_w_kv,
                     w_o=m_w_o, norm_ffn_g=m_norm_ffn_g, w_up=m_w_up, ffn_dw_w=m_ffn_dw_w, ffn_dw_b=m_ffn_dw_b,
                     w_down=m_w_down, norm_final_g=m_norm_final_g)
    moments_v = dict(norm_mix_g=v_norm_mix_g, w_in=v_w_in, conv_dw_w=v_conv_dw_w, conv_dw_b=v_conv_dw_b,
                     conv_ln_g=v_conv_ln_g, conv_ln_b=v_conv_ln_b, pool_w=v_pool_w, pool_scale=v_pool_scale,
                     w_out=v_w_out, norm_xattn_g=v_norm_xattn_g, norm_mem_g=v_norm_mem_g, w_q=v_w_q, w_kv=v_w_kv,
                     w_o=v_w_o, norm_ffn_g=v_norm_ffn_g, w_up=v_w_up, ffn_dw_w=v_ffn_dw_w, ffn_dw_b=v_ffn_dw_b,
                     w_down=v_w_down, norm_final_g=v_norm_final_g)
    order = list(weights)
    transposed = ("w_in", "w_kv", "w_up")

    n_b, seq, _ = x.shape
    tokens = n_b * seq
    tm_mix = min(512, seq // 2)
    tm_ffn = min(256, seq // 2)
    dev = 4 * lax.axis_index("x") + 2 * lax.axis_index("y") + lax.axis_index("c")

    packs = [jnp.concatenate([weights[n][0].T if n in transposed else weights[n][0] for n in names], axis=0).astype(BF16)
             for names in AG_GROUPS]
    small_sharded = _pack_rows([conv_dw_w[0], ffn_dw_w[0]])
    gw_mix, gsmall = _all_gather([packs[0], small_sharded], "weights_all_gather")
    flights = []
    after = gw_mix
    for k in (1, 2):
        own_in_place = lax.dynamic_update_slice(lax.empty((N_DEV,) + packs[k].shape, BF16), packs[k][None], (dev, 0, 0))
        flights.append(_gather_start(own_in_place, after, "weights_gather_start_%d" % k))
        after = flights[-1][3]
    gflat = gsmall.reshape(N_DEV, -1)
    n_cw = CONV_WIDTH * (D_CONV // N_DEV)
    n_fw = FFN_CONV_WIDTH * (2 * D_FF // N_DEV)
    conv_w = gflat[:, :n_cw].reshape(N_DEV, CONV_WIDTH, D_CONV // N_DEV).transpose(1, 0, 2).reshape(CONV_WIDTH, D_CONV)
    ffn_w = gflat[:, n_cw:n_cw + n_fw].reshape(N_DEV, FFN_CONV_WIDTH, 2 * D_FF // N_DEV).transpose(1, 0, 2).reshape(
        FFN_CONV_WIDTH, 2 * D_FF)

    x2d = x.reshape(tokens, D_MODEL)
    mem2d = mem.reshape(n_b * N_MEM, D_MODEL)
    tgt2d = loss_target.reshape(tokens, D_MODEL)
    g_final = norm_final_g.reshape(1, D_MODEL)

    def gather_finish(flight, after, tag):
        fwd_send, fwd_recv, buf = _gather_forward(*flight[:3], after, "weights_gather_forward_" + tag)
        return _gather_finish(fwd_send, fwd_recv, buf, "weights_gather_finish_" + tag)

    x1, u_all, c_all, pooled_all, ymix, h1 = _fwd_mix(
        x2d, gw_mix, norm_mix_g, conv_w, conv_dw_b, conv_ln_g, conv_ln_b, pool_w[0], pool_scale, flights[1][3],
        seq, tm_mix)
    gw_attn = gather_finish(flights[0], x1, "1")
    mem_n, kv = _fwd_kv(mem2d, gw_attn, norm_mem_g)
    x2, h2, q, o = _fwd_attn(x1, kv, gw_attn, norm_xattn_g, seq, tm_mix)
    gw_ffn = gather_finish(flights[1], x2, "2")
    uu_all, cc_all, a_all, h3, dx3, dx3b, loss_part, dg_final = _fwd_ffn(
        x2, tgt2d, gw_ffn, norm_ffn_g, ffn_w, ffn_dw_b, g_final, seq, tm_ffn)

    table = _owner_table()

    def reduce_start(names, tag):
        parts = [part[n].reshape(N_DEV, W_OFF[n][1], D_MODEL) for n in names]
        landed = _exchange_sibling(parts, "rs_sibling_exchange_" + tag)
        sums = _chip_partial_sums(table, parts, landed, "rs_chip_partial_sums_" + tag)
        return parts, landed, _chip_exchange_start(sums, "rs_chip_exchange_start_" + tag)

    def reduce_finish(names, parts, landed, flight, after, tag):
        from_chips = _chip_exchange_wait(*flight[:4], after, "rs_chip_exchange_wait_" + tag)
        sums = _final_grad_sums(table, parts, landed, from_chips, "rs_final_sums_" + tag)
        g_mine.update(zip(names, sums))
        return sums[-1]

    part, g_mine = {}, {}
    dx2, dx2b, duu, d_ffn_b, d_ffn_w, dg_ffn = _bwd_ffn(dx3, x2, uu_all, cc_all, gw_ffn, norm_ffn_g, ffn_w, seq, tm_ffn)
    part["w_up"] = _wgrad(duu, h3, "wgrad_w_up")
    part["w_down"] = _wgrad(a_all, dx3b, "wgrad_w_down")
    parts_a, landed_a, flight_a = reduce_start(RS_GROUPS["a"], "a")
    dx1, dx1b, dq, dkv, dg_x = _bwd_attn(dx2, x1, q, kv, gw_attn, norm_xattn_g, flight_a[4], seq, tm_mix)
    dkv_b, dg_mem = _bwd_kv(dkv, mem2d, gw_attn, norm_mem_g)
    part["w_q"] = _wgrad(h2, dq, "wgrad_w_q")
    part["w_kv"] = _wgrad(dkv_b, mem_n, "wgrad_w_kv")
    part["w_o"] = _wgrad(o, dx2b, "wgrad_w_o")
    parts_b, landed_b, flight_b = reduce_start(RS_GROUPS["b"], "b")
    dx, du, dg_mix, d_conv_w, d_conv_b, d_ln_g, d_ln_b, d_pool_w, d_pool_scale = _bwd_mix(
        dx1, x2d, u_all, c_all, pooled_all, gw_mix, norm_mix_g, conv_w, conv_ln_g, conv_ln_b, pool_w[0], pool_scale,
        flight_b[4], seq, tm_mix)
    grad_x = dx.reshape(x.shape)

    small_grads = dict(norm_mix_g=dg_mix, conv_dw_b=d_conv_b, conv_ln_g=d_ln_g, conv_ln_b=d_ln_b, pool_w=d_pool_w,
                       pool_scale=d_pool_scale, norm_xattn_g=dg_x, norm_mem_g=dg_mem, norm_ffn_g=dg_ffn,
                       ffn_dw_b=d_ffn_b, norm_final_g=dg_final)
    small_list = [small_grads[n] for n, _ in SMALL] + [d_conv_w, d_ffn_w, loss_part[:1]]
    small_mine = _pack_rows(small_list)
    small_flight = _broadcast_start(
        lax.dynamic_update_slice(lax.empty((N_DEV,) + small_mine.shape, F32), small_mine[None], (dev, 0, 0)),
        "small_grads_broadcast_start")

    part["w_in"] = _wgrad(du, h1, "wgrad_w_in", after=small_flight[3])
    part["w_out"] = _wgrad(ymix, dx1b, "wgrad_w_out")
    parts_c, landed_c, flight_c = reduce_start(RS_GROUPS["c"], "c")
    grads, delta, new_m, new_v = {}, {}, {}, {}

    def adamw_group(names):
        for n in names:
            shape = weights[n].shape
            as2d = lambda t: t.reshape(shape[1], shape[2])
            grads[n] = (g_mine[n].T if n in transposed else g_mine[n])[None]
            d, nm, nv = _adamw(as2d(weights[n]), as2d(grads[n]), as2d(moments_m[n]), as2d(moments_v[n]), "adamw_" + n)
            delta[n], new_m[n], new_v[n] = d.reshape(shape), nm.reshape(shape), nv.reshape(shape)
        return delta[names[-1]]

    reduce_finish(RS_GROUPS["a"], parts_a, landed_a, flight_a, flight_c[4], "a")
    updated_a = adamw_group(RS_GROUPS["a"])
    reduce_finish(RS_GROUPS["b"], parts_b, landed_b, flight_b, updated_a, "b")
    updated_b = adamw_group(RS_GROUPS["b"])
    reduce_finish(RS_GROUPS["c"], parts_c, landed_c, flight_c, updated_b, "c")
    updated_c = adamw_group(RS_GROUPS["c"])

    small_all = _broadcast_wait(*small_flight[:3], updated_c, "small_grads_broadcast_wait")
    small_sum = _sum_blocks(small_all).reshape(-1)

    pos = 0
    for n, shape in SMALL:
        size = 1
        for s in shape:
            size *= s
        grads[n] = small_sum[pos:pos + size].reshape(shape)
        pos += size
    full_conv_w = small_sum[pos:pos + CONV_WIDTH * D_CONV].reshape(CONV_WIDTH, D_CONV)
    pos += CONV_WIDTH * D_CONV
    full_ffn_w = small_sum[pos:pos + FFN_CONV_WIDTH * 2 * D_FF].reshape(FFN_CONV_WIDTH, 2 * D_FF)
    loss = small_sum[pos + FFN_CONV_WIDTH * 2 * D_FF]
    grads["conv_dw_w"] = lax.dynamic_slice_in_dim(full_conv_w, dev * (D_CONV // N_DEV), D_CONV // N_DEV, axis=1)[None]
    grads["ffn_dw_w"] = lax.dynamic_slice_in_dim(full_ffn_w, dev * (2 * D_FF // N_DEV), 2 * D_FF // N_DEV, axis=1)[None]

    small_names = [n for n in order if n not in W_OFF]
    two_d = lambda t: t.reshape(1, -1) if t.ndim == 1 else t
    outs = _adamw_small(*[[two_d(t[n]) for n in small_names] for t in (weights, grads, moments_m, moments_v)])
    for res, out in zip((delta, new_m, new_v), outs):
        for n, o in zip(small_names, out):
            res[n] = o.reshape(weights[n].shape)

    return (loss, grad_x, *[grads[n] for n in order], *[delta[n] for n in order],
            *[new_m[n] for n in order], *[new_v[n] for n in order])
```

```python
import functools

import jax
import jax.numpy as jnp
from jax import lax
from jax.experimental import pallas as pl
from jax.experimental.pallas import tpu as pltpu

F32 = jnp.float32
BF16 = jnp.bfloat16
MESH = pl.DeviceIdType.MESH

N_DEV = 8
D_MODEL = 1024
D_CONV = 512
D_POOL = 512
CONV_WIDTH = 31
POOL_WINDOWS = (2, 4, 8, 16)
POOL_GROUP_DIM = 128
D_IN = 1536
N_MEM = 256
HEADS = 4
HEAD_DIM = 256
D_FF = 2816
FFN_CONV_WIDTH = 3
EPS = 1e-6
ADAM_LR = 0.001
ADAM_B1 = 0.9
ADAM_B2 = 0.999
ADAM_EPS = 1e-08
ADAM_WD = 0.01
ADAM_STEP = 10

VMEM_LIMIT_V7X = 56 * 1024 * 1024
CONV_HALO = 32
POOL_HALO = 16
FFN_HALO = 8
FFN_CHUNK = 2816

W_ROWS = (("w_in", 192), ("w_out", 128), ("w_q", 128), ("w_kv", 256), ("w_o", 128), ("w_up", 704), ("w_down", 352))
AG_GROUPS = (("w_in", "w_out"), ("w_q", "w_kv", "w_o"), ("w_up", "w_down"))
W_OFF = {}
for _names in AG_GROUPS:
    _o = 0
    for _n in _names:
        W_OFF[_n] = (_o, dict(W_ROWS)[_n])
        _o += dict(W_ROWS)[_n]
RS_GROUPS = {"a": ("w_up", "w_down"), "b": ("w_q", "w_kv", "w_o"), "c": ("w_in", "w_out")}


def _dot(a, b):
    return jnp.dot(a, b, preferred_element_type=F32)


def _dot_nt(a, b):
    return lax.dot_general(a, b, (((1,), (1,)), ((), ())), preferred_element_type=F32)


def _dot_tn(a, b):
    return lax.dot_general(a, b, (((0,), (0,)), ((), ())), preferred_element_type=F32)


def _sigmoid(v):
    return 1.0 / (1.0 + jnp.exp(-v))


def _rms_fwd(v):
    r = lax.rsqrt(jnp.mean(v * v, axis=-1, keepdims=True) + EPS)
    return v * r, r


def _rms_bwd(dh, vh, r, g):
    gd = dh * g
    return r * (gd - vh * jnp.mean(gd * vh, axis=-1, keepdims=True))


def _sublane_shifts(v):
    rows = v.shape[0]
    return [v] + [pltpu.roll(v, rows - b, 0) for b in range(1, 8)]


def _colsum(v):
    return jnp.sum(v, axis=0, keepdims=True)


def _full(shape):
    return pl.BlockSpec(shape, lambda *_: (0,) * len(shape))


def _params(sem=("arbitrary",), vmem=VMEM_LIMIT_V7X):
    return pltpu.CompilerParams(dimension_semantics=sem, vmem_limit_bytes=vmem)


def _load_weight(g_hbm, name, dst, sem):
    off, rows = W_OFF[name]
    return [pltpu.make_async_copy(g_hbm.at[d, pl.ds(off, rows), :], dst.at[pl.ds(d * rows, rows), :], sem)
            for d in range(N_DEV)]


def _position():
    x, y, c = lax.axis_index("x"), lax.axis_index("y"), lax.axis_index("c")
    chips = [(1 - x, y), (x, 1 - y), (1 - x, 1 - y)]
    return x, y, c, chips


def _dev(px, py, pc):
    return 4 * px + 2 * py + pc


def _all_gather(arrs, name):
    n = len(arrs)

    def body(*refs):
        ins, outs = refs[:n], refs[n:2 * n]
        send_sems, recv_sems, local_sems = refs[2 * n:2 * n + 3]
        bounce = refs[2 * n + 3:]
        x, y, c, chips = _position()
        me, sibling = (x, y, c), (x, y, 1 - c)

        def copy(a, k, block, to, src=None):
            rows = outs[a].at[_dev(*block)]
            return pltpu.make_async_remote_copy(
                src_ref=rows if src is None else src, dst_ref=rows,
                send_sem=send_sems.at[a, k], recv_sem=recv_sems.at[a, k], device_id=to, device_id_type=MESH)

        sends = []
        for a in range(n):
            first = [copy(a, 0, me, sibling, src=ins[a])]
            first += [copy(a, 1 + j, me, (*chip, c), src=ins[a]) for j, chip in enumerate(chips)]
            for cp in first:
                cp.start()
            sends += first
        started = []
        for a in range(n):
            load = pltpu.make_async_copy(ins[a], bounce[a], local_sems.at[a, 0])
            load.start()
            load.wait()
            mine = pltpu.make_async_copy(bounce[a], outs[a].at[_dev(*me)], local_sems.at[a, 1])
            mine.start()
            started.append(mine)
        for j, chip in enumerate(chips):
            for a in range(n):
                copy(a, 1 + j, (*chip, c), me).wait_recv()
                passed = copy(a, 4 + j, (*chip, c), sibling)
                passed.start()
                sends.append(passed)
        for a in range(n):
            copy(a, 0, sibling, me).wait_recv()
            for j, chip in enumerate(chips):
                copy(a, 4 + j, (*chip, 1 - c), me).wait_recv()
        for cp in sends:
            cp.wait_send()
        for mine in started:
            mine.wait()

    any_spec = pl.BlockSpec(memory_space=pl.ANY)
    return pl.pallas_call(
        body, name=name,
        out_shape=[jax.ShapeDtypeStruct((N_DEV,) + a.shape, a.dtype) for a in arrs],
        in_specs=[any_spec] * n, out_specs=[any_spec] * n,
        scratch_shapes=[pltpu.SemaphoreType.DMA((n, 7)), pltpu.SemaphoreType.DMA((n, 7)), pltpu.SemaphoreType.DMA((n, 2))]
        + [pltpu.VMEM(a.shape, a.dtype) for a in arrs],
    )(*arrs)


_HBM = pl.BlockSpec(memory_space=pltpu.HBM)
_SEM = pl.BlockSpec(memory_space=pltpu.SEMAPHORE)
_SIDE_EFFECT = pltpu.SideEffectType.DATAFLOW_SIDE_EFFECTING


def _gather_start(buf, after, name):
    def body(buf_ref, after_ref, send_sems, recv_sems, buf_thru, token):
        del after_ref, buf_thru
        x, y, c, chips = _position()
        rows = buf_ref.at[_dev(x, y, c)]
        for k, to in enumerate([(x, y, 1 - c)] + [(*chip, c) for chip in chips]):
            pltpu.make_async_remote_copy(src_ref=rows, dst_ref=rows, send_sem=send_sems.at[k], recv_sem=recv_sems.at[k],
                                         device_id=to, device_id_type=MESH).start()
        token[...] = jnp.zeros_like(token)

    return pl.pallas_call(
        body, name=name,
        out_shape=(pltpu.SemaphoreType.DMA((4,)), pltpu.SemaphoreType.DMA((4,)), pltpu.HBM(buf.shape, buf.dtype),
                   jax.ShapeDtypeStruct((8, 128), F32)),
        in_specs=(_HBM, pl.BlockSpec(memory_space=pl.ANY)),
        out_specs=(_SEM, _SEM, _HBM, pl.BlockSpec(memory_space=pltpu.VMEM)),
        input_output_aliases={0: 2},
        compiler_params=pltpu.CompilerParams(has_side_effects=_SIDE_EFFECT),
    )(pltpu.with_memory_space_constraint(buf, pltpu.HBM), after)


def _gather_forward(send_sems, recv_sems, buf, after, name):
    def body(buf_ref, send_sems, recv_sems, after_ref, fwd_send, fwd_recv, buf_thru):
        del after_ref, buf_thru
        x, y, c, chips = _position()
        sibling = (x, y, 1 - c)

        def copy(block, k, sends, recvs):
            rows = buf_ref.at[_dev(*block)]
            return pltpu.make_async_remote_copy(src_ref=rows, dst_ref=rows, send_sem=sends.at[k], recv_sem=recvs.at[k],
                                                device_id=sibling, device_id_type=MESH)

        for k in range(4):
            copy((x, y, c), k, send_sems, recv_sems).wait_send()
        copy(sibling, 0, send_sems, recv_sems).wait_recv()
        for j, chip in enumerate(chips):
            copy((*chip, c), 1 + j, send_sems, recv_sems).wait_recv()
            copy((*chip, c), j, fwd_send, fwd_recv).start()

    return pl.pallas_call(
        body, name=name,
        out_shape=(pltpu.SemaphoreType.DMA((3,)), pltpu.SemaphoreType.DMA((3,)), pltpu.HBM(buf.shape, buf.dtype)),
        in_specs=(_HBM, _SEM, _SEM, pl.BlockSpec(memory_space=pl.ANY)), out_specs=(_SEM, _SEM, _HBM),
        input_output_aliases={0: 2},
        compiler_params=pltpu.CompilerParams(has_side_effects=_SIDE_EFFECT),
    )(buf, send_sems, recv_sems, after)


def _gather_finish(fwd_send, fwd_recv, buf, name):
    def body(buf_ref, fwd_send, fwd_recv, buf_thru):
        del buf_thru
        x, y, c, chips = _position()
        for j, chip in enumerate(chips):
            cp = pltpu.make_async_remote_copy(
                src_ref=buf_ref.at[_dev(*chip, c)], dst_ref=buf_ref.at[_dev(*chip, 1 - c)], send_sem=fwd_send.at[j],
                recv_sem=fwd_recv.at[j], device_id=(x, y, 1 - c), device_id_type=MESH)
            cp.wait_send()
            cp.wait_recv()

    return pl.pallas_call(
        body, name=name,
        out_shape=pltpu.HBM(buf.shape, buf.dtype),
        in_specs=(_HBM, _SEM, _SEM), out_specs=_HBM,
        input_output_aliases={0: 0},
        compiler_params=pltpu.CompilerParams(has_side_effects=_SIDE_EFFECT),
    )(buf, fwd_send, fwd_recv)


def _everyone_else(x, y, c, chips):
    return [(x, y, 1 - c)] + [(*chip, core) for chip in chips for core in (c, 1 - c)]


def _broadcast_start(buf, name):
    def body(buf_ref, send_sems, recv_sems, buf_thru, token):
        del buf_thru
        x, y, c, chips = _position()
        rows = buf_ref.at[_dev(x, y, c)]
        for k, to in enumerate(_everyone_else(x, y, c, chips)):
            pltpu.make_async_remote_copy(src_ref=rows, dst_ref=rows, send_sem=send_sems.at[k], recv_sem=recv_sems.at[k],
                                         device_id=to, device_id_type=MESH).start()
        token[...] = jnp.zeros_like(token)

    return pl.pallas_call(
        body, name=name,
        out_shape=(pltpu.SemaphoreType.DMA((7,)), pltpu.SemaphoreType.DMA((7,)), pltpu.HBM(buf.shape, buf.dtype),
                   jax.ShapeDtypeStruct((8, 128), F32)),
        in_specs=(_HBM,), out_specs=(_SEM, _SEM, _HBM, pl.BlockSpec(memory_space=pltpu.VMEM)),
        input_output_aliases={0: 2},
        compiler_params=pltpu.CompilerParams(has_side_effects=_SIDE_EFFECT),
    )(pltpu.with_memory_space_constraint(buf, pltpu.HBM))


def _broadcast_wait(send_sems, recv_sems, buf, after, name):
    def body(buf_ref, send_sems, recv_sems, after_ref, buf_thru):
        del after_ref, buf_thru
        x, y, c, chips = _position()
        for k, peer in enumerate(_everyone_else(x, y, c, chips)):
            cp = pltpu.make_async_remote_copy(
                src_ref=buf_ref.at[_dev(x, y, c)], dst_ref=buf_ref.at[_dev(*peer)], send_sem=send_sems.at[k],
                recv_sem=recv_sems.at[k], device_id=peer, device_id_type=MESH)
            cp.wait_send()
            cp.wait_recv()

    return pl.pallas_call(
        body, name=name,
        out_shape=pltpu.HBM(buf.shape, buf.dtype),
        in_specs=(_HBM, _SEM, _SEM, pl.BlockSpec(memory_space=pl.ANY)), out_specs=_HBM,
        input_output_aliases={0: 0},
        compiler_params=pltpu.CompilerParams(has_side_effects=_SIDE_EFFECT),
    )(buf, send_sems, recv_sems, after)


def _to_sibling(j, x, y, c, chips):
    return _dev(*([(x, y)] + chips)[j], 1 - c), (x, y, 1 - c)


def _to_chip(j, x, y, c, chips):
    return j, (*chips[j], c)


def _exchange_start(srcs, n_slots, route, name):
    n = len(srcs)

    def body(*refs):
        s_refs, land_refs = refs[:n], refs[n:2 * n]
        send_sems, recv_sems = refs[2 * n:2 * n + 2]
        token = refs[-1]
        x, y, c, chips = _position()
        for k in range(n):
            for j in range(n_slots):
                block, to = route(j, x, y, c, chips)
                pltpu.make_async_remote_copy(
                    src_ref=s_refs[k].at[block], dst_ref=land_refs[k].at[j], send_sem=send_sems.at[n_slots * k + j],
                    recv_sem=recv_sems.at[n_slots * k + j], device_id=to, device_id_type=MESH).start()
        token[...] = jnp.zeros_like(token)

    lands = [jax.ShapeDtypeStruct((n_slots,) + s.shape[1:], s.dtype) for s in srcs]
    outs = pl.pallas_call(
        body, name=name,
        out_shape=(pltpu.SemaphoreType.DMA((n_slots * n,)), pltpu.SemaphoreType.DMA((n_slots * n,)),
                   *[pltpu.HBM(s.shape, s.dtype) for s in srcs], *[pltpu.HBM(l.shape, l.dtype) for l in lands],
                   jax.ShapeDtypeStruct((8, 128), F32)),
        in_specs=[_HBM] * (2 * n), out_specs=(_SEM, _SEM, *[_HBM] * (2 * n), pl.BlockSpec(memory_space=pltpu.VMEM)),
        input_output_aliases={k: 2 + k for k in range(2 * n)},
        compiler_params=pltpu.CompilerParams(has_side_effects=_SIDE_EFFECT),
    )(*[pltpu.with_memory_space_constraint(s, pltpu.HBM) for s in srcs],
      *[pltpu.with_memory_space_constraint(lax.empty(l.shape, l.dtype), pltpu.HBM) for l in lands])
    return outs[0], outs[1], outs[2:2 + n], outs[2 + n:2 + 2 * n], outs[-1]


def _exchange_wait(send_sems, recv_sems, s_thru, land_thru, after, n_slots, route, name):
    n = len(s_thru)

    def body(*refs):
        s_refs, land_refs = refs[:n], refs[n:2 * n]
        send_sems, recv_sems = refs[2 * n:2 * n + 2]
        x, y, c, chips = _position()
        for k in range(n):
            for j in range(n_slots):
                block, to = route(j, x, y, c, chips)
                cp = pltpu.make_async_remote_copy(
                    src_ref=s_refs[k].at[block], dst_ref=land_refs[k].at[j], send_sem=send_sems.at[n_slots * k + j],
                    recv_sem=recv_sems.at[n_slots * k + j], device_id=to, device_id_type=MESH)
                cp.wait_send()
                cp.wait_recv()

    outs = pl.pallas_call(
        body, name=name,
        out_shape=(*[pltpu.HBM(s.shape, s.dtype) for s in s_thru], *[pltpu.HBM(l.shape, l.dtype) for l in land_thru]),
        in_specs=[_HBM] * (2 * n) + [_SEM, _SEM, pl.BlockSpec(memory_space=pl.ANY)], out_specs=[_HBM] * (2 * n),
        input_output_aliases={k: k for k in range(2 * n)},
        compiler_params=pltpu.CompilerParams(has_side_effects=_SIDE_EFFECT),
    )(*s_thru, *land_thru, send_sems, recv_sems, after)
    return outs[:n], outs[n:]


def _owner_table():
    x, y, c = lax.axis_index("x"), lax.axis_index("y"), lax.axis_index("c")
    chips = [(x, y), (1 - x, y), (x, 1 - y), (1 - x, 1 - y)]
    return jnp.stack([_dev(px, py, c) for px, py in chips]).astype(jnp.int32)


def _chip_partial_sums(table, parts, from_sibling, name):
    n = len(parts)

    def body(tab_ref, *refs):
        del tab_ref
        for g_ref, l_ref, out_ref in zip(refs[:n], refs[n:2 * n], refs[2 * n:]):
            out_ref[...] = (g_ref[...].astype(F32) + l_ref[...].astype(F32)).astype(out_ref.dtype)

    block = lambda p: (None,) + p.shape[1:]
    grid_spec = pltpu.PrefetchScalarGridSpec(
        num_scalar_prefetch=1, grid=(3,),
        in_specs=[pl.BlockSpec(block(p), lambda j, tab: (tab[j + 1], 0, 0)) for p in parts]
        + [pl.BlockSpec(block(p), lambda j, tab: (j + 1, 0, 0)) for p in parts],
        out_specs=[pl.BlockSpec(block(p), lambda j, tab: (j, 0, 0)) for p in parts])
    return pl.pallas_call(
        body, name=name, grid_spec=grid_spec,
        out_shape=[jax.ShapeDtypeStruct((3,) + p.shape[1:], BF16) for p in parts],
        compiler_params=_params(("arbitrary",)),
    )(table, *parts, *from_sibling)


def _final_grad_sums(table, parts, from_sibling, from_chips, name):
    n = len(parts)

    def body(tab_ref, *refs):
        del tab_ref
        for g_ref, l_ref, c_ref, out_ref in zip(refs[:n], refs[n:2 * n], refs[2 * n:3 * n], refs[3 * n:]):
            acc = g_ref[...].astype(F32) + l_ref[...].astype(F32)
            for j in range(3):
                acc = acc + c_ref[j].astype(F32)
            out_ref[...] = acc

    half = lambda p: (p.shape[1] // 2, p.shape[2])
    grid_spec = pltpu.PrefetchScalarGridSpec(
        num_scalar_prefetch=1, grid=(2,),
        in_specs=[pl.BlockSpec((None,) + half(p), lambda t, tab: (tab[0], t, 0)) for p in parts]
        + [pl.BlockSpec((None,) + half(p), lambda t, tab: (0, t, 0)) for p in parts]
        + [pl.BlockSpec((3,) + half(p), lambda t, tab: (0, t, 0)) for p in parts],
        out_specs=[pl.BlockSpec(half(p), lambda t, tab: (t, 0)) for p in parts])
    return pl.pallas_call(
        body, name=name, grid_spec=grid_spec,
        out_shape=[jax.ShapeDtypeStruct(p.shape[1:], F32) for p in parts],
        compiler_params=_params(("arbitrary",)),
    )(table, *parts, *from_sibling, *from_chips)


def _sum_blocks(g8):
    _, rows, cols = g8.shape

    def body(g_ref, out_ref):
        acc = g_ref[0]
        for d in range(1, N_DEV):
            acc = acc + g_ref[d]
        out_ref[...] = acc

    return pl.pallas_call(
        body, name="small_grad_sum", grid=(1,),
        in_specs=[_full((N_DEV, rows, cols))], out_specs=_full((rows, cols)),
        out_shape=jax.ShapeDtypeStruct((rows, cols), F32),
        compiler_params=_params(("arbitrary",)),
    )(g8)


def _fwd_mix(x2d, gw, g_mix, conv_w, conv_b, ln_g, ln_b, pool_w, pool_scale, after, seq, tm):
    tokens = x2d.shape[0]
    n_tiles = tokens // tm
    tps = seq // tm

    def body(x_ref, gmix_ref, gw_hbm, cw_ref, cb_ref, lng_ref, lnb_ref, pw_ref, ps_ref, after_ref,
             x1_ref, u_ref, c_ref, pooled_ref, ymix_ref, h1_ref,
             win_v, wout_v, hc_carry, up_carry, sem):
        del after_ref
        i = pl.program_id(0)

        @pl.when(i == 0)
        def _():
            copies = _load_weight(gw_hbm, "w_in", win_v, sem) + _load_weight(gw_hbm, "w_out", wout_v, sem)
            for cp in copies:
                cp.start()
            for cp in copies:
                cp.wait()

        @pl.when(i % tps == 0)
        def _():
            hc_carry[...] = jnp.zeros_like(hc_carry)
            up_carry[...] = jnp.zeros_like(up_carry)

        x = x_ref[...]
        xh, _ = _rms_fwd(x)
        h1 = (xh * gmix_ref[...]).astype(BF16)
        h1_ref[...] = h1
        u = _dot_nt(h1, win_v[...])
        u_ref[...] = u
        val, gate, up = u[:, :D_CONV], u[:, D_CONV:2 * D_CONV], u[:, 2 * D_CONV:]

        hc = val * _sigmoid(gate)
        ext = jnp.concatenate([hc_carry[...], hc], axis=0)
        hc_carry[...] = hc[tm - CONV_HALO:, :]
        conv = jnp.broadcast_to(cb_ref[...], (tm, D_CONV))
        ahead_by = _sublane_shifts(ext)
        for k in range(CONV_WIDTH):
            whole, part = divmod(CONV_HALO - (CONV_WIDTH - 1) + k, 8)
            conv = conv + cw_ref[k:k + 1, :] * ahead_by[part][8 * whole:8 * whole + tm, :]
        c_ref[...] = conv
        mu = jnp.mean(conv, axis=-1, keepdims=True)
        cen = conv - mu
        ln = cen * lax.rsqrt(jnp.mean(cen * cen, axis=-1, keepdims=True) + EPS) * lng_ref[...] + lnb_ref[...]
        y_conv = ln * _sigmoid(ln)

        extp = jnp.concatenate([up_carry[...], up], axis=0)
        up_carry[...] = up[tm - POOL_HALO:, :]
        pos = lax.broadcasted_iota(jnp.int32, (tm, 1), 0) + (i % tps) * tm
        run = extp
        mixed = []
        for g, w in enumerate(POOL_WINDOWS):
            lo = g * POOL_GROUP_DIM
            run = run[:, POOL_GROUP_DIM if g else 0:]
            run = run + pltpu.roll(run, w // 2, 0)
            cnt = jnp.minimum(pos + 1, w).astype(F32)
            pooled = run[POOL_HALO:, :POOL_GROUP_DIM] / cnt - up[:, lo:lo + POOL_GROUP_DIM]
            pooled = pooled.astype(BF16)
            pooled_ref[:, lo:lo + POOL_GROUP_DIM] = pooled
            mixed.append(_dot(pooled, pw_ref[g].astype(BF16)))
        y_pool = jnp.concatenate(mixed, axis=-1) * ps_ref[...]

        ymix = jnp.concatenate([y_conv, y_pool], axis=-1).astype(BF16)
        ymix_ref[...] = ymix
        x1_ref[...] = x + _dot(ymix, wout_v[...])

    row = lambda w: pl.BlockSpec((tm, w), lambda i: (i, 0))
    return pl.pallas_call(
        body, name="fwd_mix", grid=(n_tiles,),
        in_specs=[row(D_MODEL), _full((1, D_MODEL)), pl.BlockSpec(memory_space=pl.ANY),
                  _full((CONV_WIDTH, D_CONV)), _full((1, D_CONV)), _full((1, D_CONV)), _full((1, D_CONV)),
                  _full((4, POOL_GROUP_DIM, POOL_GROUP_DIM)), _full((1, D_POOL)), _full(after.shape)],
        out_specs=[row(D_MODEL), row(D_IN), row(D_CONV), row(D_POOL), row(D_MODEL), row(D_MODEL)],
        out_shape=[jax.ShapeDtypeStruct((tokens, D_MODEL), F32), jax.ShapeDtypeStruct((tokens, D_IN), F32),
                   jax.ShapeDtypeStruct((tokens, D_CONV), F32), jax.ShapeDtypeStruct((tokens, D_POOL), BF16),
                   jax.ShapeDtypeStruct((tokens, D_MODEL), BF16), jax.ShapeDtypeStruct((tokens, D_MODEL), BF16)],
        scratch_shapes=[pltpu.VMEM((D_IN, D_MODEL), BF16), pltpu.VMEM((D_MODEL, D_MODEL), BF16),
                        pltpu.VMEM((CONV_HALO, D_CONV), F32), pltpu.VMEM((POOL_HALO, D_POOL), F32),
                        pltpu.SemaphoreType.DMA],
        compiler_params=_params(),
    )(x2d, g_mix, gw, conv_w, conv_b, ln_g, ln_b, pool_w, pool_scale, after)


def _fwd_kv(mem2d, gw, g_mem):
    rows = mem2d.shape[0]
    n_b = rows // N_MEM

    def body(mem_ref, g_ref, gw_hbm, mn_ref, kv_ref, wkv_v, sem):
        @pl.when(pl.program_id(0) == 0)
        def _():
            copies = _load_weight(gw_hbm, "w_kv", wkv_v, sem)
            for cp in copies:
                cp.start()
            for cp in copies:
                cp.wait()

        mh, _ = _rms_fwd(mem_ref[...])
        mn = (mh * g_ref[...]).astype(BF16)
        mn_ref[...] = mn
        kv_ref[...] = _dot_nt(mn, wkv_v[...]).astype(BF16)

    return pl.pallas_call(
        body, name="fwd_kv", grid=(n_b,),
        in_specs=[pl.BlockSpec((N_MEM, D_MODEL), lambda b: (b, 0)), _full((1, D_MODEL)), pl.BlockSpec(memory_space=pl.ANY)],
        out_specs=[pl.BlockSpec((N_MEM, D_MODEL), lambda b: (b, 0)), pl.BlockSpec((N_MEM, 2 * D_MODEL), lambda b: (b, 0))],
        out_shape=[jax.ShapeDtypeStruct((rows, D_MODEL), BF16), jax.ShapeDtypeStruct((rows, 2 * D_MODEL), BF16)],
        scratch_shapes=[pltpu.VMEM((2 * D_MODEL, D_MODEL), BF16), pltpu.SemaphoreType.DMA],
        compiler_params=_params(),
    )(mem2d, g_mem, gw)


def _softmax_rows(s):
    e = jnp.exp(s - jnp.max(s, axis=-1, keepdims=True))
    return e / jnp.sum(e, axis=-1, keepdims=True)


def _fwd_attn(x1, kv, gw, g_x, seq, tm):
    tokens = x1.shape[0]
    n_tiles = tokens // tm
    tps = seq // tm

    def body(x1_ref, kv_ref, g_ref, gw_hbm, x2_ref, h2_ref, q_ref, o_ref, wq_v, wo_v, sem):
        @pl.when(pl.program_id(0) == 0)
        def _():
            copies = _load_weight(gw_hbm, "w_q", wq_v, sem) + _load_weight(gw_hbm, "w_o", wo_v, sem)
            for cp in copies:
                cp.start()
            for cp in copies:
                cp.wait()

        x1v = x1_ref[...]
        xh, _ = _rms_fwd(x1v)
        h2 = (xh * g_ref[...]).astype(BF16)
        h2_ref[...] = h2
        q = (_dot(h2, wq_v[...]) * (HEAD_DIM ** -0.5)).astype(BF16)
        q_ref[...] = q
        outs = []
        for h in range(HEADS):
            lo = h * HEAD_DIM
            p = _softmax_rows(_dot_nt(q[:, lo:lo + HEAD_DIM], kv_ref[:, lo:lo + HEAD_DIM]))
            outs.append(_dot(p.astype(BF16), kv_ref[:, D_MODEL + lo:D_MODEL + lo + HEAD_DIM]))
        o = jnp.concatenate(outs, axis=-1).astype(BF16)
        o_ref[...] = o
        x2_ref[...] = x1v + _dot(o, wo_v[...])

    row = lambda w: pl.BlockSpec((tm, w), lambda i: (i, 0))
    return pl.pallas_call(
        body, name="fwd_attn", grid=(n_tiles,),
        in_specs=[row(D_MODEL), pl.BlockSpec((N_MEM, 2 * D_MODEL), lambda i: (i // tps, 0)), _full((1, D_MODEL)),
                  pl.BlockSpec(memory_space=pl.ANY)],
        out_specs=[row(D_MODEL)] * 4,
        out_shape=[jax.ShapeDtypeStruct((tokens, D_MODEL), F32)] + [jax.ShapeDtypeStruct((tokens, D_MODEL), BF16)] * 3,
        scratch_shapes=[pltpu.VMEM((D_MODEL, D_MODEL), BF16), pltpu.VMEM((D_MODEL, D_MODEL), BF16), pltpu.SemaphoreType.DMA],
        compiler_params=_params(),
    )(x1, kv, g_x, gw)


def _ffn_conv(uu, halo, w_ref, b_ref, cols):
    ext = jnp.concatenate([halo, uu], axis=0)
    p1 = pltpu.roll(ext, 1, 0)[FFN_HALO:, :]
    p2 = pltpu.roll(ext, 2, 0)[FFN_HALO:, :]
    return b_ref[:, cols] + w_ref[2:3, cols] * uu + w_ref[1:2, cols] * p1 + w_ref[0:1, cols] * p2


def _fwd_ffn(x2, target, gw, g_ffn, ffn_w, ffn_b, g_final, seq, tm):
    tokens = x2.shape[0]
    n_tiles = tokens // tm
    tps = seq // tm
    n_chunks = D_FF // FFN_CHUNK

    def body(x2_ref, tgt_ref, gffn_ref, gw_hbm, fw_ref, fb_ref, gfin_ref,
             uu_ref, cc_ref, a_ref, h3_ref, dx3_ref, dx3b_ref, loss_ref, dgfin_ref,
             wup_v, wdown_v, carry, sem):
        i = pl.program_id(0)

        @pl.when(i == 0)
        def _():
            copies = _load_weight(gw_hbm, "w_up", wup_v, sem) + _load_weight(gw_hbm, "w_down", wdown_v, sem)
            for cp in copies:
                cp.start()
            for cp in copies:
                cp.wait()
            loss_ref[...] = jnp.zeros_like(loss_ref)
            dgfin_ref[...] = jnp.zeros_like(dgfin_ref)

        @pl.when(i % tps == 0)
        def _():
            carry[...] = jnp.zeros_like(carry)

        x2v = x2_ref[...]
        xh, _ = _rms_fwd(x2v)
        h3 = (xh * gffn_ref[...]).astype(BF16)
        h3_ref[...] = h3
        acc = jnp.zeros((tm, D_MODEL), F32)
        for jc in range(n_chunks):
            halves = []
            for half in range(2):
                cols = pl.ds(half * D_FF + jc * FFN_CHUNK, FFN_CHUNK)
                uu = _dot_nt(h3, wup_v[cols, :])
                uu_ref[:, cols] = uu.astype(BF16)
                cc = _ffn_conv(uu, carry[:, cols], fw_ref, fb_ref, cols)
                cc_ref[:, cols] = cc.astype(BF16)
                halves.append(cc)
                carry[:, cols] = uu[tm - FFN_HALO:, :]
            gate, val = halves
            a = (gate * _sigmoid(gate) * val).astype(BF16)
            a_ref[:, pl.ds(jc * FFN_CHUNK, FFN_CHUNK)] = a
            acc = acc + _dot(a, wdown_v[pl.ds(jc * FFN_CHUNK, FFN_CHUNK), :])
        x3 = x2v + acc

        xh3, r3 = _rms_fwd(x3)
        gfin = gfin_ref[...]
        err = xh3 * gfin - tgt_ref[...]
        loss_ref[...] += jnp.full(loss_ref.shape, jnp.sum(err * err) * (0.5 / D_MODEL), F32)
        dy = err * (1.0 / D_MODEL)
        dgfin_ref[...] += _colsum(dy * xh3)
        dx3 = _rms_bwd(dy, xh3, r3, gfin)
        dx3_ref[...] = dx3
        dx3b_ref[...] = dx3.astype(BF16)

    row = lambda w: pl.BlockSpec((tm, w), lambda i: (i, 0))
    return pl.pallas_call(
        body, name="fwd_ffn", grid=(n_tiles,),
        in_specs=[row(D_MODEL), row(D_MODEL), _full((1, D_MODEL)), pl.BlockSpec(memory_space=pl.ANY),
                  _full((FFN_CONV_WIDTH, 2 * D_FF)), _full((1, 2 * D_FF)), _full((1, D_MODEL))],
        out_specs=[row(2 * D_FF), row(2 * D_FF), row(D_FF), row(D_MODEL), row(D_MODEL), row(D_MODEL), _full((8, 128)),
                   _full((1, D_MODEL))],
        out_shape=[jax.ShapeDtypeStruct((tokens, 2 * D_FF), BF16), jax.ShapeDtypeStruct((tokens, 2 * D_FF), BF16),
                   jax.ShapeDtypeStruct((tokens, D_FF), BF16),
                   jax.ShapeDtypeStruct((tokens, D_MODEL), BF16), jax.ShapeDtypeStruct((tokens, D_MODEL), F32),
                   jax.ShapeDtypeStruct((tokens, D_MODEL), BF16),
                   jax.ShapeDtypeStruct((8, 128), F32), jax.ShapeDtypeStruct((1, D_MODEL), F32)],
        scratch_shapes=[pltpu.VMEM((2 * D_FF, D_MODEL), BF16), pltpu.VMEM((D_FF, D_MODEL), BF16),
                        pltpu.VMEM((FFN_HALO, 2 * D_FF), F32), pltpu.SemaphoreType.DMA],
        compiler_params=_params(),
    )(x2, target, g_ffn, gw, ffn_w, ffn_b, g_final)


def _bwd_ffn(dx3, x2, uu_all, cc_all, gw, g_ffn, ffn_w, seq, tm):
    tokens = x2.shape[0]
    n_tiles = tokens // tm
    tps = seq // tm
    n_chunks = D_FF // FFN_CHUNK

    def body(dx3_ref, x2_ref, uu_ref, cc_ref, gffn_ref, gw_hbm, fw_ref,
             dx2_ref, dx2b_ref, duu_ref, dfb_ref, dfw_ref, dg_ref,
             wup_v, wdown_v, carry, sem):
        i = pl.program_id(0)
        t = n_tiles - 1 - i

        @pl.when(i == 0)
        def _():
            copies = _load_weight(gw_hbm, "w_up", wup_v, sem) + _load_weight(gw_hbm, "w_down", wdown_v, sem)
            for cp in copies:
                cp.start()
            for cp in copies:
                cp.wait()
            dfb_ref[...] = jnp.zeros_like(dfb_ref)
            dfw_ref[...] = jnp.zeros_like(dfw_ref)
            dg_ref[...] = jnp.zeros_like(dg_ref)

        @pl.when(t % tps == tps - 1)
        def _():
            carry[...] = jnp.zeros_like(carry)

        dx3v = dx3_ref[...]
        dx3b = dx3v.astype(BF16)
        dh3 = jnp.zeros((tm, D_MODEL), F32)
        for jc in range(n_chunks):
            da = _dot_nt(dx3b, wdown_v[pl.ds(jc * FFN_CHUNK, FFN_CHUNK), :])
            colss = [pl.ds(half * D_FF + jc * FFN_CHUNK, FFN_CHUNK) for half in range(2)]
            gate, val = [cc_ref[:, cols].astype(F32) for cols in colss]
            sg = _sigmoid(gate)
            dgate = da * val * (sg * (1.0 + gate * (1.0 - sg)))
            dval = da * (gate * sg)
            for dcc, cols in zip((dgate, dval), colss):
                uu = uu_ref[:, cols].astype(F32)
                dfb_ref[:, cols] += _colsum(dcc)
                ext = jnp.concatenate([dcc, carry[:, cols]], axis=0)
                carry[:, cols] = dcc[:FFN_HALO, :]
                n1 = pltpu.roll(ext, tm + FFN_HALO - 1, 0)[:tm, :]
                n2 = pltpu.roll(ext, tm + FFN_HALO - 2, 0)[:tm, :]
                duu = fw_ref[2:3, cols] * dcc + fw_ref[1:2, cols] * n1 + fw_ref[0:1, cols] * n2
                dfw_ref[2:3, cols] += _colsum(uu * dcc)
                dfw_ref[1:2, cols] += _colsum(uu * n1)
                dfw_ref[0:1, cols] += _colsum(uu * n2)
                duub = duu.astype(BF16)
                duu_ref[:, cols] = duub
                dh3 = dh3 + _dot(duub, wup_v[cols, :])
        xh, r = _rms_fwd(x2_ref[...])
        dg_ref[...] += _colsum(dh3 * xh)
        dx2 = dx3v + _rms_bwd(dh3, xh, r, gffn_ref[...])
        dx2_ref[...] = dx2
        dx2b_ref[...] = dx2.astype(BF16)

    rev = lambda w: pl.BlockSpec((tm, w), lambda i: (n_tiles - 1 - i, 0))
    return pl.pallas_call(
        body, name="bwd_ffn", grid=(n_tiles,),
        in_specs=[rev(D_MODEL), rev(D_MODEL), rev(2 * D_FF), rev(2 * D_FF), _full((1, D_MODEL)),
                  pl.BlockSpec(memory_space=pl.ANY), _full((FFN_CONV_WIDTH, 2 * D_FF))],
        out_specs=[rev(D_MODEL), rev(D_MODEL), rev(2 * D_FF), _full((1, 2 * D_FF)), _full((FFN_CONV_WIDTH, 2 * D_FF)),
                   _full((1, D_MODEL))],
        out_shape=[jax.ShapeDtypeStruct((tokens, D_MODEL), F32), jax.ShapeDtypeStruct((tokens, D_MODEL), BF16),
                   jax.ShapeDtypeStruct((tokens, 2 * D_FF), BF16),
                   jax.ShapeDtypeStruct((1, 2 * D_FF), F32), jax.ShapeDtypeStruct((FFN_CONV_WIDTH, 2 * D_FF), F32),
                   jax.ShapeDtypeStruct((1, D_MODEL), F32)],
        scratch_shapes=[pltpu.VMEM((2 * D_FF, D_MODEL), BF16), pltpu.VMEM((D_FF, D_MODEL), BF16),
                        pltpu.VMEM((FFN_HALO, 2 * D_FF), F32), pltpu.SemaphoreType.DMA],
        compiler_params=_params(),
    )(dx3, x2, uu_all, cc_all, g_ffn, gw, ffn_w)


def _bwd_attn(dx2, x1, q, kv, gw, g_x, after, seq, tm):
    tokens = x1.shape[0]
    n_tiles = tokens // tm
    tps = seq // tm
    n_b = tokens // seq

    def body(dx2_ref, x1_ref, q_ref, kv_ref, g_ref, gw_hbm, after_ref, dx1_ref, dx1b_ref, dq_ref, dkv_ref, dg_ref,
             wq_v, wo_v, sem):
        del after_ref
        i = pl.program_id(0)

        @pl.when(i == 0)
        def _():
            copies = _load_weight(gw_hbm, "w_q", wq_v, sem) + _load_weight(gw_hbm, "w_o", wo_v, sem)
            for cp in copies:
                cp.start()
            for cp in copies:
                cp.wait()
            dg_ref[...] = jnp.zeros_like(dg_ref)

        @pl.when(i % tps == 0)
        def _():
            dkv_ref[...] = jnp.zeros_like(dkv_ref)

        dx2v = dx2_ref[...]
        do = _dot_nt(dx2v.astype(BF16), wo_v[...]).astype(BF16)
        q = q_ref[...]
        dqs = []
        for h in range(HEADS):
            lo = h * HEAD_DIM
            kcols, vcols = pl.ds(lo, HEAD_DIM), pl.ds(D_MODEL + lo, HEAD_DIM)
            qh, doh = q[:, lo:lo + HEAD_DIM], do[:, lo:lo + HEAD_DIM]
            p = _softmax_rows(_dot_nt(qh, kv_ref[:, kcols]))
            dp = _dot_nt(doh, kv_ref[:, vcols])
            dkv_ref[:, vcols] += _dot_tn(p.astype(BF16), doh)
            ds = (p * (dp - jnp.sum(dp * p, axis=-1, keepdims=True))).astype(BF16)
            dqs.append(_dot(ds, kv_ref[:, kcols]) * (HEAD_DIM ** -0.5))
            dkv_ref[:, kcols] += _dot_tn(ds, qh)
        dq = jnp.concatenate(dqs, axis=-1).astype(BF16)
        dq_ref[...] = dq
        dh2 = _dot_nt(dq, wq_v[...])
        xh, r = _rms_fwd(x1_ref[...])
        dg_ref[...] += _colsum(dh2 * xh)
        dx1 = dx2v + _rms_bwd(dh2, xh, r, g_ref[...])
        dx1_ref[...] = dx1
        dx1b_ref[...] = dx1.astype(BF16)

    row = lambda w: pl.BlockSpec((tm, w), lambda i: (i, 0))
    per_b = pl.BlockSpec((N_MEM, 2 * D_MODEL), lambda i: (i // tps, 0))
    return pl.pallas_call(
        body, name="bwd_attn", grid=(n_tiles,),
        in_specs=[row(D_MODEL), row(D_MODEL), row(D_MODEL), per_b, _full((1, D_MODEL)), pl.BlockSpec(memory_space=pl.ANY),
                  _full(after.shape)],
        out_specs=[row(D_MODEL), row(D_MODEL), row(D_MODEL), per_b, _full((1, D_MODEL))],
        out_shape=[jax.ShapeDtypeStruct((tokens, D_MODEL), F32), jax.ShapeDtypeStruct((tokens, D_MODEL), BF16),
                   jax.ShapeDtypeStruct((tokens, D_MODEL), BF16),
                   jax.ShapeDtypeStruct((n_b * N_MEM, 2 * D_MODEL), F32), jax.ShapeDtypeStruct((1, D_MODEL), F32)],
        scratch_shapes=[pltpu.VMEM((D_MODEL, D_MODEL), BF16), pltpu.VMEM((D_MODEL, D_MODEL), BF16), pltpu.SemaphoreType.DMA],
        compiler_params=_params(),
    )(dx2, x1, q, kv, g_x, gw, after)


def _bwd_kv(dkv, mem2d, gw, g_mem):
    rows = mem2d.shape[0]
    n_b = rows // N_MEM

    def body(dkv_ref, mem_ref, gw_hbm, dkvb_ref, dg_ref, wkv_v, sem):
        @pl.when(pl.program_id(0) == 0)
        def _():
            copies = _load_weight(gw_hbm, "w_kv", wkv_v, sem)
            for cp in copies:
                cp.start()
            for cp in copies:
                cp.wait()
            dg_ref[...] = jnp.zeros_like(dg_ref)

        dkvb = dkv_ref[...].astype(BF16)
        dkvb_ref[...] = dkvb
        dmn = _dot(dkvb, wkv_v[...])
        mh, _ = _rms_fwd(mem_ref[...])
        dg_ref[...] += _colsum(dmn * mh)

    del g_mem
    return pl.pallas_call(
        body, name="bwd_kv", grid=(n_b,),
        in_specs=[pl.BlockSpec((N_MEM, 2 * D_MODEL), lambda b: (b, 0)), pl.BlockSpec((N_MEM, D_MODEL), lambda b: (b, 0)),
                  pl.BlockSpec(memory_space=pl.ANY)],
        out_specs=[pl.BlockSpec((N_MEM, 2 * D_MODEL), lambda b: (b, 0)), _full((1, D_MODEL))],
        out_shape=[jax.ShapeDtypeStruct((rows, 2 * D_MODEL), BF16), jax.ShapeDtypeStruct((1, D_MODEL), F32)],
        scratch_shapes=[pltpu.VMEM((2 * D_MODEL, D_MODEL), BF16), pltpu.SemaphoreType.DMA],
        compiler_params=_params(),
    )(dkv, mem2d, gw)


def _bwd_mix(dx1, x2d, u_all, c_all, pooled_all, gw, g_mix, conv_w, ln_g, ln_b, pool_w, pool_scale, after, seq, tm):
    tokens = x2d.shape[0]
    n_tiles = tokens // tm
    tps = seq // tm

    def body(dx1_ref, x_ref, u_ref, c_ref, pooled_ref, gmix_ref, gw_hbm, cw_ref, lng_ref, lnb_ref, pw_ref, ps_ref,
             after_ref, dx_ref, du_ref, dgmix_ref, dcw_ref, dcb_ref, dlng_ref, dlnb_ref, dpw_ref, dps_ref,
             win_v, wout_v, dc_carry, e_carry, sem):
        del after_ref
        i = pl.program_id(0)
        t = n_tiles - 1 - i

        @pl.when(i == 0)
        def _():
            copies = _load_weight(gw_hbm, "w_in", win_v, sem) + _load_weight(gw_hbm, "w_out", wout_v, sem)
            for cp in copies:
                cp.start()
            for cp in copies:
                cp.wait()
            for ref in (dgmix_ref, dcw_ref, dcb_ref, dlng_ref, dlnb_ref, dpw_ref, dps_ref):
                ref[...] = jnp.zeros_like(ref)

        @pl.when(t % tps == tps - 1)
        def _():
            dc_carry[...] = jnp.zeros_like(dc_carry)
            e_carry[...] = jnp.zeros_like(e_carry)

        dx1v = dx1_ref[...]
        dymix = _dot_nt(dx1v.astype(BF16), wout_v[...])
        dyc, dyp = dymix[:, :D_CONV], dymix[:, D_CONV:]
        u = u_ref[...]
        val, gate = u[:, :D_CONV], u[:, D_CONV:2 * D_CONV]

        conv = c_ref[...]
        mu = jnp.mean(conv, axis=-1, keepdims=True)
        cen = conv - mu
        rs = lax.rsqrt(jnp.mean(cen * cen, axis=-1, keepdims=True) + EPS)
        chat = cen * rs
        ln = chat * lng_ref[...] + lnb_ref[...]
        sl = _sigmoid(ln)
        dln = dyc * (sl * (1.0 + ln * (1.0 - sl)))
        dlng_ref[...] += _colsum(dln * chat)
        dlnb_ref[...] += _colsum(dln)
        dchat = dln * lng_ref[...]
        dc = rs * (dchat - jnp.mean(dchat, axis=-1, keepdims=True)
                   - chat * jnp.mean(dchat * chat, axis=-1, keepdims=True))
        dcb_ref[...] += _colsum(dc)
        sg = _sigmoid(gate)
        hc = val * sg
        ext = jnp.concatenate([dc, dc_carry[...]], axis=0)
        dc_carry[...] = dc[:CONV_HALO, :]
        dhc = jnp.zeros((tm, D_CONV), F32)
        ahead_by = _sublane_shifts(ext)
        for k in range(CONV_WIDTH):
            whole, part = divmod(CONV_WIDTH - 1 - k, 8)
            tap = ahead_by[part][8 * whole:8 * whole + tm, :]
            dhc = dhc + cw_ref[k:k + 1, :] * tap
            dcw_ref[k:k + 1, :] += _colsum(hc * tap)
        du_ref[:, :D_CONV] = (dhc * sg).astype(BF16)
        du_ref[:, D_CONV:2 * D_CONV] = (dhc * val * (sg * (1.0 - sg))).astype(BF16)

        pos = lax.broadcasted_iota(jnp.int32, (tm, 1), 0) + (t % tps) * tm
        es, dpooled = [], []
        for g, w in enumerate(POOL_WINDOWS):
            cols = pl.ds(g * POOL_GROUP_DIM, POOL_GROUP_DIM)
            lo = g * POOL_GROUP_DIM
            pooled = pooled_ref[:, cols]
            pw = pw_ref[g].astype(BF16)
            dyg = dyp[:, lo:lo + POOL_GROUP_DIM]
            dps_ref[:, cols] += _colsum(dyg * _dot(pooled, pw))
            dmixed = (dyg * ps_ref[:, cols]).astype(BF16)
            dpw_ref[g] += _dot_tn(pooled, dmixed)
            dpo = _dot_nt(dmixed, pw)
            dpooled.append(dpo)
            es.append(dpo / jnp.minimum(pos + 1, w).astype(F32))
        e = jnp.concatenate(es, axis=-1)
        run = jnp.concatenate([e, e_carry[...]], axis=0)
        e_carry[...] = e[:POOL_HALO, :]
        rows = tm + POOL_HALO
        for g, w in enumerate(POOL_WINDOWS):
            lo = g * POOL_GROUP_DIM
            run = run[:, POOL_GROUP_DIM if g else 0:]
            run = run + pltpu.roll(run, rows - w // 2, 0)
            du_ref[:, 2 * D_CONV + lo:2 * D_CONV + lo + POOL_GROUP_DIM] = (
                run[:tm, :POOL_GROUP_DIM] - dpooled[g]).astype(BF16)

        dh1 = _dot(du_ref[...], win_v[...])
        xh, r = _rms_fwd(x_ref[...])
        dgmix_ref[...] += _colsum(dh1 * xh)
        dx_ref[...] = dx1v + _rms_bwd(dh1, xh, r, gmix_ref[...])

    rev = lambda w: pl.BlockSpec((tm, w), lambda i: (n_tiles - 1 - i, 0))
    return pl.pallas_call(
        body, name="bwd_mix", grid=(n_tiles,),
        in_specs=[rev(D_MODEL), rev(D_MODEL), rev(D_IN), rev(D_CONV), rev(D_POOL), _full((1, D_MODEL)),
                  pl.BlockSpec(memory_space=pl.ANY), _full((CONV_WIDTH, D_CONV)), _full((1, D_CONV)), _full((1, D_CONV)),
                  _full((4, POOL_GROUP_DIM, POOL_GROUP_DIM)), _full((1, D_POOL)), _full(after.shape)],
        out_specs=[rev(D_MODEL), rev(D_IN), _full((1, D_MODEL)), _full((CONV_WIDTH, D_CONV)), _full((1, D_CONV)),
                   _full((1, D_CONV)), _full((1, D_CONV)), _full((4, POOL_GROUP_DIM, POOL_GROUP_DIM)), _full((1, D_POOL))],
        out_shape=[jax.ShapeDtypeStruct((tokens, D_MODEL), F32), jax.ShapeDtypeStruct((tokens, D_IN), BF16),
                   jax.ShapeDtypeStruct((1, D_MODEL), F32), jax.ShapeDtypeStruct((CONV_WIDTH, D_CONV), F32),
                   jax.ShapeDtypeStruct((1, D_CONV), F32), jax.ShapeDtypeStruct((1, D_CONV), F32),
                   jax.ShapeDtypeStruct((1, D_CONV), F32),
                   jax.ShapeDtypeStruct((4, POOL_GROUP_DIM, POOL_GROUP_DIM), F32), jax.ShapeDtypeStruct((1, D_POOL), F32)],
        scratch_shapes=[pltpu.VMEM((D_IN, D_MODEL), BF16), pltpu.VMEM((D_MODEL, D_MODEL), BF16),
                        pltpu.VMEM((CONV_HALO, D_CONV), F32), pltpu.VMEM((POOL_HALO, D_POOL), F32),
                        pltpu.SemaphoreType.DMA],
        compiler_params=_params(),
    )(dx1, x2d, u_all, c_all, pooled_all, g_mix, gw, conv_w, ln_g, ln_b, pool_w, pool_scale, after)


def _wgrad(a, b, name, after=None, tm=256):
    tokens, m = a.shape
    n = b.shape[1]
    extra = [] if after is None else [after]

    def body(a_ref, b_ref, *rest):
        rest[-1][...] = _dot_tn(a_ref[...], b_ref[...]).astype(rest[-1].dtype)

    return pl.pallas_call(
        body, name=name, grid=(m // tm,),
        in_specs=[pl.BlockSpec((tokens, tm), lambda i: (0, i)), _full((tokens, n))] + [_full(t.shape) for t in extra],
        out_specs=pl.BlockSpec((tm, n), lambda i: (i, 0)),
        out_shape=jax.ShapeDtypeStruct((m, n), BF16),
        compiler_params=_params(),
    )(a, b, *extra)


def _adamw_update(w, g, m, v):
    nm = ADAM_B1 * m + (1.0 - ADAM_B1) * g
    nv = ADAM_B2 * v + (1.0 - ADAM_B2) * (g * g)
    m_hat = nm / (1.0 - ADAM_B1 ** ADAM_STEP)
    v_hat = nv / (1.0 - ADAM_B2 ** ADAM_STEP)
    return -ADAM_LR * (m_hat / (jnp.sqrt(v_hat) + ADAM_EPS) + ADAM_WD * w), nm, nv


def _adamw_small(ws, gs, ms, vs):
    n = len(ws)

    def body(*refs):
        ins, outs = refs[:4 * n], refs[4 * n:]
        for k in range(n):
            d, nm, nv = _adamw_update(*[ins[j * n + k][...] for j in range(4)])
            outs[k][...] = d
            outs[n + k][...] = nm
            outs[2 * n + k][...] = nv

    vmem = pl.BlockSpec(memory_space=pltpu.VMEM)
    outs = pl.pallas_call(
        body, name="adamw_small",
        in_specs=[vmem] * (4 * n), out_specs=[vmem] * (3 * n),
        out_shape=[jax.ShapeDtypeStruct(w.shape, F32) for w in ws] * 3,
    )(*ws, *gs, *ms, *vs)
    return outs[:n], outs[n:2 * n], outs[2 * n:]


def _adamw(w, g, m, v, name):
    rows, cols = w.shape
    tile = rows
    for cand in (512, 256, 128, 64, 32, 16, 8):
        if rows % cand == 0:
            tile = cand
            break

    def body(w_ref, g_ref, m_ref, v_ref, d_ref, nm_ref, nv_ref):
        d_ref[...], nm_ref[...], nv_ref[...] = _adamw_update(w_ref[...], g_ref[...], m_ref[...], v_ref[...])

    spec = pl.BlockSpec((tile, cols), lambda i: (i, 0))
    return pl.pallas_call(
        body, name=name, grid=(rows // tile,),
        in_specs=[spec] * 4, out_specs=[spec] * 3,
        out_shape=[jax.ShapeDtypeStruct((rows, cols), F32)] * 3,
        compiler_params=_params(("arbitrary",)),
    )(w, g, m, v)


SMALL = (("norm_mix_g", (1, 1024)), ("conv_dw_b", (1, 512)), ("conv_ln_g", (1, 512)), ("conv_ln_b", (1, 512)),
         ("pool_w", (1, 4, 128, 128)), ("pool_scale", (1, 512)), ("norm_xattn_g", (1, 1024)), ("norm_mem_g", (1, 1024)),
         ("norm_ffn_g", (1, 1024)), ("ffn_dw_b", (1, 5632)), ("norm_final_g", (1024,)))
LANES = 128


def _pack_rows(arrs):
    flat = jnp.concatenate([a.reshape(-1) for a in arrs])
    pad = (-flat.shape[0]) % (8 * LANES)
    return jnp.pad(flat, (0, pad)).reshape(-1, LANES)


def kernel(x, mem, norm_mix_g, w_in, conv_dw_w, conv_dw_b, conv_ln_g, conv_ln_b, pool_w, pool_scale, w_out, norm_xattn_g, norm_mem_g, w_q, w_kv, w_o, norm_ffn_g, w_up, ffn_dw_w, ffn_dw_b, w_down, norm_final_g, loss_target, m_norm_mix_g, m_w_in, m_conv_dw_w, m_conv_dw_b, m_conv_ln_g, m_conv_ln_b, m_pool_w, m_pool_scale, m_w_out, m_norm_xattn_g, m_norm_mem_g, m_w_q, m_w_kv, m_w_o, m_norm_ffn_g, m_w_up, m_ffn_dw_w, m_ffn_dw_b, m_w_down, m_norm_final_g, v_norm_mix_g, v_w_in, v_conv_dw_w, v_conv_dw_b, v_conv_ln_g, v_conv_ln_b, v_pool_w, v_pool_scale, v_w_out, v_norm_xattn_g, v_norm_mem_g, v_w_q, v_w_kv, v_w_o, v_norm_ffn_g, v_w_up, v_ffn_dw_w, v_ffn_dw_b, v_w_down, v_norm_final_g):
    weights = dict(norm_mix_g=norm_mix_g, w_in=w_in, conv_dw_w=conv_dw_w, conv_dw_b=conv_dw_b, conv_ln_g=conv_ln_g,
                   conv_ln_b=conv_ln_b, pool_w=pool_w, pool_scale=pool_scale, w_out=w_out, norm_xattn_g=norm_xattn_g,
                   norm_mem_g=norm_mem_g, w_q=w_q, w_kv=w_kv, w_o=w_o, norm_ffn_g=norm_ffn_g, w_up=w_up,
                   ffn_dw_w=ffn_dw_w, ffn_dw_b=ffn_dw_b, w_down=w_down, norm_final_g=norm_final_g)
    moments_m = dict(norm_mix_g=m_norm_mix_g, w_in=m_w_in, conv_dw_w=m_conv_dw_w, conv_dw_b=m_conv_dw_b,
                     conv_ln_g=m_conv_ln_g, conv_ln_b=m_conv_ln_b, pool_w=m_pool_w, pool_scale=m_pool_scale,
                     w_out=m_w_out, norm_xattn_g=m_norm_xattn_g, norm_mem_g=m_norm_mem_g, w_q=m_w_q, w_kv=m_w_kv,
                     w_o=m_w_o, norm_ffn_g=m_norm_ffn_g, w_up=m_w_up, ffn_dw_w=m_ffn_dw_w, ffn_dw_b=m_ffn_dw_b,
                     w_down=m_w_down, norm_final_g=m_norm_final_g)
    moments_v = dict(norm_mix_g=v_norm_mix_g, w_in=v_w_in, conv_dw_w=v_conv_dw_w, conv_dw_b=v_conv_dw_b,
                     conv_ln_g=v_conv_ln_g, conv_ln_b=v_conv_ln_b, pool_w=v_pool_w, pool_scale=v_pool_scale,
                     w_out=v_w_out, norm_xattn_g=v_norm_xattn_g, norm_mem_g=v_norm_mem_g, w_q=v_w_q, w_kv=v_w_kv,
                     w_o=v_w_o, norm_ffn_g=v_norm_ffn_g, w_up=v_w_up, ffn_dw_w=v_ffn_dw_w, ffn_dw_b=v_ffn_dw_b,
                     w_down=v_w_down, norm_final_g=v_norm_final_g)
    order = list(weights)
    transposed = ("w_in", "w_kv", "w_up")

    n_b, seq, _ = x.shape
    tokens = n_b * seq
    tm_mix = min(512, seq // 2)
    tm_ffn = min(256, seq // 2)
    dev = 4 * lax.axis_index("x") + 2 * lax.axis_index("y") + lax.axis_index("c")

    packs = [jnp.concatenate([weights[n][0].T if n in transposed else weights[n][0] for n in names], axis=0).astype(BF16)
             for names in AG_GROUPS]
    small_sharded = _pack_rows([conv_dw_w[0], ffn_dw_w[0]])
    gw_mix, gsmall = _all_gather([packs[0], small_sharded], "weights_all_gather")
    flights = []
    after = gw_mix
    for k in (1, 2):
        own_in_place = lax.dynamic_update_slice(lax.empty((N_DEV,) + packs[k].shape, BF16), packs[k][None], (dev, 0, 0))
        flights.append(_gather_start(own_in_place, after, "weights_gather_start_%d" % k))
        after = flights[-1][3]
    gflat = gsmall.reshape(N_DEV, -1)
    n_cw = CONV_WIDTH * (D_CONV // N_DEV)
    n_fw = FFN_CONV_WIDTH * (2 * D_FF // N_DEV)
    conv_w = gflat[:, :n_cw].reshape(N_DEV, CONV_WIDTH, D_CONV // N_DEV).transpose(1, 0, 2).reshape(CONV_WIDTH, D_CONV)
    ffn_w = gflat[:, n_cw:n_cw + n_fw].reshape(N_DEV, FFN_CONV_WIDTH, 2 * D_FF // N_DEV).transpose(1, 0, 2).reshape(
        FFN_CONV_WIDTH, 2 * D_FF)

    x2d = x.reshape(tokens, D_MODEL)
    mem2d = mem.reshape(n_b * N_MEM, D_MODEL)
    tgt2d = loss_target.reshape(tokens, D_MODEL)
    g_final = norm_final_g.reshape(1, D_MODEL)

    def gather_finish(flight, after, tag):
        fwd_send, fwd_recv, buf = _gather_forward(*flight[:3], after, "weights_gather_forward_" + tag)
        return _gather_finish(fwd_send, fwd_recv, buf, "weights_gather_finish_" + tag)

    x1, u_all, c_all, pooled_all, ymix, h1 = _fwd_mix(
        x2d, gw_mix, norm_mix_g, conv_w, conv_dw_b, conv_ln_g, conv_ln_b, pool_w[0], pool_scale, flights[1][3],
        seq, tm_mix)
    gw_attn = gather_finish(flights[0], x1, "1")
    mem_n, kv = _fwd_kv(mem2d, gw_attn, norm_mem_g)
    x2, h2, q, o = _fwd_attn(x1, kv, gw_attn, norm_xattn_g, seq, tm_mix)
    gw_ffn = gather_finish(flights[1], x2, "2")
    uu_all, cc_all, a_all, h3, dx3, dx3b, loss_part, dg_final = _fwd_ffn(
        x2, tgt2d, gw_ffn, norm_ffn_g, ffn_w, ffn_dw_b, g_final, seq, tm_ffn)

    table = _owner_table()

    def sibling_start(names, tag):
        parts = [part[n].reshape(N_DEV, W_OFF[n][1], D_MODEL) for n in names]
        return _exchange_start(parts, 4, _to_sibling, "rs_sibling_exchange_start_" + tag)

    def chips_start(flight, after, tag):
        parts, landed = _exchange_wait(*flight[:4], after, 4, _to_sibling, "rs_sibling_exchange_wait_" + tag)
        sums = _chip_partial_sums(table, parts, landed, "rs_chip_partial_sums_" + tag)
        return parts, landed, _exchange_start(sums, 3, _to_chip, "rs_chip_exchange_start_" + tag)

    def reduce_finish(names, parts, landed, flight, after, tag):
        _, from_chips = _exchange_wait(*flight[:4], after, 3, _to_chip, "rs_chip_exchange_wait_" + tag)
        sums = _final_grad_sums(table, parts, landed, from_chips, "rs_final_sums_" + tag)
        g_mine.update(zip(names, sums))
        return sums[-1]

    part, g_mine = {}, {}
    dx2, dx2b, duu, d_ffn_b, d_ffn_w, dg_ffn = _bwd_ffn(dx3, x2, uu_all, cc_all, gw_ffn, norm_ffn_g, ffn_w, seq, tm_ffn)
    part["w_up"] = _wgrad(duu, h3, "wgrad_w_up")
    part["w_down"] = _wgrad(a_all, dx3b, "wgrad_w_down")
    to_sibling_a = sibling_start(RS_GROUPS["a"], "a")
    dx1, dx1b, dq, dkv, dg_x = _bwd_attn(dx2, x1, q, kv, gw_attn, norm_xattn_g, to_sibling_a[4], seq, tm_mix)
    parts_a, landed_a, flight_a = chips_start(to_sibling_a, dx1, "a")
    dkv_b, dg_mem = _bwd_kv(dkv, mem2d, gw_attn, norm_mem_g)
    part["w_q"] = _wgrad(h2, dq, "wgrad_w_q", after=flight_a[4])
    part["w_kv"] = _wgrad(dkv_b, mem_n, "wgrad_w_kv")
    part["w_o"] = _wgrad(o, dx2b, "wgrad_w_o")
    to_sibling_b = sibling_start(RS_GROUPS["b"], "b")
    dx, du, dg_mix, d_conv_w, d_conv_b, d_ln_g, d_ln_b, d_pool_w, d_pool_scale = _bwd_mix(
        dx1, x2d, u_all, c_all, pooled_all, gw_mix, norm_mix_g, conv_w, conv_ln_g, conv_ln_b, pool_w[0], pool_scale,
        to_sibling_b[4], seq, tm_mix)
    parts_b, landed_b, flight_b = chips_start(to_sibling_b, dx, "b")
    grad_x = dx.reshape(x.shape)

    small_grads = dict(norm_mix_g=dg_mix, conv_dw_b=d_conv_b, conv_ln_g=d_ln_g, conv_ln_b=d_ln_b, pool_w=d_pool_w,
                       pool_scale=d_pool_scale, norm_xattn_g=dg_x, norm_mem_g=dg_mem, norm_ffn_g=dg_ffn,
                       ffn_dw_b=d_ffn_b, norm_final_g=dg_final)
    small_list = [small_grads[n] for n, _ in SMALL] + [d_conv_w, d_ffn_w, loss_part[:1]]
    small_mine = _pack_rows(small_list)
    small_flight = _broadcast_start(
        lax.dynamic_update_slice(lax.empty((N_DEV,) + small_mine.shape, F32), small_mine[None], (dev, 0, 0)),
        "small_grads_broadcast_start")

    part["w_in"] = _wgrad(du, h1, "wgrad_w_in", after=small_flight[3])
    part["w_out"] = _wgrad(ymix, dx1b, "wgrad_w_out", after=flight_b[4])
    to_sibling_c = sibling_start(RS_GROUPS["c"], "c")
    parts_c, landed_c, flight_c = chips_start(to_sibling_c, to_sibling_c[4], "c")
    grads, delta, new_m, new_v = {}, {}, {}, {}

    def adamw_group(names):
        for n in names:
            shape = weights[n].shape
            as2d = lambda t: t.reshape(shape[1], shape[2])
            grads[n] = (g_mine[n].T if n in transposed else g_mine[n])[None]
            d, nm, nv = _adamw(as2d(weights[n]), as2d(grads[n]), as2d(moments_m[n]), as2d(moments_v[n]), "adamw_" + n)
            delta[n], new_m[n], new_v[n] = d.reshape(shape), nm.reshape(shape), nv.reshape(shape)
        return delta[names[-1]]

    reduce_finish(RS_GROUPS["a"], parts_a, landed_a, flight_a, flight_c[4], "a")
    updated_a = adamw_group(RS_GROUPS["a"])
    reduce_finish(RS_GROUPS["b"], parts_b, landed_b, flight_b, updated_a, "b")
    updated_b = adamw_group(RS_GROUPS["b"])
    reduce_finish(RS_GROUPS["c"], parts_c, landed_c, flight_c, updated_b, "c")
    updated_c = adamw_group(RS_GROUPS["c"])

    small_all = _broadcast_wait(*small_flight[:3], updated_c, "small_grads_broadcast_wait")
    small_sum = _sum_blocks(small_all).reshape(-1)

    pos = 0
    for n, shape in SMALL:
        size = 1
        for s in shape:
            size *= s
        grads[n] = small_sum[pos:pos + size].reshape(shape)
        pos += size
    full_conv_w = small_sum[pos:pos + CONV_WIDTH * D_CONV].reshape(CONV_WIDTH, D_CONV)
    pos += CONV_WIDTH * D_CONV
    full_ffn_w = small_sum[pos:pos + FFN_CONV_WIDTH * 2 * D_FF].reshape(FFN_CONV_WIDTH, 2 * D_FF)
    loss = small_sum[pos + FFN_CONV_WIDTH * 2 * D_FF]
    grads["conv_dw_w"] = lax.dynamic_slice_in_dim(full_conv_w, dev * (D_CONV // N_DEV), D_CONV // N_DEV, axis=1)[None]
    grads["ffn_dw_w"] = lax.dynamic_slice_in_dim(full_ffn_w, dev * (2 * D_FF // N_DEV), 2 * D_FF // N_DEV, axis=1)[None]

    small_names = [n for n in order if n not in W_OFF]
    two_d = lambda t: t.reshape(1, -1) if t.ndim == 1 else t
    outs = _adamw_small(*[[two_d(t[n]) for n in small_names] for t in (weights, grads, moments_m, moments_v)])
    for res, out in zip((delta, new_m, new_v), outs):
        for n, o in zip(small_names, out):
            res[n] = o.reshape(weights[n].shape)

    return (loss, grad_x, *[grads[n] for n in order], *[delta[n] for n in order],
            *[new_m[n] for n in order], *[new_v[n] for n in order])
```

```python
import functools

import jax
import jax.numpy as jnp
from jax import lax
from jax.experimental import pallas as pl
from jax.experimental.pallas import tpu as pltpu

F32 = jnp.float32
BF16 = jnp.bfloat16
MESH = pl.DeviceIdType.MESH

N_DEV = 8
D_MODEL = 1024
D_CONV = 512
D_POOL = 512
CONV_WIDTH = 31
POOL_WINDOWS = (2, 4, 8, 16)
POOL_GROUP_DIM = 128
D_IN = 1536
N_MEM = 256
HEADS = 4
HEAD_DIM = 256
D_FF = 2816
FFN_CONV_WIDTH = 3
EPS = 1e-6
ADAM_LR = 0.001
ADAM_B1 = 0.9
ADAM_B2 = 0.999
ADAM_EPS = 1e-08
ADAM_WD = 0.01
ADAM_STEP = 10

VMEM_LIMIT_V7X = 56 * 1024 * 1024
CONV_HALO = 32
POOL_HALO = 16
FFN_HALO = 8
FFN_CHUNK = 2816

W_ROWS = (("w_in", 192), ("w_out", 128), ("w_q", 128), ("w_kv", 256), ("w_o", 128), ("w_up", 704), ("w_down", 352))
AG_GROUPS = (("w_in", "w_out"), ("w_q", "w_kv", "w_o"), ("w_up", "w_down"))
W_OFF = {}
for _names in AG_GROUPS:
    _o = 0
    for _n in _names:
        W_OFF[_n] = (_o, dict(W_ROWS)[_n])
        _o += dict(W_ROWS)[_n]
RS_GROUPS = {"a": ("w_up", "w_down"), "b": ("w_q", "w_kv", "w_o"), "c": ("w_in", "w_out")}


def _dot(a, b):
    return jnp.dot(a, b, preferred_element_type=F32)


def _dot_nt(a, b):
    return lax.dot_general(a, b, (((1,), (1,)), ((), ())), preferred_element_type=F32)


def _dot_tn(a, b):
    return lax.dot_general(a, b, (((0,), (0,)), ((), ())), preferred_element_type=F32)


def _sigmoid(v):
    return 1.0 / (1.0 + jnp.exp(-v))


def _rms_fwd(v):
    r = lax.rsqrt(jnp.mean(v * v, axis=-1, keepdims=True) + EPS)
    return v * r, r


def _rms_bwd(dh, vh, r, g):
    gd = dh * g
    return r * (gd - vh * jnp.mean(gd * vh, axis=-1, keepdims=True))


def _sublane_shifts(v):
    rows = v.shape[0]
    return [v] + [pltpu.roll(v, rows - b, 0) for b in range(1, 8)]


def _colsum(v):
    return jnp.sum(v, axis=0, keepdims=True)


def _full(shape):
    return pl.BlockSpec(shape, lambda *_: (0,) * len(shape))


def _params(sem=("arbitrary",), vmem=VMEM_LIMIT_V7X):
    return pltpu.CompilerParams(dimension_semantics=sem, vmem_limit_bytes=vmem)


def _load_weight(g_hbm, name, dst, sem):
    off, rows = W_OFF[name]
    return [pltpu.make_async_copy(g_hbm.at[d, pl.ds(off, rows), :], dst.at[pl.ds(d * rows, rows), :], sem)
            for d in range(N_DEV)]


def _position():
    x, y, c = lax.axis_index("x"), lax.axis_index("y"), lax.axis_index("c")
    chips = [(1 - x, y), (x, 1 - y), (1 - x, 1 - y)]
    return x, y, c, chips


def _dev(px, py, pc):
    return 4 * px + 2 * py + pc


def _all_gather(arrs, name):
    n = len(arrs)

    def body(*refs):
        ins, outs = refs[:n], refs[n:2 * n]
        send_sems, recv_sems, local_sems = refs[2 * n:2 * n + 3]
        bounce = refs[2 * n + 3:]
        x, y, c, chips = _position()
        me, sibling = (x, y, c), (x, y, 1 - c)

        def copy(a, k, block, to, src=None):
            rows = outs[a].at[_dev(*block)]
            return pltpu.make_async_remote_copy(
                src_ref=rows if src is None else src, dst_ref=rows,
                send_sem=send_sems.at[a, k], recv_sem=recv_sems.at[a, k], device_id=to, device_id_type=MESH)

        sends = []
        for a in range(n):
            first = [copy(a, 0, me, sibling, src=ins[a])]
            first += [copy(a, 1 + j, me, (*chip, c), src=ins[a]) for j, chip in enumerate(chips)]
            for cp in first:
                cp.start()
            sends += first
        started = []
        for a in range(n):
            load = pltpu.make_async_copy(ins[a], bounce[a], local_sems.at[a, 0])
            load.start()
            load.wait()
            mine = pltpu.make_async_copy(bounce[a], outs[a].at[_dev(*me)], local_sems.at[a, 1])
            mine.start()
            started.append(mine)
        for j, chip in enumerate(chips):
            for a in range(n):
                copy(a, 1 + j, (*chip, c), me).wait_recv()
                passed = copy(a, 4 + j, (*chip, c), sibling)
                passed.start()
                sends.append(passed)
        for a in range(n):
            copy(a, 0, sibling, me).wait_recv()
            for j, chip in enumerate(chips):
                copy(a, 4 + j, (*chip, 1 - c), me).wait_recv()
        for cp in sends:
            cp.wait_send()
        for mine in started:
            mine.wait()

    any_spec = pl.BlockSpec(memory_space=pl.ANY)
    return pl.pallas_call(
        body, name=name,
        out_shape=[jax.ShapeDtypeStruct((N_DEV,) + a.shape, a.dtype) for a in arrs],
        in_specs=[any_spec] * n, out_specs=[any_spec] * n,
        scratch_shapes=[pltpu.SemaphoreType.DMA((n, 7)), pltpu.SemaphoreType.DMA((n, 7)), pltpu.SemaphoreType.DMA((n, 2))]
        + [pltpu.VMEM(a.shape, a.dtype) for a in arrs],
    )(*arrs)


_HBM = pl.BlockSpec(memory_space=pltpu.HBM)
_SEM = pl.BlockSpec(memory_space=pltpu.SEMAPHORE)
_SIDE_EFFECT = pltpu.SideEffectType.DATAFLOW_SIDE_EFFECTING


def _gather_start(buf, after, name):
    def body(buf_ref, after_ref, send_sems, recv_sems, buf_thru, token):
        del after_ref, buf_thru
        x, y, c, chips = _position()
        rows = buf_ref.at[_dev(x, y, c)]
        for k, to in enumerate([(x, y, 1 - c)] + [(*chip, c) for chip in chips]):
            pltpu.make_async_remote_copy(src_ref=rows, dst_ref=rows, send_sem=send_sems.at[k], recv_sem=recv_sems.at[k],
                                         device_id=to, device_id_type=MESH).start()
        token[...] = jnp.zeros_like(token)

    return pl.pallas_call(
        body, name=name,
        out_shape=(pltpu.SemaphoreType.DMA((4,)), pltpu.SemaphoreType.DMA((4,)), pltpu.HBM(buf.shape, buf.dtype),
                   jax.ShapeDtypeStruct((8, 128), F32)),
        in_specs=(_HBM, pl.BlockSpec(memory_space=pl.ANY)),
        out_specs=(_SEM, _SEM, _HBM, pl.BlockSpec(memory_space=pltpu.VMEM)),
        input_output_aliases={0: 2},
        compiler_params=pltpu.CompilerParams(has_side_effects=_SIDE_EFFECT),
    )(pltpu.with_memory_space_constraint(buf, pltpu.HBM), after)


def _gather_forward(send_sems, recv_sems, buf, after, name):
    def body(buf_ref, send_sems, recv_sems, after_ref, fwd_send, fwd_recv, buf_thru):
        del after_ref, buf_thru
        x, y, c, chips = _position()
        sibling = (x, y, 1 - c)

        def copy(block, k, sends, recvs):
            rows = buf_ref.at[_dev(*block)]
            return pltpu.make_async_remote_copy(src_ref=rows, dst_ref=rows, send_sem=sends.at[k], recv_sem=recvs.at[k],
                                                device_id=sibling, device_id_type=MESH)

        for k in range(4):
            copy((x, y, c), k, send_sems, recv_sems).wait_send()
        copy(sibling, 0, send_sems, recv_sems).wait_recv()
        for j, chip in enumerate(chips):
            copy((*chip, c), 1 + j, send_sems, recv_sems).wait_recv()
            copy((*chip, c), j, fwd_send, fwd_recv).start()

    return pl.pallas_call(
        body, name=name,
        out_shape=(pltpu.SemaphoreType.DMA((3,)), pltpu.SemaphoreType.DMA((3,)), pltpu.HBM(buf.shape, buf.dtype)),
        in_specs=(_HBM, _SEM, _SEM, pl.BlockSpec(memory_space=pl.ANY)), out_specs=(_SEM, _SEM, _HBM),
        input_output_aliases={0: 2},
        compiler_params=pltpu.CompilerParams(has_side_effects=_SIDE_EFFECT),
    )(buf, send_sems, recv_sems, after)


def _gather_finish(fwd_send, fwd_recv, buf, name):
    def body(buf_ref, fwd_send, fwd_recv, buf_thru):
        del buf_thru
        x, y, c, chips = _position()
        for j, chip in enumerate(chips):
            cp = pltpu.make_async_remote_copy(
                src_ref=buf_ref.at[_dev(*chip, c)], dst_ref=buf_ref.at[_dev(*chip, 1 - c)], send_sem=fwd_send.at[j],
                recv_sem=fwd_recv.at[j], device_id=(x, y, 1 - c), device_id_type=MESH)
            cp.wait_send()
            cp.wait_recv()

    return pl.pallas_call(
        body, name=name,
        out_shape=pltpu.HBM(buf.shape, buf.dtype),
        in_specs=(_HBM, _SEM, _SEM), out_specs=_HBM,
        input_output_aliases={0: 0},
        compiler_params=pltpu.CompilerParams(has_side_effects=_SIDE_EFFECT),
    )(buf, fwd_send, fwd_recv)


def _everyone_else(x, y, c, chips):
    return [(x, y, 1 - c)] + [(*chip, core) for chip in chips for core in (c, 1 - c)]


def _broadcast_start(buf, name):
    def body(buf_ref, send_sems, recv_sems, buf_thru, token):
        del buf_thru
        x, y, c, chips = _position()
        rows = buf_ref.at[_dev(x, y, c)]
        for k, to in enumerate(_everyone_else(x, y, c, chips)):
            pltpu.make_async_remote_copy(src_ref=rows, dst_ref=rows, send_sem=send_sems.at[k], recv_sem=recv_sems.at[k],
                                         device_id=to, device_id_type=MESH).start()
        token[...] = jnp.zeros_like(token)

    return pl.pallas_call(
        body, name=name,
        out_shape=(pltpu.SemaphoreType.DMA((7,)), pltpu.SemaphoreType.DMA((7,)), pltpu.HBM(buf.shape, buf.dtype),
                   jax.ShapeDtypeStruct((8, 128), F32)),
        in_specs=(_HBM,), out_specs=(_SEM, _SEM, _HBM, pl.BlockSpec(memory_space=pltpu.VMEM)),
        input_output_aliases={0: 2},
        compiler_params=pltpu.CompilerParams(has_side_effects=_SIDE_EFFECT),
    )(pltpu.with_memory_space_constraint(buf, pltpu.HBM))


def _broadcast_wait(send_sems, recv_sems, buf, after, name):
    def body(buf_ref, send_sems, recv_sems, after_ref, buf_thru):
        del after_ref, buf_thru
        x, y, c, chips = _position()
        for k, peer in enumerate(_everyone_else(x, y, c, chips)):
            cp = pltpu.make_async_remote_copy(
                src_ref=buf_ref.at[_dev(x, y, c)], dst_ref=buf_ref.at[_dev(*peer)], send_sem=send_sems.at[k],
                recv_sem=recv_sems.at[k], device_id=peer, device_id_type=MESH)
            cp.wait_send()
            cp.wait_recv()

    return pl.pallas_call(
        body, name=name,
        out_shape=pltpu.HBM(buf.shape, buf.dtype),
        in_specs=(_HBM, _SEM, _SEM, pl.BlockSpec(memory_space=pl.ANY)), out_specs=_HBM,
        input_output_aliases={0: 0},
        compiler_params=pltpu.CompilerParams(has_side_effects=_SIDE_EFFECT),
    )(buf, send_sems, recv_sems, after)


def _to_sibling(j, x, y, c, chips):
    return _dev(*([(x, y)] + chips)[j], 1 - c), (x, y, 1 - c)


def _to_chip(j, x, y, c, chips):
    return j, (*chips[j], c)


def _exchange_start(srcs, n_slots, route, name):
    n = len(srcs)

    def body(*refs):
        s_refs, land_refs = refs[:n], refs[n:2 * n]
        send_sems, recv_sems = refs[2 * n:2 * n + 2]
        token = refs[-1]
        x, y, c, chips = _position()
        for k in range(n):
            for j in range(n_slots):
                block, to = route(j, x, y, c, chips)
                pltpu.make_async_remote_copy(
                    src_ref=s_refs[k].at[block], dst_ref=land_refs[k].at[j], send_sem=send_sems.at[n_slots * k + j],
                    recv_sem=recv_sems.at[n_slots * k + j], device_id=to, device_id_type=MESH).start()
        token[...] = jnp.zeros_like(token)

    lands = [jax.ShapeDtypeStruct((n_slots,) + s.shape[1:], s.dtype) for s in srcs]
    outs = pl.pallas_call(
        body, name=name,
        out_shape=(pltpu.SemaphoreType.DMA((n_slots * n,)), pltpu.SemaphoreType.DMA((n_slots * n,)),
                   *[pltpu.HBM(s.shape, s.dtype) for s in srcs], *[pltpu.HBM(l.shape, l.dtype) for l in lands],
                   jax.ShapeDtypeStruct((8, 128), F32)),
        in_specs=[_HBM] * (2 * n), out_specs=(_SEM, _SEM, *[_HBM] * (2 * n), pl.BlockSpec(memory_space=pltpu.VMEM)),
        input_output_aliases={k: 2 + k for k in range(2 * n)},
        compiler_params=pltpu.CompilerParams(has_side_effects=_SIDE_EFFECT),
    )(*[pltpu.with_memory_space_constraint(s, pltpu.HBM) for s in srcs],
      *[pltpu.with_memory_space_constraint(lax.empty(l.shape, l.dtype), pltpu.HBM) for l in lands])
    return outs[0], outs[1], outs[2:2 + n], outs[2 + n:2 + 2 * n], outs[-1]


def _exchange_wait(send_sems, recv_sems, s_thru, land_thru, after, n_slots, route, name):
    n = len(s_thru)

    def body(*refs):
        s_refs, land_refs = refs[:n], refs[n:2 * n]
        send_sems, recv_sems = refs[2 * n:2 * n + 2]
        x, y, c, chips = _position()
        for k in range(n):
            for j in range(n_slots):
                block, to = route(j, x, y, c, chips)
                cp = pltpu.make_async_remote_copy(
                    src_ref=s_refs[k].at[block], dst_ref=land_refs[k].at[j], send_sem=send_sems.at[n_slots * k + j],
                    recv_sem=recv_sems.at[n_slots * k + j], device_id=to, device_id_type=MESH)
                cp.wait_send()
                cp.wait_recv()

    outs = pl.pallas_call(
        body, name=name,
        out_shape=(*[pltpu.HBM(s.shape, s.dtype) for s in s_thru], *[pltpu.HBM(l.shape, l.dtype) for l in land_thru]),
        in_specs=[_HBM] * (2 * n) + [_SEM, _SEM, pl.BlockSpec(memory_space=pl.ANY)], out_specs=[_HBM] * (2 * n),
        input_output_aliases={k: k for k in range(2 * n)},
        compiler_params=pltpu.CompilerParams(has_side_effects=_SIDE_EFFECT),
    )(*s_thru, *land_thru, send_sems, recv_sems, after)
    return outs[:n], outs[n:]


def _owner_table():
    x, y, c = lax.axis_index("x"), lax.axis_index("y"), lax.axis_index("c")
    chips = [(x, y), (1 - x, y), (x, 1 - y), (1 - x, 1 - y)]
    return jnp.stack([_dev(px, py, c) for px, py in chips]).astype(jnp.int32)


def _chip_partial_sums(table, parts, from_sibling, name):
    n = len(parts)

    def body(tab_ref, *refs):
        del tab_ref
        for g_ref, l_ref, out_ref in zip(refs[:n], refs[n:2 * n], refs[2 * n:]):
            out_ref[...] = (g_ref[...].astype(F32) + l_ref[...].astype(F32)).astype(out_ref.dtype)

    block = lambda p: (None,) + p.shape[1:]
    grid_spec = pltpu.PrefetchScalarGridSpec(
        num_scalar_prefetch=1, grid=(3,),
        in_specs=[pl.BlockSpec(block(p), lambda j, tab: (tab[j + 1], 0, 0)) for p in parts]
        + [pl.BlockSpec(block(p), lambda j, tab: (j + 1, 0, 0)) for p in parts],
        out_specs=[pl.BlockSpec(block(p), lambda j, tab: (j, 0, 0)) for p in parts])
    return pl.pallas_call(
        body, name=name, grid_spec=grid_spec,
        out_shape=[jax.ShapeDtypeStruct((3,) + p.shape[1:], BF16) for p in parts],
        compiler_params=_params(("arbitrary",)),
    )(table, *parts, *from_sibling)


def _final_update(table, parts, from_sibling, from_chips, states, name):
    n = len(parts)
    updated = [k for k in range(n) if states[k] is not None]

    def body(tab_ref, *refs):
        del tab_ref
        ins, outs = refs[:3 * n + 3 * len(updated)], list(refs[3 * n + 3 * len(updated):])
        wmv = list(ins[3 * n:])
        for k in range(n):
            acc = ins[k][...].astype(F32) + ins[n + k][...].astype(F32)
            for j in range(3):
                acc = acc + ins[2 * n + k][j].astype(F32)
            outs.pop(0)[...] = acc
            if k in updated:
                w_ref, m_ref, v_ref = wmv[:3]
                del wmv[:3]
                for out_ref, val in zip(outs[:3], _adamw_update(w_ref[...], acc, m_ref[...], v_ref[...])):
                    out_ref[...] = val
                del outs[:3]

    half = lambda p: (p.shape[1] // 2, p.shape[2])
    rows = lambda p: pl.BlockSpec(half(p), lambda t, tab: (t, 0))
    grid_spec = pltpu.PrefetchScalarGridSpec(
        num_scalar_prefetch=1, grid=(2,),
        in_specs=[pl.BlockSpec((None,) + half(p), lambda t, tab: (tab[0], t, 0)) for p in parts]
        + [pl.BlockSpec((None,) + half(p), lambda t, tab: (0, t, 0)) for p in parts]
        + [pl.BlockSpec((3,) + half(p), lambda t, tab: (0, t, 0)) for p in parts]
        + [rows(parts[k]) for k in updated for _ in range(3)],
        out_specs=[rows(parts[k]) for k in range(n) for _ in range(4 if k in updated else 1)])
    outs = pl.pallas_call(
        body, name=name, grid_spec=grid_spec,
        out_shape=[jax.ShapeDtypeStruct(parts[k].shape[1:], F32) for k in range(n) for _ in range(4 if k in updated else 1)],
        compiler_params=_params(("arbitrary",)),
    )(table, *parts, *from_sibling, *from_chips, *[t for k in updated for t in states[k]])
    result = []
    for k in range(n):
        count = 4 if k in updated else 1
        result.append(outs[:count])
        outs = outs[count:]
    return result


def _sum_blocks(g8):
    _, rows, cols = g8.shape

    def body(g_ref, out_ref):
        acc = g_ref[0]
        for d in range(1, N_DEV):
            acc = acc + g_ref[d]
        out_ref[...] = acc

    return pl.pallas_call(
        body, name="small_grad_sum", grid=(1,),
        in_specs=[_full((N_DEV, rows, cols))], out_specs=_full((rows, cols)),
        out_shape=jax.ShapeDtypeStruct((rows, cols), F32),
        compiler_params=_params(("arbitrary",)),
    )(g8)


def _fwd_mix(x2d, gw, g_mix, conv_w, conv_b, ln_g, ln_b, pool_w, pool_scale, after, seq, tm):
    tokens = x2d.shape[0]
    n_tiles = tokens // tm
    tps = seq // tm

    def body(x_ref, gmix_ref, gw_hbm, cw_ref, cb_ref, lng_ref, lnb_ref, pw_ref, ps_ref, after_ref,
             x1_ref, u_ref, c_ref, pooled_ref, ymix_ref, h1_ref,
             win_v, wout_v, hc_carry, up_carry, sem):
        del after_ref
        i = pl.program_id(0)

        @pl.when(i == 0)
        def _():
            copies = _load_weight(gw_hbm, "w_in", win_v, sem) + _load_weight(gw_hbm, "w_out", wout_v, sem)
            for cp in copies:
                cp.start()
            for cp in copies:
                cp.wait()

        @pl.when(i % tps == 0)
        def _():
            hc_carry[...] = jnp.zeros_like(hc_carry)
            up_carry[...] = jnp.zeros_like(up_carry)

        x = x_ref[...]
        xh, _ = _rms_fwd(x)
        h1 = (xh * gmix_ref[...]).astype(BF16)
        h1_ref[...] = h1
        u = _dot_nt(h1, win_v[...])
        u_ref[...] = u
        val, gate, up = u[:, :D_CONV], u[:, D_CONV:2 * D_CONV], u[:, 2 * D_CONV:]

        hc = val * _sigmoid(gate)
        ext = jnp.concatenate([hc_carry[...], hc], axis=0)
        hc_carry[...] = hc[tm - CONV_HALO:, :]
        conv = jnp.broadcast_to(cb_ref[...], (tm, D_CONV))
        ahead_by = _sublane_shifts(ext)
        for k in range(CONV_WIDTH):
            whole, part = divmod(CONV_HALO - (CONV_WIDTH - 1) + k, 8)
            conv = conv + cw_ref[k:k + 1, :] * ahead_by[part][8 * whole:8 * whole + tm, :]
        c_ref[...] = conv
        mu = jnp.mean(conv, axis=-1, keepdims=True)
        cen = conv - mu
        ln = cen * lax.rsqrt(jnp.mean(cen * cen, axis=-1, keepdims=True) + EPS) * lng_ref[...] + lnb_ref[...]
        y_conv = ln * _sigmoid(ln)

        extp = jnp.concatenate([up_carry[...], up], axis=0)
        up_carry[...] = up[tm - POOL_HALO:, :]
        pos = lax.broadcasted_iota(jnp.int32, (tm, 1), 0) + (i % tps) * tm
        run = extp
        mixed = []
        for g, w in enumerate(POOL_WINDOWS):
            lo = g * POOL_GROUP_DIM
            run = run[:, POOL_GROUP_DIM if g else 0:]
            run = run + pltpu.roll(run, w // 2, 0)
            cnt = jnp.minimum(pos + 1, w).astype(F32)
            pooled = run[POOL_HALO:, :POOL_GROUP_DIM] / cnt - up[:, lo:lo + POOL_GROUP_DIM]
            pooled = pooled.astype(BF16)
            pooled_ref[:, lo:lo + POOL_GROUP_DIM] = pooled
            mixed.append(_dot(pooled, pw_ref[g].astype(BF16)))
        y_pool = jnp.concatenate(mixed, axis=-1) * ps_ref[...]

        ymix = jnp.concatenate([y_conv, y_pool], axis=-1).astype(BF16)
        ymix_ref[...] = ymix
        x1_ref[...] = x + _dot(ymix, wout_v[...])

    row = lambda w: pl.BlockSpec((tm, w), lambda i: (i, 0))
    return pl.pallas_call(
        body, name="fwd_mix", grid=(n_tiles,),
        in_specs=[row(D_MODEL), _full((1, D_MODEL)), pl.BlockSpec(memory_space=pl.ANY),
                  _full((CONV_WIDTH, D_CONV)), _full((1, D_CONV)), _full((1, D_CONV)), _full((1, D_CONV)),
                  _full((4, POOL_GROUP_DIM, POOL_GROUP_DIM)), _full((1, D_POOL)), _full(after.shape)],
        out_specs=[row(D_MODEL), row(D_IN), row(D_CONV), row(D_POOL), row(D_MODEL), row(D_MODEL)],
        out_shape=[jax.ShapeDtypeStruct((tokens, D_MODEL), F32), jax.ShapeDtypeStruct((tokens, D_IN), F32),
                   jax.ShapeDtypeStruct((tokens, D_CONV), F32), jax.ShapeDtypeStruct((tokens, D_POOL), BF16),
                   jax.ShapeDtypeStruct((tokens, D_MODEL), BF16), jax.ShapeDtypeStruct((tokens, D_MODEL), BF16)],
        scratch_shapes=[pltpu.VMEM((D_IN, D_MODEL), BF16), pltpu.VMEM((D_MODEL, D_MODEL), BF16),
                        pltpu.VMEM((CONV_HALO, D_CONV), F32), pltpu.VMEM((POOL_HALO, D_POOL), F32),
                        pltpu.SemaphoreType.DMA],
        compiler_params=_params(),
    )(x2d, g_mix, gw, conv_w, conv_b, ln_g, ln_b, pool_w, pool_scale, after)


def _fwd_kv(mem2d, gw, g_mem):
    rows = mem2d.shape[0]
    n_b = rows // N_MEM

    def body(mem_ref, g_ref, gw_hbm, mn_ref, kv_ref, wkv_v, sem):
        @pl.when(pl.program_id(0) == 0)
        def _():
            copies = _load_weight(gw_hbm, "w_kv", wkv_v, sem)
            for cp in copies:
                cp.start()
            for cp in copies:
                cp.wait()

        mh, _ = _rms_fwd(mem_ref[...])
        mn = (mh * g_ref[...]).astype(BF16)
        mn_ref[...] = mn
        kv_ref[...] = _dot_nt(mn, wkv_v[...]).astype(BF16)

    return pl.pallas_call(
        body, name="fwd_kv", grid=(n_b,),
        in_specs=[pl.BlockSpec((N_MEM, D_MODEL), lambda b: (b, 0)), _full((1, D_MODEL)), pl.BlockSpec(memory_space=pl.ANY)],
        out_specs=[pl.BlockSpec((N_MEM, D_MODEL), lambda b: (b, 0)), pl.BlockSpec((N_MEM, 2 * D_MODEL), lambda b: (b, 0))],
        out_shape=[jax.ShapeDtypeStruct((rows, D_MODEL), BF16), jax.ShapeDtypeStruct((rows, 2 * D_MODEL), BF16)],
        scratch_shapes=[pltpu.VMEM((2 * D_MODEL, D_MODEL), BF16), pltpu.SemaphoreType.DMA],
        compiler_params=_params(),
    )(mem2d, g_mem, gw)


def _softmax_rows(s):
    e = jnp.exp(s - jnp.max(s, axis=-1, keepdims=True))
    return e / jnp.sum(e, axis=-1, keepdims=True)


def _fwd_attn(x1, kv, gw, g_x, seq, tm):
    tokens = x1.shape[0]
    n_tiles = tokens // tm
    tps = seq // tm

    def body(x1_ref, kv_ref, g_ref, gw_hbm, x2_ref, h2_ref, q_ref, o_ref, wq_v, wo_v, sem):
        @pl.when(pl.program_id(0) == 0)
        def _():
            copies = _load_weight(gw_hbm, "w_q", wq_v, sem) + _load_weight(gw_hbm, "w_o", wo_v, sem)
            for cp in copies:
                cp.start()
            for cp in copies:
                cp.wait()

        x1v = x1_ref[...]
        xh, _ = _rms_fwd(x1v)
        h2 = (xh * g_ref[...]).astype(BF16)
        h2_ref[...] = h2
        q = (_dot(h2, wq_v[...]) * (HEAD_DIM ** -0.5)).astype(BF16)
        q_ref[...] = q
        outs = []
        for h in range(HEADS):
            lo = h * HEAD_DIM
            p = _softmax_rows(_dot_nt(q[:, lo:lo + HEAD_DIM], kv_ref[:, lo:lo + HEAD_DIM]))
            outs.append(_dot(p.astype(BF16), kv_ref[:, D_MODEL + lo:D_MODEL + lo + HEAD_DIM]))
        o = jnp.concatenate(outs, axis=-1).astype(BF16)
        o_ref[...] = o
        x2_ref[...] = x1v + _dot(o, wo_v[...])

    row = lambda w: pl.BlockSpec((tm, w), lambda i: (i, 0))
    return pl.pallas_call(
        body, name="fwd_attn", grid=(n_tiles,),
        in_specs=[row(D_MODEL), pl.BlockSpec((N_MEM, 2 * D_MODEL), lambda i: (i // tps, 0)), _full((1, D_MODEL)),
                  pl.BlockSpec(memory_space=pl.ANY)],
        out_specs=[row(D_MODEL)] * 4,
        out_shape=[jax.ShapeDtypeStruct((tokens, D_MODEL), F32)] + [jax.ShapeDtypeStruct((tokens, D_MODEL), BF16)] * 3,
        scratch_shapes=[pltpu.VMEM((D_MODEL, D_MODEL), BF16), pltpu.VMEM((D_MODEL, D_MODEL), BF16), pltpu.SemaphoreType.DMA],
        compiler_params=_params(),
    )(x1, kv, g_x, gw)


def _ffn_conv(uu, halo, w_ref, b_ref, cols):
    ext = jnp.concatenate([halo, uu], axis=0)
    p1 = pltpu.roll(ext, 1, 0)[FFN_HALO:, :]
    p2 = pltpu.roll(ext, 2, 0)[FFN_HALO:, :]
    return b_ref[:, cols] + w_ref[2:3, cols] * uu + w_ref[1:2, cols] * p1 + w_ref[0:1, cols] * p2


def _fwd_ffn(x2, target, gw, g_ffn, ffn_w, ffn_b, g_final, seq, tm):
    tokens = x2.shape[0]
    n_tiles = tokens // tm
    tps = seq // tm
    n_chunks = D_FF // FFN_CHUNK

    def body(x2_ref, tgt_ref, gffn_ref, gw_hbm, fw_ref, fb_ref, gfin_ref,
             uu_ref, cc_ref, a_ref, h3_ref, dx3_ref, dx3b_ref, loss_ref, dgfin_ref,
             wup_v, wdown_v, carry, sem):
        i = pl.program_id(0)

        @pl.when(i == 0)
        def _():
            copies = _load_weight(gw_hbm, "w_up", wup_v, sem) + _load_weight(gw_hbm, "w_down", wdown_v, sem)
            for cp in copies:
                cp.start()
            for cp in copies:
                cp.wait()
            loss_ref[...] = jnp.zeros_like(loss_ref)
            dgfin_ref[...] = jnp.zeros_like(dgfin_ref)

        @pl.when(i % tps == 0)
        def _():
            carry[...] = jnp.zeros_like(carry)

        x2v = x2_ref[...]
        xh, _ = _rms_fwd(x2v)
        h3 = (xh * gffn_ref[...]).astype(BF16)
        h3_ref[...] = h3
        acc = jnp.zeros((tm, D_MODEL), F32)
        for jc in range(n_chunks):
            halves = []
            for half in range(2):
                cols = pl.ds(half * D_FF + jc * FFN_CHUNK, FFN_CHUNK)
                uu = _dot_nt(h3, wup_v[cols, :])
                uu_ref[:, cols] = uu.astype(BF16)
                cc = _ffn_conv(uu, carry[:, cols], fw_ref, fb_ref, cols)
                cc_ref[:, cols] = cc.astype(BF16)
                halves.append(cc)
                carry[:, cols] = uu[tm - FFN_HALO:, :]
            gate, val = halves
            a = (gate * _sigmoid(gate) * val).astype(BF16)
            a_ref[:, pl.ds(jc * FFN_CHUNK, FFN_CHUNK)] = a
            acc = acc + _dot(a, wdown_v[pl.ds(jc * FFN_CHUNK, FFN_CHUNK), :])
        x3 = x2v + acc

        xh3, r3 = _rms_fwd(x3)
        gfin = gfin_ref[...]
        err = xh3 * gfin - tgt_ref[...]
        loss_ref[...] += jnp.full(loss_ref.shape, jnp.sum(err * err) * (0.5 / D_MODEL), F32)
        dy = err * (1.0 / D_MODEL)
        dgfin_ref[...] += _colsum(dy * xh3)
        dx3 = _rms_bwd(dy, xh3, r3, gfin)
        dx3_ref[...] = dx3
        dx3b_ref[...] = dx3.astype(BF16)

    row = lambda w: pl.BlockSpec((tm, w), lambda i: (i, 0))
    return pl.pallas_call(
        body, name="fwd_ffn", grid=(n_tiles,),
        in_specs=[row(D_MODEL), row(D_MODEL), _full((1, D_MODEL)), pl.BlockSpec(memory_space=pl.ANY),
                  _full((FFN_CONV_WIDTH, 2 * D_FF)), _full((1, 2 * D_FF)), _full((1, D_MODEL))],
        out_specs=[row(2 * D_FF), row(2 * D_FF), row(D_FF), row(D_MODEL), row(D_MODEL), row(D_MODEL), _full((8, 128)),
                   _full((1, D_MODEL))],
        out_shape=[jax.ShapeDtypeStruct((tokens, 2 * D_FF), BF16), jax.ShapeDtypeStruct((tokens, 2 * D_FF), BF16),
                   jax.ShapeDtypeStruct((tokens, D_FF), BF16),
                   jax.ShapeDtypeStruct((tokens, D_MODEL), BF16), jax.ShapeDtypeStruct((tokens, D_MODEL), F32),
                   jax.ShapeDtypeStruct((tokens, D_MODEL), BF16),
                   jax.ShapeDtypeStruct((8, 128), F32), jax.ShapeDtypeStruct((1, D_MODEL), F32)],
        scratch_shapes=[pltpu.VMEM((2 * D_FF, D_MODEL), BF16), pltpu.VMEM((D_FF, D_MODEL), BF16),
                        pltpu.VMEM((FFN_HALO, 2 * D_FF), F32), pltpu.SemaphoreType.DMA],
        compiler_params=_params(),
    )(x2, target, g_ffn, gw, ffn_w, ffn_b, g_final)


def _bwd_ffn(dx3, x2, uu_all, cc_all, gw, g_ffn, ffn_w, seq, tm):
    tokens = x2.shape[0]
    n_tiles = tokens // tm
    tps = seq // tm
    n_chunks = D_FF // FFN_CHUNK

    def body(dx3_ref, x2_ref, uu_ref, cc_ref, gffn_ref, gw_hbm, fw_ref,
             dx2_ref, dx2b_ref, duu_ref, dfb_ref, dfw_ref, dg_ref,
             wup_v, wdown_v, carry, sem):
        i = pl.program_id(0)
        t = n_tiles - 1 - i

        @pl.when(i == 0)
        def _():
            copies = _load_weight(gw_hbm, "w_up", wup_v, sem) + _load_weight(gw_hbm, "w_down", wdown_v, sem)
            for cp in copies:
                cp.start()
            for cp in copies:
                cp.wait()
            dfb_ref[...] = jnp.zeros_like(dfb_ref)
            dfw_ref[...] = jnp.zeros_like(dfw_ref)
            dg_ref[...] = jnp.zeros_like(dg_ref)

        @pl.when(t % tps == tps - 1)
        def _():
            carry[...] = jnp.zeros_like(carry)

        dx3v = dx3_ref[...]
        dx3b = dx3v.astype(BF16)
        dh3 = jnp.zeros((tm, D_MODEL), F32)
        for jc in range(n_chunks):
            da = _dot_nt(dx3b, wdown_v[pl.ds(jc * FFN_CHUNK, FFN_CHUNK), :])
            colss = [pl.ds(half * D_FF + jc * FFN_CHUNK, FFN_CHUNK) for half in range(2)]
            gate, val = [cc_ref[:, cols].astype(F32) for cols in colss]
            sg = _sigmoid(gate)
            dgate = da * val * (sg * (1.0 + gate * (1.0 - sg)))
            dval = da * (gate * sg)
            for dcc, cols in zip((dgate, dval), colss):
                uu = uu_ref[:, cols].astype(F32)
                dfb_ref[:, cols] += _colsum(dcc)
                ext = jnp.concatenate([dcc, carry[:, cols]], axis=0)
                carry[:, cols] = dcc[:FFN_HALO, :]
                n1 = pltpu.roll(ext, tm + FFN_HALO - 1, 0)[:tm, :]
                n2 = pltpu.roll(ext, tm + FFN_HALO - 2, 0)[:tm, :]
                duu = fw_ref[2:3, cols] * dcc + fw_ref[1:2, cols] * n1 + fw_ref[0:1, cols] * n2
                dfw_ref[2:3, cols] += _colsum(uu * dcc)
                dfw_ref[1:2, cols] += _colsum(uu * n1)
                dfw_ref[0:1, cols] += _colsum(uu * n2)
                duub = duu.astype(BF16)
                duu_ref[:, cols] = duub
                dh3 = dh3 + _dot(duub, wup_v[cols, :])
        xh, r = _rms_fwd(x2_ref[...])
        dg_ref[...] += _colsum(dh3 * xh)
        dx2 = dx3v + _rms_bwd(dh3, xh, r, gffn_ref[...])
        dx2_ref[...] = dx2
        dx2b_ref[...] = dx2.astype(BF16)

    rev = lambda w: pl.BlockSpec((tm, w), lambda i: (n_tiles - 1 - i, 0))
    return pl.pallas_call(
        body, name="bwd_ffn", grid=(n_tiles,),
        in_specs=[rev(D_MODEL), rev(D_MODEL), rev(2 * D_FF), rev(2 * D_FF), _full((1, D_MODEL)),
                  pl.BlockSpec(memory_space=pl.ANY), _full((FFN_CONV_WIDTH, 2 * D_FF))],
        out_specs=[rev(D_MODEL), rev(D_MODEL), rev(2 * D_FF), _full((1, 2 * D_FF)), _full((FFN_CONV_WIDTH, 2 * D_FF)),
                   _full((1, D_MODEL))],
        out_shape=[jax.ShapeDtypeStruct((tokens, D_MODEL), F32), jax.ShapeDtypeStruct((tokens, D_MODEL), BF16),
                   jax.ShapeDtypeStruct((tokens, 2 * D_FF), BF16),
                   jax.ShapeDtypeStruct((1, 2 * D_FF), F32), jax.ShapeDtypeStruct((FFN_CONV_WIDTH, 2 * D_FF), F32),
                   jax.ShapeDtypeStruct((1, D_MODEL), F32)],
        scratch_shapes=[pltpu.VMEM((2 * D_FF, D_MODEL), BF16), pltpu.VMEM((D_FF, D_MODEL), BF16),
                        pltpu.VMEM((FFN_HALO, 2 * D_FF), F32), pltpu.SemaphoreType.DMA],
        compiler_params=_params(),
    )(dx3, x2, uu_all, cc_all, g_ffn, gw, ffn_w)


def _bwd_attn(dx2, x1, q, kv, gw, g_x, after, seq, tm):
    tokens = x1.shape[0]
    n_tiles = tokens // tm
    tps = seq // tm
    n_b = tokens // seq

    def body(dx2_ref, x1_ref, q_ref, kv_ref, g_ref, gw_hbm, after_ref, dx1_ref, dx1b_ref, dq_ref, dkv_ref, dg_ref,
             wq_v, wo_v, sem):
        del after_ref
        i = pl.program_id(0)

        @pl.when(i == 0)
        def _():
            copies = _load_weight(gw_hbm, "w_q", wq_v, sem) + _load_weight(gw_hbm, "w_o", wo_v, sem)
            for cp in copies:
                cp.start()
            for cp in copies:
                cp.wait()
            dg_ref[...] = jnp.zeros_like(dg_ref)

        @pl.when(i % tps == 0)
        def _():
            dkv_ref[...] = jnp.zeros_like(dkv_ref)

        dx2v = dx2_ref[...]
        do = _dot_nt(dx2v.astype(BF16), wo_v[...]).astype(BF16)
        q = q_ref[...]
        dqs = []
        for h in range(HEADS):
            lo = h * HEAD_DIM
            kcols, vcols = pl.ds(lo, HEAD_DIM), pl.ds(D_MODEL + lo, HEAD_DIM)
            qh, doh = q[:, lo:lo + HEAD_DIM], do[:, lo:lo + HEAD_DIM]
            p = _softmax_rows(_dot_nt(qh, kv_ref[:, kcols]))
            dp = _dot_nt(doh, kv_ref[:, vcols])
            dkv_ref[:, vcols] += _dot_tn(p.astype(BF16), doh)
            ds = (p * (dp - jnp.sum(dp * p, axis=-1, keepdims=True))).astype(BF16)
            dqs.append(_dot(ds, kv_ref[:, kcols]) * (HEAD_DIM ** -0.5))
            dkv_ref[:, kcols] += _dot_tn(ds, qh)
        dq = jnp.concatenate(dqs, axis=-1).astype(BF16)
        dq_ref[...] = dq
        dh2 = _dot_nt(dq, wq_v[...])
        xh, r = _rms_fwd(x1_ref[...])
        dg_ref[...] += _colsum(dh2 * xh)
        dx1 = dx2v + _rms_bwd(dh2, xh, r, g_ref[...])
        dx1_ref[...] = dx1
        dx1b_ref[...] = dx1.astype(BF16)

    row = lambda w: pl.BlockSpec((tm, w), lambda i: (i, 0))
    per_b = pl.BlockSpec((N_MEM, 2 * D_MODEL), lambda i: (i // tps, 0))
    return pl.pallas_call(
        body, name="bwd_attn", grid=(n_tiles,),
        in_specs=[row(D_MODEL), row(D_MODEL), row(D_MODEL), per_b, _full((1, D_MODEL)), pl.BlockSpec(memory_space=pl.ANY),
                  _full(after.shape)],
        out_specs=[row(D_MODEL), row(D_MODEL), row(D_MODEL), per_b, _full((1, D_MODEL))],
        out_shape=[jax.ShapeDtypeStruct((tokens, D_MODEL), F32), jax.ShapeDtypeStruct((tokens, D_MODEL), BF16),
                   jax.ShapeDtypeStruct((tokens, D_MODEL), BF16),
                   jax.ShapeDtypeStruct((n_b * N_MEM, 2 * D_MODEL), F32), jax.ShapeDtypeStruct((1, D_MODEL), F32)],
        scratch_shapes=[pltpu.VMEM((D_MODEL, D_MODEL), BF16), pltpu.VMEM((D_MODEL, D_MODEL), BF16), pltpu.SemaphoreType.DMA],
        compiler_params=_params(),
    )(dx2, x1, q, kv, g_x, gw, after)


def _bwd_kv(dkv, mem2d, gw, g_mem):
    rows = mem2d.shape[0]
    n_b = rows // N_MEM

    def body(dkv_ref, mem_ref, gw_hbm, dkvb_ref, dg_ref, wkv_v, sem):
        @pl.when(pl.program_id(0) == 0)
        def _():
            copies = _load_weight(gw_hbm, "w_kv", wkv_v, sem)
            for cp in copies:
                cp.start()
            for cp in copies:
                cp.wait()
            dg_ref[...] = jnp.zeros_like(dg_ref)

        dkvb = dkv_ref[...].astype(BF16)
        dkvb_ref[...] = dkvb
        dmn = _dot(dkvb, wkv_v[...])
        mh, _ = _rms_fwd(mem_ref[...])
        dg_ref[...] += _colsum(dmn * mh)

    del g_mem
    return pl.pallas_call(
        body, name="bwd_kv", grid=(n_b,),
        in_specs=[pl.BlockSpec((N_MEM, 2 * D_MODEL), lambda b: (b, 0)), pl.BlockSpec((N_MEM, D_MODEL), lambda b: (b, 0)),
                  pl.BlockSpec(memory_space=pl.ANY)],
        out_specs=[pl.BlockSpec((N_MEM, 2 * D_MODEL), lambda b: (b, 0)), _full((1, D_MODEL))],
        out_shape=[jax.ShapeDtypeStruct((rows, 2 * D_MODEL), BF16), jax.ShapeDtypeStruct((1, D_MODEL), F32)],
        scratch_shapes=[pltpu.VMEM((2 * D_MODEL, D_MODEL), BF16), pltpu.SemaphoreType.DMA],
        compiler_params=_params(),
    )(dkv, mem2d, gw)


def _bwd_mix(dx1, x2d, u_all, c_all, pooled_all, gw, g_mix, conv_w, ln_g, ln_b, pool_w, pool_scale, after, seq, tm):
    tokens = x2d.shape[0]
    n_tiles = tokens // tm
    tps = seq // tm

    def body(dx1_ref, x_ref, u_ref, c_ref, pooled_ref, gmix_ref, gw_hbm, cw_ref, lng_ref, lnb_ref, pw_ref, ps_ref,
             after_ref, dx_ref, du_ref, dgmix_ref, dcw_ref, dcb_ref, dlng_ref, dlnb_ref, dpw_ref, dps_ref,
             win_v, wout_v, dc_carry, e_carry, sem):
        del after_ref
        i = pl.program_id(0)
        t = n_tiles - 1 - i

        @pl.when(i == 0)
        def _():
            copies = _load_weight(gw_hbm, "w_in", win_v, sem) + _load_weight(gw_hbm, "w_out", wout_v, sem)
            for cp in copies:
                cp.start()
            for cp in copies:
                cp.wait()
            for ref in (dgmix_ref, dcw_ref, dcb_ref, dlng_ref, dlnb_ref, dpw_ref, dps_ref):
                ref[...] = jnp.zeros_like(ref)

        @pl.when(t % tps == tps - 1)
        def _():
            dc_carry[...] = jnp.zeros_like(dc_carry)
            e_carry[...] = jnp.zeros_like(e_carry)

        dx1v = dx1_ref[...]
        dymix = _dot_nt(dx1v.astype(BF16), wout_v[...])
        dyc, dyp = dymix[:, :D_CONV], dymix[:, D_CONV:]
        u = u_ref[...]
        val, gate = u[:, :D_CONV], u[:, D_CONV:2 * D_CONV]

        conv = c_ref[...]
        mu = jnp.mean(conv, axis=-1, keepdims=True)
        cen = conv - mu
        rs = lax.rsqrt(jnp.mean(cen * cen, axis=-1, keepdims=True) + EPS)
        chat = cen * rs
        ln = chat * lng_ref[...] + lnb_ref[...]
        sl = _sigmoid(ln)
        dln = dyc * (sl * (1.0 + ln * (1.0 - sl)))
        dlng_ref[...] += _colsum(dln * chat)
        dlnb_ref[...] += _colsum(dln)
        dchat = dln * lng_ref[...]
        dc = rs * (dchat - jnp.mean(dchat, axis=-1, keepdims=True)
                   - chat * jnp.mean(dchat * chat, axis=-1, keepdims=True))
        dcb_ref[...] += _colsum(dc)
        sg = _sigmoid(gate)
        hc = val * sg
        ext = jnp.concatenate([dc, dc_carry[...]], axis=0)
        dc_carry[...] = dc[:CONV_HALO, :]
        dhc = jnp.zeros((tm, D_CONV), F32)
        ahead_by = _sublane_shifts(ext)
        for k in range(CONV_WIDTH):
            whole, part = divmod(CONV_WIDTH - 1 - k, 8)
            tap = ahead_by[part][8 * whole:8 * whole + tm, :]
            dhc = dhc + cw_ref[k:k + 1, :] * tap
            dcw_ref[k:k + 1, :] += _colsum(hc * tap)
        du_ref[:, :D_CONV] = (dhc * sg).astype(BF16)
        du_ref[:, D_CONV:2 * D_CONV] = (dhc * val * (sg * (1.0 - sg))).astype(BF16)

        pos = lax.broadcasted_iota(jnp.int32, (tm, 1), 0) + (t % tps) * tm
        es, dpooled = [], []
        for g, w in enumerate(POOL_WINDOWS):
            cols = pl.ds(g * POOL_GROUP_DIM, POOL_GROUP_DIM)
            lo = g * POOL_GROUP_DIM
            pooled = pooled_ref[:, cols]
            pw = pw_ref[g].astype(BF16)
            dyg = dyp[:, lo:lo + POOL_GROUP_DIM]
            dps_ref[:, cols] += _colsum(dyg * _dot(pooled, pw))
            dmixed = (dyg * ps_ref[:, cols]).astype(BF16)
            dpw_ref[g] += _dot_tn(pooled, dmixed)
            dpo = _dot_nt(dmixed, pw)
            dpooled.append(dpo)
            es.append(dpo / jnp.minimum(pos + 1, w).astype(F32))
        e = jnp.concatenate(es, axis=-1)
        run = jnp.concatenate([e, e_carry[...]], axis=0)
        e_carry[...] = e[:POOL_HALO, :]
        rows = tm + POOL_HALO
        for g, w in enumerate(POOL_WINDOWS):
            lo = g * POOL_GROUP_DIM
            run = run[:, POOL_GROUP_DIM if g else 0:]
            run = run + pltpu.roll(run, rows - w // 2, 0)
            du_ref[:, 2 * D_CONV + lo:2 * D_CONV + lo + POOL_GROUP_DIM] = (
                run[:tm, :POOL_GROUP_DIM] - dpooled[g]).astype(BF16)

        dh1 = _dot(du_ref[...], win_v[...])
        xh, r = _rms_fwd(x_ref[...])
        dgmix_ref[...] += _colsum(dh1 * xh)
        dx_ref[...] = dx1v + _rms_bwd(dh1, xh, r, gmix_ref[...])

    rev = lambda w: pl.BlockSpec((tm, w), lambda i: (n_tiles - 1 - i, 0))
    return pl.pallas_call(
        body, name="bwd_mix", grid=(n_tiles,),
        in_specs=[rev(D_MODEL), rev(D_MODEL), rev(D_IN), rev(D_CONV), rev(D_POOL), _full((1, D_MODEL)),
                  pl.BlockSpec(memory_space=pl.ANY), _full((CONV_WIDTH, D_CONV)), _full((1, D_CONV)), _full((1, D_CONV)),
                  _full((4, POOL_GROUP_DIM, POOL_GROUP_DIM)), _full((1, D_POOL)), _full(after.shape)],
        out_specs=[rev(D_MODEL), rev(D_IN), _full((1, D_MODEL)), _full((CONV_WIDTH, D_CONV)), _full((1, D_CONV)),
                   _full((1, D_CONV)), _full((1, D_CONV)), _full((4, POOL_GROUP_DIM, POOL_GROUP_DIM)), _full((1, D_POOL))],
        out_shape=[jax.ShapeDtypeStruct((tokens, D_MODEL), F32), jax.ShapeDtypeStruct((tokens, D_IN), BF16),
                   jax.ShapeDtypeStruct((1, D_MODEL), F32), jax.ShapeDtypeStruct((CONV_WIDTH, D_CONV), F32),
                   jax.ShapeDtypeStruct((1, D_CONV), F32), jax.ShapeDtypeStruct((1, D_CONV), F32),
                   jax.ShapeDtypeStruct((1, D_CONV), F32),
                   jax.ShapeDtypeStruct((4, POOL_GROUP_DIM, POOL_GROUP_DIM), F32), jax.ShapeDtypeStruct((1, D_POOL), F32)],
        scratch_shapes=[pltpu.VMEM((D_IN, D_MODEL), BF16), pltpu.VMEM((D_MODEL, D_MODEL), BF16),
                        pltpu.VMEM((CONV_HALO, D_CONV), F32), pltpu.VMEM((POOL_HALO, D_POOL), F32),
                        pltpu.SemaphoreType.DMA],
        compiler_params=_params(),
    )(dx1, x2d, u_all, c_all, pooled_all, g_mix, gw, conv_w, ln_g, ln_b, pool_w, pool_scale, after)


def _wgrad(a, b, name, after=None, tm=256):
    tokens, m = a.shape
    n = b.shape[1]
    extra = [] if after is None else [after]

    def body(a_ref, b_ref, *rest):
        rest[-1][...] = _dot_tn(a_ref[...], b_ref[...]).astype(rest[-1].dtype)

    return pl.pallas_call(
        body, name=name, grid=(m // tm,),
        in_specs=[pl.BlockSpec((tokens, tm), lambda i: (0, i)), _full((tokens, n))] + [_full(t.shape) for t in extra],
        out_specs=pl.BlockSpec((tm, n), lambda i: (i, 0)),
        out_shape=jax.ShapeDtypeStruct((m, n), BF16),
        compiler_params=_params(),
    )(a, b, *extra)


def _adamw_update(w, g, m, v):
    nm = ADAM_B1 * m + (1.0 - ADAM_B1) * g
    nv = ADAM_B2 * v + (1.0 - ADAM_B2) * (g * g)
    m_hat = nm / (1.0 - ADAM_B1 ** ADAM_STEP)
    v_hat = nv / (1.0 - ADAM_B2 ** ADAM_STEP)
    return -ADAM_LR * (m_hat / (jnp.sqrt(v_hat) + ADAM_EPS) + ADAM_WD * w), nm, nv


def _adamw_small(ws, gs, ms, vs):
    n = len(ws)

    def body(*refs):
        ins, outs = refs[:4 * n], refs[4 * n:]
        for k in range(n):
            d, nm, nv = _adamw_update(*[ins[j * n + k][...] for j in range(4)])
            outs[k][...] = d
            outs[n + k][...] = nm
            outs[2 * n + k][...] = nv

    vmem = pl.BlockSpec(memory_space=pltpu.VMEM)
    outs = pl.pallas_call(
        body, name="adamw_small",
        in_specs=[vmem] * (4 * n), out_specs=[vmem] * (3 * n),
        out_shape=[jax.ShapeDtypeStruct(w.shape, F32) for w in ws] * 3,
    )(*ws, *gs, *ms, *vs)
    return outs[:n], outs[n:2 * n], outs[2 * n:]


def _adamw(w, g, m, v, name):
    rows, cols = w.shape
    tile = rows
    for cand in (512, 256, 128, 64, 32, 16, 8):
        if rows % cand == 0:
            tile = cand
            break

    def body(w_ref, g_ref, m_ref, v_ref, d_ref, nm_ref, nv_ref):
        d_ref[...], nm_ref[...], nv_ref[...] = _adamw_update(w_ref[...], g_ref[...], m_ref[...], v_ref[...])

    spec = pl.BlockSpec((tile, cols), lambda i: (i, 0))
    return pl.pallas_call(
        body, name=name, grid=(rows // tile,),
        in_specs=[spec] * 4, out_specs=[spec] * 3,
        out_shape=[jax.ShapeDtypeStruct((rows, cols), F32)] * 3,
        compiler_params=_params(("arbitrary",)),
    )(w, g, m, v)


SMALL = (("norm_mix_g", (1, 1024)), ("conv_dw_b", (1, 512)), ("conv_ln_g", (1, 512)), ("conv_ln_b", (1, 512)),
         ("pool_w", (1, 4, 128, 128)), ("pool_scale", (1, 512)), ("norm_xattn_g", (1, 1024)), ("norm_mem_g", (1, 1024)),
         ("norm_ffn_g", (1, 1024)), ("ffn_dw_b", (1, 5632)), ("norm_final_g", (1024,)))
LANES = 128


def _pack_rows(arrs):
    flat = jnp.concatenate([a.reshape(-1) for a in arrs])
    pad = (-flat.shape[0]) % (8 * LANES)
    return jnp.pad(flat, (0, pad)).reshape(-1, LANES)


def kernel(x, mem, norm_mix_g, w_in, conv_dw_w, conv_dw_b, conv_ln_g, conv_ln_b, pool_w, pool_scale, w_out, norm_xattn_g, norm_mem_g, w_q, w_kv, w_o, norm_ffn_g, w_up, ffn_dw_w, ffn_dw_b, w_down, norm_final_g, loss_target, m_norm_mix_g, m_w_in, m_conv_dw_w, m_conv_dw_b, m_conv_ln_g, m_conv_ln_b, m_pool_w, m_pool_scale, m_w_out, m_norm_xattn_g, m_norm_mem_g, m_w_q, m_w_kv, m_w_o, m_norm_ffn_g, m_w_up, m_ffn_dw_w, m_ffn_dw_b, m_w_down, m_norm_final_g, v_norm_mix_g, v_w_in, v_conv_dw_w, v_conv_dw_b, v_conv_ln_g, v_conv_ln_b, v_pool_w, v_pool_scale, v_w_out, v_norm_xattn_g, v_norm_mem_g, v_w_q, v_w_kv, v_w_o, v_norm_ffn_g, v_w_up, v_ffn_dw_w, v_ffn_dw_b, v_w_down, v_norm_final_g):
    weights = dict(norm_mix_g=norm_mix_g, w_in=w_in, conv_dw_w=conv_dw_w, conv_dw_b=conv_dw_b, conv_ln_g=conv_ln_g,
                   conv_ln_b=conv_ln_b, pool_w=pool_w, pool_scale=pool_scale, w_out=w_out, norm_xattn_g=norm_xattn_g,
                   norm_mem_g=norm_mem_g, w_q=w_q, w_kv=w_kv, w_o=w_o, norm_ffn_g=norm_ffn_g, w_up=w_up,
                   ffn_dw_w=ffn_dw_w, ffn_dw_b=ffn_dw_b, w_down=w_down, norm_final_g=norm_final_g)
    moments_m = dict(norm_mix_g=m_norm_mix_g, w_in=m_w_in, conv_dw_w=m_conv_dw_w, conv_dw_b=m_conv_dw_b,
                     conv_ln_g=m_conv_ln_g, conv_ln_b=m_conv_ln_b, pool_w=m_pool_w, pool_scale=m_pool_scale,
                     w_out=m_w_out, norm_xattn_g=m_norm_xattn_g, norm_mem_g=m_norm_mem_g, w_q=m_w_q, w_kv=m_w_kv,
                     w_o=m_w_o, norm_ffn_g=m_norm_ffn_g, w_up=m_w_up, ffn_dw_w=m_ffn_dw_w, ffn_dw_b=m_ffn_dw_b,
                     w_down=m_w_down, norm_final_g=m_norm_final_g)
    moments_v = dict(norm_mix_g=v_norm_mix_g, w_in=v_w_in, conv_dw_w=v_conv_dw_w, conv_dw_b=v_conv_dw_b,
                     conv_ln_g=v_conv_ln_g, conv_ln_b=v_conv_ln_b, pool_w=v_pool_w, pool_scale=v_pool_scale,
                     w_out=v_w_out, norm_xattn_g=v_norm_xattn_g, norm_mem_g=v_norm_mem_g, w_q=v_w_q, w_kv=v_w_kv,
                     w_o=v_w_o, norm_ffn_g=v_norm_ffn_g, w_up=v_w_up, ffn_dw_w=v_ffn_dw_w, ffn_dw_b=v_ffn_dw_b,
                     w_down=v_w_down, norm_final_g=v_norm_final_g)
    order = list(weights)
    transposed = ("w_in", "w_kv", "w_up")

    n_b, seq, _ = x.shape
    tokens = n_b * seq
    tm_mix = min(512, seq // 2)
    tm_ffn = min(256, seq // 2)
    dev = 4 * lax.axis_index("x") + 2 * lax.axis_index("y") + lax.axis_index("c")

    packs = [jnp.concatenate([weights[n][0].T if n in transposed else weights[n][0] for n in names], axis=0).astype(BF16)
             for names in AG_GROUPS]
    small_sharded = _pack_rows([conv_dw_w[0], ffn_dw_w[0]])
    gw_mix, gsmall = _all_gather([packs[0], small_sharded], "weights_all_gather")
    flights = []
    after = gw_mix
    for k in (1, 2):
        own_in_place = lax.dynamic_update_slice(lax.empty((N_DEV,) + packs[k].shape, BF16), packs[k][None], (dev, 0, 0))
        flights.append(_gather_start(own_in_place, after, "weights_gather_start_%d" % k))
        after = flights[-1][3]
    gflat = gsmall.reshape(N_DEV, -1)
    n_cw = CONV_WIDTH * (D_CONV // N_DEV)
    n_fw = FFN_CONV_WIDTH * (2 * D_FF // N_DEV)
    conv_w = gflat[:, :n_cw].reshape(N_DEV, CONV_WIDTH, D_CONV // N_DEV).transpose(1, 0, 2).reshape(CONV_WIDTH, D_CONV)
    ffn_w = gflat[:, n_cw:n_cw + n_fw].reshape(N_DEV, FFN_CONV_WIDTH, 2 * D_FF // N_DEV).transpose(1, 0, 2).reshape(
        FFN_CONV_WIDTH, 2 * D_FF)

    x2d = x.reshape(tokens, D_MODEL)
    mem2d = mem.reshape(n_b * N_MEM, D_MODEL)
    tgt2d = loss_target.reshape(tokens, D_MODEL)
    g_final = norm_final_g.reshape(1, D_MODEL)

    def gather_finish(flight, after, tag):
        fwd_send, fwd_recv, buf = _gather_forward(*flight[:3], after, "weights_gather_forward_" + tag)
        return _gather_finish(fwd_send, fwd_recv, buf, "weights_gather_finish_" + tag)

    x1, u_all, c_all, pooled_all, ymix, h1 = _fwd_mix(
        x2d, gw_mix, norm_mix_g, conv_w, conv_dw_b, conv_ln_g, conv_ln_b, pool_w[0], pool_scale, flights[1][3],
        seq, tm_mix)
    gw_attn = gather_finish(flights[0], x1, "1")
    mem_n, kv = _fwd_kv(mem2d, gw_attn, norm_mem_g)
    x2, h2, q, o = _fwd_attn(x1, kv, gw_attn, norm_xattn_g, seq, tm_mix)
    gw_ffn = gather_finish(flights[1], x2, "2")
    uu_all, cc_all, a_all, h3, dx3, dx3b, loss_part, dg_final = _fwd_ffn(
        x2, tgt2d, gw_ffn, norm_ffn_g, ffn_w, ffn_dw_b, g_final, seq, tm_ffn)

    table = _owner_table()

    def sibling_start(names, tag):
        parts = [part[n].reshape(N_DEV, W_OFF[n][1], D_MODEL) for n in names]
        return _exchange_start(parts, 4, _to_sibling, "rs_sibling_exchange_start_" + tag)

    def chips_start(flight, after, tag):
        parts, landed = _exchange_wait(*flight[:4], after, 4, _to_sibling, "rs_sibling_exchange_wait_" + tag)
        sums = _chip_partial_sums(table, parts, landed, "rs_chip_partial_sums_" + tag)
        return parts, landed, _exchange_start(sums, 3, _to_chip, "rs_chip_exchange_start_" + tag)

    grads, delta, new_m, new_v = {}, {}, {}, {}

    def reduce_finish(names, parts, landed, flight, after, tag):
        _, from_chips = _exchange_wait(*flight[:4], after, 3, _to_chip, "rs_chip_exchange_wait_" + tag)
        as_rows = {n: n not in transposed or W_OFF[n][1] % LANES != 0 for n in names}
        states = [tuple(t[n][0].T if n in transposed else t[n][0] for t in (weights, moments_m, moments_v))
                  if as_rows[n] else None for n in names]
        results = _final_update(table, parts, landed, from_chips, states, "rs_final_update_" + tag)
        for n, res in zip(names, results):
            back = (lambda t: t.T[None]) if n in transposed else (lambda t: t[None])
            grads[n] = back(res[0])
            if as_rows[n]:
                delta[n], new_m[n], new_v[n] = [back(t) for t in res[1:]]
            else:
                delta[n], new_m[n], new_v[n] = [t[None] for t in _adamw(
                    weights[n][0], grads[n][0], moments_m[n][0], moments_v[n][0], "adamw_" + n)]
        return delta[names[-1]]

    part = {}
    dx2, dx2b, duu, d_ffn_b, d_ffn_w, dg_ffn = _bwd_ffn(dx3, x2, uu_all, cc_all, gw_ffn, norm_ffn_g, ffn_w, seq, tm_ffn)
    part["w_up"] = _wgrad(duu, h3, "wgrad_w_up")
    part["w_down"] = _wgrad(a_all, dx3b, "wgrad_w_down")
    to_sibling_a = sibling_start(RS_GROUPS["a"], "a")
    dx1, dx1b, dq, dkv, dg_x = _bwd_attn(dx2, x1, q, kv, gw_attn, norm_xattn_g, to_sibling_a[4], seq, tm_mix)
    parts_a, landed_a, flight_a = chips_start(to_sibling_a, dx1, "a")
    dkv_b, dg_mem = _bwd_kv(dkv, mem2d, gw_attn, norm_mem_g)
    part["w_q"] = _wgrad(h2, dq, "wgrad_w_q", after=flight_a[4])
    part["w_kv"] = _wgrad(dkv_b, mem_n, "wgrad_w_kv")
    part["w_o"] = _wgrad(o, dx2b, "wgrad_w_o")
    to_sibling_b = sibling_start(RS_GROUPS["b"], "b")
    parts_b, landed_b, flight_b = chips_start(to_sibling_b, to_sibling_b[4], "b")
    dx, du, dg_mix, d_conv_w, d_conv_b, d_ln_g, d_ln_b, d_pool_w, d_pool_scale = _bwd_mix(
        dx1, x2d, u_all, c_all, pooled_all, gw_mix, norm_mix_g, conv_w, conv_ln_g, conv_ln_b, pool_w[0], pool_scale,
        flight_b[4], seq, tm_mix)
    grad_x = dx.reshape(x.shape)

    small_grads = dict(norm_mix_g=dg_mix, conv_dw_b=d_conv_b, conv_ln_g=d_ln_g, conv_ln_b=d_ln_b, pool_w=d_pool_w,
                       pool_scale=d_pool_scale, norm_xattn_g=dg_x, norm_mem_g=dg_mem, norm_ffn_g=dg_ffn,
                       ffn_dw_b=d_ffn_b, norm_final_g=dg_final)
    small_list = [small_grads[n] for n, _ in SMALL] + [d_conv_w, d_ffn_w, loss_part[:1]]
    small_mine = _pack_rows(small_list)
    small_flight = _broadcast_start(
        lax.dynamic_update_slice(lax.empty((N_DEV,) + small_mine.shape, F32), small_mine[None], (dev, 0, 0)),
        "small_grads_broadcast_start")

    part["w_in"] = _wgrad(du, h1, "wgrad_w_in", after=small_flight[3])
    part["w_out"] = _wgrad(ymix, dx1b, "wgrad_w_out")
    to_sibling_c = sibling_start(RS_GROUPS["c"], "c")
    parts_c, landed_c, flight_c = chips_start(to_sibling_c, to_sibling_c[4], "c")
    updated_a = reduce_finish(RS_GROUPS["a"], parts_a, landed_a, flight_a, flight_c[4], "a")
    updated_b = reduce_finish(RS_GROUPS["b"], parts_b, landed_b, flight_b, updated_a, "b")
    updated_c = reduce_finish(RS_GROUPS["c"], parts_c, landed_c, flight_c, updated_b, "c")

    small_all = _broadcast_wait(*small_flight[:3], updated_c, "small_grads_broadcast_wait")
    small_sum = _sum_blocks(small_all).reshape(-1)

    pos = 0
    for n, shape in SMALL:
        size = 1
        for s in shape:
            size *= s
        grads[n] = small_sum[pos:pos + size].reshape(shape)
        pos += size
    full_conv_w = small_sum[pos:pos + CONV_WIDTH * D_CONV].reshape(CONV_WIDTH, D_CONV)
    pos += CONV_WIDTH * D_CONV
    full_ffn_w = small_sum[pos:pos + FFN_CONV_WIDTH * 2 * D_FF].reshape(FFN_CONV_WIDTH, 2 * D_FF)
    loss = small_sum[pos + FFN_CONV_WIDTH * 2 * D_FF]
    grads["conv_dw_w"] = lax.dynamic_slice_in_dim(full_conv_w, dev * (D_CONV // N_DEV), D_CONV // N_DEV, axis=1)[None]
    grads["ffn_dw_w"] = lax.dynamic_slice_in_dim(full_ffn_w, dev * (2 * D_FF // N_DEV), 2 * D_FF // N_DEV, axis=1)[None]

    small_names = [n for n in order if n not in W_OFF]
    two_d = lambda t: t.reshape(1, -1) if t.ndim == 1 else (t[0] if t.ndim == 3 else t)
    outs = _adamw_small(*[[two_d(t[n]) for n in small_names] for t in (weights, grads, moments_m, moments_v)])
    for res, out in zip((delta, new_m, new_v), outs):
        for n, o in zip(small_names, out):
            res[n] = o.reshape(weights[n].shape)

    return (loss, grad_x, *[grads[n] for n in order], *[delta[n] for n in order],
            *[new_m[n] for n in order], *[new_v[n] for n in order])
```

```python
import functools

import jax
import jax.numpy as jnp
from jax import lax
from jax.experimental import pallas as pl
from jax.experimental.pallas import tpu as pltpu

F32 = jnp.float32
BF16 = jnp.bfloat16
MESH = pl.DeviceIdType.MESH

N_DEV = 8
D_MODEL = 1024
D_CONV = 512
D_POOL = 512
CONV_WIDTH = 31
POOL_WINDOWS = (2, 4, 8, 16)
POOL_GROUP_DIM = 128
D_IN = 1536
N_MEM = 256
HEADS = 4
HEAD_DIM = 256
D_FF = 2816
FFN_CONV_WIDTH = 3
EPS = 1e-6
ADAM_LR = 0.001
ADAM_B1 = 0.9
ADAM_B2 = 0.999
ADAM_EPS = 1e-08
ADAM_WD = 0.01
ADAM_STEP = 10

VMEM_LIMIT_V7X = 56 * 1024 * 1024
CONV_HALO = 32
POOL_HALO = 16
FFN_HALO = 8
FFN_CHUNK = 2816

W_ROWS = (("w_in", 192), ("w_out", 128), ("w_q", 128), ("w_kv", 256), ("w_o", 128), ("w_up", 704), ("w_down", 352))
AG_GROUPS = (("w_in", "w_out"), ("w_q", "w_kv", "w_o"), ("w_up", "w_down"))
W_OFF = {}
for _names in AG_GROUPS:
    _o = 0
    for _n in _names:
        W_OFF[_n] = (_o, dict(W_ROWS)[_n])
        _o += dict(W_ROWS)[_n]
RS_GROUPS = {"a": ("w_up", "w_down"), "b": ("w_q", "w_kv", "w_o"), "c": ("w_in", "w_out")}


def _dot(a, b):
    return jnp.dot(a, b, preferred_element_type=F32)


def _dot_nt(a, b):
    return lax.dot_general(a, b, (((1,), (1,)), ((), ())), preferred_element_type=F32)


def _dot_tn(a, b):
    return lax.dot_general(a, b, (((0,), (0,)), ((), ())), preferred_element_type=F32)


def _sigmoid(v):
    return 1.0 / (1.0 + jnp.exp(-v))


def _rms_fwd(v):
    r = lax.rsqrt(jnp.mean(v * v, axis=-1, keepdims=True) + EPS)
    return v * r, r


def _rms_bwd(dh, vh, r, g):
    gd = dh * g
    return r * (gd - vh * jnp.mean(gd * vh, axis=-1, keepdims=True))


def _sublane_shifts(v):
    rows = v.shape[0]
    return [v] + [pltpu.roll(v, rows - b, 0) for b in range(1, 8)]


def _colsum(v):
    return jnp.sum(v, axis=0, keepdims=True)


def _full(shape):
    return pl.BlockSpec(shape, lambda *_: (0,) * len(shape))


def _params(sem=("arbitrary",), vmem=VMEM_LIMIT_V7X):
    return pltpu.CompilerParams(dimension_semantics=sem, vmem_limit_bytes=vmem)


def _load_weight(g_hbm, name, dst, sem):
    off, rows = W_OFF[name]
    return [pltpu.make_async_copy(g_hbm.at[d, pl.ds(off, rows), :], dst.at[pl.ds(d * rows, rows), :], sem)
            for d in range(N_DEV)]


def _position():
    x, y, c = lax.axis_index("x"), lax.axis_index("y"), lax.axis_index("c")
    chips = [(1 - x, y), (x, 1 - y), (1 - x, 1 - y)]
    return x, y, c, chips


def _dev(px, py, pc):
    return 4 * px + 2 * py + pc


def _all_gather(arrs, name):
    n = len(arrs)

    def body(*refs):
        ins, outs = refs[:n], refs[n:2 * n]
        send_sems, recv_sems, local_sems = refs[2 * n:2 * n + 3]
        bounce = refs[2 * n + 3:]
        x, y, c, chips = _position()
        me, sibling = (x, y, c), (x, y, 1 - c)

        def copy(a, k, block, to, src=None):
            rows = outs[a].at[_dev(*block)]
            return pltpu.make_async_remote_copy(
                src_ref=rows if src is None else src, dst_ref=rows,
                send_sem=send_sems.at[a, k], recv_sem=recv_sems.at[a, k], device_id=to, device_id_type=MESH)

        sends = []
        for a in range(n):
            first = [copy(a, 0, me, sibling, src=ins[a])]
            first += [copy(a, 1 + j, me, (*chip, c), src=ins[a]) for j, chip in enumerate(chips)]
            for cp in first:
                cp.start()
            sends += first
        started = []
        for a in range(n):
            load = pltpu.make_async_copy(ins[a], bounce[a], local_sems.at[a, 0])
            load.start()
            load.wait()
            mine = pltpu.make_async_copy(bounce[a], outs[a].at[_dev(*me)], local_sems.at[a, 1])
            mine.start()
            started.append(mine)
        for j, chip in enumerate(chips):
            for a in range(n):
                copy(a, 1 + j, (*chip, c), me).wait_recv()
                passed = copy(a, 4 + j, (*chip, c), sibling)
                passed.start()
                sends.append(passed)
        for a in range(n):
            copy(a, 0, sibling, me).wait_recv()
            for j, chip in enumerate(chips):
                copy(a, 4 + j, (*chip, 1 - c), me).wait_recv()
        for cp in sends:
            cp.wait_send()
        for mine in started:
            mine.wait()

    any_spec = pl.BlockSpec(memory_space=pl.ANY)
    return pl.pallas_call(
        body, name=name,
        out_shape=[jax.ShapeDtypeStruct((N_DEV,) + a.shape, a.dtype) for a in arrs],
        in_specs=[any_spec] * n, out_specs=[any_spec] * n,
        scratch_shapes=[pltpu.SemaphoreType.DMA((n, 7)), pltpu.SemaphoreType.DMA((n, 7)), pltpu.SemaphoreType.DMA((n, 2))]
        + [pltpu.VMEM(a.shape, a.dtype) for a in arrs],
    )(*arrs)


_HBM = pl.BlockSpec(memory_space=pltpu.HBM)
_SEM = pl.BlockSpec(memory_space=pltpu.SEMAPHORE)
_SIDE_EFFECT = pltpu.SideEffectType.DATAFLOW_SIDE_EFFECTING


def _gather_start(buf, after, name):
    def body(buf_ref, after_ref, send_sems, recv_sems, buf_thru, token):
        del after_ref, buf_thru
        x, y, c, chips = _position()
        rows = buf_ref.at[_dev(x, y, c)]
        for k, to in enumerate([(x, y, 1 - c)] + [(*chip, c) for chip in chips]):
            pltpu.make_async_remote_copy(src_ref=rows, dst_ref=rows, send_sem=send_sems.at[k], recv_sem=recv_sems.at[k],
                                         device_id=to, device_id_type=MESH).start()
        token[...] = jnp.zeros_like(token)

    return pl.pallas_call(
        body, name=name,
        out_shape=(pltpu.SemaphoreType.DMA((4,)), pltpu.SemaphoreType.DMA((4,)), pltpu.HBM(buf.shape, buf.dtype),
                   jax.ShapeDtypeStruct((8, 128), F32)),
        in_specs=(_HBM, pl.BlockSpec(memory_space=pl.ANY)),
        out_specs=(_SEM, _SEM, _HBM, pl.BlockSpec(memory_space=pltpu.VMEM)),
        input_output_aliases={0: 2},
        compiler_params=pltpu.CompilerParams(has_side_effects=_SIDE_EFFECT),
    )(pltpu.with_memory_space_constraint(buf, pltpu.HBM), after)


def _gather_forward(send_sems, recv_sems, buf, after, name):
    def body(buf_ref, send_sems, recv_sems, after_ref, fwd_send, fwd_recv, buf_thru):
        del after_ref, buf_thru
        x, y, c, chips = _position()
        sibling = (x, y, 1 - c)

        def copy(block, k, sends, recvs):
            rows = buf_ref.at[_dev(*block)]
            return pltpu.make_async_remote_copy(src_ref=rows, dst_ref=rows, send_sem=sends.at[k], recv_sem=recvs.at[k],
                                                device_id=sibling, device_id_type=MESH)

        for k in range(4):
            copy((x, y, c), k, send_sems, recv_sems).wait_send()
        copy(sibling, 0, send_sems, recv_sems).wait_recv()
        for j, chip in enumerate(chips):
            copy((*chip, c), 1 + j, send_sems, recv_sems).wait_recv()
            copy((*chip, c), j, fwd_send, fwd_recv).start()

    return pl.pallas_call(
        body, name=name,
        out_shape=(pltpu.SemaphoreType.DMA((3,)), pltpu.SemaphoreType.DMA((3,)), pltpu.HBM(buf.shape, buf.dtype)),
        in_specs=(_HBM, _SEM, _SEM, pl.BlockSpec(memory_space=pl.ANY)), out_specs=(_SEM, _SEM, _HBM),
        input_output_aliases={0: 2},
        compiler_params=pltpu.CompilerParams(has_side_effects=_SIDE_EFFECT),
    )(buf, send_sems, recv_sems, after)


def _gather_finish(fwd_send, fwd_recv, buf, name):
    def body(buf_ref, fwd_send, fwd_recv, buf_thru):
        del buf_thru
        x, y, c, chips = _position()
        for j, chip in enumerate(chips):
            cp = pltpu.make_async_remote_copy(
                src_ref=buf_ref.at[_dev(*chip, c)], dst_ref=buf_ref.at[_dev(*chip, 1 - c)], send_sem=fwd_send.at[j],
                recv_sem=fwd_recv.at[j], device_id=(x, y, 1 - c), device_id_type=MESH)
            cp.wait_send()
            cp.wait_recv()

    return pl.pallas_call(
        body, name=name,
        out_shape=pltpu.HBM(buf.shape, buf.dtype),
        in_specs=(_HBM, _SEM, _SEM), out_specs=_HBM,
        input_output_aliases={0: 0},
        compiler_params=pltpu.CompilerParams(has_side_effects=_SIDE_EFFECT),
    )(buf, fwd_send, fwd_recv)


def _everyone_else(x, y, c, chips):
    return [(x, y, 1 - c)] + [(*chip, core) for chip in chips for core in (c, 1 - c)]


def _broadcast_start(buf, name):
    def body(buf_ref, send_sems, recv_sems, buf_thru, token):
        del buf_thru
        x, y, c, chips = _position()
        rows = buf_ref.at[_dev(x, y, c)]
        for k, to in enumerate(_everyone_else(x, y, c, chips)):
            pltpu.make_async_remote_copy(src_ref=rows, dst_ref=rows, send_sem=send_sems.at[k], recv_sem=recv_sems.at[k],
                                         device_id=to, device_id_type=MESH).start()
        token[...] = jnp.zeros_like(token)

    return pl.pallas_call(
        body, name=name,
        out_shape=(pltpu.SemaphoreType.DMA((7,)), pltpu.SemaphoreType.DMA((7,)), pltpu.HBM(buf.shape, buf.dtype),
                   jax.ShapeDtypeStruct((8, 128), F32)),
        in_specs=(_HBM,), out_specs=(_SEM, _SEM, _HBM, pl.BlockSpec(memory_space=pltpu.VMEM)),
        input_output_aliases={0: 2},
        compiler_params=pltpu.CompilerParams(has_side_effects=_SIDE_EFFECT),
    )(pltpu.with_memory_space_constraint(buf, pltpu.HBM))


def _broadcast_wait(send_sems, recv_sems, buf, after, name):
    def body(buf_ref, send_sems, recv_sems, after_ref, buf_thru):
        del after_ref, buf_thru
        x, y, c, chips = _position()
        for k, peer in enumerate(_everyone_else(x, y, c, chips)):
            cp = pltpu.make_async_remote_copy(
                src_ref=buf_ref.at[_dev(x, y, c)], dst_ref=buf_ref.at[_dev(*peer)], send_sem=send_sems.at[k],
                recv_sem=recv_sems.at[k], device_id=peer, device_id_type=MESH)
            cp.wait_send()
            cp.wait_recv()

    return pl.pallas_call(
        body, name=name,
        out_shape=pltpu.HBM(buf.shape, buf.dtype),
        in_specs=(_HBM, _SEM, _SEM, pl.BlockSpec(memory_space=pl.ANY)), out_specs=_HBM,
        input_output_aliases={0: 0},
        compiler_params=pltpu.CompilerParams(has_side_effects=_SIDE_EFFECT),
    )(buf, send_sems, recv_sems, after)


def _to_sibling(j, x, y, c, chips):
    return _dev(*([(x, y)] + chips)[j], 1 - c), (x, y, 1 - c)


def _to_chip(j, x, y, c, chips):
    return j, (*chips[j], c)


def _exchange_start(srcs, n_slots, route, name):
    n = len(srcs)

    def body(*refs):
        s_refs, land_refs = refs[:n], refs[n:2 * n]
        send_sems, recv_sems = refs[2 * n:2 * n + 2]
        token = refs[-1]
        x, y, c, chips = _position()
        for k in range(n):
            for j in range(n_slots):
                block, to = route(j, x, y, c, chips)
                pltpu.make_async_remote_copy(
                    src_ref=s_refs[k].at[block], dst_ref=land_refs[k].at[j], send_sem=send_sems.at[n_slots * k + j],
                    recv_sem=recv_sems.at[n_slots * k + j], device_id=to, device_id_type=MESH).start()
        token[...] = jnp.zeros_like(token)

    lands = [jax.ShapeDtypeStruct((n_slots,) + s.shape[1:], s.dtype) for s in srcs]
    outs = pl.pallas_call(
        body, name=name,
        out_shape=(pltpu.SemaphoreType.DMA((n_slots * n,)), pltpu.SemaphoreType.DMA((n_slots * n,)),
                   *[pltpu.HBM(s.shape, s.dtype) for s in srcs], *[pltpu.HBM(l.shape, l.dtype) for l in lands],
                   jax.ShapeDtypeStruct((8, 128), F32)),
        in_specs=[_HBM] * (2 * n), out_specs=(_SEM, _SEM, *[_HBM] * (2 * n), pl.BlockSpec(memory_space=pltpu.VMEM)),
        input_output_aliases={k: 2 + k for k in range(2 * n)},
        compiler_params=pltpu.CompilerParams(has_side_effects=_SIDE_EFFECT),
    )(*[pltpu.with_memory_space_constraint(s, pltpu.HBM) for s in srcs],
      *[pltpu.with_memory_space_constraint(lax.empty(l.shape, l.dtype), pltpu.HBM) for l in lands])
    return outs[0], outs[1], outs[2:2 + n], outs[2 + n:2 + 2 * n], outs[-1]


def _exchange_wait(send_sems, recv_sems, s_thru, land_thru, after, n_slots, route, name):
    n = len(s_thru)

    def body(*refs):
        s_refs, land_refs = refs[:n], refs[n:2 * n]
        send_sems, recv_sems = refs[2 * n:2 * n + 2]
        x, y, c, chips = _position()
        for k in range(n):
            for j in range(n_slots):
                block, to = route(j, x, y, c, chips)
                cp = pltpu.make_async_remote_copy(
                    src_ref=s_refs[k].at[block], dst_ref=land_refs[k].at[j], send_sem=send_sems.at[n_slots * k + j],
                    recv_sem=recv_sems.at[n_slots * k + j], device_id=to, device_id_type=MESH)
                cp.wait_send()
                cp.wait_recv()

    outs = pl.pallas_call(
        body, name=name,
        out_shape=(*[pltpu.HBM(s.shape, s.dtype) for s in s_thru], *[pltpu.HBM(l.shape, l.dtype) for l in land_thru]),
        in_specs=[_HBM] * (2 * n) + [_SEM, _SEM, pl.BlockSpec(memory_space=pl.ANY)], out_specs=[_HBM] * (2 * n),
        input_output_aliases={k: k for k in range(2 * n)},
        compiler_params=pltpu.CompilerParams(has_side_effects=_SIDE_EFFECT),
    )(*s_thru, *land_thru, send_sems, recv_sems, after)
    return outs[:n], outs[n:]


def _owner_table():
    x, y, c = lax.axis_index("x"), lax.axis_index("y"), lax.axis_index("c")
    chips = [(x, y), (1 - x, y), (x, 1 - y), (1 - x, 1 - y)]
    return jnp.stack([_dev(px, py, c) for px, py in chips]).astype(jnp.int32)


def _chip_partial_sums(table, parts, from_sibling, name):
    n = len(parts)

    def body(tab_ref, *refs):
        del tab_ref
        for g_ref, l_ref, out_ref in zip(refs[:n], refs[n:2 * n], refs[2 * n:]):
            out_ref[...] = (g_ref[...].astype(F32) + l_ref[...].astype(F32)).astype(out_ref.dtype)

    block = lambda p: (None,) + p.shape[1:]
    grid_spec = pltpu.PrefetchScalarGridSpec(
        num_scalar_prefetch=1, grid=(3,),
        in_specs=[pl.BlockSpec(block(p), lambda j, tab: (tab[j + 1], 0, 0)) for p in parts]
        + [pl.BlockSpec(block(p), lambda j, tab: (j + 1, 0, 0)) for p in parts],
        out_specs=[pl.BlockSpec(block(p), lambda j, tab: (j, 0, 0)) for p in parts])
    return pl.pallas_call(
        body, name=name, grid_spec=grid_spec,
        out_shape=[jax.ShapeDtypeStruct((3,) + p.shape[1:], BF16) for p in parts],
        compiler_params=_params(("arbitrary",)),
    )(table, *parts, *from_sibling)


def _final_update(table, parts, from_sibling, from_chips, states, name):
    n = len(parts)
    updated = [k for k in range(n) if states[k] is not None]

    def body(tab_ref, *refs):
        del tab_ref
        ins, outs = refs[:3 * n + 3 * len(updated)], list(refs[3 * n + 3 * len(updated):])
        wmv = list(ins[3 * n:])
        for k in range(n):
            acc = ins[k][...].astype(F32) + ins[n + k][...].astype(F32)
            for j in range(3):
                acc = acc + ins[2 * n + k][j].astype(F32)
            outs.pop(0)[...] = acc
            if k in updated:
                w_ref, m_ref, v_ref = wmv[:3]
                del wmv[:3]
                for out_ref, val in zip(outs[:3], _adamw_update(w_ref[...], acc, m_ref[...], v_ref[...])):
                    out_ref[...] = val
                del outs[:3]

    half = lambda p: (p.shape[1] // 2, p.shape[2])
    rows = lambda p: pl.BlockSpec(half(p), lambda t, tab: (t, 0))
    grid_spec = pltpu.PrefetchScalarGridSpec(
        num_scalar_prefetch=1, grid=(2,),
        in_specs=[pl.BlockSpec((None,) + half(p), lambda t, tab: (tab[0], t, 0)) for p in parts]
        + [pl.BlockSpec((None,) + half(p), lambda t, tab: (0, t, 0)) for p in parts]
        + [pl.BlockSpec((3,) + half(p), lambda t, tab: (0, t, 0)) for p in parts]
        + [rows(parts[k]) for k in updated for _ in range(3)],
        out_specs=[rows(parts[k]) for k in range(n) for _ in range(4 if k in updated else 1)])
    outs = pl.pallas_call(
        body, name=name, grid_spec=grid_spec,
        out_shape=[jax.ShapeDtypeStruct(parts[k].shape[1:], F32) for k in range(n) for _ in range(4 if k in updated else 1)],
        compiler_params=_params(("arbitrary",)),
    )(table, *parts, *from_sibling, *from_chips, *[t for k in updated for t in states[k]])
    result = []
    for k in range(n):
        count = 4 if k in updated else 1
        result.append(outs[:count])
        outs = outs[count:]
    return result


def _sum_blocks(g8):
    _, rows, cols = g8.shape

    def body(g_ref, out_ref):
        acc = g_ref[0]
        for d in range(1, N_DEV):
            acc = acc + g_ref[d]
        out_ref[...] = acc

    return pl.pallas_call(
        body, name="small_grad_sum", grid=(1,),
        in_specs=[_full((N_DEV, rows, cols))], out_specs=_full((rows, cols)),
        out_shape=jax.ShapeDtypeStruct((rows, cols), F32),
        compiler_params=_params(("arbitrary",)),
    )(g8)


def _fwd_mix(x2d, gw, g_mix, conv_w, conv_b, ln_g, ln_b, pool_w, pool_scale, after, seq, tm):
    tokens = x2d.shape[0]
    n_tiles = tokens // tm
    tps = seq // tm

    def body(x_ref, gmix_ref, gw_hbm, cw_ref, cb_ref, lng_ref, lnb_ref, pw_ref, ps_ref, after_ref,
             x1_ref, u_ref, c_ref, pooled_ref, ymix_ref, h1_ref,
             win_v, wout_v, hc_carry, up_carry, sem):
        del after_ref
        i = pl.program_id(0)

        @pl.when(i == 0)
        def _():
            copies = _load_weight(gw_hbm, "w_in", win_v, sem) + _load_weight(gw_hbm, "w_out", wout_v, sem)
            for cp in copies:
                cp.start()
            for cp in copies:
                cp.wait()

        @pl.when(i % tps == 0)
        def _():
            hc_carry[...] = jnp.zeros_like(hc_carry)
            up_carry[...] = jnp.zeros_like(up_carry)

        x = x_ref[...]
        xh, _ = _rms_fwd(x)
        h1 = (xh * gmix_ref[...]).astype(BF16)
        h1_ref[...] = h1
        u = _dot_nt(h1, win_v[...])
        u_ref[...] = u
        val, gate, up = u[:, :D_CONV], u[:, D_CONV:2 * D_CONV], u[:, 2 * D_CONV:]

        hc = val * _sigmoid(gate)
        ext = jnp.concatenate([hc_carry[...], hc], axis=0)
        hc_carry[...] = hc[tm - CONV_HALO:, :]
        conv = jnp.broadcast_to(cb_ref[...], (tm, D_CONV))
        ahead_by = _sublane_shifts(ext)
        for k in range(CONV_WIDTH):
            whole, part = divmod(CONV_HALO - (CONV_WIDTH - 1) + k, 8)
            conv = conv + cw_ref[k:k + 1, :] * ahead_by[part][8 * whole:8 * whole + tm, :]
        c_ref[...] = conv
        mu = jnp.mean(conv, axis=-1, keepdims=True)
        cen = conv - mu
        ln = cen * lax.rsqrt(jnp.mean(cen * cen, axis=-1, keepdims=True) + EPS) * lng_ref[...] + lnb_ref[...]
        y_conv = ln * _sigmoid(ln)

        extp = jnp.concatenate([up_carry[...], up], axis=0)
        up_carry[...] = up[tm - POOL_HALO:, :]
        pos = lax.broadcasted_iota(jnp.int32, (tm, 1), 0) + (i % tps) * tm
        run = extp
        mixed = []
        for g, w in enumerate(POOL_WINDOWS):
            lo = g * POOL_GROUP_DIM
            run = run[:, POOL_GROUP_DIM if g else 0:]
            run = run + pltpu.roll(run, w // 2, 0)
            cnt = jnp.minimum(pos + 1, w).astype(F32)
            pooled = run[POOL_HALO:, :POOL_GROUP_DIM] / cnt - up[:, lo:lo + POOL_GROUP_DIM]
            pooled = pooled.astype(BF16)
            pooled_ref[:, lo:lo + POOL_GROUP_DIM] = pooled
            mixed.append(_dot(pooled, pw_ref[g].astype(BF16)))
        y_pool = jnp.concatenate(mixed, axis=-1) * ps_ref[...]

        ymix = jnp.concatenate([y_conv, y_pool], axis=-1).astype(BF16)
        ymix_ref[...] = ymix
        x1_ref[...] = x + _dot(ymix, wout_v[...])

    row = lambda w: pl.BlockSpec((tm, w), lambda i: (i, 0))
    return pl.pallas_call(
        body, name="fwd_mix", grid=(n_tiles,),
        in_specs=[row(D_MODEL), _full((1, D_MODEL)), pl.BlockSpec(memory_space=pl.ANY),
                  _full((CONV_WIDTH, D_CONV)), _full((1, D_CONV)), _full((1, D_CONV)), _full((1, D_CONV)),
                  _full((4, POOL_GROUP_DIM, POOL_GROUP_DIM)), _full((1, D_POOL)), _full(after.shape)],
        out_specs=[row(D_MODEL), row(D_IN), row(D_CONV), row(D_POOL), row(D_MODEL), row(D_MODEL)],
        out_shape=[jax.ShapeDtypeStruct((tokens, D_MODEL), F32), jax.ShapeDtypeStruct((tokens, D_IN), F32),
                   jax.ShapeDtypeStruct((tokens, D_CONV), F32), jax.ShapeDtypeStruct((tokens, D_POOL), BF16),
                   jax.ShapeDtypeStruct((tokens, D_MODEL), BF16), jax.ShapeDtypeStruct((tokens, D_MODEL), BF16)],
        scratch_shapes=[pltpu.VMEM((D_IN, D_MODEL), BF16), pltpu.VMEM((D_MODEL, D_MODEL), BF16),
                        pltpu.VMEM((CONV_HALO, D_CONV), F32), pltpu.VMEM((POOL_HALO, D_POOL), F32),
                        pltpu.SemaphoreType.DMA],
        compiler_params=_params(),
    )(x2d, g_mix, gw, conv_w, conv_b, ln_g, ln_b, pool_w, pool_scale, after)


def _fwd_kv(mem2d, gw, g_mem):
    rows = mem2d.shape[0]
    n_b = rows // N_MEM

    def body(mem_ref, g_ref, gw_hbm, mn_ref, kv_ref, wkv_v, sem):
        @pl.when(pl.program_id(0) == 0)
        def _():
            copies = _load_weight(gw_hbm, "w_kv", wkv_v, sem)
            for cp in copies:
                cp.start()
            for cp in copies:
                cp.wait()

        mh, _ = _rms_fwd(mem_ref[...])
        mn = (mh * g_ref[...]).astype(BF16)
        mn_ref[...] = mn
        kv_ref[...] = _dot_nt(mn, wkv_v[...]).astype(BF16)

    return pl.pallas_call(
        body, name="fwd_kv", grid=(n_b,),
        in_specs=[pl.BlockSpec((N_MEM, D_MODEL), lambda b: (b, 0)), _full((1, D_MODEL)), pl.BlockSpec(memory_space=pl.ANY)],
        out_specs=[pl.BlockSpec((N_MEM, D_MODEL), lambda b: (b, 0)), pl.BlockSpec((N_MEM, 2 * D_MODEL), lambda b: (b, 0))],
        out_shape=[jax.ShapeDtypeStruct((rows, D_MODEL), BF16), jax.ShapeDtypeStruct((rows, 2 * D_MODEL), BF16)],
        scratch_shapes=[pltpu.VMEM((2 * D_MODEL, D_MODEL), BF16), pltpu.SemaphoreType.DMA],
        compiler_params=_params(),
    )(mem2d, g_mem, gw)


def _softmax_rows(s):
    e = jnp.exp(s - jnp.max(s, axis=-1, keepdims=True))
    return e / jnp.sum(e, axis=-1, keepdims=True)


def _fwd_attn(x1, kv, gw, g_x, seq, tm):
    tokens = x1.shape[0]
    n_tiles = tokens // tm
    tps = seq // tm

    def body(x1_ref, kv_ref, g_ref, gw_hbm, x2_ref, h2_ref, q_ref, o_ref, wq_v, wo_v, sem):
        @pl.when(pl.program_id(0) == 0)
        def _():
            copies = _load_weight(gw_hbm, "w_q", wq_v, sem) + _load_weight(gw_hbm, "w_o", wo_v, sem)
            for cp in copies:
                cp.start()
            for cp in copies:
                cp.wait()

        x1v = x1_ref[...]
        xh, _ = _rms_fwd(x1v)
        h2 = (xh * g_ref[...]).astype(BF16)
        h2_ref[...] = h2
        q = (_dot(h2, wq_v[...]) * (HEAD_DIM ** -0.5)).astype(BF16)
        q_ref[...] = q
        outs = []
        for h in range(HEADS):
            lo = h * HEAD_DIM
            p = _softmax_rows(_dot_nt(q[:, lo:lo + HEAD_DIM], kv_ref[:, lo:lo + HEAD_DIM]))
            outs.append(_dot(p.astype(BF16), kv_ref[:, D_MODEL + lo:D_MODEL + lo + HEAD_DIM]))
        o = jnp.concatenate(outs, axis=-1).astype(BF16)
        o_ref[...] = o
        x2_ref[...] = x1v + _dot(o, wo_v[...])

    row = lambda w: pl.BlockSpec((tm, w), lambda i: (i, 0))
    return pl.pallas_call(
        body, name="fwd_attn", grid=(n_tiles,),
        in_specs=[row(D_MODEL), pl.BlockSpec((N_MEM, 2 * D_MODEL), lambda i: (i // tps, 0)), _full((1, D_MODEL)),
                  pl.BlockSpec(memory_space=pl.ANY)],
        out_specs=[row(D_MODEL)] * 4,
        out_shape=[jax.ShapeDtypeStruct((tokens, D_MODEL), F32)] + [jax.ShapeDtypeStruct((tokens, D_MODEL), BF16)] * 3,
        scratch_shapes=[pltpu.VMEM((D_MODEL, D_MODEL), BF16), pltpu.VMEM((D_MODEL, D_MODEL), BF16), pltpu.SemaphoreType.DMA],
        compiler_params=_params(),
    )(x1, kv, g_x, gw)


def _ffn_conv(uu, halo, w_ref, b_ref, cols):
    ext = jnp.concatenate([halo, uu], axis=0)
    p1 = pltpu.roll(ext, 1, 0)[FFN_HALO:, :]
    p2 = pltpu.roll(ext, 2, 0)[FFN_HALO:, :]
    return b_ref[:, cols] + w_ref[2:3, cols] * uu + w_ref[1:2, cols] * p1 + w_ref[0:1, cols] * p2


def _fwd_ffn(x2, target, gw, g_ffn, ffn_w, ffn_b, g_final, seq, tm):
    tokens = x2.shape[0]
    n_tiles = tokens // tm
    tps = seq // tm
    n_chunks = D_FF // FFN_CHUNK

    def body(x2_ref, tgt_ref, gffn_ref, gw_hbm, fw_ref, fb_ref, gfin_ref,
             uu_ref, cc_ref, a_ref, h3_ref, dx3_ref, dx3b_ref, loss_ref, dgfin_ref,
             wup_v, wdown_v, carry, sem):
        i = pl.program_id(0)

        @pl.when(i == 0)
        def _():
            copies = _load_weight(gw_hbm, "w_up", wup_v, sem) + _load_weight(gw_hbm, "w_down", wdown_v, sem)
            for cp in copies:
                cp.start()
            for cp in copies:
                cp.wait()
            loss_ref[...] = jnp.zeros_like(loss_ref)
            dgfin_ref[...] = jnp.zeros_like(dgfin_ref)

        @pl.when(i % tps == 0)
        def _():
            carry[...] = jnp.zeros_like(carry)

        x2v = x2_ref[...]
        xh, _ = _rms_fwd(x2v)
        h3 = (xh * gffn_ref[...]).astype(BF16)
        h3_ref[...] = h3
        acc = jnp.zeros((tm, D_MODEL), F32)
        for jc in range(n_chunks):
            halves = []
            for half in range(2):
                cols = pl.ds(half * D_FF + jc * FFN_CHUNK, FFN_CHUNK)
                uu = _dot_nt(h3, wup_v[cols, :])
                uu_ref[:, cols] = uu.astype(BF16)
                cc = _ffn_conv(uu, carry[:, cols], fw_ref, fb_ref, cols)
                cc_ref[:, cols] = cc.astype(BF16)
                halves.append(cc)
                carry[:, cols] = uu[tm - FFN_HALO:, :]
            gate, val = halves
            a = (gate * _sigmoid(gate) * val).astype(BF16)
            a_ref[:, pl.ds(jc * FFN_CHUNK, FFN_CHUNK)] = a
            acc = acc + _dot(a, wdown_v[pl.ds(jc * FFN_CHUNK, FFN_CHUNK), :])
        x3 = x2v + acc

        xh3, r3 = _rms_fwd(x3)
        gfin = gfin_ref[...]
        err = xh3 * gfin - tgt_ref[...]
        loss_ref[...] += jnp.full(loss_ref.shape, jnp.sum(err * err) * (0.5 / D_MODEL), F32)
        dy = err * (1.0 / D_MODEL)
        dgfin_ref[...] += _colsum(dy * xh3)
        dx3 = _rms_bwd(dy, xh3, r3, gfin)
        dx3_ref[...] = dx3
        dx3b_ref[...] = dx3.astype(BF16)

    row = lambda w: pl.BlockSpec((tm, w), lambda i: (i, 0))
    return pl.pallas_call(
        body, name="fwd_ffn", grid=(n_tiles,),
        in_specs=[row(D_MODEL), row(D_MODEL), _full((1, D_MODEL)), pl.BlockSpec(memory_space=pl.ANY),
                  _full((FFN_CONV_WIDTH, 2 * D_FF)), _full((1, 2 * D_FF)), _full((1, D_MODEL))],
        out_specs=[row(2 * D_FF), row(2 * D_FF), row(D_FF), row(D_MODEL), row(D_MODEL), row(D_MODEL), _full((8, 128)),
                   _full((1, D_MODEL))],
        out_shape=[jax.ShapeDtypeStruct((tokens, 2 * D_FF), BF16), jax.ShapeDtypeStruct((tokens, 2 * D_FF), BF16),
                   jax.ShapeDtypeStruct((tokens, D_FF), BF16),
                   jax.ShapeDtypeStruct((tokens, D_MODEL), BF16), jax.ShapeDtypeStruct((tokens, D_MODEL), F32),
                   jax.ShapeDtypeStruct((tokens, D_MODEL), BF16),
                   jax.ShapeDtypeStruct((8, 128), F32), jax.ShapeDtypeStruct((1, D_MODEL), F32)],
        scratch_shapes=[pltpu.VMEM((2 * D_FF, D_MODEL), BF16), pltpu.VMEM((D_FF, D_MODEL), BF16),
                        pltpu.VMEM((FFN_HALO, 2 * D_FF), F32), pltpu.SemaphoreType.DMA],
        compiler_params=_params(),
    )(x2, target, g_ffn, gw, ffn_w, ffn_b, g_final)


def _bwd_ffn(dx3, x2, uu_all, cc_all, gw, g_ffn, ffn_w, seq, tm):
    tokens = x2.shape[0]
    n_tiles = tokens // tm
    tps = seq // tm
    n_chunks = D_FF // FFN_CHUNK

    def body(dx3_ref, x2_ref, uu_ref, cc_ref, gffn_ref, gw_hbm, fw_ref,
             dx2_ref, dx2b_ref, duu_ref, dfb_ref, dfw_ref, dg_ref,
             wup_v, wdown_v, carry, sem):
        i = pl.program_id(0)
        t = n_tiles - 1 - i

        @pl.when(i == 0)
        def _():
            copies = _load_weight(gw_hbm, "w_up", wup_v, sem) + _load_weight(gw_hbm, "w_down", wdown_v, sem)
            for cp in copies:
                cp.start()
            for cp in copies:
                cp.wait()
            dfb_ref[...] = jnp.zeros_like(dfb_ref)
            dfw_ref[...] = jnp.zeros_like(dfw_ref)
            dg_ref[...] = jnp.zeros_like(dg_ref)

        @pl.when(t % tps == tps - 1)
        def _():
            carry[...] = jnp.zeros_like(carry)

        dx3v = dx3_ref[...]
        dx3b = dx3v.astype(BF16)
        dh3 = jnp.zeros((tm, D_MODEL), F32)
        for jc in range(n_chunks):
            da = _dot_nt(dx3b, wdown_v[pl.ds(jc * FFN_CHUNK, FFN_CHUNK), :])
            colss = [pl.ds(half * D_FF + jc * FFN_CHUNK, FFN_CHUNK) for half in range(2)]
            gate, val = [cc_ref[:, cols].astype(F32) for cols in colss]
            sg = _sigmoid(gate)
            dgate = da * val * (sg * (1.0 + gate * (1.0 - sg)))
            dval = da * (gate * sg)
            for dcc, cols in zip((dgate, dval), colss):
                uu = uu_ref[:, cols].astype(F32)
                dfb_ref[:, cols] += _colsum(dcc)
                ext = jnp.concatenate([dcc, carry[:, cols]], axis=0)
                carry[:, cols] = dcc[:FFN_HALO, :]
                n1 = pltpu.roll(ext, tm + FFN_HALO - 1, 0)[:tm, :]
                n2 = pltpu.roll(ext, tm + FFN_HALO - 2, 0)[:tm, :]
                duu = fw_ref[2:3, cols] * dcc + fw_ref[1:2, cols] * n1 + fw_ref[0:1, cols] * n2
                dfw_ref[2:3, cols] += _colsum(uu * dcc)
                dfw_ref[1:2, cols] += _colsum(uu * n1)
                dfw_ref[0:1, cols] += _colsum(uu * n2)
                duub = duu.astype(BF16)
                duu_ref[:, cols] = duub
                dh3 = dh3 + _dot(duub, wup_v[cols, :])
        xh, r = _rms_fwd(x2_ref[...])
        dg_ref[...] += _colsum(dh3 * xh)
        dx2 = dx3v + _rms_bwd(dh3, xh, r, gffn_ref[...])
        dx2_ref[...] = dx2
        dx2b_ref[...] = dx2.astype(BF16)

    rev = lambda w: pl.BlockSpec((tm, w), lambda i: (n_tiles - 1 - i, 0))
    return pl.pallas_call(
        body, name="bwd_ffn", grid=(n_tiles,),
        in_specs=[rev(D_MODEL), rev(D_MODEL), rev(2 * D_FF), rev(2 * D_FF), _full((1, D_MODEL)),
                  pl.BlockSpec(memory_space=pl.ANY), _full((FFN_CONV_WIDTH, 2 * D_FF))],
        out_specs=[rev(D_MODEL), rev(D_MODEL), rev(2 * D_FF), _full((1, 2 * D_FF)), _full((FFN_CONV_WIDTH, 2 * D_FF)),
                   _full((1, D_MODEL))],
        out_shape=[jax.ShapeDtypeStruct((tokens, D_MODEL), F32), jax.ShapeDtypeStruct((tokens, D_MODEL), BF16),
                   jax.ShapeDtypeStruct((tokens, 2 * D_FF), BF16),
                   jax.ShapeDtypeStruct((1, 2 * D_FF), F32), jax.ShapeDtypeStruct((FFN_CONV_WIDTH, 2 * D_FF), F32),
                   jax.ShapeDtypeStruct((1, D_MODEL), F32)],
        scratch_shapes=[pltpu.VMEM((2 * D_FF, D_MODEL), BF16), pltpu.VMEM((D_FF, D_MODEL), BF16),
                        pltpu.VMEM((FFN_HALO, 2 * D_FF), F32), pltpu.SemaphoreType.DMA],
        compiler_params=_params(),
    )(dx3, x2, uu_all, cc_all, g_ffn, gw, ffn_w)


def _bwd_attn(dx2, x1, q, kv, gw, g_x, after, seq, tm):
    tokens = x1.shape[0]
    n_tiles = tokens // tm
    tps = seq // tm
    n_b = tokens // seq

    def body(dx2_ref, x1_ref, q_ref, kv_ref, g_ref, gw_hbm, after_ref, dx1_ref, dx1b_ref, dq_ref, dkv_ref, dg_ref,
             wq_v, wo_v, sem):
        del after_ref
        i = pl.program_id(0)

        @pl.when(i == 0)
        def _():
            copies = _load_weight(gw_hbm, "w_q", wq_v, sem) + _load_weight(gw_hbm, "w_o", wo_v, sem)
            for cp in copies:
                cp.start()
            for cp in copies:
                cp.wait()
            dg_ref[...] = jnp.zeros_like(dg_ref)

        @pl.when(i % tps == 0)
        def _():
            dkv_ref[...] = jnp.zeros_like(dkv_ref)

        dx2v = dx2_ref[...]
        do = _dot_nt(dx2v.astype(BF16), wo_v[...]).astype(BF16)
        q = q_ref[...]
        dqs = []
        for h in range(HEADS):
            lo = h * HEAD_DIM
            kcols, vcols = pl.ds(lo, HEAD_DIM), pl.ds(D_MODEL + lo, HEAD_DIM)
            qh, doh = q[:, lo:lo + HEAD_DIM], do[:, lo:lo + HEAD_DIM]
            p = _softmax_rows(_dot_nt(qh, kv_ref[:, kcols]))
            dp = _dot_nt(doh, kv_ref[:, vcols])
            dkv_ref[:, vcols] += _dot_tn(p.astype(BF16), doh)
            ds = (p * (dp - jnp.sum(dp * p, axis=-1, keepdims=True))).astype(BF16)
            dqs.append(_dot(ds, kv_ref[:, kcols]) * (HEAD_DIM ** -0.5))
            dkv_ref[:, kcols] += _dot_tn(ds, qh)
        dq = jnp.concatenate(dqs, axis=-1).astype(BF16)
        dq_ref[...] = dq
        dh2 = _dot_nt(dq, wq_v[...])
        xh, r = _rms_fwd(x1_ref[...])
        dg_ref[...] += _colsum(dh2 * xh)
        dx1 = dx2v + _rms_bwd(dh2, xh, r, g_ref[...])
        dx1_ref[...] = dx1
        dx1b_ref[...] = dx1.astype(BF16)

    row = lambda w: pl.BlockSpec((tm, w), lambda i: (i, 0))
    per_b = pl.BlockSpec((N_MEM, 2 * D_MODEL), lambda i: (i // tps, 0))
    return pl.pallas_call(
        body, name="bwd_attn", grid=(n_tiles,),
        in_specs=[row(D_MODEL), row(D_MODEL), row(D_MODEL), per_b, _full((1, D_MODEL)), pl.BlockSpec(memory_space=pl.ANY),
                  _full(after.shape)],
        out_specs=[row(D_MODEL), row(D_MODEL), row(D_MODEL), per_b, _full((1, D_MODEL))],
        out_shape=[jax.ShapeDtypeStruct((tokens, D_MODEL), F32), jax.ShapeDtypeStruct((tokens, D_MODEL), BF16),
                   jax.ShapeDtypeStruct((tokens, D_MODEL), BF16),
                   jax.ShapeDtypeStruct((n_b * N_MEM, 2 * D_MODEL), F32), jax.ShapeDtypeStruct((1, D_MODEL), F32)],
        scratch_shapes=[pltpu.VMEM((D_MODEL, D_MODEL), BF16), pltpu.VMEM((D_MODEL, D_MODEL), BF16), pltpu.SemaphoreType.DMA],
        compiler_params=_params(),
    )(dx2, x1, q, kv, g_x, gw, after)


def _bwd_kv(dkv, mem2d, gw, g_mem):
    rows = mem2d.shape[0]
    n_b = rows // N_MEM

    def body(dkv_ref, mem_ref, gw_hbm, dkvb_ref, dg_ref, wkv_v, sem):
        @pl.when(pl.program_id(0) == 0)
        def _():
            copies = _load_weight(gw_hbm, "w_kv", wkv_v, sem)
            for cp in copies:
                cp.start()
            for cp in copies:
                cp.wait()
            dg_ref[...] = jnp.zeros_like(dg_ref)

        dkvb = dkv_ref[...].astype(BF16)
        dkvb_ref[...] = dkvb
        dmn = _dot(dkvb, wkv_v[...])
        mh, _ = _rms_fwd(mem_ref[...])
        dg_ref[...] += _colsum(dmn * mh)

    del g_mem
    return pl.pallas_call(
        body, name="bwd_kv", grid=(n_b,),
        in_specs=[pl.BlockSpec((N_MEM, 2 * D_MODEL), lambda b: (b, 0)), pl.BlockSpec((N_MEM, D_MODEL), lambda b: (b, 0)),
                  pl.BlockSpec(memory_space=pl.ANY)],
        out_specs=[pl.BlockSpec((N_MEM, 2 * D_MODEL), lambda b: (b, 0)), _full((1, D_MODEL))],
        out_shape=[jax.ShapeDtypeStruct((rows, 2 * D_MODEL), BF16), jax.ShapeDtypeStruct((1, D_MODEL), F32)],
        scratch_shapes=[pltpu.VMEM((2 * D_MODEL, D_MODEL), BF16), pltpu.SemaphoreType.DMA],
        compiler_params=_params(),
    )(dkv, mem2d, gw)


def _bwd_mix(dx1, x2d, u_all, c_all, pooled_all, gw, g_mix, conv_w, ln_g, ln_b, pool_w, pool_scale, after, seq, tm):
    tokens = x2d.shape[0]
    n_tiles = tokens // tm
    tps = seq // tm

    def body(dx1_ref, x_ref, u_ref, c_ref, pooled_ref, gmix_ref, gw_hbm, cw_ref, lng_ref, lnb_ref, pw_ref, ps_ref,
             after_ref, dx_ref, du_ref, dgmix_ref, dcw_ref, dcb_ref, dlng_ref, dlnb_ref, dpw_ref, dps_ref,
             win_v, wout_v, dc_carry, e_carry, sem):
        del after_ref
        i = pl.program_id(0)
        t = n_tiles - 1 - i

        @pl.when(i == 0)
        def _():
            copies = _load_weight(gw_hbm, "w_in", win_v, sem) + _load_weight(gw_hbm, "w_out", wout_v, sem)
            for cp in copies:
                cp.start()
            for cp in copies:
                cp.wait()
            for ref in (dgmix_ref, dcw_ref, dcb_ref, dlng_ref, dlnb_ref, dpw_ref, dps_ref):
                ref[...] = jnp.zeros_like(ref)

        @pl.when(t % tps == tps - 1)
        def _():
            dc_carry[...] = jnp.zeros_like(dc_carry)
            e_carry[...] = jnp.zeros_like(e_carry)

        dx1v = dx1_ref[...]
        dymix = _dot_nt(dx1v.astype(BF16), wout_v[...])
        dyc, dyp = dymix[:, :D_CONV], dymix[:, D_CONV:]
        u = u_ref[...]
        val, gate = u[:, :D_CONV], u[:, D_CONV:2 * D_CONV]

        conv = c_ref[...]
        mu = jnp.mean(conv, axis=-1, keepdims=True)
        cen = conv - mu
        rs = lax.rsqrt(jnp.mean(cen * cen, axis=-1, keepdims=True) + EPS)
        chat = cen * rs
        ln = chat * lng_ref[...] + lnb_ref[...]
        sl = _sigmoid(ln)
        dln = dyc * (sl * (1.0 + ln * (1.0 - sl)))
        dlng_ref[...] += _colsum(dln * chat)
        dlnb_ref[...] += _colsum(dln)
        dchat = dln * lng_ref[...]
        dc = rs * (dchat - jnp.mean(dchat, axis=-1, keepdims=True)
                   - chat * jnp.mean(dchat * chat, axis=-1, keepdims=True))
        dcb_ref[...] += _colsum(dc)
        sg = _sigmoid(gate)
        hc = val * sg
        ext = jnp.concatenate([dc, dc_carry[...]], axis=0)
        dc_carry[...] = dc[:CONV_HALO, :]
        dhc = jnp.zeros((tm, D_CONV), F32)
        ahead_by = _sublane_shifts(ext)
        for k in range(CONV_WIDTH):
            whole, part = divmod(CONV_WIDTH - 1 - k, 8)
            tap = ahead_by[part][8 * whole:8 * whole + tm, :]
            dhc = dhc + cw_ref[k:k + 1, :] * tap
            dcw_ref[k:k + 1, :] += _colsum(hc * tap)
        du_ref[:, :D_CONV] = (dhc * sg).astype(BF16)
        du_ref[:, D_CONV:2 * D_CONV] = (dhc * val * (sg * (1.0 - sg))).astype(BF16)

        pos = lax.broadcasted_iota(jnp.int32, (tm, 1), 0) + (t % tps) * tm
        es, dpooled = [], []
        for g, w in enumerate(POOL_WINDOWS):
            cols = pl.ds(g * POOL_GROUP_DIM, POOL_GROUP_DIM)
            lo = g * POOL_GROUP_DIM
            pooled = pooled_ref[:, cols]
            pw = pw_ref[g].astype(BF16)
            dyg = dyp[:, lo:lo + POOL_GROUP_DIM]
            dps_ref[:, cols] += _colsum(dyg * _dot(pooled, pw))
            dmixed = (dyg * ps_ref[:, cols]).astype(BF16)
            dpw_ref[g] += _dot_tn(pooled, dmixed)
            dpo = _dot_nt(dmixed, pw)
            dpooled.append(dpo)
            es.append(dpo / jnp.minimum(pos + 1, w).astype(F32))
        e = jnp.concatenate(es, axis=-1)
        run = jnp.concatenate([e, e_carry[...]], axis=0)
        e_carry[...] = e[:POOL_HALO, :]
        rows = tm + POOL_HALO
        for g, w in enumerate(POOL_WINDOWS):
            lo = g * POOL_GROUP_DIM
            run = run[:, POOL_GROUP_DIM if g else 0:]
            run = run + pltpu.roll(run, rows - w // 2, 0)
            du_ref[:, 2 * D_CONV + lo:2 * D_CONV + lo + POOL_GROUP_DIM] = (
                run[:tm, :POOL_GROUP_DIM] - dpooled[g]).astype(BF16)

        dh1 = _dot(du_ref[...], win_v[...])
        xh, r = _rms_fwd(x_ref[...])
        dgmix_ref[...] += _colsum(dh1 * xh)
        dx_ref[...] = dx1v + _rms_bwd(dh1, xh, r, gmix_ref[...])

    rev = lambda w: pl.BlockSpec((tm, w), lambda i: (n_tiles - 1 - i, 0))
    return pl.pallas_call(
        body, name="bwd_mix", grid=(n_tiles,),
        in_specs=[rev(D_MODEL), rev(D_MODEL), rev(D_IN), rev(D_CONV), rev(D_POOL), _full((1, D_MODEL)),
                  pl.BlockSpec(memory_space=pl.ANY), _full((CONV_WIDTH, D_CONV)), _full((1, D_CONV)), _full((1, D_CONV)),
                  _full((4, POOL_GROUP_DIM, POOL_GROUP_DIM)), _full((1, D_POOL)), _full(after.shape)],
        out_specs=[rev(D_MODEL), rev(D_IN), _full((1, D_MODEL)), _full((CONV_WIDTH, D_CONV)), _full((1, D_CONV)),
                   _full((1, D_CONV)), _full((1, D_CONV)), _full((4, POOL_GROUP_DIM, POOL_GROUP_DIM)), _full((1, D_POOL))],
        out_shape=[jax.ShapeDtypeStruct((tokens, D_MODEL), F32), jax.ShapeDtypeStruct((tokens, D_IN), BF16),
                   jax.ShapeDtypeStruct((1, D_MODEL), F32), jax.ShapeDtypeStruct((CONV_WIDTH, D_CONV), F32),
                   jax.ShapeDtypeStruct((1, D_CONV), F32), jax.ShapeDtypeStruct((1, D_CONV), F32),
                   jax.ShapeDtypeStruct((1, D_CONV), F32),
                   jax.ShapeDtypeStruct((4, POOL_GROUP_DIM, POOL_GROUP_DIM), F32), jax.ShapeDtypeStruct((1, D_POOL), F32)],
        scratch_shapes=[pltpu.VMEM((D_IN, D_MODEL), BF16), pltpu.VMEM((D_MODEL, D_MODEL), BF16),
                        pltpu.VMEM((CONV_HALO, D_CONV), F32), pltpu.VMEM((POOL_HALO, D_POOL), F32),
                        pltpu.SemaphoreType.DMA],
        compiler_params=_params(),
    )(dx1, x2d, u_all, c_all, pooled_all, g_mix, gw, conv_w, ln_g, ln_b, pool_w, pool_scale, after)


def _wgrad(a, b, name, after=None):
    tokens, m = a.shape
    n = b.shape[1]
    tm = 512 if m % 512 == 0 else 256
    extra = [] if after is None else [after]

    def body(a_ref, b_ref, *rest):
        rest[-1][...] = _dot_tn(a_ref[...], b_ref[...]).astype(rest[-1].dtype)

    return pl.pallas_call(
        body, name=name, grid=(m // tm,),
        in_specs=[pl.BlockSpec((tokens, tm), lambda i: (0, i)), _full((tokens, n))] + [_full(t.shape) for t in extra],
        out_specs=pl.BlockSpec((tm, n), lambda i: (i, 0)),
        out_shape=jax.ShapeDtypeStruct((m, n), BF16),
        compiler_params=_params(),
    )(a, b, *extra)


def _adamw_update(w, g, m, v):
    nm = ADAM_B1 * m + (1.0 - ADAM_B1) * g
    nv = ADAM_B2 * v + (1.0 - ADAM_B2) * (g * g)
    m_hat = nm / (1.0 - ADAM_B1 ** ADAM_STEP)
    v_hat = nv / (1.0 - ADAM_B2 ** ADAM_STEP)
    return -ADAM_LR * (m_hat / (jnp.sqrt(v_hat) + ADAM_EPS) + ADAM_WD * w), nm, nv


def _adamw_small(ws, gs, ms, vs):
    n = len(ws)

    def body(*refs):
        ins, outs = refs[:4 * n], refs[4 * n:]
        for k in range(n):
            d, nm, nv = _adamw_update(*[ins[j * n + k][...] for j in range(4)])
            outs[k][...] = d
            outs[n + k][...] = nm
            outs[2 * n + k][...] = nv

    vmem = pl.BlockSpec(memory_space=pltpu.VMEM)
    outs = pl.pallas_call(
        body, name="adamw_small",
        in_specs=[vmem] * (4 * n), out_specs=[vmem] * (3 * n),
        out_shape=[jax.ShapeDtypeStruct(w.shape, F32) for w in ws] * 3,
    )(*ws, *gs, *ms, *vs)
    return outs[:n], outs[n:2 * n], outs[2 * n:]


def _adamw(w, g, m, v, name):
    rows, cols = w.shape
    tile = rows
    for cand in (512, 256, 128, 64, 32, 16, 8):
        if rows % cand == 0:
            tile = cand
            break

    def body(w_ref, g_ref, m_ref, v_ref, d_ref, nm_ref, nv_ref):
        d_ref[...], nm_ref[...], nv_ref[...] = _adamw_update(w_ref[...], g_ref[...], m_ref[...], v_ref[...])

    spec = pl.BlockSpec((tile, cols), lambda i: (i, 0))
    return pl.pallas_call(
        body, name=name, grid=(rows // tile,),
        in_specs=[spec] * 4, out_specs=[spec] * 3,
        out_shape=[jax.ShapeDtypeStruct((rows, cols), F32)] * 3,
        compiler_params=_params(("arbitrary",)),
    )(w, g, m, v)


SMALL = (("norm_mix_g", (1, 1024)), ("conv_dw_b", (1, 512)), ("conv_ln_g", (1, 512)), ("conv_ln_b", (1, 512)),
         ("pool_w", (1, 4, 128, 128)), ("pool_scale", (1, 512)), ("norm_xattn_g", (1, 1024)), ("norm_mem_g", (1, 1024)),
         ("norm_ffn_g", (1, 1024)), ("ffn_dw_b", (1, 5632)), ("norm_final_g", (1024,)))
LANES = 128


def _pack_rows(arrs):
    flat = jnp.concatenate([a.reshape(-1) for a in arrs])
    pad = (-flat.shape[0]) % (8 * LANES)
    return jnp.pad(flat, (0, pad)).reshape(-1, LANES)


def kernel(x, mem, norm_mix_g, w_in, conv_dw_w, conv_dw_b, conv_ln_g, conv_ln_b, pool_w, pool_scale, w_out, norm_xattn_g, norm_mem_g, w_q, w_kv, w_o, norm_ffn_g, w_up, ffn_dw_w, ffn_dw_b, w_down, norm_final_g, loss_target, m_norm_mix_g, m_w_in, m_conv_dw_w, m_conv_dw_b, m_conv_ln_g, m_conv_ln_b, m_pool_w, m_pool_scale, m_w_out, m_norm_xattn_g, m_norm_mem_g, m_w_q, m_w_kv, m_w_o, m_norm_ffn_g, m_w_up, m_ffn_dw_w, m_ffn_dw_b, m_w_down, m_norm_final_g, v_norm_mix_g, v_w_in, v_conv_dw_w, v_conv_dw_b, v_conv_ln_g, v_conv_ln_b, v_pool_w, v_pool_scale, v_w_out, v_norm_xattn_g, v_norm_mem_g, v_w_q, v_w_kv, v_w_o, v_norm_ffn_g, v_w_up, v_ffn_dw_w, v_ffn_dw_b, v_w_down, v_norm_final_g):
    weights = dict(norm_mix_g=norm_mix_g, w_in=w_in, conv_dw_w=conv_dw_w, conv_dw_b=conv_dw_b, conv_ln_g=conv_ln_g,
                   conv_ln_b=conv_ln_b, pool_w=pool_w, pool_scale=pool_scale, w_out=w_out, norm_xattn_g=norm_xattn_g,
                   norm_mem_g=norm_mem_g, w_q=w_q, w_kv=w_kv, w_o=w_o, norm_ffn_g=norm_ffn_g, w_up=w_up,
                   ffn_dw_w=ffn_dw_w, ffn_dw_b=ffn_dw_b, w_down=w_down, norm_final_g=norm_final_g)
    moments_m = dict(norm_mix_g=m_norm_mix_g, w_in=m_w_in, conv_dw_w=m_conv_dw_w, conv_dw_b=m_conv_dw_b,
                     conv_ln_g=m_conv_ln_g, conv_ln_b=m_conv_ln_b, pool_w=m_pool_w, pool_scale=m_pool_scale,
                     w_out=m_w_out, norm_xattn_g=m_norm_xattn_g, norm_mem_g=m_norm_mem_g, w_q=m_w_q, w_kv=m_w_kv,
                     w_o=m_w_o, norm_ffn_g=m_norm_ffn_g, w_up=m_w_up, ffn_dw_w=m_ffn_dw_w, ffn_dw_b=m_ffn_dw_b,
                     w_down=m_w_down, norm_final_g=m_norm_final_g)
    moments_v = dict(norm_mix_g=v_norm_mix_g, w_in=v_w_in, conv_dw_w=v_conv_dw_w, conv_dw_b=v_conv_dw_b,
                     conv_ln_g=v_conv_ln_g, conv_ln_b=v_conv_ln_b, pool_w=v_pool_w, pool_scale=v_pool_scale,
                     w_out=v_w_out, norm_xattn_g=v_norm_xattn_g, norm_mem_g=v_norm_mem_g, w_q=v_w_q, w_kv=v_w_kv,
                     w_o=v_w_o, norm_ffn_g=v_norm_ffn_g, w_up=v_w_up, ffn_dw_w=v_ffn_dw_w, ffn_dw_b=v_ffn_dw_b,
                     w_down=v_w_down, norm_final_g=v_norm_final_g)
    order = list(weights)
    transposed = ("w_in", "w_kv", "w_up")

    n_b, seq, _ = x.shape
    tokens = n_b * seq
    tm_mix = min(512, seq // 2)
    tm_ffn = min(256, seq // 2)
    dev = 4 * lax.axis_index("x") + 2 * lax.axis_index("y") + lax.axis_index("c")

    packs = [jnp.concatenate([weights[n][0].T if n in transposed else weights[n][0] for n in names], axis=0).astype(BF16)
             for names in AG_GROUPS]
    small_sharded = _pack_rows([conv_dw_w[0], ffn_dw_w[0]])
    gw_mix, gsmall = _all_gather([packs[0], small_sharded], "weights_all_gather")
    flights = []
    after = gw_mix
    for k in (1, 2):
        own_in_place = lax.dynamic_update_slice(lax.empty((N_DEV,) + packs[k].shape, BF16), packs[k][None], (dev, 0, 0))
        flights.append(_gather_start(own_in_place, after, "weights_gather_start_%d" % k))
        after = flights[-1][3]
    gflat = gsmall.reshape(N_DEV, -1)
    n_cw = CONV_WIDTH * (D_CONV // N_DEV)
    n_fw = FFN_CONV_WIDTH * (2 * D_FF // N_DEV)
    conv_w = gflat[:, :n_cw].reshape(N_DEV, CONV_WIDTH, D_CONV // N_DEV).transpose(1, 0, 2).reshape(CONV_WIDTH, D_CONV)
    ffn_w = gflat[:, n_cw:n_cw + n_fw].reshape(N_DEV, FFN_CONV_WIDTH, 2 * D_FF // N_DEV).transpose(1, 0, 2).reshape(
        FFN_CONV_WIDTH, 2 * D_FF)

    x2d = x.reshape(tokens, D_MODEL)
    mem2d = mem.reshape(n_b * N_MEM, D_MODEL)
    tgt2d = loss_target.reshape(tokens, D_MODEL)
    g_final = norm_final_g.reshape(1, D_MODEL)

    def gather_finish(flight, after, tag):
        fwd_send, fwd_recv, buf = _gather_forward(*flight[:3], after, "weights_gather_forward_" + tag)
        return _gather_finish(fwd_send, fwd_recv, buf, "weights_gather_finish_" + tag)

    x1, u_all, c_all, pooled_all, ymix, h1 = _fwd_mix(
        x2d, gw_mix, norm_mix_g, conv_w, conv_dw_b, conv_ln_g, conv_ln_b, pool_w[0], pool_scale, flights[1][3],
        seq, tm_mix)
    gw_attn = gather_finish(flights[0], x1, "1")
    mem_n, kv = _fwd_kv(mem2d, gw_attn, norm_mem_g)
    x2, h2, q, o = _fwd_attn(x1, kv, gw_attn, norm_xattn_g, seq, tm_mix)
    gw_ffn = gather_finish(flights[1], x2, "2")
    uu_all, cc_all, a_all, h3, dx3, dx3b, loss_part, dg_final = _fwd_ffn(
        x2, tgt2d, gw_ffn, norm_ffn_g, ffn_w, ffn_dw_b, g_final, seq, tm_ffn)

    table = _owner_table()

    def sibling_start(names, tag):
        parts = [part[n].reshape(N_DEV, W_OFF[n][1], D_MODEL) for n in names]
        return _exchange_start(parts, 4, _to_sibling, "rs_sibling_exchange_start_" + tag)

    def chips_start(flight, after, tag):
        parts, landed = _exchange_wait(*flight[:4], after, 4, _to_sibling, "rs_sibling_exchange_wait_" + tag)
        sums = _chip_partial_sums(table, parts, landed, "rs_chip_partial_sums_" + tag)
        return parts, landed, _exchange_start(sums, 3, _to_chip, "rs_chip_exchange_start_" + tag)

    grads, delta, new_m, new_v = {}, {}, {}, {}

    def reduce_finish(names, parts, landed, flight, after, tag):
        _, from_chips = _exchange_wait(*flight[:4], after, 3, _to_chip, "rs_chip_exchange_wait_" + tag)
        as_rows = {n: n not in transposed or W_OFF[n][1] % LANES != 0 for n in names}
        states = [tuple(t[n][0].T if n in transposed else t[n][0] for t in (weights, moments_m, moments_v))
                  if as_rows[n] else None for n in names]
        results = _final_update(table, parts, landed, from_chips, states, "rs_final_update_" + tag)
        for n, res in zip(names, results):
            back = (lambda t: t.T[None]) if n in transposed else (lambda t: t[None])
            grads[n] = back(res[0])
            if as_rows[n]:
                delta[n], new_m[n], new_v[n] = [back(t) for t in res[1:]]
            else:
                delta[n], new_m[n], new_v[n] = [t[None] for t in _adamw(
                    weights[n][0], grads[n][0], moments_m[n][0], moments_v[n][0], "adamw_" + n)]
        alone = [n for n in names if not as_rows[n]]
        return delta[alone[-1] if alone else names[-1]]

    part = {}
    dx2, dx2b, duu, d_ffn_b, d_ffn_w, dg_ffn = _bwd_ffn(dx3, x2, uu_all, cc_all, gw_ffn, norm_ffn_g, ffn_w, seq, tm_ffn)
    part["w_up"] = _wgrad(duu, h3, "wgrad_w_up")
    part["w_down"] = _wgrad(a_all, dx3b, "wgrad_w_down")
    to_sibling_a = sibling_start(RS_GROUPS["a"], "a")
    dx1, dx1b, dq, dkv, dg_x = _bwd_attn(dx2, x1, q, kv, gw_attn, norm_xattn_g, to_sibling_a[4], seq, tm_mix)
    parts_a, landed_a, flight_a = chips_start(to_sibling_a, dx1, "a")
    dkv_b, dg_mem = _bwd_kv(dkv, mem2d, gw_attn, norm_mem_g)
    part["w_q"] = _wgrad(h2, dq, "wgrad_w_q", after=flight_a[4])
    part["w_kv"] = _wgrad(dkv_b, mem_n, "wgrad_w_kv")
    part["w_o"] = _wgrad(o, dx2b, "wgrad_w_o")
    to_sibling_b = sibling_start(RS_GROUPS["b"], "b")
    parts_b, landed_b, flight_b = chips_start(to_sibling_b, to_sibling_b[4], "b")
    dx, du, dg_mix, d_conv_w, d_conv_b, d_ln_g, d_ln_b, d_pool_w, d_pool_scale = _bwd_mix(
        dx1, x2d, u_all, c_all, pooled_all, gw_mix, norm_mix_g, conv_w, conv_ln_g, conv_ln_b, pool_w[0], pool_scale,
        flight_b[4], seq, tm_mix)
    grad_x = dx.reshape(x.shape)

    small_grads = dict(norm_mix_g=dg_mix, conv_dw_b=d_conv_b, conv_ln_g=d_ln_g, conv_ln_b=d_ln_b, pool_w=d_pool_w,
                       pool_scale=d_pool_scale, norm_xattn_g=dg_x, norm_mem_g=dg_mem, norm_ffn_g=dg_ffn,
                       ffn_dw_b=d_ffn_b, norm_final_g=dg_final)
    small_list = [small_grads[n] for n, _ in SMALL] + [d_conv_w, d_ffn_w, loss_part[:1]]
    small_mine = _pack_rows(small_list)
    small_flight = _broadcast_start(
        lax.dynamic_update_slice(lax.empty((N_DEV,) + small_mine.shape, F32), small_mine[None], (dev, 0, 0)),
        "small_grads_broadcast_start")

    part["w_in"] = _wgrad(du, h1, "wgrad_w_in", after=small_flight[3])
    part["w_out"] = _wgrad(ymix, dx1b, "wgrad_w_out")
    to_sibling_c = sibling_start(RS_GROUPS["c"], "c")
    parts_c, landed_c, flight_c = chips_start(to_sibling_c, to_sibling_c[4], "c")
    updated_a = reduce_finish(RS_GROUPS["a"], parts_a, landed_a, flight_a, flight_c[4], "a")
    updated_b = reduce_finish(RS_GROUPS["b"], parts_b, landed_b, flight_b, updated_a, "b")
    updated_c = reduce_finish(RS_GROUPS["c"], parts_c, landed_c, flight_c, updated_b, "c")

    small_all = _broadcast_wait(*small_flight[:3], updated_c, "small_grads_broadcast_wait")
    small_sum = _sum_blocks(small_all).reshape(-1)

    pos = 0
    for n, shape in SMALL:
        size = 1
        for s in shape:
            size *= s
        grads[n] = small_sum[pos:pos + size].reshape(shape)
        pos += size
    full_conv_w = small_sum[pos:pos + CONV_WIDTH * D_CONV].reshape(CONV_WIDTH, D_CONV)
    pos += CONV_WIDTH * D_CONV
    full_ffn_w = small_sum[pos:pos + FFN_CONV_WIDTH * 2 * D_FF].reshape(FFN_CONV_WIDTH, 2 * D_FF)
    loss = small_sum[pos + FFN_CONV_WIDTH * 2 * D_FF]
    grads["conv_dw_w"] = lax.dynamic_slice_in_dim(full_conv_w, dev * (D_CONV // N_DEV), D_CONV // N_DEV, axis=1)[None]
    grads["ffn_dw_w"] = lax.dynamic_slice_in_dim(full_ffn_w, dev * (2 * D_FF // N_DEV), 2 * D_FF // N_DEV, axis=1)[None]

    small_names = [n for n in order if n not in W_OFF]
    swap = lambda t: jnp.transpose(t, (1, 0, 2))
    two_d = lambda t: t.reshape(1, -1) if t.ndim == 1 else (swap(t) if t.ndim == 3 else t)
    outs = _adamw_small(*[[two_d(t[n]) for n in small_names] for t in (weights, grads, moments_m, moments_v)])
    for res, out in zip((delta, new_m, new_v), outs):
        for n, o in zip(small_names, out):
            res[n] = swap(o) if o.ndim == 3 else o.reshape(weights[n].shape)

    return (loss, grad_x, *[grads[n] for n in order], *[delta[n] for n in order],
            *[new_m[n] for n in order], *[new_v[n] for n in order])
```

```python
import functools

import jax
import jax.numpy as jnp
from jax import lax
from jax.experimental import pallas as pl
from jax.experimental.pallas import tpu as pltpu

F32 = jnp.float32
BF16 = jnp.bfloat16
MESH = pl.DeviceIdType.MESH

N_DEV = 8
D_MODEL = 1024
D_CONV = 512
D_POOL = 512
CONV_WIDTH = 31
POOL_WINDOWS = (2, 4, 8, 16)
POOL_GROUP_DIM = 128
D_IN = 1536
N_MEM = 256
HEADS = 4
HEAD_DIM = 256
D_FF = 2816
FFN_CONV_WIDTH = 3
EPS = 1e-6
ADAM_LR = 0.001
ADAM_B1 = 0.9
ADAM_B2 = 0.999
ADAM_EPS = 1e-08
ADAM_WD = 0.01
ADAM_STEP = 10

VMEM_LIMIT_V7X = 56 * 1024 * 1024
CONV_HALO = 32
POOL_HALO = 16
FFN_HALO = 8
FFN_CHUNK = 2816

W_ROWS = (("w_in", 192), ("w_out", 128), ("w_q", 128), ("w_kv", 256), ("w_o", 128), ("w_up", 704), ("w_down", 352))
AG_GROUPS = (("w_in", "w_out"), ("w_q", "w_kv", "w_o"), ("w_up", "w_down"))
W_OFF = {}
for _names in AG_GROUPS:
    _o = 0
    for _n in _names:
        W_OFF[_n] = (_o, dict(W_ROWS)[_n])
        _o += dict(W_ROWS)[_n]
RS_GROUPS = {"a": ("w_up", "w_down"), "b": ("w_q", "w_kv", "w_o"), "c": ("w_in", "w_out")}


def _dot(a, b):
    return jnp.dot(a, b, preferred_element_type=F32)


def _dot_nt(a, b):
    return lax.dot_general(a, b, (((1,), (1,)), ((), ())), preferred_element_type=F32)


def _dot_tn(a, b):
    return lax.dot_general(a, b, (((0,), (0,)), ((), ())), preferred_element_type=F32)


def _sigmoid(v):
    return 1.0 / (1.0 + jnp.exp(-v))


def _rms_fwd(v):
    r = lax.rsqrt(jnp.mean(v * v, axis=-1, keepdims=True) + EPS)
    return v * r, r


def _rms_bwd(dh, vh, r, g):
    gd = dh * g
    return r * (gd - vh * jnp.mean(gd * vh, axis=-1, keepdims=True))


def _sublane_shifts(v):
    rows = v.shape[0]
    return [v] + [pltpu.roll(v, rows - b, 0) for b in range(1, 8)]


def _colsum(v):
    return jnp.sum(v, axis=0, keepdims=True)


def _full(shape):
    return pl.BlockSpec(shape, lambda *_: (0,) * len(shape))


def _params(sem=("arbitrary",), vmem=VMEM_LIMIT_V7X):
    return pltpu.CompilerParams(dimension_semantics=sem, vmem_limit_bytes=vmem)


def _load_weight(g_hbm, name, dst, sem):
    off, rows = W_OFF[name]
    return [pltpu.make_async_copy(g_hbm.at[d, pl.ds(off, rows), :], dst.at[pl.ds(d * rows, rows), :], sem)
            for d in range(N_DEV)]


def _position():
    x, y, c = lax.axis_index("x"), lax.axis_index("y"), lax.axis_index("c")
    chips = [(1 - x, y), (x, 1 - y), (1 - x, 1 - y)]
    return x, y, c, chips


def _dev(px, py, pc):
    return 4 * px + 2 * py + pc


def _all_gather(arrs, name):
    n = len(arrs)

    def body(*refs):
        ins, outs = refs[:n], refs[n:2 * n]
        send_sems, recv_sems, local_sems = refs[2 * n:2 * n + 3]
        bounce = refs[2 * n + 3:]
        x, y, c, chips = _position()
        me, sibling = (x, y, c), (x, y, 1 - c)

        def copy(a, k, block, to, src=None):
            rows = outs[a].at[_dev(*block)]
            return pltpu.make_async_remote_copy(
                src_ref=rows if src is None else src, dst_ref=rows,
                send_sem=send_sems.at[a, k], recv_sem=recv_sems.at[a, k], device_id=to, device_id_type=MESH)

        sends = []
        for a in range(n):
            first = [copy(a, 0, me, sibling, src=ins[a])]
            first += [copy(a, 1 + j, me, (*chip, c), src=ins[a]) for j, chip in enumerate(chips)]
            for cp in first:
                cp.start()
            sends += first
        started = []
        for a in range(n):
            load = pltpu.make_async_copy(ins[a], bounce[a], local_sems.at[a, 0])
            load.start()
            load.wait()
            mine = pltpu.make_async_copy(bounce[a], outs[a].at[_dev(*me)], local_sems.at[a, 1])
            mine.start()
            started.append(mine)
        for j, chip in enumerate(chips):
            for a in range(n):
                copy(a, 1 + j, (*chip, c), me).wait_recv()
                passed = copy(a, 4 + j, (*chip, c), sibling)
                passed.start()
                sends.append(passed)
        for a in range(n):
            copy(a, 0, sibling, me).wait_recv()
            for j, chip in enumerate(chips):
                copy(a, 4 + j, (*chip, 1 - c), me).wait_recv()
        for cp in sends:
            cp.wait_send()
        for mine in started:
            mine.wait()

    any_spec = pl.BlockSpec(memory_space=pl.ANY)
    return pl.pallas_call(
        body, name=name,
        out_shape=[jax.ShapeDtypeStruct((N_DEV,) + a.shape, a.dtype) for a in arrs],
        in_specs=[any_spec] * n, out_specs=[any_spec] * n,
        scratch_shapes=[pltpu.SemaphoreType.DMA((n, 7)), pltpu.SemaphoreType.DMA((n, 7)), pltpu.SemaphoreType.DMA((n, 2))]
        + [pltpu.VMEM(a.shape, a.dtype) for a in arrs],
    )(*arrs)


_HBM = pl.BlockSpec(memory_space=pltpu.HBM)
_SEM = pl.BlockSpec(memory_space=pltpu.SEMAPHORE)
_SIDE_EFFECT = pltpu.SideEffectType.DATAFLOW_SIDE_EFFECTING


def _gather_start(buf, after, name):
    def body(buf_ref, after_ref, send_sems, recv_sems, buf_thru, token):
        del after_ref, buf_thru
        x, y, c, chips = _position()
        rows = buf_ref.at[_dev(x, y, c)]
        for k, to in enumerate([(x, y, 1 - c)] + [(*chip, c) for chip in chips]):
            pltpu.make_async_remote_copy(src_ref=rows, dst_ref=rows, send_sem=send_sems.at[k], recv_sem=recv_sems.at[k],
                                         device_id=to, device_id_type=MESH).start()
        token[...] = jnp.zeros_like(token)

    return pl.pallas_call(
        body, name=name,
        out_shape=(pltpu.SemaphoreType.DMA((4,)), pltpu.SemaphoreType.DMA((4,)), pltpu.HBM(buf.shape, buf.dtype),
                   jax.ShapeDtypeStruct((8, 128), F32)),
        in_specs=(_HBM, pl.BlockSpec(memory_space=pl.ANY)),
        out_specs=(_SEM, _SEM, _HBM, pl.BlockSpec(memory_space=pltpu.VMEM)),
        input_output_aliases={0: 2},
        compiler_params=pltpu.CompilerParams(has_side_effects=_SIDE_EFFECT),
    )(pltpu.with_memory_space_constraint(buf, pltpu.HBM), after)


def _gather_forward(send_sems, recv_sems, buf, after, name):
    def body(buf_ref, send_sems, recv_sems, after_ref, fwd_send, fwd_recv, buf_thru):
        del after_ref, buf_thru
        x, y, c, chips = _position()
        sibling = (x, y, 1 - c)

        def copy(block, k, sends, recvs):
            rows = buf_ref.at[_dev(*block)]
            return pltpu.make_async_remote_copy(src_ref=rows, dst_ref=rows, send_sem=sends.at[k], recv_sem=recvs.at[k],
                                                device_id=sibling, device_id_type=MESH)

        for k in range(4):
            copy((x, y, c), k, send_sems, recv_sems).wait_send()
        copy(sibling, 0, send_sems, recv_sems).wait_recv()
        for j, chip in enumerate(chips):
            copy((*chip, c), 1 + j, send_sems, recv_sems).wait_recv()
            copy((*chip, c), j, fwd_send, fwd_recv).start()

    return pl.pallas_call(
        body, name=name,
        out_shape=(pltpu.SemaphoreType.DMA((3,)), pltpu.SemaphoreType.DMA((3,)), pltpu.HBM(buf.shape, buf.dtype)),
        in_specs=(_HBM, _SEM, _SEM, pl.BlockSpec(memory_space=pl.ANY)), out_specs=(_SEM, _SEM, _HBM),
        input_output_aliases={0: 2},
        compiler_params=pltpu.CompilerParams(has_side_effects=_SIDE_EFFECT),
    )(buf, send_sems, recv_sems, after)


def _gather_finish(fwd_send, fwd_recv, buf, name):
    def body(buf_ref, fwd_send, fwd_recv, buf_thru):
        del buf_thru
        x, y, c, chips = _position()
        for j, chip in enumerate(chips):
            cp = pltpu.make_async_remote_copy(
                src_ref=buf_ref.at[_dev(*chip, c)], dst_ref=buf_ref.at[_dev(*chip, 1 - c)], send_sem=fwd_send.at[j],
                recv_sem=fwd_recv.at[j], device_id=(x, y, 1 - c), device_id_type=MESH)
            cp.wait_send()
            cp.wait_recv()

    return pl.pallas_call(
        body, name=name,
        out_shape=pltpu.HBM(buf.shape, buf.dtype),
        in_specs=(_HBM, _SEM, _SEM), out_specs=_HBM,
        input_output_aliases={0: 0},
        compiler_params=pltpu.CompilerParams(has_side_effects=_SIDE_EFFECT),
    )(buf, fwd_send, fwd_recv)


def _everyone_else(x, y, c, chips):
    return [(x, y, 1 - c)] + [(*chip, core) for chip in chips for core in (c, 1 - c)]


def _broadcast_start(buf, name):
    def body(buf_ref, send_sems, recv_sems, buf_thru, token):
        del buf_thru
        x, y, c, chips = _position()
        rows = buf_ref.at[_dev(x, y, c)]
        for k, to in enumerate(_everyone_else(x, y, c, chips)):
            pltpu.make_async_remote_copy(src_ref=rows, dst_ref=rows, send_sem=send_sems.at[k], recv_sem=recv_sems.at[k],
                                         device_id=to, device_id_type=MESH).start()
        token[...] = jnp.zeros_like(token)

    return pl.pallas_call(
        body, name=name,
        out_shape=(pltpu.SemaphoreType.DMA((7,)), pltpu.SemaphoreType.DMA((7,)), pltpu.HBM(buf.shape, buf.dtype),
                   jax.ShapeDtypeStruct((8, 128), F32)),
        in_specs=(_HBM,), out_specs=(_SEM, _SEM, _HBM, pl.BlockSpec(memory_space=pltpu.VMEM)),
        input_output_aliases={0: 2},
        compiler_params=pltpu.CompilerParams(has_side_effects=_SIDE_EFFECT),
    )(pltpu.with_memory_space_constraint(buf, pltpu.HBM))


def _broadcast_wait(send_sems, recv_sems, buf, after, name):
    def body(buf_ref, send_sems, recv_sems, after_ref, buf_thru):
        del after_ref, buf_thru
        x, y, c, chips = _position()
        for k, peer in enumerate(_everyone_else(x, y, c, chips)):
            cp = pltpu.make_async_remote_copy(
                src_ref=buf_ref.at[_dev(x, y, c)], dst_ref=buf_ref.at[_dev(*peer)], send_sem=send_sems.at[k],
                recv_sem=recv_sems.at[k], device_id=peer, device_id_type=MESH)
            cp.wait_send()
            cp.wait_recv()

    return pl.pallas_call(
        body, name=name,
        out_shape=pltpu.HBM(buf.shape, buf.dtype),
        in_specs=(_HBM, _SEM, _SEM, pl.BlockSpec(memory_space=pl.ANY)), out_specs=_HBM,
        input_output_aliases={0: 0},
        compiler_params=pltpu.CompilerParams(has_side_effects=_SIDE_EFFECT),
    )(buf, send_sems, recv_sems, after)


def _to_sibling(j, x, y, c, chips):
    return _dev(*([(x, y)] + chips)[j], 1 - c), (x, y, 1 - c)


def _to_chip(j, x, y, c, chips):
    return j, (*chips[j], c)


def _exchange_start(srcs, n_slots, route, name):
    n = len(srcs)

    def body(*refs):
        s_refs, land_refs = refs[:n], refs[n:2 * n]
        send_sems, recv_sems = refs[2 * n:2 * n + 2]
        token = refs[-1]
        x, y, c, chips = _position()
        for k in range(n):
            for j in range(n_slots):
                block, to = route(j, x, y, c, chips)
                pltpu.make_async_remote_copy(
                    src_ref=s_refs[k].at[block], dst_ref=land_refs[k].at[j], send_sem=send_sems.at[n_slots * k + j],
                    recv_sem=recv_sems.at[n_slots * k + j], device_id=to, device_id_type=MESH).start()
        token[...] = jnp.zeros_like(token)

    lands = [jax.ShapeDtypeStruct((n_slots,) + s.shape[1:], s.dtype) for s in srcs]
    outs = pl.pallas_call(
        body, name=name,
        out_shape=(pltpu.SemaphoreType.DMA((n_slots * n,)), pltpu.SemaphoreType.DMA((n_slots * n,)),
                   *[pltpu.HBM(s.shape, s.dtype) for s in srcs], *[pltpu.HBM(l.shape, l.dtype) for l in lands],
                   jax.ShapeDtypeStruct((8, 128), F32)),
        in_specs=[_HBM] * (2 * n), out_specs=(_SEM, _SEM, *[_HBM] * (2 * n), pl.BlockSpec(memory_space=pltpu.VMEM)),
        input_output_aliases={k: 2 + k for k in range(2 * n)},
        compiler_params=pltpu.CompilerParams(has_side_effects=_SIDE_EFFECT),
    )(*[pltpu.with_memory_space_constraint(s, pltpu.HBM) for s in srcs],
      *[pltpu.with_memory_space_constraint(lax.empty(l.shape, l.dtype), pltpu.HBM) for l in lands])
    return outs[0], outs[1], outs[2:2 + n], outs[2 + n:2 + 2 * n], outs[-1]


def _exchange_wait(send_sems, recv_sems, s_thru, land_thru, after, n_slots, route, name):
    n = len(s_thru)

    def body(*refs):
        s_refs, land_refs = refs[:n], refs[n:2 * n]
        send_sems, recv_sems = refs[2 * n:2 * n + 2]
        x, y, c, chips = _position()
        for k in range(n):
            for j in range(n_slots):
                block, to = route(j, x, y, c, chips)
                cp = pltpu.make_async_remote_copy(
                    src_ref=s_refs[k].at[block], dst_ref=land_refs[k].at[j], send_sem=send_sems.at[n_slots * k + j],
                    recv_sem=recv_sems.at[n_slots * k + j], device_id=to, device_id_type=MESH)
                cp.wait_send()
                cp.wait_recv()

    outs = pl.pallas_call(
        body, name=name,
        out_shape=(*[pltpu.HBM(s.shape, s.dtype) for s in s_thru], *[pltpu.HBM(l.shape, l.dtype) for l in land_thru]),
        in_specs=[_HBM] * (2 * n) + [_SEM, _SEM, pl.BlockSpec(memory_space=pl.ANY)], out_specs=[_HBM] * (2 * n),
        input_output_aliases={k: k for k in range(2 * n)},
        compiler_params=pltpu.CompilerParams(has_side_effects=_SIDE_EFFECT),
    )(*s_thru, *land_thru, send_sems, recv_sems, after)
    return outs[:n], outs[n:]


def _owner_table():
    x, y, c = lax.axis_index("x"), lax.axis_index("y"), lax.axis_index("c")
    chips = [(x, y), (1 - x, y), (x, 1 - y), (1 - x, 1 - y)]
    return jnp.stack([_dev(px, py, c) for px, py in chips]).astype(jnp.int32)


def _chip_partial_sums(table, parts, from_sibling, name):
    n = len(parts)

    def body(tab_ref, *refs):
        del tab_ref
        for g_ref, l_ref, out_ref in zip(refs[:n], refs[n:2 * n], refs[2 * n:]):
            out_ref[...] = (g_ref[...].astype(F32) + l_ref[...].astype(F32)).astype(out_ref.dtype)

    block = lambda p: (None,) + p.shape[1:]
    grid_spec = pltpu.PrefetchScalarGridSpec(
        num_scalar_prefetch=1, grid=(3,),
        in_specs=[pl.BlockSpec(block(p), lambda j, tab: (tab[j + 1], 0, 0)) for p in parts]
        + [pl.BlockSpec(block(p), lambda j, tab: (j + 1, 0, 0)) for p in parts],
        out_specs=[pl.BlockSpec(block(p), lambda j, tab: (j, 0, 0)) for p in parts])
    return pl.pallas_call(
        body, name=name, grid_spec=grid_spec,
        out_shape=[jax.ShapeDtypeStruct((3,) + p.shape[1:], BF16) for p in parts],
        compiler_params=_params(("arbitrary",)),
    )(table, *parts, *from_sibling)


def _final_update(table, parts, from_sibling, from_chips, states, name):
    n = len(parts)
    updated = [k for k in range(n) if states[k] is not None]

    def body(tab_ref, *refs):
        del tab_ref
        ins, outs = refs[:3 * n + 3 * len(updated)], list(refs[3 * n + 3 * len(updated):])
        wmv = list(ins[3 * n:])
        for k in range(n):
            acc = ins[k][...].astype(F32) + ins[n + k][...].astype(F32)
            for j in range(3):
                acc = acc + ins[2 * n + k][j].astype(F32)
            outs.pop(0)[...] = acc
            if k in updated:
                w_ref, m_ref, v_ref = wmv[:3]
                del wmv[:3]
                for out_ref, val in zip(outs[:3], _adamw_update(w_ref[...], acc, m_ref[...], v_ref[...])):
                    out_ref[...] = val
                del outs[:3]

    half = lambda p: (p.shape[1] // 2, p.shape[2])
    rows = lambda p: pl.BlockSpec(half(p), lambda t, tab: (t, 0))
    grid_spec = pltpu.PrefetchScalarGridSpec(
        num_scalar_prefetch=1, grid=(2,),
        in_specs=[pl.BlockSpec((None,) + half(p), lambda t, tab: (tab[0], t, 0)) for p in parts]
        + [pl.BlockSpec((None,) + half(p), lambda t, tab: (0, t, 0)) for p in parts]
        + [pl.BlockSpec((3,) + half(p), lambda t, tab: (0, t, 0)) for p in parts]
        + [rows(parts[k]) for k in updated for _ in range(3)],
        out_specs=[rows(parts[k]) for k in range(n) for _ in range(4 if k in updated else 1)])
    outs = pl.pallas_call(
        body, name=name, grid_spec=grid_spec,
        out_shape=[jax.ShapeDtypeStruct(parts[k].shape[1:], F32) for k in range(n) for _ in range(4 if k in updated else 1)],
        compiler_params=_params(("arbitrary",)),
    )(table, *parts, *from_sibling, *from_chips, *[t for k in updated for t in states[k]])
    result = []
    for k in range(n):
        count = 4 if k in updated else 1
        result.append(outs[:count])
        outs = outs[count:]
    return result


def _sum_blocks(g8):
    _, rows, cols = g8.shape

    def body(g_ref, out_ref):
        acc = g_ref[0]
        for d in range(1, N_DEV):
            acc = acc + g_ref[d]
        out_ref[...] = acc

    return pl.pallas_call(
        body, name="small_grad_sum", grid=(1,),
        in_specs=[_full((N_DEV, rows, cols))], out_specs=_full((rows, cols)),
        out_shape=jax.ShapeDtypeStruct((rows, cols), F32),
        compiler_params=_params(("arbitrary",)),
    )(g8)


def _fwd_mix(x2d, gw, g_mix, conv_w, conv_b, ln_g, ln_b, pool_w, pool_scale, after, seq, tm):
    tokens = x2d.shape[0]
    n_tiles = tokens // tm
    tps = seq // tm

    def body(x_ref, gmix_ref, gw_hbm, cw_ref, cb_ref, lng_ref, lnb_ref, pw_ref, ps_ref, after_ref,
             x1_ref, u_ref, c_ref, pooled_ref, ymix_ref, h1_ref,
             win_v, wout_v, hc_carry, up_carry, sem):
        del after_ref
        i = pl.program_id(0)

        @pl.when(i == 0)
        def _():
            copies = _load_weight(gw_hbm, "w_in", win_v, sem) + _load_weight(gw_hbm, "w_out", wout_v, sem)
            for cp in copies:
                cp.start()
            for cp in copies:
                cp.wait()

        @pl.when(i % tps == 0)
        def _():
            hc_carry[...] = jnp.zeros_like(hc_carry)
            up_carry[...] = jnp.zeros_like(up_carry)

        x = x_ref[...]
        xh, _ = _rms_fwd(x)
        h1 = (xh * gmix_ref[...]).astype(BF16)
        h1_ref[...] = h1
        u = _dot_nt(h1, win_v[...])
        u_ref[...] = u
        val, gate, up = u[:, :D_CONV], u[:, D_CONV:2 * D_CONV], u[:, 2 * D_CONV:]

        hc = val * _sigmoid(gate)
        ext = jnp.concatenate([hc_carry[...], hc], axis=0)
        hc_carry[...] = hc[tm - CONV_HALO:, :]
        conv = jnp.broadcast_to(cb_ref[...], (tm, D_CONV))
        ahead_by = _sublane_shifts(ext)
        for k in range(CONV_WIDTH):
            whole, part = divmod(CONV_HALO - (CONV_WIDTH - 1) + k, 8)
            conv = conv + cw_ref[k:k + 1, :] * ahead_by[part][8 * whole:8 * whole + tm, :]
        c_ref[...] = conv
        mu = jnp.mean(conv, axis=-1, keepdims=True)
        cen = conv - mu
        ln = cen * lax.rsqrt(jnp.mean(cen * cen, axis=-1, keepdims=True) + EPS) * lng_ref[...] + lnb_ref[...]
        y_conv = ln * _sigmoid(ln)

        extp = jnp.concatenate([up_carry[...], up], axis=0)
        up_carry[...] = up[tm - POOL_HALO:, :]
        pos = lax.broadcasted_iota(jnp.int32, (tm, 1), 0) + (i % tps) * tm
        run = extp
        mixed = []
        for g, w in enumerate(POOL_WINDOWS):
            lo = g * POOL_GROUP_DIM
            run = run[:, POOL_GROUP_DIM if g else 0:]
            run = run + pltpu.roll(run, w // 2, 0)
            cnt = jnp.minimum(pos + 1, w).astype(F32)
            pooled = run[POOL_HALO:, :POOL_GROUP_DIM] / cnt - up[:, lo:lo + POOL_GROUP_DIM]
            pooled = pooled.astype(BF16)
            pooled_ref[:, lo:lo + POOL_GROUP_DIM] = pooled
            mixed.append(_dot(pooled, pw_ref[g].astype(BF16)))
        y_pool = jnp.concatenate(mixed, axis=-1) * ps_ref[...]

        ymix = jnp.concatenate([y_conv, y_pool], axis=-1).astype(BF16)
        ymix_ref[...] = ymix
        x1_ref[...] = x + _dot(ymix, wout_v[...])

    row = lambda w: pl.BlockSpec((tm, w), lambda i: (i, 0))
    return pl.pallas_call(
        body, name="fwd_mix", grid=(n_tiles,),
        in_specs=[row(D_MODEL), _full((1, D_MODEL)), pl.BlockSpec(memory_space=pl.ANY),
                  _full((CONV_WIDTH, D_CONV)), _full((1, D_CONV)), _full((1, D_CONV)), _full((1, D_CONV)),
                  _full((4, POOL_GROUP_DIM, POOL_GROUP_DIM)), _full((1, D_POOL)), _full(after.shape)],
        out_specs=[row(D_MODEL), row(D_IN), row(D_CONV), row(D_POOL), row(D_MODEL), row(D_MODEL)],
        out_shape=[jax.ShapeDtypeStruct((tokens, D_MODEL), F32), jax.ShapeDtypeStruct((tokens, D_IN), F32),
                   jax.ShapeDtypeStruct((tokens, D_CONV), F32), jax.ShapeDtypeStruct((tokens, D_POOL), BF16),
                   jax.ShapeDtypeStruct((tokens, D_MODEL), BF16), jax.ShapeDtypeStruct((tokens, D_MODEL), BF16)],
        scratch_shapes=[pltpu.VMEM((D_IN, D_MODEL), BF16), pltpu.VMEM((D_MODEL, D_MODEL), BF16),
                        pltpu.VMEM((CONV_HALO, D_CONV), F32), pltpu.VMEM((POOL_HALO, D_POOL), F32),
                        pltpu.SemaphoreType.DMA],
        compiler_params=_params(),
    )(x2d, g_mix, gw, conv_w, conv_b, ln_g, ln_b, pool_w, pool_scale, after)


def _fwd_kv(mem2d, gw, g_mem):
    rows = mem2d.shape[0]
    n_b = rows // N_MEM

    def body(mem_ref, g_ref, gw_hbm, mn_ref, kv_ref, wkv_v, sem):
        @pl.when(pl.program_id(0) == 0)
        def _():
            copies = _load_weight(gw_hbm, "w_kv", wkv_v, sem)
            for cp in copies:
                cp.start()
            for cp in copies:
                cp.wait()

        mh, _ = _rms_fwd(mem_ref[...])
        mn = (mh * g_ref[...]).astype(BF16)
        mn_ref[...] = mn
        kv_ref[...] = _dot_nt(mn, wkv_v[...]).astype(BF16)

    return pl.pallas_call(
        body, name="fwd_kv", grid=(n_b,),
        in_specs=[pl.BlockSpec((N_MEM, D_MODEL), lambda b: (b, 0)), _full((1, D_MODEL)), pl.BlockSpec(memory_space=pl.ANY)],
        out_specs=[pl.BlockSpec((N_MEM, D_MODEL), lambda b: (b, 0)), pl.BlockSpec((N_MEM, 2 * D_MODEL), lambda b: (b, 0))],
        out_shape=[jax.ShapeDtypeStruct((rows, D_MODEL), BF16), jax.ShapeDtypeStruct((rows, 2 * D_MODEL), BF16)],
        scratch_shapes=[pltpu.VMEM((2 * D_MODEL, D_MODEL), BF16), pltpu.SemaphoreType.DMA],
        compiler_params=_params(),
    )(mem2d, g_mem, gw)


def _softmax_rows(s):
    e = jnp.exp(s - jnp.max(s, axis=-1, keepdims=True))
    return e / jnp.sum(e, axis=-1, keepdims=True)


def _fwd_attn(x1, kv, gw, g_x, seq, tm):
    tokens = x1.shape[0]
    n_tiles = tokens // tm
    tps = seq // tm

    def body(x1_ref, kv_ref, g_ref, gw_hbm, x2_ref, h2_ref, q_ref, o_ref, wq_v, wo_v, sem):
        @pl.when(pl.program_id(0) == 0)
        def _():
            copies = _load_weight(gw_hbm, "w_q", wq_v, sem) + _load_weight(gw_hbm, "w_o", wo_v, sem)
            for cp in copies:
                cp.start()
            for cp in copies:
                cp.wait()

        x1v = x1_ref[...]
        xh, _ = _rms_fwd(x1v)
        h2 = (xh * g_ref[...]).astype(BF16)
        h2_ref[...] = h2
        q = (_dot(h2, wq_v[...]) * (HEAD_DIM ** -0.5)).astype(BF16)
        q_ref[...] = q
        heads = [slice(h * HEAD_DIM, (h + 1) * HEAD_DIM) for h in range(HEADS)]
        scores = [_dot_nt(q[:, hd], kv_ref[:, hd]) for hd in heads]
        probs = [_softmax_rows(s).astype(BF16) for s in scores]
        outs = [_dot(p, kv_ref[:, pl.ds(D_MODEL + h * HEAD_DIM, HEAD_DIM)]) for h, p in enumerate(probs)]
        o = jnp.concatenate(outs, axis=-1).astype(BF16)
        o_ref[...] = o
        x2_ref[...] = x1v + _dot(o, wo_v[...])

    row = lambda w: pl.BlockSpec((tm, w), lambda i: (i, 0))
    return pl.pallas_call(
        body, name="fwd_attn", grid=(n_tiles,),
        in_specs=[row(D_MODEL), pl.BlockSpec((N_MEM, 2 * D_MODEL), lambda i: (i // tps, 0)), _full((1, D_MODEL)),
                  pl.BlockSpec(memory_space=pl.ANY)],
        out_specs=[row(D_MODEL)] * 4,
        out_shape=[jax.ShapeDtypeStruct((tokens, D_MODEL), F32)] + [jax.ShapeDtypeStruct((tokens, D_MODEL), BF16)] * 3,
        scratch_shapes=[pltpu.VMEM((D_MODEL, D_MODEL), BF16), pltpu.VMEM((D_MODEL, D_MODEL), BF16), pltpu.SemaphoreType.DMA],
        compiler_params=_params(),
    )(x1, kv, g_x, gw)


def _ffn_conv(uu, halo, w_ref, b_ref, cols):
    ext = jnp.concatenate([halo, uu], axis=0)
    p1 = pltpu.roll(ext, 1, 0)[FFN_HALO:, :]
    p2 = pltpu.roll(ext, 2, 0)[FFN_HALO:, :]
    return b_ref[:, cols] + w_ref[2:3, cols] * uu + w_ref[1:2, cols] * p1 + w_ref[0:1, cols] * p2


def _fwd_ffn(x2, target, gw, g_ffn, ffn_w, ffn_b, g_final, seq, tm):
    tokens = x2.shape[0]
    n_tiles = tokens // tm
    tps = seq // tm
    n_chunks = D_FF // FFN_CHUNK

    def body(x2_ref, tgt_ref, gffn_ref, gw_hbm, fw_ref, fb_ref, gfin_ref,
             uu_ref, cc_ref, a_ref, h3_ref, dx3_ref, dx3b_ref, loss_ref, dgfin_ref,
             wup_v, wdown_v, carry, sem):
        i = pl.program_id(0)

        @pl.when(i == 0)
        def _():
            copies = _load_weight(gw_hbm, "w_up", wup_v, sem) + _load_weight(gw_hbm, "w_down", wdown_v, sem)
            for cp in copies:
                cp.start()
            for cp in copies:
                cp.wait()
            loss_ref[...] = jnp.zeros_like(loss_ref)
            dgfin_ref[...] = jnp.zeros_like(dgfin_ref)

        @pl.when(i % tps == 0)
        def _():
            carry[...] = jnp.zeros_like(carry)

        x2v = x2_ref[...]
        xh, _ = _rms_fwd(x2v)
        h3 = (xh * gffn_ref[...]).astype(BF16)
        h3_ref[...] = h3
        acc = jnp.zeros((tm, D_MODEL), F32)
        for jc in range(n_chunks):
            halves = []
            for half in range(2):
                cols = pl.ds(half * D_FF + jc * FFN_CHUNK, FFN_CHUNK)
                uu = _dot_nt(h3, wup_v[cols, :])
                uu_ref[:, cols] = uu.astype(BF16)
                cc = _ffn_conv(uu, carry[:, cols], fw_ref, fb_ref, cols)
                cc_ref[:, cols] = cc.astype(BF16)
                halves.append(cc)
                carry[:, cols] = uu[tm - FFN_HALO:, :]
            gate, val = halves
            a = (gate * _sigmoid(gate) * val).astype(BF16)
            a_ref[:, pl.ds(jc * FFN_CHUNK, FFN_CHUNK)] = a
            acc = acc + _dot(a, wdown_v[pl.ds(jc * FFN_CHUNK, FFN_CHUNK), :])
        x3 = x2v + acc

        xh3, r3 = _rms_fwd(x3)
        gfin = gfin_ref[...]
        err = xh3 * gfin - tgt_ref[...]
        loss_ref[...] += jnp.full(loss_ref.shape, jnp.sum(err * err) * (0.5 / D_MODEL), F32)
        dy = err * (1.0 / D_MODEL)
        dgfin_ref[...] += _colsum(dy * xh3)
        dx3 = _rms_bwd(dy, xh3, r3, gfin)
        dx3_ref[...] = dx3
        dx3b_ref[...] = dx3.astype(BF16)

    row = lambda w: pl.BlockSpec((tm, w), lambda i: (i, 0))
    return pl.pallas_call(
        body, name="fwd_ffn", grid=(n_tiles,),
        in_specs=[row(D_MODEL), row(D_MODEL), _full((1, D_MODEL)), pl.BlockSpec(memory_space=pl.ANY),
                  _full((FFN_CONV_WIDTH, 2 * D_FF)), _full((1, 2 * D_FF)), _full((1, D_MODEL))],
        out_specs=[row(2 * D_FF), row(2 * D_FF), row(D_FF), row(D_MODEL), row(D_MODEL), row(D_MODEL), _full((8, 128)),
                   _full((1, D_MODEL))],
        out_shape=[jax.ShapeDtypeStruct((tokens, 2 * D_FF), BF16), jax.ShapeDtypeStruct((tokens, 2 * D_FF), BF16),
                   jax.ShapeDtypeStruct((tokens, D_FF), BF16),
                   jax.ShapeDtypeStruct((tokens, D_MODEL), BF16), jax.ShapeDtypeStruct((tokens, D_MODEL), F32),
                   jax.ShapeDtypeStruct((tokens, D_MODEL), BF16),
                   jax.ShapeDtypeStruct((8, 128), F32), jax.ShapeDtypeStruct((1, D_MODEL), F32)],
        scratch_shapes=[pltpu.VMEM((2 * D_FF, D_MODEL), BF16), pltpu.VMEM((D_FF, D_MODEL), BF16),
                        pltpu.VMEM((FFN_HALO, 2 * D_FF), F32), pltpu.SemaphoreType.DMA],
        compiler_params=_params(),
    )(x2, target, g_ffn, gw, ffn_w, ffn_b, g_final)


def _bwd_ffn(dx3, x2, uu_all, cc_all, gw, g_ffn, ffn_w, seq, tm):
    tokens = x2.shape[0]
    n_tiles = tokens // tm
    tps = seq // tm
    n_chunks = D_FF // FFN_CHUNK

    def body(dx3_ref, x2_ref, uu_ref, cc_ref, gffn_ref, gw_hbm, fw_ref,
             dx2_ref, dx2b_ref, duu_ref, dfb_ref, dfw_ref, dg_ref,
             wup_v, wdown_v, carry, sem):
        i = pl.program_id(0)
        t = n_tiles - 1 - i

        @pl.when(i == 0)
        def _():
            copies = _load_weight(gw_hbm, "w_up", wup_v, sem) + _load_weight(gw_hbm, "w_down", wdown_v, sem)
            for cp in copies:
                cp.start()
            for cp in copies:
                cp.wait()
            dfb_ref[...] = jnp.zeros_like(dfb_ref)
            dfw_ref[...] = jnp.zeros_like(dfw_ref)
            dg_ref[...] = jnp.zeros_like(dg_ref)

        @pl.when(t % tps == tps - 1)
        def _():
            carry[...] = jnp.zeros_like(carry)

        dx3v = dx3_ref[...]
        dx3b = dx3v.astype(BF16)
        dh3 = jnp.zeros((tm, D_MODEL), F32)
        for jc in range(n_chunks):
            da = _dot_nt(dx3b, wdown_v[pl.ds(jc * FFN_CHUNK, FFN_CHUNK), :])
            colss = [pl.ds(half * D_FF + jc * FFN_CHUNK, FFN_CHUNK) for half in range(2)]
            gate, val = [cc_ref[:, cols].astype(F32) for cols in colss]
            sg = _sigmoid(gate)
            dgate = da * val * (sg * (1.0 + gate * (1.0 - sg)))
            dval = da * (gate * sg)
            for dcc, cols in zip((dgate, dval), colss):
                uu = uu_ref[:, cols].astype(F32)
                dfb_ref[:, cols] += _colsum(dcc)
                ext = jnp.concatenate([dcc, carry[:, cols]], axis=0)
                carry[:, cols] = dcc[:FFN_HALO, :]
                n1 = pltpu.roll(ext, tm + FFN_HALO - 1, 0)[:tm, :]
                n2 = pltpu.roll(ext, tm + FFN_HALO - 2, 0)[:tm, :]
                duu = fw_ref[2:3, cols] * dcc + fw_ref[1:2, cols] * n1 + fw_ref[0:1, cols] * n2
                dfw_ref[2:3, cols] += _colsum(uu * dcc)
                dfw_ref[1:2, cols] += _colsum(uu * n1)
                dfw_ref[0:1, cols] += _colsum(uu * n2)
                duub = duu.astype(BF16)
                duu_ref[:, cols] = duub
                dh3 = dh3 + _dot(duub, wup_v[cols, :])
        xh, r = _rms_fwd(x2_ref[...])
        dg_ref[...] += _colsum(dh3 * xh)
        dx2 = dx3v + _rms_bwd(dh3, xh, r, gffn_ref[...])
        dx2_ref[...] = dx2
        dx2b_ref[...] = dx2.astype(BF16)

    rev = lambda w: pl.BlockSpec((tm, w), lambda i: (n_tiles - 1 - i, 0))
    return pl.pallas_call(
        body, name="bwd_ffn", grid=(n_tiles,),
        in_specs=[rev(D_MODEL), rev(D_MODEL), rev(2 * D_FF), rev(2 * D_FF), _full((1, D_MODEL)),
                  pl.BlockSpec(memory_space=pl.ANY), _full((FFN_CONV_WIDTH, 2 * D_FF))],
        out_specs=[rev(D_MODEL), rev(D_MODEL), rev(2 * D_FF), _full((1, 2 * D_FF)), _full((FFN_CONV_WIDTH, 2 * D_FF)),
                   _full((1, D_MODEL))],
        out_shape=[jax.ShapeDtypeStruct((tokens, D_MODEL), F32), jax.ShapeDtypeStruct((tokens, D_MODEL), BF16),
                   jax.ShapeDtypeStruct((tokens, 2 * D_FF), BF16),
                   jax.ShapeDtypeStruct((1, 2 * D_FF), F32), jax.ShapeDtypeStruct((FFN_CONV_WIDTH, 2 * D_FF), F32),
                   jax.ShapeDtypeStruct((1, D_MODEL), F32)],
        scratch_shapes=[pltpu.VMEM((2 * D_FF, D_MODEL), BF16), pltpu.VMEM((D_FF, D_MODEL), BF16),
                        pltpu.VMEM((FFN_HALO, 2 * D_FF), F32), pltpu.SemaphoreType.DMA],
        compiler_params=_params(),
    )(dx3, x2, uu_all, cc_all, g_ffn, gw, ffn_w)


def _bwd_attn(dx2, x1, q, kv, gw, g_x, after, seq, tm):
    tokens = x1.shape[0]
    n_tiles = tokens // tm
    tps = seq // tm
    n_b = tokens // seq

    def body(dx2_ref, x1_ref, q_ref, kv_ref, g_ref, gw_hbm, after_ref, dx1_ref, dx1b_ref, dq_ref, dkv_ref, dg_ref,
             wq_v, wo_v, sem):
        del after_ref
        i = pl.program_id(0)

        @pl.when(i == 0)
        def _():
            copies = _load_weight(gw_hbm, "w_q", wq_v, sem) + _load_weight(gw_hbm, "w_o", wo_v, sem)
            for cp in copies:
                cp.start()
            for cp in copies:
                cp.wait()
            dg_ref[...] = jnp.zeros_like(dg_ref)

        @pl.when(i % tps == 0)
        def _():
            dkv_ref[...] = jnp.zeros_like(dkv_ref)

        dx2v = dx2_ref[...]
        do = _dot_nt(dx2v.astype(BF16), wo_v[...]).astype(BF16)
        q = q_ref[...]
        heads = [slice(h * HEAD_DIM, (h + 1) * HEAD_DIM) for h in range(HEADS)]
        kcols = [pl.ds(h * HEAD_DIM, HEAD_DIM) for h in range(HEADS)]
        vcols = [pl.ds(D_MODEL + h * HEAD_DIM, HEAD_DIM) for h in range(HEADS)]
        scores = [_dot_nt(q[:, hd], kv_ref[:, kc]) for hd, kc in zip(heads, kcols)]
        dps = [_dot_nt(do[:, hd], kv_ref[:, vc]) for hd, vc in zip(heads, vcols)]
        probs = [_softmax_rows(s) for s in scores]
        dss = [(p * (dp - jnp.sum(dp * p, axis=-1, keepdims=True))).astype(BF16) for p, dp in zip(probs, dps)]
        for p, hd, vc in zip(probs, heads, vcols):
            dkv_ref[:, vc] += _dot_tn(p.astype(BF16), do[:, hd])
        dqs = [_dot(ds, kv_ref[:, kc]) * (HEAD_DIM ** -0.5) for ds, kc in zip(dss, kcols)]
        for ds, hd, kc in zip(dss, heads, kcols):
            dkv_ref[:, kc] += _dot_tn(ds, q[:, hd])
        dq = jnp.concatenate(dqs, axis=-1).astype(BF16)
        dq_ref[...] = dq
        dh2 = _dot_nt(dq, wq_v[...])
        xh, r = _rms_fwd(x1_ref[...])
        dg_ref[...] += _colsum(dh2 * xh)
        dx1 = dx2v + _rms_bwd(dh2, xh, r, g_ref[...])
        dx1_ref[...] = dx1
        dx1b_ref[...] = dx1.astype(BF16)

    row = lambda w: pl.BlockSpec((tm, w), lambda i: (i, 0))
    per_b = pl.BlockSpec((N_MEM, 2 * D_MODEL), lambda i: (i // tps, 0))
    return pl.pallas_call(
        body, name="bwd_attn", grid=(n_tiles,),
        in_specs=[row(D_MODEL), row(D_MODEL), row(D_MODEL), per_b, _full((1, D_MODEL)), pl.BlockSpec(memory_space=pl.ANY),
                  _full(after.shape)],
        out_specs=[row(D_MODEL), row(D_MODEL), row(D_MODEL), per_b, _full((1, D_MODEL))],
        out_shape=[jax.ShapeDtypeStruct((tokens, D_MODEL), F32), jax.ShapeDtypeStruct((tokens, D_MODEL), BF16),
                   jax.ShapeDtypeStruct((tokens, D_MODEL), BF16),
                   jax.ShapeDtypeStruct((n_b * N_MEM, 2 * D_MODEL), F32), jax.ShapeDtypeStruct((1, D_MODEL), F32)],
        scratch_shapes=[pltpu.VMEM((D_MODEL, D_MODEL), BF16), pltpu.VMEM((D_MODEL, D_MODEL), BF16), pltpu.SemaphoreType.DMA],
        compiler_params=_params(),
    )(dx2, x1, q, kv, g_x, gw, after)


def _bwd_kv(dkv, mem2d, gw, g_mem):
    rows = mem2d.shape[0]
    n_b = rows // N_MEM

    def body(dkv_ref, mem_ref, gw_hbm, dkvb_ref, dg_ref, wkv_v, sem):
        @pl.when(pl.program_id(0) == 0)
        def _():
            copies = _load_weight(gw_hbm, "w_kv", wkv_v, sem)
            for cp in copies:
                cp.start()
            for cp in copies:
                cp.wait()
            dg_ref[...] = jnp.zeros_like(dg_ref)

        dkvb = dkv_ref[...].astype(BF16)
        dkvb_ref[...] = dkvb
        dmn = _dot(dkvb, wkv_v[...])
        mh, _ = _rms_fwd(mem_ref[...])
        dg_ref[...] += _colsum(dmn * mh)

    del g_mem
    return pl.pallas_call(
        body, name="bwd_kv", grid=(n_b,),
        in_specs=[pl.BlockSpec((N_MEM, 2 * D_MODEL), lambda b: (b, 0)), pl.BlockSpec((N_MEM, D_MODEL), lambda b: (b, 0)),
                  pl.BlockSpec(memory_space=pl.ANY)],
        out_specs=[pl.BlockSpec((N_MEM, 2 * D_MODEL), lambda b: (b, 0)), _full((1, D_MODEL))],
        out_shape=[jax.ShapeDtypeStruct((rows, 2 * D_MODEL), BF16), jax.ShapeDtypeStruct((1, D_MODEL), F32)],
        scratch_shapes=[pltpu.VMEM((2 * D_MODEL, D_MODEL), BF16), pltpu.SemaphoreType.DMA],
        compiler_params=_params(),
    )(dkv, mem2d, gw)


def _bwd_mix(dx1, x2d, u_all, c_all, pooled_all, gw, g_mix, conv_w, ln_g, ln_b, pool_w, pool_scale, after, seq, tm):
    tokens = x2d.shape[0]
    n_tiles = tokens // tm
    tps = seq // tm

    def body(dx1_ref, x_ref, u_ref, c_ref, pooled_ref, gmix_ref, gw_hbm, cw_ref, lng_ref, lnb_ref, pw_ref, ps_ref,
             after_ref, dx_ref, du_ref, dgmix_ref, dcw_ref, dcb_ref, dlng_ref, dlnb_ref, dpw_ref, dps_ref,
             win_v, wout_v, dc_carry, e_carry, sem):
        del after_ref
        i = pl.program_id(0)
        t = n_tiles - 1 - i

        @pl.when(i == 0)
        def _():
            copies = _load_weight(gw_hbm, "w_in", win_v, sem) + _load_weight(gw_hbm, "w_out", wout_v, sem)
            for cp in copies:
                cp.start()
            for cp in copies:
                cp.wait()
            for ref in (dgmix_ref, dcw_ref, dcb_ref, dlng_ref, dlnb_ref, dpw_ref, dps_ref):
                ref[...] = jnp.zeros_like(ref)

        @pl.when(t % tps == tps - 1)
        def _():
            dc_carry[...] = jnp.zeros_like(dc_carry)
            e_carry[...] = jnp.zeros_like(e_carry)

        dx1v = dx1_ref[...]
        dymix = _dot_nt(dx1v.astype(BF16), wout_v[...])
        dyc, dyp = dymix[:, :D_CONV], dymix[:, D_CONV:]
        u = u_ref[...]
        val, gate = u[:, :D_CONV], u[:, D_CONV:2 * D_CONV]

        conv = c_ref[...]
        mu = jnp.mean(conv, axis=-1, keepdims=True)
        cen = conv - mu
        rs = lax.rsqrt(jnp.mean(cen * cen, axis=-1, keepdims=True) + EPS)
        chat = cen * rs
        ln = chat * lng_ref[...] + lnb_ref[...]
        sl = _sigmoid(ln)
        dln = dyc * (sl * (1.0 + ln * (1.0 - sl)))
        dlng_ref[...] += _colsum(dln * chat)
        dlnb_ref[...] += _colsum(dln)
        dchat = dln * lng_ref[...]
        dc = rs * (dchat - jnp.mean(dchat, axis=-1, keepdims=True)
                   - chat * jnp.mean(dchat * chat, axis=-1, keepdims=True))
        dcb_ref[...] += _colsum(dc)
        sg = _sigmoid(gate)
        hc = val * sg
        ext = jnp.concatenate([dc, dc_carry[...]], axis=0)
        dc_carry[...] = dc[:CONV_HALO, :]
        dhc = jnp.zeros((tm, D_CONV), F32)
        ahead_by = _sublane_shifts(ext)
        for k in range(CONV_WIDTH):
            whole, part = divmod(CONV_WIDTH - 1 - k, 8)
            tap = ahead_by[part][8 * whole:8 * whole + tm, :]
            dhc = dhc + cw_ref[k:k + 1, :] * tap
            dcw_ref[k:k + 1, :] += _colsum(hc * tap)
        du_ref[:, :D_CONV] = (dhc * sg).astype(BF16)
        du_ref[:, D_CONV:2 * D_CONV] = (dhc * val * (sg * (1.0 - sg))).astype(BF16)

        pos = lax.broadcasted_iota(jnp.int32, (tm, 1), 0) + (t % tps) * tm
        es, dpooled = [], []
        for g, w in enumerate(POOL_WINDOWS):
            cols = pl.ds(g * POOL_GROUP_DIM, POOL_GROUP_DIM)
            lo = g * POOL_GROUP_DIM
            pooled = pooled_ref[:, cols]
            pw = pw_ref[g].astype(BF16)
            dyg = dyp[:, lo:lo + POOL_GROUP_DIM]
            dps_ref[:, cols] += _colsum(dyg * _dot(pooled, pw))
            dmixed = (dyg * ps_ref[:, cols]).astype(BF16)
            dpw_ref[g] += _dot_tn(pooled, dmixed)
            dpo = _dot_nt(dmixed, pw)
            dpooled.append(dpo)
            es.append(dpo / jnp.minimum(pos + 1, w).astype(F32))
        e = jnp.concatenate(es, axis=-1)
        run = jnp.concatenate([e, e_carry[...]], axis=0)
        e_carry[...] = e[:POOL_HALO, :]
        rows = tm + POOL_HALO
        for g, w in enumerate(POOL_WINDOWS):
            lo = g * POOL_GROUP_DIM
            run = run[:, POOL_GROUP_DIM if g else 0:]
            run = run + pltpu.roll(run, rows - w // 2, 0)
            du_ref[:, 2 * D_CONV + lo:2 * D_CONV + lo + POOL_GROUP_DIM] = (
                run[:tm, :POOL_GROUP_DIM] - dpooled[g]).astype(BF16)

        dh1 = _dot(du_ref[...], win_v[...])
        xh, r = _rms_fwd(x_ref[...])
        dgmix_ref[...] += _colsum(dh1 * xh)
        dx_ref[...] = dx1v + _rms_bwd(dh1, xh, r, gmix_ref[...])

    rev = lambda w: pl.BlockSpec((tm, w), lambda i: (n_tiles - 1 - i, 0))
    return pl.pallas_call(
        body, name="bwd_mix", grid=(n_tiles,),
        in_specs=[rev(D_MODEL), rev(D_MODEL), rev(D_IN), rev(D_CONV), rev(D_POOL), _full((1, D_MODEL)),
                  pl.BlockSpec(memory_space=pl.ANY), _full((CONV_WIDTH, D_CONV)), _full((1, D_CONV)), _full((1, D_CONV)),
                  _full((4, POOL_GROUP_DIM, POOL_GROUP_DIM)), _full((1, D_POOL)), _full(after.shape)],
        out_specs=[rev(D_MODEL), rev(D_IN), _full((1, D_MODEL)), _full((CONV_WIDTH, D_CONV)), _full((1, D_CONV)),
                   _full((1, D_CONV)), _full((1, D_CONV)), _full((4, POOL_GROUP_DIM, POOL_GROUP_DIM)), _full((1, D_POOL))],
        out_shape=[jax.ShapeDtypeStruct((tokens, D_MODEL), F32), jax.ShapeDtypeStruct((tokens, D_IN), BF16),
                   jax.ShapeDtypeStruct((1, D_MODEL), F32), jax.ShapeDtypeStruct((CONV_WIDTH, D_CONV), F32),
                   jax.ShapeDtypeStruct((1, D_CONV), F32), jax.ShapeDtypeStruct((1, D_CONV), F32),
                   jax.ShapeDtypeStruct((1, D_CONV), F32),
                   jax.ShapeDtypeStruct((4, POOL_GROUP_DIM, POOL_GROUP_DIM), F32), jax.ShapeDtypeStruct((1, D_POOL), F32)],
        scratch_shapes=[pltpu.VMEM((D_IN, D_MODEL), BF16), pltpu.VMEM((D_MODEL, D_MODEL), BF16),
                        pltpu.VMEM((CONV_HALO, D_CONV), F32), pltpu.VMEM((POOL_HALO, D_POOL), F32),
                        pltpu.SemaphoreType.DMA],
        compiler_params=_params(),
    )(dx1, x2d, u_all, c_all, pooled_all, g_mix, gw, conv_w, ln_g, ln_b, pool_w, pool_scale, after)


def _wgrad(a, b, name, after=None):
    tokens, m = a.shape
    n = b.shape[1]
    tm = 512 if m % 512 == 0 else 256
    extra = [] if after is None else [after]

    def body(a_ref, b_ref, *rest):
        rest[-1][...] = _dot_tn(a_ref[...], b_ref[...]).astype(rest[-1].dtype)

    return pl.pallas_call(
        body, name=name, grid=(m // tm,),
        in_specs=[pl.BlockSpec((tokens, tm), lambda i: (0, i)), _full((tokens, n))] + [_full(t.shape) for t in extra],
        out_specs=pl.BlockSpec((tm, n), lambda i: (i, 0)),
        out_shape=jax.ShapeDtypeStruct((m, n), BF16),
        compiler_params=_params(),
    )(a, b, *extra)


def _adamw_update(w, g, m, v):
    nm = ADAM_B1 * m + (1.0 - ADAM_B1) * g
    nv = ADAM_B2 * v + (1.0 - ADAM_B2) * (g * g)
    m_hat = nm / (1.0 - ADAM_B1 ** ADAM_STEP)
    v_hat = nv / (1.0 - ADAM_B2 ** ADAM_STEP)
    return -ADAM_LR * (m_hat / (jnp.sqrt(v_hat) + ADAM_EPS) + ADAM_WD * w), nm, nv


def _adamw_small(ws, gs, ms, vs):
    n = len(ws)

    def body(*refs):
        ins, outs = refs[:4 * n], refs[4 * n:]
        for k in range(n):
            d, nm, nv = _adamw_update(*[ins[j * n + k][...] for j in range(4)])
            outs[k][...] = d
            outs[n + k][...] = nm
            outs[2 * n + k][...] = nv

    vmem = pl.BlockSpec(memory_space=pltpu.VMEM)
    outs = pl.pallas_call(
        body, name="adamw_small",
        in_specs=[vmem] * (4 * n), out_specs=[vmem] * (3 * n),
        out_shape=[jax.ShapeDtypeStruct(w.shape, F32) for w in ws] * 3,
    )(*ws, *gs, *ms, *vs)
    return outs[:n], outs[n:2 * n], outs[2 * n:]


def _adamw(w, g, m, v, name):
    rows, cols = w.shape
    tile = rows
    for cand in (512, 256, 128, 64, 32, 16, 8):
        if rows % cand == 0:
            tile = cand
            break

    def body(w_ref, g_ref, m_ref, v_ref, d_ref, nm_ref, nv_ref):
        d_ref[...], nm_ref[...], nv_ref[...] = _adamw_update(w_ref[...], g_ref[...], m_ref[...], v_ref[...])

    spec = pl.BlockSpec((tile, cols), lambda i: (i, 0))
    return pl.pallas_call(
        body, name=name, grid=(rows // tile,),
        in_specs=[spec] * 4, out_specs=[spec] * 3,
        out_shape=[jax.ShapeDtypeStruct((rows, cols), F32)] * 3,
        compiler_params=_params(("arbitrary",)),
    )(w, g, m, v)


SMALL = (("norm_mix_g", (1, 1024)), ("conv_dw_b", (1, 512)), ("conv_ln_g", (1, 512)), ("conv_ln_b", (1, 512)),
         ("pool_w", (1, 4, 128, 128)), ("pool_scale", (1, 512)), ("norm_xattn_g", (1, 1024)), ("norm_mem_g", (1, 1024)),
         ("norm_ffn_g", (1, 1024)), ("ffn_dw_b", (1, 5632)), ("norm_final_g", (1024,)))
LANES = 128


def _pack_rows(arrs):
    flat = jnp.concatenate([a.reshape(-1) for a in arrs])
    pad = (-flat.shape[0]) % (8 * LANES)
    return jnp.pad(flat, (0, pad)).reshape(-1, LANES)


def kernel(x, mem, norm_mix_g, w_in, conv_dw_w, conv_dw_b, conv_ln_g, conv_ln_b, pool_w, pool_scale, w_out, norm_xattn_g, norm_mem_g, w_q, w_kv, w_o, norm_ffn_g, w_up, ffn_dw_w, ffn_dw_b, w_down, norm_final_g, loss_target, m_norm_mix_g, m_w_in, m_conv_dw_w, m_conv_dw_b, m_conv_ln_g, m_conv_ln_b, m_pool_w, m_pool_scale, m_w_out, m_norm_xattn_g, m_norm_mem_g, m_w_q, m_w_kv, m_w_o, m_norm_ffn_g, m_w_up, m_ffn_dw_w, m_ffn_dw_b, m_w_down, m_norm_final_g, v_norm_mix_g, v_w_in, v_conv_dw_w, v_conv_dw_b, v_conv_ln_g, v_conv_ln_b, v_pool_w, v_pool_scale, v_w_out, v_norm_xattn_g, v_norm_mem_g, v_w_q, v_w_kv, v_w_o, v_norm_ffn_g, v_w_up, v_ffn_dw_w, v_ffn_dw_b, v_w_down, v_norm_final_g):
    weights = dict(norm_mix_g=norm_mix_g, w_in=w_in, conv_dw_w=conv_dw_w, conv_dw_b=conv_dw_b, conv_ln_g=conv_ln_g,
                   conv_ln_b=conv_ln_b, pool_w=pool_w, pool_scale=pool_scale, w_out=w_out, norm_xattn_g=norm_xattn_g,
                   norm_mem_g=norm_mem_g, w_q=w_q, w_kv=w_kv, w_o=w_o, norm_ffn_g=norm_ffn_g, w_up=w_up,
                   ffn_dw_w=ffn_dw_w, ffn_dw_b=ffn_dw_b, w_down=w_down, norm_final_g=norm_final_g)
    moments_m = dict(norm_mix_g=m_norm_mix_g, w_in=m_w_in, conv_dw_w=m_conv_dw_w, conv_dw_b=m_conv_dw_b,
                     conv_ln_g=m_conv_ln_g, conv_ln_b=m_conv_ln_b, pool_w=m_pool_w, pool_scale=m_pool_scale,
                     w_out=m_w_out, norm_xattn_g=m_norm_xattn_g, norm_mem_g=m_norm_mem_g, w_q=m_w_q, w_kv=m_w_kv,
                     w_o=m_w_o, norm_ffn_g=m_norm_ffn_g, w_up=m_w_up, ffn_dw_w=m_ffn_dw_w, ffn_dw_b=m_ffn_dw_b,
                     w_down=m_w_down, norm_final_g=m_norm_final_g)
    moments_v = dict(norm_mix_g=v_norm_mix_g, w_in=v_w_in, conv_dw_w=v_conv_dw_w, conv_dw_b=v_conv_dw_b,
                     conv_ln_g=v_conv_ln_g, conv_ln_b=v_conv_ln_b, pool_w=v_pool_w, pool_scale=v_pool_scale,
                     w_out=v_w_out, norm_xattn_g=v_norm_xattn_g, norm_mem_g=v_norm_mem_g, w_q=v_w_q, w_kv=v_w_kv,
                     w_o=v_w_o, norm_ffn_g=v_norm_ffn_g, w_up=v_w_up, ffn_dw_w=v_ffn_dw_w, ffn_dw_b=v_ffn_dw_b,
                     w_down=v_w_down, norm_final_g=v_norm_final_g)
    order = list(weights)
    transposed = ("w_in", "w_kv", "w_up")

    n_b, seq, _ = x.shape
    tokens = n_b * seq
    tm_mix = min(512, seq // 2)
    tm_ffn = min(256, seq // 2)
    dev = 4 * lax.axis_index("x") + 2 * lax.axis_index("y") + lax.axis_index("c")

    packs = [jnp.concatenate([weights[n][0].T if n in transposed else weights[n][0] for n in names], axis=0).astype(BF16)
             for names in AG_GROUPS]
    small_sharded = _pack_rows([conv_dw_w[0], ffn_dw_w[0]])
    gw_mix, gsmall = _all_gather([packs[0], small_sharded], "weights_all_gather")
    flights = []
    after = gw_mix
    for k in (1, 2):
        own_in_place = lax.dynamic_update_slice(lax.empty((N_DEV,) + packs[k].shape, BF16), packs[k][None], (dev, 0, 0))
        flights.append(_gather_start(own_in_place, after, "weights_gather_start_%d" % k))
        after = flights[-1][3]
    gflat = gsmall.reshape(N_DEV, -1)
    n_cw = CONV_WIDTH * (D_CONV // N_DEV)
    n_fw = FFN_CONV_WIDTH * (2 * D_FF // N_DEV)
    conv_w = gflat[:, :n_cw].reshape(N_DEV, CONV_WIDTH, D_CONV // N_DEV).transpose(1, 0, 2).reshape(CONV_WIDTH, D_CONV)
    ffn_w = gflat[:, n_cw:n_cw + n_fw].reshape(N_DEV, FFN_CONV_WIDTH, 2 * D_FF // N_DEV).transpose(1, 0, 2).reshape(
        FFN_CONV_WIDTH, 2 * D_FF)

    x2d = x.reshape(tokens, D_MODEL)
    mem2d = mem.reshape(n_b * N_MEM, D_MODEL)
    tgt2d = loss_target.reshape(tokens, D_MODEL)
    g_final = norm_final_g.reshape(1, D_MODEL)

    def gather_finish(flight, after, tag):
        fwd_send, fwd_recv, buf = _gather_forward(*flight[:3], after, "weights_gather_forward_" + tag)
        return _gather_finish(fwd_send, fwd_recv, buf, "weights_gather_finish_" + tag)

    x1, u_all, c_all, pooled_all, ymix, h1 = _fwd_mix(
        x2d, gw_mix, norm_mix_g, conv_w, conv_dw_b, conv_ln_g, conv_ln_b, pool_w[0], pool_scale, flights[1][3],
        seq, tm_mix)
    gw_attn = gather_finish(flights[0], x1, "1")
    mem_n, kv = _fwd_kv(mem2d, gw_attn, norm_mem_g)
    x2, h2, q, o = _fwd_attn(x1, kv, gw_attn, norm_xattn_g, seq, tm_mix)
    gw_ffn = gather_finish(flights[1], x2, "2")
    uu_all, cc_all, a_all, h3, dx3, dx3b, loss_part, dg_final = _fwd_ffn(
        x2, tgt2d, gw_ffn, norm_ffn_g, ffn_w, ffn_dw_b, g_final, seq, tm_ffn)

    table = _owner_table()

    def sibling_start(names, tag):
        parts = [part[n].reshape(N_DEV, W_OFF[n][1], D_MODEL) for n in names]
        return _exchange_start(parts, 4, _to_sibling, "rs_sibling_exchange_start_" + tag)

    def chips_start(flight, after, tag):
        parts, landed = _exchange_wait(*flight[:4], after, 4, _to_sibling, "rs_sibling_exchange_wait_" + tag)
        sums = _chip_partial_sums(table, parts, landed, "rs_chip_partial_sums_" + tag)
        return parts, landed, _exchange_start(sums, 3, _to_chip, "rs_chip_exchange_start_" + tag)

    grads, delta, new_m, new_v = {}, {}, {}, {}

    def reduce_finish(names, parts, landed, flight, after, tag):
        _, from_chips = _exchange_wait(*flight[:4], after, 3, _to_chip, "rs_chip_exchange_wait_" + tag)
        as_rows = {n: n not in transposed or W_OFF[n][1] % LANES != 0 for n in names}
        states = [tuple(t[n][0].T if n in transposed else t[n][0] for t in (weights, moments_m, moments_v))
                  if as_rows[n] else None for n in names]
        results = _final_update(table, parts, landed, from_chips, states, "rs_final_update_" + tag)
        for n, res in zip(names, results):
            back = (lambda t: t.T[None]) if n in transposed else (lambda t: t[None])
            grads[n] = back(res[0])
            if as_rows[n]:
                delta[n], new_m[n], new_v[n] = [back(t) for t in res[1:]]
            else:
                delta[n], new_m[n], new_v[n] = [t[None] for t in _adamw(
                    weights[n][0], grads[n][0], moments_m[n][0], moments_v[n][0], "adamw_" + n)]
        alone = [n for n in names if not as_rows[n]]
        return delta[alone[-1] if alone else names[-1]]

    part = {}
    dx2, dx2b, duu, d_ffn_b, d_ffn_w, dg_ffn = _bwd_ffn(dx3, x2, uu_all, cc_all, gw_ffn, norm_ffn_g, ffn_w, seq, tm_ffn)
    part["w_up"] = _wgrad(duu, h3, "wgrad_w_up")
    part["w_down"] = _wgrad(a_all, dx3b, "wgrad_w_down")
    to_sibling_a = sibling_start(RS_GROUPS["a"], "a")
    dx1, dx1b, dq, dkv, dg_x = _bwd_attn(dx2, x1, q, kv, gw_attn, norm_xattn_g, to_sibling_a[4], seq, tm_mix)
    parts_a, landed_a, flight_a = chips_start(to_sibling_a, dx1, "a")
    dkv_b, dg_mem = _bwd_kv(dkv, mem2d, gw_attn, norm_mem_g)
    part["w_q"] = _wgrad(h2, dq, "wgrad_w_q", after=flight_a[4])
    part["w_kv"] = _wgrad(dkv_b, mem_n, "wgrad_w_kv")
    part["w_o"] = _wgrad(o, dx2b, "wgrad_w_o")
    to_sibling_b = sibling_start(RS_GROUPS["b"], "b")
    parts_b, landed_b, flight_b = chips_start(to_sibling_b, to_sibling_b[4], "b")
    dx, du, dg_mix, d_conv_w, d_conv_b, d_ln_g, d_ln_b, d_pool_w, d_pool_scale = _bwd_mix(
        dx1, x2d, u_all, c_all, pooled_all, gw_mix, norm_mix_g, conv_w, conv_ln_g, conv_ln_b, pool_w[0], pool_scale,
        flight_b[4], seq, tm_mix)
    grad_x = dx.reshape(x.shape)

    small_grads = dict(norm_mix_g=dg_mix, conv_dw_b=d_conv_b, conv_ln_g=d_ln_g, conv_ln_b=d_ln_b, pool_w=d_pool_w,
                       pool_scale=d_pool_scale, norm_xattn_g=dg_x, norm_mem_g=dg_mem, norm_ffn_g=dg_ffn,
                       ffn_dw_b=d_ffn_b, norm_final_g=dg_final)
    small_list = [small_grads[n] for n, _ in SMALL] + [d_conv_w, d_ffn_w, loss_part[:1]]
    small_mine = _pack_rows(small_list)
    small_flight = _broadcast_start(
        lax.dynamic_update_slice(lax.empty((N_DEV,) + small_mine.shape, F32), small_mine[None], (dev, 0, 0)),
        "small_grads_broadcast_start")

    part["w_in"] = _wgrad(du, h1, "wgrad_w_in", after=small_flight[3])
    part["w_out"] = _wgrad(ymix, dx1b, "wgrad_w_out")
    to_sibling_c = sibling_start(RS_GROUPS["c"], "c")
    parts_c, landed_c, flight_c = chips_start(to_sibling_c, to_sibling_c[4], "c")
    updated_a = reduce_finish(RS_GROUPS["a"], parts_a, landed_a, flight_a, flight_c[4], "a")
    updated_b = reduce_finish(RS_GROUPS["b"], parts_b, landed_b, flight_b, updated_a, "b")
    small_all = _broadcast_wait(*small_flight[:3], updated_b, "small_grads_broadcast_wait")
    small_sum = _sum_blocks(small_all).reshape(-1)

    pos = 0
    for n, shape in SMALL:
        size = 1
        for s in shape:
            size *= s
        grads[n] = small_sum[pos:pos + size].reshape(shape)
        pos += size
    full_conv_w = small_sum[pos:pos + CONV_WIDTH * D_CONV].reshape(CONV_WIDTH, D_CONV)
    pos += CONV_WIDTH * D_CONV
    full_ffn_w = small_sum[pos:pos + FFN_CONV_WIDTH * 2 * D_FF].reshape(FFN_CONV_WIDTH, 2 * D_FF)
    loss = small_sum[pos + FFN_CONV_WIDTH * 2 * D_FF]
    grads["conv_dw_w"] = lax.dynamic_slice_in_dim(full_conv_w, dev * (D_CONV // N_DEV), D_CONV // N_DEV, axis=1)[None]
    grads["ffn_dw_w"] = lax.dynamic_slice_in_dim(full_ffn_w, dev * (2 * D_FF // N_DEV), 2 * D_FF // N_DEV, axis=1)[None]

    small_names = [n for n in order if n not in W_OFF]
    swap = lambda t: jnp.transpose(t, (1, 0, 2))
    two_d = lambda t: t.reshape(1, -1) if t.ndim == 1 else (swap(t) if t.ndim == 3 else t)
    outs = _adamw_small(*[[two_d(t[n]) for n in small_names] for t in (weights, grads, moments_m, moments_v)])
    for res, out in zip((delta, new_m, new_v), outs):
        for n, o in zip(small_names, out):
            res[n] = swap(o) if o.ndim == 3 else o.reshape(weights[n].shape)

    reduce_finish(RS_GROUPS["c"], parts_c, landed_c, flight_c, delta[small_names[-1]], "c")

    return (loss, grad_x, *[grads[n] for n in order], *[delta[n] for n in order],
            *[new_m[n] for n in order], *[new_v[n] for n in order])
```

```python
import functools

import jax
import jax.numpy as jnp
from jax import lax
from jax.experimental import pallas as pl
from jax.experimental.pallas import tpu as pltpu

F32 = jnp.float32
BF16 = jnp.bfloat16
MESH = pl.DeviceIdType.MESH

N_DEV = 8
D_MODEL = 1024
D_CONV = 512
D_POOL = 512
CONV_WIDTH = 31
POOL_WINDOWS = (2, 4, 8, 16)
POOL_GROUP_DIM = 128
D_IN = 1536
N_MEM = 256
HEADS = 4
HEAD_DIM = 256
D_FF = 2816
FFN_CONV_WIDTH = 3
EPS = 1e-6
ADAM_LR = 0.001
ADAM_B1 = 0.9
ADAM_B2 = 0.999
ADAM_EPS = 1e-08
ADAM_WD = 0.01
ADAM_STEP = 10

VMEM_LIMIT_V7X = 56 * 1024 * 1024
CONV_HALO = 32
POOL_HALO = 16
FFN_HALO = 8
FFN_CHUNK = 2816

W_ROWS = (("w_in", 192), ("w_out", 128), ("w_q", 128), ("w_kv", 256), ("w_o", 128), ("w_up", 704), ("w_down", 352))
AG_GROUPS = (("w_in", "w_out"), ("w_q", "w_kv", "w_o"), ("w_up", "w_down"))
W_OFF = {}
for _names in AG_GROUPS:
    _o = 0
    for _n in _names:
        W_OFF[_n] = (_o, dict(W_ROWS)[_n])
        _o += dict(W_ROWS)[_n]
RS_GROUPS = {"a": ("w_up", "w_down"), "b": ("w_q", "w_kv", "w_o"), "c": ("w_in", "w_out")}
BARRIER_IDS = {"gather_start": (0, 1), "gather_forward": (2, 3), "sibling": {"a": 4, "b": 5, "c": 6},
               "chips": {"a": 7, "b": 8, "c": 9}, "broadcast": 10}


def _dot(a, b):
    return jnp.dot(a, b, preferred_element_type=F32)


def _dot_nt(a, b):
    return lax.dot_general(a, b, (((1,), (1,)), ((), ())), preferred_element_type=F32)


def _dot_tn(a, b):
    return lax.dot_general(a, b, (((0,), (0,)), ((), ())), preferred_element_type=F32)


def _sigmoid(v):
    return 1.0 / (1.0 + jnp.exp(-v))


def _rms_fwd(v):
    r = lax.rsqrt(jnp.mean(v * v, axis=-1, keepdims=True) + EPS)
    return v * r, r


def _rms_bwd(dh, vh, r, g):
    gd = dh * g
    return r * (gd - vh * jnp.mean(gd * vh, axis=-1, keepdims=True))


def _sublane_shifts(v):
    rows = v.shape[0]
    return [v] + [pltpu.roll(v, rows - b, 0) for b in range(1, 8)]


def _colsum(v):
    return jnp.sum(v, axis=0, keepdims=True)


def _full(shape):
    return pl.BlockSpec(shape, lambda *_: (0,) * len(shape))


def _params(sem=("arbitrary",), vmem=VMEM_LIMIT_V7X):
    return pltpu.CompilerParams(dimension_semantics=sem, vmem_limit_bytes=vmem)


def _load_weight(g_hbm, name, dst, sem):
    off, rows = W_OFF[name]
    return [pltpu.make_async_copy(g_hbm.at[d, pl.ds(off, rows), :], dst.at[pl.ds(d * rows, rows), :], sem)
            for d in range(N_DEV)]


def _position():
    x, y, c = lax.axis_index("x"), lax.axis_index("y"), lax.axis_index("c")
    chips = [(1 - x, y), (x, 1 - y), (1 - x, 1 - y)]
    return x, y, c, chips


def _dev(px, py, pc):
    return 4 * px + 2 * py + pc


def _all_gather(arrs, name):
    n = len(arrs)

    def body(*refs):
        ins, outs = refs[:n], refs[n:2 * n]
        send_sems, recv_sems, local_sems = refs[2 * n:2 * n + 3]
        bounce = refs[2 * n + 3:]
        x, y, c, chips = _position()
        me, sibling = (x, y, c), (x, y, 1 - c)

        def copy(a, k, block, to, src=None):
            rows = outs[a].at[_dev(*block)]
            return pltpu.make_async_remote_copy(
                src_ref=rows if src is None else src, dst_ref=rows,
                send_sem=send_sems.at[a, k], recv_sem=recv_sems.at[a, k], device_id=to, device_id_type=MESH)

        sends = []
        for a in range(n):
            first = [copy(a, 0, me, sibling, src=ins[a])]
            first += [copy(a, 1 + j, me, (*chip, c), src=ins[a]) for j, chip in enumerate(chips)]
            for cp in first:
                cp.start()
            sends += first
        started = []
        for a in range(n):
            load = pltpu.make_async_copy(ins[a], bounce[a], local_sems.at[a, 0])
            load.start()
            load.wait()
            mine = pltpu.make_async_copy(bounce[a], outs[a].at[_dev(*me)], local_sems.at[a, 1])
            mine.start()
            started.append(mine)
        for j, chip in enumerate(chips):
            for a in range(n):
                copy(a, 1 + j, (*chip, c), me).wait_recv()
                passed = copy(a, 4 + j, (*chip, c), sibling)
                passed.start()
                sends.append(passed)
        for a in range(n):
            copy(a, 0, sibling, me).wait_recv()
            for j, chip in enumerate(chips):
                copy(a, 4 + j, (*chip, 1 - c), me).wait_recv()
        for cp in sends:
            cp.wait_send()
        for mine in started:
            mine.wait()

    any_spec = pl.BlockSpec(memory_space=pl.ANY)
    return pl.pallas_call(
        body, name=name,
        out_shape=[jax.ShapeDtypeStruct((N_DEV,) + a.shape, a.dtype) for a in arrs],
        in_specs=[any_spec] * n, out_specs=[any_spec] * n,
        scratch_shapes=[pltpu.SemaphoreType.DMA((n, 7)), pltpu.SemaphoreType.DMA((n, 7)), pltpu.SemaphoreType.DMA((n, 2))]
        + [pltpu.VMEM(a.shape, a.dtype) for a in arrs],
    )(*arrs)


_HBM = pl.BlockSpec(memory_space=pltpu.HBM)
_SEM = pl.BlockSpec(memory_space=pltpu.SEMAPHORE)
_SIDE_EFFECT = pltpu.SideEffectType.DATAFLOW_SIDE_EFFECTING


def _handshake(peers):
    barrier = pltpu.get_barrier_semaphore()
    for peer in peers:
        pl.semaphore_signal(barrier, inc=1, device_id=peer, device_id_type=MESH)
    pl.semaphore_wait(barrier, len(peers))


def _gather_start(buf, after, name, collective_id):
    def body(buf_ref, after_ref, send_sems, recv_sems, buf_thru, token):
        del after_ref, buf_thru
        x, y, c, chips = _position()
        rows = buf_ref.at[_dev(x, y, c)]
        targets = [(x, y, 1 - c)] + [(*chip, c) for chip in chips]
        _handshake(targets)
        for k, to in enumerate(targets):
            pltpu.make_async_remote_copy(src_ref=rows, dst_ref=rows, send_sem=send_sems.at[k], recv_sem=recv_sems.at[k],
                                         device_id=to, device_id_type=MESH).start()
        token[...] = jnp.zeros_like(token)

    return pl.pallas_call(
        body, name=name,
        out_shape=(pltpu.SemaphoreType.DMA((4,)), pltpu.SemaphoreType.DMA((4,)), pltpu.HBM(buf.shape, buf.dtype),
                   jax.ShapeDtypeStruct((8, 128), F32)),
        in_specs=(_HBM, pl.BlockSpec(memory_space=pl.ANY)),
        out_specs=(_SEM, _SEM, _HBM, pl.BlockSpec(memory_space=pltpu.VMEM)),
        input_output_aliases={0: 2},
        compiler_params=pltpu.CompilerParams(has_side_effects=_SIDE_EFFECT, collective_id=collective_id),
    )(pltpu.with_memory_space_constraint(buf, pltpu.HBM), after)


def _gather_forward(send_sems, recv_sems, buf, after, name, collective_id):
    def body(buf_ref, send_sems, recv_sems, after_ref, fwd_send, fwd_recv, buf_thru):
        del after_ref, buf_thru
        x, y, c, chips = _position()
        sibling = (x, y, 1 - c)

        def copy(block, k, sends, recvs):
            rows = buf_ref.at[_dev(*block)]
            return pltpu.make_async_remote_copy(src_ref=rows, dst_ref=rows, send_sem=sends.at[k], recv_sem=recvs.at[k],
                                                device_id=sibling, device_id_type=MESH)

        _handshake([sibling])
        for k in range(4):
            copy((x, y, c), k, send_sems, recv_sems).wait_send()
        copy(sibling, 0, send_sems, recv_sems).wait_recv()
        for j, chip in enumerate(chips):
            copy((*chip, c), 1 + j, send_sems, recv_sems).wait_recv()
            copy((*chip, c), j, fwd_send, fwd_recv).start()

    return pl.pallas_call(
        body, name=name,
        out_shape=(pltpu.SemaphoreType.DMA((3,)), pltpu.SemaphoreType.DMA((3,)), pltpu.HBM(buf.shape, buf.dtype)),
        in_specs=(_HBM, _SEM, _SEM, pl.BlockSpec(memory_space=pl.ANY)), out_specs=(_SEM, _SEM, _HBM),
        input_output_aliases={0: 2},
        compiler_params=pltpu.CompilerParams(has_side_effects=_SIDE_EFFECT, collective_id=collective_id),
    )(buf, send_sems, recv_sems, after)


def _gather_finish(fwd_send, fwd_recv, buf, name):
    def body(buf_ref, fwd_send, fwd_recv, buf_thru):
        del buf_thru
        x, y, c, chips = _position()
        for j, chip in enumerate(chips):
            cp = pltpu.make_async_remote_copy(
                src_ref=buf_ref.at[_dev(*chip, c)], dst_ref=buf_ref.at[_dev(*chip, 1 - c)], send_sem=fwd_send.at[j],
                recv_sem=fwd_recv.at[j], device_id=(x, y, 1 - c), device_id_type=MESH)
            cp.wait_send()
            cp.wait_recv()

    return pl.pallas_call(
        body, name=name,
        out_shape=pltpu.HBM(buf.shape, buf.dtype),
        in_specs=(_HBM, _SEM, _SEM), out_specs=_HBM,
        input_output_aliases={0: 0},
        compiler_params=pltpu.CompilerParams(has_side_effects=_SIDE_EFFECT),
    )(buf, fwd_send, fwd_recv)


def _everyone_else(x, y, c, chips):
    return [(x, y, 1 - c)] + [(*chip, core) for chip in chips for core in (c, 1 - c)]


def _broadcast_start(buf, name, collective_id):
    def body(buf_ref, send_sems, recv_sems, buf_thru, token):
        del buf_thru
        x, y, c, chips = _position()
        rows = buf_ref.at[_dev(x, y, c)]
        _handshake(_everyone_else(x, y, c, chips))
        for k, to in enumerate(_everyone_else(x, y, c, chips)):
            pltpu.make_async_remote_copy(src_ref=rows, dst_ref=rows, send_sem=send_sems.at[k], recv_sem=recv_sems.at[k],
                                         device_id=to, device_id_type=MESH).start()
        token[...] = jnp.zeros_like(token)

    return pl.pallas_call(
        body, name=name,
        out_shape=(pltpu.SemaphoreType.DMA((7,)), pltpu.SemaphoreType.DMA((7,)), pltpu.HBM(buf.shape, buf.dtype),
                   jax.ShapeDtypeStruct((8, 128), F32)),
        in_specs=(_HBM,), out_specs=(_SEM, _SEM, _HBM, pl.BlockSpec(memory_space=pltpu.VMEM)),
        input_output_aliases={0: 2},
        compiler_params=pltpu.CompilerParams(has_side_effects=_SIDE_EFFECT, collective_id=collective_id),
    )(pltpu.with_memory_space_constraint(buf, pltpu.HBM))


def _broadcast_wait(send_sems, recv_sems, buf, after, name):
    def body(buf_ref, send_sems, recv_sems, after_ref, buf_thru):
        del after_ref, buf_thru
        x, y, c, chips = _position()
        for k, peer in enumerate(_everyone_else(x, y, c, chips)):
            cp = pltpu.make_async_remote_copy(
                src_ref=buf_ref.at[_dev(x, y, c)], dst_ref=buf_ref.at[_dev(*peer)], send_sem=send_sems.at[k],
                recv_sem=recv_sems.at[k], device_id=peer, device_id_type=MESH)
            cp.wait_send()
            cp.wait_recv()

    return pl.pallas_call(
        body, name=name,
        out_shape=pltpu.HBM(buf.shape, buf.dtype),
        in_specs=(_HBM, _SEM, _SEM, pl.BlockSpec(memory_space=pl.ANY)), out_specs=_HBM,
        input_output_aliases={0: 0},
        compiler_params=pltpu.CompilerParams(has_side_effects=_SIDE_EFFECT),
    )(buf, send_sems, recv_sems, after)


def _to_sibling(j, x, y, c, chips):
    return _dev(*([(x, y)] + chips)[j], 1 - c), (x, y, 1 - c)


def _to_chip(j, x, y, c, chips):
    return j, (*chips[j], c)


def _exchange_start(srcs, n_slots, route, name, collective_id):
    n = len(srcs)

    def body(*refs):
        s_refs, land_refs = refs[:n], refs[n:2 * n]
        send_sems, recv_sems = refs[2 * n:2 * n + 2]
        token = refs[-1]
        x, y, c, chips = _position()
        _handshake([(x, y, 1 - c)] if route is _to_sibling else [route(j, x, y, c, chips)[1] for j in range(n_slots)])
        for k in range(n):
            for j in range(n_slots):
                block, to = route(j, x, y, c, chips)
                pltpu.make_async_remote_copy(
                    src_ref=s_refs[k].at[block], dst_ref=land_refs[k].at[j], send_sem=send_sems.at[n_slots * k + j],
                    recv_sem=recv_sems.at[n_slots * k + j], device_id=to, device_id_type=MESH).start()
        token[...] = jnp.zeros_like(token)

    lands = [jax.ShapeDtypeStruct((n_slots,) + s.shape[1:], s.dtype) for s in srcs]
    outs = pl.pallas_call(
        body, name=name,
        out_shape=(pltpu.SemaphoreType.DMA((n_slots * n,)), pltpu.SemaphoreType.DMA((n_slots * n,)),
                   *[pltpu.HBM(s.shape, s.dtype) for s in srcs], *[pltpu.HBM(l.shape, l.dtype) for l in lands],
                   jax.ShapeDtypeStruct((8, 128), F32)),
        in_specs=[_HBM] * (2 * n), out_specs=(_SEM, _SEM, *[_HBM] * (2 * n), pl.BlockSpec(memory_space=pltpu.VMEM)),
        input_output_aliases={k: 2 + k for k in range(2 * n)},
        compiler_params=pltpu.CompilerParams(has_side_effects=_SIDE_EFFECT, collective_id=collective_id),
    )(*[pltpu.with_memory_space_constraint(s, pltpu.HBM) for s in srcs],
      *[pltpu.with_memory_space_constraint(lax.empty(l.shape, l.dtype), pltpu.HBM) for l in lands])
    return outs[0], outs[1], outs[2:2 + n], outs[2 + n:2 + 2 * n], outs[-1]


def _exchange_wait(send_sems, recv_sems, s_thru, land_thru, after, n_slots, route, name):
    n = len(s_thru)

    def body(*refs):
        s_refs, land_refs = refs[:n], refs[n:2 * n]
        send_sems, recv_sems = refs[2 * n:2 * n + 2]
        x, y, c, chips = _position()
        for k in range(n):
            for j in range(n_slots):
                block, to = route(j, x, y, c, chips)
                cp = pltpu.make_async_remote_copy(
                    src_ref=s_refs[k].at[block], dst_ref=land_refs[k].at[j], send_sem=send_sems.at[n_slots * k + j],
                    recv_sem=recv_sems.at[n_slots * k + j], device_id=to, device_id_type=MESH)
                cp.wait_send()
                cp.wait_recv()

    outs = pl.pallas_call(
        body, name=name,
        out_shape=(*[pltpu.HBM(s.shape, s.dtype) for s in s_thru], *[pltpu.HBM(l.shape, l.dtype) for l in land_thru]),
        in_specs=[_HBM] * (2 * n) + [_SEM, _SEM, pl.BlockSpec(memory_space=pl.ANY)], out_specs=[_HBM] * (2 * n),
        input_output_aliases={k: k for k in range(2 * n)},
        compiler_params=pltpu.CompilerParams(has_side_effects=_SIDE_EFFECT),
    )(*s_thru, *land_thru, send_sems, recv_sems, after)
    return outs[:n], outs[n:]


def _owner_table():
    x, y, c = lax.axis_index("x"), lax.axis_index("y"), lax.axis_index("c")
    chips = [(x, y), (1 - x, y), (x, 1 - y), (1 - x, 1 - y)]
    return jnp.stack([_dev(px, py, c) for px, py in chips]).astype(jnp.int32)


def _chip_partial_sums(table, parts, from_sibling, name):
    n = len(parts)

    def body(tab_ref, *refs):
        del tab_ref
        for g_ref, l_ref, out_ref in zip(refs[:n], refs[n:2 * n], refs[2 * n:]):
            out_ref[...] = (g_ref[...].astype(F32) + l_ref[...].astype(F32)).astype(out_ref.dtype)

    block = lambda p: (None,) + p.shape[1:]
    grid_spec = pltpu.PrefetchScalarGridSpec(
        num_scalar_prefetch=1, grid=(3,),
        in_specs=[pl.BlockSpec(block(p), lambda j, tab: (tab[j + 1], 0, 0)) for p in parts]
        + [pl.BlockSpec(block(p), lambda j, tab: (j + 1, 0, 0)) for p in parts],
        out_specs=[pl.BlockSpec(block(p), lambda j, tab: (j, 0, 0)) for p in parts])
    return pl.pallas_call(
        body, name=name, grid_spec=grid_spec,
        out_shape=[jax.ShapeDtypeStruct((3,) + p.shape[1:], BF16) for p in parts],
        compiler_params=_params(("arbitrary",)),
    )(table, *parts, *from_sibling)


def _final_update(table, parts, from_sibling, from_chips, states, name):
    n = len(parts)
    updated = [k for k in range(n) if states[k] is not None]

    def body(tab_ref, *refs):
        del tab_ref
        ins, outs = refs[:3 * n + 3 * len(updated)], list(refs[3 * n + 3 * len(updated):])
        wmv = list(ins[3 * n:])
        for k in range(n):
            acc = ins[k][...].astype(F32) + ins[n + k][...].astype(F32)
            for j in range(3):
                acc = acc + ins[2 * n + k][j].astype(F32)
            outs.pop(0)[...] = acc
            if k in updated:
                w_ref, m_ref, v_ref = wmv[:3]
                del wmv[:3]
                for out_ref, val in zip(outs[:3], _adamw_update(w_ref[...], acc, m_ref[...], v_ref[...])):
                    out_ref[...] = val
                del outs[:3]

    half = lambda p: (p.shape[1] // 2, p.shape[2])
    rows = lambda p: pl.BlockSpec(half(p), lambda t, tab: (t, 0))
    grid_spec = pltpu.PrefetchScalarGridSpec(
        num_scalar_prefetch=1, grid=(2,),
        in_specs=[pl.BlockSpec((None,) + half(p), lambda t, tab: (tab[0], t, 0)) for p in parts]
        + [pl.BlockSpec((None,) + half(p), lambda t, tab: (0, t, 0)) for p in parts]
        + [pl.BlockSpec((3,) + half(p), lambda t, tab: (0, t, 0)) for p in parts]
        + [rows(parts[k]) for k in updated for _ in range(3)],
        out_specs=[rows(parts[k]) for k in range(n) for _ in range(4 if k in updated else 1)])
    outs = pl.pallas_call(
        body, name=name, grid_spec=grid_spec,
        out_shape=[jax.ShapeDtypeStruct(parts[k].shape[1:], F32) for k in range(n) for _ in range(4 if k in updated else 1)],
        compiler_params=_params(("arbitrary",)),
    )(table, *parts, *from_sibling, *from_chips, *[t for k in updated for t in states[k]])
    result = []
    for k in range(n):
        count = 4 if k in updated else 1
        result.append(outs[:count])
        outs = outs[count:]
    return result


def _sum_blocks(g8):
    _, rows, cols = g8.shape

    def body(g_ref, out_ref):
        acc = g_ref[0]
        for d in range(1, N_DEV):
            acc = acc + g_ref[d]
        out_ref[...] = acc

    return pl.pallas_call(
        body, name="small_grad_sum", grid=(1,),
        in_specs=[_full((N_DEV, rows, cols))], out_specs=_full((rows, cols)),
        out_shape=jax.ShapeDtypeStruct((rows, cols), F32),
        compiler_params=_params(("arbitrary",)),
    )(g8)


def _fwd_mix(x2d, gw, g_mix, conv_w, conv_b, ln_g, ln_b, pool_w, pool_scale, after, seq, tm):
    tokens = x2d.shape[0]
    n_tiles = tokens // tm
    tps = seq // tm

    def body(x_ref, gmix_ref, gw_hbm, cw_ref, cb_ref, lng_ref, lnb_ref, pw_ref, ps_ref, after_ref,
             x1_ref, u_ref, c_ref, pooled_ref, ymix_ref, h1_ref,
             win_v, wout_v, hc_carry, up_carry, sem):
        del after_ref
        i = pl.program_id(0)

        @pl.when(i == 0)
        def _():
            copies = _load_weight(gw_hbm, "w_in", win_v, sem) + _load_weight(gw_hbm, "w_out", wout_v, sem)
            for cp in copies:
                cp.start()
            for cp in copies:
                cp.wait()

        @pl.when(i % tps == 0)
        def _():
            hc_carry[...] = jnp.zeros_like(hc_carry)
            up_carry[...] = jnp.zeros_like(up_carry)

        x = x_ref[...]
        xh, _ = _rms_fwd(x)
        h1 = (xh * gmix_ref[...]).astype(BF16)
        h1_ref[...] = h1
        u = _dot_nt(h1, win_v[...])
        u_ref[...] = u
        val, gate, up = u[:, :D_CONV], u[:, D_CONV:2 * D_CONV], u[:, 2 * D_CONV:]

        hc = val * _sigmoid(gate)
        ext = jnp.concatenate([hc_carry[...], hc], axis=0)
        hc_carry[...] = hc[tm - CONV_HALO:, :]
        conv = jnp.broadcast_to(cb_ref[...], (tm, D_CONV))
        ahead_by = _sublane_shifts(ext)
        for k in range(CONV_WIDTH):
            whole, part = divmod(CONV_HALO - (CONV_WIDTH - 1) + k, 8)
            conv = conv + cw_ref[k:k + 1, :] * ahead_by[part][8 * whole:8 * whole + tm, :]
        c_ref[...] = conv
        mu = jnp.mean(conv, axis=-1, keepdims=True)
        cen = conv - mu
        ln = cen * lax.rsqrt(jnp.mean(cen * cen, axis=-1, keepdims=True) + EPS) * lng_ref[...] + lnb_ref[...]
        y_conv = ln * _sigmoid(ln)

        extp = jnp.concatenate([up_carry[...], up], axis=0)
        up_carry[...] = up[tm - POOL_HALO:, :]
        pos = lax.broadcasted_iota(jnp.int32, (tm, 1), 0) + (i % tps) * tm
        run = extp
        mixed = []
        for g, w in enumerate(POOL_WINDOWS):
            lo = g * POOL_GROUP_DIM
            run = run[:, POOL_GROUP_DIM if g else 0:]
            run = run + pltpu.roll(run, w // 2, 0)
            cnt = jnp.minimum(pos + 1, w).astype(F32)
            pooled = run[POOL_HALO:, :POOL_GROUP_DIM] / cnt - up[:, lo:lo + POOL_GROUP_DIM]
            pooled = pooled.astype(BF16)
            pooled_ref[:, lo:lo + POOL_GROUP_DIM] = pooled
            mixed.append(_dot(pooled, pw_ref[g].astype(BF16)))
        y_pool = jnp.concatenate(mixed, axis=-1) * ps_ref[...]

        ymix = jnp.concatenate([y_conv, y_pool], axis=-1).astype(BF16)
        ymix_ref[...] = ymix
        x1_ref[...] = x + _dot(ymix, wout_v[...])

    row = lambda w: pl.BlockSpec((tm, w), lambda i: (i, 0))
    return pl.pallas_call(
        body, name="fwd_mix", grid=(n_tiles,),
        in_specs=[row(D_MODEL), _full((1, D_MODEL)), pl.BlockSpec(memory_space=pl.ANY),
                  _full((CONV_WIDTH, D_CONV)), _full((1, D_CONV)), _full((1, D_CONV)), _full((1, D_CONV)),
                  _full((4, POOL_GROUP_DIM, POOL_GROUP_DIM)), _full((1, D_POOL)), _full(after.shape)],
        out_specs=[row(D_MODEL), row(D_IN), row(D_CONV), row(D_POOL), row(D_MODEL), row(D_MODEL)],
        out_shape=[jax.ShapeDtypeStruct((tokens, D_MODEL), F32), jax.ShapeDtypeStruct((tokens, D_IN), F32),
                   jax.ShapeDtypeStruct((tokens, D_CONV), F32), jax.ShapeDtypeStruct((tokens, D_POOL), BF16),
                   jax.ShapeDtypeStruct((tokens, D_MODEL), BF16), jax.ShapeDtypeStruct((tokens, D_MODEL), BF16)],
        scratch_shapes=[pltpu.VMEM((D_IN, D_MODEL), BF16), pltpu.VMEM((D_MODEL, D_MODEL), BF16),
                        pltpu.VMEM((CONV_HALO, D_CONV), F32), pltpu.VMEM((POOL_HALO, D_POOL), F32),
                        pltpu.SemaphoreType.DMA],
        compiler_params=_params(),
    )(x2d, g_mix, gw, conv_w, conv_b, ln_g, ln_b, pool_w, pool_scale, after)


def _fwd_kv(mem2d, gw, g_mem):
    rows = mem2d.shape[0]
    n_b = rows // N_MEM

    def body(mem_ref, g_ref, gw_hbm, mn_ref, kv_ref, wkv_v, sem):
        @pl.when(pl.program_id(0) == 0)
        def _():
            copies = _load_weight(gw_hbm, "w_kv", wkv_v, sem)
            for cp in copies:
                cp.start()
            for cp in copies:
                cp.wait()

        mh, _ = _rms_fwd(mem_ref[...])
        mn = (mh * g_ref[...]).astype(BF16)
        mn_ref[...] = mn
        kv_ref[...] = _dot_nt(mn, wkv_v[...]).astype(BF16)

    return pl.pallas_call(
        body, name="fwd_kv", grid=(n_b,),
        in_specs=[pl.BlockSpec((N_MEM, D_MODEL), lambda b: (b, 0)), _full((1, D_MODEL)), pl.BlockSpec(memory_space=pl.ANY)],
        out_specs=[pl.BlockSpec((N_MEM, D_MODEL), lambda b: (b, 0)), pl.BlockSpec((N_MEM, 2 * D_MODEL), lambda b: (b, 0))],
        out_shape=[jax.ShapeDtypeStruct((rows, D_MODEL), BF16), jax.ShapeDtypeStruct((rows, 2 * D_MODEL), BF16)],
        scratch_shapes=[pltpu.VMEM((2 * D_MODEL, D_MODEL), BF16), pltpu.SemaphoreType.DMA],
        compiler_params=_params(),
    )(mem2d, g_mem, gw)


def _softmax_rows(s):
    e = jnp.exp(s - jnp.max(s, axis=-1, keepdims=True))
    return e / jnp.sum(e, axis=-1, keepdims=True)


def _fwd_attn(x1, kv, gw, g_x, seq, tm):
    tokens = x1.shape[0]
    n_tiles = tokens // tm
    tps = seq // tm

    def body(x1_ref, kv_ref, g_ref, gw_hbm, x2_ref, h2_ref, q_ref, o_ref, wq_v, wo_v, sem):
        @pl.when(pl.program_id(0) == 0)
        def _():
            copies = _load_weight(gw_hbm, "w_q", wq_v, sem) + _load_weight(gw_hbm, "w_o", wo_v, sem)
            for cp in copies:
                cp.start()
            for cp in copies:
                cp.wait()

        x1v = x1_ref[...]
        xh, _ = _rms_fwd(x1v)
        h2 = (xh * g_ref[...]).astype(BF16)
        h2_ref[...] = h2
        q = (_dot(h2, wq_v[...]) * (HEAD_DIM ** -0.5)).astype(BF16)
        q_ref[...] = q
        heads = [slice(h * HEAD_DIM, (h + 1) * HEAD_DIM) for h in range(HEADS)]
        scores = [_dot_nt(q[:, hd], kv_ref[:, hd]) for hd in heads]
        probs = [_softmax_rows(s).astype(BF16) for s in scores]
        outs = [_dot(p, kv_ref[:, pl.ds(D_MODEL + h * HEAD_DIM, HEAD_DIM)]) for h, p in enumerate(probs)]
        o = jnp.concatenate(outs, axis=-1).astype(BF16)
        o_ref[...] = o
        x2_ref[...] = x1v + _dot(o, wo_v[...])

    row = lambda w: pl.BlockSpec((tm, w), lambda i: (i, 0))
    return pl.pallas_call(
        body, name="fwd_attn", grid=(n_tiles,),
        in_specs=[row(D_MODEL), pl.BlockSpec((N_MEM, 2 * D_MODEL), lambda i: (i // tps, 0)), _full((1, D_MODEL)),
                  pl.BlockSpec(memory_space=pl.ANY)],
        out_specs=[row(D_MODEL)] * 4,
        out_shape=[jax.ShapeDtypeStruct((tokens, D_MODEL), F32)] + [jax.ShapeDtypeStruct((tokens, D_MODEL), BF16)] * 3,
        scratch_shapes=[pltpu.VMEM((D_MODEL, D_MODEL), BF16), pltpu.VMEM((D_MODEL, D_MODEL), BF16), pltpu.SemaphoreType.DMA],
        compiler_params=_params(),
    )(x1, kv, g_x, gw)


def _ffn_conv(uu, halo, w_ref, b_ref, cols):
    ext = jnp.concatenate([halo, uu], axis=0)
    p1 = pltpu.roll(ext, 1, 0)[FFN_HALO:, :]
    p2 = pltpu.roll(ext, 2, 0)[FFN_HALO:, :]
    return b_ref[:, cols] + w_ref[2:3, cols] * uu + w_ref[1:2, cols] * p1 + w_ref[0:1, cols] * p2


def _fwd_ffn(x2, target, gw, g_ffn, ffn_w, ffn_b, g_final, seq, tm):
    tokens = x2.shape[0]
    n_tiles = tokens // tm
    tps = seq // tm
    n_chunks = D_FF // FFN_CHUNK

    def body(x2_ref, tgt_ref, gffn_ref, gw_hbm, fw_ref, fb_ref, gfin_ref,
             uu_ref, cc_ref, a_ref, h3_ref, dx3_ref, dx3b_ref, loss_ref, dgfin_ref,
             wup_v, wdown_v, carry, sem):
        i = pl.program_id(0)

        @pl.when(i == 0)
        def _():
            copies = _load_weight(gw_hbm, "w_up", wup_v, sem) + _load_weight(gw_hbm, "w_down", wdown_v, sem)
            for cp in copies:
                cp.start()
            for cp in copies:
                cp.wait()
            loss_ref[...] = jnp.zeros_like(loss_ref)
            dgfin_ref[...] = jnp.zeros_like(dgfin_ref)

        @pl.when(i % tps == 0)
        def _():
            carry[...] = jnp.zeros_like(carry)

        x2v = x2_ref[...]
        xh, _ = _rms_fwd(x2v)
        h3 = (xh * gffn_ref[...]).astype(BF16)
        h3_ref[...] = h3
        acc = jnp.zeros((tm, D_MODEL), F32)
        for jc in range(n_chunks):
            halves = []
            for half in range(2):
                cols = pl.ds(half * D_FF + jc * FFN_CHUNK, FFN_CHUNK)
                uu = _dot_nt(h3, wup_v[cols, :])
                uu_ref[:, cols] = uu.astype(BF16)
                cc = _ffn_conv(uu, carry[:, cols], fw_ref, fb_ref, cols)
                cc_ref[:, cols] = cc.astype(BF16)
                halves.append(cc)
                carry[:, cols] = uu[tm - FFN_HALO:, :]
            gate, val = halves
            a = (gate * _sigmoid(gate) * val).astype(BF16)
            a_ref[:, pl.ds(jc * FFN_CHUNK, FFN_CHUNK)] = a
            acc = acc + _dot(a, wdown_v[pl.ds(jc * FFN_CHUNK, FFN_CHUNK), :])
        x3 = x2v + acc

        xh3, r3 = _rms_fwd(x3)
        gfin = gfin_ref[...]
        err = xh3 * gfin - tgt_ref[...]
        loss_ref[...] += jnp.full(loss_ref.shape, jnp.sum(err * err) * (0.5 / D_MODEL), F32)
        dy = err * (1.0 / D_MODEL)
        dgfin_ref[...] += _colsum(dy * xh3)
        dx3 = _rms_bwd(dy, xh3, r3, gfin)
        dx3_ref[...] = dx3
        dx3b_ref[...] = dx3.astype(BF16)

    row = lambda w: pl.BlockSpec((tm, w), lambda i: (i, 0))
    return pl.pallas_call(
        body, name="fwd_ffn", grid=(n_tiles,),
        in_specs=[row(D_MODEL), row(D_MODEL), _full((1, D_MODEL)), pl.BlockSpec(memory_space=pl.ANY),
                  _full((FFN_CONV_WIDTH, 2 * D_FF)), _full((1, 2 * D_FF)), _full((1, D_MODEL))],
        out_specs=[row(2 * D_FF), row(2 * D_FF), row(D_FF), row(D_MODEL), row(D_MODEL), row(D_MODEL), _full((8, 128)),
                   _full((1, D_MODEL))],
        out_shape=[jax.ShapeDtypeStruct((tokens, 2 * D_FF), BF16), jax.ShapeDtypeStruct((tokens, 2 * D_FF), BF16),
                   jax.ShapeDtypeStruct((tokens, D_FF), BF16),
                   jax.ShapeDtypeStruct((tokens, D_MODEL), BF16), jax.ShapeDtypeStruct((tokens, D_MODEL), F32),
                   jax.ShapeDtypeStruct((tokens, D_MODEL), BF16),
                   jax.ShapeDtypeStruct((8, 128), F32), jax.ShapeDtypeStruct((1, D_MODEL), F32)],
        scratch_shapes=[pltpu.VMEM((2 * D_FF, D_MODEL), BF16), pltpu.VMEM((D_FF, D_MODEL), BF16),
                        pltpu.VMEM((FFN_HALO, 2 * D_FF), F32), pltpu.SemaphoreType.DMA],
        compiler_params=_params(),
    )(x2, target, g_ffn, gw, ffn_w, ffn_b, g_final)


def _bwd_ffn(dx3, x2, uu_all, cc_all, gw, g_ffn, ffn_w, seq, tm):
    tokens = x2.shape[0]
    n_tiles = tokens // tm
    tps = seq // tm
    n_chunks = D_FF // FFN_CHUNK

    def body(dx3_ref, x2_ref, uu_ref, cc_ref, gffn_ref, gw_hbm, fw_ref,
             dx2_ref, dx2b_ref, duu_ref, dfb_ref, dfw_ref, dg_ref,
             wup_v, wdown_v, carry, sem):
        i = pl.program_id(0)
        t = n_tiles - 1 - i

        @pl.when(i == 0)
        def _():
            copies = _load_weight(gw_hbm, "w_up", wup_v, sem) + _load_weight(gw_hbm, "w_down", wdown_v, sem)
            for cp in copies:
                cp.start()
            for cp in copies:
                cp.wait()
            dfb_ref[...] = jnp.zeros_like(dfb_ref)
            dfw_ref[...] = jnp.zeros_like(dfw_ref)
            dg_ref[...] = jnp.zeros_like(dg_ref)

        @pl.when(t % tps == tps - 1)
        def _():
            carry[...] = jnp.zeros_like(carry)

        dx3v = dx3_ref[...]
        dx3b = dx3v.astype(BF16)
        dh3 = jnp.zeros((tm, D_MODEL), F32)
        for jc in range(n_chunks):
            da = _dot_nt(dx3b, wdown_v[pl.ds(jc * FFN_CHUNK, FFN_CHUNK), :])
            colss = [pl.ds(half * D_FF + jc * FFN_CHUNK, FFN_CHUNK) for half in range(2)]
            gate, val = [cc_ref[:, cols].astype(F32) for cols in colss]
            sg = _sigmoid(gate)
            dgate = da * val * (sg * (1.0 + gate * (1.0 - sg)))
            dval = da * (gate * sg)
            for dcc, cols in zip((dgate, dval), colss):
                uu = uu_ref[:, cols].astype(F32)
                dfb_ref[:, cols] += _colsum(dcc)
                ext = jnp.concatenate([dcc, carry[:, cols]], axis=0)
                carry[:, cols] = dcc[:FFN_HALO, :]
                n1 = pltpu.roll(ext, tm + FFN_HALO - 1, 0)[:tm, :]
                n2 = pltpu.roll(ext, tm + FFN_HALO - 2, 0)[:tm, :]
                duu = fw_ref[2:3, cols] * dcc + fw_ref[1:2, cols] * n1 + fw_ref[0:1, cols] * n2
                dfw_ref[2:3, cols] += _colsum(uu * dcc)
                dfw_ref[1:2, cols] += _colsum(uu * n1)
                dfw_ref[0:1, cols] += _colsum(uu * n2)
                duub = duu.astype(BF16)
                duu_ref[:, cols] = duub
                dh3 = dh3 + _dot(duub, wup_v[cols, :])
        xh, r = _rms_fwd(x2_ref[...])
        dg_ref[...] += _colsum(dh3 * xh)
        dx2 = dx3v + _rms_bwd(dh3, xh, r, gffn_ref[...])
        dx2_ref[...] = dx2
        dx2b_ref[...] = dx2.astype(BF16)

    rev = lambda w: pl.BlockSpec((tm, w), lambda i: (n_tiles - 1 - i, 0))
    return pl.pallas_call(
        body, name="bwd_ffn", grid=(n_tiles,),
        in_specs=[rev(D_MODEL), rev(D_MODEL), rev(2 * D_FF), rev(2 * D_FF), _full((1, D_MODEL)),
                  pl.BlockSpec(memory_space=pl.ANY), _full((FFN_CONV_WIDTH, 2 * D_FF))],
        out_specs=[rev(D_MODEL), rev(D_MODEL), rev(2 * D_FF), _full((1, 2 * D_FF)), _full((FFN_CONV_WIDTH, 2 * D_FF)),
                   _full((1, D_MODEL))],
        out_shape=[jax.ShapeDtypeStruct((tokens, D_MODEL), F32), jax.ShapeDtypeStruct((tokens, D_MODEL), BF16),
                   jax.ShapeDtypeStruct((tokens, 2 * D_FF), BF16),
                   jax.ShapeDtypeStruct((1, 2 * D_FF), F32), jax.ShapeDtypeStruct((FFN_CONV_WIDTH, 2 * D_FF), F32),
                   jax.ShapeDtypeStruct((1, D_MODEL), F32)],
        scratch_shapes=[pltpu.VMEM((2 * D_FF, D_MODEL), BF16), pltpu.VMEM((D_FF, D_MODEL), BF16),
                        pltpu.VMEM((FFN_HALO, 2 * D_FF), F32), pltpu.SemaphoreType.DMA],
        compiler_params=_params(),
    )(dx3, x2, uu_all, cc_all, g_ffn, gw, ffn_w)


def _bwd_attn(dx2, x1, q, kv, gw, g_x, after, seq, tm):
    tokens = x1.shape[0]
    n_tiles = tokens // tm
    tps = seq // tm
    n_b = tokens // seq

    def body(dx2_ref, x1_ref, q_ref, kv_ref, g_ref, gw_hbm, after_ref, dx1_ref, dx1b_ref, dq_ref, dkv_ref, dg_ref,
             wq_v, wo_v, sem):
        del after_ref
        i = pl.program_id(0)

        @pl.when(i == 0)
        def _():
            copies = _load_weight(gw_hbm, "w_q", wq_v, sem) + _load_weight(gw_hbm, "w_o", wo_v, sem)
            for cp in copies:
                cp.start()
            for cp in copies:
                cp.wait()
            dg_ref[...] = jnp.zeros_like(dg_ref)

        @pl.when(i % tps == 0)
        def _():
            dkv_ref[...] = jnp.zeros_like(dkv_ref)

        dx2v = dx2_ref[...]
        do = _dot_nt(dx2v.astype(BF16), wo_v[...]).astype(BF16)
        q = q_ref[...]
        heads = [slice(h * HEAD_DIM, (h + 1) * HEAD_DIM) for h in range(HEADS)]
        kcols = [pl.ds(h * HEAD_DIM, HEAD_DIM) for h in range(HEADS)]
        vcols = [pl.ds(D_MODEL + h * HEAD_DIM, HEAD_DIM) for h in range(HEADS)]
        scores = [_dot_nt(q[:, hd], kv_ref[:, kc]) for hd, kc in zip(heads, kcols)]
        dps = [_dot_nt(do[:, hd], kv_ref[:, vc]) for hd, vc in zip(heads, vcols)]
        probs = [_softmax_rows(s) for s in scores]
        dss = [(p * (dp - jnp.sum(dp * p, axis=-1, keepdims=True))).astype(BF16) for p, dp in zip(probs, dps)]
        for p, hd, vc in zip(probs, heads, vcols):
            dkv_ref[:, vc] += _dot_tn(p.astype(BF16), do[:, hd])
        dqs = [_dot(ds, kv_ref[:, kc]) * (HEAD_DIM ** -0.5) for ds, kc in zip(dss, kcols)]
        for ds, hd, kc in zip(dss, heads, kcols):
            dkv_ref[:, kc] += _dot_tn(ds, q[:, hd])
        dq = jnp.concatenate(dqs, axis=-1).astype(BF16)
        dq_ref[...] = dq
        dh2 = _dot_nt(dq, wq_v[...])
        xh, r = _rms_fwd(x1_ref[...])
        dg_ref[...] += _colsum(dh2 * xh)
        dx1 = dx2v + _rms_bwd(dh2, xh, r, g_ref[...])
        dx1_ref[...] = dx1
        dx1b_ref[...] = dx1.astype(BF16)

    row = lambda w: pl.BlockSpec((tm, w), lambda i: (i, 0))
    per_b = pl.BlockSpec((N_MEM, 2 * D_MODEL), lambda i: (i // tps, 0))
    return pl.pallas_call(
        body, name="bwd_attn", grid=(n_tiles,),
        in_specs=[row(D_MODEL), row(D_MODEL), row(D_MODEL), per_b, _full((1, D_MODEL)), pl.BlockSpec(memory_space=pl.ANY),
                  _full(after.shape)],
        out_specs=[row(D_MODEL), row(D_MODEL), row(D_MODEL), per_b, _full((1, D_MODEL))],
        out_shape=[jax.ShapeDtypeStruct((tokens, D_MODEL), F32), jax.ShapeDtypeStruct((tokens, D_MODEL), BF16),
                   jax.ShapeDtypeStruct((tokens, D_MODEL), BF16),
                   jax.ShapeDtypeStruct((n_b * N_MEM, 2 * D_MODEL), F32), jax.ShapeDtypeStruct((1, D_MODEL), F32)],
        scratch_shapes=[pltpu.VMEM((D_MODEL, D_MODEL), BF16), pltpu.VMEM((D_MODEL, D_MODEL), BF16), pltpu.SemaphoreType.DMA],
        compiler_params=_params(),
    )(dx2, x1, q, kv, g_x, gw, after)


def _bwd_kv(dkv, mem2d, gw, g_mem):
    rows = mem2d.shape[0]
    n_b = rows // N_MEM

    def body(dkv_ref, mem_ref, gw_hbm, dkvb_ref, dg_ref, wkv_v, sem):
        @pl.when(pl.program_id(0) == 0)
        def _():
            copies = _load_weight(gw_hbm, "w_kv", wkv_v, sem)
            for cp in copies:
                cp.start()
            for cp in copies:
                cp.wait()
            dg_ref[...] = jnp.zeros_like(dg_ref)

        dkvb = dkv_ref[...].astype(BF16)
        dkvb_ref[...] = dkvb
        dmn = _dot(dkvb, wkv_v[...])
        mh, _ = _rms_fwd(mem_ref[...])
        dg_ref[...] += _colsum(dmn * mh)

    del g_mem
    return pl.pallas_call(
        body, name="bwd_kv", grid=(n_b,),
        in_specs=[pl.BlockSpec((N_MEM, 2 * D_MODEL), lambda b: (b, 0)), pl.BlockSpec((N_MEM, D_MODEL), lambda b: (b, 0)),
                  pl.BlockSpec(memory_space=pl.ANY)],
        out_specs=[pl.BlockSpec((N_MEM, 2 * D_MODEL), lambda b: (b, 0)), _full((1, D_MODEL))],
        out_shape=[jax.ShapeDtypeStruct((rows, 2 * D_MODEL), BF16), jax.ShapeDtypeStruct((1, D_MODEL), F32)],
        scratch_shapes=[pltpu.VMEM((2 * D_MODEL, D_MODEL), BF16), pltpu.SemaphoreType.DMA],
        compiler_params=_params(),
    )(dkv, mem2d, gw)


def _bwd_mix(dx1, x2d, u_all, c_all, pooled_all, gw, g_mix, conv_w, ln_g, ln_b, pool_w, pool_scale, after, seq, tm):
    tokens = x2d.shape[0]
    n_tiles = tokens // tm
    tps = seq // tm

    def body(dx1_ref, x_ref, u_ref, c_ref, pooled_ref, gmix_ref, gw_hbm, cw_ref, lng_ref, lnb_ref, pw_ref, ps_ref,
             after_ref, dx_ref, du_ref, dgmix_ref, dcw_ref, dcb_ref, dlng_ref, dlnb_ref, dpw_ref, dps_ref,
             win_v, wout_v, dc_carry, e_carry, sem):
        del after_ref
        i = pl.program_id(0)
        t = n_tiles - 1 - i

        @pl.when(i == 0)
        def _():
            copies = _load_weight(gw_hbm, "w_in", win_v, sem) + _load_weight(gw_hbm, "w_out", wout_v, sem)
            for cp in copies:
                cp.start()
            for cp in copies:
                cp.wait()
            for ref in (dgmix_ref, dcw_ref, dcb_ref, dlng_ref, dlnb_ref, dpw_ref, dps_ref):
                ref[...] = jnp.zeros_like(ref)

        @pl.when(t % tps == tps - 1)
        def _():
            dc_carry[...] = jnp.zeros_like(dc_carry)
            e_carry[...] = jnp.zeros_like(e_carry)

        dx1v = dx1_ref[...]
        dymix = _dot_nt(dx1v.astype(BF16), wout_v[...])
        dyc, dyp = dymix[:, :D_CONV], dymix[:, D_CONV:]
        u = u_ref[...]
        val, gate = u[:, :D_CONV], u[:, D_CONV:2 * D_CONV]

        conv = c_ref[...]
        mu = jnp.mean(conv, axis=-1, keepdims=True)
        cen = conv - mu
        rs = lax.rsqrt(jnp.mean(cen * cen, axis=-1, keepdims=True) + EPS)
        chat = cen * rs
        ln = chat * lng_ref[...] + lnb_ref[...]
        sl = _sigmoid(ln)
        dln = dyc * (sl * (1.0 + ln * (1.0 - sl)))
        dlng_ref[...] += _colsum(dln * chat)
        dlnb_ref[...] += _colsum(dln)
        dchat = dln * lng_ref[...]
        dc = rs * (dchat - jnp.mean(dchat, axis=-1, keepdims=True)
                   - chat * jnp.mean(dchat * chat, axis=-1, keepdims=True))
        dcb_ref[...] += _colsum(dc)
        sg = _sigmoid(gate)
        hc = val * sg
        ext = jnp.concatenate([dc, dc_carry[...]], axis=0)
        dc_carry[...] = dc[:CONV_HALO, :]
        dhc = jnp.zeros((tm, D_CONV), F32)
        ahead_by = _sublane_shifts(ext)
        for k in range(CONV_WIDTH):
            whole, part = divmod(CONV_WIDTH - 1 - k, 8)
            tap = ahead_by[part][8 * whole:8 * whole + tm, :]
            dhc = dhc + cw_ref[k:k + 1, :] * tap
            dcw_ref[k:k + 1, :] += _colsum(hc * tap)
        du_ref[:, :D_CONV] = (dhc * sg).astype(BF16)
        du_ref[:, D_CONV:2 * D_CONV] = (dhc * val * (sg * (1.0 - sg))).astype(BF16)

        pos = lax.broadcasted_iota(jnp.int32, (tm, 1), 0) + (t % tps) * tm
        es, dpooled = [], []
        for g, w in enumerate(POOL_WINDOWS):
            cols = pl.ds(g * POOL_GROUP_DIM, POOL_GROUP_DIM)
            lo = g * POOL_GROUP_DIM
            pooled = pooled_ref[:, cols]
            pw = pw_ref[g].astype(BF16)
            dyg = dyp[:, lo:lo + POOL_GROUP_DIM]
            dps_ref[:, cols] += _colsum(dyg * _dot(pooled, pw))
            dmixed = (dyg * ps_ref[:, cols]).astype(BF16)
            dpw_ref[g] += _dot_tn(pooled, dmixed)
            dpo = _dot_nt(dmixed, pw)
            dpooled.append(dpo)
            es.append(dpo / jnp.minimum(pos + 1, w).astype(F32))
        e = jnp.concatenate(es, axis=-1)
        run = jnp.concatenate([e, e_carry[...]], axis=0)
        e_carry[...] = e[:POOL_HALO, :]
        rows = tm + POOL_HALO
        for g, w in enumerate(POOL_WINDOWS):
            lo = g * POOL_GROUP_DIM
            run = run[:, POOL_GROUP_DIM if g else 0:]
            run = run + pltpu.roll(run, rows - w // 2, 0)
            du_ref[:, 2 * D_CONV + lo:2 * D_CONV + lo + POOL_GROUP_DIM] = (
                run[:tm, :POOL_GROUP_DIM] - dpooled[g]).astype(BF16)

        dh1 = _dot(du_ref[...], win_v[...])
        xh, r = _rms_fwd(x_ref[...])
        dgmix_ref[...] += _colsum(dh1 * xh)
        dx_ref[...] = dx1v + _rms_bwd(dh1, xh, r, gmix_ref[...])

    rev = lambda w: pl.BlockSpec((tm, w), lambda i: (n_tiles - 1 - i, 0))
    return pl.pallas_call(
        body, name="bwd_mix", grid=(n_tiles,),
        in_specs=[rev(D_MODEL), rev(D_MODEL), rev(D_IN), rev(D_CONV), rev(D_POOL), _full((1, D_MODEL)),
                  pl.BlockSpec(memory_space=pl.ANY), _full((CONV_WIDTH, D_CONV)), _full((1, D_CONV)), _full((1, D_CONV)),
                  _full((4, POOL_GROUP_DIM, POOL_GROUP_DIM)), _full((1, D_POOL)), _full(after.shape)],
        out_specs=[rev(D_MODEL), rev(D_IN), _full((1, D_MODEL)), _full((CONV_WIDTH, D_CONV)), _full((1, D_CONV)),
                   _full((1, D_CONV)), _full((1, D_CONV)), _full((4, POOL_GROUP_DIM, POOL_GROUP_DIM)), _full((1, D_POOL))],
        out_shape=[jax.ShapeDtypeStruct((tokens, D_MODEL), F32), jax.ShapeDtypeStruct((tokens, D_IN), BF16),
                   jax.ShapeDtypeStruct((1, D_MODEL), F32), jax.ShapeDtypeStruct((CONV_WIDTH, D_CONV), F32),
                   jax.ShapeDtypeStruct((1, D_CONV), F32), jax.ShapeDtypeStruct((1, D_CONV), F32),
                   jax.ShapeDtypeStruct((1, D_CONV), F32),
                   jax.ShapeDtypeStruct((4, POOL_GROUP_DIM, POOL_GROUP_DIM), F32), jax.ShapeDtypeStruct((1, D_POOL), F32)],
        scratch_shapes=[pltpu.VMEM((D_IN, D_MODEL), BF16), pltpu.VMEM((D_MODEL, D_MODEL), BF16),
                        pltpu.VMEM((CONV_HALO, D_CONV), F32), pltpu.VMEM((POOL_HALO, D_POOL), F32),
                        pltpu.SemaphoreType.DMA],
        compiler_params=_params(),
    )(dx1, x2d, u_all, c_all, pooled_all, g_mix, gw, conv_w, ln_g, ln_b, pool_w, pool_scale, after)


def _wgrad(a, b, name, after=None):
    tokens, m = a.shape
    n = b.shape[1]
    tm = 512 if m % 512 == 0 else 256
    extra = [] if after is None else [after]

    def body(a_ref, b_ref, *rest):
        rest[-1][...] = _dot_tn(a_ref[...], b_ref[...]).astype(rest[-1].dtype)

    return pl.pallas_call(
        body, name=name, grid=(m // tm,),
        in_specs=[pl.BlockSpec((tokens, tm), lambda i: (0, i)), _full((tokens, n))] + [_full(t.shape) for t in extra],
        out_specs=pl.BlockSpec((tm, n), lambda i: (i, 0)),
        out_shape=jax.ShapeDtypeStruct((m, n), BF16),
        compiler_params=_params(),
    )(a, b, *extra)


def _adamw_update(w, g, m, v):
    nm = ADAM_B1 * m + (1.0 - ADAM_B1) * g
    nv = ADAM_B2 * v + (1.0 - ADAM_B2) * (g * g)
    m_hat = nm / (1.0 - ADAM_B1 ** ADAM_STEP)
    v_hat = nv / (1.0 - ADAM_B2 ** ADAM_STEP)
    return -ADAM_LR * (m_hat / (jnp.sqrt(v_hat) + ADAM_EPS) + ADAM_WD * w), nm, nv


def _adamw_small(ws, gs, ms, vs):
    n = len(ws)

    def body(*refs):
        ins, outs = refs[:4 * n], refs[4 * n:]
        for k in range(n):
            d, nm, nv = _adamw_update(*[ins[j * n + k][...] for j in range(4)])
            outs[k][...] = d
            outs[n + k][...] = nm
            outs[2 * n + k][...] = nv

    vmem = pl.BlockSpec(memory_space=pltpu.VMEM)
    outs = pl.pallas_call(
        body, name="adamw_small",
        in_specs=[vmem] * (4 * n), out_specs=[vmem] * (3 * n),
        out_shape=[jax.ShapeDtypeStruct(w.shape, F32) for w in ws] * 3,
    )(*ws, *gs, *ms, *vs)
    return outs[:n], outs[n:2 * n], outs[2 * n:]


def _adamw(w, g, m, v, name):
    rows, cols = w.shape
    tile = rows
    for cand in (512, 256, 128, 64, 32, 16, 8):
        if rows % cand == 0:
            tile = cand
            break

    def body(w_ref, g_ref, m_ref, v_ref, d_ref, nm_ref, nv_ref):
        d_ref[...], nm_ref[...], nv_ref[...] = _adamw_update(w_ref[...], g_ref[...], m_ref[...], v_ref[...])

    spec = pl.BlockSpec((tile, cols), lambda i: (i, 0))
    return pl.pallas_call(
        body, name=name, grid=(rows // tile,),
        in_specs=[spec] * 4, out_specs=[spec] * 3,
        out_shape=[jax.ShapeDtypeStruct((rows, cols), F32)] * 3,
        compiler_params=_params(("arbitrary",)),
    )(w, g, m, v)


SMALL = (("norm_mix_g", (1, 1024)), ("conv_dw_b", (1, 512)), ("conv_ln_g", (1, 512)), ("conv_ln_b", (1, 512)),
         ("pool_w", (1, 4, 128, 128)), ("pool_scale", (1, 512)), ("norm_xattn_g", (1, 1024)), ("norm_mem_g", (1, 1024)),
         ("norm_ffn_g", (1, 1024)), ("ffn_dw_b", (1, 5632)), ("norm_final_g", (1024,)))
LANES = 128


def _pack_rows(arrs):
    flat = jnp.concatenate([a.reshape(-1) for a in arrs])
    pad = (-flat.shape[0]) % (8 * LANES)
    return jnp.pad(flat, (0, pad)).reshape(-1, LANES)


def kernel(x, mem, norm_mix_g, w_in, conv_dw_w, conv_dw_b, conv_ln_g, conv_ln_b, pool_w, pool_scale, w_out, norm_xattn_g, norm_mem_g, w_q, w_kv, w_o, norm_ffn_g, w_up, ffn_dw_w, ffn_dw_b, w_down, norm_final_g, loss_target, m_norm_mix_g, m_w_in, m_conv_dw_w, m_conv_dw_b, m_conv_ln_g, m_conv_ln_b, m_pool_w, m_pool_scale, m_w_out, m_norm_xattn_g, m_norm_mem_g, m_w_q, m_w_kv, m_w_o, m_norm_ffn_g, m_w_up, m_ffn_dw_w, m_ffn_dw_b, m_w_down, m_norm_final_g, v_norm_mix_g, v_w_in, v_conv_dw_w, v_conv_dw_b, v_conv_ln_g, v_conv_ln_b, v_pool_w, v_pool_scale, v_w_out, v_norm_xattn_g, v_norm_mem_g, v_w_q, v_w_kv, v_w_o, v_norm_ffn_g, v_w_up, v_ffn_dw_w, v_ffn_dw_b, v_w_down, v_norm_final_g):
    weights = dict(norm_mix_g=norm_mix_g, w_in=w_in, conv_dw_w=conv_dw_w, conv_dw_b=conv_dw_b, conv_ln_g=conv_ln_g,
                   conv_ln_b=conv_ln_b, pool_w=pool_w, pool_scale=pool_scale, w_out=w_out, norm_xattn_g=norm_xattn_g,
                   norm_mem_g=norm_mem_g, w_q=w_q, w_kv=w_kv, w_o=w_o, norm_ffn_g=norm_ffn_g, w_up=w_up,
                   ffn_dw_w=ffn_dw_w, ffn_dw_b=ffn_dw_b, w_down=w_down, norm_final_g=norm_final_g)
    moments_m = dict(norm_mix_g=m_norm_mix_g, w_in=m_w_in, conv_dw_w=m_conv_dw_w, conv_dw_b=m_conv_dw_b,
                     conv_ln_g=m_conv_ln_g, conv_ln_b=m_conv_ln_b, pool_w=m_pool_w, pool_scale=m_pool_scale,
                     w_out=m_w_out, norm_xattn_g=m_norm_xattn_g, norm_mem_g=m_norm_mem_g, w_q=m_w_q, w_kv=m_w_kv,
                     w_o=m_w_o, norm_ffn_g=m_norm_ffn_g, w_up=m_w_up, ffn_dw_w=m_ffn_dw_w, ffn_dw_b=m_ffn_dw_b,
                     w_down=m_w_down, norm_final_g=m_norm_final_g)
    moments_v = dict(norm_mix_g=v_norm_mix_g, w_in=v_w_in, conv_dw_w=v_conv_dw_w, conv_dw_b=v_conv_dw_b,
                     conv_ln_g=v_conv_ln_g, conv_ln_b=v_conv_ln_b, pool_w=v_pool_w, pool_scale=v_pool_scale,
                     w_out=v_w_out, norm_xattn_g=v_norm_xattn_g, norm_mem_g=v_norm_mem_g, w_q=v_w_q, w_kv=v_w_kv,
                     w_o=v_w_o, norm_ffn_g=v_norm_ffn_g, w_up=v_w_up, ffn_dw_w=v_ffn_dw_w, ffn_dw_b=v_ffn_dw_b,
                     w_down=v_w_down, norm_final_g=v_norm_final_g)
    order = list(weights)
    transposed = ("w_in", "w_kv", "w_up")

    n_b, seq, _ = x.shape
    tokens = n_b * seq
    tm_mix = min(512, seq // 2)
    tm_ffn = min(256, seq // 2)
    dev = 4 * lax.axis_index("x") + 2 * lax.axis_index("y") + lax.axis_index("c")

    packs = [jnp.concatenate([weights[n][0].T if n in transposed else weights[n][0] for n in names], axis=0).astype(BF16)
             for names in AG_GROUPS]
    small_sharded = _pack_rows([conv_dw_w[0], ffn_dw_w[0]])
    gw_mix, gsmall = _all_gather([packs[0], small_sharded], "weights_all_gather")
    flights = []
    after = gw_mix
    for k in (1, 2):
        own_in_place = lax.dynamic_update_slice(lax.empty((N_DEV,) + packs[k].shape, BF16), packs[k][None], (dev, 0, 0))
        flights.append(_gather_start(own_in_place, after, "weights_gather_start_%d" % k, BARRIER_IDS["gather_start"][k - 1]))
        after = flights[-1][3]
    gflat = gsmall.reshape(N_DEV, -1)
    n_cw = CONV_WIDTH * (D_CONV // N_DEV)
    n_fw = FFN_CONV_WIDTH * (2 * D_FF // N_DEV)
    conv_w = gflat[:, :n_cw].reshape(N_DEV, CONV_WIDTH, D_CONV // N_DEV).transpose(1, 0, 2).reshape(CONV_WIDTH, D_CONV)
    ffn_w = gflat[:, n_cw:n_cw + n_fw].reshape(N_DEV, FFN_CONV_WIDTH, 2 * D_FF // N_DEV).transpose(1, 0, 2).reshape(
        FFN_CONV_WIDTH, 2 * D_FF)

    x2d = x.reshape(tokens, D_MODEL)
    mem2d = mem.reshape(n_b * N_MEM, D_MODEL)
    tgt2d = loss_target.reshape(tokens, D_MODEL)
    g_final = norm_final_g.reshape(1, D_MODEL)

    def gather_finish(flight, after, tag):
        fwd_send, fwd_recv, buf = _gather_forward(*flight[:3], after, "weights_gather_forward_" + tag,
                                                  BARRIER_IDS["gather_forward"][int(tag) - 1])
        return _gather_finish(fwd_send, fwd_recv, buf, "weights_gather_finish_" + tag)

    x1, u_all, c_all, pooled_all, ymix, h1 = _fwd_mix(
        x2d, gw_mix, norm_mix_g, conv_w, conv_dw_b, conv_ln_g, conv_ln_b, pool_w[0], pool_scale, flights[1][3],
        seq, tm_mix)
    gw_attn = gather_finish(flights[0], x1, "1")
    mem_n, kv = _fwd_kv(mem2d, gw_attn, norm_mem_g)
    x2, h2, q, o = _fwd_attn(x1, kv, gw_attn, norm_xattn_g, seq, tm_mix)
    gw_ffn = gather_finish(flights[1], x2, "2")
    uu_all, cc_all, a_all, h3, dx3, dx3b, loss_part, dg_final = _fwd_ffn(
        x2, tgt2d, gw_ffn, norm_ffn_g, ffn_w, ffn_dw_b, g_final, seq, tm_ffn)

    table = _owner_table()

    def sibling_start(names, tag):
        parts = [part[n].reshape(N_DEV, W_OFF[n][1], D_MODEL) for n in names]
        return _exchange_start(parts, 4, _to_sibling, "rs_sibling_exchange_start_" + tag, BARRIER_IDS["sibling"][tag])

    def chips_start(flight, after, tag):
        parts, landed = _exchange_wait(*flight[:4], after, 4, _to_sibling, "rs_sibling_exchange_wait_" + tag)
        sums = _chip_partial_sums(table, parts, landed, "rs_chip_partial_sums_" + tag)
        return parts, landed, _exchange_start(sums, 3, _to_chip, "rs_chip_exchange_start_" + tag,
                                              BARRIER_IDS["chips"][tag])

    grads, delta, new_m, new_v = {}, {}, {}, {}

    def reduce_finish(names, parts, landed, flight, after, tag):
        _, from_chips = _exchange_wait(*flight[:4], after, 3, _to_chip, "rs_chip_exchange_wait_" + tag)
        as_rows = {n: n not in transposed or W_OFF[n][1] % LANES != 0 for n in names}
        states = [tuple(t[n][0].T if n in transposed else t[n][0] for t in (weights, moments_m, moments_v))
                  if as_rows[n] else None for n in names]
        results = _final_update(table, parts, landed, from_chips, states, "rs_final_update_" + tag)
        for n, res in zip(names, results):
            back = (lambda t: t.T[None]) if n in transposed else (lambda t: t[None])
            grads[n] = back(res[0])
            if as_rows[n]:
                delta[n], new_m[n], new_v[n] = [back(t) for t in res[1:]]
            else:
                delta[n], new_m[n], new_v[n] = [t[None] for t in _adamw(
                    weights[n][0], grads[n][0], moments_m[n][0], moments_v[n][0], "adamw_" + n)]
        alone = [n for n in names if not as_rows[n]]
        return delta[alone[-1] if alone else names[-1]]

    part = {}
    dx2, dx2b, duu, d_ffn_b, d_ffn_w, dg_ffn = _bwd_ffn(dx3, x2, uu_all, cc_all, gw_ffn, norm_ffn_g, ffn_w, seq, tm_ffn)
    part["w_up"] = _wgrad(duu, h3, "wgrad_w_up")
    part["w_down"] = _wgrad(a_all, dx3b, "wgrad_w_down")
    to_sibling_a = sibling_start(RS_GROUPS["a"], "a")
    dx1, dx1b, dq, dkv, dg_x = _bwd_attn(dx2, x1, q, kv, gw_attn, norm_xattn_g, to_sibling_a[4], seq, tm_mix)
    parts_a, landed_a, flight_a = chips_start(to_sibling_a, dx1, "a")
    dkv_b, dg_mem = _bwd_kv(dkv, mem2d, gw_attn, norm_mem_g)
    part["w_q"] = _wgrad(h2, dq, "wgrad_w_q", after=flight_a[4])
    part["w_kv"] = _wgrad(dkv_b, mem_n, "wgrad_w_kv")
    part["w_o"] = _wgrad(o, dx2b, "wgrad_w_o")
    to_sibling_b = sibling_start(RS_GROUPS["b"], "b")
    parts_b, landed_b, flight_b = chips_start(to_sibling_b, to_sibling_b[4], "b")
    dx, du, dg_mix, d_conv_w, d_conv_b, d_ln_g, d_ln_b, d_pool_w, d_pool_scale = _bwd_mix(
        dx1, x2d, u_all, c_all, pooled_all, gw_mix, norm_mix_g, conv_w, conv_ln_g, conv_ln_b, pool_w[0], pool_scale,
        flight_b[4], seq, tm_mix)
    grad_x = dx.reshape(x.shape)

    small_grads = dict(norm_mix_g=dg_mix, conv_dw_b=d_conv_b, conv_ln_g=d_ln_g, conv_ln_b=d_ln_b, pool_w=d_pool_w,
                       pool_scale=d_pool_scale, norm_xattn_g=dg_x, norm_mem_g=dg_mem, norm_ffn_g=dg_ffn,
                       ffn_dw_b=d_ffn_b, norm_final_g=dg_final)
    small_list = [small_grads[n] for n, _ in SMALL] + [d_conv_w, d_ffn_w, loss_part[:1]]
    small_mine = _pack_rows(small_list)
    small_flight = _broadcast_start(
        lax.dynamic_update_slice(lax.empty((N_DEV,) + small_mine.shape, F32), small_mine[None], (dev, 0, 0)),
        "small_grads_broadcast_start", BARRIER_IDS["broadcast"])

    part["w_in"] = _wgrad(du, h1, "wgrad_w_in", after=small_flight[3])
    part["w_out"] = _wgrad(ymix, dx1b, "wgrad_w_out")
    to_sibling_c = sibling_start(RS_GROUPS["c"], "c")
    parts_c, landed_c, flight_c = chips_start(to_sibling_c, to_sibling_c[4], "c")
    updated_a = reduce_finish(RS_GROUPS["a"], parts_a, landed_a, flight_a, flight_c[4], "a")
    updated_b = reduce_finish(RS_GROUPS["b"], parts_b, landed_b, flight_b, updated_a, "b")
    small_all = _broadcast_wait(*small_flight[:3], updated_b, "small_grads_broadcast_wait")
    small_sum = _sum_blocks(small_all).reshape(-1)

    pos = 0
    for n, shape in SMALL:
        size = 1
        for s in shape:
            size *= s
        grads[n] = small_sum[pos:pos + size].reshape(shape)
        pos += size
    full_conv_w = small_sum[pos:pos + CONV_WIDTH * D_CONV].reshape(CONV_WIDTH, D_CONV)
    pos += CONV_WIDTH * D_CONV
    full_ffn_w = small_sum[pos:pos + FFN_CONV_WIDTH * 2 * D_FF].reshape(FFN_CONV_WIDTH, 2 * D_FF)
    loss = small_sum[pos + FFN_CONV_WIDTH * 2 * D_FF]
    grads["conv_dw_w"] = lax.dynamic_slice_in_dim(full_conv_w, dev * (D_CONV // N_DEV), D_CONV // N_DEV, axis=1)[None]
    grads["ffn_dw_w"] = lax.dynamic_slice_in_dim(full_ffn_w, dev * (2 * D_FF // N_DEV), 2 * D_FF // N_DEV, axis=1)[None]

    small_names = [n for n in order if n not in W_OFF]
    swap = lambda t: jnp.transpose(t, (1, 0, 2))
    two_d = lambda t: t.reshape(1, -1) if t.ndim == 1 else (swap(t) if t.ndim == 3 else t)
    outs = _adamw_small(*[[two_d(t[n]) for n in small_names] for t in (weights, grads, moments_m, moments_v)])
    for res, out in zip((delta, new_m, new_v), outs):
        for n, o in zip(small_names, out):
            res[n] = swap(o) if o.ndim == 3 else o.reshape(weights[n].shape)

    reduce_finish(RS_GROUPS["c"], parts_c, landed_c, flight_c, delta[small_names[-1]], "c")

    return (loss, grad_x, *[grads[n] for n in order], *[delta[n] for n in order],
            *[new_m[n] for n in order], *[new_v[n] for n in order])
```

```python
import functools

import jax
import jax.numpy as jnp
from jax import lax
from jax.experimental import pallas as pl
from jax.experimental.pallas import tpu as pltpu

F32 = jnp.float32
BF16 = jnp.bfloat16
MESH = pl.DeviceIdType.MESH

N_DEV = 8
D_MODEL = 1024
D_CONV = 512
D_POOL = 512
CONV_WIDTH = 31
POOL_WINDOWS = (2, 4, 8, 16)
POOL_GROUP_DIM = 128
D_IN = 1536
N_MEM = 256
HEADS = 4
HEAD_DIM = 256
D_FF = 2816
FFN_CONV_WIDTH = 3
EPS = 1e-6
ADAM_LR = 0.001
ADAM_B1 = 0.9
ADAM_B2 = 0.999
ADAM_EPS = 1e-08
ADAM_WD = 0.01
ADAM_STEP = 10

VMEM_LIMIT_V7X = 56 * 1024 * 1024
CONV_HALO = 32
POOL_HALO = 16
FFN_HALO = 8
FFN_CHUNK = 2816

W_ROWS = (("w_in", 192), ("w_out", 128), ("w_q", 128), ("w_kv", 256), ("w_o", 128), ("w_up", 704), ("w_down", 352))
AG_GROUPS = (("w_in", "w_out"), ("w_q", "w_kv", "w_o"), ("w_up", "w_down"))
W_OFF = {}
for _names in AG_GROUPS:
    _o = 0
    for _n in _names:
        W_OFF[_n] = (_o, dict(W_ROWS)[_n])
        _o += dict(W_ROWS)[_n]
RS_GROUPS = {"a": ("w_up", "w_down"), "b": ("w_q", "w_kv", "w_o"), "c": ("w_in", "w_out")}
BARRIER_IDS = {"gather_start": (0, 1), "gather_forward": (2, 3), "sibling": {"a": 4, "b": 5, "c": 6},
               "chips": {"a": 7, "b": 8, "c": 9}, "broadcast": 10}


def _dot(a, b):
    return jnp.dot(a, b, preferred_element_type=F32)


def _dot_nt(a, b):
    return lax.dot_general(a, b, (((1,), (1,)), ((), ())), preferred_element_type=F32)


def _dot_tn(a, b):
    return lax.dot_general(a, b, (((0,), (0,)), ((), ())), preferred_element_type=F32)


def _sigmoid(v):
    return 1.0 / (1.0 + jnp.exp(-v))


def _rms_fwd(v):
    r = lax.rsqrt(jnp.mean(v * v, axis=-1, keepdims=True) + EPS)
    return v * r, r


def _rms_bwd(dh, vh, r, g):
    gd = dh * g
    return r * (gd - vh * jnp.mean(gd * vh, axis=-1, keepdims=True))


def _sublane_shifts(v):
    rows = v.shape[0]
    return [v] + [pltpu.roll(v, rows - b, 0) for b in range(1, 8)]


def _colsum(v):
    return jnp.sum(v, axis=0, keepdims=True)


def _colsum_mxu(v):
    return _dot(jnp.ones((8, v.shape[0]), BF16), v.astype(BF16))[0:1, :]


def _full(shape):
    return pl.BlockSpec(shape, lambda *_: (0,) * len(shape))


def _params(sem=("arbitrary",), vmem=VMEM_LIMIT_V7X):
    return pltpu.CompilerParams(dimension_semantics=sem, vmem_limit_bytes=vmem)


def _load_weight(g_hbm, name, dst, sem):
    off, rows = W_OFF[name]
    return [pltpu.make_async_copy(g_hbm.at[d, pl.ds(off, rows), :], dst.at[pl.ds(d * rows, rows), :], sem)
            for d in range(N_DEV)]


def _position():
    x, y, c = lax.axis_index("x"), lax.axis_index("y"), lax.axis_index("c")
    chips = [(1 - x, y), (x, 1 - y), (1 - x, 1 - y)]
    return x, y, c, chips


def _dev(px, py, pc):
    return 4 * px + 2 * py + pc


def _all_gather(arrs, name):
    n = len(arrs)

    def body(*refs):
        ins, outs = refs[:n], refs[n:2 * n]
        send_sems, recv_sems, local_sems = refs[2 * n:2 * n + 3]
        bounce = refs[2 * n + 3:]
        x, y, c, chips = _position()
        me, sibling = (x, y, c), (x, y, 1 - c)

        def copy(a, k, block, to, src=None):
            rows = outs[a].at[_dev(*block)]
            return pltpu.make_async_remote_copy(
                src_ref=rows if src is None else src, dst_ref=rows,
                send_sem=send_sems.at[a, k], recv_sem=recv_sems.at[a, k], device_id=to, device_id_type=MESH)

        sends = []
        for a in range(n):
            first = [copy(a, 0, me, sibling, src=ins[a])]
            first += [copy(a, 1 + j, me, (*chip, c), src=ins[a]) for j, chip in enumerate(chips)]
            for cp in first:
                cp.start()
            sends += first
        started = []
        for a in range(n):
            load = pltpu.make_async_copy(ins[a], bounce[a], local_sems.at[a, 0])
            load.start()
            load.wait()
            mine = pltpu.make_async_copy(bounce[a], outs[a].at[_dev(*me)], local_sems.at[a, 1])
            mine.start()
            started.append(mine)
        for j, chip in enumerate(chips):
            for a in range(n):
                copy(a, 1 + j, (*chip, c), me).wait_recv()
                passed = copy(a, 4 + j, (*chip, c), sibling)
                passed.start()
                sends.append(passed)
        for a in range(n):
            copy(a, 0, sibling, me).wait_recv()
            for j, chip in enumerate(chips):
                copy(a, 4 + j, (*chip, 1 - c), me).wait_recv()
        for cp in sends:
            cp.wait_send()
        for mine in started:
            mine.wait()

    any_spec = pl.BlockSpec(memory_space=pl.ANY)
    return pl.pallas_call(
        body, name=name,
        out_shape=[jax.ShapeDtypeStruct((N_DEV,) + a.shape, a.dtype) for a in arrs],
        in_specs=[any_spec] * n, out_specs=[any_spec] * n,
        scratch_shapes=[pltpu.SemaphoreType.DMA((n, 7)), pltpu.SemaphoreType.DMA((n, 7)), pltpu.SemaphoreType.DMA((n, 2))]
        + [pltpu.VMEM(a.shape, a.dtype) for a in arrs],
    )(*arrs)


_HBM = pl.BlockSpec(memory_space=pltpu.HBM)
_SEM = pl.BlockSpec(memory_space=pltpu.SEMAPHORE)
_SIDE_EFFECT = pltpu.SideEffectType.DATAFLOW_SIDE_EFFECTING


def _handshake(peers):
    barrier = pltpu.get_barrier_semaphore()
    for peer in peers:
        pl.semaphore_signal(barrier, inc=1, device_id=peer, device_id_type=MESH)
    pl.semaphore_wait(barrier, len(peers))


def _gather_start(buf, after, name, collective_id):
    def body(buf_ref, after_ref, send_sems, recv_sems, buf_thru, token):
        del after_ref, buf_thru
        x, y, c, chips = _position()
        rows = buf_ref.at[_dev(x, y, c)]
        targets = [(x, y, 1 - c)] + [(*chip, c) for chip in chips]
        _handshake(targets)
        for k, to in enumerate(targets):
            pltpu.make_async_remote_copy(src_ref=rows, dst_ref=rows, send_sem=send_sems.at[k], recv_sem=recv_sems.at[k],
                                         device_id=to, device_id_type=MESH).start()
        token[...] = jnp.zeros_like(token)

    return pl.pallas_call(
        body, name=name,
        out_shape=(pltpu.SemaphoreType.DMA((4,)), pltpu.SemaphoreType.DMA((4,)), pltpu.HBM(buf.shape, buf.dtype),
                   jax.ShapeDtypeStruct((8, 128), F32)),
        in_specs=(_HBM, pl.BlockSpec(memory_space=pl.ANY)),
        out_specs=(_SEM, _SEM, _HBM, pl.BlockSpec(memory_space=pltpu.VMEM)),
        input_output_aliases={0: 2},
        compiler_params=pltpu.CompilerParams(has_side_effects=_SIDE_EFFECT, collective_id=collective_id),
    )(pltpu.with_memory_space_constraint(buf, pltpu.HBM), after)


def _gather_forward(send_sems, recv_sems, buf, after, name, collective_id):
    def body(buf_ref, send_sems, recv_sems, after_ref, fwd_send, fwd_recv, buf_thru):
        del after_ref, buf_thru
        x, y, c, chips = _position()
        sibling = (x, y, 1 - c)

        def copy(block, k, sends, recvs):
            rows = buf_ref.at[_dev(*block)]
            return pltpu.make_async_remote_copy(src_ref=rows, dst_ref=rows, send_sem=sends.at[k], recv_sem=recvs.at[k],
                                                device_id=sibling, device_id_type=MESH)

        _handshake([sibling])
        for k in range(4):
            copy((x, y, c), k, send_sems, recv_sems).wait_send()
        copy(sibling, 0, send_sems, recv_sems).wait_recv()
        for j, chip in enumerate(chips):
            copy((*chip, c), 1 + j, send_sems, recv_sems).wait_recv()
            copy((*chip, c), j, fwd_send, fwd_recv).start()

    return pl.pallas_call(
        body, name=name,
        out_shape=(pltpu.SemaphoreType.DMA((3,)), pltpu.SemaphoreType.DMA((3,)), pltpu.HBM(buf.shape, buf.dtype)),
        in_specs=(_HBM, _SEM, _SEM, pl.BlockSpec(memory_space=pl.ANY)), out_specs=(_SEM, _SEM, _HBM),
        input_output_aliases={0: 2},
        compiler_params=pltpu.CompilerParams(has_side_effects=_SIDE_EFFECT, collective_id=collective_id),
    )(buf, send_sems, recv_sems, after)


def _gather_finish(fwd_send, fwd_recv, buf, name):
    def body(buf_ref, fwd_send, fwd_recv, buf_thru):
        del buf_thru
        x, y, c, chips = _position()
        for j, chip in enumerate(chips):
            cp = pltpu.make_async_remote_copy(
                src_ref=buf_ref.at[_dev(*chip, c)], dst_ref=buf_ref.at[_dev(*chip, 1 - c)], send_sem=fwd_send.at[j],
                recv_sem=fwd_recv.at[j], device_id=(x, y, 1 - c), device_id_type=MESH)
            cp.wait_send()
            cp.wait_recv()

    return pl.pallas_call(
        body, name=name,
        out_shape=pltpu.HBM(buf.shape, buf.dtype),
        in_specs=(_HBM, _SEM, _SEM), out_specs=_HBM,
        input_output_aliases={0: 0},
        compiler_params=pltpu.CompilerParams(has_side_effects=_SIDE_EFFECT),
    )(buf, fwd_send, fwd_recv)


def _everyone_else(x, y, c, chips):
    return [(x, y, 1 - c)] + [(*chip, core) for chip in chips for core in (c, 1 - c)]


def _broadcast_start(buf, name, collective_id):
    def body(buf_ref, send_sems, recv_sems, buf_thru, token):
        del buf_thru
        x, y, c, chips = _position()
        rows = buf_ref.at[_dev(x, y, c)]
        _handshake(_everyone_else(x, y, c, chips))
        for k, to in enumerate(_everyone_else(x, y, c, chips)):
            pltpu.make_async_remote_copy(src_ref=rows, dst_ref=rows, send_sem=send_sems.at[k], recv_sem=recv_sems.at[k],
                                         device_id=to, device_id_type=MESH).start()
        token[...] = jnp.zeros_like(token)

    return pl.pallas_call(
        body, name=name,
        out_shape=(pltpu.SemaphoreType.DMA((7,)), pltpu.SemaphoreType.DMA((7,)), pltpu.HBM(buf.shape, buf.dtype),
                   jax.ShapeDtypeStruct((8, 128), F32)),
        in_specs=(_HBM,), out_specs=(_SEM, _SEM, _HBM, pl.BlockSpec(memory_space=pltpu.VMEM)),
        input_output_aliases={0: 2},
        compiler_params=pltpu.CompilerParams(has_side_effects=_SIDE_EFFECT, collective_id=collective_id),
    )(pltpu.with_memory_space_constraint(buf, pltpu.HBM))


def _broadcast_wait(send_sems, recv_sems, buf, after, name):
    def body(buf_ref, send_sems, recv_sems, after_ref, buf_thru):
        del after_ref, buf_thru
        x, y, c, chips = _position()
        for k, peer in enumerate(_everyone_else(x, y, c, chips)):
            cp = pltpu.make_async_remote_copy(
                src_ref=buf_ref.at[_dev(x, y, c)], dst_ref=buf_ref.at[_dev(*peer)], send_sem=send_sems.at[k],
                recv_sem=recv_sems.at[k], device_id=peer, device_id_type=MESH)
            cp.wait_send()
            cp.wait_recv()

    return pl.pallas_call(
        body, name=name,
        out_shape=pltpu.HBM(buf.shape, buf.dtype),
        in_specs=(_HBM, _SEM, _SEM, pl.BlockSpec(memory_space=pl.ANY)), out_specs=_HBM,
        input_output_aliases={0: 0},
        compiler_params=pltpu.CompilerParams(has_side_effects=_SIDE_EFFECT),
    )(buf, send_sems, recv_sems, after)


def _to_sibling(j, x, y, c, chips):
    return _dev(*([(x, y)] + chips)[j], 1 - c), (x, y, 1 - c)


def _to_chip(j, x, y, c, chips):
    return j, (*chips[j], c)


def _exchange_start(srcs, n_slots, route, name, collective_id):
    n = len(srcs)

    def body(*refs):
        s_refs, land_refs = refs[:n], refs[n:2 * n]
        send_sems, recv_sems = refs[2 * n:2 * n + 2]
        token = refs[-1]
        x, y, c, chips = _position()
        _handshake([(x, y, 1 - c)] if route is _to_sibling else [route(j, x, y, c, chips)[1] for j in range(n_slots)])
        for k in range(n):
            for j in range(n_slots):
                block, to = route(j, x, y, c, chips)
                pltpu.make_async_remote_copy(
                    src_ref=s_refs[k].at[block], dst_ref=land_refs[k].at[j], send_sem=send_sems.at[n_slots * k + j],
                    recv_sem=recv_sems.at[n_slots * k + j], device_id=to, device_id_type=MESH).start()
        token[...] = jnp.zeros_like(token)

    lands = [jax.ShapeDtypeStruct((n_slots,) + s.shape[1:], s.dtype) for s in srcs]
    outs = pl.pallas_call(
        body, name=name,
        out_shape=(pltpu.SemaphoreType.DMA((n_slots * n,)), pltpu.SemaphoreType.DMA((n_slots * n,)),
                   *[pltpu.HBM(s.shape, s.dtype) for s in srcs], *[pltpu.HBM(l.shape, l.dtype) for l in lands],
                   jax.ShapeDtypeStruct((8, 128), F32)),
        in_specs=[_HBM] * (2 * n), out_specs=(_SEM, _SEM, *[_HBM] * (2 * n), pl.BlockSpec(memory_space=pltpu.VMEM)),
        input_output_aliases={k: 2 + k for k in range(2 * n)},
        compiler_params=pltpu.CompilerParams(has_side_effects=_SIDE_EFFECT, collective_id=collective_id),
    )(*[pltpu.with_memory_space_constraint(s, pltpu.HBM) for s in srcs],
      *[pltpu.with_memory_space_constraint(lax.empty(l.shape, l.dtype), pltpu.HBM) for l in lands])
    return outs[0], outs[1], outs[2:2 + n], outs[2 + n:2 + 2 * n], outs[-1]


def _exchange_wait(send_sems, recv_sems, s_thru, land_thru, after, n_slots, route, name):
    n = len(s_thru)

    def body(*refs):
        s_refs, land_refs = refs[:n], refs[n:2 * n]
        send_sems, recv_sems = refs[2 * n:2 * n + 2]
        x, y, c, chips = _position()
        for k in range(n):
            for j in range(n_slots):
                block, to = route(j, x, y, c, chips)
                cp = pltpu.make_async_remote_copy(
                    src_ref=s_refs[k].at[block], dst_ref=land_refs[k].at[j], send_sem=send_sems.at[n_slots * k + j],
                    recv_sem=recv_sems.at[n_slots * k + j], device_id=to, device_id_type=MESH)
                cp.wait_send()
                cp.wait_recv()

    outs = pl.pallas_call(
        body, name=name,
        out_shape=(*[pltpu.HBM(s.shape, s.dtype) for s in s_thru], *[pltpu.HBM(l.shape, l.dtype) for l in land_thru]),
        in_specs=[_HBM] * (2 * n) + [_SEM, _SEM, pl.BlockSpec(memory_space=pl.ANY)], out_specs=[_HBM] * (2 * n),
        input_output_aliases={k: k for k in range(2 * n)},
        compiler_params=pltpu.CompilerParams(has_side_effects=_SIDE_EFFECT),
    )(*s_thru, *land_thru, send_sems, recv_sems, after)
    return outs[:n], outs[n:]


def _owner_table():
    x, y, c = lax.axis_index("x"), lax.axis_index("y"), lax.axis_index("c")
    chips = [(x, y), (1 - x, y), (x, 1 - y), (1 - x, 1 - y)]
    return jnp.stack([_dev(px, py, c) for px, py in chips]).astype(jnp.int32)


def _chip_partial_sums(table, parts, from_sibling, name):
    n = len(parts)

    def body(tab_ref, *refs):
        del tab_ref
        for g_ref, l_ref, out_ref in zip(refs[:n], refs[n:2 * n], refs[2 * n:]):
            out_ref[...] = (g_ref[...].astype(F32) + l_ref[...].astype(F32)).astype(out_ref.dtype)

    block = lambda p: (None,) + p.shape[1:]
    grid_spec = pltpu.PrefetchScalarGridSpec(
        num_scalar_prefetch=1, grid=(3,),
        in_specs=[pl.BlockSpec(block(p), lambda j, tab: (tab[j + 1], 0, 0)) for p in parts]
        + [pl.BlockSpec(block(p), lambda j, tab: (j + 1, 0, 0)) for p in parts],
        out_specs=[pl.BlockSpec(block(p), lambda j, tab: (j, 0, 0)) for p in parts])
    return pl.pallas_call(
        body, name=name, grid_spec=grid_spec,
        out_shape=[jax.ShapeDtypeStruct((3,) + p.shape[1:], BF16) for p in parts],
        compiler_params=_params(("arbitrary",)),
    )(table, *parts, *from_sibling)


def _final_update(table, parts, from_sibling, from_chips, states, name):
    n = len(parts)
    updated = [k for k in range(n) if states[k] is not None]

    def body(tab_ref, *refs):
        del tab_ref
        ins, outs = refs[:3 * n + 3 * len(updated)], list(refs[3 * n + 3 * len(updated):])
        wmv = list(ins[3 * n:])
        for k in range(n):
            acc = ins[k][...].astype(F32) + ins[n + k][...].astype(F32)
            for j in range(3):
                acc = acc + ins[2 * n + k][j].astype(F32)
            outs.pop(0)[...] = acc
            if k in updated:
                w_ref, m_ref, v_ref = wmv[:3]
                del wmv[:3]
                for out_ref, val in zip(outs[:3], _adamw_update(w_ref[...], acc, m_ref[...], v_ref[...])):
                    out_ref[...] = val
                del outs[:3]

    half = lambda p: (p.shape[1] // 2, p.shape[2])
    rows = lambda p: pl.BlockSpec(half(p), lambda t, tab: (t, 0))
    grid_spec = pltpu.PrefetchScalarGridSpec(
        num_scalar_prefetch=1, grid=(2,),
        in_specs=[pl.BlockSpec((None,) + half(p), lambda t, tab: (tab[0], t, 0)) for p in parts]
        + [pl.BlockSpec((None,) + half(p), lambda t, tab: (0, t, 0)) for p in parts]
        + [pl.BlockSpec((3,) + half(p), lambda t, tab: (0, t, 0)) for p in parts]
        + [rows(parts[k]) for k in updated for _ in range(3)],
        out_specs=[rows(parts[k]) for k in range(n) for _ in range(4 if k in updated else 1)])
    outs = pl.pallas_call(
        body, name=name, grid_spec=grid_spec,
        out_shape=[jax.ShapeDtypeStruct(parts[k].shape[1:], F32) for k in range(n) for _ in range(4 if k in updated else 1)],
        compiler_params=_params(("arbitrary",)),
    )(table, *parts, *from_sibling, *from_chips, *[t for k in updated for t in states[k]])
    result = []
    for k in range(n):
        count = 4 if k in updated else 1
        result.append(outs[:count])
        outs = outs[count:]
    return result


def _sum_blocks(g8):
    _, rows, cols = g8.shape

    def body(g_ref, out_ref):
        acc = g_ref[0]
        for d in range(1, N_DEV):
            acc = acc + g_ref[d]
        out_ref[...] = acc

    return pl.pallas_call(
        body, name="small_grad_sum", grid=(1,),
        in_specs=[_full((N_DEV, rows, cols))], out_specs=_full((rows, cols)),
        out_shape=jax.ShapeDtypeStruct((rows, cols), F32),
        compiler_params=_params(("arbitrary",)),
    )(g8)


def _fwd_mix(x2d, gw, g_mix, conv_w, conv_b, ln_g, ln_b, pool_w, pool_scale, after, seq, tm):
    tokens = x2d.shape[0]
    n_tiles = tokens // tm
    tps = seq // tm

    def body(x_ref, gmix_ref, gw_hbm, cw_ref, cb_ref, lng_ref, lnb_ref, pw_ref, ps_ref, after_ref,
             x1_ref, u_ref, c_ref, pooled_ref, ymix_ref, h1_ref,
             win_v, wout_v, hc_carry, up_carry, sem):
        del after_ref
        i = pl.program_id(0)

        @pl.when(i == 0)
        def _():
            copies = _load_weight(gw_hbm, "w_in", win_v, sem) + _load_weight(gw_hbm, "w_out", wout_v, sem)
            for cp in copies:
                cp.start()
            for cp in copies:
                cp.wait()

        @pl.when(i % tps == 0)
        def _():
            hc_carry[...] = jnp.zeros_like(hc_carry)
            up_carry[...] = jnp.zeros_like(up_carry)

        x = x_ref[...]
        xh, _ = _rms_fwd(x)
        h1 = (xh * gmix_ref[...]).astype(BF16)
        h1_ref[...] = h1
        u = _dot_nt(h1, win_v[...])
        u_ref[...] = u
        val, gate, up = u[:, :D_CONV], u[:, D_CONV:2 * D_CONV], u[:, 2 * D_CONV:]

        hc = val * _sigmoid(gate)
        ext = jnp.concatenate([hc_carry[...], hc], axis=0)
        hc_carry[...] = hc[tm - CONV_HALO:, :]
        conv = jnp.broadcast_to(cb_ref[...], (tm, D_CONV))
        ahead_by = _sublane_shifts(ext)
        for k in range(CONV_WIDTH):
            whole, part = divmod(CONV_HALO - (CONV_WIDTH - 1) + k, 8)
            conv = conv + cw_ref[k:k + 1, :] * ahead_by[part][8 * whole:8 * whole + tm, :]
        c_ref[...] = conv
        mu = jnp.mean(conv, axis=-1, keepdims=True)
        cen = conv - mu
        ln = cen * lax.rsqrt(jnp.mean(cen * cen, axis=-1, keepdims=True) + EPS) * lng_ref[...] + lnb_ref[...]
        y_conv = ln * _sigmoid(ln)

        extp = jnp.concatenate([up_carry[...], up], axis=0)
        up_carry[...] = up[tm - POOL_HALO:, :]
        pos = lax.broadcasted_iota(jnp.int32, (tm, 1), 0) + (i % tps) * tm
        run = extp
        mixed = []
        for g, w in enumerate(POOL_WINDOWS):
            lo = g * POOL_GROUP_DIM
            run = run[:, POOL_GROUP_DIM if g else 0:]
            run = run + pltpu.roll(run, w // 2, 0)
            cnt = jnp.minimum(pos + 1, w).astype(F32)
            pooled = run[POOL_HALO:, :POOL_GROUP_DIM] / cnt - up[:, lo:lo + POOL_GROUP_DIM]
            pooled = pooled.astype(BF16)
            pooled_ref[:, lo:lo + POOL_GROUP_DIM] = pooled
            mixed.append(_dot(pooled, pw_ref[g].astype(BF16)))
        y_pool = jnp.concatenate(mixed, axis=-1) * ps_ref[...]

        ymix = jnp.concatenate([y_conv, y_pool], axis=-1).astype(BF16)
        ymix_ref[...] = ymix
        x1_ref[...] = x + _dot(ymix, wout_v[...])

    row = lambda w: pl.BlockSpec((tm, w), lambda i: (i, 0))
    return pl.pallas_call(
        body, name="fwd_mix", grid=(n_tiles,),
        in_specs=[row(D_MODEL), _full((1, D_MODEL)), pl.BlockSpec(memory_space=pl.ANY),
                  _full((CONV_WIDTH, D_CONV)), _full((1, D_CONV)), _full((1, D_CONV)), _full((1, D_CONV)),
                  _full((4, POOL_GROUP_DIM, POOL_GROUP_DIM)), _full((1, D_POOL)), _full(after.shape)],
        out_specs=[row(D_MODEL), row(D_IN), row(D_CONV), row(D_POOL), row(D_MODEL), row(D_MODEL)],
        out_shape=[jax.ShapeDtypeStruct((tokens, D_MODEL), F32), jax.ShapeDtypeStruct((tokens, D_IN), F32),
                   jax.ShapeDtypeStruct((tokens, D_CONV), F32), jax.ShapeDtypeStruct((tokens, D_POOL), BF16),
                   jax.ShapeDtypeStruct((tokens, D_MODEL), BF16), jax.ShapeDtypeStruct((tokens, D_MODEL), BF16)],
        scratch_shapes=[pltpu.VMEM((D_IN, D_MODEL), BF16), pltpu.VMEM((D_MODEL, D_MODEL), BF16),
                        pltpu.VMEM((CONV_HALO, D_CONV), F32), pltpu.VMEM((POOL_HALO, D_POOL), F32),
                        pltpu.SemaphoreType.DMA],
        compiler_params=_params(),
    )(x2d, g_mix, gw, conv_w, conv_b, ln_g, ln_b, pool_w, pool_scale, after)


def _fwd_kv(mem2d, gw, g_mem):
    rows = mem2d.shape[0]
    n_b = rows // N_MEM

    def body(mem_ref, g_ref, gw_hbm, mn_ref, kv_ref, wkv_v, sem):
        @pl.when(pl.program_id(0) == 0)
        def _():
            copies = _load_weight(gw_hbm, "w_kv", wkv_v, sem)
            for cp in copies:
                cp.start()
            for cp in copies:
                cp.wait()

        mh, _ = _rms_fwd(mem_ref[...])
        mn = (mh * g_ref[...]).astype(BF16)
        mn_ref[...] = mn
        kv_ref[...] = _dot_nt(mn, wkv_v[...]).astype(BF16)

    return pl.pallas_call(
        body, name="fwd_kv", grid=(n_b,),
        in_specs=[pl.BlockSpec((N_MEM, D_MODEL), lambda b: (b, 0)), _full((1, D_MODEL)), pl.BlockSpec(memory_space=pl.ANY)],
        out_specs=[pl.BlockSpec((N_MEM, D_MODEL), lambda b: (b, 0)), pl.BlockSpec((N_MEM, 2 * D_MODEL), lambda b: (b, 0))],
        out_shape=[jax.ShapeDtypeStruct((rows, D_MODEL), BF16), jax.ShapeDtypeStruct((rows, 2 * D_MODEL), BF16)],
        scratch_shapes=[pltpu.VMEM((2 * D_MODEL, D_MODEL), BF16), pltpu.SemaphoreType.DMA],
        compiler_params=_params(),
    )(mem2d, g_mem, gw)


def _softmax_rows(s):
    e = jnp.exp(s - jnp.max(s, axis=-1, keepdims=True))
    return e / jnp.sum(e, axis=-1, keepdims=True)


def _fwd_attn(x1, kv, gw, g_x, seq, tm):
    tokens = x1.shape[0]
    n_tiles = tokens // tm
    tps = seq // tm

    def body(x1_ref, kv_ref, g_ref, gw_hbm, x2_ref, h2_ref, q_ref, o_ref, wq_v, wo_v, sem):
        @pl.when(pl.program_id(0) == 0)
        def _():
            copies = _load_weight(gw_hbm, "w_q", wq_v, sem) + _load_weight(gw_hbm, "w_o", wo_v, sem)
            for cp in copies:
                cp.start()
            for cp in copies:
                cp.wait()

        x1v = x1_ref[...]
        xh, _ = _rms_fwd(x1v)
        h2 = (xh * g_ref[...]).astype(BF16)
        h2_ref[...] = h2
        q = (_dot(h2, wq_v[...]) * (HEAD_DIM ** -0.5)).astype(BF16)
        q_ref[...] = q
        heads = [slice(h * HEAD_DIM, (h + 1) * HEAD_DIM) for h in range(HEADS)]
        scores = [_dot_nt(q[:, hd], kv_ref[:, hd]) for hd in heads]
        probs = [_softmax_rows(s).astype(BF16) for s in scores]
        outs = [_dot(p, kv_ref[:, pl.ds(D_MODEL + h * HEAD_DIM, HEAD_DIM)]) for h, p in enumerate(probs)]
        o = jnp.concatenate(outs, axis=-1).astype(BF16)
        o_ref[...] = o
        x2_ref[...] = x1v + _dot(o, wo_v[...])

    row = lambda w: pl.BlockSpec((tm, w), lambda i: (i, 0))
    return pl.pallas_call(
        body, name="fwd_attn", grid=(n_tiles,),
        in_specs=[row(D_MODEL), pl.BlockSpec((N_MEM, 2 * D_MODEL), lambda i: (i // tps, 0)), _full((1, D_MODEL)),
                  pl.BlockSpec(memory_space=pl.ANY)],
        out_specs=[row(D_MODEL)] * 4,
        out_shape=[jax.ShapeDtypeStruct((tokens, D_MODEL), F32)] + [jax.ShapeDtypeStruct((tokens, D_MODEL), BF16)] * 3,
        scratch_shapes=[pltpu.VMEM((D_MODEL, D_MODEL), BF16), pltpu.VMEM((D_MODEL, D_MODEL), BF16), pltpu.SemaphoreType.DMA],
        compiler_params=_params(),
    )(x1, kv, g_x, gw)


def _ffn_conv(uu, halo, w_ref, b_ref, cols):
    ext = jnp.concatenate([halo, uu], axis=0)
    p1 = pltpu.roll(ext, 1, 0)[FFN_HALO:, :]
    p2 = pltpu.roll(ext, 2, 0)[FFN_HALO:, :]
    return b_ref[:, cols] + w_ref[2:3, cols] * uu + w_ref[1:2, cols] * p1 + w_ref[0:1, cols] * p2


def _fwd_ffn(x2, target, gw, g_ffn, ffn_w, ffn_b, g_final, seq, tm):
    tokens = x2.shape[0]
    n_tiles = tokens // tm
    tps = seq // tm
    n_chunks = D_FF // FFN_CHUNK

    def body(x2_ref, tgt_ref, gffn_ref, gw_hbm, fw_ref, fb_ref, gfin_ref,
             uu_ref, cc_ref, a_ref, h3_ref, dx3_ref, dx3b_ref, loss_ref, dgfin_ref,
             wup_v, wdown_v, carry, sem):
        i = pl.program_id(0)

        @pl.when(i == 0)
        def _():
            copies = _load_weight(gw_hbm, "w_up", wup_v, sem) + _load_weight(gw_hbm, "w_down", wdown_v, sem)
            for cp in copies:
                cp.start()
            for cp in copies:
                cp.wait()
            loss_ref[...] = jnp.zeros_like(loss_ref)
            dgfin_ref[...] = jnp.zeros_like(dgfin_ref)

        @pl.when(i % tps == 0)
        def _():
            carry[...] = jnp.zeros_like(carry)

        x2v = x2_ref[...]
        xh, _ = _rms_fwd(x2v)
        h3 = (xh * gffn_ref[...]).astype(BF16)
        h3_ref[...] = h3
        acc = jnp.zeros((tm, D_MODEL), F32)
        for jc in range(n_chunks):
            halves = []
            for half in range(2):
                cols = pl.ds(half * D_FF + jc * FFN_CHUNK, FFN_CHUNK)
                uu = _dot_nt(h3, wup_v[cols, :])
                uu_ref[:, cols] = uu.astype(BF16)
                cc = _ffn_conv(uu, carry[:, cols], fw_ref, fb_ref, cols)
                cc_ref[:, cols] = cc.astype(BF16)
                halves.append(cc)
                carry[:, cols] = uu[tm - FFN_HALO:, :]
            gate, val = halves
            a = (gate * _sigmoid(gate) * val).astype(BF16)
            a_ref[:, pl.ds(jc * FFN_CHUNK, FFN_CHUNK)] = a
            acc = acc + _dot(a, wdown_v[pl.ds(jc * FFN_CHUNK, FFN_CHUNK), :])
        x3 = x2v + acc

        xh3, r3 = _rms_fwd(x3)
        gfin = gfin_ref[...]
        err = xh3 * gfin - tgt_ref[...]
        loss_ref[...] += jnp.full(loss_ref.shape, jnp.sum(err * err) * (0.5 / D_MODEL), F32)
        dy = err * (1.0 / D_MODEL)
        dgfin_ref[...] += _colsum(dy * xh3)
        dx3 = _rms_bwd(dy, xh3, r3, gfin)
        dx3_ref[...] = dx3
        dx3b_ref[...] = dx3.astype(BF16)

    row = lambda w: pl.BlockSpec((tm, w), lambda i: (i, 0))
    return pl.pallas_call(
        body, name="fwd_ffn", grid=(n_tiles,),
        in_specs=[row(D_MODEL), row(D_MODEL), _full((1, D_MODEL)), pl.BlockSpec(memory_space=pl.ANY),
                  _full((FFN_CONV_WIDTH, 2 * D_FF)), _full((1, 2 * D_FF)), _full((1, D_MODEL))],
        out_specs=[row(2 * D_FF), row(2 * D_FF), row(D_FF), row(D_MODEL), row(D_MODEL), row(D_MODEL), _full((8, 128)),
                   _full((1, D_MODEL))],
        out_shape=[jax.ShapeDtypeStruct((tokens, 2 * D_FF), BF16), jax.ShapeDtypeStruct((tokens, 2 * D_FF), BF16),
                   jax.ShapeDtypeStruct((tokens, D_FF), BF16),
                   jax.ShapeDtypeStruct((tokens, D_MODEL), BF16), jax.ShapeDtypeStruct((tokens, D_MODEL), F32),
                   jax.ShapeDtypeStruct((tokens, D_MODEL), BF16),
                   jax.ShapeDtypeStruct((8, 128), F32), jax.ShapeDtypeStruct((1, D_MODEL), F32)],
        scratch_shapes=[pltpu.VMEM((2 * D_FF, D_MODEL), BF16), pltpu.VMEM((D_FF, D_MODEL), BF16),
                        pltpu.VMEM((FFN_HALO, 2 * D_FF), F32), pltpu.SemaphoreType.DMA],
        compiler_params=_params(),
    )(x2, target, g_ffn, gw, ffn_w, ffn_b, g_final)


def _bwd_ffn(dx3, x2, uu_all, cc_all, gw, g_ffn, ffn_w, seq, tm):
    tokens = x2.shape[0]
    n_tiles = tokens // tm
    tps = seq // tm
    n_chunks = D_FF // FFN_CHUNK

    def body(dx3_ref, x2_ref, uu_ref, cc_ref, gffn_ref, gw_hbm, fw_ref,
             dx2_ref, dx2b_ref, duu_ref, dfb_ref, dfw_ref, dg_ref,
             wup_v, wdown_v, carry, sem):
        i = pl.program_id(0)
        t = n_tiles - 1 - i

        @pl.when(i == 0)
        def _():
            copies = _load_weight(gw_hbm, "w_up", wup_v, sem) + _load_weight(gw_hbm, "w_down", wdown_v, sem)
            for cp in copies:
                cp.start()
            for cp in copies:
                cp.wait()
            dfb_ref[...] = jnp.zeros_like(dfb_ref)
            dfw_ref[...] = jnp.zeros_like(dfw_ref)
            dg_ref[...] = jnp.zeros_like(dg_ref)

        @pl.when(t % tps == tps - 1)
        def _():
            carry[...] = jnp.zeros_like(carry)

        dx3v = dx3_ref[...]
        dx3b = dx3v.astype(BF16)
        dh3 = jnp.zeros((tm, D_MODEL), F32)
        for jc in range(n_chunks):
            da = _dot_nt(dx3b, wdown_v[pl.ds(jc * FFN_CHUNK, FFN_CHUNK), :])
            colss = [pl.ds(half * D_FF + jc * FFN_CHUNK, FFN_CHUNK) for half in range(2)]
            gate, val = [cc_ref[:, cols].astype(F32) for cols in colss]
            sg = _sigmoid(gate)
            dgate = da * val * (sg * (1.0 + gate * (1.0 - sg)))
            dval = da * (gate * sg)
            for dcc, cols in zip((dgate, dval), colss):
                uu = uu_ref[:, cols].astype(F32)
                dfb_ref[:, cols] += _colsum(dcc)
                ext = jnp.concatenate([dcc, carry[:, cols]], axis=0)
                carry[:, cols] = dcc[:FFN_HALO, :]
                n1 = pltpu.roll(ext, tm + FFN_HALO - 1, 0)[:tm, :]
                n2 = pltpu.roll(ext, tm + FFN_HALO - 2, 0)[:tm, :]
                duu = fw_ref[2:3, cols] * dcc + fw_ref[1:2, cols] * n1 + fw_ref[0:1, cols] * n2
                dfw_ref[2:3, cols] += _colsum(uu * dcc)
                dfw_ref[1:2, cols] += _colsum(uu * n1)
                dfw_ref[0:1, cols] += _colsum(uu * n2)
                duub = duu.astype(BF16)
                duu_ref[:, cols] = duub
                dh3 = dh3 + _dot(duub, wup_v[cols, :])
        xh, r = _rms_fwd(x2_ref[...])
        dg_ref[...] += _colsum(dh3 * xh)
        dx2 = dx3v + _rms_bwd(dh3, xh, r, gffn_ref[...])
        dx2_ref[...] = dx2
        dx2b_ref[...] = dx2.astype(BF16)

    rev = lambda w: pl.BlockSpec((tm, w), lambda i: (n_tiles - 1 - i, 0))
    return pl.pallas_call(
        body, name="bwd_ffn", grid=(n_tiles,),
        in_specs=[rev(D_MODEL), rev(D_MODEL), rev(2 * D_FF), rev(2 * D_FF), _full((1, D_MODEL)),
                  pl.BlockSpec(memory_space=pl.ANY), _full((FFN_CONV_WIDTH, 2 * D_FF))],
        out_specs=[rev(D_MODEL), rev(D_MODEL), rev(2 * D_FF), _full((1, 2 * D_FF)), _full((FFN_CONV_WIDTH, 2 * D_FF)),
                   _full((1, D_MODEL))],
        out_shape=[jax.ShapeDtypeStruct((tokens, D_MODEL), F32), jax.ShapeDtypeStruct((tokens, D_MODEL), BF16),
                   jax.ShapeDtypeStruct((tokens, 2 * D_FF), BF16),
                   jax.ShapeDtypeStruct((1, 2 * D_FF), F32), jax.ShapeDtypeStruct((FFN_CONV_WIDTH, 2 * D_FF), F32),
                   jax.ShapeDtypeStruct((1, D_MODEL), F32)],
        scratch_shapes=[pltpu.VMEM((2 * D_FF, D_MODEL), BF16), pltpu.VMEM((D_FF, D_MODEL), BF16),
                        pltpu.VMEM((FFN_HALO, 2 * D_FF), F32), pltpu.SemaphoreType.DMA],
        compiler_params=_params(),
    )(dx3, x2, uu_all, cc_all, g_ffn, gw, ffn_w)


def _bwd_attn(dx2, x1, q, kv, gw, g_x, after, seq, tm):
    tokens = x1.shape[0]
    n_tiles = tokens // tm
    tps = seq // tm
    n_b = tokens // seq

    def body(dx2_ref, x1_ref, q_ref, kv_ref, g_ref, gw_hbm, after_ref, dx1_ref, dx1b_ref, dq_ref, dkv_ref, dg_ref,
             wq_v, wo_v, sem):
        del after_ref
        i = pl.program_id(0)

        @pl.when(i == 0)
        def _():
            copies = _load_weight(gw_hbm, "w_q", wq_v, sem) + _load_weight(gw_hbm, "w_o", wo_v, sem)
            for cp in copies:
                cp.start()
            for cp in copies:
                cp.wait()
            dg_ref[...] = jnp.zeros_like(dg_ref)

        @pl.when(i % tps == 0)
        def _():
            dkv_ref[...] = jnp.zeros_like(dkv_ref)

        dx2v = dx2_ref[...]
        do = _dot_nt(dx2v.astype(BF16), wo_v[...]).astype(BF16)
        q = q_ref[...]
        heads = [slice(h * HEAD_DIM, (h + 1) * HEAD_DIM) for h in range(HEADS)]
        kcols = [pl.ds(h * HEAD_DIM, HEAD_DIM) for h in range(HEADS)]
        vcols = [pl.ds(D_MODEL + h * HEAD_DIM, HEAD_DIM) for h in range(HEADS)]
        scores = [_dot_nt(q[:, hd], kv_ref[:, kc]) for hd, kc in zip(heads, kcols)]
        dps = [_dot_nt(do[:, hd], kv_ref[:, vc]) for hd, vc in zip(heads, vcols)]
        probs = [_softmax_rows(s) for s in scores]
        dss = [(p * (dp - jnp.sum(dp * p, axis=-1, keepdims=True))).astype(BF16) for p, dp in zip(probs, dps)]
        for p, hd, vc in zip(probs, heads, vcols):
            dkv_ref[:, vc] += _dot_tn(p.astype(BF16), do[:, hd])
        dqs = [_dot(ds, kv_ref[:, kc]) * (HEAD_DIM ** -0.5) for ds, kc in zip(dss, kcols)]
        for ds, hd, kc in zip(dss, heads, kcols):
            dkv_ref[:, kc] += _dot_tn(ds, q[:, hd])
        dq = jnp.concatenate(dqs, axis=-1).astype(BF16)
        dq_ref[...] = dq
        dh2 = _dot_nt(dq, wq_v[...])
        xh, r = _rms_fwd(x1_ref[...])
        dg_ref[...] += _colsum(dh2 * xh)
        dx1 = dx2v + _rms_bwd(dh2, xh, r, g_ref[...])
        dx1_ref[...] = dx1
        dx1b_ref[...] = dx1.astype(BF16)

    row = lambda w: pl.BlockSpec((tm, w), lambda i: (i, 0))
    per_b = pl.BlockSpec((N_MEM, 2 * D_MODEL), lambda i: (i // tps, 0))
    return pl.pallas_call(
        body, name="bwd_attn", grid=(n_tiles,),
        in_specs=[row(D_MODEL), row(D_MODEL), row(D_MODEL), per_b, _full((1, D_MODEL)), pl.BlockSpec(memory_space=pl.ANY),
                  _full(after.shape)],
        out_specs=[row(D_MODEL), row(D_MODEL), row(D_MODEL), per_b, _full((1, D_MODEL))],
        out_shape=[jax.ShapeDtypeStruct((tokens, D_MODEL), F32), jax.ShapeDtypeStruct((tokens, D_MODEL), BF16),
                   jax.ShapeDtypeStruct((tokens, D_MODEL), BF16),
                   jax.ShapeDtypeStruct((n_b * N_MEM, 2 * D_MODEL), F32), jax.ShapeDtypeStruct((1, D_MODEL), F32)],
        scratch_shapes=[pltpu.VMEM((D_MODEL, D_MODEL), BF16), pltpu.VMEM((D_MODEL, D_MODEL), BF16), pltpu.SemaphoreType.DMA],
        compiler_params=_params(),
    )(dx2, x1, q, kv, g_x, gw, after)


def _bwd_kv(dkv, mem2d, gw, g_mem):
    rows = mem2d.shape[0]
    n_b = rows // N_MEM

    def body(dkv_ref, mem_ref, gw_hbm, dkvb_ref, dg_ref, wkv_v, sem):
        @pl.when(pl.program_id(0) == 0)
        def _():
            copies = _load_weight(gw_hbm, "w_kv", wkv_v, sem)
            for cp in copies:
                cp.start()
            for cp in copies:
                cp.wait()
            dg_ref[...] = jnp.zeros_like(dg_ref)

        dkvb = dkv_ref[...].astype(BF16)
        dkvb_ref[...] = dkvb
        dmn = _dot(dkvb, wkv_v[...])
        mh, _ = _rms_fwd(mem_ref[...])
        dg_ref[...] += _colsum(dmn * mh)

    del g_mem
    return pl.pallas_call(
        body, name="bwd_kv", grid=(n_b,),
        in_specs=[pl.BlockSpec((N_MEM, 2 * D_MODEL), lambda b: (b, 0)), pl.BlockSpec((N_MEM, D_MODEL), lambda b: (b, 0)),
                  pl.BlockSpec(memory_space=pl.ANY)],
        out_specs=[pl.BlockSpec((N_MEM, 2 * D_MODEL), lambda b: (b, 0)), _full((1, D_MODEL))],
        out_shape=[jax.ShapeDtypeStruct((rows, 2 * D_MODEL), BF16), jax.ShapeDtypeStruct((1, D_MODEL), F32)],
        scratch_shapes=[pltpu.VMEM((2 * D_MODEL, D_MODEL), BF16), pltpu.SemaphoreType.DMA],
        compiler_params=_params(),
    )(dkv, mem2d, gw)


def _bwd_mix(dx1, x2d, u_all, c_all, pooled_all, gw, g_mix, conv_w, ln_g, ln_b, pool_w, pool_scale, after, seq, tm):
    tokens = x2d.shape[0]
    n_tiles = tokens // tm
    tps = seq // tm

    def body(dx1_ref, x_ref, u_ref, c_ref, pooled_ref, gmix_ref, gw_hbm, cw_ref, lng_ref, lnb_ref, pw_ref, ps_ref,
             after_ref, dx_ref, du_ref, dgmix_ref, dcw_ref, dcb_ref, dlng_ref, dlnb_ref, dpw_ref, dps_ref,
             win_v, wout_v, dc_carry, e_carry, sem):
        del after_ref
        i = pl.program_id(0)
        t = n_tiles - 1 - i

        @pl.when(i == 0)
        def _():
            copies = _load_weight(gw_hbm, "w_in", win_v, sem) + _load_weight(gw_hbm, "w_out", wout_v, sem)
            for cp in copies:
                cp.start()
            for cp in copies:
                cp.wait()
            for ref in (dgmix_ref, dcw_ref, dcb_ref, dlng_ref, dlnb_ref, dpw_ref, dps_ref):
                ref[...] = jnp.zeros_like(ref)

        @pl.when(t % tps == tps - 1)
        def _():
            dc_carry[...] = jnp.zeros_like(dc_carry)
            e_carry[...] = jnp.zeros_like(e_carry)

        dx1v = dx1_ref[...]
        dymix = _dot_nt(dx1v.astype(BF16), wout_v[...])
        dyc, dyp = dymix[:, :D_CONV], dymix[:, D_CONV:]
        u = u_ref[...]
        val, gate = u[:, :D_CONV], u[:, D_CONV:2 * D_CONV]

        conv = c_ref[...]
        mu = jnp.mean(conv, axis=-1, keepdims=True)
        cen = conv - mu
        rs = lax.rsqrt(jnp.mean(cen * cen, axis=-1, keepdims=True) + EPS)
        chat = cen * rs
        ln = chat * lng_ref[...] + lnb_ref[...]
        sl = _sigmoid(ln)
        dln = dyc * (sl * (1.0 + ln * (1.0 - sl)))
        dlng_ref[...] += _colsum(dln * chat)
        dlnb_ref[...] += _colsum(dln)
        dchat = dln * lng_ref[...]
        dc = rs * (dchat - jnp.mean(dchat, axis=-1, keepdims=True)
                   - chat * jnp.mean(dchat * chat, axis=-1, keepdims=True))
        dcb_ref[...] += _colsum(dc)
        sg = _sigmoid(gate)
        hc = val * sg
        ext = jnp.concatenate([dc, dc_carry[...]], axis=0)
        dc_carry[...] = dc[:CONV_HALO, :]
        dhc = jnp.zeros((tm, D_CONV), F32)
        ahead_by = _sublane_shifts(ext)
        for k in range(CONV_WIDTH):
            whole, part = divmod(CONV_WIDTH - 1 - k, 8)
            tap = ahead_by[part][8 * whole:8 * whole + tm, :]
            dhc = dhc + cw_ref[k:k + 1, :] * tap
            dcw_ref[k:k + 1, :] += _colsum_mxu(hc * tap)
        du_ref[:, :D_CONV] = (dhc * sg).astype(BF16)
        du_ref[:, D_CONV:2 * D_CONV] = (dhc * val * (sg * (1.0 - sg))).astype(BF16)

        pos = lax.broadcasted_iota(jnp.int32, (tm, 1), 0) + (t % tps) * tm
        es, dpooled = [], []
        for g, w in enumerate(POOL_WINDOWS):
            cols = pl.ds(g * POOL_GROUP_DIM, POOL_GROUP_DIM)
            lo = g * POOL_GROUP_DIM
            pooled = pooled_ref[:, cols]
            pw = pw_ref[g].astype(BF16)
            dyg = dyp[:, lo:lo + POOL_GROUP_DIM]
            dps_ref[:, cols] += _colsum(dyg * _dot(pooled, pw))
            dmixed = (dyg * ps_ref[:, cols]).astype(BF16)
            dpw_ref[g] += _dot_tn(pooled, dmixed)
            dpo = _dot_nt(dmixed, pw)
            dpooled.append(dpo)
            es.append(dpo / jnp.minimum(pos + 1, w).astype(F32))
        e = jnp.concatenate(es, axis=-1)
        run = jnp.concatenate([e, e_carry[...]], axis=0)
        e_carry[...] = e[:POOL_HALO, :]
        rows = tm + POOL_HALO
        for g, w in enumerate(POOL_WINDOWS):
            lo = g * POOL_GROUP_DIM
            run = run[:, POOL_GROUP_DIM if g else 0:]
            run = run + pltpu.roll(run, rows - w // 2, 0)
            du_ref[:, 2 * D_CONV + lo:2 * D_CONV + lo + POOL_GROUP_DIM] = (
                run[:tm, :POOL_GROUP_DIM] - dpooled[g]).astype(BF16)

        dh1 = _dot(du_ref[...], win_v[...])
        xh, r = _rms_fwd(x_ref[...])
        dgmix_ref[...] += _colsum(dh1 * xh)
        dx_ref[...] = dx1v + _rms_bwd(dh1, xh, r, gmix_ref[...])

    rev = lambda w: pl.BlockSpec((tm, w), lambda i: (n_tiles - 1 - i, 0))
    return pl.pallas_call(
        body, name="bwd_mix", grid=(n_tiles,),
        in_specs=[rev(D_MODEL), rev(D_MODEL), rev(D_IN), rev(D_CONV), rev(D_POOL), _full((1, D_MODEL)),
                  pl.BlockSpec(memory_space=pl.ANY), _full((CONV_WIDTH, D_CONV)), _full((1, D_CONV)), _full((1, D_CONV)),
                  _full((4, POOL_GROUP_DIM, POOL_GROUP_DIM)), _full((1, D_POOL)), _full(after.shape)],
        out_specs=[rev(D_MODEL), rev(D_IN), _full((1, D_MODEL)), _full((CONV_WIDTH, D_CONV)), _full((1, D_CONV)),
                   _full((1, D_CONV)), _full((1, D_CONV)), _full((4, POOL_GROUP_DIM, POOL_GROUP_DIM)), _full((1, D_POOL))],
        out_shape=[jax.ShapeDtypeStruct((tokens, D_MODEL), F32), jax.ShapeDtypeStruct((tokens, D_IN), BF16),
                   jax.ShapeDtypeStruct((1, D_MODEL), F32), jax.ShapeDtypeStruct((CONV_WIDTH, D_CONV), F32),
                   jax.ShapeDtypeStruct((1, D_CONV), F32), jax.ShapeDtypeStruct((1, D_CONV), F32),
                   jax.ShapeDtypeStruct((1, D_CONV), F32),
                   jax.ShapeDtypeStruct((4, POOL_GROUP_DIM, POOL_GROUP_DIM), F32), jax.ShapeDtypeStruct((1, D_POOL), F32)],
        scratch_shapes=[pltpu.VMEM((D_IN, D_MODEL), BF16), pltpu.VMEM((D_MODEL, D_MODEL), BF16),
                        pltpu.VMEM((CONV_HALO, D_CONV), F32), pltpu.VMEM((POOL_HALO, D_POOL), F32),
                        pltpu.SemaphoreType.DMA],
        compiler_params=_params(),
    )(dx1, x2d, u_all, c_all, pooled_all, g_mix, gw, conv_w, ln_g, ln_b, pool_w, pool_scale, after)


def _wgrad(a, b, name, after=None):
    tokens, m = a.shape
    n = b.shape[1]
    tm = 512 if m % 512 == 0 else 256
    extra = [] if after is None else [after]

    def body(a_ref, b_ref, *rest):
        rest[-1][...] = _dot_tn(a_ref[...], b_ref[...]).astype(rest[-1].dtype)

    return pl.pallas_call(
        body, name=name, grid=(m // tm,),
        in_specs=[pl.BlockSpec((tokens, tm), lambda i: (0, i)), _full((tokens, n))] + [_full(t.shape) for t in extra],
        out_specs=pl.BlockSpec((tm, n), lambda i: (i, 0)),
        out_shape=jax.ShapeDtypeStruct((m, n), BF16),
        compiler_params=_params(),
    )(a, b, *extra)


def _adamw_update(w, g, m, v):
    nm = ADAM_B1 * m + (1.0 - ADAM_B1) * g
    nv = ADAM_B2 * v + (1.0 - ADAM_B2) * (g * g)
    m_hat = nm / (1.0 - ADAM_B1 ** ADAM_STEP)
    v_hat = nv / (1.0 - ADAM_B2 ** ADAM_STEP)
    return -ADAM_LR * (m_hat / (jnp.sqrt(v_hat) + ADAM_EPS) + ADAM_WD * w), nm, nv


def _adamw_small(ws, gs, ms, vs):
    n = len(ws)

    def body(*refs):
        ins, outs = refs[:4 * n], refs[4 * n:]
        for k in range(n):
            d, nm, nv = _adamw_update(*[ins[j * n + k][...] for j in range(4)])
            outs[k][...] = d
            outs[n + k][...] = nm
            outs[2 * n + k][...] = nv

    vmem = pl.BlockSpec(memory_space=pltpu.VMEM)
    outs = pl.pallas_call(
        body, name="adamw_small",
        in_specs=[vmem] * (4 * n), out_specs=[vmem] * (3 * n),
        out_shape=[jax.ShapeDtypeStruct(w.shape, F32) for w in ws] * 3,
    )(*ws, *gs, *ms, *vs)
    return outs[:n], outs[n:2 * n], outs[2 * n:]


def _adamw(w, g, m, v, name):
    rows, cols = w.shape
    tile = rows
    for cand in (512, 256, 128, 64, 32, 16, 8):
        if rows % cand == 0:
            tile = cand
            break

    def body(w_ref, g_ref, m_ref, v_ref, d_ref, nm_ref, nv_ref):
        d_ref[...], nm_ref[...], nv_ref[...] = _adamw_update(w_ref[...], g_ref[...], m_ref[...], v_ref[...])

    spec = pl.BlockSpec((tile, cols), lambda i: (i, 0))
    return pl.pallas_call(
        body, name=name, grid=(rows // tile,),
        in_specs=[spec] * 4, out_specs=[spec] * 3,
        out_shape=[jax.ShapeDtypeStruct((rows, cols), F32)] * 3,
        compiler_params=_params(("arbitrary",)),
    )(w, g, m, v)


SMALL = (("norm_mix_g", (1, 1024)), ("conv_dw_b", (1, 512)), ("conv_ln_g", (1, 512)), ("conv_ln_b", (1, 512)),
         ("pool_w", (1, 4, 128, 128)), ("pool_scale", (1, 512)), ("norm_xattn_g", (1, 1024)), ("norm_mem_g", (1, 1024)),
         ("norm_ffn_g", (1, 1024)), ("ffn_dw_b", (1, 5632)), ("norm_final_g", (1024,)))
LANES = 128


def _pack_rows(arrs):
    flat = jnp.concatenate([a.reshape(-1) for a in arrs])
    pad = (-flat.shape[0]) % (8 * LANES)
    return jnp.pad(flat, (0, pad)).reshape(-1, LANES)


def kernel(x, mem, norm_mix_g, w_in, conv_dw_w, conv_dw_b, conv_ln_g, conv_ln_b, pool_w, pool_scale, w_out, norm_xattn_g, norm_mem_g, w_q, w_kv, w_o, norm_ffn_g, w_up, ffn_dw_w, ffn_dw_b, w_down, norm_final_g, loss_target, m_norm_mix_g, m_w_in, m_conv_dw_w, m_conv_dw_b, m_conv_ln_g, m_conv_ln_b, m_pool_w, m_pool_scale, m_w_out, m_norm_xattn_g, m_norm_mem_g, m_w_q, m_w_kv, m_w_o, m_norm_ffn_g, m_w_up, m_ffn_dw_w, m_ffn_dw_b, m_w_down, m_norm_final_g, v_norm_mix_g, v_w_in, v_conv_dw_w, v_conv_dw_b, v_conv_ln_g, v_conv_ln_b, v_pool_w, v_pool_scale, v_w_out, v_norm_xattn_g, v_norm_mem_g, v_w_q, v_w_kv, v_w_o, v_norm_ffn_g, v_w_up, v_ffn_dw_w, v_ffn_dw_b, v_w_down, v_norm_final_g):
    weights = dict(norm_mix_g=norm_mix_g, w_in=w_in, conv_dw_w=conv_dw_w, conv_dw_b=conv_dw_b, conv_ln_g=conv_ln_g,
                   conv_ln_b=conv_ln_b, pool_w=pool_w, pool_scale=pool_scale, w_out=w_out, norm_xattn_g=norm_xattn_g,
                   norm_mem_g=norm_mem_g, w_q=w_q, w_kv=w_kv, w_o=w_o, norm_ffn_g=norm_ffn_g, w_up=w_up,
                   ffn_dw_w=ffn_dw_w, ffn_dw_b=ffn_dw_b, w_down=w_down, norm_final_g=norm_final_g)
    moments_m = dict(norm_mix_g=m_norm_mix_g, w_in=m_w_in, conv_dw_w=m_conv_dw_w, conv_dw_b=m_conv_dw_b,
                     conv_ln_g=m_conv_ln_g, conv_ln_b=m_conv_ln_b, pool_w=m_pool_w, pool_scale=m_pool_scale,
                     w_out=m_w_out, norm_xattn_g=m_norm_xattn_g, norm_mem_g=m_norm_mem_g, w_q=m_w_q, w_kv=m_w_kv,
                     w_o=m_w_o, norm_ffn_g=m_norm_ffn_g, w_up=m_w_up, ffn_dw_w=m_ffn_dw_w, ffn_dw_b=m_ffn_dw_b,
                     w_down=m_w_down, norm_final_g=m_norm_final_g)
    moments_v = dict(norm_mix_g=v_norm_mix_g, w_in=v_w_in, conv_dw_w=v_conv_dw_w, conv_dw_b=v_conv_dw_b,
                     conv_ln_g=v_conv_ln_g, conv_ln_b=v_conv_ln_b, pool_w=v_pool_w, pool_scale=v_pool_scale,
                     w_out=v_w_out, norm_xattn_g=v_norm_xattn_g, norm_mem_g=v_norm_mem_g, w_q=v_w_q, w_kv=v_w_kv,
                     w_o=v_w_o, norm_ffn_g=v_norm_ffn_g, w_up=v_w_up, ffn_dw_w=v_ffn_dw_w, ffn_dw_b=v_ffn_dw_b,
                     w_down=v_w_down, norm_final_g=v_norm_final_g)
    order = list(weights)
    transposed = ("w_in", "w_kv", "w_up")

    n_b, seq, _ = x.shape
    tokens = n_b * seq
    tm_mix = min(512, seq // 2)
    tm_ffn = min(256, seq // 2)
    dev = 4 * lax.axis_index("x") + 2 * lax.axis_index("y") + lax.axis_index("c")

    packs = [jnp.concatenate([weights[n][0].T if n in transposed else weights[n][0] for n in names], axis=0).astype(BF16)
             for names in AG_GROUPS]
    small_sharded = _pack_rows([conv_dw_w[0], ffn_dw_w[0]])
    gw_mix, gsmall = _all_gather([packs[0], small_sharded], "weights_all_gather")
    flights = []
    after = gw_mix
    for k in (1, 2):
        own_in_place = lax.dynamic_update_slice(lax.empty((N_DEV,) + packs[k].shape, BF16), packs[k][None], (dev, 0, 0))
        flights.append(_gather_start(own_in_place, after, "weights_gather_start_%d" % k, BARRIER_IDS["gather_start"][k - 1]))
        after = flights[-1][3]
    gflat = gsmall.reshape(N_DEV, -1)
    n_cw = CONV_WIDTH * (D_CONV // N_DEV)
    n_fw = FFN_CONV_WIDTH * (2 * D_FF // N_DEV)
    conv_w = gflat[:, :n_cw].reshape(N_DEV, CONV_WIDTH, D_CONV // N_DEV).transpose(1, 0, 2).reshape(CONV_WIDTH, D_CONV)
    ffn_w = gflat[:, n_cw:n_cw + n_fw].reshape(N_DEV, FFN_CONV_WIDTH, 2 * D_FF // N_DEV).transpose(1, 0, 2).reshape(
        FFN_CONV_WIDTH, 2 * D_FF)

    x2d = x.reshape(tokens, D_MODEL)
    mem2d = mem.reshape(n_b * N_MEM, D_MODEL)
    tgt2d = loss_target.reshape(tokens, D_MODEL)
    g_final = norm_final_g.reshape(1, D_MODEL)

    def gather_finish(flight, after, tag):
        fwd_send, fwd_recv, buf = _gather_forward(*flight[:3], after, "weights_gather_forward_" + tag,
                                                  BARRIER_IDS["gather_forward"][int(tag) - 1])
        return _gather_finish(fwd_send, fwd_recv, buf, "weights_gather_finish_" + tag)

    x1, u_all, c_all, pooled_all, ymix, h1 = _fwd_mix(
        x2d, gw_mix, norm_mix_g, conv_w, conv_dw_b, conv_ln_g, conv_ln_b, pool_w[0], pool_scale, flights[1][3],
        seq, tm_mix)
    gw_attn = gather_finish(flights[0], x1, "1")
    mem_n, kv = _fwd_kv(mem2d, gw_attn, norm_mem_g)
    x2, h2, q, o = _fwd_attn(x1, kv, gw_attn, norm_xattn_g, seq, tm_mix)
    gw_ffn = gather_finish(flights[1], x2, "2")
    uu_all, cc_all, a_all, h3, dx3, dx3b, loss_part, dg_final = _fwd_ffn(
        x2, tgt2d, gw_ffn, norm_ffn_g, ffn_w, ffn_dw_b, g_final, seq, tm_ffn)

    table = _owner_table()

    def sibling_start(names, tag):
        parts = [part[n].reshape(N_DEV, W_OFF[n][1], D_MODEL) for n in names]
        return _exchange_start(parts, 4, _to_sibling, "rs_sibling_exchange_start_" + tag, BARRIER_IDS["sibling"][tag])

    def chips_start(flight, after, tag):
        parts, landed = _exchange_wait(*flight[:4], after, 4, _to_sibling, "rs_sibling_exchange_wait_" + tag)
        sums = _chip_partial_sums(table, parts, landed, "rs_chip_partial_sums_" + tag)
        return parts, landed, _exchange_start(sums, 3, _to_chip, "rs_chip_exchange_start_" + tag,
                                              BARRIER_IDS["chips"][tag])

    grads, delta, new_m, new_v = {}, {}, {}, {}

    def reduce_finish(names, parts, landed, flight, after, tag):
        _, from_chips = _exchange_wait(*flight[:4], after, 3, _to_chip, "rs_chip_exchange_wait_" + tag)
        as_rows = {n: n not in transposed or W_OFF[n][1] % LANES != 0 for n in names}
        states = [tuple(t[n][0].T if n in transposed else t[n][0] for t in (weights, moments_m, moments_v))
                  if as_rows[n] else None for n in names]
        results = _final_update(table, parts, landed, from_chips, states, "rs_final_update_" + tag)
        for n, res in zip(names, results):
            back = (lambda t: t.T[None]) if n in transposed else (lambda t: t[None])
            grads[n] = back(res[0])
            if as_rows[n]:
                delta[n], new_m[n], new_v[n] = [back(t) for t in res[1:]]
            else:
                delta[n], new_m[n], new_v[n] = [t[None] for t in _adamw(
                    weights[n][0], grads[n][0], moments_m[n][0], moments_v[n][0], "adamw_" + n)]
        alone = [n for n in names if not as_rows[n]]
        return delta[alone[-1] if alone else names[-1]]

    part = {}
    dx2, dx2b, duu, d_ffn_b, d_ffn_w, dg_ffn = _bwd_ffn(dx3, x2, uu_all, cc_all, gw_ffn, norm_ffn_g, ffn_w, seq, tm_ffn)
    part["w_up"] = _wgrad(duu, h3, "wgrad_w_up")
    part["w_down"] = _wgrad(a_all, dx3b, "wgrad_w_down")
    to_sibling_a = sibling_start(RS_GROUPS["a"], "a")
    dx1, dx1b, dq, dkv, dg_x = _bwd_attn(dx2, x1, q, kv, gw_attn, norm_xattn_g, to_sibling_a[4], seq, tm_mix)
    parts_a, landed_a, flight_a = chips_start(to_sibling_a, dx1, "a")
    dkv_b, dg_mem = _bwd_kv(dkv, mem2d, gw_attn, norm_mem_g)
    part["w_q"] = _wgrad(h2, dq, "wgrad_w_q", after=flight_a[4])
    part["w_kv"] = _wgrad(dkv_b, mem_n, "wgrad_w_kv")
    part["w_o"] = _wgrad(o, dx2b, "wgrad_w_o")
    to_sibling_b = sibling_start(RS_GROUPS["b"], "b")
    parts_b, landed_b, flight_b = chips_start(to_sibling_b, to_sibling_b[4], "b")
    dx, du, dg_mix, d_conv_w, d_conv_b, d_ln_g, d_ln_b, d_pool_w, d_pool_scale = _bwd_mix(
        dx1, x2d, u_all, c_all, pooled_all, gw_mix, norm_mix_g, conv_w, conv_ln_g, conv_ln_b, pool_w[0], pool_scale,
        flight_b[4], seq, tm_mix)
    grad_x = dx.reshape(x.shape)

    small_grads = dict(norm_mix_g=dg_mix, conv_dw_b=d_conv_b, conv_ln_g=d_ln_g, conv_ln_b=d_ln_b, pool_w=d_pool_w,
                       pool_scale=d_pool_scale, norm_xattn_g=dg_x, norm_mem_g=dg_mem, norm_ffn_g=dg_ffn,
                       ffn_dw_b=d_ffn_b, norm_final_g=dg_final)
    small_list = [small_grads[n] for n, _ in SMALL] + [d_conv_w, d_ffn_w, loss_part[:1]]
    small_mine = _pack_rows(small_list)
    small_flight = _broadcast_start(
        lax.dynamic_update_slice(lax.empty((N_DEV,) + small_mine.shape, F32), small_mine[None], (dev, 0, 0)),
        "small_grads_broadcast_start", BARRIER_IDS["broadcast"])

    part["w_in"] = _wgrad(du, h1, "wgrad_w_in", after=small_flight[3])
    part["w_out"] = _wgrad(ymix, dx1b, "wgrad_w_out")
    to_sibling_c = sibling_start(RS_GROUPS["c"], "c")
    parts_c, landed_c, flight_c = chips_start(to_sibling_c, to_sibling_c[4], "c")
    updated_a = reduce_finish(RS_GROUPS["a"], parts_a, landed_a, flight_a, flight_c[4], "a")
    updated_b = reduce_finish(RS_GROUPS["b"], parts_b, landed_b, flight_b, updated_a, "b")
    small_all = _broadcast_wait(*small_flight[:3], updated_b, "small_grads_broadcast_wait")
    small_sum = _sum_blocks(small_all).reshape(-1)

    pos = 0
    for n, shape in SMALL:
        size = 1
        for s in shape:
            size *= s
        grads[n] = small_sum[pos:pos + size].reshape(shape)
        pos += size
    full_conv_w = small_sum[pos:pos + CONV_WIDTH * D_CONV].reshape(CONV_WIDTH, D_CONV)
    pos += CONV_WIDTH * D_CONV
    full_ffn_w = small_sum[pos:pos + FFN_CONV_WIDTH * 2 * D_FF].reshape(FFN_CONV_WIDTH, 2 * D_FF)
    loss = small_sum[pos + FFN_CONV_WIDTH * 2 * D_FF]
    grads["conv_dw_w"] = lax.dynamic_slice_in_dim(full_conv_w, dev * (D_CONV // N_DEV), D_CONV // N_DEV, axis=1)[None]
    grads["ffn_dw_w"] = lax.dynamic_slice_in_dim(full_ffn_w, dev * (2 * D_FF // N_DEV), 2 * D_FF // N_DEV, axis=1)[None]

    small_names = [n for n in order if n not in W_OFF]
    swap = lambda t: jnp.transpose(t, (1, 0, 2))
    two_d = lambda t: t.reshape(1, -1) if t.ndim == 1 else (swap(t) if t.ndim == 3 else t)
    outs = _adamw_small(*[[two_d(t[n]) for n in small_names] for t in (weights, grads, moments_m, moments_v)])
    for res, out in zip((delta, new_m, new_v), outs):
        for n, o in zip(small_names, out):
            res[n] = swap(o) if o.ndim == 3 else o.reshape(weights[n].shape)

    reduce_finish(RS_GROUPS["c"], parts_c, landed_c, flight_c, delta[small_names[-1]], "c")

    return (loss, grad_x, *[grads[n] for n in order], *[delta[n] for n in order],
            *[new_m[n] for n in order], *[new_v[n] for n in order])
```

```python
import jax
import jax.numpy as jnp
from jax import lax
from jax.experimental import pallas as pl
from jax.experimental.pallas import tpu as pltpu

F32 = jnp.float32
BF16 = jnp.bfloat16
MESH = pl.DeviceIdType.MESH

N_DEV = 8
D_MODEL = 1024
D_CONV = 512
D_POOL = 512
CONV_WIDTH = 31
POOL_WINDOWS = (2, 4, 8, 16)
POOL_GROUP_DIM = 128
D_IN = 1536
N_MEM = 256
HEADS = 4
HEAD_DIM = 256
D_FF = 2816
FFN_CONV_WIDTH = 3
EPS = 1e-6
ADAM_LR = 0.001
ADAM_B1 = 0.9
ADAM_B2 = 0.999
ADAM_EPS = 1e-08
ADAM_WD = 0.01
ADAM_STEP = 10

VMEM_LIMIT_V7X = 56 * 1024 * 1024
CONV_HALO = 32
POOL_HALO = 16
FFN_HALO = 8
FFN_CHUNK = 2816

W_ROWS = (("w_in", 192), ("w_out", 128), ("w_q", 128), ("w_kv", 256), ("w_o", 128), ("w_up", 704), ("w_down", 352))
AG_GROUPS = (("w_in", "w_out"), ("w_q", "w_kv", "w_o"), ("w_up", "w_down"))
W_OFF = {}
for _names in AG_GROUPS:
    _o = 0
    for _n in _names:
        W_OFF[_n] = (_o, dict(W_ROWS)[_n])
        _o += dict(W_ROWS)[_n]
RS_GROUPS = {"a": ("w_up", "w_down"), "b": ("w_q", "w_kv", "w_o"), "c": ("w_in", "w_out")}
BARRIER_IDS = {"gather_start": (0, 1), "gather_forward": (2, 3), "sibling": {"a": 4, "b": 5, "c": 6},
               "chips": {"a": 7, "b": 8, "c": 9}, "broadcast": 10}


def _dot(a, b):
    return jnp.dot(a, b, preferred_element_type=F32)


def _dot_nt(a, b):
    return lax.dot_general(a, b, (((1,), (1,)), ((), ())), preferred_element_type=F32)


def _dot_tn(a, b):
    return lax.dot_general(a, b, (((0,), (0,)), ((), ())), preferred_element_type=F32)


def _sigmoid(v):
    return 1.0 / (1.0 + jnp.exp(-v))


def _rms_fwd(v):
    r = lax.rsqrt(jnp.mean(v * v, axis=-1, keepdims=True) + EPS)
    return v * r, r


def _rms_bwd(dh, vh, r, g):
    gd = dh * g
    return r * (gd - vh * jnp.mean(gd * vh, axis=-1, keepdims=True))


def _sublane_shifts(v):
    rows = v.shape[0]
    return [v] + [pltpu.roll(v, rows - b, 0) for b in range(1, 8)]


def _colsum(v):
    return jnp.sum(v, axis=0, keepdims=True)


def _colsum_mxu(v):
    return _dot(jnp.ones((8, v.shape[0]), BF16), v.astype(BF16))[0:1, :]


def _full(shape):
    return pl.BlockSpec(shape, lambda *_: (0,) * len(shape))


def _params(sem=("arbitrary",), vmem=VMEM_LIMIT_V7X):
    return pltpu.CompilerParams(dimension_semantics=sem, vmem_limit_bytes=vmem)


def _load_weight(g_hbm, name, dst, sem):
    off, rows = W_OFF[name]
    return [pltpu.make_async_copy(g_hbm.at[d, pl.ds(off, rows), :], dst.at[pl.ds(d * rows, rows), :], sem)
            for d in range(N_DEV)]


def _start_weights(g_hbm, names, dsts, sems):
    @pl.when(pl.program_id(0) == 0)
    def _():
        copies = [_load_weight(g_hbm, name, dst, sems.at[k]) for k, (name, dst) in enumerate(zip(names, dsts))]
        for cp in sum(copies, []):
            cp.start()
        for cp in sum(copies, []):
            cp.wait()


def _position():
    x, y, c = lax.axis_index("x"), lax.axis_index("y"), lax.axis_index("c")
    chips = [(1 - x, y), (x, 1 - y), (1 - x, 1 - y)]
    return x, y, c, chips


def _dev(px, py, pc):
    return 4 * px + 2 * py + pc


def _all_gather(arrs, name):
    n = len(arrs)

    def body(*refs):
        ins, outs = refs[:n], refs[n:2 * n]
        send_sems, recv_sems, local_sems = refs[2 * n:2 * n + 3]
        bounce = refs[2 * n + 3:]
        x, y, c, chips = _position()
        me, sibling = (x, y, c), (x, y, 1 - c)

        def copy(a, k, block, to, src=None):
            rows = outs[a].at[_dev(*block)]
            return pltpu.make_async_remote_copy(
                src_ref=rows if src is None else src, dst_ref=rows,
                send_sem=send_sems.at[a, k], recv_sem=recv_sems.at[a, k], device_id=to, device_id_type=MESH)

        sends = []
        for a in range(n):
            first = [copy(a, 0, me, sibling, src=ins[a])]
            first += [copy(a, 1 + j, me, (*chip, c), src=ins[a]) for j, chip in enumerate(chips)]
            for cp in first:
                cp.start()
            sends += first
        started = []
        for a in range(n):
            load = pltpu.make_async_copy(ins[a], bounce[a], local_sems.at[a, 0])
            load.start()
            load.wait()
            mine = pltpu.make_async_copy(bounce[a], outs[a].at[_dev(*me)], local_sems.at[a, 1])
            mine.start()
            started.append(mine)
        for j, chip in enumerate(chips):
            for a in range(n):
                copy(a, 1 + j, (*chip, c), me).wait_recv()
                passed = copy(a, 4 + j, (*chip, c), sibling)
                passed.start()
                sends.append(passed)
        for a in range(n):
            copy(a, 0, sibling, me).wait_recv()
            for j, chip in enumerate(chips):
                copy(a, 4 + j, (*chip, 1 - c), me).wait_recv()
        for cp in sends:
            cp.wait_send()
        for mine in started:
            mine.wait()

    any_spec = pl.BlockSpec(memory_space=pl.ANY)
    return pl.pallas_call(
        body, name=name,
        out_shape=[jax.ShapeDtypeStruct((N_DEV,) + a.shape, a.dtype) for a in arrs],
        in_specs=[any_spec] * n, out_specs=[any_spec] * n,
        scratch_shapes=[pltpu.SemaphoreType.DMA((n, 7)), pltpu.SemaphoreType.DMA((n, 7)), pltpu.SemaphoreType.DMA((n, 2))]
        + [pltpu.VMEM(a.shape, a.dtype) for a in arrs],
    )(*arrs)


_HBM = pl.BlockSpec(memory_space=pltpu.HBM)
_SEM = pl.BlockSpec(memory_space=pltpu.SEMAPHORE)
_SIDE_EFFECT = pltpu.SideEffectType.DATAFLOW_SIDE_EFFECTING


def _handshake(peers):
    barrier = pltpu.get_barrier_semaphore()
    for peer in peers:
        pl.semaphore_signal(barrier, inc=1, device_id=peer, device_id_type=MESH)
    pl.semaphore_wait(barrier, len(peers))


def _gather_start(buf, after, name, collective_id):
    def body(buf_ref, after_ref, send_sems, recv_sems, buf_thru, token):
        del after_ref, buf_thru
        x, y, c, chips = _position()
        rows = buf_ref.at[_dev(x, y, c)]
        targets = [(x, y, 1 - c)] + [(*chip, c) for chip in chips]
        _handshake(targets)
        for k, to in enumerate(targets):
            pltpu.make_async_remote_copy(src_ref=rows, dst_ref=rows, send_sem=send_sems.at[k], recv_sem=recv_sems.at[k],
                                         device_id=to, device_id_type=MESH).start()
        token[...] = jnp.zeros_like(token)

    return pl.pallas_call(
        body, name=name,
        out_shape=(pltpu.SemaphoreType.DMA((4,)), pltpu.SemaphoreType.DMA((4,)), pltpu.HBM(buf.shape, buf.dtype),
                   jax.ShapeDtypeStruct((8, 128), F32)),
        in_specs=(_HBM, pl.BlockSpec(memory_space=pl.ANY)),
        out_specs=(_SEM, _SEM, _HBM, pl.BlockSpec(memory_space=pltpu.VMEM)),
        input_output_aliases={0: 2},
        compiler_params=pltpu.CompilerParams(has_side_effects=_SIDE_EFFECT, collective_id=collective_id),
    )(pltpu.with_memory_space_constraint(buf, pltpu.HBM), after)


def _gather_forward(send_sems, recv_sems, buf, after, name, collective_id):
    def body(buf_ref, send_sems, recv_sems, after_ref, fwd_send, fwd_recv, buf_thru):
        del after_ref, buf_thru
        x, y, c, chips = _position()
        sibling = (x, y, 1 - c)

        def copy(block, k, sends, recvs):
            rows = buf_ref.at[_dev(*block)]
            return pltpu.make_async_remote_copy(src_ref=rows, dst_ref=rows, send_sem=sends.at[k], recv_sem=recvs.at[k],
                                                device_id=sibling, device_id_type=MESH)

        _handshake([sibling])
        for k in range(4):
            copy((x, y, c), k, send_sems, recv_sems).wait_send()
        copy(sibling, 0, send_sems, recv_sems).wait_recv()
        for j, chip in enumerate(chips):
            copy((*chip, c), 1 + j, send_sems, recv_sems).wait_recv()
            copy((*chip, c), j, fwd_send, fwd_recv).start()

    return pl.pallas_call(
        body, name=name,
        out_shape=(pltpu.SemaphoreType.DMA((3,)), pltpu.SemaphoreType.DMA((3,)), pltpu.HBM(buf.shape, buf.dtype)),
        in_specs=(_HBM, _SEM, _SEM, pl.BlockSpec(memory_space=pl.ANY)), out_specs=(_SEM, _SEM, _HBM),
        input_output_aliases={0: 2},
        compiler_params=pltpu.CompilerParams(has_side_effects=_SIDE_EFFECT, collective_id=collective_id),
    )(buf, send_sems, recv_sems, after)


def _gather_finish(fwd_send, fwd_recv, buf, name):
    def body(buf_ref, fwd_send, fwd_recv, buf_thru):
        del buf_thru
        x, y, c, chips = _position()
        for j, chip in enumerate(chips):
            cp = pltpu.make_async_remote_copy(
                src_ref=buf_ref.at[_dev(*chip, c)], dst_ref=buf_ref.at[_dev(*chip, 1 - c)], send_sem=fwd_send.at[j],
                recv_sem=fwd_recv.at[j], device_id=(x, y, 1 - c), device_id_type=MESH)
            cp.wait_send()
            cp.wait_recv()

    return pl.pallas_call(
        body, name=name,
        out_shape=pltpu.HBM(buf.shape, buf.dtype),
        in_specs=(_HBM, _SEM, _SEM), out_specs=_HBM,
        input_output_aliases={0: 0},
        compiler_params=pltpu.CompilerParams(has_side_effects=_SIDE_EFFECT),
    )(buf, fwd_send, fwd_recv)


def _everyone_else(x, y, c, chips):
    return [(x, y, 1 - c)] + [(*chip, core) for chip in chips for core in (c, 1 - c)]


def _broadcast_start(buf, name, collective_id):
    def body(buf_ref, send_sems, recv_sems, buf_thru, token):
        del buf_thru
        x, y, c, chips = _position()
        rows = buf_ref.at[_dev(x, y, c)]
        _handshake(_everyone_else(x, y, c, chips))
        for k, to in enumerate(_everyone_else(x, y, c, chips)):
            pltpu.make_async_remote_copy(src_ref=rows, dst_ref=rows, send_sem=send_sems.at[k], recv_sem=recv_sems.at[k],
                                         device_id=to, device_id_type=MESH).start()
        token[...] = jnp.zeros_like(token)

    return pl.pallas_call(
        body, name=name,
        out_shape=(pltpu.SemaphoreType.DMA((7,)), pltpu.SemaphoreType.DMA((7,)), pltpu.HBM(buf.shape, buf.dtype),
                   jax.ShapeDtypeStruct((8, 128), F32)),
        in_specs=(_HBM,), out_specs=(_SEM, _SEM, _HBM, pl.BlockSpec(memory_space=pltpu.VMEM)),
        input_output_aliases={0: 2},
        compiler_params=pltpu.CompilerParams(has_side_effects=_SIDE_EFFECT, collective_id=collective_id),
    )(pltpu.with_memory_space_constraint(buf, pltpu.HBM))


def _broadcast_wait(send_sems, recv_sems, buf, after, name):
    def body(buf_ref, send_sems, recv_sems, after_ref, buf_thru):
        del after_ref, buf_thru
        x, y, c, chips = _position()
        for k, peer in enumerate(_everyone_else(x, y, c, chips)):
            cp = pltpu.make_async_remote_copy(
                src_ref=buf_ref.at[_dev(x, y, c)], dst_ref=buf_ref.at[_dev(*peer)], send_sem=send_sems.at[k],
                recv_sem=recv_sems.at[k], device_id=peer, device_id_type=MESH)
            cp.wait_send()
            cp.wait_recv()

    return pl.pallas_call(
        body, name=name,
        out_shape=pltpu.HBM(buf.shape, buf.dtype),
        in_specs=(_HBM, _SEM, _SEM, pl.BlockSpec(memory_space=pl.ANY)), out_specs=_HBM,
        input_output_aliases={0: 0},
        compiler_params=pltpu.CompilerParams(has_side_effects=_SIDE_EFFECT),
    )(buf, send_sems, recv_sems, after)


def _to_sibling(j, x, y, c, chips):
    return _dev(*([(x, y)] + chips)[j], 1 - c), (x, y, 1 - c)


def _to_chip(j, x, y, c, chips):
    return j, (*chips[j], c)


def _exchange_start(srcs, n_slots, route, name, collective_id):
    n = len(srcs)

    def body(*refs):
        s_refs, land_refs = refs[:n], refs[n:2 * n]
        send_sems, recv_sems = refs[2 * n:2 * n + 2]
        token = refs[-1]
        x, y, c, chips = _position()
        _handshake([(x, y, 1 - c)] if route is _to_sibling else [route(j, x, y, c, chips)[1] for j in range(n_slots)])
        for k in range(n):
            for j in range(n_slots):
                block, to = route(j, x, y, c, chips)
                pltpu.make_async_remote_copy(
                    src_ref=s_refs[k].at[block], dst_ref=land_refs[k].at[j], send_sem=send_sems.at[n_slots * k + j],
                    recv_sem=recv_sems.at[n_slots * k + j], device_id=to, device_id_type=MESH).start()
        token[...] = jnp.zeros_like(token)

    lands = [jax.ShapeDtypeStruct((n_slots,) + s.shape[1:], s.dtype) for s in srcs]
    outs = pl.pallas_call(
        body, name=name,
        out_shape=(pltpu.SemaphoreType.DMA((n_slots * n,)), pltpu.SemaphoreType.DMA((n_slots * n,)),
                   *[pltpu.HBM(s.shape, s.dtype) for s in srcs], *[pltpu.HBM(l.shape, l.dtype) for l in lands],
                   jax.ShapeDtypeStruct((8, 128), F32)),
        in_specs=[_HBM] * (2 * n), out_specs=(_SEM, _SEM, *[_HBM] * (2 * n), pl.BlockSpec(memory_space=pltpu.VMEM)),
        input_output_aliases={k: 2 + k for k in range(2 * n)},
        compiler_params=pltpu.CompilerParams(has_side_effects=_SIDE_EFFECT, collective_id=collective_id),
    )(*[pltpu.with_memory_space_constraint(s, pltpu.HBM) for s in srcs],
      *[pltpu.with_memory_space_constraint(lax.empty(l.shape, l.dtype), pltpu.HBM) for l in lands])
    return outs[0], outs[1], outs[2:2 + n], outs[2 + n:2 + 2 * n], outs[-1]


def _exchange_wait(send_sems, recv_sems, s_thru, land_thru, after, n_slots, route, name):
    n = len(s_thru)

    def body(*refs):
        s_refs, land_refs = refs[:n], refs[n:2 * n]
        send_sems, recv_sems = refs[2 * n:2 * n + 2]
        x, y, c, chips = _position()
        for k in range(n):
            for j in range(n_slots):
                block, to = route(j, x, y, c, chips)
                cp = pltpu.make_async_remote_copy(
                    src_ref=s_refs[k].at[block], dst_ref=land_refs[k].at[j], send_sem=send_sems.at[n_slots * k + j],
                    recv_sem=recv_sems.at[n_slots * k + j], device_id=to, device_id_type=MESH)
                cp.wait_send()
                cp.wait_recv()

    outs = pl.pallas_call(
        body, name=name,
        out_shape=(*[pltpu.HBM(s.shape, s.dtype) for s in s_thru], *[pltpu.HBM(l.shape, l.dtype) for l in land_thru]),
        in_specs=[_HBM] * (2 * n) + [_SEM, _SEM, pl.BlockSpec(memory_space=pl.ANY)], out_specs=[_HBM] * (2 * n),
        input_output_aliases={k: k for k in range(2 * n)},
        compiler_params=pltpu.CompilerParams(has_side_effects=_SIDE_EFFECT),
    )(*s_thru, *land_thru, send_sems, recv_sems, after)
    return outs[:n], outs[n:]


def _owner_table():
    x, y, c = lax.axis_index("x"), lax.axis_index("y"), lax.axis_index("c")
    chips = [(x, y), (1 - x, y), (x, 1 - y), (1 - x, 1 - y)]
    return jnp.stack([_dev(px, py, c) for px, py in chips]).astype(jnp.int32)


def _chip_partial_sums(table, parts, from_sibling, name):
    n = len(parts)

    def body(tab_ref, *refs):
        del tab_ref
        for g_ref, l_ref, out_ref in zip(refs[:n], refs[n:2 * n], refs[2 * n:]):
            out_ref[...] = (g_ref[...].astype(F32) + l_ref[...].astype(F32)).astype(out_ref.dtype)

    block = lambda p: (None,) + p.shape[1:]
    grid_spec = pltpu.PrefetchScalarGridSpec(
        num_scalar_prefetch=1, grid=(3,),
        in_specs=[pl.BlockSpec(block(p), lambda j, tab: (tab[j + 1], 0, 0)) for p in parts]
        + [pl.BlockSpec(block(p), lambda j, tab: (j + 1, 0, 0)) for p in parts],
        out_specs=[pl.BlockSpec(block(p), lambda j, tab: (j, 0, 0)) for p in parts])
    return pl.pallas_call(
        body, name=name, grid_spec=grid_spec,
        out_shape=[jax.ShapeDtypeStruct((3,) + p.shape[1:], BF16) for p in parts],
        compiler_params=_params(("arbitrary",)),
    )(table, *parts, *from_sibling)


def _final_update(table, parts, from_sibling, from_chips, states, name):
    n = len(parts)
    updated = [k for k in range(n) if states[k] is not None]

    def body(tab_ref, *refs):
        del tab_ref
        ins, outs = refs[:3 * n + 3 * len(updated)], list(refs[3 * n + 3 * len(updated):])
        wmv = list(ins[3 * n:])
        for k in range(n):
            acc = ins[k][...].astype(F32) + ins[n + k][...].astype(F32)
            for j in range(3):
                acc = acc + ins[2 * n + k][j].astype(F32)
            outs.pop(0)[...] = acc
            if k in updated:
                w_ref, m_ref, v_ref = wmv[:3]
                del wmv[:3]
                for out_ref, val in zip(outs[:3], _adamw_update(w_ref[...], acc, m_ref[...], v_ref[...])):
                    out_ref[...] = val
                del outs[:3]

    half = lambda p: (p.shape[1] // 2, p.shape[2])
    rows = lambda p: pl.BlockSpec(half(p), lambda t, tab: (t, 0))
    grid_spec = pltpu.PrefetchScalarGridSpec(
        num_scalar_prefetch=1, grid=(2,),
        in_specs=[pl.BlockSpec((None,) + half(p), lambda t, tab: (tab[0], t, 0)) for p in parts]
        + [pl.BlockSpec((None,) + half(p), lambda t, tab: (0, t, 0)) for p in parts]
        + [pl.BlockSpec((3,) + half(p), lambda t, tab: (0, t, 0)) for p in parts]
        + [rows(parts[k]) for k in updated for _ in range(3)],
        out_specs=[rows(parts[k]) for k in range(n) for _ in range(4 if k in updated else 1)])
    outs = pl.pallas_call(
        body, name=name, grid_spec=grid_spec,
        out_shape=[jax.ShapeDtypeStruct(parts[k].shape[1:], F32) for k in range(n) for _ in range(4 if k in updated else 1)],
        compiler_params=_params(("arbitrary",)),
    )(table, *parts, *from_sibling, *from_chips, *[t for k in updated for t in states[k]])
    result = []
    for k in range(n):
        count = 4 if k in updated else 1
        result.append(outs[:count])
        outs = outs[count:]
    return result


def _sum_blocks(g8):
    _, rows, cols = g8.shape

    def body(g_ref, out_ref):
        acc = g_ref[0]
        for d in range(1, N_DEV):
            acc = acc + g_ref[d]
        out_ref[...] = acc

    return pl.pallas_call(
        body, name="small_grad_sum", grid=(1,),
        in_specs=[_full((N_DEV, rows, cols))], out_specs=_full((rows, cols)),
        out_shape=jax.ShapeDtypeStruct((rows, cols), F32),
        compiler_params=_params(("arbitrary",)),
    )(g8)


def _fwd_mix(x2d, gw, g_mix, conv_w, conv_b, ln_g, ln_b, pool_w, pool_scale, after, seq, tm):
    tokens = x2d.shape[0]
    n_tiles = tokens // tm
    tps = seq // tm

    def body(x_ref, gmix_ref, gw_hbm, cw_ref, cb_ref, lng_ref, lnb_ref, pw_ref, ps_ref, after_ref,
             x1_ref, u_ref, c_ref, pooled_ref, ymix_ref, h1_ref,
             win_v, wout_v, hc_carry, up_carry, sem):
        del after_ref
        i = pl.program_id(0)

        _start_weights(gw_hbm, ("w_in", "w_out"), (win_v, wout_v), sem)

        @pl.when(i % tps == 0)
        def _():
            hc_carry[...] = jnp.zeros_like(hc_carry)
            up_carry[...] = jnp.zeros_like(up_carry)

        x = x_ref[...]
        xh, _ = _rms_fwd(x)
        h1 = (xh * gmix_ref[...]).astype(BF16)
        h1_ref[...] = h1
        u = _dot_nt(h1, win_v[...])
        u_ref[...] = u
        val, gate, up = u[:, :D_CONV], u[:, D_CONV:2 * D_CONV], u[:, 2 * D_CONV:]

        extp = jnp.concatenate([up_carry[...], up], axis=0)
        up_carry[...] = up[tm - POOL_HALO:, :]
        pos = lax.broadcasted_iota(jnp.int32, (tm, 1), 0) + (i % tps) * tm
        run = extp
        mixed = []
        for g, w in enumerate(POOL_WINDOWS):
            lo = g * POOL_GROUP_DIM
            run = run[:, POOL_GROUP_DIM if g else 0:]
            run = run + pltpu.roll(run, w // 2, 0)
            cnt = jnp.minimum(pos + 1, w).astype(F32)
            pooled = run[POOL_HALO:, :POOL_GROUP_DIM] / cnt - up[:, lo:lo + POOL_GROUP_DIM]
            pooled = pooled.astype(BF16)
            pooled_ref[:, lo:lo + POOL_GROUP_DIM] = pooled
            mixed.append(_dot(pooled, pw_ref[g].astype(BF16)))
        y_pool = jnp.concatenate(mixed, axis=-1) * ps_ref[...]
        y_pool = y_pool.astype(BF16)
        ymix_ref[:, D_CONV:] = y_pool
        out = _dot(y_pool, wout_v[D_CONV:, :])

        hc = val * _sigmoid(gate)
        ext = jnp.concatenate([hc_carry[...], hc], axis=0)
        hc_carry[...] = hc[tm - CONV_HALO:, :]
        conv = jnp.broadcast_to(cb_ref[...], (tm, D_CONV))
        ahead_by = _sublane_shifts(ext)
        for k in range(CONV_WIDTH):
            whole, part = divmod(CONV_HALO - (CONV_WIDTH - 1) + k, 8)
            conv = conv + cw_ref[k:k + 1, :] * ahead_by[part][8 * whole:8 * whole + tm, :]
        c_ref[...] = conv
        mu = jnp.mean(conv, axis=-1, keepdims=True)
        cen = conv - mu
        ln = cen * lax.rsqrt(jnp.mean(cen * cen, axis=-1, keepdims=True) + EPS) * lng_ref[...] + lnb_ref[...]
        y_conv = ln * _sigmoid(ln)
        y_conv = y_conv.astype(BF16)
        ymix_ref[:, :D_CONV] = y_conv
        x1_ref[...] = x + (out + _dot(y_conv, wout_v[:D_CONV, :]))

    row = lambda w: pl.BlockSpec((tm, w), lambda i: (i, 0))
    return pl.pallas_call(
        body, name="fwd_mix", grid=(n_tiles,),
        in_specs=[row(D_MODEL), _full((1, D_MODEL)), pl.BlockSpec(memory_space=pl.ANY),
                  _full((CONV_WIDTH, D_CONV)), _full((1, D_CONV)), _full((1, D_CONV)), _full((1, D_CONV)),
                  _full((4, POOL_GROUP_DIM, POOL_GROUP_DIM)), _full((1, D_POOL)), _full(after.shape)],
        out_specs=[row(D_MODEL), row(D_IN), row(D_CONV), row(D_POOL), row(D_MODEL), row(D_MODEL)],
        out_shape=[jax.ShapeDtypeStruct((tokens, D_MODEL), F32), jax.ShapeDtypeStruct((tokens, D_IN), F32),
                   jax.ShapeDtypeStruct((tokens, D_CONV), F32), jax.ShapeDtypeStruct((tokens, D_POOL), BF16),
                   jax.ShapeDtypeStruct((tokens, D_MODEL), BF16), jax.ShapeDtypeStruct((tokens, D_MODEL), BF16)],
        scratch_shapes=[pltpu.VMEM((D_IN, D_MODEL), BF16), pltpu.VMEM((D_MODEL, D_MODEL), BF16),
                        pltpu.VMEM((CONV_HALO, D_CONV), F32), pltpu.VMEM((POOL_HALO, D_POOL), F32),
                        pltpu.SemaphoreType.DMA((2,))],
        compiler_params=_params(),
    )(x2d, g_mix, gw, conv_w, conv_b, ln_g, ln_b, pool_w, pool_scale, after)


def _fwd_kv(mem2d, gw, g_mem):
    rows = mem2d.shape[0]
    n_b = rows // N_MEM

    def body(mem_ref, g_ref, gw_hbm, mn_ref, kv_ref, wkv_v, sem):
        @pl.when(pl.program_id(0) == 0)
        def _():
            copies = _load_weight(gw_hbm, "w_kv", wkv_v, sem)
            for cp in copies:
                cp.start()
            for cp in copies:
                cp.wait()

        mh, _ = _rms_fwd(mem_ref[...])
        mn = (mh * g_ref[...]).astype(BF16)
        mn_ref[...] = mn
        kv_ref[...] = _dot_nt(mn, wkv_v[...]).astype(BF16)

    return pl.pallas_call(
        body, name="fwd_kv", grid=(n_b,),
        in_specs=[pl.BlockSpec((N_MEM, D_MODEL), lambda b: (b, 0)), _full((1, D_MODEL)), pl.BlockSpec(memory_space=pl.ANY)],
        out_specs=[pl.BlockSpec((N_MEM, D_MODEL), lambda b: (b, 0)), pl.BlockSpec((N_MEM, 2 * D_MODEL), lambda b: (b, 0))],
        out_shape=[jax.ShapeDtypeStruct((rows, D_MODEL), BF16), jax.ShapeDtypeStruct((rows, 2 * D_MODEL), BF16)],
        scratch_shapes=[pltpu.VMEM((2 * D_MODEL, D_MODEL), BF16), pltpu.SemaphoreType.DMA],
        compiler_params=_params(),
    )(mem2d, g_mem, gw)


def _softmax_rows(s):
    e = jnp.exp(s - jnp.max(s, axis=-1, keepdims=True))
    return e / jnp.sum(e, axis=-1, keepdims=True)


def _fwd_attn(x1, kv, gw, g_x, seq, tm):
    tokens = x1.shape[0]
    n_tiles = tokens // tm
    tps = seq // tm

    def body(x1_ref, kv_ref, g_ref, gw_hbm, x2_ref, h2_ref, q_ref, o_ref, wq_v, wo_v, sem):
        _start_weights(gw_hbm, ("w_q", "w_o"), (wq_v, wo_v), sem)
        x1v = x1_ref[...]
        xh, _ = _rms_fwd(x1v)
        h2 = (xh * g_ref[...]).astype(BF16)
        h2_ref[...] = h2
        q = (_dot(h2, wq_v[...]) * (HEAD_DIM ** -0.5)).astype(BF16)
        q_ref[...] = q
        heads = [slice(h * HEAD_DIM, (h + 1) * HEAD_DIM) for h in range(HEADS)]
        scores = [_dot_nt(q[:, hd], kv_ref[:, hd]) for hd in heads]
        probs = [_softmax_rows(s).astype(BF16) for s in scores]
        outs = [_dot(p, kv_ref[:, pl.ds(D_MODEL + h * HEAD_DIM, HEAD_DIM)]) for h, p in enumerate(probs)]
        o = jnp.concatenate(outs, axis=-1).astype(BF16)
        o_ref[...] = o
        x2_ref[...] = x1v + _dot(o, wo_v[...])

    row = lambda w: pl.BlockSpec((tm, w), lambda i: (i, 0))
    return pl.pallas_call(
        body, name="fwd_attn", grid=(n_tiles,),
        in_specs=[row(D_MODEL), pl.BlockSpec((N_MEM, 2 * D_MODEL), lambda i: (i // tps, 0)), _full((1, D_MODEL)),
                  pl.BlockSpec(memory_space=pl.ANY)],
        out_specs=[row(D_MODEL)] * 4,
        out_shape=[jax.ShapeDtypeStruct((tokens, D_MODEL), F32)] + [jax.ShapeDtypeStruct((tokens, D_MODEL), BF16)] * 3,
        scratch_shapes=[pltpu.VMEM((D_MODEL, D_MODEL), BF16), pltpu.VMEM((D_MODEL, D_MODEL), BF16), pltpu.SemaphoreType.DMA((2,))],
        compiler_params=_params(),
    )(x1, kv, g_x, gw)


def _ffn_conv(uu, halo, w_ref, b_ref, cols):
    ext = jnp.concatenate([halo, uu], axis=0)
    p1 = pltpu.roll(ext, 1, 0)[FFN_HALO:, :]
    p2 = pltpu.roll(ext, 2, 0)[FFN_HALO:, :]
    return b_ref[:, cols] + w_ref[2:3, cols] * uu + w_ref[1:2, cols] * p1 + w_ref[0:1, cols] * p2


def _fwd_ffn(x2, target, gw, g_ffn, ffn_w, ffn_b, g_final, seq, tm):
    tokens = x2.shape[0]
    n_tiles = tokens // tm
    tps = seq // tm
    n_chunks = D_FF // FFN_CHUNK

    def body(x2_ref, tgt_ref, gffn_ref, gw_hbm, fw_ref, fb_ref, gfin_ref,
             uu_ref, cc_ref, a_ref, h3_ref, dx3_ref, dx3b_ref, loss_ref, dgfin_ref,
             wup_v, wdown_v, carry, sem):
        i = pl.program_id(0)

        _start_weights(gw_hbm, ("w_up", "w_down"), (wup_v, wdown_v), sem)

        @pl.when(i == 0)
        def _():
            loss_ref[...] = jnp.zeros_like(loss_ref)
            dgfin_ref[...] = jnp.zeros_like(dgfin_ref)

        @pl.when(i % tps == 0)
        def _():
            carry[...] = jnp.zeros_like(carry)

        x2v = x2_ref[...]
        xh, _ = _rms_fwd(x2v)
        h3 = (xh * gffn_ref[...]).astype(BF16)
        h3_ref[...] = h3
        acc = jnp.zeros((tm, D_MODEL), F32)
        for jc in range(n_chunks):
            halves = []
            for half in range(2):
                cols = pl.ds(half * D_FF + jc * FFN_CHUNK, FFN_CHUNK)
                uu = _dot_nt(h3, wup_v[cols, :])
                uu_ref[:, cols] = uu.astype(BF16)
                cc = _ffn_conv(uu, carry[:, cols], fw_ref, fb_ref, cols)
                cc_ref[:, cols] = cc.astype(BF16)
                halves.append(cc)
                carry[:, cols] = uu[tm - FFN_HALO:, :]
            gate, val = halves
            a = (gate * _sigmoid(gate) * val).astype(BF16)
            a_ref[:, pl.ds(jc * FFN_CHUNK, FFN_CHUNK)] = a
            acc = acc + _dot(a, wdown_v[pl.ds(jc * FFN_CHUNK, FFN_CHUNK), :])
        x3 = x2v + acc

        xh3, r3 = _rms_fwd(x3)
        gfin = gfin_ref[...]
        err = xh3 * gfin - tgt_ref[...]
        loss_ref[...] += jnp.full(loss_ref.shape, jnp.sum(err * err) * (0.5 / D_MODEL), F32)
        dy = err * (1.0 / D_MODEL)
        dgfin_ref[...] += _colsum(dy * xh3)
        dx3 = _rms_bwd(dy, xh3, r3, gfin)
        dx3_ref[...] = dx3
        dx3b_ref[...] = dx3.astype(BF16)

    row = lambda w: pl.BlockSpec((tm, w), lambda i: (i, 0))
    return pl.pallas_call(
        body, name="fwd_ffn", grid=(n_tiles,),
        in_specs=[row(D_MODEL), row(D_MODEL), _full((1, D_MODEL)), pl.BlockSpec(memory_space=pl.ANY),
                  _full((FFN_CONV_WIDTH, 2 * D_FF)), _full((1, 2 * D_FF)), _full((1, D_MODEL))],
        out_specs=[row(2 * D_FF), row(2 * D_FF), row(D_FF), row(D_MODEL), row(D_MODEL), row(D_MODEL), _full((8, 128)),
                   _full((1, D_MODEL))],
        out_shape=[jax.ShapeDtypeStruct((tokens, 2 * D_FF), BF16), jax.ShapeDtypeStruct((tokens, 2 * D_FF), BF16),
                   jax.ShapeDtypeStruct((tokens, D_FF), BF16),
                   jax.ShapeDtypeStruct((tokens, D_MODEL), BF16), jax.ShapeDtypeStruct((tokens, D_MODEL), F32),
                   jax.ShapeDtypeStruct((tokens, D_MODEL), BF16),
                   jax.ShapeDtypeStruct((8, 128), F32), jax.ShapeDtypeStruct((1, D_MODEL), F32)],
        scratch_shapes=[pltpu.VMEM((2 * D_FF, D_MODEL), BF16), pltpu.VMEM((D_FF, D_MODEL), BF16),
                        pltpu.VMEM((FFN_HALO, 2 * D_FF), F32), pltpu.SemaphoreType.DMA((2,))],
        compiler_params=_params(),
    )(x2, target, g_ffn, gw, ffn_w, ffn_b, g_final)


def _bwd_ffn(dx3, x2, uu_all, cc_all, gw, g_ffn, ffn_w, seq, tm):
    tokens = x2.shape[0]
    n_tiles = tokens // tm
    tps = seq // tm
    n_chunks = D_FF // FFN_CHUNK

    def body(dx3_ref, x2_ref, uu_ref, cc_ref, gffn_ref, gw_hbm, fw_ref,
             dx2_ref, dx2b_ref, duu_ref, dfb_ref, dfw_ref, dg_ref,
             wup_v, wdown_v, carry, sem):
        i = pl.program_id(0)
        t = n_tiles - 1 - i

        _start_weights(gw_hbm, ("w_down", "w_up"), (wdown_v, wup_v), sem)

        @pl.when(i == 0)
        def _():
            dfb_ref[...] = jnp.zeros_like(dfb_ref)
            dfw_ref[...] = jnp.zeros_like(dfw_ref)
            dg_ref[...] = jnp.zeros_like(dg_ref)

        @pl.when(t % tps == tps - 1)
        def _():
            carry[...] = jnp.zeros_like(carry)

        dx3v = dx3_ref[...]
        dx3b = dx3v.astype(BF16)
        dh3 = jnp.zeros((tm, D_MODEL), F32)
        for jc in range(n_chunks):
            da = _dot_nt(dx3b, wdown_v[pl.ds(jc * FFN_CHUNK, FFN_CHUNK), :])
            colss = [pl.ds(half * D_FF + jc * FFN_CHUNK, FFN_CHUNK) for half in range(2)]
            gate, val = [cc_ref[:, cols].astype(F32) for cols in colss]
            sg = _sigmoid(gate)
            dgate = da * val * (sg * (1.0 + gate * (1.0 - sg)))
            dval = da * (gate * sg)
            for dcc, cols in zip((dgate, dval), colss):
                uu = uu_ref[:, cols].astype(F32)
                dfb_ref[:, cols] += _colsum(dcc)
                ext = jnp.concatenate([dcc, carry[:, cols]], axis=0)
                carry[:, cols] = dcc[:FFN_HALO, :]
                n1 = pltpu.roll(ext, tm + FFN_HALO - 1, 0)[:tm, :]
                n2 = pltpu.roll(ext, tm + FFN_HALO - 2, 0)[:tm, :]
                duu = fw_ref[2:3, cols] * dcc + fw_ref[1:2, cols] * n1 + fw_ref[0:1, cols] * n2
                dfw_ref[2:3, cols] += _colsum(uu * dcc)
                dfw_ref[1:2, cols] += _colsum(uu * n1)
                dfw_ref[0:1, cols] += _colsum(uu * n2)
                duub = duu.astype(BF16)
                duu_ref[:, cols] = duub
                dh3 = dh3 + _dot(duub, wup_v[cols, :])
        xh, r = _rms_fwd(x2_ref[...])
        dg_ref[...] += _colsum(dh3 * xh)
        dx2 = dx3v + _rms_bwd(dh3, xh, r, gffn_ref[...])
        dx2_ref[...] = dx2
        dx2b_ref[...] = dx2.astype(BF16)

    rev = lambda w: pl.BlockSpec((tm, w), lambda i: (n_tiles - 1 - i, 0))
    return pl.pallas_call(
        body, name="bwd_ffn", grid=(n_tiles,),
        in_specs=[rev(D_MODEL), rev(D_MODEL), rev(2 * D_FF), rev(2 * D_FF), _full((1, D_MODEL)),
                  pl.BlockSpec(memory_space=pl.ANY), _full((FFN_CONV_WIDTH, 2 * D_FF))],
        out_specs=[rev(D_MODEL), rev(D_MODEL), rev(2 * D_FF), _full((1, 2 * D_FF)), _full((FFN_CONV_WIDTH, 2 * D_FF)),
                   _full((1, D_MODEL))],
        out_shape=[jax.ShapeDtypeStruct((tokens, D_MODEL), F32), jax.ShapeDtypeStruct((tokens, D_MODEL), BF16),
                   jax.ShapeDtypeStruct((tokens, 2 * D_FF), BF16),
                   jax.ShapeDtypeStruct((1, 2 * D_FF), F32), jax.ShapeDtypeStruct((FFN_CONV_WIDTH, 2 * D_FF), F32),
                   jax.ShapeDtypeStruct((1, D_MODEL), F32)],
        scratch_shapes=[pltpu.VMEM((2 * D_FF, D_MODEL), BF16), pltpu.VMEM((D_FF, D_MODEL), BF16),
                        pltpu.VMEM((FFN_HALO, 2 * D_FF), F32), pltpu.SemaphoreType.DMA((2,))],
        compiler_params=_params(),
    )(dx3, x2, uu_all, cc_all, g_ffn, gw, ffn_w)


def _bwd_attn(dx2, x1, q, kv, gw, g_x, after, seq, tm):
    tokens = x1.shape[0]
    n_tiles = tokens // tm
    tps = seq // tm
    n_b = tokens // seq

    def body(dx2_ref, x1_ref, q_ref, kv_ref, g_ref, gw_hbm, after_ref, dx1_ref, dx1b_ref, dq_ref, dkv_ref, dg_ref,
             wq_v, wo_v, sem):
        del after_ref
        i = pl.program_id(0)

        _start_weights(gw_hbm, ("w_o", "w_q"), (wo_v, wq_v), sem)

        @pl.when(i == 0)
        def _():
            dg_ref[...] = jnp.zeros_like(dg_ref)

        @pl.when(i % tps == 0)
        def _():
            dkv_ref[...] = jnp.zeros_like(dkv_ref)

        dx2v = dx2_ref[...]
        do = _dot_nt(dx2v.astype(BF16), wo_v[...]).astype(BF16)
        q = q_ref[...]
        heads = [slice(h * HEAD_DIM, (h + 1) * HEAD_DIM) for h in range(HEADS)]
        kcols = [pl.ds(h * HEAD_DIM, HEAD_DIM) for h in range(HEADS)]
        vcols = [pl.ds(D_MODEL + h * HEAD_DIM, HEAD_DIM) for h in range(HEADS)]
        scores = [_dot_nt(q[:, hd], kv_ref[:, kc]) for hd, kc in zip(heads, kcols)]
        dps = [_dot_nt(do[:, hd], kv_ref[:, vc]) for hd, vc in zip(heads, vcols)]
        probs = [_softmax_rows(s) for s in scores]
        dss = [(p * (dp - jnp.sum(dp * p, axis=-1, keepdims=True))).astype(BF16) for p, dp in zip(probs, dps)]
        for p, hd, vc in zip(probs, heads, vcols):
            dkv_ref[:, vc] += _dot_tn(p.astype(BF16), do[:, hd])
        dqs = [_dot(ds, kv_ref[:, kc]) * (HEAD_DIM ** -0.5) for ds, kc in zip(dss, kcols)]
        for ds, hd, kc in zip(dss, heads, kcols):
            dkv_ref[:, kc] += _dot_tn(ds, q[:, hd])
        dq = jnp.concatenate(dqs, axis=-1).astype(BF16)
        dq_ref[...] = dq
        dh2 = _dot_nt(dq, wq_v[...])
        xh, r = _rms_fwd(x1_ref[...])
        dg_ref[...] += _colsum(dh2 * xh)
        dx1 = dx2v + _rms_bwd(dh2, xh, r, g_ref[...])
        dx1_ref[...] = dx1
        dx1b_ref[...] = dx1.astype(BF16)

    row = lambda w: pl.BlockSpec((tm, w), lambda i: (i, 0))
    per_b = pl.BlockSpec((N_MEM, 2 * D_MODEL), lambda i: (i // tps, 0))
    return pl.pallas_call(
        body, name="bwd_attn", grid=(n_tiles,),
        in_specs=[row(D_MODEL), row(D_MODEL), row(D_MODEL), per_b, _full((1, D_MODEL)), pl.BlockSpec(memory_space=pl.ANY),
                  _full(after.shape)],
        out_specs=[row(D_MODEL), row(D_MODEL), row(D_MODEL), per_b, _full((1, D_MODEL))],
        out_shape=[jax.ShapeDtypeStruct((tokens, D_MODEL), F32), jax.ShapeDtypeStruct((tokens, D_MODEL), BF16),
                   jax.ShapeDtypeStruct((tokens, D_MODEL), BF16),
                   jax.ShapeDtypeStruct((n_b * N_MEM, 2 * D_MODEL), F32), jax.ShapeDtypeStruct((1, D_MODEL), F32)],
        scratch_shapes=[pltpu.VMEM((D_MODEL, D_MODEL), BF16), pltpu.VMEM((D_MODEL, D_MODEL), BF16), pltpu.SemaphoreType.DMA((2,))],
        compiler_params=_params(),
    )(dx2, x1, q, kv, g_x, gw, after)


def _bwd_kv(dkv, mem2d, gw):
    rows = mem2d.shape[0]
    n_b = rows // N_MEM

    def body(dkv_ref, mem_ref, gw_hbm, dkvb_ref, dg_ref, wkv_v, sem):
        @pl.when(pl.program_id(0) == 0)
        def _():
            copies = _load_weight(gw_hbm, "w_kv", wkv_v, sem)
            for cp in copies:
                cp.start()
            for cp in copies:
                cp.wait()
            dg_ref[...] = jnp.zeros_like(dg_ref)

        dkvb = dkv_ref[...].astype(BF16)
        dkvb_ref[...] = dkvb
        dmn = _dot(dkvb, wkv_v[...])
        mh, _ = _rms_fwd(mem_ref[...])
        dg_ref[...] += _colsum(dmn * mh)

    return pl.pallas_call(
        body, name="bwd_kv", grid=(n_b,),
        in_specs=[pl.BlockSpec((N_MEM, 2 * D_MODEL), lambda b: (b, 0)), pl.BlockSpec((N_MEM, D_MODEL), lambda b: (b, 0)),
                  pl.BlockSpec(memory_space=pl.ANY)],
        out_specs=[pl.BlockSpec((N_MEM, 2 * D_MODEL), lambda b: (b, 0)), _full((1, D_MODEL))],
        out_shape=[jax.ShapeDtypeStruct((rows, 2 * D_MODEL), BF16), jax.ShapeDtypeStruct((1, D_MODEL), F32)],
        scratch_shapes=[pltpu.VMEM((2 * D_MODEL, D_MODEL), BF16), pltpu.SemaphoreType.DMA],
        compiler_params=_params(),
    )(dkv, mem2d, gw)


def _bwd_mix(dx1, x2d, u_all, c_all, pooled_all, gw, g_mix, conv_w, ln_g, ln_b, pool_w, pool_scale, after, seq, tm):
    tokens = x2d.shape[0]
    n_tiles = tokens // tm
    tps = seq // tm

    def body(dx1_ref, x_ref, u_ref, c_ref, pooled_ref, gmix_ref, gw_hbm, cw_ref, lng_ref, lnb_ref, pw_ref, ps_ref,
             after_ref, dx_ref, du_ref, dgmix_ref, dcw_ref, dcb_ref, dlng_ref, dlnb_ref, dpw_ref, dps_ref,
             win_v, wout_v, dc_carry, e_carry, sem):
        del after_ref
        i = pl.program_id(0)
        t = n_tiles - 1 - i

        _start_weights(gw_hbm, ("w_out", "w_in"), (wout_v, win_v), sem)

        @pl.when(i == 0)
        def _():
            for ref in (dgmix_ref, dcw_ref, dcb_ref, dlng_ref, dlnb_ref, dpw_ref, dps_ref):
                ref[...] = jnp.zeros_like(ref)

        @pl.when(t % tps == tps - 1)
        def _():
            dc_carry[...] = jnp.zeros_like(dc_carry)
            e_carry[...] = jnp.zeros_like(e_carry)

        dx1v = dx1_ref[...]
        dymix = _dot_nt(dx1v.astype(BF16), wout_v[...])
        dyc, dyp = dymix[:, :D_CONV], dymix[:, D_CONV:]
        u = u_ref[...]
        val, gate = u[:, :D_CONV], u[:, D_CONV:2 * D_CONV]

        conv = c_ref[...]
        mu = jnp.mean(conv, axis=-1, keepdims=True)
        cen = conv - mu
        rs = lax.rsqrt(jnp.mean(cen * cen, axis=-1, keepdims=True) + EPS)
        chat = cen * rs
        ln = chat * lng_ref[...] + lnb_ref[...]
        sl = _sigmoid(ln)
        dln = dyc * (sl * (1.0 + ln * (1.0 - sl)))
        dlng_ref[...] += _colsum(dln * chat)
        dlnb_ref[...] += _colsum(dln)
        dchat = dln * lng_ref[...]
        dc = rs * (dchat - jnp.mean(dchat, axis=-1, keepdims=True)
                   - chat * jnp.mean(dchat * chat, axis=-1, keepdims=True))
        dcb_ref[...] += _colsum(dc)
        sg = _sigmoid(gate)
        hc = val * sg
        ext = jnp.concatenate([dc, dc_carry[...]], axis=0)
        dc_carry[...] = dc[:CONV_HALO, :]
        dhc = jnp.zeros((tm, D_CONV), F32)
        ahead_by = _sublane_shifts(ext)
        for k in range(CONV_WIDTH):
            whole, part = divmod(CONV_WIDTH - 1 - k, 8)
            tap = ahead_by[part][8 * whole:8 * whole + tm, :]
            dhc = dhc + cw_ref[k:k + 1, :] * tap
            dcw_ref[k:k + 1, :] += _colsum_mxu(hc * tap)
        du_ref[:, :D_CONV] = (dhc * sg).astype(BF16)
        du_ref[:, D_CONV:2 * D_CONV] = (dhc * val * (sg * (1.0 - sg))).astype(BF16)

        pos = lax.broadcasted_iota(jnp.int32, (tm, 1), 0) + (t % tps) * tm
        es, dpooled = [], []
        for g, w in enumerate(POOL_WINDOWS):
            cols = pl.ds(g * POOL_GROUP_DIM, POOL_GROUP_DIM)
            lo = g * POOL_GROUP_DIM
            pooled = pooled_ref[:, cols]
            pw = pw_ref[g].astype(BF16)
            dyg = dyp[:, lo:lo + POOL_GROUP_DIM]
            dps_ref[:, cols] += _colsum(dyg * _dot(pooled, pw))
            dmixed = (dyg * ps_ref[:, cols]).astype(BF16)
            dpw_ref[g] += _dot_tn(pooled, dmixed)
            dpo = _dot_nt(dmixed, pw)
            dpooled.append(dpo)
            es.append(dpo / jnp.minimum(pos + 1, w).astype(F32))
        e = jnp.concatenate(es, axis=-1)
        run = jnp.concatenate([e, e_carry[...]], axis=0)
        e_carry[...] = e[:POOL_HALO, :]
        rows = tm + POOL_HALO
        for g, w in enumerate(POOL_WINDOWS):
            lo = g * POOL_GROUP_DIM
            run = run[:, POOL_GROUP_DIM if g else 0:]
            run = run + pltpu.roll(run, rows - w // 2, 0)
            du_ref[:, 2 * D_CONV + lo:2 * D_CONV + lo + POOL_GROUP_DIM] = (
                run[:tm, :POOL_GROUP_DIM] - dpooled[g]).astype(BF16)

        dh1 = _dot(du_ref[...], win_v[...])
        xh, r = _rms_fwd(x_ref[...])
        dgmix_ref[...] += _colsum(dh1 * xh)
        dx_ref[...] = dx1v + _rms_bwd(dh1, xh, r, gmix_ref[...])

    rev = lambda w: pl.BlockSpec((tm, w), lambda i: (n_tiles - 1 - i, 0))
    return pl.pallas_call(
        body, name="bwd_mix", grid=(n_tiles,),
        in_specs=[rev(D_MODEL), rev(D_MODEL), rev(D_IN), rev(D_CONV), rev(D_POOL), _full((1, D_MODEL)),
                  pl.BlockSpec(memory_space=pl.ANY), _full((CONV_WIDTH, D_CONV)), _full((1, D_CONV)), _full((1, D_CONV)),
                  _full((4, POOL_GROUP_DIM, POOL_GROUP_DIM)), _full((1, D_POOL)), _full(after.shape)],
        out_specs=[rev(D_MODEL), rev(D_IN), _full((1, D_MODEL)), _full((CONV_WIDTH, D_CONV)), _full((1, D_CONV)),
                   _full((1, D_CONV)), _full((1, D_CONV)), _full((4, POOL_GROUP_DIM, POOL_GROUP_DIM)), _full((1, D_POOL))],
        out_shape=[jax.ShapeDtypeStruct((tokens, D_MODEL), F32), jax.ShapeDtypeStruct((tokens, D_IN), BF16),
                   jax.ShapeDtypeStruct((1, D_MODEL), F32), jax.ShapeDtypeStruct((CONV_WIDTH, D_CONV), F32),
                   jax.ShapeDtypeStruct((1, D_CONV), F32), jax.ShapeDtypeStruct((1, D_CONV), F32),
                   jax.ShapeDtypeStruct((1, D_CONV), F32),
                   jax.ShapeDtypeStruct((4, POOL_GROUP_DIM, POOL_GROUP_DIM), F32), jax.ShapeDtypeStruct((1, D_POOL), F32)],
        scratch_shapes=[pltpu.VMEM((D_IN, D_MODEL), BF16), pltpu.VMEM((D_MODEL, D_MODEL), BF16),
                        pltpu.VMEM((CONV_HALO, D_CONV), F32), pltpu.VMEM((POOL_HALO, D_POOL), F32),
                        pltpu.SemaphoreType.DMA((2,))],
        compiler_params=_params(),
    )(dx1, x2d, u_all, c_all, pooled_all, g_mix, gw, conv_w, ln_g, ln_b, pool_w, pool_scale, after)


def _wgrad(a, b, name, after=None):
    tokens, m = a.shape
    n = b.shape[1]
    tm = 512 if m % 512 == 0 else 256
    extra = [] if after is None else [after]

    def body(a_ref, b_ref, *rest):
        rest[-1][...] = _dot_tn(a_ref[...], b_ref[...]).astype(rest[-1].dtype)

    return pl.pallas_call(
        body, name=name, grid=(m // tm,),
        in_specs=[pl.BlockSpec((tokens, tm), lambda i: (0, i)), _full((tokens, n))] + [_full(t.shape) for t in extra],
        out_specs=pl.BlockSpec((tm, n), lambda i: (i, 0)),
        out_shape=jax.ShapeDtypeStruct((m, n), BF16),
        compiler_params=_params(),
    )(a, b, *extra)


def _adamw_update(w, g, m, v):
    nm = ADAM_B1 * m + (1.0 - ADAM_B1) * g
    nv = ADAM_B2 * v + (1.0 - ADAM_B2) * (g * g)
    m_hat = nm / (1.0 - ADAM_B1 ** ADAM_STEP)
    v_hat = nv / (1.0 - ADAM_B2 ** ADAM_STEP)
    return -ADAM_LR * (m_hat / (jnp.sqrt(v_hat) + ADAM_EPS) + ADAM_WD * w), nm, nv


def _adamw_small(ws, gs, ms, vs):
    n = len(ws)

    def body(*refs):
        ins, outs = refs[:4 * n], refs[4 * n:]
        for k in range(n):
            d, nm, nv = _adamw_update(*[ins[j * n + k][...] for j in range(4)])
            outs[k][...] = d
            outs[n + k][...] = nm
            outs[2 * n + k][...] = nv

    vmem = pl.BlockSpec(memory_space=pltpu.VMEM)
    outs = pl.pallas_call(
        body, name="adamw_small",
        in_specs=[vmem] * (4 * n), out_specs=[vmem] * (3 * n),
        out_shape=[jax.ShapeDtypeStruct(w.shape, F32) for w in ws] * 3,
    )(*ws, *gs, *ms, *vs)
    return outs[:n], outs[n:2 * n], outs[2 * n:]


def _adamw(w, g, m, v, name):
    rows, cols = w.shape
    tile = rows
    for cand in (512, 256, 128, 64, 32, 16, 8):
        if rows % cand == 0:
            tile = cand
            break

    def body(w_ref, g_ref, m_ref, v_ref, d_ref, nm_ref, nv_ref):
        d_ref[...], nm_ref[...], nv_ref[...] = _adamw_update(w_ref[...], g_ref[...], m_ref[...], v_ref[...])

    spec = pl.BlockSpec((tile, cols), lambda i: (i, 0))
    return pl.pallas_call(
        body, name=name, grid=(rows // tile,),
        in_specs=[spec] * 4, out_specs=[spec] * 3,
        out_shape=[jax.ShapeDtypeStruct((rows, cols), F32)] * 3,
        compiler_params=_params(("arbitrary",)),
    )(w, g, m, v)


SMALL = (("norm_mix_g", (1, 1024)), ("conv_dw_b", (1, 512)), ("conv_ln_g", (1, 512)), ("conv_ln_b", (1, 512)),
         ("pool_w", (1, 4, 128, 128)), ("pool_scale", (1, 512)), ("norm_xattn_g", (1, 1024)), ("norm_mem_g", (1, 1024)),
         ("norm_ffn_g", (1, 1024)), ("ffn_dw_b", (1, 5632)), ("norm_final_g", (1024,)))
LANES = 128


def _pack_rows(arrs):
    flat = jnp.concatenate([a.reshape(-1) for a in arrs])
    pad = (-flat.shape[0]) % (8 * LANES)
    return jnp.pad(flat, (0, pad)).reshape(-1, LANES)


def kernel(x, mem, norm_mix_g, w_in, conv_dw_w, conv_dw_b, conv_ln_g, conv_ln_b, pool_w, pool_scale, w_out, norm_xattn_g, norm_mem_g, w_q, w_kv, w_o, norm_ffn_g, w_up, ffn_dw_w, ffn_dw_b, w_down, norm_final_g, loss_target, m_norm_mix_g, m_w_in, m_conv_dw_w, m_conv_dw_b, m_conv_ln_g, m_conv_ln_b, m_pool_w, m_pool_scale, m_w_out, m_norm_xattn_g, m_norm_mem_g, m_w_q, m_w_kv, m_w_o, m_norm_ffn_g, m_w_up, m_ffn_dw_w, m_ffn_dw_b, m_w_down, m_norm_final_g, v_norm_mix_g, v_w_in, v_conv_dw_w, v_conv_dw_b, v_conv_ln_g, v_conv_ln_b, v_pool_w, v_pool_scale, v_w_out, v_norm_xattn_g, v_norm_mem_g, v_w_q, v_w_kv, v_w_o, v_norm_ffn_g, v_w_up, v_ffn_dw_w, v_ffn_dw_b, v_w_down, v_norm_final_g):
    weights = dict(norm_mix_g=norm_mix_g, w_in=w_in, conv_dw_w=conv_dw_w, conv_dw_b=conv_dw_b, conv_ln_g=conv_ln_g,
                   conv_ln_b=conv_ln_b, pool_w=pool_w, pool_scale=pool_scale, w_out=w_out, norm_xattn_g=norm_xattn_g,
                   norm_mem_g=norm_mem_g, w_q=w_q, w_kv=w_kv, w_o=w_o, norm_ffn_g=norm_ffn_g, w_up=w_up,
                   ffn_dw_w=ffn_dw_w, ffn_dw_b=ffn_dw_b, w_down=w_down, norm_final_g=norm_final_g)
    moments_m = dict(norm_mix_g=m_norm_mix_g, w_in=m_w_in, conv_dw_w=m_conv_dw_w, conv_dw_b=m_conv_dw_b,
                     conv_ln_g=m_conv_ln_g, conv_ln_b=m_conv_ln_b, pool_w=m_pool_w, pool_scale=m_pool_scale,
                     w_out=m_w_out, norm_xattn_g=m_norm_xattn_g, norm_mem_g=m_norm_mem_g, w_q=m_w_q, w_kv=m_w_kv,
                     w_o=m_w_o, norm_ffn_g=m_norm_ffn_g, w_up=m_w_up, ffn_dw_w=m_ffn_dw_w, ffn_dw_b=m_ffn_dw_b,
                     w_down=m_w_down, norm_final_g=m_norm_final_g)
    moments_v = dict(norm_mix_g=v_norm_mix_g, w_in=v_w_in, conv_dw_w=v_conv_dw_w, conv_dw_b=v_conv_dw_b,
                     conv_ln_g=v_conv_ln_g, conv_ln_b=v_conv_ln_b, pool_w=v_pool_w, pool_scale=v_pool_scale,
                     w_out=v_w_out, norm_xattn_g=v_norm_xattn_g, norm_mem_g=v_norm_mem_g, w_q=v_w_q, w_kv=v_w_kv,
                     w_o=v_w_o, norm_ffn_g=v_norm_ffn_g, w_up=v_w_up, ffn_dw_w=v_ffn_dw_w, ffn_dw_b=v_ffn_dw_b,
                     w_down=v_w_down, norm_final_g=v_norm_final_g)
    order = list(weights)
    transposed = ("w_in", "w_kv", "w_up")

    n_b, seq, _ = x.shape
    tokens = n_b * seq
    tm_mix = min(512, seq // 2)
    tm_ffn = min(256, seq // 2)
    dev = 4 * lax.axis_index("x") + 2 * lax.axis_index("y") + lax.axis_index("c")

    packs = [jnp.concatenate([weights[n][0].T if n in transposed else weights[n][0] for n in names], axis=0).astype(BF16)
             for names in AG_GROUPS]
    small_sharded = _pack_rows([conv_dw_w[0], ffn_dw_w[0]])
    gw_mix, gsmall = _all_gather([packs[0], small_sharded], "weights_all_gather")
    flights = []
    after = gw_mix
    for k in (1, 2):
        own_in_place = lax.dynamic_update_slice(lax.empty((N_DEV,) + packs[k].shape, BF16), packs[k][None], (dev, 0, 0))
        flights.append(_gather_start(own_in_place, after, "weights_gather_start_%d" % k, BARRIER_IDS["gather_start"][k - 1]))
        after = flights[-1][3]
    gflat = gsmall.reshape(N_DEV, -1)
    n_cw = CONV_WIDTH * (D_CONV // N_DEV)
    n_fw = FFN_CONV_WIDTH * (2 * D_FF // N_DEV)
    conv_w = gflat[:, :n_cw].reshape(N_DEV, CONV_WIDTH, D_CONV // N_DEV).transpose(1, 0, 2).reshape(CONV_WIDTH, D_CONV)
    ffn_w = gflat[:, n_cw:n_cw + n_fw].reshape(N_DEV, FFN_CONV_WIDTH, 2 * D_FF // N_DEV).transpose(1, 0, 2).reshape(
        FFN_CONV_WIDTH, 2 * D_FF)

    x2d = x.reshape(tokens, D_MODEL)
    mem2d = mem.reshape(n_b * N_MEM, D_MODEL)
    tgt2d = loss_target.reshape(tokens, D_MODEL)
    g_final = norm_final_g.reshape(1, D_MODEL)

    def gather_finish(flight, after, tag):
        fwd_send, fwd_recv, buf = _gather_forward(*flight[:3], after, "weights_gather_forward_" + tag,
                                                  BARRIER_IDS["gather_forward"][int(tag) - 1])
        return _gather_finish(fwd_send, fwd_recv, buf, "weights_gather_finish_" + tag)

    x1, u_all, c_all, pooled_all, ymix, h1 = _fwd_mix(
        x2d, gw_mix, norm_mix_g, conv_w, conv_dw_b, conv_ln_g, conv_ln_b, pool_w[0], pool_scale, flights[1][3],
        seq, tm_mix)
    gw_attn = gather_finish(flights[0], x1, "1")
    mem_n, kv = _fwd_kv(mem2d, gw_attn, norm_mem_g)
    x2, h2, q, o = _fwd_attn(x1, kv, gw_attn, norm_xattn_g, seq, tm_mix)
    gw_ffn = gather_finish(flights[1], x2, "2")
    uu_all, cc_all, a_all, h3, dx3, dx3b, loss_part, dg_final = _fwd_ffn(
        x2, tgt2d, gw_ffn, norm_ffn_g, ffn_w, ffn_dw_b, g_final, seq, tm_ffn)

    table = _owner_table()

    def sibling_start(names, tag):
        parts = [part[n].reshape(N_DEV, W_OFF[n][1], D_MODEL) for n in names]
        return _exchange_start(parts, 4, _to_sibling, "rs_sibling_exchange_start_" + tag, BARRIER_IDS["sibling"][tag])

    def chips_start(flight, after, tag):
        parts, landed = _exchange_wait(*flight[:4], after, 4, _to_sibling, "rs_sibling_exchange_wait_" + tag)
        sums = _chip_partial_sums(table, parts, landed, "rs_chip_partial_sums_" + tag)
        return parts, landed, _exchange_start(sums, 3, _to_chip, "rs_chip_exchange_start_" + tag,
                                              BARRIER_IDS["chips"][tag])

    grads, delta, new_m, new_v = {}, {}, {}, {}

    def reduce_finish(names, parts, landed, flight, after, tag):
        _, from_chips = _exchange_wait(*flight[:4], after, 3, _to_chip, "rs_chip_exchange_wait_" + tag)
        as_rows = {n: n not in transposed or W_OFF[n][1] % LANES != 0 for n in names}
        states = [tuple(t[n][0].T if n in transposed else t[n][0] for t in (weights, moments_m, moments_v))
                  if as_rows[n] else None for n in names]
        results = _final_update(table, parts, landed, from_chips, states, "rs_final_update_" + tag)
        for n, res in zip(names, results):
            back = (lambda t: t.T[None]) if n in transposed else (lambda t: t[None])
            grads[n] = back(res[0])
            if as_rows[n]:
                delta[n], new_m[n], new_v[n] = [back(t) for t in res[1:]]
            else:
                delta[n], new_m[n], new_v[n] = [t[None] for t in _adamw(
                    weights[n][0], grads[n][0], moments_m[n][0], moments_v[n][0], "adamw_" + n)]
        alone = [n for n in names if not as_rows[n]]
        return delta[alone[-1] if alone else names[-1]]

    part = {}
    dx2, dx2b, duu, d_ffn_b, d_ffn_w, dg_ffn = _bwd_ffn(dx3, x2, uu_all, cc_all, gw_ffn, norm_ffn_g, ffn_w, seq, tm_ffn)
    part["w_up"] = _wgrad(duu, h3, "wgrad_w_up")
    part["w_down"] = _wgrad(a_all, dx3b, "wgrad_w_down")
    to_sibling_a = sibling_start(RS_GROUPS["a"], "a")
    dx1, dx1b, dq, dkv, dg_x = _bwd_attn(dx2, x1, q, kv, gw_attn, norm_xattn_g, to_sibling_a[4], seq, tm_mix)
    parts_a, landed_a, flight_a = chips_start(to_sibling_a, dx1, "a")
    dkv_b, dg_mem = _bwd_kv(dkv, mem2d, gw_attn)
    part["w_q"] = _wgrad(h2, dq, "wgrad_w_q", after=flight_a[4])
    part["w_kv"] = _wgrad(dkv_b, mem_n, "wgrad_w_kv")
    part["w_o"] = _wgrad(o, dx2b, "wgrad_w_o")
    to_sibling_b = sibling_start(RS_GROUPS["b"], "b")
    parts_b, landed_b, flight_b = chips_start(to_sibling_b, to_sibling_b[4], "b")
    dx, du, dg_mix, d_conv_w, d_conv_b, d_ln_g, d_ln_b, d_pool_w, d_pool_scale = _bwd_mix(
        dx1, x2d, u_all, c_all, pooled_all, gw_mix, norm_mix_g, conv_w, conv_ln_g, conv_ln_b, pool_w[0], pool_scale,
        flight_b[4], seq, tm_mix)
    grad_x = dx.reshape(x.shape)

    small_grads = dict(norm_mix_g=dg_mix, conv_dw_b=d_conv_b, conv_ln_g=d_ln_g, conv_ln_b=d_ln_b, pool_w=d_pool_w,
                       pool_scale=d_pool_scale, norm_xattn_g=dg_x, norm_mem_g=dg_mem, norm_ffn_g=dg_ffn,
                       ffn_dw_b=d_ffn_b, norm_final_g=dg_final)
    small_list = [small_grads[n] for n, _ in SMALL] + [d_conv_w, d_ffn_w, loss_part[:1]]
    small_mine = _pack_rows(small_list)
    small_flight = _broadcast_start(
        lax.dynamic_update_slice(lax.empty((N_DEV,) + small_mine.shape, F32), small_mine[None], (dev, 0, 0)),
        "small_grads_broadcast_start", BARRIER_IDS["broadcast"])

    part["w_in"] = _wgrad(du, h1, "wgrad_w_in", after=small_flight[3])
    part["w_out"] = _wgrad(ymix, dx1b, "wgrad_w_out")
    to_sibling_c = sibling_start(RS_GROUPS["c"], "c")
    parts_c, landed_c, flight_c = chips_start(to_sibling_c, to_sibling_c[4], "c")
    updated_a = reduce_finish(RS_GROUPS["a"], parts_a, landed_a, flight_a, flight_c[4], "a")
    updated_b = reduce_finish(RS_GROUPS["b"], parts_b, landed_b, flight_b, updated_a, "b")
    small_all = _broadcast_wait(*small_flight[:3], updated_b, "small_grads_broadcast_wait")
    small_sum = _sum_blocks(small_all).reshape(-1)

    pos = 0
    for n, shape in SMALL:
        size = 1
        for s in shape:
            size *= s
        grads[n] = small_sum[pos:pos + size].reshape(shape)
        pos += size
    full_conv_w = small_sum[pos:pos + CONV_WIDTH * D_CONV].reshape(CONV_WIDTH, D_CONV)
    pos += CONV_WIDTH * D_CONV
    full_ffn_w = small_sum[pos:pos + FFN_CONV_WIDTH * 2 * D_FF].reshape(FFN_CONV_WIDTH, 2 * D_FF)
    loss = small_sum[pos + FFN_CONV_WIDTH * 2 * D_FF]
    grads["conv_dw_w"] = lax.dynamic_slice_in_dim(full_conv_w, dev * (D_CONV // N_DEV), D_CONV // N_DEV, axis=1)[None]
    grads["ffn_dw_w"] = lax.dynamic_slice_in_dim(full_ffn_w, dev * (2 * D_FF // N_DEV), 2 * D_FF // N_DEV, axis=1)[None]

    small_names = [n for n in order if n not in W_OFF]
    swap = lambda t: jnp.transpose(t, (1, 0, 2))
    two_d = lambda t: t.reshape(1, -1) if t.ndim == 1 else (swap(t) if t.ndim == 3 else t)
    outs = _adamw_small(*[[two_d(t[n]) for n in small_names] for t in (weights, grads, moments_m, moments_v)])
    for res, out in zip((delta, new_m, new_v), outs):
        for n, o in zip(small_names, out):
            res[n] = swap(o) if o.ndim == 3 else o.reshape(weights[n].shape)

    reduce_finish(RS_GROUPS["c"], parts_c, landed_c, flight_c, delta[small_names[-1]], "c")

    return (loss, grad_x, *[grads[n] for n in order], *[delta[n] for n in order],
            *[new_m[n] for n in order], *[new_v[n] for n in order])
```

```python
import jax
import jax.numpy as jnp
from jax import lax
from jax.experimental import pallas as pl
from jax.experimental.pallas import tpu as pltpu

F32 = jnp.float32
BF16 = jnp.bfloat16
MESH = pl.DeviceIdType.MESH

N_DEV = 8
D_MODEL = 1024
D_CONV = 512
D_POOL = 512
CONV_WIDTH = 31
POOL_WINDOWS = (2, 4, 8, 16)
POOL_GROUP_DIM = 128
D_IN = 1536
N_MEM = 256
HEADS = 4
HEAD_DIM = 256
D_FF = 2816
FFN_CONV_WIDTH = 3
EPS = 1e-6
ADAM_LR = 0.001
ADAM_B1 = 0.9
ADAM_B2 = 0.999
ADAM_EPS = 1e-08
ADAM_WD = 0.01
ADAM_STEP = 10

VMEM_LIMIT_V7X = 56 * 1024 * 1024
CONV_HALO = 32
POOL_HALO = 16
FFN_HALO = 8
FFN_CHUNK = 2816

W_ROWS = (("w_in", 192), ("w_out", 128), ("w_q", 128), ("w_kv", 256), ("w_o", 128), ("w_up", 704), ("w_down", 352))
AG_GROUPS = (("w_in", "w_out"), ("w_q", "w_kv", "w_o"), ("w_up", "w_down"))
W_OFF = {}
for _names in AG_GROUPS:
    _o = 0
    for _n in _names:
        W_OFF[_n] = (_o, dict(W_ROWS)[_n])
        _o += dict(W_ROWS)[_n]
RS_GROUPS = {"a": ("w_up", "w_down"), "b": ("w_q", "w_kv", "w_o", "w_out"), "c": ("w_in",)}
BARRIER_IDS = {"gather_start": (0, 1), "gather_forward": (2, 3), "sibling": {"a": 4, "b": 5, "c": 6},
               "chips": {"a": 7, "b": 8, "c": 9}, "broadcast": 10}


def _dot(a, b):
    return jnp.dot(a, b, preferred_element_type=F32)


def _dot_nt(a, b):
    return lax.dot_general(a, b, (((1,), (1,)), ((), ())), preferred_element_type=F32)


def _dot_tn(a, b):
    return lax.dot_general(a, b, (((0,), (0,)), ((), ())), preferred_element_type=F32)


def _sigmoid(v):
    return 1.0 / (1.0 + jnp.exp(-v))


def _rms_fwd(v):
    r = lax.rsqrt(jnp.mean(v * v, axis=-1, keepdims=True) + EPS)
    return v * r, r


def _rms_bwd(dh, vh, r, g):
    gd = dh * g
    return r * (gd - vh * jnp.mean(gd * vh, axis=-1, keepdims=True))


def _sublane_shifts(v):
    rows = v.shape[0]
    return [v] + [pltpu.roll(v, rows - b, 0) for b in range(1, 8)]


def _colsum(v):
    return jnp.sum(v, axis=0, keepdims=True)


def _colsum_mxu(v):
    return _dot(jnp.ones((8, v.shape[0]), BF16), v.astype(BF16))[0:1, :]


def _full(shape):
    return pl.BlockSpec(shape, lambda *_: (0,) * len(shape))


def _params(sem=("arbitrary",), vmem=VMEM_LIMIT_V7X):
    return pltpu.CompilerParams(dimension_semantics=sem, vmem_limit_bytes=vmem)


def _load_weight(g_hbm, name, dst, sem):
    off, rows = W_OFF[name]
    return [pltpu.make_async_copy(g_hbm.at[d, pl.ds(off, rows), :], dst.at[pl.ds(d * rows, rows), :], sem)
            for d in range(N_DEV)]


def _start_weights(g_hbm, names, dsts, sems):
    @pl.when(pl.program_id(0) == 0)
    def _():
        copies = [_load_weight(g_hbm, name, dst, sems.at[k]) for k, (name, dst) in enumerate(zip(names, dsts))]
        for cp in sum(copies, []):
            cp.start()
        for cp in sum(copies, []):
            cp.wait()


def _position():
    x, y, c = lax.axis_index("x"), lax.axis_index("y"), lax.axis_index("c")
    chips = [(1 - x, y), (x, 1 - y), (1 - x, 1 - y)]
    return x, y, c, chips


def _dev(px, py, pc):
    return 4 * px + 2 * py + pc


def _all_gather(arrs, name):
    n = len(arrs)

    def body(*refs):
        ins, outs = refs[:n], refs[n:2 * n]
        send_sems, recv_sems, local_sems = refs[2 * n:2 * n + 3]
        bounce = refs[2 * n + 3:]
        x, y, c, chips = _position()
        me, sibling = (x, y, c), (x, y, 1 - c)

        def copy(a, k, block, to, src=None):
            rows = outs[a].at[_dev(*block)]
            return pltpu.make_async_remote_copy(
                src_ref=rows if src is None else src, dst_ref=rows,
                send_sem=send_sems.at[a, k], recv_sem=recv_sems.at[a, k], device_id=to, device_id_type=MESH)

        sends = []
        for a in range(n):
            first = [copy(a, 0, me, sibling, src=ins[a])]
            first += [copy(a, 1 + j, me, (*chip, c), src=ins[a]) for j, chip in enumerate(chips)]
            for cp in first:
                cp.start()
            sends += first
        started = []
        for a in range(n):
            load = pltpu.make_async_copy(ins[a], bounce[a], local_sems.at[a, 0])
            load.start()
            load.wait()
            mine = pltpu.make_async_copy(bounce[a], outs[a].at[_dev(*me)], local_sems.at[a, 1])
            mine.start()
            started.append(mine)
        for j, chip in enumerate(chips):
            for a in range(n):
                copy(a, 1 + j, (*chip, c), me).wait_recv()
                passed = copy(a, 4 + j, (*chip, c), sibling)
                passed.start()
                sends.append(passed)
        for a in range(n):
            copy(a, 0, sibling, me).wait_recv()
            for j, chip in enumerate(chips):
                copy(a, 4 + j, (*chip, 1 - c), me).wait_recv()
        for cp in sends:
            cp.wait_send()
        for mine in started:
            mine.wait()

    any_spec = pl.BlockSpec(memory_space=pl.ANY)
    return pl.pallas_call(
        body, name=name,
        out_shape=[jax.ShapeDtypeStruct((N_DEV,) + a.shape, a.dtype) for a in arrs],
        in_specs=[any_spec] * n, out_specs=[any_spec] * n,
        scratch_shapes=[pltpu.SemaphoreType.DMA((n, 7)), pltpu.SemaphoreType.DMA((n, 7)), pltpu.SemaphoreType.DMA((n, 2))]
        + [pltpu.VMEM(a.shape, a.dtype) for a in arrs],
    )(*arrs)


_HBM = pl.BlockSpec(memory_space=pltpu.HBM)
_SEM = pl.BlockSpec(memory_space=pltpu.SEMAPHORE)
_SIDE_EFFECT = pltpu.SideEffectType.DATAFLOW_SIDE_EFFECTING


def _handshake(peers):
    barrier = pltpu.get_barrier_semaphore()
    for peer in peers:
        pl.semaphore_signal(barrier, inc=1, device_id=peer, device_id_type=MESH)
    pl.semaphore_wait(barrier, len(peers))


def _gather_start(buf, after, name, collective_id):
    def body(buf_ref, after_ref, send_sems, recv_sems, buf_thru, token):
        del after_ref, buf_thru
        x, y, c, chips = _position()
        rows = buf_ref.at[_dev(x, y, c)]
        targets = [(x, y, 1 - c)] + [(*chip, c) for chip in chips]
        _handshake(targets)
        for k, to in enumerate(targets):
            pltpu.make_async_remote_copy(src_ref=rows, dst_ref=rows, send_sem=send_sems.at[k], recv_sem=recv_sems.at[k],
                                         device_id=to, device_id_type=MESH).start()
        token[...] = jnp.zeros_like(token)

    return pl.pallas_call(
        body, name=name,
        out_shape=(pltpu.SemaphoreType.DMA((4,)), pltpu.SemaphoreType.DMA((4,)), pltpu.HBM(buf.shape, buf.dtype),
                   jax.ShapeDtypeStruct((8, 128), F32)),
        in_specs=(_HBM, pl.BlockSpec(memory_space=pl.ANY)),
        out_specs=(_SEM, _SEM, _HBM, pl.BlockSpec(memory_space=pltpu.VMEM)),
        input_output_aliases={0: 2},
        compiler_params=pltpu.CompilerParams(has_side_effects=_SIDE_EFFECT, collective_id=collective_id),
    )(pltpu.with_memory_space_constraint(buf, pltpu.HBM), after)


def _gather_forward(send_sems, recv_sems, buf, after, name, collective_id):
    def body(buf_ref, send_sems, recv_sems, after_ref, fwd_send, fwd_recv, buf_thru):
        del after_ref, buf_thru
        x, y, c, chips = _position()
        sibling = (x, y, 1 - c)

        def copy(block, k, sends, recvs):
            rows = buf_ref.at[_dev(*block)]
            return pltpu.make_async_remote_copy(src_ref=rows, dst_ref=rows, send_sem=sends.at[k], recv_sem=recvs.at[k],
                                                device_id=sibling, device_id_type=MESH)

        _handshake([sibling])
        for k in range(4):
            copy((x, y, c), k, send_sems, recv_sems).wait_send()
        copy(sibling, 0, send_sems, recv_sems).wait_recv()
        for j, chip in enumerate(chips):
            copy((*chip, c), 1 + j, send_sems, recv_sems).wait_recv()
            copy((*chip, c), j, fwd_send, fwd_recv).start()

    return pl.pallas_call(
        body, name=name,
        out_shape=(pltpu.SemaphoreType.DMA((3,)), pltpu.SemaphoreType.DMA((3,)), pltpu.HBM(buf.shape, buf.dtype)),
        in_specs=(_HBM, _SEM, _SEM, pl.BlockSpec(memory_space=pl.ANY)), out_specs=(_SEM, _SEM, _HBM),
        input_output_aliases={0: 2},
        compiler_params=pltpu.CompilerParams(has_side_effects=_SIDE_EFFECT, collective_id=collective_id),
    )(buf, send_sems, recv_sems, after)


def _gather_finish(fwd_send, fwd_recv, buf, name):
    def body(buf_ref, fwd_send, fwd_recv, buf_thru):
        del buf_thru
        x, y, c, chips = _position()
        for j, chip in enumerate(chips):
            cp = pltpu.make_async_remote_copy(
                src_ref=buf_ref.at[_dev(*chip, c)], dst_ref=buf_ref.at[_dev(*chip, 1 - c)], send_sem=fwd_send.at[j],
                recv_sem=fwd_recv.at[j], device_id=(x, y, 1 - c), device_id_type=MESH)
            cp.wait_send()
            cp.wait_recv()

    return pl.pallas_call(
        body, name=name,
        out_shape=pltpu.HBM(buf.shape, buf.dtype),
        in_specs=(_HBM, _SEM, _SEM), out_specs=_HBM,
        input_output_aliases={0: 0},
        compiler_params=pltpu.CompilerParams(has_side_effects=_SIDE_EFFECT),
    )(buf, fwd_send, fwd_recv)


def _everyone_else(x, y, c, chips):
    return [(x, y, 1 - c)] + [(*chip, core) for chip in chips for core in (c, 1 - c)]


def _broadcast_start(buf, name, collective_id):
    def body(buf_ref, send_sems, recv_sems, buf_thru, token):
        del buf_thru
        x, y, c, chips = _position()
        rows = buf_ref.at[_dev(x, y, c)]
        _handshake(_everyone_else(x, y, c, chips))
        for k, to in enumerate(_everyone_else(x, y, c, chips)):
            pltpu.make_async_remote_copy(src_ref=rows, dst_ref=rows, send_sem=send_sems.at[k], recv_sem=recv_sems.at[k],
                                         device_id=to, device_id_type=MESH).start()
        token[...] = jnp.zeros_like(token)

    return pl.pallas_call(
        body, name=name,
        out_shape=(pltpu.SemaphoreType.DMA((7,)), pltpu.SemaphoreType.DMA((7,)), pltpu.HBM(buf.shape, buf.dtype),
                   jax.ShapeDtypeStruct((8, 128), F32)),
        in_specs=(_HBM,), out_specs=(_SEM, _SEM, _HBM, pl.BlockSpec(memory_space=pltpu.VMEM)),
        input_output_aliases={0: 2},
        compiler_params=pltpu.CompilerParams(has_side_effects=_SIDE_EFFECT, collective_id=collective_id),
    )(pltpu.with_memory_space_constraint(buf, pltpu.HBM))


def _broadcast_wait(send_sems, recv_sems, buf, after, name):
    def body(buf_ref, send_sems, recv_sems, after_ref, buf_thru):
        del after_ref, buf_thru
        x, y, c, chips = _position()
        for k, peer in enumerate(_everyone_else(x, y, c, chips)):
            cp = pltpu.make_async_remote_copy(
                src_ref=buf_ref.at[_dev(x, y, c)], dst_ref=buf_ref.at[_dev(*peer)], send_sem=send_sems.at[k],
                recv_sem=recv_sems.at[k], device_id=peer, device_id_type=MESH)
            cp.wait_send()
            cp.wait_recv()

    return pl.pallas_call(
        body, name=name,
        out_shape=pltpu.HBM(buf.shape, buf.dtype),
        in_specs=(_HBM, _SEM, _SEM, pl.BlockSpec(memory_space=pl.ANY)), out_specs=_HBM,
        input_output_aliases={0: 0},
        compiler_params=pltpu.CompilerParams(has_side_effects=_SIDE_EFFECT),
    )(buf, send_sems, recv_sems, after)


def _to_sibling(j, x, y, c, chips):
    return _dev(*([(x, y)] + chips)[j], 1 - c), (x, y, 1 - c)


def _to_chip(j, x, y, c, chips):
    return j, (*chips[j], c)


def _exchange_start(srcs, n_slots, route, name, collective_id):
    n = len(srcs)

    def body(*refs):
        s_refs, land_refs = refs[:n], refs[n:2 * n]
        send_sems, recv_sems = refs[2 * n:2 * n + 2]
        token = refs[-1]
        x, y, c, chips = _position()
        _handshake([(x, y, 1 - c)] if route is _to_sibling else [route(j, x, y, c, chips)[1] for j in range(n_slots)])
        for k in range(n):
            for j in range(n_slots):
                block, to = route(j, x, y, c, chips)
                pltpu.make_async_remote_copy(
                    src_ref=s_refs[k].at[block], dst_ref=land_refs[k].at[j], send_sem=send_sems.at[n_slots * k + j],
                    recv_sem=recv_sems.at[n_slots * k + j], device_id=to, device_id_type=MESH).start()
        token[...] = jnp.zeros_like(token)

    lands = [jax.ShapeDtypeStruct((n_slots,) + s.shape[1:], s.dtype) for s in srcs]
    outs = pl.pallas_call(
        body, name=name,
        out_shape=(pltpu.SemaphoreType.DMA((n_slots * n,)), pltpu.SemaphoreType.DMA((n_slots * n,)),
                   *[pltpu.HBM(s.shape, s.dtype) for s in srcs], *[pltpu.HBM(l.shape, l.dtype) for l in lands],
                   jax.ShapeDtypeStruct((8, 128), F32)),
        in_specs=[_HBM] * (2 * n), out_specs=(_SEM, _SEM, *[_HBM] * (2 * n), pl.BlockSpec(memory_space=pltpu.VMEM)),
        input_output_aliases={k: 2 + k for k in range(2 * n)},
        compiler_params=pltpu.CompilerParams(has_side_effects=_SIDE_EFFECT, collective_id=collective_id),
    )(*[pltpu.with_memory_space_constraint(s, pltpu.HBM) for s in srcs],
      *[pltpu.with_memory_space_constraint(lax.empty(l.shape, l.dtype), pltpu.HBM) for l in lands])
    return outs[0], outs[1], outs[2:2 + n], outs[2 + n:2 + 2 * n], outs[-1]


def _exchange_wait(send_sems, recv_sems, s_thru, land_thru, after, n_slots, route, name):
    n = len(s_thru)

    def body(*refs):
        s_refs, land_refs = refs[:n], refs[n:2 * n]
        send_sems, recv_sems = refs[2 * n:2 * n + 2]
        x, y, c, chips = _position()
        for k in range(n):
            for j in range(n_slots):
                block, to = route(j, x, y, c, chips)
                cp = pltpu.make_async_remote_copy(
                    src_ref=s_refs[k].at[block], dst_ref=land_refs[k].at[j], send_sem=send_sems.at[n_slots * k + j],
                    recv_sem=recv_sems.at[n_slots * k + j], device_id=to, device_id_type=MESH)
                cp.wait_send()
                cp.wait_recv()

    outs = pl.pallas_call(
        body, name=name,
        out_shape=(*[pltpu.HBM(s.shape, s.dtype) for s in s_thru], *[pltpu.HBM(l.shape, l.dtype) for l in land_thru]),
        in_specs=[_HBM] * (2 * n) + [_SEM, _SEM, pl.BlockSpec(memory_space=pl.ANY)], out_specs=[_HBM] * (2 * n),
        input_output_aliases={k: k for k in range(2 * n)},
        compiler_params=pltpu.CompilerParams(has_side_effects=_SIDE_EFFECT),
    )(*s_thru, *land_thru, send_sems, recv_sems, after)
    return outs[:n], outs[n:]


def _owner_table():
    x, y, c = lax.axis_index("x"), lax.axis_index("y"), lax.axis_index("c")
    chips = [(x, y), (1 - x, y), (x, 1 - y), (1 - x, 1 - y)]
    return jnp.stack([_dev(px, py, c) for px, py in chips]).astype(jnp.int32)


def _chip_partial_sums(table, parts, from_sibling, name):
    n = len(parts)

    def body(tab_ref, *refs):
        del tab_ref
        for g_ref, l_ref, out_ref in zip(refs[:n], refs[n:2 * n], refs[2 * n:]):
            out_ref[...] = (g_ref[...].astype(F32) + l_ref[...].astype(F32)).astype(out_ref.dtype)

    block = lambda p: (None,) + p.shape[1:]
    grid_spec = pltpu.PrefetchScalarGridSpec(
        num_scalar_prefetch=1, grid=(3,),
        in_specs=[pl.BlockSpec(block(p), lambda j, tab: (tab[j + 1], 0, 0)) for p in parts]
        + [pl.BlockSpec(block(p), lambda j, tab: (j + 1, 0, 0)) for p in parts],
        out_specs=[pl.BlockSpec(block(p), lambda j, tab: (j, 0, 0)) for p in parts])
    return pl.pallas_call(
        body, name=name, grid_spec=grid_spec,
        out_shape=[jax.ShapeDtypeStruct((3,) + p.shape[1:], BF16) for p in parts],
        compiler_params=_params(("arbitrary",)),
    )(table, *parts, *from_sibling)


def _final_update(table, parts, from_sibling, from_chips, states, name):
    n = len(parts)
    updated = [k for k in range(n) if states[k] is not None]

    def body(tab_ref, *refs):
        del tab_ref
        ins, outs = refs[:3 * n + 3 * len(updated)], list(refs[3 * n + 3 * len(updated):])
        wmv = list(ins[3 * n:])
        for k in range(n):
            acc = ins[k][...].astype(F32) + ins[n + k][...].astype(F32)
            for j in range(3):
                acc = acc + ins[2 * n + k][j].astype(F32)
            outs.pop(0)[...] = acc
            if k in updated:
                w_ref, m_ref, v_ref = wmv[:3]
                del wmv[:3]
                for out_ref, val in zip(outs[:3], _adamw_update(w_ref[...], acc, m_ref[...], v_ref[...])):
                    out_ref[...] = val
                del outs[:3]

    half = lambda p: (p.shape[1] // 2, p.shape[2])
    rows = lambda p: pl.BlockSpec(half(p), lambda t, tab: (t, 0))
    grid_spec = pltpu.PrefetchScalarGridSpec(
        num_scalar_prefetch=1, grid=(2,),
        in_specs=[pl.BlockSpec((None,) + half(p), lambda t, tab: (tab[0], t, 0)) for p in parts]
        + [pl.BlockSpec((None,) + half(p), lambda t, tab: (0, t, 0)) for p in parts]
        + [pl.BlockSpec((3,) + half(p), lambda t, tab: (0, t, 0)) for p in parts]
        + [rows(parts[k]) for k in updated for _ in range(3)],
        out_specs=[rows(parts[k]) for k in range(n) for _ in range(4 if k in updated else 1)])
    outs = pl.pallas_call(
        body, name=name, grid_spec=grid_spec,
        out_shape=[jax.ShapeDtypeStruct(parts[k].shape[1:], F32) for k in range(n) for _ in range(4 if k in updated else 1)],
        compiler_params=_params(("arbitrary",)),
    )(table, *parts, *from_sibling, *from_chips, *[t for k in updated for t in states[k]])
    result = []
    for k in range(n):
        count = 4 if k in updated else 1
        result.append(outs[:count])
        outs = outs[count:]
    return result


def _sum_blocks(g8):
    _, rows, cols = g8.shape

    def body(g_ref, out_ref):
        acc = g_ref[0]
        for d in range(1, N_DEV):
            acc = acc + g_ref[d]
        out_ref[...] = acc

    return pl.pallas_call(
        body, name="small_grad_sum", grid=(1,),
        in_specs=[_full((N_DEV, rows, cols))], out_specs=_full((rows, cols)),
        out_shape=jax.ShapeDtypeStruct((rows, cols), F32),
        compiler_params=_params(("arbitrary",)),
    )(g8)


def _fwd_mix(x2d, gw, g_mix, conv_w, conv_b, ln_g, ln_b, pool_w, pool_scale, after, seq, tm):
    tokens = x2d.shape[0]
    n_tiles = tokens // tm
    tps = seq // tm

    def body(x_ref, gmix_ref, gw_hbm, cw_ref, cb_ref, lng_ref, lnb_ref, pw_ref, ps_ref, after_ref,
             x1_ref, u_ref, c_ref, pooled_ref, ymix_ref, h1_ref,
             win_v, wout_v, hc_carry, up_carry, sem):
        del after_ref
        i = pl.program_id(0)

        _start_weights(gw_hbm, ("w_in", "w_out"), (win_v, wout_v), sem)

        @pl.when(i % tps == 0)
        def _():
            hc_carry[...] = jnp.zeros_like(hc_carry)
            up_carry[...] = jnp.zeros_like(up_carry)

        x = x_ref[...]
        xh, _ = _rms_fwd(x)
        h1 = (xh * gmix_ref[...]).astype(BF16)
        h1_ref[...] = h1
        u = _dot_nt(h1, win_v[...])
        u_ref[...] = u
        val, gate, up = u[:, :D_CONV], u[:, D_CONV:2 * D_CONV], u[:, 2 * D_CONV:]

        extp = jnp.concatenate([up_carry[...], up], axis=0)
        up_carry[...] = up[tm - POOL_HALO:, :]
        pos = lax.broadcasted_iota(jnp.int32, (tm, 1), 0) + (i % tps) * tm
        run = extp
        mixed = []
        for g, w in enumerate(POOL_WINDOWS):
            lo = g * POOL_GROUP_DIM
            run = run[:, POOL_GROUP_DIM if g else 0:]
            run = run + pltpu.roll(run, w // 2, 0)
            cnt = jnp.minimum(pos + 1, w).astype(F32)
            pooled = run[POOL_HALO:, :POOL_GROUP_DIM] / cnt - up[:, lo:lo + POOL_GROUP_DIM]
            pooled = pooled.astype(BF16)
            pooled_ref[:, lo:lo + POOL_GROUP_DIM] = pooled
            mixed.append(_dot(pooled, pw_ref[g].astype(BF16)))
        y_pool = jnp.concatenate(mixed, axis=-1) * ps_ref[...]
        y_pool = y_pool.astype(BF16)
        ymix_ref[:, D_CONV:] = y_pool
        out = _dot(y_pool, wout_v[D_CONV:, :])

        hc = val * _sigmoid(gate)
        ext = jnp.concatenate([hc_carry[...], hc], axis=0)
        hc_carry[...] = hc[tm - CONV_HALO:, :]
        conv = jnp.broadcast_to(cb_ref[...], (tm, D_CONV))
        ahead_by = _sublane_shifts(ext)
        for k in range(CONV_WIDTH):
            whole, part = divmod(CONV_HALO - (CONV_WIDTH - 1) + k, 8)
            conv = conv + cw_ref[k:k + 1, :] * ahead_by[part][8 * whole:8 * whole + tm, :]
        c_ref[...] = conv
        mu = jnp.mean(conv, axis=-1, keepdims=True)
        cen = conv - mu
        ln = cen * lax.rsqrt(jnp.mean(cen * cen, axis=-1, keepdims=True) + EPS) * lng_ref[...] + lnb_ref[...]
        y_conv = ln * _sigmoid(ln)
        y_conv = y_conv.astype(BF16)
        ymix_ref[:, :D_CONV] = y_conv
        x1_ref[...] = x + (out + _dot(y_conv, wout_v[:D_CONV, :]))

    row = lambda w: pl.BlockSpec((tm, w), lambda i: (i, 0))
    return pl.pallas_call(
        body, name="fwd_mix", grid=(n_tiles,),
        in_specs=[row(D_MODEL), _full((1, D_MODEL)), pl.BlockSpec(memory_space=pl.ANY),
                  _full((CONV_WIDTH, D_CONV)), _full((1, D_CONV)), _full((1, D_CONV)), _full((1, D_CONV)),
                  _full((4, POOL_GROUP_DIM, POOL_GROUP_DIM)), _full((1, D_POOL)), _full(after.shape)],
        out_specs=[row(D_MODEL), row(D_IN), row(D_CONV), row(D_POOL), row(D_MODEL), row(D_MODEL)],
        out_shape=[jax.ShapeDtypeStruct((tokens, D_MODEL), F32), jax.ShapeDtypeStruct((tokens, D_IN), F32),
                   jax.ShapeDtypeStruct((tokens, D_CONV), F32), jax.ShapeDtypeStruct((tokens, D_POOL), BF16),
                   jax.ShapeDtypeStruct((tokens, D_MODEL), BF16), jax.ShapeDtypeStruct((tokens, D_MODEL), BF16)],
        scratch_shapes=[pltpu.VMEM((D_IN, D_MODEL), BF16), pltpu.VMEM((D_MODEL, D_MODEL), BF16),
                        pltpu.VMEM((CONV_HALO, D_CONV), F32), pltpu.VMEM((POOL_HALO, D_POOL), F32),
                        pltpu.SemaphoreType.DMA((2,))],
        compiler_params=_params(),
    )(x2d, g_mix, gw, conv_w, conv_b, ln_g, ln_b, pool_w, pool_scale, after)


def _fwd_kv(mem2d, gw, g_mem):
    rows = mem2d.shape[0]
    n_b = rows // N_MEM

    def body(mem_ref, g_ref, gw_hbm, mn_ref, kv_ref, wkv_v, sem):
        @pl.when(pl.program_id(0) == 0)
        def _():
            copies = _load_weight(gw_hbm, "w_kv", wkv_v, sem)
            for cp in copies:
                cp.start()
            for cp in copies:
                cp.wait()

        mh, _ = _rms_fwd(mem_ref[...])
        mn = (mh * g_ref[...]).astype(BF16)
        mn_ref[...] = mn
        kv_ref[...] = _dot_nt(mn, wkv_v[...]).astype(BF16)

    return pl.pallas_call(
        body, name="fwd_kv", grid=(n_b,),
        in_specs=[pl.BlockSpec((N_MEM, D_MODEL), lambda b: (b, 0)), _full((1, D_MODEL)), pl.BlockSpec(memory_space=pl.ANY)],
        out_specs=[pl.BlockSpec((N_MEM, D_MODEL), lambda b: (b, 0)), pl.BlockSpec((N_MEM, 2 * D_MODEL), lambda b: (b, 0))],
        out_shape=[jax.ShapeDtypeStruct((rows, D_MODEL), BF16), jax.ShapeDtypeStruct((rows, 2 * D_MODEL), BF16)],
        scratch_shapes=[pltpu.VMEM((2 * D_MODEL, D_MODEL), BF16), pltpu.SemaphoreType.DMA],
        compiler_params=_params(),
    )(mem2d, g_mem, gw)


def _softmax_rows(s):
    e = jnp.exp(s - jnp.max(s, axis=-1, keepdims=True))
    return e / jnp.sum(e, axis=-1, keepdims=True)


def _fwd_attn(x1, kv, gw, g_x, seq, tm):
    tokens = x1.shape[0]
    n_tiles = tokens // tm
    tps = seq // tm

    def body(x1_ref, kv_ref, g_ref, gw_hbm, x2_ref, h2_ref, q_ref, o_ref, wq_v, wo_v, sem):
        _start_weights(gw_hbm, ("w_q", "w_o"), (wq_v, wo_v), sem)
        x1v = x1_ref[...]
        xh, _ = _rms_fwd(x1v)
        h2 = (xh * g_ref[...]).astype(BF16)
        h2_ref[...] = h2
        q = (_dot(h2, wq_v[...]) * (HEAD_DIM ** -0.5)).astype(BF16)
        q_ref[...] = q
        heads = [slice(h * HEAD_DIM, (h + 1) * HEAD_DIM) for h in range(HEADS)]
        scores = [_dot_nt(q[:, hd], kv_ref[:, hd]) for hd in heads]
        probs = [_softmax_rows(s).astype(BF16) for s in scores]
        outs = [_dot(p, kv_ref[:, pl.ds(D_MODEL + h * HEAD_DIM, HEAD_DIM)]) for h, p in enumerate(probs)]
        o = jnp.concatenate(outs, axis=-1).astype(BF16)
        o_ref[...] = o
        x2_ref[...] = x1v + _dot(o, wo_v[...])

    row = lambda w: pl.BlockSpec((tm, w), lambda i: (i, 0))
    return pl.pallas_call(
        body, name="fwd_attn", grid=(n_tiles,),
        in_specs=[row(D_MODEL), pl.BlockSpec((N_MEM, 2 * D_MODEL), lambda i: (i // tps, 0)), _full((1, D_MODEL)),
                  pl.BlockSpec(memory_space=pl.ANY)],
        out_specs=[row(D_MODEL)] * 4,
        out_shape=[jax.ShapeDtypeStruct((tokens, D_MODEL), F32)] + [jax.ShapeDtypeStruct((tokens, D_MODEL), BF16)] * 3,
        scratch_shapes=[pltpu.VMEM((D_MODEL, D_MODEL), BF16), pltpu.VMEM((D_MODEL, D_MODEL), BF16), pltpu.SemaphoreType.DMA((2,))],
        compiler_params=_params(),
    )(x1, kv, g_x, gw)


def _ffn_conv(uu, halo, w_ref, b_ref, cols):
    ext = jnp.concatenate([halo, uu], axis=0)
    p1 = pltpu.roll(ext, 1, 0)[FFN_HALO:, :]
    p2 = pltpu.roll(ext, 2, 0)[FFN_HALO:, :]
    return b_ref[:, cols] + w_ref[2:3, cols] * uu + w_ref[1:2, cols] * p1 + w_ref[0:1, cols] * p2


def _fwd_ffn(x2, target, gw, g_ffn, ffn_w, ffn_b, g_final, seq, tm):
    tokens = x2.shape[0]
    n_tiles = tokens // tm
    tps = seq // tm
    n_chunks = D_FF // FFN_CHUNK

    def body(x2_ref, tgt_ref, gffn_ref, gw_hbm, fw_ref, fb_ref, gfin_ref,
             uu_ref, cc_ref, a_ref, h3_ref, dx3_ref, dx3b_ref, loss_ref, dgfin_ref,
             wup_v, wdown_v, carry, sem):
        i = pl.program_id(0)

        _start_weights(gw_hbm, ("w_up", "w_down"), (wup_v, wdown_v), sem)

        @pl.when(i == 0)
        def _():
            loss_ref[...] = jnp.zeros_like(loss_ref)
            dgfin_ref[...] = jnp.zeros_like(dgfin_ref)

        @pl.when(i % tps == 0)
        def _():
            carry[...] = jnp.zeros_like(carry)

        x2v = x2_ref[...]
        xh, _ = _rms_fwd(x2v)
        h3 = (xh * gffn_ref[...]).astype(BF16)
        h3_ref[...] = h3
        acc = jnp.zeros((tm, D_MODEL), F32)
        for jc in range(n_chunks):
            halves = []
            for half in range(2):
                cols = pl.ds(half * D_FF + jc * FFN_CHUNK, FFN_CHUNK)
                uu = _dot_nt(h3, wup_v[cols, :])
                uu_ref[:, cols] = uu.astype(BF16)
                cc = _ffn_conv(uu, carry[:, cols], fw_ref, fb_ref, cols)
                cc_ref[:, cols] = cc.astype(BF16)
                halves.append(cc)
                carry[:, cols] = uu[tm - FFN_HALO:, :]
            gate, val = halves
            a = (gate * _sigmoid(gate) * val).astype(BF16)
            a_ref[:, pl.ds(jc * FFN_CHUNK, FFN_CHUNK)] = a
            acc = acc + _dot(a, wdown_v[pl.ds(jc * FFN_CHUNK, FFN_CHUNK), :])
        x3 = x2v + acc

        xh3, r3 = _rms_fwd(x3)
        gfin = gfin_ref[...]
        err = xh3 * gfin - tgt_ref[...]
        loss_ref[...] += jnp.full(loss_ref.shape, jnp.sum(err * err) * (0.5 / D_MODEL), F32)
        dy = err * (1.0 / D_MODEL)
        dgfin_ref[...] += _colsum(dy * xh3)
        dx3 = _rms_bwd(dy, xh3, r3, gfin)
        dx3_ref[...] = dx3
        dx3b_ref[...] = dx3.astype(BF16)

    row = lambda w: pl.BlockSpec((tm, w), lambda i: (i, 0))
    return pl.pallas_call(
        body, name="fwd_ffn", grid=(n_tiles,),
        in_specs=[row(D_MODEL), row(D_MODEL), _full((1, D_MODEL)), pl.BlockSpec(memory_space=pl.ANY),
                  _full((FFN_CONV_WIDTH, 2 * D_FF)), _full((1, 2 * D_FF)), _full((1, D_MODEL))],
        out_specs=[row(2 * D_FF), row(2 * D_FF), row(D_FF), row(D_MODEL), row(D_MODEL), row(D_MODEL), _full((8, 128)),
                   _full((1, D_MODEL))],
        out_shape=[jax.ShapeDtypeStruct((tokens, 2 * D_FF), BF16), jax.ShapeDtypeStruct((tokens, 2 * D_FF), BF16),
                   jax.ShapeDtypeStruct((tokens, D_FF), BF16),
                   jax.ShapeDtypeStruct((tokens, D_MODEL), BF16), jax.ShapeDtypeStruct((tokens, D_MODEL), F32),
                   jax.ShapeDtypeStruct((tokens, D_MODEL), BF16),
                   jax.ShapeDtypeStruct((8, 128), F32), jax.ShapeDtypeStruct((1, D_MODEL), F32)],
        scratch_shapes=[pltpu.VMEM((2 * D_FF, D_MODEL), BF16), pltpu.VMEM((D_FF, D_MODEL), BF16),
                        pltpu.VMEM((FFN_HALO, 2 * D_FF), F32), pltpu.SemaphoreType.DMA((2,))],
        compiler_params=_params(),
    )(x2, target, g_ffn, gw, ffn_w, ffn_b, g_final)


def _bwd_ffn(dx3, x2, uu_all, cc_all, gw, g_ffn, ffn_w, seq, tm):
    tokens = x2.shape[0]
    n_tiles = tokens // tm
    tps = seq // tm
    n_chunks = D_FF // FFN_CHUNK

    def body(dx3_ref, x2_ref, uu_ref, cc_ref, gffn_ref, gw_hbm, fw_ref,
             dx2_ref, dx2b_ref, duu_ref, dfb_ref, dfw_ref, dg_ref,
             wup_v, wdown_v, carry, sem):
        i = pl.program_id(0)
        t = n_tiles - 1 - i

        _start_weights(gw_hbm, ("w_down", "w_up"), (wdown_v, wup_v), sem)

        @pl.when(i == 0)
        def _():
            dfb_ref[...] = jnp.zeros_like(dfb_ref)
            dfw_ref[...] = jnp.zeros_like(dfw_ref)
            dg_ref[...] = jnp.zeros_like(dg_ref)

        @pl.when(t % tps == tps - 1)
        def _():
            carry[...] = jnp.zeros_like(carry)

        dx3v = dx3_ref[...]
        dx3b = dx3v.astype(BF16)
        dh3 = jnp.zeros((tm, D_MODEL), F32)
        for jc in range(n_chunks):
            da = _dot_nt(dx3b, wdown_v[pl.ds(jc * FFN_CHUNK, FFN_CHUNK), :])
            colss = [pl.ds(half * D_FF + jc * FFN_CHUNK, FFN_CHUNK) for half in range(2)]
            gate, val = [cc_ref[:, cols].astype(F32) for cols in colss]
            sg = _sigmoid(gate)
            dgate = da * val * (sg * (1.0 + gate * (1.0 - sg)))
            dval = da * (gate * sg)
            for dcc, cols in zip((dgate, dval), colss):
                uu = uu_ref[:, cols].astype(F32)
                dfb_ref[:, cols] += _colsum(dcc)
                ext = jnp.concatenate([dcc, carry[:, cols]], axis=0)
                carry[:, cols] = dcc[:FFN_HALO, :]
                n1 = pltpu.roll(ext, tm + FFN_HALO - 1, 0)[:tm, :]
                n2 = pltpu.roll(ext, tm + FFN_HALO - 2, 0)[:tm, :]
                duu = fw_ref[2:3, cols] * dcc + fw_ref[1:2, cols] * n1 + fw_ref[0:1, cols] * n2
                dfw_ref[2:3, cols] += _colsum(uu * dcc)
                dfw_ref[1:2, cols] += _colsum(uu * n1)
                dfw_ref[0:1, cols] += _colsum(uu * n2)
                duub = duu.astype(BF16)
                duu_ref[:, cols] = duub
                dh3 = dh3 + _dot(duub, wup_v[cols, :])
        xh, r = _rms_fwd(x2_ref[...])
        dg_ref[...] += _colsum(dh3 * xh)
        dx2 = dx3v + _rms_bwd(dh3, xh, r, gffn_ref[...])
        dx2_ref[...] = dx2
        dx2b_ref[...] = dx2.astype(BF16)

    rev = lambda w: pl.BlockSpec((tm, w), lambda i: (n_tiles - 1 - i, 0))
    return pl.pallas_call(
        body, name="bwd_ffn", grid=(n_tiles,),
        in_specs=[rev(D_MODEL), rev(D_MODEL), rev(2 * D_FF), rev(2 * D_FF), _full((1, D_MODEL)),
                  pl.BlockSpec(memory_space=pl.ANY), _full((FFN_CONV_WIDTH, 2 * D_FF))],
        out_specs=[rev(D_MODEL), rev(D_MODEL), rev(2 * D_FF), _full((1, 2 * D_FF)), _full((FFN_CONV_WIDTH, 2 * D_FF)),
                   _full((1, D_MODEL))],
        out_shape=[jax.ShapeDtypeStruct((tokens, D_MODEL), F32), jax.ShapeDtypeStruct((tokens, D_MODEL), BF16),
                   jax.ShapeDtypeStruct((tokens, 2 * D_FF), BF16),
                   jax.ShapeDtypeStruct((1, 2 * D_FF), F32), jax.ShapeDtypeStruct((FFN_CONV_WIDTH, 2 * D_FF), F32),
                   jax.ShapeDtypeStruct((1, D_MODEL), F32)],
        scratch_shapes=[pltpu.VMEM((2 * D_FF, D_MODEL), BF16), pltpu.VMEM((D_FF, D_MODEL), BF16),
                        pltpu.VMEM((FFN_HALO, 2 * D_FF), F32), pltpu.SemaphoreType.DMA((2,))],
        compiler_params=_params(),
    )(dx3, x2, uu_all, cc_all, g_ffn, gw, ffn_w)


def _bwd_attn(dx2, x1, q, kv, gw, g_x, after, seq, tm):
    tokens = x1.shape[0]
    n_tiles = tokens // tm
    tps = seq // tm
    n_b = tokens // seq

    def body(dx2_ref, x1_ref, q_ref, kv_ref, g_ref, gw_hbm, after_ref, dx1_ref, dx1b_ref, dq_ref, dkv_ref, dg_ref,
             wq_v, wo_v, sem):
        del after_ref
        i = pl.program_id(0)

        _start_weights(gw_hbm, ("w_o", "w_q"), (wo_v, wq_v), sem)

        @pl.when(i == 0)
        def _():
            dg_ref[...] = jnp.zeros_like(dg_ref)

        @pl.when(i % tps == 0)
        def _():
            dkv_ref[...] = jnp.zeros_like(dkv_ref)

        dx2v = dx2_ref[...]
        do = _dot_nt(dx2v.astype(BF16), wo_v[...]).astype(BF16)
        q = q_ref[...]
        heads = [slice(h * HEAD_DIM, (h + 1) * HEAD_DIM) for h in range(HEADS)]
        kcols = [pl.ds(h * HEAD_DIM, HEAD_DIM) for h in range(HEADS)]
        vcols = [pl.ds(D_MODEL + h * HEAD_DIM, HEAD_DIM) for h in range(HEADS)]
        scores = [_dot_nt(q[:, hd], kv_ref[:, kc]) for hd, kc in zip(heads, kcols)]
        dps = [_dot_nt(do[:, hd], kv_ref[:, vc]) for hd, vc in zip(heads, vcols)]
        probs = [_softmax_rows(s) for s in scores]
        dss = [(p * (dp - jnp.sum(dp * p, axis=-1, keepdims=True))).astype(BF16) for p, dp in zip(probs, dps)]
        for p, hd, vc in zip(probs, heads, vcols):
            dkv_ref[:, vc] += _dot_tn(p.astype(BF16), do[:, hd])
        dqs = [_dot(ds, kv_ref[:, kc]) * (HEAD_DIM ** -0.5) for ds, kc in zip(dss, kcols)]
        for ds, hd, kc in zip(dss, heads, kcols):
            dkv_ref[:, kc] += _dot_tn(ds, q[:, hd])
        dq = jnp.concatenate(dqs, axis=-1).astype(BF16)
        dq_ref[...] = dq
        dh2 = _dot_nt(dq, wq_v[...])
        xh, r = _rms_fwd(x1_ref[...])
        dg_ref[...] += _colsum(dh2 * xh)
        dx1 = dx2v + _rms_bwd(dh2, xh, r, g_ref[...])
        dx1_ref[...] = dx1
        dx1b_ref[...] = dx1.astype(BF16)

    row = lambda w: pl.BlockSpec((tm, w), lambda i: (i, 0))
    per_b = pl.BlockSpec((N_MEM, 2 * D_MODEL), lambda i: (i // tps, 0))
    return pl.pallas_call(
        body, name="bwd_attn", grid=(n_tiles,),
        in_specs=[row(D_MODEL), row(D_MODEL), row(D_MODEL), per_b, _full((1, D_MODEL)), pl.BlockSpec(memory_space=pl.ANY),
                  _full(after.shape)],
        out_specs=[row(D_MODEL), row(D_MODEL), row(D_MODEL), per_b, _full((1, D_MODEL))],
        out_shape=[jax.ShapeDtypeStruct((tokens, D_MODEL), F32), jax.ShapeDtypeStruct((tokens, D_MODEL), BF16),
                   jax.ShapeDtypeStruct((tokens, D_MODEL), BF16),
                   jax.ShapeDtypeStruct((n_b * N_MEM, 2 * D_MODEL), F32), jax.ShapeDtypeStruct((1, D_MODEL), F32)],
        scratch_shapes=[pltpu.VMEM((D_MODEL, D_MODEL), BF16), pltpu.VMEM((D_MODEL, D_MODEL), BF16), pltpu.SemaphoreType.DMA((2,))],
        compiler_params=_params(),
    )(dx2, x1, q, kv, g_x, gw, after)


def _bwd_kv(dkv, mem2d, gw):
    rows = mem2d.shape[0]
    n_b = rows // N_MEM

    def body(dkv_ref, mem_ref, gw_hbm, dkvb_ref, dg_ref, wkv_v, sem):
        @pl.when(pl.program_id(0) == 0)
        def _():
            copies = _load_weight(gw_hbm, "w_kv", wkv_v, sem)
            for cp in copies:
                cp.start()
            for cp in copies:
                cp.wait()
            dg_ref[...] = jnp.zeros_like(dg_ref)

        dkvb = dkv_ref[...].astype(BF16)
        dkvb_ref[...] = dkvb
        dmn = _dot(dkvb, wkv_v[...])
        mh, _ = _rms_fwd(mem_ref[...])
        dg_ref[...] += _colsum(dmn * mh)

    return pl.pallas_call(
        body, name="bwd_kv", grid=(n_b,),
        in_specs=[pl.BlockSpec((N_MEM, 2 * D_MODEL), lambda b: (b, 0)), pl.BlockSpec((N_MEM, D_MODEL), lambda b: (b, 0)),
                  pl.BlockSpec(memory_space=pl.ANY)],
        out_specs=[pl.BlockSpec((N_MEM, 2 * D_MODEL), lambda b: (b, 0)), _full((1, D_MODEL))],
        out_shape=[jax.ShapeDtypeStruct((rows, 2 * D_MODEL), BF16), jax.ShapeDtypeStruct((1, D_MODEL), F32)],
        scratch_shapes=[pltpu.VMEM((2 * D_MODEL, D_MODEL), BF16), pltpu.SemaphoreType.DMA],
        compiler_params=_params(),
    )(dkv, mem2d, gw)


def _bwd_mix(dx1, x2d, u_all, c_all, pooled_all, gw, g_mix, conv_w, ln_g, ln_b, pool_w, pool_scale, after, seq, tm):
    tokens = x2d.shape[0]
    n_tiles = tokens // tm
    tps = seq // tm

    def body(dx1_ref, x_ref, u_ref, c_ref, pooled_ref, gmix_ref, gw_hbm, cw_ref, lng_ref, lnb_ref, pw_ref, ps_ref,
             after_ref, dx_ref, du_ref, dgmix_ref, dcw_ref, dcb_ref, dlng_ref, dlnb_ref, dpw_ref, dps_ref,
             win_v, wout_v, dc_carry, e_carry, sem):
        del after_ref
        i = pl.program_id(0)
        t = n_tiles - 1 - i

        _start_weights(gw_hbm, ("w_out", "w_in"), (wout_v, win_v), sem)

        @pl.when(i == 0)
        def _():
            for ref in (dgmix_ref, dcw_ref, dcb_ref, dlng_ref, dlnb_ref, dpw_ref, dps_ref):
                ref[...] = jnp.zeros_like(ref)

        @pl.when(t % tps == tps - 1)
        def _():
            dc_carry[...] = jnp.zeros_like(dc_carry)
            e_carry[...] = jnp.zeros_like(e_carry)

        dx1v = dx1_ref[...]
        dymix = _dot_nt(dx1v.astype(BF16), wout_v[...])
        dyc, dyp = dymix[:, :D_CONV], dymix[:, D_CONV:]
        u = u_ref[...]
        val, gate = u[:, :D_CONV], u[:, D_CONV:2 * D_CONV]

        conv = c_ref[...]
        mu = jnp.mean(conv, axis=-1, keepdims=True)
        cen = conv - mu
        rs = lax.rsqrt(jnp.mean(cen * cen, axis=-1, keepdims=True) + EPS)
        chat = cen * rs
        ln = chat * lng_ref[...] + lnb_ref[...]
        sl = _sigmoid(ln)
        dln = dyc * (sl * (1.0 + ln * (1.0 - sl)))
        dlng_ref[...] += _colsum(dln * chat)
        dlnb_ref[...] += _colsum(dln)
        dchat = dln * lng_ref[...]
        dc = rs * (dchat - jnp.mean(dchat, axis=-1, keepdims=True)
                   - chat * jnp.mean(dchat * chat, axis=-1, keepdims=True))
        dcb_ref[...] += _colsum(dc)
        sg = _sigmoid(gate)
        hc = val * sg
        ext = jnp.concatenate([dc, dc_carry[...]], axis=0)
        dc_carry[...] = dc[:CONV_HALO, :]
        dhc = jnp.zeros((tm, D_CONV), F32)
        ahead_by = _sublane_shifts(ext)
        for k in range(CONV_WIDTH):
            whole, part = divmod(CONV_WIDTH - 1 - k, 8)
            tap = ahead_by[part][8 * whole:8 * whole + tm, :]
            dhc = dhc + cw_ref[k:k + 1, :] * tap
            dcw_ref[k:k + 1, :] += _colsum_mxu(hc * tap)
        du_ref[:, :D_CONV] = (dhc * sg).astype(BF16)
        du_ref[:, D_CONV:2 * D_CONV] = (dhc * val * (sg * (1.0 - sg))).astype(BF16)

        pos = lax.broadcasted_iota(jnp.int32, (tm, 1), 0) + (t % tps) * tm
        es, dpooled = [], []
        for g, w in enumerate(POOL_WINDOWS):
            cols = pl.ds(g * POOL_GROUP_DIM, POOL_GROUP_DIM)
            lo = g * POOL_GROUP_DIM
            pooled = pooled_ref[:, cols]
            pw = pw_ref[g].astype(BF16)
            dyg = dyp[:, lo:lo + POOL_GROUP_DIM]
            dps_ref[:, cols] += _colsum(dyg * _dot(pooled, pw))
            dmixed = (dyg * ps_ref[:, cols]).astype(BF16)
            dpw_ref[g] += _dot_tn(pooled, dmixed)
            dpo = _dot_nt(dmixed, pw)
            dpooled.append(dpo)
            es.append(dpo / jnp.minimum(pos + 1, w).astype(F32))
        e = jnp.concatenate(es, axis=-1)
        run = jnp.concatenate([e, e_carry[...]], axis=0)
        e_carry[...] = e[:POOL_HALO, :]
        rows = tm + POOL_HALO
        for g, w in enumerate(POOL_WINDOWS):
            lo = g * POOL_GROUP_DIM
            run = run[:, POOL_GROUP_DIM if g else 0:]
            run = run + pltpu.roll(run, rows - w // 2, 0)
            du_ref[:, 2 * D_CONV + lo:2 * D_CONV + lo + POOL_GROUP_DIM] = (
                run[:tm, :POOL_GROUP_DIM] - dpooled[g]).astype(BF16)

        dh1 = _dot(du_ref[...], win_v[...])
        xh, r = _rms_fwd(x_ref[...])
        dgmix_ref[...] += _colsum(dh1 * xh)
        dx_ref[...] = dx1v + _rms_bwd(dh1, xh, r, gmix_ref[...])

    rev = lambda w: pl.BlockSpec((tm, w), lambda i: (n_tiles - 1 - i, 0))
    return pl.pallas_call(
        body, name="bwd_mix", grid=(n_tiles,),
        in_specs=[rev(D_MODEL), rev(D_MODEL), rev(D_IN), rev(D_CONV), rev(D_POOL), _full((1, D_MODEL)),
                  pl.BlockSpec(memory_space=pl.ANY), _full((CONV_WIDTH, D_CONV)), _full((1, D_CONV)), _full((1, D_CONV)),
                  _full((4, POOL_GROUP_DIM, POOL_GROUP_DIM)), _full((1, D_POOL)), _full(after.shape)],
        out_specs=[rev(D_MODEL), rev(D_IN), _full((1, D_MODEL)), _full((CONV_WIDTH, D_CONV)), _full((1, D_CONV)),
                   _full((1, D_CONV)), _full((1, D_CONV)), _full((4, POOL_GROUP_DIM, POOL_GROUP_DIM)), _full((1, D_POOL))],
        out_shape=[jax.ShapeDtypeStruct((tokens, D_MODEL), F32), jax.ShapeDtypeStruct((tokens, D_IN), BF16),
                   jax.ShapeDtypeStruct((1, D_MODEL), F32), jax.ShapeDtypeStruct((CONV_WIDTH, D_CONV), F32),
                   jax.ShapeDtypeStruct((1, D_CONV), F32), jax.ShapeDtypeStruct((1, D_CONV), F32),
                   jax.ShapeDtypeStruct((1, D_CONV), F32),
                   jax.ShapeDtypeStruct((4, POOL_GROUP_DIM, POOL_GROUP_DIM), F32), jax.ShapeDtypeStruct((1, D_POOL), F32)],
        scratch_shapes=[pltpu.VMEM((D_IN, D_MODEL), BF16), pltpu.VMEM((D_MODEL, D_MODEL), BF16),
                        pltpu.VMEM((CONV_HALO, D_CONV), F32), pltpu.VMEM((POOL_HALO, D_POOL), F32),
                        pltpu.SemaphoreType.DMA((2,))],
        compiler_params=_params(),
    )(dx1, x2d, u_all, c_all, pooled_all, g_mix, gw, conv_w, ln_g, ln_b, pool_w, pool_scale, after)


def _wgrad(a, b, name, after=None):
    tokens, m = a.shape
    n = b.shape[1]
    tm = 512 if m % 512 == 0 else 256
    extra = [] if after is None else [after]

    def body(a_ref, b_ref, *rest):
        rest[-1][...] = _dot_tn(a_ref[...], b_ref[...]).astype(rest[-1].dtype)

    return pl.pallas_call(
        body, name=name, grid=(m // tm,),
        in_specs=[pl.BlockSpec((tokens, tm), lambda i: (0, i)), _full((tokens, n))] + [_full(t.shape) for t in extra],
        out_specs=pl.BlockSpec((tm, n), lambda i: (i, 0)),
        out_shape=jax.ShapeDtypeStruct((m, n), BF16),
        compiler_params=_params(),
    )(a, b, *extra)


def _adamw_update(w, g, m, v):
    nm = ADAM_B1 * m + (1.0 - ADAM_B1) * g
    nv = ADAM_B2 * v + (1.0 - ADAM_B2) * (g * g)
    m_hat = nm / (1.0 - ADAM_B1 ** ADAM_STEP)
    v_hat = nv / (1.0 - ADAM_B2 ** ADAM_STEP)
    return -ADAM_LR * (m_hat / (jnp.sqrt(v_hat) + ADAM_EPS) + ADAM_WD * w), nm, nv


def _adamw_small(ws, gs, ms, vs):
    n = len(ws)

    def body(*refs):
        ins, outs = refs[:4 * n], refs[4 * n:]
        for k in range(n):
            d, nm, nv = _adamw_update(*[ins[j * n + k][...] for j in range(4)])
            outs[k][...] = d
            outs[n + k][...] = nm
            outs[2 * n + k][...] = nv

    vmem = pl.BlockSpec(memory_space=pltpu.VMEM)
    outs = pl.pallas_call(
        body, name="adamw_small",
        in_specs=[vmem] * (4 * n), out_specs=[vmem] * (3 * n),
        out_shape=[jax.ShapeDtypeStruct(w.shape, F32) for w in ws] * 3,
    )(*ws, *gs, *ms, *vs)
    return outs[:n], outs[n:2 * n], outs[2 * n:]


def _adamw(w, g, m, v, name):
    rows, cols = w.shape
    tile = rows
    for cand in (512, 256, 128, 64, 32, 16, 8):
        if rows % cand == 0:
            tile = cand
            break

    def body(w_ref, g_ref, m_ref, v_ref, d_ref, nm_ref, nv_ref):
        d_ref[...], nm_ref[...], nv_ref[...] = _adamw_update(w_ref[...], g_ref[...], m_ref[...], v_ref[...])

    spec = pl.BlockSpec((tile, cols), lambda i: (i, 0))
    return pl.pallas_call(
        body, name=name, grid=(rows // tile,),
        in_specs=[spec] * 4, out_specs=[spec] * 3,
        out_shape=[jax.ShapeDtypeStruct((rows, cols), F32)] * 3,
        compiler_params=_params(("arbitrary",)),
    )(w, g, m, v)


SMALL = (("norm_mix_g", (1, 1024)), ("conv_dw_b", (1, 512)), ("conv_ln_g", (1, 512)), ("conv_ln_b", (1, 512)),
         ("pool_w", (1, 4, 128, 128)), ("pool_scale", (1, 512)), ("norm_xattn_g", (1, 1024)), ("norm_mem_g", (1, 1024)),
         ("norm_ffn_g", (1, 1024)), ("ffn_dw_b", (1, 5632)), ("norm_final_g", (1024,)))
LANES = 128


def _pack_rows(arrs):
    flat = jnp.concatenate([a.reshape(-1) for a in arrs])
    pad = (-flat.shape[0]) % (8 * LANES)
    return jnp.pad(flat, (0, pad)).reshape(-1, LANES)


def kernel(x, mem, norm_mix_g, w_in, conv_dw_w, conv_dw_b, conv_ln_g, conv_ln_b, pool_w, pool_scale, w_out, norm_xattn_g, norm_mem_g, w_q, w_kv, w_o, norm_ffn_g, w_up, ffn_dw_w, ffn_dw_b, w_down, norm_final_g, loss_target, m_norm_mix_g, m_w_in, m_conv_dw_w, m_conv_dw_b, m_conv_ln_g, m_conv_ln_b, m_pool_w, m_pool_scale, m_w_out, m_norm_xattn_g, m_norm_mem_g, m_w_q, m_w_kv, m_w_o, m_norm_ffn_g, m_w_up, m_ffn_dw_w, m_ffn_dw_b, m_w_down, m_norm_final_g, v_norm_mix_g, v_w_in, v_conv_dw_w, v_conv_dw_b, v_conv_ln_g, v_conv_ln_b, v_pool_w, v_pool_scale, v_w_out, v_norm_xattn_g, v_norm_mem_g, v_w_q, v_w_kv, v_w_o, v_norm_ffn_g, v_w_up, v_ffn_dw_w, v_ffn_dw_b, v_w_down, v_norm_final_g):
    weights = dict(norm_mix_g=norm_mix_g, w_in=w_in, conv_dw_w=conv_dw_w, conv_dw_b=conv_dw_b, conv_ln_g=conv_ln_g,
                   conv_ln_b=conv_ln_b, pool_w=pool_w, pool_scale=pool_scale, w_out=w_out, norm_xattn_g=norm_xattn_g,
                   norm_mem_g=norm_mem_g, w_q=w_q, w_kv=w_kv, w_o=w_o, norm_ffn_g=norm_ffn_g, w_up=w_up,
                   ffn_dw_w=ffn_dw_w, ffn_dw_b=ffn_dw_b, w_down=w_down, norm_final_g=norm_final_g)
    moments_m = dict(norm_mix_g=m_norm_mix_g, w_in=m_w_in, conv_dw_w=m_conv_dw_w, conv_dw_b=m_conv_dw_b,
                     conv_ln_g=m_conv_ln_g, conv_ln_b=m_conv_ln_b, pool_w=m_pool_w, pool_scale=m_pool_scale,
                     w_out=m_w_out, norm_xattn_g=m_norm_xattn_g, norm_mem_g=m_norm_mem_g, w_q=m_w_q, w_kv=m_w_kv,
                     w_o=m_w_o, norm_ffn_g=m_norm_ffn_g, w_up=m_w_up, ffn_dw_w=m_ffn_dw_w, ffn_dw_b=m_ffn_dw_b,
                     w_down=m_w_down, norm_final_g=m_norm_final_g)
    moments_v = dict(norm_mix_g=v_norm_mix_g, w_in=v_w_in, conv_dw_w=v_conv_dw_w, conv_dw_b=v_conv_dw_b,
                     conv_ln_g=v_conv_ln_g, conv_ln_b=v_conv_ln_b, pool_w=v_pool_w, pool_scale=v_pool_scale,
                     w_out=v_w_out, norm_xattn_g=v_norm_xattn_g, norm_mem_g=v_norm_mem_g, w_q=v_w_q, w_kv=v_w_kv,
                     w_o=v_w_o, norm_ffn_g=v_norm_ffn_g, w_up=v_w_up, ffn_dw_w=v_ffn_dw_w, ffn_dw_b=v_ffn_dw_b,
                     w_down=v_w_down, norm_final_g=v_norm_final_g)
    order = list(weights)
    transposed = ("w_in", "w_kv", "w_up")

    n_b, seq, _ = x.shape
    tokens = n_b * seq
    tm_mix = min(512, seq // 2)
    tm_ffn = min(256, seq // 2)
    dev = 4 * lax.axis_index("x") + 2 * lax.axis_index("y") + lax.axis_index("c")

    packs = [jnp.concatenate([weights[n][0].T if n in transposed else weights[n][0] for n in names], axis=0).astype(BF16)
             for names in AG_GROUPS]
    small_sharded = _pack_rows([conv_dw_w[0], ffn_dw_w[0]])
    gw_mix, gsmall = _all_gather([packs[0], small_sharded], "weights_all_gather")
    flights = []
    after = gw_mix
    for k in (1, 2):
        own_in_place = lax.dynamic_update_slice(lax.empty((N_DEV,) + packs[k].shape, BF16), packs[k][None], (dev, 0, 0))
        flights.append(_gather_start(own_in_place, after, "weights_gather_start_%d" % k, BARRIER_IDS["gather_start"][k - 1]))
        after = flights[-1][3]
    gflat = gsmall.reshape(N_DEV, -1)
    n_cw = CONV_WIDTH * (D_CONV // N_DEV)
    n_fw = FFN_CONV_WIDTH * (2 * D_FF // N_DEV)
    conv_w = gflat[:, :n_cw].reshape(N_DEV, CONV_WIDTH, D_CONV // N_DEV).transpose(1, 0, 2).reshape(CONV_WIDTH, D_CONV)
    ffn_w = gflat[:, n_cw:n_cw + n_fw].reshape(N_DEV, FFN_CONV_WIDTH, 2 * D_FF // N_DEV).transpose(1, 0, 2).reshape(
        FFN_CONV_WIDTH, 2 * D_FF)

    x2d = x.reshape(tokens, D_MODEL)
    mem2d = mem.reshape(n_b * N_MEM, D_MODEL)
    tgt2d = loss_target.reshape(tokens, D_MODEL)
    g_final = norm_final_g.reshape(1, D_MODEL)

    def gather_finish(flight, after, tag):
        fwd_send, fwd_recv, buf = _gather_forward(*flight[:3], after, "weights_gather_forward_" + tag,
                                                  BARRIER_IDS["gather_forward"][int(tag) - 1])
        return _gather_finish(fwd_send, fwd_recv, buf, "weights_gather_finish_" + tag)

    x1, u_all, c_all, pooled_all, ymix, h1 = _fwd_mix(
        x2d, gw_mix, norm_mix_g, conv_w, conv_dw_b, conv_ln_g, conv_ln_b, pool_w[0], pool_scale, flights[1][3],
        seq, tm_mix)
    gw_attn = gather_finish(flights[0], x1, "1")
    mem_n, kv = _fwd_kv(mem2d, gw_attn, norm_mem_g)
    x2, h2, q, o = _fwd_attn(x1, kv, gw_attn, norm_xattn_g, seq, tm_mix)
    gw_ffn = gather_finish(flights[1], x2, "2")
    uu_all, cc_all, a_all, h3, dx3, dx3b, loss_part, dg_final = _fwd_ffn(
        x2, tgt2d, gw_ffn, norm_ffn_g, ffn_w, ffn_dw_b, g_final, seq, tm_ffn)

    table = _owner_table()

    def sibling_start(names, tag):
        parts = [part[n].reshape(N_DEV, W_OFF[n][1], D_MODEL) for n in names]
        return _exchange_start(parts, 4, _to_sibling, "rs_sibling_exchange_start_" + tag, BARRIER_IDS["sibling"][tag])

    def chips_start(flight, after, tag):
        parts, landed = _exchange_wait(*flight[:4], after, 4, _to_sibling, "rs_sibling_exchange_wait_" + tag)
        sums = _chip_partial_sums(table, parts, landed, "rs_chip_partial_sums_" + tag)
        return parts, landed, _exchange_start(sums, 3, _to_chip, "rs_chip_exchange_start_" + tag,
                                              BARRIER_IDS["chips"][tag])

    grads, delta, new_m, new_v = {}, {}, {}, {}

    def reduce_finish(names, parts, landed, flight, after, tag):
        _, from_chips = _exchange_wait(*flight[:4], after, 3, _to_chip, "rs_chip_exchange_wait_" + tag)
        as_rows = {n: n not in transposed or W_OFF[n][1] % LANES != 0 for n in names}
        states = [tuple(t[n][0].T if n in transposed else t[n][0] for t in (weights, moments_m, moments_v))
                  if as_rows[n] else None for n in names]
        results = _final_update(table, parts, landed, from_chips, states, "rs_final_update_" + tag)
        for n, res in zip(names, results):
            back = (lambda t: t.T[None]) if n in transposed else (lambda t: t[None])
            grads[n] = back(res[0])
            if as_rows[n]:
                delta[n], new_m[n], new_v[n] = [back(t) for t in res[1:]]
            else:
                delta[n], new_m[n], new_v[n] = [t[None] for t in _adamw(
                    weights[n][0], grads[n][0], moments_m[n][0], moments_v[n][0], "adamw_" + n)]
        alone = [n for n in names if not as_rows[n]]
        return delta[alone[-1] if alone else names[-1]]

    part = {}
    dx2, dx2b, duu, d_ffn_b, d_ffn_w, dg_ffn = _bwd_ffn(dx3, x2, uu_all, cc_all, gw_ffn, norm_ffn_g, ffn_w, seq, tm_ffn)
    part["w_up"] = _wgrad(duu, h3, "wgrad_w_up")
    part["w_down"] = _wgrad(a_all, dx3b, "wgrad_w_down")
    to_sibling_a = sibling_start(RS_GROUPS["a"], "a")
    dx1, dx1b, dq, dkv, dg_x = _bwd_attn(dx2, x1, q, kv, gw_attn, norm_xattn_g, to_sibling_a[4], seq, tm_mix)
    parts_a, landed_a, flight_a = chips_start(to_sibling_a, dx1, "a")
    dkv_b, dg_mem = _bwd_kv(dkv, mem2d, gw_attn)
    part["w_q"] = _wgrad(h2, dq, "wgrad_w_q", after=flight_a[4])
    part["w_kv"] = _wgrad(dkv_b, mem_n, "wgrad_w_kv")
    part["w_o"] = _wgrad(o, dx2b, "wgrad_w_o")
    part["w_out"] = _wgrad(ymix, dx1b, "wgrad_w_out")
    to_sibling_b = sibling_start(RS_GROUPS["b"], "b")
    parts_b, landed_b, flight_b = chips_start(to_sibling_b, to_sibling_b[4], "b")
    dx, du, dg_mix, d_conv_w, d_conv_b, d_ln_g, d_ln_b, d_pool_w, d_pool_scale = _bwd_mix(
        dx1, x2d, u_all, c_all, pooled_all, gw_mix, norm_mix_g, conv_w, conv_ln_g, conv_ln_b, pool_w[0], pool_scale,
        flight_b[4], seq, tm_mix)
    grad_x = dx.reshape(x.shape)

    small_grads = dict(norm_mix_g=dg_mix, conv_dw_b=d_conv_b, conv_ln_g=d_ln_g, conv_ln_b=d_ln_b, pool_w=d_pool_w,
                       pool_scale=d_pool_scale, norm_xattn_g=dg_x, norm_mem_g=dg_mem, norm_ffn_g=dg_ffn,
                       ffn_dw_b=d_ffn_b, norm_final_g=dg_final)
    small_list = [small_grads[n] for n, _ in SMALL] + [d_conv_w, d_ffn_w, loss_part[:1]]
    small_mine = _pack_rows(small_list)
    small_flight = _broadcast_start(
        lax.dynamic_update_slice(lax.empty((N_DEV,) + small_mine.shape, F32), small_mine[None], (dev, 0, 0)),
        "small_grads_broadcast_start", BARRIER_IDS["broadcast"])

    part["w_in"] = _wgrad(du, h1, "wgrad_w_in", after=small_flight[3])
    to_sibling_c = sibling_start(RS_GROUPS["c"], "c")
    parts_c, landed_c, flight_c = chips_start(to_sibling_c, to_sibling_c[4], "c")
    updated_a = reduce_finish(RS_GROUPS["a"], parts_a, landed_a, flight_a, flight_c[4], "a")
    updated_b = reduce_finish(RS_GROUPS["b"], parts_b, landed_b, flight_b, updated_a, "b")
    small_all = _broadcast_wait(*small_flight[:3], updated_b, "small_grads_broadcast_wait")
    small_sum = _sum_blocks(small_all).reshape(-1)

    pos = 0
    for n, shape in SMALL:
        size = 1
        for s in shape:
            size *= s
        grads[n] = small_sum[pos:pos + size].reshape(shape)
        pos += size
    full_conv_w = small_sum[pos:pos + CONV_WIDTH * D_CONV].reshape(CONV_WIDTH, D_CONV)
    pos += CONV_WIDTH * D_CONV
    full_ffn_w = small_sum[pos:pos + FFN_CONV_WIDTH * 2 * D_FF].reshape(FFN_CONV_WIDTH, 2 * D_FF)
    loss = small_sum[pos + FFN_CONV_WIDTH * 2 * D_FF]
    grads["conv_dw_w"] = lax.dynamic_slice_in_dim(full_conv_w, dev * (D_CONV // N_DEV), D_CONV // N_DEV, axis=1)[None]
    grads["ffn_dw_w"] = lax.dynamic_slice_in_dim(full_ffn_w, dev * (2 * D_FF // N_DEV), 2 * D_FF // N_DEV, axis=1)[None]

    small_names = [n for n in order if n not in W_OFF]
    swap = lambda t: jnp.transpose(t, (1, 0, 2))
    two_d = lambda t: t.reshape(1, -1) if t.ndim == 1 else (swap(t) if t.ndim == 3 else t)
    outs = _adamw_small(*[[two_d(t[n]) for n in small_names] for t in (weights, grads, moments_m, moments_v)])
    for res, out in zip((delta, new_m, new_v), outs):
        for n, o in zip(small_names, out):
            res[n] = swap(o) if o.ndim == 3 else o.reshape(weights[n].shape)

    reduce_finish(RS_GROUPS["c"], parts_c, landed_c, flight_c, delta[small_names[-1]], "c")

    return (loss, grad_x, *[grads[n] for n in order], *[delta[n] for n in order],
            *[new_m[n] for n in order], *[new_v[n] for n in order])
```

```python
import jax
import jax.numpy as jnp
from jax import lax
from jax.experimental import pallas as pl
from jax.experimental.pallas import tpu as pltpu

F32 = jnp.float32
BF16 = jnp.bfloat16
MESH = pl.DeviceIdType.MESH

N_DEV = 8
D_MODEL = 1024
D_CONV = 512
D_POOL = 512
CONV_WIDTH = 31
POOL_WINDOWS = (2, 4, 8, 16)
POOL_GROUP_DIM = 128
D_IN = 1536
N_MEM = 256
HEADS = 4
HEAD_DIM = 256
D_FF = 2816
FFN_CONV_WIDTH = 3
EPS = 1e-6
ADAM_LR = 0.001
ADAM_B1 = 0.9
ADAM_B2 = 0.999
ADAM_EPS = 1e-08
ADAM_WD = 0.01
ADAM_STEP = 10

VMEM_LIMIT_V7X = 56 * 1024 * 1024
CONV_HALO = 32
POOL_HALO = 16
FFN_HALO = 8
FFN_CHUNK = 2816

W_ROWS = (("w_in", 192), ("w_out", 128), ("w_q", 128), ("w_kv", 256), ("w_o", 128), ("w_up", 704), ("w_down", 352))
AG_GROUPS = (("w_in", "w_out"), ("w_q", "w_kv", "w_o"), ("w_up", "w_down"))
W_OFF = {}
for _names in AG_GROUPS:
    _o = 0
    for _n in _names:
        W_OFF[_n] = (_o, dict(W_ROWS)[_n])
        _o += dict(W_ROWS)[_n]
RS_GROUPS = {"a": ("w_up", "w_down"), "b": ("w_q", "w_kv", "w_o", "w_out"), "c": ("w_in",)}
BARRIER_IDS = {"gather_start": (0, 1), "gather_forward": (2, 3), "sibling": {"a": 4, "b": 5, "c": 6},
               "chips": {"a": 7, "b": 8, "c": 9}, "broadcast": 10}


def _dot(a, b):
    return jnp.dot(a, b, preferred_element_type=F32)


def _dot_nt(a, b):
    return lax.dot_general(a, b, (((1,), (1,)), ((), ())), preferred_element_type=F32)


def _dot_tn(a, b):
    return lax.dot_general(a, b, (((0,), (0,)), ((), ())), preferred_element_type=F32)


def _sigmoid(v):
    return 1.0 / (1.0 + jnp.exp(-v))


def _rms_fwd(v):
    r = lax.rsqrt(jnp.mean(v * v, axis=-1, keepdims=True) + EPS)
    return v * r, r


def _rms_bwd(dh, vh, r, g):
    gd = dh * g
    return r * (gd - vh * jnp.mean(gd * vh, axis=-1, keepdims=True))


def _sublane_shifts(v):
    rows = v.shape[0]
    return [v] + [pltpu.roll(v, rows - b, 0) for b in range(1, 8)]


def _colsum(v):
    return jnp.sum(v, axis=0, keepdims=True)


def _colsum_mxu(v):
    return _dot(jnp.ones((8, v.shape[0]), BF16), v.astype(BF16))[0:1, :]


def _full(shape):
    return pl.BlockSpec(shape, lambda *_: (0,) * len(shape))


def _params(sem=("arbitrary",), vmem=VMEM_LIMIT_V7X):
    return pltpu.CompilerParams(dimension_semantics=sem, vmem_limit_bytes=vmem)


def _load_weight(g_hbm, name, dst, sem):
    off, rows = W_OFF[name]
    return [pltpu.make_async_copy(g_hbm.at[d, pl.ds(off, rows), :], dst.at[pl.ds(d * rows, rows), :], sem)
            for d in range(N_DEV)]


def _start_weights(g_hbm, names, dsts, sems):
    @pl.when(pl.program_id(0) == 0)
    def _():
        copies = [_load_weight(g_hbm, name, dst, sems.at[k]) for k, (name, dst) in enumerate(zip(names, dsts))]
        for cp in sum(copies, []):
            cp.start()
        for cp in sum(copies, []):
            cp.wait()


def _position():
    x, y, c = lax.axis_index("x"), lax.axis_index("y"), lax.axis_index("c")
    chips = [(1 - x, y), (x, 1 - y), (1 - x, 1 - y)]
    return x, y, c, chips


def _dev(px, py, pc):
    return 4 * px + 2 * py + pc


def _all_gather(arrs, name):
    n = len(arrs)

    def body(*refs):
        ins, outs = refs[:n], refs[n:2 * n]
        send_sems, recv_sems, local_sems = refs[2 * n:2 * n + 3]
        bounce = refs[2 * n + 3:]
        x, y, c, chips = _position()
        me, sibling = (x, y, c), (x, y, 1 - c)

        def copy(a, k, block, to, src=None):
            rows = outs[a].at[_dev(*block)]
            return pltpu.make_async_remote_copy(
                src_ref=rows if src is None else src, dst_ref=rows,
                send_sem=send_sems.at[a, k], recv_sem=recv_sems.at[a, k], device_id=to, device_id_type=MESH)

        sends = []
        for a in range(n):
            first = [copy(a, 0, me, sibling, src=ins[a])]
            first += [copy(a, 1 + j, me, (*chip, c), src=ins[a]) for j, chip in enumerate(chips)]
            for cp in first:
                cp.start()
            sends += first
        started = []
        for a in range(n):
            load = pltpu.make_async_copy(ins[a], bounce[a], local_sems.at[a, 0])
            load.start()
            load.wait()
            mine = pltpu.make_async_copy(bounce[a], outs[a].at[_dev(*me)], local_sems.at[a, 1])
            mine.start()
            started.append(mine)
        for j, chip in enumerate(chips):
            for a in range(n):
                copy(a, 1 + j, (*chip, c), me).wait_recv()
                passed = copy(a, 4 + j, (*chip, c), sibling)
                passed.start()
                sends.append(passed)
        for a in range(n):
            copy(a, 0, sibling, me).wait_recv()
            for j, chip in enumerate(chips):
                copy(a, 4 + j, (*chip, 1 - c), me).wait_recv()
        for cp in sends:
            cp.wait_send()
        for mine in started:
            mine.wait()

    any_spec = pl.BlockSpec(memory_space=pl.ANY)
    return pl.pallas_call(
        body, name=name,
        out_shape=[jax.ShapeDtypeStruct((N_DEV,) + a.shape, a.dtype) for a in arrs],
        in_specs=[any_spec] * n, out_specs=[any_spec] * n,
        scratch_shapes=[pltpu.SemaphoreType.DMA((n, 7)), pltpu.SemaphoreType.DMA((n, 7)), pltpu.SemaphoreType.DMA((n, 2))]
        + [pltpu.VMEM(a.shape, a.dtype) for a in arrs],
    )(*arrs)


_HBM = pl.BlockSpec(memory_space=pltpu.HBM)
_SEM = pl.BlockSpec(memory_space=pltpu.SEMAPHORE)
_SIDE_EFFECT = pltpu.SideEffectType.DATAFLOW_SIDE_EFFECTING


def _handshake(peers):
    barrier = pltpu.get_barrier_semaphore()
    for peer in peers:
        pl.semaphore_signal(barrier, inc=1, device_id=peer, device_id_type=MESH)
    pl.semaphore_wait(barrier, len(peers))


def _gather_start(buf, after, name, collective_id):
    def body(buf_ref, after_ref, send_sems, recv_sems, buf_thru, token):
        del after_ref, buf_thru
        x, y, c, chips = _position()
        rows = buf_ref.at[_dev(x, y, c)]
        targets = [(x, y, 1 - c)] + [(*chip, c) for chip in chips]
        _handshake(targets)
        for k, to in enumerate(targets):
            pltpu.make_async_remote_copy(src_ref=rows, dst_ref=rows, send_sem=send_sems.at[k], recv_sem=recv_sems.at[k],
                                         device_id=to, device_id_type=MESH).start()
        token[...] = jnp.zeros_like(token)

    return pl.pallas_call(
        body, name=name,
        out_shape=(pltpu.SemaphoreType.DMA((4,)), pltpu.SemaphoreType.DMA((4,)), pltpu.HBM(buf.shape, buf.dtype),
                   jax.ShapeDtypeStruct((8, 128), F32)),
        in_specs=(_HBM, pl.BlockSpec(memory_space=pl.ANY)),
        out_specs=(_SEM, _SEM, _HBM, pl.BlockSpec(memory_space=pltpu.VMEM)),
        input_output_aliases={0: 2},
        compiler_params=pltpu.CompilerParams(has_side_effects=_SIDE_EFFECT, collective_id=collective_id),
    )(pltpu.with_memory_space_constraint(buf, pltpu.HBM), after)


def _gather_forward(send_sems, recv_sems, buf, after, name, collective_id):
    def body(buf_ref, send_sems, recv_sems, after_ref, fwd_send, fwd_recv, buf_thru):
        del after_ref, buf_thru
        x, y, c, chips = _position()
        sibling = (x, y, 1 - c)

        def copy(block, k, sends, recvs):
            rows = buf_ref.at[_dev(*block)]
            return pltpu.make_async_remote_copy(src_ref=rows, dst_ref=rows, send_sem=sends.at[k], recv_sem=recvs.at[k],
                                                device_id=sibling, device_id_type=MESH)

        _handshake([sibling])
        for k in range(4):
            copy((x, y, c), k, send_sems, recv_sems).wait_send()
        copy(sibling, 0, send_sems, recv_sems).wait_recv()
        for j, chip in enumerate(chips):
            copy((*chip, c), 1 + j, send_sems, recv_sems).wait_recv()
            copy((*chip, c), j, fwd_send, fwd_recv).start()

    return pl.pallas_call(
        body, name=name,
        out_shape=(pltpu.SemaphoreType.DMA((3,)), pltpu.SemaphoreType.DMA((3,)), pltpu.HBM(buf.shape, buf.dtype)),
        in_specs=(_HBM, _SEM, _SEM, pl.BlockSpec(memory_space=pl.ANY)), out_specs=(_SEM, _SEM, _HBM),
        input_output_aliases={0: 2},
        compiler_params=pltpu.CompilerParams(has_side_effects=_SIDE_EFFECT, collective_id=collective_id),
    )(buf, send_sems, recv_sems, after)


def _gather_finish(fwd_send, fwd_recv, buf, name):
    def body(buf_ref, fwd_send, fwd_recv, buf_thru):
        del buf_thru
        x, y, c, chips = _position()
        for j, chip in enumerate(chips):
            cp = pltpu.make_async_remote_copy(
                src_ref=buf_ref.at[_dev(*chip, c)], dst_ref=buf_ref.at[_dev(*chip, 1 - c)], send_sem=fwd_send.at[j],
                recv_sem=fwd_recv.at[j], device_id=(x, y, 1 - c), device_id_type=MESH)
            cp.wait_send()
            cp.wait_recv()

    return pl.pallas_call(
        body, name=name,
        out_shape=pltpu.HBM(buf.shape, buf.dtype),
        in_specs=(_HBM, _SEM, _SEM), out_specs=_HBM,
        input_output_aliases={0: 0},
        compiler_params=pltpu.CompilerParams(has_side_effects=_SIDE_EFFECT),
    )(buf, fwd_send, fwd_recv)


def _everyone_else(x, y, c, chips):
    return [(x, y, 1 - c)] + [(*chip, core) for chip in chips for core in (c, 1 - c)]


def _broadcast_start(buf, name, collective_id):
    def body(buf_ref, send_sems, recv_sems, buf_thru, token):
        del buf_thru
        x, y, c, chips = _position()
        rows = buf_ref.at[_dev(x, y, c)]
        _handshake(_everyone_else(x, y, c, chips))
        for k, to in enumerate(_everyone_else(x, y, c, chips)):
            pltpu.make_async_remote_copy(src_ref=rows, dst_ref=rows, send_sem=send_sems.at[k], recv_sem=recv_sems.at[k],
                                         device_id=to, device_id_type=MESH).start()
        token[...] = jnp.zeros_like(token)

    return pl.pallas_call(
        body, name=name,
        out_shape=(pltpu.SemaphoreType.DMA((7,)), pltpu.SemaphoreType.DMA((7,)), pltpu.HBM(buf.shape, buf.dtype),
                   jax.ShapeDtypeStruct((8, 128), F32)),
        in_specs=(_HBM,), out_specs=(_SEM, _SEM, _HBM, pl.BlockSpec(memory_space=pltpu.VMEM)),
        input_output_aliases={0: 2},
        compiler_params=pltpu.CompilerParams(has_side_effects=_SIDE_EFFECT, collective_id=collective_id),
    )(pltpu.with_memory_space_constraint(buf, pltpu.HBM))


def _broadcast_wait(send_sems, recv_sems, buf, after, name):
    def body(buf_ref, send_sems, recv_sems, after_ref, buf_thru):
        del after_ref, buf_thru
        x, y, c, chips = _position()
        for k, peer in enumerate(_everyone_else(x, y, c, chips)):
            cp = pltpu.make_async_remote_copy(
                src_ref=buf_ref.at[_dev(x, y, c)], dst_ref=buf_ref.at[_dev(*peer)], send_sem=send_sems.at[k],
                recv_sem=recv_sems.at[k], device_id=peer, device_id_type=MESH)
            cp.wait_send()
            cp.wait_recv()

    return pl.pallas_call(
        body, name=name,
        out_shape=pltpu.HBM(buf.shape, buf.dtype),
        in_specs=(_HBM, _SEM, _SEM, pl.BlockSpec(memory_space=pl.ANY)), out_specs=_HBM,
        input_output_aliases={0: 0},
        compiler_params=pltpu.CompilerParams(has_side_effects=_SIDE_EFFECT),
    )(buf, send_sems, recv_sems, after)


def _to_sibling(j, x, y, c, chips):
    return _dev(*([(x, y)] + chips)[j], 1 - c), (x, y, 1 - c)


def _to_chip(j, x, y, c, chips):
    return j, (*chips[j], c)


def _exchange_start(srcs, n_slots, route, name, collective_id):
    n = len(srcs)

    def body(*refs):
        s_refs, land_refs = refs[:n], refs[n:2 * n]
        send_sems, recv_sems = refs[2 * n:2 * n + 2]
        token = refs[-1]
        x, y, c, chips = _position()
        _handshake([(x, y, 1 - c)] if route is _to_sibling else [route(j, x, y, c, chips)[1] for j in range(n_slots)])
        for k in range(n):
            for j in range(n_slots):
                block, to = route(j, x, y, c, chips)
                pltpu.make_async_remote_copy(
                    src_ref=s_refs[k].at[block], dst_ref=land_refs[k].at[j], send_sem=send_sems.at[n_slots * k + j],
                    recv_sem=recv_sems.at[n_slots * k + j], device_id=to, device_id_type=MESH).start()
        token[...] = jnp.zeros_like(token)

    lands = [jax.ShapeDtypeStruct((n_slots,) + s.shape[1:], s.dtype) for s in srcs]
    outs = pl.pallas_call(
        body, name=name,
        out_shape=(pltpu.SemaphoreType.DMA((n_slots * n,)), pltpu.SemaphoreType.DMA((n_slots * n,)),
                   *[pltpu.HBM(s.shape, s.dtype) for s in srcs], *[pltpu.HBM(l.shape, l.dtype) for l in lands],
                   jax.ShapeDtypeStruct((8, 128), F32)),
        in_specs=[_HBM] * (2 * n), out_specs=(_SEM, _SEM, *[_HBM] * (2 * n), pl.BlockSpec(memory_space=pltpu.VMEM)),
        input_output_aliases={k: 2 + k for k in range(2 * n)},
        compiler_params=pltpu.CompilerParams(has_side_effects=_SIDE_EFFECT, collective_id=collective_id),
    )(*[pltpu.with_memory_space_constraint(s, pltpu.HBM) for s in srcs],
      *[pltpu.with_memory_space_constraint(lax.empty(l.shape, l.dtype), pltpu.HBM) for l in lands])
    return outs[0], outs[1], outs[2:2 + n], outs[2 + n:2 + 2 * n], outs[-1]


def _exchange_wait(send_sems, recv_sems, s_thru, land_thru, after, n_slots, route, name):
    n = len(s_thru)

    def body(*refs):
        s_refs, land_refs = refs[:n], refs[n:2 * n]
        send_sems, recv_sems = refs[2 * n:2 * n + 2]
        x, y, c, chips = _position()
        for k in range(n):
            for j in range(n_slots):
                block, to = route(j, x, y, c, chips)
                cp = pltpu.make_async_remote_copy(
                    src_ref=s_refs[k].at[block], dst_ref=land_refs[k].at[j], send_sem=send_sems.at[n_slots * k + j],
                    recv_sem=recv_sems.at[n_slots * k + j], device_id=to, device_id_type=MESH)
                cp.wait_send()
                cp.wait_recv()

    outs = pl.pallas_call(
        body, name=name,
        out_shape=(*[pltpu.HBM(s.shape, s.dtype) for s in s_thru], *[pltpu.HBM(l.shape, l.dtype) for l in land_thru]),
        in_specs=[_HBM] * (2 * n) + [_SEM, _SEM, pl.BlockSpec(memory_space=pl.ANY)], out_specs=[_HBM] * (2 * n),
        input_output_aliases={k: k for k in range(2 * n)},
        compiler_params=pltpu.CompilerParams(has_side_effects=_SIDE_EFFECT),
    )(*s_thru, *land_thru, send_sems, recv_sems, after)
    return outs[:n], outs[n:]


def _owner_table():
    x, y, c = lax.axis_index("x"), lax.axis_index("y"), lax.axis_index("c")
    chips = [(x, y), (1 - x, y), (x, 1 - y), (1 - x, 1 - y)]
    return jnp.stack([_dev(px, py, c) for px, py in chips]).astype(jnp.int32)


def _chip_partial_sums(table, parts, from_sibling, name):
    n = len(parts)

    def body(tab_ref, *refs):
        del tab_ref
        for g_ref, l_ref, out_ref in zip(refs[:n], refs[n:2 * n], refs[2 * n:]):
            out_ref[...] = (g_ref[...].astype(F32) + l_ref[...].astype(F32)).astype(out_ref.dtype)

    block = lambda p: (None,) + p.shape[1:]
    grid_spec = pltpu.PrefetchScalarGridSpec(
        num_scalar_prefetch=1, grid=(3,),
        in_specs=[pl.BlockSpec(block(p), lambda j, tab: (tab[j + 1], 0, 0)) for p in parts]
        + [pl.BlockSpec(block(p), lambda j, tab: (j + 1, 0, 0)) for p in parts],
        out_specs=[pl.BlockSpec(block(p), lambda j, tab: (j, 0, 0)) for p in parts])
    return pl.pallas_call(
        body, name=name, grid_spec=grid_spec,
        out_shape=[jax.ShapeDtypeStruct((3,) + p.shape[1:], BF16) for p in parts],
        compiler_params=_params(("arbitrary",)),
    )(table, *parts, *from_sibling)


def _final_update(table, parts, from_sibling, from_chips, states, name):
    n = len(parts)
    updated = [k for k in range(n) if states[k] is not None]

    def body(tab_ref, *refs):
        del tab_ref
        ins, outs = refs[:3 * n + 3 * len(updated)], list(refs[3 * n + 3 * len(updated):])
        wmv = list(ins[3 * n:])
        for k in range(n):
            acc = ins[k][...].astype(F32) + ins[n + k][...].astype(F32)
            for j in range(3):
                acc = acc + ins[2 * n + k][j].astype(F32)
            outs.pop(0)[...] = acc
            if k in updated:
                w_ref, m_ref, v_ref = wmv[:3]
                del wmv[:3]
                for out_ref, val in zip(outs[:3], _adamw_update(w_ref[...], acc, m_ref[...], v_ref[...])):
                    out_ref[...] = val
                del outs[:3]

    half = lambda p: (p.shape[1] // 2, p.shape[2])
    rows = lambda p: pl.BlockSpec(half(p), lambda t, tab: (t, 0))
    grid_spec = pltpu.PrefetchScalarGridSpec(
        num_scalar_prefetch=1, grid=(2,),
        in_specs=[pl.BlockSpec((None,) + half(p), lambda t, tab: (tab[0], t, 0)) for p in parts]
        + [pl.BlockSpec((None,) + half(p), lambda t, tab: (0, t, 0)) for p in parts]
        + [pl.BlockSpec((3,) + half(p), lambda t, tab: (0, t, 0)) for p in parts]
        + [rows(parts[k]) for k in updated for _ in range(3)],
        out_specs=[rows(parts[k]) for k in range(n) for _ in range(4 if k in updated else 1)])
    outs = pl.pallas_call(
        body, name=name, grid_spec=grid_spec,
        out_shape=[jax.ShapeDtypeStruct(parts[k].shape[1:], F32) for k in range(n) for _ in range(4 if k in updated else 1)],
        compiler_params=_params(("arbitrary",)),
    )(table, *parts, *from_sibling, *from_chips, *[t for k in updated for t in states[k]])
    result = []
    for k in range(n):
        count = 4 if k in updated else 1
        result.append(outs[:count])
        outs = outs[count:]
    return result


def _sum_blocks(g8):
    _, rows, cols = g8.shape

    def body(g_ref, out_ref):
        acc = g_ref[0]
        for d in range(1, N_DEV):
            acc = acc + g_ref[d]
        out_ref[...] = acc

    return pl.pallas_call(
        body, name="small_grad_sum", grid=(1,),
        in_specs=[_full((N_DEV, rows, cols))], out_specs=_full((rows, cols)),
        out_shape=jax.ShapeDtypeStruct((rows, cols), F32),
        compiler_params=_params(("arbitrary",)),
    )(g8)


def _fwd_mix(x2d, gw, g_mix, conv_w, conv_b, ln_g, ln_b, pool_w, pool_scale, after, seq, tm):
    tokens = x2d.shape[0]
    n_tiles = tokens // tm
    tps = seq // tm

    def body(x_ref, gmix_ref, gw_hbm, cw_ref, cb_ref, lng_ref, lnb_ref, pw_ref, ps_ref, after_ref,
             x1_ref, u_ref, c_ref, pooled_ref, ymix_ref, h1_ref,
             win_v, wout_v, hc_carry, up_carry, sem):
        del after_ref
        i = pl.program_id(0)

        _start_weights(gw_hbm, ("w_in", "w_out"), (win_v, wout_v), sem)

        @pl.when(i % tps == 0)
        def _():
            hc_carry[...] = jnp.zeros_like(hc_carry)
            up_carry[...] = jnp.zeros_like(up_carry)

        x = x_ref[...]
        xh, _ = _rms_fwd(x)
        h1 = (xh * gmix_ref[...]).astype(BF16)
        h1_ref[...] = h1
        u = _dot_nt(h1, win_v[...])
        u_ref[...] = u
        val, gate, up = u[:, :D_CONV], u[:, D_CONV:2 * D_CONV], u[:, 2 * D_CONV:]

        extp = jnp.concatenate([up_carry[...], up], axis=0)
        up_carry[...] = up[tm - POOL_HALO:, :]
        pos = lax.broadcasted_iota(jnp.int32, (tm, 1), 0) + (i % tps) * tm
        run = extp
        mixed = []
        for g, w in enumerate(POOL_WINDOWS):
            lo = g * POOL_GROUP_DIM
            run = run[:, POOL_GROUP_DIM if g else 0:]
            run = run + pltpu.roll(run, w // 2, 0)
            cnt = jnp.minimum(pos + 1, w).astype(F32)
            pooled = run[POOL_HALO:, :POOL_GROUP_DIM] / cnt - up[:, lo:lo + POOL_GROUP_DIM]
            pooled = pooled.astype(BF16)
            pooled_ref[:, lo:lo + POOL_GROUP_DIM] = pooled
            mixed.append(_dot(pooled, pw_ref[g].astype(BF16)))
        y_pool = jnp.concatenate(mixed, axis=-1) * ps_ref[...]
        y_pool = y_pool.astype(BF16)
        ymix_ref[:, D_CONV:] = y_pool
        out = _dot(y_pool, wout_v[D_CONV:, :])

        hc = val * _sigmoid(gate)
        ext = jnp.concatenate([hc_carry[...], hc], axis=0)
        hc_carry[...] = hc[tm - CONV_HALO:, :]
        conv = jnp.broadcast_to(cb_ref[...], (tm, D_CONV))
        ahead_by = _sublane_shifts(ext)
        for k in range(CONV_WIDTH):
            whole, part = divmod(CONV_HALO - (CONV_WIDTH - 1) + k, 8)
            conv = conv + cw_ref[k:k + 1, :] * ahead_by[part][8 * whole:8 * whole + tm, :]
        c_ref[...] = conv
        mu = jnp.mean(conv, axis=-1, keepdims=True)
        cen = conv - mu
        ln = cen * lax.rsqrt(jnp.mean(cen * cen, axis=-1, keepdims=True) + EPS) * lng_ref[...] + lnb_ref[...]
        y_conv = ln * _sigmoid(ln)
        y_conv = y_conv.astype(BF16)
        ymix_ref[:, :D_CONV] = y_conv
        x1_ref[...] = x + (out + _dot(y_conv, wout_v[:D_CONV, :]))

    row = lambda w: pl.BlockSpec((tm, w), lambda i: (i, 0))
    return pl.pallas_call(
        body, name="fwd_mix", grid=(n_tiles,),
        in_specs=[row(D_MODEL), _full((1, D_MODEL)), pl.BlockSpec(memory_space=pl.ANY),
                  _full((CONV_WIDTH, D_CONV)), _full((1, D_CONV)), _full((1, D_CONV)), _full((1, D_CONV)),
                  _full((4, POOL_GROUP_DIM, POOL_GROUP_DIM)), _full((1, D_POOL)), _full(after.shape)],
        out_specs=[row(D_MODEL), row(D_IN), row(D_CONV), row(D_POOL), row(D_MODEL), row(D_MODEL)],
        out_shape=[jax.ShapeDtypeStruct((tokens, D_MODEL), F32), jax.ShapeDtypeStruct((tokens, D_IN), F32),
                   jax.ShapeDtypeStruct((tokens, D_CONV), F32), jax.ShapeDtypeStruct((tokens, D_POOL), BF16),
                   jax.ShapeDtypeStruct((tokens, D_MODEL), BF16), jax.ShapeDtypeStruct((tokens, D_MODEL), BF16)],
        scratch_shapes=[pltpu.VMEM((D_IN, D_MODEL), BF16), pltpu.VMEM((D_MODEL, D_MODEL), BF16),
                        pltpu.VMEM((CONV_HALO, D_CONV), F32), pltpu.VMEM((POOL_HALO, D_POOL), F32),
                        pltpu.SemaphoreType.DMA((2,))],
        compiler_params=_params(),
    )(x2d, g_mix, gw, conv_w, conv_b, ln_g, ln_b, pool_w, pool_scale, after)


def _fwd_kv(mem2d, gw, g_mem):
    rows = mem2d.shape[0]
    n_b = rows // N_MEM

    def body(mem_ref, g_ref, gw_hbm, mn_ref, kv_ref, wkv_v, sem):
        @pl.when(pl.program_id(0) == 0)
        def _():
            copies = _load_weight(gw_hbm, "w_kv", wkv_v, sem)
            for cp in copies:
                cp.start()
            for cp in copies:
                cp.wait()

        mh, _ = _rms_fwd(mem_ref[...])
        mn = (mh * g_ref[...]).astype(BF16)
        mn_ref[...] = mn
        kv_ref[...] = _dot_nt(mn, wkv_v[...]).astype(BF16)

    return pl.pallas_call(
        body, name="fwd_kv", grid=(n_b,),
        in_specs=[pl.BlockSpec((N_MEM, D_MODEL), lambda b: (b, 0)), _full((1, D_MODEL)), pl.BlockSpec(memory_space=pl.ANY)],
        out_specs=[pl.BlockSpec((N_MEM, D_MODEL), lambda b: (b, 0)), pl.BlockSpec((N_MEM, 2 * D_MODEL), lambda b: (b, 0))],
        out_shape=[jax.ShapeDtypeStruct((rows, D_MODEL), BF16), jax.ShapeDtypeStruct((rows, 2 * D_MODEL), BF16)],
        scratch_shapes=[pltpu.VMEM((2 * D_MODEL, D_MODEL), BF16), pltpu.SemaphoreType.DMA],
        compiler_params=_params(),
    )(mem2d, g_mem, gw)


def _softmax_rows(s):
    e = jnp.exp(s - jnp.max(s, axis=-1, keepdims=True))
    return e / jnp.sum(e, axis=-1, keepdims=True)


def _fwd_attn(x1, kv, gw, g_x, seq, tm):
    tokens = x1.shape[0]
    n_tiles = tokens // tm
    tps = seq // tm

    def body(x1_ref, kv_ref, g_ref, gw_hbm, x2_ref, h2_ref, q_ref, o_ref, wq_v, wo_v, sem):
        _start_weights(gw_hbm, ("w_q", "w_o"), (wq_v, wo_v), sem)
        x1v = x1_ref[...]
        xh, _ = _rms_fwd(x1v)
        h2 = (xh * g_ref[...]).astype(BF16)
        h2_ref[...] = h2
        q = (_dot(h2, wq_v[...]) * (HEAD_DIM ** -0.5)).astype(BF16)
        q_ref[...] = q
        heads = [slice(h * HEAD_DIM, (h + 1) * HEAD_DIM) for h in range(HEADS)]
        scores = [_dot_nt(q[:, hd], kv_ref[:, hd]) for hd in heads]
        probs = [_softmax_rows(s).astype(BF16) for s in scores]
        outs = [_dot(p, kv_ref[:, pl.ds(D_MODEL + h * HEAD_DIM, HEAD_DIM)]) for h, p in enumerate(probs)]
        o = jnp.concatenate(outs, axis=-1).astype(BF16)
        o_ref[...] = o
        x2_ref[...] = x1v + _dot(o, wo_v[...])

    row = lambda w: pl.BlockSpec((tm, w), lambda i: (i, 0))
    return pl.pallas_call(
        body, name="fwd_attn", grid=(n_tiles,),
        in_specs=[row(D_MODEL), pl.BlockSpec((N_MEM, 2 * D_MODEL), lambda i: (i // tps, 0)), _full((1, D_MODEL)),
                  pl.BlockSpec(memory_space=pl.ANY)],
        out_specs=[row(D_MODEL)] * 4,
        out_shape=[jax.ShapeDtypeStruct((tokens, D_MODEL), F32)] + [jax.ShapeDtypeStruct((tokens, D_MODEL), BF16)] * 3,
        scratch_shapes=[pltpu.VMEM((D_MODEL, D_MODEL), BF16), pltpu.VMEM((D_MODEL, D_MODEL), BF16), pltpu.SemaphoreType.DMA((2,))],
        compiler_params=_params(),
    )(x1, kv, g_x, gw)


def _ffn_conv(uu, halo, w_ref, b_ref, cols):
    ext = jnp.concatenate([halo, uu], axis=0)
    p1 = pltpu.roll(ext, 1, 0)[FFN_HALO:, :]
    p2 = pltpu.roll(ext, 2, 0)[FFN_HALO:, :]
    return b_ref[:, cols] + w_ref[2:3, cols] * uu + w_ref[1:2, cols] * p1 + w_ref[0:1, cols] * p2


def _fwd_ffn(x2, target, gw, g_ffn, ffn_w, ffn_b, g_final, seq, tm):
    tokens = x2.shape[0]
    n_tiles = tokens // tm
    tps = seq // tm
    n_chunks = D_FF // FFN_CHUNK

    def body(x2_ref, tgt_ref, gffn_ref, gw_hbm, fw_ref, fb_ref, gfin_ref,
             uu_ref, cc_ref, a_ref, h3_ref, dx3_ref, dx3b_ref, loss_ref, dgfin_ref,
             wup_v, wdown_v, carry, sem):
        i = pl.program_id(0)

        _start_weights(gw_hbm, ("w_up", "w_down"), (wup_v, wdown_v), sem)

        @pl.when(i == 0)
        def _():
            loss_ref[...] = jnp.zeros_like(loss_ref)
            dgfin_ref[...] = jnp.zeros_like(dgfin_ref)

        @pl.when(i % tps == 0)
        def _():
            carry[...] = jnp.zeros_like(carry)

        x2v = x2_ref[...]
        xh, _ = _rms_fwd(x2v)
        h3 = (xh * gffn_ref[...]).astype(BF16)
        h3_ref[...] = h3
        acc = jnp.zeros((tm, D_MODEL), F32)
        for jc in range(n_chunks):
            halves = []
            for half in range(2):
                cols = pl.ds(half * D_FF + jc * FFN_CHUNK, FFN_CHUNK)
                uu = _dot_nt(h3, wup_v[cols, :])
                uu_ref[:, cols] = uu.astype(BF16)
                cc = _ffn_conv(uu, carry[:, cols], fw_ref, fb_ref, cols)
                cc_ref[:, cols] = cc.astype(BF16)
                halves.append(cc)
                carry[:, cols] = uu[tm - FFN_HALO:, :]
            gate, val = halves
            a = (gate * _sigmoid(gate) * val).astype(BF16)
            a_ref[:, pl.ds(jc * FFN_CHUNK, FFN_CHUNK)] = a
            acc = acc + _dot(a, wdown_v[pl.ds(jc * FFN_CHUNK, FFN_CHUNK), :])
        x3 = x2v + acc

        xh3, r3 = _rms_fwd(x3)
        gfin = gfin_ref[...]
        err = xh3 * gfin - tgt_ref[...]
        loss_ref[...] += jnp.full(loss_ref.shape, jnp.sum(err * err) * (0.5 / D_MODEL), F32)
        dy = err * (1.0 / D_MODEL)
        dgfin_ref[...] += _colsum(dy * xh3)
        dx3 = _rms_bwd(dy, xh3, r3, gfin)
        dx3_ref[...] = dx3
        dx3b_ref[...] = dx3.astype(BF16)

    row = lambda w: pl.BlockSpec((tm, w), lambda i: (i, 0))
    return pl.pallas_call(
        body, name="fwd_ffn", grid=(n_tiles,),
        in_specs=[row(D_MODEL), row(D_MODEL), _full((1, D_MODEL)), pl.BlockSpec(memory_space=pl.ANY),
                  _full((FFN_CONV_WIDTH, 2 * D_FF)), _full((1, 2 * D_FF)), _full((1, D_MODEL))],
        out_specs=[row(2 * D_FF), row(2 * D_FF), row(D_FF), row(D_MODEL), row(D_MODEL), row(D_MODEL), _full((8, 128)),
                   _full((1, D_MODEL))],
        out_shape=[jax.ShapeDtypeStruct((tokens, 2 * D_FF), BF16), jax.ShapeDtypeStruct((tokens, 2 * D_FF), BF16),
                   jax.ShapeDtypeStruct((tokens, D_FF), BF16),
                   jax.ShapeDtypeStruct((tokens, D_MODEL), BF16), jax.ShapeDtypeStruct((tokens, D_MODEL), F32),
                   jax.ShapeDtypeStruct((tokens, D_MODEL), BF16),
                   jax.ShapeDtypeStruct((8, 128), F32), jax.ShapeDtypeStruct((1, D_MODEL), F32)],
        scratch_shapes=[pltpu.VMEM((2 * D_FF, D_MODEL), BF16), pltpu.VMEM((D_FF, D_MODEL), BF16),
                        pltpu.VMEM((FFN_HALO, 2 * D_FF), F32), pltpu.SemaphoreType.DMA((2,))],
        compiler_params=_params(),
    )(x2, target, g_ffn, gw, ffn_w, ffn_b, g_final)


def _bwd_ffn(dx3, x2, uu_all, cc_all, gw, g_ffn, ffn_w, seq, tm):
    tokens = x2.shape[0]
    n_tiles = tokens // tm
    tps = seq // tm
    n_chunks = D_FF // FFN_CHUNK

    def body(dx3_ref, x2_ref, uu_ref, cc_ref, gffn_ref, gw_hbm, fw_ref,
             dx2_ref, dx2b_ref, duu_ref, dfb_ref, dfw_ref, dg_ref,
             wup_v, wdown_v, carry, sem):
        i = pl.program_id(0)
        t = n_tiles - 1 - i

        _start_weights(gw_hbm, ("w_down", "w_up"), (wdown_v, wup_v), sem)

        @pl.when(i == 0)
        def _():
            dfb_ref[...] = jnp.zeros_like(dfb_ref)
            dfw_ref[...] = jnp.zeros_like(dfw_ref)
            dg_ref[...] = jnp.zeros_like(dg_ref)

        @pl.when(t % tps == tps - 1)
        def _():
            carry[...] = jnp.zeros_like(carry)

        dx3v = dx3_ref[...]
        dx3b = dx3v.astype(BF16)
        dh3 = jnp.zeros((tm, D_MODEL), F32)
        for jc in range(n_chunks):
            da = _dot_nt(dx3b, wdown_v[pl.ds(jc * FFN_CHUNK, FFN_CHUNK), :])
            colss = [pl.ds(half * D_FF + jc * FFN_CHUNK, FFN_CHUNK) for half in range(2)]
            gate, val = [cc_ref[:, cols].astype(F32) for cols in colss]
            sg = _sigmoid(gate)
            dgate = da * val * (sg * (1.0 + gate * (1.0 - sg)))
            dval = da * (gate * sg)
            for dcc, cols in zip((dgate, dval), colss):
                uu = uu_ref[:, cols].astype(F32)
                dfb_ref[:, cols] += _colsum(dcc)
                ext = jnp.concatenate([dcc, carry[:, cols]], axis=0)
                carry[:, cols] = dcc[:FFN_HALO, :]
                n1 = pltpu.roll(ext, tm + FFN_HALO - 1, 0)[:tm, :]
                n2 = pltpu.roll(ext, tm + FFN_HALO - 2, 0)[:tm, :]
                duu = fw_ref[2:3, cols] * dcc + fw_ref[1:2, cols] * n1 + fw_ref[0:1, cols] * n2
                dfw_ref[2:3, cols] += _colsum(uu * dcc)
                dfw_ref[1:2, cols] += _colsum(uu * n1)
                dfw_ref[0:1, cols] += _colsum(uu * n2)
                duub = duu.astype(BF16)
                duu_ref[:, cols] = duub
                dh3 = dh3 + _dot(duub, wup_v[cols, :])
        xh, r = _rms_fwd(x2_ref[...])
        dg_ref[...] += _colsum(dh3 * xh)
        dx2 = dx3v + _rms_bwd(dh3, xh, r, gffn_ref[...])
        dx2_ref[...] = dx2
        dx2b_ref[...] = dx2.astype(BF16)

    rev = lambda w: pl.BlockSpec((tm, w), lambda i: (n_tiles - 1 - i, 0))
    return pl.pallas_call(
        body, name="bwd_ffn", grid=(n_tiles,),
        in_specs=[rev(D_MODEL), rev(D_MODEL), rev(2 * D_FF), rev(2 * D_FF), _full((1, D_MODEL)),
                  pl.BlockSpec(memory_space=pl.ANY), _full((FFN_CONV_WIDTH, 2 * D_FF))],
        out_specs=[rev(D_MODEL), rev(D_MODEL), rev(2 * D_FF), _full((1, 2 * D_FF)), _full((FFN_CONV_WIDTH, 2 * D_FF)),
                   _full((1, D_MODEL))],
        out_shape=[jax.ShapeDtypeStruct((tokens, D_MODEL), F32), jax.ShapeDtypeStruct((tokens, D_MODEL), BF16),
                   jax.ShapeDtypeStruct((tokens, 2 * D_FF), BF16),
                   jax.ShapeDtypeStruct((1, 2 * D_FF), F32), jax.ShapeDtypeStruct((FFN_CONV_WIDTH, 2 * D_FF), F32),
                   jax.ShapeDtypeStruct((1, D_MODEL), F32)],
        scratch_shapes=[pltpu.VMEM((2 * D_FF, D_MODEL), BF16), pltpu.VMEM((D_FF, D_MODEL), BF16),
                        pltpu.VMEM((FFN_HALO, 2 * D_FF), F32), pltpu.SemaphoreType.DMA((2,))],
        compiler_params=_params(),
    )(dx3, x2, uu_all, cc_all, g_ffn, gw, ffn_w)


def _bwd_attn(dx2, x1, q, kv, gw, g_x, after, seq, tm):
    tokens = x1.shape[0]
    n_tiles = tokens // tm
    tps = seq // tm
    n_b = tokens // seq

    def body(dx2_ref, x1_ref, q_ref, kv_ref, g_ref, gw_hbm, after_ref, dx1_ref, dx1b_ref, dq_ref, dkv_ref, dg_ref,
             wq_v, wo_v, sem):
        del after_ref
        i = pl.program_id(0)

        _start_weights(gw_hbm, ("w_o", "w_q"), (wo_v, wq_v), sem)

        @pl.when(i == 0)
        def _():
            dg_ref[...] = jnp.zeros_like(dg_ref)

        @pl.when(i % tps == 0)
        def _():
            dkv_ref[...] = jnp.zeros_like(dkv_ref)

        dx2v = dx2_ref[...]
        do = _dot_nt(dx2v.astype(BF16), wo_v[...]).astype(BF16)
        q = q_ref[...]
        heads = [slice(h * HEAD_DIM, (h + 1) * HEAD_DIM) for h in range(HEADS)]
        kcols = [pl.ds(h * HEAD_DIM, HEAD_DIM) for h in range(HEADS)]
        vcols = [pl.ds(D_MODEL + h * HEAD_DIM, HEAD_DIM) for h in range(HEADS)]
        scores = [_dot_nt(q[:, hd], kv_ref[:, kc]) for hd, kc in zip(heads, kcols)]
        dps = [_dot_nt(do[:, hd], kv_ref[:, vc]) for hd, vc in zip(heads, vcols)]
        probs = [_softmax_rows(s) for s in scores]
        dss = [(p * (dp - jnp.sum(dp * p, axis=-1, keepdims=True))).astype(BF16) for p, dp in zip(probs, dps)]
        for p, hd, vc in zip(probs, heads, vcols):
            dkv_ref[:, vc] += _dot_tn(p.astype(BF16), do[:, hd])
        dqs = [_dot(ds, kv_ref[:, kc]) * (HEAD_DIM ** -0.5) for ds, kc in zip(dss, kcols)]
        for ds, hd, kc in zip(dss, heads, kcols):
            dkv_ref[:, kc] += _dot_tn(ds, q[:, hd])
        dq = jnp.concatenate(dqs, axis=-1).astype(BF16)
        dq_ref[...] = dq
        dh2 = _dot_nt(dq, wq_v[...])
        xh, r = _rms_fwd(x1_ref[...])
        dg_ref[...] += _colsum(dh2 * xh)
        dx1 = dx2v + _rms_bwd(dh2, xh, r, g_ref[...])
        dx1_ref[...] = dx1
        dx1b_ref[...] = dx1.astype(BF16)

    row = lambda w: pl.BlockSpec((tm, w), lambda i: (i, 0))
    per_b = pl.BlockSpec((N_MEM, 2 * D_MODEL), lambda i: (i // tps, 0))
    return pl.pallas_call(
        body, name="bwd_attn", grid=(n_tiles,),
        in_specs=[row(D_MODEL), row(D_MODEL), row(D_MODEL), per_b, _full((1, D_MODEL)), pl.BlockSpec(memory_space=pl.ANY),
                  _full(after.shape)],
        out_specs=[row(D_MODEL), row(D_MODEL), row(D_MODEL), per_b, _full((1, D_MODEL))],
        out_shape=[jax.ShapeDtypeStruct((tokens, D_MODEL), F32), jax.ShapeDtypeStruct((tokens, D_MODEL), BF16),
                   jax.ShapeDtypeStruct((tokens, D_MODEL), BF16),
                   jax.ShapeDtypeStruct((n_b * N_MEM, 2 * D_MODEL), F32), jax.ShapeDtypeStruct((1, D_MODEL), F32)],
        scratch_shapes=[pltpu.VMEM((D_MODEL, D_MODEL), BF16), pltpu.VMEM((D_MODEL, D_MODEL), BF16), pltpu.SemaphoreType.DMA((2,))],
        compiler_params=_params(),
    )(dx2, x1, q, kv, g_x, gw, after)


def _bwd_kv(dkv, mem2d, gw):
    rows = mem2d.shape[0]
    n_b = rows // N_MEM

    def body(dkv_ref, mem_ref, gw_hbm, dkvb_ref, dg_ref, wkv_v, sem):
        @pl.when(pl.program_id(0) == 0)
        def _():
            copies = _load_weight(gw_hbm, "w_kv", wkv_v, sem)
            for cp in copies:
                cp.start()
            for cp in copies:
                cp.wait()
            dg_ref[...] = jnp.zeros_like(dg_ref)

        dkvb = dkv_ref[...].astype(BF16)
        dkvb_ref[...] = dkvb
        dmn = _dot(dkvb, wkv_v[...])
        mh, _ = _rms_fwd(mem_ref[...])
        dg_ref[...] += _colsum(dmn * mh)

    return pl.pallas_call(
        body, name="bwd_kv", grid=(n_b,),
        in_specs=[pl.BlockSpec((N_MEM, 2 * D_MODEL), lambda b: (b, 0)), pl.BlockSpec((N_MEM, D_MODEL), lambda b: (b, 0)),
                  pl.BlockSpec(memory_space=pl.ANY)],
        out_specs=[pl.BlockSpec((N_MEM, 2 * D_MODEL), lambda b: (b, 0)), _full((1, D_MODEL))],
        out_shape=[jax.ShapeDtypeStruct((rows, 2 * D_MODEL), BF16), jax.ShapeDtypeStruct((1, D_MODEL), F32)],
        scratch_shapes=[pltpu.VMEM((2 * D_MODEL, D_MODEL), BF16), pltpu.SemaphoreType.DMA],
        compiler_params=_params(),
    )(dkv, mem2d, gw)


def _bwd_mix(dx1, x2d, u_all, c_all, pooled_all, gw, g_mix, conv_w, ln_g, ln_b, pool_w, pool_scale, after, seq, tm):
    tokens = x2d.shape[0]
    n_tiles = tokens // tm
    tps = seq // tm

    def body(dx1_ref, x_ref, u_ref, c_ref, pooled_ref, gmix_ref, gw_hbm, cw_ref, lng_ref, lnb_ref, pw_ref, ps_ref,
             after_ref, dx_ref, du_ref, dgmix_ref, dcw_ref, dcb_ref, dlng_ref, dlnb_ref, dpw_ref, dps_ref,
             win_v, wout_v, dc_carry, e_carry, sem):
        del after_ref
        i = pl.program_id(0)
        t = n_tiles - 1 - i

        _start_weights(gw_hbm, ("w_out", "w_in"), (wout_v, win_v), sem)

        @pl.when(i == 0)
        def _():
            for ref in (dgmix_ref, dcw_ref, dcb_ref, dlng_ref, dlnb_ref, dpw_ref, dps_ref):
                ref[...] = jnp.zeros_like(ref)

        @pl.when(t % tps == tps - 1)
        def _():
            dc_carry[...] = jnp.zeros_like(dc_carry)
            e_carry[...] = jnp.zeros_like(e_carry)

        dx1v = dx1_ref[...]
        dymix = _dot_nt(dx1v.astype(BF16), wout_v[...])
        dyc, dyp = dymix[:, :D_CONV], dymix[:, D_CONV:]
        u = u_ref[...]
        val, gate = u[:, :D_CONV], u[:, D_CONV:2 * D_CONV]

        conv = c_ref[...]
        mu = jnp.mean(conv, axis=-1, keepdims=True)
        cen = conv - mu
        rs = lax.rsqrt(jnp.mean(cen * cen, axis=-1, keepdims=True) + EPS)
        chat = cen * rs
        ln = chat * lng_ref[...] + lnb_ref[...]
        sl = _sigmoid(ln)
        dln = dyc * (sl * (1.0 + ln * (1.0 - sl)))
        dlng_ref[...] += _colsum(dln * chat)
        dlnb_ref[...] += _colsum(dln)
        dchat = dln * lng_ref[...]
        dc = rs * (dchat - jnp.mean(dchat, axis=-1, keepdims=True)
                   - chat * jnp.mean(dchat * chat, axis=-1, keepdims=True))
        dcb_ref[...] += _colsum(dc)
        sg = _sigmoid(gate)
        hc = val * sg
        ext = jnp.concatenate([dc, dc_carry[...]], axis=0)
        dc_carry[...] = dc[:CONV_HALO, :]
        dhc = jnp.zeros((tm, D_CONV), F32)
        ahead_by = _sublane_shifts(ext)
        for k in range(CONV_WIDTH):
            whole, part = divmod(CONV_WIDTH - 1 - k, 8)
            tap = ahead_by[part][8 * whole:8 * whole + tm, :]
            dhc = dhc + cw_ref[k:k + 1, :] * tap
            dcw_ref[k:k + 1, :] += _colsum_mxu(hc * tap)
        du_ref[:, :D_CONV] = (dhc * sg).astype(BF16)
        du_ref[:, D_CONV:2 * D_CONV] = (dhc * val * (sg * (1.0 - sg))).astype(BF16)

        pos = lax.broadcasted_iota(jnp.int32, (tm, 1), 0) + (t % tps) * tm
        es, dpooled = [], []
        for g, w in enumerate(POOL_WINDOWS):
            cols = pl.ds(g * POOL_GROUP_DIM, POOL_GROUP_DIM)
            lo = g * POOL_GROUP_DIM
            pooled = pooled_ref[:, cols]
            pw = pw_ref[g].astype(BF16)
            dyg = dyp[:, lo:lo + POOL_GROUP_DIM]
            dps_ref[:, cols] += _colsum(dyg * _dot(pooled, pw))
            dmixed = (dyg * ps_ref[:, cols]).astype(BF16)
            dpw_ref[g] += _dot_tn(pooled, dmixed)
            dpo = _dot_nt(dmixed, pw)
            dpooled.append(dpo)
            es.append(dpo / jnp.minimum(pos + 1, w).astype(F32))
        e = jnp.concatenate(es, axis=-1)
        run = jnp.concatenate([e, e_carry[...]], axis=0)
        e_carry[...] = e[:POOL_HALO, :]
        rows = tm + POOL_HALO
        for g, w in enumerate(POOL_WINDOWS):
            lo = g * POOL_GROUP_DIM
            run = run[:, POOL_GROUP_DIM if g else 0:]
            run = run + pltpu.roll(run, rows - w // 2, 0)
            du_ref[:, 2 * D_CONV + lo:2 * D_CONV + lo + POOL_GROUP_DIM] = (
                run[:tm, :POOL_GROUP_DIM] - dpooled[g]).astype(BF16)

        dh1 = _dot(du_ref[...], win_v[...])
        xh, r = _rms_fwd(x_ref[...])
        dgmix_ref[...] += _colsum(dh1 * xh)
        dx_ref[...] = dx1v + _rms_bwd(dh1, xh, r, gmix_ref[...])

    rev = lambda w: pl.BlockSpec((tm, w), lambda i: (n_tiles - 1 - i, 0))
    return pl.pallas_call(
        body, name="bwd_mix", grid=(n_tiles,),
        in_specs=[rev(D_MODEL), rev(D_MODEL), rev(D_IN), rev(D_CONV), rev(D_POOL), _full((1, D_MODEL)),
                  pl.BlockSpec(memory_space=pl.ANY), _full((CONV_WIDTH, D_CONV)), _full((1, D_CONV)), _full((1, D_CONV)),
                  _full((4, POOL_GROUP_DIM, POOL_GROUP_DIM)), _full((1, D_POOL)), _full(after.shape)],
        out_specs=[rev(D_MODEL), rev(D_IN), _full((1, D_MODEL)), _full((CONV_WIDTH, D_CONV)), _full((1, D_CONV)),
                   _full((1, D_CONV)), _full((1, D_CONV)), _full((4, POOL_GROUP_DIM, POOL_GROUP_DIM)), _full((1, D_POOL))],
        out_shape=[jax.ShapeDtypeStruct((tokens, D_MODEL), F32), jax.ShapeDtypeStruct((tokens, D_IN), BF16),
                   jax.ShapeDtypeStruct((1, D_MODEL), F32), jax.ShapeDtypeStruct((CONV_WIDTH, D_CONV), F32),
                   jax.ShapeDtypeStruct((1, D_CONV), F32), jax.ShapeDtypeStruct((1, D_CONV), F32),
                   jax.ShapeDtypeStruct((1, D_CONV), F32),
                   jax.ShapeDtypeStruct((4, POOL_GROUP_DIM, POOL_GROUP_DIM), F32), jax.ShapeDtypeStruct((1, D_POOL), F32)],
        scratch_shapes=[pltpu.VMEM((D_IN, D_MODEL), BF16), pltpu.VMEM((D_MODEL, D_MODEL), BF16),
                        pltpu.VMEM((CONV_HALO, D_CONV), F32), pltpu.VMEM((POOL_HALO, D_POOL), F32),
                        pltpu.SemaphoreType.DMA((2,))],
        compiler_params=_params(),
    )(dx1, x2d, u_all, c_all, pooled_all, g_mix, gw, conv_w, ln_g, ln_b, pool_w, pool_scale, after)


def _wgrad(a, b, name, after=None):
    tokens, m = a.shape
    n = b.shape[1]
    tm = 512 if m % 512 == 0 else 256
    extra = [] if after is None else [after]

    def body(a_ref, b_ref, *rest):
        rest[-1][...] = _dot_tn(a_ref[...], b_ref[...]).astype(rest[-1].dtype)

    return pl.pallas_call(
        body, name=name, grid=(m // tm,),
        in_specs=[pl.BlockSpec((tokens, tm), lambda i: (0, i)), _full((tokens, n))] + [_full(t.shape) for t in extra],
        out_specs=pl.BlockSpec((tm, n), lambda i: (i, 0)),
        out_shape=jax.ShapeDtypeStruct((m, n), BF16),
        compiler_params=_params(),
    )(a, b, *extra)


def _adamw_update(w, g, m, v):
    nm = ADAM_B1 * m + (1.0 - ADAM_B1) * g
    nv = ADAM_B2 * v + (1.0 - ADAM_B2) * (g * g)
    m_hat = nm / (1.0 - ADAM_B1 ** ADAM_STEP)
    v_hat = nv / (1.0 - ADAM_B2 ** ADAM_STEP)
    return -ADAM_LR * (m_hat / (jnp.sqrt(v_hat) + ADAM_EPS) + ADAM_WD * w), nm, nv


def _adamw_small(ws, gs, ms, vs):
    n = len(ws)

    def body(*refs):
        ins, outs = refs[:4 * n], refs[4 * n:]
        for k in range(n):
            d, nm, nv = _adamw_update(*[ins[j * n + k][...] for j in range(4)])
            outs[k][...] = d
            outs[n + k][...] = nm
            outs[2 * n + k][...] = nv

    vmem = pl.BlockSpec(memory_space=pltpu.VMEM)
    outs = pl.pallas_call(
        body, name="adamw_small",
        in_specs=[vmem] * (4 * n), out_specs=[vmem] * (3 * n),
        out_shape=[jax.ShapeDtypeStruct(w.shape, F32) for w in ws] * 3,
    )(*ws, *gs, *ms, *vs)
    return outs[:n], outs[n:2 * n], outs[2 * n:]


def _adamw(w, g, m, v, name):
    rows, cols = w.shape
    tile = rows
    for cand in (512, 256, 128, 64, 32, 16, 8):
        if rows % cand == 0:
            tile = cand
            break

    def body(w_ref, g_ref, m_ref, v_ref, d_ref, nm_ref, nv_ref):
        d_ref[...], nm_ref[...], nv_ref[...] = _adamw_update(w_ref[...], g_ref[...], m_ref[...], v_ref[...])

    spec = pl.BlockSpec((tile, cols), lambda i: (i, 0))
    return pl.pallas_call(
        body, name=name, grid=(rows // tile,),
        in_specs=[spec] * 4, out_specs=[spec] * 3,
        out_shape=[jax.ShapeDtypeStruct((rows, cols), F32)] * 3,
        compiler_params=_params(("arbitrary",)),
    )(w, g, m, v)


SMALL = (("norm_mix_g", (1, 1024)), ("conv_dw_b", (1, 512)), ("conv_ln_g", (1, 512)), ("conv_ln_b", (1, 512)),
         ("pool_w", (1, 4, 128, 128)), ("pool_scale", (1, 512)), ("norm_xattn_g", (1, 1024)), ("norm_mem_g", (1, 1024)),
         ("norm_ffn_g", (1, 1024)), ("ffn_dw_b", (1, 5632)), ("norm_final_g", (1024,)))
LANES = 128


def _pack_rows(arrs):
    flat = jnp.concatenate([a.reshape(-1) for a in arrs])
    pad = (-flat.shape[0]) % (8 * LANES)
    return jnp.pad(flat, (0, pad)).reshape(-1, LANES)


def kernel(x, mem, norm_mix_g, w_in, conv_dw_w, conv_dw_b, conv_ln_g, conv_ln_b, pool_w, pool_scale, w_out, norm_xattn_g, norm_mem_g, w_q, w_kv, w_o, norm_ffn_g, w_up, ffn_dw_w, ffn_dw_b, w_down, norm_final_g, loss_target, m_norm_mix_g, m_w_in, m_conv_dw_w, m_conv_dw_b, m_conv_ln_g, m_conv_ln_b, m_pool_w, m_pool_scale, m_w_out, m_norm_xattn_g, m_norm_mem_g, m_w_q, m_w_kv, m_w_o, m_norm_ffn_g, m_w_up, m_ffn_dw_w, m_ffn_dw_b, m_w_down, m_norm_final_g, v_norm_mix_g, v_w_in, v_conv_dw_w, v_conv_dw_b, v_conv_ln_g, v_conv_ln_b, v_pool_w, v_pool_scale, v_w_out, v_norm_xattn_g, v_norm_mem_g, v_w_q, v_w_kv, v_w_o, v_norm_ffn_g, v_w_up, v_ffn_dw_w, v_ffn_dw_b, v_w_down, v_norm_final_g):
    weights = dict(norm_mix_g=norm_mix_g, w_in=w_in, conv_dw_w=conv_dw_w, conv_dw_b=conv_dw_b, conv_ln_g=conv_ln_g,
                   conv_ln_b=conv_ln_b, pool_w=pool_w, pool_scale=pool_scale, w_out=w_out, norm_xattn_g=norm_xattn_g,
                   norm_mem_g=norm_mem_g, w_q=w_q, w_kv=w_kv, w_o=w_o, norm_ffn_g=norm_ffn_g, w_up=w_up,
                   ffn_dw_w=ffn_dw_w, ffn_dw_b=ffn_dw_b, w_down=w_down, norm_final_g=norm_final_g)
    moments_m = dict(norm_mix_g=m_norm_mix_g, w_in=m_w_in, conv_dw_w=m_conv_dw_w, conv_dw_b=m_conv_dw_b,
                     conv_ln_g=m_conv_ln_g, conv_ln_b=m_conv_ln_b, pool_w=m_pool_w, pool_scale=m_pool_scale,
                     w_out=m_w_out, norm_xattn_g=m_norm_xattn_g, norm_mem_g=m_norm_mem_g, w_q=m_w_q, w_kv=m_w_kv,
                     w_o=m_w_o, norm_ffn_g=m_norm_ffn_g, w_up=m_w_up, ffn_dw_w=m_ffn_dw_w, ffn_dw_b=m_ffn_dw_b,
                     w_down=m_w_down, norm_final_g=m_norm_final_g)
    moments_v = dict(norm_mix_g=v_norm_mix_g, w_in=v_w_in, conv_dw_w=v_conv_dw_w, conv_dw_b=v_conv_dw_b,
                     conv_ln_g=v_conv_ln_g, conv_ln_b=v_conv_ln_b, pool_w=v_pool_w, pool_scale=v_pool_scale,
                     w_out=v_w_out, norm_xattn_g=v_norm_xattn_g, norm_mem_g=v_norm_mem_g, w_q=v_w_q, w_kv=v_w_kv,
                     w_o=v_w_o, norm_ffn_g=v_norm_ffn_g, w_up=v_w_up, ffn_dw_w=v_ffn_dw_w, ffn_dw_b=v_ffn_dw_b,
                     w_down=v_w_down, norm_final_g=v_norm_final_g)
    order = list(weights)
    transposed = ("w_in", "w_kv", "w_up")

    n_b, seq, _ = x.shape
    tokens = n_b * seq
    tm_mix = min(512, seq // 2)
    tm_attn = min(1024, seq // 2)
    tm_ffn = min(256, seq // 2)
    dev = 4 * lax.axis_index("x") + 2 * lax.axis_index("y") + lax.axis_index("c")

    packs = [jnp.concatenate([weights[n][0].T if n in transposed else weights[n][0] for n in names], axis=0).astype(BF16)
             for names in AG_GROUPS]
    small_sharded = _pack_rows([conv_dw_w[0], ffn_dw_w[0]])
    gw_mix, gsmall = _all_gather([packs[0], small_sharded], "weights_all_gather")
    flights = []
    after = gw_mix
    for k in (1, 2):
        own_in_place = lax.dynamic_update_slice(lax.empty((N_DEV,) + packs[k].shape, BF16), packs[k][None], (dev, 0, 0))
        flights.append(_gather_start(own_in_place, after, "weights_gather_start_%d" % k, BARRIER_IDS["gather_start"][k - 1]))
        after = flights[-1][3]
    gflat = gsmall.reshape(N_DEV, -1)
    n_cw = CONV_WIDTH * (D_CONV // N_DEV)
    n_fw = FFN_CONV_WIDTH * (2 * D_FF // N_DEV)
    conv_w = gflat[:, :n_cw].reshape(N_DEV, CONV_WIDTH, D_CONV // N_DEV).transpose(1, 0, 2).reshape(CONV_WIDTH, D_CONV)
    ffn_w = gflat[:, n_cw:n_cw + n_fw].reshape(N_DEV, FFN_CONV_WIDTH, 2 * D_FF // N_DEV).transpose(1, 0, 2).reshape(
        FFN_CONV_WIDTH, 2 * D_FF)

    x2d = x.reshape(tokens, D_MODEL)
    mem2d = mem.reshape(n_b * N_MEM, D_MODEL)
    tgt2d = loss_target.reshape(tokens, D_MODEL)
    g_final = norm_final_g.reshape(1, D_MODEL)

    def gather_finish(flight, after, tag):
        fwd_send, fwd_recv, buf = _gather_forward(*flight[:3], after, "weights_gather_forward_" + tag,
                                                  BARRIER_IDS["gather_forward"][int(tag) - 1])
        return _gather_finish(fwd_send, fwd_recv, buf, "weights_gather_finish_" + tag)

    x1, u_all, c_all, pooled_all, ymix, h1 = _fwd_mix(
        x2d, gw_mix, norm_mix_g, conv_w, conv_dw_b, conv_ln_g, conv_ln_b, pool_w[0], pool_scale, flights[1][3],
        seq, tm_mix)
    gw_attn = gather_finish(flights[0], x1, "1")
    mem_n, kv = _fwd_kv(mem2d, gw_attn, norm_mem_g)
    x2, h2, q, o = _fwd_attn(x1, kv, gw_attn, norm_xattn_g, seq, tm_attn)
    gw_ffn = gather_finish(flights[1], x2, "2")
    uu_all, cc_all, a_all, h3, dx3, dx3b, loss_part, dg_final = _fwd_ffn(
        x2, tgt2d, gw_ffn, norm_ffn_g, ffn_w, ffn_dw_b, g_final, seq, tm_ffn)

    table = _owner_table()

    def sibling_start(names, tag):
        parts = [part[n].reshape(N_DEV, W_OFF[n][1], D_MODEL) for n in names]
        return _exchange_start(parts, 4, _to_sibling, "rs_sibling_exchange_start_" + tag, BARRIER_IDS["sibling"][tag])

    def chips_start(flight, after, tag):
        parts, landed = _exchange_wait(*flight[:4], after, 4, _to_sibling, "rs_sibling_exchange_wait_" + tag)
        sums = _chip_partial_sums(table, parts, landed, "rs_chip_partial_sums_" + tag)
        return parts, landed, _exchange_start(sums, 3, _to_chip, "rs_chip_exchange_start_" + tag,
                                              BARRIER_IDS["chips"][tag])

    grads, delta, new_m, new_v = {}, {}, {}, {}

    def reduce_finish(names, parts, landed, flight, after, tag):
        _, from_chips = _exchange_wait(*flight[:4], after, 3, _to_chip, "rs_chip_exchange_wait_" + tag)
        as_rows = {n: n not in transposed or W_OFF[n][1] % LANES != 0 for n in names}
        states = [tuple(t[n][0].T if n in transposed else t[n][0] for t in (weights, moments_m, moments_v))
                  if as_rows[n] else None for n in names]
        results = _final_update(table, parts, landed, from_chips, states, "rs_final_update_" + tag)
        for n, res in zip(names, results):
            back = (lambda t: t.T[None]) if n in transposed else (lambda t: t[None])
            grads[n] = back(res[0])
            if as_rows[n]:
                delta[n], new_m[n], new_v[n] = [back(t) for t in res[1:]]
            else:
                delta[n], new_m[n], new_v[n] = [t[None] for t in _adamw(
                    weights[n][0], grads[n][0], moments_m[n][0], moments_v[n][0], "adamw_" + n)]
        alone = [n for n in names if not as_rows[n]]
        return delta[alone[-1] if alone else names[-1]]

    part = {}
    dx2, dx2b, duu, d_ffn_b, d_ffn_w, dg_ffn = _bwd_ffn(dx3, x2, uu_all, cc_all, gw_ffn, norm_ffn_g, ffn_w, seq, tm_ffn)
    part["w_up"] = _wgrad(duu, h3, "wgrad_w_up")
    part["w_down"] = _wgrad(a_all, dx3b, "wgrad_w_down")
    to_sibling_a = sibling_start(RS_GROUPS["a"], "a")
    dx1, dx1b, dq, dkv, dg_x = _bwd_attn(dx2, x1, q, kv, gw_attn, norm_xattn_g, to_sibling_a[4], seq, tm_mix)
    parts_a, landed_a, flight_a = chips_start(to_sibling_a, dx1, "a")
    dkv_b, dg_mem = _bwd_kv(dkv, mem2d, gw_attn)
    part["w_q"] = _wgrad(h2, dq, "wgrad_w_q", after=flight_a[4])
    part["w_kv"] = _wgrad(dkv_b, mem_n, "wgrad_w_kv")
    part["w_o"] = _wgrad(o, dx2b, "wgrad_w_o")
    part["w_out"] = _wgrad(ymix, dx1b, "wgrad_w_out")
    to_sibling_b = sibling_start(RS_GROUPS["b"], "b")
    parts_b, landed_b, flight_b = chips_start(to_sibling_b, to_sibling_b[4], "b")
    dx, du, dg_mix, d_conv_w, d_conv_b, d_ln_g, d_ln_b, d_pool_w, d_pool_scale = _bwd_mix(
        dx1, x2d, u_all, c_all, pooled_all, gw_mix, norm_mix_g, conv_w, conv_ln_g, conv_ln_b, pool_w[0], pool_scale,
        flight_b[4], seq, tm_mix)
    grad_x = dx.reshape(x.shape)

    small_grads = dict(norm_mix_g=dg_mix, conv_dw_b=d_conv_b, conv_ln_g=d_ln_g, conv_ln_b=d_ln_b, pool_w=d_pool_w,
                       pool_scale=d_pool_scale, norm_xattn_g=dg_x, norm_mem_g=dg_mem, norm_ffn_g=dg_ffn,
                       ffn_dw_b=d_ffn_b, norm_final_g=dg_final)
    small_list = [small_grads[n] for n, _ in SMALL] + [d_conv_w, d_ffn_w, loss_part[:1]]
    small_mine = _pack_rows(small_list)
    small_flight = _broadcast_start(
        lax.dynamic_update_slice(lax.empty((N_DEV,) + small_mine.shape, F32), small_mine[None], (dev, 0, 0)),
        "small_grads_broadcast_start", BARRIER_IDS["broadcast"])

    part["w_in"] = _wgrad(du, h1, "wgrad_w_in", after=small_flight[3])
    to_sibling_c = sibling_start(RS_GROUPS["c"], "c")
    parts_c, landed_c, flight_c = chips_start(to_sibling_c, to_sibling_c[4], "c")
    updated_a = reduce_finish(RS_GROUPS["a"], parts_a, landed_a, flight_a, flight_c[4], "a")
    updated_b = reduce_finish(RS_GROUPS["b"], parts_b, landed_b, flight_b, updated_a, "b")
    small_all = _broadcast_wait(*small_flight[:3], updated_b, "small_grads_broadcast_wait")
    small_sum = _sum_blocks(small_all).reshape(-1)

    pos = 0
    for n, shape in SMALL:
        size = 1
        for s in shape:
            size *= s
        grads[n] = small_sum[pos:pos + size].reshape(shape)
        pos += size
    full_conv_w = small_sum[pos:pos + CONV_WIDTH * D_CONV].reshape(CONV_WIDTH, D_CONV)
    pos += CONV_WIDTH * D_CONV
    full_ffn_w = small_sum[pos:pos + FFN_CONV_WIDTH * 2 * D_FF].reshape(FFN_CONV_WIDTH, 2 * D_FF)
    loss = small_sum[pos + FFN_CONV_WIDTH * 2 * D_FF]
    grads["conv_dw_w"] = lax.dynamic_slice_in_dim(full_conv_w, dev * (D_CONV // N_DEV), D_CONV // N_DEV, axis=1)[None]
    grads["ffn_dw_w"] = lax.dynamic_slice_in_dim(full_ffn_w, dev * (2 * D_FF // N_DEV), 2 * D_FF // N_DEV, axis=1)[None]

    small_names = [n for n in order if n not in W_OFF]
    swap = lambda t: jnp.transpose(t, (1, 0, 2))
    two_d = lambda t: t.reshape(1, -1) if t.ndim == 1 else (swap(t) if t.ndim == 3 else t)
    outs = _adamw_small(*[[two_d(t[n]) for n in small_names] for t in (weights, grads, moments_m, moments_v)])
    for res, out in zip((delta, new_m, new_v), outs):
        for n, o in zip(small_names, out):
            res[n] = swap(o) if o.ndim == 3 else o.reshape(weights[n].shape)

    reduce_finish(RS_GROUPS["c"], parts_c, landed_c, flight_c, delta[small_names[-1]], "c")

    return (loss, grad_x, *[grads[n] for n in order], *[delta[n] for n in order],
            *[new_m[n] for n in order], *[new_v[n] for n in order])
```

```python
import jax
import jax.numpy as jnp
from jax import lax
from jax.experimental import pallas as pl
from jax.experimental.pallas import tpu as pltpu

F32 = jnp.float32
BF16 = jnp.bfloat16
MESH = pl.DeviceIdType.MESH

N_DEV = 8
D_MODEL = 1024
D_CONV = 512
D_POOL = 512
CONV_WIDTH = 31
POOL_WINDOWS = (2, 4, 8, 16)
POOL_GROUP_DIM = 128
D_IN = 1536
N_MEM = 256
HEADS = 4
HEAD_DIM = 256
D_FF = 2816
FFN_CONV_WIDTH = 3
EPS = 1e-6
ADAM_LR = 0.001
ADAM_B1 = 0.9
ADAM_B2 = 0.999
ADAM_EPS = 1e-08
ADAM_WD = 0.01
ADAM_STEP = 10

VMEM_LIMIT_V7X = 56 * 1024 * 1024
CONV_HALO = 32
POOL_HALO = 16
FFN_HALO = 8
FFN_CHUNK = 2816

W_ROWS = (("w_in", 192), ("w_out", 128), ("w_q", 128), ("w_kv", 256), ("w_o", 128), ("w_up", 704), ("w_down", 352))
AG_GROUPS = (("w_in", "w_out"), ("w_q", "w_kv", "w_o"), ("w_up", "w_down"))
W_OFF = {}
for _names in AG_GROUPS:
    _o = 0
    for _n in _names:
        W_OFF[_n] = (_o, dict(W_ROWS)[_n])
        _o += dict(W_ROWS)[_n]
RS_GROUPS = {"a": ("w_up", "w_down"), "b": ("w_q", "w_kv", "w_o", "w_out"), "c": ("w_in",)}
BARRIER_IDS = {"gather_start": (0, 1), "gather_forward": (2, 3), "sibling": {"a": 4, "b": 5, "c": 6},
               "chips": {"a": 7, "b": 8, "c": 9}, "broadcast": 10}


def _dot(a, b):
    return jnp.dot(a, b, preferred_element_type=F32)


def _dot_nt(a, b):
    return lax.dot_general(a, b, (((1,), (1,)), ((), ())), preferred_element_type=F32)


def _dot_tn(a, b):
    return lax.dot_general(a, b, (((0,), (0,)), ((), ())), preferred_element_type=F32)


def _sigmoid(v):
    return 1.0 / (1.0 + jnp.exp(-v))


def _rms_fwd(v):
    r = lax.rsqrt(jnp.mean(v * v, axis=-1, keepdims=True) + EPS)
    return v * r, r


def _rms_bwd(dh, vh, r, g):
    gd = dh * g
    return r * (gd - vh * jnp.mean(gd * vh, axis=-1, keepdims=True))


def _sublane_shifts(v):
    rows = v.shape[0]
    return [v] + [pltpu.roll(v, rows - b, 0) for b in range(1, 8)]


def _colsum(v):
    return jnp.sum(v, axis=0, keepdims=True)


def _colsum_mxu(v):
    return _dot(jnp.ones((8, v.shape[0]), BF16), v.astype(BF16))[0:1, :]


def _full(shape):
    return pl.BlockSpec(shape, lambda *_: (0,) * len(shape))


def _params(sem=("arbitrary",), vmem=VMEM_LIMIT_V7X):
    return pltpu.CompilerParams(dimension_semantics=sem, vmem_limit_bytes=vmem)


def _load_weight(g_hbm, name, dst, sem):
    off, rows = W_OFF[name]
    return [pltpu.make_async_copy(g_hbm.at[d, pl.ds(off, rows), :], dst.at[pl.ds(d * rows, rows), :], sem)
            for d in range(N_DEV)]


def _start_weights(g_hbm, names, dsts, sems):
    @pl.when(pl.program_id(0) == 0)
    def _():
        copies = [_load_weight(g_hbm, name, dst, sems.at[k]) for k, (name, dst) in enumerate(zip(names, dsts))]
        for cp in sum(copies, []):
            cp.start()
        for cp in sum(copies, []):
            cp.wait()


def _position():
    x, y, c = lax.axis_index("x"), lax.axis_index("y"), lax.axis_index("c")
    chips = [(1 - x, y), (x, 1 - y), (1 - x, 1 - y)]
    return x, y, c, chips


def _dev(px, py, pc):
    return 4 * px + 2 * py + pc


def _all_gather(arrs, name):
    n = len(arrs)

    def body(*refs):
        ins, outs = refs[:n], refs[n:2 * n]
        send_sems, recv_sems, local_sems = refs[2 * n:2 * n + 3]
        bounce = refs[2 * n + 3:]
        x, y, c, chips = _position()
        me, sibling = (x, y, c), (x, y, 1 - c)

        def copy(a, k, block, to, src=None):
            rows = outs[a].at[_dev(*block)]
            return pltpu.make_async_remote_copy(
                src_ref=rows if src is None else src, dst_ref=rows,
                send_sem=send_sems.at[a, k], recv_sem=recv_sems.at[a, k], device_id=to, device_id_type=MESH)

        sends = []
        for a in range(n):
            first = [copy(a, 0, me, sibling, src=ins[a])]
            first += [copy(a, 1 + j, me, (*chip, c), src=ins[a]) for j, chip in enumerate(chips)]
            for cp in first:
                cp.start()
            sends += first
        started = []
        for a in range(n):
            load = pltpu.make_async_copy(ins[a], bounce[a], local_sems.at[a, 0])
            load.start()
            load.wait()
            mine = pltpu.make_async_copy(bounce[a], outs[a].at[_dev(*me)], local_sems.at[a, 1])
            mine.start()
            started.append(mine)
        for j, chip in enumerate(chips):
            for a in range(n):
                copy(a, 1 + j, (*chip, c), me).wait_recv()
                passed = copy(a, 4 + j, (*chip, c), sibling)
                passed.start()
                sends.append(passed)
        for a in range(n):
            copy(a, 0, sibling, me).wait_recv()
            for j, chip in enumerate(chips):
                copy(a, 4 + j, (*chip, 1 - c), me).wait_recv()
        for cp in sends:
            cp.wait_send()
        for mine in started:
            mine.wait()

    any_spec = pl.BlockSpec(memory_space=pl.ANY)
    return pl.pallas_call(
        body, name=name,
        out_shape=[jax.ShapeDtypeStruct((N_DEV,) + a.shape, a.dtype) for a in arrs],
        in_specs=[any_spec] * n, out_specs=[any_spec] * n,
        scratch_shapes=[pltpu.SemaphoreType.DMA((n, 7)), pltpu.SemaphoreType.DMA((n, 7)), pltpu.SemaphoreType.DMA((n, 2))]
        + [pltpu.VMEM(a.shape, a.dtype) for a in arrs],
    )(*arrs)


_HBM = pl.BlockSpec(memory_space=pltpu.HBM)
_SEM = pl.BlockSpec(memory_space=pltpu.SEMAPHORE)
_SIDE_EFFECT = pltpu.SideEffectType.DATAFLOW_SIDE_EFFECTING


def _handshake(peers):
    barrier = pltpu.get_barrier_semaphore()
    for peer in peers:
        pl.semaphore_signal(barrier, inc=1, device_id=peer, device_id_type=MESH)
    pl.semaphore_wait(barrier, len(peers))


def _gather_start(buf, after, name, collective_id):
    def body(buf_ref, after_ref, send_sems, recv_sems, buf_thru, token):
        del after_ref, buf_thru
        x, y, c, chips = _position()
        rows = buf_ref.at[_dev(x, y, c)]
        targets = [(x, y, 1 - c)] + [(*chip, c) for chip in chips]
        _handshake(targets)
        for k, to in enumerate(targets):
            pltpu.make_async_remote_copy(src_ref=rows, dst_ref=rows, send_sem=send_sems.at[k], recv_sem=recv_sems.at[k],
                                         device_id=to, device_id_type=MESH).start()
        token[...] = jnp.zeros_like(token)

    return pl.pallas_call(
        body, name=name,
        out_shape=(pltpu.SemaphoreType.DMA((4,)), pltpu.SemaphoreType.DMA((4,)), pltpu.HBM(buf.shape, buf.dtype),
                   jax.ShapeDtypeStruct((8, 128), F32)),
        in_specs=(_HBM, pl.BlockSpec(memory_space=pl.ANY)),
        out_specs=(_SEM, _SEM, _HBM, pl.BlockSpec(memory_space=pltpu.VMEM)),
        input_output_aliases={0: 2},
        compiler_params=pltpu.CompilerParams(has_side_effects=_SIDE_EFFECT, collective_id=collective_id),
    )(pltpu.with_memory_space_constraint(buf, pltpu.HBM), after)


def _gather_forward(send_sems, recv_sems, buf, after, name, collective_id):
    def body(buf_ref, send_sems, recv_sems, after_ref, fwd_send, fwd_recv, buf_thru):
        del after_ref, buf_thru
        x, y, c, chips = _position()
        sibling = (x, y, 1 - c)

        def copy(block, k, sends, recvs):
            rows = buf_ref.at[_dev(*block)]
            return pltpu.make_async_remote_copy(src_ref=rows, dst_ref=rows, send_sem=sends.at[k], recv_sem=recvs.at[k],
                                                device_id=sibling, device_id_type=MESH)

        _handshake([sibling])
        for k in range(4):
            copy((x, y, c), k, send_sems, recv_sems).wait_send()
        copy(sibling, 0, send_sems, recv_sems).wait_recv()
        for j, chip in enumerate(chips):
            copy((*chip, c), 1 + j, send_sems, recv_sems).wait_recv()
            copy((*chip, c), j, fwd_send, fwd_recv).start()

    return pl.pallas_call(
        body, name=name,
        out_shape=(pltpu.SemaphoreType.DMA((3,)), pltpu.SemaphoreType.DMA((3,)), pltpu.HBM(buf.shape, buf.dtype)),
        in_specs=(_HBM, _SEM, _SEM, pl.BlockSpec(memory_space=pl.ANY)), out_specs=(_SEM, _SEM, _HBM),
        input_output_aliases={0: 2},
        compiler_params=pltpu.CompilerParams(has_side_effects=_SIDE_EFFECT, collective_id=collective_id),
    )(buf, send_sems, recv_sems, after)


def _gather_finish(fwd_send, fwd_recv, buf, name):
    def body(buf_ref, fwd_send, fwd_recv, buf_thru):
        del buf_thru
        x, y, c, chips = _position()
        for j, chip in enumerate(chips):
            cp = pltpu.make_async_remote_copy(
                src_ref=buf_ref.at[_dev(*chip, c)], dst_ref=buf_ref.at[_dev(*chip, 1 - c)], send_sem=fwd_send.at[j],
                recv_sem=fwd_recv.at[j], device_id=(x, y, 1 - c), device_id_type=MESH)
            cp.wait_send()
            cp.wait_recv()

    return pl.pallas_call(
        body, name=name,
        out_shape=pltpu.HBM(buf.shape, buf.dtype),
        in_specs=(_HBM, _SEM, _SEM), out_specs=_HBM,
        input_output_aliases={0: 0},
        compiler_params=pltpu.CompilerParams(has_side_effects=_SIDE_EFFECT),
    )(buf, fwd_send, fwd_recv)


def _everyone_else(x, y, c, chips):
    return [(x, y, 1 - c)] + [(*chip, core) for chip in chips for core in (c, 1 - c)]


def _broadcast_start(buf, name, collective_id):
    def body(buf_ref, send_sems, recv_sems, buf_thru, token):
        del buf_thru
        x, y, c, chips = _position()
        rows = buf_ref.at[_dev(x, y, c)]
        _handshake(_everyone_else(x, y, c, chips))
        for k, to in enumerate(_everyone_else(x, y, c, chips)):
            pltpu.make_async_remote_copy(src_ref=rows, dst_ref=rows, send_sem=send_sems.at[k], recv_sem=recv_sems.at[k],
                                         device_id=to, device_id_type=MESH).start()
        token[...] = jnp.zeros_like(token)

    return pl.pallas_call(
        body, name=name,
        out_shape=(pltpu.SemaphoreType.DMA((7,)), pltpu.SemaphoreType.DMA((7,)), pltpu.HBM(buf.shape, buf.dtype),
                   jax.ShapeDtypeStruct((8, 128), F32)),
        in_specs=(_HBM,), out_specs=(_SEM, _SEM, _HBM, pl.BlockSpec(memory_space=pltpu.VMEM)),
        input_output_aliases={0: 2},
        compiler_params=pltpu.CompilerParams(has_side_effects=_SIDE_EFFECT, collective_id=collective_id),
    )(pltpu.with_memory_space_constraint(buf, pltpu.HBM))


def _broadcast_wait(send_sems, recv_sems, buf, after, name):
    def body(buf_ref, send_sems, recv_sems, after_ref, buf_thru):
        del after_ref, buf_thru
        x, y, c, chips = _position()
        for k, peer in enumerate(_everyone_else(x, y, c, chips)):
            cp = pltpu.make_async_remote_copy(
                src_ref=buf_ref.at[_dev(x, y, c)], dst_ref=buf_ref.at[_dev(*peer)], send_sem=send_sems.at[k],
                recv_sem=recv_sems.at[k], device_id=peer, device_id_type=MESH)
            cp.wait_send()
            cp.wait_recv()

    return pl.pallas_call(
        body, name=name,
        out_shape=pltpu.HBM(buf.shape, buf.dtype),
        in_specs=(_HBM, _SEM, _SEM, pl.BlockSpec(memory_space=pl.ANY)), out_specs=_HBM,
        input_output_aliases={0: 0},
        compiler_params=pltpu.CompilerParams(has_side_effects=_SIDE_EFFECT),
    )(buf, send_sems, recv_sems, after)


def _to_sibling(j, x, y, c, chips):
    return _dev(*([(x, y)] + chips)[j], 1 - c), (x, y, 1 - c)


def _to_chip(j, x, y, c, chips):
    return j, (*chips[j], c)


def _exchange_start(srcs, n_slots, route, name, collective_id):
    n = len(srcs)

    def body(*refs):
        s_refs, land_refs = refs[:n], refs[n:2 * n]
        send_sems, recv_sems = refs[2 * n:2 * n + 2]
        token = refs[-1]
        x, y, c, chips = _position()
        _handshake([(x, y, 1 - c)] if route is _to_sibling else [route(j, x, y, c, chips)[1] for j in range(n_slots)])
        for k in range(n):
            for j in range(n_slots):
                block, to = route(j, x, y, c, chips)
                pltpu.make_async_remote_copy(
                    src_ref=s_refs[k].at[block], dst_ref=land_refs[k].at[j], send_sem=send_sems.at[n_slots * k + j],
                    recv_sem=recv_sems.at[n_slots * k + j], device_id=to, device_id_type=MESH).start()
        token[...] = jnp.zeros_like(token)

    lands = [jax.ShapeDtypeStruct((n_slots,) + s.shape[1:], s.dtype) for s in srcs]
    outs = pl.pallas_call(
        body, name=name,
        out_shape=(pltpu.SemaphoreType.DMA((n_slots * n,)), pltpu.SemaphoreType.DMA((n_slots * n,)),
                   *[pltpu.HBM(s.shape, s.dtype) for s in srcs], *[pltpu.HBM(l.shape, l.dtype) for l in lands],
                   jax.ShapeDtypeStruct((8, 128), F32)),
        in_specs=[_HBM] * (2 * n), out_specs=(_SEM, _SEM, *[_HBM] * (2 * n), pl.BlockSpec(memory_space=pltpu.VMEM)),
        input_output_aliases={k: 2 + k for k in range(2 * n)},
        compiler_params=pltpu.CompilerParams(has_side_effects=_SIDE_EFFECT, collective_id=collective_id),
    )(*[pltpu.with_memory_space_constraint(s, pltpu.HBM) for s in srcs],
      *[pltpu.with_memory_space_constraint(lax.empty(l.shape, l.dtype), pltpu.HBM) for l in lands])
    return outs[0], outs[1], outs[2:2 + n], outs[2 + n:2 + 2 * n], outs[-1]


def _exchange_wait(send_sems, recv_sems, s_thru, land_thru, after, n_slots, route, name):
    n = len(s_thru)

    def body(*refs):
        s_refs, land_refs = refs[:n], refs[n:2 * n]
        send_sems, recv_sems = refs[2 * n:2 * n + 2]
        x, y, c, chips = _position()
        for k in range(n):
            for j in range(n_slots):
                block, to = route(j, x, y, c, chips)
                cp = pltpu.make_async_remote_copy(
                    src_ref=s_refs[k].at[block], dst_ref=land_refs[k].at[j], send_sem=send_sems.at[n_slots * k + j],
                    recv_sem=recv_sems.at[n_slots * k + j], device_id=to, device_id_type=MESH)
                cp.wait_send()
                cp.wait_recv()

    outs = pl.pallas_call(
        body, name=name,
        out_shape=(*[pltpu.HBM(s.shape, s.dtype) for s in s_thru], *[pltpu.HBM(l.shape, l.dtype) for l in land_thru]),
        in_specs=[_HBM] * (2 * n) + [_SEM, _SEM, pl.BlockSpec(memory_space=pl.ANY)], out_specs=[_HBM] * (2 * n),
        input_output_aliases={k: k for k in range(2 * n)},
        compiler_params=pltpu.CompilerParams(has_side_effects=_SIDE_EFFECT),
    )(*s_thru, *land_thru, send_sems, recv_sems, after)
    return outs[:n], outs[n:]


def _owner_table():
    x, y, c = lax.axis_index("x"), lax.axis_index("y"), lax.axis_index("c")
    chips = [(x, y), (1 - x, y), (x, 1 - y), (1 - x, 1 - y)]
    return jnp.stack([_dev(px, py, c) for px, py in chips]).astype(jnp.int32)


def _chip_partial_sums(table, parts, from_sibling, name):
    n = len(parts)

    def body(tab_ref, *refs):
        del tab_ref
        for g_ref, l_ref, out_ref in zip(refs[:n], refs[n:2 * n], refs[2 * n:]):
            out_ref[...] = (g_ref[...].astype(F32) + l_ref[...].astype(F32)).astype(out_ref.dtype)

    block = lambda p: (None,) + p.shape[1:]
    grid_spec = pltpu.PrefetchScalarGridSpec(
        num_scalar_prefetch=1, grid=(3,),
        in_specs=[pl.BlockSpec(block(p), lambda j, tab: (tab[j + 1], 0, 0)) for p in parts]
        + [pl.BlockSpec(block(p), lambda j, tab: (j + 1, 0, 0)) for p in parts],
        out_specs=[pl.BlockSpec(block(p), lambda j, tab: (j, 0, 0)) for p in parts])
    return pl.pallas_call(
        body, name=name, grid_spec=grid_spec,
        out_shape=[jax.ShapeDtypeStruct((3,) + p.shape[1:], BF16) for p in parts],
        compiler_params=_params(("arbitrary",)),
    )(table, *parts, *from_sibling)


def _final_update(table, parts, from_sibling, from_chips, states, name):
    n = len(parts)
    flipped = [states[k][0].shape != parts[k].shape[1:] for k in range(n)]

    def body(tab_ref, *refs):
        del tab_ref
        ins, outs = refs[:6 * n], refs[6 * n:]
        for k in range(n):
            acc = ins[k][...].astype(F32) + ins[n + k][...].astype(F32)
            for j in range(3):
                acc = acc + ins[2 * n + k][j].astype(F32)
            if flipped[k]:
                acc = acc.T
            w_ref, m_ref, v_ref = ins[3 * n + 3 * k:3 * n + 3 * k + 3]
            outs[4 * k][...] = acc
            for out_ref, val in zip(outs[4 * k + 1:4 * k + 4], _adamw_update(w_ref[...], acc, m_ref[...], v_ref[...])):
                out_ref[...] = val

    def grad_block(k, lead, at):
        r, c = parts[k].shape[1:]
        if flipped[k]:
            return pl.BlockSpec(lead + (r, c // 2), lambda t, tab: (*at(tab), 0, t))
        return pl.BlockSpec(lead + (r // 2, c), lambda t, tab: (*at(tab), t, 0))

    def state_block(k):
        a, b = states[k][0].shape
        return pl.BlockSpec((a // 2, b), lambda t, tab: (t, 0))

    grid_spec = pltpu.PrefetchScalarGridSpec(
        num_scalar_prefetch=1, grid=(2,),
        in_specs=[grad_block(k, (None,), lambda tab: (tab[0],)) for k in range(n)]
        + [grad_block(k, (None,), lambda tab: (0,)) for k in range(n)]
        + [grad_block(k, (3,), lambda tab: (0,)) for k in range(n)]
        + [state_block(k) for k in range(n) for _ in range(3)],
        out_specs=[state_block(k) for k in range(n) for _ in range(4)])
    outs = pl.pallas_call(
        body, name=name, grid_spec=grid_spec,
        out_shape=[jax.ShapeDtypeStruct(states[k][0].shape, F32) for k in range(n) for _ in range(4)],
        compiler_params=_params(("arbitrary",)),
    )(table, *parts, *from_sibling, *from_chips, *[t for k in range(n) for t in states[k]])
    return [outs[4 * k:4 * k + 4] for k in range(n)]


def _sum_blocks(g8):
    _, rows, cols = g8.shape

    def body(g_ref, out_ref):
        acc = g_ref[0]
        for d in range(1, N_DEV):
            acc = acc + g_ref[d]
        out_ref[...] = acc

    return pl.pallas_call(
        body, name="small_grad_sum", grid=(1,),
        in_specs=[_full((N_DEV, rows, cols))], out_specs=_full((rows, cols)),
        out_shape=jax.ShapeDtypeStruct((rows, cols), F32),
        compiler_params=_params(("arbitrary",)),
    )(g8)


def _fwd_mix(x2d, gw, g_mix, conv_w, conv_b, ln_g, ln_b, pool_w, pool_scale, after, seq, tm):
    tokens = x2d.shape[0]
    n_tiles = tokens // tm
    tps = seq // tm

    def body(x_ref, gmix_ref, gw_hbm, cw_ref, cb_ref, lng_ref, lnb_ref, pw_ref, ps_ref, after_ref,
             x1_ref, u_ref, c_ref, pooled_ref, ymix_ref, h1_ref,
             win_v, wout_v, hc_carry, up_carry, sem):
        del after_ref
        i = pl.program_id(0)

        _start_weights(gw_hbm, ("w_in", "w_out"), (win_v, wout_v), sem)

        @pl.when(i % tps == 0)
        def _():
            hc_carry[...] = jnp.zeros_like(hc_carry)
            up_carry[...] = jnp.zeros_like(up_carry)

        x = x_ref[...]
        xh, _ = _rms_fwd(x)
        h1 = (xh * gmix_ref[...]).astype(BF16)
        h1_ref[...] = h1
        u = _dot_nt(h1, win_v[...])
        u_ref[...] = u
        val, gate, up = u[:, :D_CONV], u[:, D_CONV:2 * D_CONV], u[:, 2 * D_CONV:]

        extp = jnp.concatenate([up_carry[...], up], axis=0)
        up_carry[...] = up[tm - POOL_HALO:, :]
        pos = lax.broadcasted_iota(jnp.int32, (tm, 1), 0) + (i % tps) * tm
        run = extp
        mixed = []
        for g, w in enumerate(POOL_WINDOWS):
            lo = g * POOL_GROUP_DIM
            run = run[:, POOL_GROUP_DIM if g else 0:]
            run = run + pltpu.roll(run, w // 2, 0)
            cnt = jnp.minimum(pos + 1, w).astype(F32)
            pooled = run[POOL_HALO:, :POOL_GROUP_DIM] / cnt - up[:, lo:lo + POOL_GROUP_DIM]
            pooled = pooled.astype(BF16)
            pooled_ref[:, lo:lo + POOL_GROUP_DIM] = pooled
            mixed.append(_dot(pooled, pw_ref[g].astype(BF16)))
        y_pool = jnp.concatenate(mixed, axis=-1) * ps_ref[...]
        y_pool = y_pool.astype(BF16)
        ymix_ref[:, D_CONV:] = y_pool
        out = _dot(y_pool, wout_v[D_CONV:, :])

        hc = val * _sigmoid(gate)
        ext = jnp.concatenate([hc_carry[...], hc], axis=0)
        hc_carry[...] = hc[tm - CONV_HALO:, :]
        conv = jnp.broadcast_to(cb_ref[...], (tm, D_CONV))
        ahead_by = _sublane_shifts(ext)
        for k in range(CONV_WIDTH):
            whole, part = divmod(CONV_HALO - (CONV_WIDTH - 1) + k, 8)
            conv = conv + cw_ref[k:k + 1, :] * ahead_by[part][8 * whole:8 * whole + tm, :]
        c_ref[...] = conv
        mu = jnp.mean(conv, axis=-1, keepdims=True)
        cen = conv - mu
        ln = cen * lax.rsqrt(jnp.mean(cen * cen, axis=-1, keepdims=True) + EPS) * lng_ref[...] + lnb_ref[...]
        y_conv = ln * _sigmoid(ln)
        y_conv = y_conv.astype(BF16)
        ymix_ref[:, :D_CONV] = y_conv
        x1_ref[...] = x + (out + _dot(y_conv, wout_v[:D_CONV, :]))

    row = lambda w: pl.BlockSpec((tm, w), lambda i: (i, 0))
    return pl.pallas_call(
        body, name="fwd_mix", grid=(n_tiles,),
        in_specs=[row(D_MODEL), _full((1, D_MODEL)), pl.BlockSpec(memory_space=pl.ANY),
                  _full((CONV_WIDTH, D_CONV)), _full((1, D_CONV)), _full((1, D_CONV)), _full((1, D_CONV)),
                  _full((4, POOL_GROUP_DIM, POOL_GROUP_DIM)), _full((1, D_POOL)), _full(after.shape)],
        out_specs=[row(D_MODEL), row(D_IN), row(D_CONV), row(D_POOL), row(D_MODEL), row(D_MODEL)],
        out_shape=[jax.ShapeDtypeStruct((tokens, D_MODEL), F32), jax.ShapeDtypeStruct((tokens, D_IN), F32),
                   jax.ShapeDtypeStruct((tokens, D_CONV), F32), jax.ShapeDtypeStruct((tokens, D_POOL), BF16),
                   jax.ShapeDtypeStruct((tokens, D_MODEL), BF16), jax.ShapeDtypeStruct((tokens, D_MODEL), BF16)],
        scratch_shapes=[pltpu.VMEM((D_IN, D_MODEL), BF16), pltpu.VMEM((D_MODEL, D_MODEL), BF16),
                        pltpu.VMEM((CONV_HALO, D_CONV), F32), pltpu.VMEM((POOL_HALO, D_POOL), F32),
                        pltpu.SemaphoreType.DMA((2,))],
        compiler_params=_params(),
    )(x2d, g_mix, gw, conv_w, conv_b, ln_g, ln_b, pool_w, pool_scale, after)


def _fwd_kv(mem2d, gw, g_mem):
    rows = mem2d.shape[0]
    n_b = rows // N_MEM

    def body(mem_ref, g_ref, gw_hbm, mn_ref, kv_ref, wkv_v, sem):
        @pl.when(pl.program_id(0) == 0)
        def _():
            copies = _load_weight(gw_hbm, "w_kv", wkv_v, sem)
            for cp in copies:
                cp.start()
            for cp in copies:
                cp.wait()

        mh, _ = _rms_fwd(mem_ref[...])
        mn = (mh * g_ref[...]).astype(BF16)
        mn_ref[...] = mn
        kv_ref[...] = _dot_nt(mn, wkv_v[...]).astype(BF16)

    return pl.pallas_call(
        body, name="fwd_kv", grid=(n_b,),
        in_specs=[pl.BlockSpec((N_MEM, D_MODEL), lambda b: (b, 0)), _full((1, D_MODEL)), pl.BlockSpec(memory_space=pl.ANY)],
        out_specs=[pl.BlockSpec((N_MEM, D_MODEL), lambda b: (b, 0)), pl.BlockSpec((N_MEM, 2 * D_MODEL), lambda b: (b, 0))],
        out_shape=[jax.ShapeDtypeStruct((rows, D_MODEL), BF16), jax.ShapeDtypeStruct((rows, 2 * D_MODEL), BF16)],
        scratch_shapes=[pltpu.VMEM((2 * D_MODEL, D_MODEL), BF16), pltpu.SemaphoreType.DMA],
        compiler_params=_params(),
    )(mem2d, g_mem, gw)


def _softmax_rows(s):
    e = jnp.exp(s - jnp.max(s, axis=-1, keepdims=True))
    return e / jnp.sum(e, axis=-1, keepdims=True)


def _fwd_attn(x1, kv, gw, g_x, seq, tm):
    tokens = x1.shape[0]
    n_tiles = tokens // tm
    tps = seq // tm

    def body(x1_ref, kv_ref, g_ref, gw_hbm, x2_ref, h2_ref, q_ref, o_ref, wq_v, wo_v, sem):
        _start_weights(gw_hbm, ("w_q", "w_o"), (wq_v, wo_v), sem)
        x1v = x1_ref[...]
        xh, _ = _rms_fwd(x1v)
        h2 = (xh * g_ref[...]).astype(BF16)
        h2_ref[...] = h2
        q = (_dot(h2, wq_v[...]) * (HEAD_DIM ** -0.5)).astype(BF16)
        q_ref[...] = q
        heads = [slice(h * HEAD_DIM, (h + 1) * HEAD_DIM) for h in range(HEADS)]
        scores = [_dot_nt(q[:, hd], kv_ref[:, hd]) for hd in heads]
        probs = [_softmax_rows(s).astype(BF16) for s in scores]
        outs = [_dot(p, kv_ref[:, pl.ds(D_MODEL + h * HEAD_DIM, HEAD_DIM)]) for h, p in enumerate(probs)]
        o = jnp.concatenate(outs, axis=-1).astype(BF16)
        o_ref[...] = o
        x2_ref[...] = x1v + _dot(o, wo_v[...])

    row = lambda w: pl.BlockSpec((tm, w), lambda i: (i, 0))
    return pl.pallas_call(
        body, name="fwd_attn", grid=(n_tiles,),
        in_specs=[row(D_MODEL), pl.BlockSpec((N_MEM, 2 * D_MODEL), lambda i: (i // tps, 0)), _full((1, D_MODEL)),
                  pl.BlockSpec(memory_space=pl.ANY)],
        out_specs=[row(D_MODEL)] * 4,
        out_shape=[jax.ShapeDtypeStruct((tokens, D_MODEL), F32)] + [jax.ShapeDtypeStruct((tokens, D_MODEL), BF16)] * 3,
        scratch_shapes=[pltpu.VMEM((D_MODEL, D_MODEL), BF16), pltpu.VMEM((D_MODEL, D_MODEL), BF16), pltpu.SemaphoreType.DMA((2,))],
        compiler_params=_params(),
    )(x1, kv, g_x, gw)


def _ffn_conv(uu, halo, w_ref, b_ref, cols):
    ext = jnp.concatenate([halo, uu], axis=0)
    p1 = pltpu.roll(ext, 1, 0)[FFN_HALO:, :]
    p2 = pltpu.roll(ext, 2, 0)[FFN_HALO:, :]
    return b_ref[:, cols] + w_ref[2:3, cols] * uu + w_ref[1:2, cols] * p1 + w_ref[0:1, cols] * p2


def _fwd_ffn(x2, target, gw, g_ffn, ffn_w, ffn_b, g_final, seq, tm):
    tokens = x2.shape[0]
    n_tiles = tokens // tm
    tps = seq // tm
    n_chunks = D_FF // FFN_CHUNK

    def body(x2_ref, tgt_ref, gffn_ref, gw_hbm, fw_ref, fb_ref, gfin_ref,
             uu_ref, cc_ref, a_ref, h3_ref, dx3_ref, dx3b_ref, loss_ref, dgfin_ref,
             wup_v, wdown_v, carry, sem):
        i = pl.program_id(0)

        _start_weights(gw_hbm, ("w_up", "w_down"), (wup_v, wdown_v), sem)

        @pl.when(i == 0)
        def _():
            loss_ref[...] = jnp.zeros_like(loss_ref)
            dgfin_ref[...] = jnp.zeros_like(dgfin_ref)

        @pl.when(i % tps == 0)
        def _():
            carry[...] = jnp.zeros_like(carry)

        x2v = x2_ref[...]
        xh, _ = _rms_fwd(x2v)
        h3 = (xh * gffn_ref[...]).astype(BF16)
        h3_ref[...] = h3
        acc = jnp.zeros((tm, D_MODEL), F32)
        for jc in range(n_chunks):
            halves = []
            for half in range(2):
                cols = pl.ds(half * D_FF + jc * FFN_CHUNK, FFN_CHUNK)
                uu = _dot_nt(h3, wup_v[cols, :])
                uu_ref[:, cols] = uu.astype(BF16)
                cc = _ffn_conv(uu, carry[:, cols], fw_ref, fb_ref, cols)
                cc_ref[:, cols] = cc.astype(BF16)
                halves.append(cc)
                carry[:, cols] = uu[tm - FFN_HALO:, :]
            gate, val = halves
            a = (gate * _sigmoid(gate) * val).astype(BF16)
            a_ref[:, pl.ds(jc * FFN_CHUNK, FFN_CHUNK)] = a
            acc = acc + _dot(a, wdown_v[pl.ds(jc * FFN_CHUNK, FFN_CHUNK), :])
        x3 = x2v + acc

        xh3, r3 = _rms_fwd(x3)
        gfin = gfin_ref[...]
        err = xh3 * gfin - tgt_ref[...]
        loss_ref[...] += jnp.full(loss_ref.shape, jnp.sum(err * err) * (0.5 / D_MODEL), F32)
        dy = err * (1.0 / D_MODEL)
        dgfin_ref[...] += _colsum(dy * xh3)
        dx3 = _rms_bwd(dy, xh3, r3, gfin)
        dx3_ref[...] = dx3
        dx3b_ref[...] = dx3.astype(BF16)

    row = lambda w: pl.BlockSpec((tm, w), lambda i: (i, 0))
    return pl.pallas_call(
        body, name="fwd_ffn", grid=(n_tiles,),
        in_specs=[row(D_MODEL), row(D_MODEL), _full((1, D_MODEL)), pl.BlockSpec(memory_space=pl.ANY),
                  _full((FFN_CONV_WIDTH, 2 * D_FF)), _full((1, 2 * D_FF)), _full((1, D_MODEL))],
        out_specs=[row(2 * D_FF), row(2 * D_FF), row(D_FF), row(D_MODEL), row(D_MODEL), row(D_MODEL), _full((8, 128)),
                   _full((1, D_MODEL))],
        out_shape=[jax.ShapeDtypeStruct((tokens, 2 * D_FF), BF16), jax.ShapeDtypeStruct((tokens, 2 * D_FF), BF16),
                   jax.ShapeDtypeStruct((tokens, D_FF), BF16),
                   jax.ShapeDtypeStruct((tokens, D_MODEL), BF16), jax.ShapeDtypeStruct((tokens, D_MODEL), F32),
                   jax.ShapeDtypeStruct((tokens, D_MODEL), BF16),
                   jax.ShapeDtypeStruct((8, 128), F32), jax.ShapeDtypeStruct((1, D_MODEL), F32)],
        scratch_shapes=[pltpu.VMEM((2 * D_FF, D_MODEL), BF16), pltpu.VMEM((D_FF, D_MODEL), BF16),
                        pltpu.VMEM((FFN_HALO, 2 * D_FF), F32), pltpu.SemaphoreType.DMA((2,))],
        compiler_params=_params(),
    )(x2, target, g_ffn, gw, ffn_w, ffn_b, g_final)


def _bwd_ffn(dx3, x2, uu_all, cc_all, gw, g_ffn, ffn_w, seq, tm):
    tokens = x2.shape[0]
    n_tiles = tokens // tm
    tps = seq // tm
    n_chunks = D_FF // FFN_CHUNK

    def body(dx3_ref, x2_ref, uu_ref, cc_ref, gffn_ref, gw_hbm, fw_ref,
             dx2_ref, dx2b_ref, duu_ref, dfb_ref, dfw_ref, dg_ref,
             wup_v, wdown_v, carry, sem):
        i = pl.program_id(0)
        t = n_tiles - 1 - i

        _start_weights(gw_hbm, ("w_down", "w_up"), (wdown_v, wup_v), sem)

        @pl.when(i == 0)
        def _():
            dfb_ref[...] = jnp.zeros_like(dfb_ref)
            dfw_ref[...] = jnp.zeros_like(dfw_ref)
            dg_ref[...] = jnp.zeros_like(dg_ref)

        @pl.when(t % tps == tps - 1)
        def _():
            carry[...] = jnp.zeros_like(carry)

        dx3v = dx3_ref[...]
        dx3b = dx3v.astype(BF16)
        dh3 = jnp.zeros((tm, D_MODEL), F32)
        for jc in range(n_chunks):
            da = _dot_nt(dx3b, wdown_v[pl.ds(jc * FFN_CHUNK, FFN_CHUNK), :])
            colss = [pl.ds(half * D_FF + jc * FFN_CHUNK, FFN_CHUNK) for half in range(2)]
            gate, val = [cc_ref[:, cols].astype(F32) for cols in colss]
            sg = _sigmoid(gate)
            dgate = da * val * (sg * (1.0 + gate * (1.0 - sg)))
            dval = da * (gate * sg)
            for dcc, cols in zip((dgate, dval), colss):
                uu = uu_ref[:, cols].astype(F32)
                dfb_ref[:, cols] += _colsum(dcc)
                ext = jnp.concatenate([dcc, carry[:, cols]], axis=0)
                carry[:, cols] = dcc[:FFN_HALO, :]
                n1 = pltpu.roll(ext, tm + FFN_HALO - 1, 0)[:tm, :]
                n2 = pltpu.roll(ext, tm + FFN_HALO - 2, 0)[:tm, :]
                duu = fw_ref[2:3, cols] * dcc + fw_ref[1:2, cols] * n1 + fw_ref[0:1, cols] * n2
                dfw_ref[2:3, cols] += _colsum(uu * dcc)
                dfw_ref[1:2, cols] += _colsum(uu * n1)
                dfw_ref[0:1, cols] += _colsum(uu * n2)
                duub = duu.astype(BF16)
                duu_ref[:, cols] = duub
                dh3 = dh3 + _dot(duub, wup_v[cols, :])
        xh, r = _rms_fwd(x2_ref[...])
        dg_ref[...] += _colsum(dh3 * xh)
        dx2 = dx3v + _rms_bwd(dh3, xh, r, gffn_ref[...])
        dx2_ref[...] = dx2
        dx2b_ref[...] = dx2.astype(BF16)

    rev = lambda w: pl.BlockSpec((tm, w), lambda i: (n_tiles - 1 - i, 0))
    return pl.pallas_call(
        body, name="bwd_ffn", grid=(n_tiles,),
        in_specs=[rev(D_MODEL), rev(D_MODEL), rev(2 * D_FF), rev(2 * D_FF), _full((1, D_MODEL)),
                  pl.BlockSpec(memory_space=pl.ANY), _full((FFN_CONV_WIDTH, 2 * D_FF))],
        out_specs=[rev(D_MODEL), rev(D_MODEL), rev(2 * D_FF), _full((1, 2 * D_FF)), _full((FFN_CONV_WIDTH, 2 * D_FF)),
                   _full((1, D_MODEL))],
        out_shape=[jax.ShapeDtypeStruct((tokens, D_MODEL), F32), jax.ShapeDtypeStruct((tokens, D_MODEL), BF16),
                   jax.ShapeDtypeStruct((tokens, 2 * D_FF), BF16),
                   jax.ShapeDtypeStruct((1, 2 * D_FF), F32), jax.ShapeDtypeStruct((FFN_CONV_WIDTH, 2 * D_FF), F32),
                   jax.ShapeDtypeStruct((1, D_MODEL), F32)],
        scratch_shapes=[pltpu.VMEM((2 * D_FF, D_MODEL), BF16), pltpu.VMEM((D_FF, D_MODEL), BF16),
                        pltpu.VMEM((FFN_HALO, 2 * D_FF), F32), pltpu.SemaphoreType.DMA((2,))],
        compiler_params=_params(),
    )(dx3, x2, uu_all, cc_all, g_ffn, gw, ffn_w)


def _bwd_attn(dx2, x1, q, kv, gw, g_x, after, seq, tm):
    tokens = x1.shape[0]
    n_tiles = tokens // tm
    tps = seq // tm
    n_b = tokens // seq

    def body(dx2_ref, x1_ref, q_ref, kv_ref, g_ref, gw_hbm, after_ref, dx1_ref, dx1b_ref, dq_ref, dkv_ref, dg_ref,
             wq_v, wo_v, sem):
        del after_ref
        i = pl.program_id(0)

        _start_weights(gw_hbm, ("w_o", "w_q"), (wo_v, wq_v), sem)

        @pl.when(i == 0)
        def _():
            dg_ref[...] = jnp.zeros_like(dg_ref)

        @pl.when(i % tps == 0)
        def _():
            dkv_ref[...] = jnp.zeros_like(dkv_ref)

        dx2v = dx2_ref[...]
        do = _dot_nt(dx2v.astype(BF16), wo_v[...]).astype(BF16)
        q = q_ref[...]
        heads = [slice(h * HEAD_DIM, (h + 1) * HEAD_DIM) for h in range(HEADS)]
        kcols = [pl.ds(h * HEAD_DIM, HEAD_DIM) for h in range(HEADS)]
        vcols = [pl.ds(D_MODEL + h * HEAD_DIM, HEAD_DIM) for h in range(HEADS)]
        scores = [_dot_nt(q[:, hd], kv_ref[:, kc]) for hd, kc in zip(heads, kcols)]
        dps = [_dot_nt(do[:, hd], kv_ref[:, vc]) for hd, vc in zip(heads, vcols)]
        probs = [_softmax_rows(s) for s in scores]
        dss = [(p * (dp - jnp.sum(dp * p, axis=-1, keepdims=True))).astype(BF16) for p, dp in zip(probs, dps)]
        for p, hd, vc in zip(probs, heads, vcols):
            dkv_ref[:, vc] += _dot_tn(p.astype(BF16), do[:, hd])
        dqs = [_dot(ds, kv_ref[:, kc]) * (HEAD_DIM ** -0.5) for ds, kc in zip(dss, kcols)]
        for ds, hd, kc in zip(dss, heads, kcols):
            dkv_ref[:, kc] += _dot_tn(ds, q[:, hd])
        dq = jnp.concatenate(dqs, axis=-1).astype(BF16)
        dq_ref[...] = dq
        dh2 = _dot_nt(dq, wq_v[...])
        xh, r = _rms_fwd(x1_ref[...])
        dg_ref[...] += _colsum(dh2 * xh)
        dx1 = dx2v + _rms_bwd(dh2, xh, r, g_ref[...])
        dx1_ref[...] = dx1
        dx1b_ref[...] = dx1.astype(BF16)

    row = lambda w: pl.BlockSpec((tm, w), lambda i: (i, 0))
    per_b = pl.BlockSpec((N_MEM, 2 * D_MODEL), lambda i: (i // tps, 0))
    return pl.pallas_call(
        body, name="bwd_attn", grid=(n_tiles,),
        in_specs=[row(D_MODEL), row(D_MODEL), row(D_MODEL), per_b, _full((1, D_MODEL)), pl.BlockSpec(memory_space=pl.ANY),
                  _full(after.shape)],
        out_specs=[row(D_MODEL), row(D_MODEL), row(D_MODEL), per_b, _full((1, D_MODEL))],
        out_shape=[jax.ShapeDtypeStruct((tokens, D_MODEL), F32), jax.ShapeDtypeStruct((tokens, D_MODEL), BF16),
                   jax.ShapeDtypeStruct((tokens, D_MODEL), BF16),
                   jax.ShapeDtypeStruct((n_b * N_MEM, 2 * D_MODEL), F32), jax.ShapeDtypeStruct((1, D_MODEL), F32)],
        scratch_shapes=[pltpu.VMEM((D_MODEL, D_MODEL), BF16), pltpu.VMEM((D_MODEL, D_MODEL), BF16), pltpu.SemaphoreType.DMA((2,))],
        compiler_params=_params(),
    )(dx2, x1, q, kv, g_x, gw, after)


def _bwd_kv(dkv, mem2d, gw):
    rows = mem2d.shape[0]
    n_b = rows // N_MEM

    def body(dkv_ref, mem_ref, gw_hbm, dkvb_ref, dg_ref, wkv_v, sem):
        @pl.when(pl.program_id(0) == 0)
        def _():
            copies = _load_weight(gw_hbm, "w_kv", wkv_v, sem)
            for cp in copies:
                cp.start()
            for cp in copies:
                cp.wait()
            dg_ref[...] = jnp.zeros_like(dg_ref)

        dkvb = dkv_ref[...].astype(BF16)
        dkvb_ref[...] = dkvb
        dmn = _dot(dkvb, wkv_v[...])
        mh, _ = _rms_fwd(mem_ref[...])
        dg_ref[...] += _colsum(dmn * mh)

    return pl.pallas_call(
        body, name="bwd_kv", grid=(n_b,),
        in_specs=[pl.BlockSpec((N_MEM, 2 * D_MODEL), lambda b: (b, 0)), pl.BlockSpec((N_MEM, D_MODEL), lambda b: (b, 0)),
                  pl.BlockSpec(memory_space=pl.ANY)],
        out_specs=[pl.BlockSpec((N_MEM, 2 * D_MODEL), lambda b: (b, 0)), _full((1, D_MODEL))],
        out_shape=[jax.ShapeDtypeStruct((rows, 2 * D_MODEL), BF16), jax.ShapeDtypeStruct((1, D_MODEL), F32)],
        scratch_shapes=[pltpu.VMEM((2 * D_MODEL, D_MODEL), BF16), pltpu.SemaphoreType.DMA],
        compiler_params=_params(),
    )(dkv, mem2d, gw)


def _bwd_mix(dx1, x2d, u_all, c_all, pooled_all, gw, g_mix, conv_w, ln_g, ln_b, pool_w, pool_scale, after, seq, tm):
    tokens = x2d.shape[0]
    n_tiles = tokens // tm
    tps = seq // tm

    def body(dx1_ref, x_ref, u_ref, c_ref, pooled_ref, gmix_ref, gw_hbm, cw_ref, lng_ref, lnb_ref, pw_ref, ps_ref,
             after_ref, dx_ref, du_ref, dgmix_ref, dcw_ref, dcb_ref, dlng_ref, dlnb_ref, dpw_ref, dps_ref,
             win_v, wout_v, dc_carry, e_carry, sem):
        del after_ref
        i = pl.program_id(0)
        t = n_tiles - 1 - i

        _start_weights(gw_hbm, ("w_out", "w_in"), (wout_v, win_v), sem)

        @pl.when(i == 0)
        def _():
            for ref in (dgmix_ref, dcw_ref, dcb_ref, dlng_ref, dlnb_ref, dpw_ref, dps_ref):
                ref[...] = jnp.zeros_like(ref)

        @pl.when(t % tps == tps - 1)
        def _():
            dc_carry[...] = jnp.zeros_like(dc_carry)
            e_carry[...] = jnp.zeros_like(e_carry)

        dx1v = dx1_ref[...]
        dymix = _dot_nt(dx1v.astype(BF16), wout_v[...])
        dyc, dyp = dymix[:, :D_CONV], dymix[:, D_CONV:]
        u = u_ref[...]
        val, gate = u[:, :D_CONV], u[:, D_CONV:2 * D_CONV]

        conv = c_ref[...]
        mu = jnp.mean(conv, axis=-1, keepdims=True)
        cen = conv - mu
        rs = lax.rsqrt(jnp.mean(cen * cen, axis=-1, keepdims=True) + EPS)
        chat = cen * rs
        ln = chat * lng_ref[...] + lnb_ref[...]
        sl = _sigmoid(ln)
        dln = dyc * (sl * (1.0 + ln * (1.0 - sl)))
        dlng_ref[...] += _colsum(dln * chat)
        dlnb_ref[...] += _colsum(dln)
        dchat = dln * lng_ref[...]
        dc = rs * (dchat - jnp.mean(dchat, axis=-1, keepdims=True)
                   - chat * jnp.mean(dchat * chat, axis=-1, keepdims=True))
        dcb_ref[...] += _colsum(dc)
        sg = _sigmoid(gate)
        hc = val * sg
        ext = jnp.concatenate([dc, dc_carry[...]], axis=0)
        dc_carry[...] = dc[:CONV_HALO, :]
        dhc = jnp.zeros((tm, D_CONV), F32)
        ahead_by = _sublane_shifts(ext)
        for k in range(CONV_WIDTH):
            whole, part = divmod(CONV_WIDTH - 1 - k, 8)
            tap = ahead_by[part][8 * whole:8 * whole + tm, :]
            dhc = dhc + cw_ref[k:k + 1, :] * tap
            dcw_ref[k:k + 1, :] += _colsum_mxu(hc * tap)
        du_ref[:, :D_CONV] = (dhc * sg).astype(BF16)
        du_ref[:, D_CONV:2 * D_CONV] = (dhc * val * (sg * (1.0 - sg))).astype(BF16)

        pos = lax.broadcasted_iota(jnp.int32, (tm, 1), 0) + (t % tps) * tm
        es, dpooled = [], []
        for g, w in enumerate(POOL_WINDOWS):
            cols = pl.ds(g * POOL_GROUP_DIM, POOL_GROUP_DIM)
            lo = g * POOL_GROUP_DIM
            pooled = pooled_ref[:, cols]
            pw = pw_ref[g].astype(BF16)
            dyg = dyp[:, lo:lo + POOL_GROUP_DIM]
            dps_ref[:, cols] += _colsum(dyg * _dot(pooled, pw))
            dmixed = (dyg * ps_ref[:, cols]).astype(BF16)
            dpw_ref[g] += _dot_tn(pooled, dmixed)
            dpo = _dot_nt(dmixed, pw)
            dpooled.append(dpo)
            es.append(dpo / jnp.minimum(pos + 1, w).astype(F32))
        e = jnp.concatenate(es, axis=-1)
        run = jnp.concatenate([e, e_carry[...]], axis=0)
        e_carry[...] = e[:POOL_HALO, :]
        rows = tm + POOL_HALO
        for g, w in enumerate(POOL_WINDOWS):
            lo = g * POOL_GROUP_DIM
            run = run[:, POOL_GROUP_DIM if g else 0:]
            run = run + pltpu.roll(run, rows - w // 2, 0)
            du_ref[:, 2 * D_CONV + lo:2 * D_CONV + lo + POOL_GROUP_DIM] = (
                run[:tm, :POOL_GROUP_DIM] - dpooled[g]).astype(BF16)

        dh1 = _dot(du_ref[...], win_v[...])
        xh, r = _rms_fwd(x_ref[...])
        dgmix_ref[...] += _colsum(dh1 * xh)
        dx_ref[...] = dx1v + _rms_bwd(dh1, xh, r, gmix_ref[...])

    rev = lambda w: pl.BlockSpec((tm, w), lambda i: (n_tiles - 1 - i, 0))
    return pl.pallas_call(
        body, name="bwd_mix", grid=(n_tiles,),
        in_specs=[rev(D_MODEL), rev(D_MODEL), rev(D_IN), rev(D_CONV), rev(D_POOL), _full((1, D_MODEL)),
                  pl.BlockSpec(memory_space=pl.ANY), _full((CONV_WIDTH, D_CONV)), _full((1, D_CONV)), _full((1, D_CONV)),
                  _full((4, POOL_GROUP_DIM, POOL_GROUP_DIM)), _full((1, D_POOL)), _full(after.shape)],
        out_specs=[rev(D_MODEL), rev(D_IN), _full((1, D_MODEL)), _full((CONV_WIDTH, D_CONV)), _full((1, D_CONV)),
                   _full((1, D_CONV)), _full((1, D_CONV)), _full((4, POOL_GROUP_DIM, POOL_GROUP_DIM)), _full((1, D_POOL))],
        out_shape=[jax.ShapeDtypeStruct((tokens, D_MODEL), F32), jax.ShapeDtypeStruct((tokens, D_IN), BF16),
                   jax.ShapeDtypeStruct((1, D_MODEL), F32), jax.ShapeDtypeStruct((CONV_WIDTH, D_CONV), F32),
                   jax.ShapeDtypeStruct((1, D_CONV), F32), jax.ShapeDtypeStruct((1, D_CONV), F32),
                   jax.ShapeDtypeStruct((1, D_CONV), F32),
                   jax.ShapeDtypeStruct((4, POOL_GROUP_DIM, POOL_GROUP_DIM), F32), jax.ShapeDtypeStruct((1, D_POOL), F32)],
        scratch_shapes=[pltpu.VMEM((D_IN, D_MODEL), BF16), pltpu.VMEM((D_MODEL, D_MODEL), BF16),
                        pltpu.VMEM((CONV_HALO, D_CONV), F32), pltpu.VMEM((POOL_HALO, D_POOL), F32),
                        pltpu.SemaphoreType.DMA((2,))],
        compiler_params=_params(),
    )(dx1, x2d, u_all, c_all, pooled_all, g_mix, gw, conv_w, ln_g, ln_b, pool_w, pool_scale, after)


def _wgrad(a, b, name, after=None):
    tokens, m = a.shape
    n = b.shape[1]
    tm = 512 if m % 512 == 0 else 256
    extra = [] if after is None else [after]

    def body(a_ref, b_ref, *rest):
        rest[-1][...] = _dot_tn(a_ref[...], b_ref[...]).astype(rest[-1].dtype)

    return pl.pallas_call(
        body, name=name, grid=(m // tm,),
        in_specs=[pl.BlockSpec((tokens, tm), lambda i: (0, i)), _full((tokens, n))] + [_full(t.shape) for t in extra],
        out_specs=pl.BlockSpec((tm, n), lambda i: (i, 0)),
        out_shape=jax.ShapeDtypeStruct((m, n), BF16),
        compiler_params=_params(),
    )(a, b, *extra)


def _adamw_update(w, g, m, v):
    nm = ADAM_B1 * m + (1.0 - ADAM_B1) * g
    nv = ADAM_B2 * v + (1.0 - ADAM_B2) * (g * g)
    m_hat = nm / (1.0 - ADAM_B1 ** ADAM_STEP)
    v_hat = nv / (1.0 - ADAM_B2 ** ADAM_STEP)
    return -ADAM_LR * (m_hat / (jnp.sqrt(v_hat) + ADAM_EPS) + ADAM_WD * w), nm, nv


def _adamw_small(ws, gs, ms, vs):
    n = len(ws)

    def body(*refs):
        ins, outs = refs[:4 * n], refs[4 * n:]
        for k in range(n):
            d, nm, nv = _adamw_update(*[ins[j * n + k][...] for j in range(4)])
            outs[k][...] = d
            outs[n + k][...] = nm
            outs[2 * n + k][...] = nv

    vmem = pl.BlockSpec(memory_space=pltpu.VMEM)
    outs = pl.pallas_call(
        body, name="adamw_small",
        in_specs=[vmem] * (4 * n), out_specs=[vmem] * (3 * n),
        out_shape=[jax.ShapeDtypeStruct(w.shape, F32) for w in ws] * 3,
    )(*ws, *gs, *ms, *vs)
    return outs[:n], outs[n:2 * n], outs[2 * n:]


SMALL = (("norm_mix_g", (1, 1024)), ("conv_dw_b", (1, 512)), ("conv_ln_g", (1, 512)), ("conv_ln_b", (1, 512)),
         ("pool_w", (1, 4, 128, 128)), ("pool_scale", (1, 512)), ("norm_xattn_g", (1, 1024)), ("norm_mem_g", (1, 1024)),
         ("norm_ffn_g", (1, 1024)), ("ffn_dw_b", (1, 5632)), ("norm_final_g", (1024,)))
LANES = 128


def _pack_rows(arrs):
    flat = jnp.concatenate([a.reshape(-1) for a in arrs])
    pad = (-flat.shape[0]) % (8 * LANES)
    return jnp.pad(flat, (0, pad)).reshape(-1, LANES)


def kernel(x, mem, norm_mix_g, w_in, conv_dw_w, conv_dw_b, conv_ln_g, conv_ln_b, pool_w, pool_scale, w_out, norm_xattn_g, norm_mem_g, w_q, w_kv, w_o, norm_ffn_g, w_up, ffn_dw_w, ffn_dw_b, w_down, norm_final_g, loss_target, m_norm_mix_g, m_w_in, m_conv_dw_w, m_conv_dw_b, m_conv_ln_g, m_conv_ln_b, m_pool_w, m_pool_scale, m_w_out, m_norm_xattn_g, m_norm_mem_g, m_w_q, m_w_kv, m_w_o, m_norm_ffn_g, m_w_up, m_ffn_dw_w, m_ffn_dw_b, m_w_down, m_norm_final_g, v_norm_mix_g, v_w_in, v_conv_dw_w, v_conv_dw_b, v_conv_ln_g, v_conv_ln_b, v_pool_w, v_pool_scale, v_w_out, v_norm_xattn_g, v_norm_mem_g, v_w_q, v_w_kv, v_w_o, v_norm_ffn_g, v_w_up, v_ffn_dw_w, v_ffn_dw_b, v_w_down, v_norm_final_g):
    weights = dict(norm_mix_g=norm_mix_g, w_in=w_in, conv_dw_w=conv_dw_w, conv_dw_b=conv_dw_b, conv_ln_g=conv_ln_g,
                   conv_ln_b=conv_ln_b, pool_w=pool_w, pool_scale=pool_scale, w_out=w_out, norm_xattn_g=norm_xattn_g,
                   norm_mem_g=norm_mem_g, w_q=w_q, w_kv=w_kv, w_o=w_o, norm_ffn_g=norm_ffn_g, w_up=w_up,
                   ffn_dw_w=ffn_dw_w, ffn_dw_b=ffn_dw_b, w_down=w_down, norm_final_g=norm_final_g)
    moments_m = dict(norm_mix_g=m_norm_mix_g, w_in=m_w_in, conv_dw_w=m_conv_dw_w, conv_dw_b=m_conv_dw_b,
                     conv_ln_g=m_conv_ln_g, conv_ln_b=m_conv_ln_b, pool_w=m_pool_w, pool_scale=m_pool_scale,
                     w_out=m_w_out, norm_xattn_g=m_norm_xattn_g, norm_mem_g=m_norm_mem_g, w_q=m_w_q, w_kv=m_w_kv,
                     w_o=m_w_o, norm_ffn_g=m_norm_ffn_g, w_up=m_w_up, ffn_dw_w=m_ffn_dw_w, ffn_dw_b=m_ffn_dw_b,
                     w_down=m_w_down, norm_final_g=m_norm_final_g)
    moments_v = dict(norm_mix_g=v_norm_mix_g, w_in=v_w_in, conv_dw_w=v_conv_dw_w, conv_dw_b=v_conv_dw_b,
                     conv_ln_g=v_conv_ln_g, conv_ln_b=v_conv_ln_b, pool_w=v_pool_w, pool_scale=v_pool_scale,
                     w_out=v_w_out, norm_xattn_g=v_norm_xattn_g, norm_mem_g=v_norm_mem_g, w_q=v_w_q, w_kv=v_w_kv,
                     w_o=v_w_o, norm_ffn_g=v_norm_ffn_g, w_up=v_w_up, ffn_dw_w=v_ffn_dw_w, ffn_dw_b=v_ffn_dw_b,
                     w_down=v_w_down, norm_final_g=v_norm_final_g)
    order = list(weights)
    transposed = ("w_in", "w_kv", "w_up")

    n_b, seq, _ = x.shape
    tokens = n_b * seq
    tm_mix = min(512, seq // 2)
    tm_attn = min(1024, seq // 2)
    tm_ffn = min(256, seq // 2)
    dev = 4 * lax.axis_index("x") + 2 * lax.axis_index("y") + lax.axis_index("c")

    packs = [jnp.concatenate([weights[n][0].T if n in transposed else weights[n][0] for n in names], axis=0).astype(BF16)
             for names in AG_GROUPS]
    small_sharded = _pack_rows([conv_dw_w[0], ffn_dw_w[0]])
    gw_mix, gsmall = _all_gather([packs[0], small_sharded], "weights_all_gather")
    flights = []
    after = gw_mix
    for k in (1, 2):
        own_in_place = lax.dynamic_update_slice(lax.empty((N_DEV,) + packs[k].shape, BF16), packs[k][None], (dev, 0, 0))
        flights.append(_gather_start(own_in_place, after, "weights_gather_start_%d" % k, BARRIER_IDS["gather_start"][k - 1]))
        after = flights[-1][3]
    gflat = gsmall.reshape(N_DEV, -1)
    n_cw = CONV_WIDTH * (D_CONV // N_DEV)
    n_fw = FFN_CONV_WIDTH * (2 * D_FF // N_DEV)
    conv_w = gflat[:, :n_cw].reshape(N_DEV, CONV_WIDTH, D_CONV // N_DEV).transpose(1, 0, 2).reshape(CONV_WIDTH, D_CONV)
    ffn_w = gflat[:, n_cw:n_cw + n_fw].reshape(N_DEV, FFN_CONV_WIDTH, 2 * D_FF // N_DEV).transpose(1, 0, 2).reshape(
        FFN_CONV_WIDTH, 2 * D_FF)

    x2d = x.reshape(tokens, D_MODEL)
    mem2d = mem.reshape(n_b * N_MEM, D_MODEL)
    tgt2d = loss_target.reshape(tokens, D_MODEL)
    g_final = norm_final_g.reshape(1, D_MODEL)

    def gather_finish(flight, after, tag):
        fwd_send, fwd_recv, buf = _gather_forward(*flight[:3], after, "weights_gather_forward_" + tag,
                                                  BARRIER_IDS["gather_forward"][int(tag) - 1])
        return _gather_finish(fwd_send, fwd_recv, buf, "weights_gather_finish_" + tag)

    x1, u_all, c_all, pooled_all, ymix, h1 = _fwd_mix(
        x2d, gw_mix, norm_mix_g, conv_w, conv_dw_b, conv_ln_g, conv_ln_b, pool_w[0], pool_scale, flights[1][3],
        seq, tm_mix)
    gw_attn = gather_finish(flights[0], x1, "1")
    mem_n, kv = _fwd_kv(mem2d, gw_attn, norm_mem_g)
    x2, h2, q, o = _fwd_attn(x1, kv, gw_attn, norm_xattn_g, seq, tm_attn)
    gw_ffn = gather_finish(flights[1], x2, "2")
    uu_all, cc_all, a_all, h3, dx3, dx3b, loss_part, dg_final = _fwd_ffn(
        x2, tgt2d, gw_ffn, norm_ffn_g, ffn_w, ffn_dw_b, g_final, seq, tm_ffn)

    table = _owner_table()

    def sibling_start(names, tag):
        parts = [part[n].reshape(N_DEV, W_OFF[n][1], D_MODEL) for n in names]
        return _exchange_start(parts, 4, _to_sibling, "rs_sibling_exchange_start_" + tag, BARRIER_IDS["sibling"][tag])

    def chips_start(flight, after, tag):
        parts, landed = _exchange_wait(*flight[:4], after, 4, _to_sibling, "rs_sibling_exchange_wait_" + tag)
        sums = _chip_partial_sums(table, parts, landed, "rs_chip_partial_sums_" + tag)
        return parts, landed, _exchange_start(sums, 3, _to_chip, "rs_chip_exchange_start_" + tag,
                                              BARRIER_IDS["chips"][tag])

    grads, delta, new_m, new_v = {}, {}, {}, {}

    def reduce_finish(names, parts, landed, flight, after, tag):
        _, from_chips = _exchange_wait(*flight[:4], after, 3, _to_chip, "rs_chip_exchange_wait_" + tag)
        as_rows = {n: n in transposed and W_OFF[n][1] % LANES != 0 for n in names}
        states = [tuple(t[n][0].T if as_rows[n] else t[n][0] for t in (weights, moments_m, moments_v)) for n in names]
        results = _final_update(table, parts, landed, from_chips, states, "rs_final_update_" + tag)
        for n, res in zip(names, results):
            grads[n], delta[n], new_m[n], new_v[n] = [t.T[None] if as_rows[n] else t[None] for t in res]
        return delta[names[-1]]

    part = {}
    dx2, dx2b, duu, d_ffn_b, d_ffn_w, dg_ffn = _bwd_ffn(dx3, x2, uu_all, cc_all, gw_ffn, norm_ffn_g, ffn_w, seq, tm_ffn)
    part["w_up"] = _wgrad(duu, h3, "wgrad_w_up")
    part["w_down"] = _wgrad(a_all, dx3b, "wgrad_w_down")
    to_sibling_a = sibling_start(RS_GROUPS["a"], "a")
    dx1, dx1b, dq, dkv, dg_x = _bwd_attn(dx2, x1, q, kv, gw_attn, norm_xattn_g, to_sibling_a[4], seq, tm_mix)
    parts_a, landed_a, flight_a = chips_start(to_sibling_a, dx1, "a")
    dkv_b, dg_mem = _bwd_kv(dkv, mem2d, gw_attn)
    part["w_q"] = _wgrad(h2, dq, "wgrad_w_q", after=flight_a[4])
    part["w_kv"] = _wgrad(dkv_b, mem_n, "wgrad_w_kv")
    part["w_o"] = _wgrad(o, dx2b, "wgrad_w_o")
    part["w_out"] = _wgrad(ymix, dx1b, "wgrad_w_out")
    to_sibling_b = sibling_start(RS_GROUPS["b"], "b")
    parts_b, landed_b, flight_b = chips_start(to_sibling_b, to_sibling_b[4], "b")
    dx, du, dg_mix, d_conv_w, d_conv_b, d_ln_g, d_ln_b, d_pool_w, d_pool_scale = _bwd_mix(
        dx1, x2d, u_all, c_all, pooled_all, gw_mix, norm_mix_g, conv_w, conv_ln_g, conv_ln_b, pool_w[0], pool_scale,
        flight_b[4], seq, tm_mix)
    grad_x = dx.reshape(x.shape)

    small_grads = dict(norm_mix_g=dg_mix, conv_dw_b=d_conv_b, conv_ln_g=d_ln_g, conv_ln_b=d_ln_b, pool_w=d_pool_w,
                       pool_scale=d_pool_scale, norm_xattn_g=dg_x, norm_mem_g=dg_mem, norm_ffn_g=dg_ffn,
                       ffn_dw_b=d_ffn_b, norm_final_g=dg_final)
    small_list = [small_grads[n] for n, _ in SMALL] + [d_conv_w, d_ffn_w, loss_part[:1]]
    small_mine = _pack_rows(small_list)
    small_flight = _broadcast_start(
        lax.dynamic_update_slice(lax.empty((N_DEV,) + small_mine.shape, F32), small_mine[None], (dev, 0, 0)),
        "small_grads_broadcast_start", BARRIER_IDS["broadcast"])

    part["w_in"] = _wgrad(du, h1, "wgrad_w_in", after=small_flight[3])
    to_sibling_c = sibling_start(RS_GROUPS["c"], "c")
    parts_c, landed_c, flight_c = chips_start(to_sibling_c, to_sibling_c[4], "c")
    updated_a = reduce_finish(RS_GROUPS["a"], parts_a, landed_a, flight_a, flight_c[4], "a")
    updated_b = reduce_finish(RS_GROUPS["b"], parts_b, landed_b, flight_b, updated_a, "b")
    small_all = _broadcast_wait(*small_flight[:3], updated_b, "small_grads_broadcast_wait")
    small_sum = _sum_blocks(small_all).reshape(-1)

    pos = 0
    for n, shape in SMALL:
        size = 1
        for s in shape:
            size *= s
        grads[n] = small_sum[pos:pos + size].reshape(shape)
        pos += size
    full_conv_w = small_sum[pos:pos + CONV_WIDTH * D_CONV].reshape(CONV_WIDTH, D_CONV)
    pos += CONV_WIDTH * D_CONV
    full_ffn_w = small_sum[pos:pos + FFN_CONV_WIDTH * 2 * D_FF].reshape(FFN_CONV_WIDTH, 2 * D_FF)
    loss = small_sum[pos + FFN_CONV_WIDTH * 2 * D_FF]
    grads["conv_dw_w"] = lax.dynamic_slice_in_dim(full_conv_w, dev * (D_CONV // N_DEV), D_CONV // N_DEV, axis=1)[None]
    grads["ffn_dw_w"] = lax.dynamic_slice_in_dim(full_ffn_w, dev * (2 * D_FF // N_DEV), 2 * D_FF // N_DEV, axis=1)[None]

    small_names = [n for n in order if n not in W_OFF]
    swap = lambda t: jnp.transpose(t, (1, 0, 2))
    two_d = lambda t: t.reshape(1, -1) if t.ndim == 1 else (swap(t) if t.ndim == 3 else t)
    outs = _adamw_small(*[[two_d(t[n]) for n in small_names] for t in (weights, grads, moments_m, moments_v)])
    for res, out in zip((delta, new_m, new_v), outs):
        for n, o in zip(small_names, out):
            res[n] = swap(o) if o.ndim == 3 else o.reshape(weights[n].shape)

    reduce_finish(RS_GROUPS["c"], parts_c, landed_c, flight_c, delta[small_names[-1]], "c")

    return (loss, grad_x, *[grads[n] for n in order], *[delta[n] for n in order],
            *[new_m[n] for n in order], *[new_v[n] for n in order])
```

```python
import jax
import jax.numpy as jnp
from jax import lax
from jax.experimental import pallas as pl
from jax.experimental.pallas import tpu as pltpu

F32 = jnp.float32
BF16 = jnp.bfloat16
MESH = pl.DeviceIdType.MESH

N_DEV = 8
D_MODEL = 1024
D_CONV = 512
D_POOL = 512
CONV_WIDTH = 31
POOL_WINDOWS = (2, 4, 8, 16)
POOL_GROUP_DIM = 128
D_IN = 1536
N_MEM = 256
HEADS = 4
HEAD_DIM = 256
D_FF = 2816
FFN_CONV_WIDTH = 3
EPS = 1e-6
ADAM_LR = 0.001
ADAM_B1 = 0.9
ADAM_B2 = 0.999
ADAM_EPS = 1e-08
ADAM_WD = 0.01
ADAM_STEP = 10

VMEM_LIMIT_V7X = 56 * 1024 * 1024
CONV_HALO = 32
POOL_HALO = 16
FFN_HALO = 8
FFN_CHUNK = 2816
FWD_FFN_VMEM_V7X = 62 * 1024 * 1024

W_ROWS = (("w_in", 192), ("w_out", 128), ("w_q", 128), ("w_kv", 256), ("w_o", 128), ("w_up", 704), ("w_down", 352))
AG_GROUPS = (("w_in", "w_out"), ("w_q", "w_kv", "w_o"), ("w_up", "w_down"))
W_OFF = {}
for _names in AG_GROUPS:
    _o = 0
    for _n in _names:
        W_OFF[_n] = (_o, dict(W_ROWS)[_n])
        _o += dict(W_ROWS)[_n]
RS_GROUPS = {"a": ("w_up", "w_down"), "b": ("w_q", "w_kv", "w_o", "w_out"), "c": ("w_in",)}
BARRIER_IDS = {"gather_start": (0, 1), "gather_forward": (2, 3), "sibling": {"a": 4, "b": 5, "c": 6},
               "chips": {"a": 7, "b": 8, "c": 9}, "broadcast": 10}


def _dot(a, b):
    return jnp.dot(a, b, preferred_element_type=F32)


def _dot_nt(a, b):
    return lax.dot_general(a, b, (((1,), (1,)), ((), ())), preferred_element_type=F32)


def _dot_tn(a, b):
    return lax.dot_general(a, b, (((0,), (0,)), ((), ())), preferred_element_type=F32)


def _sigmoid(v):
    return 1.0 / (1.0 + jnp.exp(-v))


def _rms_fwd(v):
    r = lax.rsqrt(jnp.mean(v * v, axis=-1, keepdims=True) + EPS)
    return v * r, r


def _rms_bwd(dh, vh, r, g):
    gd = dh * g
    return r * (gd - vh * jnp.mean(gd * vh, axis=-1, keepdims=True))


def _sublane_shifts(v):
    rows = v.shape[0]
    return [v] + [pltpu.roll(v, rows - b, 0) for b in range(1, 8)]


def _colsum(v):
    return jnp.sum(v, axis=0, keepdims=True)


def _colsum_mxu(v):
    return _dot(jnp.ones((8, v.shape[0]), BF16), v.astype(BF16))[0:1, :]


def _full(shape):
    return pl.BlockSpec(shape, lambda *_: (0,) * len(shape))


def _params(sem=("arbitrary",), vmem=VMEM_LIMIT_V7X):
    return pltpu.CompilerParams(dimension_semantics=sem, vmem_limit_bytes=vmem)


def _load_weight(g_hbm, name, dst, sem):
    off, rows = W_OFF[name]
    return [pltpu.make_async_copy(g_hbm.at[d, pl.ds(off, rows), :], dst.at[pl.ds(d * rows, rows), :], sem)
            for d in range(N_DEV)]


def _start_weights(g_hbm, names, dsts, sems):
    @pl.when(pl.program_id(0) == 0)
    def _():
        copies = [_load_weight(g_hbm, name, dst, sems.at[k]) for k, (name, dst) in enumerate(zip(names, dsts))]
        for cp in sum(copies, []):
            cp.start()
        for cp in sum(copies, []):
            cp.wait()


def _position():
    x, y, c = lax.axis_index("x"), lax.axis_index("y"), lax.axis_index("c")
    chips = [(1 - x, y), (x, 1 - y), (1 - x, 1 - y)]
    return x, y, c, chips


def _dev(px, py, pc):
    return 4 * px + 2 * py + pc


def _all_gather(arrs, name):
    n = len(arrs)

    def body(*refs):
        ins, outs = refs[:n], refs[n:2 * n]
        send_sems, recv_sems, local_sems = refs[2 * n:2 * n + 3]
        bounce = refs[2 * n + 3:]
        x, y, c, chips = _position()
        me, sibling = (x, y, c), (x, y, 1 - c)

        def copy(a, k, block, to, src=None):
            rows = outs[a].at[_dev(*block)]
            return pltpu.make_async_remote_copy(
                src_ref=rows if src is None else src, dst_ref=rows,
                send_sem=send_sems.at[a, k], recv_sem=recv_sems.at[a, k], device_id=to, device_id_type=MESH)

        sends = []
        for a in range(n):
            first = [copy(a, 0, me, sibling, src=ins[a])]
            first += [copy(a, 1 + j, me, (*chip, c), src=ins[a]) for j, chip in enumerate(chips)]
            for cp in first:
                cp.start()
            sends += first
        started = []
        for a in range(n):
            load = pltpu.make_async_copy(ins[a], bounce[a], local_sems.at[a, 0])
            load.start()
            load.wait()
            mine = pltpu.make_async_copy(bounce[a], outs[a].at[_dev(*me)], local_sems.at[a, 1])
            mine.start()
            started.append(mine)
        for j, chip in enumerate(chips):
            for a in range(n):
                copy(a, 1 + j, (*chip, c), me).wait_recv()
                passed = copy(a, 4 + j, (*chip, c), sibling)
                passed.start()
                sends.append(passed)
        for a in range(n):
            copy(a, 0, sibling, me).wait_recv()
            for j, chip in enumerate(chips):
                copy(a, 4 + j, (*chip, 1 - c), me).wait_recv()
        for cp in sends:
            cp.wait_send()
        for mine in started:
            mine.wait()

    any_spec = pl.BlockSpec(memory_space=pl.ANY)
    return pl.pallas_call(
        body, name=name,
        out_shape=[jax.ShapeDtypeStruct((N_DEV,) + a.shape, a.dtype) for a in arrs],
        in_specs=[any_spec] * n, out_specs=[any_spec] * n,
        scratch_shapes=[pltpu.SemaphoreType.DMA((n, 7)), pltpu.SemaphoreType.DMA((n, 7)), pltpu.SemaphoreType.DMA((n, 2))]
        + [pltpu.VMEM(a.shape, a.dtype) for a in arrs],
    )(*arrs)


_HBM = pl.BlockSpec(memory_space=pltpu.HBM)
_SEM = pl.BlockSpec(memory_space=pltpu.SEMAPHORE)
_SIDE_EFFECT = pltpu.SideEffectType.DATAFLOW_SIDE_EFFECTING


def _handshake(peers):
    barrier = pltpu.get_barrier_semaphore()
    for peer in peers:
        pl.semaphore_signal(barrier, inc=1, device_id=peer, device_id_type=MESH)
    pl.semaphore_wait(barrier, len(peers))


def _gather_start(buf, after, name, collective_id):
    def body(buf_ref, after_ref, send_sems, recv_sems, buf_thru, token):
        del after_ref, buf_thru
        x, y, c, chips = _position()
        rows = buf_ref.at[_dev(x, y, c)]
        targets = [(x, y, 1 - c)] + [(*chip, c) for chip in chips]
        _handshake(targets)
        for k, to in enumerate(targets):
            pltpu.make_async_remote_copy(src_ref=rows, dst_ref=rows, send_sem=send_sems.at[k], recv_sem=recv_sems.at[k],
                                         device_id=to, device_id_type=MESH).start()
        token[...] = jnp.zeros_like(token)

    return pl.pallas_call(
        body, name=name,
        out_shape=(pltpu.SemaphoreType.DMA((4,)), pltpu.SemaphoreType.DMA((4,)), pltpu.HBM(buf.shape, buf.dtype),
                   jax.ShapeDtypeStruct((8, 128), F32)),
        in_specs=(_HBM, pl.BlockSpec(memory_space=pl.ANY)),
        out_specs=(_SEM, _SEM, _HBM, pl.BlockSpec(memory_space=pltpu.VMEM)),
        input_output_aliases={0: 2},
        compiler_params=pltpu.CompilerParams(has_side_effects=_SIDE_EFFECT, collective_id=collective_id),
    )(pltpu.with_memory_space_constraint(buf, pltpu.HBM), after)


def _gather_forward(send_sems, recv_sems, buf, after, name, collective_id):
    def body(buf_ref, send_sems, recv_sems, after_ref, fwd_send, fwd_recv, buf_thru):
        del after_ref, buf_thru
        x, y, c, chips = _position()
        sibling = (x, y, 1 - c)

        def copy(block, k, sends, recvs):
            rows = buf_ref.at[_dev(*block)]
            return pltpu.make_async_remote_copy(src_ref=rows, dst_ref=rows, send_sem=sends.at[k], recv_sem=recvs.at[k],
                                                device_id=sibling, device_id_type=MESH)

        _handshake([sibling])
        for k in range(4):
            copy((x, y, c), k, send_sems, recv_sems).wait_send()
        copy(sibling, 0, send_sems, recv_sems).wait_recv()
        for j, chip in enumerate(chips):
            copy((*chip, c), 1 + j, send_sems, recv_sems).wait_recv()
            copy((*chip, c), j, fwd_send, fwd_recv).start()

    return pl.pallas_call(
        body, name=name,
        out_shape=(pltpu.SemaphoreType.DMA((3,)), pltpu.SemaphoreType.DMA((3,)), pltpu.HBM(buf.shape, buf.dtype)),
        in_specs=(_HBM, _SEM, _SEM, pl.BlockSpec(memory_space=pl.ANY)), out_specs=(_SEM, _SEM, _HBM),
        input_output_aliases={0: 2},
        compiler_params=pltpu.CompilerParams(has_side_effects=_SIDE_EFFECT, collective_id=collective_id),
    )(buf, send_sems, recv_sems, after)


def _gather_finish(fwd_send, fwd_recv, buf, name):
    def body(buf_ref, fwd_send, fwd_recv, buf_thru):
        del buf_thru
        x, y, c, chips = _position()
        for j, chip in enumerate(chips):
            cp = pltpu.make_async_remote_copy(
                src_ref=buf_ref.at[_dev(*chip, c)], dst_ref=buf_ref.at[_dev(*chip, 1 - c)], send_sem=fwd_send.at[j],
                recv_sem=fwd_recv.at[j], device_id=(x, y, 1 - c), device_id_type=MESH)
            cp.wait_send()
            cp.wait_recv()

    return pl.pallas_call(
        body, name=name,
        out_shape=pltpu.HBM(buf.shape, buf.dtype),
        in_specs=(_HBM, _SEM, _SEM), out_specs=_HBM,
        input_output_aliases={0: 0},
        compiler_params=pltpu.CompilerParams(has_side_effects=_SIDE_EFFECT),
    )(buf, fwd_send, fwd_recv)


def _everyone_else(x, y, c, chips):
    return [(x, y, 1 - c)] + [(*chip, core) for chip in chips for core in (c, 1 - c)]


def _broadcast_start(buf, name, collective_id):
    def body(buf_ref, send_sems, recv_sems, buf_thru, token):
        del buf_thru
        x, y, c, chips = _position()
        rows = buf_ref.at[_dev(x, y, c)]
        _handshake(_everyone_else(x, y, c, chips))
        for k, to in enumerate(_everyone_else(x, y, c, chips)):
            pltpu.make_async_remote_copy(src_ref=rows, dst_ref=rows, send_sem=send_sems.at[k], recv_sem=recv_sems.at[k],
                                         device_id=to, device_id_type=MESH).start()
        token[...] = jnp.zeros_like(token)

    return pl.pallas_call(
        body, name=name,
        out_shape=(pltpu.SemaphoreType.DMA((7,)), pltpu.SemaphoreType.DMA((7,)), pltpu.HBM(buf.shape, buf.dtype),
                   jax.ShapeDtypeStruct((8, 128), F32)),
        in_specs=(_HBM,), out_specs=(_SEM, _SEM, _HBM, pl.BlockSpec(memory_space=pltpu.VMEM)),
        input_output_aliases={0: 2},
        compiler_params=pltpu.CompilerParams(has_side_effects=_SIDE_EFFECT, collective_id=collective_id),
    )(pltpu.with_memory_space_constraint(buf, pltpu.HBM))


def _broadcast_wait(send_sems, recv_sems, buf, after, name):
    def body(buf_ref, send_sems, recv_sems, after_ref, buf_thru):
        del after_ref, buf_thru
        x, y, c, chips = _position()
        for k, peer in enumerate(_everyone_else(x, y, c, chips)):
            cp = pltpu.make_async_remote_copy(
                src_ref=buf_ref.at[_dev(x, y, c)], dst_ref=buf_ref.at[_dev(*peer)], send_sem=send_sems.at[k],
                recv_sem=recv_sems.at[k], device_id=peer, device_id_type=MESH)
            cp.wait_send()
            cp.wait_recv()

    return pl.pallas_call(
        body, name=name,
        out_shape=pltpu.HBM(buf.shape, buf.dtype),
        in_specs=(_HBM, _SEM, _SEM, pl.BlockSpec(memory_space=pl.ANY)), out_specs=_HBM,
        input_output_aliases={0: 0},
        compiler_params=pltpu.CompilerParams(has_side_effects=_SIDE_EFFECT),
    )(buf, send_sems, recv_sems, after)


def _to_sibling(j, x, y, c, chips):
    return _dev(*([(x, y)] + chips)[j], 1 - c), (x, y, 1 - c)


def _to_chip(j, x, y, c, chips):
    return j, (*chips[j], c)


def _exchange_start(srcs, n_slots, route, name, collective_id):
    n = len(srcs)

    def body(*refs):
        s_refs, land_refs = refs[:n], refs[n:2 * n]
        send_sems, recv_sems = refs[2 * n:2 * n + 2]
        token = refs[-1]
        x, y, c, chips = _position()
        _handshake([(x, y, 1 - c)] if route is _to_sibling else [route(j, x, y, c, chips)[1] for j in range(n_slots)])
        for k in range(n):
            for j in range(n_slots):
                block, to = route(j, x, y, c, chips)
                pltpu.make_async_remote_copy(
                    src_ref=s_refs[k].at[block], dst_ref=land_refs[k].at[j], send_sem=send_sems.at[n_slots * k + j],
                    recv_sem=recv_sems.at[n_slots * k + j], device_id=to, device_id_type=MESH).start()
        token[...] = jnp.zeros_like(token)

    lands = [jax.ShapeDtypeStruct((n_slots,) + s.shape[1:], s.dtype) for s in srcs]
    outs = pl.pallas_call(
        body, name=name,
        out_shape=(pltpu.SemaphoreType.DMA((n_slots * n,)), pltpu.SemaphoreType.DMA((n_slots * n,)),
                   *[pltpu.HBM(s.shape, s.dtype) for s in srcs], *[pltpu.HBM(l.shape, l.dtype) for l in lands],
                   jax.ShapeDtypeStruct((8, 128), F32)),
        in_specs=[_HBM] * (2 * n), out_specs=(_SEM, _SEM, *[_HBM] * (2 * n), pl.BlockSpec(memory_space=pltpu.VMEM)),
        input_output_aliases={k: 2 + k for k in range(2 * n)},
        compiler_params=pltpu.CompilerParams(has_side_effects=_SIDE_EFFECT, collective_id=collective_id),
    )(*[pltpu.with_memory_space_constraint(s, pltpu.HBM) for s in srcs],
      *[pltpu.with_memory_space_constraint(lax.empty(l.shape, l.dtype), pltpu.HBM) for l in lands])
    return outs[0], outs[1], outs[2:2 + n], outs[2 + n:2 + 2 * n], outs[-1]


def _exchange_wait(send_sems, recv_sems, s_thru, land_thru, after, n_slots, route, name):
    n = len(s_thru)

    def body(*refs):
        s_refs, land_refs = refs[:n], refs[n:2 * n]
        send_sems, recv_sems = refs[2 * n:2 * n + 2]
        x, y, c, chips = _position()
        for k in range(n):
            for j in range(n_slots):
                block, to = route(j, x, y, c, chips)
                cp = pltpu.make_async_remote_copy(
                    src_ref=s_refs[k].at[block], dst_ref=land_refs[k].at[j], send_sem=send_sems.at[n_slots * k + j],
                    recv_sem=recv_sems.at[n_slots * k + j], device_id=to, device_id_type=MESH)
                cp.wait_send()
                cp.wait_recv()

    outs = pl.pallas_call(
        body, name=name,
        out_shape=(*[pltpu.HBM(s.shape, s.dtype) for s in s_thru], *[pltpu.HBM(l.shape, l.dtype) for l in land_thru]),
        in_specs=[_HBM] * (2 * n) + [_SEM, _SEM, pl.BlockSpec(memory_space=pl.ANY)], out_specs=[_HBM] * (2 * n),
        input_output_aliases={k: k for k in range(2 * n)},
        compiler_params=pltpu.CompilerParams(has_side_effects=_SIDE_EFFECT),
    )(*s_thru, *land_thru, send_sems, recv_sems, after)
    return outs[:n], outs[n:]


def _owner_table():
    x, y, c = lax.axis_index("x"), lax.axis_index("y"), lax.axis_index("c")
    chips = [(x, y), (1 - x, y), (x, 1 - y), (1 - x, 1 - y)]
    return jnp.stack([_dev(px, py, c) for px, py in chips]).astype(jnp.int32)


def _chip_partial_sums(table, parts, from_sibling, name):
    n = len(parts)

    def body(tab_ref, *refs):
        del tab_ref
        for g_ref, l_ref, out_ref in zip(refs[:n], refs[n:2 * n], refs[2 * n:]):
            out_ref[...] = (g_ref[...].astype(F32) + l_ref[...].astype(F32)).astype(out_ref.dtype)

    block = lambda p: (None,) + p.shape[1:]
    grid_spec = pltpu.PrefetchScalarGridSpec(
        num_scalar_prefetch=1, grid=(3,),
        in_specs=[pl.BlockSpec(block(p), lambda j, tab: (tab[j + 1], 0, 0)) for p in parts]
        + [pl.BlockSpec(block(p), lambda j, tab: (j + 1, 0, 0)) for p in parts],
        out_specs=[pl.BlockSpec(block(p), lambda j, tab: (j, 0, 0)) for p in parts])
    return pl.pallas_call(
        body, name=name, grid_spec=grid_spec,
        out_shape=[jax.ShapeDtypeStruct((3,) + p.shape[1:], BF16) for p in parts],
        compiler_params=_params(("arbitrary",)),
    )(table, *parts, *from_sibling)


def _final_update(table, parts, from_sibling, from_chips, states, name):
    n = len(parts)
    flipped = [states[k][0].shape != parts[k].shape[1:] for k in range(n)]

    def body(tab_ref, *refs):
        del tab_ref
        ins, outs = refs[:6 * n], refs[6 * n:]
        for k in range(n):
            acc = ins[k][...].astype(F32) + ins[n + k][...].astype(F32)
            for j in range(3):
                acc = acc + ins[2 * n + k][j].astype(F32)
            if flipped[k]:
                acc = acc.T
            w_ref, m_ref, v_ref = ins[3 * n + 3 * k:3 * n + 3 * k + 3]
            outs[4 * k][...] = acc
            for out_ref, val in zip(outs[4 * k + 1:4 * k + 4], _adamw_update(w_ref[...], acc, m_ref[...], v_ref[...])):
                out_ref[...] = val

    def grad_block(k, lead, at):
        r, c = parts[k].shape[1:]
        if flipped[k]:
            return pl.BlockSpec(lead + (r, c // 2), lambda t, tab: (*at(tab), 0, t))
        return pl.BlockSpec(lead + (r // 2, c), lambda t, tab: (*at(tab), t, 0))

    def state_block(k):
        a, b = states[k][0].shape
        return pl.BlockSpec((a // 2, b), lambda t, tab: (t, 0))

    grid_spec = pltpu.PrefetchScalarGridSpec(
        num_scalar_prefetch=1, grid=(2,),
        in_specs=[grad_block(k, (None,), lambda tab: (tab[0],)) for k in range(n)]
        + [grad_block(k, (None,), lambda tab: (0,)) for k in range(n)]
        + [grad_block(k, (3,), lambda tab: (0,)) for k in range(n)]
        + [state_block(k) for k in range(n) for _ in range(3)],
        out_specs=[state_block(k) for k in range(n) for _ in range(4)])
    outs = pl.pallas_call(
        body, name=name, grid_spec=grid_spec,
        out_shape=[jax.ShapeDtypeStruct(states[k][0].shape, F32) for k in range(n) for _ in range(4)],
        compiler_params=_params(("arbitrary",)),
    )(table, *parts, *from_sibling, *from_chips, *[t for k in range(n) for t in states[k]])
    return [outs[4 * k:4 * k + 4] for k in range(n)]


def _sum_blocks(g8):
    _, rows, cols = g8.shape

    def body(g_ref, out_ref):
        acc = g_ref[0]
        for d in range(1, N_DEV):
            acc = acc + g_ref[d]
        out_ref[...] = acc

    return pl.pallas_call(
        body, name="small_grad_sum", grid=(1,),
        in_specs=[_full((N_DEV, rows, cols))], out_specs=_full((rows, cols)),
        out_shape=jax.ShapeDtypeStruct((rows, cols), F32),
        compiler_params=_params(("arbitrary",)),
    )(g8)


def _fwd_mix(x2d, gw, g_mix, conv_w, conv_b, ln_g, ln_b, pool_w, pool_scale, after, seq, tm):
    tokens = x2d.shape[0]
    n_tiles = tokens // tm
    tps = seq // tm

    def body(x_ref, gmix_ref, gw_hbm, cw_ref, cb_ref, lng_ref, lnb_ref, pw_ref, ps_ref, after_ref,
             x1_ref, u_ref, c_ref, pooled_ref, ymix_ref, h1_ref,
             win_v, wout_v, hc_carry, up_carry, sem):
        del after_ref
        i = pl.program_id(0)

        _start_weights(gw_hbm, ("w_in", "w_out"), (win_v, wout_v), sem)

        @pl.when(i % tps == 0)
        def _():
            hc_carry[...] = jnp.zeros_like(hc_carry)
            up_carry[...] = jnp.zeros_like(up_carry)

        x = x_ref[...]
        xh, _ = _rms_fwd(x)
        h1 = (xh * gmix_ref[...]).astype(BF16)
        h1_ref[...] = h1
        u = _dot_nt(h1, win_v[...])
        u_ref[...] = u
        val, gate, up = u[:, :D_CONV], u[:, D_CONV:2 * D_CONV], u[:, 2 * D_CONV:]

        extp = jnp.concatenate([up_carry[...], up], axis=0)
        up_carry[...] = up[tm - POOL_HALO:, :]
        pos = lax.broadcasted_iota(jnp.int32, (tm, 1), 0) + (i % tps) * tm
        run = extp
        mixed = []
        for g, w in enumerate(POOL_WINDOWS):
            lo = g * POOL_GROUP_DIM
            run = run[:, POOL_GROUP_DIM if g else 0:]
            run = run + pltpu.roll(run, w // 2, 0)
            cnt = jnp.minimum(pos + 1, w).astype(F32)
            pooled = run[POOL_HALO:, :POOL_GROUP_DIM] / cnt - up[:, lo:lo + POOL_GROUP_DIM]
            pooled = pooled.astype(BF16)
            pooled_ref[:, lo:lo + POOL_GROUP_DIM] = pooled
            mixed.append(_dot(pooled, pw_ref[g].astype(BF16)))
        y_pool = jnp.concatenate(mixed, axis=-1) * ps_ref[...]
        y_pool = y_pool.astype(BF16)
        ymix_ref[:, D_CONV:] = y_pool
        out = _dot(y_pool, wout_v[D_CONV:, :])

        hc = val * _sigmoid(gate)
        ext = jnp.concatenate([hc_carry[...], hc], axis=0)
        hc_carry[...] = hc[tm - CONV_HALO:, :]
        conv = jnp.broadcast_to(cb_ref[...], (tm, D_CONV))
        ahead_by = _sublane_shifts(ext)
        for k in range(CONV_WIDTH):
            whole, part = divmod(CONV_HALO - (CONV_WIDTH - 1) + k, 8)
            conv = conv + cw_ref[k:k + 1, :] * ahead_by[part][8 * whole:8 * whole + tm, :]
        c_ref[...] = conv
        mu = jnp.mean(conv, axis=-1, keepdims=True)
        cen = conv - mu
        ln = cen * lax.rsqrt(jnp.mean(cen * cen, axis=-1, keepdims=True) + EPS) * lng_ref[...] + lnb_ref[...]
        y_conv = ln * _sigmoid(ln)
        y_conv = y_conv.astype(BF16)
        ymix_ref[:, :D_CONV] = y_conv
        x1_ref[...] = x + (out + _dot(y_conv, wout_v[:D_CONV, :]))

    row = lambda w: pl.BlockSpec((tm, w), lambda i: (i, 0))
    return pl.pallas_call(
        body, name="fwd_mix", grid=(n_tiles,),
        in_specs=[row(D_MODEL), _full((1, D_MODEL)), pl.BlockSpec(memory_space=pl.ANY),
                  _full((CONV_WIDTH, D_CONV)), _full((1, D_CONV)), _full((1, D_CONV)), _full((1, D_CONV)),
                  _full((4, POOL_GROUP_DIM, POOL_GROUP_DIM)), _full((1, D_POOL)), _full(after.shape)],
        out_specs=[row(D_MODEL), row(D_IN), row(D_CONV), row(D_POOL), row(D_MODEL), row(D_MODEL)],
        out_shape=[jax.ShapeDtypeStruct((tokens, D_MODEL), F32), jax.ShapeDtypeStruct((tokens, D_IN), F32),
                   jax.ShapeDtypeStruct((tokens, D_CONV), F32), jax.ShapeDtypeStruct((tokens, D_POOL), BF16),
                   jax.ShapeDtypeStruct((tokens, D_MODEL), BF16), jax.ShapeDtypeStruct((tokens, D_MODEL), BF16)],
        scratch_shapes=[pltpu.VMEM((D_IN, D_MODEL), BF16), pltpu.VMEM((D_MODEL, D_MODEL), BF16),
                        pltpu.VMEM((CONV_HALO, D_CONV), F32), pltpu.VMEM((POOL_HALO, D_POOL), F32),
                        pltpu.SemaphoreType.DMA((2,))],
        compiler_params=_params(),
    )(x2d, g_mix, gw, conv_w, conv_b, ln_g, ln_b, pool_w, pool_scale, after)


def _fwd_kv(mem2d, gw, g_mem):
    rows = mem2d.shape[0]
    n_b = rows // N_MEM

    def body(mem_ref, g_ref, gw_hbm, mn_ref, kv_ref, wkv_v, sem):
        @pl.when(pl.program_id(0) == 0)
        def _():
            copies = _load_weight(gw_hbm, "w_kv", wkv_v, sem)
            for cp in copies:
                cp.start()
            for cp in copies:
                cp.wait()

        mh, _ = _rms_fwd(mem_ref[...])
        mn = (mh * g_ref[...]).astype(BF16)
        mn_ref[...] = mn
        kv_ref[...] = _dot_nt(mn, wkv_v[...]).astype(BF16)

    return pl.pallas_call(
        body, name="fwd_kv", grid=(n_b,),
        in_specs=[pl.BlockSpec((N_MEM, D_MODEL), lambda b: (b, 0)), _full((1, D_MODEL)), pl.BlockSpec(memory_space=pl.ANY)],
        out_specs=[pl.BlockSpec((N_MEM, D_MODEL), lambda b: (b, 0)), pl.BlockSpec((N_MEM, 2 * D_MODEL), lambda b: (b, 0))],
        out_shape=[jax.ShapeDtypeStruct((rows, D_MODEL), BF16), jax.ShapeDtypeStruct((rows, 2 * D_MODEL), BF16)],
        scratch_shapes=[pltpu.VMEM((2 * D_MODEL, D_MODEL), BF16), pltpu.SemaphoreType.DMA],
        compiler_params=_params(),
    )(mem2d, g_mem, gw)


def _softmax_rows(s):
    e = jnp.exp(s - jnp.max(s, axis=-1, keepdims=True))
    return e / jnp.sum(e, axis=-1, keepdims=True)


def _fwd_attn(x1, kv, gw, g_x, seq, tm):
    tokens = x1.shape[0]
    n_tiles = tokens // tm
    tps = seq // tm

    def body(x1_ref, kv_ref, g_ref, gw_hbm, x2_ref, h2_ref, q_ref, o_ref, wq_v, wo_v, sem):
        _start_weights(gw_hbm, ("w_q", "w_o"), (wq_v, wo_v), sem)
        x1v = x1_ref[...]
        xh, _ = _rms_fwd(x1v)
        h2 = (xh * g_ref[...]).astype(BF16)
        h2_ref[...] = h2
        q = (_dot(h2, wq_v[...]) * (HEAD_DIM ** -0.5)).astype(BF16)
        q_ref[...] = q
        heads = [slice(h * HEAD_DIM, (h + 1) * HEAD_DIM) for h in range(HEADS)]
        scores = [_dot_nt(q[:, hd], kv_ref[:, hd]) for hd in heads]
        probs = [_softmax_rows(s).astype(BF16) for s in scores]
        outs = [_dot(p, kv_ref[:, pl.ds(D_MODEL + h * HEAD_DIM, HEAD_DIM)]) for h, p in enumerate(probs)]
        o = jnp.concatenate(outs, axis=-1).astype(BF16)
        o_ref[...] = o
        x2_ref[...] = x1v + _dot(o, wo_v[...])

    row = lambda w: pl.BlockSpec((tm, w), lambda i: (i, 0))
    return pl.pallas_call(
        body, name="fwd_attn", grid=(n_tiles,),
        in_specs=[row(D_MODEL), pl.BlockSpec((N_MEM, 2 * D_MODEL), lambda i: (i // tps, 0)), _full((1, D_MODEL)),
                  pl.BlockSpec(memory_space=pl.ANY)],
        out_specs=[row(D_MODEL)] * 4,
        out_shape=[jax.ShapeDtypeStruct((tokens, D_MODEL), F32)] + [jax.ShapeDtypeStruct((tokens, D_MODEL), BF16)] * 3,
        scratch_shapes=[pltpu.VMEM((D_MODEL, D_MODEL), BF16), pltpu.VMEM((D_MODEL, D_MODEL), BF16), pltpu.SemaphoreType.DMA((2,))],
        compiler_params=_params(),
    )(x1, kv, g_x, gw)


def _ffn_conv(uu, halo, w_ref, b_ref, cols):
    ext = jnp.concatenate([halo, uu], axis=0)
    p1 = pltpu.roll(ext, 1, 0)[FFN_HALO:, :]
    p2 = pltpu.roll(ext, 2, 0)[FFN_HALO:, :]
    return b_ref[:, cols] + w_ref[2:3, cols] * uu + w_ref[1:2, cols] * p1 + w_ref[0:1, cols] * p2


def _fwd_ffn(x2, target, gw, g_ffn, ffn_w, ffn_b, g_final, seq, tm):
    tokens = x2.shape[0]
    n_tiles = tokens // tm
    tps = seq // tm
    n_chunks = D_FF // FFN_CHUNK
    last = n_tiles - 1

    def body(x2_ref, x2_next_ref, tgt_ref, gffn_ref, gw_hbm, fw_ref, fb_ref, gfin_ref,
             uu_ref, cc_ref, a_ref, h3_ref, dx3_ref, dx3b_ref, loss_ref, dgfin_ref,
             wup_v, wdown_v, carry, sem, h3_next, x3_s):
        i = pl.program_id(0)

        def normed(ref):
            return (_rms_fwd(ref[...])[0] * gffn_ref[...]).astype(BF16)

        def tail(counts):
            xh3, r3 = _rms_fwd(x3_s[...])
            gfin = gfin_ref[...]
            err = xh3 * gfin - tgt_ref[...]
            loss = jnp.full(loss_ref.shape, jnp.sum(err * err) * (0.5 / D_MODEL), F32)
            dy = err * (1.0 / D_MODEL)
            dgfin = _colsum(dy * xh3)
            loss_ref[...] += loss if counts is None else jnp.where(counts, loss, 0.0)
            dgfin_ref[...] += dgfin if counts is None else jnp.where(counts, dgfin, 0.0)
            dx3 = _rms_bwd(dy, xh3, r3, gfin)
            dx3_ref[...] = dx3
            dx3b_ref[...] = dx3.astype(BF16)

        _start_weights(gw_hbm, ("w_up", "w_down"), (wup_v, wdown_v), sem)

        @pl.when(i == 0)
        def _():
            loss_ref[...] = jnp.zeros_like(loss_ref)
            dgfin_ref[...] = jnp.zeros_like(dgfin_ref)
            x3_s[...] = jnp.zeros_like(x3_s)
            h3_next[...] = normed(x2_ref)

        @pl.when(i % tps == 0)
        def _():
            carry[...] = jnp.zeros_like(carry)

        @pl.when(i < n_tiles)
        def _():
            h3_ref[...] = h3_next[...]
            tail(i >= 1)
            h3_next[...] = normed(x2_next_ref)

            h3 = h3_ref[...]
            acc = jnp.zeros((tm, D_MODEL), F32)
            for jc in range(n_chunks):
                halves = []
                for half in range(2):
                    cols = pl.ds(half * D_FF + jc * FFN_CHUNK, FFN_CHUNK)
                    uu = _dot_nt(h3, wup_v[cols, :])
                    uu_ref[:, cols] = uu.astype(BF16)
                    cc = _ffn_conv(uu, carry[:, cols], fw_ref, fb_ref, cols)
                    cc_ref[:, cols] = cc.astype(BF16)
                    halves.append(cc)
                    carry[:, cols] = uu[tm - FFN_HALO:, :]
                gate, val = halves
                a = (gate * _sigmoid(gate) * val).astype(BF16)
                a_ref[:, pl.ds(jc * FFN_CHUNK, FFN_CHUNK)] = a
                acc = acc + _dot(a, wdown_v[pl.ds(jc * FFN_CHUNK, FFN_CHUNK), :])
            x3_s[...] = x2_ref[...] + acc

        @pl.when(i == n_tiles)
        def _():
            tail(None)

    at = lambda f: (lambda w: pl.BlockSpec((tm, w), lambda i: (f(i), 0)))
    this, after, before = at(lambda i: jnp.minimum(i, last)), at(lambda i: jnp.minimum(i + 1, last)), at(
        lambda i: jnp.maximum(i - 1, 0))
    return pl.pallas_call(
        body, name="fwd_ffn", grid=(n_tiles + 1,),
        in_specs=[this(D_MODEL), after(D_MODEL), before(D_MODEL), _full((1, D_MODEL)), pl.BlockSpec(memory_space=pl.ANY),
                  _full((FFN_CONV_WIDTH, 2 * D_FF)), _full((1, 2 * D_FF)), _full((1, D_MODEL))],
        out_specs=[this(2 * D_FF), this(2 * D_FF), this(D_FF), this(D_MODEL), before(D_MODEL), before(D_MODEL),
                   _full((8, 128)), _full((1, D_MODEL))],
        out_shape=[jax.ShapeDtypeStruct((tokens, 2 * D_FF), BF16), jax.ShapeDtypeStruct((tokens, 2 * D_FF), BF16),
                   jax.ShapeDtypeStruct((tokens, D_FF), BF16),
                   jax.ShapeDtypeStruct((tokens, D_MODEL), BF16), jax.ShapeDtypeStruct((tokens, D_MODEL), F32),
                   jax.ShapeDtypeStruct((tokens, D_MODEL), BF16),
                   jax.ShapeDtypeStruct((8, 128), F32), jax.ShapeDtypeStruct((1, D_MODEL), F32)],
        scratch_shapes=[pltpu.VMEM((2 * D_FF, D_MODEL), BF16), pltpu.VMEM((D_FF, D_MODEL), BF16),
                        pltpu.VMEM((FFN_HALO, 2 * D_FF), F32), pltpu.SemaphoreType.DMA((2,)),
                        pltpu.VMEM((tm, D_MODEL), BF16), pltpu.VMEM((tm, D_MODEL), F32)],
        compiler_params=_params(vmem=FWD_FFN_VMEM_V7X),
    )(x2, x2, target, g_ffn, gw, ffn_w, ffn_b, g_final)


def _bwd_ffn(dx3, x2, uu_all, cc_all, gw, g_ffn, ffn_w, seq, tm):
    tokens = x2.shape[0]
    n_tiles = tokens // tm
    tps = seq // tm
    n_chunks = D_FF // FFN_CHUNK

    def body(dx3_ref, x2_ref, uu_ref, cc_ref, gffn_ref, gw_hbm, fw_ref,
             dx2_ref, dx2b_ref, duu_ref, dfb_ref, dfw_ref, dg_ref,
             wup_v, wdown_v, carry, sem):
        i = pl.program_id(0)
        t = n_tiles - 1 - i

        _start_weights(gw_hbm, ("w_down", "w_up"), (wdown_v, wup_v), sem)

        @pl.when(i == 0)
        def _():
            dfb_ref[...] = jnp.zeros_like(dfb_ref)
            dfw_ref[...] = jnp.zeros_like(dfw_ref)
            dg_ref[...] = jnp.zeros_like(dg_ref)

        @pl.when(t % tps == tps - 1)
        def _():
            carry[...] = jnp.zeros_like(carry)

        dx3v = dx3_ref[...]
        dx3b = dx3v.astype(BF16)
        dh3 = jnp.zeros((tm, D_MODEL), F32)
        for jc in range(n_chunks):
            da = _dot_nt(dx3b, wdown_v[pl.ds(jc * FFN_CHUNK, FFN_CHUNK), :])
            colss = [pl.ds(half * D_FF + jc * FFN_CHUNK, FFN_CHUNK) for half in range(2)]
            gate, val = [cc_ref[:, cols].astype(F32) for cols in colss]
            sg = _sigmoid(gate)
            dgate = da * val * (sg * (1.0 + gate * (1.0 - sg)))
            dval = da * (gate * sg)
            for dcc, cols in zip((dgate, dval), colss):
                uu = uu_ref[:, cols].astype(F32)
                dfb_ref[:, cols] += _colsum(dcc)
                ext = jnp.concatenate([dcc, carry[:, cols]], axis=0)
                carry[:, cols] = dcc[:FFN_HALO, :]
                n1 = pltpu.roll(ext, tm + FFN_HALO - 1, 0)[:tm, :]
                n2 = pltpu.roll(ext, tm + FFN_HALO - 2, 0)[:tm, :]
                duu = fw_ref[2:3, cols] * dcc + fw_ref[1:2, cols] * n1 + fw_ref[0:1, cols] * n2
                dfw_ref[2:3, cols] += _colsum(uu * dcc)
                dfw_ref[1:2, cols] += _colsum(uu * n1)
                dfw_ref[0:1, cols] += _colsum(uu * n2)
                duub = duu.astype(BF16)
                duu_ref[:, cols] = duub
                dh3 = dh3 + _dot(duub, wup_v[cols, :])
        xh, r = _rms_fwd(x2_ref[...])
        dg_ref[...] += _colsum(dh3 * xh)
        dx2 = dx3v + _rms_bwd(dh3, xh, r, gffn_ref[...])
        dx2_ref[...] = dx2
        dx2b_ref[...] = dx2.astype(BF16)

    rev = lambda w: pl.BlockSpec((tm, w), lambda i: (n_tiles - 1 - i, 0))
    return pl.pallas_call(
        body, name="bwd_ffn", grid=(n_tiles,),
        in_specs=[rev(D_MODEL), rev(D_MODEL), rev(2 * D_FF), rev(2 * D_FF), _full((1, D_MODEL)),
                  pl.BlockSpec(memory_space=pl.ANY), _full((FFN_CONV_WIDTH, 2 * D_FF))],
        out_specs=[rev(D_MODEL), rev(D_MODEL), rev(2 * D_FF), _full((1, 2 * D_FF)), _full((FFN_CONV_WIDTH, 2 * D_FF)),
                   _full((1, D_MODEL))],
        out_shape=[jax.ShapeDtypeStruct((tokens, D_MODEL), F32), jax.ShapeDtypeStruct((tokens, D_MODEL), BF16),
                   jax.ShapeDtypeStruct((tokens, 2 * D_FF), BF16),
                   jax.ShapeDtypeStruct((1, 2 * D_FF), F32), jax.ShapeDtypeStruct((FFN_CONV_WIDTH, 2 * D_FF), F32),
                   jax.ShapeDtypeStruct((1, D_MODEL), F32)],
        scratch_shapes=[pltpu.VMEM((2 * D_FF, D_MODEL), BF16), pltpu.VMEM((D_FF, D_MODEL), BF16),
                        pltpu.VMEM((FFN_HALO, 2 * D_FF), F32), pltpu.SemaphoreType.DMA((2,))],
        compiler_params=_params(),
    )(dx3, x2, uu_all, cc_all, g_ffn, gw, ffn_w)


def _bwd_attn(dx2, x1, q, kv, gw, g_x, after, seq, tm):
    tokens = x1.shape[0]
    n_tiles = tokens // tm
    tps = seq // tm
    n_b = tokens // seq

    def body(dx2_ref, x1_ref, q_ref, kv_ref, g_ref, gw_hbm, after_ref, dx1_ref, dx1b_ref, dq_ref, dkv_ref, dg_ref,
             wq_v, wo_v, sem):
        del after_ref
        i = pl.program_id(0)

        _start_weights(gw_hbm, ("w_o", "w_q"), (wo_v, wq_v), sem)

        @pl.when(i == 0)
        def _():
            dg_ref[...] = jnp.zeros_like(dg_ref)

        @pl.when(i % tps == 0)
        def _():
            dkv_ref[...] = jnp.zeros_like(dkv_ref)

        dx2v = dx2_ref[...]
        do = _dot_nt(dx2v.astype(BF16), wo_v[...]).astype(BF16)
        q = q_ref[...]
        heads = [slice(h * HEAD_DIM, (h + 1) * HEAD_DIM) for h in range(HEADS)]
        kcols = [pl.ds(h * HEAD_DIM, HEAD_DIM) for h in range(HEADS)]
        vcols = [pl.ds(D_MODEL + h * HEAD_DIM, HEAD_DIM) for h in range(HEADS)]
        scores = [_dot_nt(q[:, hd], kv_ref[:, kc]) for hd, kc in zip(heads, kcols)]
        dps = [_dot_nt(do[:, hd], kv_ref[:, vc]) for hd, vc in zip(heads, vcols)]
        probs = [_softmax_rows(s) for s in scores]
        dss = [(p * (dp - jnp.sum(dp * p, axis=-1, keepdims=True))).astype(BF16) for p, dp in zip(probs, dps)]
        for p, hd, vc in zip(probs, heads, vcols):
            dkv_ref[:, vc] += _dot_tn(p.astype(BF16), do[:, hd])
        dqs = [_dot(ds, kv_ref[:, kc]) * (HEAD_DIM ** -0.5) for ds, kc in zip(dss, kcols)]
        for ds, hd, kc in zip(dss, heads, kcols):
            dkv_ref[:, kc] += _dot_tn(ds, q[:, hd])
        dq = jnp.concatenate(dqs, axis=-1).astype(BF16)
        dq_ref[...] = dq
        dh2 = _dot_nt(dq, wq_v[...])
        xh, r = _rms_fwd(x1_ref[...])
        dg_ref[...] += _colsum(dh2 * xh)
        dx1 = dx2v + _rms_bwd(dh2, xh, r, g_ref[...])
        dx1_ref[...] = dx1
        dx1b_ref[...] = dx1.astype(BF16)

    row = lambda w: pl.BlockSpec((tm, w), lambda i: (i, 0))
    per_b = pl.BlockSpec((N_MEM, 2 * D_MODEL), lambda i: (i // tps, 0))
    return pl.pallas_call(
        body, name="bwd_attn", grid=(n_tiles,),
        in_specs=[row(D_MODEL), row(D_MODEL), row(D_MODEL), per_b, _full((1, D_MODEL)), pl.BlockSpec(memory_space=pl.ANY),
                  _full(after.shape)],
        out_specs=[row(D_MODEL), row(D_MODEL), row(D_MODEL), per_b, _full((1, D_MODEL))],
        out_shape=[jax.ShapeDtypeStruct((tokens, D_MODEL), F32), jax.ShapeDtypeStruct((tokens, D_MODEL), BF16),
                   jax.ShapeDtypeStruct((tokens, D_MODEL), BF16),
                   jax.ShapeDtypeStruct((n_b * N_MEM, 2 * D_MODEL), F32), jax.ShapeDtypeStruct((1, D_MODEL), F32)],
        scratch_shapes=[pltpu.VMEM((D_MODEL, D_MODEL), BF16), pltpu.VMEM((D_MODEL, D_MODEL), BF16), pltpu.SemaphoreType.DMA((2,))],
        compiler_params=_params(),
    )(dx2, x1, q, kv, g_x, gw, after)


def _bwd_kv(dkv, mem2d, gw):
    rows = mem2d.shape[0]
    n_b = rows // N_MEM

    def body(dkv_ref, mem_ref, gw_hbm, dkvb_ref, dg_ref, wkv_v, sem):
        @pl.when(pl.program_id(0) == 0)
        def _():
            copies = _load_weight(gw_hbm, "w_kv", wkv_v, sem)
            for cp in copies:
                cp.start()
            for cp in copies:
                cp.wait()
            dg_ref[...] = jnp.zeros_like(dg_ref)

        dkvb = dkv_ref[...].astype(BF16)
        dkvb_ref[...] = dkvb
        dmn = _dot(dkvb, wkv_v[...])
        mh, _ = _rms_fwd(mem_ref[...])
        dg_ref[...] += _colsum(dmn * mh)

    return pl.pallas_call(
        body, name="bwd_kv", grid=(n_b,),
        in_specs=[pl.BlockSpec((N_MEM, 2 * D_MODEL), lambda b: (b, 0)), pl.BlockSpec((N_MEM, D_MODEL), lambda b: (b, 0)),
                  pl.BlockSpec(memory_space=pl.ANY)],
        out_specs=[pl.BlockSpec((N_MEM, 2 * D_MODEL), lambda b: (b, 0)), _full((1, D_MODEL))],
        out_shape=[jax.ShapeDtypeStruct((rows, 2 * D_MODEL), BF16), jax.ShapeDtypeStruct((1, D_MODEL), F32)],
        scratch_shapes=[pltpu.VMEM((2 * D_MODEL, D_MODEL), BF16), pltpu.SemaphoreType.DMA],
        compiler_params=_params(),
    )(dkv, mem2d, gw)


def _bwd_mix(dx1, x2d, u_all, c_all, pooled_all, gw, g_mix, conv_w, ln_g, ln_b, pool_w, pool_scale, after, seq, tm):
    tokens = x2d.shape[0]
    n_tiles = tokens // tm
    tps = seq // tm

    def body(dx1_ref, x_ref, u_ref, c_ref, pooled_ref, gmix_ref, gw_hbm, cw_ref, lng_ref, lnb_ref, pw_ref, ps_ref,
             after_ref, dx_ref, du_ref, dgmix_ref, dcw_ref, dcb_ref, dlng_ref, dlnb_ref, dpw_ref, dps_ref,
             win_v, wout_v, dc_carry, e_carry, sem):
        del after_ref
        i = pl.program_id(0)
        t = n_tiles - 1 - i

        _start_weights(gw_hbm, ("w_out", "w_in"), (wout_v, win_v), sem)

        @pl.when(i == 0)
        def _():
            for ref in (dgmix_ref, dcw_ref, dcb_ref, dlng_ref, dlnb_ref, dpw_ref, dps_ref):
                ref[...] = jnp.zeros_like(ref)

        @pl.when(t % tps == tps - 1)
        def _():
            dc_carry[...] = jnp.zeros_like(dc_carry)
            e_carry[...] = jnp.zeros_like(e_carry)

        dx1v = dx1_ref[...]
        dymix = _dot_nt(dx1v.astype(BF16), wout_v[...])
        dyc, dyp = dymix[:, :D_CONV], dymix[:, D_CONV:]
        u = u_ref[...]
        val, gate = u[:, :D_CONV], u[:, D_CONV:2 * D_CONV]

        conv = c_ref[...]
        mu = jnp.mean(conv, axis=-1, keepdims=True)
        cen = conv - mu
        rs = lax.rsqrt(jnp.mean(cen * cen, axis=-1, keepdims=True) + EPS)
        chat = cen * rs
        ln = chat * lng_ref[...] + lnb_ref[...]
        sl = _sigmoid(ln)
        dln = dyc * (sl * (1.0 + ln * (1.0 - sl)))
        dlng_ref[...] += _colsum(dln * chat)
        dlnb_ref[...] += _colsum(dln)
        dchat = dln * lng_ref[...]
        dc = rs * (dchat - jnp.mean(dchat, axis=-1, keepdims=True)
                   - chat * jnp.mean(dchat * chat, axis=-1, keepdims=True))
        dcb_ref[...] += _colsum(dc)
        sg = _sigmoid(gate)
        hc = val * sg
        ext = jnp.concatenate([dc, dc_carry[...]], axis=0)
        dc_carry[...] = dc[:CONV_HALO, :]
        dhc = jnp.zeros((tm, D_CONV), F32)
        ahead_by = _sublane_shifts(ext)
        for k in range(CONV_WIDTH):
            whole, part = divmod(CONV_WIDTH - 1 - k, 8)
            tap = ahead_by[part][8 * whole:8 * whole + tm, :]
            dhc = dhc + cw_ref[k:k + 1, :] * tap
            dcw_ref[k:k + 1, :] += _colsum_mxu(hc * tap)
        du_ref[:, :D_CONV] = (dhc * sg).astype(BF16)
        du_ref[:, D_CONV:2 * D_CONV] = (dhc * val * (sg * (1.0 - sg))).astype(BF16)

        pos = lax.broadcasted_iota(jnp.int32, (tm, 1), 0) + (t % tps) * tm
        es, dpooled = [], []
        for g, w in enumerate(POOL_WINDOWS):
            cols = pl.ds(g * POOL_GROUP_DIM, POOL_GROUP_DIM)
            lo = g * POOL_GROUP_DIM
            pooled = pooled_ref[:, cols]
            pw = pw_ref[g].astype(BF16)
            dyg = dyp[:, lo:lo + POOL_GROUP_DIM]
            dps_ref[:, cols] += _colsum(dyg * _dot(pooled, pw))
            dmixed = (dyg * ps_ref[:, cols]).astype(BF16)
            dpw_ref[g] += _dot_tn(pooled, dmixed)
            dpo = _dot_nt(dmixed, pw)
            dpooled.append(dpo)
            es.append(dpo / jnp.minimum(pos + 1, w).astype(F32))
        e = jnp.concatenate(es, axis=-1)
        run = jnp.concatenate([e, e_carry[...]], axis=0)
        e_carry[...] = e[:POOL_HALO, :]
        rows = tm + POOL_HALO
        for g, w in enumerate(POOL_WINDOWS):
            lo = g * POOL_GROUP_DIM
            run = run[:, POOL_GROUP_DIM if g else 0:]
            run = run + pltpu.roll(run, rows - w // 2, 0)
            du_ref[:, 2 * D_CONV + lo:2 * D_CONV + lo + POOL_GROUP_DIM] = (
                run[:tm, :POOL_GROUP_DIM] - dpooled[g]).astype(BF16)

        dh1 = _dot(du_ref[...], win_v[...])
        xh, r = _rms_fwd(x_ref[...])
        dgmix_ref[...] += _colsum(dh1 * xh)
        dx_ref[...] = dx1v + _rms_bwd(dh1, xh, r, gmix_ref[...])

    rev = lambda w: pl.BlockSpec((tm, w), lambda i: (n_tiles - 1 - i, 0))
    return pl.pallas_call(
        body, name="bwd_mix", grid=(n_tiles,),
        in_specs=[rev(D_MODEL), rev(D_MODEL), rev(D_IN), rev(D_CONV), rev(D_POOL), _full((1, D_MODEL)),
                  pl.BlockSpec(memory_space=pl.ANY), _full((CONV_WIDTH, D_CONV)), _full((1, D_CONV)), _full((1, D_CONV)),
                  _full((4, POOL_GROUP_DIM, POOL_GROUP_DIM)), _full((1, D_POOL)), _full(after.shape)],
        out_specs=[rev(D_MODEL), rev(D_IN), _full((1, D_MODEL)), _full((CONV_WIDTH, D_CONV)), _full((1, D_CONV)),
                   _full((1, D_CONV)), _full((1, D_CONV)), _full((4, POOL_GROUP_DIM, POOL_GROUP_DIM)), _full((1, D_POOL))],
        out_shape=[jax.ShapeDtypeStruct((tokens, D_MODEL), F32), jax.ShapeDtypeStruct((tokens, D_IN), BF16),
                   jax.ShapeDtypeStruct((1, D_MODEL), F32), jax.ShapeDtypeStruct((CONV_WIDTH, D_CONV), F32),
                   jax.ShapeDtypeStruct((1, D_CONV), F32), jax.ShapeDtypeStruct((1, D_CONV), F32),
                   jax.ShapeDtypeStruct((1, D_CONV), F32),
                   jax.ShapeDtypeStruct((4, POOL_GROUP_DIM, POOL_GROUP_DIM), F32), jax.ShapeDtypeStruct((1, D_POOL), F32)],
        scratch_shapes=[pltpu.VMEM((D_IN, D_MODEL), BF16), pltpu.VMEM((D_MODEL, D_MODEL), BF16),
                        pltpu.VMEM((CONV_HALO, D_CONV), F32), pltpu.VMEM((POOL_HALO, D_POOL), F32),
                        pltpu.SemaphoreType.DMA((2,))],
        compiler_params=_params(),
    )(dx1, x2d, u_all, c_all, pooled_all, g_mix, gw, conv_w, ln_g, ln_b, pool_w, pool_scale, after)


def _wgrad(a, b, name, after=None):
    tokens, m = a.shape
    n = b.shape[1]
    tm = 512 if m % 512 == 0 else 256
    extra = [] if after is None else [after]

    def body(a_ref, b_ref, *rest):
        rest[-1][...] = _dot_tn(a_ref[...], b_ref[...]).astype(rest[-1].dtype)

    return pl.pallas_call(
        body, name=name, grid=(m // tm,),
        in_specs=[pl.BlockSpec((tokens, tm), lambda i: (0, i)), _full((tokens, n))] + [_full(t.shape) for t in extra],
        out_specs=pl.BlockSpec((tm, n), lambda i: (i, 0)),
        out_shape=jax.ShapeDtypeStruct((m, n), BF16),
        compiler_params=_params(),
    )(a, b, *extra)


def _adamw_update(w, g, m, v):
    nm = ADAM_B1 * m + (1.0 - ADAM_B1) * g
    nv = ADAM_B2 * v + (1.0 - ADAM_B2) * (g * g)
    m_hat = nm / (1.0 - ADAM_B1 ** ADAM_STEP)
    v_hat = nv / (1.0 - ADAM_B2 ** ADAM_STEP)
    return -ADAM_LR * (m_hat / (jnp.sqrt(v_hat) + ADAM_EPS) + ADAM_WD * w), nm, nv


def _adamw_small(ws, gs, ms, vs):
    n = len(ws)

    def body(*refs):
        ins, outs = refs[:4 * n], refs[4 * n:]
        for k in range(n):
            d, nm, nv = _adamw_update(*[ins[j * n + k][...] for j in range(4)])
            outs[k][...] = d
            outs[n + k][...] = nm
            outs[2 * n + k][...] = nv

    vmem = pl.BlockSpec(memory_space=pltpu.VMEM)
    outs = pl.pallas_call(
        body, name="adamw_small",
        in_specs=[vmem] * (4 * n), out_specs=[vmem] * (3 * n),
        out_shape=[jax.ShapeDtypeStruct(w.shape, F32) for w in ws] * 3,
    )(*ws, *gs, *ms, *vs)
    return outs[:n], outs[n:2 * n], outs[2 * n:]


SMALL = (("norm_mix_g", (1, 1024)), ("conv_dw_b", (1, 512)), ("conv_ln_g", (1, 512)), ("conv_ln_b", (1, 512)),
         ("pool_w", (1, 4, 128, 128)), ("pool_scale", (1, 512)), ("norm_xattn_g", (1, 1024)), ("norm_mem_g", (1, 1024)),
         ("norm_ffn_g", (1, 1024)), ("ffn_dw_b", (1, 5632)), ("norm_final_g", (1024,)))
LANES = 128


def _pack_rows(arrs):
    flat = jnp.concatenate([a.reshape(-1) for a in arrs])
    pad = (-flat.shape[0]) % (8 * LANES)
    return jnp.pad(flat, (0, pad)).reshape(-1, LANES)


def kernel(x, mem, norm_mix_g, w_in, conv_dw_w, conv_dw_b, conv_ln_g, conv_ln_b, pool_w, pool_scale, w_out, norm_xattn_g, norm_mem_g, w_q, w_kv, w_o, norm_ffn_g, w_up, ffn_dw_w, ffn_dw_b, w_down, norm_final_g, loss_target, m_norm_mix_g, m_w_in, m_conv_dw_w, m_conv_dw_b, m_conv_ln_g, m_conv_ln_b, m_pool_w, m_pool_scale, m_w_out, m_norm_xattn_g, m_norm_mem_g, m_w_q, m_w_kv, m_w_o, m_norm_ffn_g, m_w_up, m_ffn_dw_w, m_ffn_dw_b, m_w_down, m_norm_final_g, v_norm_mix_g, v_w_in, v_conv_dw_w, v_conv_dw_b, v_conv_ln_g, v_conv_ln_b, v_pool_w, v_pool_scale, v_w_out, v_norm_xattn_g, v_norm_mem_g, v_w_q, v_w_kv, v_w_o, v_norm_ffn_g, v_w_up, v_ffn_dw_w, v_ffn_dw_b, v_w_down, v_norm_final_g):
    weights = dict(norm_mix_g=norm_mix_g, w_in=w_in, conv_dw_w=conv_dw_w, conv_dw_b=conv_dw_b, conv_ln_g=conv_ln_g,
                   conv_ln_b=conv_ln_b, pool_w=pool_w, pool_scale=pool_scale, w_out=w_out, norm_xattn_g=norm_xattn_g,
                   norm_mem_g=norm_mem_g, w_q=w_q, w_kv=w_kv, w_o=w_o, norm_ffn_g=norm_ffn_g, w_up=w_up,
                   ffn_dw_w=ffn_dw_w, ffn_dw_b=ffn_dw_b, w_down=w_down, norm_final_g=norm_final_g)
    moments_m = dict(norm_mix_g=m_norm_mix_g, w_in=m_w_in, conv_dw_w=m_conv_dw_w, conv_dw_b=m_conv_dw_b,
                     conv_ln_g=m_conv_ln_g, conv_ln_b=m_conv_ln_b, pool_w=m_pool_w, pool_scale=m_pool_scale,
                     w_out=m_w_out, norm_xattn_g=m_norm_xattn_g, norm_mem_g=m_norm_mem_g, w_q=m_w_q, w_kv=m_w_kv,
                     w_o=m_w_o, norm_ffn_g=m_norm_ffn_g, w_up=m_w_up, ffn_dw_w=m_ffn_dw_w, ffn_dw_b=m_ffn_dw_b,
                     w_down=m_w_down, norm_final_g=m_norm_final_g)
    moments_v = dict(norm_mix_g=v_norm_mix_g, w_in=v_w_in, conv_dw_w=v_conv_dw_w, conv_dw_b=v_conv_dw_b,
                     conv_ln_g=v_conv_ln_g, conv_ln_b=v_conv_ln_b, pool_w=v_pool_w, pool_scale=v_pool_scale,
                     w_out=v_w_out, norm_xattn_g=v_norm_xattn_g, norm_mem_g=v_norm_mem_g, w_q=v_w_q, w_kv=v_w_kv,
                     w_o=v_w_o, norm_ffn_g=v_norm_ffn_g, w_up=v_w_up, ffn_dw_w=v_ffn_dw_w, ffn_dw_b=v_ffn_dw_b,
                     w_down=v_w_down, norm_final_g=v_norm_final_g)
    order = list(weights)
    transposed = ("w_in", "w_kv", "w_up")

    n_b, seq, _ = x.shape
    tokens = n_b * seq
    tm_mix = min(512, seq // 2)
    tm_attn = min(1024, seq // 2)
    tm_ffn = min(256, seq // 2)
    dev = 4 * lax.axis_index("x") + 2 * lax.axis_index("y") + lax.axis_index("c")

    packs = [jnp.concatenate([weights[n][0].T if n in transposed else weights[n][0] for n in names], axis=0).astype(BF16)
             for names in AG_GROUPS]
    small_sharded = _pack_rows([conv_dw_w[0], ffn_dw_w[0]])
    gw_mix, gsmall = _all_gather([packs[0], small_sharded], "weights_all_gather")
    flights = []
    after = gw_mix
    for k in (1, 2):
        own_in_place = lax.dynamic_update_slice(lax.empty((N_DEV,) + packs[k].shape, BF16), packs[k][None], (dev, 0, 0))
        flights.append(_gather_start(own_in_place, after, "weights_gather_start_%d" % k, BARRIER_IDS["gather_start"][k - 1]))
        after = flights[-1][3]
    gflat = gsmall.reshape(N_DEV, -1)
    n_cw = CONV_WIDTH * (D_CONV // N_DEV)
    n_fw = FFN_CONV_WIDTH * (2 * D_FF // N_DEV)
    conv_w = gflat[:, :n_cw].reshape(N_DEV, CONV_WIDTH, D_CONV // N_DEV).transpose(1, 0, 2).reshape(CONV_WIDTH, D_CONV)
    ffn_w = gflat[:, n_cw:n_cw + n_fw].reshape(N_DEV, FFN_CONV_WIDTH, 2 * D_FF // N_DEV).transpose(1, 0, 2).reshape(
        FFN_CONV_WIDTH, 2 * D_FF)

    x2d = x.reshape(tokens, D_MODEL)
    mem2d = mem.reshape(n_b * N_MEM, D_MODEL)
    tgt2d = loss_target.reshape(tokens, D_MODEL)
    g_final = norm_final_g.reshape(1, D_MODEL)

    def gather_finish(flight, after, tag):
        fwd_send, fwd_recv, buf = _gather_forward(*flight[:3], after, "weights_gather_forward_" + tag,
                                                  BARRIER_IDS["gather_forward"][int(tag) - 1])
        return _gather_finish(fwd_send, fwd_recv, buf, "weights_gather_finish_" + tag)

    x1, u_all, c_all, pooled_all, ymix, h1 = _fwd_mix(
        x2d, gw_mix, norm_mix_g, conv_w, conv_dw_b, conv_ln_g, conv_ln_b, pool_w[0], pool_scale, flights[1][3],
        seq, tm_mix)
    gw_attn = gather_finish(flights[0], x1, "1")
    mem_n, kv = _fwd_kv(mem2d, gw_attn, norm_mem_g)
    x2, h2, q, o = _fwd_attn(x1, kv, gw_attn, norm_xattn_g, seq, tm_attn)
    gw_ffn = gather_finish(flights[1], x2, "2")
    uu_all, cc_all, a_all, h3, dx3, dx3b, loss_part, dg_final = _fwd_ffn(
        x2, tgt2d, gw_ffn, norm_ffn_g, ffn_w, ffn_dw_b, g_final, seq, tm_ffn)

    table = _owner_table()

    def sibling_start(names, tag):
        parts = [part[n].reshape(N_DEV, W_OFF[n][1], D_MODEL) for n in names]
        return _exchange_start(parts, 4, _to_sibling, "rs_sibling_exchange_start_" + tag, BARRIER_IDS["sibling"][tag])

    def chips_start(flight, after, tag):
        parts, landed = _exchange_wait(*flight[:4], after, 4, _to_sibling, "rs_sibling_exchange_wait_" + tag)
        sums = _chip_partial_sums(table, parts, landed, "rs_chip_partial_sums_" + tag)
        return parts, landed, _exchange_start(sums, 3, _to_chip, "rs_chip_exchange_start_" + tag,
                                              BARRIER_IDS["chips"][tag])

    grads, delta, new_m, new_v = {}, {}, {}, {}

    def reduce_finish(names, parts, landed, flight, after, tag):
        _, from_chips = _exchange_wait(*flight[:4], after, 3, _to_chip, "rs_chip_exchange_wait_" + tag)
        as_rows = {n: n in transposed and W_OFF[n][1] % LANES != 0 for n in names}
        states = [tuple(t[n][0].T if as_rows[n] else t[n][0] for t in (weights, moments_m, moments_v)) for n in names]
        results = _final_update(table, parts, landed, from_chips, states, "rs_final_update_" + tag)
        for n, res in zip(names, results):
            grads[n], delta[n], new_m[n], new_v[n] = [t.T[None] if as_rows[n] else t[None] for t in res]
        return delta[names[-1]]

    part = {}
    dx2, dx2b, duu, d_ffn_b, d_ffn_w, dg_ffn = _bwd_ffn(dx3, x2, uu_all, cc_all, gw_ffn, norm_ffn_g, ffn_w, seq, tm_ffn)
    part["w_up"] = _wgrad(duu, h3, "wgrad_w_up")
    part["w_down"] = _wgrad(a_all, dx3b, "wgrad_w_down")
    to_sibling_a = sibling_start(RS_GROUPS["a"], "a")
    dx1, dx1b, dq, dkv, dg_x = _bwd_attn(dx2, x1, q, kv, gw_attn, norm_xattn_g, to_sibling_a[4], seq, tm_mix)
    parts_a, landed_a, flight_a = chips_start(to_sibling_a, dx1, "a")
    dkv_b, dg_mem = _bwd_kv(dkv, mem2d, gw_attn)
    part["w_q"] = _wgrad(h2, dq, "wgrad_w_q", after=flight_a[4])
    part["w_kv"] = _wgrad(dkv_b, mem_n, "wgrad_w_kv")
    part["w_o"] = _wgrad(o, dx2b, "wgrad_w_o")
    part["w_out"] = _wgrad(ymix, dx1b, "wgrad_w_out")
    to_sibling_b = sibling_start(RS_GROUPS["b"], "b")
    parts_b, landed_b, flight_b = chips_start(to_sibling_b, to_sibling_b[4], "b")
    dx, du, dg_mix, d_conv_w, d_conv_b, d_ln_g, d_ln_b, d_pool_w, d_pool_scale = _bwd_mix(
        dx1, x2d, u_all, c_all, pooled_all, gw_mix, norm_mix_g, conv_w, conv_ln_g, conv_ln_b, pool_w[0], pool_scale,
        flight_b[4], seq, tm_mix)
    grad_x = dx.reshape(x.shape)

    small_grads = dict(norm_mix_g=dg_mix, conv_dw_b=d_conv_b, conv_ln_g=d_ln_g, conv_ln_b=d_ln_b, pool_w=d_pool_w,
                       pool_scale=d_pool_scale, norm_xattn_g=dg_x, norm_mem_g=dg_mem, norm_ffn_g=dg_ffn,
                       ffn_dw_b=d_ffn_b, norm_final_g=dg_final)
    small_list = [small_grads[n] for n, _ in SMALL] + [d_conv_w, d_ffn_w, loss_part[:1]]
    small_mine = _pack_rows(small_list)
    small_flight = _broadcast_start(
        lax.dynamic_update_slice(lax.empty((N_DEV,) + small_mine.shape, F32), small_mine[None], (dev, 0, 0)),
        "small_grads_broadcast_start", BARRIER_IDS["broadcast"])

    part["w_in"] = _wgrad(du, h1, "wgrad_w_in", after=small_flight[3])
    to_sibling_c = sibling_start(RS_GROUPS["c"], "c")
    parts_c, landed_c, flight_c = chips_start(to_sibling_c, to_sibling_c[4], "c")
    updated_a = reduce_finish(RS_GROUPS["a"], parts_a, landed_a, flight_a, flight_c[4], "a")
    updated_b = reduce_finish(RS_GROUPS["b"], parts_b, landed_b, flight_b, updated_a, "b")
    small_all = _broadcast_wait(*small_flight[:3], updated_b, "small_grads_broadcast_wait")
    small_sum = _sum_blocks(small_all).reshape(-1)

    pos = 0
    for n, shape in SMALL:
        size = 1
        for s in shape:
            size *= s
        grads[n] = small_sum[pos:pos + size].reshape(shape)
        pos += size
    full_conv_w = small_sum[pos:pos + CONV_WIDTH * D_CONV].reshape(CONV_WIDTH, D_CONV)
    pos += CONV_WIDTH * D_CONV
    full_ffn_w = small_sum[pos:pos + FFN_CONV_WIDTH * 2 * D_FF].reshape(FFN_CONV_WIDTH, 2 * D_FF)
    loss = small_sum[pos + FFN_CONV_WIDTH * 2 * D_FF]
    grads["conv_dw_w"] = lax.dynamic_slice_in_dim(full_conv_w, dev * (D_CONV // N_DEV), D_CONV // N_DEV, axis=1)[None]
    grads["ffn_dw_w"] = lax.dynamic_slice_in_dim(full_ffn_w, dev * (2 * D_FF // N_DEV), 2 * D_FF // N_DEV, axis=1)[None]

    small_names = [n for n in order if n not in W_OFF]
    swap = lambda t: jnp.transpose(t, (1, 0, 2))
    two_d = lambda t: t.reshape(1, -1) if t.ndim == 1 else (swap(t) if t.ndim == 3 else t)
    outs = _adamw_small(*[[two_d(t[n]) for n in small_names] for t in (weights, grads, moments_m, moments_v)])
    for res, out in zip((delta, new_m, new_v), outs):
        for n, o in zip(small_names, out):
            res[n] = swap(o) if o.ndim == 3 else o.reshape(weights[n].shape)

    reduce_finish(RS_GROUPS["c"], parts_c, landed_c, flight_c, delta[small_names[-1]], "c")

    return (loss, grad_x, *[grads[n] for n in order], *[delta[n] for n in order],
            *[new_m[n] for n in order], *[new_v[n] for n in order])
```

```python
import jax
import jax.numpy as jnp
from jax import lax
from jax.experimental import pallas as pl
from jax.experimental.pallas import tpu as pltpu

F32 = jnp.float32
BF16 = jnp.bfloat16
MESH = pl.DeviceIdType.MESH

N_DEV = 8
D_MODEL = 1024
D_CONV = 512
D_POOL = 512
CONV_WIDTH = 31
POOL_WINDOWS = (2, 4, 8, 16)
POOL_GROUP_DIM = 128
D_IN = 1536
N_MEM = 256
HEADS = 4
HEAD_DIM = 256
D_FF = 2816
FFN_CONV_WIDTH = 3
EPS = 1e-6
ADAM_LR = 0.001
ADAM_B1 = 0.9
ADAM_B2 = 0.999
ADAM_EPS = 1e-08
ADAM_WD = 0.01
ADAM_STEP = 10

VMEM_LIMIT_V7X = 56 * 1024 * 1024
CONV_HALO = 32
POOL_HALO = 16
FFN_HALO = 8
FFN_CHUNK = 2816

W_ROWS = (("w_in", 192), ("w_out", 128), ("w_q", 128), ("w_kv", 256), ("w_o", 128), ("w_up", 704), ("w_down", 352))
AG_GROUPS = (("w_in", "w_out"), ("w_q", "w_kv", "w_o"), ("w_up", "w_down"))
W_OFF = {}
for _names in AG_GROUPS:
    _o = 0
    for _n in _names:
        W_OFF[_n] = (_o, dict(W_ROWS)[_n])
        _o += dict(W_ROWS)[_n]
RS_GROUPS = {"a": ("w_up", "w_down"), "b": ("w_q", "w_kv", "w_o", "w_out"), "c": ("w_in",)}
BARRIER_IDS = {"gather_start": (0, 1), "gather_forward": (2, 3), "sibling": {"a": 4, "b": 5, "c": 6},
               "chips": {"a": 7, "b": 8, "c": 9}, "broadcast": 10}


def _dot(a, b):
    return jnp.dot(a, b, preferred_element_type=F32)


def _dot_nt(a, b):
    return lax.dot_general(a, b, (((1,), (1,)), ((), ())), preferred_element_type=F32)


def _dot_tn(a, b):
    return lax.dot_general(a, b, (((0,), (0,)), ((), ())), preferred_element_type=F32)


def _sigmoid(v):
    return 1.0 / (1.0 + jnp.exp(-v))


def _rms_fwd(v):
    r = lax.rsqrt(jnp.mean(v * v, axis=-1, keepdims=True) + EPS)
    return v * r, r


def _rms_bwd(dh, vh, r, g):
    gd = dh * g
    return r * (gd - vh * jnp.mean(gd * vh, axis=-1, keepdims=True))


def _sublane_shifts(v):
    rows = v.shape[0]
    return [v] + [pltpu.roll(v, rows - b, 0) for b in range(1, 8)]


def _colsum(v):
    return jnp.sum(v, axis=0, keepdims=True)


def _colsum_mxu(v):
    return _dot(jnp.ones((8, v.shape[0]), BF16), v.astype(BF16))[0:1, :]


def _full(shape):
    return pl.BlockSpec(shape, lambda *_: (0,) * len(shape))


def _params(sem=("arbitrary",), vmem=VMEM_LIMIT_V7X):
    return pltpu.CompilerParams(dimension_semantics=sem, vmem_limit_bytes=vmem)


def _load_weight(g_hbm, name, dst, sem):
    off, rows = W_OFF[name]
    return [pltpu.make_async_copy(g_hbm.at[d, pl.ds(off, rows), :], dst.at[pl.ds(d * rows, rows), :], sem)
            for d in range(N_DEV)]


def _start_weights(g_hbm, names, dsts, sems):
    @pl.when(pl.program_id(0) == 0)
    def _():
        copies = [_load_weight(g_hbm, name, dst, sems.at[k]) for k, (name, dst) in enumerate(zip(names, dsts))]
        for cp in sum(copies, []):
            cp.start()
        for cp in sum(copies, []):
            cp.wait()


def _position():
    x, y, c = lax.axis_index("x"), lax.axis_index("y"), lax.axis_index("c")
    chips = [(1 - x, y), (x, 1 - y), (1 - x, 1 - y)]
    return x, y, c, chips


def _dev(px, py, pc):
    return 4 * px + 2 * py + pc


def _all_gather(arrs, name):
    n = len(arrs)

    def body(*refs):
        ins, outs = refs[:n], refs[n:2 * n]
        send_sems, recv_sems, local_sems = refs[2 * n:2 * n + 3]
        bounce = refs[2 * n + 3:]
        x, y, c, chips = _position()
        me, sibling = (x, y, c), (x, y, 1 - c)

        def copy(a, k, block, to, src=None):
            rows = outs[a].at[_dev(*block)]
            return pltpu.make_async_remote_copy(
                src_ref=rows if src is None else src, dst_ref=rows,
                send_sem=send_sems.at[a, k], recv_sem=recv_sems.at[a, k], device_id=to, device_id_type=MESH)

        sends = []
        for a in range(n):
            first = [copy(a, 0, me, sibling, src=ins[a])]
            first += [copy(a, 1 + j, me, (*chip, c), src=ins[a]) for j, chip in enumerate(chips)]
            for cp in first:
                cp.start()
            sends += first
        started = []
        for a in range(n):
            load = pltpu.make_async_copy(ins[a], bounce[a], local_sems.at[a, 0])
            load.start()
            load.wait()
            mine = pltpu.make_async_copy(bounce[a], outs[a].at[_dev(*me)], local_sems.at[a, 1])
            mine.start()
            started.append(mine)
        for j, chip in enumerate(chips):
            for a in range(n):
                copy(a, 1 + j, (*chip, c), me).wait_recv()
                passed = copy(a, 4 + j, (*chip, c), sibling)
                passed.start()
                sends.append(passed)
        for a in range(n):
            copy(a, 0, sibling, me).wait_recv()
            for j, chip in enumerate(chips):
                copy(a, 4 + j, (*chip, 1 - c), me).wait_recv()
        for cp in sends:
            cp.wait_send()
        for mine in started:
            mine.wait()

    any_spec = pl.BlockSpec(memory_space=pl.ANY)
    return pl.pallas_call(
        body, name=name,
        out_shape=[jax.ShapeDtypeStruct((N_DEV,) + a.shape, a.dtype) for a in arrs],
        in_specs=[any_spec] * n, out_specs=[any_spec] * n,
        scratch_shapes=[pltpu.SemaphoreType.DMA((n, 7)), pltpu.SemaphoreType.DMA((n, 7)), pltpu.SemaphoreType.DMA((n, 2))]
        + [pltpu.VMEM(a.shape, a.dtype) for a in arrs],
    )(*arrs)


_HBM = pl.BlockSpec(memory_space=pltpu.HBM)
_SEM = pl.BlockSpec(memory_space=pltpu.SEMAPHORE)
_SIDE_EFFECT = pltpu.SideEffectType.DATAFLOW_SIDE_EFFECTING


def _handshake(peers):
    barrier = pltpu.get_barrier_semaphore()
    for peer in peers:
        pl.semaphore_signal(barrier, inc=1, device_id=peer, device_id_type=MESH)
    pl.semaphore_wait(barrier, len(peers))


def _gather_start(buf, after, name, collective_id):
    def body(buf_ref, after_ref, send_sems, recv_sems, buf_thru, token):
        del after_ref, buf_thru
        x, y, c, chips = _position()
        rows = buf_ref.at[_dev(x, y, c)]
        targets = [(x, y, 1 - c)] + [(*chip, c) for chip in chips]
        _handshake(targets)
        for k, to in enumerate(targets):
            pltpu.make_async_remote_copy(src_ref=rows, dst_ref=rows, send_sem=send_sems.at[k], recv_sem=recv_sems.at[k],
                                         device_id=to, device_id_type=MESH).start()
        token[...] = jnp.zeros_like(token)

    return pl.pallas_call(
        body, name=name,
        out_shape=(pltpu.SemaphoreType.DMA((4,)), pltpu.SemaphoreType.DMA((4,)), pltpu.HBM(buf.shape, buf.dtype),
                   jax.ShapeDtypeStruct((8, 128), F32)),
        in_specs=(_HBM, pl.BlockSpec(memory_space=pl.ANY)),
        out_specs=(_SEM, _SEM, _HBM, pl.BlockSpec(memory_space=pltpu.VMEM)),
        input_output_aliases={0: 2},
        compiler_params=pltpu.CompilerParams(has_side_effects=_SIDE_EFFECT, collective_id=collective_id),
    )(pltpu.with_memory_space_constraint(buf, pltpu.HBM), after)


def _gather_forward(send_sems, recv_sems, buf, after, name, collective_id):
    def body(buf_ref, send_sems, recv_sems, after_ref, fwd_send, fwd_recv, buf_thru):
        del after_ref, buf_thru
        x, y, c, chips = _position()
        sibling = (x, y, 1 - c)

        def copy(block, k, sends, recvs):
            rows = buf_ref.at[_dev(*block)]
            return pltpu.make_async_remote_copy(src_ref=rows, dst_ref=rows, send_sem=sends.at[k], recv_sem=recvs.at[k],
                                                device_id=sibling, device_id_type=MESH)

        _handshake([sibling])
        for k in range(4):
            copy((x, y, c), k, send_sems, recv_sems).wait_send()
        copy(sibling, 0, send_sems, recv_sems).wait_recv()
        for j, chip in enumerate(chips):
            copy((*chip, c), 1 + j, send_sems, recv_sems).wait_recv()
            copy((*chip, c), j, fwd_send, fwd_recv).start()

    return pl.pallas_call(
        body, name=name,
        out_shape=(pltpu.SemaphoreType.DMA((3,)), pltpu.SemaphoreType.DMA((3,)), pltpu.HBM(buf.shape, buf.dtype)),
        in_specs=(_HBM, _SEM, _SEM, pl.BlockSpec(memory_space=pl.ANY)), out_specs=(_SEM, _SEM, _HBM),
        input_output_aliases={0: 2},
        compiler_params=pltpu.CompilerParams(has_side_effects=_SIDE_EFFECT, collective_id=collective_id),
    )(buf, send_sems, recv_sems, after)


def _gather_finish(fwd_send, fwd_recv, buf, name):
    def body(buf_ref, fwd_send, fwd_recv, buf_thru):
        del buf_thru
        x, y, c, chips = _position()
        for j, chip in enumerate(chips):
            cp = pltpu.make_async_remote_copy(
                src_ref=buf_ref.at[_dev(*chip, c)], dst_ref=buf_ref.at[_dev(*chip, 1 - c)], send_sem=fwd_send.at[j],
                recv_sem=fwd_recv.at[j], device_id=(x, y, 1 - c), device_id_type=MESH)
            cp.wait_send()
            cp.wait_recv()

    return pl.pallas_call(
        body, name=name,
        out_shape=pltpu.HBM(buf.shape, buf.dtype),
        in_specs=(_HBM, _SEM, _SEM), out_specs=_HBM,
        input_output_aliases={0: 0},
        compiler_params=pltpu.CompilerParams(has_side_effects=_SIDE_EFFECT),
    )(buf, fwd_send, fwd_recv)


def _everyone_else(x, y, c, chips):
    return [(x, y, 1 - c)] + [(*chip, core) for chip in chips for core in (c, 1 - c)]


def _broadcast_start(buf, name, collective_id):
    def body(buf_ref, send_sems, recv_sems, buf_thru, token):
        del buf_thru
        x, y, c, chips = _position()
        rows = buf_ref.at[_dev(x, y, c)]
        _handshake(_everyone_else(x, y, c, chips))
        for k, to in enumerate(_everyone_else(x, y, c, chips)):
            pltpu.make_async_remote_copy(src_ref=rows, dst_ref=rows, send_sem=send_sems.at[k], recv_sem=recv_sems.at[k],
                                         device_id=to, device_id_type=MESH).start()
        token[...] = jnp.zeros_like(token)

    return pl.pallas_call(
        body, name=name,
        out_shape=(pltpu.SemaphoreType.DMA((7,)), pltpu.SemaphoreType.DMA((7,)), pltpu.HBM(buf.shape, buf.dtype),
                   jax.ShapeDtypeStruct((8, 128), F32)),
        in_specs=(_HBM,), out_specs=(_SEM, _SEM, _HBM, pl.BlockSpec(memory_space=pltpu.VMEM)),
        input_output_aliases={0: 2},
        compiler_params=pltpu.CompilerParams(has_side_effects=_SIDE_EFFECT, collective_id=collective_id),
    )(pltpu.with_memory_space_constraint(buf, pltpu.HBM))


def _broadcast_wait(send_sems, recv_sems, buf, after, name):
    def body(buf_ref, send_sems, recv_sems, after_ref, buf_thru):
        del after_ref, buf_thru
        x, y, c, chips = _position()
        for k, peer in enumerate(_everyone_else(x, y, c, chips)):
            cp = pltpu.make_async_remote_copy(
                src_ref=buf_ref.at[_dev(x, y, c)], dst_ref=buf_ref.at[_dev(*peer)], send_sem=send_sems.at[k],
                recv_sem=recv_sems.at[k], device_id=peer, device_id_type=MESH)
            cp.wait_send()
            cp.wait_recv()

    return pl.pallas_call(
        body, name=name,
        out_shape=pltpu.HBM(buf.shape, buf.dtype),
        in_specs=(_HBM, _SEM, _SEM, pl.BlockSpec(memory_space=pl.ANY)), out_specs=_HBM,
        input_output_aliases={0: 0},
        compiler_params=pltpu.CompilerParams(has_side_effects=_SIDE_EFFECT),
    )(buf, send_sems, recv_sems, after)


def _to_sibling(j, x, y, c, chips):
    return _dev(*([(x, y)] + chips)[j], 1 - c), (x, y, 1 - c)


def _to_chip(j, x, y, c, chips):
    return j, (*chips[j], c)


def _exchange_start(srcs, n_slots, route, name, collective_id):
    n = len(srcs)

    def body(*refs):
        s_refs, land_refs = refs[:n], refs[n:2 * n]
        send_sems, recv_sems = refs[2 * n:2 * n + 2]
        token = refs[-1]
        x, y, c, chips = _position()
        _handshake([(x, y, 1 - c)] if route is _to_sibling else [route(j, x, y, c, chips)[1] for j in range(n_slots)])
        for k in range(n):
            for j in range(n_slots):
                block, to = route(j, x, y, c, chips)
                pltpu.make_async_remote_copy(
                    src_ref=s_refs[k].at[block], dst_ref=land_refs[k].at[j], send_sem=send_sems.at[n_slots * k + j],
                    recv_sem=recv_sems.at[n_slots * k + j], device_id=to, device_id_type=MESH).start()
        token[...] = jnp.zeros_like(token)

    lands = [jax.ShapeDtypeStruct((n_slots,) + s.shape[1:], s.dtype) for s in srcs]
    outs = pl.pallas_call(
        body, name=name,
        out_shape=(pltpu.SemaphoreType.DMA((n_slots * n,)), pltpu.SemaphoreType.DMA((n_slots * n,)),
                   *[pltpu.HBM(s.shape, s.dtype) for s in srcs], *[pltpu.HBM(l.shape, l.dtype) for l in lands],
                   jax.ShapeDtypeStruct((8, 128), F32)),
        in_specs=[_HBM] * (2 * n), out_specs=(_SEM, _SEM, *[_HBM] * (2 * n), pl.BlockSpec(memory_space=pltpu.VMEM)),
        input_output_aliases={k: 2 + k for k in range(2 * n)},
        compiler_params=pltpu.CompilerParams(has_side_effects=_SIDE_EFFECT, collective_id=collective_id),
    )(*[pltpu.with_memory_space_constraint(s, pltpu.HBM) for s in srcs],
      *[pltpu.with_memory_space_constraint(lax.empty(l.shape, l.dtype), pltpu.HBM) for l in lands])
    return outs[0], outs[1], outs[2:2 + n], outs[2 + n:2 + 2 * n], outs[-1]


def _exchange_wait(send_sems, recv_sems, s_thru, land_thru, after, n_slots, route, name):
    n = len(s_thru)

    def body(*refs):
        s_refs, land_refs = refs[:n], refs[n:2 * n]
        send_sems, recv_sems = refs[2 * n:2 * n + 2]
        x, y, c, chips = _position()
        for k in range(n):
            for j in range(n_slots):
                block, to = route(j, x, y, c, chips)
                cp = pltpu.make_async_remote_copy(
                    src_ref=s_refs[k].at[block], dst_ref=land_refs[k].at[j], send_sem=send_sems.at[n_slots * k + j],
                    recv_sem=recv_sems.at[n_slots * k + j], device_id=to, device_id_type=MESH)
                cp.wait_send()
                cp.wait_recv()

    outs = pl.pallas_call(
        body, name=name,
        out_shape=(*[pltpu.HBM(s.shape, s.dtype) for s in s_thru], *[pltpu.HBM(l.shape, l.dtype) for l in land_thru]),
        in_specs=[_HBM] * (2 * n) + [_SEM, _SEM, pl.BlockSpec(memory_space=pl.ANY)], out_specs=[_HBM] * (2 * n),
        input_output_aliases={k: k for k in range(2 * n)},
        compiler_params=pltpu.CompilerParams(has_side_effects=_SIDE_EFFECT),
    )(*s_thru, *land_thru, send_sems, recv_sems, after)
    return outs[:n], outs[n:]


def _owner_table():
    x, y, c = lax.axis_index("x"), lax.axis_index("y"), lax.axis_index("c")
    chips = [(x, y), (1 - x, y), (x, 1 - y), (1 - x, 1 - y)]
    return jnp.stack([_dev(px, py, c) for px, py in chips]).astype(jnp.int32)


def _chip_partial_sums(table, parts, from_sibling, name):
    n = len(parts)

    def body(tab_ref, *refs):
        del tab_ref
        for g_ref, l_ref, out_ref in zip(refs[:n], refs[n:2 * n], refs[2 * n:]):
            out_ref[...] = (g_ref[...].astype(F32) + l_ref[...].astype(F32)).astype(out_ref.dtype)

    block = lambda p: (None,) + p.shape[1:]
    grid_spec = pltpu.PrefetchScalarGridSpec(
        num_scalar_prefetch=1, grid=(3,),
        in_specs=[pl.BlockSpec(block(p), lambda j, tab: (tab[j + 1], 0, 0)) for p in parts]
        + [pl.BlockSpec(block(p), lambda j, tab: (j + 1, 0, 0)) for p in parts],
        out_specs=[pl.BlockSpec(block(p), lambda j, tab: (j, 0, 0)) for p in parts])
    return pl.pallas_call(
        body, name=name, grid_spec=grid_spec,
        out_shape=[jax.ShapeDtypeStruct((3,) + p.shape[1:], BF16) for p in parts],
        compiler_params=_params(("arbitrary",)),
    )(table, *parts, *from_sibling)


def _final_update(table, parts, from_sibling, from_chips, states, name):
    n = len(parts)
    flipped = [states[k][0].shape != parts[k].shape[1:] for k in range(n)]

    def body(tab_ref, *refs):
        del tab_ref
        ins, outs = refs[:6 * n], refs[6 * n:]
        for k in range(n):
            acc = ins[k][...].astype(F32) + ins[n + k][...].astype(F32)
            for j in range(3):
                acc = acc + ins[2 * n + k][j].astype(F32)
            if flipped[k]:
                acc = acc.T
            w_ref, m_ref, v_ref = ins[3 * n + 3 * k:3 * n + 3 * k + 3]
            outs[4 * k][...] = acc
            for out_ref, val in zip(outs[4 * k + 1:4 * k + 4], _adamw_update(w_ref[...], acc, m_ref[...], v_ref[...])):
                out_ref[...] = val

    def grad_block(k, lead, at):
        r, c = parts[k].shape[1:]
        if flipped[k]:
            return pl.BlockSpec(lead + (r, c // 2), lambda t, tab: (*at(tab), 0, t))
        return pl.BlockSpec(lead + (r // 2, c), lambda t, tab: (*at(tab), t, 0))

    def state_block(k):
        a, b = states[k][0].shape
        return pl.BlockSpec((a // 2, b), lambda t, tab: (t, 0))

    grid_spec = pltpu.PrefetchScalarGridSpec(
        num_scalar_prefetch=1, grid=(2,),
        in_specs=[grad_block(k, (None,), lambda tab: (tab[0],)) for k in range(n)]
        + [grad_block(k, (None,), lambda tab: (0,)) for k in range(n)]
        + [grad_block(k, (3,), lambda tab: (0,)) for k in range(n)]
        + [state_block(k) for k in range(n) for _ in range(3)],
        out_specs=[state_block(k) for k in range(n) for _ in range(4)])
    outs = pl.pallas_call(
        body, name=name, grid_spec=grid_spec,
        out_shape=[jax.ShapeDtypeStruct(states[k][0].shape, F32) for k in range(n) for _ in range(4)],
        compiler_params=_params(("arbitrary",)),
    )(table, *parts, *from_sibling, *from_chips, *[t for k in range(n) for t in states[k]])
    return [outs[4 * k:4 * k + 4] for k in range(n)]


def _sum_blocks(g8):
    _, rows, cols = g8.shape

    def body(g_ref, out_ref):
        acc = g_ref[0]
        for d in range(1, N_DEV):
            acc = acc + g_ref[d]
        out_ref[...] = acc

    return pl.pallas_call(
        body, name="small_grad_sum", grid=(1,),
        in_specs=[_full((N_DEV, rows, cols))], out_specs=_full((rows, cols)),
        out_shape=jax.ShapeDtypeStruct((rows, cols), F32),
        compiler_params=_params(("arbitrary",)),
    )(g8)


def _fwd_mix(x2d, gw, g_mix, conv_w, conv_b, ln_g, ln_b, pool_w, pool_scale, after, seq, tm):
    tokens = x2d.shape[0]
    n_tiles = tokens // tm
    tps = seq // tm

    def body(x_ref, gmix_ref, gw_hbm, cw_ref, cb_ref, lng_ref, lnb_ref, pw_ref, ps_ref, after_ref,
             x1_ref, u_ref, c_ref, pooled_ref, ymix_ref, h1_ref,
             win_v, wout_v, hc_carry, up_carry, sem):
        del after_ref
        i = pl.program_id(0)

        _start_weights(gw_hbm, ("w_in", "w_out"), (win_v, wout_v), sem)

        @pl.when(i % tps == 0)
        def _():
            hc_carry[...] = jnp.zeros_like(hc_carry)
            up_carry[...] = jnp.zeros_like(up_carry)

        x = x_ref[...]
        xh, _ = _rms_fwd(x)
        h1 = (xh * gmix_ref[...]).astype(BF16)
        h1_ref[...] = h1
        u = _dot_nt(h1, win_v[...])
        u_ref[...] = u
        val, gate, up = u[:, :D_CONV], u[:, D_CONV:2 * D_CONV], u[:, 2 * D_CONV:]

        extp = jnp.concatenate([up_carry[...], up], axis=0)
        up_carry[...] = up[tm - POOL_HALO:, :]
        pos = lax.broadcasted_iota(jnp.int32, (tm, 1), 0) + (i % tps) * tm
        run = extp
        mixed = []
        for g, w in enumerate(POOL_WINDOWS):
            lo = g * POOL_GROUP_DIM
            run = run[:, POOL_GROUP_DIM if g else 0:]
            run = run + pltpu.roll(run, w // 2, 0)
            cnt = jnp.minimum(pos + 1, w).astype(F32)
            pooled = run[POOL_HALO:, :POOL_GROUP_DIM] / cnt - up[:, lo:lo + POOL_GROUP_DIM]
            pooled = pooled.astype(BF16)
            pooled_ref[:, lo:lo + POOL_GROUP_DIM] = pooled
            mixed.append(_dot(pooled, pw_ref[g].astype(BF16)))
        y_pool = jnp.concatenate(mixed, axis=-1) * ps_ref[...]
        y_pool = y_pool.astype(BF16)
        ymix_ref[:, D_CONV:] = y_pool
        out = _dot(y_pool, wout_v[D_CONV:, :])

        hc = val * _sigmoid(gate)
        ext = jnp.concatenate([hc_carry[...], hc], axis=0)
        hc_carry[...] = hc[tm - CONV_HALO:, :]
        conv = jnp.broadcast_to(cb_ref[...], (tm, D_CONV))
        ahead_by = _sublane_shifts(ext)
        for k in range(CONV_WIDTH):
            whole, part = divmod(CONV_HALO - (CONV_WIDTH - 1) + k, 8)
            conv = conv + cw_ref[k:k + 1, :] * ahead_by[part][8 * whole:8 * whole + tm, :]
        c_ref[...] = conv
        mu = jnp.mean(conv, axis=-1, keepdims=True)
        cen = conv - mu
        ln = cen * lax.rsqrt(jnp.mean(cen * cen, axis=-1, keepdims=True) + EPS) * lng_ref[...] + lnb_ref[...]
        y_conv = ln * _sigmoid(ln)
        y_conv = y_conv.astype(BF16)
        ymix_ref[:, :D_CONV] = y_conv
        x1_ref[...] = x + (out + _dot(y_conv, wout_v[:D_CONV, :]))

    row = lambda w: pl.BlockSpec((tm, w), lambda i: (i, 0))
    return pl.pallas_call(
        body, name="fwd_mix", grid=(n_tiles,),
        in_specs=[row(D_MODEL), _full((1, D_MODEL)), pl.BlockSpec(memory_space=pl.ANY),
                  _full((CONV_WIDTH, D_CONV)), _full((1, D_CONV)), _full((1, D_CONV)), _full((1, D_CONV)),
                  _full((4, POOL_GROUP_DIM, POOL_GROUP_DIM)), _full((1, D_POOL)), _full(after.shape)],
        out_specs=[row(D_MODEL), row(D_IN), row(D_CONV), row(D_POOL), row(D_MODEL), row(D_MODEL)],
        out_shape=[jax.ShapeDtypeStruct((tokens, D_MODEL), F32), jax.ShapeDtypeStruct((tokens, D_IN), F32),
                   jax.ShapeDtypeStruct((tokens, D_CONV), F32), jax.ShapeDtypeStruct((tokens, D_POOL), BF16),
                   jax.ShapeDtypeStruct((tokens, D_MODEL), BF16), jax.ShapeDtypeStruct((tokens, D_MODEL), BF16)],
        scratch_shapes=[pltpu.VMEM((D_IN, D_MODEL), BF16), pltpu.VMEM((D_MODEL, D_MODEL), BF16),
                        pltpu.VMEM((CONV_HALO, D_CONV), F32), pltpu.VMEM((POOL_HALO, D_POOL), F32),
                        pltpu.SemaphoreType.DMA((2,))],
        compiler_params=_params(),
    )(x2d, g_mix, gw, conv_w, conv_b, ln_g, ln_b, pool_w, pool_scale, after)


def _fwd_kv(mem2d, gw, g_mem):
    rows = mem2d.shape[0]
    n_b = rows // N_MEM

    def body(mem_ref, g_ref, gw_hbm, mn_ref, kv_ref, wkv_v, sem):
        @pl.when(pl.program_id(0) == 0)
        def _():
            copies = _load_weight(gw_hbm, "w_kv", wkv_v, sem)
            for cp in copies:
                cp.start()
            for cp in copies:
                cp.wait()

        mh, _ = _rms_fwd(mem_ref[...])
        mn = (mh * g_ref[...]).astype(BF16)
        mn_ref[...] = mn
        kv_ref[...] = _dot_nt(mn, wkv_v[...]).astype(BF16)

    return pl.pallas_call(
        body, name="fwd_kv", grid=(n_b,),
        in_specs=[pl.BlockSpec((N_MEM, D_MODEL), lambda b: (b, 0)), _full((1, D_MODEL)), pl.BlockSpec(memory_space=pl.ANY)],
        out_specs=[pl.BlockSpec((N_MEM, D_MODEL), lambda b: (b, 0)), pl.BlockSpec((N_MEM, 2 * D_MODEL), lambda b: (b, 0))],
        out_shape=[jax.ShapeDtypeStruct((rows, D_MODEL), BF16), jax.ShapeDtypeStruct((rows, 2 * D_MODEL), BF16)],
        scratch_shapes=[pltpu.VMEM((2 * D_MODEL, D_MODEL), BF16), pltpu.SemaphoreType.DMA],
        compiler_params=_params(),
    )(mem2d, g_mem, gw)


def _softmax_rows(s):
    e = jnp.exp(s - jnp.max(s, axis=-1, keepdims=True))
    return e / jnp.sum(e, axis=-1, keepdims=True)


def _fwd_attn(x1, kv, gw, g_x, seq, tm):
    tokens = x1.shape[0]
    n_tiles = tokens // tm
    tps = seq // tm

    def body(x1_ref, kv_ref, g_ref, gw_hbm, x2_ref, h2_ref, q_ref, o_ref, wq_v, wo_v, sem):
        _start_weights(gw_hbm, ("w_q", "w_o"), (wq_v, wo_v), sem)
        x1v = x1_ref[...]
        xh, _ = _rms_fwd(x1v)
        h2 = (xh * g_ref[...]).astype(BF16)
        h2_ref[...] = h2
        q = (_dot(h2, wq_v[...]) * (HEAD_DIM ** -0.5)).astype(BF16)
        q_ref[...] = q
        heads = [slice(h * HEAD_DIM, (h + 1) * HEAD_DIM) for h in range(HEADS)]
        scores = [_dot_nt(q[:, hd], kv_ref[:, hd]) for hd in heads]
        probs = [_softmax_rows(s).astype(BF16) for s in scores]
        outs = [_dot(p, kv_ref[:, pl.ds(D_MODEL + h * HEAD_DIM, HEAD_DIM)]) for h, p in enumerate(probs)]
        o = jnp.concatenate(outs, axis=-1).astype(BF16)
        o_ref[...] = o
        x2_ref[...] = x1v + _dot(o, wo_v[...])

    row = lambda w: pl.BlockSpec((tm, w), lambda i: (i, 0))
    return pl.pallas_call(
        body, name="fwd_attn", grid=(n_tiles,),
        in_specs=[row(D_MODEL), pl.BlockSpec((N_MEM, 2 * D_MODEL), lambda i: (i // tps, 0)), _full((1, D_MODEL)),
                  pl.BlockSpec(memory_space=pl.ANY)],
        out_specs=[row(D_MODEL)] * 4,
        out_shape=[jax.ShapeDtypeStruct((tokens, D_MODEL), F32)] + [jax.ShapeDtypeStruct((tokens, D_MODEL), BF16)] * 3,
        scratch_shapes=[pltpu.VMEM((D_MODEL, D_MODEL), BF16), pltpu.VMEM((D_MODEL, D_MODEL), BF16), pltpu.SemaphoreType.DMA((2,))],
        compiler_params=_params(),
    )(x1, kv, g_x, gw)


def _ffn_conv(uu, halo, w_ref, b_ref, cols):
    ext = jnp.concatenate([halo, uu], axis=0)
    p1 = pltpu.roll(ext, 1, 0)[FFN_HALO:, :]
    p2 = pltpu.roll(ext, 2, 0)[FFN_HALO:, :]
    return b_ref[:, cols] + w_ref[2:3, cols] * uu + w_ref[1:2, cols] * p1 + w_ref[0:1, cols] * p2


def _fwd_ffn(x2, target, gw, g_ffn, ffn_w, ffn_b, g_final, seq, tm):
    tokens = x2.shape[0]
    n_tiles = tokens // tm
    tps = seq // tm
    n_chunks = D_FF // FFN_CHUNK

    def body(x2_ref, tgt_ref, gffn_ref, gw_hbm, fw_ref, fb_ref, gfin_ref,
             uu_ref, cc_ref, a_ref, h3_ref, dx3_ref, dx3b_ref, loss_ref, dgfin_ref,
             wup_v, wdown_v, carry, sem):
        i = pl.program_id(0)

        _start_weights(gw_hbm, ("w_up", "w_down"), (wup_v, wdown_v), sem)

        @pl.when(i == 0)
        def _():
            loss_ref[...] = jnp.zeros_like(loss_ref)
            dgfin_ref[...] = jnp.zeros_like(dgfin_ref)

        @pl.when(i % tps == 0)
        def _():
            carry[...] = jnp.zeros_like(carry)

        x2v = x2_ref[...]
        xh, _ = _rms_fwd(x2v)
        h3 = (xh * gffn_ref[...]).astype(BF16)
        h3_ref[...] = h3
        acc = jnp.zeros((tm, D_MODEL), F32)
        for jc in range(n_chunks):
            halves = []
            for half in range(2):
                cols = pl.ds(half * D_FF + jc * FFN_CHUNK, FFN_CHUNK)
                uu = _dot_nt(h3, wup_v[cols, :])
                uu_ref[:, cols] = uu.astype(BF16)
                cc = _ffn_conv(uu, carry[:, cols], fw_ref, fb_ref, cols)
                cc_ref[:, cols] = cc.astype(BF16)
                halves.append(cc)
                carry[:, cols] = uu[tm - FFN_HALO:, :]
            gate, val = halves
            a = (gate * _sigmoid(gate) * val).astype(BF16)
            a_ref[:, pl.ds(jc * FFN_CHUNK, FFN_CHUNK)] = a
            acc = acc + _dot(a, wdown_v[pl.ds(jc * FFN_CHUNK, FFN_CHUNK), :])
        x3 = x2v + acc

        xh3, r3 = _rms_fwd(x3)
        gfin = gfin_ref[...]
        err = xh3 * gfin - tgt_ref[...]
        loss_ref[...] += jnp.full(loss_ref.shape, jnp.sum(err * err) * (0.5 / D_MODEL), F32)
        dy = err * (1.0 / D_MODEL)
        dgfin_ref[...] += _colsum(dy * xh3)
        dx3 = _rms_bwd(dy, xh3, r3, gfin)
        dx3_ref[...] = dx3
        dx3b_ref[...] = dx3.astype(BF16)

    row = lambda w: pl.BlockSpec((tm, w), lambda i: (i, 0))
    return pl.pallas_call(
        body, name="fwd_ffn", grid=(n_tiles,),
        in_specs=[row(D_MODEL), row(D_MODEL), _full((1, D_MODEL)), pl.BlockSpec(memory_space=pl.ANY),
                  _full((FFN_CONV_WIDTH, 2 * D_FF)), _full((1, 2 * D_FF)), _full((1, D_MODEL))],
        out_specs=[row(2 * D_FF), row(2 * D_FF), row(D_FF), row(D_MODEL), row(D_MODEL), row(D_MODEL), _full((8, 128)),
                   _full((1, D_MODEL))],
        out_shape=[jax.ShapeDtypeStruct((tokens, 2 * D_FF), BF16), jax.ShapeDtypeStruct((tokens, 2 * D_FF), BF16),
                   jax.ShapeDtypeStruct((tokens, D_FF), BF16),
                   jax.ShapeDtypeStruct((tokens, D_MODEL), BF16), jax.ShapeDtypeStruct((tokens, D_MODEL), F32),
                   jax.ShapeDtypeStruct((tokens, D_MODEL), BF16),
                   jax.ShapeDtypeStruct((8, 128), F32), jax.ShapeDtypeStruct((1, D_MODEL), F32)],
        scratch_shapes=[pltpu.VMEM((2 * D_FF, D_MODEL), BF16), pltpu.VMEM((D_FF, D_MODEL), BF16),
                        pltpu.VMEM((FFN_HALO, 2 * D_FF), F32), pltpu.SemaphoreType.DMA((2,))],
        compiler_params=_params(),
    )(x2, target, g_ffn, gw, ffn_w, ffn_b, g_final)


def _bwd_ffn(dx3, x2, uu_all, cc_all, gw, g_ffn, ffn_w, seq, tm):
    tokens = x2.shape[0]
    n_tiles = tokens // tm
    tps = seq // tm
    n_chunks = D_FF // FFN_CHUNK

    def body(dx3_ref, x2_ref, uu_ref, cc_ref, gffn_ref, gw_hbm, fw_ref,
             dx2_ref, dx2b_ref, duu_ref, dfb_ref, dfw_ref, dg_ref,
             wup_v, wdown_v, carry, sem):
        i = pl.program_id(0)
        t = n_tiles - 1 - i

        _start_weights(gw_hbm, ("w_down", "w_up"), (wdown_v, wup_v), sem)

        @pl.when(i == 0)
        def _():
            dfb_ref[...] = jnp.zeros_like(dfb_ref)
            dfw_ref[...] = jnp.zeros_like(dfw_ref)
            dg_ref[...] = jnp.zeros_like(dg_ref)

        @pl.when(t % tps == tps - 1)
        def _():
            carry[...] = jnp.zeros_like(carry)

        dx3v = dx3_ref[...]
        dx3b = dx3v.astype(BF16)
        dh3 = jnp.zeros((tm, D_MODEL), F32)
        for jc in range(n_chunks):
            da = _dot_nt(dx3b, wdown_v[pl.ds(jc * FFN_CHUNK, FFN_CHUNK), :])
            colss = [pl.ds(half * D_FF + jc * FFN_CHUNK, FFN_CHUNK) for half in range(2)]
            gate, val = [cc_ref[:, cols].astype(F32) for cols in colss]
            sg = _sigmoid(gate)
            dgate = da * val * (sg * (1.0 + gate * (1.0 - sg)))
            dval = da * (gate * sg)
            for dcc, cols in zip((dgate, dval), colss):
                uu = uu_ref[:, cols].astype(F32)
                dfb_ref[:, cols] += _colsum(dcc)
                ext = jnp.concatenate([dcc, carry[:, cols]], axis=0)
                carry[:, cols] = dcc[:FFN_HALO, :]
                n1 = pltpu.roll(ext, tm + FFN_HALO - 1, 0)[:tm, :]
                n2 = pltpu.roll(ext, tm + FFN_HALO - 2, 0)[:tm, :]
                duu = fw_ref[2:3, cols] * dcc + fw_ref[1:2, cols] * n1 + fw_ref[0:1, cols] * n2
                dfw_ref[2:3, cols] += _colsum(uu * dcc)
                dfw_ref[1:2, cols] += _colsum(uu * n1)
                dfw_ref[0:1, cols] += _colsum(uu * n2)
                duub = duu.astype(BF16)
                duu_ref[:, cols] = duub
                dh3 = dh3 + _dot(duub, wup_v[cols, :])
        xh, r = _rms_fwd(x2_ref[...])
        dg_ref[...] += _colsum(dh3 * xh)
        dx2 = dx3v + _rms_bwd(dh3, xh, r, gffn_ref[...])
        dx2_ref[...] = dx2
        dx2b_ref[...] = dx2.astype(BF16)

    rev = lambda w: pl.BlockSpec((tm, w), lambda i: (n_tiles - 1 - i, 0))
    return pl.pallas_call(
        body, name="bwd_ffn", grid=(n_tiles,),
        in_specs=[rev(D_MODEL), rev(D_MODEL), rev(2 * D_FF), rev(2 * D_FF), _full((1, D_MODEL)),
                  pl.BlockSpec(memory_space=pl.ANY), _full((FFN_CONV_WIDTH, 2 * D_FF))],
        out_specs=[rev(D_MODEL), rev(D_MODEL), rev(2 * D_FF), _full((1, 2 * D_FF)), _full((FFN_CONV_WIDTH, 2 * D_FF)),
                   _full((1, D_MODEL))],
        out_shape=[jax.ShapeDtypeStruct((tokens, D_MODEL), F32), jax.ShapeDtypeStruct((tokens, D_MODEL), BF16),
                   jax.ShapeDtypeStruct((tokens, 2 * D_FF), BF16),
                   jax.ShapeDtypeStruct((1, 2 * D_FF), F32), jax.ShapeDtypeStruct((FFN_CONV_WIDTH, 2 * D_FF), F32),
                   jax.ShapeDtypeStruct((1, D_MODEL), F32)],
        scratch_shapes=[pltpu.VMEM((2 * D_FF, D_MODEL), BF16), pltpu.VMEM((D_FF, D_MODEL), BF16),
                        pltpu.VMEM((FFN_HALO, 2 * D_FF), F32), pltpu.SemaphoreType.DMA((2,))],
        compiler_params=_params(),
    )(dx3, x2, uu_all, cc_all, g_ffn, gw, ffn_w)


def _bwd_attn(dx2, x1, q, kv, gw, g_x, after, seq, tm):
    tokens = x1.shape[0]
    n_tiles = tokens // tm
    tps = seq // tm
    n_b = tokens // seq

    def body(dx2_ref, x1_ref, q_ref, kv_ref, g_ref, gw_hbm, after_ref, dx1_ref, dx1b_ref, dq_ref, dkv_ref, dg_ref,
             wq_v, wo_v, sem):
        del after_ref
        i = pl.program_id(0)

        _start_weights(gw_hbm, ("w_o", "w_q"), (wo_v, wq_v), sem)

        @pl.when(i == 0)
        def _():
            dg_ref[...] = jnp.zeros_like(dg_ref)

        @pl.when(i % tps == 0)
        def _():
            dkv_ref[...] = jnp.zeros_like(dkv_ref)

        dx2v = dx2_ref[...]
        do = _dot_nt(dx2v.astype(BF16), wo_v[...]).astype(BF16)
        q = q_ref[...]
        heads = [slice(h * HEAD_DIM, (h + 1) * HEAD_DIM) for h in range(HEADS)]
        kcols = [pl.ds(h * HEAD_DIM, HEAD_DIM) for h in range(HEADS)]
        vcols = [pl.ds(D_MODEL + h * HEAD_DIM, HEAD_DIM) for h in range(HEADS)]
        scores = [_dot_nt(q[:, hd], kv_ref[:, kc]) for hd, kc in zip(heads, kcols)]
        dps = [_dot_nt(do[:, hd], kv_ref[:, vc]) for hd, vc in zip(heads, vcols)]
        probs = [_softmax_rows(s) for s in scores]
        dss = [(p * (dp - jnp.sum(dp * p, axis=-1, keepdims=True))).astype(BF16) for p, dp in zip(probs, dps)]
        for p, hd, vc in zip(probs, heads, vcols):
            dkv_ref[:, vc] += _dot_tn(p.astype(BF16), do[:, hd])
        dqs = [_dot(ds, kv_ref[:, kc]) * (HEAD_DIM ** -0.5) for ds, kc in zip(dss, kcols)]
        for ds, hd, kc in zip(dss, heads, kcols):
            dkv_ref[:, kc] += _dot_tn(ds, q[:, hd])
        dq = jnp.concatenate(dqs, axis=-1).astype(BF16)
        dq_ref[...] = dq
        dh2 = _dot_nt(dq, wq_v[...])
        xh, r = _rms_fwd(x1_ref[...])
        dg_ref[...] += _colsum(dh2 * xh)
        dx1 = dx2v + _rms_bwd(dh2, xh, r, g_ref[...])
        dx1_ref[...] = dx1
        dx1b_ref[...] = dx1.astype(BF16)

    row = lambda w: pl.BlockSpec((tm, w), lambda i: (i, 0))
    per_b = pl.BlockSpec((N_MEM, 2 * D_MODEL), lambda i: (i // tps, 0))
    return pl.pallas_call(
        body, name="bwd_attn", grid=(n_tiles,),
        in_specs=[row(D_MODEL), row(D_MODEL), row(D_MODEL), per_b, _full((1, D_MODEL)), pl.BlockSpec(memory_space=pl.ANY),
                  _full(after.shape)],
        out_specs=[row(D_MODEL), row(D_MODEL), row(D_MODEL), per_b, _full((1, D_MODEL))],
        out_shape=[jax.ShapeDtypeStruct((tokens, D_MODEL), F32), jax.ShapeDtypeStruct((tokens, D_MODEL), BF16),
                   jax.ShapeDtypeStruct((tokens, D_MODEL), BF16),
                   jax.ShapeDtypeStruct((n_b * N_MEM, 2 * D_MODEL), F32), jax.ShapeDtypeStruct((1, D_MODEL), F32)],
        scratch_shapes=[pltpu.VMEM((D_MODEL, D_MODEL), BF16), pltpu.VMEM((D_MODEL, D_MODEL), BF16), pltpu.SemaphoreType.DMA((2,))],
        compiler_params=_params(),
    )(dx2, x1, q, kv, g_x, gw, after)


def _bwd_kv(dkv, mem2d, gw):
    rows = mem2d.shape[0]
    n_b = rows // N_MEM

    def body(dkv_ref, mem_ref, gw_hbm, dkvb_ref, dg_ref, wkv_v, sem):
        @pl.when(pl.program_id(0) == 0)
        def _():
            copies = _load_weight(gw_hbm, "w_kv", wkv_v, sem)
            for cp in copies:
                cp.start()
            for cp in copies:
                cp.wait()
            dg_ref[...] = jnp.zeros_like(dg_ref)

        dkvb = dkv_ref[...].astype(BF16)
        dkvb_ref[...] = dkvb
        dmn = _dot(dkvb, wkv_v[...])
        mh, _ = _rms_fwd(mem_ref[...])
        dg_ref[...] += _colsum(dmn * mh)

    return pl.pallas_call(
        body, name="bwd_kv", grid=(n_b,),
        in_specs=[pl.BlockSpec((N_MEM, 2 * D_MODEL), lambda b: (b, 0)), pl.BlockSpec((N_MEM, D_MODEL), lambda b: (b, 0)),
                  pl.BlockSpec(memory_space=pl.ANY)],
        out_specs=[pl.BlockSpec((N_MEM, 2 * D_MODEL), lambda b: (b, 0)), _full((1, D_MODEL))],
        out_shape=[jax.ShapeDtypeStruct((rows, 2 * D_MODEL), BF16), jax.ShapeDtypeStruct((1, D_MODEL), F32)],
        scratch_shapes=[pltpu.VMEM((2 * D_MODEL, D_MODEL), BF16), pltpu.SemaphoreType.DMA],
        compiler_params=_params(),
    )(dkv, mem2d, gw)


def _bwd_mix(dx1, x2d, u_all, c_all, pooled_all, gw, g_mix, conv_w, ln_g, ln_b, pool_w, pool_scale, after, seq, tm):
    tokens = x2d.shape[0]
    n_tiles = tokens // tm
    tps = seq // tm

    def body(dx1_ref, x_ref, u_ref, c_ref, pooled_ref, gmix_ref, gw_hbm, cw_ref, lng_ref, lnb_ref, pw_ref, ps_ref,
             after_ref, dx_ref, du_ref, dgmix_ref, dcw_ref, dcb_ref, dlng_ref, dlnb_ref, dpw_ref, dps_ref,
             win_v, wout_v, dc_carry, e_carry, sem):
        del after_ref
        i = pl.program_id(0)
        t = n_tiles - 1 - i

        _start_weights(gw_hbm, ("w_out", "w_in"), (wout_v, win_v), sem)

        @pl.when(i == 0)
        def _():
            for ref in (dgmix_ref, dcw_ref, dcb_ref, dlng_ref, dlnb_ref, dpw_ref, dps_ref):
                ref[...] = jnp.zeros_like(ref)

        @pl.when(t % tps == tps - 1)
        def _():
            dc_carry[...] = jnp.zeros_like(dc_carry)
            e_carry[...] = jnp.zeros_like(e_carry)

        dx1v = dx1_ref[...]
        dymix = _dot_nt(dx1v.astype(BF16), wout_v[...])
        dyc, dyp = dymix[:, :D_CONV], dymix[:, D_CONV:]
        u = u_ref[...]
        val, gate = u[:, :D_CONV], u[:, D_CONV:2 * D_CONV]

        conv = c_ref[...]
        mu = jnp.mean(conv, axis=-1, keepdims=True)
        cen = conv - mu
        rs = lax.rsqrt(jnp.mean(cen * cen, axis=-1, keepdims=True) + EPS)
        chat = cen * rs
        ln = chat * lng_ref[...] + lnb_ref[...]
        sl = _sigmoid(ln)
        dln = dyc * (sl * (1.0 + ln * (1.0 - sl)))
        dlng_ref[...] += _colsum(dln * chat)
        dlnb_ref[...] += _colsum(dln)
        dchat = dln * lng_ref[...]
        dc = rs * (dchat - jnp.mean(dchat, axis=-1, keepdims=True)
                   - chat * jnp.mean(dchat * chat, axis=-1, keepdims=True))
        dcb_ref[...] += _colsum(dc)
        sg = _sigmoid(gate)
        hc = val * sg
        ext = jnp.concatenate([dc, dc_carry[...]], axis=0)
        dc_carry[...] = dc[:CONV_HALO, :]
        dhc = jnp.zeros((tm, D_CONV), F32)
        ahead_by = _sublane_shifts(ext)
        for k in range(CONV_WIDTH):
            whole, part = divmod(CONV_WIDTH - 1 - k, 8)
            tap = ahead_by[part][8 * whole:8 * whole + tm, :]
            dhc = dhc + cw_ref[k:k + 1, :] * tap
            dcw_ref[k:k + 1, :] += _colsum_mxu(hc * tap)
        du_ref[:, :D_CONV] = (dhc * sg).astype(BF16)
        du_ref[:, D_CONV:2 * D_CONV] = (dhc * val * (sg * (1.0 - sg))).astype(BF16)

        pos = lax.broadcasted_iota(jnp.int32, (tm, 1), 0) + (t % tps) * tm
        es, dpooled = [], []
        for g, w in enumerate(POOL_WINDOWS):
            cols = pl.ds(g * POOL_GROUP_DIM, POOL_GROUP_DIM)
            lo = g * POOL_GROUP_DIM
            pooled = pooled_ref[:, cols]
            pw = pw_ref[g].astype(BF16)
            dyg = dyp[:, lo:lo + POOL_GROUP_DIM]
            dps_ref[:, cols] += _colsum(dyg * _dot(pooled, pw))
            dmixed = (dyg * ps_ref[:, cols]).astype(BF16)
            dpw_ref[g] += _dot_tn(pooled, dmixed)
            dpo = _dot_nt(dmixed, pw)
            dpooled.append(dpo)
            es.append(dpo / jnp.minimum(pos + 1, w).astype(F32))
        e = jnp.concatenate(es, axis=-1)
        run = jnp.concatenate([e, e_carry[...]], axis=0)
        e_carry[...] = e[:POOL_HALO, :]
        rows = tm + POOL_HALO
        for g, w in enumerate(POOL_WINDOWS):
            lo = g * POOL_GROUP_DIM
            run = run[:, POOL_GROUP_DIM if g else 0:]
            run = run + pltpu.roll(run, rows - w // 2, 0)
            du_ref[:, 2 * D_CONV + lo:2 * D_CONV + lo + POOL_GROUP_DIM] = (
                run[:tm, :POOL_GROUP_DIM] - dpooled[g]).astype(BF16)

        dh1 = _dot(du_ref[...], win_v[...])
        xh, r = _rms_fwd(x_ref[...])
        dgmix_ref[...] += _colsum(dh1 * xh)
        dx_ref[...] = dx1v + _rms_bwd(dh1, xh, r, gmix_ref[...])

    rev = lambda w: pl.BlockSpec((tm, w), lambda i: (n_tiles - 1 - i, 0))
    return pl.pallas_call(
        body, name="bwd_mix", grid=(n_tiles,),
        in_specs=[rev(D_MODEL), rev(D_MODEL), rev(D_IN), rev(D_CONV), rev(D_POOL), _full((1, D_MODEL)),
                  pl.BlockSpec(memory_space=pl.ANY), _full((CONV_WIDTH, D_CONV)), _full((1, D_CONV)), _full((1, D_CONV)),
                  _full((4, POOL_GROUP_DIM, POOL_GROUP_DIM)), _full((1, D_POOL)), _full(after.shape)],
        out_specs=[rev(D_MODEL), rev(D_IN), _full((1, D_MODEL)), _full((CONV_WIDTH, D_CONV)), _full((1, D_CONV)),
                   _full((1, D_CONV)), _full((1, D_CONV)), _full((4, POOL_GROUP_DIM, POOL_GROUP_DIM)), _full((1, D_POOL))],
        out_shape=[jax.ShapeDtypeStruct((tokens, D_MODEL), F32), jax.ShapeDtypeStruct((tokens, D_IN), BF16),
                   jax.ShapeDtypeStruct((1, D_MODEL), F32), jax.ShapeDtypeStruct((CONV_WIDTH, D_CONV), F32),
                   jax.ShapeDtypeStruct((1, D_CONV), F32), jax.ShapeDtypeStruct((1, D_CONV), F32),
                   jax.ShapeDtypeStruct((1, D_CONV), F32),
                   jax.ShapeDtypeStruct((4, POOL_GROUP_DIM, POOL_GROUP_DIM), F32), jax.ShapeDtypeStruct((1, D_POOL), F32)],
        scratch_shapes=[pltpu.VMEM((D_IN, D_MODEL), BF16), pltpu.VMEM((D_MODEL, D_MODEL), BF16),
                        pltpu.VMEM((CONV_HALO, D_CONV), F32), pltpu.VMEM((POOL_HALO, D_POOL), F32),
                        pltpu.SemaphoreType.DMA((2,))],
        compiler_params=_params(),
    )(dx1, x2d, u_all, c_all, pooled_all, g_mix, gw, conv_w, ln_g, ln_b, pool_w, pool_scale, after)


def _wgrad(a, b, name, after=None):
    tokens, m = a.shape
    n = b.shape[1]
    tm = 512 if m % 512 == 0 else 256
    extra = [] if after is None else [after]

    def body(a_ref, b_ref, *rest):
        rest[-1][...] = _dot_tn(a_ref[...], b_ref[...]).astype(rest[-1].dtype)

    return pl.pallas_call(
        body, name=name, grid=(m // tm,),
        in_specs=[pl.BlockSpec((tokens, tm), lambda i: (0, i)), _full((tokens, n))] + [_full(t.shape) for t in extra],
        out_specs=pl.BlockSpec((tm, n), lambda i: (i, 0)),
        out_shape=jax.ShapeDtypeStruct((m, n), BF16),
        compiler_params=_params(),
    )(a, b, *extra)


def _adamw_update(w, g, m, v):
    nm = ADAM_B1 * m + (1.0 - ADAM_B1) * g
    nv = ADAM_B2 * v + (1.0 - ADAM_B2) * (g * g)
    m_hat = nm / (1.0 - ADAM_B1 ** ADAM_STEP)
    v_hat = nv / (1.0 - ADAM_B2 ** ADAM_STEP)
    return -ADAM_LR * (m_hat / (jnp.sqrt(v_hat) + ADAM_EPS) + ADAM_WD * w), nm, nv


def _adamw_small(ws, gs, ms, vs):
    n = len(ws)

    def body(*refs):
        ins, outs = refs[:4 * n], refs[4 * n:]
        for k in range(n):
            d, nm, nv = _adamw_update(*[ins[j * n + k][...] for j in range(4)])
            outs[k][...] = d
            outs[n + k][...] = nm
            outs[2 * n + k][...] = nv

    vmem = pl.BlockSpec(memory_space=pltpu.VMEM)
    outs = pl.pallas_call(
        body, name="adamw_small",
        in_specs=[vmem] * (4 * n), out_specs=[vmem] * (3 * n),
        out_shape=[jax.ShapeDtypeStruct(w.shape, F32) for w in ws] * 3,
    )(*ws, *gs, *ms, *vs)
    return outs[:n], outs[n:2 * n], outs[2 * n:]


SMALL = (("norm_mix_g", (1, 1024)), ("conv_dw_b", (1, 512)), ("conv_ln_g", (1, 512)), ("conv_ln_b", (1, 512)),
         ("pool_w", (1, 4, 128, 128)), ("pool_scale", (1, 512)), ("norm_xattn_g", (1, 1024)), ("norm_mem_g", (1, 1024)),
         ("norm_ffn_g", (1, 1024)), ("ffn_dw_b", (1, 5632)), ("norm_final_g", (1024,)))
LANES = 128


def _pack_rows(arrs):
    flat = jnp.concatenate([a.reshape(-1) for a in arrs])
    pad = (-flat.shape[0]) % (8 * LANES)
    return jnp.pad(flat, (0, pad)).reshape(-1, LANES)


def kernel(x, mem, norm_mix_g, w_in, conv_dw_w, conv_dw_b, conv_ln_g, conv_ln_b, pool_w, pool_scale, w_out, norm_xattn_g, norm_mem_g, w_q, w_kv, w_o, norm_ffn_g, w_up, ffn_dw_w, ffn_dw_b, w_down, norm_final_g, loss_target, m_norm_mix_g, m_w_in, m_conv_dw_w, m_conv_dw_b, m_conv_ln_g, m_conv_ln_b, m_pool_w, m_pool_scale, m_w_out, m_norm_xattn_g, m_norm_mem_g, m_w_q, m_w_kv, m_w_o, m_norm_ffn_g, m_w_up, m_ffn_dw_w, m_ffn_dw_b, m_w_down, m_norm_final_g, v_norm_mix_g, v_w_in, v_conv_dw_w, v_conv_dw_b, v_conv_ln_g, v_conv_ln_b, v_pool_w, v_pool_scale, v_w_out, v_norm_xattn_g, v_norm_mem_g, v_w_q, v_w_kv, v_w_o, v_norm_ffn_g, v_w_up, v_ffn_dw_w, v_ffn_dw_b, v_w_down, v_norm_final_g):
    weights = dict(norm_mix_g=norm_mix_g, w_in=w_in, conv_dw_w=conv_dw_w, conv_dw_b=conv_dw_b, conv_ln_g=conv_ln_g,
                   conv_ln_b=conv_ln_b, pool_w=pool_w, pool_scale=pool_scale, w_out=w_out, norm_xattn_g=norm_xattn_g,
                   norm_mem_g=norm_mem_g, w_q=w_q, w_kv=w_kv, w_o=w_o, norm_ffn_g=norm_ffn_g, w_up=w_up,
                   ffn_dw_w=ffn_dw_w, ffn_dw_b=ffn_dw_b, w_down=w_down, norm_final_g=norm_final_g)
    moments_m = dict(norm_mix_g=m_norm_mix_g, w_in=m_w_in, conv_dw_w=m_conv_dw_w, conv_dw_b=m_conv_dw_b,
                     conv_ln_g=m_conv_ln_g, conv_ln_b=m_conv_ln_b, pool_w=m_pool_w, pool_scale=m_pool_scale,
                     w_out=m_w_out, norm_xattn_g=m_norm_xattn_g, norm_mem_g=m_norm_mem_g, w_q=m_w_q, w_kv=m_w_kv,
                     w_o=m_w_o, norm_ffn_g=m_norm_ffn_g, w_up=m_w_up, ffn_dw_w=m_ffn_dw_w, ffn_dw_b=m_ffn_dw_b,
                     w_down=m_w_down, norm_final_g=m_norm_final_g)
    moments_v = dict(norm_mix_g=v_norm_mix_g, w_in=v_w_in, conv_dw_w=v_conv_dw_w, conv_dw_b=v_conv_dw_b,
                     conv_ln_g=v_conv_ln_g, conv_ln_b=v_conv_ln_b, pool_w=v_pool_w, pool_scale=v_pool_scale,
                     w_out=v_w_out, norm_xattn_g=v_norm_xattn_g, norm_mem_g=v_norm_mem_g, w_q=v_w_q, w_kv=v_w_kv,
                     w_o=v_w_o, norm_ffn_g=v_norm_ffn_g, w_up=v_w_up, ffn_dw_w=v_ffn_dw_w, ffn_dw_b=v_ffn_dw_b,
                     w_down=v_w_down, norm_final_g=v_norm_final_g)
    order = list(weights)
    transposed = ("w_in", "w_kv", "w_up")

    n_b, seq, _ = x.shape
    tokens = n_b * seq
    tm_mix = min(512, seq // 2)
    tm_attn = min(1024, seq // 2)
    tm_ffn = min(256, seq // 2)
    dev = 4 * lax.axis_index("x") + 2 * lax.axis_index("y") + lax.axis_index("c")

    packs = [jnp.concatenate([weights[n][0].T if n in transposed else weights[n][0] for n in names], axis=0).astype(BF16)
             for names in AG_GROUPS]
    small_sharded = _pack_rows([conv_dw_w[0], ffn_dw_w[0]])
    gw_mix, gsmall = _all_gather([packs[0], small_sharded], "weights_all_gather")
    flights = []
    after = gw_mix
    for k in (1, 2):
        own_in_place = lax.dynamic_update_slice(lax.empty((N_DEV,) + packs[k].shape, BF16), packs[k][None], (dev, 0, 0))
        flights.append(_gather_start(own_in_place, after, "weights_gather_start_%d" % k, BARRIER_IDS["gather_start"][k - 1]))
        after = flights[-1][3]
    gflat = gsmall.reshape(N_DEV, -1)
    n_cw = CONV_WIDTH * (D_CONV // N_DEV)
    n_fw = FFN_CONV_WIDTH * (2 * D_FF // N_DEV)
    conv_w = gflat[:, :n_cw].reshape(N_DEV, CONV_WIDTH, D_CONV // N_DEV).transpose(1, 0, 2).reshape(CONV_WIDTH, D_CONV)
    ffn_w = gflat[:, n_cw:n_cw + n_fw].reshape(N_DEV, FFN_CONV_WIDTH, 2 * D_FF // N_DEV).transpose(1, 0, 2).reshape(
        FFN_CONV_WIDTH, 2 * D_FF)

    x2d = x.reshape(tokens, D_MODEL)
    mem2d = mem.reshape(n_b * N_MEM, D_MODEL)
    tgt2d = loss_target.reshape(tokens, D_MODEL)
    g_final = norm_final_g.reshape(1, D_MODEL)

    def gather_finish(flight, after, tag):
        fwd_send, fwd_recv, buf = _gather_forward(*flight[:3], after, "weights_gather_forward_" + tag,
                                                  BARRIER_IDS["gather_forward"][int(tag) - 1])
        return _gather_finish(fwd_send, fwd_recv, buf, "weights_gather_finish_" + tag)

    x1, u_all, c_all, pooled_all, ymix, h1 = _fwd_mix(
        x2d, gw_mix, norm_mix_g, conv_w, conv_dw_b, conv_ln_g, conv_ln_b, pool_w[0], pool_scale, flights[1][3],
        seq, tm_mix)
    gw_attn = gather_finish(flights[0], x1, "1")
    mem_n, kv = _fwd_kv(mem2d, gw_attn, norm_mem_g)
    x2, h2, q, o = _fwd_attn(x1, kv, gw_attn, norm_xattn_g, seq, tm_attn)
    gw_ffn = gather_finish(flights[1], x2, "2")
    uu_all, cc_all, a_all, h3, dx3, dx3b, loss_part, dg_final = _fwd_ffn(
        x2, tgt2d, gw_ffn, norm_ffn_g, ffn_w, ffn_dw_b, g_final, seq, tm_ffn)

    table = _owner_table()

    def sibling_start(names, tag):
        parts = [part[n].reshape(N_DEV, W_OFF[n][1], D_MODEL) for n in names]
        return _exchange_start(parts, 4, _to_sibling, "rs_sibling_exchange_start_" + tag, BARRIER_IDS["sibling"][tag])

    def chips_start(flight, after, tag):
        parts, landed = _exchange_wait(*flight[:4], after, 4, _to_sibling, "rs_sibling_exchange_wait_" + tag)
        sums = _chip_partial_sums(table, parts, landed, "rs_chip_partial_sums_" + tag)
        return parts, landed, _exchange_start(sums, 3, _to_chip, "rs_chip_exchange_start_" + tag,
                                              BARRIER_IDS["chips"][tag])

    grads, delta, new_m, new_v = {}, {}, {}, {}

    def reduce_finish(names, parts, landed, flight, after, tag):
        _, from_chips = _exchange_wait(*flight[:4], after, 3, _to_chip, "rs_chip_exchange_wait_" + tag)
        as_rows = {n: n in transposed and W_OFF[n][1] % LANES != 0 for n in names}
        states = [tuple(t[n][0].T if as_rows[n] else t[n][0] for t in (weights, moments_m, moments_v)) for n in names]
        results = _final_update(table, parts, landed, from_chips, states, "rs_final_update_" + tag)
        for n, res in zip(names, results):
            grads[n], delta[n], new_m[n], new_v[n] = [t.T[None] if as_rows[n] else t[None] for t in res]
        return delta[names[-1]]

    part = {}
    dx2, dx2b, duu, d_ffn_b, d_ffn_w, dg_ffn = _bwd_ffn(dx3, x2, uu_all, cc_all, gw_ffn, norm_ffn_g, ffn_w, seq, tm_ffn)
    part["w_up"] = _wgrad(duu, h3, "wgrad_w_up")
    part["w_down"] = _wgrad(a_all, dx3b, "wgrad_w_down")
    to_sibling_a = sibling_start(RS_GROUPS["a"], "a")
    dx1, dx1b, dq, dkv, dg_x = _bwd_attn(dx2, x1, q, kv, gw_attn, norm_xattn_g, to_sibling_a[4], seq, tm_mix)
    dkv_b, dg_mem = _bwd_kv(dkv, mem2d, gw_attn)
    part["w_q"] = _wgrad(h2, dq, "wgrad_w_q")
    part["w_kv"] = _wgrad(dkv_b, mem_n, "wgrad_w_kv")
    part["w_o"] = _wgrad(o, dx2b, "wgrad_w_o")
    part["w_out"] = _wgrad(ymix, dx1b, "wgrad_w_out")
    to_sibling_b = sibling_start(RS_GROUPS["b"], "b")
    parts_a, landed_a, flight_a = chips_start(to_sibling_a, to_sibling_b[4], "a")
    parts_b, landed_b, flight_b = chips_start(to_sibling_b, flight_a[4], "b")
    dx, du, dg_mix, d_conv_w, d_conv_b, d_ln_g, d_ln_b, d_pool_w, d_pool_scale = _bwd_mix(
        dx1, x2d, u_all, c_all, pooled_all, gw_mix, norm_mix_g, conv_w, conv_ln_g, conv_ln_b, pool_w[0], pool_scale,
        flight_b[4], seq, tm_mix)
    grad_x = dx.reshape(x.shape)

    small_grads = dict(norm_mix_g=dg_mix, conv_dw_b=d_conv_b, conv_ln_g=d_ln_g, conv_ln_b=d_ln_b, pool_w=d_pool_w,
                       pool_scale=d_pool_scale, norm_xattn_g=dg_x, norm_mem_g=dg_mem, norm_ffn_g=dg_ffn,
                       ffn_dw_b=d_ffn_b, norm_final_g=dg_final)
    small_list = [small_grads[n] for n, _ in SMALL] + [d_conv_w, d_ffn_w, loss_part[:1]]
    small_mine = _pack_rows(small_list)
    small_flight = _broadcast_start(
        lax.dynamic_update_slice(lax.empty((N_DEV,) + small_mine.shape, F32), small_mine[None], (dev, 0, 0)),
        "small_grads_broadcast_start", BARRIER_IDS["broadcast"])

    part["w_in"] = _wgrad(du, h1, "wgrad_w_in", after=small_flight[3])
    to_sibling_c = sibling_start(RS_GROUPS["c"], "c")
    parts_c, landed_c, flight_c = chips_start(to_sibling_c, to_sibling_c[4], "c")
    updated_a = reduce_finish(RS_GROUPS["a"], parts_a, landed_a, flight_a, flight_c[4], "a")
    updated_b = reduce_finish(RS_GROUPS["b"], parts_b, landed_b, flight_b, updated_a, "b")
    small_all = _broadcast_wait(*small_flight[:3], updated_b, "small_grads_broadcast_wait")
    small_sum = _sum_blocks(small_all).reshape(-1)

    pos = 0
    for n, shape in SMALL:
        size = 1
        for s in shape:
            size *= s
        grads[n] = small_sum[pos:pos + size].reshape(shape)
        pos += size
    full_conv_w = small_sum[pos:pos + CONV_WIDTH * D_CONV].reshape(CONV_WIDTH, D_CONV)
    pos += CONV_WIDTH * D_CONV
    full_ffn_w = small_sum[pos:pos + FFN_CONV_WIDTH * 2 * D_FF].reshape(FFN_CONV_WIDTH, 2 * D_FF)
    loss = small_sum[pos + FFN_CONV_WIDTH * 2 * D_FF]
    grads["conv_dw_w"] = lax.dynamic_slice_in_dim(full_conv_w, dev * (D_CONV // N_DEV), D_CONV // N_DEV, axis=1)[None]
    grads["ffn_dw_w"] = lax.dynamic_slice_in_dim(full_ffn_w, dev * (2 * D_FF // N_DEV), 2 * D_FF // N_DEV, axis=1)[None]

    small_names = [n for n in order if n not in W_OFF]
    swap = lambda t: jnp.transpose(t, (1, 0, 2))
    two_d = lambda t: t.reshape(1, -1) if t.ndim == 1 else (swap(t) if t.ndim == 3 else t)
    outs = _adamw_small(*[[two_d(t[n]) for n in small_names] for t in (weights, grads, moments_m, moments_v)])
    for res, out in zip((delta, new_m, new_v), outs):
        for n, o in zip(small_names, out):
            res[n] = swap(o) if o.ndim == 3 else o.reshape(weights[n].shape)

    reduce_finish(RS_GROUPS["c"], parts_c, landed_c, flight_c, delta[small_names[-1]], "c")

    return (loss, grad_x, *[grads[n] for n in order], *[delta[n] for n in order],
            *[new_m[n] for n in order], *[new_v[n] for n in order])
```

```python
import jax
import jax.numpy as jnp
from jax import lax
from jax.experimental import pallas as pl
from jax.experimental.pallas import tpu as pltpu

F32 = jnp.float32
BF16 = jnp.bfloat16
MESH = pl.DeviceIdType.MESH

N_DEV = 8
D_MODEL = 1024
D_CONV = 512
D_POOL = 512
CONV_WIDTH = 31
POOL_WINDOWS = (2, 4, 8, 16)
POOL_GROUP_DIM = 128
D_IN = 1536
N_MEM = 256
HEADS = 4
HEAD_DIM = 256
D_FF = 2816
FFN_CONV_WIDTH = 3
EPS = 1e-6
ADAM_LR = 0.001
ADAM_B1 = 0.9
ADAM_B2 = 0.999
ADAM_EPS = 1e-08
ADAM_WD = 0.01
ADAM_STEP = 10

VMEM_LIMIT_V7X = 56 * 1024 * 1024
CONV_HALO = 32
POOL_HALO = 16
FFN_HALO = 8
FFN_CHUNK = 2816

W_ROWS = (("w_in", 192), ("w_out", 128), ("w_q", 128), ("w_kv", 256), ("w_o", 128), ("w_up", 704), ("w_down", 352))
AG_GROUPS = (("w_in", "w_out"), ("w_q", "w_kv", "w_o"), ("w_up", "w_down"))
W_OFF = {}
for _names in AG_GROUPS:
    _o = 0
    for _n in _names:
        W_OFF[_n] = (_o, dict(W_ROWS)[_n])
        _o += dict(W_ROWS)[_n]
RS_GROUPS = {"a": ("w_down",), "b": ("w_up",), "c": ("w_q", "w_kv", "w_o", "w_out"), "d": ("w_in",)}
BARRIER_IDS = {"gather_start": (0, 1), "gather_forward": (2, 3), "sibling": {"a": 4, "b": 5, "c": 6, "d": 11},
               "chips": {"a": 7, "b": 8, "c": 9, "d": 12}, "broadcast": 10}


def _dot(a, b):
    return jnp.dot(a, b, preferred_element_type=F32)


def _dot_nt(a, b):
    return lax.dot_general(a, b, (((1,), (1,)), ((), ())), preferred_element_type=F32)


def _dot_tn(a, b):
    return lax.dot_general(a, b, (((0,), (0,)), ((), ())), preferred_element_type=F32)


def _sigmoid(v):
    return 1.0 / (1.0 + jnp.exp(-v))


def _rms_fwd(v):
    r = lax.rsqrt(jnp.mean(v * v, axis=-1, keepdims=True) + EPS)
    return v * r, r


def _rms_bwd(dh, vh, r, g):
    gd = dh * g
    return r * (gd - vh * jnp.mean(gd * vh, axis=-1, keepdims=True))


def _sublane_shifts(v):
    rows = v.shape[0]
    return [v] + [pltpu.roll(v, rows - b, 0) for b in range(1, 8)]


def _colsum(v):
    return jnp.sum(v, axis=0, keepdims=True)


def _colsum_mxu(v):
    return _dot(jnp.ones((8, v.shape[0]), BF16), v.astype(BF16))[0:1, :]


def _full(shape):
    return pl.BlockSpec(shape, lambda *_: (0,) * len(shape))


def _params(sem=("arbitrary",), vmem=VMEM_LIMIT_V7X):
    return pltpu.CompilerParams(dimension_semantics=sem, vmem_limit_bytes=vmem)


def _load_weight(g_hbm, name, dst, sem):
    off, rows = W_OFF[name]
    return [pltpu.make_async_copy(g_hbm.at[d, pl.ds(off, rows), :], dst.at[pl.ds(d * rows, rows), :], sem)
            for d in range(N_DEV)]


def _start_weights(g_hbm, names, dsts, sems):
    @pl.when(pl.program_id(0) == 0)
    def _():
        copies = [_load_weight(g_hbm, name, dst, sems.at[k]) for k, (name, dst) in enumerate(zip(names, dsts))]
        for cp in sum(copies, []):
            cp.start()
        for cp in sum(copies, []):
            cp.wait()


def _position():
    x, y, c = lax.axis_index("x"), lax.axis_index("y"), lax.axis_index("c")
    chips = [(1 - x, y), (x, 1 - y), (1 - x, 1 - y)]
    return x, y, c, chips


def _dev(px, py, pc):
    return 4 * px + 2 * py + pc


def _all_gather(arrs, name):
    n = len(arrs)

    def body(*refs):
        ins, outs = refs[:n], refs[n:2 * n]
        send_sems, recv_sems, local_sems = refs[2 * n:2 * n + 3]
        bounce = refs[2 * n + 3:]
        x, y, c, chips = _position()
        me, sibling = (x, y, c), (x, y, 1 - c)

        def copy(a, k, block, to, src=None):
            rows = outs[a].at[_dev(*block)]
            return pltpu.make_async_remote_copy(
                src_ref=rows if src is None else src, dst_ref=rows,
                send_sem=send_sems.at[a, k], recv_sem=recv_sems.at[a, k], device_id=to, device_id_type=MESH)

        sends = []
        for a in range(n):
            first = [copy(a, 0, me, sibling, src=ins[a])]
            first += [copy(a, 1 + j, me, (*chip, c), src=ins[a]) for j, chip in enumerate(chips)]
            for cp in first:
                cp.start()
            sends += first
        started = []
        for a in range(n):
            load = pltpu.make_async_copy(ins[a], bounce[a], local_sems.at[a, 0])
            load.start()
            load.wait()
            mine = pltpu.make_async_copy(bounce[a], outs[a].at[_dev(*me)], local_sems.at[a, 1])
            mine.start()
            started.append(mine)
        for j, chip in enumerate(chips):
            for a in range(n):
                copy(a, 1 + j, (*chip, c), me).wait_recv()
                passed = copy(a, 4 + j, (*chip, c), sibling)
                passed.start()
                sends.append(passed)
        for a in range(n):
            copy(a, 0, sibling, me).wait_recv()
            for j, chip in enumerate(chips):
                copy(a, 4 + j, (*chip, 1 - c), me).wait_recv()
        for cp in sends:
            cp.wait_send()
        for mine in started:
            mine.wait()

    any_spec = pl.BlockSpec(memory_space=pl.ANY)
    return pl.pallas_call(
        body, name=name,
        out_shape=[jax.ShapeDtypeStruct((N_DEV,) + a.shape, a.dtype) for a in arrs],
        in_specs=[any_spec] * n, out_specs=[any_spec] * n,
        scratch_shapes=[pltpu.SemaphoreType.DMA((n, 7)), pltpu.SemaphoreType.DMA((n, 7)), pltpu.SemaphoreType.DMA((n, 2))]
        + [pltpu.VMEM(a.shape, a.dtype) for a in arrs],
    )(*arrs)


_HBM = pl.BlockSpec(memory_space=pltpu.HBM)
_SEM = pl.BlockSpec(memory_space=pltpu.SEMAPHORE)
_SIDE_EFFECT = pltpu.SideEffectType.DATAFLOW_SIDE_EFFECTING


def _handshake(peers):
    barrier = pltpu.get_barrier_semaphore()
    for peer in peers:
        pl.semaphore_signal(barrier, inc=1, device_id=peer, device_id_type=MESH)
    pl.semaphore_wait(barrier, len(peers))


def _gather_start(buf, after, name, collective_id):
    def body(buf_ref, after_ref, send_sems, recv_sems, buf_thru, token):
        del after_ref, buf_thru
        x, y, c, chips = _position()
        rows = buf_ref.at[_dev(x, y, c)]
        targets = [(x, y, 1 - c)] + [(*chip, c) for chip in chips]
        _handshake(targets)
        for k, to in enumerate(targets):
            pltpu.make_async_remote_copy(src_ref=rows, dst_ref=rows, send_sem=send_sems.at[k], recv_sem=recv_sems.at[k],
                                         device_id=to, device_id_type=MESH).start()
        token[...] = jnp.zeros_like(token)

    return pl.pallas_call(
        body, name=name,
        out_shape=(pltpu.SemaphoreType.DMA((4,)), pltpu.SemaphoreType.DMA((4,)), pltpu.HBM(buf.shape, buf.dtype),
                   jax.ShapeDtypeStruct((8, 128), F32)),
        in_specs=(_HBM, pl.BlockSpec(memory_space=pl.ANY)),
        out_specs=(_SEM, _SEM, _HBM, pl.BlockSpec(memory_space=pltpu.VMEM)),
        input_output_aliases={0: 2},
        compiler_params=pltpu.CompilerParams(has_side_effects=_SIDE_EFFECT, collective_id=collective_id),
    )(pltpu.with_memory_space_constraint(buf, pltpu.HBM), after)


def _gather_forward(send_sems, recv_sems, buf, after, name, collective_id):
    def body(buf_ref, send_sems, recv_sems, after_ref, fwd_send, fwd_recv, buf_thru):
        del after_ref, buf_thru
        x, y, c, chips = _position()
        sibling = (x, y, 1 - c)

        def copy(block, k, sends, recvs):
            rows = buf_ref.at[_dev(*block)]
            return pltpu.make_async_remote_copy(src_ref=rows, dst_ref=rows, send_sem=sends.at[k], recv_sem=recvs.at[k],
                                                device_id=sibling, device_id_type=MESH)

        _handshake([sibling])
        for k in range(4):
            copy((x, y, c), k, send_sems, recv_sems).wait_send()
        copy(sibling, 0, send_sems, recv_sems).wait_recv()
        for j, chip in enumerate(chips):
            copy((*chip, c), 1 + j, send_sems, recv_sems).wait_recv()
            copy((*chip, c), j, fwd_send, fwd_recv).start()

    return pl.pallas_call(
        body, name=name,
        out_shape=(pltpu.SemaphoreType.DMA((3,)), pltpu.SemaphoreType.DMA((3,)), pltpu.HBM(buf.shape, buf.dtype)),
        in_specs=(_HBM, _SEM, _SEM, pl.BlockSpec(memory_space=pl.ANY)), out_specs=(_SEM, _SEM, _HBM),
        input_output_aliases={0: 2},
        compiler_params=pltpu.CompilerParams(has_side_effects=_SIDE_EFFECT, collective_id=collective_id),
    )(buf, send_sems, recv_sems, after)


def _gather_finish(fwd_send, fwd_recv, buf, name):
    def body(buf_ref, fwd_send, fwd_recv, buf_thru):
        del buf_thru
        x, y, c, chips = _position()
        for j, chip in enumerate(chips):
            cp = pltpu.make_async_remote_copy(
                src_ref=buf_ref.at[_dev(*chip, c)], dst_ref=buf_ref.at[_dev(*chip, 1 - c)], send_sem=fwd_send.at[j],
                recv_sem=fwd_recv.at[j], device_id=(x, y, 1 - c), device_id_type=MESH)
            cp.wait_send()
            cp.wait_recv()

    return pl.pallas_call(
        body, name=name,
        out_shape=pltpu.HBM(buf.shape, buf.dtype),
        in_specs=(_HBM, _SEM, _SEM), out_specs=_HBM,
        input_output_aliases={0: 0},
        compiler_params=pltpu.CompilerParams(has_side_effects=_SIDE_EFFECT),
    )(buf, fwd_send, fwd_recv)


def _everyone_else(x, y, c, chips):
    return [(x, y, 1 - c)] + [(*chip, core) for chip in chips for core in (c, 1 - c)]


def _broadcast_start(buf, name, collective_id):
    def body(buf_ref, send_sems, recv_sems, buf_thru, token):
        del buf_thru
        x, y, c, chips = _position()
        rows = buf_ref.at[_dev(x, y, c)]
        _handshake(_everyone_else(x, y, c, chips))
        for k, to in enumerate(_everyone_else(x, y, c, chips)):
            pltpu.make_async_remote_copy(src_ref=rows, dst_ref=rows, send_sem=send_sems.at[k], recv_sem=recv_sems.at[k],
                                         device_id=to, device_id_type=MESH).start()
        token[...] = jnp.zeros_like(token)

    return pl.pallas_call(
        body, name=name,
        out_shape=(pltpu.SemaphoreType.DMA((7,)), pltpu.SemaphoreType.DMA((7,)), pltpu.HBM(buf.shape, buf.dtype),
                   jax.ShapeDtypeStruct((8, 128), F32)),
        in_specs=(_HBM,), out_specs=(_SEM, _SEM, _HBM, pl.BlockSpec(memory_space=pltpu.VMEM)),
        input_output_aliases={0: 2},
        compiler_params=pltpu.CompilerParams(has_side_effects=_SIDE_EFFECT, collective_id=collective_id),
    )(pltpu.with_memory_space_constraint(buf, pltpu.HBM))


def _broadcast_wait(send_sems, recv_sems, buf, after, name):
    def body(buf_ref, send_sems, recv_sems, after_ref, buf_thru):
        del after_ref, buf_thru
        x, y, c, chips = _position()
        for k, peer in enumerate(_everyone_else(x, y, c, chips)):
            cp = pltpu.make_async_remote_copy(
                src_ref=buf_ref.at[_dev(x, y, c)], dst_ref=buf_ref.at[_dev(*peer)], send_sem=send_sems.at[k],
                recv_sem=recv_sems.at[k], device_id=peer, device_id_type=MESH)
            cp.wait_send()
            cp.wait_recv()

    return pl.pallas_call(
        body, name=name,
        out_shape=pltpu.HBM(buf.shape, buf.dtype),
        in_specs=(_HBM, _SEM, _SEM, pl.BlockSpec(memory_space=pl.ANY)), out_specs=_HBM,
        input_output_aliases={0: 0},
        compiler_params=pltpu.CompilerParams(has_side_effects=_SIDE_EFFECT),
    )(buf, send_sems, recv_sems, after)


def _to_sibling(j, x, y, c, chips):
    return _dev(*([(x, y)] + chips)[j], 1 - c), (x, y, 1 - c)


def _to_chip(j, x, y, c, chips):
    return j, (*chips[j], c)


def _exchange_start(srcs, n_slots, route, name, collective_id):
    n = len(srcs)

    def body(*refs):
        s_refs, land_refs = refs[:n], refs[n:2 * n]
        send_sems, recv_sems = refs[2 * n:2 * n + 2]
        token = refs[-1]
        x, y, c, chips = _position()
        _handshake([(x, y, 1 - c)] if route is _to_sibling else [route(j, x, y, c, chips)[1] for j in range(n_slots)])
        for k in range(n):
            for j in range(n_slots):
                block, to = route(j, x, y, c, chips)
                pltpu.make_async_remote_copy(
                    src_ref=s_refs[k].at[block], dst_ref=land_refs[k].at[j], send_sem=send_sems.at[n_slots * k + j],
                    recv_sem=recv_sems.at[n_slots * k + j], device_id=to, device_id_type=MESH).start()
        token[...] = jnp.zeros_like(token)

    lands = [jax.ShapeDtypeStruct((n_slots,) + s.shape[1:], s.dtype) for s in srcs]
    outs = pl.pallas_call(
        body, name=name,
        out_shape=(pltpu.SemaphoreType.DMA((n_slots * n,)), pltpu.SemaphoreType.DMA((n_slots * n,)),
                   *[pltpu.HBM(s.shape, s.dtype) for s in srcs], *[pltpu.HBM(l.shape, l.dtype) for l in lands],
                   jax.ShapeDtypeStruct((8, 128), F32)),
        in_specs=[_HBM] * (2 * n), out_specs=(_SEM, _SEM, *[_HBM] * (2 * n), pl.BlockSpec(memory_space=pltpu.VMEM)),
        input_output_aliases={k: 2 + k for k in range(2 * n)},
        compiler_params=pltpu.CompilerParams(has_side_effects=_SIDE_EFFECT, collective_id=collective_id),
    )(*[pltpu.with_memory_space_constraint(s, pltpu.HBM) for s in srcs],
      *[pltpu.with_memory_space_constraint(lax.empty(l.shape, l.dtype), pltpu.HBM) for l in lands])
    return outs[0], outs[1], outs[2:2 + n], outs[2 + n:2 + 2 * n], outs[-1]


def _exchange_wait(send_sems, recv_sems, s_thru, land_thru, after, n_slots, route, name):
    n = len(s_thru)

    def body(*refs):
        s_refs, land_refs = refs[:n], refs[n:2 * n]
        send_sems, recv_sems = refs[2 * n:2 * n + 2]
        x, y, c, chips = _position()
        for k in range(n):
            for j in range(n_slots):
                block, to = route(j, x, y, c, chips)
                cp = pltpu.make_async_remote_copy(
                    src_ref=s_refs[k].at[block], dst_ref=land_refs[k].at[j], send_sem=send_sems.at[n_slots * k + j],
                    recv_sem=recv_sems.at[n_slots * k + j], device_id=to, device_id_type=MESH)
                cp.wait_send()
                cp.wait_recv()

    outs = pl.pallas_call(
        body, name=name,
        out_shape=(*[pltpu.HBM(s.shape, s.dtype) for s in s_thru], *[pltpu.HBM(l.shape, l.dtype) for l in land_thru]),
        in_specs=[_HBM] * (2 * n) + [_SEM, _SEM, pl.BlockSpec(memory_space=pl.ANY)], out_specs=[_HBM] * (2 * n),
        input_output_aliases={k: k for k in range(2 * n)},
        compiler_params=pltpu.CompilerParams(has_side_effects=_SIDE_EFFECT),
    )(*s_thru, *land_thru, send_sems, recv_sems, after)
    return outs[:n], outs[n:]


def _owner_table():
    x, y, c = lax.axis_index("x"), lax.axis_index("y"), lax.axis_index("c")
    chips = [(x, y), (1 - x, y), (x, 1 - y), (1 - x, 1 - y)]
    return jnp.stack([_dev(px, py, c) for px, py in chips]).astype(jnp.int32)


def _chip_partial_sums(table, parts, from_sibling, name):
    n = len(parts)

    def body(tab_ref, *refs):
        del tab_ref
        for g_ref, l_ref, out_ref in zip(refs[:n], refs[n:2 * n], refs[2 * n:]):
            out_ref[...] = (g_ref[...].astype(F32) + l_ref[...].astype(F32)).astype(out_ref.dtype)

    block = lambda p: (None,) + p.shape[1:]
    grid_spec = pltpu.PrefetchScalarGridSpec(
        num_scalar_prefetch=1, grid=(3,),
        in_specs=[pl.BlockSpec(block(p), lambda j, tab: (tab[j + 1], 0, 0)) for p in parts]
        + [pl.BlockSpec(block(p), lambda j, tab: (j + 1, 0, 0)) for p in parts],
        out_specs=[pl.BlockSpec(block(p), lambda j, tab: (j, 0, 0)) for p in parts])
    return pl.pallas_call(
        body, name=name, grid_spec=grid_spec,
        out_shape=[jax.ShapeDtypeStruct((3,) + p.shape[1:], BF16) for p in parts],
        compiler_params=_params(("arbitrary",)),
    )(table, *parts, *from_sibling)


def _final_update(table, parts, from_sibling, from_chips, states, name):
    n = len(parts)
    flipped = [states[k][0].shape != parts[k].shape[1:] for k in range(n)]

    def body(tab_ref, *refs):
        del tab_ref
        ins, outs = refs[:6 * n], refs[6 * n:]
        for k in range(n):
            acc = ins[k][...].astype(F32) + ins[n + k][...].astype(F32)
            for j in range(3):
                acc = acc + ins[2 * n + k][j].astype(F32)
            if flipped[k]:
                acc = acc.T
            w_ref, m_ref, v_ref = ins[3 * n + 3 * k:3 * n + 3 * k + 3]
            outs[4 * k][...] = acc
            for out_ref, val in zip(outs[4 * k + 1:4 * k + 4], _adamw_update(w_ref[...], acc, m_ref[...], v_ref[...])):
                out_ref[...] = val

    def grad_block(k, lead, at):
        r, c = parts[k].shape[1:]
        if flipped[k]:
            return pl.BlockSpec(lead + (r, c // 2), lambda t, tab: (*at(tab), 0, t))
        return pl.BlockSpec(lead + (r // 2, c), lambda t, tab: (*at(tab), t, 0))

    def state_block(k):
        a, b = states[k][0].shape
        return pl.BlockSpec((a // 2, b), lambda t, tab: (t, 0))

    grid_spec = pltpu.PrefetchScalarGridSpec(
        num_scalar_prefetch=1, grid=(2,),
        in_specs=[grad_block(k, (None,), lambda tab: (tab[0],)) for k in range(n)]
        + [grad_block(k, (None,), lambda tab: (0,)) for k in range(n)]
        + [grad_block(k, (3,), lambda tab: (0,)) for k in range(n)]
        + [state_block(k) for k in range(n) for _ in range(3)],
        out_specs=[state_block(k) for k in range(n) for _ in range(4)])
    outs = pl.pallas_call(
        body, name=name, grid_spec=grid_spec,
        out_shape=[jax.ShapeDtypeStruct(states[k][0].shape, F32) for k in range(n) for _ in range(4)],
        compiler_params=_params(("arbitrary",)),
    )(table, *parts, *from_sibling, *from_chips, *[t for k in range(n) for t in states[k]])
    return [outs[4 * k:4 * k + 4] for k in range(n)]


def _sum_blocks(g8):
    _, rows, cols = g8.shape

    def body(g_ref, out_ref):
        acc = g_ref[0]
        for d in range(1, N_DEV):
            acc = acc + g_ref[d]
        out_ref[...] = acc

    return pl.pallas_call(
        body, name="small_grad_sum", grid=(1,),
        in_specs=[_full((N_DEV, rows, cols))], out_specs=_full((rows, cols)),
        out_shape=jax.ShapeDtypeStruct((rows, cols), F32),
        compiler_params=_params(("arbitrary",)),
    )(g8)


def _fwd_mix(x2d, gw, g_mix, conv_w, conv_b, ln_g, ln_b, pool_w, pool_scale, after, seq, tm):
    tokens = x2d.shape[0]
    n_tiles = tokens // tm
    tps = seq // tm

    def body(x_ref, gmix_ref, gw_hbm, cw_ref, cb_ref, lng_ref, lnb_ref, pw_ref, ps_ref, after_ref,
             x1_ref, u_ref, c_ref, pooled_ref, ymix_ref, h1_ref,
             win_v, wout_v, hc_carry, up_carry, sem):
        del after_ref
        i = pl.program_id(0)

        _start_weights(gw_hbm, ("w_in", "w_out"), (win_v, wout_v), sem)

        @pl.when(i % tps == 0)
        def _():
            hc_carry[...] = jnp.zeros_like(hc_carry)
            up_carry[...] = jnp.zeros_like(up_carry)

        x = x_ref[...]
        xh, _ = _rms_fwd(x)
        h1 = (xh * gmix_ref[...]).astype(BF16)
        h1_ref[...] = h1
        u = _dot_nt(h1, win_v[...])
        u_ref[...] = u
        val, gate, up = u[:, :D_CONV], u[:, D_CONV:2 * D_CONV], u[:, 2 * D_CONV:]

        extp = jnp.concatenate([up_carry[...], up], axis=0)
        up_carry[...] = up[tm - POOL_HALO:, :]
        pos = lax.broadcasted_iota(jnp.int32, (tm, 1), 0) + (i % tps) * tm
        run = extp
        mixed = []
        for g, w in enumerate(POOL_WINDOWS):
            lo = g * POOL_GROUP_DIM
            run = run[:, POOL_GROUP_DIM if g else 0:]
            run = run + pltpu.roll(run, w // 2, 0)
            cnt = jnp.minimum(pos + 1, w).astype(F32)
            pooled = run[POOL_HALO:, :POOL_GROUP_DIM] / cnt - up[:, lo:lo + POOL_GROUP_DIM]
            pooled = pooled.astype(BF16)
            pooled_ref[:, lo:lo + POOL_GROUP_DIM] = pooled
            mixed.append(_dot(pooled, pw_ref[g].astype(BF16)))
        y_pool = jnp.concatenate(mixed, axis=-1) * ps_ref[...]
        y_pool = y_pool.astype(BF16)
        ymix_ref[:, D_CONV:] = y_pool
        out = _dot(y_pool, wout_v[D_CONV:, :])

        hc = val * _sigmoid(gate)
        ext = jnp.concatenate([hc_carry[...], hc], axis=0)
        hc_carry[...] = hc[tm - CONV_HALO:, :]
        conv = jnp.broadcast_to(cb_ref[...], (tm, D_CONV))
        ahead_by = _sublane_shifts(ext)
        for k in range(CONV_WIDTH):
            whole, part = divmod(CONV_HALO - (CONV_WIDTH - 1) + k, 8)
            conv = conv + cw_ref[k:k + 1, :] * ahead_by[part][8 * whole:8 * whole + tm, :]
        c_ref[...] = conv
        mu = jnp.mean(conv, axis=-1, keepdims=True)
        cen = conv - mu
        ln = cen * lax.rsqrt(jnp.mean(cen * cen, axis=-1, keepdims=True) + EPS) * lng_ref[...] + lnb_ref[...]
        y_conv = ln * _sigmoid(ln)
        y_conv = y_conv.astype(BF16)
        ymix_ref[:, :D_CONV] = y_conv
        x1_ref[...] = x + (out + _dot(y_conv, wout_v[:D_CONV, :]))

    row = lambda w: pl.BlockSpec((tm, w), lambda i: (i, 0))
    return pl.pallas_call(
        body, name="fwd_mix", grid=(n_tiles,),
        in_specs=[row(D_MODEL), _full((1, D_MODEL)), pl.BlockSpec(memory_space=pl.ANY),
                  _full((CONV_WIDTH, D_CONV)), _full((1, D_CONV)), _full((1, D_CONV)), _full((1, D_CONV)),
                  _full((4, POOL_GROUP_DIM, POOL_GROUP_DIM)), _full((1, D_POOL)), _full(after.shape)],
        out_specs=[row(D_MODEL), row(D_IN), row(D_CONV), row(D_POOL), row(D_MODEL), row(D_MODEL)],
        out_shape=[jax.ShapeDtypeStruct((tokens, D_MODEL), F32), jax.ShapeDtypeStruct((tokens, D_IN), F32),
                   jax.ShapeDtypeStruct((tokens, D_CONV), F32), jax.ShapeDtypeStruct((tokens, D_POOL), BF16),
                   jax.ShapeDtypeStruct((tokens, D_MODEL), BF16), jax.ShapeDtypeStruct((tokens, D_MODEL), BF16)],
        scratch_shapes=[pltpu.VMEM((D_IN, D_MODEL), BF16), pltpu.VMEM((D_MODEL, D_MODEL), BF16),
                        pltpu.VMEM((CONV_HALO, D_CONV), F32), pltpu.VMEM((POOL_HALO, D_POOL), F32),
                        pltpu.SemaphoreType.DMA((2,))],
        compiler_params=_params(),
    )(x2d, g_mix, gw, conv_w, conv_b, ln_g, ln_b, pool_w, pool_scale, after)


def _fwd_kv(mem2d, gw, g_mem):
    rows = mem2d.shape[0]
    n_b = rows // N_MEM

    def body(mem_ref, g_ref, gw_hbm, mn_ref, kv_ref, wkv_v, sem):
        @pl.when(pl.program_id(0) == 0)
        def _():
            copies = _load_weight(gw_hbm, "w_kv", wkv_v, sem)
            for cp in copies:
                cp.start()
            for cp in copies:
                cp.wait()

        mh, _ = _rms_fwd(mem_ref[...])
        mn = (mh * g_ref[...]).astype(BF16)
        mn_ref[...] = mn
        kv_ref[...] = _dot_nt(mn, wkv_v[...]).astype(BF16)

    return pl.pallas_call(
        body, name="fwd_kv", grid=(n_b,),
        in_specs=[pl.BlockSpec((N_MEM, D_MODEL), lambda b: (b, 0)), _full((1, D_MODEL)), pl.BlockSpec(memory_space=pl.ANY)],
        out_specs=[pl.BlockSpec((N_MEM, D_MODEL), lambda b: (b, 0)), pl.BlockSpec((N_MEM, 2 * D_MODEL), lambda b: (b, 0))],
        out_shape=[jax.ShapeDtypeStruct((rows, D_MODEL), BF16), jax.ShapeDtypeStruct((rows, 2 * D_MODEL), BF16)],
        scratch_shapes=[pltpu.VMEM((2 * D_MODEL, D_MODEL), BF16), pltpu.SemaphoreType.DMA],
        compiler_params=_params(),
    )(mem2d, g_mem, gw)


def _softmax_rows(s):
    e = jnp.exp(s - jnp.max(s, axis=-1, keepdims=True))
    return e / jnp.sum(e, axis=-1, keepdims=True)


def _fwd_attn(x1, kv, gw, g_x, seq, tm):
    tokens = x1.shape[0]
    n_tiles = tokens // tm
    tps = seq // tm

    def body(x1_ref, kv_ref, g_ref, gw_hbm, x2_ref, h2_ref, q_ref, o_ref, wq_v, wo_v, sem):
        _start_weights(gw_hbm, ("w_q", "w_o"), (wq_v, wo_v), sem)
        x1v = x1_ref[...]
        xh, _ = _rms_fwd(x1v)
        h2 = (xh * g_ref[...]).astype(BF16)
        h2_ref[...] = h2
        q = (_dot(h2, wq_v[...]) * (HEAD_DIM ** -0.5)).astype(BF16)
        q_ref[...] = q
        heads = [slice(h * HEAD_DIM, (h + 1) * HEAD_DIM) for h in range(HEADS)]
        scores = [_dot_nt(q[:, hd], kv_ref[:, hd]) for hd in heads]
        probs = [_softmax_rows(s).astype(BF16) for s in scores]
        outs = [_dot(p, kv_ref[:, pl.ds(D_MODEL + h * HEAD_DIM, HEAD_DIM)]) for h, p in enumerate(probs)]
        o = jnp.concatenate(outs, axis=-1).astype(BF16)
        o_ref[...] = o
        x2_ref[...] = x1v + _dot(o, wo_v[...])

    row = lambda w: pl.BlockSpec((tm, w), lambda i: (i, 0))
    return pl.pallas_call(
        body, name="fwd_attn", grid=(n_tiles,),
        in_specs=[row(D_MODEL), pl.BlockSpec((N_MEM, 2 * D_MODEL), lambda i: (i // tps, 0)), _full((1, D_MODEL)),
                  pl.BlockSpec(memory_space=pl.ANY)],
        out_specs=[row(D_MODEL)] * 4,
        out_shape=[jax.ShapeDtypeStruct((tokens, D_MODEL), F32)] + [jax.ShapeDtypeStruct((tokens, D_MODEL), BF16)] * 3,
        scratch_shapes=[pltpu.VMEM((D_MODEL, D_MODEL), BF16), pltpu.VMEM((D_MODEL, D_MODEL), BF16), pltpu.SemaphoreType.DMA((2,))],
        compiler_params=_params(),
    )(x1, kv, g_x, gw)


def _ffn_conv(uu, halo, w_ref, b_ref, cols):
    ext = jnp.concatenate([halo, uu], axis=0)
    p1 = pltpu.roll(ext, 1, 0)[FFN_HALO:, :]
    p2 = pltpu.roll(ext, 2, 0)[FFN_HALO:, :]
    return b_ref[:, cols] + w_ref[2:3, cols] * uu + w_ref[1:2, cols] * p1 + w_ref[0:1, cols] * p2


def _fwd_ffn(x2, target, gw, g_ffn, ffn_w, ffn_b, g_final, seq, tm):
    tokens = x2.shape[0]
    n_tiles = tokens // tm
    tps = seq // tm
    n_chunks = D_FF // FFN_CHUNK

    def body(x2_ref, tgt_ref, gffn_ref, gw_hbm, fw_ref, fb_ref, gfin_ref,
             uu_ref, cc_ref, a_ref, h3_ref, dx3_ref, dx3b_ref, loss_ref, dgfin_ref,
             wup_v, wdown_v, carry, sem):
        i = pl.program_id(0)

        _start_weights(gw_hbm, ("w_up", "w_down"), (wup_v, wdown_v), sem)

        @pl.when(i == 0)
        def _():
            loss_ref[...] = jnp.zeros_like(loss_ref)
            dgfin_ref[...] = jnp.zeros_like(dgfin_ref)

        @pl.when(i % tps == 0)
        def _():
            carry[...] = jnp.zeros_like(carry)

        x2v = x2_ref[...]
        xh, _ = _rms_fwd(x2v)
        h3 = (xh * gffn_ref[...]).astype(BF16)
        h3_ref[...] = h3
        acc = jnp.zeros((tm, D_MODEL), F32)
        for jc in range(n_chunks):
            halves = []
            for half in range(2):
                cols = pl.ds(half * D_FF + jc * FFN_CHUNK, FFN_CHUNK)
                uu = _dot_nt(h3, wup_v[cols, :])
                uu_ref[:, cols] = uu.astype(BF16)
                cc = _ffn_conv(uu, carry[:, cols], fw_ref, fb_ref, cols)
                cc_ref[:, cols] = cc.astype(BF16)
                halves.append(cc)
                carry[:, cols] = uu[tm - FFN_HALO:, :]
            gate, val = halves
            a = (gate * _sigmoid(gate) * val).astype(BF16)
            a_ref[:, pl.ds(jc * FFN_CHUNK, FFN_CHUNK)] = a
            acc = acc + _dot(a, wdown_v[pl.ds(jc * FFN_CHUNK, FFN_CHUNK), :])
        x3 = x2v + acc

        xh3, r3 = _rms_fwd(x3)
        gfin = gfin_ref[...]
        err = xh3 * gfin - tgt_ref[...]
        loss_ref[...] += jnp.full(loss_ref.shape, jnp.sum(err * err) * (0.5 / D_MODEL), F32)
        dy = err * (1.0 / D_MODEL)
        dgfin_ref[...] += _colsum(dy * xh3)
        dx3 = _rms_bwd(dy, xh3, r3, gfin)
        dx3_ref[...] = dx3
        dx3b_ref[...] = dx3.astype(BF16)

    row = lambda w: pl.BlockSpec((tm, w), lambda i: (i, 0))
    return pl.pallas_call(
        body, name="fwd_ffn", grid=(n_tiles,),
        in_specs=[row(D_MODEL), row(D_MODEL), _full((1, D_MODEL)), pl.BlockSpec(memory_space=pl.ANY),
                  _full((FFN_CONV_WIDTH, 2 * D_FF)), _full((1, 2 * D_FF)), _full((1, D_MODEL))],
        out_specs=[row(2 * D_FF), row(2 * D_FF), row(D_FF), row(D_MODEL), row(D_MODEL), row(D_MODEL), _full((8, 128)),
                   _full((1, D_MODEL))],
        out_shape=[jax.ShapeDtypeStruct((tokens, 2 * D_FF), BF16), jax.ShapeDtypeStruct((tokens, 2 * D_FF), BF16),
                   jax.ShapeDtypeStruct((tokens, D_FF), BF16),
                   jax.ShapeDtypeStruct((tokens, D_MODEL), BF16), jax.ShapeDtypeStruct((tokens, D_MODEL), F32),
                   jax.ShapeDtypeStruct((tokens, D_MODEL), BF16),
                   jax.ShapeDtypeStruct((8, 128), F32), jax.ShapeDtypeStruct((1, D_MODEL), F32)],
        scratch_shapes=[pltpu.VMEM((2 * D_FF, D_MODEL), BF16), pltpu.VMEM((D_FF, D_MODEL), BF16),
                        pltpu.VMEM((FFN_HALO, 2 * D_FF), F32), pltpu.SemaphoreType.DMA((2,))],
        compiler_params=_params(),
    )(x2, target, g_ffn, gw, ffn_w, ffn_b, g_final)


def _bwd_ffn(dx3, x2, uu_all, cc_all, gw, g_ffn, ffn_w, seq, tm):
    tokens = x2.shape[0]
    n_tiles = tokens // tm
    tps = seq // tm
    n_chunks = D_FF // FFN_CHUNK

    def body(dx3_ref, x2_ref, uu_ref, cc_ref, gffn_ref, gw_hbm, fw_ref,
             dx2_ref, dx2b_ref, duu_ref, dfb_ref, dfw_ref, dg_ref,
             wup_v, wdown_v, carry, sem):
        i = pl.program_id(0)
        t = n_tiles - 1 - i

        _start_weights(gw_hbm, ("w_down", "w_up"), (wdown_v, wup_v), sem)

        @pl.when(i == 0)
        def _():
            dfb_ref[...] = jnp.zeros_like(dfb_ref)
            dfw_ref[...] = jnp.zeros_like(dfw_ref)
            dg_ref[...] = jnp.zeros_like(dg_ref)

        @pl.when(t % tps == tps - 1)
        def _():
            carry[...] = jnp.zeros_like(carry)

        dx3v = dx3_ref[...]
        dx3b = dx3v.astype(BF16)
        dh3 = jnp.zeros((tm, D_MODEL), F32)
        for jc in range(n_chunks):
            da = _dot_nt(dx3b, wdown_v[pl.ds(jc * FFN_CHUNK, FFN_CHUNK), :])
            colss = [pl.ds(half * D_FF + jc * FFN_CHUNK, FFN_CHUNK) for half in range(2)]
            gate, val = [cc_ref[:, cols].astype(F32) for cols in colss]
            sg = _sigmoid(gate)
            dgate = da * val * (sg * (1.0 + gate * (1.0 - sg)))
            dval = da * (gate * sg)
            for dcc, cols in zip((dgate, dval), colss):
                uu = uu_ref[:, cols].astype(F32)
                dfb_ref[:, cols] += _colsum(dcc)
                ext = jnp.concatenate([dcc, carry[:, cols]], axis=0)
                carry[:, cols] = dcc[:FFN_HALO, :]
                n1 = pltpu.roll(ext, tm + FFN_HALO - 1, 0)[:tm, :]
                n2 = pltpu.roll(ext, tm + FFN_HALO - 2, 0)[:tm, :]
                duu = fw_ref[2:3, cols] * dcc + fw_ref[1:2, cols] * n1 + fw_ref[0:1, cols] * n2
                dfw_ref[2:3, cols] += _colsum(uu * dcc)
                dfw_ref[1:2, cols] += _colsum(uu * n1)
                dfw_ref[0:1, cols] += _colsum(uu * n2)
                duub = duu.astype(BF16)
                duu_ref[:, cols] = duub
                dh3 = dh3 + _dot(duub, wup_v[cols, :])
        xh, r = _rms_fwd(x2_ref[...])
        dg_ref[...] += _colsum(dh3 * xh)
        dx2 = dx3v + _rms_bwd(dh3, xh, r, gffn_ref[...])
        dx2_ref[...] = dx2
        dx2b_ref[...] = dx2.astype(BF16)

    rev = lambda w: pl.BlockSpec((tm, w), lambda i: (n_tiles - 1 - i, 0))
    return pl.pallas_call(
        body, name="bwd_ffn", grid=(n_tiles,),
        in_specs=[rev(D_MODEL), rev(D_MODEL), rev(2 * D_FF), rev(2 * D_FF), _full((1, D_MODEL)),
                  pl.BlockSpec(memory_space=pl.ANY), _full((FFN_CONV_WIDTH, 2 * D_FF))],
        out_specs=[rev(D_MODEL), rev(D_MODEL), rev(2 * D_FF), _full((1, 2 * D_FF)), _full((FFN_CONV_WIDTH, 2 * D_FF)),
                   _full((1, D_MODEL))],
        out_shape=[jax.ShapeDtypeStruct((tokens, D_MODEL), F32), jax.ShapeDtypeStruct((tokens, D_MODEL), BF16),
                   jax.ShapeDtypeStruct((tokens, 2 * D_FF), BF16),
                   jax.ShapeDtypeStruct((1, 2 * D_FF), F32), jax.ShapeDtypeStruct((FFN_CONV_WIDTH, 2 * D_FF), F32),
                   jax.ShapeDtypeStruct((1, D_MODEL), F32)],
        scratch_shapes=[pltpu.VMEM((2 * D_FF, D_MODEL), BF16), pltpu.VMEM((D_FF, D_MODEL), BF16),
                        pltpu.VMEM((FFN_HALO, 2 * D_FF), F32), pltpu.SemaphoreType.DMA((2,))],
        compiler_params=_params(),
    )(dx3, x2, uu_all, cc_all, g_ffn, gw, ffn_w)


def _bwd_attn(dx2, x1, q, kv, gw, g_x, after, seq, tm):
    tokens = x1.shape[0]
    n_tiles = tokens // tm
    tps = seq // tm
    n_b = tokens // seq

    def body(dx2_ref, x1_ref, q_ref, kv_ref, g_ref, gw_hbm, after_ref, dx1_ref, dx1b_ref, dq_ref, dkv_ref, dg_ref,
             wq_v, wo_v, sem):
        del after_ref
        i = pl.program_id(0)

        _start_weights(gw_hbm, ("w_o", "w_q"), (wo_v, wq_v), sem)

        @pl.when(i == 0)
        def _():
            dg_ref[...] = jnp.zeros_like(dg_ref)

        @pl.when(i % tps == 0)
        def _():
            dkv_ref[...] = jnp.zeros_like(dkv_ref)

        dx2v = dx2_ref[...]
        do = _dot_nt(dx2v.astype(BF16), wo_v[...]).astype(BF16)
        q = q_ref[...]
        heads = [slice(h * HEAD_DIM, (h + 1) * HEAD_DIM) for h in range(HEADS)]
        kcols = [pl.ds(h * HEAD_DIM, HEAD_DIM) for h in range(HEADS)]
        vcols = [pl.ds(D_MODEL + h * HEAD_DIM, HEAD_DIM) for h in range(HEADS)]
        scores = [_dot_nt(q[:, hd], kv_ref[:, kc]) for hd, kc in zip(heads, kcols)]
        dps = [_dot_nt(do[:, hd], kv_ref[:, vc]) for hd, vc in zip(heads, vcols)]
        probs = [_softmax_rows(s) for s in scores]
        dss = [(p * (dp - jnp.sum(dp * p, axis=-1, keepdims=True))).astype(BF16) for p, dp in zip(probs, dps)]
        for p, hd, vc in zip(probs, heads, vcols):
            dkv_ref[:, vc] += _dot_tn(p.astype(BF16), do[:, hd])
        dqs = [_dot(ds, kv_ref[:, kc]) * (HEAD_DIM ** -0.5) for ds, kc in zip(dss, kcols)]
        for ds, hd, kc in zip(dss, heads, kcols):
            dkv_ref[:, kc] += _dot_tn(ds, q[:, hd])
        dq = jnp.concatenate(dqs, axis=-1).astype(BF16)
        dq_ref[...] = dq
        dh2 = _dot_nt(dq, wq_v[...])
        xh, r = _rms_fwd(x1_ref[...])
        dg_ref[...] += _colsum(dh2 * xh)
        dx1 = dx2v + _rms_bwd(dh2, xh, r, g_ref[...])
        dx1_ref[...] = dx1
        dx1b_ref[...] = dx1.astype(BF16)

    row = lambda w: pl.BlockSpec((tm, w), lambda i: (i, 0))
    per_b = pl.BlockSpec((N_MEM, 2 * D_MODEL), lambda i: (i // tps, 0))
    return pl.pallas_call(
        body, name="bwd_attn", grid=(n_tiles,),
        in_specs=[row(D_MODEL), row(D_MODEL), row(D_MODEL), per_b, _full((1, D_MODEL)), pl.BlockSpec(memory_space=pl.ANY),
                  _full(after.shape)],
        out_specs=[row(D_MODEL), row(D_MODEL), row(D_MODEL), per_b, _full((1, D_MODEL))],
        out_shape=[jax.ShapeDtypeStruct((tokens, D_MODEL), F32), jax.ShapeDtypeStruct((tokens, D_MODEL), BF16),
                   jax.ShapeDtypeStruct((tokens, D_MODEL), BF16),
                   jax.ShapeDtypeStruct((n_b * N_MEM, 2 * D_MODEL), F32), jax.ShapeDtypeStruct((1, D_MODEL), F32)],
        scratch_shapes=[pltpu.VMEM((D_MODEL, D_MODEL), BF16), pltpu.VMEM((D_MODEL, D_MODEL), BF16), pltpu.SemaphoreType.DMA((2,))],
        compiler_params=_params(),
    )(dx2, x1, q, kv, g_x, gw, after)


def _bwd_kv(dkv, mem2d, gw):
    rows = mem2d.shape[0]
    n_b = rows // N_MEM

    def body(dkv_ref, mem_ref, gw_hbm, dkvb_ref, dg_ref, wkv_v, sem):
        @pl.when(pl.program_id(0) == 0)
        def _():
            copies = _load_weight(gw_hbm, "w_kv", wkv_v, sem)
            for cp in copies:
                cp.start()
            for cp in copies:
                cp.wait()
            dg_ref[...] = jnp.zeros_like(dg_ref)

        dkvb = dkv_ref[...].astype(BF16)
        dkvb_ref[...] = dkvb
        dmn = _dot(dkvb, wkv_v[...])
        mh, _ = _rms_fwd(mem_ref[...])
        dg_ref[...] += _colsum(dmn * mh)

    return pl.pallas_call(
        body, name="bwd_kv", grid=(n_b,),
        in_specs=[pl.BlockSpec((N_MEM, 2 * D_MODEL), lambda b: (b, 0)), pl.BlockSpec((N_MEM, D_MODEL), lambda b: (b, 0)),
                  pl.BlockSpec(memory_space=pl.ANY)],
        out_specs=[pl.BlockSpec((N_MEM, 2 * D_MODEL), lambda b: (b, 0)), _full((1, D_MODEL))],
        out_shape=[jax.ShapeDtypeStruct((rows, 2 * D_MODEL), BF16), jax.ShapeDtypeStruct((1, D_MODEL), F32)],
        scratch_shapes=[pltpu.VMEM((2 * D_MODEL, D_MODEL), BF16), pltpu.SemaphoreType.DMA],
        compiler_params=_params(),
    )(dkv, mem2d, gw)


def _bwd_mix(dx1, x2d, u_all, c_all, pooled_all, gw, g_mix, conv_w, ln_g, ln_b, pool_w, pool_scale, after, seq, tm):
    tokens = x2d.shape[0]
    n_tiles = tokens // tm
    tps = seq // tm

    def body(dx1_ref, x_ref, u_ref, c_ref, pooled_ref, gmix_ref, gw_hbm, cw_ref, lng_ref, lnb_ref, pw_ref, ps_ref,
             after_ref, dx_ref, du_ref, dgmix_ref, dcw_ref, dcb_ref, dlng_ref, dlnb_ref, dpw_ref, dps_ref,
             win_v, wout_v, dc_carry, e_carry, sem):
        del after_ref
        i = pl.program_id(0)
        t = n_tiles - 1 - i

        _start_weights(gw_hbm, ("w_out", "w_in"), (wout_v, win_v), sem)

        @pl.when(i == 0)
        def _():
            for ref in (dgmix_ref, dcw_ref, dcb_ref, dlng_ref, dlnb_ref, dpw_ref, dps_ref):
                ref[...] = jnp.zeros_like(ref)

        @pl.when(t % tps == tps - 1)
        def _():
            dc_carry[...] = jnp.zeros_like(dc_carry)
            e_carry[...] = jnp.zeros_like(e_carry)

        dx1v = dx1_ref[...]
        dymix = _dot_nt(dx1v.astype(BF16), wout_v[...])
        dyc, dyp = dymix[:, :D_CONV], dymix[:, D_CONV:]
        u = u_ref[...]
        val, gate = u[:, :D_CONV], u[:, D_CONV:2 * D_CONV]

        conv = c_ref[...]
        mu = jnp.mean(conv, axis=-1, keepdims=True)
        cen = conv - mu
        rs = lax.rsqrt(jnp.mean(cen * cen, axis=-1, keepdims=True) + EPS)
        chat = cen * rs
        ln = chat * lng_ref[...] + lnb_ref[...]
        sl = _sigmoid(ln)
        dln = dyc * (sl * (1.0 + ln * (1.0 - sl)))
        dlng_ref[...] += _colsum(dln * chat)
        dlnb_ref[...] += _colsum(dln)
        dchat = dln * lng_ref[...]
        dc = rs * (dchat - jnp.mean(dchat, axis=-1, keepdims=True)
                   - chat * jnp.mean(dchat * chat, axis=-1, keepdims=True))
        dcb_ref[...] += _colsum(dc)
        sg = _sigmoid(gate)
        hc = val * sg
        ext = jnp.concatenate([dc, dc_carry[...]], axis=0)
        dc_carry[...] = dc[:CONV_HALO, :]
        dhc = jnp.zeros((tm, D_CONV), F32)
        ahead_by = _sublane_shifts(ext)
        for k in range(CONV_WIDTH):
            whole, part = divmod(CONV_WIDTH - 1 - k, 8)
            tap = ahead_by[part][8 * whole:8 * whole + tm, :]
            dhc = dhc + cw_ref[k:k + 1, :] * tap
            dcw_ref[k:k + 1, :] += _colsum_mxu(hc * tap)
        du_ref[:, :D_CONV] = (dhc * sg).astype(BF16)
        du_ref[:, D_CONV:2 * D_CONV] = (dhc * val * (sg * (1.0 - sg))).astype(BF16)

        pos = lax.broadcasted_iota(jnp.int32, (tm, 1), 0) + (t % tps) * tm
        es, dpooled = [], []
        for g, w in enumerate(POOL_WINDOWS):
            cols = pl.ds(g * POOL_GROUP_DIM, POOL_GROUP_DIM)
            lo = g * POOL_GROUP_DIM
            pooled = pooled_ref[:, cols]
            pw = pw_ref[g].astype(BF16)
            dyg = dyp[:, lo:lo + POOL_GROUP_DIM]
            dps_ref[:, cols] += _colsum(dyg * _dot(pooled, pw))
            dmixed = (dyg * ps_ref[:, cols]).astype(BF16)
            dpw_ref[g] += _dot_tn(pooled, dmixed)
            dpo = _dot_nt(dmixed, pw)
            dpooled.append(dpo)
            es.append(dpo / jnp.minimum(pos + 1, w).astype(F32))
        e = jnp.concatenate(es, axis=-1)
        run = jnp.concatenate([e, e_carry[...]], axis=0)
        e_carry[...] = e[:POOL_HALO, :]
        rows = tm + POOL_HALO
        for g, w in enumerate(POOL_WINDOWS):
            lo = g * POOL_GROUP_DIM
            run = run[:, POOL_GROUP_DIM if g else 0:]
            run = run + pltpu.roll(run, rows - w // 2, 0)
            du_ref[:, 2 * D_CONV + lo:2 * D_CONV + lo + POOL_GROUP_DIM] = (
                run[:tm, :POOL_GROUP_DIM] - dpooled[g]).astype(BF16)

        dh1 = _dot(du_ref[...], win_v[...])
        xh, r = _rms_fwd(x_ref[...])
        dgmix_ref[...] += _colsum(dh1 * xh)
        dx_ref[...] = dx1v + _rms_bwd(dh1, xh, r, gmix_ref[...])

    rev = lambda w: pl.BlockSpec((tm, w), lambda i: (n_tiles - 1 - i, 0))
    return pl.pallas_call(
        body, name="bwd_mix", grid=(n_tiles,),
        in_specs=[rev(D_MODEL), rev(D_MODEL), rev(D_IN), rev(D_CONV), rev(D_POOL), _full((1, D_MODEL)),
                  pl.BlockSpec(memory_space=pl.ANY), _full((CONV_WIDTH, D_CONV)), _full((1, D_CONV)), _full((1, D_CONV)),
                  _full((4, POOL_GROUP_DIM, POOL_GROUP_DIM)), _full((1, D_POOL)), _full(after.shape)],
        out_specs=[rev(D_MODEL), rev(D_IN), _full((1, D_MODEL)), _full((CONV_WIDTH, D_CONV)), _full((1, D_CONV)),
                   _full((1, D_CONV)), _full((1, D_CONV)), _full((4, POOL_GROUP_DIM, POOL_GROUP_DIM)), _full((1, D_POOL))],
        out_shape=[jax.ShapeDtypeStruct((tokens, D_MODEL), F32), jax.ShapeDtypeStruct((tokens, D_IN), BF16),
                   jax.ShapeDtypeStruct((1, D_MODEL), F32), jax.ShapeDtypeStruct((CONV_WIDTH, D_CONV), F32),
                   jax.ShapeDtypeStruct((1, D_CONV), F32), jax.ShapeDtypeStruct((1, D_CONV), F32),
                   jax.ShapeDtypeStruct((1, D_CONV), F32),
                   jax.ShapeDtypeStruct((4, POOL_GROUP_DIM, POOL_GROUP_DIM), F32), jax.ShapeDtypeStruct((1, D_POOL), F32)],
        scratch_shapes=[pltpu.VMEM((D_IN, D_MODEL), BF16), pltpu.VMEM((D_MODEL, D_MODEL), BF16),
                        pltpu.VMEM((CONV_HALO, D_CONV), F32), pltpu.VMEM((POOL_HALO, D_POOL), F32),
                        pltpu.SemaphoreType.DMA((2,))],
        compiler_params=_params(),
    )(dx1, x2d, u_all, c_all, pooled_all, g_mix, gw, conv_w, ln_g, ln_b, pool_w, pool_scale, after)


def _wgrad(a, b, name, after=None):
    tokens, m = a.shape
    n = b.shape[1]
    tm = 512 if m % 512 == 0 else 256
    extra = [] if after is None else [after]

    def body(a_ref, b_ref, *rest):
        rest[-1][...] = _dot_tn(a_ref[...], b_ref[...]).astype(rest[-1].dtype)

    return pl.pallas_call(
        body, name=name, grid=(m // tm,),
        in_specs=[pl.BlockSpec((tokens, tm), lambda i: (0, i)), _full((tokens, n))] + [_full(t.shape) for t in extra],
        out_specs=pl.BlockSpec((tm, n), lambda i: (i, 0)),
        out_shape=jax.ShapeDtypeStruct((m, n), BF16),
        compiler_params=_params(),
    )(a, b, *extra)


def _adamw_update(w, g, m, v):
    nm = ADAM_B1 * m + (1.0 - ADAM_B1) * g
    nv = ADAM_B2 * v + (1.0 - ADAM_B2) * (g * g)
    m_hat = nm / (1.0 - ADAM_B1 ** ADAM_STEP)
    v_hat = nv / (1.0 - ADAM_B2 ** ADAM_STEP)
    return -ADAM_LR * (m_hat / (jnp.sqrt(v_hat) + ADAM_EPS) + ADAM_WD * w), nm, nv


def _adamw_small(ws, gs, ms, vs):
    n = len(ws)

    def body(*refs):
        ins, outs = refs[:4 * n], refs[4 * n:]
        for k in range(n):
            d, nm, nv = _adamw_update(*[ins[j * n + k][...] for j in range(4)])
            outs[k][...] = d
            outs[n + k][...] = nm
            outs[2 * n + k][...] = nv

    vmem = pl.BlockSpec(memory_space=pltpu.VMEM)
    outs = pl.pallas_call(
        body, name="adamw_small",
        in_specs=[vmem] * (4 * n), out_specs=[vmem] * (3 * n),
        out_shape=[jax.ShapeDtypeStruct(w.shape, F32) for w in ws] * 3,
    )(*ws, *gs, *ms, *vs)
    return outs[:n], outs[n:2 * n], outs[2 * n:]


SMALL = (("norm_mix_g", (1, 1024)), ("conv_dw_b", (1, 512)), ("conv_ln_g", (1, 512)), ("conv_ln_b", (1, 512)),
         ("pool_w", (1, 4, 128, 128)), ("pool_scale", (1, 512)), ("norm_xattn_g", (1, 1024)), ("norm_mem_g", (1, 1024)),
         ("norm_ffn_g", (1, 1024)), ("ffn_dw_b", (1, 5632)), ("norm_final_g", (1024,)))
LANES = 128


def _pack_rows(arrs):
    flat = jnp.concatenate([a.reshape(-1) for a in arrs])
    pad = (-flat.shape[0]) % (8 * LANES)
    return jnp.pad(flat, (0, pad)).reshape(-1, LANES)


def kernel(x, mem, norm_mix_g, w_in, conv_dw_w, conv_dw_b, conv_ln_g, conv_ln_b, pool_w, pool_scale, w_out, norm_xattn_g, norm_mem_g, w_q, w_kv, w_o, norm_ffn_g, w_up, ffn_dw_w, ffn_dw_b, w_down, norm_final_g, loss_target, m_norm_mix_g, m_w_in, m_conv_dw_w, m_conv_dw_b, m_conv_ln_g, m_conv_ln_b, m_pool_w, m_pool_scale, m_w_out, m_norm_xattn_g, m_norm_mem_g, m_w_q, m_w_kv, m_w_o, m_norm_ffn_g, m_w_up, m_ffn_dw_w, m_ffn_dw_b, m_w_down, m_norm_final_g, v_norm_mix_g, v_w_in, v_conv_dw_w, v_conv_dw_b, v_conv_ln_g, v_conv_ln_b, v_pool_w, v_pool_scale, v_w_out, v_norm_xattn_g, v_norm_mem_g, v_w_q, v_w_kv, v_w_o, v_norm_ffn_g, v_w_up, v_ffn_dw_w, v_ffn_dw_b, v_w_down, v_norm_final_g):
    weights = dict(norm_mix_g=norm_mix_g, w_in=w_in, conv_dw_w=conv_dw_w, conv_dw_b=conv_dw_b, conv_ln_g=conv_ln_g,
                   conv_ln_b=conv_ln_b, pool_w=pool_w, pool_scale=pool_scale, w_out=w_out, norm_xattn_g=norm_xattn_g,
                   norm_mem_g=norm_mem_g, w_q=w_q, w_kv=w_kv, w_o=w_o, norm_ffn_g=norm_ffn_g, w_up=w_up,
                   ffn_dw_w=ffn_dw_w, ffn_dw_b=ffn_dw_b, w_down=w_down, norm_final_g=norm_final_g)
    moments_m = dict(norm_mix_g=m_norm_mix_g, w_in=m_w_in, conv_dw_w=m_conv_dw_w, conv_dw_b=m_conv_dw_b,
                     conv_ln_g=m_conv_ln_g, conv_ln_b=m_conv_ln_b, pool_w=m_pool_w, pool_scale=m_pool_scale,
                     w_out=m_w_out, norm_xattn_g=m_norm_xattn_g, norm_mem_g=m_norm_mem_g, w_q=m_w_q, w_kv=m_w_kv,
                     w_o=m_w_o, norm_ffn_g=m_norm_ffn_g, w_up=m_w_up, ffn_dw_w=m_ffn_dw_w, ffn_dw_b=m_ffn_dw_b,
                     w_down=m_w_down, norm_final_g=m_norm_final_g)
    moments_v = dict(norm_mix_g=v_norm_mix_g, w_in=v_w_in, conv_dw_w=v_conv_dw_w, conv_dw_b=v_conv_dw_b,
                     conv_ln_g=v_conv_ln_g, conv_ln_b=v_conv_ln_b, pool_w=v_pool_w, pool_scale=v_pool_scale,
                     w_out=v_w_out, norm_xattn_g=v_norm_xattn_g, norm_mem_g=v_norm_mem_g, w_q=v_w_q, w_kv=v_w_kv,
                     w_o=v_w_o, norm_ffn_g=v_norm_ffn_g, w_up=v_w_up, ffn_dw_w=v_ffn_dw_w, ffn_dw_b=v_ffn_dw_b,
                     w_down=v_w_down, norm_final_g=v_norm_final_g)
    order = list(weights)
    transposed = ("w_in", "w_kv", "w_up")

    n_b, seq, _ = x.shape
    tokens = n_b * seq
    tm_mix = min(512, seq // 2)
    tm_attn = min(1024, seq // 2)
    tm_ffn = min(256, seq // 2)
    dev = 4 * lax.axis_index("x") + 2 * lax.axis_index("y") + lax.axis_index("c")

    packs = [jnp.concatenate([weights[n][0].T if n in transposed else weights[n][0] for n in names], axis=0).astype(BF16)
             for names in AG_GROUPS]
    small_sharded = _pack_rows([conv_dw_w[0], ffn_dw_w[0]])
    gw_mix, gsmall = _all_gather([packs[0], small_sharded], "weights_all_gather")
    flights = []
    after = gw_mix
    for k in (1, 2):
        own_in_place = lax.dynamic_update_slice(lax.empty((N_DEV,) + packs[k].shape, BF16), packs[k][None], (dev, 0, 0))
        flights.append(_gather_start(own_in_place, after, "weights_gather_start_%d" % k, BARRIER_IDS["gather_start"][k - 1]))
        after = flights[-1][3]
    gflat = gsmall.reshape(N_DEV, -1)
    n_cw = CONV_WIDTH * (D_CONV // N_DEV)
    n_fw = FFN_CONV_WIDTH * (2 * D_FF // N_DEV)
    conv_w = gflat[:, :n_cw].reshape(N_DEV, CONV_WIDTH, D_CONV // N_DEV).transpose(1, 0, 2).reshape(CONV_WIDTH, D_CONV)
    ffn_w = gflat[:, n_cw:n_cw + n_fw].reshape(N_DEV, FFN_CONV_WIDTH, 2 * D_FF // N_DEV).transpose(1, 0, 2).reshape(
        FFN_CONV_WIDTH, 2 * D_FF)

    x2d = x.reshape(tokens, D_MODEL)
    mem2d = mem.reshape(n_b * N_MEM, D_MODEL)
    tgt2d = loss_target.reshape(tokens, D_MODEL)
    g_final = norm_final_g.reshape(1, D_MODEL)

    def gather_finish(flight, after, tag):
        fwd_send, fwd_recv, buf = _gather_forward(*flight[:3], after, "weights_gather_forward_" + tag,
                                                  BARRIER_IDS["gather_forward"][int(tag) - 1])
        return _gather_finish(fwd_send, fwd_recv, buf, "weights_gather_finish_" + tag)

    x1, u_all, c_all, pooled_all, ymix, h1 = _fwd_mix(
        x2d, gw_mix, norm_mix_g, conv_w, conv_dw_b, conv_ln_g, conv_ln_b, pool_w[0], pool_scale, flights[1][3],
        seq, tm_mix)
    gw_attn = gather_finish(flights[0], x1, "1")
    mem_n, kv = _fwd_kv(mem2d, gw_attn, norm_mem_g)
    x2, h2, q, o = _fwd_attn(x1, kv, gw_attn, norm_xattn_g, seq, tm_attn)
    gw_ffn = gather_finish(flights[1], x2, "2")
    uu_all, cc_all, a_all, h3, dx3, dx3b, loss_part, dg_final = _fwd_ffn(
        x2, tgt2d, gw_ffn, norm_ffn_g, ffn_w, ffn_dw_b, g_final, seq, tm_ffn)

    table = _owner_table()

    def sibling_start(names, tag):
        parts = [part[n].reshape(N_DEV, W_OFF[n][1], D_MODEL) for n in names]
        return _exchange_start(parts, 4, _to_sibling, "rs_sibling_exchange_start_" + tag, BARRIER_IDS["sibling"][tag])

    def chips_start(flight, after, tag):
        parts, landed = _exchange_wait(*flight[:4], after, 4, _to_sibling, "rs_sibling_exchange_wait_" + tag)
        sums = _chip_partial_sums(table, parts, landed, "rs_chip_partial_sums_" + tag)
        return parts, landed, _exchange_start(sums, 3, _to_chip, "rs_chip_exchange_start_" + tag,
                                              BARRIER_IDS["chips"][tag])

    grads, delta, new_m, new_v = {}, {}, {}, {}

    def reduce_finish(names, parts, landed, flight, after, tag):
        _, from_chips = _exchange_wait(*flight[:4], after, 3, _to_chip, "rs_chip_exchange_wait_" + tag)
        as_rows = {n: n in transposed and W_OFF[n][1] % LANES != 0 for n in names}
        states = [tuple(t[n][0].T if as_rows[n] else t[n][0] for t in (weights, moments_m, moments_v)) for n in names]
        results = _final_update(table, parts, landed, from_chips, states, "rs_final_update_" + tag)
        for n, res in zip(names, results):
            grads[n], delta[n], new_m[n], new_v[n] = [t.T[None] if as_rows[n] else t[None] for t in res]
        return delta[names[-1]]

    part = {}
    part["w_down"] = _wgrad(a_all, dx3b, "wgrad_w_down")
    to_sibling_a = sibling_start(RS_GROUPS["a"], "a")
    dx2, dx2b, duu, d_ffn_b, d_ffn_w, dg_ffn = _bwd_ffn(dx3, x2, uu_all, cc_all, gw_ffn, norm_ffn_g, ffn_w, seq, tm_ffn)
    parts_a, landed_a, flight_a = chips_start(to_sibling_a, dx2, "a")
    part["w_up"] = _wgrad(duu, h3, "wgrad_w_up", after=flight_a[4])
    to_sibling_b = sibling_start(RS_GROUPS["b"], "b")
    dx1, dx1b, dq, dkv, dg_x = _bwd_attn(dx2, x1, q, kv, gw_attn, norm_xattn_g, to_sibling_b[4], seq, tm_mix)
    dkv_b, dg_mem = _bwd_kv(dkv, mem2d, gw_attn)
    part["w_q"] = _wgrad(h2, dq, "wgrad_w_q")
    part["w_kv"] = _wgrad(dkv_b, mem_n, "wgrad_w_kv")
    part["w_o"] = _wgrad(o, dx2b, "wgrad_w_o")
    part["w_out"] = _wgrad(ymix, dx1b, "wgrad_w_out")
    to_sibling_c = sibling_start(RS_GROUPS["c"], "c")
    parts_b, landed_b, flight_b = chips_start(to_sibling_b, to_sibling_c[4], "b")
    parts_c, landed_c, flight_c = chips_start(to_sibling_c, flight_b[4], "c")
    dx, du, dg_mix, d_conv_w, d_conv_b, d_ln_g, d_ln_b, d_pool_w, d_pool_scale = _bwd_mix(
        dx1, x2d, u_all, c_all, pooled_all, gw_mix, norm_mix_g, conv_w, conv_ln_g, conv_ln_b, pool_w[0], pool_scale,
        flight_c[4], seq, tm_mix)
    grad_x = dx.reshape(x.shape)

    small_grads = dict(norm_mix_g=dg_mix, conv_dw_b=d_conv_b, conv_ln_g=d_ln_g, conv_ln_b=d_ln_b, pool_w=d_pool_w,
                       pool_scale=d_pool_scale, norm_xattn_g=dg_x, norm_mem_g=dg_mem, norm_ffn_g=dg_ffn,
                       ffn_dw_b=d_ffn_b, norm_final_g=dg_final)
    small_list = [small_grads[n] for n, _ in SMALL] + [d_conv_w, d_ffn_w, loss_part[:1]]
    small_mine = _pack_rows(small_list)
    small_flight = _broadcast_start(
        lax.dynamic_update_slice(lax.empty((N_DEV,) + small_mine.shape, F32), small_mine[None], (dev, 0, 0)),
        "small_grads_broadcast_start", BARRIER_IDS["broadcast"])

    part["w_in"] = _wgrad(du, h1, "wgrad_w_in", after=small_flight[3])
    to_sibling_d = sibling_start(RS_GROUPS["d"], "d")
    parts_d, landed_d, flight_d = chips_start(to_sibling_d, to_sibling_d[4], "d")
    updated_a = reduce_finish(RS_GROUPS["a"], parts_a, landed_a, flight_a, flight_d[4], "a")
    updated_b = reduce_finish(RS_GROUPS["b"], parts_b, landed_b, flight_b, updated_a, "b")
    updated_c = reduce_finish(RS_GROUPS["c"], parts_c, landed_c, flight_c, updated_b, "c")
    small_all = _broadcast_wait(*small_flight[:3], updated_c, "small_grads_broadcast_wait")
    small_sum = _sum_blocks(small_all).reshape(-1)

    pos = 0
    for n, shape in SMALL:
        size = 1
        for s in shape:
            size *= s
        grads[n] = small_sum[pos:pos + size].reshape(shape)
        pos += size
    full_conv_w = small_sum[pos:pos + CONV_WIDTH * D_CONV].reshape(CONV_WIDTH, D_CONV)
    pos += CONV_WIDTH * D_CONV
    full_ffn_w = small_sum[pos:pos + FFN_CONV_WIDTH * 2 * D_FF].reshape(FFN_CONV_WIDTH, 2 * D_FF)
    loss = small_sum[pos + FFN_CONV_WIDTH * 2 * D_FF]
    grads["conv_dw_w"] = lax.dynamic_slice_in_dim(full_conv_w, dev * (D_CONV // N_DEV), D_CONV // N_DEV, axis=1)[None]
    grads["ffn_dw_w"] = lax.dynamic_slice_in_dim(full_ffn_w, dev * (2 * D_FF // N_DEV), 2 * D_FF // N_DEV, axis=1)[None]

    small_names = [n for n in order if n not in W_OFF]
    swap = lambda t: jnp.transpose(t, (1, 0, 2))
    two_d = lambda t: t.reshape(1, -1) if t.ndim == 1 else (swap(t) if t.ndim == 3 else t)
    outs = _adamw_small(*[[two_d(t[n]) for n in small_names] for t in (weights, grads, moments_m, moments_v)])
    for res, out in zip((delta, new_m, new_v), outs):
        for n, o in zip(small_names, out):
            res[n] = swap(o) if o.ndim == 3 else o.reshape(weights[n].shape)

    reduce_finish(RS_GROUPS["d"], parts_d, landed_d, flight_d, delta[small_names[-1]], "d")

    return (loss, grad_x, *[grads[n] for n in order], *[delta[n] for n in order],
            *[new_m[n] for n in order], *[new_v[n] for n in order])
```

```python
import jax
import jax.numpy as jnp
from jax import lax
from jax.experimental import pallas as pl
from jax.experimental.pallas import tpu as pltpu

F32 = jnp.float32
BF16 = jnp.bfloat16
MESH = pl.DeviceIdType.MESH

N_DEV = 8
D_MODEL = 1024
D_CONV = 512
D_POOL = 512
CONV_WIDTH = 31
POOL_WINDOWS = (2, 4, 8, 16)
POOL_GROUP_DIM = 128
D_IN = 1536
N_MEM = 256
HEADS = 4
HEAD_DIM = 256
D_FF = 2816
FFN_CONV_WIDTH = 3
EPS = 1e-6
ADAM_LR = 0.001
ADAM_B1 = 0.9
ADAM_B2 = 0.999
ADAM_EPS = 1e-08
ADAM_WD = 0.01
ADAM_STEP = 10

VMEM_LIMIT_V7X = 56 * 1024 * 1024
CONV_HALO = 32
POOL_HALO = 16
FFN_HALO = 8
FFN_CHUNK = 2816

W_ROWS = (("w_in", 192), ("w_out", 128), ("w_q", 128), ("w_kv", 256), ("w_o", 128), ("w_up", 704), ("w_down", 352))
AG_GROUPS = (("w_in", "w_out"), ("w_q", "w_kv", "w_o"), ("w_up", "w_down"))
W_OFF = {}
for _names in AG_GROUPS:
    _o = 0
    for _n in _names:
        W_OFF[_n] = (_o, dict(W_ROWS)[_n])
        _o += dict(W_ROWS)[_n]
RS_GROUPS = {"a": ("w_up", "w_down"), "b": ("w_q", "w_kv", "w_o", "w_out"), "c": ("w_in",)}
BARRIER_IDS = {"gather_start": (0, 1), "gather_forward": (2, 3), "sibling": {"a": 4, "b": 5, "c": 6},
               "chips": {"a": 7, "b": 8, "c": 9}, "broadcast": 10}


def _dot(a, b):
    return jnp.dot(a, b, preferred_element_type=F32)


def _dot_nt(a, b):
    return lax.dot_general(a, b, (((1,), (1,)), ((), ())), preferred_element_type=F32)


def _dot_tn(a, b):
    return lax.dot_general(a, b, (((0,), (0,)), ((), ())), preferred_element_type=F32)


def _sigmoid(v):
    return 1.0 / (1.0 + jnp.exp(-v))


def _rms_fwd(v):
    r = lax.rsqrt(jnp.mean(v * v, axis=-1, keepdims=True) + EPS)
    return v * r, r


def _rms_bwd(dh, vh, r, g):
    gd = dh * g
    return r * (gd - vh * jnp.mean(gd * vh, axis=-1, keepdims=True))


def _sublane_shifts(v):
    rows = v.shape[0]
    return [v] + [pltpu.roll(v, rows - b, 0) for b in range(1, 8)]


def _colsum(v):
    return jnp.sum(v, axis=0, keepdims=True)


def _colsum_mxu(v):
    return _dot(jnp.ones((8, v.shape[0]), BF16), v.astype(BF16))[0:1, :]


def _full(shape):
    return pl.BlockSpec(shape, lambda *_: (0,) * len(shape))


def _params(sem=("arbitrary",), vmem=VMEM_LIMIT_V7X):
    return pltpu.CompilerParams(dimension_semantics=sem, vmem_limit_bytes=vmem)


def _load_weight(g_hbm, name, dst, sem):
    off, rows = W_OFF[name]
    return [pltpu.make_async_copy(g_hbm.at[d, pl.ds(off, rows), :], dst.at[pl.ds(d * rows, rows), :], sem)
            for d in range(N_DEV)]


def _start_weights(g_hbm, names, dsts, sems):
    @pl.when(pl.program_id(0) == 0)
    def _():
        copies = [_load_weight(g_hbm, name, dst, sems.at[k]) for k, (name, dst) in enumerate(zip(names, dsts))]
        for cp in sum(copies, []):
            cp.start()
        for cp in sum(copies, []):
            cp.wait()


def _position():
    x, y, c = lax.axis_index("x"), lax.axis_index("y"), lax.axis_index("c")
    chips = [(1 - x, y), (x, 1 - y), (1 - x, 1 - y)]
    return x, y, c, chips


def _dev(px, py, pc):
    return 4 * px + 2 * py + pc


def _all_gather(arrs, name):
    n = len(arrs)

    def body(*refs):
        ins, outs = refs[:n], refs[n:2 * n]
        send_sems, recv_sems, local_sems = refs[2 * n:2 * n + 3]
        bounce = refs[2 * n + 3:]
        x, y, c, chips = _position()
        me, sibling = (x, y, c), (x, y, 1 - c)

        def copy(a, k, block, to, src=None):
            rows = outs[a].at[_dev(*block)]
            return pltpu.make_async_remote_copy(
                src_ref=rows if src is None else src, dst_ref=rows,
                send_sem=send_sems.at[a, k], recv_sem=recv_sems.at[a, k], device_id=to, device_id_type=MESH)

        sends = []
        for a in range(n):
            first = [copy(a, 0, me, sibling, src=ins[a])]
            first += [copy(a, 1 + j, me, (*chip, c), src=ins[a]) for j, chip in enumerate(chips)]
            for cp in first:
                cp.start()
            sends += first
        started = []
        for a in range(n):
            load = pltpu.make_async_copy(ins[a], bounce[a], local_sems.at[a, 0])
            load.start()
            load.wait()
            mine = pltpu.make_async_copy(bounce[a], outs[a].at[_dev(*me)], local_sems.at[a, 1])
            mine.start()
            started.append(mine)
        for j, chip in enumerate(chips):
            for a in range(n):
                copy(a, 1 + j, (*chip, c), me).wait_recv()
                passed = copy(a, 4 + j, (*chip, c), sibling)
                passed.start()
                sends.append(passed)
        for a in range(n):
            copy(a, 0, sibling, me).wait_recv()
            for j, chip in enumerate(chips):
                copy(a, 4 + j, (*chip, 1 - c), me).wait_recv()
        for cp in sends:
            cp.wait_send()
        for mine in started:
            mine.wait()

    any_spec = pl.BlockSpec(memory_space=pl.ANY)
    return pl.pallas_call(
        body, name=name,
        out_shape=[jax.ShapeDtypeStruct((N_DEV,) + a.shape, a.dtype) for a in arrs],
        in_specs=[any_spec] * n, out_specs=[any_spec] * n,
        scratch_shapes=[pltpu.SemaphoreType.DMA((n, 7)), pltpu.SemaphoreType.DMA((n, 7)), pltpu.SemaphoreType.DMA((n, 2))]
        + [pltpu.VMEM(a.shape, a.dtype) for a in arrs],
    )(*arrs)


_HBM = pl.BlockSpec(memory_space=pltpu.HBM)
_SEM = pl.BlockSpec(memory_space=pltpu.SEMAPHORE)
_SIDE_EFFECT = pltpu.SideEffectType.DATAFLOW_SIDE_EFFECTING


def _handshake(peers):
    barrier = pltpu.get_barrier_semaphore()
    for peer in peers:
        pl.semaphore_signal(barrier, inc=1, device_id=peer, device_id_type=MESH)
    pl.semaphore_wait(barrier, len(peers))


def _gather_start(buf, after, name, collective_id):
    def body(buf_ref, after_ref, send_sems, recv_sems, buf_thru, token):
        del after_ref, buf_thru
        x, y, c, chips = _position()
        rows = buf_ref.at[_dev(x, y, c)]
        targets = [(x, y, 1 - c)] + [(*chip, c) for chip in chips]
        _handshake(targets)
        for k, to in enumerate(targets):
            pltpu.make_async_remote_copy(src_ref=rows, dst_ref=rows, send_sem=send_sems.at[k], recv_sem=recv_sems.at[k],
                                         device_id=to, device_id_type=MESH).start()
        token[...] = jnp.zeros_like(token)

    return pl.pallas_call(
        body, name=name,
        out_shape=(pltpu.SemaphoreType.DMA((4,)), pltpu.SemaphoreType.DMA((4,)), pltpu.HBM(buf.shape, buf.dtype),
                   jax.ShapeDtypeStruct((8, 128), F32)),
        in_specs=(_HBM, pl.BlockSpec(memory_space=pl.ANY)),
        out_specs=(_SEM, _SEM, _HBM, pl.BlockSpec(memory_space=pltpu.VMEM)),
        input_output_aliases={0: 2},
        compiler_params=pltpu.CompilerParams(has_side_effects=_SIDE_EFFECT, collective_id=collective_id),
    )(pltpu.with_memory_space_constraint(buf, pltpu.HBM), after)


def _gather_forward(send_sems, recv_sems, buf, after, name, collective_id):
    def body(buf_ref, send_sems, recv_sems, after_ref, fwd_send, fwd_recv, buf_thru):
        del after_ref, buf_thru
        x, y, c, chips = _position()
        sibling = (x, y, 1 - c)

        def copy(block, k, sends, recvs):
            rows = buf_ref.at[_dev(*block)]
            return pltpu.make_async_remote_copy(src_ref=rows, dst_ref=rows, send_sem=sends.at[k], recv_sem=recvs.at[k],
                                                device_id=sibling, device_id_type=MESH)

        _handshake([sibling])
        for k in range(4):
            copy((x, y, c), k, send_sems, recv_sems).wait_send()
        copy(sibling, 0, send_sems, recv_sems).wait_recv()
        for j, chip in enumerate(chips):
            copy((*chip, c), 1 + j, send_sems, recv_sems).wait_recv()
            copy((*chip, c), j, fwd_send, fwd_recv).start()

    return pl.pallas_call(
        body, name=name,
        out_shape=(pltpu.SemaphoreType.DMA((3,)), pltpu.SemaphoreType.DMA((3,)), pltpu.HBM(buf.shape, buf.dtype)),
        in_specs=(_HBM, _SEM, _SEM, pl.BlockSpec(memory_space=pl.ANY)), out_specs=(_SEM, _SEM, _HBM),
        input_output_aliases={0: 2},
        compiler_params=pltpu.CompilerParams(has_side_effects=_SIDE_EFFECT, collective_id=collective_id),
    )(buf, send_sems, recv_sems, after)


def _gather_finish(fwd_send, fwd_recv, buf, name):
    def body(buf_ref, fwd_send, fwd_recv, buf_thru):
        del buf_thru
        x, y, c, chips = _position()
        for j, chip in enumerate(chips):
            cp = pltpu.make_async_remote_copy(
                src_ref=buf_ref.at[_dev(*chip, c)], dst_ref=buf_ref.at[_dev(*chip, 1 - c)], send_sem=fwd_send.at[j],
                recv_sem=fwd_recv.at[j], device_id=(x, y, 1 - c), device_id_type=MESH)
            cp.wait_send()
            cp.wait_recv()

    return pl.pallas_call(
        body, name=name,
        out_shape=pltpu.HBM(buf.shape, buf.dtype),
        in_specs=(_HBM, _SEM, _SEM), out_specs=_HBM,
        input_output_aliases={0: 0},
        compiler_params=pltpu.CompilerParams(has_side_effects=_SIDE_EFFECT),
    )(buf, fwd_send, fwd_recv)


def _everyone_else(x, y, c, chips):
    return [(x, y, 1 - c)] + [(*chip, core) for chip in chips for core in (c, 1 - c)]


def _broadcast_start(buf, name, collective_id):
    def body(buf_ref, send_sems, recv_sems, buf_thru, token):
        del buf_thru
        x, y, c, chips = _position()
        rows = buf_ref.at[_dev(x, y, c)]
        _handshake(_everyone_else(x, y, c, chips))
        for k, to in enumerate(_everyone_else(x, y, c, chips)):
            pltpu.make_async_remote_copy(src_ref=rows, dst_ref=rows, send_sem=send_sems.at[k], recv_sem=recv_sems.at[k],
                                         device_id=to, device_id_type=MESH).start()
        token[...] = jnp.zeros_like(token)

    return pl.pallas_call(
        body, name=name,
        out_shape=(pltpu.SemaphoreType.DMA((7,)), pltpu.SemaphoreType.DMA((7,)), pltpu.HBM(buf.shape, buf.dtype),
                   jax.ShapeDtypeStruct((8, 128), F32)),
        in_specs=(_HBM,), out_specs=(_SEM, _SEM, _HBM, pl.BlockSpec(memory_space=pltpu.VMEM)),
        input_output_aliases={0: 2},
        compiler_params=pltpu.CompilerParams(has_side_effects=_SIDE_EFFECT, collective_id=collective_id),
    )(pltpu.with_memory_space_constraint(buf, pltpu.HBM))


def _broadcast_wait(send_sems, recv_sems, buf, after, name):
    def body(buf_ref, send_sems, recv_sems, after_ref, buf_thru):
        del after_ref, buf_thru
        x, y, c, chips = _position()
        for k, peer in enumerate(_everyone_else(x, y, c, chips)):
            cp = pltpu.make_async_remote_copy(
                src_ref=buf_ref.at[_dev(x, y, c)], dst_ref=buf_ref.at[_dev(*peer)], send_sem=send_sems.at[k],
                recv_sem=recv_sems.at[k], device_id=peer, device_id_type=MESH)
            cp.wait_send()
            cp.wait_recv()

    return pl.pallas_call(
        body, name=name,
        out_shape=pltpu.HBM(buf.shape, buf.dtype),
        in_specs=(_HBM, _SEM, _SEM, pl.BlockSpec(memory_space=pl.ANY)), out_specs=_HBM,
        input_output_aliases={0: 0},
        compiler_params=pltpu.CompilerParams(has_side_effects=_SIDE_EFFECT),
    )(buf, send_sems, recv_sems, after)


def _to_sibling(j, x, y, c, chips):
    return _dev(*([(x, y)] + chips)[j], 1 - c), (x, y, 1 - c)


def _to_chip(j, x, y, c, chips):
    return j, (*chips[j], c)


def _exchange_start(srcs, n_slots, route, name, collective_id):
    n = len(srcs)

    def body(*refs):
        s_refs, land_refs = refs[:n], refs[n:2 * n]
        send_sems, recv_sems = refs[2 * n:2 * n + 2]
        token = refs[-1]
        x, y, c, chips = _position()
        _handshake([(x, y, 1 - c)] if route is _to_sibling else [route(j, x, y, c, chips)[1] for j in range(n_slots)])
        for k in range(n):
            for j in range(n_slots):
                block, to = route(j, x, y, c, chips)
                pltpu.make_async_remote_copy(
                    src_ref=s_refs[k].at[block], dst_ref=land_refs[k].at[j], send_sem=send_sems.at[n_slots * k + j],
                    recv_sem=recv_sems.at[n_slots * k + j], device_id=to, device_id_type=MESH).start()
        token[...] = jnp.zeros_like(token)

    lands = [jax.ShapeDtypeStruct((n_slots,) + s.shape[1:], s.dtype) for s in srcs]
    outs = pl.pallas_call(
        body, name=name,
        out_shape=(pltpu.SemaphoreType.DMA((n_slots * n,)), pltpu.SemaphoreType.DMA((n_slots * n,)),
                   *[pltpu.HBM(s.shape, s.dtype) for s in srcs], *[pltpu.HBM(l.shape, l.dtype) for l in lands],
                   jax.ShapeDtypeStruct((8, 128), F32)),
        in_specs=[_HBM] * (2 * n), out_specs=(_SEM, _SEM, *[_HBM] * (2 * n), pl.BlockSpec(memory_space=pltpu.VMEM)),
        input_output_aliases={k: 2 + k for k in range(2 * n)},
        compiler_params=pltpu.CompilerParams(has_side_effects=_SIDE_EFFECT, collective_id=collective_id),
    )(*[pltpu.with_memory_space_constraint(s, pltpu.HBM) for s in srcs],
      *[pltpu.with_memory_space_constraint(lax.empty(l.shape, l.dtype), pltpu.HBM) for l in lands])
    return outs[0], outs[1], outs[2:2 + n], outs[2 + n:2 + 2 * n], outs[-1]


def _exchange_wait(send_sems, recv_sems, s_thru, land_thru, after, n_slots, route, name):
    n = len(s_thru)

    def body(*refs):
        s_refs, land_refs = refs[:n], refs[n:2 * n]
        send_sems, recv_sems = refs[2 * n:2 * n + 2]
        x, y, c, chips = _position()
        for k in range(n):
            for j in range(n_slots):
                block, to = route(j, x, y, c, chips)
                cp = pltpu.make_async_remote_copy(
                    src_ref=s_refs[k].at[block], dst_ref=land_refs[k].at[j], send_sem=send_sems.at[n_slots * k + j],
                    recv_sem=recv_sems.at[n_slots * k + j], device_id=to, device_id_type=MESH)
                cp.wait_send()
                cp.wait_recv()

    outs = pl.pallas_call(
        body, name=name,
        out_shape=(*[pltpu.HBM(s.shape, s.dtype) for s in s_thru], *[pltpu.HBM(l.shape, l.dtype) for l in land_thru]),
        in_specs=[_HBM] * (2 * n) + [_SEM, _SEM, pl.BlockSpec(memory_space=pl.ANY)], out_specs=[_HBM] * (2 * n),
        input_output_aliases={k: k for k in range(2 * n)},
        compiler_params=pltpu.CompilerParams(has_side_effects=_SIDE_EFFECT),
    )(*s_thru, *land_thru, send_sems, recv_sems, after)
    return outs[:n], outs[n:]


def _owner_table():
    x, y, c = lax.axis_index("x"), lax.axis_index("y"), lax.axis_index("c")
    chips = [(x, y), (1 - x, y), (x, 1 - y), (1 - x, 1 - y)]
    return jnp.stack([_dev(px, py, c) for px, py in chips]).astype(jnp.int32)


def _chip_partial_sums(table, parts, from_sibling, name):
    n = len(parts)

    def body(tab_ref, *refs):
        del tab_ref
        for g_ref, l_ref, out_ref in zip(refs[:n], refs[n:2 * n], refs[2 * n:]):
            out_ref[...] = (g_ref[...].astype(F32) + l_ref[...].astype(F32)).astype(out_ref.dtype)

    block = lambda p: (None,) + p.shape[1:]
    grid_spec = pltpu.PrefetchScalarGridSpec(
        num_scalar_prefetch=1, grid=(3,),
        in_specs=[pl.BlockSpec(block(p), lambda j, tab: (tab[j + 1], 0, 0)) for p in parts]
        + [pl.BlockSpec(block(p), lambda j, tab: (j + 1, 0, 0)) for p in parts],
        out_specs=[pl.BlockSpec(block(p), lambda j, tab: (j, 0, 0)) for p in parts])
    return pl.pallas_call(
        body, name=name, grid_spec=grid_spec,
        out_shape=[jax.ShapeDtypeStruct((3,) + p.shape[1:], BF16) for p in parts],
        compiler_params=_params(("arbitrary",)),
    )(table, *parts, *from_sibling)


def _final_update(table, parts, from_sibling, from_chips, states, name):
    n = len(parts)
    flipped = [states[k][0].shape != parts[k].shape[1:] for k in range(n)]

    def body(tab_ref, *refs):
        del tab_ref
        ins, outs = refs[:6 * n], refs[6 * n:]
        for k in range(n):
            acc = ins[k][...].astype(F32) + ins[n + k][...].astype(F32)
            for j in range(3):
                acc = acc + ins[2 * n + k][j].astype(F32)
            if flipped[k]:
                acc = acc.T
            w_ref, m_ref, v_ref = ins[3 * n + 3 * k:3 * n + 3 * k + 3]
            outs[4 * k][...] = acc
            for out_ref, val in zip(outs[4 * k + 1:4 * k + 4], _adamw_update(w_ref[...], acc, m_ref[...], v_ref[...])):
                out_ref[...] = val

    def grad_block(k, lead, at):
        r, c = parts[k].shape[1:]
        if flipped[k]:
            return pl.BlockSpec(lead + (r, c // 2), lambda t, tab: (*at(tab), 0, t))
        return pl.BlockSpec(lead + (r // 2, c), lambda t, tab: (*at(tab), t, 0))

    def state_block(k):
        a, b = states[k][0].shape
        return pl.BlockSpec((a // 2, b), lambda t, tab: (t, 0))

    grid_spec = pltpu.PrefetchScalarGridSpec(
        num_scalar_prefetch=1, grid=(2,),
        in_specs=[grad_block(k, (None,), lambda tab: (tab[0],)) for k in range(n)]
        + [grad_block(k, (None,), lambda tab: (0,)) for k in range(n)]
        + [grad_block(k, (3,), lambda tab: (0,)) for k in range(n)]
        + [state_block(k) for k in range(n) for _ in range(3)],
        out_specs=[state_block(k) for k in range(n) for _ in range(4)])
    outs = pl.pallas_call(
        body, name=name, grid_spec=grid_spec,
        out_shape=[jax.ShapeDtypeStruct(states[k][0].shape, F32) for k in range(n) for _ in range(4)],
        compiler_params=_params(("arbitrary",)),
    )(table, *parts, *from_sibling, *from_chips, *[t for k in range(n) for t in states[k]])
    return [outs[4 * k:4 * k + 4] for k in range(n)]


def _sum_blocks(g8):
    _, rows, cols = g8.shape

    def body(g_ref, out_ref):
        acc = g_ref[0]
        for d in range(1, N_DEV):
            acc = acc + g_ref[d]
        out_ref[...] = acc

    return pl.pallas_call(
        body, name="small_grad_sum", grid=(1,),
        in_specs=[_full((N_DEV, rows, cols))], out_specs=_full((rows, cols)),
        out_shape=jax.ShapeDtypeStruct((rows, cols), F32),
        compiler_params=_params(("arbitrary",)),
    )(g8)


def _fwd_mix(x2d, gw, g_mix, conv_w, conv_b, ln_g, ln_b, pool_w, pool_scale, after, seq, tm):
    tokens = x2d.shape[0]
    n_tiles = tokens // tm
    tps = seq // tm

    def body(x_ref, gmix_ref, gw_hbm, cw_ref, cb_ref, lng_ref, lnb_ref, pw_ref, ps_ref, after_ref,
             x1_ref, u_ref, c_ref, pooled_ref, ymix_ref, h1_ref,
             win_v, wout_v, hc_carry, up_carry, sem):
        del after_ref
        i = pl.program_id(0)

        _start_weights(gw_hbm, ("w_in", "w_out"), (win_v, wout_v), sem)

        @pl.when(i % tps == 0)
        def _():
            hc_carry[...] = jnp.zeros_like(hc_carry)
            up_carry[...] = jnp.zeros_like(up_carry)

        x = x_ref[...]
        xh, _ = _rms_fwd(x)
        h1 = (xh * gmix_ref[...]).astype(BF16)
        h1_ref[...] = h1
        u = _dot_nt(h1, win_v[...])
        u_ref[...] = u
        val, gate, up = u[:, :D_CONV], u[:, D_CONV:2 * D_CONV], u[:, 2 * D_CONV:]

        extp = jnp.concatenate([up_carry[...], up], axis=0)
        up_carry[...] = up[tm - POOL_HALO:, :]
        pos = lax.broadcasted_iota(jnp.int32, (tm, 1), 0) + (i % tps) * tm
        run = extp
        mixed = []
        for g, w in enumerate(POOL_WINDOWS):
            lo = g * POOL_GROUP_DIM
            run = run[:, POOL_GROUP_DIM if g else 0:]
            run = run + pltpu.roll(run, w // 2, 0)
            cnt = jnp.minimum(pos + 1, w).astype(F32)
            pooled = run[POOL_HALO:, :POOL_GROUP_DIM] / cnt - up[:, lo:lo + POOL_GROUP_DIM]
            pooled = pooled.astype(BF16)
            pooled_ref[:, lo:lo + POOL_GROUP_DIM] = pooled
            mixed.append(_dot(pooled, pw_ref[g].astype(BF16)))
        y_pool = jnp.concatenate(mixed, axis=-1) * ps_ref[...]
        y_pool = y_pool.astype(BF16)
        ymix_ref[:, D_CONV:] = y_pool
        out = _dot(y_pool, wout_v[D_CONV:, :])

        hc = val * _sigmoid(gate)
        ext = jnp.concatenate([hc_carry[...], hc], axis=0)
        hc_carry[...] = hc[tm - CONV_HALO:, :]
        conv = jnp.broadcast_to(cb_ref[...], (tm, D_CONV))
        ahead_by = _sublane_shifts(ext)
        for k in range(CONV_WIDTH):
            whole, part = divmod(CONV_HALO - (CONV_WIDTH - 1) + k, 8)
            conv = conv + cw_ref[k:k + 1, :] * ahead_by[part][8 * whole:8 * whole + tm, :]
        c_ref[...] = conv
        mu = jnp.mean(conv, axis=-1, keepdims=True)
        cen = conv - mu
        ln = cen * lax.rsqrt(jnp.mean(cen * cen, axis=-1, keepdims=True) + EPS) * lng_ref[...] + lnb_ref[...]
        y_conv = ln * _sigmoid(ln)
        y_conv = y_conv.astype(BF16)
        ymix_ref[:, :D_CONV] = y_conv
        x1_ref[...] = x + (out + _dot(y_conv, wout_v[:D_CONV, :]))

    row = lambda w: pl.BlockSpec((tm, w), lambda i: (i, 0))
    return pl.pallas_call(
        body, name="fwd_mix", grid=(n_tiles,),
        in_specs=[row(D_MODEL), _full((1, D_MODEL)), pl.BlockSpec(memory_space=pl.ANY),
                  _full((CONV_WIDTH, D_CONV)), _full((1, D_CONV)), _full((1, D_CONV)), _full((1, D_CONV)),
                  _full((4, POOL_GROUP_DIM, POOL_GROUP_DIM)), _full((1, D_POOL)), _full(after.shape)],
        out_specs=[row(D_MODEL), row(D_IN), row(D_CONV), row(D_POOL), row(D_MODEL), row(D_MODEL)],
        out_shape=[jax.ShapeDtypeStruct((tokens, D_MODEL), F32), jax.ShapeDtypeStruct((tokens, D_IN), F32),
                   jax.ShapeDtypeStruct((tokens, D_CONV), F32), jax.ShapeDtypeStruct((tokens, D_POOL), BF16),
                   jax.ShapeDtypeStruct((tokens, D_MODEL), BF16), jax.ShapeDtypeStruct((tokens, D_MODEL), BF16)],
        scratch_shapes=[pltpu.VMEM((D_IN, D_MODEL), BF16), pltpu.VMEM((D_MODEL, D_MODEL), BF16),
                        pltpu.VMEM((CONV_HALO, D_CONV), F32), pltpu.VMEM((POOL_HALO, D_POOL), F32),
                        pltpu.SemaphoreType.DMA((2,))],
        compiler_params=_params(),
    )(x2d, g_mix, gw, conv_w, conv_b, ln_g, ln_b, pool_w, pool_scale, after)


def _fwd_kv(mem2d, gw, g_mem):
    rows = mem2d.shape[0]
    n_b = rows // N_MEM

    def body(mem_ref, g_ref, gw_hbm, mn_ref, kv_ref, wkv_v, sem):
        @pl.when(pl.program_id(0) == 0)
        def _():
            copies = _load_weight(gw_hbm, "w_kv", wkv_v, sem)
            for cp in copies:
                cp.start()
            for cp in copies:
                cp.wait()

        mh, _ = _rms_fwd(mem_ref[...])
        mn = (mh * g_ref[...]).astype(BF16)
        mn_ref[...] = mn
        kv_ref[...] = _dot_nt(mn, wkv_v[...]).astype(BF16)

    return pl.pallas_call(
        body, name="fwd_kv", grid=(n_b,),
        in_specs=[pl.BlockSpec((N_MEM, D_MODEL), lambda b: (b, 0)), _full((1, D_MODEL)), pl.BlockSpec(memory_space=pl.ANY)],
        out_specs=[pl.BlockSpec((N_MEM, D_MODEL), lambda b: (b, 0)), pl.BlockSpec((N_MEM, 2 * D_MODEL), lambda b: (b, 0))],
        out_shape=[jax.ShapeDtypeStruct((rows, D_MODEL), BF16), jax.ShapeDtypeStruct((rows, 2 * D_MODEL), BF16)],
        scratch_shapes=[pltpu.VMEM((2 * D_MODEL, D_MODEL), BF16), pltpu.SemaphoreType.DMA],
        compiler_params=_params(),
    )(mem2d, g_mem, gw)


def _softmax_rows(s):
    e = jnp.exp(s - jnp.max(s, axis=-1, keepdims=True))
    return e / jnp.sum(e, axis=-1, keepdims=True)


def _fwd_attn(x1, kv, gw, g_x, seq, tm):
    tokens = x1.shape[0]
    n_tiles = tokens // tm
    tps = seq // tm

    def body(x1_ref, kv_ref, g_ref, gw_hbm, x2_ref, h2_ref, q_ref, o_ref, wq_v, wo_v, sem):
        _start_weights(gw_hbm, ("w_q", "w_o"), (wq_v, wo_v), sem)
        x1v = x1_ref[...]
        xh, _ = _rms_fwd(x1v)
        h2 = (xh * g_ref[...]).astype(BF16)
        h2_ref[...] = h2
        q = (_dot(h2, wq_v[...]) * (HEAD_DIM ** -0.5)).astype(BF16)
        q_ref[...] = q
        heads = [slice(h * HEAD_DIM, (h + 1) * HEAD_DIM) for h in range(HEADS)]
        scores = [_dot_nt(q[:, hd], kv_ref[:, hd]) for hd in heads]
        probs = [_softmax_rows(s).astype(BF16) for s in scores]
        outs = [_dot(p, kv_ref[:, pl.ds(D_MODEL + h * HEAD_DIM, HEAD_DIM)]) for h, p in enumerate(probs)]
        o = jnp.concatenate(outs, axis=-1).astype(BF16)
        o_ref[...] = o
        x2_ref[...] = x1v + _dot(o, wo_v[...])

    row = lambda w: pl.BlockSpec((tm, w), lambda i: (i, 0))
    return pl.pallas_call(
        body, name="fwd_attn", grid=(n_tiles,),
        in_specs=[row(D_MODEL), pl.BlockSpec((N_MEM, 2 * D_MODEL), lambda i: (i // tps, 0)), _full((1, D_MODEL)),
                  pl.BlockSpec(memory_space=pl.ANY)],
        out_specs=[row(D_MODEL)] * 4,
        out_shape=[jax.ShapeDtypeStruct((tokens, D_MODEL), F32)] + [jax.ShapeDtypeStruct((tokens, D_MODEL), BF16)] * 3,
        scratch_shapes=[pltpu.VMEM((D_MODEL, D_MODEL), BF16), pltpu.VMEM((D_MODEL, D_MODEL), BF16), pltpu.SemaphoreType.DMA((2,))],
        compiler_params=_params(),
    )(x1, kv, g_x, gw)


def _ffn_conv(uu, halo, w_ref, b_ref, cols):
    ext = jnp.concatenate([halo, uu], axis=0)
    p1 = pltpu.roll(ext, 1, 0)[FFN_HALO:, :]
    p2 = pltpu.roll(ext, 2, 0)[FFN_HALO:, :]
    return b_ref[:, cols] + w_ref[2:3, cols] * uu + w_ref[1:2, cols] * p1 + w_ref[0:1, cols] * p2


def _fwd_ffn(x2, target, gw, g_ffn, ffn_w, ffn_b, g_final, seq, tm):
    tokens = x2.shape[0]
    n_tiles = tokens // tm
    tps = seq // tm
    n_chunks = D_FF // FFN_CHUNK

    def body(x2_ref, tgt_ref, gffn_ref, gw_hbm, fw_ref, fb_ref, gfin_ref,
             uu_ref, cc_ref, a_ref, h3_ref, dx3_ref, dx3b_ref, loss_ref, dgfin_ref,
             wup_v, wdown_v, carry, sem):
        i = pl.program_id(0)

        _start_weights(gw_hbm, ("w_up", "w_down"), (wup_v, wdown_v), sem)

        @pl.when(i == 0)
        def _():
            loss_ref[...] = jnp.zeros_like(loss_ref)
            dgfin_ref[...] = jnp.zeros_like(dgfin_ref)

        @pl.when(i % tps == 0)
        def _():
            carry[...] = jnp.zeros_like(carry)

        x2v = x2_ref[...]
        xh, _ = _rms_fwd(x2v)
        h3 = (xh * gffn_ref[...]).astype(BF16)
        h3_ref[...] = h3
        acc = jnp.zeros((tm, D_MODEL), F32)
        for jc in range(n_chunks):
            halves = []
            for half in range(2):
                cols = pl.ds(half * D_FF + jc * FFN_CHUNK, FFN_CHUNK)
                uu = _dot_nt(h3, wup_v[cols, :])
                uu_ref[:, cols] = uu.astype(BF16)
                cc = _ffn_conv(uu, carry[:, cols], fw_ref, fb_ref, cols)
                cc_ref[:, cols] = cc.astype(BF16)
                halves.append(cc)
                carry[:, cols] = uu[tm - FFN_HALO:, :]
            gate, val = halves
            a = (gate * _sigmoid(gate) * val).astype(BF16)
            a_ref[:, pl.ds(jc * FFN_CHUNK, FFN_CHUNK)] = a
            acc = acc + _dot(a, wdown_v[pl.ds(jc * FFN_CHUNK, FFN_CHUNK), :])
        x3 = x2v + acc

        xh3, r3 = _rms_fwd(x3)
        gfin = gfin_ref[...]
        err = xh3 * gfin - tgt_ref[...]
        loss_ref[...] += jnp.full(loss_ref.shape, jnp.sum(err * err) * (0.5 / D_MODEL), F32)
        dy = err * (1.0 / D_MODEL)
        dgfin_ref[...] += _colsum(dy * xh3)
        dx3 = _rms_bwd(dy, xh3, r3, gfin)
        dx3_ref[...] = dx3
        dx3b_ref[...] = dx3.astype(BF16)

    row = lambda w: pl.BlockSpec((tm, w), lambda i: (i, 0))
    return pl.pallas_call(
        body, name="fwd_ffn", grid=(n_tiles,),
        in_specs=[row(D_MODEL), row(D_MODEL), _full((1, D_MODEL)), pl.BlockSpec(memory_space=pl.ANY),
                  _full((FFN_CONV_WIDTH, 2 * D_FF)), _full((1, 2 * D_FF)), _full((1, D_MODEL))],
        out_specs=[row(2 * D_FF), row(2 * D_FF), row(D_FF), row(D_MODEL), row(D_MODEL), row(D_MODEL), _full((8, 128)),
                   _full((1, D_MODEL))],
        out_shape=[jax.ShapeDtypeStruct((tokens, 2 * D_FF), BF16), jax.ShapeDtypeStruct((tokens, 2 * D_FF), BF16),
                   jax.ShapeDtypeStruct((tokens, D_FF), BF16),
                   jax.ShapeDtypeStruct((tokens, D_MODEL), BF16), jax.ShapeDtypeStruct((tokens, D_MODEL), F32),
                   jax.ShapeDtypeStruct((tokens, D_MODEL), BF16),
                   jax.ShapeDtypeStruct((8, 128), F32), jax.ShapeDtypeStruct((1, D_MODEL), F32)],
        scratch_shapes=[pltpu.VMEM((2 * D_FF, D_MODEL), BF16), pltpu.VMEM((D_FF, D_MODEL), BF16),
                        pltpu.VMEM((FFN_HALO, 2 * D_FF), F32), pltpu.SemaphoreType.DMA((2,))],
        compiler_params=_params(),
    )(x2, target, g_ffn, gw, ffn_w, ffn_b, g_final)


def _bwd_ffn(dx3, x2, uu_all, cc_all, gw, g_ffn, ffn_w, seq, tm):
    tokens = x2.shape[0]
    n_tiles = tokens // tm
    tps = seq // tm
    n_chunks = D_FF // FFN_CHUNK

    def body(dx3_ref, x2_ref, uu_ref, cc_ref, gffn_ref, gw_hbm, fw_ref,
             dx2_ref, dx2b_ref, duu_ref, dfb_ref, dfw_ref, dg_ref,
             wup_v, wdown_v, carry, sem):
        i = pl.program_id(0)
        t = n_tiles - 1 - i

        _start_weights(gw_hbm, ("w_down", "w_up"), (wdown_v, wup_v), sem)

        @pl.when(i == 0)
        def _():
            dfb_ref[...] = jnp.zeros_like(dfb_ref)
            dfw_ref[...] = jnp.zeros_like(dfw_ref)
            dg_ref[...] = jnp.zeros_like(dg_ref)

        @pl.when(t % tps == tps - 1)
        def _():
            carry[...] = jnp.zeros_like(carry)

        dx3v = dx3_ref[...]
        dx3b = dx3v.astype(BF16)
        dh3 = jnp.zeros((tm, D_MODEL), F32)
        for jc in range(n_chunks):
            da = _dot_nt(dx3b, wdown_v[pl.ds(jc * FFN_CHUNK, FFN_CHUNK), :])
            colss = [pl.ds(half * D_FF + jc * FFN_CHUNK, FFN_CHUNK) for half in range(2)]
            gate, val = [cc_ref[:, cols].astype(F32) for cols in colss]
            sg = _sigmoid(gate)
            dgate = da * val * (sg * (1.0 + gate * (1.0 - sg)))
            dval = da * (gate * sg)
            for dcc, cols in zip((dgate, dval), colss):
                uu = uu_ref[:, cols].astype(F32)
                dfb_ref[:, cols] += _colsum(dcc)
                ext = jnp.concatenate([dcc, carry[:, cols]], axis=0)
                carry[:, cols] = dcc[:FFN_HALO, :]
                n1 = pltpu.roll(ext, tm + FFN_HALO - 1, 0)[:tm, :]
                n2 = pltpu.roll(ext, tm + FFN_HALO - 2, 0)[:tm, :]
                duu = fw_ref[2:3, cols] * dcc + fw_ref[1:2, cols] * n1 + fw_ref[0:1, cols] * n2
                dfw_ref[2:3, cols] += _colsum(uu * dcc)
                dfw_ref[1:2, cols] += _colsum(uu * n1)
                dfw_ref[0:1, cols] += _colsum(uu * n2)
                duub = duu.astype(BF16)
                duu_ref[:, cols] = duub
                dh3 = dh3 + _dot(duub, wup_v[cols, :])
        xh, r = _rms_fwd(x2_ref[...])
        dg_ref[...] += _colsum(dh3 * xh)
        dx2 = dx3v + _rms_bwd(dh3, xh, r, gffn_ref[...])
        dx2_ref[...] = dx2
        dx2b_ref[...] = dx2.astype(BF16)

    rev = lambda w: pl.BlockSpec((tm, w), lambda i: (n_tiles - 1 - i, 0))
    return pl.pallas_call(
        body, name="bwd_ffn", grid=(n_tiles,),
        in_specs=[rev(D_MODEL), rev(D_MODEL), rev(2 * D_FF), rev(2 * D_FF), _full((1, D_MODEL)),
                  pl.BlockSpec(memory_space=pl.ANY), _full((FFN_CONV_WIDTH, 2 * D_FF))],
        out_specs=[rev(D_MODEL), rev(D_MODEL), rev(2 * D_FF), _full((1, 2 * D_FF)), _full((FFN_CONV_WIDTH, 2 * D_FF)),
                   _full((1, D_MODEL))],
        out_shape=[jax.ShapeDtypeStruct((tokens, D_MODEL), F32), jax.ShapeDtypeStruct((tokens, D_MODEL), BF16),
                   jax.ShapeDtypeStruct((tokens, 2 * D_FF), BF16),
                   jax.ShapeDtypeStruct((1, 2 * D_FF), F32), jax.ShapeDtypeStruct((FFN_CONV_WIDTH, 2 * D_FF), F32),
                   jax.ShapeDtypeStruct((1, D_MODEL), F32)],
        scratch_shapes=[pltpu.VMEM((2 * D_FF, D_MODEL), BF16), pltpu.VMEM((D_FF, D_MODEL), BF16),
                        pltpu.VMEM((FFN_HALO, 2 * D_FF), F32), pltpu.SemaphoreType.DMA((2,))],
        compiler_params=_params(),
    )(dx3, x2, uu_all, cc_all, g_ffn, gw, ffn_w)


def _bwd_attn(dx2, x1, q, kv, gw, g_x, after, seq, tm):
    tokens = x1.shape[0]
    n_tiles = tokens // tm
    tps = seq // tm
    n_b = tokens // seq

    def body(dx2_ref, x1_ref, q_ref, kv_ref, g_ref, gw_hbm, after_ref, dx1_ref, dx1b_ref, dq_ref, dkv_ref, dg_ref,
             wq_v, wo_v, sem):
        del after_ref
        i = pl.program_id(0)

        _start_weights(gw_hbm, ("w_o", "w_q"), (wo_v, wq_v), sem)

        @pl.when(i == 0)
        def _():
            dg_ref[...] = jnp.zeros_like(dg_ref)

        @pl.when(i % tps == 0)
        def _():
            dkv_ref[...] = jnp.zeros_like(dkv_ref)

        dx2v = dx2_ref[...]
        do = _dot_nt(dx2v.astype(BF16), wo_v[...]).astype(BF16)
        q = q_ref[...]
        heads = [slice(h * HEAD_DIM, (h + 1) * HEAD_DIM) for h in range(HEADS)]
        kcols = [pl.ds(h * HEAD_DIM, HEAD_DIM) for h in range(HEADS)]
        vcols = [pl.ds(D_MODEL + h * HEAD_DIM, HEAD_DIM) for h in range(HEADS)]
        scores = [_dot_nt(q[:, hd], kv_ref[:, kc]) for hd, kc in zip(heads, kcols)]
        dps = [_dot_nt(do[:, hd], kv_ref[:, vc]) for hd, vc in zip(heads, vcols)]
        probs = [_softmax_rows(s) for s in scores]
        dss = [(p * (dp - jnp.sum(dp * p, axis=-1, keepdims=True))).astype(BF16) for p, dp in zip(probs, dps)]
        for p, hd, vc in zip(probs, heads, vcols):
            dkv_ref[:, vc] += _dot_tn(p.astype(BF16), do[:, hd])
        dqs = [_dot(ds, kv_ref[:, kc]) * (HEAD_DIM ** -0.5) for ds, kc in zip(dss, kcols)]
        for ds, hd, kc in zip(dss, heads, kcols):
            dkv_ref[:, kc] += _dot_tn(ds, q[:, hd])
        dq = jnp.concatenate(dqs, axis=-1).astype(BF16)
        dq_ref[...] = dq
        dh2 = _dot_nt(dq, wq_v[...])
        xh, r = _rms_fwd(x1_ref[...])
        dg_ref[...] += _colsum(dh2 * xh)
        dx1 = dx2v + _rms_bwd(dh2, xh, r, g_ref[...])
        dx1_ref[...] = dx1
        dx1b_ref[...] = dx1.astype(BF16)

    row = lambda w: pl.BlockSpec((tm, w), lambda i: (i, 0))
    per_b = pl.BlockSpec((N_MEM, 2 * D_MODEL), lambda i: (i // tps, 0))
    return pl.pallas_call(
        body, name="bwd_attn", grid=(n_tiles,),
        in_specs=[row(D_MODEL), row(D_MODEL), row(D_MODEL), per_b, _full((1, D_MODEL)), pl.BlockSpec(memory_space=pl.ANY),
                  _full(after.shape)],
        out_specs=[row(D_MODEL), row(D_MODEL), row(D_MODEL), per_b, _full((1, D_MODEL))],
        out_shape=[jax.ShapeDtypeStruct((tokens, D_MODEL), F32), jax.ShapeDtypeStruct((tokens, D_MODEL), BF16),
                   jax.ShapeDtypeStruct((tokens, D_MODEL), BF16),
                   jax.ShapeDtypeStruct((n_b * N_MEM, 2 * D_MODEL), F32), jax.ShapeDtypeStruct((1, D_MODEL), F32)],
        scratch_shapes=[pltpu.VMEM((D_MODEL, D_MODEL), BF16), pltpu.VMEM((D_MODEL, D_MODEL), BF16), pltpu.SemaphoreType.DMA((2,))],
        compiler_params=_params(),
    )(dx2, x1, q, kv, g_x, gw, after)


def _bwd_kv(dkv, mem2d, gw):
    rows = mem2d.shape[0]
    n_b = rows // N_MEM

    def body(dkv_ref, mem_ref, gw_hbm, dkvb_ref, dg_ref, wkv_v, sem):
        @pl.when(pl.program_id(0) == 0)
        def _():
            copies = _load_weight(gw_hbm, "w_kv", wkv_v, sem)
            for cp in copies:
                cp.start()
            for cp in copies:
                cp.wait()
            dg_ref[...] = jnp.zeros_like(dg_ref)

        dkvb = dkv_ref[...].astype(BF16)
        dkvb_ref[...] = dkvb
        dmn = _dot(dkvb, wkv_v[...])
        mh, _ = _rms_fwd(mem_ref[...])
        dg_ref[...] += _colsum(dmn * mh)

    return pl.pallas_call(
        body, name="bwd_kv", grid=(n_b,),
        in_specs=[pl.BlockSpec((N_MEM, 2 * D_MODEL), lambda b: (b, 0)), pl.BlockSpec((N_MEM, D_MODEL), lambda b: (b, 0)),
                  pl.BlockSpec(memory_space=pl.ANY)],
        out_specs=[pl.BlockSpec((N_MEM, 2 * D_MODEL), lambda b: (b, 0)), _full((1, D_MODEL))],
        out_shape=[jax.ShapeDtypeStruct((rows, 2 * D_MODEL), BF16), jax.ShapeDtypeStruct((1, D_MODEL), F32)],
        scratch_shapes=[pltpu.VMEM((2 * D_MODEL, D_MODEL), BF16), pltpu.SemaphoreType.DMA],
        compiler_params=_params(),
    )(dkv, mem2d, gw)


def _bwd_mix(dx1, x2d, u_all, c_all, pooled_all, gw, g_mix, conv_w, ln_g, ln_b, pool_w, pool_scale, after, seq, tm):
    tokens = x2d.shape[0]
    n_tiles = tokens // tm
    tps = seq // tm

    def body(dx1_ref, x_ref, u_ref, c_ref, pooled_ref, gmix_ref, gw_hbm, cw_ref, lng_ref, lnb_ref, pw_ref, ps_ref,
             after_ref, dx_ref, du_ref, dgmix_ref, dcw_ref, dcb_ref, dlng_ref, dlnb_ref, dpw_ref, dps_ref,
             win_v, wout_v, dc_carry, e_carry, sem):
        del after_ref
        i = pl.program_id(0)
        t = n_tiles - 1 - i

        _start_weights(gw_hbm, ("w_out", "w_in"), (wout_v, win_v), sem)

        @pl.when(i == 0)
        def _():
            for ref in (dgmix_ref, dcw_ref, dcb_ref, dlng_ref, dlnb_ref, dpw_ref, dps_ref):
                ref[...] = jnp.zeros_like(ref)

        @pl.when(t % tps == tps - 1)
        def _():
            dc_carry[...] = jnp.zeros_like(dc_carry)
            e_carry[...] = jnp.zeros_like(e_carry)

        dx1v = dx1_ref[...]
        dymix = _dot_nt(dx1v.astype(BF16), wout_v[...])
        dyc, dyp = dymix[:, :D_CONV], dymix[:, D_CONV:]
        u = u_ref[...]
        val, gate = u[:, :D_CONV], u[:, D_CONV:2 * D_CONV]

        conv = c_ref[...]
        mu = jnp.mean(conv, axis=-1, keepdims=True)
        cen = conv - mu
        rs = lax.rsqrt(jnp.mean(cen * cen, axis=-1, keepdims=True) + EPS)
        chat = cen * rs
        ln = chat * lng_ref[...] + lnb_ref[...]
        sl = _sigmoid(ln)
        dln = dyc * (sl * (1.0 + ln * (1.0 - sl)))
        dlng_ref[...] += _colsum(dln * chat)
        dlnb_ref[...] += _colsum(dln)
        dchat = dln * lng_ref[...]
        dc = rs * (dchat - jnp.mean(dchat, axis=-1, keepdims=True)
                   - chat * jnp.mean(dchat * chat, axis=-1, keepdims=True))
        dcb_ref[...] += _colsum(dc)
        sg = _sigmoid(gate)
        hc = val * sg
        ext = jnp.concatenate([dc, dc_carry[...]], axis=0)
        dc_carry[...] = dc[:CONV_HALO, :]
        dhc = jnp.zeros((tm, D_CONV), F32)
        ahead_by = _sublane_shifts(ext)
        for k in range(CONV_WIDTH):
            whole, part = divmod(CONV_WIDTH - 1 - k, 8)
            tap = ahead_by[part][8 * whole:8 * whole + tm, :]
            dhc = dhc + cw_ref[k:k + 1, :] * tap
            dcw_ref[k:k + 1, :] += _colsum_mxu(hc * tap)
        du_ref[:, :D_CONV] = (dhc * sg).astype(BF16)
        du_ref[:, D_CONV:2 * D_CONV] = (dhc * val * (sg * (1.0 - sg))).astype(BF16)

        pos = lax.broadcasted_iota(jnp.int32, (tm, 1), 0) + (t % tps) * tm
        es, dpooled = [], []
        for g, w in enumerate(POOL_WINDOWS):
            cols = pl.ds(g * POOL_GROUP_DIM, POOL_GROUP_DIM)
            lo = g * POOL_GROUP_DIM
            pooled = pooled_ref[:, cols]
            pw = pw_ref[g].astype(BF16)
            dyg = dyp[:, lo:lo + POOL_GROUP_DIM]
            dps_ref[:, cols] += _colsum(dyg * _dot(pooled, pw))
            dmixed = (dyg * ps_ref[:, cols]).astype(BF16)
            dpw_ref[g] += _dot_tn(pooled, dmixed)
            dpo = _dot_nt(dmixed, pw)
            dpooled.append(dpo)
            es.append(dpo / jnp.minimum(pos + 1, w).astype(F32))
        e = jnp.concatenate(es, axis=-1)
        run = jnp.concatenate([e, e_carry[...]], axis=0)
        e_carry[...] = e[:POOL_HALO, :]
        rows = tm + POOL_HALO
        for g, w in enumerate(POOL_WINDOWS):
            lo = g * POOL_GROUP_DIM
            run = run[:, POOL_GROUP_DIM if g else 0:]
            run = run + pltpu.roll(run, rows - w // 2, 0)
            du_ref[:, 2 * D_CONV + lo:2 * D_CONV + lo + POOL_GROUP_DIM] = (
                run[:tm, :POOL_GROUP_DIM] - dpooled[g]).astype(BF16)

        dh1 = _dot(du_ref[...], win_v[...])
        xh, r = _rms_fwd(x_ref[...])
        dgmix_ref[...] += _colsum(dh1 * xh)
        dx_ref[...] = dx1v + _rms_bwd(dh1, xh, r, gmix_ref[...])

    rev = lambda w: pl.BlockSpec((tm, w), lambda i: (n_tiles - 1 - i, 0))
    return pl.pallas_call(
        body, name="bwd_mix", grid=(n_tiles,),
        in_specs=[rev(D_MODEL), rev(D_MODEL), rev(D_IN), rev(D_CONV), rev(D_POOL), _full((1, D_MODEL)),
                  pl.BlockSpec(memory_space=pl.ANY), _full((CONV_WIDTH, D_CONV)), _full((1, D_CONV)), _full((1, D_CONV)),
                  _full((4, POOL_GROUP_DIM, POOL_GROUP_DIM)), _full((1, D_POOL)), _full(after.shape)],
        out_specs=[rev(D_MODEL), rev(D_IN), _full((1, D_MODEL)), _full((CONV_WIDTH, D_CONV)), _full((1, D_CONV)),
                   _full((1, D_CONV)), _full((1, D_CONV)), _full((4, POOL_GROUP_DIM, POOL_GROUP_DIM)), _full((1, D_POOL))],
        out_shape=[jax.ShapeDtypeStruct((tokens, D_MODEL), F32), jax.ShapeDtypeStruct((tokens, D_IN), BF16),
                   jax.ShapeDtypeStruct((1, D_MODEL), F32), jax.ShapeDtypeStruct((CONV_WIDTH, D_CONV), F32),
                   jax.ShapeDtypeStruct((1, D_CONV), F32), jax.ShapeDtypeStruct((1, D_CONV), F32),
                   jax.ShapeDtypeStruct((1, D_CONV), F32),
                   jax.ShapeDtypeStruct((4, POOL_GROUP_DIM, POOL_GROUP_DIM), F32), jax.ShapeDtypeStruct((1, D_POOL), F32)],
        scratch_shapes=[pltpu.VMEM((D_IN, D_MODEL), BF16), pltpu.VMEM((D_MODEL, D_MODEL), BF16),
                        pltpu.VMEM((CONV_HALO, D_CONV), F32), pltpu.VMEM((POOL_HALO, D_POOL), F32),
                        pltpu.SemaphoreType.DMA((2,))],
        compiler_params=_params(),
    )(dx1, x2d, u_all, c_all, pooled_all, g_mix, gw, conv_w, ln_g, ln_b, pool_w, pool_scale, after)


def _wgrad(a, b, name, after=None):
    tokens, m = a.shape
    n = b.shape[1]
    tm = 512 if m % 512 == 0 else 256
    extra = [] if after is None else [after]

    def body(a_ref, b_ref, *rest):
        rest[-1][...] = _dot_tn(a_ref[...], b_ref[...]).astype(rest[-1].dtype)

    return pl.pallas_call(
        body, name=name, grid=(m // tm,),
        in_specs=[pl.BlockSpec((tokens, tm), lambda i: (0, i)), _full((tokens, n))] + [_full(t.shape) for t in extra],
        out_specs=pl.BlockSpec((tm, n), lambda i: (i, 0)),
        out_shape=jax.ShapeDtypeStruct((m, n), BF16),
        compiler_params=_params(),
    )(a, b, *extra)


def _adamw_update(w, g, m, v):
    nm = ADAM_B1 * m + (1.0 - ADAM_B1) * g
    nv = ADAM_B2 * v + (1.0 - ADAM_B2) * (g * g)
    m_hat = nm / (1.0 - ADAM_B1 ** ADAM_STEP)
    v_hat = nv / (1.0 - ADAM_B2 ** ADAM_STEP)
    return -ADAM_LR * (m_hat / (jnp.sqrt(v_hat) + ADAM_EPS) + ADAM_WD * w), nm, nv


def _adamw_small(ws, gs, ms, vs):
    n = len(ws)

    def body(*refs):
        ins, outs = refs[:4 * n], refs[4 * n:]
        for k in range(n):
            d, nm, nv = _adamw_update(*[ins[j * n + k][...] for j in range(4)])
            outs[k][...] = d
            outs[n + k][...] = nm
            outs[2 * n + k][...] = nv

    vmem = pl.BlockSpec(memory_space=pltpu.VMEM)
    outs = pl.pallas_call(
        body, name="adamw_small",
        in_specs=[vmem] * (4 * n), out_specs=[vmem] * (3 * n),
        out_shape=[jax.ShapeDtypeStruct(w.shape, F32) for w in ws] * 3,
    )(*ws, *gs, *ms, *vs)
    return outs[:n], outs[n:2 * n], outs[2 * n:]


SMALL = (("norm_mix_g", (1, 1024)), ("conv_dw_b", (1, 512)), ("conv_ln_g", (1, 512)), ("conv_ln_b", (1, 512)),
         ("pool_w", (1, 4, 128, 128)), ("pool_scale", (1, 512)), ("norm_xattn_g", (1, 1024)), ("norm_mem_g", (1, 1024)),
         ("norm_ffn_g", (1, 1024)), ("ffn_dw_b", (1, 5632)), ("norm_final_g", (1024,)))
LANES = 128


def _pack_rows(arrs):
    flat = jnp.concatenate([a.reshape(-1) for a in arrs])
    pad = (-flat.shape[0]) % (8 * LANES)
    return jnp.pad(flat, (0, pad)).reshape(-1, LANES)


def kernel(x, mem, norm_mix_g, w_in, conv_dw_w, conv_dw_b, conv_ln_g, conv_ln_b, pool_w, pool_scale, w_out, norm_xattn_g, norm_mem_g, w_q, w_kv, w_o, norm_ffn_g, w_up, ffn_dw_w, ffn_dw_b, w_down, norm_final_g, loss_target, m_norm_mix_g, m_w_in, m_conv_dw_w, m_conv_dw_b, m_conv_ln_g, m_conv_ln_b, m_pool_w, m_pool_scale, m_w_out, m_norm_xattn_g, m_norm_mem_g, m_w_q, m_w_kv, m_w_o, m_norm_ffn_g, m_w_up, m_ffn_dw_w, m_ffn_dw_b, m_w_down, m_norm_final_g, v_norm_mix_g, v_w_in, v_conv_dw_w, v_conv_dw_b, v_conv_ln_g, v_conv_ln_b, v_pool_w, v_pool_scale, v_w_out, v_norm_xattn_g, v_norm_mem_g, v_w_q, v_w_kv, v_w_o, v_norm_ffn_g, v_w_up, v_ffn_dw_w, v_ffn_dw_b, v_w_down, v_norm_final_g):
    weights = dict(norm_mix_g=norm_mix_g, w_in=w_in, conv_dw_w=conv_dw_w, conv_dw_b=conv_dw_b, conv_ln_g=conv_ln_g,
                   conv_ln_b=conv_ln_b, pool_w=pool_w, pool_scale=pool_scale, w_out=w_out, norm_xattn_g=norm_xattn_g,
                   norm_mem_g=norm_mem_g, w_q=w_q, w_kv=w_kv, w_o=w_o, norm_ffn_g=norm_ffn_g, w_up=w_up,
                   ffn_dw_w=ffn_dw_w, ffn_dw_b=ffn_dw_b, w_down=w_down, norm_final_g=norm_final_g)
    moments_m = dict(norm_mix_g=m_norm_mix_g, w_in=m_w_in, conv_dw_w=m_conv_dw_w, conv_dw_b=m_conv_dw_b,
                     conv_ln_g=m_conv_ln_g, conv_ln_b=m_conv_ln_b, pool_w=m_pool_w, pool_scale=m_pool_scale,
                     w_out=m_w_out, norm_xattn_g=m_norm_xattn_g, norm_mem_g=m_norm_mem_g, w_q=m_w_q, w_kv=m_w_kv,
                     w_o=m_w_o, norm_ffn_g=m_norm_ffn_g, w_up=m_w_up, ffn_dw_w=m_ffn_dw_w, ffn_dw_b=m_ffn_dw_b,
                     w_down=m_w_down, norm_final_g=m_norm_final_g)
    moments_v = dict(norm_mix_g=v_norm_mix_g, w_in=v_w_in, conv_dw_w=v_conv_dw_w, conv_dw_b=v_conv_dw_b,
                     conv_ln_g=v_conv_ln_g, conv_ln_b=v_conv_ln_b, pool_w=v_pool_w, pool_scale=v_pool_scale,
                     w_out=v_w_out, norm_xattn_g=v_norm_xattn_g, norm_mem_g=v_norm_mem_g, w_q=v_w_q, w_kv=v_w_kv,
                     w_o=v_w_o, norm_ffn_g=v_norm_ffn_g, w_up=v_w_up, ffn_dw_w=v_ffn_dw_w, ffn_dw_b=v_ffn_dw_b,
                     w_down=v_w_down, norm_final_g=v_norm_final_g)
    order = list(weights)
    transposed = ("w_in", "w_kv", "w_up")

    n_b, seq, _ = x.shape
    tokens = n_b * seq
    tm_mix = min(512, seq // 2)
    tm_attn = min(1024, seq // 2)
    tm_ffn = min(256, seq // 2)
    dev = 4 * lax.axis_index("x") + 2 * lax.axis_index("y") + lax.axis_index("c")

    packs = [jnp.concatenate([weights[n][0].T if n in transposed else weights[n][0] for n in names], axis=0).astype(BF16)
             for names in AG_GROUPS]
    small_sharded = _pack_rows([conv_dw_w[0], ffn_dw_w[0]])
    gw_mix, gsmall = _all_gather([packs[0], small_sharded], "weights_all_gather")
    flights = []
    after = gw_mix
    for k in (1, 2):
        own_in_place = lax.dynamic_update_slice(lax.empty((N_DEV,) + packs[k].shape, BF16), packs[k][None], (dev, 0, 0))
        flights.append(_gather_start(own_in_place, after, "weights_gather_start_%d" % k, BARRIER_IDS["gather_start"][k - 1]))
        after = flights[-1][3]
    gflat = gsmall.reshape(N_DEV, -1)
    n_cw = CONV_WIDTH * (D_CONV // N_DEV)
    n_fw = FFN_CONV_WIDTH * (2 * D_FF // N_DEV)
    conv_w = gflat[:, :n_cw].reshape(N_DEV, CONV_WIDTH, D_CONV // N_DEV).transpose(1, 0, 2).reshape(CONV_WIDTH, D_CONV)
    ffn_w = gflat[:, n_cw:n_cw + n_fw].reshape(N_DEV, FFN_CONV_WIDTH, 2 * D_FF // N_DEV).transpose(1, 0, 2).reshape(
        FFN_CONV_WIDTH, 2 * D_FF)

    x2d = x.reshape(tokens, D_MODEL)
    mem2d = mem.reshape(n_b * N_MEM, D_MODEL)
    tgt2d = loss_target.reshape(tokens, D_MODEL)
    g_final = norm_final_g.reshape(1, D_MODEL)

    def gather_finish(flight, after, tag):
        fwd_send, fwd_recv, buf = _gather_forward(*flight[:3], after, "weights_gather_forward_" + tag,
                                                  BARRIER_IDS["gather_forward"][int(tag) - 1])
        return _gather_finish(fwd_send, fwd_recv, buf, "weights_gather_finish_" + tag)

    x1, u_all, c_all, pooled_all, ymix, h1 = _fwd_mix(
        x2d, gw_mix, norm_mix_g, conv_w, conv_dw_b, conv_ln_g, conv_ln_b, pool_w[0], pool_scale, flights[1][3],
        seq, tm_mix)
    gw_attn = gather_finish(flights[0], x1, "1")
    mem_n, kv = _fwd_kv(mem2d, gw_attn, norm_mem_g)
    x2, h2, q, o = _fwd_attn(x1, kv, gw_attn, norm_xattn_g, seq, tm_attn)
    gw_ffn = gather_finish(flights[1], x2, "2")
    uu_all, cc_all, a_all, h3, dx3, dx3b, loss_part, dg_final = _fwd_ffn(
        x2, tgt2d, gw_ffn, norm_ffn_g, ffn_w, ffn_dw_b, g_final, seq, tm_ffn)

    table = _owner_table()

    def sibling_start(names, tag):
        parts = [part[n].reshape(N_DEV, W_OFF[n][1], D_MODEL) for n in names]
        return _exchange_start(parts, 4, _to_sibling, "rs_sibling_exchange_start_" + tag, BARRIER_IDS["sibling"][tag])

    def chips_start(flight, after, tag):
        parts, landed = _exchange_wait(*flight[:4], after, 4, _to_sibling, "rs_sibling_exchange_wait_" + tag)
        sums = _chip_partial_sums(table, parts, landed, "rs_chip_partial_sums_" + tag)
        return parts, landed, _exchange_start(sums, 3, _to_chip, "rs_chip_exchange_start_" + tag,
                                              BARRIER_IDS["chips"][tag])

    grads, delta, new_m, new_v = {}, {}, {}, {}

    def reduce_finish(names, parts, landed, flight, after, tag):
        _, from_chips = _exchange_wait(*flight[:4], after, 3, _to_chip, "rs_chip_exchange_wait_" + tag)
        as_rows = {n: n in transposed and W_OFF[n][1] % LANES != 0 for n in names}
        states = [tuple(t[n][0].T if as_rows[n] else t[n][0] for t in (weights, moments_m, moments_v)) for n in names]
        results = _final_update(table, parts, landed, from_chips, states, "rs_final_update_" + tag)
        for n, res in zip(names, results):
            grads[n], delta[n], new_m[n], new_v[n] = [t.T[None] if as_rows[n] else t[None] for t in res]
        return delta[names[-1]]

    part = {}
    dx2, dx2b, duu, d_ffn_b, d_ffn_w, dg_ffn = _bwd_ffn(dx3, x2, uu_all, cc_all, gw_ffn, norm_ffn_g, ffn_w, seq, tm_ffn)
    part["w_up"] = _wgrad(duu, h3, "wgrad_w_up")
    part["w_down"] = _wgrad(a_all, dx3b, "wgrad_w_down")
    to_sibling_a = sibling_start(RS_GROUPS["a"], "a")
    part["w_o"] = _wgrad(o, dx2b, "wgrad_w_o", after=to_sibling_a[4])
    dx1, dx1b, dq, dkv, dg_x = _bwd_attn(dx2, x1, q, kv, gw_attn, norm_xattn_g, part["w_o"][:8, :128], seq, tm_mix)
    dkv_b, dg_mem = _bwd_kv(dkv, mem2d, gw_attn)
    part["w_q"] = _wgrad(h2, dq, "wgrad_w_q")
    part["w_kv"] = _wgrad(dkv_b, mem_n, "wgrad_w_kv")
    part["w_out"] = _wgrad(ymix, dx1b, "wgrad_w_out")
    to_sibling_b = sibling_start(RS_GROUPS["b"], "b")
    parts_a, landed_a, flight_a = chips_start(to_sibling_a, to_sibling_b[4], "a")
    parts_b, landed_b, flight_b = chips_start(to_sibling_b, flight_a[4], "b")
    dx, du, dg_mix, d_conv_w, d_conv_b, d_ln_g, d_ln_b, d_pool_w, d_pool_scale = _bwd_mix(
        dx1, x2d, u_all, c_all, pooled_all, gw_mix, norm_mix_g, conv_w, conv_ln_g, conv_ln_b, pool_w[0], pool_scale,
        flight_b[4], seq, tm_mix)
    grad_x = dx.reshape(x.shape)

    small_grads = dict(norm_mix_g=dg_mix, conv_dw_b=d_conv_b, conv_ln_g=d_ln_g, conv_ln_b=d_ln_b, pool_w=d_pool_w,
                       pool_scale=d_pool_scale, norm_xattn_g=dg_x, norm_mem_g=dg_mem, norm_ffn_g=dg_ffn,
                       ffn_dw_b=d_ffn_b, norm_final_g=dg_final)
    small_list = [small_grads[n] for n, _ in SMALL] + [d_conv_w, d_ffn_w, loss_part[:1]]
    small_mine = _pack_rows(small_list)
    small_flight = _broadcast_start(
        lax.dynamic_update_slice(lax.empty((N_DEV,) + small_mine.shape, F32), small_mine[None], (dev, 0, 0)),
        "small_grads_broadcast_start", BARRIER_IDS["broadcast"])

    part["w_in"] = _wgrad(du, h1, "wgrad_w_in", after=small_flight[3])
    to_sibling_c = sibling_start(RS_GROUPS["c"], "c")
    parts_c, landed_c, flight_c = chips_start(to_sibling_c, to_sibling_c[4], "c")
    updated_a = reduce_finish(RS_GROUPS["a"], parts_a, landed_a, flight_a, flight_c[4], "a")
    updated_b = reduce_finish(RS_GROUPS["b"], parts_b, landed_b, flight_b, updated_a, "b")
    small_all = _broadcast_wait(*small_flight[:3], updated_b, "small_grads_broadcast_wait")
    small_sum = _sum_blocks(small_all).reshape(-1)

    pos = 0
    for n, shape in SMALL:
        size = 1
        for s in shape:
            size *= s
        grads[n] = small_sum[pos:pos + size].reshape(shape)
        pos += size
    full_conv_w = small_sum[pos:pos + CONV_WIDTH * D_CONV].reshape(CONV_WIDTH, D_CONV)
    pos += CONV_WIDTH * D_CONV
    full_ffn_w = small_sum[pos:pos + FFN_CONV_WIDTH * 2 * D_FF].reshape(FFN_CONV_WIDTH, 2 * D_FF)
    loss = small_sum[pos + FFN_CONV_WIDTH * 2 * D_FF]
    grads["conv_dw_w"] = lax.dynamic_slice_in_dim(full_conv_w, dev * (D_CONV // N_DEV), D_CONV // N_DEV, axis=1)[None]
    grads["ffn_dw_w"] = lax.dynamic_slice_in_dim(full_ffn_w, dev * (2 * D_FF // N_DEV), 2 * D_FF // N_DEV, axis=1)[None]

    small_names = [n for n in order if n not in W_OFF]
    swap = lambda t: jnp.transpose(t, (1, 0, 2))
    two_d = lambda t: t.reshape(1, -1) if t.ndim == 1 else (swap(t) if t.ndim == 3 else t)
    outs = _adamw_small(*[[two_d(t[n]) for n in small_names] for t in (weights, grads, moments_m, moments_v)])
    for res, out in zip((delta, new_m, new_v), outs):
        for n, o in zip(small_names, out):
            res[n] = swap(o) if o.ndim == 3 else o.reshape(weights[n].shape)

    reduce_finish(RS_GROUPS["c"], parts_c, landed_c, flight_c, delta[small_names[-1]], "c")

    return (loss, grad_x, *[grads[n] for n in order], *[delta[n] for n in order],
            *[new_m[n] for n in order], *[new_v[n] for n in order])
```

```python
import jax
import jax.numpy as jnp
from jax import lax
from jax.experimental import pallas as pl
from jax.experimental.pallas import tpu as pltpu

F32 = jnp.float32
BF16 = jnp.bfloat16
MESH = pl.DeviceIdType.MESH

N_DEV = 8
D_MODEL = 1024
D_CONV = 512
D_POOL = 512
CONV_WIDTH = 31
POOL_WINDOWS = (2, 4, 8, 16)
POOL_GROUP_DIM = 128
D_IN = 1536
N_MEM = 256
HEADS = 4
HEAD_DIM = 256
D_FF = 2816
FFN_CONV_WIDTH = 3
EPS = 1e-6
ADAM_LR = 0.001
ADAM_B1 = 0.9
ADAM_B2 = 0.999
ADAM_EPS = 1e-08
ADAM_WD = 0.01
ADAM_STEP = 10

VMEM_LIMIT_V7X = 56 * 1024 * 1024
CONV_HALO = 32
POOL_HALO = 16
FFN_HALO = 8
FFN_CHUNK = 2816

W_ROWS = (("w_in", 192), ("w_out", 128), ("w_q", 128), ("w_kv", 256), ("w_o", 128), ("w_up", 704), ("w_down", 352))
AG_GROUPS = (("w_in", "w_out"), ("w_q", "w_kv", "w_o"), ("w_up", "w_down"))
W_OFF = {}
for _names in AG_GROUPS:
    _o = 0
    for _n in _names:
        W_OFF[_n] = (_o, dict(W_ROWS)[_n])
        _o += dict(W_ROWS)[_n]
RS_GROUPS = {"a": ("w_up", "w_down"), "b": ("w_q", "w_kv", "w_out"), "c": ("w_o", "w_in")}
BARRIER_IDS = {"gather_start": (0, 1), "gather_forward": (2, 3), "sibling": {"a": 4, "b": 5, "c": 6},
               "chips": {"a": 7, "b": 8, "c": 9}, "broadcast": 10}


def _dot(a, b):
    return jnp.dot(a, b, preferred_element_type=F32)


def _dot_nt(a, b):
    return lax.dot_general(a, b, (((1,), (1,)), ((), ())), preferred_element_type=F32)


def _dot_tn(a, b):
    return lax.dot_general(a, b, (((0,), (0,)), ((), ())), preferred_element_type=F32)


def _sigmoid(v):
    return 1.0 / (1.0 + jnp.exp(-v))


def _rms_fwd(v):
    r = lax.rsqrt(jnp.mean(v * v, axis=-1, keepdims=True) + EPS)
    return v * r, r


def _rms_bwd(dh, vh, r, g):
    gd = dh * g
    return r * (gd - vh * jnp.mean(gd * vh, axis=-1, keepdims=True))


def _sublane_shifts(v):
    rows = v.shape[0]
    return [v] + [pltpu.roll(v, rows - b, 0) for b in range(1, 8)]


def _colsum(v):
    return jnp.sum(v, axis=0, keepdims=True)


def _colsum_mxu(v):
    return _dot(jnp.ones((8, v.shape[0]), BF16), v.astype(BF16))[0:1, :]


def _full(shape):
    return pl.BlockSpec(shape, lambda *_: (0,) * len(shape))


def _params(sem=("arbitrary",), vmem=VMEM_LIMIT_V7X):
    return pltpu.CompilerParams(dimension_semantics=sem, vmem_limit_bytes=vmem)


def _load_weight(g_hbm, name, dst, sem):
    off, rows = W_OFF[name]
    return [pltpu.make_async_copy(g_hbm.at[d, pl.ds(off, rows), :], dst.at[pl.ds(d * rows, rows), :], sem)
            for d in range(N_DEV)]


def _start_weights(g_hbm, names, dsts, sems):
    @pl.when(pl.program_id(0) == 0)
    def _():
        copies = [_load_weight(g_hbm, name, dst, sems.at[k]) for k, (name, dst) in enumerate(zip(names, dsts))]
        for cp in sum(copies, []):
            cp.start()
        for cp in sum(copies, []):
            cp.wait()


def _position():
    x, y, c = lax.axis_index("x"), lax.axis_index("y"), lax.axis_index("c")
    chips = [(1 - x, y), (x, 1 - y), (1 - x, 1 - y)]
    return x, y, c, chips


def _dev(px, py, pc):
    return 4 * px + 2 * py + pc


def _all_gather(arrs, name):
    n = len(arrs)

    def body(*refs):
        ins, outs = refs[:n], refs[n:2 * n]
        send_sems, recv_sems, local_sems = refs[2 * n:2 * n + 3]
        bounce = refs[2 * n + 3:]
        x, y, c, chips = _position()
        me, sibling = (x, y, c), (x, y, 1 - c)

        def copy(a, k, block, to, src=None):
            rows = outs[a].at[_dev(*block)]
            return pltpu.make_async_remote_copy(
                src_ref=rows if src is None else src, dst_ref=rows,
                send_sem=send_sems.at[a, k], recv_sem=recv_sems.at[a, k], device_id=to, device_id_type=MESH)

        sends = []
        for a in range(n):
            first = [copy(a, 0, me, sibling, src=ins[a])]
            first += [copy(a, 1 + j, me, (*chip, c), src=ins[a]) for j, chip in enumerate(chips)]
            for cp in first:
                cp.start()
            sends += first
        started = []
        for a in range(n):
            load = pltpu.make_async_copy(ins[a], bounce[a], local_sems.at[a, 0])
            load.start()
            load.wait()
            mine = pltpu.make_async_copy(bounce[a], outs[a].at[_dev(*me)], local_sems.at[a, 1])
            mine.start()
            started.append(mine)
        for j, chip in enumerate(chips):
            for a in range(n):
                copy(a, 1 + j, (*chip, c), me).wait_recv()
                passed = copy(a, 4 + j, (*chip, c), sibling)
                passed.start()
                sends.append(passed)
        for a in range(n):
            copy(a, 0, sibling, me).wait_recv()
            for j, chip in enumerate(chips):
                copy(a, 4 + j, (*chip, 1 - c), me).wait_recv()
        for cp in sends:
            cp.wait_send()
        for mine in started:
            mine.wait()

    any_spec = pl.BlockSpec(memory_space=pl.ANY)
    return pl.pallas_call(
        body, name=name,
        out_shape=[jax.ShapeDtypeStruct((N_DEV,) + a.shape, a.dtype) for a in arrs],
        in_specs=[any_spec] * n, out_specs=[any_spec] * n,
        scratch_shapes=[pltpu.SemaphoreType.DMA((n, 7)), pltpu.SemaphoreType.DMA((n, 7)), pltpu.SemaphoreType.DMA((n, 2))]
        + [pltpu.VMEM(a.shape, a.dtype) for a in arrs],
    )(*arrs)


_HBM = pl.BlockSpec(memory_space=pltpu.HBM)
_SEM = pl.BlockSpec(memory_space=pltpu.SEMAPHORE)
_SIDE_EFFECT = pltpu.SideEffectType.DATAFLOW_SIDE_EFFECTING


def _handshake(peers):
    barrier = pltpu.get_barrier_semaphore()
    for peer in peers:
        pl.semaphore_signal(barrier, inc=1, device_id=peer, device_id_type=MESH)
    pl.semaphore_wait(barrier, len(peers))


def _gather_start(buf, after, name, collective_id):
    def body(buf_ref, after_ref, send_sems, recv_sems, buf_thru, token):
        del after_ref, buf_thru
        x, y, c, chips = _position()
        rows = buf_ref.at[_dev(x, y, c)]
        targets = [(x, y, 1 - c)] + [(*chip, c) for chip in chips]
        _handshake(targets)
        for k, to in enumerate(targets):
            pltpu.make_async_remote_copy(src_ref=rows, dst_ref=rows, send_sem=send_sems.at[k], recv_sem=recv_sems.at[k],
                                         device_id=to, device_id_type=MESH).start()
        token[...] = jnp.zeros_like(token)

    return pl.pallas_call(
        body, name=name,
        out_shape=(pltpu.SemaphoreType.DMA((4,)), pltpu.SemaphoreType.DMA((4,)), pltpu.HBM(buf.shape, buf.dtype),
                   jax.ShapeDtypeStruct((8, 128), F32)),
        in_specs=(_HBM, pl.BlockSpec(memory_space=pl.ANY)),
        out_specs=(_SEM, _SEM, _HBM, pl.BlockSpec(memory_space=pltpu.VMEM)),
        input_output_aliases={0: 2},
        compiler_params=pltpu.CompilerParams(has_side_effects=_SIDE_EFFECT, collective_id=collective_id),
    )(pltpu.with_memory_space_constraint(buf, pltpu.HBM), after)


def _gather_forward(send_sems, recv_sems, buf, after, name, collective_id):
    def body(buf_ref, send_sems, recv_sems, after_ref, fwd_send, fwd_recv, buf_thru):
        del after_ref, buf_thru
        x, y, c, chips = _position()
        sibling = (x, y, 1 - c)

        def copy(block, k, sends, recvs):
            rows = buf_ref.at[_dev(*block)]
            return pltpu.make_async_remote_copy(src_ref=rows, dst_ref=rows, send_sem=sends.at[k], recv_sem=recvs.at[k],
                                                device_id=sibling, device_id_type=MESH)

        _handshake([sibling])
        for k in range(4):
            copy((x, y, c), k, send_sems, recv_sems).wait_send()
        copy(sibling, 0, send_sems, recv_sems).wait_recv()
        for j, chip in enumerate(chips):
            copy((*chip, c), 1 + j, send_sems, recv_sems).wait_recv()
            copy((*chip, c), j, fwd_send, fwd_recv).start()

    return pl.pallas_call(
        body, name=name,
        out_shape=(pltpu.SemaphoreType.DMA((3,)), pltpu.SemaphoreType.DMA((3,)), pltpu.HBM(buf.shape, buf.dtype)),
        in_specs=(_HBM, _SEM, _SEM, pl.BlockSpec(memory_space=pl.ANY)), out_specs=(_SEM, _SEM, _HBM),
        input_output_aliases={0: 2},
        compiler_params=pltpu.CompilerParams(has_side_effects=_SIDE_EFFECT, collective_id=collective_id),
    )(buf, send_sems, recv_sems, after)


def _gather_finish(fwd_send, fwd_recv, buf, name):
    def body(buf_ref, fwd_send, fwd_recv, buf_thru):
        del buf_thru
        x, y, c, chips = _position()
        for j, chip in enumerate(chips):
            cp = pltpu.make_async_remote_copy(
                src_ref=buf_ref.at[_dev(*chip, c)], dst_ref=buf_ref.at[_dev(*chip, 1 - c)], send_sem=fwd_send.at[j],
                recv_sem=fwd_recv.at[j], device_id=(x, y, 1 - c), device_id_type=MESH)
            cp.wait_send()
            cp.wait_recv()

    return pl.pallas_call(
        body, name=name,
        out_shape=pltpu.HBM(buf.shape, buf.dtype),
        in_specs=(_HBM, _SEM, _SEM), out_specs=_HBM,
        input_output_aliases={0: 0},
        compiler_params=pltpu.CompilerParams(has_side_effects=_SIDE_EFFECT),
    )(buf, fwd_send, fwd_recv)


def _everyone_else(x, y, c, chips):
    return [(x, y, 1 - c)] + [(*chip, core) for chip in chips for core in (c, 1 - c)]


def _broadcast_start(buf, name, collective_id):
    def body(buf_ref, send_sems, recv_sems, buf_thru, token):
        del buf_thru
        x, y, c, chips = _position()
        rows = buf_ref.at[_dev(x, y, c)]
        _handshake(_everyone_else(x, y, c, chips))
        for k, to in enumerate(_everyone_else(x, y, c, chips)):
            pltpu.make_async_remote_copy(src_ref=rows, dst_ref=rows, send_sem=send_sems.at[k], recv_sem=recv_sems.at[k],
                                         device_id=to, device_id_type=MESH).start()
        token[...] = jnp.zeros_like(token)

    return pl.pallas_call(
        body, name=name,
        out_shape=(pltpu.SemaphoreType.DMA((7,)), pltpu.SemaphoreType.DMA((7,)), pltpu.HBM(buf.shape, buf.dtype),
                   jax.ShapeDtypeStruct((8, 128), F32)),
        in_specs=(_HBM,), out_specs=(_SEM, _SEM, _HBM, pl.BlockSpec(memory_space=pltpu.VMEM)),
        input_output_aliases={0: 2},
        compiler_params=pltpu.CompilerParams(has_side_effects=_SIDE_EFFECT, collective_id=collective_id),
    )(pltpu.with_memory_space_constraint(buf, pltpu.HBM))


def _broadcast_wait(send_sems, recv_sems, buf, after, name):
    def body(buf_ref, send_sems, recv_sems, after_ref, buf_thru):
        del after_ref, buf_thru
        x, y, c, chips = _position()
        for k, peer in enumerate(_everyone_else(x, y, c, chips)):
            cp = pltpu.make_async_remote_copy(
                src_ref=buf_ref.at[_dev(x, y, c)], dst_ref=buf_ref.at[_dev(*peer)], send_sem=send_sems.at[k],
                recv_sem=recv_sems.at[k], device_id=peer, device_id_type=MESH)
            cp.wait_send()
            cp.wait_recv()

    return pl.pallas_call(
        body, name=name,
        out_shape=pltpu.HBM(buf.shape, buf.dtype),
        in_specs=(_HBM, _SEM, _SEM, pl.BlockSpec(memory_space=pl.ANY)), out_specs=_HBM,
        input_output_aliases={0: 0},
        compiler_params=pltpu.CompilerParams(has_side_effects=_SIDE_EFFECT),
    )(buf, send_sems, recv_sems, after)


def _to_sibling(j, x, y, c, chips):
    return _dev(*([(x, y)] + chips)[j], 1 - c), (x, y, 1 - c)


def _to_chip(j, x, y, c, chips):
    return j, (*chips[j], c)


def _exchange_start(srcs, n_slots, route, name, collective_id):
    n = len(srcs)

    def body(*refs):
        s_refs, land_refs = refs[:n], refs[n:2 * n]
        send_sems, recv_sems = refs[2 * n:2 * n + 2]
        token = refs[-1]
        x, y, c, chips = _position()
        _handshake([(x, y, 1 - c)] if route is _to_sibling else [route(j, x, y, c, chips)[1] for j in range(n_slots)])
        for k in range(n):
            for j in range(n_slots):
                block, to = route(j, x, y, c, chips)
                pltpu.make_async_remote_copy(
                    src_ref=s_refs[k].at[block], dst_ref=land_refs[k].at[j], send_sem=send_sems.at[n_slots * k + j],
                    recv_sem=recv_sems.at[n_slots * k + j], device_id=to, device_id_type=MESH).start()
        token[...] = jnp.zeros_like(token)

    lands = [jax.ShapeDtypeStruct((n_slots,) + s.shape[1:], s.dtype) for s in srcs]
    outs = pl.pallas_call(
        body, name=name,
        out_shape=(pltpu.SemaphoreType.DMA((n_slots * n,)), pltpu.SemaphoreType.DMA((n_slots * n,)),
                   *[pltpu.HBM(s.shape, s.dtype) for s in srcs], *[pltpu.HBM(l.shape, l.dtype) for l in lands],
                   jax.ShapeDtypeStruct((8, 128), F32)),
        in_specs=[_HBM] * (2 * n), out_specs=(_SEM, _SEM, *[_HBM] * (2 * n), pl.BlockSpec(memory_space=pltpu.VMEM)),
        input_output_aliases={k: 2 + k for k in range(2 * n)},
        compiler_params=pltpu.CompilerParams(has_side_effects=_SIDE_EFFECT, collective_id=collective_id),
    )(*[pltpu.with_memory_space_constraint(s, pltpu.HBM) for s in srcs],
      *[pltpu.with_memory_space_constraint(lax.empty(l.shape, l.dtype), pltpu.HBM) for l in lands])
    return outs[0], outs[1], outs[2:2 + n], outs[2 + n:2 + 2 * n], outs[-1]


def _exchange_wait(send_sems, recv_sems, s_thru, land_thru, after, n_slots, route, name):
    n = len(s_thru)

    def body(*refs):
        s_refs, land_refs = refs[:n], refs[n:2 * n]
        send_sems, recv_sems = refs[2 * n:2 * n + 2]
        x, y, c, chips = _position()
        for k in range(n):
            for j in range(n_slots):
                block, to = route(j, x, y, c, chips)
                cp = pltpu.make_async_remote_copy(
                    src_ref=s_refs[k].at[block], dst_ref=land_refs[k].at[j], send_sem=send_sems.at[n_slots * k + j],
                    recv_sem=recv_sems.at[n_slots * k + j], device_id=to, device_id_type=MESH)
                cp.wait_send()
                cp.wait_recv()

    outs = pl.pallas_call(
        body, name=name,
        out_shape=(*[pltpu.HBM(s.shape, s.dtype) for s in s_thru], *[pltpu.HBM(l.shape, l.dtype) for l in land_thru]),
        in_specs=[_HBM] * (2 * n) + [_SEM, _SEM, pl.BlockSpec(memory_space=pl.ANY)], out_specs=[_HBM] * (2 * n),
        input_output_aliases={k: k for k in range(2 * n)},
        compiler_params=pltpu.CompilerParams(has_side_effects=_SIDE_EFFECT),
    )(*s_thru, *land_thru, send_sems, recv_sems, after)
    return outs[:n], outs[n:]


def _owner_table():
    x, y, c = lax.axis_index("x"), lax.axis_index("y"), lax.axis_index("c")
    chips = [(x, y), (1 - x, y), (x, 1 - y), (1 - x, 1 - y)]
    return jnp.stack([_dev(px, py, c) for px, py in chips]).astype(jnp.int32)


def _chip_partial_sums(table, parts, from_sibling, name):
    n = len(parts)

    def body(tab_ref, *refs):
        del tab_ref
        for g_ref, l_ref, out_ref in zip(refs[:n], refs[n:2 * n], refs[2 * n:]):
            out_ref[...] = (g_ref[...].astype(F32) + l_ref[...].astype(F32)).astype(out_ref.dtype)

    block = lambda p: (None,) + p.shape[1:]
    grid_spec = pltpu.PrefetchScalarGridSpec(
        num_scalar_prefetch=1, grid=(3,),
        in_specs=[pl.BlockSpec(block(p), lambda j, tab: (tab[j + 1], 0, 0)) for p in parts]
        + [pl.BlockSpec(block(p), lambda j, tab: (j + 1, 0, 0)) for p in parts],
        out_specs=[pl.BlockSpec(block(p), lambda j, tab: (j, 0, 0)) for p in parts])
    return pl.pallas_call(
        body, name=name, grid_spec=grid_spec,
        out_shape=[jax.ShapeDtypeStruct((3,) + p.shape[1:], BF16) for p in parts],
        compiler_params=_params(("arbitrary",)),
    )(table, *parts, *from_sibling)


def _final_update(table, parts, from_sibling, from_chips, states, name):
    n = len(parts)
    flipped = [states[k][0].shape != parts[k].shape[1:] for k in range(n)]

    def body(tab_ref, *refs):
        del tab_ref
        ins, outs = refs[:6 * n], refs[6 * n:]
        for k in range(n):
            acc = ins[k][...].astype(F32) + ins[n + k][...].astype(F32)
            for j in range(3):
                acc = acc + ins[2 * n + k][j].astype(F32)
            if flipped[k]:
                acc = acc.T
            w_ref, m_ref, v_ref = ins[3 * n + 3 * k:3 * n + 3 * k + 3]
            outs[4 * k][...] = acc
            for out_ref, val in zip(outs[4 * k + 1:4 * k + 4], _adamw_update(w_ref[...], acc, m_ref[...], v_ref[...])):
                out_ref[...] = val

    def grad_block(k, lead, at):
        r, c = parts[k].shape[1:]
        if flipped[k]:
            return pl.BlockSpec(lead + (r, c // 2), lambda t, tab: (*at(tab), 0, t))
        return pl.BlockSpec(lead + (r // 2, c), lambda t, tab: (*at(tab), t, 0))

    def state_block(k):
        a, b = states[k][0].shape
        return pl.BlockSpec((a // 2, b), lambda t, tab: (t, 0))

    grid_spec = pltpu.PrefetchScalarGridSpec(
        num_scalar_prefetch=1, grid=(2,),
        in_specs=[grad_block(k, (None,), lambda tab: (tab[0],)) for k in range(n)]
        + [grad_block(k, (None,), lambda tab: (0,)) for k in range(n)]
        + [grad_block(k, (3,), lambda tab: (0,)) for k in range(n)]
        + [state_block(k) for k in range(n) for _ in range(3)],
        out_specs=[state_block(k) for k in range(n) for _ in range(4)])
    outs = pl.pallas_call(
        body, name=name, grid_spec=grid_spec,
        out_shape=[jax.ShapeDtypeStruct(states[k][0].shape, F32) for k in range(n) for _ in range(4)],
        compiler_params=_params(("arbitrary",)),
    )(table, *parts, *from_sibling, *from_chips, *[t for k in range(n) for t in states[k]])
    return [outs[4 * k:4 * k + 4] for k in range(n)]


def _sum_blocks(g8):
    _, rows, cols = g8.shape

    def body(g_ref, out_ref):
        acc = g_ref[0]
        for d in range(1, N_DEV):
            acc = acc + g_ref[d]
        out_ref[...] = acc

    return pl.pallas_call(
        body, name="small_grad_sum", grid=(1,),
        in_specs=[_full((N_DEV, rows, cols))], out_specs=_full((rows, cols)),
        out_shape=jax.ShapeDtypeStruct((rows, cols), F32),
        compiler_params=_params(("arbitrary",)),
    )(g8)


def _fwd_mix(x2d, gw, g_mix, conv_w, conv_b, ln_g, ln_b, pool_w, pool_scale, after, seq, tm):
    tokens = x2d.shape[0]
    n_tiles = tokens // tm
    tps = seq // tm

    def body(x_ref, gmix_ref, gw_hbm, cw_ref, cb_ref, lng_ref, lnb_ref, pw_ref, ps_ref, after_ref,
             x1_ref, u_ref, c_ref, pooled_ref, ymix_ref, h1_ref,
             win_v, wout_v, hc_carry, up_carry, sem):
        del after_ref
        i = pl.program_id(0)

        _start_weights(gw_hbm, ("w_in", "w_out"), (win_v, wout_v), sem)

        @pl.when(i % tps == 0)
        def _():
            hc_carry[...] = jnp.zeros_like(hc_carry)
            up_carry[...] = jnp.zeros_like(up_carry)

        x = x_ref[...]
        xh, _ = _rms_fwd(x)
        h1 = (xh * gmix_ref[...]).astype(BF16)
        h1_ref[...] = h1
        u = _dot_nt(h1, win_v[...])
        u_ref[...] = u
        val, gate, up = u[:, :D_CONV], u[:, D_CONV:2 * D_CONV], u[:, 2 * D_CONV:]

        extp = jnp.concatenate([up_carry[...], up], axis=0)
        up_carry[...] = up[tm - POOL_HALO:, :]
        pos = lax.broadcasted_iota(jnp.int32, (tm, 1), 0) + (i % tps) * tm
        run = extp
        mixed = []
        for g, w in enumerate(POOL_WINDOWS):
            lo = g * POOL_GROUP_DIM
            run = run[:, POOL_GROUP_DIM if g else 0:]
            run = run + pltpu.roll(run, w // 2, 0)
            cnt = jnp.minimum(pos + 1, w).astype(F32)
            pooled = run[POOL_HALO:, :POOL_GROUP_DIM] / cnt - up[:, lo:lo + POOL_GROUP_DIM]
            pooled = pooled.astype(BF16)
            pooled_ref[:, lo:lo + POOL_GROUP_DIM] = pooled
            mixed.append(_dot(pooled, pw_ref[g].astype(BF16)))
        y_pool = jnp.concatenate(mixed, axis=-1) * ps_ref[...]
        y_pool = y_pool.astype(BF16)
        ymix_ref[:, D_CONV:] = y_pool
        out = _dot(y_pool, wout_v[D_CONV:, :])

        hc = val * _sigmoid(gate)
        ext = jnp.concatenate([hc_carry[...], hc], axis=0)
        hc_carry[...] = hc[tm - CONV_HALO:, :]
        conv = jnp.broadcast_to(cb_ref[...], (tm, D_CONV))
        ahead_by = _sublane_shifts(ext)
        for k in range(CONV_WIDTH):
            whole, part = divmod(CONV_HALO - (CONV_WIDTH - 1) + k, 8)
            conv = conv + cw_ref[k:k + 1, :] * ahead_by[part][8 * whole:8 * whole + tm, :]
        c_ref[...] = conv
        mu = jnp.mean(conv, axis=-1, keepdims=True)
        cen = conv - mu
        ln = cen * lax.rsqrt(jnp.mean(cen * cen, axis=-1, keepdims=True) + EPS) * lng_ref[...] + lnb_ref[...]
        y_conv = ln * _sigmoid(ln)
        y_conv = y_conv.astype(BF16)
        ymix_ref[:, :D_CONV] = y_conv
        x1_ref[...] = x + (out + _dot(y_conv, wout_v[:D_CONV, :]))

    row = lambda w: pl.BlockSpec((tm, w), lambda i: (i, 0))
    return pl.pallas_call(
        body, name="fwd_mix", grid=(n_tiles,),
        in_specs=[row(D_MODEL), _full((1, D_MODEL)), pl.BlockSpec(memory_space=pl.ANY),
                  _full((CONV_WIDTH, D_CONV)), _full((1, D_CONV)), _full((1, D_CONV)), _full((1, D_CONV)),
                  _full((4, POOL_GROUP_DIM, POOL_GROUP_DIM)), _full((1, D_POOL)), _full(after.shape)],
        out_specs=[row(D_MODEL), row(D_IN), row(D_CONV), row(D_POOL), row(D_MODEL), row(D_MODEL)],
        out_shape=[jax.ShapeDtypeStruct((tokens, D_MODEL), F32), jax.ShapeDtypeStruct((tokens, D_IN), F32),
                   jax.ShapeDtypeStruct((tokens, D_CONV), F32), jax.ShapeDtypeStruct((tokens, D_POOL), BF16),
                   jax.ShapeDtypeStruct((tokens, D_MODEL), BF16), jax.ShapeDtypeStruct((tokens, D_MODEL), BF16)],
        scratch_shapes=[pltpu.VMEM((D_IN, D_MODEL), BF16), pltpu.VMEM((D_MODEL, D_MODEL), BF16),
                        pltpu.VMEM((CONV_HALO, D_CONV), F32), pltpu.VMEM((POOL_HALO, D_POOL), F32),
                        pltpu.SemaphoreType.DMA((2,))],
        compiler_params=_params(),
    )(x2d, g_mix, gw, conv_w, conv_b, ln_g, ln_b, pool_w, pool_scale, after)


def _fwd_kv(mem2d, gw, g_mem):
    rows = mem2d.shape[0]
    n_b = rows // N_MEM

    def body(mem_ref, g_ref, gw_hbm, mn_ref, kv_ref, wkv_v, sem):
        @pl.when(pl.program_id(0) == 0)
        def _():
            copies = _load_weight(gw_hbm, "w_kv", wkv_v, sem)
            for cp in copies:
                cp.start()
            for cp in copies:
                cp.wait()

        mh, _ = _rms_fwd(mem_ref[...])
        mn = (mh * g_ref[...]).astype(BF16)
        mn_ref[...] = mn
        kv_ref[...] = _dot_nt(mn, wkv_v[...]).astype(BF16)

    return pl.pallas_call(
        body, name="fwd_kv", grid=(n_b,),
        in_specs=[pl.BlockSpec((N_MEM, D_MODEL), lambda b: (b, 0)), _full((1, D_MODEL)), pl.BlockSpec(memory_space=pl.ANY)],
        out_specs=[pl.BlockSpec((N_MEM, D_MODEL), lambda b: (b, 0)), pl.BlockSpec((N_MEM, 2 * D_MODEL), lambda b: (b, 0))],
        out_shape=[jax.ShapeDtypeStruct((rows, D_MODEL), BF16), jax.ShapeDtypeStruct((rows, 2 * D_MODEL), BF16)],
        scratch_shapes=[pltpu.VMEM((2 * D_MODEL, D_MODEL), BF16), pltpu.SemaphoreType.DMA],
        compiler_params=_params(),
    )(mem2d, g_mem, gw)


def _softmax_rows(s):
    e = jnp.exp(s - jnp.max(s, axis=-1, keepdims=True))
    return e / jnp.sum(e, axis=-1, keepdims=True)


def _fwd_attn(x1, kv, gw, g_x, seq, tm):
    tokens = x1.shape[0]
    n_tiles = tokens // tm
    tps = seq // tm

    def body(x1_ref, kv_ref, g_ref, gw_hbm, x2_ref, h2_ref, q_ref, o_ref, wq_v, wo_v, sem):
        _start_weights(gw_hbm, ("w_q", "w_o"), (wq_v, wo_v), sem)
        x1v = x1_ref[...]
        xh, _ = _rms_fwd(x1v)
        h2 = (xh * g_ref[...]).astype(BF16)
        h2_ref[...] = h2
        q = (_dot(h2, wq_v[...]) * (HEAD_DIM ** -0.5)).astype(BF16)
        q_ref[...] = q
        heads = [slice(h * HEAD_DIM, (h + 1) * HEAD_DIM) for h in range(HEADS)]
        scores = [_dot_nt(q[:, hd], kv_ref[:, hd]) for hd in heads]
        probs = [_softmax_rows(s).astype(BF16) for s in scores]
        outs = [_dot(p, kv_ref[:, pl.ds(D_MODEL + h * HEAD_DIM, HEAD_DIM)]) for h, p in enumerate(probs)]
        o = jnp.concatenate(outs, axis=-1).astype(BF16)
        o_ref[...] = o
        x2_ref[...] = x1v + _dot(o, wo_v[...])

    row = lambda w: pl.BlockSpec((tm, w), lambda i: (i, 0))
    return pl.pallas_call(
        body, name="fwd_attn", grid=(n_tiles,),
        in_specs=[row(D_MODEL), pl.BlockSpec((N_MEM, 2 * D_MODEL), lambda i: (i // tps, 0)), _full((1, D_MODEL)),
                  pl.BlockSpec(memory_space=pl.ANY)],
        out_specs=[row(D_MODEL)] * 4,
        out_shape=[jax.ShapeDtypeStruct((tokens, D_MODEL), F32)] + [jax.ShapeDtypeStruct((tokens, D_MODEL), BF16)] * 3,
        scratch_shapes=[pltpu.VMEM((D_MODEL, D_MODEL), BF16), pltpu.VMEM((D_MODEL, D_MODEL), BF16), pltpu.SemaphoreType.DMA((2,))],
        compiler_params=_params(),
    )(x1, kv, g_x, gw)


def _ffn_conv(uu, halo, w_ref, b_ref, cols):
    ext = jnp.concatenate([halo, uu], axis=0)
    p1 = pltpu.roll(ext, 1, 0)[FFN_HALO:, :]
    p2 = pltpu.roll(ext, 2, 0)[FFN_HALO:, :]
    return b_ref[:, cols] + w_ref[2:3, cols] * uu + w_ref[1:2, cols] * p1 + w_ref[0:1, cols] * p2


def _fwd_ffn(x2, target, gw, g_ffn, ffn_w, ffn_b, g_final, seq, tm):
    tokens = x2.shape[0]
    n_tiles = tokens // tm
    tps = seq // tm
    n_chunks = D_FF // FFN_CHUNK

    def body(x2_ref, tgt_ref, gffn_ref, gw_hbm, fw_ref, fb_ref, gfin_ref,
             uu_ref, cc_ref, a_ref, h3_ref, dx3_ref, dx3b_ref, loss_ref, dgfin_ref,
             wup_v, wdown_v, carry, sem):
        i = pl.program_id(0)

        _start_weights(gw_hbm, ("w_up", "w_down"), (wup_v, wdown_v), sem)

        @pl.when(i == 0)
        def _():
            loss_ref[...] = jnp.zeros_like(loss_ref)
            dgfin_ref[...] = jnp.zeros_like(dgfin_ref)

        @pl.when(i % tps == 0)
        def _():
            carry[...] = jnp.zeros_like(carry)

        x2v = x2_ref[...]
        xh, _ = _rms_fwd(x2v)
        h3 = (xh * gffn_ref[...]).astype(BF16)
        h3_ref[...] = h3
        acc = jnp.zeros((tm, D_MODEL), F32)
        for jc in range(n_chunks):
            halves = []
            for half in range(2):
                cols = pl.ds(half * D_FF + jc * FFN_CHUNK, FFN_CHUNK)
                uu = _dot_nt(h3, wup_v[cols, :])
                uu_ref[:, cols] = uu.astype(BF16)
                cc = _ffn_conv(uu, carry[:, cols], fw_ref, fb_ref, cols)
                cc_ref[:, cols] = cc.astype(BF16)
                halves.append(cc)
                carry[:, cols] = uu[tm - FFN_HALO:, :]
            gate, val = halves
            a = (gate * _sigmoid(gate) * val).astype(BF16)
            a_ref[:, pl.ds(jc * FFN_CHUNK, FFN_CHUNK)] = a
            acc = acc + _dot(a, wdown_v[pl.ds(jc * FFN_CHUNK, FFN_CHUNK), :])
        x3 = x2v + acc

        xh3, r3 = _rms_fwd(x3)
        gfin = gfin_ref[...]
        err = xh3 * gfin - tgt_ref[...]
        loss_ref[...] += jnp.full(loss_ref.shape, jnp.sum(err * err) * (0.5 / D_MODEL), F32)
        dy = err * (1.0 / D_MODEL)
        dgfin_ref[...] += _colsum(dy * xh3)
        dx3 = _rms_bwd(dy, xh3, r3, gfin)
        dx3_ref[...] = dx3
        dx3b_ref[...] = dx3.astype(BF16)

    row = lambda w: pl.BlockSpec((tm, w), lambda i: (i, 0))
    return pl.pallas_call(
        body, name="fwd_ffn", grid=(n_tiles,),
        in_specs=[row(D_MODEL), row(D_MODEL), _full((1, D_MODEL)), pl.BlockSpec(memory_space=pl.ANY),
                  _full((FFN_CONV_WIDTH, 2 * D_FF)), _full((1, 2 * D_FF)), _full((1, D_MODEL))],
        out_specs=[row(2 * D_FF), row(2 * D_FF), row(D_FF), row(D_MODEL), row(D_MODEL), row(D_MODEL), _full((8, 128)),
                   _full((1, D_MODEL))],
        out_shape=[jax.ShapeDtypeStruct((tokens, 2 * D_FF), BF16), jax.ShapeDtypeStruct((tokens, 2 * D_FF), BF16),
                   jax.ShapeDtypeStruct((tokens, D_FF), BF16),
                   jax.ShapeDtypeStruct((tokens, D_MODEL), BF16), jax.ShapeDtypeStruct((tokens, D_MODEL), F32),
                   jax.ShapeDtypeStruct((tokens, D_MODEL), BF16),
                   jax.ShapeDtypeStruct((8, 128), F32), jax.ShapeDtypeStruct((1, D_MODEL), F32)],
        scratch_shapes=[pltpu.VMEM((2 * D_FF, D_MODEL), BF16), pltpu.VMEM((D_FF, D_MODEL), BF16),
                        pltpu.VMEM((FFN_HALO, 2 * D_FF), F32), pltpu.SemaphoreType.DMA((2,))],
        compiler_params=_params(),
    )(x2, target, g_ffn, gw, ffn_w, ffn_b, g_final)


def _bwd_ffn(dx3, x2, uu_all, cc_all, gw, g_ffn, ffn_w, seq, tm):
    tokens = x2.shape[0]
    n_tiles = tokens // tm
    tps = seq // tm
    n_chunks = D_FF // FFN_CHUNK

    def body(dx3_ref, x2_ref, uu_ref, cc_ref, gffn_ref, gw_hbm, fw_ref,
             dx2_ref, dx2b_ref, duu_ref, dfb_ref, dfw_ref, dg_ref,
             wup_v, wdown_v, carry, sem):
        i = pl.program_id(0)
        t = n_tiles - 1 - i

        _start_weights(gw_hbm, ("w_down", "w_up"), (wdown_v, wup_v), sem)

        @pl.when(i == 0)
        def _():
            dfb_ref[...] = jnp.zeros_like(dfb_ref)
            dfw_ref[...] = jnp.zeros_like(dfw_ref)
            dg_ref[...] = jnp.zeros_like(dg_ref)

        @pl.when(t % tps == tps - 1)
        def _():
            carry[...] = jnp.zeros_like(carry)

        dx3v = dx3_ref[...]
        dx3b = dx3v.astype(BF16)
        dh3 = jnp.zeros((tm, D_MODEL), F32)
        for jc in range(n_chunks):
            da = _dot_nt(dx3b, wdown_v[pl.ds(jc * FFN_CHUNK, FFN_CHUNK), :])
            colss = [pl.ds(half * D_FF + jc * FFN_CHUNK, FFN_CHUNK) for half in range(2)]
            gate, val = [cc_ref[:, cols].astype(F32) for cols in colss]
            sg = _sigmoid(gate)
            dgate = da * val * (sg * (1.0 + gate * (1.0 - sg)))
            dval = da * (gate * sg)
            for dcc, cols in zip((dgate, dval), colss):
                uu = uu_ref[:, cols].astype(F32)
                dfb_ref[:, cols] += _colsum(dcc)
                ext = jnp.concatenate([dcc, carry[:, cols]], axis=0)
                carry[:, cols] = dcc[:FFN_HALO, :]
                n1 = pltpu.roll(ext, tm + FFN_HALO - 1, 0)[:tm, :]
                n2 = pltpu.roll(ext, tm + FFN_HALO - 2, 0)[:tm, :]
                duu = fw_ref[2:3, cols] * dcc + fw_ref[1:2, cols] * n1 + fw_ref[0:1, cols] * n2
                dfw_ref[2:3, cols] += _colsum(uu * dcc)
                dfw_ref[1:2, cols] += _colsum(uu * n1)
                dfw_ref[0:1, cols] += _colsum(uu * n2)
                duub = duu.astype(BF16)
                duu_ref[:, cols] = duub
                dh3 = dh3 + _dot(duub, wup_v[cols, :])
        xh, r = _rms_fwd(x2_ref[...])
        dg_ref[...] += _colsum(dh3 * xh)
        dx2 = dx3v + _rms_bwd(dh3, xh, r, gffn_ref[...])
        dx2_ref[...] = dx2
        dx2b_ref[...] = dx2.astype(BF16)

    rev = lambda w: pl.BlockSpec((tm, w), lambda i: (n_tiles - 1 - i, 0))
    return pl.pallas_call(
        body, name="bwd_ffn", grid=(n_tiles,),
        in_specs=[rev(D_MODEL), rev(D_MODEL), rev(2 * D_FF), rev(2 * D_FF), _full((1, D_MODEL)),
                  pl.BlockSpec(memory_space=pl.ANY), _full((FFN_CONV_WIDTH, 2 * D_FF))],
        out_specs=[rev(D_MODEL), rev(D_MODEL), rev(2 * D_FF), _full((1, 2 * D_FF)), _full((FFN_CONV_WIDTH, 2 * D_FF)),
                   _full((1, D_MODEL))],
        out_shape=[jax.ShapeDtypeStruct((tokens, D_MODEL), F32), jax.ShapeDtypeStruct((tokens, D_MODEL), BF16),
                   jax.ShapeDtypeStruct((tokens, 2 * D_FF), BF16),
                   jax.ShapeDtypeStruct((1, 2 * D_FF), F32), jax.ShapeDtypeStruct((FFN_CONV_WIDTH, 2 * D_FF), F32),
                   jax.ShapeDtypeStruct((1, D_MODEL), F32)],
        scratch_shapes=[pltpu.VMEM((2 * D_FF, D_MODEL), BF16), pltpu.VMEM((D_FF, D_MODEL), BF16),
                        pltpu.VMEM((FFN_HALO, 2 * D_FF), F32), pltpu.SemaphoreType.DMA((2,))],
        compiler_params=_params(),
    )(dx3, x2, uu_all, cc_all, g_ffn, gw, ffn_w)


def _bwd_attn(dx2, x1, q, kv, gw, g_x, after, seq, tm):
    tokens = x1.shape[0]
    n_tiles = tokens // tm
    tps = seq // tm
    n_b = tokens // seq

    def body(dx2_ref, x1_ref, q_ref, kv_ref, g_ref, gw_hbm, after_ref, dx1_ref, dx1b_ref, dq_ref, dkv_ref, dg_ref,
             wq_v, wo_v, sem):
        del after_ref
        i = pl.program_id(0)

        _start_weights(gw_hbm, ("w_o", "w_q"), (wo_v, wq_v), sem)

        @pl.when(i == 0)
        def _():
            dg_ref[...] = jnp.zeros_like(dg_ref)

        @pl.when(i % tps == 0)
        def _():
            dkv_ref[...] = jnp.zeros_like(dkv_ref)

        dx2v = dx2_ref[...]
        do = _dot_nt(dx2v.astype(BF16), wo_v[...]).astype(BF16)
        q = q_ref[...]
        heads = [slice(h * HEAD_DIM, (h + 1) * HEAD_DIM) for h in range(HEADS)]
        kcols = [pl.ds(h * HEAD_DIM, HEAD_DIM) for h in range(HEADS)]
        vcols = [pl.ds(D_MODEL + h * HEAD_DIM, HEAD_DIM) for h in range(HEADS)]
        scores = [_dot_nt(q[:, hd], kv_ref[:, kc]) for hd, kc in zip(heads, kcols)]
        dps = [_dot_nt(do[:, hd], kv_ref[:, vc]) for hd, vc in zip(heads, vcols)]
        probs = [_softmax_rows(s) for s in scores]
        dss = [(p * (dp - jnp.sum(dp * p, axis=-1, keepdims=True))).astype(BF16) for p, dp in zip(probs, dps)]
        for p, hd, vc in zip(probs, heads, vcols):
            dkv_ref[:, vc] += _dot_tn(p.astype(BF16), do[:, hd])
        dqs = [_dot(ds, kv_ref[:, kc]) * (HEAD_DIM ** -0.5) for ds, kc in zip(dss, kcols)]
        for ds, hd, kc in zip(dss, heads, kcols):
            dkv_ref[:, kc] += _dot_tn(ds, q[:, hd])
        dq = jnp.concatenate(dqs, axis=-1).astype(BF16)
        dq_ref[...] = dq
        dh2 = _dot_nt(dq, wq_v[...])
        xh, r = _rms_fwd(x1_ref[...])
        dg_ref[...] += _colsum(dh2 * xh)
        dx1 = dx2v + _rms_bwd(dh2, xh, r, g_ref[...])
        dx1_ref[...] = dx1
        dx1b_ref[...] = dx1.astype(BF16)

    row = lambda w: pl.BlockSpec((tm, w), lambda i: (i, 0))
    per_b = pl.BlockSpec((N_MEM, 2 * D_MODEL), lambda i: (i // tps, 0))
    return pl.pallas_call(
        body, name="bwd_attn", grid=(n_tiles,),
        in_specs=[row(D_MODEL), row(D_MODEL), row(D_MODEL), per_b, _full((1, D_MODEL)), pl.BlockSpec(memory_space=pl.ANY),
                  _full(after.shape)],
        out_specs=[row(D_MODEL), row(D_MODEL), row(D_MODEL), per_b, _full((1, D_MODEL))],
        out_shape=[jax.ShapeDtypeStruct((tokens, D_MODEL), F32), jax.ShapeDtypeStruct((tokens, D_MODEL), BF16),
                   jax.ShapeDtypeStruct((tokens, D_MODEL), BF16),
                   jax.ShapeDtypeStruct((n_b * N_MEM, 2 * D_MODEL), F32), jax.ShapeDtypeStruct((1, D_MODEL), F32)],
        scratch_shapes=[pltpu.VMEM((D_MODEL, D_MODEL), BF16), pltpu.VMEM((D_MODEL, D_MODEL), BF16), pltpu.SemaphoreType.DMA((2,))],
        compiler_params=_params(),
    )(dx2, x1, q, kv, g_x, gw, after)


def _bwd_kv(dkv, mem2d, gw):
    rows = mem2d.shape[0]
    n_b = rows // N_MEM

    def body(dkv_ref, mem_ref, gw_hbm, dkvb_ref, dg_ref, wkv_v, sem):
        @pl.when(pl.program_id(0) == 0)
        def _():
            copies = _load_weight(gw_hbm, "w_kv", wkv_v, sem)
            for cp in copies:
                cp.start()
            for cp in copies:
                cp.wait()
            dg_ref[...] = jnp.zeros_like(dg_ref)

        dkvb = dkv_ref[...].astype(BF16)
        dkvb_ref[...] = dkvb
        dmn = _dot(dkvb, wkv_v[...])
        mh, _ = _rms_fwd(mem_ref[...])
        dg_ref[...] += _colsum(dmn * mh)

    return pl.pallas_call(
        body, name="bwd_kv", grid=(n_b,),
        in_specs=[pl.BlockSpec((N_MEM, 2 * D_MODEL), lambda b: (b, 0)), pl.BlockSpec((N_MEM, D_MODEL), lambda b: (b, 0)),
                  pl.BlockSpec(memory_space=pl.ANY)],
        out_specs=[pl.BlockSpec((N_MEM, 2 * D_MODEL), lambda b: (b, 0)), _full((1, D_MODEL))],
        out_shape=[jax.ShapeDtypeStruct((rows, 2 * D_MODEL), BF16), jax.ShapeDtypeStruct((1, D_MODEL), F32)],
        scratch_shapes=[pltpu.VMEM((2 * D_MODEL, D_MODEL), BF16), pltpu.SemaphoreType.DMA],
        compiler_params=_params(),
    )(dkv, mem2d, gw)


def _bwd_mix(dx1, x2d, u_all, c_all, pooled_all, gw, g_mix, conv_w, ln_g, ln_b, pool_w, pool_scale, after, seq, tm):
    tokens = x2d.shape[0]
    n_tiles = tokens // tm
    tps = seq // tm

    def body(dx1_ref, x_ref, u_ref, c_ref, pooled_ref, gmix_ref, gw_hbm, cw_ref, lng_ref, lnb_ref, pw_ref, ps_ref,
             after_ref, dx_ref, du_ref, dgmix_ref, dcw_ref, dcb_ref, dlng_ref, dlnb_ref, dpw_ref, dps_ref,
             win_v, wout_v, dc_carry, e_carry, sem):
        del after_ref
        i = pl.program_id(0)
        t = n_tiles - 1 - i

        _start_weights(gw_hbm, ("w_out", "w_in"), (wout_v, win_v), sem)

        @pl.when(i == 0)
        def _():
            for ref in (dgmix_ref, dcw_ref, dcb_ref, dlng_ref, dlnb_ref, dpw_ref, dps_ref):
                ref[...] = jnp.zeros_like(ref)

        @pl.when(t % tps == tps - 1)
        def _():
            dc_carry[...] = jnp.zeros_like(dc_carry)
            e_carry[...] = jnp.zeros_like(e_carry)

        dx1v = dx1_ref[...]
        dymix = _dot_nt(dx1v.astype(BF16), wout_v[...])
        dyc, dyp = dymix[:, :D_CONV], dymix[:, D_CONV:]
        u = u_ref[...]
        val, gate = u[:, :D_CONV], u[:, D_CONV:2 * D_CONV]

        conv = c_ref[...]
        mu = jnp.mean(conv, axis=-1, keepdims=True)
        cen = conv - mu
        rs = lax.rsqrt(jnp.mean(cen * cen, axis=-1, keepdims=True) + EPS)
        chat = cen * rs
        ln = chat * lng_ref[...] + lnb_ref[...]
        sl = _sigmoid(ln)
        dln = dyc * (sl * (1.0 + ln * (1.0 - sl)))
        dlng_ref[...] += _colsum(dln * chat)
        dlnb_ref[...] += _colsum(dln)
        dchat = dln * lng_ref[...]
        dc = rs * (dchat - jnp.mean(dchat, axis=-1, keepdims=True)
                   - chat * jnp.mean(dchat * chat, axis=-1, keepdims=True))
        dcb_ref[...] += _colsum(dc)
        sg = _sigmoid(gate)
        hc = val * sg
        ext = jnp.concatenate([dc, dc_carry[...]], axis=0)
        dc_carry[...] = dc[:CONV_HALO, :]
        dhc = jnp.zeros((tm, D_CONV), F32)
        ahead_by = _sublane_shifts(ext)
        for k in range(CONV_WIDTH):
            whole, part = divmod(CONV_WIDTH - 1 - k, 8)
            tap = ahead_by[part][8 * whole:8 * whole + tm, :]
            dhc = dhc + cw_ref[k:k + 1, :] * tap
            dcw_ref[k:k + 1, :] += _colsum_mxu(hc * tap)
        du_ref[:, :D_CONV] = (dhc * sg).astype(BF16)
        du_ref[:, D_CONV:2 * D_CONV] = (dhc * val * (sg * (1.0 - sg))).astype(BF16)

        pos = lax.broadcasted_iota(jnp.int32, (tm, 1), 0) + (t % tps) * tm
        es, dpooled = [], []
        for g, w in enumerate(POOL_WINDOWS):
            cols = pl.ds(g * POOL_GROUP_DIM, POOL_GROUP_DIM)
            lo = g * POOL_GROUP_DIM
            pooled = pooled_ref[:, cols]
            pw = pw_ref[g].astype(BF16)
            dyg = dyp[:, lo:lo + POOL_GROUP_DIM]
            dps_ref[:, cols] += _colsum(dyg * _dot(pooled, pw))
            dmixed = (dyg * ps_ref[:, cols]).astype(BF16)
            dpw_ref[g] += _dot_tn(pooled, dmixed)
            dpo = _dot_nt(dmixed, pw)
            dpooled.append(dpo)
            es.append(dpo / jnp.minimum(pos + 1, w).astype(F32))
        e = jnp.concatenate(es, axis=-1)
        run = jnp.concatenate([e, e_carry[...]], axis=0)
        e_carry[...] = e[:POOL_HALO, :]
        rows = tm + POOL_HALO
        for g, w in enumerate(POOL_WINDOWS):
            lo = g * POOL_GROUP_DIM
            run = run[:, POOL_GROUP_DIM if g else 0:]
            run = run + pltpu.roll(run, rows - w // 2, 0)
            du_ref[:, 2 * D_CONV + lo:2 * D_CONV + lo + POOL_GROUP_DIM] = (
                run[:tm, :POOL_GROUP_DIM] - dpooled[g]).astype(BF16)

        dh1 = _dot(du_ref[...], win_v[...])
        xh, r = _rms_fwd(x_ref[...])
        dgmix_ref[...] += _colsum(dh1 * xh)
        dx_ref[...] = dx1v + _rms_bwd(dh1, xh, r, gmix_ref[...])

    rev = lambda w: pl.BlockSpec((tm, w), lambda i: (n_tiles - 1 - i, 0))
    return pl.pallas_call(
        body, name="bwd_mix", grid=(n_tiles,),
        in_specs=[rev(D_MODEL), rev(D_MODEL), rev(D_IN), rev(D_CONV), rev(D_POOL), _full((1, D_MODEL)),
                  pl.BlockSpec(memory_space=pl.ANY), _full((CONV_WIDTH, D_CONV)), _full((1, D_CONV)), _full((1, D_CONV)),
                  _full((4, POOL_GROUP_DIM, POOL_GROUP_DIM)), _full((1, D_POOL)), _full(after.shape)],
        out_specs=[rev(D_MODEL), rev(D_IN), _full((1, D_MODEL)), _full((CONV_WIDTH, D_CONV)), _full((1, D_CONV)),
                   _full((1, D_CONV)), _full((1, D_CONV)), _full((4, POOL_GROUP_DIM, POOL_GROUP_DIM)), _full((1, D_POOL))],
        out_shape=[jax.ShapeDtypeStruct((tokens, D_MODEL), F32), jax.ShapeDtypeStruct((tokens, D_IN), BF16),
                   jax.ShapeDtypeStruct((1, D_MODEL), F32), jax.ShapeDtypeStruct((CONV_WIDTH, D_CONV), F32),
                   jax.ShapeDtypeStruct((1, D_CONV), F32), jax.ShapeDtypeStruct((1, D_CONV), F32),
                   jax.ShapeDtypeStruct((1, D_CONV), F32),
                   jax.ShapeDtypeStruct((4, POOL_GROUP_DIM, POOL_GROUP_DIM), F32), jax.ShapeDtypeStruct((1, D_POOL), F32)],
        scratch_shapes=[pltpu.VMEM((D_IN, D_MODEL), BF16), pltpu.VMEM((D_MODEL, D_MODEL), BF16),
                        pltpu.VMEM((CONV_HALO, D_CONV), F32), pltpu.VMEM((POOL_HALO, D_POOL), F32),
                        pltpu.SemaphoreType.DMA((2,))],
        compiler_params=_params(),
    )(dx1, x2d, u_all, c_all, pooled_all, g_mix, gw, conv_w, ln_g, ln_b, pool_w, pool_scale, after)


def _wgrad(a, b, name, after=None):
    tokens, m = a.shape
    n = b.shape[1]
    tm = 512 if m % 512 == 0 else 256
    extra = [] if after is None else [after]

    def body(a_ref, b_ref, *rest):
        rest[-1][...] = _dot_tn(a_ref[...], b_ref[...]).astype(rest[-1].dtype)

    return pl.pallas_call(
        body, name=name, grid=(m // tm,),
        in_specs=[pl.BlockSpec((tokens, tm), lambda i: (0, i)), _full((tokens, n))] + [_full(t.shape) for t in extra],
        out_specs=pl.BlockSpec((tm, n), lambda i: (i, 0)),
        out_shape=jax.ShapeDtypeStruct((m, n), BF16),
        compiler_params=_params(),
    )(a, b, *extra)


def _adamw_update(w, g, m, v):
    nm = ADAM_B1 * m + (1.0 - ADAM_B1) * g
    nv = ADAM_B2 * v + (1.0 - ADAM_B2) * (g * g)
    m_hat = nm / (1.0 - ADAM_B1 ** ADAM_STEP)
    v_hat = nv / (1.0 - ADAM_B2 ** ADAM_STEP)
    return -ADAM_LR * (m_hat / (jnp.sqrt(v_hat) + ADAM_EPS) + ADAM_WD * w), nm, nv


def _adamw_small(ws, gs, ms, vs):
    n = len(ws)

    def body(*refs):
        ins, outs = refs[:4 * n], refs[4 * n:]
        for k in range(n):
            d, nm, nv = _adamw_update(*[ins[j * n + k][...] for j in range(4)])
            outs[k][...] = d
            outs[n + k][...] = nm
            outs[2 * n + k][...] = nv

    vmem = pl.BlockSpec(memory_space=pltpu.VMEM)
    outs = pl.pallas_call(
        body, name="adamw_small",
        in_specs=[vmem] * (4 * n), out_specs=[vmem] * (3 * n),
        out_shape=[jax.ShapeDtypeStruct(w.shape, F32) for w in ws] * 3,
    )(*ws, *gs, *ms, *vs)
    return outs[:n], outs[n:2 * n], outs[2 * n:]


SMALL = (("norm_mix_g", (1, 1024)), ("conv_dw_b", (1, 512)), ("conv_ln_g", (1, 512)), ("conv_ln_b", (1, 512)),
         ("pool_w", (1, 4, 128, 128)), ("pool_scale", (1, 512)), ("norm_xattn_g", (1, 1024)), ("norm_mem_g", (1, 1024)),
         ("norm_ffn_g", (1, 1024)), ("ffn_dw_b", (1, 5632)), ("norm_final_g", (1024,)))
LANES = 128


def _pack_rows(arrs):
    flat = jnp.concatenate([a.reshape(-1) for a in arrs])
    pad = (-flat.shape[0]) % (8 * LANES)
    return jnp.pad(flat, (0, pad)).reshape(-1, LANES)


def kernel(x, mem, norm_mix_g, w_in, conv_dw_w, conv_dw_b, conv_ln_g, conv_ln_b, pool_w, pool_scale, w_out, norm_xattn_g, norm_mem_g, w_q, w_kv, w_o, norm_ffn_g, w_up, ffn_dw_w, ffn_dw_b, w_down, norm_final_g, loss_target, m_norm_mix_g, m_w_in, m_conv_dw_w, m_conv_dw_b, m_conv_ln_g, m_conv_ln_b, m_pool_w, m_pool_scale, m_w_out, m_norm_xattn_g, m_norm_mem_g, m_w_q, m_w_kv, m_w_o, m_norm_ffn_g, m_w_up, m_ffn_dw_w, m_ffn_dw_b, m_w_down, m_norm_final_g, v_norm_mix_g, v_w_in, v_conv_dw_w, v_conv_dw_b, v_conv_ln_g, v_conv_ln_b, v_pool_w, v_pool_scale, v_w_out, v_norm_xattn_g, v_norm_mem_g, v_w_q, v_w_kv, v_w_o, v_norm_ffn_g, v_w_up, v_ffn_dw_w, v_ffn_dw_b, v_w_down, v_norm_final_g):
    weights = dict(norm_mix_g=norm_mix_g, w_in=w_in, conv_dw_w=conv_dw_w, conv_dw_b=conv_dw_b, conv_ln_g=conv_ln_g,
                   conv_ln_b=conv_ln_b, pool_w=pool_w, pool_scale=pool_scale, w_out=w_out, norm_xattn_g=norm_xattn_g,
                   norm_mem_g=norm_mem_g, w_q=w_q, w_kv=w_kv, w_o=w_o, norm_ffn_g=norm_ffn_g, w_up=w_up,
                   ffn_dw_w=ffn_dw_w, ffn_dw_b=ffn_dw_b, w_down=w_down, norm_final_g=norm_final_g)
    moments_m = dict(norm_mix_g=m_norm_mix_g, w_in=m_w_in, conv_dw_w=m_conv_dw_w, conv_dw_b=m_conv_dw_b,
                     conv_ln_g=m_conv_ln_g, conv_ln_b=m_conv_ln_b, pool_w=m_pool_w, pool_scale=m_pool_scale,
                     w_out=m_w_out, norm_xattn_g=m_norm_xattn_g, norm_mem_g=m_norm_mem_g, w_q=m_w_q, w_kv=m_w_kv,
                     w_o=m_w_o, norm_ffn_g=m_norm_ffn_g, w_up=m_w_up, ffn_dw_w=m_ffn_dw_w, ffn_dw_b=m_ffn_dw_b,
                     w_down=m_w_down, norm_final_g=m_norm_final_g)
    moments_v = dict(norm_mix_g=v_norm_mix_g, w_in=v_w_in, conv_dw_w=v_conv_dw_w, conv_dw_b=v_conv_dw_b,
                     conv_ln_g=v_conv_ln_g, conv_ln_b=v_conv_ln_b, pool_w=v_pool_w, pool_scale=v_pool_scale,
                     w_out=v_w_out, norm_xattn_g=v_norm_xattn_g, norm_mem_g=v_norm_mem_g, w_q=v_w_q, w_kv=v_w_kv,
                     w_o=v_w_o, norm_ffn_g=v_norm_ffn_g, w_up=v_w_up, ffn_dw_w=v_ffn_dw_w, ffn_dw_b=v_ffn_dw_b,
                     w_down=v_w_down, norm_final_g=v_norm_final_g)
    order = list(weights)
    transposed = ("w_in", "w_kv", "w_up")

    n_b, seq, _ = x.shape
    tokens = n_b * seq
    tm_mix = min(512, seq // 2)
    tm_attn = min(1024, seq // 2)
    tm_ffn = min(256, seq // 2)
    dev = 4 * lax.axis_index("x") + 2 * lax.axis_index("y") + lax.axis_index("c")

    packs = [jnp.concatenate([weights[n][0].T if n in transposed else weights[n][0] for n in names], axis=0).astype(BF16)
             for names in AG_GROUPS]
    small_sharded = _pack_rows([conv_dw_w[0], ffn_dw_w[0]])
    gw_mix, gsmall = _all_gather([packs[0], small_sharded], "weights_all_gather")
    flights = []
    after = gw_mix
    for k in (1, 2):
        own_in_place = lax.dynamic_update_slice(lax.empty((N_DEV,) + packs[k].shape, BF16), packs[k][None], (dev, 0, 0))
        flights.append(_gather_start(own_in_place, after, "weights_gather_start_%d" % k, BARRIER_IDS["gather_start"][k - 1]))
        after = flights[-1][3]
    gflat = gsmall.reshape(N_DEV, -1)
    n_cw = CONV_WIDTH * (D_CONV // N_DEV)
    n_fw = FFN_CONV_WIDTH * (2 * D_FF // N_DEV)
    conv_w = gflat[:, :n_cw].reshape(N_DEV, CONV_WIDTH, D_CONV // N_DEV).transpose(1, 0, 2).reshape(CONV_WIDTH, D_CONV)
    ffn_w = gflat[:, n_cw:n_cw + n_fw].reshape(N_DEV, FFN_CONV_WIDTH, 2 * D_FF // N_DEV).transpose(1, 0, 2).reshape(
        FFN_CONV_WIDTH, 2 * D_FF)

    x2d = x.reshape(tokens, D_MODEL)
    mem2d = mem.reshape(n_b * N_MEM, D_MODEL)
    tgt2d = loss_target.reshape(tokens, D_MODEL)
    g_final = norm_final_g.reshape(1, D_MODEL)

    def gather_finish(flight, after, tag):
        fwd_send, fwd_recv, buf = _gather_forward(*flight[:3], after, "weights_gather_forward_" + tag,
                                                  BARRIER_IDS["gather_forward"][int(tag) - 1])
        return _gather_finish(fwd_send, fwd_recv, buf, "weights_gather_finish_" + tag)

    x1, u_all, c_all, pooled_all, ymix, h1 = _fwd_mix(
        x2d, gw_mix, norm_mix_g, conv_w, conv_dw_b, conv_ln_g, conv_ln_b, pool_w[0], pool_scale, flights[1][3],
        seq, tm_mix)
    gw_attn = gather_finish(flights[0], x1, "1")
    mem_n, kv = _fwd_kv(mem2d, gw_attn, norm_mem_g)
    x2, h2, q, o = _fwd_attn(x1, kv, gw_attn, norm_xattn_g, seq, tm_attn)
    gw_ffn = gather_finish(flights[1], x2, "2")
    uu_all, cc_all, a_all, h3, dx3, dx3b, loss_part, dg_final = _fwd_ffn(
        x2, tgt2d, gw_ffn, norm_ffn_g, ffn_w, ffn_dw_b, g_final, seq, tm_ffn)

    table = _owner_table()

    def sibling_start(names, tag):
        parts = [part[n].reshape(N_DEV, W_OFF[n][1], D_MODEL) for n in names]
        return _exchange_start(parts, 4, _to_sibling, "rs_sibling_exchange_start_" + tag, BARRIER_IDS["sibling"][tag])

    def chips_start(flight, after, tag):
        parts, landed = _exchange_wait(*flight[:4], after, 4, _to_sibling, "rs_sibling_exchange_wait_" + tag)
        sums = _chip_partial_sums(table, parts, landed, "rs_chip_partial_sums_" + tag)
        return parts, landed, _exchange_start(sums, 3, _to_chip, "rs_chip_exchange_start_" + tag,
                                              BARRIER_IDS["chips"][tag])

    grads, delta, new_m, new_v = {}, {}, {}, {}

    def reduce_finish(names, parts, landed, flight, after, tag):
        _, from_chips = _exchange_wait(*flight[:4], after, 3, _to_chip, "rs_chip_exchange_wait_" + tag)
        as_rows = {n: n in transposed and W_OFF[n][1] % LANES != 0 for n in names}
        states = [tuple(t[n][0].T if as_rows[n] else t[n][0] for t in (weights, moments_m, moments_v)) for n in names]
        results = _final_update(table, parts, landed, from_chips, states, "rs_final_update_" + tag)
        for n, res in zip(names, results):
            grads[n], delta[n], new_m[n], new_v[n] = [t.T[None] if as_rows[n] else t[None] for t in res]
        return delta[names[-1]]

    part = {}
    dx2, dx2b, duu, d_ffn_b, d_ffn_w, dg_ffn = _bwd_ffn(dx3, x2, uu_all, cc_all, gw_ffn, norm_ffn_g, ffn_w, seq, tm_ffn)
    part["w_up"] = _wgrad(duu, h3, "wgrad_w_up")
    part["w_down"] = _wgrad(a_all, dx3b, "wgrad_w_down")
    to_sibling_a = sibling_start(RS_GROUPS["a"], "a")
    dx1, dx1b, dq, dkv, dg_x = _bwd_attn(dx2, x1, q, kv, gw_attn, norm_xattn_g, to_sibling_a[4], seq, tm_mix)
    parts_a, landed_a, flight_a = chips_start(to_sibling_a, dx1, "a")
    dkv_b, dg_mem = _bwd_kv(dkv, mem2d, gw_attn)
    part["w_q"] = _wgrad(h2, dq, "wgrad_w_q", after=flight_a[4])
    part["w_kv"] = _wgrad(dkv_b, mem_n, "wgrad_w_kv", after=flight_a[4])
    part["w_out"] = _wgrad(ymix, dx1b, "wgrad_w_out", after=flight_a[4])
    to_sibling_b = sibling_start(RS_GROUPS["b"], "b")
    part["w_o"] = _wgrad(o, dx2b, "wgrad_w_o", after=to_sibling_b[4])
    parts_b, landed_b, flight_b = chips_start(to_sibling_b, part["w_o"], "b")
    dx, du, dg_mix, d_conv_w, d_conv_b, d_ln_g, d_ln_b, d_pool_w, d_pool_scale = _bwd_mix(
        dx1, x2d, u_all, c_all, pooled_all, gw_mix, norm_mix_g, conv_w, conv_ln_g, conv_ln_b, pool_w[0], pool_scale,
        flight_b[4], seq, tm_mix)
    grad_x = dx.reshape(x.shape)

    small_grads = dict(norm_mix_g=dg_mix, conv_dw_b=d_conv_b, conv_ln_g=d_ln_g, conv_ln_b=d_ln_b, pool_w=d_pool_w,
                       pool_scale=d_pool_scale, norm_xattn_g=dg_x, norm_mem_g=dg_mem, norm_ffn_g=dg_ffn,
                       ffn_dw_b=d_ffn_b, norm_final_g=dg_final)
    small_list = [small_grads[n] for n, _ in SMALL] + [d_conv_w, d_ffn_w, loss_part[:1]]
    small_mine = _pack_rows(small_list)
    small_flight = _broadcast_start(
        lax.dynamic_update_slice(lax.empty((N_DEV,) + small_mine.shape, F32), small_mine[None], (dev, 0, 0)),
        "small_grads_broadcast_start", BARRIER_IDS["broadcast"])

    part["w_in"] = _wgrad(du, h1, "wgrad_w_in", after=small_flight[3])
    to_sibling_c = sibling_start(RS_GROUPS["c"], "c")
    parts_c, landed_c, flight_c = chips_start(to_sibling_c, to_sibling_c[4], "c")
    updated_a = reduce_finish(RS_GROUPS["a"], parts_a, landed_a, flight_a, flight_c[4], "a")
    updated_b = reduce_finish(RS_GROUPS["b"], parts_b, landed_b, flight_b, updated_a, "b")
    small_all = _broadcast_wait(*small_flight[:3], updated_b, "small_grads_broadcast_wait")
    small_sum = _sum_blocks(small_all).reshape(-1)

    pos = 0
    for n, shape in SMALL:
        size = 1
        for s in shape:
            size *= s
        grads[n] = small_sum[pos:pos + size].reshape(shape)
        pos += size
    full_conv_w = small_sum[pos:pos + CONV_WIDTH * D_CONV].reshape(CONV_WIDTH, D_CONV)
    pos += CONV_WIDTH * D_CONV
    full_ffn_w = small_sum[pos:pos + FFN_CONV_WIDTH * 2 * D_FF].reshape(FFN_CONV_WIDTH, 2 * D_FF)
    loss = small_sum[pos + FFN_CONV_WIDTH * 2 * D_FF]
    grads["conv_dw_w"] = lax.dynamic_slice_in_dim(full_conv_w, dev * (D_CONV // N_DEV), D_CONV // N_DEV, axis=1)[None]
    grads["ffn_dw_w"] = lax.dynamic_slice_in_dim(full_ffn_w, dev * (2 * D_FF // N_DEV), 2 * D_FF // N_DEV, axis=1)[None]

    small_names = [n for n in order if n not in W_OFF]
    swap = lambda t: jnp.transpose(t, (1, 0, 2))
    two_d = lambda t: t.reshape(1, -1) if t.ndim == 1 else (swap(t) if t.ndim == 3 else t)
    outs = _adamw_small(*[[two_d(t[n]) for n in small_names] for t in (weights, grads, moments_m, moments_v)])
    for res, out in zip((delta, new_m, new_v), outs):
        for n, o in zip(small_names, out):
            res[n] = swap(o) if o.ndim == 3 else o.reshape(weights[n].shape)

    reduce_finish(RS_GROUPS["c"], parts_c, landed_c, flight_c, delta[small_names[-1]], "c")

    return (loss, grad_x, *[grads[n] for n in order], *[delta[n] for n in order],
            *[new_m[n] for n in order], *[new_v[n] for n in order])
```

```python
import jax
import jax.numpy as jnp
from jax import lax
from jax.experimental import pallas as pl
from jax.experimental.pallas import tpu as pltpu

F32 = jnp.float32
BF16 = jnp.bfloat16
MESH = pl.DeviceIdType.MESH

N_DEV = 8
D_MODEL = 1024
D_CONV = 512
D_POOL = 512
CONV_WIDTH = 31
POOL_WINDOWS = (2, 4, 8, 16)
POOL_GROUP_DIM = 128
D_IN = 1536
N_MEM = 256
HEADS = 4
HEAD_DIM = 256
D_FF = 2816
FFN_CONV_WIDTH = 3
EPS = 1e-6
ADAM_LR = 0.001
ADAM_B1 = 0.9
ADAM_B2 = 0.999
ADAM_EPS = 1e-08
ADAM_WD = 0.01
ADAM_STEP = 10

VMEM_LIMIT_V7X = 56 * 1024 * 1024
CONV_HALO = 32
POOL_HALO = 16
FFN_HALO = 8
FFN_CHUNK = 2816

W_ROWS = (("w_in", 192), ("w_out", 128), ("w_q", 128), ("w_kv", 256), ("w_o", 128), ("w_up", 704), ("w_down", 352))
AG_GROUPS = (("w_in", "w_out"), ("w_q", "w_kv", "w_o"), ("w_up", "w_down"))
W_OFF = {}
for _names in AG_GROUPS:
    _o = 0
    for _n in _names:
        W_OFF[_n] = (_o, dict(W_ROWS)[_n])
        _o += dict(W_ROWS)[_n]
RS_GROUPS = {"a": ("w_up", "w_down"), "b": ("w_q", "w_kv", "w_out"), "c": ("w_o", "w_in")}
BARRIER_IDS = {"gather_start": (0, 1), "gather_forward": (2, 3), "sibling": {"a": 4, "b": 5, "c": 6},
               "chips": {"a": 7, "b": 8, "c": 9}, "broadcast": 10}


def _dot(a, b):
    return jnp.dot(a, b, preferred_element_type=F32)


def _dot_nt(a, b):
    return lax.dot_general(a, b, (((1,), (1,)), ((), ())), preferred_element_type=F32)


def _dot_tn(a, b):
    return lax.dot_general(a, b, (((0,), (0,)), ((), ())), preferred_element_type=F32)


def _sigmoid(v):
    return 1.0 / (1.0 + jnp.exp(-v))


def _rms_fwd(v):
    r = lax.rsqrt(jnp.mean(v * v, axis=-1, keepdims=True) + EPS)
    return v * r, r


def _rms_bwd(dh, vh, r, g):
    gd = dh * g
    return r * (gd - vh * jnp.mean(gd * vh, axis=-1, keepdims=True))


def _sublane_shifts(v):
    rows = v.shape[0]
    return [v] + [pltpu.roll(v, rows - b, 0) for b in range(1, 8)]


def _colsum(v):
    return jnp.sum(v, axis=0, keepdims=True)


def _colsum_mxu(v):
    return _dot(jnp.ones((8, v.shape[0]), BF16), v.astype(BF16))[0:1, :]


def _full(shape):
    return pl.BlockSpec(shape, lambda *_: (0,) * len(shape))


def _params(sem=("arbitrary",), vmem=VMEM_LIMIT_V7X):
    return pltpu.CompilerParams(dimension_semantics=sem, vmem_limit_bytes=vmem)


def _load_weight(g_hbm, name, dst, sem):
    off, rows = W_OFF[name]
    return [pltpu.make_async_copy(g_hbm.at[d, pl.ds(off, rows), :], dst.at[pl.ds(d * rows, rows), :], sem)
            for d in range(N_DEV)]


def _start_weights(g_hbm, names, dsts, sems):
    @pl.when(pl.program_id(0) == 0)
    def _():
        copies = [_load_weight(g_hbm, name, dst, sems.at[k]) for k, (name, dst) in enumerate(zip(names, dsts))]
        for cp in sum(copies, []):
            cp.start()
        for cp in sum(copies, []):
            cp.wait()


def _position():
    x, y, c = lax.axis_index("x"), lax.axis_index("y"), lax.axis_index("c")
    chips = [(1 - x, y), (x, 1 - y), (1 - x, 1 - y)]
    return x, y, c, chips


def _dev(px, py, pc):
    return 4 * px + 2 * py + pc


def _all_gather(arrs, name):
    n = len(arrs)

    def body(*refs):
        ins, outs = refs[:n], refs[n:2 * n]
        send_sems, recv_sems, local_sems = refs[2 * n:2 * n + 3]
        bounce = refs[2 * n + 3:]
        x, y, c, chips = _position()
        me, sibling = (x, y, c), (x, y, 1 - c)

        def copy(a, k, block, to, src=None):
            rows = outs[a].at[_dev(*block)]
            return pltpu.make_async_remote_copy(
                src_ref=rows if src is None else src, dst_ref=rows,
                send_sem=send_sems.at[a, k], recv_sem=recv_sems.at[a, k], device_id=to, device_id_type=MESH)

        sends = []
        for a in range(n):
            first = [copy(a, 0, me, sibling, src=ins[a])]
            first += [copy(a, 1 + j, me, (*chip, c), src=ins[a]) for j, chip in enumerate(chips)]
            for cp in first:
                cp.start()
            sends += first
        started = []
        for a in range(n):
            load = pltpu.make_async_copy(ins[a], bounce[a], local_sems.at[a, 0])
            load.start()
            load.wait()
            mine = pltpu.make_async_copy(bounce[a], outs[a].at[_dev(*me)], local_sems.at[a, 1])
            mine.start()
            started.append(mine)
        for j, chip in enumerate(chips):
            for a in range(n):
                copy(a, 1 + j, (*chip, c), me).wait_recv()
                passed = copy(a, 4 + j, (*chip, c), sibling)
                passed.start()
                sends.append(passed)
        for a in range(n):
            copy(a, 0, sibling, me).wait_recv()
            for j, chip in enumerate(chips):
                copy(a, 4 + j, (*chip, 1 - c), me).wait_recv()
        for cp in sends:
            cp.wait_send()
        for mine in started:
            mine.wait()

    any_spec = pl.BlockSpec(memory_space=pl.ANY)
    return pl.pallas_call(
        body, name=name,
        out_shape=[jax.ShapeDtypeStruct((N_DEV,) + a.shape, a.dtype) for a in arrs],
        in_specs=[any_spec] * n, out_specs=[any_spec] * n,
        scratch_shapes=[pltpu.SemaphoreType.DMA((n, 7)), pltpu.SemaphoreType.DMA((n, 7)), pltpu.SemaphoreType.DMA((n, 2))]
        + [pltpu.VMEM(a.shape, a.dtype) for a in arrs],
    )(*arrs)


_HBM = pl.BlockSpec(memory_space=pltpu.HBM)
_SEM = pl.BlockSpec(memory_space=pltpu.SEMAPHORE)
_SIDE_EFFECT = pltpu.SideEffectType.DATAFLOW_SIDE_EFFECTING


def _handshake(peers):
    barrier = pltpu.get_barrier_semaphore()
    for peer in peers:
        pl.semaphore_signal(barrier, inc=1, device_id=peer, device_id_type=MESH)
    pl.semaphore_wait(barrier, len(peers))


def _gather_start(buf, after, name, collective_id):
    def body(buf_ref, after_ref, send_sems, recv_sems, buf_thru, token):
        del after_ref, buf_thru
        x, y, c, chips = _position()
        rows = buf_ref.at[_dev(x, y, c)]
        targets = [(x, y, 1 - c)] + [(*chip, c) for chip in chips]
        _handshake(targets)
        for k, to in enumerate(targets):
            pltpu.make_async_remote_copy(src_ref=rows, dst_ref=rows, send_sem=send_sems.at[k], recv_sem=recv_sems.at[k],
                                         device_id=to, device_id_type=MESH).start()
        token[...] = jnp.zeros_like(token)

    return pl.pallas_call(
        body, name=name,
        out_shape=(pltpu.SemaphoreType.DMA((4,)), pltpu.SemaphoreType.DMA((4,)), pltpu.HBM(buf.shape, buf.dtype),
                   jax.ShapeDtypeStruct((8, 128), F32)),
        in_specs=(_HBM, pl.BlockSpec(memory_space=pl.ANY)),
        out_specs=(_SEM, _SEM, _HBM, pl.BlockSpec(memory_space=pltpu.VMEM)),
        input_output_aliases={0: 2},
        compiler_params=pltpu.CompilerParams(has_side_effects=_SIDE_EFFECT, collective_id=collective_id),
    )(pltpu.with_memory_space_constraint(buf, pltpu.HBM), after)


def _gather_forward(send_sems, recv_sems, buf, after, name, collective_id):
    def body(buf_ref, send_sems, recv_sems, after_ref, fwd_send, fwd_recv, buf_thru):
        del after_ref, buf_thru
        x, y, c, chips = _position()
        sibling = (x, y, 1 - c)

        def copy(block, k, sends, recvs):
            rows = buf_ref.at[_dev(*block)]
            return pltpu.make_async_remote_copy(src_ref=rows, dst_ref=rows, send_sem=sends.at[k], recv_sem=recvs.at[k],
                                                device_id=sibling, device_id_type=MESH)

        _handshake([sibling])
        for k in range(4):
            copy((x, y, c), k, send_sems, recv_sems).wait_send()
        copy(sibling, 0, send_sems, recv_sems).wait_recv()
        for j, chip in enumerate(chips):
            copy((*chip, c), 1 + j, send_sems, recv_sems).wait_recv()
            copy((*chip, c), j, fwd_send, fwd_recv).start()

    return pl.pallas_call(
        body, name=name,
        out_shape=(pltpu.SemaphoreType.DMA((3,)), pltpu.SemaphoreType.DMA((3,)), pltpu.HBM(buf.shape, buf.dtype)),
        in_specs=(_HBM, _SEM, _SEM, pl.BlockSpec(memory_space=pl.ANY)), out_specs=(_SEM, _SEM, _HBM),
        input_output_aliases={0: 2},
        compiler_params=pltpu.CompilerParams(has_side_effects=_SIDE_EFFECT, collective_id=collective_id),
    )(buf, send_sems, recv_sems, after)


def _gather_finish(fwd_send, fwd_recv, buf, name):
    def body(buf_ref, fwd_send, fwd_recv, buf_thru):
        del buf_thru
        x, y, c, chips = _position()
        for j, chip in enumerate(chips):
            cp = pltpu.make_async_remote_copy(
                src_ref=buf_ref.at[_dev(*chip, c)], dst_ref=buf_ref.at[_dev(*chip, 1 - c)], send_sem=fwd_send.at[j],
                recv_sem=fwd_recv.at[j], device_id=(x, y, 1 - c), device_id_type=MESH)
            cp.wait_send()
            cp.wait_recv()

    return pl.pallas_call(
        body, name=name,
        out_shape=pltpu.HBM(buf.shape, buf.dtype),
        in_specs=(_HBM, _SEM, _SEM), out_specs=_HBM,
        input_output_aliases={0: 0},
        compiler_params=pltpu.CompilerParams(has_side_effects=_SIDE_EFFECT),
    )(buf, fwd_send, fwd_recv)


def _everyone_else(x, y, c, chips):
    return [(x, y, 1 - c)] + [(*chip, core) for chip in chips for core in (c, 1 - c)]


def _broadcast_start(buf, name, collective_id):
    def body(buf_ref, send_sems, recv_sems, buf_thru, token):
        del buf_thru
        x, y, c, chips = _position()
        rows = buf_ref.at[_dev(x, y, c)]
        _handshake(_everyone_else(x, y, c, chips))
        for k, to in enumerate(_everyone_else(x, y, c, chips)):
            pltpu.make_async_remote_copy(src_ref=rows, dst_ref=rows, send_sem=send_sems.at[k], recv_sem=recv_sems.at[k],
                                         device_id=to, device_id_type=MESH).start()
        token[...] = jnp.zeros_like(token)

    return pl.pallas_call(
        body, name=name,
        out_shape=(pltpu.SemaphoreType.DMA((7,)), pltpu.SemaphoreType.DMA((7,)), pltpu.HBM(buf.shape, buf.dtype),
                   jax.ShapeDtypeStruct((8, 128), F32)),
        in_specs=(_HBM,), out_specs=(_SEM, _SEM, _HBM, pl.BlockSpec(memory_space=pltpu.VMEM)),
        input_output_aliases={0: 2},
        compiler_params=pltpu.CompilerParams(has_side_effects=_SIDE_EFFECT, collective_id=collective_id),
    )(pltpu.with_memory_space_constraint(buf, pltpu.HBM))


def _broadcast_wait(send_sems, recv_sems, buf, after, name):
    def body(buf_ref, send_sems, recv_sems, after_ref, buf_thru):
        del after_ref, buf_thru
        x, y, c, chips = _position()
        for k, peer in enumerate(_everyone_else(x, y, c, chips)):
            cp = pltpu.make_async_remote_copy(
                src_ref=buf_ref.at[_dev(x, y, c)], dst_ref=buf_ref.at[_dev(*peer)], send_sem=send_sems.at[k],
                recv_sem=recv_sems.at[k], device_id=peer, device_id_type=MESH)
            cp.wait_send()
            cp.wait_recv()

    return pl.pallas_call(
        body, name=name,
        out_shape=pltpu.HBM(buf.shape, buf.dtype),
        in_specs=(_HBM, _SEM, _SEM, pl.BlockSpec(memory_space=pl.ANY)), out_specs=_HBM,
        input_output_aliases={0: 0},
        compiler_params=pltpu.CompilerParams(has_side_effects=_SIDE_EFFECT),
    )(buf, send_sems, recv_sems, after)


def _to_sibling(j, x, y, c, chips):
    return _dev(*([(x, y)] + chips)[j], 1 - c), (x, y, 1 - c)


def _to_chip(j, x, y, c, chips):
    return j, (*chips[j], c)


def _exchange_start(srcs, n_slots, route, name, collective_id):
    n = len(srcs)

    def body(*refs):
        s_refs, land_refs = refs[:n], refs[n:2 * n]
        send_sems, recv_sems = refs[2 * n:2 * n + 2]
        token = refs[-1]
        x, y, c, chips = _position()
        _handshake([(x, y, 1 - c)] if route is _to_sibling else [route(j, x, y, c, chips)[1] for j in range(n_slots)])
        for k in range(n):
            for j in range(n_slots):
                block, to = route(j, x, y, c, chips)
                pltpu.make_async_remote_copy(
                    src_ref=s_refs[k].at[block], dst_ref=land_refs[k].at[j], send_sem=send_sems.at[n_slots * k + j],
                    recv_sem=recv_sems.at[n_slots * k + j], device_id=to, device_id_type=MESH).start()
        token[...] = jnp.zeros_like(token)

    lands = [jax.ShapeDtypeStruct((n_slots,) + s.shape[1:], s.dtype) for s in srcs]
    outs = pl.pallas_call(
        body, name=name,
        out_shape=(pltpu.SemaphoreType.DMA((n_slots * n,)), pltpu.SemaphoreType.DMA((n_slots * n,)),
                   *[pltpu.HBM(s.shape, s.dtype) for s in srcs], *[pltpu.HBM(l.shape, l.dtype) for l in lands],
                   jax.ShapeDtypeStruct((8, 128), F32)),
        in_specs=[_HBM] * (2 * n), out_specs=(_SEM, _SEM, *[_HBM] * (2 * n), pl.BlockSpec(memory_space=pltpu.VMEM)),
        input_output_aliases={k: 2 + k for k in range(2 * n)},
        compiler_params=pltpu.CompilerParams(has_side_effects=_SIDE_EFFECT, collective_id=collective_id),
    )(*[pltpu.with_memory_space_constraint(s, pltpu.HBM) for s in srcs],
      *[pltpu.with_memory_space_constraint(lax.empty(l.shape, l.dtype), pltpu.HBM) for l in lands])
    return outs[0], outs[1], outs[2:2 + n], outs[2 + n:2 + 2 * n], outs[-1]


def _exchange_wait(send_sems, recv_sems, s_thru, land_thru, after, n_slots, route, name):
    n = len(s_thru)

    def body(*refs):
        s_refs, land_refs = refs[:n], refs[n:2 * n]
        send_sems, recv_sems = refs[2 * n:2 * n + 2]
        x, y, c, chips = _position()
        for k in range(n):
            for j in range(n_slots):
                block, to = route(j, x, y, c, chips)
                cp = pltpu.make_async_remote_copy(
                    src_ref=s_refs[k].at[block], dst_ref=land_refs[k].at[j], send_sem=send_sems.at[n_slots * k + j],
                    recv_sem=recv_sems.at[n_slots * k + j], device_id=to, device_id_type=MESH)
                cp.wait_send()
                cp.wait_recv()

    outs = pl.pallas_call(
        body, name=name,
        out_shape=(*[pltpu.HBM(s.shape, s.dtype) for s in s_thru], *[pltpu.HBM(l.shape, l.dtype) for l in land_thru]),
        in_specs=[_HBM] * (2 * n) + [_SEM, _SEM, pl.BlockSpec(memory_space=pl.ANY)], out_specs=[_HBM] * (2 * n),
        input_output_aliases={k: k for k in range(2 * n)},
        compiler_params=pltpu.CompilerParams(has_side_effects=_SIDE_EFFECT),
    )(*s_thru, *land_thru, send_sems, recv_sems, after)
    return outs[:n], outs[n:]


def _owner_table():
    x, y, c = lax.axis_index("x"), lax.axis_index("y"), lax.axis_index("c")
    chips = [(x, y), (1 - x, y), (x, 1 - y), (1 - x, 1 - y)]
    return jnp.stack([_dev(px, py, c) for px, py in chips]).astype(jnp.int32)


def _chip_partial_sums(table, parts, from_sibling, name):
    n = len(parts)

    def body(tab_ref, *refs):
        del tab_ref
        for g_ref, l_ref, out_ref in zip(refs[:n], refs[n:2 * n], refs[2 * n:]):
            out_ref[...] = (g_ref[...].astype(F32) + l_ref[...].astype(F32)).astype(out_ref.dtype)

    block = lambda p: (None,) + p.shape[1:]
    grid_spec = pltpu.PrefetchScalarGridSpec(
        num_scalar_prefetch=1, grid=(3,),
        in_specs=[pl.BlockSpec(block(p), lambda j, tab: (tab[j + 1], 0, 0)) for p in parts]
        + [pl.BlockSpec(block(p), lambda j, tab: (j + 1, 0, 0)) for p in parts],
        out_specs=[pl.BlockSpec(block(p), lambda j, tab: (j, 0, 0)) for p in parts])
    return pl.pallas_call(
        body, name=name, grid_spec=grid_spec,
        out_shape=[jax.ShapeDtypeStruct((3,) + p.shape[1:], BF16) for p in parts],
        compiler_params=_params(("arbitrary",)),
    )(table, *parts, *from_sibling)


def _final_update(table, parts, from_sibling, from_chips, states, name):
    n = len(parts)
    flipped = [states[k][0].shape != parts[k].shape[1:] for k in range(n)]

    def body(tab_ref, *refs):
        del tab_ref
        ins, outs = refs[:6 * n], refs[6 * n:]
        for k in range(n):
            acc = ins[k][...].astype(F32) + ins[n + k][...].astype(F32)
            for j in range(3):
                acc = acc + ins[2 * n + k][j].astype(F32)
            if flipped[k]:
                acc = acc.T
            w_ref, m_ref, v_ref = ins[3 * n + 3 * k:3 * n + 3 * k + 3]
            outs[4 * k][...] = acc
            for out_ref, val in zip(outs[4 * k + 1:4 * k + 4], _adamw_update(w_ref[...], acc, m_ref[...], v_ref[...])):
                out_ref[...] = val

    def grad_block(k, lead, at):
        r, c = parts[k].shape[1:]
        if flipped[k]:
            return pl.BlockSpec(lead + (r, c // 2), lambda t, tab: (*at(tab), 0, t))
        return pl.BlockSpec(lead + (r // 2, c), lambda t, tab: (*at(tab), t, 0))

    def state_block(k):
        a, b = states[k][0].shape
        return pl.BlockSpec((a // 2, b), lambda t, tab: (t, 0))

    grid_spec = pltpu.PrefetchScalarGridSpec(
        num_scalar_prefetch=1, grid=(2,),
        in_specs=[grad_block(k, (None,), lambda tab: (tab[0],)) for k in range(n)]
        + [grad_block(k, (None,), lambda tab: (0,)) for k in range(n)]
        + [grad_block(k, (3,), lambda tab: (0,)) for k in range(n)]
        + [state_block(k) for k in range(n) for _ in range(3)],
        out_specs=[state_block(k) for k in range(n) for _ in range(4)])
    outs = pl.pallas_call(
        body, name=name, grid_spec=grid_spec,
        out_shape=[jax.ShapeDtypeStruct(states[k][0].shape, F32) for k in range(n) for _ in range(4)],
        compiler_params=_params(("arbitrary",)),
    )(table, *parts, *from_sibling, *from_chips, *[t for k in range(n) for t in states[k]])
    return [outs[4 * k:4 * k + 4] for k in range(n)]


def _sum_blocks(g8):
    _, rows, cols = g8.shape

    def body(g_ref, out_ref):
        acc = g_ref[0]
        for d in range(1, N_DEV):
            acc = acc + g_ref[d]
        out_ref[...] = acc

    return pl.pallas_call(
        body, name="small_grad_sum", grid=(1,),
        in_specs=[_full((N_DEV, rows, cols))], out_specs=_full((rows, cols)),
        out_shape=jax.ShapeDtypeStruct((rows, cols), F32),
        compiler_params=_params(("arbitrary",)),
    )(g8)


def _fwd_mix(x2d, gw, g_mix, conv_w, conv_b, ln_g, ln_b, pool_w, pool_scale, after, seq, tm):
    tokens = x2d.shape[0]
    n_tiles = tokens // tm
    tps = seq // tm

    def body(x_ref, gmix_ref, gw_hbm, cw_ref, cb_ref, lng_ref, lnb_ref, pw_ref, ps_ref, after_ref,
             x1_ref, u_ref, c_ref, pooled_ref, ymix_ref, h1_ref,
             win_v, wout_v, hc_carry, up_carry, sem):
        del after_ref
        i = pl.program_id(0)

        _start_weights(gw_hbm, ("w_in", "w_out"), (win_v, wout_v), sem)

        @pl.when(i % tps == 0)
        def _():
            hc_carry[...] = jnp.zeros_like(hc_carry)
            up_carry[...] = jnp.zeros_like(up_carry)

        x = x_ref[...]
        xh, _ = _rms_fwd(x)
        h1 = (xh * gmix_ref[...]).astype(BF16)
        h1_ref[...] = h1
        u = _dot_nt(h1, win_v[...])
        u_ref[...] = u
        val, gate, up = u[:, :D_CONV], u[:, D_CONV:2 * D_CONV], u[:, 2 * D_CONV:]

        extp = jnp.concatenate([up_carry[...], up], axis=0)
        up_carry[...] = up[tm - POOL_HALO:, :]
        pos = lax.broadcasted_iota(jnp.int32, (tm, 1), 0) + (i % tps) * tm
        run = extp
        mixed = []
        for g, w in enumerate(POOL_WINDOWS):
            lo = g * POOL_GROUP_DIM
            run = run[:, POOL_GROUP_DIM if g else 0:]
            run = run + pltpu.roll(run, w // 2, 0)
            cnt = jnp.minimum(pos + 1, w).astype(F32)
            pooled = run[POOL_HALO:, :POOL_GROUP_DIM] / cnt - up[:, lo:lo + POOL_GROUP_DIM]
            pooled = pooled.astype(BF16)
            pooled_ref[:, lo:lo + POOL_GROUP_DIM] = pooled
            mixed.append(_dot(pooled, pw_ref[g].astype(BF16)))
        y_pool = jnp.concatenate(mixed, axis=-1) * ps_ref[...]
        y_pool = y_pool.astype(BF16)
        ymix_ref[:, D_CONV:] = y_pool
        out = _dot(y_pool, wout_v[D_CONV:, :])

        hc = val * _sigmoid(gate)
        ext = jnp.concatenate([hc_carry[...], hc], axis=0)
        hc_carry[...] = hc[tm - CONV_HALO:, :]
        conv = jnp.broadcast_to(cb_ref[...], (tm, D_CONV))
        ahead_by = _sublane_shifts(ext)
        for k in range(CONV_WIDTH):
            whole, part = divmod(CONV_HALO - (CONV_WIDTH - 1) + k, 8)
            conv = conv + cw_ref[k:k + 1, :] * ahead_by[part][8 * whole:8 * whole + tm, :]
        c_ref[...] = conv
        mu = jnp.mean(conv, axis=-1, keepdims=True)
        cen = conv - mu
        ln = cen * lax.rsqrt(jnp.mean(cen * cen, axis=-1, keepdims=True) + EPS) * lng_ref[...] + lnb_ref[...]
        y_conv = ln * _sigmoid(ln)
        y_conv = y_conv.astype(BF16)
        ymix_ref[:, :D_CONV] = y_conv
        x1_ref[...] = x + (out + _dot(y_conv, wout_v[:D_CONV, :]))

    row = lambda w: pl.BlockSpec((tm, w), lambda i: (i, 0))
    return pl.pallas_call(
        body, name="fwd_mix", grid=(n_tiles,),
        in_specs=[row(D_MODEL), _full((1, D_MODEL)), pl.BlockSpec(memory_space=pl.ANY),
                  _full((CONV_WIDTH, D_CONV)), _full((1, D_CONV)), _full((1, D_CONV)), _full((1, D_CONV)),
                  _full((4, POOL_GROUP_DIM, POOL_GROUP_DIM)), _full((1, D_POOL)), _full(after.shape)],
        out_specs=[row(D_MODEL), row(D_IN), row(D_CONV), row(D_POOL), row(D_MODEL), row(D_MODEL)],
        out_shape=[jax.ShapeDtypeStruct((tokens, D_MODEL), F32), jax.ShapeDtypeStruct((tokens, D_IN), F32),
                   jax.ShapeDtypeStruct((tokens, D_CONV), F32), jax.ShapeDtypeStruct((tokens, D_POOL), BF16),
                   jax.ShapeDtypeStruct((tokens, D_MODEL), BF16), jax.ShapeDtypeStruct((tokens, D_MODEL), BF16)],
        scratch_shapes=[pltpu.VMEM((D_IN, D_MODEL), BF16), pltpu.VMEM((D_MODEL, D_MODEL), BF16),
                        pltpu.VMEM((CONV_HALO, D_CONV), F32), pltpu.VMEM((POOL_HALO, D_POOL), F32),
                        pltpu.SemaphoreType.DMA((2,))],
        compiler_params=_params(),
    )(x2d, g_mix, gw, conv_w, conv_b, ln_g, ln_b, pool_w, pool_scale, after)


def _fwd_kv(mem2d, gw, g_mem):
    rows = mem2d.shape[0]
    n_b = rows // N_MEM

    def body(mem_ref, g_ref, gw_hbm, mn_ref, kv_ref, wkv_v, sem):
        @pl.when(pl.program_id(0) == 0)
        def _():
            copies = _load_weight(gw_hbm, "w_kv", wkv_v, sem)
            for cp in copies:
                cp.start()
            for cp in copies:
                cp.wait()

        mh, _ = _rms_fwd(mem_ref[...])
        mn = (mh * g_ref[...]).astype(BF16)
        mn_ref[...] = mn
        kv_ref[...] = _dot_nt(mn, wkv_v[...]).astype(BF16)

    return pl.pallas_call(
        body, name="fwd_kv", grid=(n_b,),
        in_specs=[pl.BlockSpec((N_MEM, D_MODEL), lambda b: (b, 0)), _full((1, D_MODEL)), pl.BlockSpec(memory_space=pl.ANY)],
        out_specs=[pl.BlockSpec((N_MEM, D_MODEL), lambda b: (b, 0)), pl.BlockSpec((N_MEM, 2 * D_MODEL), lambda b: (b, 0))],
        out_shape=[jax.ShapeDtypeStruct((rows, D_MODEL), BF16), jax.ShapeDtypeStruct((rows, 2 * D_MODEL), BF16)],
        scratch_shapes=[pltpu.VMEM((2 * D_MODEL, D_MODEL), BF16), pltpu.SemaphoreType.DMA],
        compiler_params=_params(),
    )(mem2d, g_mem, gw)


def _softmax_rows(s):
    e = jnp.exp(s - jnp.max(s, axis=-1, keepdims=True))
    return e / jnp.sum(e, axis=-1, keepdims=True)


def _fwd_attn(x1, kv, gw, g_x, seq, tm):
    tokens = x1.shape[0]
    n_tiles = tokens // tm
    tps = seq // tm

    def body(x1_ref, kv_ref, g_ref, gw_hbm, x2_ref, h2_ref, q_ref, o_ref, wq_v, wo_v, sem):
        _start_weights(gw_hbm, ("w_q", "w_o"), (wq_v, wo_v), sem)
        x1v = x1_ref[...]
        xh, _ = _rms_fwd(x1v)
        h2 = (xh * g_ref[...]).astype(BF16)
        h2_ref[...] = h2
        q = (_dot(h2, wq_v[...]) * (HEAD_DIM ** -0.5)).astype(BF16)
        q_ref[...] = q
        heads = [slice(h * HEAD_DIM, (h + 1) * HEAD_DIM) for h in range(HEADS)]
        scores = [_dot_nt(q[:, hd], kv_ref[:, hd]) for hd in heads]
        probs = [_softmax_rows(s).astype(BF16) for s in scores]
        outs = [_dot(p, kv_ref[:, pl.ds(D_MODEL + h * HEAD_DIM, HEAD_DIM)]) for h, p in enumerate(probs)]
        o = jnp.concatenate(outs, axis=-1).astype(BF16)
        o_ref[...] = o
        x2_ref[...] = x1v + _dot(o, wo_v[...])

    row = lambda w: pl.BlockSpec((tm, w), lambda i: (i, 0))
    return pl.pallas_call(
        body, name="fwd_attn", grid=(n_tiles,),
        in_specs=[row(D_MODEL), pl.BlockSpec((N_MEM, 2 * D_MODEL), lambda i: (i // tps, 0)), _full((1, D_MODEL)),
                  pl.BlockSpec(memory_space=pl.ANY)],
        out_specs=[row(D_MODEL)] * 4,
        out_shape=[jax.ShapeDtypeStruct((tokens, D_MODEL), F32)] + [jax.ShapeDtypeStruct((tokens, D_MODEL), BF16)] * 3,
        scratch_shapes=[pltpu.VMEM((D_MODEL, D_MODEL), BF16), pltpu.VMEM((D_MODEL, D_MODEL), BF16), pltpu.SemaphoreType.DMA((2,))],
        compiler_params=_params(),
    )(x1, kv, g_x, gw)


def _ffn_conv(uu, halo, w_ref, b_ref, cols):
    ext = jnp.concatenate([halo, uu], axis=0)
    p1 = pltpu.roll(ext, 1, 0)[FFN_HALO:, :]
    p2 = pltpu.roll(ext, 2, 0)[FFN_HALO:, :]
    return b_ref[:, cols] + w_ref[2:3, cols] * uu + w_ref[1:2, cols] * p1 + w_ref[0:1, cols] * p2


def _fwd_ffn(x2, target, gw, g_ffn, ffn_w, ffn_b, g_final, seq, tm):
    tokens = x2.shape[0]
    n_tiles = tokens // tm
    tps = seq // tm
    n_chunks = D_FF // FFN_CHUNK

    def body(x2_ref, tgt_ref, gffn_ref, gw_hbm, fw_ref, fb_ref, gfin_ref,
             uu_ref, cc_ref, a_ref, h3_ref, dx3_ref, dx3b_ref, loss_ref, dgfin_ref,
             wup_v, wdown_v, carry, sem):
        i = pl.program_id(0)

        _start_weights(gw_hbm, ("w_up", "w_down"), (wup_v, wdown_v), sem)

        @pl.when(i == 0)
        def _():
            loss_ref[...] = jnp.zeros_like(loss_ref)
            dgfin_ref[...] = jnp.zeros_like(dgfin_ref)

        @pl.when(i % tps == 0)
        def _():
            carry[...] = jnp.zeros_like(carry)

        x2v = x2_ref[...]
        xh, _ = _rms_fwd(x2v)
        h3 = (xh * gffn_ref[...]).astype(BF16)
        h3_ref[...] = h3
        acc = jnp.zeros((tm, D_MODEL), F32)
        for jc in range(n_chunks):
            halves = []
            for half in range(2):
                cols = pl.ds(half * D_FF + jc * FFN_CHUNK, FFN_CHUNK)
                uu = _dot_nt(h3, wup_v[cols, :])
                uu_ref[:, cols] = uu.astype(BF16)
                cc = _ffn_conv(uu, carry[:, cols], fw_ref, fb_ref, cols)
                cc_ref[:, cols] = cc.astype(BF16)
                halves.append(cc)
                carry[:, cols] = uu[tm - FFN_HALO:, :]
            gate, val = halves
            a = (gate * _sigmoid(gate) * val).astype(BF16)
            a_ref[:, pl.ds(jc * FFN_CHUNK, FFN_CHUNK)] = a
            acc = acc + _dot(a, wdown_v[pl.ds(jc * FFN_CHUNK, FFN_CHUNK), :])
        x3 = x2v + acc

        xh3, r3 = _rms_fwd(x3)
        gfin = gfin_ref[...]
        err = xh3 * gfin - tgt_ref[...]
        loss_ref[...] += jnp.full(loss_ref.shape, jnp.sum(err * err) * (0.5 / D_MODEL), F32)
        dy = err * (1.0 / D_MODEL)
        dgfin_ref[...] += _colsum(dy * xh3)
        dx3 = _rms_bwd(dy, xh3, r3, gfin)
        dx3_ref[...] = dx3
        dx3b_ref[...] = dx3.astype(BF16)

    row = lambda w: pl.BlockSpec((tm, w), lambda i: (i, 0))
    return pl.pallas_call(
        body, name="fwd_ffn", grid=(n_tiles,),
        in_specs=[row(D_MODEL), row(D_MODEL), _full((1, D_MODEL)), pl.BlockSpec(memory_space=pl.ANY),
                  _full((FFN_CONV_WIDTH, 2 * D_FF)), _full((1, 2 * D_FF)), _full((1, D_MODEL))],
        out_specs=[row(2 * D_FF), row(2 * D_FF), row(D_FF), row(D_MODEL), row(D_MODEL), row(D_MODEL), _full((8, 128)),
                   _full((1, D_MODEL))],
        out_shape=[jax.ShapeDtypeStruct((tokens, 2 * D_FF), BF16), jax.ShapeDtypeStruct((tokens, 2 * D_FF), BF16),
                   jax.ShapeDtypeStruct((tokens, D_FF), BF16),
                   jax.ShapeDtypeStruct((tokens, D_MODEL), BF16), jax.ShapeDtypeStruct((tokens, D_MODEL), F32),
                   jax.ShapeDtypeStruct((tokens, D_MODEL), BF16),
                   jax.ShapeDtypeStruct((8, 128), F32), jax.ShapeDtypeStruct((1, D_MODEL), F32)],
        scratch_shapes=[pltpu.VMEM((2 * D_FF, D_MODEL), BF16), pltpu.VMEM((D_FF, D_MODEL), BF16),
                        pltpu.VMEM((FFN_HALO, 2 * D_FF), F32), pltpu.SemaphoreType.DMA((2,))],
        compiler_params=_params(),
    )(x2, target, g_ffn, gw, ffn_w, ffn_b, g_final)


def _bwd_ffn(dx3, x2, uu_all, cc_all, gw, g_ffn, ffn_w, seq, tm):
    tokens = x2.shape[0]
    n_tiles = tokens // tm
    tps = seq // tm
    n_chunks = D_FF // FFN_CHUNK

    def body(dx3_ref, x2_ref, uu_ref, cc_ref, gffn_ref, gw_hbm, fw_ref,
             dx2_ref, dx2b_ref, duu_ref, dfb_ref, dfw_ref, dg_ref,
             wup_v, wdown_v, carry, sem):
        i = pl.program_id(0)
        t = n_tiles - 1 - i

        _start_weights(gw_hbm, ("w_down", "w_up"), (wdown_v, wup_v), sem)

        @pl.when(i == 0)
        def _():
            dfb_ref[...] = jnp.zeros_like(dfb_ref)
            dfw_ref[...] = jnp.zeros_like(dfw_ref)
            dg_ref[...] = jnp.zeros_like(dg_ref)

        @pl.when(t % tps == tps - 1)
        def _():
            carry[...] = jnp.zeros_like(carry)

        dx3v = dx3_ref[...]
        dx3b = dx3v.astype(BF16)
        dh3 = jnp.zeros((tm, D_MODEL), F32)
        for jc in range(n_chunks):
            da = _dot_nt(dx3b, wdown_v[pl.ds(jc * FFN_CHUNK, FFN_CHUNK), :])
            colss = [pl.ds(half * D_FF + jc * FFN_CHUNK, FFN_CHUNK) for half in range(2)]
            gate, val = [cc_ref[:, cols].astype(F32) for cols in colss]
            sg = _sigmoid(gate)
            dgate = da * val * (sg * (1.0 + gate * (1.0 - sg)))
            dval = da * (gate * sg)
            for dcc, cols in zip((dgate, dval), colss):
                uu = uu_ref[:, cols].astype(F32)
                dfb_ref[:, cols] += _colsum(dcc)
                ext = jnp.concatenate([dcc, carry[:, cols]], axis=0)
                carry[:, cols] = dcc[:FFN_HALO, :]
                n1 = pltpu.roll(ext, tm + FFN_HALO - 1, 0)[:tm, :]
                n2 = pltpu.roll(ext, tm + FFN_HALO - 2, 0)[:tm, :]
                duu = fw_ref[2:3, cols] * dcc + fw_ref[1:2, cols] * n1 + fw_ref[0:1, cols] * n2
                dfw_ref[2:3, cols] += _colsum(uu * dcc)
                dfw_ref[1:2, cols] += _colsum(uu * n1)
                dfw_ref[0:1, cols] += _colsum(uu * n2)
                duub = duu.astype(BF16)
                duu_ref[:, cols] = duub
                dh3 = dh3 + _dot(duub, wup_v[cols, :])
        xh, r = _rms_fwd(x2_ref[...])
        dg_ref[...] += _colsum(dh3 * xh)
        dx2 = dx3v + _rms_bwd(dh3, xh, r, gffn_ref[...])
        dx2_ref[...] = dx2
        dx2b_ref[...] = dx2.astype(BF16)

    rev = lambda w: pl.BlockSpec((tm, w), lambda i: (n_tiles - 1 - i, 0))
    return pl.pallas_call(
        body, name="bwd_ffn", grid=(n_tiles,),
        in_specs=[rev(D_MODEL), rev(D_MODEL), rev(2 * D_FF), rev(2 * D_FF), _full((1, D_MODEL)),
                  pl.BlockSpec(memory_space=pl.ANY), _full((FFN_CONV_WIDTH, 2 * D_FF))],
        out_specs=[rev(D_MODEL), rev(D_MODEL), rev(2 * D_FF), _full((1, 2 * D_FF)), _full((FFN_CONV_WIDTH, 2 * D_FF)),
                   _full((1, D_MODEL))],
        out_shape=[jax.ShapeDtypeStruct((tokens, D_MODEL), F32), jax.ShapeDtypeStruct((tokens, D_MODEL), BF16),
                   jax.ShapeDtypeStruct((tokens, 2 * D_FF), BF16),
                   jax.ShapeDtypeStruct((1, 2 * D_FF), F32), jax.ShapeDtypeStruct((FFN_CONV_WIDTH, 2 * D_FF), F32),
                   jax.ShapeDtypeStruct((1, D_MODEL), F32)],
        scratch_shapes=[pltpu.VMEM((2 * D_FF, D_MODEL), BF16), pltpu.VMEM((D_FF, D_MODEL), BF16),
                        pltpu.VMEM((FFN_HALO, 2 * D_FF), F32), pltpu.SemaphoreType.DMA((2,))],
        compiler_params=_params(),
    )(dx3, x2, uu_all, cc_all, g_ffn, gw, ffn_w)


def _bwd_attn(dx2, x1, q, kv, gw, g_x, after, seq, tm):
    tokens = x1.shape[0]
    n_tiles = tokens // tm
    tps = seq // tm
    n_b = tokens // seq

    def body(dx2_ref, x1_ref, q_ref, kv_ref, g_ref, gw_hbm, after_ref, dx1_ref, dx1b_ref, dq_ref, dkv_ref, dg_ref,
             wq_v, wo_v, sem):
        del after_ref
        i = pl.program_id(0)

        _start_weights(gw_hbm, ("w_o", "w_q"), (wo_v, wq_v), sem)

        @pl.when(i == 0)
        def _():
            dg_ref[...] = jnp.zeros_like(dg_ref)

        @pl.when(i % tps == 0)
        def _():
            dkv_ref[...] = jnp.zeros_like(dkv_ref)

        dx2v = dx2_ref[...]
        do = _dot_nt(dx2v.astype(BF16), wo_v[...]).astype(BF16)
        q = q_ref[...]
        heads = [slice(h * HEAD_DIM, (h + 1) * HEAD_DIM) for h in range(HEADS)]
        kcols = [pl.ds(h * HEAD_DIM, HEAD_DIM) for h in range(HEADS)]
        vcols = [pl.ds(D_MODEL + h * HEAD_DIM, HEAD_DIM) for h in range(HEADS)]
        scores = [_dot_nt(q[:, hd], kv_ref[:, kc]) for hd, kc in zip(heads, kcols)]
        dps = [_dot_nt(do[:, hd], kv_ref[:, vc]) for hd, vc in zip(heads, vcols)]
        probs = [_softmax_rows(s) for s in scores]
        dss = [(p * (dp - jnp.sum(dp * p, axis=-1, keepdims=True))).astype(BF16) for p, dp in zip(probs, dps)]
        for p, hd, vc in zip(probs, heads, vcols):
            dkv_ref[:, vc] += _dot_tn(p.astype(BF16), do[:, hd])
        dqs = [_dot(ds, kv_ref[:, kc]) * (HEAD_DIM ** -0.5) for ds, kc in zip(dss, kcols)]
        for ds, hd, kc in zip(dss, heads, kcols):
            dkv_ref[:, kc] += _dot_tn(ds, q[:, hd])
        dq = jnp.concatenate(dqs, axis=-1).astype(BF16)
        dq_ref[...] = dq
        dh2 = _dot_nt(dq, wq_v[...])
        xh, r = _rms_fwd(x1_ref[...])
        dg_ref[...] += _colsum(dh2 * xh)
        dx1 = dx2v + _rms_bwd(dh2, xh, r, g_ref[...])
        dx1_ref[...] = dx1
        dx1b_ref[...] = dx1.astype(BF16)

    row = lambda w: pl.BlockSpec((tm, w), lambda i: (i, 0))
    per_b = pl.BlockSpec((N_MEM, 2 * D_MODEL), lambda i: (i // tps, 0))
    return pl.pallas_call(
        body, name="bwd_attn", grid=(n_tiles,),
        in_specs=[row(D_MODEL), row(D_MODEL), row(D_MODEL), per_b, _full((1, D_MODEL)), pl.BlockSpec(memory_space=pl.ANY),
                  _full(after.shape)],
        out_specs=[row(D_MODEL), row(D_MODEL), row(D_MODEL), per_b, _full((1, D_MODEL))],
        out_shape=[jax.ShapeDtypeStruct((tokens, D_MODEL), F32), jax.ShapeDtypeStruct((tokens, D_MODEL), BF16),
                   jax.ShapeDtypeStruct((tokens, D_MODEL), BF16),
                   jax.ShapeDtypeStruct((n_b * N_MEM, 2 * D_MODEL), F32), jax.ShapeDtypeStruct((1, D_MODEL), F32)],
        scratch_shapes=[pltpu.VMEM((D_MODEL, D_MODEL), BF16), pltpu.VMEM((D_MODEL, D_MODEL), BF16), pltpu.SemaphoreType.DMA((2,))],
        compiler_params=_params(),
    )(dx2, x1, q, kv, g_x, gw, after)


def _bwd_kv(dkv, mem2d, gw):
    rows = mem2d.shape[0]
    n_b = rows // N_MEM

    def body(dkv_ref, mem_ref, gw_hbm, dkvb_ref, dg_ref, wkv_v, sem):
        @pl.when(pl.program_id(0) == 0)
        def _():
            copies = _load_weight(gw_hbm, "w_kv", wkv_v, sem)
            for cp in copies:
                cp.start()
            for cp in copies:
                cp.wait()
            dg_ref[...] = jnp.zeros_like(dg_ref)

        dkvb = dkv_ref[...].astype(BF16)
        dkvb_ref[...] = dkvb
        dmn = _dot(dkvb, wkv_v[...])
        mh, _ = _rms_fwd(mem_ref[...])
        dg_ref[...] += _colsum(dmn * mh)

    return pl.pallas_call(
        body, name="bwd_kv", grid=(n_b,),
        in_specs=[pl.BlockSpec((N_MEM, 2 * D_MODEL), lambda b: (b, 0)), pl.BlockSpec((N_MEM, D_MODEL), lambda b: (b, 0)),
                  pl.BlockSpec(memory_space=pl.ANY)],
        out_specs=[pl.BlockSpec((N_MEM, 2 * D_MODEL), lambda b: (b, 0)), _full((1, D_MODEL))],
        out_shape=[jax.ShapeDtypeStruct((rows, 2 * D_MODEL), BF16), jax.ShapeDtypeStruct((1, D_MODEL), F32)],
        scratch_shapes=[pltpu.VMEM((2 * D_MODEL, D_MODEL), BF16), pltpu.SemaphoreType.DMA],
        compiler_params=_params(),
    )(dkv, mem2d, gw)


def _bwd_mix(dx1, x2d, u_all, c_all, pooled_all, gw, g_mix, conv_w, ln_g, ln_b, pool_w, pool_scale, after, seq, tm):
    tokens = x2d.shape[0]
    n_tiles = tokens // tm
    tps = seq // tm

    def body(dx1_ref, x_ref, u_ref, c_ref, pooled_ref, gmix_ref, gw_hbm, cw_ref, lng_ref, lnb_ref, pw_ref, ps_ref,
             after_ref, dx_ref, du_ref, dgmix_ref, dcw_ref, dcb_ref, dlng_ref, dlnb_ref, dpw_ref, dps_ref,
             win_v, wout_v, dc_carry, e_carry, sem):
        del after_ref
        i = pl.program_id(0)
        t = n_tiles - 1 - i

        _start_weights(gw_hbm, ("w_out", "w_in"), (wout_v, win_v), sem)

        @pl.when(i == 0)
        def _():
            for ref in (dgmix_ref, dcw_ref, dcb_ref, dlng_ref, dlnb_ref, dpw_ref, dps_ref):
                ref[...] = jnp.zeros_like(ref)

        @pl.when(t % tps == tps - 1)
        def _():
            dc_carry[...] = jnp.zeros_like(dc_carry)
            e_carry[...] = jnp.zeros_like(e_carry)

        dx1v = dx1_ref[...]
        dymix = _dot_nt(dx1v.astype(BF16), wout_v[...])
        dyc, dyp = dymix[:, :D_CONV], dymix[:, D_CONV:]
        u = u_ref[...]
        val, gate = u[:, :D_CONV], u[:, D_CONV:2 * D_CONV]

        conv = c_ref[...]
        mu = jnp.mean(conv, axis=-1, keepdims=True)
        cen = conv - mu
        rs = lax.rsqrt(jnp.mean(cen * cen, axis=-1, keepdims=True) + EPS)
        chat = cen * rs
        ln = chat * lng_ref[...] + lnb_ref[...]
        sl = _sigmoid(ln)
        dln = dyc * (sl * (1.0 + ln * (1.0 - sl)))
        dlng_ref[...] += _colsum(dln * chat)
        dlnb_ref[...] += _colsum(dln)
        dchat = dln * lng_ref[...]
        dc = rs * (dchat - jnp.mean(dchat, axis=-1, keepdims=True)
                   - chat * jnp.mean(dchat * chat, axis=-1, keepdims=True))
        dcb_ref[...] += _colsum(dc)
        sg = _sigmoid(gate)
        hc = val * sg
        ext = jnp.concatenate([dc, dc_carry[...]], axis=0)
        dc_carry[...] = dc[:CONV_HALO, :]
        dhc = jnp.zeros((tm, D_CONV), F32)
        ahead_by = _sublane_shifts(ext)
        for k in range(CONV_WIDTH):
            whole, part = divmod(CONV_WIDTH - 1 - k, 8)
            tap = ahead_by[part][8 * whole:8 * whole + tm, :]
            dhc = dhc + cw_ref[k:k + 1, :] * tap
            dcw_ref[k:k + 1, :] += _colsum_mxu(hc * tap)
        du_ref[:, :D_CONV] = (dhc * sg).astype(BF16)
        du_ref[:, D_CONV:2 * D_CONV] = (dhc * val * (sg * (1.0 - sg))).astype(BF16)

        pos = lax.broadcasted_iota(jnp.int32, (tm, 1), 0) + (t % tps) * tm
        es, dpooled = [], []
        for g, w in enumerate(POOL_WINDOWS):
            cols = pl.ds(g * POOL_GROUP_DIM, POOL_GROUP_DIM)
            lo = g * POOL_GROUP_DIM
            pooled = pooled_ref[:, cols]
            pw = pw_ref[g].astype(BF16)
            dyg = dyp[:, lo:lo + POOL_GROUP_DIM]
            dps_ref[:, cols] += _colsum(dyg * _dot(pooled, pw))
            dmixed = (dyg * ps_ref[:, cols]).astype(BF16)
            dpw_ref[g] += _dot_tn(pooled, dmixed)
            dpo = _dot_nt(dmixed, pw)
            dpooled.append(dpo)
            es.append(dpo / jnp.minimum(pos + 1, w).astype(F32))
        e = jnp.concatenate(es, axis=-1)
        run = jnp.concatenate([e, e_carry[...]], axis=0)
        e_carry[...] = e[:POOL_HALO, :]
        rows = tm + POOL_HALO
        for g, w in enumerate(POOL_WINDOWS):
            lo = g * POOL_GROUP_DIM
            run = run[:, POOL_GROUP_DIM if g else 0:]
            run = run + pltpu.roll(run, rows - w // 2, 0)
            du_ref[:, 2 * D_CONV + lo:2 * D_CONV + lo + POOL_GROUP_DIM] = (
                run[:tm, :POOL_GROUP_DIM] - dpooled[g]).astype(BF16)

        dh1 = _dot(du_ref[...], win_v[...])
        xh, r = _rms_fwd(x_ref[...])
        dgmix_ref[...] += _colsum(dh1 * xh)
        dx_ref[...] = dx1v + _rms_bwd(dh1, xh, r, gmix_ref[...])

    rev = lambda w: pl.BlockSpec((tm, w), lambda i: (n_tiles - 1 - i, 0))
    return pl.pallas_call(
        body, name="bwd_mix", grid=(n_tiles,),
        in_specs=[rev(D_MODEL), rev(D_MODEL), rev(D_IN), rev(D_CONV), rev(D_POOL), _full((1, D_MODEL)),
                  pl.BlockSpec(memory_space=pl.ANY), _full((CONV_WIDTH, D_CONV)), _full((1, D_CONV)), _full((1, D_CONV)),
                  _full((4, POOL_GROUP_DIM, POOL_GROUP_DIM)), _full((1, D_POOL)), _full(after.shape)],
        out_specs=[rev(D_MODEL), rev(D_IN), _full((1, D_MODEL)), _full((CONV_WIDTH, D_CONV)), _full((1, D_CONV)),
                   _full((1, D_CONV)), _full((1, D_CONV)), _full((4, POOL_GROUP_DIM, POOL_GROUP_DIM)), _full((1, D_POOL))],
        out_shape=[jax.ShapeDtypeStruct((tokens, D_MODEL), F32), jax.ShapeDtypeStruct((tokens, D_IN), BF16),
                   jax.ShapeDtypeStruct((1, D_MODEL), F32), jax.ShapeDtypeStruct((CONV_WIDTH, D_CONV), F32),
                   jax.ShapeDtypeStruct((1, D_CONV), F32), jax.ShapeDtypeStruct((1, D_CONV), F32),
                   jax.ShapeDtypeStruct((1, D_CONV), F32),
                   jax.ShapeDtypeStruct((4, POOL_GROUP_DIM, POOL_GROUP_DIM), F32), jax.ShapeDtypeStruct((1, D_POOL), F32)],
        scratch_shapes=[pltpu.VMEM((D_IN, D_MODEL), BF16), pltpu.VMEM((D_MODEL, D_MODEL), BF16),
                        pltpu.VMEM((CONV_HALO, D_CONV), F32), pltpu.VMEM((POOL_HALO, D_POOL), F32),
                        pltpu.SemaphoreType.DMA((2,))],
        compiler_params=_params(),
    )(dx1, x2d, u_all, c_all, pooled_all, g_mix, gw, conv_w, ln_g, ln_b, pool_w, pool_scale, after)


def _wgrad(a, b, name, after=None):
    tokens, m = a.shape
    n = b.shape[1]
    tm = 512 if m % 512 == 0 else 256
    extra = [] if after is None else [after]

    def body(a_ref, b_ref, *rest):
        rest[-1][...] = _dot_tn(a_ref[...], b_ref[...]).astype(rest[-1].dtype)

    return pl.pallas_call(
        body, name=name, grid=(m // tm,),
        in_specs=[pl.BlockSpec((tokens, tm), lambda i: (0, i)), _full((tokens, n))] + [_full(t.shape) for t in extra],
        out_specs=pl.BlockSpec((tm, n), lambda i: (i, 0)),
        out_shape=jax.ShapeDtypeStruct((m, n), BF16),
        compiler_params=_params(),
    )(a, b, *extra)


def _adamw_update(w, g, m, v):
    nm = ADAM_B1 * m + (1.0 - ADAM_B1) * g
    nv = ADAM_B2 * v + (1.0 - ADAM_B2) * (g * g)
    m_hat = nm / (1.0 - ADAM_B1 ** ADAM_STEP)
    v_hat = nv / (1.0 - ADAM_B2 ** ADAM_STEP)
    return -ADAM_LR * (m_hat / (jnp.sqrt(v_hat) + ADAM_EPS) + ADAM_WD * w), nm, nv


def _adamw_small(ws, gs, ms, vs):
    n = len(ws)

    def body(*refs):
        ins, outs = refs[:4 * n], refs[4 * n:]
        for k in range(n):
            d, nm, nv = _adamw_update(*[ins[j * n + k][...] for j in range(4)])
            outs[k][...] = d
            outs[n + k][...] = nm
            outs[2 * n + k][...] = nv

    vmem = pl.BlockSpec(memory_space=pltpu.VMEM)
    outs = pl.pallas_call(
        body, name="adamw_small",
        in_specs=[vmem] * (4 * n), out_specs=[vmem] * (3 * n),
        out_shape=[jax.ShapeDtypeStruct(w.shape, F32) for w in ws] * 3,
    )(*ws, *gs, *ms, *vs)
    return outs[:n], outs[n:2 * n], outs[2 * n:]


SMALL = (("norm_mix_g", (1, 1024)), ("conv_dw_b", (1, 512)), ("conv_ln_g", (1, 512)), ("conv_ln_b", (1, 512)),
         ("pool_w", (1, 4, 128, 128)), ("pool_scale", (1, 512)), ("norm_xattn_g", (1, 1024)), ("norm_mem_g", (1, 1024)),
         ("norm_ffn_g", (1, 1024)), ("ffn_dw_b", (1, 5632)), ("norm_final_g", (1024,)))
LANES = 128


def _pack_rows(arrs):
    flat = jnp.concatenate([a.reshape(-1) for a in arrs])
    pad = (-flat.shape[0]) % (8 * LANES)
    return jnp.pad(flat, (0, pad)).reshape(-1, LANES)


def kernel(x, mem, norm_mix_g, w_in, conv_dw_w, conv_dw_b, conv_ln_g, conv_ln_b, pool_w, pool_scale, w_out, norm_xattn_g, norm_mem_g, w_q, w_kv, w_o, norm_ffn_g, w_up, ffn_dw_w, ffn_dw_b, w_down, norm_final_g, loss_target, m_norm_mix_g, m_w_in, m_conv_dw_w, m_conv_dw_b, m_conv_ln_g, m_conv_ln_b, m_pool_w, m_pool_scale, m_w_out, m_norm_xattn_g, m_norm_mem_g, m_w_q, m_w_kv, m_w_o, m_norm_ffn_g, m_w_up, m_ffn_dw_w, m_ffn_dw_b, m_w_down, m_norm_final_g, v_norm_mix_g, v_w_in, v_conv_dw_w, v_conv_dw_b, v_conv_ln_g, v_conv_ln_b, v_pool_w, v_pool_scale, v_w_out, v_norm_xattn_g, v_norm_mem_g, v_w_q, v_w_kv, v_w_o, v_norm_ffn_g, v_w_up, v_ffn_dw_w, v_ffn_dw_b, v_w_down, v_norm_final_g):
    weights = dict(norm_mix_g=norm_mix_g, w_in=w_in, conv_dw_w=conv_dw_w, conv_dw_b=conv_dw_b, conv_ln_g=conv_ln_g,
                   conv_ln_b=conv_ln_b, pool_w=pool_w, pool_scale=pool_scale, w_out=w_out, norm_xattn_g=norm_xattn_g,
                   norm_mem_g=norm_mem_g, w_q=w_q, w_kv=w_kv, w_o=w_o, norm_ffn_g=norm_ffn_g, w_up=w_up,
                   ffn_dw_w=ffn_dw_w, ffn_dw_b=ffn_dw_b, w_down=w_down, norm_final_g=norm_final_g)
    moments_m = dict(norm_mix_g=m_norm_mix_g, w_in=m_w_in, conv_dw_w=m_conv_dw_w, conv_dw_b=m_conv_dw_b,
                     conv_ln_g=m_conv_ln_g, conv_ln_b=m_conv_ln_b, pool_w=m_pool_w, pool_scale=m_pool_scale,
                     w_out=m_w_out, norm_xattn_g=m_norm_xattn_g, norm_mem_g=m_norm_mem_g, w_q=m_w_q, w_kv=m_w_kv,
                     w_o=m_w_o, norm_ffn_g=m_norm_ffn_g, w_up=m_w_up, ffn_dw_w=m_ffn_dw_w, ffn_dw_b=m_ffn_dw_b,
                     w_down=m_w_down, norm_final_g=m_norm_final_g)
    moments_v = dict(norm_mix_g=v_norm_mix_g, w_in=v_w_in, conv_dw_w=v_conv_dw_w, conv_dw_b=v_conv_dw_b,
                     conv_ln_g=v_conv_ln_g, conv_ln_b=v_conv_ln_b, pool_w=v_pool_w, pool_scale=v_pool_scale,
                     w_out=v_w_out, norm_xattn_g=v_norm_xattn_g, norm_mem_g=v_norm_mem_g, w_q=v_w_q, w_kv=v_w_kv,
                     w_o=v_w_o, norm_ffn_g=v_norm_ffn_g, w_up=v_w_up, ffn_dw_w=v_ffn_dw_w, ffn_dw_b=v_ffn_dw_b,
                     w_down=v_w_down, norm_final_g=v_norm_final_g)
    order = list(weights)
    transposed = ("w_in", "w_kv", "w_up")

    n_b, seq, _ = x.shape
    tokens = n_b * seq
    tm_mix = min(512, seq // 2)
    tm_attn = min(1024, seq // 2)
    tm_ffn = min(256, seq // 2)
    dev = 4 * lax.axis_index("x") + 2 * lax.axis_index("y") + lax.axis_index("c")

    packs = [jnp.concatenate([weights[n][0].T if n in transposed else weights[n][0] for n in names], axis=0).astype(BF16)
             for names in AG_GROUPS]
    small_sharded = _pack_rows([conv_dw_w[0], ffn_dw_w[0]])
    gw_mix, gsmall = _all_gather([packs[0], small_sharded], "weights_all_gather")
    flights = []
    after = gw_mix
    for k in (1, 2):
        own_in_place = lax.dynamic_update_slice(lax.empty((N_DEV,) + packs[k].shape, BF16), packs[k][None], (dev, 0, 0))
        flights.append(_gather_start(own_in_place, after, "weights_gather_start_%d" % k, BARRIER_IDS["gather_start"][k - 1]))
        after = flights[-1][3]
    gflat = gsmall.reshape(N_DEV, -1)
    n_cw = CONV_WIDTH * (D_CONV // N_DEV)
    n_fw = FFN_CONV_WIDTH * (2 * D_FF // N_DEV)
    conv_w = gflat[:, :n_cw].reshape(N_DEV, CONV_WIDTH, D_CONV // N_DEV).transpose(1, 0, 2).reshape(CONV_WIDTH, D_CONV)
    ffn_w = gflat[:, n_cw:n_cw + n_fw].reshape(N_DEV, FFN_CONV_WIDTH, 2 * D_FF // N_DEV).transpose(1, 0, 2).reshape(
        FFN_CONV_WIDTH, 2 * D_FF)

    x2d = x.reshape(tokens, D_MODEL)
    mem2d = mem.reshape(n_b * N_MEM, D_MODEL)
    tgt2d = loss_target.reshape(tokens, D_MODEL)
    g_final = norm_final_g.reshape(1, D_MODEL)

    def gather_finish(flight, after, tag):
        fwd_send, fwd_recv, buf = _gather_forward(*flight[:3], after, "weights_gather_forward_" + tag,
                                                  BARRIER_IDS["gather_forward"][int(tag) - 1])
        return _gather_finish(fwd_send, fwd_recv, buf, "weights_gather_finish_" + tag)

    x1, u_all, c_all, pooled_all, ymix, h1 = _fwd_mix(
        x2d, gw_mix, norm_mix_g, conv_w, conv_dw_b, conv_ln_g, conv_ln_b, pool_w[0], pool_scale, flights[1][3],
        seq, tm_mix)
    gw_attn = gather_finish(flights[0], x1, "1")
    mem_n, kv = _fwd_kv(mem2d, gw_attn, norm_mem_g)
    x2, h2, q, o = _fwd_attn(x1, kv, gw_attn, norm_xattn_g, seq, tm_attn)
    gw_ffn = gather_finish(flights[1], x2, "2")
    uu_all, cc_all, a_all, h3, dx3, dx3b, loss_part, dg_final = _fwd_ffn(
        x2, tgt2d, gw_ffn, norm_ffn_g, ffn_w, ffn_dw_b, g_final, seq, tm_ffn)

    table = _owner_table()

    def sibling_start(names, tag):
        parts = [part[n].reshape(N_DEV, W_OFF[n][1], D_MODEL) for n in names]
        return _exchange_start(parts, 4, _to_sibling, "rs_sibling_exchange_start_" + tag, BARRIER_IDS["sibling"][tag])

    def chips_start(flight, after, tag):
        parts, landed = _exchange_wait(*flight[:4], after, 4, _to_sibling, "rs_sibling_exchange_wait_" + tag)
        sums = _chip_partial_sums(table, parts, landed, "rs_chip_partial_sums_" + tag)
        return parts, landed, _exchange_start(sums, 3, _to_chip, "rs_chip_exchange_start_" + tag,
                                              BARRIER_IDS["chips"][tag])

    grads, delta, new_m, new_v = {}, {}, {}, {}

    def reduce_finish(names, parts, landed, flight, after, tag):
        _, from_chips = _exchange_wait(*flight[:4], after, 3, _to_chip, "rs_chip_exchange_wait_" + tag)
        as_rows = {n: n in transposed and W_OFF[n][1] % LANES != 0 for n in names}
        states = [tuple(t[n][0].T if as_rows[n] else t[n][0] for t in (weights, moments_m, moments_v)) for n in names]
        results = _final_update(table, parts, landed, from_chips, states, "rs_final_update_" + tag)
        for n, res in zip(names, results):
            grads[n], delta[n], new_m[n], new_v[n] = [t.T[None] if as_rows[n] else t[None] for t in res]
        return delta[names[-1]]

    part = {}
    dx2, dx2b, duu, d_ffn_b, d_ffn_w, dg_ffn = _bwd_ffn(dx3, x2, uu_all, cc_all, gw_ffn, norm_ffn_g, ffn_w, seq, tm_ffn)
    part["w_up"] = _wgrad(duu, h3, "wgrad_w_up")
    part["w_down"] = _wgrad(a_all, dx3b, "wgrad_w_down")
    to_sibling_a = sibling_start(RS_GROUPS["a"], "a")
    dx1, dx1b, dq, dkv, dg_x = _bwd_attn(dx2, x1, q, kv, gw_attn, norm_xattn_g, to_sibling_a[4], seq, tm_mix)
    parts_a, landed_a, flight_a = chips_start(to_sibling_a, dx1, "a")
    dkv_b, dg_mem = _bwd_kv(dkv, mem2d, gw_attn)
    part["w_q"] = _wgrad(h2, dq, "wgrad_w_q", after=flight_a[4])
    part["w_kv"] = _wgrad(dkv_b, mem_n, "wgrad_w_kv", after=flight_a[4])
    part["w_out"] = _wgrad(ymix, dx1b, "wgrad_w_out", after=flight_a[4])
    to_sibling_b = sibling_start(RS_GROUPS["b"], "b")
    part["w_o"] = _wgrad(o, dx2b, "wgrad_w_o", after=to_sibling_b[4])
    parts_b, landed_b, flight_b = chips_start(to_sibling_b, part["w_o"], "b")
    dx, du, dg_mix, d_conv_w, d_conv_b, d_ln_g, d_ln_b, d_pool_w, d_pool_scale = _bwd_mix(
        dx1, x2d, u_all, c_all, pooled_all, gw_mix, norm_mix_g, conv_w, conv_ln_g, conv_ln_b, pool_w[0], pool_scale,
        flight_b[4], seq, tm_mix)
    grad_x = dx.reshape(x.shape)

    small_grads = dict(norm_mix_g=dg_mix, conv_dw_b=d_conv_b, conv_ln_g=d_ln_g, conv_ln_b=d_ln_b, pool_w=d_pool_w,
                       pool_scale=d_pool_scale, norm_xattn_g=dg_x, norm_mem_g=dg_mem, norm_ffn_g=dg_ffn,
                       ffn_dw_b=d_ffn_b, norm_final_g=dg_final)
    small_list = [small_grads[n] for n, _ in SMALL] + [d_conv_w, d_ffn_w, loss_part[:1]]
    small_mine = _pack_rows(small_list)
    small_flight = _broadcast_start(
        lax.dynamic_update_slice(lax.empty((N_DEV,) + small_mine.shape, F32), small_mine[None], (dev, 0, 0)),
        "small_grads_broadcast_start", BARRIER_IDS["broadcast"])

    part["w_in"] = _wgrad(du, h1, "wgrad_w_in", after=small_flight[3])
    to_sibling_c = sibling_start(RS_GROUPS["c"], "c")
    updated_b = reduce_finish(RS_GROUPS["b"], parts_b, landed_b, flight_b, to_sibling_c[4], "b")
    parts_c, landed_c, flight_c = chips_start(to_sibling_c, updated_b, "c")
    updated_a = reduce_finish(RS_GROUPS["a"], parts_a, landed_a, flight_a, flight_c[4], "a")
    small_all = _broadcast_wait(*small_flight[:3], updated_a, "small_grads_broadcast_wait")
    small_sum = _sum_blocks(small_all).reshape(-1)

    pos = 0
    for n, shape in SMALL:
        size = 1
        for s in shape:
            size *= s
        grads[n] = small_sum[pos:pos + size].reshape(shape)
        pos += size
    full_conv_w = small_sum[pos:pos + CONV_WIDTH * D_CONV].reshape(CONV_WIDTH, D_CONV)
    pos += CONV_WIDTH * D_CONV
    full_ffn_w = small_sum[pos:pos + FFN_CONV_WIDTH * 2 * D_FF].reshape(FFN_CONV_WIDTH, 2 * D_FF)
    loss = small_sum[pos + FFN_CONV_WIDTH * 2 * D_FF]
    grads["conv_dw_w"] = lax.dynamic_slice_in_dim(full_conv_w, dev * (D_CONV // N_DEV), D_CONV // N_DEV, axis=1)[None]
    grads["ffn_dw_w"] = lax.dynamic_slice_in_dim(full_ffn_w, dev * (2 * D_FF // N_DEV), 2 * D_FF // N_DEV, axis=1)[None]

    small_names = [n for n in order if n not in W_OFF]
    swap = lambda t: jnp.transpose(t, (1, 0, 2))
    two_d = lambda t: t.reshape(1, -1) if t.ndim == 1 else (swap(t) if t.ndim == 3 else t)
    outs = _adamw_small(*[[two_d(t[n]) for n in small_names] for t in (weights, grads, moments_m, moments_v)])
    for res, out in zip((delta, new_m, new_v), outs):
        for n, o in zip(small_names, out):
            res[n] = swap(o) if o.ndim == 3 else o.reshape(weights[n].shape)

    reduce_finish(RS_GROUPS["c"], parts_c, landed_c, flight_c, delta[small_names[-1]], "c")

    return (loss, grad_x, *[grads[n] for n in order], *[delta[n] for n in order],
            *[new_m[n] for n in order], *[new_v[n] for n in order])
```

```python
import jax
import jax.numpy as jnp
from jax import lax
from jax.experimental import pallas as pl
from jax.experimental.pallas import tpu as pltpu

F32 = jnp.float32
BF16 = jnp.bfloat16
MESH = pl.DeviceIdType.MESH

N_DEV = 8
D_MODEL = 1024
D_CONV = 512
D_POOL = 512
CONV_WIDTH = 31
POOL_WINDOWS = (2, 4, 8, 16)
POOL_GROUP_DIM = 128
D_IN = 1536
N_MEM = 256
HEADS = 4
HEAD_DIM = 256
D_FF = 2816
FFN_CONV_WIDTH = 3
EPS = 1e-6
ADAM_LR = 0.001
ADAM_B1 = 0.9
ADAM_B2 = 0.999
ADAM_EPS = 1e-08
ADAM_WD = 0.01
ADAM_STEP = 10

VMEM_LIMIT_V7X = 56 * 1024 * 1024
CONV_HALO = 32
POOL_HALO = 16
FFN_HALO = 8
FFN_CHUNK = 2816
WGRAD_RESIDENT_BYTES = 8 * 1024 * 1024
WGRAD_TOKEN_BLOCK = 1024

W_ROWS = (("w_in", 192), ("w_out", 128), ("w_q", 128), ("w_kv", 256), ("w_o", 128), ("w_up", 704), ("w_down", 352))
AG_GROUPS = (("w_in", "w_out"), ("w_q", "w_kv", "w_o"), ("w_up", "w_down"))
W_OFF = {}
for _names in AG_GROUPS:
    _o = 0
    for _n in _names:
        W_OFF[_n] = (_o, dict(W_ROWS)[_n])
        _o += dict(W_ROWS)[_n]
RS_GROUPS = {"a": ("w_up", "w_down"), "b": ("w_q", "w_kv", "w_out"), "c": ("w_o", "w_in")}
BARRIER_IDS = {"gather_start": (0, 1), "gather_forward": (2, 3), "sibling": {"a": 4, "b": 5, "c": 6},
               "chips": {"a": 7, "b": 8, "c": 9}, "broadcast": 10}


def _dot(a, b):
    return jnp.dot(a, b, preferred_element_type=F32)


def _dot_nt(a, b):
    return lax.dot_general(a, b, (((1,), (1,)), ((), ())), preferred_element_type=F32)


def _dot_tn(a, b):
    return lax.dot_general(a, b, (((0,), (0,)), ((), ())), preferred_element_type=F32)


def _sigmoid(v):
    return 1.0 / (1.0 + jnp.exp(-v))


def _rms_fwd(v):
    r = lax.rsqrt(jnp.mean(v * v, axis=-1, keepdims=True) + EPS)
    return v * r, r


def _rms_bwd(dh, vh, r, g):
    gd = dh * g
    return r * (gd - vh * jnp.mean(gd * vh, axis=-1, keepdims=True))


def _sublane_shifts(v):
    rows = v.shape[0]
    return [v] + [pltpu.roll(v, rows - b, 0) for b in range(1, 8)]


def _colsum(v):
    return jnp.sum(v, axis=0, keepdims=True)


def _colsum_mxu(v):
    return _dot(jnp.ones((8, v.shape[0]), BF16), v.astype(BF16))[0:1, :]


def _full(shape):
    return pl.BlockSpec(shape, lambda *_: (0,) * len(shape))


def _params(sem=("arbitrary",), vmem=VMEM_LIMIT_V7X):
    return pltpu.CompilerParams(dimension_semantics=sem, vmem_limit_bytes=vmem)


def _load_weight(g_hbm, name, dst, sem):
    off, rows = W_OFF[name]
    return [pltpu.make_async_copy(g_hbm.at[d, pl.ds(off, rows), :], dst.at[pl.ds(d * rows, rows), :], sem)
            for d in range(N_DEV)]


def _start_weights(g_hbm, names, dsts, sems):
    @pl.when(pl.program_id(0) == 0)
    def _():
        copies = [_load_weight(g_hbm, name, dst, sems.at[k]) for k, (name, dst) in enumerate(zip(names, dsts))]
        for cp in sum(copies, []):
            cp.start()
        for cp in sum(copies, []):
            cp.wait()


def _position():
    x, y, c = lax.axis_index("x"), lax.axis_index("y"), lax.axis_index("c")
    chips = [(1 - x, y), (x, 1 - y), (1 - x, 1 - y)]
    return x, y, c, chips


def _dev(px, py, pc):
    return 4 * px + 2 * py + pc


def _all_gather(arrs, name):
    n = len(arrs)

    def body(*refs):
        ins, outs = refs[:n], refs[n:2 * n]
        send_sems, recv_sems, local_sems = refs[2 * n:2 * n + 3]
        bounce = refs[2 * n + 3:]
        x, y, c, chips = _position()
        me, sibling = (x, y, c), (x, y, 1 - c)

        def copy(a, k, block, to, src=None):
            rows = outs[a].at[_dev(*block)]
            return pltpu.make_async_remote_copy(
                src_ref=rows if src is None else src, dst_ref=rows,
                send_sem=send_sems.at[a, k], recv_sem=recv_sems.at[a, k], device_id=to, device_id_type=MESH)

        sends = []
        for a in range(n):
            first = [copy(a, 0, me, sibling, src=ins[a])]
            first += [copy(a, 1 + j, me, (*chip, c), src=ins[a]) for j, chip in enumerate(chips)]
            for cp in first:
                cp.start()
            sends += first
        started = []
        for a in range(n):
            load = pltpu.make_async_copy(ins[a], bounce[a], local_sems.at[a, 0])
            load.start()
            load.wait()
            mine = pltpu.make_async_copy(bounce[a], outs[a].at[_dev(*me)], local_sems.at[a, 1])
            mine.start()
            started.append(mine)
        for j, chip in enumerate(chips):
            for a in range(n):
                copy(a, 1 + j, (*chip, c), me).wait_recv()
                passed = copy(a, 4 + j, (*chip, c), sibling)
                passed.start()
                sends.append(passed)
        for a in range(n):
            copy(a, 0, sibling, me).wait_recv()
            for j, chip in enumerate(chips):
                copy(a, 4 + j, (*chip, 1 - c), me).wait_recv()
        for cp in sends:
            cp.wait_send()
        for mine in started:
            mine.wait()

    any_spec = pl.BlockSpec(memory_space=pl.ANY)
    return pl.pallas_call(
        body, name=name,
        out_shape=[jax.ShapeDtypeStruct((N_DEV,) + a.shape, a.dtype) for a in arrs],
        in_specs=[any_spec] * n, out_specs=[any_spec] * n,
        scratch_shapes=[pltpu.SemaphoreType.DMA((n, 7)), pltpu.SemaphoreType.DMA((n, 7)), pltpu.SemaphoreType.DMA((n, 2))]
        + [pltpu.VMEM(a.shape, a.dtype) for a in arrs],
    )(*arrs)


_HBM = pl.BlockSpec(memory_space=pltpu.HBM)
_SEM = pl.BlockSpec(memory_space=pltpu.SEMAPHORE)
_SIDE_EFFECT = pltpu.SideEffectType.DATAFLOW_SIDE_EFFECTING


def _handshake(peers):
    barrier = pltpu.get_barrier_semaphore()
    for peer in peers:
        pl.semaphore_signal(barrier, inc=1, device_id=peer, device_id_type=MESH)
    pl.semaphore_wait(barrier, len(peers))


def _gather_start(buf, after, name, collective_id):
    def body(buf_ref, after_ref, send_sems, recv_sems, buf_thru, token):
        del after_ref, buf_thru
        x, y, c, chips = _position()
        rows = buf_ref.at[_dev(x, y, c)]
        targets = [(x, y, 1 - c)] + [(*chip, c) for chip in chips]
        _handshake(targets)
        for k, to in enumerate(targets):
            pltpu.make_async_remote_copy(src_ref=rows, dst_ref=rows, send_sem=send_sems.at[k], recv_sem=recv_sems.at[k],
                                         device_id=to, device_id_type=MESH).start()
        token[...] = jnp.zeros_like(token)

    return pl.pallas_call(
        body, name=name,
        out_shape=(pltpu.SemaphoreType.DMA((4,)), pltpu.SemaphoreType.DMA((4,)), pltpu.HBM(buf.shape, buf.dtype),
                   jax.ShapeDtypeStruct((8, 128), F32)),
        in_specs=(_HBM, pl.BlockSpec(memory_space=pl.ANY)),
        out_specs=(_SEM, _SEM, _HBM, pl.BlockSpec(memory_space=pltpu.VMEM)),
        input_output_aliases={0: 2},
        compiler_params=pltpu.CompilerParams(has_side_effects=_SIDE_EFFECT, collective_id=collective_id),
    )(pltpu.with_memory_space_constraint(buf, pltpu.HBM), after)


def _gather_forward(send_sems, recv_sems, buf, after, name, collective_id):
    def body(buf_ref, send_sems, recv_sems, after_ref, fwd_send, fwd_recv, buf_thru):
        del after_ref, buf_thru
        x, y, c, chips = _position()
        sibling = (x, y, 1 - c)

        def copy(block, k, sends, recvs):
            rows = buf_ref.at[_dev(*block)]
            return pltpu.make_async_remote_copy(src_ref=rows, dst_ref=rows, send_sem=sends.at[k], recv_sem=recvs.at[k],
                                                device_id=sibling, device_id_type=MESH)

        _handshake([sibling])
        for k in range(4):
            copy((x, y, c), k, send_sems, recv_sems).wait_send()
        copy(sibling, 0, send_sems, recv_sems).wait_recv()
        for j, chip in enumerate(chips):
            copy((*chip, c), 1 + j, send_sems, recv_sems).wait_recv()
            copy((*chip, c), j, fwd_send, fwd_recv).start()

    return pl.pallas_call(
        body, name=name,
        out_shape=(pltpu.SemaphoreType.DMA((3,)), pltpu.SemaphoreType.DMA((3,)), pltpu.HBM(buf.shape, buf.dtype)),
        in_specs=(_HBM, _SEM, _SEM, pl.BlockSpec(memory_space=pl.ANY)), out_specs=(_SEM, _SEM, _HBM),
        input_output_aliases={0: 2},
        compiler_params=pltpu.CompilerParams(has_side_effects=_SIDE_EFFECT, collective_id=collective_id),
    )(buf, send_sems, recv_sems, after)


def _gather_finish(fwd_send, fwd_recv, buf, name):
    def body(buf_ref, fwd_send, fwd_recv, buf_thru):
        del buf_thru
        x, y, c, chips = _position()
        for j, chip in enumerate(chips):
            cp = pltpu.make_async_remote_copy(
                src_ref=buf_ref.at[_dev(*chip, c)], dst_ref=buf_ref.at[_dev(*chip, 1 - c)], send_sem=fwd_send.at[j],
                recv_sem=fwd_recv.at[j], device_id=(x, y, 1 - c), device_id_type=MESH)
            cp.wait_send()
            cp.wait_recv()

    return pl.pallas_call(
        body, name=name,
        out_shape=pltpu.HBM(buf.shape, buf.dtype),
        in_specs=(_HBM, _SEM, _SEM), out_specs=_HBM,
        input_output_aliases={0: 0},
        compiler_params=pltpu.CompilerParams(has_side_effects=_SIDE_EFFECT),
    )(buf, fwd_send, fwd_recv)


def _everyone_else(x, y, c, chips):
    return [(x, y, 1 - c)] + [(*chip, core) for chip in chips for core in (c, 1 - c)]


def _broadcast_start(buf, name, collective_id):
    def body(buf_ref, send_sems, recv_sems, buf_thru, token):
        del buf_thru
        x, y, c, chips = _position()
        rows = buf_ref.at[_dev(x, y, c)]
        _handshake(_everyone_else(x, y, c, chips))
        for k, to in enumerate(_everyone_else(x, y, c, chips)):
            pltpu.make_async_remote_copy(src_ref=rows, dst_ref=rows, send_sem=send_sems.at[k], recv_sem=recv_sems.at[k],
                                         device_id=to, device_id_type=MESH).start()
        token[...] = jnp.zeros_like(token)

    return pl.pallas_call(
        body, name=name,
        out_shape=(pltpu.SemaphoreType.DMA((7,)), pltpu.SemaphoreType.DMA((7,)), pltpu.HBM(buf.shape, buf.dtype),
                   jax.ShapeDtypeStruct((8, 128), F32)),
        in_specs=(_HBM,), out_specs=(_SEM, _SEM, _HBM, pl.BlockSpec(memory_space=pltpu.VMEM)),
        input_output_aliases={0: 2},
        compiler_params=pltpu.CompilerParams(has_side_effects=_SIDE_EFFECT, collective_id=collective_id),
    )(pltpu.with_memory_space_constraint(buf, pltpu.HBM))


def _broadcast_wait(send_sems, recv_sems, buf, after, name):
    def body(buf_ref, send_sems, recv_sems, after_ref, buf_thru):
        del after_ref, buf_thru
        x, y, c, chips = _position()
        for k, peer in enumerate(_everyone_else(x, y, c, chips)):
            cp = pltpu.make_async_remote_copy(
                src_ref=buf_ref.at[_dev(x, y, c)], dst_ref=buf_ref.at[_dev(*peer)], send_sem=send_sems.at[k],
                recv_sem=recv_sems.at[k], device_id=peer, device_id_type=MESH)
            cp.wait_send()
            cp.wait_recv()

    return pl.pallas_call(
        body, name=name,
        out_shape=pltpu.HBM(buf.shape, buf.dtype),
        in_specs=(_HBM, _SEM, _SEM, pl.BlockSpec(memory_space=pl.ANY)), out_specs=_HBM,
        input_output_aliases={0: 0},
        compiler_params=pltpu.CompilerParams(has_side_effects=_SIDE_EFFECT),
    )(buf, send_sems, recv_sems, after)


def _to_sibling(j, x, y, c, chips):
    return _dev(*([(x, y)] + chips)[j], 1 - c), (x, y, 1 - c)


def _to_chip(j, x, y, c, chips):
    return j, (*chips[j], c)


def _exchange_start(srcs, n_slots, route, name, collective_id):
    n = len(srcs)

    def body(*refs):
        s_refs, land_refs = refs[:n], refs[n:2 * n]
        send_sems, recv_sems = refs[2 * n:2 * n + 2]
        token = refs[-1]
        x, y, c, chips = _position()
        _handshake([(x, y, 1 - c)] if route is _to_sibling else [route(j, x, y, c, chips)[1] for j in range(n_slots)])
        for k in range(n):
            for j in range(n_slots):
                block, to = route(j, x, y, c, chips)
                pltpu.make_async_remote_copy(
                    src_ref=s_refs[k].at[block], dst_ref=land_refs[k].at[j], send_sem=send_sems.at[n_slots * k + j],
                    recv_sem=recv_sems.at[n_slots * k + j], device_id=to, device_id_type=MESH).start()
        token[...] = jnp.zeros_like(token)

    lands = [jax.ShapeDtypeStruct((n_slots,) + s.shape[1:], s.dtype) for s in srcs]
    outs = pl.pallas_call(
        body, name=name,
        out_shape=(pltpu.SemaphoreType.DMA((n_slots * n,)), pltpu.SemaphoreType.DMA((n_slots * n,)),
                   *[pltpu.HBM(s.shape, s.dtype) for s in srcs], *[pltpu.HBM(l.shape, l.dtype) for l in lands],
                   jax.ShapeDtypeStruct((8, 128), F32)),
        in_specs=[_HBM] * (2 * n), out_specs=(_SEM, _SEM, *[_HBM] * (2 * n), pl.BlockSpec(memory_space=pltpu.VMEM)),
        input_output_aliases={k: 2 + k for k in range(2 * n)},
        compiler_params=pltpu.CompilerParams(has_side_effects=_SIDE_EFFECT, collective_id=collective_id),
    )(*[pltpu.with_memory_space_constraint(s, pltpu.HBM) for s in srcs],
      *[pltpu.with_memory_space_constraint(lax.empty(l.shape, l.dtype), pltpu.HBM) for l in lands])
    return outs[0], outs[1], outs[2:2 + n], outs[2 + n:2 + 2 * n], outs[-1]


def _exchange_wait(send_sems, recv_sems, s_thru, land_thru, after, n_slots, route, name):
    n = len(s_thru)

    def body(*refs):
        s_refs, land_refs = refs[:n], refs[n:2 * n]
        send_sems, recv_sems = refs[2 * n:2 * n + 2]
        x, y, c, chips = _position()
        for k in range(n):
            for j in range(n_slots):
                block, to = route(j, x, y, c, chips)
                cp = pltpu.make_async_remote_copy(
                    src_ref=s_refs[k].at[block], dst_ref=land_refs[k].at[j], send_sem=send_sems.at[n_slots * k + j],
                    recv_sem=recv_sems.at[n_slots * k + j], device_id=to, device_id_type=MESH)
                cp.wait_send()
                cp.wait_recv()

    outs = pl.pallas_call(
        body, name=name,
        out_shape=(*[pltpu.HBM(s.shape, s.dtype) for s in s_thru], *[pltpu.HBM(l.shape, l.dtype) for l in land_thru]),
        in_specs=[_HBM] * (2 * n) + [_SEM, _SEM, pl.BlockSpec(memory_space=pl.ANY)], out_specs=[_HBM] * (2 * n),
        input_output_aliases={k: k for k in range(2 * n)},
        compiler_params=pltpu.CompilerParams(has_side_effects=_SIDE_EFFECT),
    )(*s_thru, *land_thru, send_sems, recv_sems, after)
    return outs[:n], outs[n:]


def _owner_table():
    x, y, c = lax.axis_index("x"), lax.axis_index("y"), lax.axis_index("c")
    chips = [(x, y), (1 - x, y), (x, 1 - y), (1 - x, 1 - y)]
    return jnp.stack([_dev(px, py, c) for px, py in chips]).astype(jnp.int32)


def _chip_partial_sums(table, parts, from_sibling, name):
    n = len(parts)

    def body(tab_ref, *refs):
        del tab_ref
        for g_ref, l_ref, out_ref in zip(refs[:n], refs[n:2 * n], refs[2 * n:]):
            out_ref[...] = (g_ref[...].astype(F32) + l_ref[...].astype(F32)).astype(out_ref.dtype)

    block = lambda p: (None,) + p.shape[1:]
    grid_spec = pltpu.PrefetchScalarGridSpec(
        num_scalar_prefetch=1, grid=(3,),
        in_specs=[pl.BlockSpec(block(p), lambda j, tab: (tab[j + 1], 0, 0)) for p in parts]
        + [pl.BlockSpec(block(p), lambda j, tab: (j + 1, 0, 0)) for p in parts],
        out_specs=[pl.BlockSpec(block(p), lambda j, tab: (j, 0, 0)) for p in parts])
    return pl.pallas_call(
        body, name=name, grid_spec=grid_spec,
        out_shape=[jax.ShapeDtypeStruct((3,) + p.shape[1:], BF16) for p in parts],
        compiler_params=_params(("arbitrary",)),
    )(table, *parts, *from_sibling)


def _final_update(table, parts, from_sibling, from_chips, states, name):
    n = len(parts)
    flipped = [states[k][0].shape != parts[k].shape[1:] for k in range(n)]

    def body(tab_ref, *refs):
        del tab_ref
        ins, outs = refs[:6 * n], refs[6 * n:]
        for k in range(n):
            acc = ins[k][...].astype(F32) + ins[n + k][...].astype(F32)
            for j in range(3):
                acc = acc + ins[2 * n + k][j].astype(F32)
            if flipped[k]:
                acc = acc.T
            w_ref, m_ref, v_ref = ins[3 * n + 3 * k:3 * n + 3 * k + 3]
            outs[4 * k][...] = acc
            for out_ref, val in zip(outs[4 * k + 1:4 * k + 4], _adamw_update(w_ref[...], acc, m_ref[...], v_ref[...])):
                out_ref[...] = val

    def grad_block(k, lead, at):
        r, c = parts[k].shape[1:]
        if flipped[k]:
            return pl.BlockSpec(lead + (r, c // 2), lambda t, tab: (*at(tab), 0, t))
        return pl.BlockSpec(lead + (r // 2, c), lambda t, tab: (*at(tab), t, 0))

    def state_block(k):
        a, b = states[k][0].shape
        return pl.BlockSpec((a // 2, b), lambda t, tab: (t, 0))

    grid_spec = pltpu.PrefetchScalarGridSpec(
        num_scalar_prefetch=1, grid=(2,),
        in_specs=[grad_block(k, (None,), lambda tab: (tab[0],)) for k in range(n)]
        + [grad_block(k, (None,), lambda tab: (0,)) for k in range(n)]
        + [grad_block(k, (3,), lambda tab: (0,)) for k in range(n)]
        + [state_block(k) for k in range(n) for _ in range(3)],
        out_specs=[state_block(k) for k in range(n) for _ in range(4)])
    outs = pl.pallas_call(
        body, name=name, grid_spec=grid_spec,
        out_shape=[jax.ShapeDtypeStruct(states[k][0].shape, F32) for k in range(n) for _ in range(4)],
        compiler_params=_params(("arbitrary",)),
    )(table, *parts, *from_sibling, *from_chips, *[t for k in range(n) for t in states[k]])
    return [outs[4 * k:4 * k + 4] for k in range(n)]


def _sum_blocks(g8):
    _, rows, cols = g8.shape

    def body(g_ref, out_ref):
        acc = g_ref[0]
        for d in range(1, N_DEV):
            acc = acc + g_ref[d]
        out_ref[...] = acc

    return pl.pallas_call(
        body, name="small_grad_sum", grid=(1,),
        in_specs=[_full((N_DEV, rows, cols))], out_specs=_full((rows, cols)),
        out_shape=jax.ShapeDtypeStruct((rows, cols), F32),
        compiler_params=_params(("arbitrary",)),
    )(g8)


def _fwd_mix(x2d, gw, g_mix, conv_w, conv_b, ln_g, ln_b, pool_w, pool_scale, after, seq, tm):
    tokens = x2d.shape[0]
    n_tiles = tokens // tm
    tps = seq // tm

    def body(x_ref, gmix_ref, gw_hbm, cw_ref, cb_ref, lng_ref, lnb_ref, pw_ref, ps_ref, after_ref,
             x1_ref, u_ref, c_ref, pooled_ref, ymix_ref, h1_ref,
             win_v, wout_v, hc_carry, up_carry, sem):
        del after_ref
        i = pl.program_id(0)

        _start_weights(gw_hbm, ("w_in", "w_out"), (win_v, wout_v), sem)

        @pl.when(i % tps == 0)
        def _():
            hc_carry[...] = jnp.zeros_like(hc_carry)
            up_carry[...] = jnp.zeros_like(up_carry)

        x = x_ref[...]
        xh, _ = _rms_fwd(x)
        h1 = (xh * gmix_ref[...]).astype(BF16)
        h1_ref[...] = h1
        u = _dot_nt(h1, win_v[...])
        u_ref[...] = u
        val, gate, up = u[:, :D_CONV], u[:, D_CONV:2 * D_CONV], u[:, 2 * D_CONV:]

        extp = jnp.concatenate([up_carry[...], up], axis=0)
        up_carry[...] = up[tm - POOL_HALO:, :]
        pos = lax.broadcasted_iota(jnp.int32, (tm, 1), 0) + (i % tps) * tm
        run = extp
        mixed = []
        for g, w in enumerate(POOL_WINDOWS):
            lo = g * POOL_GROUP_DIM
            run = run[:, POOL_GROUP_DIM if g else 0:]
            run = run + pltpu.roll(run, w // 2, 0)
            cnt = jnp.minimum(pos + 1, w).astype(F32)
            pooled = run[POOL_HALO:, :POOL_GROUP_DIM] / cnt - up[:, lo:lo + POOL_GROUP_DIM]
            pooled = pooled.astype(BF16)
            pooled_ref[:, lo:lo + POOL_GROUP_DIM] = pooled
            mixed.append(_dot(pooled, pw_ref[g].astype(BF16)))
        y_pool = jnp.concatenate(mixed, axis=-1) * ps_ref[...]
        y_pool = y_pool.astype(BF16)
        ymix_ref[:, D_CONV:] = y_pool
        out = _dot(y_pool, wout_v[D_CONV:, :])

        hc = val * _sigmoid(gate)
        ext = jnp.concatenate([hc_carry[...], hc], axis=0)
        hc_carry[...] = hc[tm - CONV_HALO:, :]
        conv = jnp.broadcast_to(cb_ref[...], (tm, D_CONV))
        ahead_by = _sublane_shifts(ext)
        for k in range(CONV_WIDTH):
            whole, part = divmod(CONV_HALO - (CONV_WIDTH - 1) + k, 8)
            conv = conv + cw_ref[k:k + 1, :] * ahead_by[part][8 * whole:8 * whole + tm, :]
        c_ref[...] = conv
        mu = jnp.mean(conv, axis=-1, keepdims=True)
        cen = conv - mu
        ln = cen * lax.rsqrt(jnp.mean(cen * cen, axis=-1, keepdims=True) + EPS) * lng_ref[...] + lnb_ref[...]
        y_conv = ln * _sigmoid(ln)
        y_conv = y_conv.astype(BF16)
        ymix_ref[:, :D_CONV] = y_conv
        x1_ref[...] = x + (out + _dot(y_conv, wout_v[:D_CONV, :]))

    row = lambda w: pl.BlockSpec((tm, w), lambda i: (i, 0))
    return pl.pallas_call(
        body, name="fwd_mix", grid=(n_tiles,),
        in_specs=[row(D_MODEL), _full((1, D_MODEL)), pl.BlockSpec(memory_space=pl.ANY),
                  _full((CONV_WIDTH, D_CONV)), _full((1, D_CONV)), _full((1, D_CONV)), _full((1, D_CONV)),
                  _full((4, POOL_GROUP_DIM, POOL_GROUP_DIM)), _full((1, D_POOL)), _full(after.shape)],
        out_specs=[row(D_MODEL), row(D_IN), row(D_CONV), row(D_POOL), row(D_MODEL), row(D_MODEL)],
        out_shape=[jax.ShapeDtypeStruct((tokens, D_MODEL), F32), jax.ShapeDtypeStruct((tokens, D_IN), F32),
                   jax.ShapeDtypeStruct((tokens, D_CONV), F32), jax.ShapeDtypeStruct((tokens, D_POOL), BF16),
                   jax.ShapeDtypeStruct((tokens, D_MODEL), BF16), jax.ShapeDtypeStruct((tokens, D_MODEL), BF16)],
        scratch_shapes=[pltpu.VMEM((D_IN, D_MODEL), BF16), pltpu.VMEM((D_MODEL, D_MODEL), BF16),
                        pltpu.VMEM((CONV_HALO, D_CONV), F32), pltpu.VMEM((POOL_HALO, D_POOL), F32),
                        pltpu.SemaphoreType.DMA((2,))],
        compiler_params=_params(),
    )(x2d, g_mix, gw, conv_w, conv_b, ln_g, ln_b, pool_w, pool_scale, after)


def _fwd_kv(mem2d, gw, g_mem):
    rows = mem2d.shape[0]
    n_b = rows // N_MEM

    def body(mem_ref, g_ref, gw_hbm, mn_ref, kv_ref, wkv_v, sem):
        @pl.when(pl.program_id(0) == 0)
        def _():
            copies = _load_weight(gw_hbm, "w_kv", wkv_v, sem)
            for cp in copies:
                cp.start()
            for cp in copies:
                cp.wait()

        mh, _ = _rms_fwd(mem_ref[...])
        mn = (mh * g_ref[...]).astype(BF16)
        mn_ref[...] = mn
        kv_ref[...] = _dot_nt(mn, wkv_v[...]).astype(BF16)

    return pl.pallas_call(
        body, name="fwd_kv", grid=(n_b,),
        in_specs=[pl.BlockSpec((N_MEM, D_MODEL), lambda b: (b, 0)), _full((1, D_MODEL)), pl.BlockSpec(memory_space=pl.ANY)],
        out_specs=[pl.BlockSpec((N_MEM, D_MODEL), lambda b: (b, 0)), pl.BlockSpec((N_MEM, 2 * D_MODEL), lambda b: (b, 0))],
        out_shape=[jax.ShapeDtypeStruct((rows, D_MODEL), BF16), jax.ShapeDtypeStruct((rows, 2 * D_MODEL), BF16)],
        scratch_shapes=[pltpu.VMEM((2 * D_MODEL, D_MODEL), BF16), pltpu.SemaphoreType.DMA],
        compiler_params=_params(),
    )(mem2d, g_mem, gw)


def _softmax_rows(s):
    e = jnp.exp(s - jnp.max(s, axis=-1, keepdims=True))
    return e / jnp.sum(e, axis=-1, keepdims=True)


def _fwd_attn(x1, kv, gw, g_x, seq, tm):
    tokens = x1.shape[0]
    n_tiles = tokens // tm
    tps = seq // tm

    def body(x1_ref, kv_ref, g_ref, gw_hbm, x2_ref, h2_ref, q_ref, o_ref, wq_v, wo_v, sem):
        _start_weights(gw_hbm, ("w_q", "w_o"), (wq_v, wo_v), sem)
        x1v = x1_ref[...]
        xh, _ = _rms_fwd(x1v)
        h2 = (xh * g_ref[...]).astype(BF16)
        h2_ref[...] = h2
        q = (_dot(h2, wq_v[...]) * (HEAD_DIM ** -0.5)).astype(BF16)
        q_ref[...] = q
        heads = [slice(h * HEAD_DIM, (h + 1) * HEAD_DIM) for h in range(HEADS)]
        scores = [_dot_nt(q[:, hd], kv_ref[:, hd]) for hd in heads]
        probs = [_softmax_rows(s).astype(BF16) for s in scores]
        outs = [_dot(p, kv_ref[:, pl.ds(D_MODEL + h * HEAD_DIM, HEAD_DIM)]) for h, p in enumerate(probs)]
        o = jnp.concatenate(outs, axis=-1).astype(BF16)
        o_ref[...] = o
        x2_ref[...] = x1v + _dot(o, wo_v[...])

    row = lambda w: pl.BlockSpec((tm, w), lambda i: (i, 0))
    return pl.pallas_call(
        body, name="fwd_attn", grid=(n_tiles,),
        in_specs=[row(D_MODEL), pl.BlockSpec((N_MEM, 2 * D_MODEL), lambda i: (i // tps, 0)), _full((1, D_MODEL)),
                  pl.BlockSpec(memory_space=pl.ANY)],
        out_specs=[row(D_MODEL)] * 4,
        out_shape=[jax.ShapeDtypeStruct((tokens, D_MODEL), F32)] + [jax.ShapeDtypeStruct((tokens, D_MODEL), BF16)] * 3,
        scratch_shapes=[pltpu.VMEM((D_MODEL, D_MODEL), BF16), pltpu.VMEM((D_MODEL, D_MODEL), BF16), pltpu.SemaphoreType.DMA((2,))],
        compiler_params=_params(),
    )(x1, kv, g_x, gw)


def _ffn_conv(uu, halo, w_ref, b_ref, cols):
    ext = jnp.concatenate([halo, uu], axis=0)
    p1 = pltpu.roll(ext, 1, 0)[FFN_HALO:, :]
    p2 = pltpu.roll(ext, 2, 0)[FFN_HALO:, :]
    return b_ref[:, cols] + w_ref[2:3, cols] * uu + w_ref[1:2, cols] * p1 + w_ref[0:1, cols] * p2


def _fwd_ffn(x2, target, gw, g_ffn, ffn_w, ffn_b, g_final, seq, tm):
    tokens = x2.shape[0]
    n_tiles = tokens // tm
    tps = seq // tm
    n_chunks = D_FF // FFN_CHUNK

    def body(x2_ref, tgt_ref, gffn_ref, gw_hbm, fw_ref, fb_ref, gfin_ref,
             uu_ref, cc_ref, a_ref, h3_ref, dx3_ref, dx3b_ref, loss_ref, dgfin_ref,
             wup_v, wdown_v, carry, sem):
        i = pl.program_id(0)

        _start_weights(gw_hbm, ("w_up", "w_down"), (wup_v, wdown_v), sem)

        @pl.when(i == 0)
        def _():
            loss_ref[...] = jnp.zeros_like(loss_ref)
            dgfin_ref[...] = jnp.zeros_like(dgfin_ref)

        @pl.when(i % tps == 0)
        def _():
            carry[...] = jnp.zeros_like(carry)

        x2v = x2_ref[...]
        xh, _ = _rms_fwd(x2v)
        h3 = (xh * gffn_ref[...]).astype(BF16)
        h3_ref[...] = h3
        acc = jnp.zeros((tm, D_MODEL), F32)
        for jc in range(n_chunks):
            halves = []
            for half in range(2):
                cols = pl.ds(half * D_FF + jc * FFN_CHUNK, FFN_CHUNK)
                uu = _dot_nt(h3, wup_v[cols, :])
                uu_ref[:, cols] = uu.astype(BF16)
                cc = _ffn_conv(uu, carry[:, cols], fw_ref, fb_ref, cols)
                cc_ref[:, cols] = cc.astype(BF16)
                halves.append(cc)
                carry[:, cols] = uu[tm - FFN_HALO:, :]
            gate, val = halves
            a = (gate * _sigmoid(gate) * val).astype(BF16)
            a_ref[:, pl.ds(jc * FFN_CHUNK, FFN_CHUNK)] = a
            acc = acc + _dot(a, wdown_v[pl.ds(jc * FFN_CHUNK, FFN_CHUNK), :])
        x3 = x2v + acc

        xh3, r3 = _rms_fwd(x3)
        gfin = gfin_ref[...]
        err = xh3 * gfin - tgt_ref[...]
        loss_ref[...] += jnp.full(loss_ref.shape, jnp.sum(err * err) * (0.5 / D_MODEL), F32)
        dy = err * (1.0 / D_MODEL)
        dgfin_ref[...] += _colsum(dy * xh3)
        dx3 = _rms_bwd(dy, xh3, r3, gfin)
        dx3_ref[...] = dx3
        dx3b_ref[...] = dx3.astype(BF16)

    row = lambda w: pl.BlockSpec((tm, w), lambda i: (i, 0))
    return pl.pallas_call(
        body, name="fwd_ffn", grid=(n_tiles,),
        in_specs=[row(D_MODEL), row(D_MODEL), _full((1, D_MODEL)), pl.BlockSpec(memory_space=pl.ANY),
                  _full((FFN_CONV_WIDTH, 2 * D_FF)), _full((1, 2 * D_FF)), _full((1, D_MODEL))],
        out_specs=[row(2 * D_FF), row(2 * D_FF), row(D_FF), row(D_MODEL), row(D_MODEL), row(D_MODEL), _full((8, 128)),
                   _full((1, D_MODEL))],
        out_shape=[jax.ShapeDtypeStruct((tokens, 2 * D_FF), BF16), jax.ShapeDtypeStruct((tokens, 2 * D_FF), BF16),
                   jax.ShapeDtypeStruct((tokens, D_FF), BF16),
                   jax.ShapeDtypeStruct((tokens, D_MODEL), BF16), jax.ShapeDtypeStruct((tokens, D_MODEL), F32),
                   jax.ShapeDtypeStruct((tokens, D_MODEL), BF16),
                   jax.ShapeDtypeStruct((8, 128), F32), jax.ShapeDtypeStruct((1, D_MODEL), F32)],
        scratch_shapes=[pltpu.VMEM((2 * D_FF, D_MODEL), BF16), pltpu.VMEM((D_FF, D_MODEL), BF16),
                        pltpu.VMEM((FFN_HALO, 2 * D_FF), F32), pltpu.SemaphoreType.DMA((2,))],
        compiler_params=_params(),
    )(x2, target, g_ffn, gw, ffn_w, ffn_b, g_final)


def _bwd_ffn(dx3, x2, uu_all, cc_all, gw, g_ffn, ffn_w, seq, tm):
    tokens = x2.shape[0]
    n_tiles = tokens // tm
    tps = seq // tm
    n_chunks = D_FF // FFN_CHUNK

    def body(dx3_ref, x2_ref, uu_ref, cc_ref, gffn_ref, gw_hbm, fw_ref,
             dx2_ref, dx2b_ref, duu_ref, dfb_ref, dfw_ref, dg_ref,
             wup_v, wdown_v, carry, sem):
        i = pl.program_id(0)
        t = n_tiles - 1 - i

        _start_weights(gw_hbm, ("w_down", "w_up"), (wdown_v, wup_v), sem)

        @pl.when(i == 0)
        def _():
            dfb_ref[...] = jnp.zeros_like(dfb_ref)
            dfw_ref[...] = jnp.zeros_like(dfw_ref)
            dg_ref[...] = jnp.zeros_like(dg_ref)

        @pl.when(t % tps == tps - 1)
        def _():
            carry[...] = jnp.zeros_like(carry)

        dx3v = dx3_ref[...]
        dx3b = dx3v.astype(BF16)
        dh3 = jnp.zeros((tm, D_MODEL), F32)
        for jc in range(n_chunks):
            da = _dot_nt(dx3b, wdown_v[pl.ds(jc * FFN_CHUNK, FFN_CHUNK), :])
            colss = [pl.ds(half * D_FF + jc * FFN_CHUNK, FFN_CHUNK) for half in range(2)]
            gate, val = [cc_ref[:, cols].astype(F32) for cols in colss]
            sg = _sigmoid(gate)
            dgate = da * val * (sg * (1.0 + gate * (1.0 - sg)))
            dval = da * (gate * sg)
            for dcc, cols in zip((dgate, dval), colss):
                uu = uu_ref[:, cols].astype(F32)
                dfb_ref[:, cols] += _colsum(dcc)
                ext = jnp.concatenate([dcc, carry[:, cols]], axis=0)
                carry[:, cols] = dcc[:FFN_HALO, :]
                n1 = pltpu.roll(ext, tm + FFN_HALO - 1, 0)[:tm, :]
                n2 = pltpu.roll(ext, tm + FFN_HALO - 2, 0)[:tm, :]
                duu = fw_ref[2:3, cols] * dcc + fw_ref[1:2, cols] * n1 + fw_ref[0:1, cols] * n2
                dfw_ref[2:3, cols] += _colsum(uu * dcc)
                dfw_ref[1:2, cols] += _colsum(uu * n1)
                dfw_ref[0:1, cols] += _colsum(uu * n2)
                duub = duu.astype(BF16)
                duu_ref[:, cols] = duub
                dh3 = dh3 + _dot(duub, wup_v[cols, :])
        xh, r = _rms_fwd(x2_ref[...])
        dg_ref[...] += _colsum(dh3 * xh)
        dx2 = dx3v + _rms_bwd(dh3, xh, r, gffn_ref[...])
        dx2_ref[...] = dx2
        dx2b_ref[...] = dx2.astype(BF16)

    rev = lambda w: pl.BlockSpec((tm, w), lambda i: (n_tiles - 1 - i, 0))
    return pl.pallas_call(
        body, name="bwd_ffn", grid=(n_tiles,),
        in_specs=[rev(D_MODEL), rev(D_MODEL), rev(2 * D_FF), rev(2 * D_FF), _full((1, D_MODEL)),
                  pl.BlockSpec(memory_space=pl.ANY), _full((FFN_CONV_WIDTH, 2 * D_FF))],
        out_specs=[rev(D_MODEL), rev(D_MODEL), rev(2 * D_FF), _full((1, 2 * D_FF)), _full((FFN_CONV_WIDTH, 2 * D_FF)),
                   _full((1, D_MODEL))],
        out_shape=[jax.ShapeDtypeStruct((tokens, D_MODEL), F32), jax.ShapeDtypeStruct((tokens, D_MODEL), BF16),
                   jax.ShapeDtypeStruct((tokens, 2 * D_FF), BF16),
                   jax.ShapeDtypeStruct((1, 2 * D_FF), F32), jax.ShapeDtypeStruct((FFN_CONV_WIDTH, 2 * D_FF), F32),
                   jax.ShapeDtypeStruct((1, D_MODEL), F32)],
        scratch_shapes=[pltpu.VMEM((2 * D_FF, D_MODEL), BF16), pltpu.VMEM((D_FF, D_MODEL), BF16),
                        pltpu.VMEM((FFN_HALO, 2 * D_FF), F32), pltpu.SemaphoreType.DMA((2,))],
        compiler_params=_params(),
    )(dx3, x2, uu_all, cc_all, g_ffn, gw, ffn_w)


def _bwd_attn(dx2, x1, q, kv, gw, g_x, after, seq, tm):
    tokens = x1.shape[0]
    n_tiles = tokens // tm
    tps = seq // tm
    n_b = tokens // seq

    def body(dx2_ref, x1_ref, q_ref, kv_ref, g_ref, gw_hbm, after_ref, dx1_ref, dx1b_ref, dq_ref, dkv_ref, dg_ref,
             wq_v, wo_v, sem):
        del after_ref
        i = pl.program_id(0)

        _start_weights(gw_hbm, ("w_o", "w_q"), (wo_v, wq_v), sem)

        @pl.when(i == 0)
        def _():
            dg_ref[...] = jnp.zeros_like(dg_ref)

        @pl.when(i % tps == 0)
        def _():
            dkv_ref[...] = jnp.zeros_like(dkv_ref)

        dx2v = dx2_ref[...]
        do = _dot_nt(dx2v.astype(BF16), wo_v[...]).astype(BF16)
        q = q_ref[...]
        heads = [slice(h * HEAD_DIM, (h + 1) * HEAD_DIM) for h in range(HEADS)]
        kcols = [pl.ds(h * HEAD_DIM, HEAD_DIM) for h in range(HEADS)]
        vcols = [pl.ds(D_MODEL + h * HEAD_DIM, HEAD_DIM) for h in range(HEADS)]
        scores = [_dot_nt(q[:, hd], kv_ref[:, kc]) for hd, kc in zip(heads, kcols)]
        dps = [_dot_nt(do[:, hd], kv_ref[:, vc]) for hd, vc in zip(heads, vcols)]
        probs = [_softmax_rows(s) for s in scores]
        dss = [(p * (dp - jnp.sum(dp * p, axis=-1, keepdims=True))).astype(BF16) for p, dp in zip(probs, dps)]
        for p, hd, vc in zip(probs, heads, vcols):
            dkv_ref[:, vc] += _dot_tn(p.astype(BF16), do[:, hd])
        dqs = [_dot(ds, kv_ref[:, kc]) * (HEAD_DIM ** -0.5) for ds, kc in zip(dss, kcols)]
        for ds, hd, kc in zip(dss, heads, kcols):
            dkv_ref[:, kc] += _dot_tn(ds, q[:, hd])
        dq = jnp.concatenate(dqs, axis=-1).astype(BF16)
        dq_ref[...] = dq
        dh2 = _dot_nt(dq, wq_v[...])
        xh, r = _rms_fwd(x1_ref[...])
        dg_ref[...] += _colsum(dh2 * xh)
        dx1 = dx2v + _rms_bwd(dh2, xh, r, g_ref[...])
        dx1_ref[...] = dx1
        dx1b_ref[...] = dx1.astype(BF16)

    row = lambda w: pl.BlockSpec((tm, w), lambda i: (i, 0))
    per_b = pl.BlockSpec((N_MEM, 2 * D_MODEL), lambda i: (i // tps, 0))
    return pl.pallas_call(
        body, name="bwd_attn", grid=(n_tiles,),
        in_specs=[row(D_MODEL), row(D_MODEL), row(D_MODEL), per_b, _full((1, D_MODEL)), pl.BlockSpec(memory_space=pl.ANY),
                  _full(after.shape)],
        out_specs=[row(D_MODEL), row(D_MODEL), row(D_MODEL), per_b, _full((1, D_MODEL))],
        out_shape=[jax.ShapeDtypeStruct((tokens, D_MODEL), F32), jax.ShapeDtypeStruct((tokens, D_MODEL), BF16),
                   jax.ShapeDtypeStruct((tokens, D_MODEL), BF16),
                   jax.ShapeDtypeStruct((n_b * N_MEM, 2 * D_MODEL), F32), jax.ShapeDtypeStruct((1, D_MODEL), F32)],
        scratch_shapes=[pltpu.VMEM((D_MODEL, D_MODEL), BF16), pltpu.VMEM((D_MODEL, D_MODEL), BF16), pltpu.SemaphoreType.DMA((2,))],
        compiler_params=_params(),
    )(dx2, x1, q, kv, g_x, gw, after)


def _bwd_kv(dkv, mem2d, gw):
    rows = mem2d.shape[0]
    n_b = rows // N_MEM

    def body(dkv_ref, mem_ref, gw_hbm, dkvb_ref, dg_ref, wkv_v, sem):
        @pl.when(pl.program_id(0) == 0)
        def _():
            copies = _load_weight(gw_hbm, "w_kv", wkv_v, sem)
            for cp in copies:
                cp.start()
            for cp in copies:
                cp.wait()
            dg_ref[...] = jnp.zeros_like(dg_ref)

        dkvb = dkv_ref[...].astype(BF16)
        dkvb_ref[...] = dkvb
        dmn = _dot(dkvb, wkv_v[...])
        mh, _ = _rms_fwd(mem_ref[...])
        dg_ref[...] += _colsum(dmn * mh)

    return pl.pallas_call(
        body, name="bwd_kv", grid=(n_b,),
        in_specs=[pl.BlockSpec((N_MEM, 2 * D_MODEL), lambda b: (b, 0)), pl.BlockSpec((N_MEM, D_MODEL), lambda b: (b, 0)),
                  pl.BlockSpec(memory_space=pl.ANY)],
        out_specs=[pl.BlockSpec((N_MEM, 2 * D_MODEL), lambda b: (b, 0)), _full((1, D_MODEL))],
        out_shape=[jax.ShapeDtypeStruct((rows, 2 * D_MODEL), BF16), jax.ShapeDtypeStruct((1, D_MODEL), F32)],
        scratch_shapes=[pltpu.VMEM((2 * D_MODEL, D_MODEL), BF16), pltpu.SemaphoreType.DMA],
        compiler_params=_params(),
    )(dkv, mem2d, gw)


def _bwd_mix(dx1, x2d, u_all, c_all, pooled_all, gw, g_mix, conv_w, ln_g, ln_b, pool_w, pool_scale, after, seq, tm):
    tokens = x2d.shape[0]
    n_tiles = tokens // tm
    tps = seq // tm

    def body(dx1_ref, x_ref, u_ref, c_ref, pooled_ref, gmix_ref, gw_hbm, cw_ref, lng_ref, lnb_ref, pw_ref, ps_ref,
             after_ref, dx_ref, du_ref, dgmix_ref, dcw_ref, dcb_ref, dlng_ref, dlnb_ref, dpw_ref, dps_ref,
             win_v, wout_v, dc_carry, e_carry, sem):
        del after_ref
        i = pl.program_id(0)
        t = n_tiles - 1 - i

        _start_weights(gw_hbm, ("w_out", "w_in"), (wout_v, win_v), sem)

        @pl.when(i == 0)
        def _():
            for ref in (dgmix_ref, dcw_ref, dcb_ref, dlng_ref, dlnb_ref, dpw_ref, dps_ref):
                ref[...] = jnp.zeros_like(ref)

        @pl.when(t % tps == tps - 1)
        def _():
            dc_carry[...] = jnp.zeros_like(dc_carry)
            e_carry[...] = jnp.zeros_like(e_carry)

        dx1v = dx1_ref[...]
        dymix = _dot_nt(dx1v.astype(BF16), wout_v[...])
        dyc, dyp = dymix[:, :D_CONV], dymix[:, D_CONV:]
        u = u_ref[...]
        val, gate = u[:, :D_CONV], u[:, D_CONV:2 * D_CONV]

        conv = c_ref[...]
        mu = jnp.mean(conv, axis=-1, keepdims=True)
        cen = conv - mu
        rs = lax.rsqrt(jnp.mean(cen * cen, axis=-1, keepdims=True) + EPS)
        chat = cen * rs
        ln = chat * lng_ref[...] + lnb_ref[...]
        sl = _sigmoid(ln)
        dln = dyc * (sl * (1.0 + ln * (1.0 - sl)))
        dlng_ref[...] += _colsum(dln * chat)
        dlnb_ref[...] += _colsum(dln)
        dchat = dln * lng_ref[...]
        dc = rs * (dchat - jnp.mean(dchat, axis=-1, keepdims=True)
                   - chat * jnp.mean(dchat * chat, axis=-1, keepdims=True))
        dcb_ref[...] += _colsum(dc)
        sg = _sigmoid(gate)
        hc = val * sg
        ext = jnp.concatenate([dc, dc_carry[...]], axis=0)
        dc_carry[...] = dc[:CONV_HALO, :]
        dhc = jnp.zeros((tm, D_CONV), F32)
        ahead_by = _sublane_shifts(ext)
        for k in range(CONV_WIDTH):
            whole, part = divmod(CONV_WIDTH - 1 - k, 8)
            tap = ahead_by[part][8 * whole:8 * whole + tm, :]
            dhc = dhc + cw_ref[k:k + 1, :] * tap
            dcw_ref[k:k + 1, :] += _colsum_mxu(hc * tap)
        du_ref[:, :D_CONV] = (dhc * sg).astype(BF16)
        du_ref[:, D_CONV:2 * D_CONV] = (dhc * val * (sg * (1.0 - sg))).astype(BF16)

        pos = lax.broadcasted_iota(jnp.int32, (tm, 1), 0) + (t % tps) * tm
        es, dpooled = [], []
        for g, w in enumerate(POOL_WINDOWS):
            cols = pl.ds(g * POOL_GROUP_DIM, POOL_GROUP_DIM)
            lo = g * POOL_GROUP_DIM
            pooled = pooled_ref[:, cols]
            pw = pw_ref[g].astype(BF16)
            dyg = dyp[:, lo:lo + POOL_GROUP_DIM]
            dps_ref[:, cols] += _colsum(dyg * _dot(pooled, pw))
            dmixed = (dyg * ps_ref[:, cols]).astype(BF16)
            dpw_ref[g] += _dot_tn(pooled, dmixed)
            dpo = _dot_nt(dmixed, pw)
            dpooled.append(dpo)
            es.append(dpo / jnp.minimum(pos + 1, w).astype(F32))
        e = jnp.concatenate(es, axis=-1)
        run = jnp.concatenate([e, e_carry[...]], axis=0)
        e_carry[...] = e[:POOL_HALO, :]
        rows = tm + POOL_HALO
        for g, w in enumerate(POOL_WINDOWS):
            lo = g * POOL_GROUP_DIM
            run = run[:, POOL_GROUP_DIM if g else 0:]
            run = run + pltpu.roll(run, rows - w // 2, 0)
            du_ref[:, 2 * D_CONV + lo:2 * D_CONV + lo + POOL_GROUP_DIM] = (
                run[:tm, :POOL_GROUP_DIM] - dpooled[g]).astype(BF16)

        dh1 = _dot(du_ref[...], win_v[...])
        xh, r = _rms_fwd(x_ref[...])
        dgmix_ref[...] += _colsum(dh1 * xh)
        dx_ref[...] = dx1v + _rms_bwd(dh1, xh, r, gmix_ref[...])

    rev = lambda w: pl.BlockSpec((tm, w), lambda i: (n_tiles - 1 - i, 0))
    return pl.pallas_call(
        body, name="bwd_mix", grid=(n_tiles,),
        in_specs=[rev(D_MODEL), rev(D_MODEL), rev(D_IN), rev(D_CONV), rev(D_POOL), _full((1, D_MODEL)),
                  pl.BlockSpec(memory_space=pl.ANY), _full((CONV_WIDTH, D_CONV)), _full((1, D_CONV)), _full((1, D_CONV)),
                  _full((4, POOL_GROUP_DIM, POOL_GROUP_DIM)), _full((1, D_POOL)), _full(after.shape)],
        out_specs=[rev(D_MODEL), rev(D_IN), _full((1, D_MODEL)), _full((CONV_WIDTH, D_CONV)), _full((1, D_CONV)),
                   _full((1, D_CONV)), _full((1, D_CONV)), _full((4, POOL_GROUP_DIM, POOL_GROUP_DIM)), _full((1, D_POOL))],
        out_shape=[jax.ShapeDtypeStruct((tokens, D_MODEL), F32), jax.ShapeDtypeStruct((tokens, D_IN), BF16),
                   jax.ShapeDtypeStruct((1, D_MODEL), F32), jax.ShapeDtypeStruct((CONV_WIDTH, D_CONV), F32),
                   jax.ShapeDtypeStruct((1, D_CONV), F32), jax.ShapeDtypeStruct((1, D_CONV), F32),
                   jax.ShapeDtypeStruct((1, D_CONV), F32),
                   jax.ShapeDtypeStruct((4, POOL_GROUP_DIM, POOL_GROUP_DIM), F32), jax.ShapeDtypeStruct((1, D_POOL), F32)],
        scratch_shapes=[pltpu.VMEM((D_IN, D_MODEL), BF16), pltpu.VMEM((D_MODEL, D_MODEL), BF16),
                        pltpu.VMEM((CONV_HALO, D_CONV), F32), pltpu.VMEM((POOL_HALO, D_POOL), F32),
                        pltpu.SemaphoreType.DMA((2,))],
        compiler_params=_params(),
    )(dx1, x2d, u_all, c_all, pooled_all, g_mix, gw, conv_w, ln_g, ln_b, pool_w, pool_scale, after)


def _wgrad(a, b, name, after=None):
    tokens, m = a.shape
    n = b.shape[1]
    tm = 512 if m % 512 == 0 else 256
    extra = [] if after is None else [after]

    if m * n * 4 <= WGRAD_RESIDENT_BYTES:
        tk = min(WGRAD_TOKEN_BLOCK, tokens)
        n_k = tokens // tk

        def walk(a_ref, b_ref, *rest):
            out_ref, acc = rest[-2:]
            k = pl.program_id(0)

            @pl.when(k == 0)
            def _():
                acc[...] = jnp.zeros_like(acc)

            acc[...] += _dot_tn(a_ref[...], b_ref[...])

            @pl.when(k == n_k - 1)
            def _():
                out_ref[...] = acc[...].astype(out_ref.dtype)

        return pl.pallas_call(
            walk, name=name, grid=(n_k,),
            in_specs=[pl.BlockSpec((tk, m), lambda k: (k, 0)), pl.BlockSpec((tk, n), lambda k: (k, 0))] + [
                _full(t.shape) for t in extra],
            out_specs=_full((m, n)),
            out_shape=jax.ShapeDtypeStruct((m, n), BF16),
            scratch_shapes=[pltpu.VMEM((m, n), F32)],
            compiler_params=_params(),
        )(a, b, *extra)

    def body(a_ref, b_ref, *rest):
        rest[-1][...] = _dot_tn(a_ref[...], b_ref[...]).astype(rest[-1].dtype)

    return pl.pallas_call(
        body, name=name, grid=(m // tm,),
        in_specs=[pl.BlockSpec((tokens, tm), lambda i: (0, i)), _full((tokens, n))] + [_full(t.shape) for t in extra],
        out_specs=pl.BlockSpec((tm, n), lambda i: (i, 0)),
        out_shape=jax.ShapeDtypeStruct((m, n), BF16),
        compiler_params=_params(),
    )(a, b, *extra)


def _adamw_update(w, g, m, v):
    nm = ADAM_B1 * m + (1.0 - ADAM_B1) * g
    nv = ADAM_B2 * v + (1.0 - ADAM_B2) * (g * g)
    m_hat = nm / (1.0 - ADAM_B1 ** ADAM_STEP)
    v_hat = nv / (1.0 - ADAM_B2 ** ADAM_STEP)
    return -ADAM_LR * (m_hat / (jnp.sqrt(v_hat) + ADAM_EPS) + ADAM_WD * w), nm, nv


def _adamw_small(ws, gs, ms, vs):
    n = len(ws)

    def body(*refs):
        ins, outs = refs[:4 * n], refs[4 * n:]
        for k in range(n):
            d, nm, nv = _adamw_update(*[ins[j * n + k][...] for j in range(4)])
            outs[k][...] = d
            outs[n + k][...] = nm
            outs[2 * n + k][...] = nv

    vmem = pl.BlockSpec(memory_space=pltpu.VMEM)
    outs = pl.pallas_call(
        body, name="adamw_small",
        in_specs=[vmem] * (4 * n), out_specs=[vmem] * (3 * n),
        out_shape=[jax.ShapeDtypeStruct(w.shape, F32) for w in ws] * 3,
    )(*ws, *gs, *ms, *vs)
    return outs[:n], outs[n:2 * n], outs[2 * n:]


SMALL = (("norm_mix_g", (1, 1024)), ("conv_dw_b", (1, 512)), ("conv_ln_g", (1, 512)), ("conv_ln_b", (1, 512)),
         ("pool_w", (1, 4, 128, 128)), ("pool_scale", (1, 512)), ("norm_xattn_g", (1, 1024)), ("norm_mem_g", (1, 1024)),
         ("norm_ffn_g", (1, 1024)), ("ffn_dw_b", (1, 5632)), ("norm_final_g", (1024,)))
LANES = 128


def _pack_rows(arrs):
    flat = jnp.concatenate([a.reshape(-1) for a in arrs])
    pad = (-flat.shape[0]) % (8 * LANES)
    return jnp.pad(flat, (0, pad)).reshape(-1, LANES)


def kernel(x, mem, norm_mix_g, w_in, conv_dw_w, conv_dw_b, conv_ln_g, conv_ln_b, pool_w, pool_scale, w_out, norm_xattn_g, norm_mem_g, w_q, w_kv, w_o, norm_ffn_g, w_up, ffn_dw_w, ffn_dw_b, w_down, norm_final_g, loss_target, m_norm_mix_g, m_w_in, m_conv_dw_w, m_conv_dw_b, m_conv_ln_g, m_conv_ln_b, m_pool_w, m_pool_scale, m_w_out, m_norm_xattn_g, m_norm_mem_g, m_w_q, m_w_kv, m_w_o, m_norm_ffn_g, m_w_up, m_ffn_dw_w, m_ffn_dw_b, m_w_down, m_norm_final_g, v_norm_mix_g, v_w_in, v_conv_dw_w, v_conv_dw_b, v_conv_ln_g, v_conv_ln_b, v_pool_w, v_pool_scale, v_w_out, v_norm_xattn_g, v_norm_mem_g, v_w_q, v_w_kv, v_w_o, v_norm_ffn_g, v_w_up, v_ffn_dw_w, v_ffn_dw_b, v_w_down, v_norm_final_g):
    weights = dict(norm_mix_g=norm_mix_g, w_in=w_in, conv_dw_w=conv_dw_w, conv_dw_b=conv_dw_b, conv_ln_g=conv_ln_g,
                   conv_ln_b=conv_ln_b, pool_w=pool_w, pool_scale=pool_scale, w_out=w_out, norm_xattn_g=norm_xattn_g,
                   norm_mem_g=norm_mem_g, w_q=w_q, w_kv=w_kv, w_o=w_o, norm_ffn_g=norm_ffn_g, w_up=w_up,
                   ffn_dw_w=ffn_dw_w, ffn_dw_b=ffn_dw_b, w_down=w_down, norm_final_g=norm_final_g)
    moments_m = dict(norm_mix_g=m_norm_mix_g, w_in=m_w_in, conv_dw_w=m_conv_dw_w, conv_dw_b=m_conv_dw_b,
                     conv_ln_g=m_conv_ln_g, conv_ln_b=m_conv_ln_b, pool_w=m_pool_w, pool_scale=m_pool_scale,
                     w_out=m_w_out, norm_xattn_g=m_norm_xattn_g, norm_mem_g=m_norm_mem_g, w_q=m_w_q, w_kv=m_w_kv,
                     w_o=m_w_o, norm_ffn_g=m_norm_ffn_g, w_up=m_w_up, ffn_dw_w=m_ffn_dw_w, ffn_dw_b=m_ffn_dw_b,
                     w_down=m_w_down, norm_final_g=m_norm_final_g)
    moments_v = dict(norm_mix_g=v_norm_mix_g, w_in=v_w_in, conv_dw_w=v_conv_dw_w, conv_dw_b=v_conv_dw_b,
                     conv_ln_g=v_conv_ln_g, conv_ln_b=v_conv_ln_b, pool_w=v_pool_w, pool_scale=v_pool_scale,
                     w_out=v_w_out, norm_xattn_g=v_norm_xattn_g, norm_mem_g=v_norm_mem_g, w_q=v_w_q, w_kv=v_w_kv,
                     w_o=v_w_o, norm_ffn_g=v_norm_ffn_g, w_up=v_w_up, ffn_dw_w=v_ffn_dw_w, ffn_dw_b=v_ffn_dw_b,
                     w_down=v_w_down, norm_final_g=v_norm_final_g)
    order = list(weights)
    transposed = ("w_in", "w_kv", "w_up")

    n_b, seq, _ = x.shape
    tokens = n_b * seq
    tm_mix = min(512, seq // 2)
    tm_attn = min(1024, seq // 2)
    tm_ffn = min(256, seq // 2)
    dev = 4 * lax.axis_index("x") + 2 * lax.axis_index("y") + lax.axis_index("c")

    packs = [jnp.concatenate([weights[n][0].T if n in transposed else weights[n][0] for n in names], axis=0).astype(BF16)
             for names in AG_GROUPS]
    small_sharded = _pack_rows([conv_dw_w[0], ffn_dw_w[0]])
    gw_mix, gsmall = _all_gather([packs[0], small_sharded], "weights_all_gather")
    flights = []
    after = gw_mix
    for k in (1, 2):
        own_in_place = lax.dynamic_update_slice(lax.empty((N_DEV,) + packs[k].shape, BF16), packs[k][None], (dev, 0, 0))
        flights.append(_gather_start(own_in_place, after, "weights_gather_start_%d" % k, BARRIER_IDS["gather_start"][k - 1]))
        after = flights[-1][3]
    gflat = gsmall.reshape(N_DEV, -1)
    n_cw = CONV_WIDTH * (D_CONV // N_DEV)
    n_fw = FFN_CONV_WIDTH * (2 * D_FF // N_DEV)
    conv_w = gflat[:, :n_cw].reshape(N_DEV, CONV_WIDTH, D_CONV // N_DEV).transpose(1, 0, 2).reshape(CONV_WIDTH, D_CONV)
    ffn_w = gflat[:, n_cw:n_cw + n_fw].reshape(N_DEV, FFN_CONV_WIDTH, 2 * D_FF // N_DEV).transpose(1, 0, 2).reshape(
        FFN_CONV_WIDTH, 2 * D_FF)

    x2d = x.reshape(tokens, D_MODEL)
    mem2d = mem.reshape(n_b * N_MEM, D_MODEL)
    tgt2d = loss_target.reshape(tokens, D_MODEL)
    g_final = norm_final_g.reshape(1, D_MODEL)

    def gather_finish(flight, after, tag):
        fwd_send, fwd_recv, buf = _gather_forward(*flight[:3], after, "weights_gather_forward_" + tag,
                                                  BARRIER_IDS["gather_forward"][int(tag) - 1])
        return _gather_finish(fwd_send, fwd_recv, buf, "weights_gather_finish_" + tag)

    x1, u_all, c_all, pooled_all, ymix, h1 = _fwd_mix(
        x2d, gw_mix, norm_mix_g, conv_w, conv_dw_b, conv_ln_g, conv_ln_b, pool_w[0], pool_scale, flights[1][3],
        seq, tm_mix)
    gw_attn = gather_finish(flights[0], x1, "1")
    mem_n, kv = _fwd_kv(mem2d, gw_attn, norm_mem_g)
    x2, h2, q, o = _fwd_attn(x1, kv, gw_attn, norm_xattn_g, seq, tm_attn)
    gw_ffn = gather_finish(flights[1], x2, "2")
    uu_all, cc_all, a_all, h3, dx3, dx3b, loss_part, dg_final = _fwd_ffn(
        x2, tgt2d, gw_ffn, norm_ffn_g, ffn_w, ffn_dw_b, g_final, seq, tm_ffn)

    table = _owner_table()

    def sibling_start(names, tag):
        parts = [part[n].reshape(N_DEV, W_OFF[n][1], D_MODEL) for n in names]
        return _exchange_start(parts, 4, _to_sibling, "rs_sibling_exchange_start_" + tag, BARRIER_IDS["sibling"][tag])

    def chips_start(flight, after, tag):
        parts, landed = _exchange_wait(*flight[:4], after, 4, _to_sibling, "rs_sibling_exchange_wait_" + tag)
        sums = _chip_partial_sums(table, parts, landed, "rs_chip_partial_sums_" + tag)
        return parts, landed, _exchange_start(sums, 3, _to_chip, "rs_chip_exchange_start_" + tag,
                                              BARRIER_IDS["chips"][tag])

    grads, delta, new_m, new_v = {}, {}, {}, {}

    def reduce_finish(names, parts, landed, flight, after, tag):
        _, from_chips = _exchange_wait(*flight[:4], after, 3, _to_chip, "rs_chip_exchange_wait_" + tag)
        as_rows = {n: n in transposed and W_OFF[n][1] % LANES != 0 for n in names}
        states = [tuple(t[n][0].T if as_rows[n] else t[n][0] for t in (weights, moments_m, moments_v)) for n in names]
        results = _final_update(table, parts, landed, from_chips, states, "rs_final_update_" + tag)
        for n, res in zip(names, results):
            grads[n], delta[n], new_m[n], new_v[n] = [t.T[None] if as_rows[n] else t[None] for t in res]
        return delta[names[-1]]

    part = {}
    dx2, dx2b, duu, d_ffn_b, d_ffn_w, dg_ffn = _bwd_ffn(dx3, x2, uu_all, cc_all, gw_ffn, norm_ffn_g, ffn_w, seq, tm_ffn)
    part["w_up"] = _wgrad(duu, h3, "wgrad_w_up")
    part["w_down"] = _wgrad(a_all, dx3b, "wgrad_w_down")
    to_sibling_a = sibling_start(RS_GROUPS["a"], "a")
    dx1, dx1b, dq, dkv, dg_x = _bwd_attn(dx2, x1, q, kv, gw_attn, norm_xattn_g, to_sibling_a[4], seq, tm_mix)
    parts_a, landed_a, flight_a = chips_start(to_sibling_a, dx1, "a")
    dkv_b, dg_mem = _bwd_kv(dkv, mem2d, gw_attn)
    part["w_q"] = _wgrad(h2, dq, "wgrad_w_q", after=flight_a[4])
    part["w_kv"] = _wgrad(dkv_b, mem_n, "wgrad_w_kv", after=flight_a[4])
    part["w_out"] = _wgrad(ymix, dx1b, "wgrad_w_out", after=flight_a[4])
    to_sibling_b = sibling_start(RS_GROUPS["b"], "b")
    part["w_o"] = _wgrad(o, dx2b, "wgrad_w_o", after=to_sibling_b[4])
    parts_b, landed_b, flight_b = chips_start(to_sibling_b, part["w_o"], "b")
    dx, du, dg_mix, d_conv_w, d_conv_b, d_ln_g, d_ln_b, d_pool_w, d_pool_scale = _bwd_mix(
        dx1, x2d, u_all, c_all, pooled_all, gw_mix, norm_mix_g, conv_w, conv_ln_g, conv_ln_b, pool_w[0], pool_scale,
        flight_b[4], seq, tm_mix)
    grad_x = dx.reshape(x.shape)

    small_grads = dict(norm_mix_g=dg_mix, conv_dw_b=d_conv_b, conv_ln_g=d_ln_g, conv_ln_b=d_ln_b, pool_w=d_pool_w,
                       pool_scale=d_pool_scale, norm_xattn_g=dg_x, norm_mem_g=dg_mem, norm_ffn_g=dg_ffn,
                       ffn_dw_b=d_ffn_b, norm_final_g=dg_final)
    small_list = [small_grads[n] for n, _ in SMALL] + [d_conv_w, d_ffn_w, loss_part[:1]]
    small_mine = _pack_rows(small_list)
    small_flight = _broadcast_start(
        lax.dynamic_update_slice(lax.empty((N_DEV,) + small_mine.shape, F32), small_mine[None], (dev, 0, 0)),
        "small_grads_broadcast_start", BARRIER_IDS["broadcast"])

    part["w_in"] = _wgrad(du, h1, "wgrad_w_in", after=small_flight[3])
    to_sibling_c = sibling_start(RS_GROUPS["c"], "c")
    updated_b = reduce_finish(RS_GROUPS["b"], parts_b, landed_b, flight_b, to_sibling_c[4], "b")
    parts_c, landed_c, flight_c = chips_start(to_sibling_c, updated_b, "c")
    updated_a = reduce_finish(RS_GROUPS["a"], parts_a, landed_a, flight_a, flight_c[4], "a")
    small_all = _broadcast_wait(*small_flight[:3], updated_a, "small_grads_broadcast_wait")
    small_sum = _sum_blocks(small_all).reshape(-1)

    pos = 0
    for n, shape in SMALL:
        size = 1
        for s in shape:
            size *= s
        grads[n] = small_sum[pos:pos + size].reshape(shape)
        pos += size
    full_conv_w = small_sum[pos:pos + CONV_WIDTH * D_CONV].reshape(CONV_WIDTH, D_CONV)
    pos += CONV_WIDTH * D_CONV
    full_ffn_w = small_sum[pos:pos + FFN_CONV_WIDTH * 2 * D_FF].reshape(FFN_CONV_WIDTH, 2 * D_FF)
    loss = small_sum[pos + FFN_CONV_WIDTH * 2 * D_FF]
    grads["conv_dw_w"] = lax.dynamic_slice_in_dim(full_conv_w, dev * (D_CONV // N_DEV), D_CONV // N_DEV, axis=1)[None]
    grads["ffn_dw_w"] = lax.dynamic_slice_in_dim(full_ffn_w, dev * (2 * D_FF // N_DEV), 2 * D_FF // N_DEV, axis=1)[None]

    small_names = [n for n in order if n not in W_OFF]
    swap = lambda t: jnp.transpose(t, (1, 0, 2))
    two_d = lambda t: t.reshape(1, -1) if t.ndim == 1 else (swap(t) if t.ndim == 3 else t)
    outs = _adamw_small(*[[two_d(t[n]) for n in small_names] for t in (weights, grads, moments_m, moments_v)])
    for res, out in zip((delta, new_m, new_v), outs):
        for n, o in zip(small_names, out):
            res[n] = swap(o) if o.ndim == 3 else o.reshape(weights[n].shape)

    reduce_finish(RS_GROUPS["c"], parts_c, landed_c, flight_c, delta[small_names[-1]], "c")

    return (loss, grad_x, *[grads[n] for n in order], *[delta[n] for n in order],
            *[new_m[n] for n in order], *[new_v[n] for n in order])
```

```python
import jax
import jax.numpy as jnp
from jax import lax
from jax.experimental import pallas as pl
from jax.experimental.pallas import tpu as pltpu

F32 = jnp.float32
BF16 = jnp.bfloat16
MESH = pl.DeviceIdType.MESH

N_DEV = 8
D_MODEL = 1024
D_CONV = 512
D_POOL = 512
CONV_WIDTH = 31
POOL_WINDOWS = (2, 4, 8, 16)
POOL_GROUP_DIM = 128
D_IN = 1536
N_MEM = 256
HEADS = 4
HEAD_DIM = 256
D_FF = 2816
FFN_CONV_WIDTH = 3
EPS = 1e-6
ADAM_LR = 0.001
ADAM_B1 = 0.9
ADAM_B2 = 0.999
ADAM_EPS = 1e-08
ADAM_WD = 0.01
ADAM_STEP = 10

VMEM_LIMIT_V7X = 56 * 1024 * 1024
CONV_HALO = 32
POOL_HALO = 16
FFN_HALO = 8
FFN_CHUNK = 2816
WGRAD_RESIDENT_BYTES = 8 * 1024 * 1024
WGRAD_TOKEN_BLOCK = 1024
BF16_TILE_ROWS = 16

W_ROWS = (("w_in", 192), ("w_out", 128), ("w_q", 128), ("w_kv", 256), ("w_o", 128), ("w_up", 704), ("w_down", 352))
AG_GROUPS = (("w_in", "w_out"), ("w_q", "w_kv", "w_o"), ("w_up", "w_down"))
W_OFF = {}
for _names in AG_GROUPS:
    _o = 0
    for _n in _names:
        W_OFF[_n] = (_o, dict(W_ROWS)[_n])
        _o += dict(W_ROWS)[_n]
RS_GROUPS = {"a": ("w_up", "w_down"), "b": ("w_q", "w_kv", "w_out"), "c": ("w_o", "w_in")}
BARRIER_IDS = {"gather_start": (None, 0, 1), "gather_forward": (11, 2, 3), "sibling": {"a": 4, "b": 5, "c": 6},
               "chips": {"a": 7, "b": 8, "c": 9}, "broadcast": 10}


def _dot(a, b):
    return jnp.dot(a, b, preferred_element_type=F32)


def _dot_nt(a, b):
    return lax.dot_general(a, b, (((1,), (1,)), ((), ())), preferred_element_type=F32)


def _dot_tn(a, b):
    return lax.dot_general(a, b, (((0,), (0,)), ((), ())), preferred_element_type=F32)


def _sigmoid(v):
    return 1.0 / (1.0 + jnp.exp(-v))


def _rms_fwd(v):
    r = lax.rsqrt(jnp.mean(v * v, axis=-1, keepdims=True) + EPS)
    return v * r, r


def _rms_bwd(dh, vh, r, g):
    gd = dh * g
    return r * (gd - vh * jnp.mean(gd * vh, axis=-1, keepdims=True))


def _sublane_shifts(v):
    rows = v.shape[0]
    return [v] + [pltpu.roll(v, rows - b, 0) for b in range(1, 8)]


def _colsum(v):
    return jnp.sum(v, axis=0, keepdims=True)


def _colsum_mxu(v):
    return _dot(jnp.ones((8, v.shape[0]), BF16), v.astype(BF16))[0:1, :]


def _full(shape):
    return pl.BlockSpec(shape, lambda *_: (0,) * len(shape))


def _params(sem=("arbitrary",), vmem=VMEM_LIMIT_V7X):
    return pltpu.CompilerParams(dimension_semantics=sem, vmem_limit_bytes=vmem)


def _load_weight(g_hbm, name, dst, sem):
    off, rows = W_OFF[name]
    return [pltpu.make_async_copy(g_hbm.at[d, pl.ds(off, rows), :], dst.at[pl.ds(d * rows, rows), :], sem)
            for d in range(N_DEV)]


def _start_weights(g_hbm, names, dsts, sems):
    @pl.when(pl.program_id(0) == 0)
    def _():
        copies = [_load_weight(g_hbm, name, dst, sems.at[k]) for k, (name, dst) in enumerate(zip(names, dsts))]
        for cp in sum(copies, []):
            cp.start()
        for cp in sum(copies, []):
            cp.wait()


def _position():
    x, y, c = lax.axis_index("x"), lax.axis_index("y"), lax.axis_index("c")
    chips = [(1 - x, y), (x, 1 - y), (1 - x, 1 - y)]
    return x, y, c, chips


def _dev(px, py, pc):
    return 4 * px + 2 * py + pc


_HBM =pl.BlockSpec(memory_space=pltpu.HBM)
_SEM = pl.BlockSpec(memory_space=pltpu.SEMAPHORE)
_SIDE_EFFECT = pltpu.SideEffectType.DATAFLOW_SIDE_EFFECTING


def _handshake(peers):
    barrier = pltpu.get_barrier_semaphore()
    for peer in peers:
        pl.semaphore_signal(barrier, inc=1, device_id=peer, device_id_type=MESH)
    pl.semaphore_wait(barrier, len(peers))


def _gather_start(buf, after, name, collective_id):
    def body(buf_ref, after_ref, send_sems, recv_sems, buf_thru, token):
        del after_ref, buf_thru
        x, y, c, chips = _position()
        rows = buf_ref.at[_dev(x, y, c)]
        targets = [(x, y, 1 - c)] + [(*chip, c) for chip in chips]
        if collective_id is not None:
            _handshake(targets)
        for k, to in enumerate(targets):
            pltpu.make_async_remote_copy(src_ref=rows, dst_ref=rows, send_sem=send_sems.at[k], recv_sem=recv_sems.at[k],
                                         device_id=to, device_id_type=MESH).start()
        token[...] = jnp.zeros_like(token)

    return pl.pallas_call(
        body, name=name,
        out_shape=(pltpu.SemaphoreType.DMA((4,)), pltpu.SemaphoreType.DMA((4,)), pltpu.HBM(buf.shape, buf.dtype),
                   jax.ShapeDtypeStruct((8, 128), F32)),
        in_specs=(_HBM, pl.BlockSpec(memory_space=pl.ANY)),
        out_specs=(_SEM, _SEM, _HBM, pl.BlockSpec(memory_space=pltpu.VMEM)),
        input_output_aliases={0: 2},
        compiler_params=pltpu.CompilerParams(has_side_effects=_SIDE_EFFECT, collective_id=collective_id),
    )(pltpu.with_memory_space_constraint(buf, pltpu.HBM), after)


def _gather_forward(send_sems, recv_sems, buf, after, name, collective_id):
    def body(buf_ref, send_sems, recv_sems, after_ref, fwd_send, fwd_recv, buf_thru):
        del after_ref, buf_thru
        x, y, c, chips = _position()
        sibling = (x, y, 1 - c)

        def copy(block, k, sends, recvs):
            rows = buf_ref.at[_dev(*block)]
            return pltpu.make_async_remote_copy(src_ref=rows, dst_ref=rows, send_sem=sends.at[k], recv_sem=recvs.at[k],
                                                device_id=sibling, device_id_type=MESH)

        _handshake([sibling])
        for k in range(4):
            copy((x, y, c), k, send_sems, recv_sems).wait_send()
        copy(sibling, 0, send_sems, recv_sems).wait_recv()
        for j, chip in enumerate(chips):
            copy((*chip, c), 1 + j, send_sems, recv_sems).wait_recv()
            copy((*chip, c), j, fwd_send, fwd_recv).start()

    return pl.pallas_call(
        body, name=name,
        out_shape=(pltpu.SemaphoreType.DMA((3,)), pltpu.SemaphoreType.DMA((3,)), pltpu.HBM(buf.shape, buf.dtype)),
        in_specs=(_HBM, _SEM, _SEM, pl.BlockSpec(memory_space=pl.ANY)), out_specs=(_SEM, _SEM, _HBM),
        input_output_aliases={0: 2},
        compiler_params=pltpu.CompilerParams(has_side_effects=_SIDE_EFFECT, collective_id=collective_id),
    )(buf, send_sems, recv_sems, after)


def _gather_finish(fwd_send, fwd_recv, buf, name):
    def body(buf_ref, fwd_send, fwd_recv, buf_thru):
        del buf_thru
        x, y, c, chips = _position()
        for j, chip in enumerate(chips):
            cp = pltpu.make_async_remote_copy(
                src_ref=buf_ref.at[_dev(*chip, c)], dst_ref=buf_ref.at[_dev(*chip, 1 - c)], send_sem=fwd_send.at[j],
                recv_sem=fwd_recv.at[j], device_id=(x, y, 1 - c), device_id_type=MESH)
            cp.wait_send()
            cp.wait_recv()

    return pl.pallas_call(
        body, name=name,
        out_shape=pltpu.HBM(buf.shape, buf.dtype),
        in_specs=(_HBM, _SEM, _SEM), out_specs=_HBM,
        input_output_aliases={0: 0},
        compiler_params=pltpu.CompilerParams(has_side_effects=_SIDE_EFFECT),
    )(buf, fwd_send, fwd_recv)


def _everyone_else(x, y, c, chips):
    return [(x, y, 1 - c)] + [(*chip, core) for chip in chips for core in (c, 1 - c)]


def _broadcast_start(buf, name, collective_id):
    def body(buf_ref, send_sems, recv_sems, buf_thru, token):
        del buf_thru
        x, y, c, chips = _position()
        rows = buf_ref.at[_dev(x, y, c)]
        _handshake(_everyone_else(x, y, c, chips))
        for k, to in enumerate(_everyone_else(x, y, c, chips)):
            pltpu.make_async_remote_copy(src_ref=rows, dst_ref=rows, send_sem=send_sems.at[k], recv_sem=recv_sems.at[k],
                                         device_id=to, device_id_type=MESH).start()
        token[...] = jnp.zeros_like(token)

    return pl.pallas_call(
        body, name=name,
        out_shape=(pltpu.SemaphoreType.DMA((7,)), pltpu.SemaphoreType.DMA((7,)), pltpu.HBM(buf.shape, buf.dtype),
                   jax.ShapeDtypeStruct((8, 128), F32)),
        in_specs=(_HBM,), out_specs=(_SEM, _SEM, _HBM, pl.BlockSpec(memory_space=pltpu.VMEM)),
        input_output_aliases={0: 2},
        compiler_params=pltpu.CompilerParams(has_side_effects=_SIDE_EFFECT, collective_id=collective_id),
    )(pltpu.with_memory_space_constraint(buf, pltpu.HBM))


def _broadcast_wait(send_sems, recv_sems, buf, after, name):
    def body(buf_ref, send_sems, recv_sems, after_ref, buf_thru):
        del after_ref, buf_thru
        x, y, c, chips = _position()
        for k, peer in enumerate(_everyone_else(x, y, c, chips)):
            cp = pltpu.make_async_remote_copy(
                src_ref=buf_ref.at[_dev(x, y, c)], dst_ref=buf_ref.at[_dev(*peer)], send_sem=send_sems.at[k],
                recv_sem=recv_sems.at[k], device_id=peer, device_id_type=MESH)
            cp.wait_send()
            cp.wait_recv()

    return pl.pallas_call(
        body, name=name,
        out_shape=pltpu.HBM(buf.shape, buf.dtype),
        in_specs=(_HBM, _SEM, _SEM, pl.BlockSpec(memory_space=pl.ANY)), out_specs=_HBM,
        input_output_aliases={0: 0},
        compiler_params=pltpu.CompilerParams(has_side_effects=_SIDE_EFFECT),
    )(buf, send_sems, recv_sems, after)


def _to_sibling(j, x, y, c, chips):
    return _dev(*([(x, y)] + chips)[j], 1 - c), (x, y, 1 - c)


def _to_chip(j, x, y, c, chips):
    return j, (*chips[j], c)


def _exchange_start(srcs, n_slots, route, name, collective_id):
    n = len(srcs)

    def body(*refs):
        s_refs, land_refs = refs[:n], refs[n:2 * n]
        send_sems, recv_sems = refs[2 * n:2 * n + 2]
        token = refs[-1]
        x, y, c, chips = _position()
        _handshake([(x, y, 1 - c)] if route is _to_sibling else [route(j, x, y, c, chips)[1] for j in range(n_slots)])
        for k in range(n):
            for j in range(n_slots):
                block, to = route(j, x, y, c, chips)
                pltpu.make_async_remote_copy(
                    src_ref=s_refs[k].at[block], dst_ref=land_refs[k].at[j], send_sem=send_sems.at[n_slots * k + j],
                    recv_sem=recv_sems.at[n_slots * k + j], device_id=to, device_id_type=MESH).start()
        token[...] = jnp.zeros_like(token)

    lands = [jax.ShapeDtypeStruct((n_slots,) + s.shape[1:], s.dtype) for s in srcs]
    outs = pl.pallas_call(
        body, name=name,
        out_shape=(pltpu.SemaphoreType.DMA((n_slots * n,)), pltpu.SemaphoreType.DMA((n_slots * n,)),
                   *[pltpu.HBM(s.shape, s.dtype) for s in srcs], *[pltpu.HBM(l.shape, l.dtype) for l in lands],
                   jax.ShapeDtypeStruct((8, 128), F32)),
        in_specs=[_HBM] * (2 * n), out_specs=(_SEM, _SEM, *[_HBM] * (2 * n), pl.BlockSpec(memory_space=pltpu.VMEM)),
        input_output_aliases={k: 2 + k for k in range(2 * n)},
        compiler_params=pltpu.CompilerParams(has_side_effects=_SIDE_EFFECT, collective_id=collective_id),
    )(*[pltpu.with_memory_space_constraint(s, pltpu.HBM) for s in srcs],
      *[pltpu.with_memory_space_constraint(lax.empty(l.shape, l.dtype), pltpu.HBM) for l in lands])
    return outs[0], outs[1], outs[2:2 + n], outs[2 + n:2 + 2 * n], outs[-1]


def _exchange_wait(send_sems, recv_sems, s_thru, land_thru, after, n_slots, route, name):
    n = len(s_thru)

    def body(*refs):
        s_refs, land_refs = refs[:n], refs[n:2 * n]
        send_sems, recv_sems = refs[2 * n:2 * n + 2]
        x, y, c, chips = _position()
        for k in range(n):
            for j in range(n_slots):
                block, to = route(j, x, y, c, chips)
                cp = pltpu.make_async_remote_copy(
                    src_ref=s_refs[k].at[block], dst_ref=land_refs[k].at[j], send_sem=send_sems.at[n_slots * k + j],
                    recv_sem=recv_sems.at[n_slots * k + j], device_id=to, device_id_type=MESH)
                cp.wait_send()
                cp.wait_recv()

    outs = pl.pallas_call(
        body, name=name,
        out_shape=(*[pltpu.HBM(s.shape, s.dtype) for s in s_thru], *[pltpu.HBM(l.shape, l.dtype) for l in land_thru]),
        in_specs=[_HBM] * (2 * n) + [_SEM, _SEM, pl.BlockSpec(memory_space=pl.ANY)], out_specs=[_HBM] * (2 * n),
        input_output_aliases={k: k for k in range(2 * n)},
        compiler_params=pltpu.CompilerParams(has_side_effects=_SIDE_EFFECT),
    )(*s_thru, *land_thru, send_sems, recv_sems, after)
    return outs[:n], outs[n:]


def _owner_table():
    x, y, c = lax.axis_index("x"), lax.axis_index("y"), lax.axis_index("c")
    chips = [(x, y), (1 - x, y), (x, 1 - y), (1 - x, 1 - y)]
    return jnp.stack([_dev(px, py, c) for px, py in chips]).astype(jnp.int32)


def _chip_partial_sums(table, parts, from_sibling, name):
    n = len(parts)

    def body(tab_ref, *refs):
        del tab_ref
        for g_ref, l_ref, out_ref in zip(refs[:n], refs[n:2 * n], refs[2 * n:]):
            out_ref[...] = (g_ref[...].astype(F32) + l_ref[...].astype(F32)).astype(out_ref.dtype)

    block = lambda p: (None,) + p.shape[1:]
    grid_spec = pltpu.PrefetchScalarGridSpec(
        num_scalar_prefetch=1, grid=(3,),
        in_specs=[pl.BlockSpec(block(p), lambda j, tab: (tab[j + 1], 0, 0)) for p in parts]
        + [pl.BlockSpec(block(p), lambda j, tab: (j + 1, 0, 0)) for p in parts],
        out_specs=[pl.BlockSpec(block(p), lambda j, tab: (j, 0, 0)) for p in parts])
    return pl.pallas_call(
        body, name=name, grid_spec=grid_spec,
        out_shape=[jax.ShapeDtypeStruct((3,) + p.shape[1:], BF16) for p in parts],
        compiler_params=_params(("arbitrary",)),
    )(table, *parts, *from_sibling)


def _final_update(table, parts, from_sibling, from_chips, states, name):
    n = len(parts)
    flipped = [states[k][0].shape != parts[k].shape[1:] for k in range(n)]

    def body(tab_ref, *refs):
        del tab_ref
        ins, outs = refs[:6 * n], refs[6 * n:]
        for k in range(n):
            acc = ins[k][...].astype(F32) + ins[n + k][...].astype(F32)
            for j in range(3):
                acc = acc + ins[2 * n + k][j].astype(F32)
            if flipped[k]:
                acc = acc.T
            w_ref, m_ref, v_ref = ins[3 * n + 3 * k:3 * n + 3 * k + 3]
            outs[4 * k][...] = acc
            for out_ref, val in zip(outs[4 * k + 1:4 * k + 4], _adamw_update(w_ref[...], acc, m_ref[...], v_ref[...])):
                out_ref[...] = val

    def grad_block(k, lead, at):
        r, c = parts[k].shape[1:]
        if flipped[k]:
            return pl.BlockSpec(lead + (r, c // 2), lambda t, tab: (*at(tab), 0, t))
        return pl.BlockSpec(lead + (r // 2, c), lambda t, tab: (*at(tab), t, 0))

    def state_block(k):
        a, b = states[k][0].shape
        return pl.BlockSpec((a // 2, b), lambda t, tab: (t, 0))

    grid_spec = pltpu.PrefetchScalarGridSpec(
        num_scalar_prefetch=1, grid=(2,),
        in_specs=[grad_block(k, (None,), lambda tab: (tab[0],)) for k in range(n)]
        + [grad_block(k, (None,), lambda tab: (0,)) for k in range(n)]
        + [grad_block(k, (3,), lambda tab: (0,)) for k in range(n)]
        + [state_block(k) for k in range(n) for _ in range(3)],
        out_specs=[state_block(k) for k in range(n) for _ in range(4)])
    outs = pl.pallas_call(
        body, name=name, grid_spec=grid_spec,
        out_shape=[jax.ShapeDtypeStruct(states[k][0].shape, F32) for k in range(n) for _ in range(4)],
        compiler_params=_params(("arbitrary",)),
    )(table, *parts, *from_sibling, *from_chips, *[t for k in range(n) for t in states[k]])
    return [outs[4 * k:4 * k + 4] for k in range(n)]


def _sum_blocks(g8):
    _, rows, cols = g8.shape

    def body(g_ref, out_ref):
        acc = g_ref[0]
        for d in range(1, N_DEV):
            acc = acc + g_ref[d]
        out_ref[...] = acc

    return pl.pallas_call(
        body, name="small_grad_sum", grid=(1,),
        in_specs=[_full((N_DEV, rows, cols))], out_specs=_full((rows, cols)),
        out_shape=jax.ShapeDtypeStruct((rows, cols), F32),
        compiler_params=_params(("arbitrary",)),
    )(g8)


def _fwd_mix(x2d, gw, g_mix, conv_w, conv_b, ln_g, ln_b, pool_w, pool_scale, after, seq, tm):
    tokens = x2d.shape[0]
    n_tiles = tokens // tm
    tps = seq // tm

    def body(x_ref, gmix_ref, gw_hbm, cw_ref, cb_ref, lng_ref, lnb_ref, pw_ref, ps_ref, after_ref,
             x1_ref, u_ref, c_ref, pooled_ref, ymix_ref, h1_ref,
             win_v, wout_v, hc_carry, up_carry, sem):
        del after_ref
        i = pl.program_id(0)

        _start_weights(gw_hbm, ("w_in", "w_out"), (win_v, wout_v), sem)

        @pl.when(i % tps == 0)
        def _():
            hc_carry[...] = jnp.zeros_like(hc_carry)
            up_carry[...] = jnp.zeros_like(up_carry)

        x = x_ref[...]
        xh, _ = _rms_fwd(x)
        h1 = (xh * gmix_ref[...]).astype(BF16)
        h1_ref[...] = h1
        u = _dot_nt(h1, win_v[...])
        u_ref[...] = u
        val, gate, up = u[:, :D_CONV], u[:, D_CONV:2 * D_CONV], u[:, 2 * D_CONV:]

        extp = jnp.concatenate([up_carry[...], up], axis=0)
        up_carry[...] = up[tm - POOL_HALO:, :]
        pos = lax.broadcasted_iota(jnp.int32, (tm, 1), 0) + (i % tps) * tm
        run = extp
        mixed = []
        for g, w in enumerate(POOL_WINDOWS):
            lo = g * POOL_GROUP_DIM
            run = run[:, POOL_GROUP_DIM if g else 0:]
            run = run + pltpu.roll(run, w // 2, 0)
            cnt = jnp.minimum(pos + 1, w).astype(F32)
            pooled = run[POOL_HALO:, :POOL_GROUP_DIM] / cnt - up[:, lo:lo + POOL_GROUP_DIM]
            pooled = pooled.astype(BF16)
            pooled_ref[:, lo:lo + POOL_GROUP_DIM] = pooled
            mixed.append(_dot(pooled, pw_ref[g].astype(BF16)))
        y_pool = jnp.concatenate(mixed, axis=-1) * ps_ref[...]
        y_pool = y_pool.astype(BF16)
        ymix_ref[:, D_CONV:] = y_pool
        out = _dot(y_pool, wout_v[D_CONV:, :])

        hc = val * _sigmoid(gate)
        ext = jnp.concatenate([hc_carry[...], hc], axis=0)
        hc_carry[...] = hc[tm - CONV_HALO:, :]
        conv = jnp.broadcast_to(cb_ref[...], (tm, D_CONV))
        ahead_by = _sublane_shifts(ext)
        for k in range(CONV_WIDTH):
            whole, part = divmod(CONV_HALO - (CONV_WIDTH - 1) + k, 8)
            conv = conv + cw_ref[k:k + 1, :] * ahead_by[part][8 * whole:8 * whole + tm, :]
        c_ref[...] = conv
        mu = jnp.mean(conv, axis=-1, keepdims=True)
        cen = conv - mu
        ln = cen * lax.rsqrt(jnp.mean(cen * cen, axis=-1, keepdims=True) + EPS) * lng_ref[...] + lnb_ref[...]
        y_conv = ln * _sigmoid(ln)
        y_conv = y_conv.astype(BF16)
        ymix_ref[:, :D_CONV] = y_conv
        x1_ref[...] = x + (out + _dot(y_conv, wout_v[:D_CONV, :]))

    row = lambda w: pl.BlockSpec((tm, w), lambda i: (i, 0))
    return pl.pallas_call(
        body, name="fwd_mix", grid=(n_tiles,),
        in_specs=[row(D_MODEL), _full((1, D_MODEL)), pl.BlockSpec(memory_space=pl.ANY),
                  _full((CONV_WIDTH, D_CONV)), _full((1, D_CONV)), _full((1, D_CONV)), _full((1, D_CONV)),
                  _full((4, POOL_GROUP_DIM, POOL_GROUP_DIM)), _full((1, D_POOL)), _full(after.shape)],
        out_specs=[row(D_MODEL), row(D_IN), row(D_CONV), row(D_POOL), row(D_MODEL), row(D_MODEL)],
        out_shape=[jax.ShapeDtypeStruct((tokens, D_MODEL), F32), jax.ShapeDtypeStruct((tokens, D_IN), F32),
                   jax.ShapeDtypeStruct((tokens, D_CONV), F32), jax.ShapeDtypeStruct((tokens, D_POOL), BF16),
                   jax.ShapeDtypeStruct((tokens, D_MODEL), BF16), jax.ShapeDtypeStruct((tokens, D_MODEL), BF16)],
        scratch_shapes=[pltpu.VMEM((D_IN, D_MODEL), BF16), pltpu.VMEM((D_MODEL, D_MODEL), BF16),
                        pltpu.VMEM((CONV_HALO, D_CONV), F32), pltpu.VMEM((POOL_HALO, D_POOL), F32),
                        pltpu.SemaphoreType.DMA((2,))],
        compiler_params=_params(),
    )(x2d, g_mix, gw, conv_w, conv_b, ln_g, ln_b, pool_w, pool_scale, after)


def _fwd_kv(mem2d, gw, g_mem):
    rows = mem2d.shape[0]
    n_b = rows // N_MEM

    def body(mem_ref, g_ref, gw_hbm, mn_ref, kv_ref, wkv_v, sem):
        @pl.when(pl.program_id(0) == 0)
        def _():
            copies = _load_weight(gw_hbm, "w_kv", wkv_v, sem)
            for cp in copies:
                cp.start()
            for cp in copies:
                cp.wait()

        mh, _ = _rms_fwd(mem_ref[...])
        mn = (mh * g_ref[...]).astype(BF16)
        mn_ref[...] = mn
        kv_ref[...] = _dot_nt(mn, wkv_v[...]).astype(BF16)

    return pl.pallas_call(
        body, name="fwd_kv", grid=(n_b,),
        in_specs=[pl.BlockSpec((N_MEM, D_MODEL), lambda b: (b, 0)), _full((1, D_MODEL)), pl.BlockSpec(memory_space=pl.ANY)],
        out_specs=[pl.BlockSpec((N_MEM, D_MODEL), lambda b: (b, 0)), pl.BlockSpec((N_MEM, 2 * D_MODEL), lambda b: (b, 0))],
        out_shape=[jax.ShapeDtypeStruct((rows, D_MODEL), BF16), jax.ShapeDtypeStruct((rows, 2 * D_MODEL), BF16)],
        scratch_shapes=[pltpu.VMEM((2 * D_MODEL, D_MODEL), BF16), pltpu.SemaphoreType.DMA],
        compiler_params=_params(),
    )(mem2d, g_mem, gw)


def _softmax_rows(s):
    e = jnp.exp(s - jnp.max(s, axis=-1, keepdims=True))
    return e / jnp.sum(e, axis=-1, keepdims=True)


def _fwd_attn(x1, kv, gw, g_x, seq, tm):
    tokens = x1.shape[0]
    n_tiles = tokens // tm
    tps = seq // tm

    def body(x1_ref, kv_ref, g_ref, gw_hbm, x2_ref, h2_ref, q_ref, o_ref, wq_v, wo_v, sem):
        _start_weights(gw_hbm, ("w_q", "w_o"), (wq_v, wo_v), sem)
        x1v = x1_ref[...]
        xh, _ = _rms_fwd(x1v)
        h2 = (xh * g_ref[...]).astype(BF16)
        h2_ref[...] = h2
        q = (_dot(h2, wq_v[...]) * (HEAD_DIM ** -0.5)).astype(BF16)
        q_ref[...] = q
        heads = [slice(h * HEAD_DIM, (h + 1) * HEAD_DIM) for h in range(HEADS)]
        scores = [_dot_nt(q[:, hd], kv_ref[:, hd]) for hd in heads]
        probs = [_softmax_rows(s).astype(BF16) for s in scores]
        outs = [_dot(p, kv_ref[:, pl.ds(D_MODEL + h * HEAD_DIM, HEAD_DIM)]) for h, p in enumerate(probs)]
        o = jnp.concatenate(outs, axis=-1).astype(BF16)
        o_ref[...] = o
        x2_ref[...] = x1v + _dot(o, wo_v[...])

    row = lambda w: pl.BlockSpec((tm, w), lambda i: (i, 0))
    return pl.pallas_call(
        body, name="fwd_attn", grid=(n_tiles,),
        in_specs=[row(D_MODEL), pl.BlockSpec((N_MEM, 2 * D_MODEL), lambda i: (i // tps, 0)), _full((1, D_MODEL)),
                  pl.BlockSpec(memory_space=pl.ANY)],
        out_specs=[row(D_MODEL)] * 4,
        out_shape=[jax.ShapeDtypeStruct((tokens, D_MODEL), F32)] + [jax.ShapeDtypeStruct((tokens, D_MODEL), BF16)] * 3,
        scratch_shapes=[pltpu.VMEM((D_MODEL, D_MODEL), BF16), pltpu.VMEM((D_MODEL, D_MODEL), BF16), pltpu.SemaphoreType.DMA((2,))],
        compiler_params=_params(),
    )(x1, kv, g_x, gw)


def _ffn_conv(uu, halo, w_ref, b_ref, cols):
    ext = jnp.concatenate([halo, uu], axis=0)
    p1 = pltpu.roll(ext, 1, 0)[FFN_HALO:, :]
    p2 = pltpu.roll(ext, 2, 0)[FFN_HALO:, :]
    return b_ref[:, cols] + w_ref[2:3, cols] * uu + w_ref[1:2, cols] * p1 + w_ref[0:1, cols] * p2


def _fwd_ffn(x2, target, gw, g_ffn, ffn_w, ffn_b, g_final, seq, tm):
    tokens = x2.shape[0]
    n_tiles = tokens // tm
    tps = seq // tm
    n_chunks = D_FF // FFN_CHUNK

    def body(x2_ref, tgt_ref, gffn_ref, gw_hbm, fw_ref, fb_ref, gfin_ref,
             uu_ref, cc_ref, a_ref, h3_ref, dx3_ref, dx3b_ref, loss_ref, dgfin_ref,
             wup_v, wdown_v, carry, sem):
        i = pl.program_id(0)

        _start_weights(gw_hbm, ("w_up", "w_down"), (wup_v, wdown_v), sem)

        @pl.when(i == 0)
        def _():
            loss_ref[...] = jnp.zeros_like(loss_ref)
            dgfin_ref[...] = jnp.zeros_like(dgfin_ref)

        @pl.when(i % tps == 0)
        def _():
            carry[...] = jnp.zeros_like(carry)

        x2v = x2_ref[...]
        xh, _ = _rms_fwd(x2v)
        h3 = (xh * gffn_ref[...]).astype(BF16)
        h3_ref[...] = h3
        acc = jnp.zeros((tm, D_MODEL), F32)
        for jc in range(n_chunks):
            halves = []
            for half in range(2):
                cols = pl.ds(half * D_FF + jc * FFN_CHUNK, FFN_CHUNK)
                uu = _dot_nt(h3, wup_v[cols, :])
                uu_ref[:, cols] = uu.astype(BF16)
                cc = _ffn_conv(uu, carry[:, cols], fw_ref, fb_ref, cols)
                cc_ref[:, cols] = cc.astype(BF16)
                halves.append(cc)
                carry[:, cols] = uu[tm - FFN_HALO:, :]
            gate, val = halves
            a = (gate * _sigmoid(gate) * val).astype(BF16)
            a_ref[:, pl.ds(jc * FFN_CHUNK, FFN_CHUNK)] = a
            acc = acc + _dot(a, wdown_v[pl.ds(jc * FFN_CHUNK, FFN_CHUNK), :])
        x3 = x2v + acc

        xh3, r3 = _rms_fwd(x3)
        gfin = gfin_ref[...]
        err = xh3 * gfin - tgt_ref[...]
        loss_ref[...] += jnp.full(loss_ref.shape, jnp.sum(err * err) * (0.5 / D_MODEL), F32)
        dy = err * (1.0 / D_MODEL)
        dgfin_ref[...] += _colsum(dy * xh3)
        dx3 = _rms_bwd(dy, xh3, r3, gfin)
        dx3_ref[...] = dx3
        dx3b_ref[...] = dx3.astype(BF16)

    row = lambda w: pl.BlockSpec((tm, w), lambda i: (i, 0))
    return pl.pallas_call(
        body, name="fwd_ffn", grid=(n_tiles,),
        in_specs=[row(D_MODEL), row(D_MODEL), _full((1, D_MODEL)), pl.BlockSpec(memory_space=pl.ANY),
                  _full((FFN_CONV_WIDTH, 2 * D_FF)), _full((1, 2 * D_FF)), _full((1, D_MODEL))],
        out_specs=[row(2 * D_FF), row(2 * D_FF), row(D_FF), row(D_MODEL), row(D_MODEL), row(D_MODEL), _full((8, 128)),
                   _full((1, D_MODEL))],
        out_shape=[jax.ShapeDtypeStruct((tokens, 2 * D_FF), BF16), jax.ShapeDtypeStruct((tokens, 2 * D_FF), BF16),
                   jax.ShapeDtypeStruct((tokens, D_FF), BF16),
                   jax.ShapeDtypeStruct((tokens, D_MODEL), BF16), jax.ShapeDtypeStruct((tokens, D_MODEL), F32),
                   jax.ShapeDtypeStruct((tokens, D_MODEL), BF16),
                   jax.ShapeDtypeStruct((8, 128), F32), jax.ShapeDtypeStruct((1, D_MODEL), F32)],
        scratch_shapes=[pltpu.VMEM((2 * D_FF, D_MODEL), BF16), pltpu.VMEM((D_FF, D_MODEL), BF16),
                        pltpu.VMEM((FFN_HALO, 2 * D_FF), F32), pltpu.SemaphoreType.DMA((2,))],
        compiler_params=_params(),
    )(x2, target, g_ffn, gw, ffn_w, ffn_b, g_final)


def _bwd_ffn(dx3, x2, uu_all, cc_all, gw, g_ffn, ffn_w, seq, tm):
    tokens = x2.shape[0]
    n_tiles = tokens // tm
    tps = seq // tm
    n_chunks = D_FF // FFN_CHUNK

    def body(dx3_ref, x2_ref, uu_ref, cc_ref, gffn_ref, gw_hbm, fw_ref,
             dx2_ref, dx2b_ref, duu_ref, dfb_ref, dfw_ref, dg_ref,
             wup_v, wdown_v, carry, sem):
        i = pl.program_id(0)
        t = n_tiles - 1 - i

        _start_weights(gw_hbm, ("w_down", "w_up"), (wdown_v, wup_v), sem)

        @pl.when(i == 0)
        def _():
            dfb_ref[...] = jnp.zeros_like(dfb_ref)
            dfw_ref[...] = jnp.zeros_like(dfw_ref)
            dg_ref[...] = jnp.zeros_like(dg_ref)

        @pl.when(t % tps == tps - 1)
        def _():
            carry[...] = jnp.zeros_like(carry)

        dx3v = dx3_ref[...]
        dx3b = dx3v.astype(BF16)
        dh3 = jnp.zeros((tm, D_MODEL), F32)
        for jc in range(n_chunks):
            da = _dot_nt(dx3b, wdown_v[pl.ds(jc * FFN_CHUNK, FFN_CHUNK), :])
            colss = [pl.ds(half * D_FF + jc * FFN_CHUNK, FFN_CHUNK) for half in range(2)]
            gate, val = [cc_ref[:, cols].astype(F32) for cols in colss]
            sg = _sigmoid(gate)
            dgate = da * val * (sg * (1.0 + gate * (1.0 - sg)))
            dval = da * (gate * sg)
            for dcc, cols in zip((dgate, dval), colss):
                uu = uu_ref[:, cols].astype(F32)
                dfb_ref[:, cols] += _colsum(dcc)
                ext = jnp.concatenate([dcc, carry[:, cols]], axis=0)
                carry[:, cols] = dcc[:FFN_HALO, :]
                n1 = pltpu.roll(ext, tm + FFN_HALO - 1, 0)[:tm, :]
                n2 = pltpu.roll(ext, tm + FFN_HALO - 2, 0)[:tm, :]
                duu = fw_ref[2:3, cols] * dcc + fw_ref[1:2, cols] * n1 + fw_ref[0:1, cols] * n2
                dfw_ref[2:3, cols] += _colsum(uu * dcc)
                dfw_ref[1:2, cols] += _colsum(uu * n1)
                dfw_ref[0:1, cols] += _colsum(uu * n2)
                duub = duu.astype(BF16)
                duu_ref[:, cols] = duub
                dh3 = dh3 + _dot(duub, wup_v[cols, :])
        xh, r = _rms_fwd(x2_ref[...])
        dg_ref[...] += _colsum(dh3 * xh)
        dx2 = dx3v + _rms_bwd(dh3, xh, r, gffn_ref[...])
        dx2_ref[...] = dx2
        dx2b_ref[...] = dx2.astype(BF16)

    rev = lambda w: pl.BlockSpec((tm, w), lambda i: (n_tiles - 1 - i, 0))
    return pl.pallas_call(
        body, name="bwd_ffn", grid=(n_tiles,),
        in_specs=[rev(D_MODEL), rev(D_MODEL), rev(2 * D_FF), rev(2 * D_FF), _full((1, D_MODEL)),
                  pl.BlockSpec(memory_space=pl.ANY), _full((FFN_CONV_WIDTH, 2 * D_FF))],
        out_specs=[rev(D_MODEL), rev(D_MODEL), rev(2 * D_FF), _full((1, 2 * D_FF)), _full((FFN_CONV_WIDTH, 2 * D_FF)),
                   _full((1, D_MODEL))],
        out_shape=[jax.ShapeDtypeStruct((tokens, D_MODEL), F32), jax.ShapeDtypeStruct((tokens, D_MODEL), BF16),
                   jax.ShapeDtypeStruct((tokens, 2 * D_FF), BF16),
                   jax.ShapeDtypeStruct((1, 2 * D_FF), F32), jax.ShapeDtypeStruct((FFN_CONV_WIDTH, 2 * D_FF), F32),
                   jax.ShapeDtypeStruct((1, D_MODEL), F32)],
        scratch_shapes=[pltpu.VMEM((2 * D_FF, D_MODEL), BF16), pltpu.VMEM((D_FF, D_MODEL), BF16),
                        pltpu.VMEM((FFN_HALO, 2 * D_FF), F32), pltpu.SemaphoreType.DMA((2,))],
        compiler_params=_params(),
    )(dx3, x2, uu_all, cc_all, g_ffn, gw, ffn_w)


def _bwd_attn(dx2, x1, q, kv, gw, g_x, after, seq, tm):
    tokens = x1.shape[0]
    n_tiles = tokens // tm
    tps = seq // tm
    n_b = tokens // seq

    def body(dx2_ref, x1_ref, q_ref, kv_ref, g_ref, gw_hbm, after_ref, dx1_ref, dx1b_ref, dq_ref, dkv_ref, dg_ref,
             wq_v, wo_v, sem):
        del after_ref
        i = pl.program_id(0)

        _start_weights(gw_hbm, ("w_o", "w_q"), (wo_v, wq_v), sem)

        @pl.when(i == 0)
        def _():
            dg_ref[...] = jnp.zeros_like(dg_ref)

        @pl.when(i % tps == 0)
        def _():
            dkv_ref[...] = jnp.zeros_like(dkv_ref)

        dx2v = dx2_ref[...]
        do = _dot_nt(dx2v.astype(BF16), wo_v[...]).astype(BF16)
        q = q_ref[...]
        heads = [slice(h * HEAD_DIM, (h + 1) * HEAD_DIM) for h in range(HEADS)]
        kcols = [pl.ds(h * HEAD_DIM, HEAD_DIM) for h in range(HEADS)]
        vcols = [pl.ds(D_MODEL + h * HEAD_DIM, HEAD_DIM) for h in range(HEADS)]
        scores = [_dot_nt(q[:, hd], kv_ref[:, kc]) for hd, kc in zip(heads, kcols)]
        dps = [_dot_nt(do[:, hd], kv_ref[:, vc]) for hd, vc in zip(heads, vcols)]
        probs = [_softmax_rows(s) for s in scores]
        dss = [(p * (dp - jnp.sum(dp * p, axis=-1, keepdims=True))).astype(BF16) for p, dp in zip(probs, dps)]
        for p, hd, vc in zip(probs, heads, vcols):
            dkv_ref[:, vc] += _dot_tn(p.astype(BF16), do[:, hd])
        dqs = [_dot(ds, kv_ref[:, kc]) * (HEAD_DIM ** -0.5) for ds, kc in zip(dss, kcols)]
        for ds, hd, kc in zip(dss, heads, kcols):
            dkv_ref[:, kc] += _dot_tn(ds, q[:, hd])
        dq = jnp.concatenate(dqs, axis=-1).astype(BF16)
        dq_ref[...] = dq
        dh2 = _dot_nt(dq, wq_v[...])
        xh, r = _rms_fwd(x1_ref[...])
        dg_ref[...] += _colsum(dh2 * xh)
        dx1 = dx2v + _rms_bwd(dh2, xh, r, g_ref[...])
        dx1_ref[...] = dx1
        dx1b_ref[...] = dx1.astype(BF16)

    row = lambda w: pl.BlockSpec((tm, w), lambda i: (i, 0))
    per_b = pl.BlockSpec((N_MEM, 2 * D_MODEL), lambda i: (i // tps, 0))
    return pl.pallas_call(
        body, name="bwd_attn", grid=(n_tiles,),
        in_specs=[row(D_MODEL), row(D_MODEL), row(D_MODEL), per_b, _full((1, D_MODEL)), pl.BlockSpec(memory_space=pl.ANY),
                  _full(after.shape)],
        out_specs=[row(D_MODEL), row(D_MODEL), row(D_MODEL), per_b, _full((1, D_MODEL))],
        out_shape=[jax.ShapeDtypeStruct((tokens, D_MODEL), F32), jax.ShapeDtypeStruct((tokens, D_MODEL), BF16),
                   jax.ShapeDtypeStruct((tokens, D_MODEL), BF16),
                   jax.ShapeDtypeStruct((n_b * N_MEM, 2 * D_MODEL), F32), jax.ShapeDtypeStruct((1, D_MODEL), F32)],
        scratch_shapes=[pltpu.VMEM((D_MODEL, D_MODEL), BF16), pltpu.VMEM((D_MODEL, D_MODEL), BF16), pltpu.SemaphoreType.DMA((2,))],
        compiler_params=_params(),
    )(dx2, x1, q, kv, g_x, gw, after)


def _bwd_kv(dkv, mem2d, gw):
    rows = mem2d.shape[0]
    n_b = rows // N_MEM

    def body(dkv_ref, mem_ref, gw_hbm, dkvb_ref, dg_ref, wkv_v, sem):
        @pl.when(pl.program_id(0) == 0)
        def _():
            copies = _load_weight(gw_hbm, "w_kv", wkv_v, sem)
            for cp in copies:
                cp.start()
            for cp in copies:
                cp.wait()
            dg_ref[...] = jnp.zeros_like(dg_ref)

        dkvb = dkv_ref[...].astype(BF16)
        dkvb_ref[...] = dkvb
        dmn = _dot(dkvb, wkv_v[...])
        mh, _ = _rms_fwd(mem_ref[...])
        dg_ref[...] += _colsum(dmn * mh)

    return pl.pallas_call(
        body, name="bwd_kv", grid=(n_b,),
        in_specs=[pl.BlockSpec((N_MEM, 2 * D_MODEL), lambda b: (b, 0)), pl.BlockSpec((N_MEM, D_MODEL), lambda b: (b, 0)),
                  pl.BlockSpec(memory_space=pl.ANY)],
        out_specs=[pl.BlockSpec((N_MEM, 2 * D_MODEL), lambda b: (b, 0)), _full((1, D_MODEL))],
        out_shape=[jax.ShapeDtypeStruct((rows, 2 * D_MODEL), BF16), jax.ShapeDtypeStruct((1, D_MODEL), F32)],
        scratch_shapes=[pltpu.VMEM((2 * D_MODEL, D_MODEL), BF16), pltpu.SemaphoreType.DMA],
        compiler_params=_params(),
    )(dkv, mem2d, gw)


def _bwd_mix(dx1, x2d, u_all, c_all, pooled_all, gw, g_mix, conv_w, ln_g, ln_b, pool_w, pool_scale, after, seq, tm):
    tokens = x2d.shape[0]
    n_tiles = tokens // tm
    tps = seq // tm

    def body(dx1_ref, x_ref, u_ref, c_ref, pooled_ref, gmix_ref, gw_hbm, cw_ref, lng_ref, lnb_ref, pw_ref, ps_ref,
             after_ref, dx_ref, du_ref, dgmix_ref, dcw_ref, dcb_ref, dlng_ref, dlnb_ref, dpw_ref, dps_ref,
             win_v, wout_v, dc_carry, e_carry, sem):
        del after_ref
        i = pl.program_id(0)
        t = n_tiles - 1 - i

        _start_weights(gw_hbm, ("w_out", "w_in"), (wout_v, win_v), sem)

        @pl.when(i == 0)
        def _():
            for ref in (dgmix_ref, dcw_ref, dcb_ref, dlng_ref, dlnb_ref, dpw_ref, dps_ref):
                ref[...] = jnp.zeros_like(ref)

        @pl.when(t % tps == tps - 1)
        def _():
            dc_carry[...] = jnp.zeros_like(dc_carry)
            e_carry[...] = jnp.zeros_like(e_carry)

        dx1v = dx1_ref[...]
        dymix = _dot_nt(dx1v.astype(BF16), wout_v[...])
        dyc, dyp = dymix[:, :D_CONV], dymix[:, D_CONV:]
        u = u_ref[...]
        val, gate = u[:, :D_CONV], u[:, D_CONV:2 * D_CONV]

        conv = c_ref[...]
        mu = jnp.mean(conv, axis=-1, keepdims=True)
        cen = conv - mu
        rs = lax.rsqrt(jnp.mean(cen * cen, axis=-1, keepdims=True) + EPS)
        chat = cen * rs
        ln = chat * lng_ref[...] + lnb_ref[...]
        sl = _sigmoid(ln)
        dln = dyc * (sl * (1.0 + ln * (1.0 - sl)))
        dlng_ref[...] += _colsum(dln * chat)
        dlnb_ref[...] += _colsum(dln)
        dchat = dln * lng_ref[...]
        dc = rs * (dchat - jnp.mean(dchat, axis=-1, keepdims=True)
                   - chat * jnp.mean(dchat * chat, axis=-1, keepdims=True))
        dcb_ref[...] += _colsum(dc)
        sg = _sigmoid(gate)
        hc = val * sg
        ext = jnp.concatenate([dc, dc_carry[...]], axis=0)
        dc_carry[...] = dc[:CONV_HALO, :]
        dhc = jnp.zeros((tm, D_CONV), F32)
        ahead_by = _sublane_shifts(ext)
        for k in range(CONV_WIDTH):
            whole, part = divmod(CONV_WIDTH - 1 - k, 8)
            tap = ahead_by[part][8 * whole:8 * whole + tm, :]
            dhc = dhc + cw_ref[k:k + 1, :] * tap
            dcw_ref[k:k + 1, :] += _colsum_mxu(hc * tap)
        du_ref[:, :D_CONV] = (dhc * sg).astype(BF16)
        du_ref[:, D_CONV:2 * D_CONV] = (dhc * val * (sg * (1.0 - sg))).astype(BF16)

        pos = lax.broadcasted_iota(jnp.int32, (tm, 1), 0) + (t % tps) * tm
        es, dpooled = [], []
        for g, w in enumerate(POOL_WINDOWS):
            cols = pl.ds(g * POOL_GROUP_DIM, POOL_GROUP_DIM)
            lo = g * POOL_GROUP_DIM
            pooled = pooled_ref[:, cols]
            pw = pw_ref[g].astype(BF16)
            dyg = dyp[:, lo:lo + POOL_GROUP_DIM]
            dps_ref[:, cols] += _colsum(dyg * _dot(pooled, pw))
            dmixed = (dyg * ps_ref[:, cols]).astype(BF16)
            dpw_ref[g] += _dot_tn(pooled, dmixed)
            dpo = _dot_nt(dmixed, pw)
            dpooled.append(dpo)
            es.append(dpo / jnp.minimum(pos + 1, w).astype(F32))
        e = jnp.concatenate(es, axis=-1)
        run = jnp.concatenate([e, e_carry[...]], axis=0)
        e_carry[...] = e[:POOL_HALO, :]
        rows = tm + POOL_HALO
        for g, w in enumerate(POOL_WINDOWS):
            lo = g * POOL_GROUP_DIM
            run = run[:, POOL_GROUP_DIM if g else 0:]
            run = run + pltpu.roll(run, rows - w // 2, 0)
            du_ref[:, 2 * D_CONV + lo:2 * D_CONV + lo + POOL_GROUP_DIM] = (
                run[:tm, :POOL_GROUP_DIM] - dpooled[g]).astype(BF16)

        dh1 = _dot(du_ref[...], win_v[...])
        xh, r = _rms_fwd(x_ref[...])
        dgmix_ref[...] += _colsum(dh1 * xh)
        dx_ref[...] = dx1v + _rms_bwd(dh1, xh, r, gmix_ref[...])

    rev = lambda w: pl.BlockSpec((tm, w), lambda i: (n_tiles - 1 - i, 0))
    return pl.pallas_call(
        body, name="bwd_mix", grid=(n_tiles,),
        in_specs=[rev(D_MODEL), rev(D_MODEL), rev(D_IN), rev(D_CONV), rev(D_POOL), _full((1, D_MODEL)),
                  pl.BlockSpec(memory_space=pl.ANY), _full((CONV_WIDTH, D_CONV)), _full((1, D_CONV)), _full((1, D_CONV)),
                  _full((4, POOL_GROUP_DIM, POOL_GROUP_DIM)), _full((1, D_POOL)), _full(after.shape)],
        out_specs=[rev(D_MODEL), rev(D_IN), _full((1, D_MODEL)), _full((CONV_WIDTH, D_CONV)), _full((1, D_CONV)),
                   _full((1, D_CONV)), _full((1, D_CONV)), _full((4, POOL_GROUP_DIM, POOL_GROUP_DIM)), _full((1, D_POOL))],
        out_shape=[jax.ShapeDtypeStruct((tokens, D_MODEL), F32), jax.ShapeDtypeStruct((tokens, D_IN), BF16),
                   jax.ShapeDtypeStruct((1, D_MODEL), F32), jax.ShapeDtypeStruct((CONV_WIDTH, D_CONV), F32),
                   jax.ShapeDtypeStruct((1, D_CONV), F32), jax.ShapeDtypeStruct((1, D_CONV), F32),
                   jax.ShapeDtypeStruct((1, D_CONV), F32),
                   jax.ShapeDtypeStruct((4, POOL_GROUP_DIM, POOL_GROUP_DIM), F32), jax.ShapeDtypeStruct((1, D_POOL), F32)],
        scratch_shapes=[pltpu.VMEM((D_IN, D_MODEL), BF16), pltpu.VMEM((D_MODEL, D_MODEL), BF16),
                        pltpu.VMEM((CONV_HALO, D_CONV), F32), pltpu.VMEM((POOL_HALO, D_POOL), F32),
                        pltpu.SemaphoreType.DMA((2,))],
        compiler_params=_params(),
    )(dx1, x2d, u_all, c_all, pooled_all, g_mix, gw, conv_w, ln_g, ln_b, pool_w, pool_scale, after)


def _wgrad(a, b, name, after=None):
    tokens, m = a.shape
    n = b.shape[1]
    tm = 512 if m % 512 == 0 else 256
    extra = [] if after is None else [after]

    if m * n * 4 <= WGRAD_RESIDENT_BYTES:
        tk = min(WGRAD_TOKEN_BLOCK, tokens)
        n_k = tokens // tk

        def walk(a_ref, b_ref, *rest):
            out_ref, acc = rest[-2:]
            k = pl.program_id(0)

            @pl.when(k == 0)
            def _():
                acc[...] = jnp.zeros_like(acc)

            acc[...] += _dot_tn(a_ref[...], b_ref[...])

            @pl.when(k == n_k - 1)
            def _():
                out_ref[...] = acc[...].astype(out_ref.dtype)

        return pl.pallas_call(
            walk, name=name, grid=(n_k,),
            in_specs=[pl.BlockSpec((tk, m), lambda k: (k, 0)), pl.BlockSpec((tk, n), lambda k: (k, 0))] + [
                _full(t.shape) for t in extra],
            out_specs=_full((m, n)),
            out_shape=jax.ShapeDtypeStruct((m, n), BF16),
            scratch_shapes=[pltpu.VMEM((m, n), F32)],
            compiler_params=_params(),
        )(a, b, *extra)

    def body(a_ref, b_ref, *rest):
        rest[-1][...] = _dot_tn(a_ref[...], b_ref[...]).astype(rest[-1].dtype)

    return pl.pallas_call(
        body, name=name, grid=(m // tm,),
        in_specs=[pl.BlockSpec((tokens, tm), lambda i: (0, i)), _full((tokens, n))] + [_full(t.shape) for t in extra],
        out_specs=pl.BlockSpec((tm, n), lambda i: (i, 0)),
        out_shape=jax.ShapeDtypeStruct((m, n), BF16),
        compiler_params=_params(),
    )(a, b, *extra)


def _adamw_update(w, g, m, v):
    nm = ADAM_B1 * m + (1.0 - ADAM_B1) * g
    nv = ADAM_B2 * v + (1.0 - ADAM_B2) * (g * g)
    m_hat = nm / (1.0 - ADAM_B1 ** ADAM_STEP)
    v_hat = nv / (1.0 - ADAM_B2 ** ADAM_STEP)
    return -ADAM_LR * (m_hat / (jnp.sqrt(v_hat) + ADAM_EPS) + ADAM_WD * w), nm, nv


def _adamw_small(ws, gs, ms, vs):
    n = len(ws)

    def body(*refs):
        ins, outs = refs[:4 * n], refs[4 * n:]
        for k in range(n):
            d, nm, nv = _adamw_update(*[ins[j * n + k][...] for j in range(4)])
            outs[k][...] = d
            outs[n + k][...] = nm
            outs[2 * n + k][...] = nv

    vmem = pl.BlockSpec(memory_space=pltpu.VMEM)
    outs = pl.pallas_call(
        body, name="adamw_small",
        in_specs=[vmem] * (4 * n), out_specs=[vmem] * (3 * n),
        out_shape=[jax.ShapeDtypeStruct(w.shape, F32) for w in ws] * 3,
    )(*ws, *gs, *ms, *vs)
    return outs[:n], outs[n:2 * n], outs[2 * n:]


SMALL = (("norm_mix_g", (1, 1024)), ("conv_dw_b", (1, 512)), ("conv_ln_g", (1, 512)), ("conv_ln_b", (1, 512)),
         ("pool_w", (1, 4, 128, 128)), ("pool_scale", (1, 512)), ("norm_xattn_g", (1, 1024)), ("norm_mem_g", (1, 1024)),
         ("norm_ffn_g", (1, 1024)), ("ffn_dw_b", (1, 5632)), ("norm_final_g", (1024,)))
LANES = 128


def _pack_rows(arrs):
    flat = jnp.concatenate([a.reshape(-1) for a in arrs])
    pad = (-flat.shape[0]) % (8 * LANES)
    return jnp.pad(flat, (0, pad)).reshape(-1, LANES)


def kernel(x, mem, norm_mix_g, w_in, conv_dw_w, conv_dw_b, conv_ln_g, conv_ln_b, pool_w, pool_scale, w_out, norm_xattn_g, norm_mem_g, w_q, w_kv, w_o, norm_ffn_g, w_up, ffn_dw_w, ffn_dw_b, w_down, norm_final_g, loss_target, m_norm_mix_g, m_w_in, m_conv_dw_w, m_conv_dw_b, m_conv_ln_g, m_conv_ln_b, m_pool_w, m_pool_scale, m_w_out, m_norm_xattn_g, m_norm_mem_g, m_w_q, m_w_kv, m_w_o, m_norm_ffn_g, m_w_up, m_ffn_dw_w, m_ffn_dw_b, m_w_down, m_norm_final_g, v_norm_mix_g, v_w_in, v_conv_dw_w, v_conv_dw_b, v_conv_ln_g, v_conv_ln_b, v_pool_w, v_pool_scale, v_w_out, v_norm_xattn_g, v_norm_mem_g, v_w_q, v_w_kv, v_w_o, v_norm_ffn_g, v_w_up, v_ffn_dw_w, v_ffn_dw_b, v_w_down, v_norm_final_g):
    weights = dict(norm_mix_g=norm_mix_g, w_in=w_in, conv_dw_w=conv_dw_w, conv_dw_b=conv_dw_b, conv_ln_g=conv_ln_g,
                   conv_ln_b=conv_ln_b, pool_w=pool_w, pool_scale=pool_scale, w_out=w_out, norm_xattn_g=norm_xattn_g,
                   norm_mem_g=norm_mem_g, w_q=w_q, w_kv=w_kv, w_o=w_o, norm_ffn_g=norm_ffn_g, w_up=w_up,
                   ffn_dw_w=ffn_dw_w, ffn_dw_b=ffn_dw_b, w_down=w_down, norm_final_g=norm_final_g)
    moments_m = dict(norm_mix_g=m_norm_mix_g, w_in=m_w_in, conv_dw_w=m_conv_dw_w, conv_dw_b=m_conv_dw_b,
                     conv_ln_g=m_conv_ln_g, conv_ln_b=m_conv_ln_b, pool_w=m_pool_w, pool_scale=m_pool_scale,
                     w_out=m_w_out, norm_xattn_g=m_norm_xattn_g, norm_mem_g=m_norm_mem_g, w_q=m_w_q, w_kv=m_w_kv,
                     w_o=m_w_o, norm_ffn_g=m_norm_ffn_g, w_up=m_w_up, ffn_dw_w=m_ffn_dw_w, ffn_dw_b=m_ffn_dw_b,
                     w_down=m_w_down, norm_final_g=m_norm_final_g)
    moments_v = dict(norm_mix_g=v_norm_mix_g, w_in=v_w_in, conv_dw_w=v_conv_dw_w, conv_dw_b=v_conv_dw_b,
                     conv_ln_g=v_conv_ln_g, conv_ln_b=v_conv_ln_b, pool_w=v_pool_w, pool_scale=v_pool_scale,
                     w_out=v_w_out, norm_xattn_g=v_norm_xattn_g, norm_mem_g=v_norm_mem_g, w_q=v_w_q, w_kv=v_w_kv,
                     w_o=v_w_o, norm_ffn_g=v_norm_ffn_g, w_up=v_w_up, ffn_dw_w=v_ffn_dw_w, ffn_dw_b=v_ffn_dw_b,
                     w_down=v_w_down, norm_final_g=v_norm_final_g)
    order = list(weights)
    transposed = ("w_in", "w_kv", "w_up")

    n_b, seq, _ = x.shape
    tokens = n_b * seq
    tm_mix = min(512, seq // 2)
    tm_attn = min(1024, seq // 2)
    tm_ffn = min(256, seq // 2)
    dev = 4 * lax.axis_index("x") + 2 * lax.axis_index("y") + lax.axis_index("c")

    packs = [jnp.concatenate([weights[n][0].T if n in transposed else weights[n][0] for n in names], axis=0).astype(BF16)
             for names in AG_GROUPS]
    small_sharded = _pack_rows([conv_dw_w[0], ffn_dw_w[0]])
    n_small = small_sharded.size * 2 // D_MODEL
    small_bits = lax.bitcast_convert_type(small_sharded, BF16).reshape(n_small, D_MODEL)
    n_mix = packs[0].shape[0]
    packs[0] = jnp.concatenate([packs[0], small_bits, jnp.zeros((BF16_TILE_ROWS - n_small, D_MODEL), BF16)], axis=0)
    flights = []
    after = small_sharded
    for k in range(len(AG_GROUPS)):
        own_in_place = lax.dynamic_update_slice(lax.empty((N_DEV,) + packs[k].shape, BF16), packs[k][None], (dev, 0, 0))
        flights.append(_gather_start(own_in_place, after, "weights_gather_start_%d" % k, BARRIER_IDS["gather_start"][k]))
        after = flights[-1][3]

    def gather_finish(flight, after, tag):
        fwd_send, fwd_recv, buf = _gather_forward(*flight[:3], after, "weights_gather_forward_" + tag,
                                                  BARRIER_IDS["gather_forward"][int(tag)])
        return _gather_finish(fwd_send, fwd_recv, buf, "weights_gather_finish_" + tag)

    gw_mix = gather_finish(flights[0], after, "0")
    gsmall = lax.bitcast_convert_type(
        gw_mix[:, n_mix:n_mix + n_small, :].reshape((N_DEV,) + small_sharded.shape + (2,)), F32)
    gflat = gsmall.reshape(N_DEV, -1)
    n_cw = CONV_WIDTH * (D_CONV // N_DEV)
    n_fw = FFN_CONV_WIDTH * (2 * D_FF // N_DEV)
    conv_w = gflat[:, :n_cw].reshape(N_DEV, CONV_WIDTH, D_CONV // N_DEV).transpose(1, 0, 2).reshape(CONV_WIDTH, D_CONV)
    ffn_w = gflat[:, n_cw:n_cw + n_fw].reshape(N_DEV, FFN_CONV_WIDTH, 2 * D_FF // N_DEV).transpose(1, 0, 2).reshape(
        FFN_CONV_WIDTH, 2 * D_FF)

    x2d = x.reshape(tokens, D_MODEL)
    mem2d = mem.reshape(n_b * N_MEM, D_MODEL)
    tgt2d = loss_target.reshape(tokens, D_MODEL)
    g_final = norm_final_g.reshape(1, D_MODEL)

    x1, u_all, c_all, pooled_all, ymix, h1 = _fwd_mix(
        x2d, gw_mix, norm_mix_g, conv_w, conv_dw_b, conv_ln_g, conv_ln_b, pool_w[0], pool_scale, flights[2][3],
        seq, tm_mix)
    gw_attn = gather_finish(flights[1], x1, "1")
    mem_n, kv = _fwd_kv(mem2d, gw_attn, norm_mem_g)
    x2, h2, q, o = _fwd_attn(x1, kv, gw_attn, norm_xattn_g, seq, tm_attn)
    gw_ffn = gather_finish(flights[2], x2, "2")
    uu_all, cc_all, a_all, h3, dx3, dx3b, loss_part, dg_final = _fwd_ffn(
        x2, tgt2d, gw_ffn, norm_ffn_g, ffn_w, ffn_dw_b, g_final, seq, tm_ffn)

    table = _owner_table()

    def sibling_start(names, tag):
        parts = [part[n].reshape(N_DEV, W_OFF[n][1], D_MODEL) for n in names]
        return _exchange_start(parts, 4, _to_sibling, "rs_sibling_exchange_start_" + tag, BARRIER_IDS["sibling"][tag])

    def chips_start(flight, after, tag):
        parts, landed = _exchange_wait(*flight[:4], after, 4, _to_sibling, "rs_sibling_exchange_wait_" + tag)
        sums = _chip_partial_sums(table, parts, landed, "rs_chip_partial_sums_" + tag)
        return parts, landed, _exchange_start(sums, 3, _to_chip, "rs_chip_exchange_start_" + tag,
                                              BARRIER_IDS["chips"][tag])

    grads, delta, new_m, new_v = {}, {}, {}, {}

    def reduce_finish(names, parts, landed, flight, after, tag):
        _, from_chips = _exchange_wait(*flight[:4], after, 3, _to_chip, "rs_chip_exchange_wait_" + tag)
        as_rows = {n: n in transposed and W_OFF[n][1] % LANES != 0 for n in names}
        states = [tuple(t[n][0].T if as_rows[n] else t[n][0] for t in (weights, moments_m, moments_v)) for n in names]
        results = _final_update(table, parts, landed, from_chips, states, "rs_final_update_" + tag)
        for n, res in zip(names, results):
            grads[n], delta[n], new_m[n], new_v[n] = [t.T[None] if as_rows[n] else t[None] for t in res]
        return delta[names[-1]]

    part = {}
    dx2, dx2b, duu, d_ffn_b, d_ffn_w, dg_ffn = _bwd_ffn(dx3, x2, uu_all, cc_all, gw_ffn, norm_ffn_g, ffn_w, seq, tm_ffn)
    part["w_up"] = _wgrad(duu, h3, "wgrad_w_up")
    part["w_down"] = _wgrad(a_all, dx3b, "wgrad_w_down")
    to_sibling_a = sibling_start(RS_GROUPS["a"], "a")
    dx1, dx1b, dq, dkv, dg_x = _bwd_attn(dx2, x1, q, kv, gw_attn, norm_xattn_g, to_sibling_a[4], seq, tm_mix)
    parts_a, landed_a, flight_a = chips_start(to_sibling_a, dx1, "a")
    dkv_b, dg_mem = _bwd_kv(dkv, mem2d, gw_attn)
    part["w_q"] = _wgrad(h2, dq, "wgrad_w_q", after=flight_a[4])
    part["w_kv"] = _wgrad(dkv_b, mem_n, "wgrad_w_kv", after=flight_a[4])
    part["w_out"] = _wgrad(ymix, dx1b, "wgrad_w_out", after=flight_a[4])
    to_sibling_b = sibling_start(RS_GROUPS["b"], "b")
    part["w_o"] = _wgrad(o, dx2b, "wgrad_w_o", after=to_sibling_b[4])
    parts_b, landed_b, flight_b = chips_start(to_sibling_b, part["w_o"], "b")
    dx, du, dg_mix, d_conv_w, d_conv_b, d_ln_g, d_ln_b, d_pool_w, d_pool_scale = _bwd_mix(
        dx1, x2d, u_all, c_all, pooled_all, gw_mix, norm_mix_g, conv_w, conv_ln_g, conv_ln_b, pool_w[0], pool_scale,
        flight_b[4], seq, tm_mix)
    grad_x = dx.reshape(x.shape)

    small_grads = dict(norm_mix_g=dg_mix, conv_dw_b=d_conv_b, conv_ln_g=d_ln_g, conv_ln_b=d_ln_b, pool_w=d_pool_w,
                       pool_scale=d_pool_scale, norm_xattn_g=dg_x, norm_mem_g=dg_mem, norm_ffn_g=dg_ffn,
                       ffn_dw_b=d_ffn_b, norm_final_g=dg_final)
    small_list = [small_grads[n] for n, _ in SMALL] + [d_conv_w, d_ffn_w, loss_part[:1]]
    small_mine = _pack_rows(small_list)
    small_flight = _broadcast_start(
        lax.dynamic_update_slice(lax.empty((N_DEV,) + small_mine.shape, F32), small_mine[None], (dev, 0, 0)),
        "small_grads_broadcast_start", BARRIER_IDS["broadcast"])

    part["w_in"] = _wgrad(du, h1, "wgrad_w_in", after=small_flight[3])
    to_sibling_c = sibling_start(RS_GROUPS["c"], "c")
    updated_b = reduce_finish(RS_GROUPS["b"], parts_b, landed_b, flight_b, to_sibling_c[4], "b")
    parts_c, landed_c, flight_c = chips_start(to_sibling_c, updated_b, "c")
    updated_a = reduce_finish(RS_GROUPS["a"], parts_a, landed_a, flight_a, flight_c[4], "a")
    small_all = _broadcast_wait(*small_flight[:3], updated_a, "small_grads_broadcast_wait")
    small_sum = _sum_blocks(small_all).reshape(-1)

    pos = 0
    for n, shape in SMALL:
        size = 1
        for s in shape:
            size *= s
        grads[n] = small_sum[pos:pos + size].reshape(shape)
        pos += size
    full_conv_w = small_sum[pos:pos + CONV_WIDTH * D_CONV].reshape(CONV_WIDTH, D_CONV)
    pos += CONV_WIDTH * D_CONV
    full_ffn_w = small_sum[pos:pos + FFN_CONV_WIDTH * 2 * D_FF].reshape(FFN_CONV_WIDTH, 2 * D_FF)
    loss = small_sum[pos + FFN_CONV_WIDTH * 2 * D_FF]
    grads["conv_dw_w"] = lax.dynamic_slice_in_dim(full_conv_w, dev * (D_CONV // N_DEV), D_CONV // N_DEV, axis=1)[None]
    grads["ffn_dw_w"] = lax.dynamic_slice_in_dim(full_ffn_w, dev * (2 * D_FF // N_DEV), 2 * D_FF // N_DEV, axis=1)[None]

    small_names = [n for n in order if n not in W_OFF]
    swap = lambda t: jnp.transpose(t, (1, 0, 2))
    two_d = lambda t: t.reshape(1, -1) if t.ndim == 1 else (swap(t) if t.ndim == 3 else t)
    outs = _adamw_small(*[[two_d(t[n]) for n in small_names] for t in (weights, grads, moments_m, moments_v)])
    for res, out in zip((delta, new_m, new_v), outs):
        for n, o in zip(small_names, out):
            res[n] = swap(o) if o.ndim == 3 else o.reshape(weights[n].shape)

    reduce_finish(RS_GROUPS["c"], parts_c, landed_c, flight_c, delta[small_names[-1]], "c")

    return (loss, grad_x, *[grads[n] for n in order], *[delta[n] for n in order],
            *[new_m[n] for n in order], *[new_v[n] for n in order])
```

```python
import jax
import jax.numpy as jnp
from jax import lax
from jax.experimental import pallas as pl
from jax.experimental.pallas import tpu as pltpu

F32 = jnp.float32
BF16 = jnp.bfloat16
MESH = pl.DeviceIdType.MESH

N_DEV = 8
D_MODEL = 1024
D_CONV = 512
D_POOL = 512
CONV_WIDTH = 31
POOL_WINDOWS = (2, 4, 8, 16)
POOL_GROUP_DIM = 128
D_IN = 1536
N_MEM = 256
HEADS = 4
HEAD_DIM = 256
D_FF = 2816
FFN_CONV_WIDTH = 3
EPS = 1e-6
ADAM_LR = 0.001
ADAM_B1 = 0.9
ADAM_B2 = 0.999
ADAM_EPS = 1e-08
ADAM_WD = 0.01
ADAM_STEP = 10

VMEM_LIMIT_V7X = 56 * 1024 * 1024
CONV_HALO = 32
POOL_HALO = 16
FFN_HALO = 8
FFN_CHUNK = 2816
WGRAD_RESIDENT_BYTES = 8 * 1024 * 1024
WGRAD_TOKEN_BLOCK = 1024
BF16_TILE_ROWS = 16

W_ROWS = (("w_in", 192), ("w_out", 128), ("w_q", 128), ("w_kv", 256), ("w_o", 128), ("w_up", 704), ("w_down", 352))
AG_GROUPS = (("w_in", "w_out"), ("w_q", "w_kv", "w_o"), ("w_up", "w_down"))
W_OFF = {}
for _names in AG_GROUPS:
    _o = 0
    for _n in _names:
        W_OFF[_n] = (_o, dict(W_ROWS)[_n])
        _o += dict(W_ROWS)[_n]
RS_GROUPS = {"a": ("w_up", "w_down"), "b": ("w_q", "w_kv", "w_out"), "c": ("w_o", "w_in")}
BARRIER_IDS = {"gather_start": (None, 0, 1), "gather_forward": (11, 2, 3), "sibling": {"a": 4, "b": 5, "c": 6},
               "chips": {"a": 7, "b": 8, "c": 9}, "broadcast": 10}


def _dot(a, b):
    return jnp.dot(a, b, preferred_element_type=F32)


def _dot_nt(a, b):
    return lax.dot_general(a, b, (((1,), (1,)), ((), ())), preferred_element_type=F32)


def _dot_tn(a, b):
    return lax.dot_general(a, b, (((0,), (0,)), ((), ())), preferred_element_type=F32)


def _sigmoid(v):
    return 1.0 / (1.0 + jnp.exp(-v))


def _rms_fwd(v):
    r = lax.rsqrt(jnp.mean(v * v, axis=-1, keepdims=True) + EPS)
    return v * r, r


def _rms_bwd(dh, vh, r, g):
    gd = dh * g
    return r * (gd - vh * jnp.mean(gd * vh, axis=-1, keepdims=True))


def _sublane_shifts(v):
    rows = v.shape[0]
    return [v] + [pltpu.roll(v, rows - b, 0) for b in range(1, 8)]


def _colsum(v):
    return jnp.sum(v, axis=0, keepdims=True)


def _colsum_mxu(v):
    return _dot(jnp.ones((8, v.shape[0]), BF16), v.astype(BF16))[0:1, :]


def _full(shape):
    return pl.BlockSpec(shape, lambda *_: (0,) * len(shape))


def _params(sem=("arbitrary",), vmem=VMEM_LIMIT_V7X):
    return pltpu.CompilerParams(dimension_semantics=sem, vmem_limit_bytes=vmem)


def _load_weight(g_hbm, name, dst, sem):
    off, rows = W_OFF[name]
    return [pltpu.make_async_copy(g_hbm.at[d, pl.ds(off, rows), :], dst.at[pl.ds(d * rows, rows), :], sem)
            for d in range(N_DEV)]


def _start_weights(g_hbm, names, dsts, sems):
    @pl.when(pl.program_id(0) == 0)
    def _():
        copies = [_load_weight(g_hbm, name, dst, sems.at[k]) for k, (name, dst) in enumerate(zip(names, dsts))]
        for cp in sum(copies, []):
            cp.start()
        for cp in sum(copies, []):
            cp.wait()


def _position():
    x, y, c = lax.axis_index("x"), lax.axis_index("y"), lax.axis_index("c")
    chips = [(1 - x, y), (x, 1 - y), (1 - x, 1 - y)]
    return x, y, c, chips


def _dev(px, py, pc):
    return 4 * px + 2 * py + pc


_HBM =pl.BlockSpec(memory_space=pltpu.HBM)
_SEM = pl.BlockSpec(memory_space=pltpu.SEMAPHORE)
_SIDE_EFFECT = pltpu.SideEffectType.DATAFLOW_SIDE_EFFECTING


def _handshake(peers):
    barrier = pltpu.get_barrier_semaphore()
    for peer in peers:
        pl.semaphore_signal(barrier, inc=1, device_id=peer, device_id_type=MESH)
    pl.semaphore_wait(barrier, len(peers))


def _gather_start(buf, after, name, collective_id):
    def body(buf_ref, after_ref, send_sems, recv_sems, buf_thru, token):
        del after_ref, buf_thru
        x, y, c, chips = _position()
        rows = buf_ref.at[_dev(x, y, c)]
        targets = [(x, y, 1 - c)] + [(*chip, c) for chip in chips]
        if collective_id is not None:
            _handshake(targets)
        for k, to in enumerate(targets):
            pltpu.make_async_remote_copy(src_ref=rows, dst_ref=rows, send_sem=send_sems.at[k], recv_sem=recv_sems.at[k],
                                         device_id=to, device_id_type=MESH).start()
        token[...] = jnp.zeros_like(token)

    return pl.pallas_call(
        body, name=name,
        out_shape=(pltpu.SemaphoreType.DMA((4,)), pltpu.SemaphoreType.DMA((4,)), pltpu.HBM(buf.shape, buf.dtype),
                   jax.ShapeDtypeStruct((8, 128), F32)),
        in_specs=(_HBM, pl.BlockSpec(memory_space=pl.ANY)),
        out_specs=(_SEM, _SEM, _HBM, pl.BlockSpec(memory_space=pltpu.VMEM)),
        input_output_aliases={0: 2},
        compiler_params=pltpu.CompilerParams(has_side_effects=_SIDE_EFFECT, collective_id=collective_id),
    )(pltpu.with_memory_space_constraint(buf, pltpu.HBM), after)


def _gather_forward(send_sems, recv_sems, buf, after, name, collective_id):
    def body(buf_ref, send_sems, recv_sems, after_ref, fwd_send, fwd_recv, buf_thru):
        del after_ref, buf_thru
        x, y, c, chips = _position()
        sibling = (x, y, 1 - c)

        def copy(block, k, sends, recvs):
            rows = buf_ref.at[_dev(*block)]
            return pltpu.make_async_remote_copy(src_ref=rows, dst_ref=rows, send_sem=sends.at[k], recv_sem=recvs.at[k],
                                                device_id=sibling, device_id_type=MESH)

        _handshake([sibling])
        for k in range(4):
            copy((x, y, c), k, send_sems, recv_sems).wait_send()
        copy(sibling, 0, send_sems, recv_sems).wait_recv()
        for j, chip in enumerate(chips):
            copy((*chip, c), 1 + j, send_sems, recv_sems).wait_recv()
            copy((*chip, c), j, fwd_send, fwd_recv).start()

    return pl.pallas_call(
        body, name=name,
        out_shape=(pltpu.SemaphoreType.DMA((3,)), pltpu.SemaphoreType.DMA((3,)), pltpu.HBM(buf.shape, buf.dtype)),
        in_specs=(_HBM, _SEM, _SEM, pl.BlockSpec(memory_space=pl.ANY)), out_specs=(_SEM, _SEM, _HBM),
        input_output_aliases={0: 2},
        compiler_params=pltpu.CompilerParams(has_side_effects=_SIDE_EFFECT, collective_id=collective_id),
    )(buf, send_sems, recv_sems, after)


def _gather_finish(fwd_send, fwd_recv, buf, name):
    def body(buf_ref, fwd_send, fwd_recv, buf_thru):
        del buf_thru
        x, y, c, chips = _position()
        for j, chip in enumerate(chips):
            cp = pltpu.make_async_remote_copy(
                src_ref=buf_ref.at[_dev(*chip, c)], dst_ref=buf_ref.at[_dev(*chip, 1 - c)], send_sem=fwd_send.at[j],
                recv_sem=fwd_recv.at[j], device_id=(x, y, 1 - c), device_id_type=MESH)
            cp.wait_send()
            cp.wait_recv()

    return pl.pallas_call(
        body, name=name,
        out_shape=pltpu.HBM(buf.shape, buf.dtype),
        in_specs=(_HBM, _SEM, _SEM), out_specs=_HBM,
        input_output_aliases={0: 0},
        compiler_params=pltpu.CompilerParams(has_side_effects=_SIDE_EFFECT),
    )(buf, fwd_send, fwd_recv)


def _everyone_else(x, y, c, chips):
    return [(x, y, 1 - c)] + [(*chip, core) for chip in chips for core in (c, 1 - c)]


def _broadcast_start(buf, name, collective_id):
    def body(buf_ref, send_sems, recv_sems, buf_thru, token):
        del buf_thru
        x, y, c, chips = _position()
        rows = buf_ref.at[_dev(x, y, c)]
        _handshake(_everyone_else(x, y, c, chips))
        for k, to in enumerate(_everyone_else(x, y, c, chips)):
            pltpu.make_async_remote_copy(src_ref=rows, dst_ref=rows, send_sem=send_sems.at[k], recv_sem=recv_sems.at[k],
                                         device_id=to, device_id_type=MESH).start()
        token[...] = jnp.zeros_like(token)

    return pl.pallas_call(
        body, name=name,
        out_shape=(pltpu.SemaphoreType.DMA((7,)), pltpu.SemaphoreType.DMA((7,)), pltpu.HBM(buf.shape, buf.dtype),
                   jax.ShapeDtypeStruct((8, 128), F32)),
        in_specs=(_HBM,), out_specs=(_SEM, _SEM, _HBM, pl.BlockSpec(memory_space=pltpu.VMEM)),
        input_output_aliases={0: 2},
        compiler_params=pltpu.CompilerParams(has_side_effects=_SIDE_EFFECT, collective_id=collective_id),
    )(pltpu.with_memory_space_constraint(buf, pltpu.HBM))


def _broadcast_wait(send_sems, recv_sems, buf, after, name):
    def body(buf_ref, send_sems, recv_sems, after_ref, buf_thru):
        del after_ref, buf_thru
        x, y, c, chips = _position()
        for k, peer in enumerate(_everyone_else(x, y, c, chips)):
            cp = pltpu.make_async_remote_copy(
                src_ref=buf_ref.at[_dev(x, y, c)], dst_ref=buf_ref.at[_dev(*peer)], send_sem=send_sems.at[k],
                recv_sem=recv_sems.at[k], device_id=peer, device_id_type=MESH)
            cp.wait_send()
            cp.wait_recv()

    return pl.pallas_call(
        body, name=name,
        out_shape=pltpu.HBM(buf.shape, buf.dtype),
        in_specs=(_HBM, _SEM, _SEM, pl.BlockSpec(memory_space=pl.ANY)), out_specs=_HBM,
        input_output_aliases={0: 0},
        compiler_params=pltpu.CompilerParams(has_side_effects=_SIDE_EFFECT),
    )(buf, send_sems, recv_sems, after)


def _to_sibling(j, x, y, c, chips):
    return _dev(*([(x, y)] + chips)[j], 1 - c), (x, y, 1 - c)


def _to_chip(j, x, y, c, chips):
    return j, (*chips[j], c)


def _exchange_start(srcs, n_slots, route, name, collective_id):
    n = len(srcs)

    def body(*refs):
        s_refs, land_refs = refs[:n], refs[n:2 * n]
        send_sems, recv_sems = refs[2 * n:2 * n + 2]
        token = refs[-1]
        x, y, c, chips = _position()
        _handshake([(x, y, 1 - c)] if route is _to_sibling else [route(j, x, y, c, chips)[1] for j in range(n_slots)])
        for k in range(n):
            for j in range(n_slots):
                block, to = route(j, x, y, c, chips)
                pltpu.make_async_remote_copy(
                    src_ref=s_refs[k].at[block], dst_ref=land_refs[k].at[j], send_sem=send_sems.at[n_slots * k + j],
                    recv_sem=recv_sems.at[n_slots * k + j], device_id=to, device_id_type=MESH).start()
        token[...] = jnp.zeros_like(token)

    lands = [jax.ShapeDtypeStruct((n_slots,) + s.shape[1:], s.dtype) for s in srcs]
    outs = pl.pallas_call(
        body, name=name,
        out_shape=(pltpu.SemaphoreType.DMA((n_slots * n,)), pltpu.SemaphoreType.DMA((n_slots * n,)),
                   *[pltpu.HBM(s.shape, s.dtype) for s in srcs], *[pltpu.HBM(l.shape, l.dtype) for l in lands],
                   jax.ShapeDtypeStruct((8, 128), F32)),
        in_specs=[_HBM] * (2 * n), out_specs=(_SEM, _SEM, *[_HBM] * (2 * n), pl.BlockSpec(memory_space=pltpu.VMEM)),
        input_output_aliases={k: 2 + k for k in range(2 * n)},
        compiler_params=pltpu.CompilerParams(has_side_effects=_SIDE_EFFECT, collective_id=collective_id),
    )(*[pltpu.with_memory_space_constraint(s, pltpu.HBM) for s in srcs],
      *[pltpu.with_memory_space_constraint(lax.empty(l.shape, l.dtype), pltpu.HBM) for l in lands])
    return outs[0], outs[1], outs[2:2 + n], outs[2 + n:2 + 2 * n], outs[-1]


def _exchange_wait(send_sems, recv_sems, s_thru, land_thru, after, n_slots, route, name):
    n = len(s_thru)

    def body(*refs):
        s_refs, land_refs = refs[:n], refs[n:2 * n]
        send_sems, recv_sems = refs[2 * n:2 * n + 2]
        x, y, c, chips = _position()
        for k in range(n):
            for j in range(n_slots):
                block, to = route(j, x, y, c, chips)
                cp = pltpu.make_async_remote_copy(
                    src_ref=s_refs[k].at[block], dst_ref=land_refs[k].at[j], send_sem=send_sems.at[n_slots * k + j],
                    recv_sem=recv_sems.at[n_slots * k + j], device_id=to, device_id_type=MESH)
                cp.wait_send()
                cp.wait_recv()

    outs = pl.pallas_call(
        body, name=name,
        out_shape=(*[pltpu.HBM(s.shape, s.dtype) for s in s_thru], *[pltpu.HBM(l.shape, l.dtype) for l in land_thru]),
        in_specs=[_HBM] * (2 * n) + [_SEM, _SEM, pl.BlockSpec(memory_space=pl.ANY)], out_specs=[_HBM] * (2 * n),
        input_output_aliases={k: k for k in range(2 * n)},
        compiler_params=pltpu.CompilerParams(has_side_effects=_SIDE_EFFECT),
    )(*s_thru, *land_thru, send_sems, recv_sems, after)
    return outs[:n], outs[n:]


def _owner_table():
    x, y, c = lax.axis_index("x"), lax.axis_index("y"), lax.axis_index("c")
    chips = [(x, y), (1 - x, y), (x, 1 - y), (1 - x, 1 - y)]
    return jnp.stack([_dev(px, py, c) for px, py in chips]).astype(jnp.int32)


def _chip_partial_sums(table, parts, from_sibling, name):
    n = len(parts)

    def body(tab_ref, *refs):
        del tab_ref
        for g_ref, l_ref, out_ref in zip(refs[:n], refs[n:2 * n], refs[2 * n:]):
            out_ref[...] = (g_ref[...].astype(F32) + l_ref[...].astype(F32)).astype(out_ref.dtype)

    block = lambda p: (None,) + p.shape[1:]
    grid_spec = pltpu.PrefetchScalarGridSpec(
        num_scalar_prefetch=1, grid=(3,),
        in_specs=[pl.BlockSpec(block(p), lambda j, tab: (tab[j + 1], 0, 0)) for p in parts]
        + [pl.BlockSpec(block(p), lambda j, tab: (j + 1, 0, 0)) for p in parts],
        out_specs=[pl.BlockSpec(block(p), lambda j, tab: (j, 0, 0)) for p in parts])
    return pl.pallas_call(
        body, name=name, grid_spec=grid_spec,
        out_shape=[jax.ShapeDtypeStruct((3,) + p.shape[1:], BF16) for p in parts],
        compiler_params=_params(("arbitrary",)),
    )(table, *parts, *from_sibling)


def _final_update(table, parts, from_sibling, from_chips, states, name):
    n = len(parts)
    flipped = [states[k][0].shape != parts[k].shape[1:] for k in range(n)]

    def body(tab_ref, *refs):
        del tab_ref
        ins, outs = refs[:6 * n], refs[6 * n:]
        for k in range(n):
            acc = ins[k][...].astype(F32) + ins[n + k][...].astype(F32)
            for j in range(3):
                acc = acc + ins[2 * n + k][j].astype(F32)
            if flipped[k]:
                acc = acc.T
            w_ref, m_ref, v_ref = ins[3 * n + 3 * k:3 * n + 3 * k + 3]
            outs[4 * k][...] = acc
            for out_ref, val in zip(outs[4 * k + 1:4 * k + 4], _adamw_update(w_ref[...], acc, m_ref[...], v_ref[...])):
                out_ref[...] = val

    def grad_block(k, lead, at):
        r, c = parts[k].shape[1:]
        if flipped[k]:
            return pl.BlockSpec(lead + (r, c // 2), lambda t, tab: (*at(tab), 0, t))
        return pl.BlockSpec(lead + (r // 2, c), lambda t, tab: (*at(tab), t, 0))

    def state_block(k):
        a, b = states[k][0].shape
        return pl.BlockSpec((a // 2, b), lambda t, tab: (t, 0))

    grid_spec = pltpu.PrefetchScalarGridSpec(
        num_scalar_prefetch=1, grid=(2,),
        in_specs=[grad_block(k, (None,), lambda tab: (tab[0],)) for k in range(n)]
        + [grad_block(k, (None,), lambda tab: (0,)) for k in range(n)]
        + [grad_block(k, (3,), lambda tab: (0,)) for k in range(n)]
        + [state_block(k) for k in range(n) for _ in range(3)],
        out_specs=[state_block(k) for k in range(n) for _ in range(4)])
    outs = pl.pallas_call(
        body, name=name, grid_spec=grid_spec,
        out_shape=[jax.ShapeDtypeStruct(states[k][0].shape, F32) for k in range(n) for _ in range(4)],
        compiler_params=_params(("arbitrary",)),
    )(table, *parts, *from_sibling, *from_chips, *[t for k in range(n) for t in states[k]])
    return [outs[4 * k:4 * k + 4] for k in range(n)]


def _sum_blocks(g8):
    _, rows, cols = g8.shape

    def body(g_ref, out_ref):
        acc = g_ref[0]
        for d in range(1, N_DEV):
            acc = acc + g_ref[d]
        out_ref[...] = acc

    return pl.pallas_call(
        body, name="small_grad_sum", grid=(1,),
        in_specs=[_full((N_DEV, rows, cols))], out_specs=_full((rows, cols)),
        out_shape=jax.ShapeDtypeStruct((rows, cols), F32),
        compiler_params=_params(("arbitrary",)),
    )(g8)


def _fwd_mix(x2d, gw, g_mix, conv_w, conv_b, ln_g, ln_b, pool_w, pool_scale, after, seq, tm):
    tokens = x2d.shape[0]
    n_tiles = tokens // tm
    tps = seq // tm

    def body(x_ref, gmix_ref, gw_hbm, cw_ref, cb_ref, lng_ref, lnb_ref, pw_ref, ps_ref, after_ref,
             x1_ref, u_ref, c_ref, pooled_ref, ymix_ref, h1_ref,
             win_v, wout_v, hc_carry, up_carry, sem):
        del after_ref
        i = pl.program_id(0)

        _start_weights(gw_hbm, ("w_in", "w_out"), (win_v, wout_v), sem)

        @pl.when(i % tps == 0)
        def _():
            hc_carry[...] = jnp.zeros_like(hc_carry)
            up_carry[...] = jnp.zeros_like(up_carry)

        x = x_ref[...]
        xh, _ = _rms_fwd(x)
        h1 = (xh * gmix_ref[...]).astype(BF16)
        h1_ref[...] = h1
        u = _dot_nt(h1, win_v[...])
        u_ref[...] = u
        val, gate, up = u[:, :D_CONV], u[:, D_CONV:2 * D_CONV], u[:, 2 * D_CONV:]

        extp = jnp.concatenate([up_carry[...], up], axis=0)
        up_carry[...] = up[tm - POOL_HALO:, :]
        pos = lax.broadcasted_iota(jnp.int32, (tm, 1), 0) + (i % tps) * tm
        run = extp
        mixed = []
        for g, w in enumerate(POOL_WINDOWS):
            lo = g * POOL_GROUP_DIM
            run = run[:, POOL_GROUP_DIM if g else 0:]
            run = run + pltpu.roll(run, w // 2, 0)
            cnt = jnp.minimum(pos + 1, w).astype(F32)
            pooled = run[POOL_HALO:, :POOL_GROUP_DIM] / cnt - up[:, lo:lo + POOL_GROUP_DIM]
            pooled = pooled.astype(BF16)
            pooled_ref[:, lo:lo + POOL_GROUP_DIM] = pooled
            mixed.append(_dot(pooled, pw_ref[g].astype(BF16)))
        y_pool = jnp.concatenate(mixed, axis=-1) * ps_ref[...]
        y_pool = y_pool.astype(BF16)
        ymix_ref[:, D_CONV:] = y_pool
        out = _dot(y_pool, wout_v[D_CONV:, :])

        hc = val * _sigmoid(gate)
        ext = jnp.concatenate([hc_carry[...], hc], axis=0)
        hc_carry[...] = hc[tm - CONV_HALO:, :]
        conv = jnp.broadcast_to(cb_ref[...], (tm, D_CONV))
        ahead_by = _sublane_shifts(ext)
        for k in range(CONV_WIDTH):
            whole, part = divmod(CONV_HALO - (CONV_WIDTH - 1) + k, 8)
            conv = conv + cw_ref[k:k + 1, :] * ahead_by[part][8 * whole:8 * whole + tm, :]
        c_ref[...] = conv
        mu = jnp.mean(conv, axis=-1, keepdims=True)
        cen = conv - mu
        ln = cen * lax.rsqrt(jnp.mean(cen * cen, axis=-1, keepdims=True) + EPS) * lng_ref[...] + lnb_ref[...]
        y_conv = ln * _sigmoid(ln)
        y_conv = y_conv.astype(BF16)
        ymix_ref[:, :D_CONV] = y_conv
        x1_ref[...] = x + (out + _dot(y_conv, wout_v[:D_CONV, :]))

    row = lambda w: pl.BlockSpec((tm, w), lambda i: (i, 0))
    return pl.pallas_call(
        body, name="fwd_mix", grid=(n_tiles,),
        in_specs=[row(D_MODEL), _full((1, D_MODEL)), pl.BlockSpec(memory_space=pl.ANY),
                  _full((CONV_WIDTH, D_CONV)), _full((1, D_CONV)), _full((1, D_CONV)), _full((1, D_CONV)),
                  _full((4, POOL_GROUP_DIM, POOL_GROUP_DIM)), _full((1, D_POOL)), _full(after.shape)],
        out_specs=[row(D_MODEL), row(D_IN), row(D_CONV), row(D_POOL), row(D_MODEL), row(D_MODEL)],
        out_shape=[jax.ShapeDtypeStruct((tokens, D_MODEL), F32), jax.ShapeDtypeStruct((tokens, D_IN), F32),
                   jax.ShapeDtypeStruct((tokens, D_CONV), F32), jax.ShapeDtypeStruct((tokens, D_POOL), BF16),
                   jax.ShapeDtypeStruct((tokens, D_MODEL), BF16), jax.ShapeDtypeStruct((tokens, D_MODEL), BF16)],
        scratch_shapes=[pltpu.VMEM((D_IN, D_MODEL), BF16), pltpu.VMEM((D_MODEL, D_MODEL), BF16),
                        pltpu.VMEM((CONV_HALO, D_CONV), F32), pltpu.VMEM((POOL_HALO, D_POOL), F32),
                        pltpu.SemaphoreType.DMA((2,))],
        compiler_params=_params(),
    )(x2d, g_mix, gw, conv_w, conv_b, ln_g, ln_b, pool_w, pool_scale, after)


def _fwd_kv(mem2d, gw, g_mem):
    rows = mem2d.shape[0]
    n_b = rows // N_MEM

    def body(mem_ref, g_ref, gw_hbm, mn_ref, kv_ref, wkv_v, sem):
        @pl.when(pl.program_id(0) == 0)
        def _():
            copies = _load_weight(gw_hbm, "w_kv", wkv_v, sem)
            for cp in copies:
                cp.start()
            for cp in copies:
                cp.wait()

        mh, _ = _rms_fwd(mem_ref[...])
        mn = (mh * g_ref[...]).astype(BF16)
        mn_ref[...] = mn
        kv_ref[...] = _dot_nt(mn, wkv_v[...]).astype(BF16)

    return pl.pallas_call(
        body, name="fwd_kv", grid=(n_b,),
        in_specs=[pl.BlockSpec((N_MEM, D_MODEL), lambda b: (b, 0)), _full((1, D_MODEL)), pl.BlockSpec(memory_space=pl.ANY)],
        out_specs=[pl.BlockSpec((N_MEM, D_MODEL), lambda b: (b, 0)), pl.BlockSpec((N_MEM, 2 * D_MODEL), lambda b: (b, 0))],
        out_shape=[jax.ShapeDtypeStruct((rows, D_MODEL), BF16), jax.ShapeDtypeStruct((rows, 2 * D_MODEL), BF16)],
        scratch_shapes=[pltpu.VMEM((2 * D_MODEL, D_MODEL), BF16), pltpu.SemaphoreType.DMA],
        compiler_params=_params(),
    )(mem2d, g_mem, gw)


def _softmax_rows(s):
    e = jnp.exp(s - jnp.max(s, axis=-1, keepdims=True))
    return e / jnp.sum(e, axis=-1, keepdims=True)


def _fwd_attn(x1, kv, gw, g_x, seq, tm):
    tokens = x1.shape[0]
    n_tiles = tokens // tm
    tps = seq // tm

    def body(x1_ref, kv_ref, g_ref, gw_hbm, x2_ref, h2_ref, q_ref, o_ref, wq_v, wo_v, sem):
        _start_weights(gw_hbm, ("w_q", "w_o"), (wq_v, wo_v), sem)
        x1v = x1_ref[...]
        xh, _ = _rms_fwd(x1v)
        h2 = (xh * g_ref[...]).astype(BF16)
        h2_ref[...] = h2
        q = (_dot(h2, wq_v[...]) * (HEAD_DIM ** -0.5)).astype(BF16)
        q_ref[...] = q
        heads = [slice(h * HEAD_DIM, (h + 1) * HEAD_DIM) for h in range(HEADS)]
        scores = [_dot_nt(q[:, hd], kv_ref[:, hd]) for hd in heads]
        probs = [_softmax_rows(s).astype(BF16) for s in scores]
        outs = [_dot(p, kv_ref[:, pl.ds(D_MODEL + h * HEAD_DIM, HEAD_DIM)]) for h, p in enumerate(probs)]
        o = jnp.concatenate(outs, axis=-1).astype(BF16)
        o_ref[...] = o
        x2_ref[...] = x1v + _dot(o, wo_v[...])

    row = lambda w: pl.BlockSpec((tm, w), lambda i: (i, 0))
    return pl.pallas_call(
        body, name="fwd_attn", grid=(n_tiles,),
        in_specs=[row(D_MODEL), pl.BlockSpec((N_MEM, 2 * D_MODEL), lambda i: (i // tps, 0)), _full((1, D_MODEL)),
                  pl.BlockSpec(memory_space=pl.ANY)],
        out_specs=[row(D_MODEL)] * 4,
        out_shape=[jax.ShapeDtypeStruct((tokens, D_MODEL), F32)] + [jax.ShapeDtypeStruct((tokens, D_MODEL), BF16)] * 3,
        scratch_shapes=[pltpu.VMEM((D_MODEL, D_MODEL), BF16), pltpu.VMEM((D_MODEL, D_MODEL), BF16), pltpu.SemaphoreType.DMA((2,))],
        compiler_params=_params(),
    )(x1, kv, g_x, gw)


def _ffn_conv(uu, halo, w_ref, b_ref, cols):
    ext = jnp.concatenate([halo, uu], axis=0)
    p1 = pltpu.roll(ext, 1, 0)[FFN_HALO:, :]
    p2 = pltpu.roll(ext, 2, 0)[FFN_HALO:, :]
    return b_ref[:, cols] + w_ref[2:3, cols] * uu + w_ref[1:2, cols] * p1 + w_ref[0:1, cols] * p2


def _fwd_ffn(x2, target, gw, g_ffn, ffn_w, ffn_b, g_final, seq, tm):
    tokens = x2.shape[0]
    n_tiles = tokens // tm
    tps = seq // tm
    n_chunks = D_FF // FFN_CHUNK

    def body(x2_ref, tgt_ref, gffn_ref, gw_hbm, fw_ref, fb_ref, gfin_ref,
             uu_ref, cc_ref, a_ref, h3_ref, dx3_ref, dx3b_ref, loss_ref, dgfin_ref,
             wup_v, wdown_v, carry, sem):
        i = pl.program_id(0)

        _start_weights(gw_hbm, ("w_up", "w_down"), (wup_v, wdown_v), sem)

        @pl.when(i == 0)
        def _():
            loss_ref[...] = jnp.zeros_like(loss_ref)
            dgfin_ref[...] = jnp.zeros_like(dgfin_ref)

        @pl.when(i % tps == 0)
        def _():
            carry[...] = jnp.zeros_like(carry)

        x2v = x2_ref[...]
        xh, _ = _rms_fwd(x2v)
        h3 = (xh * gffn_ref[...]).astype(BF16)
        h3_ref[...] = h3
        acc = jnp.zeros((tm, D_MODEL), F32)
        for jc in range(n_chunks):
            halves = []
            for half in range(2):
                cols = pl.ds(half * D_FF + jc * FFN_CHUNK, FFN_CHUNK)
                uu = _dot_nt(h3, wup_v[cols, :])
                uu_ref[:, cols] = uu.astype(BF16)
                cc = _ffn_conv(uu, carry[:, cols], fw_ref, fb_ref, cols)
                cc_ref[:, cols] = cc.astype(BF16)
                halves.append(cc)
                carry[:, cols] = uu[tm - FFN_HALO:, :]
            gate, val = halves
            a = (gate * _sigmoid(gate) * val).astype(BF16)
            a_ref[:, pl.ds(jc * FFN_CHUNK, FFN_CHUNK)] = a
            acc = acc + _dot(a, wdown_v[pl.ds(jc * FFN_CHUNK, FFN_CHUNK), :])
        x3 = x2v + acc

        xh3, r3 = _rms_fwd(x3)
        gfin = gfin_ref[...]
        err = xh3 * gfin - tgt_ref[...]
        loss_ref[...] += jnp.full(loss_ref.shape, jnp.sum(err * err) * (0.5 / D_MODEL), F32)
        dy = err * (1.0 / D_MODEL)
        dgfin_ref[...] += _colsum(dy * xh3)
        dx3 = _rms_bwd(dy, xh3, r3, gfin)
        dx3_ref[...] = dx3
        dx3b_ref[...] = dx3.astype(BF16)

    row = lambda w: pl.BlockSpec((tm, w), lambda i: (i, 0))
    return pl.pallas_call(
        body, name="fwd_ffn", grid=(n_tiles,),
        in_specs=[row(D_MODEL), row(D_MODEL), _full((1, D_MODEL)), pl.BlockSpec(memory_space=pl.ANY),
                  _full((FFN_CONV_WIDTH, 2 * D_FF)), _full((1, 2 * D_FF)), _full((1, D_MODEL))],
        out_specs=[row(2 * D_FF), row(2 * D_FF), row(D_FF), row(D_MODEL), row(D_MODEL), row(D_MODEL), _full((8, 128)),
                   _full((1, D_MODEL))],
        out_shape=[jax.ShapeDtypeStruct((tokens, 2 * D_FF), BF16), jax.ShapeDtypeStruct((tokens, 2 * D_FF), BF16),
                   jax.ShapeDtypeStruct((tokens, D_FF), BF16),
                   jax.ShapeDtypeStruct((tokens, D_MODEL), BF16), jax.ShapeDtypeStruct((tokens, D_MODEL), F32),
                   jax.ShapeDtypeStruct((tokens, D_MODEL), BF16),
                   jax.ShapeDtypeStruct((8, 128), F32), jax.ShapeDtypeStruct((1, D_MODEL), F32)],
        scratch_shapes=[pltpu.VMEM((2 * D_FF, D_MODEL), BF16), pltpu.VMEM((D_FF, D_MODEL), BF16),
                        pltpu.VMEM((FFN_HALO, 2 * D_FF), F32), pltpu.SemaphoreType.DMA((2,))],
        compiler_params=_params(),
    )(x2, target, g_ffn, gw, ffn_w, ffn_b, g_final)


def _bwd_ffn(dx3, x2, uu_all, cc_all, gw, g_ffn, ffn_w, seq, tm):
    tokens = x2.shape[0]
    n_tiles = tokens // tm
    tps = seq // tm
    n_chunks = D_FF // FFN_CHUNK

    def body(dx3_ref, x2_ref, uu_ref, cc_ref, gffn_ref, gw_hbm, fw_ref,
             dx2_ref, dx2b_ref, duu_ref, dfb_ref, dfw_ref, dg_ref,
             wup_v, wdown_v, carry, sem):
        i = pl.program_id(0)
        t = n_tiles - 1 - i

        _start_weights(gw_hbm, ("w_down", "w_up"), (wdown_v, wup_v), sem)

        @pl.when(i == 0)
        def _():
            dfb_ref[...] = jnp.zeros_like(dfb_ref)
            dfw_ref[...] = jnp.zeros_like(dfw_ref)
            dg_ref[...] = jnp.zeros_like(dg_ref)

        @pl.when(t % tps == tps - 1)
        def _():
            carry[...] = jnp.zeros_like(carry)

        dx3v = dx3_ref[...]
        dx3b = dx3v.astype(BF16)
        dh3 = jnp.zeros((tm, D_MODEL), F32)
        for jc in range(n_chunks):
            da = _dot_nt(dx3b, wdown_v[pl.ds(jc * FFN_CHUNK, FFN_CHUNK), :])
            colss = [pl.ds(half * D_FF + jc * FFN_CHUNK, FFN_CHUNK) for half in range(2)]
            gate, val = [cc_ref[:, cols].astype(F32) for cols in colss]
            sg = _sigmoid(gate)
            dgate = da * val * (sg * (1.0 + gate * (1.0 - sg)))
            dval = da * (gate * sg)
            for dcc, cols in zip((dgate, dval), colss):
                uu = uu_ref[:, cols].astype(F32)
                dfb_ref[:, cols] += _colsum(dcc)
                ext = jnp.concatenate([dcc, carry[:, cols]], axis=0)
                carry[:, cols] = dcc[:FFN_HALO, :]
                n1 = pltpu.roll(ext, tm + FFN_HALO - 1, 0)[:tm, :]
                n2 = pltpu.roll(ext, tm + FFN_HALO - 2, 0)[:tm, :]
                duu = fw_ref[2:3, cols] * dcc + fw_ref[1:2, cols] * n1 + fw_ref[0:1, cols] * n2
                dfw_ref[2:3, cols] += _colsum(uu * dcc)
                dfw_ref[1:2, cols] += _colsum(uu * n1)
                dfw_ref[0:1, cols] += _colsum(uu * n2)
                duub = duu.astype(BF16)
                duu_ref[:, cols] = duub
                dh3 = dh3 + _dot(duub, wup_v[cols, :])
        xh, r = _rms_fwd(x2_ref[...])
        dg_ref[...] += _colsum(dh3 * xh)
        dx2 = dx3v + _rms_bwd(dh3, xh, r, gffn_ref[...])
        dx2_ref[...] = dx2
        dx2b_ref[...] = dx2.astype(BF16)

    rev = lambda w: pl.BlockSpec((tm, w), lambda i: (n_tiles - 1 - i, 0))
    return pl.pallas_call(
        body, name="bwd_ffn", grid=(n_tiles,),
        in_specs=[rev(D_MODEL), rev(D_MODEL), rev(2 * D_FF), rev(2 * D_FF), _full((1, D_MODEL)),
                  pl.BlockSpec(memory_space=pl.ANY), _full((FFN_CONV_WIDTH, 2 * D_FF))],
        out_specs=[rev(D_MODEL), rev(D_MODEL), rev(2 * D_FF), _full((1, 2 * D_FF)), _full((FFN_CONV_WIDTH, 2 * D_FF)),
                   _full((1, D_MODEL))],
        out_shape=[jax.ShapeDtypeStruct((tokens, D_MODEL), F32), jax.ShapeDtypeStruct((tokens, D_MODEL), BF16),
                   jax.ShapeDtypeStruct((tokens, 2 * D_FF), BF16),
                   jax.ShapeDtypeStruct((1, 2 * D_FF), F32), jax.ShapeDtypeStruct((FFN_CONV_WIDTH, 2 * D_FF), F32),
                   jax.ShapeDtypeStruct((1, D_MODEL), F32)],
        scratch_shapes=[pltpu.VMEM((2 * D_FF, D_MODEL), BF16), pltpu.VMEM((D_FF, D_MODEL), BF16),
                        pltpu.VMEM((FFN_HALO, 2 * D_FF), F32), pltpu.SemaphoreType.DMA((2,))],
        compiler_params=_params(),
    )(dx3, x2, uu_all, cc_all, g_ffn, gw, ffn_w)


def _bwd_attn(dx2, x1, q, kv, gw, g_x, after, seq, tm):
    tokens = x1.shape[0]
    n_tiles = tokens // tm
    tps = seq // tm
    n_b = tokens // seq

    def body(dx2_ref, x1_ref, q_ref, kv_ref, g_ref, gw_hbm, after_ref, dx1_ref, dx1b_ref, dq_ref, dkv_ref, dg_ref,
             wq_v, wo_v, sem):
        del after_ref
        i = pl.program_id(0)

        _start_weights(gw_hbm, ("w_o", "w_q"), (wo_v, wq_v), sem)

        @pl.when(i == 0)
        def _():
            dg_ref[...] = jnp.zeros_like(dg_ref)

        @pl.when(i % tps == 0)
        def _():
            dkv_ref[...] = jnp.zeros_like(dkv_ref)

        dx2v = dx2_ref[...]
        do = _dot_nt(dx2v.astype(BF16), wo_v[...]).astype(BF16)
        q = q_ref[...]
        heads = [slice(h * HEAD_DIM, (h + 1) * HEAD_DIM) for h in range(HEADS)]
        kcols = [pl.ds(h * HEAD_DIM, HEAD_DIM) for h in range(HEADS)]
        vcols = [pl.ds(D_MODEL + h * HEAD_DIM, HEAD_DIM) for h in range(HEADS)]
        scores = [_dot_nt(q[:, hd], kv_ref[:, kc]) for hd, kc in zip(heads, kcols)]
        dps = [_dot_nt(do[:, hd], kv_ref[:, vc]) for hd, vc in zip(heads, vcols)]
        probs = [_softmax_rows(s) for s in scores]
        dss = [(p * (dp - jnp.sum(dp * p, axis=-1, keepdims=True))).astype(BF16) for p, dp in zip(probs, dps)]
        for p, hd, vc in zip(probs, heads, vcols):
            dkv_ref[:, vc] += _dot_tn(p.astype(BF16), do[:, hd])
        dqs = [_dot(ds, kv_ref[:, kc]) * (HEAD_DIM ** -0.5) for ds, kc in zip(dss, kcols)]
        for ds, hd, kc in zip(dss, heads, kcols):
            dkv_ref[:, kc] += _dot_tn(ds, q[:, hd])
        dq = jnp.concatenate(dqs, axis=-1).astype(BF16)
        dq_ref[...] = dq
        dh2 = _dot_nt(dq, wq_v[...])
        xh, r = _rms_fwd(x1_ref[...])
        dg_ref[...] += _colsum(dh2 * xh)
        dx1 = dx2v + _rms_bwd(dh2, xh, r, g_ref[...])
        dx1_ref[...] = dx1
        dx1b_ref[...] = dx1.astype(BF16)

    row = lambda w: pl.BlockSpec((tm, w), lambda i: (i, 0))
    per_b = pl.BlockSpec((N_MEM, 2 * D_MODEL), lambda i: (i // tps, 0))
    return pl.pallas_call(
        body, name="bwd_attn", grid=(n_tiles,),
        in_specs=[row(D_MODEL), row(D_MODEL), row(D_MODEL), per_b, _full((1, D_MODEL)), pl.BlockSpec(memory_space=pl.ANY),
                  _full(after.shape)],
        out_specs=[row(D_MODEL), row(D_MODEL), row(D_MODEL), per_b, _full((1, D_MODEL))],
        out_shape=[jax.ShapeDtypeStruct((tokens, D_MODEL), F32), jax.ShapeDtypeStruct((tokens, D_MODEL), BF16),
                   jax.ShapeDtypeStruct((tokens, D_MODEL), BF16),
                   jax.ShapeDtypeStruct((n_b * N_MEM, 2 * D_MODEL), F32), jax.ShapeDtypeStruct((1, D_MODEL), F32)],
        scratch_shapes=[pltpu.VMEM((D_MODEL, D_MODEL), BF16), pltpu.VMEM((D_MODEL, D_MODEL), BF16), pltpu.SemaphoreType.DMA((2,))],
        compiler_params=_params(),
    )(dx2, x1, q, kv, g_x, gw, after)


def _bwd_kv(dkv, mem2d, gw):
    rows = mem2d.shape[0]
    n_b = rows // N_MEM

    def body(dkv_ref, mem_ref, gw_hbm, dkvb_ref, dg_ref, wkv_v, sem):
        @pl.when(pl.program_id(0) == 0)
        def _():
            copies = _load_weight(gw_hbm, "w_kv", wkv_v, sem)
            for cp in copies:
                cp.start()
            for cp in copies:
                cp.wait()
            dg_ref[...] = jnp.zeros_like(dg_ref)

        dkvb = dkv_ref[...].astype(BF16)
        dkvb_ref[...] = dkvb
        dmn = _dot(dkvb, wkv_v[...])
        mh, _ = _rms_fwd(mem_ref[...])
        dg_ref[...] += _colsum(dmn * mh)

    return pl.pallas_call(
        body, name="bwd_kv", grid=(n_b,),
        in_specs=[pl.BlockSpec((N_MEM, 2 * D_MODEL), lambda b: (b, 0)), pl.BlockSpec((N_MEM, D_MODEL), lambda b: (b, 0)),
                  pl.BlockSpec(memory_space=pl.ANY)],
        out_specs=[pl.BlockSpec((N_MEM, 2 * D_MODEL), lambda b: (b, 0)), _full((1, D_MODEL))],
        out_shape=[jax.ShapeDtypeStruct((rows, 2 * D_MODEL), BF16), jax.ShapeDtypeStruct((1, D_MODEL), F32)],
        scratch_shapes=[pltpu.VMEM((2 * D_MODEL, D_MODEL), BF16), pltpu.SemaphoreType.DMA],
        compiler_params=_params(),
    )(dkv, mem2d, gw)


def _bwd_mix(dx1, x2d, u_all, c_all, pooled_all, gw, g_mix, conv_w, ln_g, ln_b, pool_w, pool_scale, after, seq, tm):
    tokens = x2d.shape[0]
    n_tiles = tokens // tm
    tps = seq // tm

    def body(dx1_ref, x_ref, u_ref, c_ref, pooled_ref, gmix_ref, gw_hbm, cw_ref, lng_ref, lnb_ref, pw_ref, ps_ref,
             after_ref, dx_ref, du_ref, dgmix_ref, dcw_ref, dcb_ref, dlng_ref, dlnb_ref, dpw_ref, dps_ref,
             win_v, wout_v, dc_carry, e_carry, sem):
        del after_ref
        i = pl.program_id(0)
        t = n_tiles - 1 - i

        _start_weights(gw_hbm, ("w_out", "w_in"), (wout_v, win_v), sem)

        @pl.when(i == 0)
        def _():
            for ref in (dgmix_ref, dcw_ref, dcb_ref, dlng_ref, dlnb_ref, dpw_ref, dps_ref):
                ref[...] = jnp.zeros_like(ref)

        @pl.when(t % tps == tps - 1)
        def _():
            dc_carry[...] = jnp.zeros_like(dc_carry)
            e_carry[...] = jnp.zeros_like(e_carry)

        dx1v = dx1_ref[...]
        dymix = _dot_nt(dx1v.astype(BF16), wout_v[...])
        dyc, dyp = dymix[:, :D_CONV], dymix[:, D_CONV:]
        u = u_ref[...]
        val, gate = u[:, :D_CONV], u[:, D_CONV:2 * D_CONV]

        conv = c_ref[...]
        mu = jnp.mean(conv, axis=-1, keepdims=True)
        cen = conv - mu
        rs = lax.rsqrt(jnp.mean(cen * cen, axis=-1, keepdims=True) + EPS)
        chat = cen * rs
        ln = chat * lng_ref[...] + lnb_ref[...]
        sl = _sigmoid(ln)
        dln = dyc * (sl * (1.0 + ln * (1.0 - sl)))
        dlng_ref[...] += _colsum(dln * chat)
        dlnb_ref[...] += _colsum(dln)
        dchat = dln * lng_ref[...]
        dc = rs * (dchat - jnp.mean(dchat, axis=-1, keepdims=True)
                   - chat * jnp.mean(dchat * chat, axis=-1, keepdims=True))
        dcb_ref[...] += _colsum(dc)
        sg = _sigmoid(gate)
        hc = val * sg
        ext = jnp.concatenate([dc, dc_carry[...]], axis=0)
        dc_carry[...] = dc[:CONV_HALO, :]
        dhc = jnp.zeros((tm, D_CONV), F32)
        ahead_by = _sublane_shifts(ext)
        for k in range(CONV_WIDTH):
            whole, part = divmod(CONV_WIDTH - 1 - k, 8)
            tap = ahead_by[part][8 * whole:8 * whole + tm, :]
            dhc = dhc + cw_ref[k:k + 1, :] * tap
            dcw_ref[k:k + 1, :] += _colsum_mxu(hc * tap)
        du_ref[:, :D_CONV] = (dhc * sg).astype(BF16)
        du_ref[:, D_CONV:2 * D_CONV] = (dhc * val * (sg * (1.0 - sg))).astype(BF16)

        pos = lax.broadcasted_iota(jnp.int32, (tm, 1), 0) + (t % tps) * tm
        es, dpooled = [], []
        for g, w in enumerate(POOL_WINDOWS):
            cols = pl.ds(g * POOL_GROUP_DIM, POOL_GROUP_DIM)
            lo = g * POOL_GROUP_DIM
            pooled = pooled_ref[:, cols]
            pw = pw_ref[g].astype(BF16)
            dyg = dyp[:, lo:lo + POOL_GROUP_DIM]
            dps_ref[:, cols] += _colsum(dyg * _dot(pooled, pw))
            dmixed = (dyg * ps_ref[:, cols]).astype(BF16)
            dpw_ref[g] += _dot_tn(pooled, dmixed)
            dpo = _dot_nt(dmixed, pw)
            dpooled.append(dpo)
            es.append(dpo / jnp.minimum(pos + 1, w).astype(F32))
        e = jnp.concatenate(es, axis=-1)
        run = jnp.concatenate([e, e_carry[...]], axis=0)
        e_carry[...] = e[:POOL_HALO, :]
        rows = tm + POOL_HALO
        for g, w in enumerate(POOL_WINDOWS):
            lo = g * POOL_GROUP_DIM
            run = run[:, POOL_GROUP_DIM if g else 0:]
            run = run + pltpu.roll(run, rows - w // 2, 0)
            du_ref[:, 2 * D_CONV + lo:2 * D_CONV + lo + POOL_GROUP_DIM] = (
                run[:tm, :POOL_GROUP_DIM] - dpooled[g]).astype(BF16)

        dh1 = _dot(du_ref[...], win_v[...])
        xh, r = _rms_fwd(x_ref[...])
        dgmix_ref[...] += _colsum(dh1 * xh)
        dx_ref[...] = dx1v + _rms_bwd(dh1, xh, r, gmix_ref[...])

    rev = lambda w: pl.BlockSpec((tm, w), lambda i: (n_tiles - 1 - i, 0))
    return pl.pallas_call(
        body, name="bwd_mix", grid=(n_tiles,),
        in_specs=[rev(D_MODEL), rev(D_MODEL), rev(D_IN), rev(D_CONV), rev(D_POOL), _full((1, D_MODEL)),
                  pl.BlockSpec(memory_space=pl.ANY), _full((CONV_WIDTH, D_CONV)), _full((1, D_CONV)), _full((1, D_CONV)),
                  _full((4, POOL_GROUP_DIM, POOL_GROUP_DIM)), _full((1, D_POOL)), _full(after.shape)],
        out_specs=[rev(D_MODEL), rev(D_IN), _full((1, D_MODEL)), _full((CONV_WIDTH, D_CONV)), _full((1, D_CONV)),
                   _full((1, D_CONV)), _full((1, D_CONV)), _full((4, POOL_GROUP_DIM, POOL_GROUP_DIM)), _full((1, D_POOL))],
        out_shape=[jax.ShapeDtypeStruct((tokens, D_MODEL), F32), jax.ShapeDtypeStruct((tokens, D_IN), BF16),
                   jax.ShapeDtypeStruct((1, D_MODEL), F32), jax.ShapeDtypeStruct((CONV_WIDTH, D_CONV), F32),
                   jax.ShapeDtypeStruct((1, D_CONV), F32), jax.ShapeDtypeStruct((1, D_CONV), F32),
                   jax.ShapeDtypeStruct((1, D_CONV), F32),
                   jax.ShapeDtypeStruct((4, POOL_GROUP_DIM, POOL_GROUP_DIM), F32), jax.ShapeDtypeStruct((1, D_POOL), F32)],
        scratch_shapes=[pltpu.VMEM((D_IN, D_MODEL), BF16), pltpu.VMEM((D_MODEL, D_MODEL), BF16),
                        pltpu.VMEM((CONV_HALO, D_CONV), F32), pltpu.VMEM((POOL_HALO, D_POOL), F32),
                        pltpu.SemaphoreType.DMA((2,))],
        compiler_params=_params(),
    )(dx1, x2d, u_all, c_all, pooled_all, g_mix, gw, conv_w, ln_g, ln_b, pool_w, pool_scale, after)


def _wgrad(a, b, name, after=None):
    tokens, m = a.shape
    n = b.shape[1]
    tm = 512 if m % 512 == 0 else 256
    extra = [] if after is None else [after]

    if m * n * 4 <= WGRAD_RESIDENT_BYTES:
        tk = min(WGRAD_TOKEN_BLOCK, tokens)
        n_k = tokens // tk

        def walk(a_ref, b_ref, *rest):
            out_ref, acc = rest[-2:]
            k = pl.program_id(0)

            @pl.when(k == 0)
            def _():
                acc[...] = jnp.zeros_like(acc)

            acc[...] += _dot_tn(a_ref[...], b_ref[...])

            @pl.when(k == n_k - 1)
            def _():
                out_ref[...] = acc[...].astype(out_ref.dtype)

        return pl.pallas_call(
            walk, name=name, grid=(n_k,),
            in_specs=[pl.BlockSpec((tk, m), lambda k: (k, 0)), pl.BlockSpec((tk, n), lambda k: (k, 0))] + [
                _full(t.shape) for t in extra],
            out_specs=_full((m, n)),
            out_shape=jax.ShapeDtypeStruct((m, n), BF16),
            scratch_shapes=[pltpu.VMEM((m, n), F32)],
            compiler_params=_params(),
        )(a, b, *extra)

    def body(a_ref, b_ref, *rest):
        rest[-1][...] = _dot_tn(a_ref[...], b_ref[...]).astype(rest[-1].dtype)

    return pl.pallas_call(
        body, name=name, grid=(m // tm,),
        in_specs=[pl.BlockSpec((tokens, tm), lambda i: (0, i)), _full((tokens, n))] + [_full(t.shape) for t in extra],
        out_specs=pl.BlockSpec((tm, n), lambda i: (i, 0)),
        out_shape=jax.ShapeDtypeStruct((m, n), BF16),
        compiler_params=_params(),
    )(a, b, *extra)


def _adamw_update(w, g, m, v):
    nm = ADAM_B1 * m + (1.0 - ADAM_B1) * g
    nv = ADAM_B2 * v + (1.0 - ADAM_B2) * (g * g)
    m_hat = nm / (1.0 - ADAM_B1 ** ADAM_STEP)
    v_hat = nv / (1.0 - ADAM_B2 ** ADAM_STEP)
    return -ADAM_LR * (m_hat / (jnp.sqrt(v_hat) + ADAM_EPS) + ADAM_WD * w), nm, nv


def _adamw_small(ws, gs, ms, vs):
    n = len(ws)

    def body(*refs):
        ins, outs = refs[:4 * n], refs[4 * n:]
        for k in range(n):
            d, nm, nv = _adamw_update(*[ins[j * n + k][...] for j in range(4)])
            outs[k][...] = d
            outs[n + k][...] = nm
            outs[2 * n + k][...] = nv

    vmem = pl.BlockSpec(memory_space=pltpu.VMEM)
    outs = pl.pallas_call(
        body, name="adamw_small",
        in_specs=[vmem] * (4 * n), out_specs=[vmem] * (3 * n),
        out_shape=[jax.ShapeDtypeStruct(w.shape, F32) for w in ws] * 3,
    )(*ws, *gs, *ms, *vs)
    return outs[:n], outs[n:2 * n], outs[2 * n:]


SMALL = (("norm_mix_g", (1, 1024)), ("conv_dw_b", (1, 512)), ("conv_ln_g", (1, 512)), ("conv_ln_b", (1, 512)),
         ("pool_w", (1, 4, 128, 128)), ("pool_scale", (1, 512)), ("norm_xattn_g", (1, 1024)), ("norm_mem_g", (1, 1024)),
         ("norm_ffn_g", (1, 1024)), ("ffn_dw_b", (1, 5632)), ("norm_final_g", (1024,)))
LANES = 128


def _pack_rows(arrs):
    flat = jnp.concatenate([a.reshape(-1) for a in arrs])
    pad = (-flat.shape[0]) % (8 * LANES)
    return jnp.pad(flat, (0, pad)).reshape(-1, LANES)


def kernel(x, mem, norm_mix_g, w_in, conv_dw_w, conv_dw_b, conv_ln_g, conv_ln_b, pool_w, pool_scale, w_out, norm_xattn_g, norm_mem_g, w_q, w_kv, w_o, norm_ffn_g, w_up, ffn_dw_w, ffn_dw_b, w_down, norm_final_g, loss_target, m_norm_mix_g, m_w_in, m_conv_dw_w, m_conv_dw_b, m_conv_ln_g, m_conv_ln_b, m_pool_w, m_pool_scale, m_w_out, m_norm_xattn_g, m_norm_mem_g, m_w_q, m_w_kv, m_w_o, m_norm_ffn_g, m_w_up, m_ffn_dw_w, m_ffn_dw_b, m_w_down, m_norm_final_g, v_norm_mix_g, v_w_in, v_conv_dw_w, v_conv_dw_b, v_conv_ln_g, v_conv_ln_b, v_pool_w, v_pool_scale, v_w_out, v_norm_xattn_g, v_norm_mem_g, v_w_q, v_w_kv, v_w_o, v_norm_ffn_g, v_w_up, v_ffn_dw_w, v_ffn_dw_b, v_w_down, v_norm_final_g):
    weights = dict(norm_mix_g=norm_mix_g, w_in=w_in, conv_dw_w=conv_dw_w, conv_dw_b=conv_dw_b, conv_ln_g=conv_ln_g,
                   conv_ln_b=conv_ln_b, pool_w=pool_w, pool_scale=pool_scale, w_out=w_out, norm_xattn_g=norm_xattn_g,
                   norm_mem_g=norm_mem_g, w_q=w_q, w_kv=w_kv, w_o=w_o, norm_ffn_g=norm_ffn_g, w_up=w_up,
                   ffn_dw_w=ffn_dw_w, ffn_dw_b=ffn_dw_b, w_down=w_down, norm_final_g=norm_final_g)
    moments_m = dict(norm_mix_g=m_norm_mix_g, w_in=m_w_in, conv_dw_w=m_conv_dw_w, conv_dw_b=m_conv_dw_b,
                     conv_ln_g=m_conv_ln_g, conv_ln_b=m_conv_ln_b, pool_w=m_pool_w, pool_scale=m_pool_scale,
                     w_out=m_w_out, norm_xattn_g=m_norm_xattn_g, norm_mem_g=m_norm_mem_g, w_q=m_w_q, w_kv=m_w_kv,
                     w_o=m_w_o, norm_ffn_g=m_norm_ffn_g, w_up=m_w_up, ffn_dw_w=m_ffn_dw_w, ffn_dw_b=m_ffn_dw_b,
                     w_down=m_w_down, norm_final_g=m_norm_final_g)
    moments_v = dict(norm_mix_g=v_norm_mix_g, w_in=v_w_in, conv_dw_w=v_conv_dw_w, conv_dw_b=v_conv_dw_b,
                     conv_ln_g=v_conv_ln_g, conv_ln_b=v_conv_ln_b, pool_w=v_pool_w, pool_scale=v_pool_scale,
                     w_out=v_w_out, norm_xattn_g=v_norm_xattn_g, norm_mem_g=v_norm_mem_g, w_q=v_w_q, w_kv=v_w_kv,
                     w_o=v_w_o, norm_ffn_g=v_norm_ffn_g, w_up=v_w_up, ffn_dw_w=v_ffn_dw_w, ffn_dw_b=v_ffn_dw_b,
                     w_down=v_w_down, norm_final_g=v_norm_final_g)
    order = list(weights)
    transposed = ("w_in", "w_kv", "w_up")

    n_b, seq, _ = x.shape
    tokens = n_b * seq
    tm_mix = min(512, seq // 2)
    tm_attn = min(1024, seq // 2)
    tm_ffn = min(256, seq // 2)
    dev = 4 * lax.axis_index("x") + 2 * lax.axis_index("y") + lax.axis_index("c")

    packs = [jnp.concatenate([weights[n][0].T if n in transposed else weights[n][0] for n in names], axis=0).astype(BF16)
             for names in AG_GROUPS]
    small_sharded = _pack_rows([conv_dw_w[0], ffn_dw_w[0]])
    n_small = small_sharded.size * 2 // D_MODEL
    bits = lax.bitcast_convert_type(small_sharded, jnp.uint32)
    halves = [lax.bitcast_convert_type(h.astype(jnp.uint16), BF16).reshape(n_small // 2, D_MODEL)
              for h in (bits >> 16, bits & 0xFFFF)]
    small_bits = jnp.concatenate(halves, axis=0)
    n_mix = packs[0].shape[0]
    packs[0] = jnp.concatenate([packs[0], small_bits, jnp.zeros((BF16_TILE_ROWS - n_small, D_MODEL), BF16)], axis=0)
    flights = []
    after = small_sharded
    for k in range(len(AG_GROUPS)):
        own_in_place = lax.dynamic_update_slice(lax.empty((N_DEV,) + packs[k].shape, BF16), packs[k][None], (dev, 0, 0))
        flights.append(_gather_start(own_in_place, after, "weights_gather_start_%d" % k, BARRIER_IDS["gather_start"][k]))
        after = flights[-1][3]

    def gather_finish(flight, after, tag):
        fwd_send, fwd_recv, buf = _gather_forward(*flight[:3], after, "weights_gather_forward_" + tag,
                                                  BARRIER_IDS["gather_forward"][int(tag)])
        return _gather_finish(fwd_send, fwd_recv, buf, "weights_gather_finish_" + tag)

    gw_mix = gather_finish(flights[0], after, "0")
    high, low = [lax.bitcast_convert_type(gw_mix[:, r:r + n_small // 2, :].reshape((N_DEV,) + small_sharded.shape),
                                          jnp.uint16).astype(jnp.uint32) for r in (n_mix, n_mix + n_small // 2)]
    gsmall = lax.bitcast_convert_type((high << 16) | low, F32)
    gflat = gsmall.reshape(N_DEV, -1)
    n_cw = CONV_WIDTH * (D_CONV // N_DEV)
    n_fw = FFN_CONV_WIDTH * (2 * D_FF // N_DEV)
    conv_w = gflat[:, :n_cw].reshape(N_DEV, CONV_WIDTH, D_CONV // N_DEV).transpose(1, 0, 2).reshape(CONV_WIDTH, D_CONV)
    ffn_w = gflat[:, n_cw:n_cw + n_fw].reshape(N_DEV, FFN_CONV_WIDTH, 2 * D_FF // N_DEV).transpose(1, 0, 2).reshape(
        FFN_CONV_WIDTH, 2 * D_FF)

    x2d = x.reshape(tokens, D_MODEL)
    mem2d = mem.reshape(n_b * N_MEM, D_MODEL)
    tgt2d = loss_target.reshape(tokens, D_MODEL)
    g_final = norm_final_g.reshape(1, D_MODEL)

    x1, u_all, c_all, pooled_all, ymix, h1 = _fwd_mix(
        x2d, gw_mix, norm_mix_g, conv_w, conv_dw_b, conv_ln_g, conv_ln_b, pool_w[0], pool_scale, flights[2][3],
        seq, tm_mix)
    gw_attn = gather_finish(flights[1], x1, "1")
    mem_n, kv = _fwd_kv(mem2d, gw_attn, norm_mem_g)
    x2, h2, q, o = _fwd_attn(x1, kv, gw_attn, norm_xattn_g, seq, tm_attn)
    gw_ffn = gather_finish(flights[2], x2, "2")
    uu_all, cc_all, a_all, h3, dx3, dx3b, loss_part, dg_final = _fwd_ffn(
        x2, tgt2d, gw_ffn, norm_ffn_g, ffn_w, ffn_dw_b, g_final, seq, tm_ffn)

    table = _owner_table()

    def sibling_start(names, tag):
        parts = [part[n].reshape(N_DEV, W_OFF[n][1], D_MODEL) for n in names]
        return _exchange_start(parts, 4, _to_sibling, "rs_sibling_exchange_start_" + tag, BARRIER_IDS["sibling"][tag])

    def chips_start(flight, after, tag):
        parts, landed = _exchange_wait(*flight[:4], after, 4, _to_sibling, "rs_sibling_exchange_wait_" + tag)
        sums = _chip_partial_sums(table, parts, landed, "rs_chip_partial_sums_" + tag)
        return parts, landed, _exchange_start(sums, 3, _to_chip, "rs_chip_exchange_start_" + tag,
                                              BARRIER_IDS["chips"][tag])

    grads, delta, new_m, new_v = {}, {}, {}, {}

    def reduce_finish(names, parts, landed, flight, after, tag):
        _, from_chips = _exchange_wait(*flight[:4], after, 3, _to_chip, "rs_chip_exchange_wait_" + tag)
        as_rows = {n: n in transposed and W_OFF[n][1] % LANES != 0 for n in names}
        states = [tuple(t[n][0].T if as_rows[n] else t[n][0] for t in (weights, moments_m, moments_v)) for n in names]
        results = _final_update(table, parts, landed, from_chips, states, "rs_final_update_" + tag)
        for n, res in zip(names, results):
            grads[n], delta[n], new_m[n], new_v[n] = [t.T[None] if as_rows[n] else t[None] for t in res]
        return delta[names[-1]]

    part = {}
    dx2, dx2b, duu, d_ffn_b, d_ffn_w, dg_ffn = _bwd_ffn(dx3, x2, uu_all, cc_all, gw_ffn, norm_ffn_g, ffn_w, seq, tm_ffn)
    part["w_up"] = _wgrad(duu, h3, "wgrad_w_up")
    part["w_down"] = _wgrad(a_all, dx3b, "wgrad_w_down")
    to_sibling_a = sibling_start(RS_GROUPS["a"], "a")
    dx1, dx1b, dq, dkv, dg_x = _bwd_attn(dx2, x1, q, kv, gw_attn, norm_xattn_g, to_sibling_a[4], seq, tm_mix)
    parts_a, landed_a, flight_a = chips_start(to_sibling_a, dx1, "a")
    dkv_b, dg_mem = _bwd_kv(dkv, mem2d, gw_attn)
    part["w_q"] = _wgrad(h2, dq, "wgrad_w_q", after=flight_a[4])
    part["w_kv"] = _wgrad(dkv_b, mem_n, "wgrad_w_kv", after=flight_a[4])
    part["w_out"] = _wgrad(ymix, dx1b, "wgrad_w_out", after=flight_a[4])
    to_sibling_b = sibling_start(RS_GROUPS["b"], "b")
    part["w_o"] = _wgrad(o, dx2b, "wgrad_w_o", after=to_sibling_b[4])
    parts_b, landed_b, flight_b = chips_start(to_sibling_b, part["w_o"], "b")
    dx, du, dg_mix, d_conv_w, d_conv_b, d_ln_g, d_ln_b, d_pool_w, d_pool_scale = _bwd_mix(
        dx1, x2d, u_all, c_all, pooled_all, gw_mix, norm_mix_g, conv_w, conv_ln_g, conv_ln_b, pool_w[0], pool_scale,
        flight_b[4], seq, tm_mix)
    grad_x = dx.reshape(x.shape)

    small_grads = dict(norm_mix_g=dg_mix, conv_dw_b=d_conv_b, conv_ln_g=d_ln_g, conv_ln_b=d_ln_b, pool_w=d_pool_w,
                       pool_scale=d_pool_scale, norm_xattn_g=dg_x, norm_mem_g=dg_mem, norm_ffn_g=dg_ffn,
                       ffn_dw_b=d_ffn_b, norm_final_g=dg_final)
    small_list = [small_grads[n] for n, _ in SMALL] + [d_conv_w, d_ffn_w, loss_part[:1]]
    small_mine = _pack_rows(small_list)
    small_flight = _broadcast_start(
        lax.dynamic_update_slice(lax.empty((N_DEV,) + small_mine.shape, F32), small_mine[None], (dev, 0, 0)),
        "small_grads_broadcast_start", BARRIER_IDS["broadcast"])

    part["w_in"] = _wgrad(du, h1, "wgrad_w_in", after=small_flight[3])
    to_sibling_c = sibling_start(RS_GROUPS["c"], "c")
    updated_b = reduce_finish(RS_GROUPS["b"], parts_b, landed_b, flight_b, to_sibling_c[4], "b")
    parts_c, landed_c, flight_c = chips_start(to_sibling_c, updated_b, "c")
    updated_a = reduce_finish(RS_GROUPS["a"], parts_a, landed_a, flight_a, flight_c[4], "a")
    small_all = _broadcast_wait(*small_flight[:3], updated_a, "small_grads_broadcast_wait")
    small_sum = _sum_blocks(small_all).reshape(-1)

    pos = 0
    for n, shape in SMALL:
        size = 1
        for s in shape:
            size *= s
        grads[n] = small_sum[pos:pos + size].reshape(shape)
        pos += size
    full_conv_w = small_sum[pos:pos + CONV_WIDTH * D_CONV].reshape(CONV_WIDTH, D_CONV)
    pos += CONV_WIDTH * D_CONV
    full_ffn_w = small_sum[pos:pos + FFN_CONV_WIDTH * 2 * D_FF].reshape(FFN_CONV_WIDTH, 2 * D_FF)
    loss = small_sum[pos + FFN_CONV_WIDTH * 2 * D_FF]
    grads["conv_dw_w"] = lax.dynamic_slice_in_dim(full_conv_w, dev * (D_CONV // N_DEV), D_CONV // N_DEV, axis=1)[None]
    grads["ffn_dw_w"] = lax.dynamic_slice_in_dim(full_ffn_w, dev * (2 * D_FF // N_DEV), 2 * D_FF // N_DEV, axis=1)[None]

    small_names = [n for n in order if n not in W_OFF]
    swap = lambda t: jnp.transpose(t, (1, 0, 2))
    two_d = lambda t: t.reshape(1, -1) if t.ndim == 1 else (swap(t) if t.ndim == 3 else t)
    outs = _adamw_small(*[[two_d(t[n]) for n in small_names] for t in (weights, grads, moments_m, moments_v)])
    for res, out in zip((delta, new_m, new_v), outs):
        for n, o in zip(small_names, out):
            res[n] = swap(o) if o.ndim == 3 else o.reshape(weights[n].shape)

    reduce_finish(RS_GROUPS["c"], parts_c, landed_c, flight_c, delta[small_names[-1]], "c")

    return (loss, grad_x, *[grads[n] for n in order], *[delta[n] for n in order],
            *[new_m[n] for n in order], *[new_v[n] for n in order])
```

```python
import jax
import jax.numpy as jnp
from jax import lax
from jax.experimental import pallas as pl
from jax.experimental.pallas import tpu as pltpu

F32 = jnp.float32
BF16 = jnp.bfloat16
MESH = pl.DeviceIdType.MESH

N_DEV = 8
D_MODEL = 1024
D_CONV = 512
D_POOL = 512
CONV_WIDTH = 31
POOL_WINDOWS = (2, 4, 8, 16)
POOL_GROUP_DIM = 128
D_IN = 1536
N_MEM = 256
HEADS = 4
HEAD_DIM = 256
D_FF = 2816
FFN_CONV_WIDTH = 3
EPS = 1e-6
ADAM_LR = 0.001
ADAM_B1 = 0.9
ADAM_B2 = 0.999
ADAM_EPS = 1e-08
ADAM_WD = 0.01
ADAM_STEP = 10

VMEM_LIMIT_V7X = 56 * 1024 * 1024
CONV_HALO = 32
POOL_HALO = 16
FFN_HALO = 8
FFN_CHUNK = 2816
WGRAD_RESIDENT_BYTES = 8 * 1024 * 1024
WGRAD_TOKEN_BLOCK = 1024
BF16_TILE_ROWS = 16

W_ROWS = (("w_in", 192), ("w_out", 128), ("w_q", 128), ("w_kv", 256), ("w_o", 128), ("w_up", 704), ("w_down", 352))
AG_GROUPS = (("w_in", "w_out"), ("w_q", "w_kv", "w_o"), ("w_up", "w_down"))
W_OFF = {}
for _names in AG_GROUPS:
    _o = 0
    for _n in _names:
        W_OFF[_n] = (_o, dict(W_ROWS)[_n])
        _o += dict(W_ROWS)[_n]
RS_GROUPS = {"a": ("w_up", "w_down"), "b": ("w_q", "w_kv", "w_out"), "c": ("w_o", "w_in")}
BARRIER_IDS = {"gather_start": (None, 0, 1), "gather_forward": (11, 2, 3), "sibling": {"a": 4, "b": 5, "c": 6},
               "chips": {"a": 7, "b": 8, "c": 9}, "broadcast": 10}


def _dot(a, b):
    return jnp.dot(a, b, preferred_element_type=F32)


def _dot_nt(a, b):
    return lax.dot_general(a, b, (((1,), (1,)), ((), ())), preferred_element_type=F32)


def _dot_tn(a, b):
    return lax.dot_general(a, b, (((0,), (0,)), ((), ())), preferred_element_type=F32)


def _sigmoid(v):
    return 1.0 / (1.0 + jnp.exp(-v))


def _rms_fwd(v):
    r = lax.rsqrt(jnp.mean(v * v, axis=-1, keepdims=True) + EPS)
    return v * r, r


def _rms_bwd(dh, vh, r, g):
    gd = dh * g
    return r * (gd - vh * jnp.mean(gd * vh, axis=-1, keepdims=True))


def _sublane_shifts(v):
    rows = v.shape[0]
    return [v] + [pltpu.roll(v, rows - b, 0) for b in range(1, 8)]


def _colsum(v):
    return jnp.sum(v, axis=0, keepdims=True)


def _colsum_mxu(v):
    return _dot(jnp.ones((8, v.shape[0]), BF16), v.astype(BF16))[0:1, :]


def _full(shape):
    return pl.BlockSpec(shape, lambda *_: (0,) * len(shape))


def _params(sem=("arbitrary",), vmem=VMEM_LIMIT_V7X):
    return pltpu.CompilerParams(dimension_semantics=sem, vmem_limit_bytes=vmem)


def _load_weight(g_hbm, name, dst, sem):
    off, rows = W_OFF[name]
    return [pltpu.make_async_copy(g_hbm.at[d, pl.ds(off, rows), :], dst.at[pl.ds(d * rows, rows), :], sem)
            for d in range(N_DEV)]


def _start_weights(g_hbm, names, dsts, sems):
    @pl.when(pl.program_id(0) == 0)
    def _():
        copies = [_load_weight(g_hbm, name, dst, sems.at[k]) for k, (name, dst) in enumerate(zip(names, dsts))]
        for j, cp in enumerate(sum(copies, [])):
            cp.start(priority=j % 2)
        for cp in sum(copies, []):
            cp.wait()


def _position():
    x, y, c = lax.axis_index("x"), lax.axis_index("y"), lax.axis_index("c")
    chips = [(1 - x, y), (x, 1 - y), (1 - x, 1 - y)]
    return x, y, c, chips


def _dev(px, py, pc):
    return 4 * px + 2 * py + pc


_HBM =pl.BlockSpec(memory_space=pltpu.HBM)
_SEM = pl.BlockSpec(memory_space=pltpu.SEMAPHORE)
_SIDE_EFFECT = pltpu.SideEffectType.DATAFLOW_SIDE_EFFECTING


def _handshake(peers):
    barrier = pltpu.get_barrier_semaphore()
    for peer in peers:
        pl.semaphore_signal(barrier, inc=1, device_id=peer, device_id_type=MESH)
    pl.semaphore_wait(barrier, len(peers))


def _gather_start(buf, after, name, collective_id):
    def body(buf_ref, after_ref, send_sems, recv_sems, buf_thru, token):
        del after_ref, buf_thru
        x, y, c, chips = _position()
        rows = buf_ref.at[_dev(x, y, c)]
        targets = [(x, y, 1 - c)] + [(*chip, c) for chip in chips]
        if collective_id is not None:
            _handshake(targets)
        for k, to in enumerate(targets):
            pltpu.make_async_remote_copy(src_ref=rows, dst_ref=rows, send_sem=send_sems.at[k], recv_sem=recv_sems.at[k],
                                         device_id=to, device_id_type=MESH).start()
        token[...] = jnp.zeros_like(token)

    return pl.pallas_call(
        body, name=name,
        out_shape=(pltpu.SemaphoreType.DMA((4,)), pltpu.SemaphoreType.DMA((4,)), pltpu.HBM(buf.shape, buf.dtype),
                   jax.ShapeDtypeStruct((8, 128), F32)),
        in_specs=(_HBM, pl.BlockSpec(memory_space=pl.ANY)),
        out_specs=(_SEM, _SEM, _HBM, pl.BlockSpec(memory_space=pltpu.VMEM)),
        input_output_aliases={0: 2},
        compiler_params=pltpu.CompilerParams(has_side_effects=_SIDE_EFFECT, collective_id=collective_id),
    )(pltpu.with_memory_space_constraint(buf, pltpu.HBM), after)


def _gather_forward(send_sems, recv_sems, buf, after, name, collective_id):
    def body(buf_ref, send_sems, recv_sems, after_ref, fwd_send, fwd_recv, buf_thru):
        del after_ref, buf_thru
        x, y, c, chips = _position()
        sibling = (x, y, 1 - c)

        def copy(block, k, sends, recvs):
            rows = buf_ref.at[_dev(*block)]
            return pltpu.make_async_remote_copy(src_ref=rows, dst_ref=rows, send_sem=sends.at[k], recv_sem=recvs.at[k],
                                                device_id=sibling, device_id_type=MESH)

        _handshake([sibling])
        for k in range(4):
            copy((x, y, c), k, send_sems, recv_sems).wait_send()
        copy(sibling, 0, send_sems, recv_sems).wait_recv()
        for j, chip in enumerate(chips):
            copy((*chip, c), 1 + j, send_sems, recv_sems).wait_recv()
            copy((*chip, c), j, fwd_send, fwd_recv).start()

    return pl.pallas_call(
        body, name=name,
        out_shape=(pltpu.SemaphoreType.DMA((3,)), pltpu.SemaphoreType.DMA((3,)), pltpu.HBM(buf.shape, buf.dtype)),
        in_specs=(_HBM, _SEM, _SEM, pl.BlockSpec(memory_space=pl.ANY)), out_specs=(_SEM, _SEM, _HBM),
        input_output_aliases={0: 2},
        compiler_params=pltpu.CompilerParams(has_side_effects=_SIDE_EFFECT, collective_id=collective_id),
    )(buf, send_sems, recv_sems, after)


def _gather_finish(fwd_send, fwd_recv, buf, name):
    def body(buf_ref, fwd_send, fwd_recv, buf_thru):
        del buf_thru
        x, y, c, chips = _position()
        for j, chip in enumerate(chips):
            cp = pltpu.make_async_remote_copy(
                src_ref=buf_ref.at[_dev(*chip, c)], dst_ref=buf_ref.at[_dev(*chip, 1 - c)], send_sem=fwd_send.at[j],
                recv_sem=fwd_recv.at[j], device_id=(x, y, 1 - c), device_id_type=MESH)
            cp.wait_send()
            cp.wait_recv()

    return pl.pallas_call(
        body, name=name,
        out_shape=pltpu.HBM(buf.shape, buf.dtype),
        in_specs=(_HBM, _SEM, _SEM), out_specs=_HBM,
        input_output_aliases={0: 0},
        compiler_params=pltpu.CompilerParams(has_side_effects=_SIDE_EFFECT),
    )(buf, fwd_send, fwd_recv)


def _everyone_else(x, y, c, chips):
    return [(x, y, 1 - c)] + [(*chip, core) for chip in chips for core in (c, 1 - c)]


def _broadcast_start(buf, name, collective_id):
    def body(buf_ref, send_sems, recv_sems, buf_thru, token):
        del buf_thru
        x, y, c, chips = _position()
        rows = buf_ref.at[_dev(x, y, c)]
        _handshake(_everyone_else(x, y, c, chips))
        for k, to in enumerate(_everyone_else(x, y, c, chips)):
            pltpu.make_async_remote_copy(src_ref=rows, dst_ref=rows, send_sem=send_sems.at[k], recv_sem=recv_sems.at[k],
                                         device_id=to, device_id_type=MESH).start()
        token[...] = jnp.zeros_like(token)

    return pl.pallas_call(
        body, name=name,
        out_shape=(pltpu.SemaphoreType.DMA((7,)), pltpu.SemaphoreType.DMA((7,)), pltpu.HBM(buf.shape, buf.dtype),
                   jax.ShapeDtypeStruct((8, 128), F32)),
        in_specs=(_HBM,), out_specs=(_SEM, _SEM, _HBM, pl.BlockSpec(memory_space=pltpu.VMEM)),
        input_output_aliases={0: 2},
        compiler_params=pltpu.CompilerParams(has_side_effects=_SIDE_EFFECT, collective_id=collective_id),
    )(pltpu.with_memory_space_constraint(buf, pltpu.HBM))


def _broadcast_wait(send_sems, recv_sems, buf, after, name):
    def body(buf_ref, send_sems, recv_sems, after_ref, buf_thru):
        del after_ref, buf_thru
        x, y, c, chips = _position()
        for k, peer in enumerate(_everyone_else(x, y, c, chips)):
            cp = pltpu.make_async_remote_copy(
                src_ref=buf_ref.at[_dev(x, y, c)], dst_ref=buf_ref.at[_dev(*peer)], send_sem=send_sems.at[k],
                recv_sem=recv_sems.at[k], device_id=peer, device_id_type=MESH)
            cp.wait_send()
            cp.wait_recv()

    return pl.pallas_call(
        body, name=name,
        out_shape=pltpu.HBM(buf.shape, buf.dtype),
        in_specs=(_HBM, _SEM, _SEM, pl.BlockSpec(memory_space=pl.ANY)), out_specs=_HBM,
        input_output_aliases={0: 0},
        compiler_params=pltpu.CompilerParams(has_side_effects=_SIDE_EFFECT),
    )(buf, send_sems, recv_sems, after)


def _to_sibling(j, x, y, c, chips):
    return _dev(*([(x, y)] + chips)[j], 1 - c), (x, y, 1 - c)


def _to_chip(j, x, y, c, chips):
    return j, (*chips[j], c)


def _exchange_start(srcs, n_slots, route, name, collective_id):
    n = len(srcs)

    def body(*refs):
        s_refs, land_refs = refs[:n], refs[n:2 * n]
        send_sems, recv_sems = refs[2 * n:2 * n + 2]
        token = refs[-1]
        x, y, c, chips = _position()
        _handshake([(x, y, 1 - c)] if route is _to_sibling else [route(j, x, y, c, chips)[1] for j in range(n_slots)])
        for k in range(n):
            for j in range(n_slots):
                block, to = route(j, x, y, c, chips)
                pltpu.make_async_remote_copy(
                    src_ref=s_refs[k].at[block], dst_ref=land_refs[k].at[j], send_sem=send_sems.at[n_slots * k + j],
                    recv_sem=recv_sems.at[n_slots * k + j], device_id=to, device_id_type=MESH).start()
        token[...] = jnp.zeros_like(token)

    lands = [jax.ShapeDtypeStruct((n_slots,) + s.shape[1:], s.dtype) for s in srcs]
    outs = pl.pallas_call(
        body, name=name,
        out_shape=(pltpu.SemaphoreType.DMA((n_slots * n,)), pltpu.SemaphoreType.DMA((n_slots * n,)),
                   *[pltpu.HBM(s.shape, s.dtype) for s in srcs], *[pltpu.HBM(l.shape, l.dtype) for l in lands],
                   jax.ShapeDtypeStruct((8, 128), F32)),
        in_specs=[_HBM] * (2 * n), out_specs=(_SEM, _SEM, *[_HBM] * (2 * n), pl.BlockSpec(memory_space=pltpu.VMEM)),
        input_output_aliases={k: 2 + k for k in range(2 * n)},
        compiler_params=pltpu.CompilerParams(has_side_effects=_SIDE_EFFECT, collective_id=collective_id),
    )(*[pltpu.with_memory_space_constraint(s, pltpu.HBM) for s in srcs],
      *[pltpu.with_memory_space_constraint(lax.empty(l.shape, l.dtype), pltpu.HBM) for l in lands])
    return outs[0], outs[1], outs[2:2 + n], outs[2 + n:2 + 2 * n], outs[-1]


def _exchange_wait(send_sems, recv_sems, s_thru, land_thru, after, n_slots, route, name):
    n = len(s_thru)

    def body(*refs):
        s_refs, land_refs = refs[:n], refs[n:2 * n]
        send_sems, recv_sems = refs[2 * n:2 * n + 2]
        x, y, c, chips = _position()
        for k in range(n):
            for j in range(n_slots):
                block, to = route(j, x, y, c, chips)
                cp = pltpu.make_async_remote_copy(
                    src_ref=s_refs[k].at[block], dst_ref=land_refs[k].at[j], send_sem=send_sems.at[n_slots * k + j],
                    recv_sem=recv_sems.at[n_slots * k + j], device_id=to, device_id_type=MESH)
                cp.wait_send()
                cp.wait_recv()

    outs = pl.pallas_call(
        body, name=name,
        out_shape=(*[pltpu.HBM(s.shape, s.dtype) for s in s_thru], *[pltpu.HBM(l.shape, l.dtype) for l in land_thru]),
        in_specs=[_HBM] * (2 * n) + [_SEM, _SEM, pl.BlockSpec(memory_space=pl.ANY)], out_specs=[_HBM] * (2 * n),
        input_output_aliases={k: k for k in range(2 * n)},
        compiler_params=pltpu.CompilerParams(has_side_effects=_SIDE_EFFECT),
    )(*s_thru, *land_thru, send_sems, recv_sems, after)
    return outs[:n], outs[n:]


def _owner_table():
    x, y, c = lax.axis_index("x"), lax.axis_index("y"), lax.axis_index("c")
    chips = [(x, y), (1 - x, y), (x, 1 - y), (1 - x, 1 - y)]
    return jnp.stack([_dev(px, py, c) for px, py in chips]).astype(jnp.int32)


def _chip_partial_sums(table, parts, from_sibling, name):
    n = len(parts)

    def body(tab_ref, *refs):
        del tab_ref
        for g_ref, l_ref, out_ref in zip(refs[:n], refs[n:2 * n], refs[2 * n:]):
            out_ref[...] = (g_ref[...].astype(F32) + l_ref[...].astype(F32)).astype(out_ref.dtype)

    block = lambda p: (None,) + p.shape[1:]
    grid_spec = pltpu.PrefetchScalarGridSpec(
        num_scalar_prefetch=1, grid=(3,),
        in_specs=[pl.BlockSpec(block(p), lambda j, tab: (tab[j + 1], 0, 0)) for p in parts]
        + [pl.BlockSpec(block(p), lambda j, tab: (j + 1, 0, 0)) for p in parts],
        out_specs=[pl.BlockSpec(block(p), lambda j, tab: (j, 0, 0)) for p in parts])
    return pl.pallas_call(
        body, name=name, grid_spec=grid_spec,
        out_shape=[jax.ShapeDtypeStruct((3,) + p.shape[1:], BF16) for p in parts],
        compiler_params=_params(("arbitrary",)),
    )(table, *parts, *from_sibling)


def _final_update(table, parts, from_sibling, from_chips, states, name):
    n = len(parts)
    flipped = [states[k][0].shape != parts[k].shape[1:] for k in range(n)]

    def body(tab_ref, *refs):
        del tab_ref
        ins, outs = refs[:6 * n], refs[6 * n:]
        for k in range(n):
            acc = ins[k][...].astype(F32) + ins[n + k][...].astype(F32)
            for j in range(3):
                acc = acc + ins[2 * n + k][j].astype(F32)
            if flipped[k]:
                acc = acc.T
            w_ref, m_ref, v_ref = ins[3 * n + 3 * k:3 * n + 3 * k + 3]
            outs[4 * k][...] = acc
            for out_ref, val in zip(outs[4 * k + 1:4 * k + 4], _adamw_update(w_ref[...], acc, m_ref[...], v_ref[...])):
                out_ref[...] = val

    def grad_block(k, lead, at):
        r, c = parts[k].shape[1:]
        if flipped[k]:
            return pl.BlockSpec(lead + (r, c // 2), lambda t, tab: (*at(tab), 0, t))
        return pl.BlockSpec(lead + (r // 2, c), lambda t, tab: (*at(tab), t, 0))

    def state_block(k):
        a, b = states[k][0].shape
        return pl.BlockSpec((a // 2, b), lambda t, tab: (t, 0))

    grid_spec = pltpu.PrefetchScalarGridSpec(
        num_scalar_prefetch=1, grid=(2,),
        in_specs=[grad_block(k, (None,), lambda tab: (tab[0],)) for k in range(n)]
        + [grad_block(k, (None,), lambda tab: (0,)) for k in range(n)]
        + [grad_block(k, (3,), lambda tab: (0,)) for k in range(n)]
        + [state_block(k) for k in range(n) for _ in range(3)],
        out_specs=[state_block(k) for k in range(n) for _ in range(4)])
    outs = pl.pallas_call(
        body, name=name, grid_spec=grid_spec,
        out_shape=[jax.ShapeDtypeStruct(states[k][0].shape, F32) for k in range(n) for _ in range(4)],
        compiler_params=_params(("arbitrary",)),
    )(table, *parts, *from_sibling, *from_chips, *[t for k in range(n) for t in states[k]])
    return [outs[4 * k:4 * k + 4] for k in range(n)]


def _sum_blocks(g8):
    _, rows, cols = g8.shape

    def body(g_ref, out_ref):
        acc = g_ref[0]
        for d in range(1, N_DEV):
            acc = acc + g_ref[d]
        out_ref[...] = acc

    return pl.pallas_call(
        body, name="small_grad_sum", grid=(1,),
        in_specs=[_full((N_DEV, rows, cols))], out_specs=_full((rows, cols)),
        out_shape=jax.ShapeDtypeStruct((rows, cols), F32),
        compiler_params=_params(("arbitrary",)),
    )(g8)


def _fwd_mix(x2d, gw, g_mix, conv_w, conv_b, ln_g, ln_b, pool_w, pool_scale, after, seq, tm):
    tokens = x2d.shape[0]
    n_tiles = tokens // tm
    tps = seq // tm

    def body(x_ref, gmix_ref, gw_hbm, cw_ref, cb_ref, lng_ref, lnb_ref, pw_ref, ps_ref, after_ref,
             x1_ref, u_ref, c_ref, pooled_ref, ymix_ref, h1_ref,
             win_v, wout_v, hc_carry, up_carry, sem):
        del after_ref
        i = pl.program_id(0)

        _start_weights(gw_hbm, ("w_in", "w_out"), (win_v, wout_v), sem)

        @pl.when(i % tps == 0)
        def _():
            hc_carry[...] = jnp.zeros_like(hc_carry)
            up_carry[...] = jnp.zeros_like(up_carry)

        x = x_ref[...]
        xh, _ = _rms_fwd(x)
        h1 = (xh * gmix_ref[...]).astype(BF16)
        h1_ref[...] = h1
        u = _dot_nt(h1, win_v[...])
        u_ref[...] = u
        val, gate, up = u[:, :D_CONV], u[:, D_CONV:2 * D_CONV], u[:, 2 * D_CONV:]

        extp = jnp.concatenate([up_carry[...], up], axis=0)
        up_carry[...] = up[tm - POOL_HALO:, :]
        pos = lax.broadcasted_iota(jnp.int32, (tm, 1), 0) + (i % tps) * tm
        run = extp
        mixed = []
        for g, w in enumerate(POOL_WINDOWS):
            lo = g * POOL_GROUP_DIM
            run = run[:, POOL_GROUP_DIM if g else 0:]
            run = run + pltpu.roll(run, w // 2, 0)
            cnt = jnp.minimum(pos + 1, w).astype(F32)
            pooled = run[POOL_HALO:, :POOL_GROUP_DIM] / cnt - up[:, lo:lo + POOL_GROUP_DIM]
            pooled = pooled.astype(BF16)
            pooled_ref[:, lo:lo + POOL_GROUP_DIM] = pooled
            mixed.append(_dot(pooled, pw_ref[g].astype(BF16)))
        y_pool = jnp.concatenate(mixed, axis=-1) * ps_ref[...]
        y_pool = y_pool.astype(BF16)
        ymix_ref[:, D_CONV:] = y_pool
        out = _dot(y_pool, wout_v[D_CONV:, :])

        hc = val * _sigmoid(gate)
        ext = jnp.concatenate([hc_carry[...], hc], axis=0)
        hc_carry[...] = hc[tm - CONV_HALO:, :]
        conv = jnp.broadcast_to(cb_ref[...], (tm, D_CONV))
        ahead_by = _sublane_shifts(ext)
        for k in range(CONV_WIDTH):
            whole, part = divmod(CONV_HALO - (CONV_WIDTH - 1) + k, 8)
            conv = conv + cw_ref[k:k + 1, :] * ahead_by[part][8 * whole:8 * whole + tm, :]
        c_ref[...] = conv
        mu = jnp.mean(conv, axis=-1, keepdims=True)
        cen = conv - mu
        ln = cen * lax.rsqrt(jnp.mean(cen * cen, axis=-1, keepdims=True) + EPS) * lng_ref[...] + lnb_ref[...]
        y_conv = ln * _sigmoid(ln)
        y_conv = y_conv.astype(BF16)
        ymix_ref[:, :D_CONV] = y_conv
        x1_ref[...] = x + (out + _dot(y_conv, wout_v[:D_CONV, :]))

    row = lambda w: pl.BlockSpec((tm, w), lambda i: (i, 0))
    return pl.pallas_call(
        body, name="fwd_mix", grid=(n_tiles,),
        in_specs=[row(D_MODEL), _full((1, D_MODEL)), pl.BlockSpec(memory_space=pl.ANY),
                  _full((CONV_WIDTH, D_CONV)), _full((1, D_CONV)), _full((1, D_CONV)), _full((1, D_CONV)),
                  _full((4, POOL_GROUP_DIM, POOL_GROUP_DIM)), _full((1, D_POOL)), _full(after.shape)],
        out_specs=[row(D_MODEL), row(D_IN), row(D_CONV), row(D_POOL), row(D_MODEL), row(D_MODEL)],
        out_shape=[jax.ShapeDtypeStruct((tokens, D_MODEL), F32), jax.ShapeDtypeStruct((tokens, D_IN), F32),
                   jax.ShapeDtypeStruct((tokens, D_CONV), F32), jax.ShapeDtypeStruct((tokens, D_POOL), BF16),
                   jax.ShapeDtypeStruct((tokens, D_MODEL), BF16), jax.ShapeDtypeStruct((tokens, D_MODEL), BF16)],
        scratch_shapes=[pltpu.VMEM((D_IN, D_MODEL), BF16), pltpu.VMEM((D_MODEL, D_MODEL), BF16),
                        pltpu.VMEM((CONV_HALO, D_CONV), F32), pltpu.VMEM((POOL_HALO, D_POOL), F32),
                        pltpu.SemaphoreType.DMA((2,))],
        compiler_params=_params(),
    )(x2d, g_mix, gw, conv_w, conv_b, ln_g, ln_b, pool_w, pool_scale, after)


def _fwd_kv(mem2d, gw, g_mem):
    rows = mem2d.shape[0]
    n_b = rows // N_MEM

    def body(mem_ref, g_ref, gw_hbm, mn_ref, kv_ref, wkv_v, sem):
        @pl.when(pl.program_id(0) == 0)
        def _():
            copies = _load_weight(gw_hbm, "w_kv", wkv_v, sem)
            for cp in copies:
                cp.start()
            for cp in copies:
                cp.wait()

        mh, _ = _rms_fwd(mem_ref[...])
        mn = (mh * g_ref[...]).astype(BF16)
        mn_ref[...] = mn
        kv_ref[...] = _dot_nt(mn, wkv_v[...]).astype(BF16)

    return pl.pallas_call(
        body, name="fwd_kv", grid=(n_b,),
        in_specs=[pl.BlockSpec((N_MEM, D_MODEL), lambda b: (b, 0)), _full((1, D_MODEL)), pl.BlockSpec(memory_space=pl.ANY)],
        out_specs=[pl.BlockSpec((N_MEM, D_MODEL), lambda b: (b, 0)), pl.BlockSpec((N_MEM, 2 * D_MODEL), lambda b: (b, 0))],
        out_shape=[jax.ShapeDtypeStruct((rows, D_MODEL), BF16), jax.ShapeDtypeStruct((rows, 2 * D_MODEL), BF16)],
        scratch_shapes=[pltpu.VMEM((2 * D_MODEL, D_MODEL), BF16), pltpu.SemaphoreType.DMA],
        compiler_params=_params(),
    )(mem2d, g_mem, gw)


def _softmax_rows(s):
    e = jnp.exp(s - jnp.max(s, axis=-1, keepdims=True))
    return e / jnp.sum(e, axis=-1, keepdims=True)


def _fwd_attn(x1, kv, gw, g_x, seq, tm):
    tokens = x1.shape[0]
    n_tiles = tokens // tm
    tps = seq // tm

    def body(x1_ref, kv_ref, g_ref, gw_hbm, x2_ref, h2_ref, q_ref, o_ref, wq_v, wo_v, sem):
        _start_weights(gw_hbm, ("w_q", "w_o"), (wq_v, wo_v), sem)
        x1v = x1_ref[...]
        xh, _ = _rms_fwd(x1v)
        h2 = (xh * g_ref[...]).astype(BF16)
        h2_ref[...] = h2
        q = (_dot(h2, wq_v[...]) * (HEAD_DIM ** -0.5)).astype(BF16)
        q_ref[...] = q
        heads = [slice(h * HEAD_DIM, (h + 1) * HEAD_DIM) for h in range(HEADS)]
        scores = [_dot_nt(q[:, hd], kv_ref[:, hd]) for hd in heads]
        probs = [_softmax_rows(s).astype(BF16) for s in scores]
        outs = [_dot(p, kv_ref[:, pl.ds(D_MODEL + h * HEAD_DIM, HEAD_DIM)]) for h, p in enumerate(probs)]
        o = jnp.concatenate(outs, axis=-1).astype(BF16)
        o_ref[...] = o
        x2_ref[...] = x1v + _dot(o, wo_v[...])

    row = lambda w: pl.BlockSpec((tm, w), lambda i: (i, 0))
    return pl.pallas_call(
        body, name="fwd_attn", grid=(n_tiles,),
        in_specs=[row(D_MODEL), pl.BlockSpec((N_MEM, 2 * D_MODEL), lambda i: (i // tps, 0)), _full((1, D_MODEL)),
                  pl.BlockSpec(memory_space=pl.ANY)],
        out_specs=[row(D_MODEL)] * 4,
        out_shape=[jax.ShapeDtypeStruct((tokens, D_MODEL), F32)] + [jax.ShapeDtypeStruct((tokens, D_MODEL), BF16)] * 3,
        scratch_shapes=[pltpu.VMEM((D_MODEL, D_MODEL), BF16), pltpu.VMEM((D_MODEL, D_MODEL), BF16), pltpu.SemaphoreType.DMA((2,))],
        compiler_params=_params(),
    )(x1, kv, g_x, gw)


def _ffn_conv(uu, halo, w_ref, b_ref, cols):
    ext = jnp.concatenate([halo, uu], axis=0)
    p1 = pltpu.roll(ext, 1, 0)[FFN_HALO:, :]
    p2 = pltpu.roll(ext, 2, 0)[FFN_HALO:, :]
    return b_ref[:, cols] + w_ref[2:3, cols] * uu + w_ref[1:2, cols] * p1 + w_ref[0:1, cols] * p2


def _fwd_ffn(x2, target, gw, g_ffn, ffn_w, ffn_b, g_final, seq, tm):
    tokens = x2.shape[0]
    n_tiles = tokens // tm
    tps = seq // tm
    n_chunks = D_FF // FFN_CHUNK

    def body(x2_ref, tgt_ref, gffn_ref, gw_hbm, fw_ref, fb_ref, gfin_ref,
             uu_ref, cc_ref, a_ref, h3_ref, dx3_ref, dx3b_ref, loss_ref, dgfin_ref,
             wup_v, wdown_v, carry, sem):
        i = pl.program_id(0)

        _start_weights(gw_hbm, ("w_up", "w_down"), (wup_v, wdown_v), sem)

        @pl.when(i == 0)
        def _():
            loss_ref[...] = jnp.zeros_like(loss_ref)
            dgfin_ref[...] = jnp.zeros_like(dgfin_ref)

        @pl.when(i % tps == 0)
        def _():
            carry[...] = jnp.zeros_like(carry)

        x2v = x2_ref[...]
        xh, _ = _rms_fwd(x2v)
        h3 = (xh * gffn_ref[...]).astype(BF16)
        h3_ref[...] = h3
        acc = jnp.zeros((tm, D_MODEL), F32)
        for jc in range(n_chunks):
            halves = []
            for half in range(2):
                cols = pl.ds(half * D_FF + jc * FFN_CHUNK, FFN_CHUNK)
                uu = _dot_nt(h3, wup_v[cols, :])
                uu_ref[:, cols] = uu.astype(BF16)
                cc = _ffn_conv(uu, carry[:, cols], fw_ref, fb_ref, cols)
                cc_ref[:, cols] = cc.astype(BF16)
                halves.append(cc)
                carry[:, cols] = uu[tm - FFN_HALO:, :]
            gate, val = halves
            a = (gate * _sigmoid(gate) * val).astype(BF16)
            a_ref[:, pl.ds(jc * FFN_CHUNK, FFN_CHUNK)] = a
            acc = acc + _dot(a, wdown_v[pl.ds(jc * FFN_CHUNK, FFN_CHUNK), :])
        x3 = x2v + acc

        xh3, r3 = _rms_fwd(x3)
        gfin = gfin_ref[...]
        err = xh3 * gfin - tgt_ref[...]
        loss_ref[...] += jnp.full(loss_ref.shape, jnp.sum(err * err) * (0.5 / D_MODEL), F32)
        dy = err * (1.0 / D_MODEL)
        dgfin_ref[...] += _colsum(dy * xh3)
        dx3 = _rms_bwd(dy, xh3, r3, gfin)
        dx3_ref[...] = dx3
        dx3b_ref[...] = dx3.astype(BF16)

    row = lambda w: pl.BlockSpec((tm, w), lambda i: (i, 0))
    return pl.pallas_call(
        body, name="fwd_ffn", grid=(n_tiles,),
        in_specs=[row(D_MODEL), row(D_MODEL), _full((1, D_MODEL)), pl.BlockSpec(memory_space=pl.ANY),
                  _full((FFN_CONV_WIDTH, 2 * D_FF)), _full((1, 2 * D_FF)), _full((1, D_MODEL))],
        out_specs=[row(2 * D_FF), row(2 * D_FF), row(D_FF), row(D_MODEL), row(D_MODEL), row(D_MODEL), _full((8, 128)),
                   _full((1, D_MODEL))],
        out_shape=[jax.ShapeDtypeStruct((tokens, 2 * D_FF), BF16), jax.ShapeDtypeStruct((tokens, 2 * D_FF), BF16),
                   jax.ShapeDtypeStruct((tokens, D_FF), BF16),
                   jax.ShapeDtypeStruct((tokens, D_MODEL), BF16), jax.ShapeDtypeStruct((tokens, D_MODEL), F32),
                   jax.ShapeDtypeStruct((tokens, D_MODEL), BF16),
                   jax.ShapeDtypeStruct((8, 128), F32), jax.ShapeDtypeStruct((1, D_MODEL), F32)],
        scratch_shapes=[pltpu.VMEM((2 * D_FF, D_MODEL), BF16), pltpu.VMEM((D_FF, D_MODEL), BF16),
                        pltpu.VMEM((FFN_HALO, 2 * D_FF), F32), pltpu.SemaphoreType.DMA((2,))],
        compiler_params=_params(),
    )(x2, target, g_ffn, gw, ffn_w, ffn_b, g_final)


def _bwd_ffn(dx3, x2, uu_all, cc_all, gw, g_ffn, ffn_w, seq, tm):
    tokens = x2.shape[0]
    n_tiles = tokens // tm
    tps = seq // tm
    n_chunks = D_FF // FFN_CHUNK

    def body(dx3_ref, x2_ref, uu_ref, cc_ref, gffn_ref, gw_hbm, fw_ref,
             dx2_ref, dx2b_ref, duu_ref, dfb_ref, dfw_ref, dg_ref,
             wup_v, wdown_v, carry, sem):
        i = pl.program_id(0)
        t = n_tiles - 1 - i

        _start_weights(gw_hbm, ("w_down", "w_up"), (wdown_v, wup_v), sem)

        @pl.when(i == 0)
        def _():
            dfb_ref[...] = jnp.zeros_like(dfb_ref)
            dfw_ref[...] = jnp.zeros_like(dfw_ref)
            dg_ref[...] = jnp.zeros_like(dg_ref)

        @pl.when(t % tps == tps - 1)
        def _():
            carry[...] = jnp.zeros_like(carry)

        dx3v = dx3_ref[...]
        dx3b = dx3v.astype(BF16)
        dh3 = jnp.zeros((tm, D_MODEL), F32)
        for jc in range(n_chunks):
            da = _dot_nt(dx3b, wdown_v[pl.ds(jc * FFN_CHUNK, FFN_CHUNK), :])
            colss = [pl.ds(half * D_FF + jc * FFN_CHUNK, FFN_CHUNK) for half in range(2)]
            gate, val = [cc_ref[:, cols].astype(F32) for cols in colss]
            sg = _sigmoid(gate)
            dgate = da * val * (sg * (1.0 + gate * (1.0 - sg)))
            dval = da * (gate * sg)
            for dcc, cols in zip((dgate, dval), colss):
                uu = uu_ref[:, cols].astype(F32)
                dfb_ref[:, cols] += _colsum(dcc)
                ext = jnp.concatenate([dcc, carry[:, cols]], axis=0)
                carry[:, cols] = dcc[:FFN_HALO, :]
                n1 = pltpu.roll(ext, tm + FFN_HALO - 1, 0)[:tm, :]
                n2 = pltpu.roll(ext, tm + FFN_HALO - 2, 0)[:tm, :]
                duu = fw_ref[2:3, cols] * dcc + fw_ref[1:2, cols] * n1 + fw_ref[0:1, cols] * n2
                dfw_ref[2:3, cols] += _colsum(uu * dcc)
                dfw_ref[1:2, cols] += _colsum(uu * n1)
                dfw_ref[0:1, cols] += _colsum(uu * n2)
                duub = duu.astype(BF16)
                duu_ref[:, cols] = duub
                dh3 = dh3 + _dot(duub, wup_v[cols, :])
        xh, r = _rms_fwd(x2_ref[...])
        dg_ref[...] += _colsum(dh3 * xh)
        dx2 = dx3v + _rms_bwd(dh3, xh, r, gffn_ref[...])
        dx2_ref[...] = dx2
        dx2b_ref[...] = dx2.astype(BF16)

    rev = lambda w: pl.BlockSpec((tm, w), lambda i: (n_tiles - 1 - i, 0))
    return pl.pallas_call(
        body, name="bwd_ffn", grid=(n_tiles,),
        in_specs=[rev(D_MODEL), rev(D_MODEL), rev(2 * D_FF), rev(2 * D_FF), _full((1, D_MODEL)),
                  pl.BlockSpec(memory_space=pl.ANY), _full((FFN_CONV_WIDTH, 2 * D_FF))],
        out_specs=[rev(D_MODEL), rev(D_MODEL), rev(2 * D_FF), _full((1, 2 * D_FF)), _full((FFN_CONV_WIDTH, 2 * D_FF)),
                   _full((1, D_MODEL))],
        out_shape=[jax.ShapeDtypeStruct((tokens, D_MODEL), F32), jax.ShapeDtypeStruct((tokens, D_MODEL), BF16),
                   jax.ShapeDtypeStruct((tokens, 2 * D_FF), BF16),
                   jax.ShapeDtypeStruct((1, 2 * D_FF), F32), jax.ShapeDtypeStruct((FFN_CONV_WIDTH, 2 * D_FF), F32),
                   jax.ShapeDtypeStruct((1, D_MODEL), F32)],
        scratch_shapes=[pltpu.VMEM((2 * D_FF, D_MODEL), BF16), pltpu.VMEM((D_FF, D_MODEL), BF16),
                        pltpu.VMEM((FFN_HALO, 2 * D_FF), F32), pltpu.SemaphoreType.DMA((2,))],
        compiler_params=_params(),
    )(dx3, x2, uu_all, cc_all, g_ffn, gw, ffn_w)


def _bwd_attn(dx2, x1, q, kv, gw, g_x, after, seq, tm):
    tokens = x1.shape[0]
    n_tiles = tokens // tm
    tps = seq // tm
    n_b = tokens // seq

    def body(dx2_ref, x1_ref, q_ref, kv_ref, g_ref, gw_hbm, after_ref, dx1_ref, dx1b_ref, dq_ref, dkv_ref, dg_ref,
             wq_v, wo_v, sem):
        del after_ref
        i = pl.program_id(0)

        _start_weights(gw_hbm, ("w_o", "w_q"), (wo_v, wq_v), sem)

        @pl.when(i == 0)
        def _():
            dg_ref[...] = jnp.zeros_like(dg_ref)

        @pl.when(i % tps == 0)
        def _():
            dkv_ref[...] = jnp.zeros_like(dkv_ref)

        dx2v = dx2_ref[...]
        do = _dot_nt(dx2v.astype(BF16), wo_v[...]).astype(BF16)
        q = q_ref[...]
        heads = [slice(h * HEAD_DIM, (h + 1) * HEAD_DIM) for h in range(HEADS)]
        kcols = [pl.ds(h * HEAD_DIM, HEAD_DIM) for h in range(HEADS)]
        vcols = [pl.ds(D_MODEL + h * HEAD_DIM, HEAD_DIM) for h in range(HEADS)]
        scores = [_dot_nt(q[:, hd], kv_ref[:, kc]) for hd, kc in zip(heads, kcols)]
        dps = [_dot_nt(do[:, hd], kv_ref[:, vc]) for hd, vc in zip(heads, vcols)]
        probs = [_softmax_rows(s) for s in scores]
        dss = [(p * (dp - jnp.sum(dp * p, axis=-1, keepdims=True))).astype(BF16) for p, dp in zip(probs, dps)]
        for p, hd, vc in zip(probs, heads, vcols):
            dkv_ref[:, vc] += _dot_tn(p.astype(BF16), do[:, hd])
        dqs = [_dot(ds, kv_ref[:, kc]) * (HEAD_DIM ** -0.5) for ds, kc in zip(dss, kcols)]
        for ds, hd, kc in zip(dss, heads, kcols):
            dkv_ref[:, kc] += _dot_tn(ds, q[:, hd])
        dq = jnp.concatenate(dqs, axis=-1).astype(BF16)
        dq_ref[...] = dq
        dh2 = _dot_nt(dq, wq_v[...])
        xh, r = _rms_fwd(x1_ref[...])
        dg_ref[...] += _colsum(dh2 * xh)
        dx1 = dx2v + _rms_bwd(dh2, xh, r, g_ref[...])
        dx1_ref[...] = dx1
        dx1b_ref[...] = dx1.astype(BF16)

    row = lambda w: pl.BlockSpec((tm, w), lambda i: (i, 0))
    per_b = pl.BlockSpec((N_MEM, 2 * D_MODEL), lambda i: (i // tps, 0))
    return pl.pallas_call(
        body, name="bwd_attn", grid=(n_tiles,),
        in_specs=[row(D_MODEL), row(D_MODEL), row(D_MODEL), per_b, _full((1, D_MODEL)), pl.BlockSpec(memory_space=pl.ANY),
                  _full(after.shape)],
        out_specs=[row(D_MODEL), row(D_MODEL), row(D_MODEL), per_b, _full((1, D_MODEL))],
        out_shape=[jax.ShapeDtypeStruct((tokens, D_MODEL), F32), jax.ShapeDtypeStruct((tokens, D_MODEL), BF16),
                   jax.ShapeDtypeStruct((tokens, D_MODEL), BF16),
                   jax.ShapeDtypeStruct((n_b * N_MEM, 2 * D_MODEL), F32), jax.ShapeDtypeStruct((1, D_MODEL), F32)],
        scratch_shapes=[pltpu.VMEM((D_MODEL, D_MODEL), BF16), pltpu.VMEM((D_MODEL, D_MODEL), BF16), pltpu.SemaphoreType.DMA((2,))],
        compiler_params=_params(),
    )(dx2, x1, q, kv, g_x, gw, after)


def _bwd_kv(dkv, mem2d, gw):
    rows = mem2d.shape[0]
    n_b = rows // N_MEM

    def body(dkv_ref, mem_ref, gw_hbm, dkvb_ref, dg_ref, wkv_v, sem):
        @pl.when(pl.program_id(0) == 0)
        def _():
            copies = _load_weight(gw_hbm, "w_kv", wkv_v, sem)
            for cp in copies:
                cp.start()
            for cp in copies:
                cp.wait()
            dg_ref[...] = jnp.zeros_like(dg_ref)

        dkvb = dkv_ref[...].astype(BF16)
        dkvb_ref[...] = dkvb
        dmn = _dot(dkvb, wkv_v[...])
        mh, _ = _rms_fwd(mem_ref[...])
        dg_ref[...] += _colsum(dmn * mh)

    return pl.pallas_call(
        body, name="bwd_kv", grid=(n_b,),
        in_specs=[pl.BlockSpec((N_MEM, 2 * D_MODEL), lambda b: (b, 0)), pl.BlockSpec((N_MEM, D_MODEL), lambda b: (b, 0)),
                  pl.BlockSpec(memory_space=pl.ANY)],
        out_specs=[pl.BlockSpec((N_MEM, 2 * D_MODEL), lambda b: (b, 0)), _full((1, D_MODEL))],
        out_shape=[jax.ShapeDtypeStruct((rows, 2 * D_MODEL), BF16), jax.ShapeDtypeStruct((1, D_MODEL), F32)],
        scratch_shapes=[pltpu.VMEM((2 * D_MODEL, D_MODEL), BF16), pltpu.SemaphoreType.DMA],
        compiler_params=_params(),
    )(dkv, mem2d, gw)


def _bwd_mix(dx1, x2d, u_all, c_all, pooled_all, gw, g_mix, conv_w, ln_g, ln_b, pool_w, pool_scale, after, seq, tm):
    tokens = x2d.shape[0]
    n_tiles = tokens // tm
    tps = seq // tm

    def body(dx1_ref, x_ref, u_ref, c_ref, pooled_ref, gmix_ref, gw_hbm, cw_ref, lng_ref, lnb_ref, pw_ref, ps_ref,
             after_ref, dx_ref, du_ref, dgmix_ref, dcw_ref, dcb_ref, dlng_ref, dlnb_ref, dpw_ref, dps_ref,
             win_v, wout_v, dc_carry, e_carry, sem):
        del after_ref
        i = pl.program_id(0)
        t = n_tiles - 1 - i

        _start_weights(gw_hbm, ("w_out", "w_in"), (wout_v, win_v), sem)

        @pl.when(i == 0)
        def _():
            for ref in (dgmix_ref, dcw_ref, dcb_ref, dlng_ref, dlnb_ref, dpw_ref, dps_ref):
                ref[...] = jnp.zeros_like(ref)

        @pl.when(t % tps == tps - 1)
        def _():
            dc_carry[...] = jnp.zeros_like(dc_carry)
            e_carry[...] = jnp.zeros_like(e_carry)

        dx1v = dx1_ref[...]
        dymix = _dot_nt(dx1v.astype(BF16), wout_v[...])
        dyc, dyp = dymix[:, :D_CONV], dymix[:, D_CONV:]
        u = u_ref[...]
        val, gate = u[:, :D_CONV], u[:, D_CONV:2 * D_CONV]

        conv = c_ref[...]
        mu = jnp.mean(conv, axis=-1, keepdims=True)
        cen = conv - mu
        rs = lax.rsqrt(jnp.mean(cen * cen, axis=-1, keepdims=True) + EPS)
        chat = cen * rs
        ln = chat * lng_ref[...] + lnb_ref[...]
        sl = _sigmoid(ln)
        dln = dyc * (sl * (1.0 + ln * (1.0 - sl)))
        dlng_ref[...] += _colsum(dln * chat)
        dlnb_ref[...] += _colsum(dln)
        dchat = dln * lng_ref[...]
        dc = rs * (dchat - jnp.mean(dchat, axis=-1, keepdims=True)
                   - chat * jnp.mean(dchat * chat, axis=-1, keepdims=True))
        dcb_ref[...] += _colsum(dc)
        sg = _sigmoid(gate)
        hc = val * sg
        ext = jnp.concatenate([dc, dc_carry[...]], axis=0)
        dc_carry[...] = dc[:CONV_HALO, :]
        dhc = jnp.zeros((tm, D_CONV), F32)
        ahead_by = _sublane_shifts(ext)
        for k in range(CONV_WIDTH):
            whole, part = divmod(CONV_WIDTH - 1 - k, 8)
            tap = ahead_by[part][8 * whole:8 * whole + tm, :]
            dhc = dhc + cw_ref[k:k + 1, :] * tap
            dcw_ref[k:k + 1, :] += _colsum_mxu(hc * tap)
        du_ref[:, :D_CONV] = (dhc * sg).astype(BF16)
        du_ref[:, D_CONV:2 * D_CONV] = (dhc * val * (sg * (1.0 - sg))).astype(BF16)

        pos = lax.broadcasted_iota(jnp.int32, (tm, 1), 0) + (t % tps) * tm
        es, dpooled = [], []
        for g, w in enumerate(POOL_WINDOWS):
            cols = pl.ds(g * POOL_GROUP_DIM, POOL_GROUP_DIM)
            lo = g * POOL_GROUP_DIM
            pooled = pooled_ref[:, cols]
            pw = pw_ref[g].astype(BF16)
            dyg = dyp[:, lo:lo + POOL_GROUP_DIM]
            dps_ref[:, cols] += _colsum(dyg * _dot(pooled, pw))
            dmixed = (dyg * ps_ref[:, cols]).astype(BF16)
            dpw_ref[g] += _dot_tn(pooled, dmixed)
            dpo = _dot_nt(dmixed, pw)
            dpooled.append(dpo)
            es.append(dpo / jnp.minimum(pos + 1, w).astype(F32))
        e = jnp.concatenate(es, axis=-1)
        run = jnp.concatenate([e, e_carry[...]], axis=0)
        e_carry[...] = e[:POOL_HALO, :]
        rows = tm + POOL_HALO
        for g, w in enumerate(POOL_WINDOWS):
            lo = g * POOL_GROUP_DIM
            run = run[:, POOL_GROUP_DIM if g else 0:]
            run = run + pltpu.roll(run, rows - w // 2, 0)
            du_ref[:, 2 * D_CONV + lo:2 * D_CONV + lo + POOL_GROUP_DIM] = (
                run[:tm, :POOL_GROUP_DIM] - dpooled[g]).astype(BF16)

        dh1 = _dot(du_ref[...], win_v[...])
        xh, r = _rms_fwd(x_ref[...])
        dgmix_ref[...] += _colsum(dh1 * xh)
        dx_ref[...] = dx1v + _rms_bwd(dh1, xh, r, gmix_ref[...])

    rev = lambda w: pl.BlockSpec((tm, w), lambda i: (n_tiles - 1 - i, 0))
    return pl.pallas_call(
        body, name="bwd_mix", grid=(n_tiles,),
        in_specs=[rev(D_MODEL), rev(D_MODEL), rev(D_IN), rev(D_CONV), rev(D_POOL), _full((1, D_MODEL)),
                  pl.BlockSpec(memory_space=pl.ANY), _full((CONV_WIDTH, D_CONV)), _full((1, D_CONV)), _full((1, D_CONV)),
                  _full((4, POOL_GROUP_DIM, POOL_GROUP_DIM)), _full((1, D_POOL)), _full(after.shape)],
        out_specs=[rev(D_MODEL), rev(D_IN), _full((1, D_MODEL)), _full((CONV_WIDTH, D_CONV)), _full((1, D_CONV)),
                   _full((1, D_CONV)), _full((1, D_CONV)), _full((4, POOL_GROUP_DIM, POOL_GROUP_DIM)), _full((1, D_POOL))],
        out_shape=[jax.ShapeDtypeStruct((tokens, D_MODEL), F32), jax.ShapeDtypeStruct((tokens, D_IN), BF16),
                   jax.ShapeDtypeStruct((1, D_MODEL), F32), jax.ShapeDtypeStruct((CONV_WIDTH, D_CONV), F32),
                   jax.ShapeDtypeStruct((1, D_CONV), F32), jax.ShapeDtypeStruct((1, D_CONV), F32),
                   jax.ShapeDtypeStruct((1, D_CONV), F32),
                   jax.ShapeDtypeStruct((4, POOL_GROUP_DIM, POOL_GROUP_DIM), F32), jax.ShapeDtypeStruct((1, D_POOL), F32)],
        scratch_shapes=[pltpu.VMEM((D_IN, D_MODEL), BF16), pltpu.VMEM((D_MODEL, D_MODEL), BF16),
                        pltpu.VMEM((CONV_HALO, D_CONV), F32), pltpu.VMEM((POOL_HALO, D_POOL), F32),
                        pltpu.SemaphoreType.DMA((2,))],
        compiler_params=_params(),
    )(dx1, x2d, u_all, c_all, pooled_all, g_mix, gw, conv_w, ln_g, ln_b, pool_w, pool_scale, after)


def _wgrad(a, b, name, after=None):
    tokens, m = a.shape
    n = b.shape[1]
    tm = 512 if m % 512 == 0 else 256
    extra = [] if after is None else [after]

    if m * n * 4 <= WGRAD_RESIDENT_BYTES:
        tk = min(WGRAD_TOKEN_BLOCK, tokens)
        n_k = tokens // tk

        def walk(a_ref, b_ref, *rest):
            out_ref, acc = rest[-2:]
            k = pl.program_id(0)

            @pl.when(k == 0)
            def _():
                acc[...] = jnp.zeros_like(acc)

            acc[...] += _dot_tn(a_ref[...], b_ref[...])

            @pl.when(k == n_k - 1)
            def _():
                out_ref[...] = acc[...].astype(out_ref.dtype)

        return pl.pallas_call(
            walk, name=name, grid=(n_k,),
            in_specs=[pl.BlockSpec((tk, m), lambda k: (k, 0)), pl.BlockSpec((tk, n), lambda k: (k, 0))] + [
                _full(t.shape) for t in extra],
            out_specs=_full((m, n)),
            out_shape=jax.ShapeDtypeStruct((m, n), BF16),
            scratch_shapes=[pltpu.VMEM((m, n), F32)],
            compiler_params=_params(),
        )(a, b, *extra)

    def body(a_ref, b_ref, *rest):
        rest[-1][...] = _dot_tn(a_ref[...], b_ref[...]).astype(rest[-1].dtype)

    return pl.pallas_call(
        body, name=name, grid=(m // tm,),
        in_specs=[pl.BlockSpec((tokens, tm), lambda i: (0, i)), _full((tokens, n))] + [_full(t.shape) for t in extra],
        out_specs=pl.BlockSpec((tm, n), lambda i: (i, 0)),
        out_shape=jax.ShapeDtypeStruct((m, n), BF16),
        compiler_params=_params(),
    )(a, b, *extra)


def _adamw_update(w, g, m, v):
    nm = ADAM_B1 * m + (1.0 - ADAM_B1) * g
    nv = ADAM_B2 * v + (1.0 - ADAM_B2) * (g * g)
    m_hat = nm / (1.0 - ADAM_B1 ** ADAM_STEP)
    v_hat = nv / (1.0 - ADAM_B2 ** ADAM_STEP)
    return -ADAM_LR * (m_hat / (jnp.sqrt(v_hat) + ADAM_EPS) + ADAM_WD * w), nm, nv


def _adamw_small(ws, gs, ms, vs):
    n = len(ws)

    def body(*refs):
        ins, outs = refs[:4 * n], refs[4 * n:]
        for k in range(n):
            d, nm, nv = _adamw_update(*[ins[j * n + k][...] for j in range(4)])
            outs[k][...] = d
            outs[n + k][...] = nm
            outs[2 * n + k][...] = nv

    vmem = pl.BlockSpec(memory_space=pltpu.VMEM)
    outs = pl.pallas_call(
        body, name="adamw_small",
        in_specs=[vmem] * (4 * n), out_specs=[vmem] * (3 * n),
        out_shape=[jax.ShapeDtypeStruct(w.shape, F32) for w in ws] * 3,
    )(*ws, *gs, *ms, *vs)
    return outs[:n], outs[n:2 * n], outs[2 * n:]


SMALL = (("norm_mix_g", (1, 1024)), ("conv_dw_b", (1, 512)), ("conv_ln_g", (1, 512)), ("conv_ln_b", (1, 512)),
         ("pool_w", (1, 4, 128, 128)), ("pool_scale", (1, 512)), ("norm_xattn_g", (1, 1024)), ("norm_mem_g", (1, 1024)),
         ("norm_ffn_g", (1, 1024)), ("ffn_dw_b", (1, 5632)), ("norm_final_g", (1024,)))
LANES = 128


def _pack_rows(arrs):
    flat = jnp.concatenate([a.reshape(-1) for a in arrs])
    pad = (-flat.shape[0]) % (8 * LANES)
    return jnp.pad(flat, (0, pad)).reshape(-1, LANES)


def kernel(x, mem, norm_mix_g, w_in, conv_dw_w, conv_dw_b, conv_ln_g, conv_ln_b, pool_w, pool_scale, w_out, norm_xattn_g, norm_mem_g, w_q, w_kv, w_o, norm_ffn_g, w_up, ffn_dw_w, ffn_dw_b, w_down, norm_final_g, loss_target, m_norm_mix_g, m_w_in, m_conv_dw_w, m_conv_dw_b, m_conv_ln_g, m_conv_ln_b, m_pool_w, m_pool_scale, m_w_out, m_norm_xattn_g, m_norm_mem_g, m_w_q, m_w_kv, m_w_o, m_norm_ffn_g, m_w_up, m_ffn_dw_w, m_ffn_dw_b, m_w_down, m_norm_final_g, v_norm_mix_g, v_w_in, v_conv_dw_w, v_conv_dw_b, v_conv_ln_g, v_conv_ln_b, v_pool_w, v_pool_scale, v_w_out, v_norm_xattn_g, v_norm_mem_g, v_w_q, v_w_kv, v_w_o, v_norm_ffn_g, v_w_up, v_ffn_dw_w, v_ffn_dw_b, v_w_down, v_norm_final_g):
    weights = dict(norm_mix_g=norm_mix_g, w_in=w_in, conv_dw_w=conv_dw_w, conv_dw_b=conv_dw_b, conv_ln_g=conv_ln_g,
                   conv_ln_b=conv_ln_b, pool_w=pool_w, pool_scale=pool_scale, w_out=w_out, norm_xattn_g=norm_xattn_g,
                   norm_mem_g=norm_mem_g, w_q=w_q, w_kv=w_kv, w_o=w_o, norm_ffn_g=norm_ffn_g, w_up=w_up,
                   ffn_dw_w=ffn_dw_w, ffn_dw_b=ffn_dw_b, w_down=w_down, norm_final_g=norm_final_g)
    moments_m = dict(norm_mix_g=m_norm_mix_g, w_in=m_w_in, conv_dw_w=m_conv_dw_w, conv_dw_b=m_conv_dw_b,
                     conv_ln_g=m_conv_ln_g, conv_ln_b=m_conv_ln_b, pool_w=m_pool_w, pool_scale=m_pool_scale,
                     w_out=m_w_out, norm_xattn_g=m_norm_xattn_g, norm_mem_g=m_norm_mem_g, w_q=m_w_q, w_kv=m_w_kv,
                     w_o=m_w_o, norm_ffn_g=m_norm_ffn_g, w_up=m_w_up, ffn_dw_w=m_ffn_dw_w, ffn_dw_b=m_ffn_dw_b,
                     w_down=m_w_down, norm_final_g=m_norm_final_g)
    moments_v = dict(norm_mix_g=v_norm_mix_g, w_in=v_w_in, conv_dw_w=v_conv_dw_w, conv_dw_b=v_conv_dw_b,
                     conv_ln_g=v_conv_ln_g, conv_ln_b=v_conv_ln_b, pool_w=v_pool_w, pool_scale=v_pool_scale,
                     w_out=v_w_out, norm_xattn_g=v_norm_xattn_g, norm_mem_g=v_norm_mem_g, w_q=v_w_q, w_kv=v_w_kv,
                     w_o=v_w_o, norm_ffn_g=v_norm_ffn_g, w_up=v_w_up, ffn_dw_w=v_ffn_dw_w, ffn_dw_b=v_ffn_dw_b,
                     w_down=v_w_down, norm_final_g=v_norm_final_g)
    order = list(weights)
    transposed = ("w_in", "w_kv", "w_up")

    n_b, seq, _ = x.shape
    tokens = n_b * seq
    tm_mix = min(512, seq // 2)
    tm_attn = min(1024, seq // 2)
    tm_ffn = min(256, seq // 2)
    dev = 4 * lax.axis_index("x") + 2 * lax.axis_index("y") + lax.axis_index("c")

    packs = [jnp.concatenate([weights[n][0].T if n in transposed else weights[n][0] for n in names], axis=0).astype(BF16)
             for names in AG_GROUPS]
    small_sharded = _pack_rows([conv_dw_w[0], ffn_dw_w[0]])
    n_small = small_sharded.size * 2 // D_MODEL
    bits = lax.bitcast_convert_type(small_sharded, jnp.uint32)
    halves = [lax.bitcast_convert_type(h.astype(jnp.uint16), BF16).reshape(n_small // 2, D_MODEL)
              for h in (bits >> 16, bits & 0xFFFF)]
    small_bits = jnp.concatenate(halves, axis=0)
    n_mix = packs[0].shape[0]
    packs[0] = jnp.concatenate([packs[0], small_bits, jnp.zeros((BF16_TILE_ROWS - n_small, D_MODEL), BF16)], axis=0)
    flights = []
    after = small_sharded
    for k in range(len(AG_GROUPS)):
        own_in_place = lax.dynamic_update_slice(lax.empty((N_DEV,) + packs[k].shape, BF16), packs[k][None], (dev, 0, 0))
        flights.append(_gather_start(own_in_place, after, "weights_gather_start_%d" % k, BARRIER_IDS["gather_start"][k]))
        after = flights[-1][3]

    def gather_finish(flight, after, tag):
        fwd_send, fwd_recv, buf = _gather_forward(*flight[:3], after, "weights_gather_forward_" + tag,
                                                  BARRIER_IDS["gather_forward"][int(tag)])
        return _gather_finish(fwd_send, fwd_recv, buf, "weights_gather_finish_" + tag)

    gw_mix = gather_finish(flights[0], after, "0")
    high, low = [lax.bitcast_convert_type(gw_mix[:, r:r + n_small // 2, :].reshape((N_DEV,) + small_sharded.shape),
                                          jnp.uint16).astype(jnp.uint32) for r in (n_mix, n_mix + n_small // 2)]
    gsmall = lax.bitcast_convert_type((high << 16) | low, F32)
    gflat = gsmall.reshape(N_DEV, -1)
    n_cw = CONV_WIDTH * (D_CONV // N_DEV)
    n_fw = FFN_CONV_WIDTH * (2 * D_FF // N_DEV)
    conv_w = gflat[:, :n_cw].reshape(N_DEV, CONV_WIDTH, D_CONV // N_DEV).transpose(1, 0, 2).reshape(CONV_WIDTH, D_CONV)
    ffn_w = gflat[:, n_cw:n_cw + n_fw].reshape(N_DEV, FFN_CONV_WIDTH, 2 * D_FF // N_DEV).transpose(1, 0, 2).reshape(
        FFN_CONV_WIDTH, 2 * D_FF)

    x2d = x.reshape(tokens, D_MODEL)
    mem2d = mem.reshape(n_b * N_MEM, D_MODEL)
    tgt2d = loss_target.reshape(tokens, D_MODEL)
    g_final = norm_final_g.reshape(1, D_MODEL)

    x1, u_all, c_all, pooled_all, ymix, h1 = _fwd_mix(
        x2d, gw_mix, norm_mix_g, conv_w, conv_dw_b, conv_ln_g, conv_ln_b, pool_w[0], pool_scale, flights[2][3],
        seq, tm_mix)
    gw_attn = gather_finish(flights[1], x1, "1")
    mem_n, kv = _fwd_kv(mem2d, gw_attn, norm_mem_g)
    x2, h2, q, o = _fwd_attn(x1, kv, gw_attn, norm_xattn_g, seq, tm_attn)
    gw_ffn = gather_finish(flights[2], x2, "2")
    uu_all, cc_all, a_all, h3, dx3, dx3b, loss_part, dg_final = _fwd_ffn(
        x2, tgt2d, gw_ffn, norm_ffn_g, ffn_w, ffn_dw_b, g_final, seq, tm_ffn)

    table = _owner_table()

    def sibling_start(names, tag):
        parts = [part[n].reshape(N_DEV, W_OFF[n][1], D_MODEL) for n in names]
        return _exchange_start(parts, 4, _to_sibling, "rs_sibling_exchange_start_" + tag, BARRIER_IDS["sibling"][tag])

    def chips_start(flight, after, tag):
        parts, landed = _exchange_wait(*flight[:4], after, 4, _to_sibling, "rs_sibling_exchange_wait_" + tag)
        sums = _chip_partial_sums(table, parts, landed, "rs_chip_partial_sums_" + tag)
        return parts, landed, _exchange_start(sums, 3, _to_chip, "rs_chip_exchange_start_" + tag,
                                              BARRIER_IDS["chips"][tag])

    grads, delta, new_m, new_v = {}, {}, {}, {}

    def reduce_finish(names, parts, landed, flight, after, tag):
        _, from_chips = _exchange_wait(*flight[:4], after, 3, _to_chip, "rs_chip_exchange_wait_" + tag)
        as_rows = {n: n in transposed and W_OFF[n][1] % LANES != 0 for n in names}
        states = [tuple(t[n][0].T if as_rows[n] else t[n][0] for t in (weights, moments_m, moments_v)) for n in names]
        results = _final_update(table, parts, landed, from_chips, states, "rs_final_update_" + tag)
        for n, res in zip(names, results):
            grads[n], delta[n], new_m[n], new_v[n] = [t.T[None] if as_rows[n] else t[None] for t in res]
        return delta[names[-1]]

    part = {}
    dx2, dx2b, duu, d_ffn_b, d_ffn_w, dg_ffn = _bwd_ffn(dx3, x2, uu_all, cc_all, gw_ffn, norm_ffn_g, ffn_w, seq, tm_ffn)
    part["w_up"] = _wgrad(duu, h3, "wgrad_w_up")
    part["w_down"] = _wgrad(a_all, dx3b, "wgrad_w_down")
    to_sibling_a = sibling_start(RS_GROUPS["a"], "a")
    dx1, dx1b, dq, dkv, dg_x = _bwd_attn(dx2, x1, q, kv, gw_attn, norm_xattn_g, to_sibling_a[4], seq, tm_mix)
    parts_a, landed_a, flight_a = chips_start(to_sibling_a, dx1, "a")
    dkv_b, dg_mem = _bwd_kv(dkv, mem2d, gw_attn)
    part["w_q"] = _wgrad(h2, dq, "wgrad_w_q", after=flight_a[4])
    part["w_kv"] = _wgrad(dkv_b, mem_n, "wgrad_w_kv", after=flight_a[4])
    part["w_out"] = _wgrad(ymix, dx1b, "wgrad_w_out", after=flight_a[4])
    to_sibling_b = sibling_start(RS_GROUPS["b"], "b")
    part["w_o"] = _wgrad(o, dx2b, "wgrad_w_o", after=to_sibling_b[4])
    parts_b, landed_b, flight_b = chips_start(to_sibling_b, part["w_o"], "b")
    dx, du, dg_mix, d_conv_w, d_conv_b, d_ln_g, d_ln_b, d_pool_w, d_pool_scale = _bwd_mix(
        dx1, x2d, u_all, c_all, pooled_all, gw_mix, norm_mix_g, conv_w, conv_ln_g, conv_ln_b, pool_w[0], pool_scale,
        flight_b[4], seq, tm_mix)
    grad_x = dx.reshape(x.shape)

    small_grads = dict(norm_mix_g=dg_mix, conv_dw_b=d_conv_b, conv_ln_g=d_ln_g, conv_ln_b=d_ln_b, pool_w=d_pool_w,
                       pool_scale=d_pool_scale, norm_xattn_g=dg_x, norm_mem_g=dg_mem, norm_ffn_g=dg_ffn,
                       ffn_dw_b=d_ffn_b, norm_final_g=dg_final)
    small_list = [small_grads[n] for n, _ in SMALL] + [d_conv_w, d_ffn_w, loss_part[:1]]
    small_mine = _pack_rows(small_list)
    small_flight = _broadcast_start(
        lax.dynamic_update_slice(lax.empty((N_DEV,) + small_mine.shape, F32), small_mine[None], (dev, 0, 0)),
        "small_grads_broadcast_start", BARRIER_IDS["broadcast"])

    part["w_in"] = _wgrad(du, h1, "wgrad_w_in", after=small_flight[3])
    to_sibling_c = sibling_start(RS_GROUPS["c"], "c")
    updated_b = reduce_finish(RS_GROUPS["b"], parts_b, landed_b, flight_b, to_sibling_c[4], "b")
    parts_c, landed_c, flight_c = chips_start(to_sibling_c, updated_b, "c")
    updated_a = reduce_finish(RS_GROUPS["a"], parts_a, landed_a, flight_a, flight_c[4], "a")
    small_all = _broadcast_wait(*small_flight[:3], updated_a, "small_grads_broadcast_wait")
    small_sum = _sum_blocks(small_all).reshape(-1)

    pos = 0
    for n, shape in SMALL:
        size = 1
        for s in shape:
            size *= s
        grads[n] = small_sum[pos:pos + size].reshape(shape)
        pos += size
    full_conv_w = small_sum[pos:pos + CONV_WIDTH * D_CONV].reshape(CONV_WIDTH, D_CONV)
    pos += CONV_WIDTH * D_CONV
    full_ffn_w = small_sum[pos:pos + FFN_CONV_WIDTH * 2 * D_FF].reshape(FFN_CONV_WIDTH, 2 * D_FF)
    loss = small_sum[pos + FFN_CONV_WIDTH * 2 * D_FF]
    grads["conv_dw_w"] = lax.dynamic_slice_in_dim(full_conv_w, dev * (D_CONV // N_DEV), D_CONV // N_DEV, axis=1)[None]
    grads["ffn_dw_w"] = lax.dynamic_slice_in_dim(full_ffn_w, dev * (2 * D_FF // N_DEV), 2 * D_FF // N_DEV, axis=1)[None]

    small_names = [n for n in order if n not in W_OFF]
    swap = lambda t: jnp.transpose(t, (1, 0, 2))
    two_d = lambda t: t.reshape(1, -1) if t.ndim == 1 else (swap(t) if t.ndim == 3 else t)
    outs = _adamw_small(*[[two_d(t[n]) for n in small_names] for t in (weights, grads, moments_m, moments_v)])
    for res, out in zip((delta, new_m, new_v), outs):
        for n, o in zip(small_names, out):
            res[n] = swap(o) if o.ndim == 3 else o.reshape(weights[n].shape)

    reduce_finish(RS_GROUPS["c"], parts_c, landed_c, flight_c, delta[small_names[-1]], "c")

    return (loss, grad_x, *[grads[n] for n in order], *[delta[n] for n in order],
            *[new_m[n] for n in order], *[new_v[n] for n in order])
```

```python
import jax
import jax.numpy as jnp
from jax import lax
from jax.experimental import pallas as pl
from jax.experimental.pallas import tpu as pltpu

F32 = jnp.float32
BF16 = jnp.bfloat16
MESH = pl.DeviceIdType.MESH

N_DEV = 8
D_MODEL = 1024
D_CONV = 512
D_POOL = 512
CONV_WIDTH = 31
POOL_WINDOWS = (2, 4, 8, 16)
POOL_GROUP_DIM = 128
D_IN = 1536
N_MEM = 256
HEADS = 4
HEAD_DIM = 256
D_FF = 2816
FFN_CONV_WIDTH = 3
EPS = 1e-6
ADAM_LR = 0.001
ADAM_B1 = 0.9
ADAM_B2 = 0.999
ADAM_EPS = 1e-08
ADAM_WD = 0.01
ADAM_STEP = 10

VMEM_LIMIT_V7X = 56 * 1024 * 1024
CONV_HALO = 32
POOL_HALO = 16
FFN_HALO = 8
FFN_CHUNK = 2816
WGRAD_RESIDENT_BYTES = 8 * 1024 * 1024
WGRAD_TOKEN_BLOCK = 1024
BF16_TILE_ROWS = 16

W_ROWS = (("w_in", 192), ("w_out", 128), ("w_q", 128), ("w_kv", 256), ("w_o", 128), ("w_up", 704), ("w_down", 352))
AG_GROUPS = (("w_in", "w_out"), ("w_q", "w_kv", "w_o"), ("w_up", "w_down"))
W_OFF = {}
for _names in AG_GROUPS:
    _o = 0
    for _n in _names:
        W_OFF[_n] = (_o, dict(W_ROWS)[_n])
        _o += dict(W_ROWS)[_n]
RS_GROUPS = {"a": ("w_up", "w_down"), "b": ("w_q", "w_kv", "w_out"), "c": ("w_o", "w_in")}
BARRIER_IDS = {"gather_start": (None, 0, 1), "gather_forward": (11, 2, 3), "sibling": {"a": 4, "b": 5, "c": 6},
               "chips": {"a": 7, "b": 8, "c": 9}, "broadcast": 10}


def _dot(a, b):
    return jnp.dot(a, b, preferred_element_type=F32)


def _dot_nt(a, b):
    return lax.dot_general(a, b, (((1,), (1,)), ((), ())), preferred_element_type=F32)


def _dot_tn(a, b):
    return lax.dot_general(a, b, (((0,), (0,)), ((), ())), preferred_element_type=F32)


def _sigmoid(v):
    return 1.0 / (1.0 + jnp.exp(-v))


def _rms_fwd(v):
    r = lax.rsqrt(jnp.mean(v * v, axis=-1, keepdims=True) + EPS)
    return v * r, r


def _rms_bwd(dh, vh, r, g):
    gd = dh * g
    return r * (gd - vh * jnp.mean(gd * vh, axis=-1, keepdims=True))


def _sublane_shifts(v):
    rows = v.shape[0]
    return [v] + [pltpu.roll(v, rows - b, 0) for b in range(1, 8)]


def _colsum(v):
    return jnp.sum(v, axis=0, keepdims=True)


def _colsum_mxu(v):
    return _dot(jnp.ones((8, v.shape[0]), BF16), v.astype(BF16))[0:1, :]


def _full(shape):
    return pl.BlockSpec(shape, lambda *_: (0,) * len(shape))


def _params(sem=("arbitrary",), vmem=VMEM_LIMIT_V7X):
    return pltpu.CompilerParams(dimension_semantics=sem, vmem_limit_bytes=vmem)


def _load_weight(g_hbm, name, dst, sem):
    off, rows = W_OFF[name]
    return [pltpu.make_async_copy(g_hbm.at[d, pl.ds(off, rows), :], dst.at[pl.ds(d * rows, rows), :], sem)
            for d in range(N_DEV)]


def _start_weights(g_hbm, names, dsts, sems):
    @pl.when(pl.program_id(0) == 0)
    def _():
        copies = [_load_weight(g_hbm, name, dst, sems.at[k]) for k, (name, dst) in enumerate(zip(names, dsts))]
        for cp in sum(copies, []):
            cp.start()
        for cp in sum(copies, []):
            cp.wait()


def _position():
    x, y, c = lax.axis_index("x"), lax.axis_index("y"), lax.axis_index("c")
    chips = [(1 - x, y), (x, 1 - y), (1 - x, 1 - y)]
    return x, y, c, chips


def _dev(px, py, pc):
    return 4 * px + 2 * py + pc


_HBM =pl.BlockSpec(memory_space=pltpu.HBM)
_SEM = pl.BlockSpec(memory_space=pltpu.SEMAPHORE)
_SIDE_EFFECT = pltpu.SideEffectType.DATAFLOW_SIDE_EFFECTING


def _handshake(peers):
    barrier = pltpu.get_barrier_semaphore()
    for peer in peers:
        pl.semaphore_signal(barrier, inc=1, device_id=peer, device_id_type=MESH)
    pl.semaphore_wait(barrier, len(peers))


def _gather_start(buf, after, name, collective_id):
    def body(buf_ref, after_ref, send_sems, recv_sems, buf_thru, token):
        del after_ref, buf_thru
        x, y, c, chips = _position()
        rows = buf_ref.at[_dev(x, y, c)]
        targets = [(x, y, 1 - c)] + [(*chip, c) for chip in chips]
        if collective_id is not None:
            _handshake(targets)
        for k, to in enumerate(targets):
            pltpu.make_async_remote_copy(src_ref=rows, dst_ref=rows, send_sem=send_sems.at[k], recv_sem=recv_sems.at[k],
                                         device_id=to, device_id_type=MESH).start()
        token[...] = jnp.zeros_like(token)

    return pl.pallas_call(
        body, name=name,
        out_shape=(pltpu.SemaphoreType.DMA((4,)), pltpu.SemaphoreType.DMA((4,)), pltpu.HBM(buf.shape, buf.dtype),
                   jax.ShapeDtypeStruct((8, 128), F32)),
        in_specs=(_HBM, pl.BlockSpec(memory_space=pl.ANY)),
        out_specs=(_SEM, _SEM, _HBM, pl.BlockSpec(memory_space=pltpu.VMEM)),
        input_output_aliases={0: 2},
        compiler_params=pltpu.CompilerParams(has_side_effects=_SIDE_EFFECT, collective_id=collective_id),
    )(pltpu.with_memory_space_constraint(buf, pltpu.HBM), after)


def _gather_forward(send_sems, recv_sems, buf, after, name, collective_id):
    def body(buf_ref, send_sems, recv_sems, after_ref, fwd_send, fwd_recv, buf_thru):
        del after_ref, buf_thru
        x, y, c, chips = _position()
        sibling = (x, y, 1 - c)

        def copy(block, k, sends, recvs):
            rows = buf_ref.at[_dev(*block)]
            return pltpu.make_async_remote_copy(src_ref=rows, dst_ref=rows, send_sem=sends.at[k], recv_sem=recvs.at[k],
                                                device_id=sibling, device_id_type=MESH)

        _handshake([sibling])
        for k in range(4):
            copy((x, y, c), k, send_sems, recv_sems).wait_send()
        copy(sibling, 0, send_sems, recv_sems).wait_recv()
        for j, chip in enumerate(chips):
            copy((*chip, c), 1 + j, send_sems, recv_sems).wait_recv()
            copy((*chip, c), j, fwd_send, fwd_recv).start()

    return pl.pallas_call(
        body, name=name,
        out_shape=(pltpu.SemaphoreType.DMA((3,)), pltpu.SemaphoreType.DMA((3,)), pltpu.HBM(buf.shape, buf.dtype)),
        in_specs=(_HBM, _SEM, _SEM, pl.BlockSpec(memory_space=pl.ANY)), out_specs=(_SEM, _SEM, _HBM),
        input_output_aliases={0: 2},
        compiler_params=pltpu.CompilerParams(has_side_effects=_SIDE_EFFECT, collective_id=collective_id),
    )(buf, send_sems, recv_sems, after)


def _gather_finish(fwd_send, fwd_recv, buf, name):
    def body(buf_ref, fwd_send, fwd_recv, buf_thru):
        del buf_thru
        x, y, c, chips = _position()
        for j, chip in enumerate(chips):
            cp = pltpu.make_async_remote_copy(
                src_ref=buf_ref.at[_dev(*chip, c)], dst_ref=buf_ref.at[_dev(*chip, 1 - c)], send_sem=fwd_send.at[j],
                recv_sem=fwd_recv.at[j], device_id=(x, y, 1 - c), device_id_type=MESH)
            cp.wait_send()
            cp.wait_recv()

    return pl.pallas_call(
        body, name=name,
        out_shape=pltpu.HBM(buf.shape, buf.dtype),
        in_specs=(_HBM, _SEM, _SEM), out_specs=_HBM,
        input_output_aliases={0: 0},
        compiler_params=pltpu.CompilerParams(has_side_effects=_SIDE_EFFECT),
    )(buf, fwd_send, fwd_recv)


def _everyone_else(x, y, c, chips):
    return [(x, y, 1 - c)] + [(*chip, core) for chip in chips for core in (c, 1 - c)]


def _broadcast_start(buf, name, collective_id):
    def body(buf_ref, send_sems, recv_sems, buf_thru, token):
        del buf_thru
        x, y, c, chips = _position()
        rows = buf_ref.at[_dev(x, y, c)]
        _handshake(_everyone_else(x, y, c, chips))
        for k, to in enumerate(_everyone_else(x, y, c, chips)):
            pltpu.make_async_remote_copy(src_ref=rows, dst_ref=rows, send_sem=send_sems.at[k], recv_sem=recv_sems.at[k],
                                         device_id=to, device_id_type=MESH).start()
        token[...] = jnp.zeros_like(token)

    return pl.pallas_call(
        body, name=name,
        out_shape=(pltpu.SemaphoreType.DMA((7,)), pltpu.SemaphoreType.DMA((7,)), pltpu.HBM(buf.shape, buf.dtype),
                   jax.ShapeDtypeStruct((8, 128), F32)),
        in_specs=(_HBM,), out_specs=(_SEM, _SEM, _HBM, pl.BlockSpec(memory_space=pltpu.VMEM)),
        input_output_aliases={0: 2},
        compiler_params=pltpu.CompilerParams(has_side_effects=_SIDE_EFFECT, collective_id=collective_id),
    )(pltpu.with_memory_space_constraint(buf, pltpu.HBM))


def _broadcast_wait(send_sems, recv_sems, buf, after, name):
    def body(buf_ref, send_sems, recv_sems, after_ref, buf_thru):
        del after_ref, buf_thru
        x, y, c, chips = _position()
        for k, peer in enumerate(_everyone_else(x, y, c, chips)):
            cp = pltpu.make_async_remote_copy(
                src_ref=buf_ref.at[_dev(x, y, c)], dst_ref=buf_ref.at[_dev(*peer)], send_sem=send_sems.at[k],
                recv_sem=recv_sems.at[k], device_id=peer, device_id_type=MESH)
            cp.wait_send()
            cp.wait_recv()

    return pl.pallas_call(
        body, name=name,
        out_shape=pltpu.HBM(buf.shape, buf.dtype),
        in_specs=(_HBM, _SEM, _SEM, pl.BlockSpec(memory_space=pl.ANY)), out_specs=_HBM,
        input_output_aliases={0: 0},
        compiler_params=pltpu.CompilerParams(has_side_effects=_SIDE_EFFECT),
    )(buf, send_sems, recv_sems, after)


def _to_sibling(j, x, y, c, chips):
    return _dev(*([(x, y)] + chips)[j], 1 - c), (x, y, 1 - c)


def _to_chip(j, x, y, c, chips):
    return j, (*chips[j], c)


def _exchange_start(srcs, n_slots, route, name, collective_id):
    n = len(srcs)

    def body(*refs):
        s_refs, land_refs = refs[:n], refs[n:2 * n]
        send_sems, recv_sems = refs[2 * n:2 * n + 2]
        token = refs[-1]
        x, y, c, chips = _position()
        _handshake([(x, y, 1 - c)] if route is _to_sibling else [route(j, x, y, c, chips)[1] for j in range(n_slots)])
        for k in range(n):
            for j in range(n_slots):
                block, to = route(j, x, y, c, chips)
                pltpu.make_async_remote_copy(
                    src_ref=s_refs[k].at[block], dst_ref=land_refs[k].at[j], send_sem=send_sems.at[n_slots * k + j],
                    recv_sem=recv_sems.at[n_slots * k + j], device_id=to, device_id_type=MESH).start()
        token[...] = jnp.zeros_like(token)

    lands = [jax.ShapeDtypeStruct((n_slots,) + s.shape[1:], s.dtype) for s in srcs]
    outs = pl.pallas_call(
        body, name=name,
        out_shape=(pltpu.SemaphoreType.DMA((n_slots * n,)), pltpu.SemaphoreType.DMA((n_slots * n,)),
                   *[pltpu.HBM(s.shape, s.dtype) for s in srcs], *[pltpu.HBM(l.shape, l.dtype) for l in lands],
                   jax.ShapeDtypeStruct((8, 128), F32)),
        in_specs=[_HBM] * (2 * n), out_specs=(_SEM, _SEM, *[_HBM] * (2 * n), pl.BlockSpec(memory_space=pltpu.VMEM)),
        input_output_aliases={k: 2 + k for k in range(2 * n)},
        compiler_params=pltpu.CompilerParams(has_side_effects=_SIDE_EFFECT, collective_id=collective_id),
    )(*[pltpu.with_memory_space_constraint(s, pltpu.HBM) for s in srcs],
      *[pltpu.with_memory_space_constraint(lax.empty(l.shape, l.dtype), pltpu.HBM) for l in lands])
    return outs[0], outs[1], outs[2:2 + n], outs[2 + n:2 + 2 * n], outs[-1]


def _exchange_wait(send_sems, recv_sems, s_thru, land_thru, after, n_slots, route, name):
    n = len(s_thru)

    def body(*refs):
        s_refs, land_refs = refs[:n], refs[n:2 * n]
        send_sems, recv_sems = refs[2 * n:2 * n + 2]
        x, y, c, chips = _position()
        for k in range(n):
            for j in range(n_slots):
                block, to = route(j, x, y, c, chips)
                cp = pltpu.make_async_remote_copy(
                    src_ref=s_refs[k].at[block], dst_ref=land_refs[k].at[j], send_sem=send_sems.at[n_slots * k + j],
                    recv_sem=recv_sems.at[n_slots * k + j], device_id=to, device_id_type=MESH)
                cp.wait_send()
                cp.wait_recv()

    outs = pl.pallas_call(
        body, name=name,
        out_shape=(*[pltpu.HBM(s.shape, s.dtype) for s in s_thru], *[pltpu.HBM(l.shape, l.dtype) for l in land_thru]),
        in_specs=[_HBM] * (2 * n) + [_SEM, _SEM, pl.BlockSpec(memory_space=pl.ANY)], out_specs=[_HBM] * (2 * n),
        input_output_aliases={k: k for k in range(2 * n)},
        compiler_params=pltpu.CompilerParams(has_side_effects=_SIDE_EFFECT),
    )(*s_thru, *land_thru, send_sems, recv_sems, after)
    return outs[:n], outs[n:]


def _owner_table():
    x, y, c = lax.axis_index("x"), lax.axis_index("y"), lax.axis_index("c")
    chips = [(x, y), (1 - x, y), (x, 1 - y), (1 - x, 1 - y)]
    return jnp.stack([_dev(px, py, c) for px, py in chips]).astype(jnp.int32)


def _chip_partial_sums(table, parts, from_sibling, name):
    n = len(parts)

    def body(tab_ref, *refs):
        del tab_ref
        for g_ref, l_ref, out_ref in zip(refs[:n], refs[n:2 * n], refs[2 * n:]):
            out_ref[...] = (g_ref[...].astype(F32) + l_ref[...].astype(F32)).astype(out_ref.dtype)

    block = lambda p: (None,) + p.shape[1:]
    grid_spec = pltpu.PrefetchScalarGridSpec(
        num_scalar_prefetch=1, grid=(3,),
        in_specs=[pl.BlockSpec(block(p), lambda j, tab: (tab[j + 1], 0, 0)) for p in parts]
        + [pl.BlockSpec(block(p), lambda j, tab: (j + 1, 0, 0)) for p in parts],
        out_specs=[pl.BlockSpec(block(p), lambda j, tab: (j, 0, 0)) for p in parts])
    return pl.pallas_call(
        body, name=name, grid_spec=grid_spec,
        out_shape=[jax.ShapeDtypeStruct((3,) + p.shape[1:], BF16) for p in parts],
        compiler_params=_params(("arbitrary",)),
    )(table, *parts, *from_sibling)


def _final_update(table, parts, from_sibling, from_chips, states, name):
    n = len(parts)
    flipped = [states[k][0].shape != parts[k].shape[1:] for k in range(n)]

    def body(tab_ref, *refs):
        del tab_ref
        ins, outs = refs[:6 * n], refs[6 * n:]
        for k in range(n):
            acc = ins[k][...].astype(F32) + ins[n + k][...].astype(F32)
            for j in range(3):
                acc = acc + ins[2 * n + k][j].astype(F32)
            if flipped[k]:
                acc = acc.T
            w_ref, m_ref, v_ref = ins[3 * n + 3 * k:3 * n + 3 * k + 3]
            outs[4 * k][...] = acc
            for out_ref, val in zip(outs[4 * k + 1:4 * k + 4], _adamw_update(w_ref[...], acc, m_ref[...], v_ref[...])):
                out_ref[...] = val

    def grad_block(k, lead, at):
        r, c = parts[k].shape[1:]
        if flipped[k]:
            return pl.BlockSpec(lead + (r, c // 2), lambda t, tab: (*at(tab), 0, t))
        return pl.BlockSpec(lead + (r // 2, c), lambda t, tab: (*at(tab), t, 0))

    def state_block(k):
        a, b = states[k][0].shape
        return pl.BlockSpec((a // 2, b), lambda t, tab: (t, 0))

    grid_spec = pltpu.PrefetchScalarGridSpec(
        num_scalar_prefetch=1, grid=(2,),
        in_specs=[grad_block(k, (None,), lambda tab: (tab[0],)) for k in range(n)]
        + [grad_block(k, (None,), lambda tab: (0,)) for k in range(n)]
        + [grad_block(k, (3,), lambda tab: (0,)) for k in range(n)]
        + [state_block(k) for k in range(n) for _ in range(3)],
        out_specs=[state_block(k) for k in range(n) for _ in range(4)])
    outs = pl.pallas_call(
        body, name=name, grid_spec=grid_spec,
        out_shape=[jax.ShapeDtypeStruct(states[k][0].shape, F32) for k in range(n) for _ in range(4)],
        compiler_params=_params(("arbitrary",)),
    )(table, *parts, *from_sibling, *from_chips, *[t for k in range(n) for t in states[k]])
    return [outs[4 * k:4 * k + 4] for k in range(n)]


def _sum_blocks(g8):
    _, rows, cols = g8.shape

    def body(g_ref, out_ref):
        acc = g_ref[0]
        for d in range(1, N_DEV):
            acc = acc + g_ref[d]
        out_ref[...] = acc

    return pl.pallas_call(
        body, name="small_grad_sum", grid=(1,),
        in_specs=[_full((N_DEV, rows, cols))], out_specs=_full((rows, cols)),
        out_shape=jax.ShapeDtypeStruct((rows, cols), F32),
        compiler_params=_params(("arbitrary",)),
    )(g8)


def _fwd_mix(x2d, gw, g_mix, conv_w, conv_b, ln_g, ln_b, pool_w, pool_scale, after, seq, tm):
    tokens = x2d.shape[0]
    n_tiles = tokens // tm
    tps = seq // tm

    def body(x_ref, gmix_ref, gw_hbm, cw_ref, cb_ref, lng_ref, lnb_ref, pw_ref, ps_ref, after_ref,
             x1_ref, u_ref, c_ref, pooled_ref, ymix_ref, h1_ref,
             win_v, wout_v, hc_carry, up_carry, sem):
        del after_ref
        i = pl.program_id(0)

        _start_weights(gw_hbm, ("w_in", "w_out"), (win_v, wout_v), sem)

        @pl.when(i % tps == 0)
        def _():
            hc_carry[...] = jnp.zeros_like(hc_carry)
            up_carry[...] = jnp.zeros_like(up_carry)

        x = x_ref[...]
        xh, _ = _rms_fwd(x)
        h1 = (xh * gmix_ref[...]).astype(BF16)
        h1_ref[...] = h1
        u = _dot_nt(h1, win_v[...])
        u_ref[...] = u
        val, gate, up = u[:, :D_CONV], u[:, D_CONV:2 * D_CONV], u[:, 2 * D_CONV:]

        extp = jnp.concatenate([up_carry[...], up], axis=0)
        up_carry[...] = up[tm - POOL_HALO:, :]
        pos = lax.broadcasted_iota(jnp.int32, (tm, 1), 0) + (i % tps) * tm
        run = extp
        mixed = []
        for g, w in enumerate(POOL_WINDOWS):
            lo = g * POOL_GROUP_DIM
            run = run[:, POOL_GROUP_DIM if g else 0:]
            run = run + pltpu.roll(run, w // 2, 0)
            cnt = jnp.minimum(pos + 1, w).astype(F32)
            pooled = run[POOL_HALO:, :POOL_GROUP_DIM] / cnt - up[:, lo:lo + POOL_GROUP_DIM]
            pooled = pooled.astype(BF16)
            pooled_ref[:, lo:lo + POOL_GROUP_DIM] = pooled
            mixed.append(_dot(pooled, pw_ref[g].astype(BF16)))
        y_pool = jnp.concatenate(mixed, axis=-1) * ps_ref[...]
        y_pool = y_pool.astype(BF16)
        ymix_ref[:, D_CONV:] = y_pool
        out = _dot(y_pool, wout_v[D_CONV:, :])

        hc = val * _sigmoid(gate)
        ext = jnp.concatenate([hc_carry[...], hc], axis=0)
        hc_carry[...] = hc[tm - CONV_HALO:, :]
        conv = jnp.broadcast_to(cb_ref[...], (tm, D_CONV))
        ahead_by = _sublane_shifts(ext)
        for k in range(CONV_WIDTH):
            whole, part = divmod(CONV_HALO - (CONV_WIDTH - 1) + k, 8)
            conv = conv + cw_ref[k:k + 1, :] * ahead_by[part][8 * whole:8 * whole + tm, :]
        c_ref[...] = conv
        mu = jnp.mean(conv, axis=-1, keepdims=True)
        cen = conv - mu
        ln = cen * lax.rsqrt(jnp.mean(cen * cen, axis=-1, keepdims=True) + EPS) * lng_ref[...] + lnb_ref[...]
        y_conv = ln * _sigmoid(ln)
        y_conv = y_conv.astype(BF16)
        ymix_ref[:, :D_CONV] = y_conv
        x1_ref[...] = x + (out + _dot(y_conv, wout_v[:D_CONV, :]))

    row = lambda w: pl.BlockSpec((tm, w), lambda i: (i, 0))
    return pl.pallas_call(
        body, name="fwd_mix", grid=(n_tiles,),
        in_specs=[row(D_MODEL), _full((1, D_MODEL)), pl.BlockSpec(memory_space=pl.ANY),
                  _full((CONV_WIDTH, D_CONV)), _full((1, D_CONV)), _full((1, D_CONV)), _full((1, D_CONV)),
                  _full((4, POOL_GROUP_DIM, POOL_GROUP_DIM)), _full((1, D_POOL)), _full(after.shape)],
        out_specs=[row(D_MODEL), row(D_IN), row(D_CONV), row(D_POOL), row(D_MODEL), row(D_MODEL)],
        out_shape=[jax.ShapeDtypeStruct((tokens, D_MODEL), F32), jax.ShapeDtypeStruct((tokens, D_IN), F32),
                   jax.ShapeDtypeStruct((tokens, D_CONV), F32), jax.ShapeDtypeStruct((tokens, D_POOL), BF16),
                   jax.ShapeDtypeStruct((tokens, D_MODEL), BF16), jax.ShapeDtypeStruct((tokens, D_MODEL), BF16)],
        scratch_shapes=[pltpu.VMEM((D_IN, D_MODEL), BF16), pltpu.VMEM((D_MODEL, D_MODEL), BF16),
                        pltpu.VMEM((CONV_HALO, D_CONV), F32), pltpu.VMEM((POOL_HALO, D_POOL), F32),
                        pltpu.SemaphoreType.DMA((2,))],
        compiler_params=_params(),
    )(x2d, g_mix, gw, conv_w, conv_b, ln_g, ln_b, pool_w, pool_scale, after)


def _fwd_kv(mem2d, gw, g_mem):
    rows = mem2d.shape[0]
    n_b = rows // N_MEM

    def body(mem_ref, g_ref, gw_hbm, mn_ref, kv_ref, wkv_v, sem):
        @pl.when(pl.program_id(0) == 0)
        def _():
            copies = _load_weight(gw_hbm, "w_kv", wkv_v, sem)
            for cp in copies:
                cp.start()
            for cp in copies:
                cp.wait()

        mh, _ = _rms_fwd(mem_ref[...])
        mn = (mh * g_ref[...]).astype(BF16)
        mn_ref[...] = mn
        kv_ref[...] = _dot_nt(mn, wkv_v[...]).astype(BF16)

    return pl.pallas_call(
        body, name="fwd_kv", grid=(n_b,),
        in_specs=[pl.BlockSpec((N_MEM, D_MODEL), lambda b: (b, 0)), _full((1, D_MODEL)), pl.BlockSpec(memory_space=pl.ANY)],
        out_specs=[pl.BlockSpec((N_MEM, D_MODEL), lambda b: (b, 0)), pl.BlockSpec((N_MEM, 2 * D_MODEL), lambda b: (b, 0))],
        out_shape=[jax.ShapeDtypeStruct((rows, D_MODEL), BF16), jax.ShapeDtypeStruct((rows, 2 * D_MODEL), BF16)],
        scratch_shapes=[pltpu.VMEM((2 * D_MODEL, D_MODEL), BF16), pltpu.SemaphoreType.DMA],
        compiler_params=_params(),
    )(mem2d, g_mem, gw)


def _softmax_rows(s):
    e = jnp.exp(s - jnp.max(s, axis=-1, keepdims=True))
    return e / jnp.sum(e, axis=-1, keepdims=True)


def _fwd_attn(x1, kv, gw, g_x, seq, tm):
    tokens = x1.shape[0]
    n_tiles = tokens // tm
    tps = seq // tm

    def body(x1_ref, kv_ref, g_ref, gw_hbm, x2_ref, h2_ref, q_ref, o_ref, wq_v, wo_v, sem):
        _start_weights(gw_hbm, ("w_q", "w_o"), (wq_v, wo_v), sem)
        x1v = x1_ref[...]
        xh, _ = _rms_fwd(x1v)
        h2 = (xh * g_ref[...]).astype(BF16)
        h2_ref[...] = h2
        q = (_dot(h2, wq_v[...]) * (HEAD_DIM ** -0.5)).astype(BF16)
        q_ref[...] = q
        heads = [slice(h * HEAD_DIM, (h + 1) * HEAD_DIM) for h in range(HEADS)]
        scores = [_dot_nt(q[:, hd], kv_ref[:, hd]) for hd in heads]
        probs = [_softmax_rows(s).astype(BF16) for s in scores]
        outs = [_dot(p, kv_ref[:, pl.ds(D_MODEL + h * HEAD_DIM, HEAD_DIM)]) for h, p in enumerate(probs)]
        o = jnp.concatenate(outs, axis=-1).astype(BF16)
        o_ref[...] = o
        x2_ref[...] = x1v + _dot(o, wo_v[...])

    row = lambda w: pl.BlockSpec((tm, w), lambda i: (i, 0))
    return pl.pallas_call(
        body, name="fwd_attn", grid=(n_tiles,),
        in_specs=[row(D_MODEL), pl.BlockSpec((N_MEM, 2 * D_MODEL), lambda i: (i // tps, 0)), _full((1, D_MODEL)),
                  pl.BlockSpec(memory_space=pl.ANY)],
        out_specs=[row(D_MODEL)] * 4,
        out_shape=[jax.ShapeDtypeStruct((tokens, D_MODEL), F32)] + [jax.ShapeDtypeStruct((tokens, D_MODEL), BF16)] * 3,
        scratch_shapes=[pltpu.VMEM((D_MODEL, D_MODEL), BF16), pltpu.VMEM((D_MODEL, D_MODEL), BF16), pltpu.SemaphoreType.DMA((2,))],
        compiler_params=_params(),
    )(x1, kv, g_x, gw)


def _ffn_conv(uu, halo, w_ref, b_ref, cols):
    ext = jnp.concatenate([halo, uu], axis=0)
    p1 = pltpu.roll(ext, 1, 0)[FFN_HALO:, :]
    p2 = pltpu.roll(ext, 2, 0)[FFN_HALO:, :]
    return b_ref[:, cols] + w_ref[2:3, cols] * uu + w_ref[1:2, cols] * p1 + w_ref[0:1, cols] * p2


def _fwd_ffn(x2, target, gw, g_ffn, ffn_w, ffn_b, g_final, seq, tm):
    tokens = x2.shape[0]
    n_tiles = tokens // tm
    tps = seq // tm
    n_chunks = D_FF // FFN_CHUNK

    def body(x2_ref, tgt_ref, gffn_ref, gw_hbm, fw_ref, fb_ref, gfin_ref,
             uu_ref, cc_ref, a_ref, h3_ref, dx3_ref, dx3b_ref, loss_ref, dgfin_ref,
             wup_v, wdown_v, carry, sem):
        i = pl.program_id(0)

        _start_weights(gw_hbm, ("w_up", "w_down"), (wup_v, wdown_v), sem)

        @pl.when(i == 0)
        def _():
            loss_ref[...] = jnp.zeros_like(loss_ref)
            dgfin_ref[...] = jnp.zeros_like(dgfin_ref)

        @pl.when(i % tps == 0)
        def _():
            carry[...] = jnp.zeros_like(carry)

        x2v = x2_ref[...]
        xh, _ = _rms_fwd(x2v)
        h3 = (xh * gffn_ref[...]).astype(BF16)
        h3_ref[...] = h3
        acc = jnp.zeros((tm, D_MODEL), F32)
        for jc in range(n_chunks):
            halves = []
            for half in range(2):
                cols = pl.ds(half * D_FF + jc * FFN_CHUNK, FFN_CHUNK)
                uu = _dot_nt(h3, wup_v[cols, :])
                uu_ref[:, cols] = uu.astype(BF16)
                cc = _ffn_conv(uu, carry[:, cols], fw_ref, fb_ref, cols)
                cc_ref[:, cols] = cc.astype(BF16)
                halves.append(cc)
                carry[:, cols] = uu[tm - FFN_HALO:, :]
            gate, val = halves
            a = (gate * _sigmoid(gate) * val).astype(BF16)
            a_ref[:, pl.ds(jc * FFN_CHUNK, FFN_CHUNK)] = a
            acc = acc + _dot(a, wdown_v[pl.ds(jc * FFN_CHUNK, FFN_CHUNK), :])
        x3 = x2v + acc

        xh3, r3 = _rms_fwd(x3)
        gfin = gfin_ref[...]
        err = xh3 * gfin - tgt_ref[...]
        loss_ref[...] += jnp.full(loss_ref.shape, jnp.sum(err * err) * (0.5 / D_MODEL), F32)
        dy = err * (1.0 / D_MODEL)
        dgfin_ref[...] += _colsum(dy * xh3)
        dx3 = _rms_bwd(dy, xh3, r3, gfin)
        dx3_ref[...] = dx3
        dx3b_ref[...] = dx3.astype(BF16)

    row = lambda w: pl.BlockSpec((tm, w), lambda i: (i, 0))
    return pl.pallas_call(
        body, name="fwd_ffn", grid=(n_tiles,),
        in_specs=[row(D_MODEL), row(D_MODEL), _full((1, D_MODEL)), pl.BlockSpec(memory_space=pl.ANY),
                  _full((FFN_CONV_WIDTH, 2 * D_FF)), _full((1, 2 * D_FF)), _full((1, D_MODEL))],
        out_specs=[row(2 * D_FF), row(2 * D_FF), row(D_FF), row(D_MODEL), row(D_MODEL), row(D_MODEL), _full((8, 128)),
                   _full((1, D_MODEL))],
        out_shape=[jax.ShapeDtypeStruct((tokens, 2 * D_FF), BF16), jax.ShapeDtypeStruct((tokens, 2 * D_FF), BF16),
                   jax.ShapeDtypeStruct((tokens, D_FF), BF16),
                   jax.ShapeDtypeStruct((tokens, D_MODEL), BF16), jax.ShapeDtypeStruct((tokens, D_MODEL), F32),
                   jax.ShapeDtypeStruct((tokens, D_MODEL), BF16),
                   jax.ShapeDtypeStruct((8, 128), F32), jax.ShapeDtypeStruct((1, D_MODEL), F32)],
        scratch_shapes=[pltpu.VMEM((2 * D_FF, D_MODEL), BF16), pltpu.VMEM((D_FF, D_MODEL), BF16),
                        pltpu.VMEM((FFN_HALO, 2 * D_FF), F32), pltpu.SemaphoreType.DMA((2,))],
        compiler_params=_params(),
    )(x2, target, g_ffn, gw, ffn_w, ffn_b, g_final)


def _bwd_ffn(dx3, x2, uu_all, cc_all, gw, g_ffn, ffn_w, seq, tm):
    tokens = x2.shape[0]
    n_tiles = tokens // tm
    tps = seq // tm
    n_chunks = D_FF // FFN_CHUNK

    def body(dx3_ref, x2_ref, uu_ref, cc_ref, gffn_ref, gw_hbm, fw_ref,
             dx2_ref, dx2b_ref, duu_ref, dfb_ref, dfw_ref, dg_ref,
             wup_v, wdown_v, carry, sem):
        i = pl.program_id(0)
        t = n_tiles - 1 - i

        _start_weights(gw_hbm, ("w_down", "w_up"), (wdown_v, wup_v), sem)

        @pl.when(i == 0)
        def _():
            dfb_ref[...] = jnp.zeros_like(dfb_ref)
            dfw_ref[...] = jnp.zeros_like(dfw_ref)
            dg_ref[...] = jnp.zeros_like(dg_ref)

        @pl.when(t % tps == tps - 1)
        def _():
            carry[...] = jnp.zeros_like(carry)

        dx3v = dx3_ref[...]
        dx3b = dx3v.astype(BF16)
        dh3 = jnp.zeros((tm, D_MODEL), F32)
        for jc in range(n_chunks):
            da = _dot_nt(dx3b, wdown_v[pl.ds(jc * FFN_CHUNK, FFN_CHUNK), :])
            colss = [pl.ds(half * D_FF + jc * FFN_CHUNK, FFN_CHUNK) for half in range(2)]
            gate, val = [cc_ref[:, cols].astype(F32) for cols in colss]
            sg = _sigmoid(gate)
            dgate = da * val * (sg * (1.0 + gate * (1.0 - sg)))
            dval = da * (gate * sg)
            for dcc, cols in zip((dgate, dval), colss):
                uu = uu_ref[:, cols].astype(F32)
                dfb_ref[:, cols] += _colsum(dcc)
                ext = jnp.concatenate([dcc, carry[:, cols]], axis=0)
                carry[:, cols] = dcc[:FFN_HALO, :]
                n1 = pltpu.roll(ext, tm + FFN_HALO - 1, 0)[:tm, :]
                n2 = pltpu.roll(ext, tm + FFN_HALO - 2, 0)[:tm, :]
                duu = fw_ref[2:3, cols] * dcc + fw_ref[1:2, cols] * n1 + fw_ref[0:1, cols] * n2
                dfw_ref[2:3, cols] += _colsum(uu * dcc)
                dfw_ref[1:2, cols] += _colsum(uu * n1)
                dfw_ref[0:1, cols] += _colsum(uu * n2)
                duub = duu.astype(BF16)
                duu_ref[:, cols] = duub
                dh3 = dh3 + _dot(duub, wup_v[cols, :])
        xh, r = _rms_fwd(x2_ref[...])
        dg_ref[...] += _colsum(dh3 * xh)
        dx2 = dx3v + _rms_bwd(dh3, xh, r, gffn_ref[...])
        dx2_ref[...] = dx2
        dx2b_ref[...] = dx2.astype(BF16)

    rev = lambda w: pl.BlockSpec((tm, w), lambda i: (n_tiles - 1 - i, 0))
    return pl.pallas_call(
        body, name="bwd_ffn", grid=(n_tiles,),
        in_specs=[rev(D_MODEL), rev(D_MODEL), rev(2 * D_FF), rev(2 * D_FF), _full((1, D_MODEL)),
                  pl.BlockSpec(memory_space=pl.ANY), _full((FFN_CONV_WIDTH, 2 * D_FF))],
        out_specs=[rev(D_MODEL), rev(D_MODEL), rev(2 * D_FF), _full((1, 2 * D_FF)), _full((FFN_CONV_WIDTH, 2 * D_FF)),
                   _full((1, D_MODEL))],
        out_shape=[jax.ShapeDtypeStruct((tokens, D_MODEL), F32), jax.ShapeDtypeStruct((tokens, D_MODEL), BF16),
                   jax.ShapeDtypeStruct((tokens, 2 * D_FF), BF16),
                   jax.ShapeDtypeStruct((1, 2 * D_FF), F32), jax.ShapeDtypeStruct((FFN_CONV_WIDTH, 2 * D_FF), F32),
                   jax.ShapeDtypeStruct((1, D_MODEL), F32)],
        scratch_shapes=[pltpu.VMEM((2 * D_FF, D_MODEL), BF16), pltpu.VMEM((D_FF, D_MODEL), BF16),
                        pltpu.VMEM((FFN_HALO, 2 * D_FF), F32), pltpu.SemaphoreType.DMA((2,))],
        compiler_params=_params(),
    )(dx3, x2, uu_all, cc_all, g_ffn, gw, ffn_w)


def _bwd_attn(dx2, x1, q, kv, gw, g_x, after, seq, tm):
    tokens = x1.shape[0]
    n_tiles = tokens // tm
    tps = seq // tm
    n_b = tokens // seq

    def body(dx2_ref, x1_ref, q_ref, kv_ref, g_ref, gw_hbm, after_ref, dx1_ref, dx1b_ref, dq_ref, dkv_ref, dg_ref,
             wq_v, wo_v, sem):
        del after_ref
        i = pl.program_id(0)

        _start_weights(gw_hbm, ("w_o", "w_q"), (wo_v, wq_v), sem)

        @pl.when(i == 0)
        def _():
            dg_ref[...] = jnp.zeros_like(dg_ref)

        @pl.when(i % tps == 0)
        def _():
            dkv_ref[...] = jnp.zeros_like(dkv_ref)

        dx2v = dx2_ref[...]
        do = _dot_nt(dx2v.astype(BF16), wo_v[...]).astype(BF16)
        q = q_ref[...]
        heads = [slice(h * HEAD_DIM, (h + 1) * HEAD_DIM) for h in range(HEADS)]
        kcols = [pl.ds(h * HEAD_DIM, HEAD_DIM) for h in range(HEADS)]
        vcols = [pl.ds(D_MODEL + h * HEAD_DIM, HEAD_DIM) for h in range(HEADS)]
        scores = [_dot_nt(q[:, hd], kv_ref[:, kc]) for hd, kc in zip(heads, kcols)]
        dps = [_dot_nt(do[:, hd], kv_ref[:, vc]) for hd, vc in zip(heads, vcols)]
        probs = [_softmax_rows(s) for s in scores]
        dss = [(p * (dp - jnp.sum(dp * p, axis=-1, keepdims=True))).astype(BF16) for p, dp in zip(probs, dps)]
        for p, hd, vc in zip(probs, heads, vcols):
            dkv_ref[:, vc] += _dot_tn(p.astype(BF16), do[:, hd])
        dqs = [_dot(ds, kv_ref[:, kc]) * (HEAD_DIM ** -0.5) for ds, kc in zip(dss, kcols)]
        for ds, hd, kc in zip(dss, heads, kcols):
            dkv_ref[:, kc] += _dot_tn(ds, q[:, hd])
        dq = jnp.concatenate(dqs, axis=-1).astype(BF16)
        dq_ref[...] = dq
        dh2 = _dot_nt(dq, wq_v[...])
        xh, r = _rms_fwd(x1_ref[...])
        dg_ref[...] += _colsum(dh2 * xh)
        dx1 = dx2v + _rms_bwd(dh2, xh, r, g_ref[...])
        dx1_ref[...] = dx1
        dx1b_ref[...] = dx1.astype(BF16)

    row = lambda w: pl.BlockSpec((tm, w), lambda i: (i, 0))
    per_b = pl.BlockSpec((N_MEM, 2 * D_MODEL), lambda i: (i // tps, 0))
    return pl.pallas_call(
        body, name="bwd_attn", grid=(n_tiles,),
        in_specs=[row(D_MODEL), row(D_MODEL), row(D_MODEL), per_b, _full((1, D_MODEL)), pl.BlockSpec(memory_space=pl.ANY),
                  _full(after.shape)],
        out_specs=[row(D_MODEL), row(D_MODEL), row(D_MODEL), per_b, _full((1, D_MODEL))],
        out_shape=[jax.ShapeDtypeStruct((tokens, D_MODEL), F32), jax.ShapeDtypeStruct((tokens, D_MODEL), BF16),
                   jax.ShapeDtypeStruct((tokens, D_MODEL), BF16),
                   jax.ShapeDtypeStruct((n_b * N_MEM, 2 * D_MODEL), F32), jax.ShapeDtypeStruct((1, D_MODEL), F32)],
        scratch_shapes=[pltpu.VMEM((D_MODEL, D_MODEL), BF16), pltpu.VMEM((D_MODEL, D_MODEL), BF16), pltpu.SemaphoreType.DMA((2,))],
        compiler_params=_params(),
    )(dx2, x1, q, kv, g_x, gw, after)


def _bwd_kv(dkv, mem2d, gw):
    rows = mem2d.shape[0]
    n_b = rows // N_MEM

    def body(dkv_ref, mem_ref, gw_hbm, dkvb_ref, dg_ref, wkv_v, sem):
        @pl.when(pl.program_id(0) == 0)
        def _():
            copies = _load_weight(gw_hbm, "w_kv", wkv_v, sem)
            for cp in copies:
                cp.start()
            for cp in copies:
                cp.wait()
            dg_ref[...] = jnp.zeros_like(dg_ref)

        dkvb = dkv_ref[...].astype(BF16)
        dkvb_ref[...] = dkvb
        dmn = _dot(dkvb, wkv_v[...])
        mh, _ = _rms_fwd(mem_ref[...])
        dg_ref[...] += _colsum(dmn * mh)

    return pl.pallas_call(
        body, name="bwd_kv", grid=(n_b,),
        in_specs=[pl.BlockSpec((N_MEM, 2 * D_MODEL), lambda b: (b, 0)), pl.BlockSpec((N_MEM, D_MODEL), lambda b: (b, 0)),
                  pl.BlockSpec(memory_space=pl.ANY)],
        out_specs=[pl.BlockSpec((N_MEM, 2 * D_MODEL), lambda b: (b, 0)), _full((1, D_MODEL))],
        out_shape=[jax.ShapeDtypeStruct((rows, 2 * D_MODEL), BF16), jax.ShapeDtypeStruct((1, D_MODEL), F32)],
        scratch_shapes=[pltpu.VMEM((2 * D_MODEL, D_MODEL), BF16), pltpu.SemaphoreType.DMA],
        compiler_params=_params(),
    )(dkv, mem2d, gw)


def _bwd_mix(dx1, x2d, u_all, c_all, pooled_all, gw, g_mix, conv_w, ln_g, ln_b, pool_w, pool_scale, after, seq, tm):
    tokens = x2d.shape[0]
    n_tiles = tokens // tm
    tps = seq // tm

    def body(dx1_ref, x_ref, u_ref, c_ref, pooled_ref, gmix_ref, gw_hbm, cw_ref, lng_ref, lnb_ref, pw_ref, ps_ref,
             after_ref, dx_ref, du_ref, dgmix_ref, dcw_ref, dcb_ref, dlng_ref, dlnb_ref, dpw_ref, dps_ref,
             win_v, wout_v, dc_carry, e_carry, sem):
        del after_ref
        i = pl.program_id(0)
        t = n_tiles - 1 - i

        _start_weights(gw_hbm, ("w_out", "w_in"), (wout_v, win_v), sem)

        @pl.when(i == 0)
        def _():
            for ref in (dgmix_ref, dcw_ref, dcb_ref, dlng_ref, dlnb_ref, dpw_ref, dps_ref):
                ref[...] = jnp.zeros_like(ref)

        @pl.when(t % tps == tps - 1)
        def _():
            dc_carry[...] = jnp.zeros_like(dc_carry)
            e_carry[...] = jnp.zeros_like(e_carry)

        dx1v = dx1_ref[...]
        dymix = _dot_nt(dx1v.astype(BF16), wout_v[...])
        dyc, dyp = dymix[:, :D_CONV], dymix[:, D_CONV:]
        u = u_ref[...]
        val, gate = u[:, :D_CONV], u[:, D_CONV:2 * D_CONV]

        conv = c_ref[...]
        mu = jnp.mean(conv, axis=-1, keepdims=True)
        cen = conv - mu
        rs = lax.rsqrt(jnp.mean(cen * cen, axis=-1, keepdims=True) + EPS)
        chat = cen * rs
        ln = chat * lng_ref[...] + lnb_ref[...]
        sl = _sigmoid(ln)
        dln = dyc * (sl * (1.0 + ln * (1.0 - sl)))
        dlng_ref[...] += _colsum(dln * chat)
        dlnb_ref[...] += _colsum(dln)
        dchat = dln * lng_ref[...]
        dc = rs * (dchat - jnp.mean(dchat, axis=-1, keepdims=True)
                   - chat * jnp.mean(dchat * chat, axis=-1, keepdims=True))
        dcb_ref[...] += _colsum(dc)
        sg = _sigmoid(gate)
        hc = val * sg
        ext = jnp.concatenate([dc, dc_carry[...]], axis=0)
        dc_carry[...] = dc[:CONV_HALO, :]
        dhc = jnp.zeros((tm, D_CONV), F32)
        ahead_by = _sublane_shifts(ext)
        for k in range(CONV_WIDTH):
            whole, part = divmod(CONV_WIDTH - 1 - k, 8)
            tap = ahead_by[part][8 * whole:8 * whole + tm, :]
            dhc = dhc + cw_ref[k:k + 1, :] * tap
            dcw_ref[k:k + 1, :] += _colsum_mxu(hc * tap)
        du_ref[:, :D_CONV] = (dhc * sg).astype(BF16)
        du_ref[:, D_CONV:2 * D_CONV] = (dhc * val * (sg * (1.0 - sg))).astype(BF16)

        pos = lax.broadcasted_iota(jnp.int32, (tm, 1), 0) + (t % tps) * tm
        es, dpooled = [], []
        for g, w in enumerate(POOL_WINDOWS):
            cols = pl.ds(g * POOL_GROUP_DIM, POOL_GROUP_DIM)
            lo = g * POOL_GROUP_DIM
            pooled = pooled_ref[:, cols]
            pw = pw_ref[g].astype(BF16)
            dyg = dyp[:, lo:lo + POOL_GROUP_DIM]
            dps_ref[:, cols] += _colsum(dyg * _dot(pooled, pw))
            dmixed = (dyg * ps_ref[:, cols]).astype(BF16)
            dpw_ref[g] += _dot_tn(pooled, dmixed)
            dpo = _dot_nt(dmixed, pw)
            dpooled.append(dpo)
            es.append(dpo / jnp.minimum(pos + 1, w).astype(F32))
        e = jnp.concatenate(es, axis=-1)
        run = jnp.concatenate([e, e_carry[...]], axis=0)
        e_carry[...] = e[:POOL_HALO, :]
        rows = tm + POOL_HALO
        for g, w in enumerate(POOL_WINDOWS):
            lo = g * POOL_GROUP_DIM
            run = run[:, POOL_GROUP_DIM if g else 0:]
            run = run + pltpu.roll(run, rows - w // 2, 0)
            du_ref[:, 2 * D_CONV + lo:2 * D_CONV + lo + POOL_GROUP_DIM] = (
                run[:tm, :POOL_GROUP_DIM] - dpooled[g]).astype(BF16)

        dh1 = _dot(du_ref[...], win_v[...])
        xh, r = _rms_fwd(x_ref[...])
        dgmix_ref[...] += _colsum(dh1 * xh)
        dx_ref[...] = dx1v + _rms_bwd(dh1, xh, r, gmix_ref[...])

    rev = lambda w: pl.BlockSpec((tm, w), lambda i: (n_tiles - 1 - i, 0))
    return pl.pallas_call(
        body, name="bwd_mix", grid=(n_tiles,),
        in_specs=[rev(D_MODEL), rev(D_MODEL), rev(D_IN), rev(D_CONV), rev(D_POOL), _full((1, D_MODEL)),
                  pl.BlockSpec(memory_space=pl.ANY), _full((CONV_WIDTH, D_CONV)), _full((1, D_CONV)), _full((1, D_CONV)),
                  _full((4, POOL_GROUP_DIM, POOL_GROUP_DIM)), _full((1, D_POOL)), _full(after.shape)],
        out_specs=[rev(D_MODEL), rev(D_IN), _full((1, D_MODEL)), _full((CONV_WIDTH, D_CONV)), _full((1, D_CONV)),
                   _full((1, D_CONV)), _full((1, D_CONV)), _full((4, POOL_GROUP_DIM, POOL_GROUP_DIM)), _full((1, D_POOL))],
        out_shape=[jax.ShapeDtypeStruct((tokens, D_MODEL), F32), jax.ShapeDtypeStruct((tokens, D_IN), BF16),
                   jax.ShapeDtypeStruct((1, D_MODEL), F32), jax.ShapeDtypeStruct((CONV_WIDTH, D_CONV), F32),
                   jax.ShapeDtypeStruct((1, D_CONV), F32), jax.ShapeDtypeStruct((1, D_CONV), F32),
                   jax.ShapeDtypeStruct((1, D_CONV), F32),
                   jax.ShapeDtypeStruct((4, POOL_GROUP_DIM, POOL_GROUP_DIM), F32), jax.ShapeDtypeStruct((1, D_POOL), F32)],
        scratch_shapes=[pltpu.VMEM((D_IN, D_MODEL), BF16), pltpu.VMEM((D_MODEL, D_MODEL), BF16),
                        pltpu.VMEM((CONV_HALO, D_CONV), F32), pltpu.VMEM((POOL_HALO, D_POOL), F32),
                        pltpu.SemaphoreType.DMA((2,))],
        compiler_params=_params(),
    )(dx1, x2d, u_all, c_all, pooled_all, g_mix, gw, conv_w, ln_g, ln_b, pool_w, pool_scale, after)


def _wgrad(a, b, name, after=None):
    tokens, m = a.shape
    n = b.shape[1]
    tm = 512 if m % 512 == 0 else 256
    extra = [] if after is None else [after]

    if m * n * 4 <= WGRAD_RESIDENT_BYTES:
        tk = min(WGRAD_TOKEN_BLOCK, tokens)
        n_k = tokens // tk

        def walk(a_ref, b_ref, *rest):
            out_ref, acc = rest[-2:]
            k = pl.program_id(0)

            @pl.when(k == 0)
            def _():
                acc[...] = jnp.zeros_like(acc)

            acc[...] += _dot_tn(a_ref[...], b_ref[...])

            @pl.when(k == n_k - 1)
            def _():
                out_ref[...] = acc[...].astype(out_ref.dtype)

        return pl.pallas_call(
            walk, name=name, grid=(n_k,),
            in_specs=[pl.BlockSpec((tk, m), lambda k: (k, 0)), pl.BlockSpec((tk, n), lambda k: (k, 0))] + [
                _full(t.shape) for t in extra],
            out_specs=_full((m, n)),
            out_shape=jax.ShapeDtypeStruct((m, n), BF16),
            scratch_shapes=[pltpu.VMEM((m, n), F32)],
            compiler_params=_params(),
        )(a, b, *extra)

    def body(a_ref, b_ref, *rest):
        rest[-1][...] = _dot_tn(a_ref[...], b_ref[...]).astype(rest[-1].dtype)

    return pl.pallas_call(
        body, name=name, grid=(m // tm,),
        in_specs=[pl.BlockSpec((tokens, tm), lambda i: (0, i)), _full((tokens, n))] + [_full(t.shape) for t in extra],
        out_specs=pl.BlockSpec((tm, n), lambda i: (i, 0)),
        out_shape=jax.ShapeDtypeStruct((m, n), BF16),
        compiler_params=_params(),
    )(a, b, *extra)


def _adamw_update(w, g, m, v):
    nm = ADAM_B1 * m + (1.0 - ADAM_B1) * g
    nv = ADAM_B2 * v + (1.0 - ADAM_B2) * (g * g)
    m_hat = nm / (1.0 - ADAM_B1 ** ADAM_STEP)
    v_hat = nv / (1.0 - ADAM_B2 ** ADAM_STEP)
    return -ADAM_LR * (m_hat / (jnp.sqrt(v_hat) + ADAM_EPS) + ADAM_WD * w), nm, nv


def _adamw_small(ws, gs, ms, vs):
    n = len(ws)

    def body(*refs):
        ins, outs = refs[:4 * n], refs[4 * n:]
        for k in range(n):
            d, nm, nv = _adamw_update(*[ins[j * n + k][...] for j in range(4)])
            outs[k][...] = d
            outs[n + k][...] = nm
            outs[2 * n + k][...] = nv

    vmem = pl.BlockSpec(memory_space=pltpu.VMEM)
    outs = pl.pallas_call(
        body, name="adamw_small",
        in_specs=[vmem] * (4 * n), out_specs=[vmem] * (3 * n),
        out_shape=[jax.ShapeDtypeStruct(w.shape, F32) for w in ws] * 3,
    )(*ws, *gs, *ms, *vs)
    return outs[:n], outs[n:2 * n], outs[2 * n:]


SMALL = (("norm_mix_g", (1, 1024)), ("conv_dw_b", (1, 512)), ("conv_ln_g", (1, 512)), ("conv_ln_b", (1, 512)),
         ("pool_w", (1, 4, 128, 128)), ("pool_scale", (1, 512)), ("norm_xattn_g", (1, 1024)), ("norm_mem_g", (1, 1024)),
         ("norm_ffn_g", (1, 1024)), ("ffn_dw_b", (1, 5632)), ("norm_final_g", (1024,)))
LANES = 128


def _pack_rows(arrs):
    flat = jnp.concatenate([a.reshape(-1) for a in arrs])
    pad = (-flat.shape[0]) % (8 * LANES)
    return jnp.pad(flat, (0, pad)).reshape(-1, LANES)


def kernel(x, mem, norm_mix_g, w_in, conv_dw_w, conv_dw_b, conv_ln_g, conv_ln_b, pool_w, pool_scale, w_out, norm_xattn_g, norm_mem_g, w_q, w_kv, w_o, norm_ffn_g, w_up, ffn_dw_w, ffn_dw_b, w_down, norm_final_g, loss_target, m_norm_mix_g, m_w_in, m_conv_dw_w, m_conv_dw_b, m_conv_ln_g, m_conv_ln_b, m_pool_w, m_pool_scale, m_w_out, m_norm_xattn_g, m_norm_mem_g, m_w_q, m_w_kv, m_w_o, m_norm_ffn_g, m_w_up, m_ffn_dw_w, m_ffn_dw_b, m_w_down, m_norm_final_g, v_norm_mix_g, v_w_in, v_conv_dw_w, v_conv_dw_b, v_conv_ln_g, v_conv_ln_b, v_pool_w, v_pool_scale, v_w_out, v_norm_xattn_g, v_norm_mem_g, v_w_q, v_w_kv, v_w_o, v_norm_ffn_g, v_w_up, v_ffn_dw_w, v_ffn_dw_b, v_w_down, v_norm_final_g):
    weights = dict(norm_mix_g=norm_mix_g, w_in=w_in, conv_dw_w=conv_dw_w, conv_dw_b=conv_dw_b, conv_ln_g=conv_ln_g,
                   conv_ln_b=conv_ln_b, pool_w=pool_w, pool_scale=pool_scale, w_out=w_out, norm_xattn_g=norm_xattn_g,
                   norm_mem_g=norm_mem_g, w_q=w_q, w_kv=w_kv, w_o=w_o, norm_ffn_g=norm_ffn_g, w_up=w_up,
                   ffn_dw_w=ffn_dw_w, ffn_dw_b=ffn_dw_b, w_down=w_down, norm_final_g=norm_final_g)
    moments_m = dict(norm_mix_g=m_norm_mix_g, w_in=m_w_in, conv_dw_w=m_conv_dw_w, conv_dw_b=m_conv_dw_b,
                     conv_ln_g=m_conv_ln_g, conv_ln_b=m_conv_ln_b, pool_w=m_pool_w, pool_scale=m_pool_scale,
                     w_out=m_w_out, norm_xattn_g=m_norm_xattn_g, norm_mem_g=m_norm_mem_g, w_q=m_w_q, w_kv=m_w_kv,
                     w_o=m_w_o, norm_ffn_g=m_norm_ffn_g, w_up=m_w_up, ffn_dw_w=m_ffn_dw_w, ffn_dw_b=m_ffn_dw_b,
                     w_down=m_w_down, norm_final_g=m_norm_final_g)
    moments_v = dict(norm_mix_g=v_norm_mix_g, w_in=v_w_in, conv_dw_w=v_conv_dw_w, conv_dw_b=v_conv_dw_b,
                     conv_ln_g=v_conv_ln_g, conv_ln_b=v_conv_ln_b, pool_w=v_pool_w, pool_scale=v_pool_scale,
                     w_out=v_w_out, norm_xattn_g=v_norm_xattn_g, norm_mem_g=v_norm_mem_g, w_q=v_w_q, w_kv=v_w_kv,
                     w_o=v_w_o, norm_ffn_g=v_norm_ffn_g, w_up=v_w_up, ffn_dw_w=v_ffn_dw_w, ffn_dw_b=v_ffn_dw_b,
                     w_down=v_w_down, norm_final_g=v_norm_final_g)
    order = list(weights)
    transposed = ("w_in", "w_kv", "w_up")

    n_b, seq, _ = x.shape
    tokens = n_b * seq
    tm_mix = min(512, seq // 2)
    tm_attn = min(1024, seq // 2)
    tm_ffn = min(256, seq // 2)
    dev = 4 * lax.axis_index("x") + 2 * lax.axis_index("y") + lax.axis_index("c")

    packs = [jnp.concatenate([weights[n][0].T if n in transposed else weights[n][0] for n in names], axis=0).astype(BF16)
             for names in AG_GROUPS]
    small_sharded = _pack_rows([conv_dw_w, ffn_dw_w])
    n_small = small_sharded.size * 2 // D_MODEL
    bits = lax.bitcast_convert_type(small_sharded, jnp.uint32)
    halves = [lax.bitcast_convert_type(h.astype(jnp.uint16), BF16).reshape(n_small // 2, D_MODEL)
              for h in (bits >> 16, bits & 0xFFFF)]
    small_bits = jnp.concatenate(halves, axis=0)
    n_mix = packs[0].shape[0]
    packs[0] = jnp.concatenate([packs[0], small_bits, jnp.zeros((BF16_TILE_ROWS - n_small, D_MODEL), BF16)], axis=0)
    flights = []
    after = small_sharded
    for k in range(len(AG_GROUPS)):
        own_in_place = lax.dynamic_update_slice(lax.empty((N_DEV,) + packs[k].shape, BF16), packs[k][None], (dev, 0, 0))
        flights.append(_gather_start(own_in_place, after, "weights_gather_start_%d" % k, BARRIER_IDS["gather_start"][k]))
        after = flights[-1][3]

    def gather_finish(flight, after, tag):
        fwd_send, fwd_recv, buf = _gather_forward(*flight[:3], after, "weights_gather_forward_" + tag,
                                                  BARRIER_IDS["gather_forward"][int(tag)])
        return _gather_finish(fwd_send, fwd_recv, buf, "weights_gather_finish_" + tag)

    gw_mix = gather_finish(flights[0], after, "0")
    high, low = [lax.bitcast_convert_type(gw_mix[:, r:r + n_small // 2, :].reshape((N_DEV,) + small_sharded.shape),
                                          jnp.uint16).astype(jnp.uint32) for r in (n_mix, n_mix + n_small // 2)]
    gsmall = lax.bitcast_convert_type((high << 16) | low, F32)
    gflat = gsmall.reshape(N_DEV, -1)
    n_cw = CONV_WIDTH * (D_CONV // N_DEV)
    n_fw = FFN_CONV_WIDTH * (2 * D_FF // N_DEV)
    conv_w = gflat[:, :n_cw].reshape(N_DEV, CONV_WIDTH, D_CONV // N_DEV).transpose(1, 0, 2).reshape(CONV_WIDTH, D_CONV)
    ffn_w = gflat[:, n_cw:n_cw + n_fw].reshape(N_DEV, FFN_CONV_WIDTH, 2 * D_FF // N_DEV).transpose(1, 0, 2).reshape(
        FFN_CONV_WIDTH, 2 * D_FF)

    x2d = x.reshape(tokens, D_MODEL)
    mem2d = mem.reshape(n_b * N_MEM, D_MODEL)
    tgt2d = loss_target.reshape(tokens, D_MODEL)
    g_final = norm_final_g.reshape(1, D_MODEL)

    x1, u_all, c_all, pooled_all, ymix, h1 = _fwd_mix(
        x2d, gw_mix, norm_mix_g, conv_w, conv_dw_b, conv_ln_g, conv_ln_b, pool_w[0], pool_scale, flights[2][3],
        seq, tm_mix)
    gw_attn = gather_finish(flights[1], x1, "1")
    mem_n, kv = _fwd_kv(mem2d, gw_attn, norm_mem_g)
    x2, h2, q, o = _fwd_attn(x1, kv, gw_attn, norm_xattn_g, seq, tm_attn)
    gw_ffn = gather_finish(flights[2], x2, "2")
    uu_all, cc_all, a_all, h3, dx3, dx3b, loss_part, dg_final = _fwd_ffn(
        x2, tgt2d, gw_ffn, norm_ffn_g, ffn_w, ffn_dw_b, g_final, seq, tm_ffn)

    table = _owner_table()

    def sibling_start(names, tag):
        parts = [part[n].reshape(N_DEV, W_OFF[n][1], D_MODEL) for n in names]
        return _exchange_start(parts, 4, _to_sibling, "rs_sibling_exchange_start_" + tag, BARRIER_IDS["sibling"][tag])

    def chips_start(flight, after, tag):
        parts, landed = _exchange_wait(*flight[:4], after, 4, _to_sibling, "rs_sibling_exchange_wait_" + tag)
        sums = _chip_partial_sums(table, parts, landed, "rs_chip_partial_sums_" + tag)
        return parts, landed, _exchange_start(sums, 3, _to_chip, "rs_chip_exchange_start_" + tag,
                                              BARRIER_IDS["chips"][tag])

    grads, delta, new_m, new_v = {}, {}, {}, {}

    def reduce_finish(names, parts, landed, flight, after, tag):
        _, from_chips = _exchange_wait(*flight[:4], after, 3, _to_chip, "rs_chip_exchange_wait_" + tag)
        as_rows = {n: n in transposed and W_OFF[n][1] % LANES != 0 for n in names}
        states = [tuple(t[n][0].T if as_rows[n] else t[n][0] for t in (weights, moments_m, moments_v)) for n in names]
        results = _final_update(table, parts, landed, from_chips, states, "rs_final_update_" + tag)
        for n, res in zip(names, results):
            grads[n], delta[n], new_m[n], new_v[n] = [t.T[None] if as_rows[n] else t[None] for t in res]
        return delta[names[-1]]

    part = {}
    dx2, dx2b, duu, d_ffn_b, d_ffn_w, dg_ffn = _bwd_ffn(dx3, x2, uu_all, cc_all, gw_ffn, norm_ffn_g, ffn_w, seq, tm_ffn)
    part["w_up"] = _wgrad(duu, h3, "wgrad_w_up")
    part["w_down"] = _wgrad(a_all, dx3b, "wgrad_w_down")
    to_sibling_a = sibling_start(RS_GROUPS["a"], "a")
    dx1, dx1b, dq, dkv, dg_x = _bwd_attn(dx2, x1, q, kv, gw_attn, norm_xattn_g, to_sibling_a[4], seq, tm_mix)
    parts_a, landed_a, flight_a = chips_start(to_sibling_a, dx1, "a")
    dkv_b, dg_mem = _bwd_kv(dkv, mem2d, gw_attn)
    part["w_q"] = _wgrad(h2, dq, "wgrad_w_q", after=flight_a[4])
    part["w_kv"] = _wgrad(dkv_b, mem_n, "wgrad_w_kv", after=flight_a[4])
    part["w_out"] = _wgrad(ymix, dx1b, "wgrad_w_out", after=flight_a[4])
    to_sibling_b = sibling_start(RS_GROUPS["b"], "b")
    part["w_o"] = _wgrad(o, dx2b, "wgrad_w_o", after=to_sibling_b[4])
    parts_b, landed_b, flight_b = chips_start(to_sibling_b, part["w_o"], "b")
    dx, du, dg_mix, d_conv_w, d_conv_b, d_ln_g, d_ln_b, d_pool_w, d_pool_scale = _bwd_mix(
        dx1, x2d, u_all, c_all, pooled_all, gw_mix, norm_mix_g, conv_w, conv_ln_g, conv_ln_b, pool_w[0], pool_scale,
        flight_b[4], seq, tm_mix)
    grad_x = dx.reshape(x.shape)

    small_grads = dict(norm_mix_g=dg_mix, conv_dw_b=d_conv_b, conv_ln_g=d_ln_g, conv_ln_b=d_ln_b, pool_w=d_pool_w,
                       pool_scale=d_pool_scale, norm_xattn_g=dg_x, norm_mem_g=dg_mem, norm_ffn_g=dg_ffn,
                       ffn_dw_b=d_ffn_b, norm_final_g=dg_final)
    small_list = [small_grads[n] for n, _ in SMALL] + [d_conv_w, d_ffn_w, loss_part[:1]]
    small_mine = _pack_rows(small_list)
    small_flight = _broadcast_start(
        lax.dynamic_update_slice(lax.empty((N_DEV,) + small_mine.shape, F32), small_mine[None], (dev, 0, 0)),
        "small_grads_broadcast_start", BARRIER_IDS["broadcast"])

    part["w_in"] = _wgrad(du, h1, "wgrad_w_in", after=small_flight[3])
    to_sibling_c = sibling_start(RS_GROUPS["c"], "c")
    updated_b = reduce_finish(RS_GROUPS["b"], parts_b, landed_b, flight_b, to_sibling_c[4], "b")
    parts_c, landed_c, flight_c = chips_start(to_sibling_c, updated_b, "c")
    updated_a = reduce_finish(RS_GROUPS["a"], parts_a, landed_a, flight_a, flight_c[4], "a")
    small_all = _broadcast_wait(*small_flight[:3], updated_a, "small_grads_broadcast_wait")
    small_sum = _sum_blocks(small_all).reshape(-1)

    pos = 0
    for n, shape in SMALL:
        size = 1
        for s in shape:
            size *= s
        grads[n] = small_sum[pos:pos + size].reshape(shape)
        pos += size
    full_conv_w = small_sum[pos:pos + CONV_WIDTH * D_CONV].reshape(CONV_WIDTH, D_CONV)
    pos += CONV_WIDTH * D_CONV
    full_ffn_w = small_sum[pos:pos + FFN_CONV_WIDTH * 2 * D_FF].reshape(FFN_CONV_WIDTH, 2 * D_FF)
    loss = small_sum[pos + FFN_CONV_WIDTH * 2 * D_FF]
    grads["conv_dw_w"] = lax.dynamic_slice_in_dim(full_conv_w, dev * (D_CONV // N_DEV), D_CONV // N_DEV, axis=1)[None]
    grads["ffn_dw_w"] = lax.dynamic_slice_in_dim(full_ffn_w, dev * (2 * D_FF // N_DEV), 2 * D_FF // N_DEV, axis=1)[None]

    small_names = [n for n in order if n not in W_OFF]
    swap = lambda t: jnp.transpose(t, (1, 0, 2))
    two_d = lambda t: t.reshape(1, -1) if t.ndim == 1 else (swap(t) if t.ndim == 3 else t)
    outs = _adamw_small(*[[two_d(t[n]) for n in small_names] for t in (weights, grads, moments_m, moments_v)])
    for res, out in zip((delta, new_m, new_v), outs):
        for n, o in zip(small_names, out):
            res[n] = swap(o) if o.ndim == 3 else o.reshape(weights[n].shape)

    reduce_finish(RS_GROUPS["c"], parts_c, landed_c, flight_c, delta[small_names[-1]], "c")

    return (loss, grad_x, *[grads[n] for n in order], *[delta[n] for n in order],
            *[new_m[n] for n in order], *[new_v[n] for n in order])
```
